```python
import jax, jax.numpy as jnp
from jax import lax
import numpy as np

D_MODEL = 1024
BATCH = 8
SEQ = 4096
DEPTH = 1

CHUNK = 64
RWKV_HEADS = 8
RWKV_HEAD_DIM = 64
RWKV_WIDTH = RWKV_HEADS * RWKV_HEAD_DIM
DECAY_LORA = 64
ICLR_LORA = 64
GATE_LORA = 128
GDN_HEADS = 4
GDN_HEAD_DIM = 128
GDN_WIDTH = GDN_HEADS * GDN_HEAD_DIM
GDN_CONV = 4
FFN_HIDDEN = 2816
FFN_CONV = 3
NORM_EPS = 1e-6
L2_EPS = 1e-6
RWKV_GN_EPS = 64e-5

RWKV_SHIFT_WIDTH = 3 * RWKV_WIDTH + DECAY_LORA + ICLR_LORA + GATE_LORA
IN_SPLITS = (RWKV_SHIFT_WIDTH, 3 * GDN_WIDTH, GDN_WIDTH, GDN_HEADS, GDN_HEADS, D_MODEL, D_MODEL)
IN_WIDTH = RWKV_SHIFT_WIDTH + 4 * GDN_WIDTH + 2 * GDN_HEADS + 2 * D_MODEL

kernel_name = 'hybrid_rwkv7_gdn_gated_merge_block'


def _split(t, sizes):
    cuts = [int(c) for c in np.cumsum(sizes)[:-1]]
    return jnp.split(t, cuts, axis=-1)


def rms_norm(t, gain, eps=NORM_EPS):
    tf = t.astype(jnp.float32)
    y = tf * lax.rsqrt(jnp.mean(tf * tf, axis=-1, keepdims=True) + eps)
    return (y * gain.astype(jnp.float32)).astype(t.dtype)


def l2norm(t):
    tf = t.astype(jnp.float32)
    return (tf * lax.rsqrt(jnp.sum(tf * tf, axis=-1, keepdims=True) + L2_EPS)).astype(t.dtype)


def causal_depthwise_conv(t, w):
    width = w.shape[0]
    T = t.shape[1]
    tp = jnp.pad(t, ((0, 0), (width - 1, 0), (0, 0)))
    out = tp[:, 0:T] * w[0]
    for i in range(1, width):
        out = out + tp[:, i:i + T] * w[i]
    return out


def token_shift(t):
    return jnp.pad(t, ((0, 0), (1, 0), (0, 0)))[:, :-1]


def group_norm_heads(y, w, b):
    yf = y.astype(jnp.float32)
    mean = jnp.mean(yf, axis=-1, keepdims=True)
    var = jnp.mean(jnp.square(yf - mean), axis=-1, keepdims=True)
    yn = (yf - mean) * lax.rsqrt(var + RWKV_GN_EPS)
    H, D = y.shape[-2], y.shape[-1]
    return (yn * w.reshape(H, D) + b.reshape(H, D)).astype(y.dtype)


def wkv7_scan(r, w, k, v, a, b):
    dtype = r.dtype
    B, T, H, D = r.shape
    xs = tuple(jnp.moveaxis(t.astype(jnp.float32), 1, 0) for t in (r, w, k, v, a, b))

    def step(S, inp):
        r_t, w_t, k_t, v_t, a_t, b_t = inp
        sa = jnp.einsum('bhvk,bhk->bhv', S, a_t)
        S = S * w_t[:, :, None, :] + sa[..., None] * b_t[:, :, None, :] + v_t[..., None] * k_t[:, :, None, :]
        y = jnp.einsum('bhvk,bhk->bhv', S, r_t)
        return S, y

    S0 = jnp.zeros((B, H, D, D), jnp.float32)
    _, y = lax.scan(step, S0, xs)
    return jnp.moveaxis(y, 0, 1).astype(dtype)


def rwkv7_mix(p, mu, w0, w2, a0, a2, g2, k_k, k_a, r_k, ln_w, ln_b):
    B, T, _ = p.shape
    p = p + (token_shift(p) - p) * mu
    r, k, v, wl, al, gl = _split(p, (RWKV_WIDTH, RWKV_WIDTH, RWKV_WIDTH, DECAY_LORA, ICLR_LORA, GATE_LORA))
    w_log = -jax.nn.softplus(-(w0 + jnp.tanh(wl) @ w2)) - 0.5
    a = jax.nn.sigmoid(a0 + al @ a2)
    g = jax.nn.sigmoid(gl) @ g2

    def heads(t):
        return t.reshape(B, T, RWKV_HEADS, RWKV_HEAD_DIM)

    kk = l2norm(heads(k * k_k))
    k = k * (1 + (a - 1) * k_a)
    r_h, k_h, v_h, a_h = heads(r), heads(k), heads(v), heads(a)
    decay = jnp.exp(-jnp.exp(heads(w_log).astype(jnp.float32)))
    y = wkv7_scan(r_h, decay, k_h, v_h, -kk, kk * a_h)
    y = group_norm_heads(y, ln_w, ln_b)
    y = y + jnp.sum(r_h * k_h * r_k, axis=-1, keepdims=True) * v_h
    return y.reshape(B, T, RWKV_WIDTH) * g


def chunk_gated_delta_rule(q, k, v, g, beta):
    dtype = v.dtype
    B, T, H, Dk = q.shape
    Dv = v.shape[-1]
    N = T // CHUNK

    def to_chunks(t):
        t = t.astype(jnp.float32).reshape((B, N, CHUNK, H) + t.shape[3:])
        return jnp.moveaxis(t, 3, 1)

    q = to_chunks(q) * (Dk ** -0.5)
    k, v, g, beta = to_chunks(k), to_chunks(v), to_chunks(g), to_chunks(beta)
    gc = jnp.cumsum(g, axis=-1)
    causal = jnp.tril(jnp.ones((CHUNK, CHUNK), bool))
    strict = jnp.tril(jnp.ones((CHUNK, CHUNK), bool), -1)
    diff = gc[..., :, None] - gc[..., None, :]
    decay = jnp.where(causal, jnp.exp(jnp.where(causal, diff, 0.0)), 0.0)
    k_beta = k * beta[..., None]
    v_beta = v * beta[..., None]
    Lmat = jnp.where(strict, jnp.einsum('bhncd,bhnsd->bhncs', k_beta, k) * decay, 0.0)
    eye = jnp.eye(CHUNK, dtype=jnp.float32)
    Tinv = lax.linalg.triangular_solve(Lmat + eye, jnp.broadcast_to(eye, Lmat.shape),
                                       left_side=True, lower=True, unit_diagonal=True)
    u = jnp.einsum('bhncs,bhnsd->bhncd', Tinv, v_beta)
    wk = jnp.einsum('bhncs,bhnsd->bhncd', Tinv, k_beta * jnp.exp(gc)[..., None])
    attn = jnp.where(causal, jnp.einsum('bhncd,bhnsd->bhncs', q, k) * decay, 0.0)
    q_dec = q * jnp.exp(gc)[..., None]
    g_last = gc[..., -1]
    k_dec = k * jnp.exp(g_last[..., None] - gc)[..., None]
    xs = (jnp.moveaxis(q_dec, 2, 0), jnp.moveaxis(wk, 2, 0), jnp.moveaxis(u, 2, 0),
          jnp.moveaxis(attn, 2, 0), jnp.moveaxis(k_dec, 2, 0), jnp.moveaxis(g_last, 2, 0))

    def step(S, inp):
        q_n, w_n, u_n, attn_n, k_n, gl_n = inp
        v_new = u_n - jnp.einsum('bhcd,bhde->bhce', w_n, S)
        o = jnp.einsum('bhcd,bhde->bhce', q_n, S) + jnp.einsum('bhcs,bhse->bhce', attn_n, v_new)
        S = S * jnp.exp(gl_n)[..., None, None] + jnp.einsum('bhcd,bhce->bhde', k_n, v_new)
        return S, o

    S0 = jnp.zeros((B, H, Dk, Dv), jnp.float32)
    _, o = lax.scan(step, S0, xs)
    o = jnp.transpose(o, (1, 0, 3, 2, 4)).reshape(B, T, H, Dv)
    return o.astype(dtype)


def gated_deltanet_mix(qkv, z, a_raw, b_raw, conv_w, a_log, dt_bias, norm_w):
    B, T, _ = qkv.shape
    qkv = jax.nn.silu(causal_depthwise_conv(qkv, conv_w))
    q, k, v = _split(qkv, (GDN_WIDTH, GDN_WIDTH, GDN_WIDTH))
    q = l2norm(q.reshape(B, T, GDN_HEADS, GDN_HEAD_DIM))
    k = l2norm(k.reshape(B, T, GDN_HEADS, GDN_HEAD_DIM))
    v = v.reshape(B, T, GDN_HEADS, GDN_HEAD_DIM)
    beta = jax.nn.sigmoid(b_raw)
    g = -jnp.exp(a_log.astype(jnp.float32)) * jax.nn.softplus(a_raw.astype(jnp.float32) + dt_bias.astype(jnp.float32))
    o = chunk_gated_delta_rule(q, k, v, g, beta)
    o = rms_norm(o, norm_w) * jax.nn.silu(z.reshape(B, T, GDN_HEADS, GDN_HEAD_DIM))
    return o.reshape(B, T, GDN_WIDTH)


def _fwd_setup_inputs(seed: int = 0) -> dict:
    key = jax.random.key(seed)
    ks = jax.random.split(key, 32)
    L = DEPTH

    def nrm(k, shape, scale):
        return jax.random.normal(k, shape, jnp.float32) * scale

    dt = jnp.exp(jax.random.uniform(ks[17], (L, GDN_HEADS), minval=float(np.log(1e-3)), maxval=float(np.log(1e-1))))
    return {
        'x': nrm(ks[0], (BATCH, SEQ, D_MODEL), 1.0),
        'norm1_g': 1.0 + nrm(ks[1], (L, D_MODEL), 0.02),
        'w_in': nrm(ks[2], (L, D_MODEL, IN_WIDTH), D_MODEL ** -0.5),
        'rwkv_mu': jax.random.uniform(ks[3], (L, RWKV_SHIFT_WIDTH)),
        'rwkv_w0': jax.random.uniform(ks[4], (L, RWKV_WIDTH), minval=-6.5, maxval=-1.0),
        'rwkv_w2': nrm(ks[5], (L, DECAY_LORA, RWKV_WIDTH), 0.5 * DECAY_LORA ** -0.5),
        'rwkv_a0': nrm(ks[6], (L, RWKV_WIDTH), 0.1),
        'rwkv_a2': nrm(ks[7], (L, ICLR_LORA, RWKV_WIDTH), 0.5 * ICLR_LORA ** -0.5),
        'rwkv_g2': nrm(ks[8], (L, GATE_LORA, RWKV_WIDTH), GATE_LORA ** -0.5),
        'rwkv_k_k': 0.85 + nrm(ks[9], (L, RWKV_WIDTH), 0.05),
        'rwkv_k_a': 1.0 + nrm(ks[10], (L, RWKV_WIDTH), 0.05),
        'rwkv_r_k': nrm(ks[11], (L, RWKV_HEADS, RWKV_HEAD_DIM), 0.1),
        'rwkv_ln_w': 1.0 + nrm(ks[12], (L, RWKV_WIDTH), 0.02),
        'rwkv_ln_b': nrm(ks[13], (L, RWKV_WIDTH), 0.02),
        'rwkv_proj': nrm(ks[14], (L, RWKV_WIDTH, D_MODEL), RWKV_WIDTH ** -0.5),
        'gdn_conv_w': nrm(ks[15], (L, GDN_CONV, 3 * GDN_WIDTH), GDN_CONV ** -0.5),
        'gdn_a_log': jnp.log(jax.random.uniform(ks[16], (L, GDN_HEADS), minval=1.0, maxval=16.0)),
        'gdn_dt_bias': dt + jnp.log(-jnp.expm1(-dt)),
        'gdn_norm_w': 1.0 + nrm(ks[18], (L, GDN_HEAD_DIM), 0.02),
        'gdn_proj': nrm(ks[19], (L, GDN_WIDTH, D_MODEL), GDN_WIDTH ** -0.5),
        'w_out': nrm(ks[20], (L, D_MODEL, D_MODEL), D_MODEL ** -0.5),
        'norm2_g': 1.0 + nrm(ks[21], (L, D_MODEL), 0.02),
        'ffn_up': nrm(ks[22], (L, D_MODEL, 2 * FFN_HIDDEN), D_MODEL ** -0.5),
        'ffn_conv_w': nrm(ks[23], (L, FFN_CONV, 2 * FFN_HIDDEN), FFN_CONV ** -0.5),
        'ffn_down': nrm(ks[24], (L, FFN_HIDDEN, D_MODEL), FFN_HIDDEN ** -0.5),
        'final_g': 1.0 + nrm(ks[25], (D_MODEL,), 0.02),
    }


def _fwd_reference(x, norm1_g, w_in, rwkv_mu, rwkv_w0, rwkv_w2, rwkv_a0, rwkv_a2, rwkv_g2, rwkv_k_k,
              rwkv_k_a, rwkv_r_k, rwkv_ln_w, rwkv_ln_b, rwkv_proj, gdn_conv_w, gdn_a_log, gdn_dt_bias,
              gdn_norm_w, gdn_proj, w_out, norm2_g, ffn_up, ffn_conv_w, ffn_down, final_g):
    for l in range(DEPTH):
        u = rms_norm(x, norm1_g[l])
        p = u @ w_in[l]
        p_rwkv, qkv, z, a_raw, b_raw, gate_a, gate_b = _split(p, IN_SPLITS)
        y_a = rwkv7_mix(p_rwkv, rwkv_mu[l], rwkv_w0[l], rwkv_w2[l], rwkv_a0[l], rwkv_a2[l], rwkv_g2[l],
                        rwkv_k_k[l], rwkv_k_a[l], rwkv_r_k[l], rwkv_ln_w[l], rwkv_ln_b[l]) @ rwkv_proj[l]
        y_b = gated_deltanet_mix(qkv, z, a_raw, b_raw, gdn_conv_w[l], gdn_a_log[l], gdn_dt_bias[l],
                                 gdn_norm_w[l]) @ gdn_proj[l]
        mixed = jax.nn.sigmoid(gate_a) * y_a + jax.nn.sigmoid(gate_b) * y_b
        x = x + mixed @ w_out[l]
        h = rms_norm(x, norm2_g[l]) @ ffn_up[l]
        h = causal_depthwise_conv(h, ffn_conv_w[l])
        h_gate, h_up = _split(h, (FFN_HIDDEN, FFN_HIDDEN))
        x = x + (jax.nn.silu(h_gate) * h_up) @ ffn_down[l]
    return rms_norm(x, final_g)


import jax as _jax
import jax.numpy as _jnp

TWIN_FORMAT = 'train_step'
FWD_PARAMS = ['x', 'norm1_g', 'w_in', 'rwkv_mu', 'rwkv_w0', 'rwkv_w2', 'rwkv_a0', 'rwkv_a2', 'rwkv_g2', 'rwkv_k_k', 'rwkv_k_a', 'rwkv_r_k', 'rwkv_ln_w', 'rwkv_ln_b', 'rwkv_proj', 'gdn_conv_w', 'gdn_a_log', 'gdn_dt_bias', 'gdn_norm_w', 'gdn_proj', 'w_out', 'norm2_g', 'ffn_up', 'ffn_conv_w', 'ffn_down', 'final_g']
TWIN_WEIGHTS = ['norm1_g', 'w_in', 'rwkv_mu', 'rwkv_w0', 'rwkv_w2', 'rwkv_a0', 'rwkv_a2', 'rwkv_g2', 'rwkv_k_k', 'rwkv_k_a', 'rwkv_r_k', 'rwkv_ln_w', 'rwkv_ln_b', 'rwkv_proj', 'gdn_conv_w', 'gdn_a_log', 'gdn_dt_bias', 'gdn_norm_w', 'gdn_proj', 'w_out', 'norm2_g', 'ffn_up', 'ffn_conv_w', 'ffn_down', 'final_g']
TWIN_DIFF_INPUT = 'x'
TWIN_INPUTS = ['x', 'norm1_g', 'w_in', 'rwkv_mu', 'rwkv_w0', 'rwkv_w2', 'rwkv_a0', 'rwkv_a2', 'rwkv_g2', 'rwkv_k_k', 'rwkv_k_a', 'rwkv_r_k', 'rwkv_ln_w', 'rwkv_ln_b', 'rwkv_proj', 'gdn_conv_w', 'gdn_a_log', 'gdn_dt_bias', 'gdn_norm_w', 'gdn_proj', 'w_out', 'norm2_g', 'ffn_up', 'ffn_conv_w', 'ffn_down', 'final_g', 'loss_target', 'm_norm1_g', 'm_w_in', 'm_rwkv_mu', 'm_rwkv_w0', 'm_rwkv_w2', 'm_rwkv_a0', 'm_rwkv_a2', 'm_rwkv_g2', 'm_rwkv_k_k', 'm_rwkv_k_a', 'm_rwkv_r_k', 'm_rwkv_ln_w', 'm_rwkv_ln_b', 'm_rwkv_proj', 'm_gdn_conv_w', 'm_gdn_a_log', 'm_gdn_dt_bias', 'm_gdn_norm_w', 'm_gdn_proj', 'm_w_out', 'm_norm2_g', 'm_ffn_up', 'm_ffn_conv_w', 'm_ffn_down', 'm_final_g', 'v_norm1_g', 'v_w_in', 'v_rwkv_mu', 'v_rwkv_w0', 'v_rwkv_w2', 'v_rwkv_a0', 'v_rwkv_a2', 'v_rwkv_g2', 'v_rwkv_k_k', 'v_rwkv_k_a', 'v_rwkv_r_k', 'v_rwkv_ln_w', 'v_rwkv_ln_b', 'v_rwkv_proj', 'v_gdn_conv_w', 'v_gdn_a_log', 'v_gdn_dt_bias', 'v_gdn_norm_w', 'v_gdn_proj', 'v_w_out', 'v_norm2_g', 'v_ffn_up', 'v_ffn_conv_w', 'v_ffn_down', 'v_final_g']
TWIN_OUTPUTS = ['loss', 'grad_x', 'grad_norm1_g', 'grad_w_in', 'grad_rwkv_mu', 'grad_rwkv_w0', 'grad_rwkv_w2', 'grad_rwkv_a0', 'grad_rwkv_a2', 'grad_rwkv_g2', 'grad_rwkv_k_k', 'grad_rwkv_k_a', 'grad_rwkv_r_k', 'grad_rwkv_ln_w', 'grad_rwkv_ln_b', 'grad_rwkv_proj', 'grad_gdn_conv_w', 'grad_gdn_a_log', 'grad_gdn_dt_bias', 'grad_gdn_norm_w', 'grad_gdn_proj', 'grad_w_out', 'grad_norm2_g', 'grad_ffn_up', 'grad_ffn_conv_w', 'grad_ffn_down', 'grad_final_g', 'delta_norm1_g', 'delta_w_in', 'delta_rwkv_mu', 'delta_rwkv_w0', 'delta_rwkv_w2', 'delta_rwkv_a0', 'delta_rwkv_a2', 'delta_rwkv_g2', 'delta_rwkv_k_k', 'delta_rwkv_k_a', 'delta_rwkv_r_k', 'delta_rwkv_ln_w', 'delta_rwkv_ln_b', 'delta_rwkv_proj', 'delta_gdn_conv_w', 'delta_gdn_a_log', 'delta_gdn_dt_bias', 'delta_gdn_norm_w', 'delta_gdn_proj', 'delta_w_out', 'delta_norm2_g', 'delta_ffn_up', 'delta_ffn_conv_w', 'delta_ffn_down', 'delta_final_g', 'new_m_norm1_g', 'new_m_w_in', 'new_m_rwkv_mu', 'new_m_rwkv_w0', 'new_m_rwkv_w2', 'new_m_rwkv_a0', 'new_m_rwkv_a2', 'new_m_rwkv_g2', 'new_m_rwkv_k_k', 'new_m_rwkv_k_a', 'new_m_rwkv_r_k', 'new_m_rwkv_ln_w', 'new_m_rwkv_ln_b', 'new_m_rwkv_proj', 'new_m_gdn_conv_w', 'new_m_gdn_a_log', 'new_m_gdn_dt_bias', 'new_m_gdn_norm_w', 'new_m_gdn_proj', 'new_m_w_out', 'new_m_norm2_g', 'new_m_ffn_up', 'new_m_ffn_conv_w', 'new_m_ffn_down', 'new_m_final_g', 'new_v_norm1_g', 'new_v_w_in', 'new_v_rwkv_mu', 'new_v_rwkv_w0', 'new_v_rwkv_w2', 'new_v_rwkv_a0', 'new_v_rwkv_a2', 'new_v_rwkv_g2', 'new_v_rwkv_k_k', 'new_v_rwkv_k_a', 'new_v_rwkv_r_k', 'new_v_rwkv_ln_w', 'new_v_rwkv_ln_b', 'new_v_rwkv_proj', 'new_v_gdn_conv_w', 'new_v_gdn_a_log', 'new_v_gdn_dt_bias', 'new_v_gdn_norm_w', 'new_v_gdn_proj', 'new_v_w_out', 'new_v_norm2_g', 'new_v_ffn_up', 'new_v_ffn_conv_w', 'new_v_ffn_down', 'new_v_final_g']
TWIN_LEAF_KINDS = {'loss': 'loss', 'grad_x': 'grad_x', 'grad_norm1_g': 'grad_w', 'grad_w_in': 'grad_w', 'grad_rwkv_mu': 'grad_w', 'grad_rwkv_w0': 'grad_w', 'grad_rwkv_w2': 'grad_w', 'grad_rwkv_a0': 'grad_w', 'grad_rwkv_a2': 'grad_w', 'grad_rwkv_g2': 'grad_w', 'grad_rwkv_k_k': 'grad_w', 'grad_rwkv_k_a': 'grad_w', 'grad_rwkv_r_k': 'grad_w', 'grad_rwkv_ln_w': 'grad_w', 'grad_rwkv_ln_b': 'grad_w', 'grad_rwkv_proj': 'grad_w', 'grad_gdn_conv_w': 'grad_w', 'grad_gdn_a_log': 'grad_w', 'grad_gdn_dt_bias': 'grad_w', 'grad_gdn_norm_w': 'grad_w', 'grad_gdn_proj': 'grad_w', 'grad_w_out': 'grad_w', 'grad_norm2_g': 'grad_w', 'grad_ffn_up': 'grad_w', 'grad_ffn_conv_w': 'grad_w', 'grad_ffn_down': 'grad_w', 'grad_final_g': 'grad_w', 'delta_norm1_g': 'delta_w', 'delta_w_in': 'delta_w', 'delta_rwkv_mu': 'delta_w', 'delta_rwkv_w0': 'delta_w', 'delta_rwkv_w2': 'delta_w', 'delta_rwkv_a0': 'delta_w', 'delta_rwkv_a2': 'delta_w', 'delta_rwkv_g2': 'delta_w', 'delta_rwkv_k_k': 'delta_w', 'delta_rwkv_k_a': 'delta_w', 'delta_rwkv_r_k': 'delta_w', 'delta_rwkv_ln_w': 'delta_w', 'delta_rwkv_ln_b': 'delta_w', 'delta_rwkv_proj': 'delta_w', 'delta_gdn_conv_w': 'delta_w', 'delta_gdn_a_log': 'delta_w', 'delta_gdn_dt_bias': 'delta_w', 'delta_gdn_norm_w': 'delta_w', 'delta_gdn_proj': 'delta_w', 'delta_w_out': 'delta_w', 'delta_norm2_g': 'delta_w', 'delta_ffn_up': 'delta_w', 'delta_ffn_conv_w': 'delta_w', 'delta_ffn_down': 'delta_w', 'delta_final_g': 'delta_w', 'new_m_norm1_g': 'new_m', 'new_m_w_in': 'new_m', 'new_m_rwkv_mu': 'new_m', 'new_m_rwkv_w0': 'new_m', 'new_m_rwkv_w2': 'new_m', 'new_m_rwkv_a0': 'new_m', 'new_m_rwkv_a2': 'new_m', 'new_m_rwkv_g2': 'new_m', 'new_m_rwkv_k_k': 'new_m', 'new_m_rwkv_k_a': 'new_m', 'new_m_rwkv_r_k': 'new_m', 'new_m_rwkv_ln_w': 'new_m', 'new_m_rwkv_ln_b': 'new_m', 'new_m_rwkv_proj': 'new_m', 'new_m_gdn_conv_w': 'new_m', 'new_m_gdn_a_log': 'new_m', 'new_m_gdn_dt_bias': 'new_m', 'new_m_gdn_norm_w': 'new_m', 'new_m_gdn_proj': 'new_m', 'new_m_w_out': 'new_m', 'new_m_norm2_g': 'new_m', 'new_m_ffn_up': 'new_m', 'new_m_ffn_conv_w': 'new_m', 'new_m_ffn_down': 'new_m', 'new_m_final_g': 'new_m', 'new_v_norm1_g': 'new_v', 'new_v_w_in': 'new_v', 'new_v_rwkv_mu': 'new_v', 'new_v_rwkv_w0': 'new_v', 'new_v_rwkv_w2': 'new_v', 'new_v_rwkv_a0': 'new_v', 'new_v_rwkv_a2': 'new_v', 'new_v_rwkv_g2': 'new_v', 'new_v_rwkv_k_k': 'new_v', 'new_v_rwkv_k_a': 'new_v', 'new_v_rwkv_r_k': 'new_v', 'new_v_rwkv_ln_w': 'new_v', 'new_v_rwkv_ln_b': 'new_v', 'new_v_rwkv_proj': 'new_v', 'new_v_gdn_conv_w': 'new_v', 'new_v_gdn_a_log': 'new_v', 'new_v_gdn_dt_bias': 'new_v', 'new_v_gdn_norm_w': 'new_v', 'new_v_gdn_proj': 'new_v', 'new_v_w_out': 'new_v', 'new_v_norm2_g': 'new_v', 'new_v_ffn_up': 'new_v', 'new_v_ffn_conv_w': 'new_v', 'new_v_ffn_down': 'new_v', 'new_v_final_g': 'new_v'}


def _forward(args):
    return _fwd_reference(*[args[k] for k in FWD_PARAMS])


def _output_shape():
    out = _jax.eval_shape(lambda: _forward(_fwd_setup_inputs(0)))
    return out.shape, out.dtype

N_MICROBATCH = 1
ADAM_LR = 0.001
ADAM_B1 = 0.9
ADAM_B2 = 0.999
ADAM_EPS = 1e-08
ADAM_WD = 0.01
ADAM_STEP = 10
PER_EXAMPLE_BATCH_AXIS = {'x': 0, 'loss_target': 0}
SHARED_INPUTS = []
_WEIGHT_DTYPES = {'norm1_g': _jnp.float32, 'w_in': _jnp.float32, 'rwkv_mu': _jnp.float32, 'rwkv_w0': _jnp.float32, 'rwkv_w2': _jnp.float32, 'rwkv_a0': _jnp.float32, 'rwkv_a2': _jnp.float32, 'rwkv_g2': _jnp.float32, 'rwkv_k_k': _jnp.float32, 'rwkv_k_a': _jnp.float32, 'rwkv_r_k': _jnp.float32, 'rwkv_ln_w': _jnp.float32, 'rwkv_ln_b': _jnp.float32, 'rwkv_proj': _jnp.float32, 'gdn_conv_w': _jnp.float32, 'gdn_a_log': _jnp.float32, 'gdn_dt_bias': _jnp.float32, 'gdn_norm_w': _jnp.float32, 'gdn_proj': _jnp.float32, 'w_out': _jnp.float32, 'norm2_g': _jnp.float32, 'ffn_up': _jnp.float32, 'ffn_conv_w': _jnp.float32, 'ffn_down': _jnp.float32, 'final_g': _jnp.float32}
MOMENT_SCALE = {'norm1_g': 1.590245e-01, 'w_in': 5.989317e-02, 'rwkv_mu': 1.306198e-01, 'rwkv_w0': 2.486279e-02, 'rwkv_w2': 3.121470e-03, 'rwkv_a0': 3.251867e-02, 'rwkv_a2': 3.014036e-02, 'rwkv_g2': 7.533163e-02, 'rwkv_k_k': 1.462364e-01, 'rwkv_k_a': 8.062494e-02, 'rwkv_r_k': 1.660545e-01, 'rwkv_ln_w': 7.745605e-02, 'rwkv_ln_b': 8.106290e-02, 'rwkv_proj': 5.470059e-02, 'gdn_conv_w': 6.315749e-02, 'gdn_a_log': 2.944775e-01, 'gdn_dt_bias': 2.820463e-01, 'gdn_norm_w': 1.695255e-01, 'gdn_proj': 5.868606e-02, 'w_out': 8.026407e-02, 'norm2_g': 1.325319e-01, 'ffn_up': 5.395309e-02, 'ffn_conv_w': 5.380756e-02, 'ffn_down': 8.827554e-02, 'final_g': 3.195735e+01}


def _to_microbatches(a, axis):
    t = _jnp.moveaxis(a, axis, 0)
    t = t.reshape((N_MICROBATCH, t.shape[0] // N_MICROBATCH) + t.shape[1:])
    return _jnp.moveaxis(t, 1, axis + 1)


def setup_inputs(seed: int = 0) -> dict:
    inp = _fwd_setup_inputs(seed)
    key = _jax.random.fold_in(_jax.random.key(seed), 7919)
    shape, _ = _output_shape()
    out = dict(inp)
    out["loss_target"] = _jax.random.normal(_jax.random.fold_in(key, 0), shape, _jnp.float32)
    for i, name in enumerate(TWIN_WEIGHTS):
        w = inp[name].astype(_jnp.float32)
        if MOMENT_SCALE is None:
            s = _jnp.sqrt(_jnp.mean(_jnp.square(w)) + 1e-30)
        else:
            s = MOMENT_SCALE[name]
        km, kv = _jax.random.split(_jax.random.fold_in(key, i + 1))
        out[name] = w
        out["m_" + name] = s * _jax.random.normal(km, w.shape, _jnp.float32)
        out["v_" + name] = (s * s) * _jax.random.uniform(kv, w.shape, _jnp.float32, 0.5, 1.5)
    if N_MICROBATCH > 1:
        for name, axis in PER_EXAMPLE_BATCH_AXIS.items():
            out[name] = _to_microbatches(out[name], axis)
    return {'x': out['x'], 'norm1_g': out['norm1_g'], 'w_in': out['w_in'], 'rwkv_mu': out['rwkv_mu'], 'rwkv_w0': out['rwkv_w0'], 'rwkv_w2': out['rwkv_w2'], 'rwkv_a0': out['rwkv_a0'], 'rwkv_a2': out['rwkv_a2'], 'rwkv_g2': out['rwkv_g2'], 'rwkv_k_k': out['rwkv_k_k'], 'rwkv_k_a': out['rwkv_k_a'], 'rwkv_r_k': out['rwkv_r_k'], 'rwkv_ln_w': out['rwkv_ln_w'], 'rwkv_ln_b': out['rwkv_ln_b'], 'rwkv_proj': out['rwkv_proj'], 'gdn_conv_w': out['gdn_conv_w'], 'gdn_a_log': out['gdn_a_log'], 'gdn_dt_bias': out['gdn_dt_bias'], 'gdn_norm_w': out['gdn_norm_w'], 'gdn_proj': out['gdn_proj'], 'w_out': out['w_out'], 'norm2_g': out['norm2_g'], 'ffn_up': out['ffn_up'], 'ffn_conv_w': out['ffn_conv_w'], 'ffn_down': out['ffn_down'], 'final_g': out['final_g'], 'loss_target': out['loss_target'], 'm_norm1_g': out['m_norm1_g'], 'm_w_in': out['m_w_in'], 'm_rwkv_mu': out['m_rwkv_mu'], 'm_rwkv_w0': out['m_rwkv_w0'], 'm_rwkv_w2': out['m_rwkv_w2'], 'm_rwkv_a0': out['m_rwkv_a0'], 'm_rwkv_a2': out['m_rwkv_a2'], 'm_rwkv_g2': out['m_rwkv_g2'], 'm_rwkv_k_k': out['m_rwkv_k_k'], 'm_rwkv_k_a': out['m_rwkv_k_a'], 'm_rwkv_r_k': out['m_rwkv_r_k'], 'm_rwkv_ln_w': out['m_rwkv_ln_w'], 'm_rwkv_ln_b': out['m_rwkv_ln_b'], 'm_rwkv_proj': out['m_rwkv_proj'], 'm_gdn_conv_w': out['m_gdn_conv_w'], 'm_gdn_a_log': out['m_gdn_a_log'], 'm_gdn_dt_bias': out['m_gdn_dt_bias'], 'm_gdn_norm_w': out['m_gdn_norm_w'], 'm_gdn_proj': out['m_gdn_proj'], 'm_w_out': out['m_w_out'], 'm_norm2_g': out['m_norm2_g'], 'm_ffn_up': out['m_ffn_up'], 'm_ffn_conv_w': out['m_ffn_conv_w'], 'm_ffn_down': out['m_ffn_down'], 'm_final_g': out['m_final_g'], 'v_norm1_g': out['v_norm1_g'], 'v_w_in': out['v_w_in'], 'v_rwkv_mu': out['v_rwkv_mu'], 'v_rwkv_w0': out['v_rwkv_w0'], 'v_rwkv_w2': out['v_rwkv_w2'], 'v_rwkv_a0': out['v_rwkv_a0'], 'v_rwkv_a2': out['v_rwkv_a2'], 'v_rwkv_g2': out['v_rwkv_g2'], 'v_rwkv_k_k': out['v_rwkv_k_k'], 'v_rwkv_k_a': out['v_rwkv_k_a'], 'v_rwkv_r_k': out['v_rwkv_r_k'], 'v_rwkv_ln_w': out['v_rwkv_ln_w'], 'v_rwkv_ln_b': out['v_rwkv_ln_b'], 'v_rwkv_proj': out['v_rwkv_proj'], 'v_gdn_conv_w': out['v_gdn_conv_w'], 'v_gdn_a_log': out['v_gdn_a_log'], 'v_gdn_dt_bias': out['v_gdn_dt_bias'], 'v_gdn_norm_w': out['v_gdn_norm_w'], 'v_gdn_proj': out['v_gdn_proj'], 'v_w_out': out['v_w_out'], 'v_norm2_g': out['v_norm2_g'], 'v_ffn_up': out['v_ffn_up'], 'v_ffn_conv_w': out['v_ffn_conv_w'], 'v_ffn_down': out['v_ffn_down'], 'v_final_g': out['v_final_g']}


def _loss(weights, diff, rest, loss_target):
    with _jax.named_scope("forward"):
        args = {**rest, TWIN_DIFF_INPUT: diff, **{k: w.astype(_WEIGHT_DTYPES[k]) for k, w in weights.items()}}
        y = _forward(args)
    with _jax.named_scope("loss_head"):
        err = _jnp.square(y.astype(_jnp.float32) - loss_target)
        return 0.5 * _jnp.sum(_jnp.mean(err, axis=-1)) if err.ndim else 0.5 * err


def _adamw(w, g, m, v):
    m = ADAM_B1 * m + (1.0 - ADAM_B1) * g
    v = ADAM_B2 * v + (1.0 - ADAM_B2) * _jnp.square(g)
    m_hat = m / (1.0 - ADAM_B1 ** ADAM_STEP)
    v_hat = v / (1.0 - ADAM_B2 ** ADAM_STEP)
    delta = -ADAM_LR * (m_hat / (_jnp.sqrt(v_hat) + ADAM_EPS) + ADAM_WD * w)
    return delta, m, v


def reference(x, norm1_g, w_in, rwkv_mu, rwkv_w0, rwkv_w2, rwkv_a0, rwkv_a2, rwkv_g2, rwkv_k_k, rwkv_k_a, rwkv_r_k, rwkv_ln_w, rwkv_ln_b, rwkv_proj, gdn_conv_w, gdn_a_log, gdn_dt_bias, gdn_norm_w, gdn_proj, w_out, norm2_g, ffn_up, ffn_conv_w, ffn_down, final_g, loss_target, m_norm1_g, m_w_in, m_rwkv_mu, m_rwkv_w0, m_rwkv_w2, m_rwkv_a0, m_rwkv_a2, m_rwkv_g2, m_rwkv_k_k, m_rwkv_k_a, m_rwkv_r_k, m_rwkv_ln_w, m_rwkv_ln_b, m_rwkv_proj, m_gdn_conv_w, m_gdn_a_log, m_gdn_dt_bias, m_gdn_norm_w, m_gdn_proj, m_w_out, m_norm2_g, m_ffn_up, m_ffn_conv_w, m_ffn_down, m_final_g, v_norm1_g, v_w_in, v_rwkv_mu, v_rwkv_w0, v_rwkv_w2, v_rwkv_a0, v_rwkv_a2, v_rwkv_g2, v_rwkv_k_k, v_rwkv_k_a, v_rwkv_r_k, v_rwkv_ln_w, v_rwkv_ln_b, v_rwkv_proj, v_gdn_conv_w, v_gdn_a_log, v_gdn_dt_bias, v_gdn_norm_w, v_gdn_proj, v_w_out, v_norm2_g, v_ffn_up, v_ffn_conv_w, v_ffn_down, v_final_g):
    given = dict(x=x, norm1_g=norm1_g, w_in=w_in, rwkv_mu=rwkv_mu, rwkv_w0=rwkv_w0, rwkv_w2=rwkv_w2, rwkv_a0=rwkv_a0, rwkv_a2=rwkv_a2, rwkv_g2=rwkv_g2, rwkv_k_k=rwkv_k_k, rwkv_k_a=rwkv_k_a, rwkv_r_k=rwkv_r_k, rwkv_ln_w=rwkv_ln_w, rwkv_ln_b=rwkv_ln_b, rwkv_proj=rwkv_proj, gdn_conv_w=gdn_conv_w, gdn_a_log=gdn_a_log, gdn_dt_bias=gdn_dt_bias, gdn_norm_w=gdn_norm_w, gdn_proj=gdn_proj, w_out=w_out, norm2_g=norm2_g, ffn_up=ffn_up, ffn_conv_w=ffn_conv_w, ffn_down=ffn_down, final_g=final_g, loss_target=loss_target, m_norm1_g=m_norm1_g, m_w_in=m_w_in, m_rwkv_mu=m_rwkv_mu, m_rwkv_w0=m_rwkv_w0, m_rwkv_w2=m_rwkv_w2, m_rwkv_a0=m_rwkv_a0, m_rwkv_a2=m_rwkv_a2, m_rwkv_g2=m_rwkv_g2, m_rwkv_k_k=m_rwkv_k_k, m_rwkv_k_a=m_rwkv_k_a, m_rwkv_r_k=m_rwkv_r_k, m_rwkv_ln_w=m_rwkv_ln_w, m_rwkv_ln_b=m_rwkv_ln_b, m_rwkv_proj=m_rwkv_proj, m_gdn_conv_w=m_gdn_conv_w, m_gdn_a_log=m_gdn_a_log, m_gdn_dt_bias=m_gdn_dt_bias, m_gdn_norm_w=m_gdn_norm_w, m_gdn_proj=m_gdn_proj, m_w_out=m_w_out, m_norm2_g=m_norm2_g, m_ffn_up=m_ffn_up, m_ffn_conv_w=m_ffn_conv_w, m_ffn_down=m_ffn_down, m_final_g=m_final_g, v_norm1_g=v_norm1_g, v_w_in=v_w_in, v_rwkv_mu=v_rwkv_mu, v_rwkv_w0=v_rwkv_w0, v_rwkv_w2=v_rwkv_w2, v_rwkv_a0=v_rwkv_a0, v_rwkv_a2=v_rwkv_a2, v_rwkv_g2=v_rwkv_g2, v_rwkv_k_k=v_rwkv_k_k, v_rwkv_k_a=v_rwkv_k_a, v_rwkv_r_k=v_rwkv_r_k, v_rwkv_ln_w=v_rwkv_ln_w, v_rwkv_ln_b=v_rwkv_ln_b, v_rwkv_proj=v_rwkv_proj, v_gdn_conv_w=v_gdn_conv_w, v_gdn_a_log=v_gdn_a_log, v_gdn_dt_bias=v_gdn_dt_bias, v_gdn_norm_w=v_gdn_norm_w, v_gdn_proj=v_gdn_proj, v_w_out=v_w_out, v_norm2_g=v_norm2_g, v_ffn_up=v_ffn_up, v_ffn_conv_w=v_ffn_conv_w, v_ffn_down=v_ffn_down, v_final_g=v_final_g)
    weights = {n: given[n] for n in TWIN_WEIGHTS}
    shared = {n: given[n] for n in SHARED_INPUTS}
    per_example = {n: given[n] for n in ['x']}
    grad_fn = _jax.value_and_grad(_loss, argnums=(0, 1))

    def one_microbatch(ex, loss_target):
        ex = dict(ex)
        diff = ex.pop(TWIN_DIFF_INPUT)
        return grad_fn(weights, diff, {**shared, **ex}, loss_target)

    if N_MICROBATCH == 1:
        loss, (grad_w, grad_x) = one_microbatch(per_example, given["loss_target"])
    else:
        def body(carry, xs):
            loss_sum, grad_sum = carry
            l_k, (gw_k, gx_k) = one_microbatch(xs[0], xs[1])
            with _jax.named_scope("update"):
                return (loss_sum + l_k, _jax.tree.map(_jnp.add, grad_sum, gw_k)), gx_k

        init = (_jnp.zeros((), _jnp.float32), _jax.tree.map(_jnp.zeros_like, weights))
        (loss, grad_w), grad_x = _jax.lax.scan(body, init, (per_example, given["loss_target"]))
    with _jax.named_scope("update"):
        delta_w, new_m, new_v = {}, {}, {}
        for n in TWIN_WEIGHTS:
            delta_w[n], new_m[n], new_v[n] = _adamw(weights[n], grad_w[n], given["m_" + n], given["v_" + n])
    return (loss, grad_x, *[grad_w[n] for n in TWIN_WEIGHTS], *[delta_w[n] for n in TWIN_WEIGHTS],
            *[new_m[n] for n in TWIN_WEIGHTS], *[new_v[n] for n in TWIN_WEIGHTS])
```

```python
import functools

import jax
import jax.numpy as jnp
import numpy as np
from jax import lax
from jax.experimental import pallas as pl
from jax.experimental.pallas import tpu as pltpu

F32 = jnp.float32
BF16 = jnp.bfloat16
HI = lax.Precision.HIGHEST

N_DEV = 8
D_MODEL = 1024
CHUNK = 64
RWKV_HEADS = 8
RWKV_HEAD_DIM = 64
RWKV_WIDTH = 512
GDN_HEADS = 4
GDN_HEAD_DIM = 128
GDN_WIDTH = 512
GDN_CONV = 4
FFN_HIDDEN = 2816
FFN_CONV = 3
NORM_EPS = 1e-6
L2_EPS = 1e-6
RWKV_GN_EPS = 64e-5
LANES = 128
SUBLANES = 8
VMEM_LIMIT = 56 * 1024 * 1024

ADAM_LR = 0.001
ADAM_B1 = 0.9
ADAM_B2 = 0.999
ADAM_EPS = 1e-08
ADAM_WD = 0.01
ADAM_STEP = 10

OFF_GA, OFF_GB, OFF_RKV, OFF_LO, OFF_QKV, OFF_Z, OFF_AB, CAT_W = 0, 1024, 2048, 5120, 5376, 6912, 7424, 7680

WEIGHT_NAMES = ['norm1_g', 'w_in', 'rwkv_mu', 'rwkv_w0', 'rwkv_w2', 'rwkv_a0', 'rwkv_a2', 'rwkv_g2', 'rwkv_k_k', 'rwkv_k_a',
                'rwkv_r_k', 'rwkv_ln_w', 'rwkv_ln_b', 'rwkv_proj', 'gdn_conv_w', 'gdn_a_log', 'gdn_dt_bias', 'gdn_norm_w',
                'gdn_proj', 'w_out', 'norm2_g', 'ffn_up', 'ffn_conv_w', 'ffn_down', 'final_g']
BIG_SHARDED = ['w_in', 'ffn_up', 'ffn_down', 'w_out', 'rwkv_proj', 'gdn_proj']
SMALL_SHARDED = ['rwkv_w2', 'rwkv_a2', 'rwkv_g2', 'gdn_conv_w', 'ffn_conv_w']
SHARD_AXIS = {'w_in': 1, 'ffn_up': 1, 'ffn_down': 0, 'w_out': 0, 'rwkv_proj': 1, 'gdn_proj': 1,
              'rwkv_w2': 1, 'rwkv_a2': 1, 'rwkv_g2': 1, 'gdn_conv_w': 1, 'ffn_conv_w': 1}
REPLICATED = [n for n in WEIGHT_NAMES if n not in SHARD_AXIS]


def _cparams(sem=None):
    kw = dict(vmem_limit_bytes=VMEM_LIMIT)
    if sem is not None:
        kw['dimension_semantics'] = sem
    return pltpu.CompilerParams(**kw)


def _dhi(a, b):
    return jnp.dot(a, b, precision=HI, preferred_element_type=F32)


def _dnt(a, b):
    return lax.dot_general(a, b, (((1,), (1,)), ((), ())), precision=HI, preferred_element_type=F32)


def _dtn(a, b):
    return lax.dot_general(a, b, (((0,), (0,)), ((), ())), precision=HI, preferred_element_type=F32)


def _sigmoid(z):
    return 1.0 / (1.0 + jnp.exp(-z))


def _silu(z):
    return z * _sigmoid(z)


def _softplus(z):
    return jnp.maximum(z, 0.0) + jnp.log(1.0 + jnp.exp(-jnp.abs(z)))


def _rms(t, gain):
    return t * lax.rsqrt(jnp.mean(t * t, axis=-1, keepdims=True) + NORM_EPS) * gain


def _iotas(c):
    return lax.broadcasted_iota(jnp.int32, (c, c), 0), lax.broadcasted_iota(jnp.int32, (c, c), 1)


def _unit_lower_inverse(xm, eye):
    t = eye + xm
    p = xm
    for _ in range(5):
        p = _dhi(p, p)
        t = t + _dhi(t, p)
    return t


def _rwkv_head(pr, pk, pv, plo, qr, qk, qv, qlo, s0, pp, mulo, wl):
    c = pr.shape[0]
    ri, ci = _iotas(c)
    sh = (ri == ci + 1).astype(F32)
    e0 = (lax.broadcasted_iota(jnp.int32, (c, 1), 0) == 0).astype(F32)

    def mix(p, q, mu):
        prev = _dhi(sh, p) + e0 * q
        return p + (prev - p) * mu

    r = mix(pr, qr, pp[0:1])
    k = mix(pk, qk, pp[1:2])
    v = mix(pv, qv, pp[2:3])
    lo = mix(plo, qlo, mulo)
    w0, a0, k_k, k_a, ln_w, ln_b, r_k = (pp[i:i + 1] for i in range(3, 10))
    zw = _dhi(jnp.tanh(lo), wl[0])
    za = _dhi(lo, wl[1])
    g = _dhi(_sigmoid(lo), wl[2])
    w_log = -_softplus(-(w0 + zw)) - 0.5
    lw = -jnp.exp(w_log)
    a = _sigmoid(a0 + za)
    kk = k * k_k
    kk = kk * lax.rsqrt(jnp.sum(kk * kk, axis=-1, keepdims=True) + L2_EPS)
    k2 = k * (1.0 + (a - 1.0) * k_a)
    an = -kk
    b = kk * a
    causal = ri >= ci
    strict = ri > ci
    eye = (ri == ci).astype(F32)
    cl = _dhi(causal.astype(F32), lw)
    ecl = jnp.exp(-cl)
    at = an * jnp.exp(cl - lw)
    bt = b * ecl
    kt = k2 * ecl
    rt = r * jnp.exp(cl)
    a_ab = jnp.where(strict, _dnt(at, bt), 0.0)
    a_ak = jnp.where(strict, _dnt(at, kt), 0.0)
    tinv = _unit_lower_inverse(a_ab, eye)
    u = _dhi(tinv, _dnt(at, s0) + _dhi(a_ak, v))
    y = _dnt(rt, s0) + _dhi(jnp.where(causal, _dnt(rt, bt), 0.0), u) + _dhi(jnp.where(causal, _dnt(rt, kt), 0.0), v)
    cl_end = jnp.sum(lw, axis=0, keepdims=True)
    dec_end = jnp.exp(cl_end - cl)
    s1 = s0 * jnp.exp(cl_end) + _dtn(u, b * dec_end) + _dtn(v, k2 * dec_end)
    m = (lax.broadcasted_iota(jnp.int32, (1, LANES), 1) < RWKV_HEAD_DIM).astype(F32)
    mean = jnp.sum(y, axis=-1, keepdims=True) * (1.0 / RWKV_HEAD_DIM)
    yc = (y - mean) * m
    var = jnp.sum(yc * yc, axis=-1, keepdims=True) * (1.0 / RWKV_HEAD_DIM)
    yn = yc * lax.rsqrt(var + RWKV_GN_EPS) * ln_w + ln_b
    y2 = yn + jnp.sum(r * k2 * r_k, axis=-1, keepdims=True) * v
    return y2 * g, s1


def _gdn_head(xq, xk, xv, hq, hk, hv, z, ab, s0, cw, gp, oha, ohb):
    c = xq.shape[0]
    ri, ci = _iotas(c)
    re = lax.broadcasted_iota(jnp.int32, (c, c + SUBLANES), 0)
    ce = lax.broadcasted_iota(jnp.int32, (c, c + SUBLANES), 1)

    def conv(x, h, w):
        xe = jnp.concatenate([h, x], axis=0)
        out = x * w[GDN_CONV - 1:GDN_CONV]
        for s in range(1, GDN_CONV):
            sel = (ce == re + SUBLANES - s).astype(F32)
            out = out + _dhi(sel, xe) * w[GDN_CONV - 1 - s:GDN_CONV - s]
        return out

    q = _silu(conv(xq, hq, cw[0]))
    k = _silu(conv(xk, hk, cw[1]))
    v = _silu(conv(xv, hv, cw[2]))
    q = q * lax.rsqrt(jnp.sum(q * q, axis=-1, keepdims=True) + L2_EPS) * (GDN_HEAD_DIM ** -0.5)
    k = k * lax.rsqrt(jnp.sum(k * k, axis=-1, keepdims=True) + L2_EPS)
    gg = -jnp.exp(gp[0:1]) * _softplus(ab + gp[1:2])
    beta = jnp.sum(_sigmoid(ab) * ohb, axis=1, keepdims=True)
    causal = ri >= ci
    strict = ri > ci
    eye = (ri == ci).astype(F32)
    gcm = _dhi(causal.astype(F32), gg * oha)
    gc = jnp.sum(gcm, axis=1, keepdims=True)
    gc_row = _dnt(jnp.ones((c, LANES), F32), gcm)
    dec = jnp.where(causal, jnp.exp(jnp.where(causal, gc - gc_row, 0.0)), 0.0)
    kb = k * beta
    vb = v * beta
    lm = jnp.where(strict, _dnt(kb, k) * dec, 0.0)
    tinv = _unit_lower_inverse(-lm, eye)
    egc = jnp.exp(gc)
    u = _dhi(tinv, vb)
    wk = _dhi(tinv, kb * egc)
    attn = jnp.where(causal, _dnt(q, k) * dec, 0.0)
    g_last = gc[c - 1:c, :]
    v_new = u - _dhi(wk, s0)
    o = _dhi(q * egc, s0) + _dhi(attn, v_new)
    s1 = s0 * jnp.exp(g_last) + _dtn(k * jnp.exp(g_last - gc), v_new)
    return _rms(o, gp[2:3]) * _silu(z), s1


def _prev_rows_spec(width, col):
    per = CHUNK // SUBLANES
    return pl.BlockSpec((SUBLANES, width), lambda n, h: (jnp.maximum(n * per - 1, 0), col(h)))


def _rwkv_specs(nmap):
    cb = OFF_RKV // LANES
    specs = []
    for j in range(3):
        specs.append(pl.BlockSpec((CHUNK, LANES), lambda n, h, j=j: (nmap(n), cb + j * RWKV_HEADS + h)))
    specs.append(pl.BlockSpec((CHUNK, 2 * LANES), lambda n, h: (nmap(n), OFF_LO // (2 * LANES))))
    per = CHUNK // SUBLANES
    for j in range(3):
        specs.append(pl.BlockSpec((SUBLANES, LANES),
                                  lambda n, h, j=j: (jnp.maximum(nmap(n) * per - 1, 0), cb + j * RWKV_HEADS + h)))
    specs.append(pl.BlockSpec((SUBLANES, 2 * LANES), lambda n, h: (jnp.maximum(nmap(n) * per - 1, 0), OFF_LO // (2 * LANES))))
    specs.append(pl.BlockSpec((None, 16, LANES), lambda n, h: (h, 0, 0)))
    specs.append(pl.BlockSpec((1, 2 * LANES), lambda n, h: (0, 0)))
    specs.append(pl.BlockSpec((3, 2 * LANES, LANES), lambda n, h: (0, 0, h)))
    return specs


def _rwkv_forward(p_cat, ppack, mulo, wl):
    t = p_cat.shape[0]
    n_chunks = t // CHUNK

    def body(pr, pk, pv, plo, hr, hk, hv, hlo, pp, ml, w, out, st_out, s_scr):
        n, h = pl.program_id(0), pl.program_id(1)

        @pl.when(n == 0)
        def _():
            s_scr[h] = jnp.zeros((LANES, LANES), F32)

        live = (n > 0).astype(F32)
        s0 = s_scr[h]
        st_out[...] = s0
        o, s1 = _rwkv_head(pr[...], pk[...], pv[...], plo[...], hr[7:8, :] * live, hk[7:8, :] * live, hv[7:8, :] * live,
                           hlo[7:8, :] * live, s0, pp[...], ml[...], w[...])
        out[...] = o.astype(out.dtype)
        s_scr[h] = s1

    return pl.pallas_call(
        body, name="rwkv_fwd", grid=(n_chunks, RWKV_HEADS),
        in_specs=_rwkv_specs(lambda n: n),
        out_specs=(pl.BlockSpec((CHUNK, LANES), lambda n, h: (n, h)),
                   pl.BlockSpec((None, None, LANES, LANES), lambda n, h: (n, h, 0, 0))),
        out_shape=(jax.ShapeDtypeStruct((t, RWKV_HEADS * LANES), BF16),
                   jax.ShapeDtypeStruct((n_chunks, RWKV_HEADS, LANES, LANES), F32)),
        scratch_shapes=[pltpu.VMEM((RWKV_HEADS, LANES, LANES), F32)],
        compiler_params=_cparams(("arbitrary", "arbitrary")),
    )(p_cat, p_cat, p_cat, p_cat, p_cat, p_cat, p_cat, p_cat, ppack, mulo, wl)


def _rwkv_backward(p_cat, ppack, mulo, wl, states, d_out):
    t = p_cat.shape[0]
    n_chunks = t // CHUNK
    last = n_chunks - 1

    def body(pr, pk, pv, plo, hr, hk, hv, hlo, pp, ml, w, st, dy, dpr, dpk, dpv, dplo, dpp, dml, dw, ds_scr, car_scr, carlo_scr):
        n, h = pl.program_id(0), pl.program_id(1)

        @pl.when(n == 0)
        def _():
            ds_scr[h] = jnp.zeros((LANES, LANES), F32)
            car_scr[h] = jnp.zeros((3 * SUBLANES, LANES), F32)
            carlo_scr[h] = jnp.zeros((SUBLANES, 2 * LANES), F32)

        @pl.when((n == 0) & (h == 0))
        def _():
            dpp[...] = jnp.zeros(dpp.shape, F32)
            dml[...] = jnp.zeros(dml.shape, F32)
            dw[...] = jnp.zeros(dw.shape, F32)

        live = (n < last).astype(F32)
        args = (pr[...], pk[...], pv[...], plo[...], hr[7:8, :] * live, hk[7:8, :] * live, hv[7:8, :] * live,
                hlo[7:8, :] * live, st[...], pp[...], ml[...], w[...])
        _, vjp = jax.vjp(_rwkv_head, *args)
        g = vjp((dy[...], ds_scr[h]))
        car = car_scr[h]
        outs = (dpr, dpk, dpv)
        for j in range(3):
            outs[j][0:CHUNK - 1, :] = g[j][0:CHUNK - 1, :].astype(outs[j].dtype)
            outs[j][CHUNK - 1:CHUNK, :] = (g[j][CHUNK - 1:CHUNK, :] + car[SUBLANES * j:SUBLANES * j + 1, :]).astype(outs[j].dtype)
            car_scr[h, SUBLANES * j:SUBLANES * j + 1, :] = g[4 + j]
        e_last = (lax.broadcasted_iota(jnp.int32, (CHUNK, 1), 0) == CHUNK - 1).astype(F32)
        dlo = g[3] + e_last * carlo_scr[h, 0:1, :]

        @pl.when(h == 0)
        def _():
            dplo[...] = dlo

        @pl.when(h > 0)
        def _():
            dplo[...] += dlo

        carlo_scr[h, 0:1, :] = g[7]
        ds_scr[h] = g[8]
        dpp[h] += g[9]
        dml[h, 0:1, :] += g[10]
        dw[h] += g[11]

    rev = lambda n: last - n
    in_specs = _rwkv_specs(rev) + [
        pl.BlockSpec((None, None, LANES, LANES), lambda n, h: (rev(n), h, 0, 0)),
        pl.BlockSpec((CHUNK, LANES), lambda n, h: (rev(n), h)),
    ]
    hw = RWKV_HEADS * LANES
    return pl.pallas_call(
        body, name="rwkv_bwd", grid=(n_chunks, RWKV_HEADS),
        in_specs=in_specs,
        out_specs=(pl.BlockSpec((CHUNK, LANES), lambda n, h: (rev(n), h)),
                   pl.BlockSpec((CHUNK, LANES), lambda n, h: (rev(n), h)),
                   pl.BlockSpec((CHUNK, LANES), lambda n, h: (rev(n), h)),
                   pl.BlockSpec((CHUNK, 2 * LANES), lambda n, h: (rev(n), 0)),
                   pl.BlockSpec((RWKV_HEADS, 16, LANES), lambda n, h: (0, 0, 0)),
                   pl.BlockSpec((RWKV_HEADS, SUBLANES, 2 * LANES), lambda n, h: (0, 0, 0)),
                   pl.BlockSpec((RWKV_HEADS, 3, 2 * LANES, LANES), lambda n, h: (0, 0, 0, 0))),
        out_shape=(jax.ShapeDtypeStruct((t, hw), BF16), jax.ShapeDtypeStruct((t, hw), BF16), jax.ShapeDtypeStruct((t, hw), BF16),
                   jax.ShapeDtypeStruct((t, 2 * LANES), F32),
                   jax.ShapeDtypeStruct((RWKV_HEADS, 16, LANES), F32),
                   jax.ShapeDtypeStruct((RWKV_HEADS, SUBLANES, 2 * LANES), F32),
                   jax.ShapeDtypeStruct((RWKV_HEADS, 3, 2 * LANES, LANES), F32)),
        scratch_shapes=[pltpu.VMEM((RWKV_HEADS, LANES, LANES), F32),
                        pltpu.VMEM((RWKV_HEADS, 3 * SUBLANES, LANES), F32),
                        pltpu.VMEM((RWKV_HEADS, SUBLANES, 2 * LANES), F32)],
        compiler_params=_cparams(("arbitrary", "arbitrary")),
    )(p_cat, p_cat, p_cat, p_cat, p_cat, p_cat, p_cat, p_cat, ppack, mulo, wl, states, d_out)


def _gdn_specs(nmap):
    cb = OFF_QKV // LANES
    per = CHUNK // SUBLANES
    specs = []
    for j in range(3):
        specs.append(pl.BlockSpec((CHUNK, LANES), lambda n, h, j=j: (nmap(n), cb + j * GDN_HEADS + h)))
    for j in range(3):
        specs.append(pl.BlockSpec((SUBLANES, LANES),
                                  lambda n, h, j=j: (jnp.maximum(nmap(n) * per - 1, 0), cb + j * GDN_HEADS + h)))
    specs.append(pl.BlockSpec((CHUNK, LANES), lambda n, h: (nmap(n), OFF_Z // LANES + h)))
    specs.append(pl.BlockSpec((CHUNK, LANES), lambda n, h: (nmap(n), OFF_AB // LANES)))
    specs.append(pl.BlockSpec((None, 3, SUBLANES, LANES), lambda n, h: (h, 0, 0, 0)))
    specs.append(pl.BlockSpec((SUBLANES, LANES), lambda n, h: (0, 0)))
    return specs


def _onehots(h):
    lane = lax.broadcasted_iota(jnp.int32, (1, LANES), 1)
    return (lane == h).astype(F32), (lane == GDN_HEADS + h).astype(F32)


def _gdn_forward(p_cat, cwpack, gpar):
    t = p_cat.shape[0]
    n_chunks = t // CHUNK

    def body(xq, xk, xv, hq, hk, hv, z, ab, cw, gp, out, st_out, s_scr):
        n, h = pl.program_id(0), pl.program_id(1)

        @pl.when(n == 0)
        def _():
            s_scr[h] = jnp.zeros((LANES, LANES), F32)

        live = (n > 0).astype(F32)
        oha, ohb = _onehots(h)
        s0 = s_scr[h]
        st_out[...] = s0
        o, s1 = _gdn_head(xq[...], xk[...], xv[...], hq[...] * live, hk[...] * live, hv[...] * live, z[...], ab[...], s0,
                          cw[...], gp[...], oha, ohb)
        out[...] = o.astype(out.dtype)
        s_scr[h] = s1

    return pl.pallas_call(
        body, name="gdn_fwd", grid=(n_chunks, GDN_HEADS),
        in_specs=_gdn_specs(lambda n: n),
        out_specs=(pl.BlockSpec((CHUNK, LANES), lambda n, h: (n, h)),
                   pl.BlockSpec((None, None, LANES, LANES), lambda n, h: (n, h, 0, 0))),
        out_shape=(jax.ShapeDtypeStruct((t, GDN_WIDTH), BF16),
                   jax.ShapeDtypeStruct((n_chunks, GDN_HEADS, LANES, LANES), F32)),
        scratch_shapes=[pltpu.VMEM((GDN_HEADS, LANES, LANES), F32)],
        compiler_params=_cparams(("arbitrary", "arbitrary")),
    )(p_cat, p_cat, p_cat, p_cat, p_cat, p_cat, p_cat, p_cat, cwpack, gpar)


def _gdn_backward(p_cat, cwpack, gpar, states, d_out):
    t = p_cat.shape[0]
    n_chunks = t // CHUNK
    last = n_chunks - 1

    def body(xq, xk, xv, hq, hk, hv, z, ab, cw, gp, st, dy, dq, dk, dv, dz, dab, dcw, dgp, ds_scr, car_scr):
        n, h = pl.program_id(0), pl.program_id(1)

        @pl.when(n == 0)
        def _():
            ds_scr[h] = jnp.zeros((LANES, LANES), F32)
            car_scr[h] = jnp.zeros((3 * SUBLANES, LANES), F32)

        @pl.when((n == 0) & (h == 0))
        def _():
            dcw[...] = jnp.zeros(dcw.shape, F32)
            dgp[...] = jnp.zeros(dgp.shape, F32)

        live = (n < last).astype(F32)
        oha, ohb = _onehots(h)
        fn = functools.partial(_gdn_head, oha=oha, ohb=ohb)
        args = (xq[...], xk[...], xv[...], hq[...] * live, hk[...] * live, hv[...] * live, z[...], ab[...], st[...],
                cw[...], gp[...])
        _, vjp = jax.vjp(fn, *args)
        g = vjp((dy[...], ds_scr[h]))
        car = car_scr[h]
        outs = (dq, dk, dv)
        body_rows = CHUNK - SUBLANES
        for j in range(3):
            outs[j][0:body_rows, :] = g[j][0:body_rows, :].astype(outs[j].dtype)
            outs[j][body_rows:CHUNK, :] = (g[j][body_rows:CHUNK, :] + car[SUBLANES * j:SUBLANES * (j + 1), :]).astype(outs[j].dtype)
            car_scr[h, SUBLANES * j:SUBLANES * (j + 1), :] = g[3 + j]
        dz[...] = g[6].astype(dz.dtype)

        @pl.when(h == 0)
        def _():
            dab[...] = g[7]

        @pl.when(h > 0)
        def _():
            dab[...] += g[7]

        ds_scr[h] = g[8]
        dcw[h] += g[9]
        dgp[h] += g[10]

    rev = lambda n: last - n
    in_specs = _gdn_specs(rev) + [
        pl.BlockSpec((None, None, LANES, LANES), lambda n, h: (rev(n), h, 0, 0)),
        pl.BlockSpec((CHUNK, LANES), lambda n, h: (rev(n), h)),
    ]
    blk = pl.BlockSpec((CHUNK, LANES), lambda n, h: (rev(n), h))
    return pl.pallas_call(
        body, name="gdn_bwd", grid=(n_chunks, GDN_HEADS),
        in_specs=in_specs,
        out_specs=(blk, blk, blk, blk,
                   pl.BlockSpec((CHUNK, LANES), lambda n, h: (rev(n), 0)),
                   pl.BlockSpec((GDN_HEADS, 3, SUBLANES, LANES), lambda n, h: (0, 0, 0, 0)),
                   pl.BlockSpec((GDN_HEADS, SUBLANES, LANES), lambda n, h: (0, 0, 0))),
        out_shape=(jax.ShapeDtypeStruct((t, GDN_WIDTH), BF16), jax.ShapeDtypeStruct((t, GDN_WIDTH), BF16),
                   jax.ShapeDtypeStruct((t, GDN_WIDTH), BF16), jax.ShapeDtypeStruct((t, GDN_WIDTH), BF16),
                   jax.ShapeDtypeStruct((t, LANES), F32),
                   jax.ShapeDtypeStruct((GDN_HEADS, 3, SUBLANES, LANES), F32),
                   jax.ShapeDtypeStruct((GDN_HEADS, SUBLANES, LANES), F32)),
        scratch_shapes=[pltpu.VMEM((GDN_HEADS, LANES, LANES), F32),
                        pltpu.VMEM((GDN_HEADS, 3 * SUBLANES, LANES), F32)],
        compiler_params=_cparams(("arbitrary", "arbitrary")),
    )(p_cat, p_cat, p_cat, p_cat, p_cat, p_cat, p_cat, p_cat, cwpack, gpar, states, d_out)


def _pick(n, options):
    for o in options:
        if n % o == 0:
            return o
    raise ValueError(f"no tile for {n}")


_MM_DIMS = {'nn': (((1,), (0,)), ((), ())), 'nt': (((1,), (1,)), ((), ())), 'tn': (((0,), (0,)), ((), ()))}


def _matmul(a, b, mode, out_dtype, name):
    if mode == 'nn':
        (m, k), (k2, n) = a.shape, b.shape
    elif mode == 'nt':
        (m, k), (n, k2) = a.shape, b.shape
    else:
        (k, m), (k2, n) = a.shape, b.shape
    assert k == k2, (a.shape, b.shape, mode)
    tm = _pick(m, (1024, 512, 256, 128, 64))
    tn = _pick(n, (512, 256, 128))
    tk = _pick(k, (1024, 512, 256, 128, 64))
    nk = k // tk
    dims = _MM_DIMS[mode]

    def body(a_ref, b_ref, o_ref, acc_ref):
        kk = pl.program_id(2)
        part = lax.dot_general(a_ref[...], b_ref[...], dims, preferred_element_type=F32)

        @pl.when(kk == 0)
        def _():
            acc_ref[...] = part

        @pl.when(kk > 0)
        def _():
            acc_ref[...] += part

        @pl.when(kk == nk - 1)
        def _():
            o_ref[...] = acc_ref[...].astype(o_ref.dtype)

    a_spec = pl.BlockSpec((tk, tm), lambda i, j, kk: (kk, i)) if mode == 'tn' else pl.BlockSpec((tm, tk), lambda i, j, kk: (i, kk))
    b_spec = pl.BlockSpec((tn, tk), lambda i, j, kk: (j, kk)) if mode == 'nt' else pl.BlockSpec((tk, tn), lambda i, j, kk: (kk, j))
    return pl.pallas_call(
        body, name=name, grid=(m // tm, n // tn, nk),
        in_specs=[a_spec, b_spec],
        out_specs=pl.BlockSpec((tm, tn), lambda i, j, kk: (i, j)),
        out_shape=jax.ShapeDtypeStruct((m, n), out_dtype),
        scratch_shapes=[pltpu.VMEM((tm, tn), F32)],
        compiler_params=_cparams(("parallel", "parallel", "arbitrary")),
    )(a, b)


ROW_TILE = 256


def _row_specs(rows, tm):
    return [pl.BlockSpec((tm, w), lambda i, ci=ci: (i, ci)) for (_, w, ci) in rows]


def _rw_forward(fn, rows, pars, outs, name):
    t = rows[0][0].shape[0]
    tm = min(ROW_TILE, t)
    nr, npar = len(rows), len(pars)

    def body(*refs):
        vals = [r[...].astype(F32) for r in refs[:nr]] + [p[...] for p in refs[nr:nr + npar]]
        res = fn(*vals)
        for o, v in zip(refs[nr + npar:], res):
            o[...] = v.astype(o.dtype)

    return pl.pallas_call(
        body, name=name, grid=(t // tm,),
        in_specs=_row_specs(rows, tm) + [pl.BlockSpec(p.shape, lambda i: (0, 0)) for p in pars],
        out_specs=tuple(pl.BlockSpec((tm, w), lambda i: (i, 0)) for (w, _) in outs),
        out_shape=tuple(jax.ShapeDtypeStruct((t, w), dt) for (w, dt) in outs),
        compiler_params=_cparams(("parallel",)),
    )(*[r[0] for r in rows], *pars)


def _rw_backward(fn, rows, pars, cots, drow_dtypes, name):
    t = rows[0][0].shape[0]
    tm = min(ROW_TILE, t)
    nr, npar, nc = len(rows), len(pars), len(cots)
    keep = [i for i, dt in enumerate(drow_dtypes) if dt is not None]

    def body(*refs):
        vals = [r[...].astype(F32) for r in refs[:nr]] + [p[...] for p in refs[nr:nr + npar]]
        cvals = tuple(c[...].astype(F32) for c in refs[nr + npar:nr + npar + nc])
        orefs = refs[nr + npar + nc:]
        _, vjp = jax.vjp(fn, *vals)
        g = vjp(cvals)
        for o, i in zip(orefs[:len(keep)], keep):
            o[...] = g[i].astype(o.dtype)
        first = pl.program_id(0) == 0
        for o, gi in zip(orefs[len(keep):], g[nr:]):
            @pl.when(first)
            def _(o=o, gi=gi):
                o[...] = gi

            @pl.when(jnp.logical_not(first))
            def _(o=o, gi=gi):
                o[...] += gi

    out_specs = [pl.BlockSpec((tm, rows[i][1]), lambda i_: (i_, 0)) for i in keep] + \
                [pl.BlockSpec(p.shape, lambda i_: (0, 0)) for p in pars]
    out_shape = [jax.ShapeDtypeStruct((t, rows[i][1]), drow_dtypes[i]) for i in keep] + \
                [jax.ShapeDtypeStruct(p.shape, F32) for p in pars]
    return pl.pallas_call(
        body, name=name, grid=(t // tm,),
        in_specs=_row_specs(rows, tm) + [pl.BlockSpec(p.shape, lambda i: (0, 0)) for p in pars] + _row_specs(cots, tm),
        out_specs=tuple(out_specs), out_shape=tuple(out_shape),
        compiler_params=_cparams(("arbitrary",)),
    )(*[r[0] for r in rows], *pars, *[c[0] for c in cots])


def _norm_fn(x, g):
    return (_rms(x, g),)


def _norm_skip_fn(x, g):
    return _rms(x, g), x


def _merge_fn(ga, gb, ya, yb):
    return (_sigmoid(ga) * ya + _sigmoid(gb) * yb,)


def _res_norm_fn(x, mo, g):
    x1 = x + mo
    return x1, _rms(x1, g)


def _loss_head(x1, fo, gf, target, name):
    t, d = x1.shape
    tm = min(ROW_TILE, t)

    def tile_loss(x2, g, tgt):
        err = _rms(x2, g) - tgt
        per_row = jnp.sum(err * err, axis=-1, keepdims=True) * (0.5 / d)
        return jnp.sum(per_row, axis=0, keepdims=True)

    def body(x1_ref, fo_ref, g_ref, t_ref, loss_ref, dx_ref, dxb_ref, dg_ref):
        x2 = x1_ref[...] + fo_ref[...]
        val, vjp = jax.vjp(functools.partial(tile_loss, tgt=t_ref[...]), x2, g_ref[...])
        dx2, dg = vjp(jnp.ones((1, 1), F32))
        dx_ref[...] = dx2
        dxb_ref[...] = dx2.astype(BF16)
        first = pl.program_id(0) == 0

        @pl.when(first)
        def _():
            loss_ref[...] = jnp.broadcast_to(val, loss_ref.shape)
            dg_ref[...] = dg

        @pl.when(jnp.logical_not(first))
        def _():
            loss_ref[...] += jnp.broadcast_to(val, loss_ref.shape)
            dg_ref[...] += dg

    row = pl.BlockSpec((tm, d), lambda i: (i, 0))
    vec = pl.BlockSpec((1, d), lambda i: (0, 0))
    return pl.pallas_call(
        body, name=name, grid=(t // tm,),
        in_specs=[row, row, vec, row],
        out_specs=(pl.BlockSpec((1, LANES), lambda i: (0, 0)), row, row, vec),
        out_shape=(jax.ShapeDtypeStruct((1, LANES), F32), jax.ShapeDtypeStruct((t, d), F32),
                   jax.ShapeDtypeStruct((t, d), BF16), jax.ShapeDtypeStruct((1, d), F32)),
        compiler_params=_cparams(("arbitrary",)),
    )(x1, fo, gf, target)


FFN_TILE_ROWS = 512
FFN_TILE_COLS = 256
FFN_COL_BLOCKS = FFN_HIDDEN // FFN_TILE_COLS


def _conv3_past(x, halo, w):
    rows = lax.broadcasted_iota(jnp.int32, x.shape, 0)
    x1 = jnp.where(rows == 0, halo[7:8, :], pltpu.roll(x, 1, 0))
    x2 = jnp.where(rows == 0, halo[6:7, :], jnp.where(rows == 1, halo[7:8, :], pltpu.roll(x, 2, 0)))
    return x * w[2:3] + x1 * w[1:2] + x2 * w[0:1], x1, x2


def _ffn_in_specs(tm, imap, jmap):
    per = tm // SUBLANES
    tile = lambda off: pl.BlockSpec((tm, FFN_TILE_COLS), lambda *g: (imap(*g), off + jmap(*g) % FFN_COL_BLOCKS))
    halo = lambda off: pl.BlockSpec((SUBLANES, FFN_TILE_COLS),
                                    lambda *g: (jnp.maximum(imap(*g) * per - 1, 0), off + jmap(*g) % FFN_COL_BLOCKS))
    wsp = lambda off: pl.BlockSpec((FFN_CONV, FFN_TILE_COLS), lambda *g: (0, off + jmap(*g) % FFN_COL_BLOCKS))
    return [tile(0), halo(0), wsp(0), tile(FFN_COL_BLOCKS), halo(FFN_COL_BLOCKS), wsp(FFN_COL_BLOCKS)]


def _ffn_act_forward(hpre, cw):
    t = hpre.shape[0]
    tm = min(FFN_TILE_ROWS, t)

    def body(hg, pg, wg, hu, pu, wu, out):
        live = (pl.program_id(0) > 0).astype(F32)
        cg, _, _ = _conv3_past(hg[...], pg[...] * live, wg[...])
        cu, _, _ = _conv3_past(hu[...], pu[...] * live, wu[...])
        out[...] = (_silu(cg) * cu).astype(out.dtype)

    return pl.pallas_call(
        body, name="ffn_act_fwd", grid=(t // tm, FFN_COL_BLOCKS),
        in_specs=_ffn_in_specs(tm, lambda i, j: i, lambda i, j: j),
        out_specs=pl.BlockSpec((tm, FFN_TILE_COLS), lambda i, j: (i, j)),
        out_shape=jax.ShapeDtypeStruct((t, FFN_HIDDEN), BF16),
        compiler_params=_cparams(("parallel", "parallel")),
    )(hpre, hpre, cw, hpre, hpre, cw)


def _ffn_act_backward(hpre, cw, dact):
    t = hpre.shape[0]
    tm = min(FFN_TILE_ROWS, t)

    def body(hg, pg, wg, hu, pu, wu, da, dconv, dw):
        j, i = pl.program_id(0), pl.program_id(1)
        live = (i > 0).astype(F32)
        cg, g1, g2 = _conv3_past(hg[...], pg[...] * live, wg[...])
        cu, u1, u2 = _conv3_past(hu[...], pu[...] * live, wu[...])
        s = _sigmoid(cg)
        d = da[...]
        d_gate = d * cu * s * (1.0 + cg * (1.0 - s))
        d_up = d * cg * s
        is_gate = j < FFN_COL_BLOCKS
        dc = jnp.where(is_gate, d_gate, d_up)
        dconv[...] = dc
        taps = (jnp.where(is_gate, g2, u2), jnp.where(is_gate, g1, u1), jnp.where(is_gate, hg[...], hu[...]))
        sums = [jnp.sum(xs * dc, axis=0, keepdims=True) for xs in taps]

        @pl.when(i == 0)
        def _():
            for r_ in range(FFN_CONV):
                dw[r_:r_ + 1, :] = sums[r_]

        @pl.when(i > 0)
        def _():
            for r_ in range(FFN_CONV):
                dw[r_:r_ + 1, :] += sums[r_]

    return pl.pallas_call(
        body, name="ffn_act_bwd", grid=(2 * FFN_COL_BLOCKS, t // tm),
        in_specs=_ffn_in_specs(tm, lambda j, i: i, lambda j, i: j) +
        [pl.BlockSpec((tm, FFN_TILE_COLS), lambda j, i: (i, j % FFN_COL_BLOCKS))],
        out_specs=(pl.BlockSpec((tm, FFN_TILE_COLS), lambda j, i: (i, j)),
                   pl.BlockSpec((FFN_CONV, FFN_TILE_COLS), lambda j, i: (0, j))),
        out_shape=(jax.ShapeDtypeStruct((t, 2 * FFN_HIDDEN), F32), jax.ShapeDtypeStruct((FFN_CONV, 2 * FFN_HIDDEN), F32)),
        compiler_params=_cparams(("parallel", "arbitrary")),
    )(hpre, hpre, cw, hpre, hpre, cw, dact)


def _ffn_conv_backward(dconv, cw):
    t = dconv.shape[0]
    tm = min(FFN_TILE_ROWS, t)
    n_tiles = t // tm
    per = tm // SUBLANES

    def body(d_ref, nx_ref, w_ref, out):
        live = (pl.program_id(0) < n_tiles - 1).astype(F32)
        d = d_ref[...]
        nx = nx_ref[...] * live
        w = w_ref[...]
        rows = lax.broadcasted_iota(jnp.int32, d.shape, 0)
        d1 = jnp.where(rows == tm - 1, nx[0:1, :], pltpu.roll(d, tm - 1, 0))
        d2 = jnp.where(rows == tm - 1, nx[1:2, :], jnp.where(rows == tm - 2, nx[0:1, :], pltpu.roll(d, tm - 2, 0)))
        out[...] = (d * w[2:3] + d1 * w[1:2] + d2 * w[0:1]).astype(out.dtype)

    return pl.pallas_call(
        body, name="ffn_conv_bwd", grid=(n_tiles, 2 * FFN_COL_BLOCKS),
        in_specs=[pl.BlockSpec((tm, FFN_TILE_COLS), lambda i, j: (i, j)),
                  pl.BlockSpec((SUBLANES, FFN_TILE_COLS), lambda i, j: (jnp.minimum((i + 1) * per, t // SUBLANES - 1), j)),
                  pl.BlockSpec((FFN_CONV, FFN_TILE_COLS), lambda i, j: (0, j))],
        out_specs=pl.BlockSpec((tm, FFN_TILE_COLS), lambda i, j: (i, j)),
        out_shape=jax.ShapeDtypeStruct((t, 2 * FFN_HIDDEN), BF16),
        compiler_params=_cparams(("parallel", "parallel")),
    )(dconv, dconv, cw)


def _my_place():
    x, y, c = lax.axis_index("x"), lax.axis_index("y"), lax.axis_index("c")
    return x, y, c, 4 * x + 2 * y + c


def _peer(x, y, c, k):
    px, py, pc = x ^ ((k >> 2) & 1), y ^ ((k >> 1) & 1), c ^ (k & 1)
    return (px, py, pc), 4 * px + 2 * py + pc


def _all_gather(shard, name):
    r, w = shard.shape

    def body(src, out, send_sems, recv_sems, local_sem):
        x, y, c, me = _my_place()
        mine = pltpu.make_async_copy(src, out.at[me], local_sem)
        mine.start()
        sends, recvs = [], []
        for k in range(1, N_DEV):
            dev, idx = _peer(x, y, c, k)
            sends.append(pltpu.make_async_remote_copy(src_ref=src, dst_ref=out.at[me], send_sem=send_sems.at[k],
                                                      recv_sem=recv_sems.at[k], device_id=dev,
                                                      device_id_type=pl.DeviceIdType.MESH))
            recvs.append(pltpu.make_async_remote_copy(src_ref=src, dst_ref=out.at[idx], send_sem=send_sems.at[k],
                                                      recv_sem=recv_sems.at[k], device_id=dev,
                                                      device_id_type=pl.DeviceIdType.MESH))
        for cp in sends:
            cp.start()
        for cp in recvs:
            cp.wait_recv()
        for cp in sends:
            cp.wait_send()
        mine.wait()

    return pl.pallas_call(
        body, name=name,
        in_specs=[pl.BlockSpec(memory_space=pl.ANY)],
        out_specs=pl.BlockSpec(memory_space=pl.ANY),
        out_shape=jax.ShapeDtypeStruct((N_DEV, r, w), shard.dtype),
        scratch_shapes=[pltpu.SemaphoreType.DMA((N_DEV,)), pltpu.SemaphoreType.DMA((N_DEV,)), pltpu.SemaphoreType.DMA],
    )(shard)


def _exchange_slabs(slabs, name):
    def body(src, out, send_sems, recv_sems, local_sem):
        x, y, c, me = _my_place()
        mine = pltpu.make_async_copy(src.at[me], out.at[me], local_sem)
        mine.start()
        sends, recvs = [], []
        for k in range(1, N_DEV):
            dev, idx = _peer(x, y, c, k)
            sends.append(pltpu.make_async_remote_copy(src_ref=src.at[idx], dst_ref=out.at[me], send_sem=send_sems.at[k],
                                                      recv_sem=recv_sems.at[k], device_id=dev,
                                                      device_id_type=pl.DeviceIdType.MESH))
            recvs.append(pltpu.make_async_remote_copy(src_ref=src.at[idx], dst_ref=out.at[idx], send_sem=send_sems.at[k],
                                                      recv_sem=recv_sems.at[k], device_id=dev,
                                                      device_id_type=pl.DeviceIdType.MESH))
        for cp in sends:
            cp.start()
        for cp in recvs:
            cp.wait_recv()
        for cp in sends:
            cp.wait_send()
        mine.wait()

    return pl.pallas_call(
        body, name=name,
        in_specs=[pl.BlockSpec(memory_space=pl.ANY)],
        out_specs=pl.BlockSpec(memory_space=pl.ANY),
        out_shape=jax.ShapeDtypeStruct(slabs.shape, slabs.dtype),
        scratch_shapes=[pltpu.SemaphoreType.DMA((N_DEV,)), pltpu.SemaphoreType.DMA((N_DEV,)), pltpu.SemaphoreType.DMA],
    )(slabs)


def _reduce_adamw(parts, w, m, v, rows_per_step):
    _, r, wd = parts.shape
    c1 = 1.0 / (1.0 - ADAM_B1 ** ADAM_STEP)
    c2 = 1.0 / (1.0 - ADAM_B2 ** ADAM_STEP)

    def body(p_ref, w_ref, m_ref, v_ref, g_out, d_out, m_out, v_out):
        g = p_ref[0]
        for s in range(1, N_DEV):
            g = g + p_ref[s]
        mn = ADAM_B1 * m_ref[...] + (1.0 - ADAM_B1) * g
        vn = ADAM_B2 * v_ref[...] + (1.0 - ADAM_B2) * (g * g)
        g_out[...] = g
        m_out[...] = mn
        v_out[...] = vn
        d_out[...] = -ADAM_LR * ((mn * c1) / (jnp.sqrt(vn * c2) + ADAM_EPS) + ADAM_WD * w_ref[...])

    blk = pl.BlockSpec((rows_per_step, wd), lambda i: (i, 0))
    shp = jax.ShapeDtypeStruct((r, wd), F32)
    return pl.pallas_call(
        body, name="reduce_adamw", grid=(r // rows_per_step,),
        in_specs=[pl.BlockSpec((N_DEV, rows_per_step, wd), lambda i: (0, i, 0)), blk, blk, blk],
        out_specs=(blk, blk, blk, blk), out_shape=(shp, shp, shp, shp),
        compiler_params=_cparams(("parallel",)),
    )(parts, w, m, v)


PACK_W = 1024


def _pad_heads(a, slots):
    lead = a.shape[:-1]
    a = a.reshape(lead + (slots, RWKV_HEAD_DIM))
    a = jnp.pad(a, [(0, 0)] * (len(lead) + 1) + [(0, LANES - RWKV_HEAD_DIM)])
    return a.reshape(lead + (slots * LANES,))


def _unpad_heads(a, slots):
    lead = a.shape[:-1]
    return a.reshape(lead + (slots, LANES))[..., :RWKV_HEAD_DIM].reshape(lead + (slots * RWKV_HEAD_DIM,))


def _flat_pack(arrs, dtype, row_mult):
    flat = jnp.concatenate([a.reshape(-1).astype(dtype) for a in arrs])
    n = flat.shape[0]
    rows = -(-n // PACK_W)
    rows = -(-rows // row_mult) * row_mult
    return jnp.pad(flat, (0, rows * PACK_W - n)).reshape(rows, PACK_W)


def _unpack_gathered(g, names, shard_shapes):
    flat = g.reshape(N_DEV, -1)
    out, off = {}, 0
    for n in names:
        s = shard_shapes[n]
        size = s[0] * s[1]
        seg = flat[:, off:off + size].reshape((N_DEV,) + tuple(s))
        off += size
        if SHARD_AXIS[n] == 1:
            out[n] = jnp.transpose(seg, (1, 0, 2)).reshape(s[0], N_DEV * s[1])
        else:
            out[n] = seg.reshape(N_DEV * s[0], s[1])
    return out


def _shard_major(full, axis):
    a, b = full.shape
    if axis == 1:
        return jnp.transpose(full.reshape(a, N_DEV, b // N_DEV), (1, 0, 2)).reshape(N_DEV, -1)
    return full.reshape(N_DEV, -1)


def _prepare_weights(full, rep):
    w = full['w_in']
    d = w.shape[0]
    zeros = jnp.zeros((d, LANES), w.dtype)
    w_cat = jnp.concatenate([
        w[:, 3848:4872], w[:, 4872:5896], _pad_heads(w[:, 0:1536], 3 * RWKV_HEADS), w[:, 1536:1792],
        w[:, 1792:3328], w[:, 3328:3840], jnp.pad(w[:, 3840:3848], ((0, 0), (0, LANES - 8))), zeros], axis=1)
    assert w_cat.shape[1] == CAT_W
    rp = full['rwkv_proj']
    rproj = jnp.pad(rp.reshape(RWKV_HEADS, RWKV_HEAD_DIM, -1), ((0, 0), (0, LANES - RWKV_HEAD_DIM), (0, 0))).reshape(RWKV_HEADS * LANES, -1)
    mu = rep['rwkv_mu']
    vecs = [mu[0:512], mu[512:1024], mu[1024:1536], rep['rwkv_w0'], rep['rwkv_a0'], rep['rwkv_k_k'], rep['rwkv_k_a'],
            rep['rwkv_ln_w'], rep['rwkv_ln_b'], rep['rwkv_r_k'].reshape(-1)]
    ppack = jnp.stack([jnp.pad(v.reshape(RWKV_HEADS, RWKV_HEAD_DIM), ((0, 0), (0, LANES - RWKV_HEAD_DIM))) for v in vecs], axis=1)
    ppack = jnp.pad(ppack, ((0, 0), (0, 16 - len(vecs)), (0, 0)))
    mulo = mu[1536:1792].reshape(1, 2 * LANES)
    wl = jnp.zeros((3, 2 * LANES, RWKV_HEADS * LANES), F32)
    wl = wl.at[0, 0:64].set(_pad_heads(full['rwkv_w2'], RWKV_HEADS))
    wl = wl.at[1, 64:128].set(_pad_heads(full['rwkv_a2'], RWKV_HEADS))
    wl = wl.at[2, 128:256].set(_pad_heads(full['rwkv_g2'], RWKV_HEADS))
    cw = full['gdn_conv_w'].reshape(GDN_CONV, 3, GDN_HEADS, LANES)
    cwpack = jnp.pad(jnp.transpose(cw, (2, 1, 0, 3)), ((0, 0), (0, 0), (0, SUBLANES - GDN_CONV), (0, 0)))
    gpar = jnp.zeros((SUBLANES, LANES), F32)
    gpar = gpar.at[0, 0:GDN_HEADS].set(rep['gdn_a_log']).at[1, 0:GDN_HEADS].set(rep['gdn_dt_bias']).at[2].set(rep['gdn_norm_w'])
    return dict(w_cat=w_cat, rproj=rproj, gproj=full['gdn_proj'], w_out=full['w_out'], ffn_up=full['ffn_up'],
                ffn_down=full['ffn_down'], ffn_cw=full['ffn_conv_w'], ppack=ppack, mulo=mulo, wl=wl, cwpack=cwpack, gpar=gpar,
                g1=rep['norm1_g'].reshape(1, -1), g2=rep['norm2_g'].reshape(1, -1), gf=rep['final_g'].reshape(1, -1))


def _local_step(x, target, p):
    d = x.shape[1]
    full_w = lambda a: (a, a.shape[1], 0)
    (u,) = _rw_forward(_norm_fn, [full_w(x)], [p['g1']], [(d, BF16)], "norm1")
    p_cat = _matmul(u, p['w_cat'], 'nn', F32, "proj_in")
    ya_pre, st_r = _rwkv_forward(p_cat, p['ppack'], p['mulo'], p['wl'])
    yb_pre, st_g = _gdn_forward(p_cat, p['cwpack'], p['gpar'])
    ya = _matmul(ya_pre, p['rproj'], 'nn', F32, "rwkv_proj")
    yb = _matmul(yb_pre, p['gproj'], 'nn', F32, "gdn_proj")
    gates = [(p_cat, d, OFF_GA // d), (p_cat, d, OFF_GB // d)]
    (mixed,) = _rw_forward(_merge_fn, gates + [full_w(ya), full_w(yb)], [], [(d, BF16)], "merge")
    mo = _matmul(mixed, p['w_out'], 'nn', F32, "out_proj")
    x1, n2 = _rw_forward(_res_norm_fn, [full_w(x), full_w(mo)], [p['g2']], [(d, F32), (d, BF16)], "res_norm2")
    hpre = _matmul(n2, p['ffn_up'], 'nn', F32, "ffn_up")
    act = _ffn_act_forward(hpre, p['ffn_cw'])
    fo = _matmul(act, p['ffn_down'], 'nn', F32, "ffn_down")
    loss_vec, dx2, dx2b, dgf = _loss_head(x1, fo, p['gf'], target, "loss_head")

    dact = _matmul(dx2b, p['ffn_down'], 'nt', F32, "d_act")
    dw_down = _matmul(act, dx2b, 'tn', F32, "dw_ffn_down")
    dconv, dcw_f = _ffn_act_backward(hpre, p['ffn_cw'], dact)
    dh = _ffn_conv_backward(dconv, p['ffn_cw'])
    dn2 = _matmul(dh, p['ffn_up'], 'nt', F32, "d_norm2")
    dw_up = _matmul(n2, dh, 'tn', F32, "dw_ffn_up")
    dx1, dx1b, dg2 = _rw_backward(_res_norm_fn, [full_w(x), full_w(mo)], [p['g2']], [full_w(dx2), full_w(dn2)],
                                  [F32, BF16], "res_norm2_bwd")
    dmixed = _matmul(dx1b, p['w_out'], 'nt', F32, "d_mixed")
    dw_out = _matmul(mixed, dx1b, 'tn', F32, "dw_out")
    dga, dgb, dya, dyb = _rw_backward(_merge_fn, gates + [full_w(ya), full_w(yb)], [], [full_w(dmixed)],
                                      [BF16, BF16, BF16, BF16], "merge_bwd")
    d_ya_pre = _matmul(dya, p['rproj'], 'nt', F32, "d_rwkv_out")
    dw_rproj = _matmul(ya_pre, dya, 'tn', F32, "dw_rwkv_proj")
    d_yb_pre = _matmul(dyb, p['gproj'], 'nt', F32, "d_gdn_out")
    dw_gproj = _matmul(yb_pre, dyb, 'tn', F32, "dw_gdn_proj")
    dpr, dpk, dpv, dplo, dpp, dml, dwl = _rwkv_backward(p_cat, p['ppack'], p['mulo'], p['wl'], st_r, d_ya_pre)
    dq, dk, dv, dz, dab, dcw_g, dgp = _gdn_backward(p_cat, p['cwpack'], p['gpar'], st_g, d_yb_pre)
    t = x.shape[0]
    dp_cat = jnp.concatenate([dga, dgb, dpr, dpk, dpv, dplo.astype(BF16), dq, dk, dv, dz, dab.astype(BF16),
                              jnp.zeros((t, LANES), BF16)], axis=1)
    du = _matmul(dp_cat, p['w_cat'], 'nt', F32, "d_norm1")
    dw_cat = _matmul(u, dp_cat, 'tn', F32, "dw_in")
    grad_x, dg1 = _rw_backward(_norm_skip_fn, [full_w(x)], [p['g1']], [full_w(du), full_w(dx1)], [F32], "norm1_bwd")

    heads = lambda row: dpp[:, row, :RWKV_HEAD_DIM].reshape(-1)
    lora = lambda j, lo_, hi_: jnp.transpose(dwl[:, j, lo_:hi_, :RWKV_HEAD_DIM], (1, 0, 2)).reshape(hi_ - lo_, RWKV_WIDTH)
    grads = {
        'norm1_g': dg1[0],
        'w_in': jnp.concatenate([_unpad_heads(dw_cat[:, OFF_RKV:OFF_LO], 3 * RWKV_HEADS), dw_cat[:, OFF_LO:OFF_QKV],
                                 dw_cat[:, OFF_QKV:OFF_Z], dw_cat[:, OFF_Z:OFF_AB], dw_cat[:, OFF_AB:OFF_AB + 8],
                                 dw_cat[:, OFF_GA:OFF_GB], dw_cat[:, OFF_GB:OFF_RKV]], axis=1),
        'rwkv_mu': jnp.concatenate([heads(0), heads(1), heads(2), jnp.sum(dml[:, 0, :], axis=0)]),
        'rwkv_w0': heads(3), 'rwkv_a0': heads(4), 'rwkv_k_k': heads(5), 'rwkv_k_a': heads(6),
        'rwkv_ln_w': heads(7), 'rwkv_ln_b': heads(8), 'rwkv_r_k': heads(9).reshape(RWKV_HEADS, RWKV_HEAD_DIM),
        'rwkv_w2': lora(0, 0, 64), 'rwkv_a2': lora(1, 64, 128), 'rwkv_g2': lora(2, 128, 256),
        'rwkv_proj': dw_rproj.reshape(RWKV_HEADS, LANES, -1)[:, :RWKV_HEAD_DIM].reshape(RWKV_WIDTH, -1),
        'gdn_conv_w': jnp.transpose(dcw_g[:, :, :GDN_CONV, :], (2, 1, 0, 3)).reshape(GDN_CONV, 3 * GDN_WIDTH),
        'gdn_a_log': jnp.sum(dgp[:, 0, :GDN_HEADS], axis=0), 'gdn_dt_bias': jnp.sum(dgp[:, 1, :GDN_HEADS], axis=0),
        'gdn_norm_w': jnp.sum(dgp[:, 2, :], axis=0),
        'gdn_proj': dw_gproj, 'w_out': dw_out, 'norm2_g': dg2[0], 'ffn_up': dw_up, 'ffn_conv_w': dcw_f,
        'ffn_down': dw_down, 'final_g': dgf[0],
    }
    return loss_vec, grad_x, grads


def kernel(x, norm1_g, w_in, rwkv_mu, rwkv_w0, rwkv_w2, rwkv_a0, rwkv_a2, rwkv_g2, rwkv_k_k, rwkv_k_a, rwkv_r_k, rwkv_ln_w, rwkv_ln_b, rwkv_proj, gdn_conv_w, gdn_a_log, gdn_dt_bias, gdn_norm_w, gdn_proj, w_out, norm2_g, ffn_up, ffn_conv_w, ffn_down, final_g, loss_target, m_norm1_g, m_w_in, m_rwkv_mu, m_rwkv_w0, m_rwkv_w2, m_rwkv_a0, m_rwkv_a2, m_rwkv_g2, m_rwkv_k_k, m_rwkv_k_a, m_rwkv_r_k, m_rwkv_ln_w, m_rwkv_ln_b, m_rwkv_proj, m_gdn_conv_w, m_gdn_a_log, m_gdn_dt_bias, m_gdn_norm_w, m_gdn_proj, m_w_out, m_norm2_g, m_ffn_up, m_ffn_conv_w, m_ffn_down, m_final_g, v_norm1_g, v_w_in, v_rwkv_mu, v_rwkv_w0, v_rwkv_w2, v_rwkv_a0, v_rwkv_a2, v_rwkv_g2, v_rwkv_k_k, v_rwkv_k_a, v_rwkv_r_k, v_rwkv_ln_w, v_rwkv_ln_b, v_rwkv_proj, v_gdn_conv_w, v_gdn_a_log, v_gdn_dt_bias, v_gdn_norm_w, v_gdn_proj, v_w_out, v_norm2_g, v_ffn_up, v_ffn_conv_w, v_ffn_down, v_final_g):
    given = dict(zip(WEIGHT_NAMES, (norm1_g, w_in, rwkv_mu, rwkv_w0, rwkv_w2, rwkv_a0, rwkv_a2, rwkv_g2, rwkv_k_k, rwkv_k_a, rwkv_r_k,
                                    rwkv_ln_w, rwkv_ln_b, rwkv_proj, gdn_conv_w, gdn_a_log, gdn_dt_bias, gdn_norm_w, gdn_proj, w_out,
                                    norm2_g, ffn_up, ffn_conv_w, ffn_down, final_g)))
    mom1 = dict(zip(WEIGHT_NAMES, (m_norm1_g, m_w_in, m_rwkv_mu, m_rwkv_w0, m_rwkv_w2, m_rwkv_a0, m_rwkv_a2, m_rwkv_g2, m_rwkv_k_k,
                                   m_rwkv_k_a, m_rwkv_r_k, m_rwkv_ln_w, m_rwkv_ln_b, m_rwkv_proj, m_gdn_conv_w, m_gdn_a_log,
                                   m_gdn_dt_bias, m_gdn_norm_w, m_gdn_proj, m_w_out, m_norm2_g, m_ffn_up, m_ffn_conv_w, m_ffn_down,
                                   m_final_g)))
    mom2 = dict(zip(WEIGHT_NAMES, (v_norm1_g, v_w_in, v_rwkv_mu, v_rwkv_w0, v_rwkv_w2, v_rwkv_a0, v_rwkv_a2, v_rwkv_g2, v_rwkv_k_k,
                                   v_rwkv_k_a, v_rwkv_r_k, v_rwkv_ln_w, v_rwkv_ln_b, v_rwkv_proj, v_gdn_conv_w, v_gdn_a_log,
                                   v_gdn_dt_bias, v_gdn_norm_w, v_gdn_proj, v_w_out, v_norm2_g, v_ffn_up, v_ffn_conv_w, v_ffn_down,
                                   v_final_g)))
    strip = lambda n, a: a if n == 'final_g' else a.reshape(a.shape[1:])
    local = {n: strip(n, a) for n, a in given.items()}
    shard_shapes = {n: local[n].shape for n in SHARD_AXIS}
    sharded = BIG_SHARDED + SMALL_SHARDED

    g_big = _all_gather(_flat_pack([local[n] for n in BIG_SHARDED], BF16, 16), "gather_big")
    g_small = _all_gather(_flat_pack([local[n] for n in SMALL_SHARDED], F32, SUBLANES), "gather_small")
    full = _unpack_gathered(g_big, BIG_SHARDED, shard_shapes)
    full.update(_unpack_gathered(g_small, SMALL_SHARDED, shard_shapes))
    rep = {n: local[n] for n in REPLICATED}

    loss_vec, grad_x, grads = _local_step(x[0], loss_target[0], _prepare_weights(full, rep))

    rep_vec = jnp.concatenate([grads[n].reshape(-1) for n in REPLICATED] + [loss_vec[0, 0:1]])
    slab = jnp.concatenate([_shard_major(grads[n], SHARD_AXIS[n]) for n in sharded] +
                           [jnp.broadcast_to(rep_vec[None], (N_DEV, rep_vec.shape[0]))], axis=1)
    n_elem = slab.shape[1]
    rows_per_step = 208
    rows = -(-n_elem // (PACK_W * rows_per_step)) * rows_per_step
    slab = jnp.pad(slab, ((0, 0), (0, rows * PACK_W - n_elem))).reshape(N_DEV, rows, PACK_W)
    parts = _exchange_slabs(slab, "grad_exchange")

    def pack_local(src):
        flat = jnp.concatenate([strip(n, src[n]).reshape(-1) for n in sharded + REPLICATED])
        return jnp.pad(flat, (0, rows * PACK_W - flat.shape[0])).reshape(rows, PACK_W)

    packs = _reduce_adamw(parts, pack_local(given), pack_local(mom1), pack_local(mom2), rows_per_step)

    def unpack(pk):
        flat = pk.reshape(-1)
        out, off = {}, 0
        for n in sharded + REPLICATED:
            size = int(np.prod(given[n].shape))
            out[n] = flat[off:off + size].reshape(given[n].shape)
            off += size
        return out, flat[off]

    (g_out, loss), (d_out, _), (m_out, _), (v_out, _) = (unpack(pk) for pk in packs)
    return (loss, grad_x[None], *[g_out[n] for n in WEIGHT_NAMES], *[d_out[n] for n in WEIGHT_NAMES],
            *[m_out[n] for n in WEIGHT_NAMES], *[v_out[n] for n in WEIGHT_NAMES])
```

```python
import functools

import jax
import jax.numpy as jnp
import numpy as np
from jax import lax
from jax.experimental import pallas as pl
from jax.experimental.pallas import tpu as pltpu

F32 = jnp.float32
BF16 = jnp.bfloat16
HI = lax.Precision.HIGHEST

N_DEV = 8
D_MODEL = 1024
CHUNK = 64
RWKV_HEADS = 8
RWKV_HEAD_DIM = 64
RWKV_WIDTH = 512
GDN_HEADS = 4
GDN_HEAD_DIM = 128
GDN_WIDTH = 512
GDN_CONV = 4
FFN_HIDDEN = 2816
FFN_CONV = 3
NORM_EPS = 1e-6
L2_EPS = 1e-6
RWKV_GN_EPS = 64e-5
LANES = 128
SUBLANES = 8
VMEM_LIMIT = 56 * 1024 * 1024

ADAM_LR = 0.001
ADAM_B1 = 0.9
ADAM_B2 = 0.999
ADAM_EPS = 1e-08
ADAM_WD = 0.01
ADAM_STEP = 10

OFF_GA, OFF_GB, OFF_RKV, OFF_LO, OFF_QKV, OFF_Z, OFF_AB, CAT_W = 0, 1024, 2048, 5120, 5376, 6912, 7424, 7680

WEIGHT_NAMES = ['norm1_g', 'w_in', 'rwkv_mu', 'rwkv_w0', 'rwkv_w2', 'rwkv_a0', 'rwkv_a2', 'rwkv_g2', 'rwkv_k_k', 'rwkv_k_a',
                'rwkv_r_k', 'rwkv_ln_w', 'rwkv_ln_b', 'rwkv_proj', 'gdn_conv_w', 'gdn_a_log', 'gdn_dt_bias', 'gdn_norm_w',
                'gdn_proj', 'w_out', 'norm2_g', 'ffn_up', 'ffn_conv_w', 'ffn_down', 'final_g']
BIG_SHARDED = ['w_in', 'ffn_up', 'ffn_down', 'w_out', 'rwkv_proj', 'gdn_proj']
SMALL_SHARDED = ['rwkv_w2', 'rwkv_a2', 'rwkv_g2', 'gdn_conv_w', 'ffn_conv_w']
TRANSPOSED = ('w_in', 'ffn_up')
SHARD_AXIS = {'w_in': 0, 'ffn_up': 0, 'ffn_down': 0, 'w_out': 0, 'rwkv_proj': 1, 'gdn_proj': 1,
              'rwkv_w2': 1, 'rwkv_a2': 1, 'rwkv_g2': 1, 'gdn_conv_w': 1, 'ffn_conv_w': 1}
REPLICATED = [n for n in WEIGHT_NAMES if n not in SHARD_AXIS]


def _cparams(sem=None):
    kw = dict(vmem_limit_bytes=VMEM_LIMIT)
    if sem is not None:
        kw['dimension_semantics'] = sem
    return pltpu.CompilerParams(**kw)


_NN = (((1,), (0,)), ((), ()))
_NT = (((1,), (1,)), ((), ()))
_TN = (((0,), (0,)), ((), ()))


def _dot1(a, b, dims):
    return lax.dot_general(a.astype(BF16), b.astype(BF16), dims, preferred_element_type=F32)


@jax.custom_vjp
def _dhi(a, b):
    return _dot1(a, b, _NN)


_dhi.defvjp(lambda a, b: (_dot1(a, b, _NN), (a, b)),
            lambda res, ct: (_dot1(ct, res[1], _NT), _dot1(res[0], ct, _TN)))


@jax.custom_vjp
def _dnt(a, b):
    return _dot1(a, b, _NT)


_dnt.defvjp(lambda a, b: (_dot1(a, b, _NT), (a, b)),
            lambda res, ct: (_dot1(ct, res[1], _NN), _dot1(ct, res[0], _TN)))


@jax.custom_vjp
def _dtn(a, b):
    return _dot1(a, b, _TN)


_dtn.defvjp(lambda a, b: (_dot1(a, b, _TN), (a, b)),
            lambda res, ct: (_dot1(res[1], ct, _NT), _dot1(res[0], ct, _NN)))


def _split3(x):
    x1 = x.astype(BF16)
    r1 = x - x1.astype(F32)
    x2 = r1.astype(BF16)
    return x1, x2, (r1 - x2.astype(F32)).astype(BF16)


def _dot_exact_lhs(sel, x, dims):
    parts = [lax.dot_general(sel, xi, dims, preferred_element_type=F32) for xi in _split3(x)]
    return parts[0] + parts[1] + parts[2]


def _tril_ones(c):
    ri, ci = _iotas(c)
    return (ri >= ci).astype(BF16)


@jax.custom_vjp
def _cumsum_rows(x):
    return _dot_exact_lhs(_tril_ones(x.shape[0]), x, _NN)


_cumsum_rows.defvjp(lambda x: (_dot_exact_lhs(_tril_ones(x.shape[0]), x, _NN), None),
                    lambda _, ct: (_dot_exact_lhs(_tril_ones(ct.shape[0]), ct, _TN),))


@jax.custom_vjp
def _lane_sum_as_row(x):
    return _dot_exact_lhs(jnp.ones(x.shape, BF16), x, _NT)


def _lane_sum_as_row_bwd(_, ct):
    ones = jnp.ones((ct.shape[0], LANES), BF16)
    parts = [lax.dot_general(ci, ones, _TN, preferred_element_type=F32) for ci in _split3(ct)]
    return (parts[0] + parts[1] + parts[2],)


_lane_sum_as_row.defvjp(lambda x: (_dot_exact_lhs(jnp.ones(x.shape, BF16), x, _NT), None), _lane_sum_as_row_bwd)


def _shift_rows(x, halo, s):
    rows = lax.broadcasted_iota(jnp.int32, x.shape, 0)
    out = pltpu.roll(x, s, 0)
    for i in range(s):
        out = jnp.where(rows == i, halo[SUBLANES - s + i:SUBLANES - s + i + 1, :], out)
    return out


def _unshift_rows(g, carry, s):
    c = g.shape[0]
    rows = lax.broadcasted_iota(jnp.int32, g.shape, 0)
    out = pltpu.roll(g, c - s, 0)
    for i in range(s):
        out = jnp.where(rows == c - s + i, carry[i:i + 1, :], out)
    return out


def _sigmoid(z):
    return 1.0 / (1.0 + jnp.exp(-z))


def _silu(z):
    return z * _sigmoid(z)


def _softplus(z):
    return jnp.maximum(z, 0.0) + jnp.log(1.0 + jnp.exp(-jnp.abs(z)))


def _rms(t, gain):
    return t * lax.rsqrt(jnp.mean(t * t, axis=-1, keepdims=True) + NORM_EPS) * gain


def _iotas(c):
    return lax.broadcasted_iota(jnp.int32, (c, c), 0), lax.broadcasted_iota(jnp.int32, (c, c), 1)


def _unit_lower_inverse(xm, eye):
    t = eye + xm
    p = xm
    for _ in range(5):
        p = _dhi(p, p)
        t = t + _dhi(t, p)
    return t


def _rwkv_head(pr, pk, pv, plo, qr, qk, qv, qlo, s0, pp, mulo, wl):
    c = pr.shape[0]
    ri, ci = _iotas(c)

    def mix(p, q, mu):
        return p + (q - p) * mu

    r = mix(pr, qr, pp[0:1])
    k = mix(pk, qk, pp[1:2])
    v = mix(pv, qv, pp[2:3])
    lo = mix(plo, qlo, mulo)
    w0, a0, k_k, k_a, ln_w, ln_b, r_k = (pp[i:i + 1] for i in range(3, 10))
    zw = _dhi(jnp.tanh(lo), wl[0])
    za = _dhi(lo, wl[1])
    g = _dhi(_sigmoid(lo), wl[2])
    w_log = -_softplus(-(w0 + zw)) - 0.5
    lw = -jnp.exp(w_log)
    a = _sigmoid(a0 + za)
    kk = k * k_k
    kk = kk * lax.rsqrt(jnp.sum(kk * kk, axis=-1, keepdims=True) + L2_EPS)
    k2 = k * (1.0 + (a - 1.0) * k_a)
    an = -kk
    b = kk * a
    causal = ri >= ci
    strict = ri > ci
    eye = (ri == ci).astype(F32)
    cl = _cumsum_rows(lw)
    ecl = jnp.exp(-cl)
    at = an * jnp.exp(cl - lw)
    bt = b * ecl
    kt = k2 * ecl
    rt = r * jnp.exp(cl)
    a_ab = jnp.where(strict, _dnt(at, bt), 0.0)
    a_ak = jnp.where(strict, _dnt(at, kt), 0.0)
    tinv = _unit_lower_inverse(a_ab, eye)
    u = _dhi(tinv, _dnt(at, s0) + _dhi(a_ak, v))
    y = _dnt(rt, s0) + _dhi(jnp.where(causal, _dnt(rt, bt), 0.0), u) + _dhi(jnp.where(causal, _dnt(rt, kt), 0.0), v)
    cl_end = jnp.sum(lw, axis=0, keepdims=True)
    dec_end = jnp.exp(cl_end - cl)
    s1 = s0 * jnp.exp(cl_end) + _dtn(u, b * dec_end) + _dtn(v, k2 * dec_end)
    m = (lax.broadcasted_iota(jnp.int32, (1, LANES), 1) < RWKV_HEAD_DIM).astype(F32)
    mean = jnp.sum(y, axis=-1, keepdims=True) * (1.0 / RWKV_HEAD_DIM)
    yc = (y - mean) * m
    var = jnp.sum(yc * yc, axis=-1, keepdims=True) * (1.0 / RWKV_HEAD_DIM)
    yn = yc * lax.rsqrt(var + RWKV_GN_EPS) * ln_w + ln_b
    y2 = yn + jnp.sum(r * k2 * r_k, axis=-1, keepdims=True) * v
    return y2 * g, s1


def _gdn_head(xq, xk, xv, z, ab, s0, cw, gp, oha, ohb):
    c = z.shape[0]
    ri, ci = _iotas(c)

    def conv(xs, w):
        out = xs[0] * w[GDN_CONV - 1:GDN_CONV]
        for s in range(1, GDN_CONV):
            out = out + xs[s] * w[GDN_CONV - 1 - s:GDN_CONV - s]
        return out

    q = _silu(conv(xq, cw[0]))
    k = _silu(conv(xk, cw[1]))
    v = _silu(conv(xv, cw[2]))
    q = q * lax.rsqrt(jnp.sum(q * q, axis=-1, keepdims=True) + L2_EPS) * (GDN_HEAD_DIM ** -0.5)
    k = k * lax.rsqrt(jnp.sum(k * k, axis=-1, keepdims=True) + L2_EPS)
    gg = -jnp.exp(gp[0:1]) * _softplus(ab + gp[1:2])
    beta = jnp.sum(_sigmoid(ab) * ohb, axis=1, keepdims=True)
    causal = ri >= ci
    strict = ri > ci
    eye = (ri == ci).astype(F32)
    gcm = _cumsum_rows(gg * oha)
    gc = jnp.sum(gcm, axis=1, keepdims=True)
    gc_row = _lane_sum_as_row(gcm)
    dec = jnp.where(causal, jnp.exp(jnp.where(causal, gc - gc_row, 0.0)), 0.0)
    kb = k * beta
    vb = v * beta
    lm = jnp.where(strict, _dnt(kb, k) * dec, 0.0)
    tinv = _unit_lower_inverse(-lm, eye)
    egc = jnp.exp(gc)
    u = _dhi(tinv, vb)
    wk = _dhi(tinv, kb * egc)
    attn = jnp.where(causal, _dnt(q, k) * dec, 0.0)
    g_last = gc[c - 1:c, :]
    v_new = u - _dhi(wk, s0)
    o = _dhi(q * egc, s0) + _dhi(attn, v_new)
    s1 = s0 * jnp.exp(g_last) + _dtn(k * jnp.exp(g_last - gc), v_new)
    return _rms(o, gp[2:3]) * _silu(z), s1


def _prev_rows_spec(width, col):
    per = CHUNK // SUBLANES
    return pl.BlockSpec((SUBLANES, width), lambda n, h: (jnp.maximum(n * per - 1, 0), col(h)))


def _rwkv_specs(nmap):
    cb = OFF_RKV // LANES
    specs = []
    for j in range(3):
        specs.append(pl.BlockSpec((CHUNK, LANES), lambda n, h, j=j: (nmap(n), cb + j * RWKV_HEADS + h)))
    specs.append(pl.BlockSpec((CHUNK, 2 * LANES), lambda n, h: (nmap(n), OFF_LO // (2 * LANES))))
    per = CHUNK // SUBLANES
    for j in range(3):
        specs.append(pl.BlockSpec((SUBLANES, LANES),
                                  lambda n, h, j=j: (jnp.maximum(nmap(n) * per - 1, 0), cb + j * RWKV_HEADS + h)))
    specs.append(pl.BlockSpec((SUBLANES, 2 * LANES), lambda n, h: (jnp.maximum(nmap(n) * per - 1, 0), OFF_LO // (2 * LANES))))
    specs.append(pl.BlockSpec((None, 16, LANES), lambda n, h: (h, 0, 0)))
    specs.append(pl.BlockSpec((1, 2 * LANES), lambda n, h: (0, 0)))
    specs.append(pl.BlockSpec((3, 2 * LANES, LANES), lambda n, h: (0, 0, h)))
    return specs


def _rwkv_forward(p_cat, ppack, mulo, wl):
    t = p_cat.shape[0]
    n_chunks = t // CHUNK

    def body(pr, pk, pv, plo, hr, hk, hv, hlo, pp, ml, w, out, st_out, s_scr):
        n, h = pl.program_id(0), pl.program_id(1)

        @pl.when(n == 0)
        def _():
            s_scr[h] = jnp.zeros((LANES, LANES), F32)

        live = (n > 0).astype(F32)
        s0 = s_scr[h]
        st_out[...] = s0
        cur = (pr[...], pk[...], pv[...], plo[...])
        prev = tuple(_shift_rows(x, hx[...] * live, 1) for x, hx in zip(cur, (hr, hk, hv, hlo)))
        o, s1 = _rwkv_head(*cur, *prev, s0, pp[...], ml[...], w[...])
        out[...] = o.astype(out.dtype)
        s_scr[h] = s1

    return pl.pallas_call(
        body, name="rwkv_fwd", grid=(n_chunks, RWKV_HEADS),
        in_specs=_rwkv_specs(lambda n: n),
        out_specs=(pl.BlockSpec((CHUNK, LANES), lambda n, h: (n, h)),
                   pl.BlockSpec((None, None, LANES, LANES), lambda n, h: (n, h, 0, 0))),
        out_shape=(jax.ShapeDtypeStruct((t, RWKV_HEADS * LANES), BF16),
                   jax.ShapeDtypeStruct((n_chunks, RWKV_HEADS, LANES, LANES), F32)),
        scratch_shapes=[pltpu.VMEM((RWKV_HEADS, LANES, LANES), F32)],
        compiler_params=_cparams(("arbitrary", "arbitrary")),
    )(p_cat, p_cat, p_cat, p_cat, p_cat, p_cat, p_cat, p_cat, ppack, mulo, wl)


def _rwkv_backward(p_cat, ppack, mulo, wl, states, d_out):
    t = p_cat.shape[0]
    n_chunks = t // CHUNK
    last = n_chunks - 1

    def body(pr, pk, pv, plo, hr, hk, hv, hlo, pp, ml, w, st, dy, dpr, dpk, dpv, dplo, dpp, dml, dw, ds_scr, car_scr, carlo_scr):
        n, h = pl.program_id(0), pl.program_id(1)

        @pl.when(n == 0)
        def _():
            ds_scr[h] = jnp.zeros((LANES, LANES), F32)
            car_scr[h] = jnp.zeros((3 * SUBLANES, LANES), F32)
            carlo_scr[h] = jnp.zeros((SUBLANES, 2 * LANES), F32)

        @pl.when((n == 0) & (h == 0))
        def _():
            dpp[...] = jnp.zeros(dpp.shape, F32)
            dml[...] = jnp.zeros(dml.shape, F32)
            dw[...] = jnp.zeros(dw.shape, F32)

        live = (n < last).astype(F32)
        cur = (pr[...], pk[...], pv[...], plo[...])
        prev = tuple(_shift_rows(x, hx[...] * live, 1) for x, hx in zip(cur, (hr, hk, hv, hlo)))
        _, vjp = jax.vjp(_rwkv_head, *cur, *prev, st[...], pp[...], ml[...], w[...])
        g = vjp((dy[...], ds_scr[h]))
        car = car_scr[h]
        outs = (dpr, dpk, dpv)
        for j in range(3):
            tot = g[j] + _unshift_rows(g[4 + j], car[SUBLANES * j:SUBLANES * (j + 1), :], 1)
            outs[j][...] = tot.astype(outs[j].dtype)
            car_scr[h, SUBLANES * j:SUBLANES * (j + 1), :] = g[4 + j][0:SUBLANES, :]
        dlo = g[3] + _unshift_rows(g[7], carlo_scr[h], 1)

        @pl.when(h == 0)
        def _():
            dplo[...] = dlo

        @pl.when(h > 0)
        def _():
            dplo[...] += dlo

        carlo_scr[h] = g[7][0:SUBLANES, :]
        ds_scr[h] = g[8]
        dpp[h] += g[9]
        dml[h, 0:1, :] += g[10]
        dw[h] += g[11]

    rev = lambda n: last - n
    in_specs = _rwkv_specs(rev) + [
        pl.BlockSpec((None, None, LANES, LANES), lambda n, h: (rev(n), h, 0, 0)),
        pl.BlockSpec((CHUNK, LANES), lambda n, h: (rev(n), h)),
    ]
    hw = RWKV_HEADS * LANES
    return pl.pallas_call(
        body, name="rwkv_bwd", grid=(n_chunks, RWKV_HEADS),
        in_specs=in_specs,
        out_specs=(pl.BlockSpec((CHUNK, LANES), lambda n, h: (rev(n), h)),
                   pl.BlockSpec((CHUNK, LANES), lambda n, h: (rev(n), h)),
                   pl.BlockSpec((CHUNK, LANES), lambda n, h: (rev(n), h)),
                   pl.BlockSpec((CHUNK, 2 * LANES), lambda n, h: (rev(n), 0)),
                   pl.BlockSpec((RWKV_HEADS, 16, LANES), lambda n, h: (0, 0, 0)),
                   pl.BlockSpec((RWKV_HEADS, SUBLANES, 2 * LANES), lambda n, h: (0, 0, 0)),
                   pl.BlockSpec((RWKV_HEADS, 3, 2 * LANES, LANES), lambda n, h: (0, 0, 0, 0))),
        out_shape=(jax.ShapeDtypeStruct((t, hw), BF16), jax.ShapeDtypeStruct((t, hw), BF16), jax.ShapeDtypeStruct((t, hw), BF16),
                   jax.ShapeDtypeStruct((t, 2 * LANES), F32),
                   jax.ShapeDtypeStruct((RWKV_HEADS, 16, LANES), F32),
                   jax.ShapeDtypeStruct((RWKV_HEADS, SUBLANES, 2 * LANES), F32),
                   jax.ShapeDtypeStruct((RWKV_HEADS, 3, 2 * LANES, LANES), F32)),
        scratch_shapes=[pltpu.VMEM((RWKV_HEADS, LANES, LANES), F32),
                        pltpu.VMEM((RWKV_HEADS, 3 * SUBLANES, LANES), F32),
                        pltpu.VMEM((RWKV_HEADS, SUBLANES, 2 * LANES), F32)],
        compiler_params=_cparams(("arbitrary", "arbitrary")),
    )(p_cat, p_cat, p_cat, p_cat, p_cat, p_cat, p_cat, p_cat, ppack, mulo, wl, states, d_out)


def _gdn_specs(nmap):
    cb = OFF_QKV // LANES
    per = CHUNK // SUBLANES
    specs = []
    for j in range(3):
        specs.append(pl.BlockSpec((CHUNK, LANES), lambda n, h, j=j: (nmap(n), cb + j * GDN_HEADS + h)))
    for j in range(3):
        specs.append(pl.BlockSpec((SUBLANES, LANES),
                                  lambda n, h, j=j: (jnp.maximum(nmap(n) * per - 1, 0), cb + j * GDN_HEADS + h)))
    specs.append(pl.BlockSpec((CHUNK, LANES), lambda n, h: (nmap(n), OFF_Z // LANES + h)))
    specs.append(pl.BlockSpec((CHUNK, LANES), lambda n, h: (nmap(n), OFF_AB // LANES)))
    specs.append(pl.BlockSpec((None, 3, SUBLANES, LANES), lambda n, h: (h, 0, 0, 0)))
    specs.append(pl.BlockSpec((SUBLANES, LANES), lambda n, h: (0, 0)))
    return specs


def _conv_taps(x, halo):
    return (x,) + tuple(_shift_rows(x, halo, s) for s in range(1, GDN_CONV))


def _onehots(h):
    lane = lax.broadcasted_iota(jnp.int32, (1, LANES), 1)
    return (lane == h).astype(F32), (lane == GDN_HEADS + h).astype(F32)


def _gdn_forward(p_cat, cwpack, gpar):
    t = p_cat.shape[0]
    n_chunks = t // CHUNK

    def body(xq, xk, xv, hq, hk, hv, z, ab, cw, gp, out, st_out, s_scr):
        n, h = pl.program_id(0), pl.program_id(1)

        @pl.when(n == 0)
        def _():
            s_scr[h] = jnp.zeros((LANES, LANES), F32)

        live = (n > 0).astype(F32)
        oha, ohb = _onehots(h)
        s0 = s_scr[h]
        st_out[...] = s0
        taps = [_conv_taps(x[...], hx[...] * live) for x, hx in ((xq, hq), (xk, hk), (xv, hv))]
        o, s1 = _gdn_head(*taps, z[...], ab[...], s0, cw[...], gp[...], oha, ohb)
        out[...] = o.astype(out.dtype)
        s_scr[h] = s1

    return pl.pallas_call(
        body, name="gdn_fwd", grid=(n_chunks, GDN_HEADS),
        in_specs=_gdn_specs(lambda n: n),
        out_specs=(pl.BlockSpec((CHUNK, LANES), lambda n, h: (n, h)),
                   pl.BlockSpec((None, None, LANES, LANES), lambda n, h: (n, h, 0, 0))),
        out_shape=(jax.ShapeDtypeStruct((t, GDN_WIDTH), BF16),
                   jax.ShapeDtypeStruct((n_chunks, GDN_HEADS, LANES, LANES), F32)),
        scratch_shapes=[pltpu.VMEM((GDN_HEADS, LANES, LANES), F32)],
        compiler_params=_cparams(("arbitrary", "arbitrary")),
    )(p_cat, p_cat, p_cat, p_cat, p_cat, p_cat, p_cat, p_cat, cwpack, gpar)


def _gdn_backward(p_cat, cwpack, gpar, states, d_out):
    t = p_cat.shape[0]
    n_chunks = t // CHUNK
    last = n_chunks - 1

    def body(xq, xk, xv, hq, hk, hv, z, ab, cw, gp, st, dy, dq, dk, dv, dz, dab, dcw, dgp, ds_scr, car_scr):
        n, h = pl.program_id(0), pl.program_id(1)

        @pl.when(n == 0)
        def _():
            ds_scr[h] = jnp.zeros((LANES, LANES), F32)
            car_scr[h] = jnp.zeros((3 * GDN_CONV, SUBLANES, LANES), F32)

        @pl.when((n == 0) & (h == 0))
        def _():
            dcw[...] = jnp.zeros(dcw.shape, F32)
            dgp[...] = jnp.zeros(dgp.shape, F32)

        live = (n < last).astype(F32)
        oha, ohb = _onehots(h)
        fn = functools.partial(_gdn_head, oha=oha, ohb=ohb)
        taps = [_conv_taps(x[...], hx[...] * live) for x, hx in ((xq, hq), (xk, hk), (xv, hv))]
        _, vjp = jax.vjp(fn, *taps, z[...], ab[...], st[...], cw[...], gp[...])
        g = vjp((dy[...], ds_scr[h]))
        outs = (dq, dk, dv)
        for j in range(3):
            tot = g[j][0]
            for s in range(1, GDN_CONV):
                slot = j * GDN_CONV + s
                tot = tot + _unshift_rows(g[j][s], car_scr[h, slot], s)
                car_scr[h, slot] = g[j][s][0:SUBLANES, :]
            outs[j][...] = tot.astype(outs[j].dtype)
        dz[...] = g[3].astype(dz.dtype)

        @pl.when(h == 0)
        def _():
            dab[...] = g[4]

        @pl.when(h > 0)
        def _():
            dab[...] += g[4]

        ds_scr[h] = g[5]
        dcw[h] += g[6]
        dgp[h] += g[7]

    rev = lambda n: last - n
    in_specs = _gdn_specs(rev) + [
        pl.BlockSpec((None, None, LANES, LANES), lambda n, h: (rev(n), h, 0, 0)),
        pl.BlockSpec((CHUNK, LANES), lambda n, h: (rev(n), h)),
    ]
    blk = pl.BlockSpec((CHUNK, LANES), lambda n, h: (rev(n), h))
    return pl.pallas_call(
        body, name="gdn_bwd", grid=(n_chunks, GDN_HEADS),
        in_specs=in_specs,
        out_specs=(blk, blk, blk, blk,
                   pl.BlockSpec((CHUNK, LANES), lambda n, h: (rev(n), 0)),
                   pl.BlockSpec((GDN_HEADS, 3, SUBLANES, LANES), lambda n, h: (0, 0, 0, 0)),
                   pl.BlockSpec((GDN_HEADS, SUBLANES, LANES), lambda n, h: (0, 0, 0))),
        out_shape=(jax.ShapeDtypeStruct((t, GDN_WIDTH), BF16), jax.ShapeDtypeStruct((t, GDN_WIDTH), BF16),
                   jax.ShapeDtypeStruct((t, GDN_WIDTH), BF16), jax.ShapeDtypeStruct((t, GDN_WIDTH), BF16),
                   jax.ShapeDtypeStruct((t, LANES), F32),
                   jax.ShapeDtypeStruct((GDN_HEADS, 3, SUBLANES, LANES), F32),
                   jax.ShapeDtypeStruct((GDN_HEADS, SUBLANES, LANES), F32)),
        scratch_shapes=[pltpu.VMEM((GDN_HEADS, LANES, LANES), F32),
                        pltpu.VMEM((GDN_HEADS, 3 * GDN_CONV, SUBLANES, LANES), F32)],
        compiler_params=_cparams(("arbitrary", "arbitrary")),
    )(p_cat, p_cat, p_cat, p_cat, p_cat, p_cat, p_cat, p_cat, cwpack, gpar, states, d_out)


def _pick(n, options):
    for o in options:
        if n % o == 0:
            return o
    raise ValueError(f"no tile for {n}")


_MM_DIMS = {'nn': (((1,), (0,)), ((), ())), 'nt': (((1,), (1,)), ((), ())), 'tn': (((0,), (0,)), ((), ()))}


def _matmul(a, b, mode, out_dtype, name):
    if mode == 'nn':
        (m, k), (k2, n) = a.shape, b.shape
    elif mode == 'nt':
        (m, k), (n, k2) = a.shape, b.shape
    else:
        (k, m), (k2, n) = a.shape, b.shape
    assert k == k2, (a.shape, b.shape, mode)
    tm = _pick(m, (1024, 512, 256, 128, 64))
    tn = _pick(n, (512, 256, 128))
    tk = _pick(k, (1024, 512, 256, 128, 64))
    nk = k // tk
    dims = _MM_DIMS[mode]

    def body(a_ref, b_ref, o_ref, acc_ref):
        kk = pl.program_id(2)
        part = lax.dot_general(a_ref[...], b_ref[...], dims, preferred_element_type=F32)

        @pl.when(kk == 0)
        def _():
            acc_ref[...] = part

        @pl.when(kk > 0)
        def _():
            acc_ref[...] += part

        @pl.when(kk == nk - 1)
        def _():
            o_ref[...] = acc_ref[...].astype(o_ref.dtype)

    a_spec = pl.BlockSpec((tk, tm), lambda i, j, kk: (kk, i)) if mode == 'tn' else pl.BlockSpec((tm, tk), lambda i, j, kk: (i, kk))
    b_spec = pl.BlockSpec((tn, tk), lambda i, j, kk: (j, kk)) if mode == 'nt' else pl.BlockSpec((tk, tn), lambda i, j, kk: (kk, j))
    return pl.pallas_call(
        body, name=name, grid=(m // tm, n // tn, nk),
        in_specs=[a_spec, b_spec],
        out_specs=pl.BlockSpec((tm, tn), lambda i, j, kk: (i, j)),
        out_shape=jax.ShapeDtypeStruct((m, n), out_dtype),
        scratch_shapes=[pltpu.VMEM((tm, tn), F32)],
        compiler_params=_cparams(("parallel", "parallel", "arbitrary")),
    )(a, b)


ROW_TILE = 256


def _row_specs(rows, tm):
    return [pl.BlockSpec((tm, w), lambda i, ci=ci: (i, ci)) for (_, w, ci) in rows]


def _rw_forward(fn, rows, pars, outs, name):
    t = rows[0][0].shape[0]
    tm = min(ROW_TILE, t)
    nr, npar = len(rows), len(pars)

    def body(*refs):
        vals = [r[...].astype(F32) for r in refs[:nr]] + [p[...] for p in refs[nr:nr + npar]]
        res = fn(*vals)
        for o, v in zip(refs[nr + npar:], res):
            o[...] = v.astype(o.dtype)

    return pl.pallas_call(
        body, name=name, grid=(t // tm,),
        in_specs=_row_specs(rows, tm) + [pl.BlockSpec(p.shape, lambda i: (0, 0)) for p in pars],
        out_specs=tuple(pl.BlockSpec((tm, w), lambda i: (i, 0)) for (w, _) in outs),
        out_shape=tuple(jax.ShapeDtypeStruct((t, w), dt) for (w, dt) in outs),
        compiler_params=_cparams(("parallel",)),
    )(*[r[0] for r in rows], *pars)


def _rw_backward(fn, rows, pars, cots, drow_dtypes, name):
    t = rows[0][0].shape[0]
    tm = min(ROW_TILE, t)
    nr, npar, nc = len(rows), len(pars), len(cots)
    keep = [i for i, dt in enumerate(drow_dtypes) if dt is not None]

    def body(*refs):
        vals = [r[...].astype(F32) for r in refs[:nr]] + [p[...] for p in refs[nr:nr + npar]]
        cvals = tuple(c[...].astype(F32) for c in refs[nr + npar:nr + npar + nc])
        orefs = refs[nr + npar + nc:]
        _, vjp = jax.vjp(fn, *vals)
        g = vjp(cvals)
        for o, i in zip(orefs[:len(keep)], keep):
            o[...] = g[i].astype(o.dtype)
        first = pl.program_id(0) == 0
        for o, gi in zip(orefs[len(keep):], g[nr:]):
            @pl.when(first)
            def _(o=o, gi=gi):
                o[...] = gi

            @pl.when(jnp.logical_not(first))
            def _(o=o, gi=gi):
                o[...] += gi

    out_specs = [pl.BlockSpec((tm, rows[i][1]), lambda i_: (i_, 0)) for i in keep] + \
                [pl.BlockSpec(p.shape, lambda i_: (0, 0)) for p in pars]
    out_shape = [jax.ShapeDtypeStruct((t, rows[i][1]), drow_dtypes[i]) for i in keep] + \
                [jax.ShapeDtypeStruct(p.shape, F32) for p in pars]
    return pl.pallas_call(
        body, name=name, grid=(t // tm,),
        in_specs=_row_specs(rows, tm) + [pl.BlockSpec(p.shape, lambda i: (0, 0)) for p in pars] + _row_specs(cots, tm),
        out_specs=tuple(out_specs), out_shape=tuple(out_shape),
        compiler_params=_cparams(("arbitrary",)),
    )(*[r[0] for r in rows], *pars, *[c[0] for c in cots])


def _norm_fn(x, g):
    return (_rms(x, g),)


def _norm_skip_fn(x, g):
    return _rms(x, g), x


def _merge_fn(ga, gb, ya, yb):
    return (_sigmoid(ga) * ya + _sigmoid(gb) * yb,)


def _res_norm_fn(x, mo, g):
    x1 = x + mo
    return x1, _rms(x1, g)


def _loss_head(x1, fo, gf, target, name):
    t, d = x1.shape
    tm = min(ROW_TILE, t)

    def tile_loss(x2, g, tgt):
        err = _rms(x2, g) - tgt
        per_row = jnp.sum(err * err, axis=-1, keepdims=True) * (0.5 / d)
        return jnp.sum(per_row, axis=0, keepdims=True)

    def body(x1_ref, fo_ref, g_ref, t_ref, loss_ref, dx_ref, dxb_ref, dg_ref):
        x2 = x1_ref[...] + fo_ref[...]
        val, vjp = jax.vjp(functools.partial(tile_loss, tgt=t_ref[...]), x2, g_ref[...])
        dx2, dg = vjp(jnp.ones((1, 1), F32))
        dx_ref[...] = dx2
        dxb_ref[...] = dx2.astype(BF16)
        first = pl.program_id(0) == 0

        @pl.when(first)
        def _():
            loss_ref[...] = jnp.broadcast_to(val, loss_ref.shape)
            dg_ref[...] = dg

        @pl.when(jnp.logical_not(first))
        def _():
            loss_ref[...] += jnp.broadcast_to(val, loss_ref.shape)
            dg_ref[...] += dg

    row = pl.BlockSpec((tm, d), lambda i: (i, 0))
    vec = pl.BlockSpec((1, d), lambda i: (0, 0))
    return pl.pallas_call(
        body, name=name, grid=(t // tm,),
        in_specs=[row, row, vec, row],
        out_specs=(pl.BlockSpec((1, LANES), lambda i: (0, 0)), row, row, vec),
        out_shape=(jax.ShapeDtypeStruct((1, LANES), F32), jax.ShapeDtypeStruct((t, d), F32),
                   jax.ShapeDtypeStruct((t, d), BF16), jax.ShapeDtypeStruct((1, d), F32)),
        compiler_params=_cparams(("arbitrary",)),
    )(x1, fo, gf, target)


FFN_TILE_ROWS = 512
FFN_TILE_COLS = 256
FFN_COL_BLOCKS = FFN_HIDDEN // FFN_TILE_COLS


def _conv3_past(x, halo, w):
    rows = lax.broadcasted_iota(jnp.int32, x.shape, 0)
    x1 = jnp.where(rows == 0, halo[7:8, :], pltpu.roll(x, 1, 0))
    x2 = jnp.where(rows == 0, halo[6:7, :], jnp.where(rows == 1, halo[7:8, :], pltpu.roll(x, 2, 0)))
    return x * w[2:3] + x1 * w[1:2] + x2 * w[0:1], x1, x2


def _ffn_in_specs(tm, imap, jmap):
    per = tm // SUBLANES
    tile = lambda off: pl.BlockSpec((tm, FFN_TILE_COLS), lambda *g: (imap(*g), off + jmap(*g) % FFN_COL_BLOCKS))
    halo = lambda off: pl.BlockSpec((SUBLANES, FFN_TILE_COLS),
                                    lambda *g: (jnp.maximum(imap(*g) * per - 1, 0), off + jmap(*g) % FFN_COL_BLOCKS))
    wsp = lambda off: pl.BlockSpec((FFN_CONV, FFN_TILE_COLS), lambda *g: (0, off + jmap(*g) % FFN_COL_BLOCKS))
    return [tile(0), halo(0), wsp(0), tile(FFN_COL_BLOCKS), halo(FFN_COL_BLOCKS), wsp(FFN_COL_BLOCKS)]


def _ffn_act_forward(hpre, cw):
    t = hpre.shape[0]
    tm = min(FFN_TILE_ROWS, t)

    def body(hg, pg, wg, hu, pu, wu, out):
        live = (pl.program_id(0) > 0).astype(F32)
        cg, _, _ = _conv3_past(hg[...], pg[...] * live, wg[...])
        cu, _, _ = _conv3_past(hu[...], pu[...] * live, wu[...])
        out[...] = (_silu(cg) * cu).astype(out.dtype)

    return pl.pallas_call(
        body, name="ffn_act_fwd", grid=(t // tm, FFN_COL_BLOCKS),
        in_specs=_ffn_in_specs(tm, lambda i, j: i, lambda i, j: j),
        out_specs=pl.BlockSpec((tm, FFN_TILE_COLS), lambda i, j: (i, j)),
        out_shape=jax.ShapeDtypeStruct((t, FFN_HIDDEN), BF16),
        compiler_params=_cparams(("parallel", "parallel")),
    )(hpre, hpre, cw, hpre, hpre, cw)


def _ffn_act_backward(hpre, cw, dact):
    t = hpre.shape[0]
    tm = min(FFN_TILE_ROWS, t)

    def body(hg, pg, wg, hu, pu, wu, da, dconv, dw):
        j, i = pl.program_id(0), pl.program_id(1)
        live = (i > 0).astype(F32)
        cg, g1, g2 = _conv3_past(hg[...], pg[...] * live, wg[...])
        cu, u1, u2 = _conv3_past(hu[...], pu[...] * live, wu[...])
        s = _sigmoid(cg)
        d = da[...]
        d_gate = d * cu * s * (1.0 + cg * (1.0 - s))
        d_up = d * cg * s
        is_gate = j < FFN_COL_BLOCKS
        dc = jnp.where(is_gate, d_gate, d_up)
        dconv[...] = dc
        taps = (jnp.where(is_gate, g2, u2), jnp.where(is_gate, g1, u1), jnp.where(is_gate, hg[...], hu[...]))
        sums = [jnp.sum(xs * dc, axis=0, keepdims=True) for xs in taps]

        @pl.when(i == 0)
        def _():
            for r_ in range(FFN_CONV):
                dw[r_:r_ + 1, :] = sums[r_]

        @pl.when(i > 0)
        def _():
            for r_ in range(FFN_CONV):
                dw[r_:r_ + 1, :] += sums[r_]

    return pl.pallas_call(
        body, name="ffn_act_bwd", grid=(2 * FFN_COL_BLOCKS, t // tm),
        in_specs=_ffn_in_specs(tm, lambda j, i: i, lambda j, i: j) +
        [pl.BlockSpec((tm, FFN_TILE_COLS), lambda j, i: (i, j % FFN_COL_BLOCKS))],
        out_specs=(pl.BlockSpec((tm, FFN_TILE_COLS), lambda j, i: (i, j)),
                   pl.BlockSpec((FFN_CONV, FFN_TILE_COLS), lambda j, i: (0, j))),
        out_shape=(jax.ShapeDtypeStruct((t, 2 * FFN_HIDDEN), F32), jax.ShapeDtypeStruct((FFN_CONV, 2 * FFN_HIDDEN), F32)),
        compiler_params=_cparams(("parallel", "arbitrary")),
    )(hpre, hpre, cw, hpre, hpre, cw, dact)


def _ffn_conv_backward(dconv, cw):
    t = dconv.shape[0]
    tm = min(FFN_TILE_ROWS, t)
    n_tiles = t // tm
    per = tm // SUBLANES

    def body(d_ref, nx_ref, w_ref, out):
        live = (pl.program_id(0) < n_tiles - 1).astype(F32)
        d = d_ref[...]
        nx = nx_ref[...] * live
        w = w_ref[...]
        rows = lax.broadcasted_iota(jnp.int32, d.shape, 0)
        d1 = jnp.where(rows == tm - 1, nx[0:1, :], pltpu.roll(d, tm - 1, 0))
        d2 = jnp.where(rows == tm - 1, nx[1:2, :], jnp.where(rows == tm - 2, nx[0:1, :], pltpu.roll(d, tm - 2, 0)))
        out[...] = (d * w[2:3] + d1 * w[1:2] + d2 * w[0:1]).astype(out.dtype)

    return pl.pallas_call(
        body, name="ffn_conv_bwd", grid=(n_tiles, 2 * FFN_COL_BLOCKS),
        in_specs=[pl.BlockSpec((tm, FFN_TILE_COLS), lambda i, j: (i, j)),
                  pl.BlockSpec((SUBLANES, FFN_TILE_COLS), lambda i, j: (jnp.minimum((i + 1) * per, t // SUBLANES - 1), j)),
                  pl.BlockSpec((FFN_CONV, FFN_TILE_COLS), lambda i, j: (0, j))],
        out_specs=pl.BlockSpec((tm, FFN_TILE_COLS), lambda i, j: (i, j)),
        out_shape=jax.ShapeDtypeStruct((t, 2 * FFN_HIDDEN), BF16),
        compiler_params=_cparams(("parallel", "parallel")),
    )(dconv, dconv, cw)


def _my_place():
    x, y, c = lax.axis_index("x"), lax.axis_index("y"), lax.axis_index("c")
    return x, y, c, 4 * x + 2 * y + c


def _peer(x, y, c, k):
    px, py, pc = x ^ ((k >> 2) & 1), y ^ ((k >> 1) & 1), c ^ (k & 1)
    return (px, py, pc), 4 * px + 2 * py + pc


def _all_gather(shard, name):
    r, w = shard.shape

    def body(src, out, send_sems, recv_sems, local_sem):
        x, y, c, me = _my_place()
        mine = pltpu.make_async_copy(src, out.at[me], local_sem)
        mine.start()
        sends, recvs = [], []
        for k in range(1, N_DEV):
            dev, idx = _peer(x, y, c, k)
            sends.append(pltpu.make_async_remote_copy(src_ref=src, dst_ref=out.at[me], send_sem=send_sems.at[k],
                                                      recv_sem=recv_sems.at[k], device_id=dev,
                                                      device_id_type=pl.DeviceIdType.MESH))
            recvs.append(pltpu.make_async_remote_copy(src_ref=src, dst_ref=out.at[idx], send_sem=send_sems.at[k],
                                                      recv_sem=recv_sems.at[k], device_id=dev,
                                                      device_id_type=pl.DeviceIdType.MESH))
        for cp in sends:
            cp.start()
        for cp in recvs:
            cp.wait_recv()
        for cp in sends:
            cp.wait_send()
        mine.wait()

    return pl.pallas_call(
        body, name=name,
        in_specs=[pl.BlockSpec(memory_space=pl.ANY)],
        out_specs=pl.BlockSpec(memory_space=pl.ANY),
        out_shape=jax.ShapeDtypeStruct((N_DEV, r, w), shard.dtype),
        scratch_shapes=[pltpu.SemaphoreType.DMA((N_DEV,)), pltpu.SemaphoreType.DMA((N_DEV,)), pltpu.SemaphoreType.DMA],
    )(shard)


def _exchange_slabs(slabs, name):
    def body(src, out, send_sems, recv_sems, local_sem):
        x, y, c, me = _my_place()
        mine = pltpu.make_async_copy(src.at[me], out.at[me], local_sem)
        mine.start()
        sends, recvs = [], []
        for k in range(1, N_DEV):
            dev, idx = _peer(x, y, c, k)
            sends.append(pltpu.make_async_remote_copy(src_ref=src.at[idx], dst_ref=out.at[me], send_sem=send_sems.at[k],
                                                      recv_sem=recv_sems.at[k], device_id=dev,
                                                      device_id_type=pl.DeviceIdType.MESH))
            recvs.append(pltpu.make_async_remote_copy(src_ref=src.at[idx], dst_ref=out.at[idx], send_sem=send_sems.at[k],
                                                      recv_sem=recv_sems.at[k], device_id=dev,
                                                      device_id_type=pl.DeviceIdType.MESH))
        for cp in sends:
            cp.start()
        for cp in recvs:
            cp.wait_recv()
        for cp in sends:
            cp.wait_send()
        mine.wait()

    return pl.pallas_call(
        body, name=name,
        in_specs=[pl.BlockSpec(memory_space=pl.ANY)],
        out_specs=pl.BlockSpec(memory_space=pl.ANY),
        out_shape=jax.ShapeDtypeStruct(slabs.shape, slabs.dtype),
        scratch_shapes=[pltpu.SemaphoreType.DMA((N_DEV,)), pltpu.SemaphoreType.DMA((N_DEV,)), pltpu.SemaphoreType.DMA],
    )(slabs)


def _reduce_adamw(parts, w, m, v, rows_per_step):
    _, r, wd = parts.shape
    c1 = 1.0 / (1.0 - ADAM_B1 ** ADAM_STEP)
    c2 = 1.0 / (1.0 - ADAM_B2 ** ADAM_STEP)

    def body(p_ref, w_ref, m_ref, v_ref, g_out, d_out, m_out, v_out):
        g = p_ref[0]
        for s in range(1, N_DEV):
            g = g + p_ref[s]
        mn = ADAM_B1 * m_ref[...] + (1.0 - ADAM_B1) * g
        vn = ADAM_B2 * v_ref[...] + (1.0 - ADAM_B2) * (g * g)
        g_out[...] = g
        m_out[...] = mn
        v_out[...] = vn
        d_out[...] = -ADAM_LR * ((mn * c1) / (jnp.sqrt(vn * c2) + ADAM_EPS) + ADAM_WD * w_ref[...])

    blk = pl.BlockSpec((rows_per_step, wd), lambda i: (i, 0))
    shp = jax.ShapeDtypeStruct((r, wd), F32)
    return pl.pallas_call(
        body, name="reduce_adamw", grid=(r // rows_per_step,),
        in_specs=[pl.BlockSpec((N_DEV, rows_per_step, wd), lambda i: (0, i, 0)), blk, blk, blk],
        out_specs=(blk, blk, blk, blk), out_shape=(shp, shp, shp, shp),
        compiler_params=_cparams(("parallel",)),
    )(parts, w, m, v)


PACK_W = 1024


def _pad_heads(a, slots):
    lead = a.shape[:-1]
    a = a.reshape(lead + (slots, RWKV_HEAD_DIM))
    a = jnp.pad(a, [(0, 0)] * (len(lead) + 1) + [(0, LANES - RWKV_HEAD_DIM)])
    return a.reshape(lead + (slots * LANES,))


def _unpad_heads(a, slots):
    lead = a.shape[:-1]
    return a.reshape(lead + (slots, LANES))[..., :RWKV_HEAD_DIM].reshape(lead + (slots * RWKV_HEAD_DIM,))


def _flat_pack(arrs, dtype, row_mult):
    flat = jnp.concatenate([a.reshape(-1).astype(dtype) for a in arrs])
    n = flat.shape[0]
    rows = -(-n // PACK_W)
    rows = -(-rows // row_mult) * row_mult
    return jnp.pad(flat, (0, rows * PACK_W - n)).reshape(rows, PACK_W)


def _unpack_gathered(g, names, shard_shapes):
    flat = g.reshape(N_DEV, -1)
    out, off = {}, 0
    for n in names:
        s = shard_shapes[n]
        size = s[0] * s[1]
        seg = flat[:, off:off + size].reshape((N_DEV,) + tuple(s))
        off += size
        if SHARD_AXIS[n] == 1:
            out[n] = jnp.transpose(seg, (1, 0, 2)).reshape(s[0], N_DEV * s[1])
        else:
            out[n] = seg.reshape(N_DEV * s[0], s[1])
    return out


def _shard_major(full, axis):
    a, b = full.shape
    if axis == 1:
        return jnp.transpose(full.reshape(a, N_DEV, b // N_DEV), (1, 0, 2)).reshape(N_DEV, -1)
    return full.reshape(N_DEV, -1)


def _prepare_weights(full, rep):
    w = full['w_in']
    d = w.shape[1]
    rkv = jnp.pad(w[0:1536].reshape(3 * RWKV_HEADS, RWKV_HEAD_DIM, d), ((0, 0), (0, LANES - RWKV_HEAD_DIM), (0, 0)))
    w_cat = jnp.concatenate([
        w[3848:4872], w[4872:5896], rkv.reshape(3 * RWKV_HEADS * LANES, d), w[1536:1792],
        w[1792:3328], w[3328:3840], jnp.pad(w[3840:3848], ((0, LANES - 8), (0, 0))), jnp.zeros((LANES, d), w.dtype)], axis=0)
    assert w_cat.shape[0] == CAT_W
    rp = full['rwkv_proj']
    rproj = jnp.pad(rp.reshape(RWKV_HEADS, RWKV_HEAD_DIM, -1), ((0, 0), (0, LANES - RWKV_HEAD_DIM), (0, 0))).reshape(RWKV_HEADS * LANES, -1)
    mu = rep['rwkv_mu']
    vecs = [mu[0:512], mu[512:1024], mu[1024:1536], rep['rwkv_w0'], rep['rwkv_a0'], rep['rwkv_k_k'], rep['rwkv_k_a'],
            rep['rwkv_ln_w'], rep['rwkv_ln_b'], rep['rwkv_r_k'].reshape(-1)]
    ppack = jnp.stack([jnp.pad(v.reshape(RWKV_HEADS, RWKV_HEAD_DIM), ((0, 0), (0, LANES - RWKV_HEAD_DIM))) for v in vecs], axis=1)
    ppack = jnp.pad(ppack, ((0, 0), (0, 16 - len(vecs)), (0, 0)))
    mulo = mu[1536:1792].reshape(1, 2 * LANES)
    wl = jnp.zeros((3, 2 * LANES, RWKV_HEADS * LANES), F32)
    wl = wl.at[0, 0:64].set(_pad_heads(full['rwkv_w2'], RWKV_HEADS))
    wl = wl.at[1, 64:128].set(_pad_heads(full['rwkv_a2'], RWKV_HEADS))
    wl = wl.at[2, 128:256].set(_pad_heads(full['rwkv_g2'], RWKV_HEADS))
    cw = full['gdn_conv_w'].reshape(GDN_CONV, 3, GDN_HEADS, LANES)
    cwpack = jnp.pad(jnp.transpose(cw, (2, 1, 0, 3)), ((0, 0), (0, 0), (0, SUBLANES - GDN_CONV), (0, 0)))
    gpar = jnp.zeros((SUBLANES, LANES), F32)
    gpar = gpar.at[0, 0:GDN_HEADS].set(rep['gdn_a_log']).at[1, 0:GDN_HEADS].set(rep['gdn_dt_bias']).at[2].set(rep['gdn_norm_w'])
    return dict(w_cat=w_cat, rproj=rproj, gproj=full['gdn_proj'], w_out=full['w_out'], ffn_up=full['ffn_up'],
                ffn_down=full['ffn_down'], ffn_cw=full['ffn_conv_w'], ppack=ppack, mulo=mulo, wl=wl, cwpack=cwpack, gpar=gpar,
                g1=rep['norm1_g'].reshape(1, -1), g2=rep['norm2_g'].reshape(1, -1), gf=rep['final_g'].reshape(1, -1))


def _local_step(x, target, p):
    d = x.shape[1]
    full_w = lambda a: (a, a.shape[1], 0)
    (u,) = _rw_forward(_norm_fn, [full_w(x)], [p['g1']], [(d, BF16)], "norm1")
    p_cat = _matmul(u, p['w_cat'], 'nt', F32, "proj_in")
    ya_pre, st_r = _rwkv_forward(p_cat, p['ppack'], p['mulo'], p['wl'])
    yb_pre, st_g = _gdn_forward(p_cat, p['cwpack'], p['gpar'])
    ya = _matmul(ya_pre, p['rproj'], 'nn', F32, "rwkv_proj")
    yb = _matmul(yb_pre, p['gproj'], 'nn', F32, "gdn_proj")
    gates = [(p_cat, d, OFF_GA // d), (p_cat, d, OFF_GB // d)]
    (mixed,) = _rw_forward(_merge_fn, gates + [full_w(ya), full_w(yb)], [], [(d, BF16)], "merge")
    mo = _matmul(mixed, p['w_out'], 'nn', F32, "out_proj")
    x1, n2 = _rw_forward(_res_norm_fn, [full_w(x), full_w(mo)], [p['g2']], [(d, F32), (d, BF16)], "res_norm2")
    hpre = _matmul(n2, p['ffn_up'], 'nt', F32, "ffn_up")
    act = _ffn_act_forward(hpre, p['ffn_cw'])
    fo = _matmul(act, p['ffn_down'], 'nn', F32, "ffn_down")
    loss_vec, dx2, dx2b, dgf = _loss_head(x1, fo, p['gf'], target, "loss_head")

    dact = _matmul(dx2b, p['ffn_down'], 'nt', F32, "d_act")
    dw_down = _matmul(act, dx2b, 'tn', F32, "dw_ffn_down")
    dconv, dcw_f = _ffn_act_backward(hpre, p['ffn_cw'], dact)
    dh = _ffn_conv_backward(dconv, p['ffn_cw'])
    dn2 = _matmul(dh, p['ffn_up'], 'nn', F32, "d_norm2")
    dw_up = _matmul(dh, n2, 'tn', F32, "dw_ffn_up")
    dx1, dx1b, dg2 = _rw_backward(_res_norm_fn, [full_w(x), full_w(mo)], [p['g2']], [full_w(dx2), full_w(dn2)],
                                  [F32, BF16], "res_norm2_bwd")
    dmixed = _matmul(dx1b, p['w_out'], 'nt', F32, "d_mixed")
    dw_out = _matmul(mixed, dx1b, 'tn', F32, "dw_out")
    dga, dgb, dya, dyb = _rw_backward(_merge_fn, gates + [full_w(ya), full_w(yb)], [], [full_w(dmixed)],
                                      [BF16, BF16, BF16, BF16], "merge_bwd")
    d_ya_pre = _matmul(dya, p['rproj'], 'nt', F32, "d_rwkv_out")
    dw_rproj = _matmul(ya_pre, dya, 'tn', F32, "dw_rwkv_proj")
    d_yb_pre = _matmul(dyb, p['gproj'], 'nt', F32, "d_gdn_out")
    dw_gproj = _matmul(yb_pre, dyb, 'tn', F32, "dw_gdn_proj")
    dpr, dpk, dpv, dplo, dpp, dml, dwl = _rwkv_backward(p_cat, p['ppack'], p['mulo'], p['wl'], st_r, d_ya_pre)
    dq, dk, dv, dz, dab, dcw_g, dgp = _gdn_backward(p_cat, p['cwpack'], p['gpar'], st_g, d_yb_pre)
    t = x.shape[0]
    dp_cat = jnp.concatenate([dga, dgb, dpr, dpk, dpv, dplo.astype(BF16), dq, dk, dv, dz, dab.astype(BF16),
                              jnp.zeros((t, LANES), BF16)], axis=1)
    du = _matmul(dp_cat, p['w_cat'], 'nn', F32, "d_norm1")
    dw_cat = _matmul(dp_cat, u, 'tn', F32, "dw_in")
    grad_x, dg1 = _rw_backward(_norm_skip_fn, [full_w(x)], [p['g1']], [full_w(du), full_w(dx1)], [F32], "norm1_bwd")

    heads = lambda row: dpp[:, row, :RWKV_HEAD_DIM].reshape(-1)
    lora = lambda j, lo_, hi_: jnp.transpose(dwl[:, j, lo_:hi_, :RWKV_HEAD_DIM], (1, 0, 2)).reshape(hi_ - lo_, RWKV_WIDTH)
    grads = {
        'norm1_g': dg1[0],
        'w_in': jnp.concatenate([dw_cat[OFF_RKV:OFF_LO].reshape(3 * RWKV_HEADS, LANES, d)[:, :RWKV_HEAD_DIM].reshape(-1, d),
                                 dw_cat[OFF_LO:OFF_QKV], dw_cat[OFF_QKV:OFF_Z], dw_cat[OFF_Z:OFF_AB], dw_cat[OFF_AB:OFF_AB + 8],
                                 dw_cat[OFF_GA:OFF_GB], dw_cat[OFF_GB:OFF_RKV]], axis=0),
        'rwkv_mu': jnp.concatenate([heads(0), heads(1), heads(2), jnp.sum(dml[:, 0, :], axis=0)]),
        'rwkv_w0': heads(3), 'rwkv_a0': heads(4), 'rwkv_k_k': heads(5), 'rwkv_k_a': heads(6),
        'rwkv_ln_w': heads(7), 'rwkv_ln_b': heads(8), 'rwkv_r_k': heads(9).reshape(RWKV_HEADS, RWKV_HEAD_DIM),
        'rwkv_w2': lora(0, 0, 64), 'rwkv_a2': lora(1, 64, 128), 'rwkv_g2': lora(2, 128, 256),
        'rwkv_proj': dw_rproj.reshape(RWKV_HEADS, LANES, -1)[:, :RWKV_HEAD_DIM].reshape(RWKV_WIDTH, -1),
        'gdn_conv_w': jnp.transpose(dcw_g[:, :, :GDN_CONV, :], (2, 1, 0, 3)).reshape(GDN_CONV, 3 * GDN_WIDTH),
        'gdn_a_log': jnp.sum(dgp[:, 0, :GDN_HEADS], axis=0), 'gdn_dt_bias': jnp.sum(dgp[:, 1, :GDN_HEADS], axis=0),
        'gdn_norm_w': jnp.sum(dgp[:, 2, :], axis=0),
        'gdn_proj': dw_gproj, 'w_out': dw_out, 'norm2_g': dg2[0], 'ffn_up': dw_up, 'ffn_conv_w': dcw_f,
        'ffn_down': dw_down, 'final_g': dgf[0],
    }
    return loss_vec, grad_x, grads


def kernel(x, norm1_g, w_in, rwkv_mu, rwkv_w0, rwkv_w2, rwkv_a0, rwkv_a2, rwkv_g2, rwkv_k_k, rwkv_k_a, rwkv_r_k, rwkv_ln_w, rwkv_ln_b, rwkv_proj, gdn_conv_w, gdn_a_log, gdn_dt_bias, gdn_norm_w, gdn_proj, w_out, norm2_g, ffn_up, ffn_conv_w, ffn_down, final_g, loss_target, m_norm1_g, m_w_in, m_rwkv_mu, m_rwkv_w0, m_rwkv_w2, m_rwkv_a0, m_rwkv_a2, m_rwkv_g2, m_rwkv_k_k, m_rwkv_k_a, m_rwkv_r_k, m_rwkv_ln_w, m_rwkv_ln_b, m_rwkv_proj, m_gdn_conv_w, m_gdn_a_log, m_gdn_dt_bias, m_gdn_norm_w, m_gdn_proj, m_w_out, m_norm2_g, m_ffn_up, m_ffn_conv_w, m_ffn_down, m_final_g, v_norm1_g, v_w_in, v_rwkv_mu, v_rwkv_w0, v_rwkv_w2, v_rwkv_a0, v_rwkv_a2, v_rwkv_g2, v_rwkv_k_k, v_rwkv_k_a, v_rwkv_r_k, v_rwkv_ln_w, v_rwkv_ln_b, v_rwkv_proj, v_gdn_conv_w, v_gdn_a_log, v_gdn_dt_bias, v_gdn_norm_w, v_gdn_proj, v_w_out, v_norm2_g, v_ffn_up, v_ffn_conv_w, v_ffn_down, v_final_g):
    given = dict(zip(WEIGHT_NAMES, (norm1_g, w_in, rwkv_mu, rwkv_w0, rwkv_w2, rwkv_a0, rwkv_a2, rwkv_g2, rwkv_k_k, rwkv_k_a, rwkv_r_k,
                                    rwkv_ln_w, rwkv_ln_b, rwkv_proj, gdn_conv_w, gdn_a_log, gdn_dt_bias, gdn_norm_w, gdn_proj, w_out,
                                    norm2_g, ffn_up, ffn_conv_w, ffn_down, final_g)))
    mom1 = dict(zip(WEIGHT_NAMES, (m_norm1_g, m_w_in, m_rwkv_mu, m_rwkv_w0, m_rwkv_w2, m_rwkv_a0, m_rwkv_a2, m_rwkv_g2, m_rwkv_k_k,
                                   m_rwkv_k_a, m_rwkv_r_k, m_rwkv_ln_w, m_rwkv_ln_b, m_rwkv_proj, m_gdn_conv_w, m_gdn_a_log,
                                   m_gdn_dt_bias, m_gdn_norm_w, m_gdn_proj, m_w_out, m_norm2_g, m_ffn_up, m_ffn_conv_w, m_ffn_down,
                                   m_final_g)))
    mom2 = dict(zip(WEIGHT_NAMES, (v_norm1_g, v_w_in, v_rwkv_mu, v_rwkv_w0, v_rwkv_w2, v_rwkv_a0, v_rwkv_a2, v_rwkv_g2, v_rwkv_k_k,
                                   v_rwkv_k_a, v_rwkv_r_k, v_rwkv_ln_w, v_rwkv_ln_b, v_rwkv_proj, v_gdn_conv_w, v_gdn_a_log,
                                   v_gdn_dt_bias, v_gdn_norm_w, v_gdn_proj, v_w_out, v_norm2_g, v_ffn_up, v_ffn_conv_w, v_ffn_down,
                                   v_final_g)))
    def strip(n, a):
        a = a if n == 'final_g' else a.reshape(a.shape[1:])
        return a.T if n in TRANSPOSED else a

    local = {n: strip(n, a) for n, a in given.items()}
    shard_shapes = {n: local[n].shape for n in SHARD_AXIS}
    sharded = BIG_SHARDED + SMALL_SHARDED

    g_big = _all_gather(_flat_pack([local[n] for n in BIG_SHARDED], BF16, 16), "gather_big")
    g_small = _all_gather(_flat_pack([local[n] for n in SMALL_SHARDED], F32, SUBLANES), "gather_small")
    full = _unpack_gathered(g_big, BIG_SHARDED, shard_shapes)
    full.update(_unpack_gathered(g_small, SMALL_SHARDED, shard_shapes))
    rep = {n: local[n] for n in REPLICATED}

    loss_vec, grad_x, grads = _local_step(x[0], loss_target[0], _prepare_weights(full, rep))

    rep_vec = jnp.concatenate([grads[n].reshape(-1) for n in REPLICATED] + [loss_vec[0, 0:1]])
    slab = jnp.concatenate([_shard_major(grads[n], SHARD_AXIS[n]) for n in sharded] +
                           [jnp.broadcast_to(rep_vec[None], (N_DEV, rep_vec.shape[0]))], axis=1)
    n_elem = slab.shape[1]
    rows_per_step = 208
    rows = -(-n_elem // (PACK_W * rows_per_step)) * rows_per_step
    slab = jnp.pad(slab, ((0, 0), (0, rows * PACK_W - n_elem))).reshape(N_DEV, rows, PACK_W)
    parts = _exchange_slabs(slab, "grad_exchange")

    def pack_local(src):
        flat = jnp.concatenate([strip(n, src[n]).reshape(-1) for n in sharded + REPLICATED])
        return jnp.pad(flat, (0, rows * PACK_W - flat.shape[0])).reshape(rows, PACK_W)

    packs = _reduce_adamw(parts, pack_local(given), pack_local(mom1), pack_local(mom2), rows_per_step)

    def unpack(pk):
        flat = pk.reshape(-1)
        out, off = {}, 0
        for n in sharded + REPLICATED:
            size = int(np.prod(given[n].shape))
            seg = flat[off:off + size].reshape(local[n].shape)
            out[n] = (seg.T if n in TRANSPOSED else seg).reshape(given[n].shape)
            off += size
        return out, flat[off]

    (g_out, loss), (d_out, _), (m_out, _), (v_out, _) = (unpack(pk) for pk in packs)
    return (loss, grad_x[None], *[g_out[n] for n in WEIGHT_NAMES], *[d_out[n] for n in WEIGHT_NAMES],
            *[m_out[n] for n in WEIGHT_NAMES], *[v_out[n] for n in WEIGHT_NAMES])
```

```python
import functools

import jax
import jax.numpy as jnp
import numpy as np
from jax import lax
from jax.experimental import pallas as pl
from jax.experimental.pallas import tpu as pltpu

F32 = jnp.float32
BF16 = jnp.bfloat16
HI = lax.Precision.HIGHEST

N_DEV = 8
D_MODEL = 1024
CHUNK = 64
RWKV_HEADS = 8
RWKV_HEAD_DIM = 64
RWKV_WIDTH = 512
GDN_HEADS = 4
GDN_HEAD_DIM = 128
GDN_WIDTH = 512
GDN_CONV = 4
FFN_HIDDEN = 2816
FFN_CONV = 3
NORM_EPS = 1e-6
L2_EPS = 1e-6
RWKV_GN_EPS = 64e-5
LANES = 128
SUBLANES = 8
VMEM_LIMIT = 56 * 1024 * 1024

ADAM_LR = 0.001
ADAM_B1 = 0.9
ADAM_B2 = 0.999
ADAM_EPS = 1e-08
ADAM_WD = 0.01
ADAM_STEP = 10

OFF_GA, OFF_GB, OFF_RKV, OFF_QKV, OFF_Z, OFF_LO, OFF_AB, CAT_W = 0, 1024, 2048, 5120, 6656, 7168, 7424, 7680
RWKV_HB = 8
GDN_HB = 4

WEIGHT_NAMES = ['norm1_g', 'w_in', 'rwkv_mu', 'rwkv_w0', 'rwkv_w2', 'rwkv_a0', 'rwkv_a2', 'rwkv_g2', 'rwkv_k_k', 'rwkv_k_a',
                'rwkv_r_k', 'rwkv_ln_w', 'rwkv_ln_b', 'rwkv_proj', 'gdn_conv_w', 'gdn_a_log', 'gdn_dt_bias', 'gdn_norm_w',
                'gdn_proj', 'w_out', 'norm2_g', 'ffn_up', 'ffn_conv_w', 'ffn_down', 'final_g']
BIG_SHARDED = ['w_in', 'ffn_up', 'ffn_down', 'w_out', 'rwkv_proj', 'gdn_proj']
SMALL_SHARDED = ['rwkv_w2', 'rwkv_a2', 'rwkv_g2', 'gdn_conv_w', 'ffn_conv_w']
TRANSPOSED = ('w_in', 'ffn_up')
SHARD_AXIS = {'w_in': 0, 'ffn_up': 0, 'ffn_down': 0, 'w_out': 0, 'rwkv_proj': 1, 'gdn_proj': 1,
              'rwkv_w2': 1, 'rwkv_a2': 1, 'rwkv_g2': 1, 'gdn_conv_w': 1, 'ffn_conv_w': 1}
REPLICATED = [n for n in WEIGHT_NAMES if n not in SHARD_AXIS]


def _cparams(sem=None):
    kw = dict(vmem_limit_bytes=VMEM_LIMIT)
    if sem is not None:
        kw['dimension_semantics'] = sem
    return pltpu.CompilerParams(**kw)


_NN, _NT, _TN = 'nn', 'nt', 'tn'
_DIMS_2D = {'nn': (((1,), (0,)), ((), ())), 'nt': (((1,), (1,)), ((), ())), 'tn': (((0,), (0,)), ((), ()))}
_DIMS_3D = {'nn': (((2,), (1,)), ((0,), (0,))), 'nt': (((2,), (2,)), ((0,), (0,))), 'tn': (((1,), (1,)), ((0,), (0,)))}


def _dg(a, b, kind):
    return lax.dot_general(a, b, (_DIMS_2D if a.ndim == 2 else _DIMS_3D)[kind], preferred_element_type=F32)


def _dot1(a, b, kind):
    return _dg(a.astype(BF16), b.astype(BF16), kind)


@jax.custom_vjp
def _dhi(a, b):
    return _dot1(a, b, _NN)


_dhi.defvjp(lambda a, b: (_dot1(a, b, _NN), (a, b)),
            lambda res, ct: (_dot1(ct, res[1], _NT), _dot1(res[0], ct, _TN)))


@jax.custom_vjp
def _dnt(a, b):
    return _dot1(a, b, _NT)


_dnt.defvjp(lambda a, b: (_dot1(a, b, _NT), (a, b)),
            lambda res, ct: (_dot1(ct, res[1], _NN), _dot1(ct, res[0], _TN)))


@jax.custom_vjp
def _dtn(a, b):
    return _dot1(a, b, _TN)


_dtn.defvjp(lambda a, b: (_dot1(a, b, _TN), (a, b)),
            lambda res, ct: (_dot1(res[1], ct, _NT), _dot1(res[0], ct, _NN)))


def _split3(x):
    x1 = x.astype(BF16)
    r1 = x - x1.astype(F32)
    x2 = r1.astype(BF16)
    return x1, x2, (r1 - x2.astype(F32)).astype(BF16)


def _dot_exact_lhs(sel, x, kind):
    parts = [_dg(sel, xi, kind) for xi in _split3(x)]
    return parts[0] + parts[1] + parts[2]


def _tril_ones(like):
    c = like.shape[-2]
    ri, ci = _iotas(c)
    return jnp.broadcast_to((ri >= ci).astype(BF16), like.shape[:-2] + (c, c))


@jax.custom_vjp
def _cumsum_rows(x):
    return _dot_exact_lhs(_tril_ones(x), x, _NN)


_cumsum_rows.defvjp(lambda x: (_dot_exact_lhs(_tril_ones(x), x, _NN), None),
                    lambda _, ct: (_dot_exact_lhs(_tril_ones(ct), ct, _TN),))


@jax.custom_vjp
def _lane_sum_as_row(x):
    return _dot_exact_lhs(jnp.ones(x.shape, BF16), x, _NT)


def _lane_sum_as_row_bwd(_, ct):
    ones = jnp.ones(ct.shape[:-1] + (LANES,), BF16)
    parts = [_dg(ci, ones, _TN) for ci in _split3(ct)]
    return (parts[0] + parts[1] + parts[2],)


_lane_sum_as_row.defvjp(lambda x: (_dot_exact_lhs(jnp.ones(x.shape, BF16), x, _NT), None), _lane_sum_as_row_bwd)


def _shift_rows(x, halo, s):
    rows = lax.broadcasted_iota(jnp.int32, x.shape, 0)
    out = pltpu.roll(x, s, 0)
    for i in range(s):
        out = jnp.where(rows == i, halo[SUBLANES - s + i:SUBLANES - s + i + 1, :], out)
    return out


def _unshift_rows(g, carry, s):
    c = g.shape[0]
    rows = lax.broadcasted_iota(jnp.int32, g.shape, 0)
    out = pltpu.roll(g, c - s, 0)
    for i in range(s):
        out = jnp.where(rows == c - s + i, carry[i:i + 1, :], out)
    return out


def _sigmoid(z):
    return 1.0 / (1.0 + jnp.exp(-z))


def _silu(z):
    return z * _sigmoid(z)


def _softplus(z):
    return jnp.maximum(z, 0.0) + jnp.log(1.0 + jnp.exp(-jnp.abs(z)))


def _rms(t, gain):
    return t * lax.rsqrt(jnp.mean(t * t, axis=-1, keepdims=True) + NORM_EPS) * gain


def _iotas(c):
    return lax.broadcasted_iota(jnp.int32, (c, c), 0), lax.broadcasted_iota(jnp.int32, (c, c), 1)


def _unit_lower_inverse(xm, eye):
    t = eye + xm
    p = xm
    for _ in range(5):
        p = _dhi(p, p)
        t = t + _dhi(t, p)
    return t


def _rwkv_head(pr, pk, pv, plo, qr, qk, qv, qlo, s0, pp, mulo, wl):
    c = pr.shape[1]
    ri, ci = _iotas(c)

    def mix(p, q, mu):
        return p + (q - p) * mu

    r = mix(pr, qr, pp[:, 0:1])
    k = mix(pk, qk, pp[:, 1:2])
    v = mix(pv, qv, pp[:, 2:3])
    lo = mix(plo, qlo, mulo)
    w0, a0, k_k, k_a, ln_w, ln_b, r_k = (pp[:, i:i + 1] for i in range(3, 10))
    per_head = lambda t: jnp.broadcast_to(t, (pr.shape[0],) + t.shape)
    zw = _dhi(per_head(jnp.tanh(lo)), wl[:, 0])
    za = _dhi(per_head(lo), wl[:, 1])
    g = _dhi(per_head(_sigmoid(lo)), wl[:, 2])
    w_log = -_softplus(-(w0 + zw)) - 0.5
    lw = -jnp.exp(w_log)
    a = _sigmoid(a0 + za)
    kk = k * k_k
    kk = kk * lax.rsqrt(jnp.sum(kk * kk, axis=-1, keepdims=True) + L2_EPS)
    k2 = k * (1.0 + (a - 1.0) * k_a)
    an = -kk
    b = kk * a
    causal = ri >= ci
    strict = ri > ci
    eye = (ri == ci).astype(F32)
    cl = _cumsum_rows(lw)
    ecl = jnp.exp(-cl)
    at = an * jnp.exp(cl - lw)
    bt = b * ecl
    kt = k2 * ecl
    rt = r * jnp.exp(cl)
    a_ab = jnp.where(strict, _dnt(at, bt), 0.0)
    a_ak = jnp.where(strict, _dnt(at, kt), 0.0)
    tinv = _unit_lower_inverse(a_ab, eye)
    u = _dhi(tinv, _dnt(at, s0) + _dhi(a_ak, v))
    y = _dnt(rt, s0) + _dhi(jnp.where(causal, _dnt(rt, bt), 0.0), u) + _dhi(jnp.where(causal, _dnt(rt, kt), 0.0), v)
    cl_end = jnp.sum(lw, axis=1, keepdims=True)
    dec_end = jnp.exp(cl_end - cl)
    s1 = s0 * jnp.exp(cl_end) + _dtn(u, b * dec_end) + _dtn(v, k2 * dec_end)
    m = (lax.broadcasted_iota(jnp.int32, (1, LANES), 1) < RWKV_HEAD_DIM).astype(F32)
    mean = jnp.sum(y, axis=-1, keepdims=True) * (1.0 / RWKV_HEAD_DIM)
    yc = (y - mean) * m
    var = jnp.sum(yc * yc, axis=-1, keepdims=True) * (1.0 / RWKV_HEAD_DIM)
    yn = yc * lax.rsqrt(var + RWKV_GN_EPS) * ln_w + ln_b
    y2 = yn + jnp.sum(r * k2 * r_k, axis=-1, keepdims=True) * v
    return y2 * g, s1


def _gdn_head(xq, xk, xv, z, ab, s0, cw, gp, oha, ohb):
    c = z.shape[1]
    ri, ci = _iotas(c)

    def conv(xs, w):
        out = xs[0] * w[:, GDN_CONV - 1:GDN_CONV]
        for s in range(1, GDN_CONV):
            out = out + xs[s] * w[:, GDN_CONV - 1 - s:GDN_CONV - s]
        return out

    q = _silu(conv(xq, cw[:, 0]))
    k = _silu(conv(xk, cw[:, 1]))
    v = _silu(conv(xv, cw[:, 2]))
    q = q * lax.rsqrt(jnp.sum(q * q, axis=-1, keepdims=True) + L2_EPS) * (GDN_HEAD_DIM ** -0.5)
    k = k * lax.rsqrt(jnp.sum(k * k, axis=-1, keepdims=True) + L2_EPS)
    gg = -jnp.exp(gp[0:1]) * _softplus(ab + gp[1:2])
    beta = jnp.sum(_sigmoid(ab) * ohb, axis=-1, keepdims=True)
    causal = ri >= ci
    strict = ri > ci
    eye = (ri == ci).astype(F32)
    gcm = _cumsum_rows(gg * oha)
    gc = jnp.sum(gcm, axis=-1, keepdims=True)
    gc_row = _lane_sum_as_row(gcm)
    dec = jnp.where(causal, jnp.exp(jnp.where(causal, gc - gc_row, 0.0)), 0.0)
    kb = k * beta
    vb = v * beta
    lm = jnp.where(strict, _dnt(kb, k) * dec, 0.0)
    tinv = _unit_lower_inverse(-lm, eye)
    egc = jnp.exp(gc)
    u = _dhi(tinv, vb)
    wk = _dhi(tinv, kb * egc)
    attn = jnp.where(causal, _dnt(q, k) * dec, 0.0)
    g_last = gc[:, c - 1:c, :]
    v_new = u - _dhi(wk, s0)
    o = _dhi(q * egc, s0) + _dhi(attn, v_new)
    s1 = s0 * jnp.exp(g_last) + _dtn(k * jnp.exp(g_last - gc), v_new)
    return _rms(o, gp[2:3]) * _silu(z), s1


def _head_id(grp, i, per_step, heads):
    return i if per_step == heads else grp * per_step + i


def _head_range(grp, per_step, heads):
    return slice(None) if per_step == heads else pl.ds(grp * per_step, per_step)


def _stack_heads(ref, n, fn=None):
    parts = []
    for i in range(n):
        sl = slice(i * LANES, (i + 1) * LANES)
        v = ref[:, sl]
        parts.append(v if fn is None else fn(v, sl))
    return jnp.stack(parts)


def _prev_rows_spec(width, col):
    per = CHUNK // SUBLANES
    return pl.BlockSpec((SUBLANES, width), lambda n, h: (jnp.maximum(n * per - 1, 0), col(h)))


def _rwkv_specs(nmap):
    hb, groups = RWKV_HB, RWKV_HEADS // RWKV_HB
    cb = OFF_RKV // (hb * LANES)
    specs = []
    for j in range(3):
        specs.append(pl.BlockSpec((CHUNK, hb * LANES), lambda n, g, j=j: (nmap(n), cb + j * groups + g)))
    specs.append(pl.BlockSpec((CHUNK, 2 * LANES), lambda n, g: (nmap(n), OFF_LO // (2 * LANES))))
    per = CHUNK // SUBLANES
    for j in range(3):
        specs.append(pl.BlockSpec((SUBLANES, hb * LANES),
                                  lambda n, g, j=j: (jnp.maximum(nmap(n) * per - 1, 0), cb + j * groups + g)))
    specs.append(pl.BlockSpec((SUBLANES, 2 * LANES), lambda n, g: (jnp.maximum(nmap(n) * per - 1, 0), OFF_LO // (2 * LANES))))
    specs.append(pl.BlockSpec((hb, 16, LANES), lambda n, g: (g, 0, 0)))
    specs.append(pl.BlockSpec((1, 2 * LANES), lambda n, g: (0, 0)))
    specs.append(pl.BlockSpec((hb, 3, 2 * LANES, LANES), lambda n, g: (g, 0, 0, 0)))
    return specs


def _rwkv_forward(p_cat, ppack, mulo, wl):
    t = p_cat.shape[0]
    n_chunks = t // CHUNK

    def body(pr, pk, pv, plo, hr, hk, hv, hlo, pp, ml, w, out, st_out, s_scr):
        n, grp = pl.program_id(0), pl.program_id(1)

        hsl = _head_range(grp, RWKV_HB, RWKV_HEADS)

        @pl.when(n == 0)
        def _():
            s_scr[hsl] = jnp.zeros((RWKV_HB, LANES, LANES), F32)

        live = (n > 0).astype(F32)
        lo = plo[...]
        lo_prev = _shift_rows(lo, hlo[...] * live, 1)
        cur = [_stack_heads(x, RWKV_HB) for x in (pr, pk, pv)]
        prev = [_stack_heads(x, RWKV_HB, lambda v, sl, hx=hx: _shift_rows(v, hx[:, sl] * live, 1))
                for x, hx in ((pr, hr), (pk, hk), (pv, hv))]
        s0 = s_scr[hsl]
        st_out[...] = s0
        o, s1 = _rwkv_head(*cur, lo, *prev, lo_prev, s0, pp[...], ml[...], w[...])
        for i in range(RWKV_HB):
            out[:, i * LANES:(i + 1) * LANES] = o[i].astype(out.dtype)
        s_scr[hsl] = s1

    return pl.pallas_call(
        body, name="rwkv_fwd", grid=(n_chunks, RWKV_HEADS // RWKV_HB),
        in_specs=_rwkv_specs(lambda n: n),
        out_specs=(pl.BlockSpec((CHUNK, RWKV_HB * LANES), lambda n, g: (n, g)),
                   pl.BlockSpec((None, RWKV_HB, LANES, LANES), lambda n, g: (n, g, 0, 0))),
        out_shape=(jax.ShapeDtypeStruct((t, RWKV_HEADS * LANES), BF16),
                   jax.ShapeDtypeStruct((n_chunks, RWKV_HEADS, LANES, LANES), F32)),
        scratch_shapes=[pltpu.VMEM((RWKV_HEADS, LANES, LANES), F32)],
        compiler_params=_cparams(("arbitrary", "arbitrary")),
    )(p_cat, p_cat, p_cat, p_cat, p_cat, p_cat, p_cat, p_cat, ppack, mulo, wl)


def _rwkv_backward(p_cat, ppack, mulo, wl, states, d_out):
    t = p_cat.shape[0]
    n_chunks = t // CHUNK
    last = n_chunks - 1

    def body(pr, pk, pv, plo, hr, hk, hv, hlo, pp, ml, w, st, dy, dpr, dpk, dpv, dplo, dpp, dml, dw, ds_scr, car_scr, carlo_scr):
        n, grp = pl.program_id(0), pl.program_id(1)

        hsl = _head_range(grp, RWKV_HB, RWKV_HEADS)
        gi = _head_id(grp, 0, 1, RWKV_HEADS // RWKV_HB)

        @pl.when(n == 0)
        def _():
            ds_scr[hsl] = jnp.zeros((RWKV_HB, LANES, LANES), F32)
            car_scr[hsl] = jnp.zeros((RWKV_HB, 3 * SUBLANES, LANES), F32)
            carlo_scr[gi] = jnp.zeros((SUBLANES, 2 * LANES), F32)

        @pl.when((n == 0) & (grp == 0))
        def _():
            dpp[...] = jnp.zeros(dpp.shape, F32)
            dml[...] = jnp.zeros(dml.shape, F32)
            dw[...] = jnp.zeros(dw.shape, F32)

        live = (n < last).astype(F32)
        lo = plo[...]
        lo_prev = _shift_rows(lo, hlo[...] * live, 1)
        cur = [_stack_heads(x, RWKV_HB) for x in (pr, pk, pv)]
        prev = [_stack_heads(x, RWKV_HB, lambda v, sl, hx=hx: _shift_rows(v, hx[:, sl] * live, 1))
                for x, hx in ((pr, hr), (pk, hk), (pv, hv))]
        _, vjp = jax.vjp(_rwkv_head, *cur, lo, *prev, lo_prev, st[...], pp[...], ml[...], w[...])
        g = vjp((_stack_heads(dy, RWKV_HB), ds_scr[hsl]))
        outs = (dpr, dpk, dpv)
        for i in range(RWKV_HB):
            sl = slice(i * LANES, (i + 1) * LANES)
            h = _head_id(grp, i, RWKV_HB, RWKV_HEADS)
            car = car_scr[h]
            for j in range(3):
                tot = g[j][i] + _unshift_rows(g[4 + j][i], car[SUBLANES * j:SUBLANES * (j + 1), :], 1)
                outs[j][:, sl] = tot.astype(outs[j].dtype)
                car_scr[h, SUBLANES * j:SUBLANES * (j + 1), :] = g[4 + j][i][0:SUBLANES, :]
        dlo = g[3] + _unshift_rows(g[7], carlo_scr[gi], 1)
        carlo_scr[gi] = g[7][0:SUBLANES, :]
        ds_scr[hsl] = g[8]
        dpp[hsl] += g[9]
        dml[0, 0:1, :] += g[10]
        dw[hsl] += g[11]

        @pl.when(grp == 0)
        def _():
            dplo[...] = dlo

        @pl.when(grp > 0)
        def _():
            dplo[...] += dlo

    rev = lambda n: last - n
    in_specs = _rwkv_specs(rev) + [
        pl.BlockSpec((None, RWKV_HB, LANES, LANES), lambda n, g: (rev(n), g, 0, 0)),
        pl.BlockSpec((CHUNK, RWKV_HB * LANES), lambda n, g: (rev(n), g)),
    ]
    hw = RWKV_HEADS * LANES
    return pl.pallas_call(
        body, name="rwkv_bwd", grid=(n_chunks, RWKV_HEADS // RWKV_HB),
        in_specs=in_specs,
        out_specs=(pl.BlockSpec((CHUNK, RWKV_HB * LANES), lambda n, g: (rev(n), g)),
                   pl.BlockSpec((CHUNK, RWKV_HB * LANES), lambda n, g: (rev(n), g)),
                   pl.BlockSpec((CHUNK, RWKV_HB * LANES), lambda n, g: (rev(n), g)),
                   pl.BlockSpec((CHUNK, 2 * LANES), lambda n, h: (rev(n), 0)),
                   pl.BlockSpec((RWKV_HEADS, 16, LANES), lambda n, h: (0, 0, 0)),
                   pl.BlockSpec((RWKV_HEADS, SUBLANES, 2 * LANES), lambda n, h: (0, 0, 0)),
                   pl.BlockSpec((RWKV_HEADS, 3, 2 * LANES, LANES), lambda n, h: (0, 0, 0, 0))),
        out_shape=(jax.ShapeDtypeStruct((t, hw), BF16), jax.ShapeDtypeStruct((t, hw), BF16), jax.ShapeDtypeStruct((t, hw), BF16),
                   jax.ShapeDtypeStruct((t, 2 * LANES), F32),
                   jax.ShapeDtypeStruct((RWKV_HEADS, 16, LANES), F32),
                   jax.ShapeDtypeStruct((RWKV_HEADS, SUBLANES, 2 * LANES), F32),
                   jax.ShapeDtypeStruct((RWKV_HEADS, 3, 2 * LANES, LANES), F32)),
        scratch_shapes=[pltpu.VMEM((RWKV_HEADS, LANES, LANES), F32),
                        pltpu.VMEM((RWKV_HEADS, 3 * SUBLANES, LANES), F32),
                        pltpu.VMEM((RWKV_HEADS, SUBLANES, 2 * LANES), F32)],
        compiler_params=_cparams(("arbitrary", "arbitrary")),
    )(p_cat, p_cat, p_cat, p_cat, p_cat, p_cat, p_cat, p_cat, ppack, mulo, wl, states, d_out)


def _gdn_specs(nmap):
    cb = OFF_QKV // LANES
    per = CHUNK // SUBLANES
    hb, groups = GDN_HB, GDN_HEADS // GDN_HB
    cb = OFF_QKV // (hb * LANES)
    specs = []
    for j in range(3):
        specs.append(pl.BlockSpec((CHUNK, hb * LANES), lambda n, g, j=j: (nmap(n), cb + j * groups + g)))
    for j in range(3):
        specs.append(pl.BlockSpec((SUBLANES, hb * LANES),
                                  lambda n, g, j=j: (jnp.maximum(nmap(n) * per - 1, 0), cb + j * groups + g)))
    specs.append(pl.BlockSpec((CHUNK, hb * LANES), lambda n, g: (nmap(n), OFF_Z // (hb * LANES) + g)))
    specs.append(pl.BlockSpec((CHUNK, LANES), lambda n, g: (nmap(n), OFF_AB // LANES)))
    specs.append(pl.BlockSpec((hb, 3, SUBLANES, LANES), lambda n, g: (g, 0, 0, 0)))
    specs.append(pl.BlockSpec((SUBLANES, LANES), lambda n, g: (0, 0)))
    return specs


def _conv_taps(x, halo):
    return (x,) + tuple(_shift_rows(x, halo, s) for s in range(1, GDN_CONV))


def _onehots(grp):
    lane = lax.broadcasted_iota(jnp.int32, (GDN_HB, 1, LANES), 2)
    head = lax.broadcasted_iota(jnp.int32, (GDN_HB, 1, LANES), 0) + _head_id(grp, 0, GDN_HB, GDN_HEADS)
    return (lane == head).astype(F32), (lane == GDN_HEADS + head).astype(F32)


def _gdn_taps(refs, halos, live):
    out = []
    for x, hx in zip(refs, halos):
        per_head = [_conv_taps(x[:, i * LANES:(i + 1) * LANES], hx[:, i * LANES:(i + 1) * LANES] * live) for i in range(GDN_HB)]
        out.append(tuple(jnp.stack([per_head[i][s] for i in range(GDN_HB)]) for s in range(GDN_CONV)))
    return out


def _gdn_forward(p_cat, cwpack, gpar):
    t = p_cat.shape[0]
    n_chunks = t // CHUNK

    def body(xq, xk, xv, hq, hk, hv, z, ab, cw, gp, out, st_out, s_scr):
        n, grp = pl.program_id(0), pl.program_id(1)

        hsl = _head_range(grp, GDN_HB, GDN_HEADS)

        @pl.when(n == 0)
        def _():
            s_scr[hsl] = jnp.zeros((GDN_HB, LANES, LANES), F32)

        live = (n > 0).astype(F32)
        oha, ohb = _onehots(grp)
        s0 = s_scr[hsl]
        st_out[...] = s0
        taps = _gdn_taps((xq, xk, xv), (hq, hk, hv), live)
        o, s1 = _gdn_head(*taps, _stack_heads(z, GDN_HB), ab[...], s0, cw[...], gp[...], oha, ohb)
        for i in range(GDN_HB):
            out[:, i * LANES:(i + 1) * LANES] = o[i].astype(out.dtype)
        s_scr[hsl] = s1

    return pl.pallas_call(
        body, name="gdn_fwd", grid=(n_chunks, GDN_HEADS // GDN_HB),
        in_specs=_gdn_specs(lambda n: n),
        out_specs=(pl.BlockSpec((CHUNK, GDN_HB * LANES), lambda n, g: (n, g)),
                   pl.BlockSpec((None, GDN_HB, LANES, LANES), lambda n, g: (n, g, 0, 0))),
        out_shape=(jax.ShapeDtypeStruct((t, GDN_WIDTH), BF16),
                   jax.ShapeDtypeStruct((n_chunks, GDN_HEADS, LANES, LANES), F32)),
        scratch_shapes=[pltpu.VMEM((GDN_HEADS, LANES, LANES), F32)],
        compiler_params=_cparams(("arbitrary", "arbitrary")),
    )(p_cat, p_cat, p_cat, p_cat, p_cat, p_cat, p_cat, p_cat, cwpack, gpar)


def _gdn_backward(p_cat, cwpack, gpar, states, d_out):
    t = p_cat.shape[0]
    n_chunks = t // CHUNK
    last = n_chunks - 1

    def body(xq, xk, xv, hq, hk, hv, z, ab, cw, gp, st, dy, dq, dk, dv, dz, dab, dcw, dgp, ds_scr, car_scr):
        n, grp = pl.program_id(0), pl.program_id(1)

        hsl = _head_range(grp, GDN_HB, GDN_HEADS)

        @pl.when(n == 0)
        def _():
            ds_scr[hsl] = jnp.zeros((GDN_HB, LANES, LANES), F32)
            car_scr[hsl] = jnp.zeros((GDN_HB, 3 * GDN_CONV, SUBLANES, LANES), F32)

        @pl.when((n == 0) & (grp == 0))
        def _():
            dcw[...] = jnp.zeros(dcw.shape, F32)
            dgp[...] = jnp.zeros(dgp.shape, F32)

        live = (n < last).astype(F32)
        oha, ohb = _onehots(grp)
        fn = functools.partial(_gdn_head, oha=oha, ohb=ohb)
        taps = _gdn_taps((xq, xk, xv), (hq, hk, hv), live)
        _, vjp = jax.vjp(fn, *taps, _stack_heads(z, GDN_HB), ab[...], st[...], cw[...], gp[...])
        g = vjp((_stack_heads(dy, GDN_HB), ds_scr[hsl]))
        outs = (dq, dk, dv)
        for i in range(GDN_HB):
            sl = slice(i * LANES, (i + 1) * LANES)
            h = _head_id(grp, i, GDN_HB, GDN_HEADS)
            for j in range(3):
                tot = g[j][0][i]
                for s in range(1, GDN_CONV):
                    slot = j * GDN_CONV + s
                    tot = tot + _unshift_rows(g[j][s][i], car_scr[h, slot], s)
                    car_scr[h, slot] = g[j][s][i][0:SUBLANES, :]
                outs[j][:, sl] = tot.astype(outs[j].dtype)
            dz[:, sl] = g[3][i].astype(dz.dtype)
        dab_sum = g[4]
        ds_scr[hsl] = g[5]
        dcw[hsl] += g[6]
        dgp[0] += g[7]

        @pl.when(grp == 0)
        def _():
            dab[...] = dab_sum

        @pl.when(grp > 0)
        def _():
            dab[...] += dab_sum

    rev = lambda n: last - n
    in_specs = _gdn_specs(rev) + [
        pl.BlockSpec((None, GDN_HB, LANES, LANES), lambda n, g: (rev(n), g, 0, 0)),
        pl.BlockSpec((CHUNK, GDN_HB * LANES), lambda n, g: (rev(n), g)),
    ]
    blk = pl.BlockSpec((CHUNK, GDN_HB * LANES), lambda n, g: (rev(n), g))
    return pl.pallas_call(
        body, name="gdn_bwd", grid=(n_chunks, GDN_HEADS // GDN_HB),
        in_specs=in_specs,
        out_specs=(blk, blk, blk, blk,
                   pl.BlockSpec((CHUNK, LANES), lambda n, h: (rev(n), 0)),
                   pl.BlockSpec((GDN_HEADS, 3, SUBLANES, LANES), lambda n, h: (0, 0, 0, 0)),
                   pl.BlockSpec((GDN_HEADS, SUBLANES, LANES), lambda n, h: (0, 0, 0))),
        out_shape=(jax.ShapeDtypeStruct((t, GDN_WIDTH), BF16), jax.ShapeDtypeStruct((t, GDN_WIDTH), BF16),
                   jax.ShapeDtypeStruct((t, GDN_WIDTH), BF16), jax.ShapeDtypeStruct((t, GDN_WIDTH), BF16),
                   jax.ShapeDtypeStruct((t, LANES), F32),
                   jax.ShapeDtypeStruct((GDN_HEADS, 3, SUBLANES, LANES), F32),
                   jax.ShapeDtypeStruct((GDN_HEADS, SUBLANES, LANES), F32)),
        scratch_shapes=[pltpu.VMEM((GDN_HEADS, LANES, LANES), F32),
                        pltpu.VMEM((GDN_HEADS, 3 * GDN_CONV, SUBLANES, LANES), F32)],
        compiler_params=_cparams(("arbitrary", "arbitrary")),
    )(p_cat, p_cat, p_cat, p_cat, p_cat, p_cat, p_cat, p_cat, cwpack, gpar, states, d_out)


def _pick(n, options):
    for o in options:
        if n % o == 0:
            return o
    raise ValueError(f"no tile for {n}")


_MM_DIMS = {'nn': (((1,), (0,)), ((), ())), 'nt': (((1,), (1,)), ((), ())), 'tn': (((0,), (0,)), ((), ()))}


def _matmul(a, b, mode, out_dtype, name):
    if mode == 'nn':
        (m, k), (k2, n) = a.shape, b.shape
    elif mode == 'nt':
        (m, k), (n, k2) = a.shape, b.shape
    else:
        (k, m), (k2, n) = a.shape, b.shape
    assert k == k2, (a.shape, b.shape, mode)
    tm = _pick(m, (1024, 512, 256, 128, 64))
    tn = _pick(n, (512, 256, 128))
    tk = _pick(k, (1024, 512, 256, 128, 64))
    nk = k // tk
    dims = _MM_DIMS[mode]

    def body(a_ref, b_ref, o_ref, acc_ref):
        kk = pl.program_id(2)
        part = lax.dot_general(a_ref[...], b_ref[...], dims, preferred_element_type=F32)

        @pl.when(kk == 0)
        def _():
            acc_ref[...] = part

        @pl.when(kk > 0)
        def _():
            acc_ref[...] += part

        @pl.when(kk == nk - 1)
        def _():
            o_ref[...] = acc_ref[...].astype(o_ref.dtype)

    a_spec = pl.BlockSpec((tk, tm), lambda i, j, kk: (kk, i)) if mode == 'tn' else pl.BlockSpec((tm, tk), lambda i, j, kk: (i, kk))
    b_spec = pl.BlockSpec((tn, tk), lambda i, j, kk: (j, kk)) if mode == 'nt' else pl.BlockSpec((tk, tn), lambda i, j, kk: (kk, j))
    return pl.pallas_call(
        body, name=name, grid=(m // tm, n // tn, nk),
        in_specs=[a_spec, b_spec],
        out_specs=pl.BlockSpec((tm, tn), lambda i, j, kk: (i, j)),
        out_shape=jax.ShapeDtypeStruct((m, n), out_dtype),
        scratch_shapes=[pltpu.VMEM((tm, tn), F32)],
        compiler_params=_cparams(("parallel", "parallel", "arbitrary")),
    )(a, b)


ROW_TILE = 256


def _row_specs(rows, tm):
    return [pl.BlockSpec((tm, w), lambda i, ci=ci: (i, ci)) for (_, w, ci) in rows]


def _rw_forward(fn, rows, pars, outs, name):
    t = rows[0][0].shape[0]
    tm = min(ROW_TILE, t)
    nr, npar = len(rows), len(pars)

    def body(*refs):
        vals = [r[...].astype(F32) for r in refs[:nr]] + [p[...] for p in refs[nr:nr + npar]]
        res = fn(*vals)
        for o, v in zip(refs[nr + npar:], res):
            o[...] = v.astype(o.dtype)

    return pl.pallas_call(
        body, name=name, grid=(t // tm,),
        in_specs=_row_specs(rows, tm) + [pl.BlockSpec(p.shape, lambda i: (0, 0)) for p in pars],
        out_specs=tuple(pl.BlockSpec((tm, w), lambda i: (i, 0)) for (w, _) in outs),
        out_shape=tuple(jax.ShapeDtypeStruct((t, w), dt) for (w, dt) in outs),
        compiler_params=_cparams(("parallel",)),
    )(*[r[0] for r in rows], *pars)


def _rw_backward(fn, rows, pars, cots, drow_dtypes, name):
    t = rows[0][0].shape[0]
    tm = min(ROW_TILE, t)
    nr, npar, nc = len(rows), len(pars), len(cots)
    keep = [i for i, dt in enumerate(drow_dtypes) if dt is not None]

    def body(*refs):
        vals = [r[...].astype(F32) for r in refs[:nr]] + [p[...] for p in refs[nr:nr + npar]]
        cvals = tuple(c[...].astype(F32) for c in refs[nr + npar:nr + npar + nc])
        orefs = refs[nr + npar + nc:]
        _, vjp = jax.vjp(fn, *vals)
        g = vjp(cvals)
        for o, i in zip(orefs[:len(keep)], keep):
            o[...] = g[i].astype(o.dtype)
        first = pl.program_id(0) == 0
        for o, gi in zip(orefs[len(keep):], g[nr:]):
            @pl.when(first)
            def _(o=o, gi=gi):
                o[...] = gi

            @pl.when(jnp.logical_not(first))
            def _(o=o, gi=gi):
                o[...] += gi

    out_specs = [pl.BlockSpec((tm, rows[i][1]), lambda i_: (i_, 0)) for i in keep] + \
                [pl.BlockSpec(p.shape, lambda i_: (0, 0)) for p in pars]
    out_shape = [jax.ShapeDtypeStruct((t, rows[i][1]), drow_dtypes[i]) for i in keep] + \
                [jax.ShapeDtypeStruct(p.shape, F32) for p in pars]
    return pl.pallas_call(
        body, name=name, grid=(t // tm,),
        in_specs=_row_specs(rows, tm) + [pl.BlockSpec(p.shape, lambda i: (0, 0)) for p in pars] + _row_specs(cots, tm),
        out_specs=tuple(out_specs), out_shape=tuple(out_shape),
        compiler_params=_cparams(("arbitrary",)),
    )(*[r[0] for r in rows], *pars, *[c[0] for c in cots])


def _norm_fn(x, g):
    return (_rms(x, g),)


def _norm_skip_fn(x, g):
    return _rms(x, g), x


def _merge_fn(ga, gb, ya, yb):
    return (_sigmoid(ga) * ya + _sigmoid(gb) * yb,)


def _res_norm_fn(x, mo, g):
    x1 = x + mo
    return x1, _rms(x1, g)


def _loss_head(x1, fo, gf, target, name):
    t, d = x1.shape
    tm = min(ROW_TILE, t)

    def tile_loss(x2, g, tgt):
        err = _rms(x2, g) - tgt
        per_row = jnp.sum(err * err, axis=-1, keepdims=True) * (0.5 / d)
        return jnp.sum(per_row, axis=0, keepdims=True)

    def body(x1_ref, fo_ref, g_ref, t_ref, loss_ref, dx_ref, dxb_ref, dg_ref):
        x2 = x1_ref[...] + fo_ref[...]
        val, vjp = jax.vjp(functools.partial(tile_loss, tgt=t_ref[...]), x2, g_ref[...])
        dx2, dg = vjp(jnp.ones((1, 1), F32))
        dx_ref[...] = dx2
        dxb_ref[...] = dx2.astype(BF16)
        first = pl.program_id(0) == 0

        @pl.when(first)
        def _():
            loss_ref[...] = jnp.broadcast_to(val, loss_ref.shape)
            dg_ref[...] = dg

        @pl.when(jnp.logical_not(first))
        def _():
            loss_ref[...] += jnp.broadcast_to(val, loss_ref.shape)
            dg_ref[...] += dg

    row = pl.BlockSpec((tm, d), lambda i: (i, 0))
    vec = pl.BlockSpec((1, d), lambda i: (0, 0))
    return pl.pallas_call(
        body, name=name, grid=(t // tm,),
        in_specs=[row, row, vec, row],
        out_specs=(pl.BlockSpec((1, LANES), lambda i: (0, 0)), row, row, vec),
        out_shape=(jax.ShapeDtypeStruct((1, LANES), F32), jax.ShapeDtypeStruct((t, d), F32),
                   jax.ShapeDtypeStruct((t, d), BF16), jax.ShapeDtypeStruct((1, d), F32)),
        compiler_params=_cparams(("arbitrary",)),
    )(x1, fo, gf, target)


FFN_TILE_ROWS = 512
FFN_TILE_COLS = 256
FFN_COL_BLOCKS = FFN_HIDDEN // FFN_TILE_COLS


def _conv3_past(x, halo, w):
    rows = lax.broadcasted_iota(jnp.int32, x.shape, 0)
    x1 = jnp.where(rows == 0, halo[7:8, :], pltpu.roll(x, 1, 0))
    x2 = jnp.where(rows == 0, halo[6:7, :], jnp.where(rows == 1, halo[7:8, :], pltpu.roll(x, 2, 0)))
    return x * w[2:3] + x1 * w[1:2] + x2 * w[0:1], x1, x2


def _ffn_in_specs(tm, imap, jmap):
    per = tm // SUBLANES
    tile = lambda off: pl.BlockSpec((tm, FFN_TILE_COLS), lambda *g: (imap(*g), off + jmap(*g) % FFN_COL_BLOCKS))
    halo = lambda off: pl.BlockSpec((SUBLANES, FFN_TILE_COLS),
                                    lambda *g: (jnp.maximum(imap(*g) * per - 1, 0), off + jmap(*g) % FFN_COL_BLOCKS))
    wsp = lambda off: pl.BlockSpec((FFN_CONV, FFN_TILE_COLS), lambda *g: (0, off + jmap(*g) % FFN_COL_BLOCKS))
    return [tile(0), halo(0), wsp(0), tile(FFN_COL_BLOCKS), halo(FFN_COL_BLOCKS), wsp(FFN_COL_BLOCKS)]


def _ffn_act_forward(hpre, cw):
    t = hpre.shape[0]
    tm = min(FFN_TILE_ROWS, t)

    def body(hg, pg, wg, hu, pu, wu, out):
        live = (pl.program_id(0) > 0).astype(F32)
        cg, _, _ = _conv3_past(hg[...], pg[...] * live, wg[...])
        cu, _, _ = _conv3_past(hu[...], pu[...] * live, wu[...])
        out[...] = (_silu(cg) * cu).astype(out.dtype)

    return pl.pallas_call(
        body, name="ffn_act_fwd", grid=(t // tm, FFN_COL_BLOCKS),
        in_specs=_ffn_in_specs(tm, lambda i, j: i, lambda i, j: j),
        out_specs=pl.BlockSpec((tm, FFN_TILE_COLS), lambda i, j: (i, j)),
        out_shape=jax.ShapeDtypeStruct((t, FFN_HIDDEN), BF16),
        compiler_params=_cparams(("parallel", "parallel")),
    )(hpre, hpre, cw, hpre, hpre, cw)


def _ffn_act_backward(hpre, cw, dact):
    t = hpre.shape[0]
    tm = min(FFN_TILE_ROWS, t)

    def body(hg, pg, wg, hu, pu, wu, da, dconv, dw):
        j, i = pl.program_id(0), pl.program_id(1)
        live = (i > 0).astype(F32)
        cg, g1, g2 = _conv3_past(hg[...], pg[...] * live, wg[...])
        cu, u1, u2 = _conv3_past(hu[...], pu[...] * live, wu[...])
        s = _sigmoid(cg)
        d = da[...]
        d_gate = d * cu * s * (1.0 + cg * (1.0 - s))
        d_up = d * cg * s
        is_gate = j < FFN_COL_BLOCKS
        dc = jnp.where(is_gate, d_gate, d_up)
        dconv[...] = dc
        taps = (jnp.where(is_gate, g2, u2), jnp.where(is_gate, g1, u1), jnp.where(is_gate, hg[...], hu[...]))
        sums = [jnp.sum(xs * dc, axis=0, keepdims=True) for xs in taps]

        @pl.when(i == 0)
        def _():
            for r_ in range(FFN_CONV):
                dw[r_:r_ + 1, :] = sums[r_]

        @pl.when(i > 0)
        def _():
            for r_ in range(FFN_CONV):
                dw[r_:r_ + 1, :] += sums[r_]

    return pl.pallas_call(
        body, name="ffn_act_bwd", grid=(2 * FFN_COL_BLOCKS, t // tm),
        in_specs=_ffn_in_specs(tm, lambda j, i: i, lambda j, i: j) +
        [pl.BlockSpec((tm, FFN_TILE_COLS), lambda j, i: (i, j % FFN_COL_BLOCKS))],
        out_specs=(pl.BlockSpec((tm, FFN_TILE_COLS), lambda j, i: (i, j)),
                   pl.BlockSpec((FFN_CONV, FFN_TILE_COLS), lambda j, i: (0, j))),
        out_shape=(jax.ShapeDtypeStruct((t, 2 * FFN_HIDDEN), F32), jax.ShapeDtypeStruct((FFN_CONV, 2 * FFN_HIDDEN), F32)),
        compiler_params=_cparams(("parallel", "arbitrary")),
    )(hpre, hpre, cw, hpre, hpre, cw, dact)


def _ffn_conv_backward(dconv, cw):
    t = dconv.shape[0]
    tm = min(FFN_TILE_ROWS, t)
    n_tiles = t // tm
    per = tm // SUBLANES

    def body(d_ref, nx_ref, w_ref, out):
        live = (pl.program_id(0) < n_tiles - 1).astype(F32)
        d = d_ref[...]
        nx = nx_ref[...] * live
        w = w_ref[...]
        rows = lax.broadcasted_iota(jnp.int32, d.shape, 0)
        d1 = jnp.where(rows == tm - 1, nx[0:1, :], pltpu.roll(d, tm - 1, 0))
        d2 = jnp.where(rows == tm - 1, nx[1:2, :], jnp.where(rows == tm - 2, nx[0:1, :], pltpu.roll(d, tm - 2, 0)))
        out[...] = (d * w[2:3] + d1 * w[1:2] + d2 * w[0:1]).astype(out.dtype)

    return pl.pallas_call(
        body, name="ffn_conv_bwd", grid=(n_tiles, 2 * FFN_COL_BLOCKS),
        in_specs=[pl.BlockSpec((tm, FFN_TILE_COLS), lambda i, j: (i, j)),
                  pl.BlockSpec((SUBLANES, FFN_TILE_COLS), lambda i, j: (jnp.minimum((i + 1) * per, t // SUBLANES - 1), j)),
                  pl.BlockSpec((FFN_CONV, FFN_TILE_COLS), lambda i, j: (0, j))],
        out_specs=pl.BlockSpec((tm, FFN_TILE_COLS), lambda i, j: (i, j)),
        out_shape=jax.ShapeDtypeStruct((t, 2 * FFN_HIDDEN), BF16),
        compiler_params=_cparams(("parallel", "parallel")),
    )(dconv, dconv, cw)


def _my_place():
    x, y, c = lax.axis_index("x"), lax.axis_index("y"), lax.axis_index("c")
    return x, y, c, 4 * x + 2 * y + c


def _peer(x, y, c, k):
    px, py, pc = x ^ ((k >> 2) & 1), y ^ ((k >> 1) & 1), c ^ (k & 1)
    return (px, py, pc), 4 * px + 2 * py + pc


def _all_gather(shard, name):
    r, w = shard.shape

    def body(src, out, send_sems, recv_sems, local_sem):
        x, y, c, me = _my_place()
        mine = pltpu.make_async_copy(src, out.at[me], local_sem)
        mine.start()
        sends, recvs = [], []
        for k in range(1, N_DEV):
            dev, idx = _peer(x, y, c, k)
            sends.append(pltpu.make_async_remote_copy(src_ref=src, dst_ref=out.at[me], send_sem=send_sems.at[k],
                                                      recv_sem=recv_sems.at[k], device_id=dev,
                                                      device_id_type=pl.DeviceIdType.MESH))
            recvs.append(pltpu.make_async_remote_copy(src_ref=src, dst_ref=out.at[idx], send_sem=send_sems.at[k],
                                                      recv_sem=recv_sems.at[k], device_id=dev,
                                                      device_id_type=pl.DeviceIdType.MESH))
        for cp in sends:
            cp.start()
        for cp in recvs:
            cp.wait_recv()
        for cp in sends:
            cp.wait_send()
        mine.wait()

    return pl.pallas_call(
        body, name=name,
        in_specs=[pl.BlockSpec(memory_space=pl.ANY)],
        out_specs=pl.BlockSpec(memory_space=pl.ANY),
        out_shape=jax.ShapeDtypeStruct((N_DEV, r, w), shard.dtype),
        scratch_shapes=[pltpu.SemaphoreType.DMA((N_DEV,)), pltpu.SemaphoreType.DMA((N_DEV,)), pltpu.SemaphoreType.DMA],
    )(shard)


def _exchange_slabs(slabs, name):
    def body(src, out, send_sems, recv_sems, local_sem):
        x, y, c, me = _my_place()
        mine = pltpu.make_async_copy(src.at[me], out.at[me], local_sem)
        mine.start()
        sends, recvs = [], []
        for k in range(1, N_DEV):
            dev, idx = _peer(x, y, c, k)
            sends.append(pltpu.make_async_remote_copy(src_ref=src.at[idx], dst_ref=out.at[me], send_sem=send_sems.at[k],
                                                      recv_sem=recv_sems.at[k], device_id=dev,
                                                      device_id_type=pl.DeviceIdType.MESH))
            recvs.append(pltpu.make_async_remote_copy(src_ref=src.at[idx], dst_ref=out.at[idx], send_sem=send_sems.at[k],
                                                      recv_sem=recv_sems.at[k], device_id=dev,
                                                      device_id_type=pl.DeviceIdType.MESH))
        for cp in sends:
            cp.start()
        for cp in recvs:
            cp.wait_recv()
        for cp in sends:
            cp.wait_send()
        mine.wait()

    return pl.pallas_call(
        body, name=name,
        in_specs=[pl.BlockSpec(memory_space=pl.ANY)],
        out_specs=pl.BlockSpec(memory_space=pl.ANY),
        out_shape=jax.ShapeDtypeStruct(slabs.shape, slabs.dtype),
        scratch_shapes=[pltpu.SemaphoreType.DMA((N_DEV,)), pltpu.SemaphoreType.DMA((N_DEV,)), pltpu.SemaphoreType.DMA],
    )(slabs)


def _reduce_adamw(parts, w, m, v, rows_per_step):
    _, r, wd = parts.shape
    c1 = 1.0 / (1.0 - ADAM_B1 ** ADAM_STEP)
    c2 = 1.0 / (1.0 - ADAM_B2 ** ADAM_STEP)

    def body(p_ref, w_ref, m_ref, v_ref, g_out, d_out, m_out, v_out):
        g = p_ref[0]
        for s in range(1, N_DEV):
            g = g + p_ref[s]
        mn = ADAM_B1 * m_ref[...] + (1.0 - ADAM_B1) * g
        vn = ADAM_B2 * v_ref[...] + (1.0 - ADAM_B2) * (g * g)
        g_out[...] = g
        m_out[...] = mn
        v_out[...] = vn
        d_out[...] = -ADAM_LR * ((mn * c1) / (jnp.sqrt(vn * c2) + ADAM_EPS) + ADAM_WD * w_ref[...])

    blk = pl.BlockSpec((rows_per_step, wd), lambda i: (i, 0))
    shp = jax.ShapeDtypeStruct((r, wd), F32)
    return pl.pallas_call(
        body, name="reduce_adamw", grid=(r // rows_per_step,),
        in_specs=[pl.BlockSpec((N_DEV, rows_per_step, wd), lambda i: (0, i, 0)), blk, blk, blk],
        out_specs=(blk, blk, blk, blk), out_shape=(shp, shp, shp, shp),
        compiler_params=_cparams(("parallel",)),
    )(parts, w, m, v)


PACK_W = 1024


def _pad_heads(a, slots):
    lead = a.shape[:-1]
    a = a.reshape(lead + (slots, RWKV_HEAD_DIM))
    a = jnp.pad(a, [(0, 0)] * (len(lead) + 1) + [(0, LANES - RWKV_HEAD_DIM)])
    return a.reshape(lead + (slots * LANES,))


def _unpad_heads(a, slots):
    lead = a.shape[:-1]
    return a.reshape(lead + (slots, LANES))[..., :RWKV_HEAD_DIM].reshape(lead + (slots * RWKV_HEAD_DIM,))


def _flat_pack(arrs, dtype, row_mult):
    flat = jnp.concatenate([a.reshape(-1).astype(dtype) for a in arrs])
    n = flat.shape[0]
    rows = -(-n // PACK_W)
    rows = -(-rows // row_mult) * row_mult
    return jnp.pad(flat, (0, rows * PACK_W - n)).reshape(rows, PACK_W)


def _unpack_gathered(g, names, shard_shapes):
    flat = g.reshape(N_DEV, -1)
    out, off = {}, 0
    for n in names:
        s = shard_shapes[n]
        size = s[0] * s[1]
        seg = flat[:, off:off + size].reshape((N_DEV,) + tuple(s))
        off += size
        if SHARD_AXIS[n] == 1:
            out[n] = jnp.transpose(seg, (1, 0, 2)).reshape(s[0], N_DEV * s[1])
        else:
            out[n] = seg.reshape(N_DEV * s[0], s[1])
    return out


def _shard_major(full, axis):
    a, b = full.shape
    if axis == 1:
        return jnp.transpose(full.reshape(a, N_DEV, b // N_DEV), (1, 0, 2)).reshape(N_DEV, -1)
    return full.reshape(N_DEV, -1)


def _prepare_weights(full, rep):
    w = full['w_in']
    d = w.shape[1]
    rkv = jnp.pad(w[0:1536].reshape(3 * RWKV_HEADS, RWKV_HEAD_DIM, d), ((0, 0), (0, LANES - RWKV_HEAD_DIM), (0, 0)))
    w_cat = jnp.concatenate([
        w[3848:4872], w[4872:5896], rkv.reshape(3 * RWKV_HEADS * LANES, d), w[1792:3328], w[3328:3840],
        w[1536:1792], jnp.pad(w[3840:3848], ((0, LANES - 8), (0, 0))), jnp.zeros((LANES, d), w.dtype)], axis=0)
    assert w_cat.shape[0] == CAT_W
    rp = full['rwkv_proj']
    rproj = jnp.pad(rp.reshape(RWKV_HEADS, RWKV_HEAD_DIM, -1), ((0, 0), (0, LANES - RWKV_HEAD_DIM), (0, 0))).reshape(RWKV_HEADS * LANES, -1)
    mu = rep['rwkv_mu']
    vecs = [mu[0:512], mu[512:1024], mu[1024:1536], rep['rwkv_w0'], rep['rwkv_a0'], rep['rwkv_k_k'], rep['rwkv_k_a'],
            rep['rwkv_ln_w'], rep['rwkv_ln_b'], rep['rwkv_r_k'].reshape(-1)]
    ppack = jnp.stack([jnp.pad(v.reshape(RWKV_HEADS, RWKV_HEAD_DIM), ((0, 0), (0, LANES - RWKV_HEAD_DIM))) for v in vecs], axis=1)
    ppack = jnp.pad(ppack, ((0, 0), (0, 16 - len(vecs)), (0, 0)))
    mulo = mu[1536:1792].reshape(1, 2 * LANES)
    wl = jnp.zeros((3, 2 * LANES, RWKV_HEADS * LANES), F32)
    wl = wl.at[0, 0:64].set(_pad_heads(full['rwkv_w2'], RWKV_HEADS))
    wl = wl.at[1, 64:128].set(_pad_heads(full['rwkv_a2'], RWKV_HEADS))
    wl = wl.at[2, 128:256].set(_pad_heads(full['rwkv_g2'], RWKV_HEADS))
    wl = jnp.transpose(wl.reshape(3, 2 * LANES, RWKV_HEADS, LANES), (2, 0, 1, 3))
    cw = full['gdn_conv_w'].reshape(GDN_CONV, 3, GDN_HEADS, LANES)
    cwpack = jnp.pad(jnp.transpose(cw, (2, 1, 0, 3)), ((0, 0), (0, 0), (0, SUBLANES - GDN_CONV), (0, 0)))
    gpar = jnp.zeros((SUBLANES, LANES), F32)
    gpar = gpar.at[0, 0:GDN_HEADS].set(rep['gdn_a_log']).at[1, 0:GDN_HEADS].set(rep['gdn_dt_bias']).at[2].set(rep['gdn_norm_w'])
    return dict(w_cat=w_cat, rproj=rproj, gproj=full['gdn_proj'], w_out=full['w_out'], ffn_up=full['ffn_up'],
                ffn_down=full['ffn_down'], ffn_cw=full['ffn_conv_w'], ppack=ppack, mulo=mulo, wl=wl, cwpack=cwpack, gpar=gpar,
                g1=rep['norm1_g'].reshape(1, -1), g2=rep['norm2_g'].reshape(1, -1), gf=rep['final_g'].reshape(1, -1))


def _local_step(x, target, p):
    d = x.shape[1]
    full_w = lambda a: (a, a.shape[1], 0)
    (u,) = _rw_forward(_norm_fn, [full_w(x)], [p['g1']], [(d, BF16)], "norm1")
    p_cat = _matmul(u, p['w_cat'], 'nt', F32, "proj_in")
    ya_pre, st_r = _rwkv_forward(p_cat, p['ppack'], p['mulo'], p['wl'])
    yb_pre, st_g = _gdn_forward(p_cat, p['cwpack'], p['gpar'])
    ya = _matmul(ya_pre, p['rproj'], 'nn', F32, "rwkv_proj")
    yb = _matmul(yb_pre, p['gproj'], 'nn', F32, "gdn_proj")
    gates = [(p_cat, d, OFF_GA // d), (p_cat, d, OFF_GB // d)]
    (mixed,) = _rw_forward(_merge_fn, gates + [full_w(ya), full_w(yb)], [], [(d, BF16)], "merge")
    mo = _matmul(mixed, p['w_out'], 'nn', F32, "out_proj")
    x1, n2 = _rw_forward(_res_norm_fn, [full_w(x), full_w(mo)], [p['g2']], [(d, F32), (d, BF16)], "res_norm2")
    hpre = _matmul(n2, p['ffn_up'], 'nt', F32, "ffn_up")
    act = _ffn_act_forward(hpre, p['ffn_cw'])
    fo = _matmul(act, p['ffn_down'], 'nn', F32, "ffn_down")
    loss_vec, dx2, dx2b, dgf = _loss_head(x1, fo, p['gf'], target, "loss_head")

    dact = _matmul(dx2b, p['ffn_down'], 'nt', F32, "d_act")
    dw_down = _matmul(act, dx2b, 'tn', F32, "dw_ffn_down")
    dconv, dcw_f = _ffn_act_backward(hpre, p['ffn_cw'], dact)
    dh = _ffn_conv_backward(dconv, p['ffn_cw'])
    dn2 = _matmul(dh, p['ffn_up'], 'nn', F32, "d_norm2")
    dw_up = _matmul(dh, n2, 'tn', F32, "dw_ffn_up")
    dx1, dx1b, dg2 = _rw_backward(_res_norm_fn, [full_w(x), full_w(mo)], [p['g2']], [full_w(dx2), full_w(dn2)],
                                  [F32, BF16], "res_norm2_bwd")
    dmixed = _matmul(dx1b, p['w_out'], 'nt', F32, "d_mixed")
    dw_out = _matmul(mixed, dx1b, 'tn', F32, "dw_out")
    dga, dgb, dya, dyb = _rw_backward(_merge_fn, gates + [full_w(ya), full_w(yb)], [], [full_w(dmixed)],
                                      [BF16, BF16, BF16, BF16], "merge_bwd")
    d_ya_pre = _matmul(dya, p['rproj'], 'nt', F32, "d_rwkv_out")
    dw_rproj = _matmul(ya_pre, dya, 'tn', F32, "dw_rwkv_proj")
    d_yb_pre = _matmul(dyb, p['gproj'], 'nt', F32, "d_gdn_out")
    dw_gproj = _matmul(yb_pre, dyb, 'tn', F32, "dw_gdn_proj")
    dpr, dpk, dpv, dplo, dpp, dml, dwl = _rwkv_backward(p_cat, p['ppack'], p['mulo'], p['wl'], st_r, d_ya_pre)
    dq, dk, dv, dz, dab, dcw_g, dgp = _gdn_backward(p_cat, p['cwpack'], p['gpar'], st_g, d_yb_pre)
    t = x.shape[0]
    dp_cat = jnp.concatenate([dga, dgb, dpr, dpk, dpv, dq, dk, dv, dz, dplo.astype(BF16), dab.astype(BF16),
                              jnp.zeros((t, LANES), BF16)], axis=1)
    du = _matmul(dp_cat, p['w_cat'], 'nn', F32, "d_norm1")
    dw_cat = _matmul(dp_cat, u, 'tn', F32, "dw_in")
    grad_x, dg1 = _rw_backward(_norm_skip_fn, [full_w(x)], [p['g1']], [full_w(du), full_w(dx1)], [F32], "norm1_bwd")

    heads = lambda row: dpp[:, row, :RWKV_HEAD_DIM].reshape(-1)
    lora = lambda j, lo_, hi_: jnp.transpose(dwl[:, j, lo_:hi_, :RWKV_HEAD_DIM], (1, 0, 2)).reshape(hi_ - lo_, RWKV_WIDTH)
    grads = {
        'norm1_g': dg1[0],
        'w_in': jnp.concatenate([dw_cat[OFF_RKV:OFF_QKV].reshape(3 * RWKV_HEADS, LANES, d)[:, :RWKV_HEAD_DIM].reshape(-1, d),
                                 dw_cat[OFF_LO:OFF_AB], dw_cat[OFF_QKV:OFF_Z], dw_cat[OFF_Z:OFF_LO], dw_cat[OFF_AB:OFF_AB + 8],
                                 dw_cat[OFF_GA:OFF_GB], dw_cat[OFF_GB:OFF_RKV]], axis=0),
        'rwkv_mu': jnp.concatenate([heads(0), heads(1), heads(2), jnp.sum(dml[:, 0, :], axis=0)]),
        'rwkv_w0': heads(3), 'rwkv_a0': heads(4), 'rwkv_k_k': heads(5), 'rwkv_k_a': heads(6),
        'rwkv_ln_w': heads(7), 'rwkv_ln_b': heads(8), 'rwkv_r_k': heads(9).reshape(RWKV_HEADS, RWKV_HEAD_DIM),
        'rwkv_w2': lora(0, 0, 64), 'rwkv_a2': lora(1, 64, 128), 'rwkv_g2': lora(2, 128, 256),
        'rwkv_proj': dw_rproj.reshape(RWKV_HEADS, LANES, -1)[:, :RWKV_HEAD_DIM].reshape(RWKV_WIDTH, -1),
        'gdn_conv_w': jnp.transpose(dcw_g[:, :, :GDN_CONV, :], (2, 1, 0, 3)).reshape(GDN_CONV, 3 * GDN_WIDTH),
        'gdn_a_log': jnp.sum(dgp[:, 0, :GDN_HEADS], axis=0), 'gdn_dt_bias': jnp.sum(dgp[:, 1, :GDN_HEADS], axis=0),
        'gdn_norm_w': jnp.sum(dgp[:, 2, :], axis=0),
        'gdn_proj': dw_gproj, 'w_out': dw_out, 'norm2_g': dg2[0], 'ffn_up': dw_up, 'ffn_conv_w': dcw_f,
        'ffn_down': dw_down, 'final_g': dgf[0],
    }
    return loss_vec, grad_x, grads


def kernel(x, norm1_g, w_in, rwkv_mu, rwkv_w0, rwkv_w2, rwkv_a0, rwkv_a2, rwkv_g2, rwkv_k_k, rwkv_k_a, rwkv_r_k, rwkv_ln_w, rwkv_ln_b, rwkv_proj, gdn_conv_w, gdn_a_log, gdn_dt_bias, gdn_norm_w, gdn_proj, w_out, norm2_g, ffn_up, ffn_conv_w, ffn_down, final_g, loss_target, m_norm1_g, m_w_in, m_rwkv_mu, m_rwkv_w0, m_rwkv_w2, m_rwkv_a0, m_rwkv_a2, m_rwkv_g2, m_rwkv_k_k, m_rwkv_k_a, m_rwkv_r_k, m_rwkv_ln_w, m_rwkv_ln_b, m_rwkv_proj, m_gdn_conv_w, m_gdn_a_log, m_gdn_dt_bias, m_gdn_norm_w, m_gdn_proj, m_w_out, m_norm2_g, m_ffn_up, m_ffn_conv_w, m_ffn_down, m_final_g, v_norm1_g, v_w_in, v_rwkv_mu, v_rwkv_w0, v_rwkv_w2, v_rwkv_a0, v_rwkv_a2, v_rwkv_g2, v_rwkv_k_k, v_rwkv_k_a, v_rwkv_r_k, v_rwkv_ln_w, v_rwkv_ln_b, v_rwkv_proj, v_gdn_conv_w, v_gdn_a_log, v_gdn_dt_bias, v_gdn_norm_w, v_gdn_proj, v_w_out, v_norm2_g, v_ffn_up, v_ffn_conv_w, v_ffn_down, v_final_g):
    given = dict(zip(WEIGHT_NAMES, (norm1_g, w_in, rwkv_mu, rwkv_w0, rwkv_w2, rwkv_a0, rwkv_a2, rwkv_g2, rwkv_k_k, rwkv_k_a, rwkv_r_k,
                                    rwkv_ln_w, rwkv_ln_b, rwkv_proj, gdn_conv_w, gdn_a_log, gdn_dt_bias, gdn_norm_w, gdn_proj, w_out,
                                    norm2_g, ffn_up, ffn_conv_w, ffn_down, final_g)))
    mom1 = dict(zip(WEIGHT_NAMES, (m_norm1_g, m_w_in, m_rwkv_mu, m_rwkv_w0, m_rwkv_w2, m_rwkv_a0, m_rwkv_a2, m_rwkv_g2, m_rwkv_k_k,
                                   m_rwkv_k_a, m_rwkv_r_k, m_rwkv_ln_w, m_rwkv_ln_b, m_rwkv_proj, m_gdn_conv_w, m_gdn_a_log,
                                   m_gdn_dt_bias, m_gdn_norm_w, m_gdn_proj, m_w_out, m_norm2_g, m_ffn_up, m_ffn_conv_w, m_ffn_down,
                                   m_final_g)))
    mom2 = dict(zip(WEIGHT_NAMES, (v_norm1_g, v_w_in, v_rwkv_mu, v_rwkv_w0, v_rwkv_w2, v_rwkv_a0, v_rwkv_a2, v_rwkv_g2, v_rwkv_k_k,
                                   v_rwkv_k_a, v_rwkv_r_k, v_rwkv_ln_w, v_rwkv_ln_b, v_rwkv_proj, v_gdn_conv_w, v_gdn_a_log,
                                   v_gdn_dt_bias, v_gdn_norm_w, v_gdn_proj, v_w_out, v_norm2_g, v_ffn_up, v_ffn_conv_w, v_ffn_down,
                                   v_final_g)))
    def strip(n, a):
        a = a if n == 'final_g' else a.reshape(a.shape[1:])
        return a.T if n in TRANSPOSED else a

    local = {n: strip(n, a) for n, a in given.items()}
    shard_shapes = {n: local[n].shape for n in SHARD_AXIS}
    sharded = BIG_SHARDED + SMALL_SHARDED

    g_big = _all_gather(_flat_pack([local[n] for n in BIG_SHARDED], BF16, 16), "gather_big")
    g_small = _all_gather(_flat_pack([local[n] for n in SMALL_SHARDED], F32, SUBLANES), "gather_small")
    full = _unpack_gathered(g_big, BIG_SHARDED, shard_shapes)
    full.update(_unpack_gathered(g_small, SMALL_SHARDED, shard_shapes))
    rep = {n: local[n] for n in REPLICATED}

    loss_vec, grad_x, grads = _local_step(x[0], loss_target[0], _prepare_weights(full, rep))

    rep_vec = jnp.concatenate([grads[n].reshape(-1) for n in REPLICATED] + [loss_vec[0, 0:1]])
    slab = jnp.concatenate([_shard_major(grads[n], SHARD_AXIS[n]) for n in sharded] +
                           [jnp.broadcast_to(rep_vec[None], (N_DEV, rep_vec.shape[0]))], axis=1)
    n_elem = slab.shape[1]
    rows_per_step = 208
    rows = -(-n_elem // (PACK_W * rows_per_step)) * rows_per_step
    slab = jnp.pad(slab, ((0, 0), (0, rows * PACK_W - n_elem))).reshape(N_DEV, rows, PACK_W)
    parts = _exchange_slabs(slab, "grad_exchange")

    def pack_local(src):
        flat = jnp.concatenate([strip(n, src[n]).reshape(-1) for n in sharded + REPLICATED])
        return jnp.pad(flat, (0, rows * PACK_W - flat.shape[0])).reshape(rows, PACK_W)

    packs = _reduce_adamw(parts, pack_local(given), pack_local(mom1), pack_local(mom2), rows_per_step)

    def unpack(pk):
        flat = pk.reshape(-1)
        out, off = {}, 0
        for n in sharded + REPLICATED:
            size = int(np.prod(given[n].shape))
            seg = flat[off:off + size].reshape(local[n].shape)
            out[n] = (seg.T if n in TRANSPOSED else seg).reshape(given[n].shape)
            off += size
        return out, flat[off]

    (g_out, loss), (d_out, _), (m_out, _), (v_out, _) = (unpack(pk) for pk in packs)
    return (loss, grad_x[None], *[g_out[n] for n in WEIGHT_NAMES], *[d_out[n] for n in WEIGHT_NAMES],
            *[m_out[n] for n in WEIGHT_NAMES], *[v_out[n] for n in WEIGHT_NAMES])
```

```python
import functools

import jax
import jax.numpy as jnp
import numpy as np
from jax import lax
from jax.experimental import pallas as pl
from jax.experimental.pallas import tpu as pltpu

F32 = jnp.float32
BF16 = jnp.bfloat16
HI = lax.Precision.HIGHEST

N_DEV = 8
D_MODEL = 1024
CHUNK = 64
RWKV_HEADS = 8
RWKV_HEAD_DIM = 64
RWKV_WIDTH = 512
GDN_HEADS = 4
GDN_HEAD_DIM = 128
GDN_WIDTH = 512
GDN_CONV = 4
FFN_HIDDEN = 2816
FFN_CONV = 3
NORM_EPS = 1e-6
L2_EPS = 1e-6
RWKV_GN_EPS = 64e-5
LANES = 128
SUBLANES = 8
VMEM_LIMIT = 56 * 1024 * 1024

ADAM_LR = 0.001
ADAM_B1 = 0.9
ADAM_B2 = 0.999
ADAM_EPS = 1e-08
ADAM_WD = 0.01
ADAM_STEP = 10

OFF_GA, OFF_GB, OFF_RKV, OFF_QKV, OFF_Z, OFF_LO, OFF_AB, CAT_W = 0, 1024, 2048, 5120, 6656, 7168, 7424, 7680
RWKV_HB = 8
GDN_HB = 4

WEIGHT_NAMES = ['norm1_g', 'w_in', 'rwkv_mu', 'rwkv_w0', 'rwkv_w2', 'rwkv_a0', 'rwkv_a2', 'rwkv_g2', 'rwkv_k_k', 'rwkv_k_a',
                'rwkv_r_k', 'rwkv_ln_w', 'rwkv_ln_b', 'rwkv_proj', 'gdn_conv_w', 'gdn_a_log', 'gdn_dt_bias', 'gdn_norm_w',
                'gdn_proj', 'w_out', 'norm2_g', 'ffn_up', 'ffn_conv_w', 'ffn_down', 'final_g']
BIG_SHARDED = ['w_in', 'ffn_up', 'ffn_down', 'w_out', 'rwkv_proj', 'gdn_proj']
SMALL_SHARDED = ['rwkv_w2', 'rwkv_a2', 'rwkv_g2', 'gdn_conv_w', 'ffn_conv_w']
TRANSPOSED = ('w_in', 'ffn_up')
SHARD_AXIS = {'w_in': 0, 'ffn_up': 0, 'ffn_down': 0, 'w_out': 0, 'rwkv_proj': 1, 'gdn_proj': 1,
              'rwkv_w2': 1, 'rwkv_a2': 1, 'rwkv_g2': 1, 'gdn_conv_w': 1, 'ffn_conv_w': 1}
REPLICATED = [n for n in WEIGHT_NAMES if n not in SHARD_AXIS]


def _cparams(sem=None):
    kw = dict(vmem_limit_bytes=VMEM_LIMIT)
    if sem is not None:
        kw['dimension_semantics'] = sem
    return pltpu.CompilerParams(**kw)


_NN, _NT, _TN = 'nn', 'nt', 'tn'
_DIMS_2D = {'nn': (((1,), (0,)), ((), ())), 'nt': (((1,), (1,)), ((), ())), 'tn': (((0,), (0,)), ((), ()))}
_DIMS_3D = {'nn': (((2,), (1,)), ((0,), (0,))), 'nt': (((2,), (2,)), ((0,), (0,))), 'tn': (((1,), (1,)), ((0,), (0,)))}


def _dg(a, b, kind):
    return lax.dot_general(a, b, (_DIMS_2D if a.ndim == 2 else _DIMS_3D)[kind], preferred_element_type=F32)


def _dot1(a, b, kind):
    return _dg(a.astype(BF16), b.astype(BF16), kind)


@jax.custom_vjp
def _dhi(a, b):
    return _dot1(a, b, _NN)


_dhi.defvjp(lambda a, b: (_dot1(a, b, _NN), (a, b)),
            lambda res, ct: (_dot1(ct, res[1], _NT), _dot1(res[0], ct, _TN)))


@jax.custom_vjp
def _dnt(a, b):
    return _dot1(a, b, _NT)


_dnt.defvjp(lambda a, b: (_dot1(a, b, _NT), (a, b)),
            lambda res, ct: (_dot1(ct, res[1], _NN), _dot1(ct, res[0], _TN)))


@jax.custom_vjp
def _dtn(a, b):
    return _dot1(a, b, _TN)


_dtn.defvjp(lambda a, b: (_dot1(a, b, _TN), (a, b)),
            lambda res, ct: (_dot1(res[1], ct, _NT), _dot1(res[0], ct, _NN)))


def _split3(x):
    x1 = x.astype(BF16)
    r1 = x - x1.astype(F32)
    x2 = r1.astype(BF16)
    return x1, x2, (r1 - x2.astype(F32)).astype(BF16)


def _dot_exact_lhs(sel, x, kind):
    parts = [_dg(sel, xi, kind) for xi in _split3(x)]
    return parts[0] + parts[1] + parts[2]


def _tril_ones(like):
    c = like.shape[-2]
    ri, ci = _iotas(c)
    return jnp.broadcast_to((ri >= ci).astype(BF16), like.shape[:-2] + (c, c))


@jax.custom_vjp
def _cumsum_rows(x):
    return _dot_exact_lhs(_tril_ones(x), x, _NN)


_cumsum_rows.defvjp(lambda x: (_dot_exact_lhs(_tril_ones(x), x, _NN), None),
                    lambda _, ct: (_dot_exact_lhs(_tril_ones(ct), ct, _TN),))


@jax.custom_vjp
def _lane_sum_as_row(x):
    return _dot_exact_lhs(jnp.ones(x.shape, BF16), x, _NT)


def _lane_sum_as_row_bwd(_, ct):
    ones = jnp.ones(ct.shape[:-1] + (LANES,), BF16)
    parts = [_dg(ci, ones, _TN) for ci in _split3(ct)]
    return (parts[0] + parts[1] + parts[2],)


_lane_sum_as_row.defvjp(lambda x: (_dot_exact_lhs(jnp.ones(x.shape, BF16), x, _NT), None), _lane_sum_as_row_bwd)


def _shift_rows(x, halo, s):
    rows = lax.broadcasted_iota(jnp.int32, x.shape, 0)
    out = pltpu.roll(x, s, 0)
    for i in range(s):
        out = jnp.where(rows == i, halo[SUBLANES - s + i:SUBLANES - s + i + 1, :], out)
    return out


def _unshift_rows(g, carry, s):
    c = g.shape[0]
    rows = lax.broadcasted_iota(jnp.int32, g.shape, 0)
    out = pltpu.roll(g, c - s, 0)
    for i in range(s):
        out = jnp.where(rows == c - s + i, carry[i:i + 1, :], out)
    return out


def _sigmoid(z):
    return 1.0 / (1.0 + jnp.exp(-z))


def _silu(z):
    return z * _sigmoid(z)


def _softplus(z):
    return jnp.maximum(z, 0.0) + jnp.log(1.0 + jnp.exp(-jnp.abs(z)))


def _rms(t, gain):
    return t * lax.rsqrt(jnp.mean(t * t, axis=-1, keepdims=True) + NORM_EPS) * gain


def _iotas(c):
    return lax.broadcasted_iota(jnp.int32, (c, c), 0), lax.broadcasted_iota(jnp.int32, (c, c), 1)


def _unit_lower_inverse(xm, eye):
    t = eye + xm
    p = xm
    for _ in range(5):
        p = _dhi(p, p)
        t = t + _dhi(t, p)
    return t


def _rwkv_head(pr, pk, pv, plo, qr, qk, qv, qlo, s0, pp, mulo, wl):
    c = pr.shape[1]
    ri, ci = _iotas(c)

    def mix(p, q, mu):
        return p + (q - p) * mu

    r = mix(pr, qr, pp[:, 0:1])
    k = mix(pk, qk, pp[:, 1:2])
    v = mix(pv, qv, pp[:, 2:3])
    lo = mix(plo, qlo, mulo)
    w0, a0, k_k, k_a, ln_w, ln_b, r_k = (pp[:, i:i + 1] for i in range(3, 10))
    per_head = lambda t: jnp.broadcast_to(t, (pr.shape[0],) + t.shape)
    zw = _dhi(per_head(jnp.tanh(lo)), wl[:, 0])
    za = _dhi(per_head(lo), wl[:, 1])
    g = _dhi(per_head(_sigmoid(lo)), wl[:, 2])
    w_log = -_softplus(-(w0 + zw)) - 0.5
    lw = -jnp.exp(w_log)
    a = _sigmoid(a0 + za)
    kk = k * k_k
    kk = kk * lax.rsqrt(jnp.sum(kk * kk, axis=-1, keepdims=True) + L2_EPS)
    k2 = k * (1.0 + (a - 1.0) * k_a)
    an = -kk
    b = kk * a
    causal = ri >= ci
    strict = ri > ci
    eye = (ri == ci).astype(F32)
    cl = _cumsum_rows(lw)
    ecl = jnp.exp(-cl)
    at = an * jnp.exp(cl - lw)
    bt = b * ecl
    kt = k2 * ecl
    rt = r * jnp.exp(cl)
    a_ab = jnp.where(strict, _dnt(at, bt), 0.0)
    a_ak = jnp.where(strict, _dnt(at, kt), 0.0)
    tinv = _unit_lower_inverse(a_ab, eye)
    u = _dhi(tinv, _dnt(at, s0) + _dhi(a_ak, v))
    y = _dnt(rt, s0) + _dhi(jnp.where(causal, _dnt(rt, bt), 0.0), u) + _dhi(jnp.where(causal, _dnt(rt, kt), 0.0), v)
    cl_end = jnp.sum(lw, axis=1, keepdims=True)
    dec_end = jnp.exp(cl_end - cl)
    s1 = s0 * jnp.exp(cl_end) + _dtn(u, b * dec_end) + _dtn(v, k2 * dec_end)
    m = (lax.broadcasted_iota(jnp.int32, (1, LANES), 1) < RWKV_HEAD_DIM).astype(F32)
    mean = jnp.sum(y, axis=-1, keepdims=True) * (1.0 / RWKV_HEAD_DIM)
    yc = (y - mean) * m
    var = jnp.sum(yc * yc, axis=-1, keepdims=True) * (1.0 / RWKV_HEAD_DIM)
    yn = yc * lax.rsqrt(var + RWKV_GN_EPS) * ln_w + ln_b
    y2 = yn + jnp.sum(r * k2 * r_k, axis=-1, keepdims=True) * v
    return y2 * g, s1


def _gdn_head(xq, xk, xv, z, ab, s0, cw, gp, oha, ohb):
    c = z.shape[1]
    ri, ci = _iotas(c)

    def conv(xs, w):
        out = xs[0] * w[:, GDN_CONV - 1:GDN_CONV]
        for s in range(1, GDN_CONV):
            out = out + xs[s] * w[:, GDN_CONV - 1 - s:GDN_CONV - s]
        return out

    q = _silu(conv(xq, cw[:, 0]))
    k = _silu(conv(xk, cw[:, 1]))
    v = _silu(conv(xv, cw[:, 2]))
    q = q * lax.rsqrt(jnp.sum(q * q, axis=-1, keepdims=True) + L2_EPS) * (GDN_HEAD_DIM ** -0.5)
    k = k * lax.rsqrt(jnp.sum(k * k, axis=-1, keepdims=True) + L2_EPS)
    gg = -jnp.exp(gp[0:1]) * _softplus(ab + gp[1:2])
    beta = jnp.sum(_sigmoid(ab) * ohb, axis=-1, keepdims=True)
    causal = ri >= ci
    strict = ri > ci
    eye = (ri == ci).astype(F32)
    gcm = _cumsum_rows(gg * oha)
    gc = jnp.sum(gcm, axis=-1, keepdims=True)
    gc_row = _lane_sum_as_row(gcm)
    dec = jnp.where(causal, jnp.exp(jnp.where(causal, gc - gc_row, 0.0)), 0.0)
    kb = k * beta
    vb = v * beta
    lm = jnp.where(strict, _dnt(kb, k) * dec, 0.0)
    tinv = _unit_lower_inverse(-lm, eye)
    egc = jnp.exp(gc)
    u = _dhi(tinv, vb)
    wk = _dhi(tinv, kb * egc)
    attn = jnp.where(causal, _dnt(q, k) * dec, 0.0)
    g_last = gc[:, c - 1:c, :]
    v_new = u - _dhi(wk, s0)
    o = _dhi(q * egc, s0) + _dhi(attn, v_new)
    s1 = s0 * jnp.exp(g_last) + _dtn(k * jnp.exp(g_last - gc), v_new)
    return _rms(o, gp[2:3]) * _silu(z), s1


def _head_id(grp, i, per_step, heads):
    return i if per_step == heads else grp * per_step + i


def _head_range(grp, per_step, heads):
    return slice(None) if per_step == heads else pl.ds(grp * per_step, per_step)


def _stack_heads(ref, n, fn=None):
    parts = []
    for i in range(n):
        sl = slice(i * LANES, (i + 1) * LANES)
        v = ref[:, sl]
        parts.append(v if fn is None else fn(v, sl))
    return jnp.stack(parts)


def _prev_rows_spec(width, col):
    per = CHUNK // SUBLANES
    return pl.BlockSpec((SUBLANES, width), lambda n, h: (jnp.maximum(n * per - 1, 0), col(h)))


def _rwkv_specs(nmap):
    hb, groups = RWKV_HB, RWKV_HEADS // RWKV_HB
    cb = OFF_RKV // (hb * LANES)
    specs = []
    for j in range(3):
        specs.append(pl.BlockSpec((CHUNK, hb * LANES), lambda n, g, j=j: (nmap(n), cb + j * groups + g)))
    specs.append(pl.BlockSpec((CHUNK, 2 * LANES), lambda n, g: (nmap(n), OFF_LO // (2 * LANES))))
    per = CHUNK // SUBLANES
    for j in range(3):
        specs.append(pl.BlockSpec((SUBLANES, hb * LANES),
                                  lambda n, g, j=j: (jnp.maximum(nmap(n) * per - 1, 0), cb + j * groups + g)))
    specs.append(pl.BlockSpec((SUBLANES, 2 * LANES), lambda n, g: (jnp.maximum(nmap(n) * per - 1, 0), OFF_LO // (2 * LANES))))
    specs.append(pl.BlockSpec((hb, 16, LANES), lambda n, g: (g, 0, 0)))
    specs.append(pl.BlockSpec((1, 2 * LANES), lambda n, g: (0, 0)))
    specs.append(pl.BlockSpec((hb, 3, 2 * LANES, LANES), lambda n, g: (g, 0, 0, 0)))
    return specs


def _rwkv_forward(p_cat, ppack, mulo, wl):
    t = p_cat.shape[0]
    n_chunks = t // CHUNK

    def body(pr, pk, pv, plo, hr, hk, hv, hlo, pp, ml, w, out, st_out, s_scr):
        n, grp = pl.program_id(0), pl.program_id(1)

        hsl = _head_range(grp, RWKV_HB, RWKV_HEADS)

        @pl.when(n == 0)
        def _():
            s_scr[hsl] = jnp.zeros((RWKV_HB, LANES, LANES), F32)

        live = (n > 0).astype(F32)
        lo = plo[...]
        lo_prev = _shift_rows(lo, hlo[...] * live, 1)
        cur = [_stack_heads(x, RWKV_HB) for x in (pr, pk, pv)]
        prev = [_stack_heads(x, RWKV_HB, lambda v, sl, hx=hx: _shift_rows(v, hx[:, sl] * live, 1))
                for x, hx in ((pr, hr), (pk, hk), (pv, hv))]
        s0 = s_scr[hsl]
        st_out[...] = s0
        o, s1 = _rwkv_head(*cur, lo, *prev, lo_prev, s0, pp[...], ml[...], w[...])
        for i in range(RWKV_HB):
            out[:, i * LANES:(i + 1) * LANES] = o[i].astype(out.dtype)
        s_scr[hsl] = s1

    return pl.pallas_call(
        body, name="rwkv_fwd", grid=(n_chunks, RWKV_HEADS // RWKV_HB),
        in_specs=_rwkv_specs(lambda n: n),
        out_specs=(pl.BlockSpec((CHUNK, RWKV_HB * LANES), lambda n, g: (n, g)),
                   pl.BlockSpec((None, RWKV_HB, LANES, LANES), lambda n, g: (n, g, 0, 0))),
        out_shape=(jax.ShapeDtypeStruct((t, RWKV_HEADS * LANES), BF16),
                   jax.ShapeDtypeStruct((n_chunks, RWKV_HEADS, LANES, LANES), F32)),
        scratch_shapes=[pltpu.VMEM((RWKV_HEADS, LANES, LANES), F32)],
        compiler_params=_cparams(("arbitrary", "arbitrary")),
    )(p_cat, p_cat, p_cat, p_cat, p_cat, p_cat, p_cat, p_cat, ppack, mulo, wl)


def _rwkv_backward(p_cat, ppack, mulo, wl, states, d_out):
    t = p_cat.shape[0]
    n_chunks = t // CHUNK
    last = n_chunks - 1

    def body(pr, pk, pv, plo, hr, hk, hv, hlo, pp, ml, w, st, dy, dpr, dpk, dpv, dplo, dpp, dml, dw, ds_scr, car_scr, carlo_scr):
        n, grp = pl.program_id(0), pl.program_id(1)

        hsl = _head_range(grp, RWKV_HB, RWKV_HEADS)
        gi = _head_id(grp, 0, 1, RWKV_HEADS // RWKV_HB)

        @pl.when(n == 0)
        def _():
            ds_scr[hsl] = jnp.zeros((RWKV_HB, LANES, LANES), F32)
            car_scr[hsl] = jnp.zeros((RWKV_HB, 3 * SUBLANES, LANES), F32)
            carlo_scr[gi] = jnp.zeros((SUBLANES, 2 * LANES), F32)

        @pl.when((n == 0) & (grp == 0))
        def _():
            dpp[...] = jnp.zeros(dpp.shape, F32)
            dml[...] = jnp.zeros(dml.shape, F32)
            dw[...] = jnp.zeros(dw.shape, F32)

        live = (n < last).astype(F32)
        lo = plo[...]
        lo_prev = _shift_rows(lo, hlo[...] * live, 1)
        cur = [_stack_heads(x, RWKV_HB) for x in (pr, pk, pv)]
        prev = [_stack_heads(x, RWKV_HB, lambda v, sl, hx=hx: _shift_rows(v, hx[:, sl] * live, 1))
                for x, hx in ((pr, hr), (pk, hk), (pv, hv))]
        _, vjp = jax.vjp(_rwkv_head, *cur, lo, *prev, lo_prev, st[...], pp[...], ml[...], w[...])
        g = vjp((_stack_heads(dy, RWKV_HB), ds_scr[hsl]))
        outs = (dpr, dpk, dpv)
        for i in range(RWKV_HB):
            sl = slice(i * LANES, (i + 1) * LANES)
            h = _head_id(grp, i, RWKV_HB, RWKV_HEADS)
            car = car_scr[h]
            for j in range(3):
                tot = g[j][i] + _unshift_rows(g[4 + j][i], car[SUBLANES * j:SUBLANES * (j + 1), :], 1)
                outs[j][:, sl] = tot.astype(outs[j].dtype)
                car_scr[h, SUBLANES * j:SUBLANES * (j + 1), :] = g[4 + j][i][0:SUBLANES, :]
        dlo = g[3] + _unshift_rows(g[7], carlo_scr[gi], 1)
        carlo_scr[gi] = g[7][0:SUBLANES, :]
        ds_scr[hsl] = g[8]
        dpp[hsl] += g[9]
        dml[0, 0:1, :] += g[10]
        dw[hsl] += g[11]

        @pl.when(grp == 0)
        def _():
            dplo[...] = dlo

        @pl.when(grp > 0)
        def _():
            dplo[...] += dlo

    rev = lambda n: last - n
    in_specs = _rwkv_specs(rev) + [
        pl.BlockSpec((None, RWKV_HB, LANES, LANES), lambda n, g: (rev(n), g, 0, 0)),
        pl.BlockSpec((CHUNK, RWKV_HB * LANES), lambda n, g: (rev(n), g)),
    ]
    hw = RWKV_HEADS * LANES
    return pl.pallas_call(
        body, name="rwkv_bwd", grid=(n_chunks, RWKV_HEADS // RWKV_HB),
        in_specs=in_specs,
        out_specs=(pl.BlockSpec((CHUNK, RWKV_HB * LANES), lambda n, g: (rev(n), g)),
                   pl.BlockSpec((CHUNK, RWKV_HB * LANES), lambda n, g: (rev(n), g)),
                   pl.BlockSpec((CHUNK, RWKV_HB * LANES), lambda n, g: (rev(n), g)),
                   pl.BlockSpec((CHUNK, 2 * LANES), lambda n, h: (rev(n), 0)),
                   pl.BlockSpec((RWKV_HEADS, 16, LANES), lambda n, h: (0, 0, 0)),
                   pl.BlockSpec((RWKV_HEADS, SUBLANES, 2 * LANES), lambda n, h: (0, 0, 0)),
                   pl.BlockSpec((RWKV_HEADS, 3, 2 * LANES, LANES), lambda n, h: (0, 0, 0, 0))),
        out_shape=(jax.ShapeDtypeStruct((t, hw), BF16), jax.ShapeDtypeStruct((t, hw), BF16), jax.ShapeDtypeStruct((t, hw), BF16),
                   jax.ShapeDtypeStruct((t, 2 * LANES), F32),
                   jax.ShapeDtypeStruct((RWKV_HEADS, 16, LANES), F32),
                   jax.ShapeDtypeStruct((RWKV_HEADS, SUBLANES, 2 * LANES), F32),
                   jax.ShapeDtypeStruct((RWKV_HEADS, 3, 2 * LANES, LANES), F32)),
        scratch_shapes=[pltpu.VMEM((RWKV_HEADS, LANES, LANES), F32),
                        pltpu.VMEM((RWKV_HEADS, 3 * SUBLANES, LANES), F32),
                        pltpu.VMEM((RWKV_HEADS, SUBLANES, 2 * LANES), F32)],
        compiler_params=_cparams(("arbitrary", "arbitrary")),
    )(p_cat, p_cat, p_cat, p_cat, p_cat, p_cat, p_cat, p_cat, ppack, mulo, wl, states, d_out)


def _gdn_specs(nmap):
    cb = OFF_QKV // LANES
    per = CHUNK // SUBLANES
    hb, groups = GDN_HB, GDN_HEADS // GDN_HB
    cb = OFF_QKV // (hb * LANES)
    specs = []
    for j in range(3):
        specs.append(pl.BlockSpec((CHUNK, hb * LANES), lambda n, g, j=j: (nmap(n), cb + j * groups + g)))
    for j in range(3):
        specs.append(pl.BlockSpec((SUBLANES, hb * LANES),
                                  lambda n, g, j=j: (jnp.maximum(nmap(n) * per - 1, 0), cb + j * groups + g)))
    specs.append(pl.BlockSpec((CHUNK, hb * LANES), lambda n, g: (nmap(n), OFF_Z // (hb * LANES) + g)))
    specs.append(pl.BlockSpec((CHUNK, LANES), lambda n, g: (nmap(n), OFF_AB // LANES)))
    specs.append(pl.BlockSpec((hb, 3, SUBLANES, LANES), lambda n, g: (g, 0, 0, 0)))
    specs.append(pl.BlockSpec((SUBLANES, LANES), lambda n, g: (0, 0)))
    return specs


def _conv_taps(x, halo):
    return (x,) + tuple(_shift_rows(x, halo, s) for s in range(1, GDN_CONV))


def _onehots(grp):
    lane = lax.broadcasted_iota(jnp.int32, (GDN_HB, 1, LANES), 2)
    head = lax.broadcasted_iota(jnp.int32, (GDN_HB, 1, LANES), 0) + _head_id(grp, 0, GDN_HB, GDN_HEADS)
    return (lane == head).astype(F32), (lane == GDN_HEADS + head).astype(F32)


def _gdn_taps(refs, halos, live):
    out = []
    for x, hx in zip(refs, halos):
        per_head = [_conv_taps(x[:, i * LANES:(i + 1) * LANES], hx[:, i * LANES:(i + 1) * LANES] * live) for i in range(GDN_HB)]
        out.append(tuple(jnp.stack([per_head[i][s] for i in range(GDN_HB)]) for s in range(GDN_CONV)))
    return out


def _gdn_forward(p_cat, cwpack, gpar):
    t = p_cat.shape[0]
    n_chunks = t // CHUNK

    def body(xq, xk, xv, hq, hk, hv, z, ab, cw, gp, out, st_out, s_scr):
        n, grp = pl.program_id(0), pl.program_id(1)

        hsl = _head_range(grp, GDN_HB, GDN_HEADS)

        @pl.when(n == 0)
        def _():
            s_scr[hsl] = jnp.zeros((GDN_HB, LANES, LANES), F32)

        live = (n > 0).astype(F32)
        oha, ohb = _onehots(grp)
        s0 = s_scr[hsl]
        st_out[...] = s0
        taps = _gdn_taps((xq, xk, xv), (hq, hk, hv), live)
        o, s1 = _gdn_head(*taps, _stack_heads(z, GDN_HB), ab[...], s0, cw[...], gp[...], oha, ohb)
        for i in range(GDN_HB):
            out[:, i * LANES:(i + 1) * LANES] = o[i].astype(out.dtype)
        s_scr[hsl] = s1

    return pl.pallas_call(
        body, name="gdn_fwd", grid=(n_chunks, GDN_HEADS // GDN_HB),
        in_specs=_gdn_specs(lambda n: n),
        out_specs=(pl.BlockSpec((CHUNK, GDN_HB * LANES), lambda n, g: (n, g)),
                   pl.BlockSpec((None, GDN_HB, LANES, LANES), lambda n, g: (n, g, 0, 0))),
        out_shape=(jax.ShapeDtypeStruct((t, GDN_WIDTH), BF16),
                   jax.ShapeDtypeStruct((n_chunks, GDN_HEADS, LANES, LANES), F32)),
        scratch_shapes=[pltpu.VMEM((GDN_HEADS, LANES, LANES), F32)],
        compiler_params=_cparams(("arbitrary", "arbitrary")),
    )(p_cat, p_cat, p_cat, p_cat, p_cat, p_cat, p_cat, p_cat, cwpack, gpar)


def _gdn_backward(p_cat, cwpack, gpar, states, d_out):
    t = p_cat.shape[0]
    n_chunks = t // CHUNK
    last = n_chunks - 1

    def body(xq, xk, xv, hq, hk, hv, z, ab, cw, gp, st, dy, dq, dk, dv, dz, dab, dcw, dgp, ds_scr, car_scr):
        n, grp = pl.program_id(0), pl.program_id(1)

        hsl = _head_range(grp, GDN_HB, GDN_HEADS)

        @pl.when(n == 0)
        def _():
            ds_scr[hsl] = jnp.zeros((GDN_HB, LANES, LANES), F32)
            car_scr[hsl] = jnp.zeros((GDN_HB, 3 * GDN_CONV, SUBLANES, LANES), F32)

        @pl.when((n == 0) & (grp == 0))
        def _():
            dcw[...] = jnp.zeros(dcw.shape, F32)
            dgp[...] = jnp.zeros(dgp.shape, F32)

        live = (n < last).astype(F32)
        oha, ohb = _onehots(grp)
        fn = functools.partial(_gdn_head, oha=oha, ohb=ohb)
        taps = _gdn_taps((xq, xk, xv), (hq, hk, hv), live)
        _, vjp = jax.vjp(fn, *taps, _stack_heads(z, GDN_HB), ab[...], st[...], cw[...], gp[...])
        g = vjp((_stack_heads(dy, GDN_HB), ds_scr[hsl]))
        outs = (dq, dk, dv)
        for i in range(GDN_HB):
            sl = slice(i * LANES, (i + 1) * LANES)
            h = _head_id(grp, i, GDN_HB, GDN_HEADS)
            for j in range(3):
                tot = g[j][0][i]
                for s in range(1, GDN_CONV):
                    slot = j * GDN_CONV + s
                    tot = tot + _unshift_rows(g[j][s][i], car_scr[h, slot], s)
                    car_scr[h, slot] = g[j][s][i][0:SUBLANES, :]
                outs[j][:, sl] = tot.astype(outs[j].dtype)
            dz[:, sl] = g[3][i].astype(dz.dtype)
        dab_sum = g[4]
        ds_scr[hsl] = g[5]
        dcw[hsl] += g[6]
        dgp[0] += g[7]

        @pl.when(grp == 0)
        def _():
            dab[...] = dab_sum

        @pl.when(grp > 0)
        def _():
            dab[...] += dab_sum

    rev = lambda n: last - n
    in_specs = _gdn_specs(rev) + [
        pl.BlockSpec((None, GDN_HB, LANES, LANES), lambda n, g: (rev(n), g, 0, 0)),
        pl.BlockSpec((CHUNK, GDN_HB * LANES), lambda n, g: (rev(n), g)),
    ]
    blk = pl.BlockSpec((CHUNK, GDN_HB * LANES), lambda n, g: (rev(n), g))
    return pl.pallas_call(
        body, name="gdn_bwd", grid=(n_chunks, GDN_HEADS // GDN_HB),
        in_specs=in_specs,
        out_specs=(blk, blk, blk, blk,
                   pl.BlockSpec((CHUNK, LANES), lambda n, h: (rev(n), 0)),
                   pl.BlockSpec((GDN_HEADS, 3, SUBLANES, LANES), lambda n, h: (0, 0, 0, 0)),
                   pl.BlockSpec((GDN_HEADS, SUBLANES, LANES), lambda n, h: (0, 0, 0))),
        out_shape=(jax.ShapeDtypeStruct((t, GDN_WIDTH), BF16), jax.ShapeDtypeStruct((t, GDN_WIDTH), BF16),
                   jax.ShapeDtypeStruct((t, GDN_WIDTH), BF16), jax.ShapeDtypeStruct((t, GDN_WIDTH), BF16),
                   jax.ShapeDtypeStruct((t, LANES), F32),
                   jax.ShapeDtypeStruct((GDN_HEADS, 3, SUBLANES, LANES), F32),
                   jax.ShapeDtypeStruct((GDN_HEADS, SUBLANES, LANES), F32)),
        scratch_shapes=[pltpu.VMEM((GDN_HEADS, LANES, LANES), F32),
                        pltpu.VMEM((GDN_HEADS, 3 * GDN_CONV, SUBLANES, LANES), F32)],
        compiler_params=_cparams(("arbitrary", "arbitrary")),
    )(p_cat, p_cat, p_cat, p_cat, p_cat, p_cat, p_cat, p_cat, cwpack, gpar, states, d_out)


def _pick(n, options):
    for o in options:
        if n % o == 0:
            return o
    raise ValueError(f"no tile for {n}")


_MM_DIMS = {'nn': (((1,), (0,)), ((), ())), 'nt': (((1,), (1,)), ((), ())), 'tn': (((0,), (0,)), ((), ()))}


def _matmul(a, b, mode, out_dtype, name):
    if mode == 'nn':
        (m, k), (k2, n) = a.shape, b.shape
    elif mode == 'nt':
        (m, k), (n, k2) = a.shape, b.shape
    else:
        (k, m), (k2, n) = a.shape, b.shape
    assert k == k2, (a.shape, b.shape, mode)
    tm = _pick(m, (512, 256, 128, 64) if mode == 'tn' else (1024, 512, 256, 128, 64))
    tn = _pick(n, (1024, 512, 256, 128) if mode == 'tn' else (512, 256, 128))
    tk = _pick(k, (1920, 1408, 1024, 512, 256, 128, 64))
    nk = k // tk
    dims = _MM_DIMS[mode]

    def body(a_ref, b_ref, o_ref, acc_ref):
        kk = pl.program_id(2)
        part = lax.dot_general(a_ref[...], b_ref[...], dims, preferred_element_type=F32)
        if nk == 1:
            o_ref[...] = part.astype(o_ref.dtype)
            return

        @pl.when(kk == 0)
        def _():
            acc_ref[...] = part

        @pl.when((kk > 0) & (kk < nk - 1))
        def _():
            acc_ref[...] += part

        @pl.when(kk == nk - 1)
        def _():
            o_ref[...] = (acc_ref[...] + part).astype(o_ref.dtype)

    a_spec = pl.BlockSpec((tk, tm), lambda i, j, kk: (kk, i)) if mode == 'tn' else pl.BlockSpec((tm, tk), lambda i, j, kk: (i, kk))
    b_spec = pl.BlockSpec((tn, tk), lambda i, j, kk: (j, kk)) if mode == 'nt' else pl.BlockSpec((tk, tn), lambda i, j, kk: (kk, j))
    return pl.pallas_call(
        body, name=name, grid=(m // tm, n // tn, nk),
        in_specs=[a_spec, b_spec],
        out_specs=pl.BlockSpec((tm, tn), lambda i, j, kk: (i, j)),
        out_shape=jax.ShapeDtypeStruct((m, n), out_dtype),
        scratch_shapes=[pltpu.VMEM((tm, tn), F32)],
        compiler_params=_cparams(("parallel", "parallel", "arbitrary")),
    )(a, b)


ROW_TILE = 256


def _row_specs(rows, tm):
    return [pl.BlockSpec((tm, w), lambda i, ci=ci: (i, ci)) for (_, w, ci) in rows]


def _rw_forward(fn, rows, pars, outs, name):
    t = rows[0][0].shape[0]
    tm = min(ROW_TILE, t)
    nr, npar = len(rows), len(pars)

    def body(*refs):
        vals = [r[...].astype(F32) for r in refs[:nr]] + [p[...] for p in refs[nr:nr + npar]]
        res = fn(*vals)
        for o, v in zip(refs[nr + npar:], res):
            o[...] = v.astype(o.dtype)

    return pl.pallas_call(
        body, name=name, grid=(t // tm,),
        in_specs=_row_specs(rows, tm) + [pl.BlockSpec(p.shape, lambda i: (0, 0)) for p in pars],
        out_specs=tuple(pl.BlockSpec((tm, w), lambda i: (i, 0)) for (w, _) in outs),
        out_shape=tuple(jax.ShapeDtypeStruct((t, w), dt) for (w, dt) in outs),
        compiler_params=_cparams(("parallel",)),
    )(*[r[0] for r in rows], *pars)


def _rw_backward(fn, rows, pars, cots, drow_dtypes, name):
    t = rows[0][0].shape[0]
    tm = min(ROW_TILE, t)
    nr, npar, nc = len(rows), len(pars), len(cots)
    keep = [i for i, dt in enumerate(drow_dtypes) if dt is not None]

    def body(*refs):
        vals = [r[...].astype(F32) for r in refs[:nr]] + [p[...] for p in refs[nr:nr + npar]]
        cvals = tuple(c[...].astype(F32) for c in refs[nr + npar:nr + npar + nc])
        orefs = refs[nr + npar + nc:]
        _, vjp = jax.vjp(fn, *vals)
        g = vjp(cvals)
        for o, i in zip(orefs[:len(keep)], keep):
            o[...] = g[i].astype(o.dtype)
        first = pl.program_id(0) == 0
        for o, gi in zip(orefs[len(keep):], g[nr:]):
            @pl.when(first)
            def _(o=o, gi=gi):
                o[...] = gi

            @pl.when(jnp.logical_not(first))
            def _(o=o, gi=gi):
                o[...] += gi

    out_specs = [pl.BlockSpec((tm, rows[i][1]), lambda i_: (i_, 0)) for i in keep] + \
                [pl.BlockSpec(p.shape, lambda i_: (0, 0)) for p in pars]
    out_shape = [jax.ShapeDtypeStruct((t, rows[i][1]), drow_dtypes[i]) for i in keep] + \
                [jax.ShapeDtypeStruct(p.shape, F32) for p in pars]
    return pl.pallas_call(
        body, name=name, grid=(t // tm,),
        in_specs=_row_specs(rows, tm) + [pl.BlockSpec(p.shape, lambda i: (0, 0)) for p in pars] + _row_specs(cots, tm),
        out_specs=tuple(out_specs), out_shape=tuple(out_shape),
        compiler_params=_cparams(("arbitrary",)),
    )(*[r[0] for r in rows], *pars, *[c[0] for c in cots])


def _norm_fn(x, g):
    return (_rms(x, g),)


def _norm_skip_fn(x, g):
    return _rms(x, g), x


def _merge_fn(ga, gb, ya, yb):
    return (_sigmoid(ga) * ya + _sigmoid(gb) * yb,)


def _res_norm_fn(x, mo, g):
    x1 = x + mo
    return x1, _rms(x1, g)


def _loss_head(x1, fo, gf, target, name):
    t, d = x1.shape
    tm = min(ROW_TILE, t)

    def tile_loss(x2, g, tgt):
        err = _rms(x2, g) - tgt
        per_row = jnp.sum(err * err, axis=-1, keepdims=True) * (0.5 / d)
        return jnp.sum(per_row, axis=0, keepdims=True)

    def body(x1_ref, fo_ref, g_ref, t_ref, loss_ref, dx_ref, dxb_ref, dg_ref):
        x2 = x1_ref[...] + fo_ref[...]
        val, vjp = jax.vjp(functools.partial(tile_loss, tgt=t_ref[...]), x2, g_ref[...])
        dx2, dg = vjp(jnp.ones((1, 1), F32))
        dx_ref[...] = dx2
        dxb_ref[...] = dx2.astype(BF16)
        first = pl.program_id(0) == 0

        @pl.when(first)
        def _():
            loss_ref[...] = jnp.broadcast_to(val, loss_ref.shape)
            dg_ref[...] = dg

        @pl.when(jnp.logical_not(first))
        def _():
            loss_ref[...] += jnp.broadcast_to(val, loss_ref.shape)
            dg_ref[...] += dg

    row = pl.BlockSpec((tm, d), lambda i: (i, 0))
    vec = pl.BlockSpec((1, d), lambda i: (0, 0))
    return pl.pallas_call(
        body, name=name, grid=(t // tm,),
        in_specs=[row, row, vec, row],
        out_specs=(pl.BlockSpec((1, LANES), lambda i: (0, 0)), row, row, vec),
        out_shape=(jax.ShapeDtypeStruct((1, LANES), F32), jax.ShapeDtypeStruct((t, d), F32),
                   jax.ShapeDtypeStruct((t, d), BF16), jax.ShapeDtypeStruct((1, d), F32)),
        compiler_params=_cparams(("arbitrary",)),
    )(x1, fo, gf, target)


FFN_TILE_ROWS = 512
FFN_TILE_COLS = 256
FFN_COL_BLOCKS = FFN_HIDDEN // FFN_TILE_COLS


def _conv3_past(x, halo, w):
    rows = lax.broadcasted_iota(jnp.int32, x.shape, 0)
    x1 = jnp.where(rows == 0, halo[7:8, :], pltpu.roll(x, 1, 0))
    x2 = jnp.where(rows == 0, halo[6:7, :], jnp.where(rows == 1, halo[7:8, :], pltpu.roll(x, 2, 0)))
    return x * w[2:3] + x1 * w[1:2] + x2 * w[0:1], x1, x2


def _ffn_in_specs(tm, imap, jmap):
    per = tm // SUBLANES
    tile = lambda off: pl.BlockSpec((tm, FFN_TILE_COLS), lambda *g: (imap(*g), off + jmap(*g) % FFN_COL_BLOCKS))
    halo = lambda off: pl.BlockSpec((SUBLANES, FFN_TILE_COLS),
                                    lambda *g: (jnp.maximum(imap(*g) * per - 1, 0), off + jmap(*g) % FFN_COL_BLOCKS))
    wsp = lambda off: pl.BlockSpec((FFN_CONV, FFN_TILE_COLS), lambda *g: (0, off + jmap(*g) % FFN_COL_BLOCKS))
    return [tile(0), halo(0), wsp(0), tile(FFN_COL_BLOCKS), halo(FFN_COL_BLOCKS), wsp(FFN_COL_BLOCKS)]


def _ffn_act_forward(hpre, cw):
    t = hpre.shape[0]
    tm = min(FFN_TILE_ROWS, t)

    def body(hg, pg, wg, hu, pu, wu, out):
        live = (pl.program_id(0) > 0).astype(F32)
        cg, _, _ = _conv3_past(hg[...], pg[...] * live, wg[...])
        cu, _, _ = _conv3_past(hu[...], pu[...] * live, wu[...])
        out[...] = (_silu(cg) * cu).astype(out.dtype)

    return pl.pallas_call(
        body, name="ffn_act_fwd", grid=(t // tm, FFN_COL_BLOCKS),
        in_specs=_ffn_in_specs(tm, lambda i, j: i, lambda i, j: j),
        out_specs=pl.BlockSpec((tm, FFN_TILE_COLS), lambda i, j: (i, j)),
        out_shape=jax.ShapeDtypeStruct((t, FFN_HIDDEN), BF16),
        compiler_params=_cparams(("parallel", "parallel")),
    )(hpre, hpre, cw, hpre, hpre, cw)


def _ffn_act_backward(hpre, cw, dact):
    t = hpre.shape[0]
    tm = min(FFN_TILE_ROWS, t)

    def body(hg, pg, wg, hu, pu, wu, da, dconv, dw):
        j, i = pl.program_id(0), pl.program_id(1)
        live = (i > 0).astype(F32)
        cg, g1, g2 = _conv3_past(hg[...], pg[...] * live, wg[...])
        cu, u1, u2 = _conv3_past(hu[...], pu[...] * live, wu[...])
        s = _sigmoid(cg)
        d = da[...]
        d_gate = d * cu * s * (1.0 + cg * (1.0 - s))
        d_up = d * cg * s
        is_gate = j < FFN_COL_BLOCKS
        dc = jnp.where(is_gate, d_gate, d_up)
        dconv[...] = dc
        taps = (jnp.where(is_gate, g2, u2), jnp.where(is_gate, g1, u1), jnp.where(is_gate, hg[...], hu[...]))
        sums = [jnp.sum(xs * dc, axis=0, keepdims=True) for xs in taps]

        @pl.when(i == 0)
        def _():
            for r_ in range(FFN_CONV):
                dw[r_:r_ + 1, :] = sums[r_]

        @pl.when(i > 0)
        def _():
            for r_ in range(FFN_CONV):
                dw[r_:r_ + 1, :] += sums[r_]

    return pl.pallas_call(
        body, name="ffn_act_bwd", grid=(2 * FFN_COL_BLOCKS, t // tm),
        in_specs=_ffn_in_specs(tm, lambda j, i: i, lambda j, i: j) +
        [pl.BlockSpec((tm, FFN_TILE_COLS), lambda j, i: (i, j % FFN_COL_BLOCKS))],
        out_specs=(pl.BlockSpec((tm, FFN_TILE_COLS), lambda j, i: (i, j)),
                   pl.BlockSpec((FFN_CONV, FFN_TILE_COLS), lambda j, i: (0, j))),
        out_shape=(jax.ShapeDtypeStruct((t, 2 * FFN_HIDDEN), F32), jax.ShapeDtypeStruct((FFN_CONV, 2 * FFN_HIDDEN), F32)),
        compiler_params=_cparams(("parallel", "arbitrary")),
    )(hpre, hpre, cw, hpre, hpre, cw, dact)


def _ffn_conv_backward(dconv, cw):
    t = dconv.shape[0]
    tm = min(FFN_TILE_ROWS, t)
    n_tiles = t // tm
    per = tm // SUBLANES

    def body(d_ref, nx_ref, w_ref, out):
        live = (pl.program_id(0) < n_tiles - 1).astype(F32)
        d = d_ref[...]
        nx = nx_ref[...] * live
        w = w_ref[...]
        rows = lax.broadcasted_iota(jnp.int32, d.shape, 0)
        d1 = jnp.where(rows == tm - 1, nx[0:1, :], pltpu.roll(d, tm - 1, 0))
        d2 = jnp.where(rows == tm - 1, nx[1:2, :], jnp.where(rows == tm - 2, nx[0:1, :], pltpu.roll(d, tm - 2, 0)))
        out[...] = (d * w[2:3] + d1 * w[1:2] + d2 * w[0:1]).astype(out.dtype)

    return pl.pallas_call(
        body, name="ffn_conv_bwd", grid=(n_tiles, 2 * FFN_COL_BLOCKS),
        in_specs=[pl.BlockSpec((tm, FFN_TILE_COLS), lambda i, j: (i, j)),
                  pl.BlockSpec((SUBLANES, FFN_TILE_COLS), lambda i, j: (jnp.minimum((i + 1) * per, t // SUBLANES - 1), j)),
                  pl.BlockSpec((FFN_CONV, FFN_TILE_COLS), lambda i, j: (0, j))],
        out_specs=pl.BlockSpec((tm, FFN_TILE_COLS), lambda i, j: (i, j)),
        out_shape=jax.ShapeDtypeStruct((t, 2 * FFN_HIDDEN), BF16),
        compiler_params=_cparams(("parallel", "parallel")),
    )(dconv, dconv, cw)


def _my_place():
    x, y, c = lax.axis_index("x"), lax.axis_index("y"), lax.axis_index("c")
    return x, y, c, 4 * x + 2 * y + c


N_CHIPS = 4


def _remote(src, dst, send_sem, recv_sem, dev):
    return pltpu.make_async_remote_copy(src_ref=src, dst_ref=dst, send_sem=send_sem, recv_sem=recv_sem, device_id=dev,
                                        device_id_type=pl.DeviceIdType.MESH)


def _chip_peer(x, y, k):
    return x ^ ((k >> 1) & 1), y ^ (k & 1)


def _all_gather_two_level(shard, name):
    r, w = shard.shape

    def body(src, out, send_sems, recv_sems, local_sem):
        x, y, c, me = _my_place()
        sibling = (x, y, 1 - c)
        mine = pltpu.make_async_copy(src, out.at[me], local_sem)
        mine.start()
        first = [_remote(src, out.at[me], send_sems.at[0], recv_sems.at[0], sibling)]
        for k in range(1, N_CHIPS):
            px, py = _chip_peer(x, y, k)
            first.append(_remote(src, out.at[me], send_sems.at[k], recv_sems.at[k], (px, py, c)))
        for cp in first:
            cp.start()
        passed = []
        for k in range(1, N_CHIPS):
            px, py = _chip_peer(x, y, k)
            landed = out.at[me ^ (2 * k)]
            _remote(src, landed, send_sems.at[k], recv_sems.at[k], (px, py, c)).wait_recv()
            fwd = _remote(landed, landed, send_sems.at[N_CHIPS - 1 + k], recv_sems.at[N_CHIPS - 1 + k], sibling)
            fwd.start()
            passed.append(fwd)
        _remote(src, out.at[me ^ 1], send_sems.at[0], recv_sems.at[0], sibling).wait_recv()
        for k in range(1, N_CHIPS):
            got = out.at[(me ^ 1) ^ (2 * k)]
            _remote(got, got, send_sems.at[N_CHIPS - 1 + k], recv_sems.at[N_CHIPS - 1 + k], sibling).wait_recv()
        for cp in first + passed:
            cp.wait_send()
        mine.wait()

    return pl.pallas_call(
        body, name=name,
        in_specs=[pl.BlockSpec(memory_space=pl.ANY)],
        out_specs=pl.BlockSpec(memory_space=pl.ANY),
        out_shape=jax.ShapeDtypeStruct((N_DEV, r, w), shard.dtype),
        scratch_shapes=[pltpu.SemaphoreType.DMA((N_DEV - 1,)), pltpu.SemaphoreType.DMA((N_DEV - 1,)), pltpu.SemaphoreType.DMA],
    )(shard)


def _pair_exchange(big, small, name):
    def body(big_ref, small_ref, rbig, rsmall, send_sems, recv_sems):
        x, y, c, _ = _my_place()
        sibling = (x, y, 1 - c)
        copies = []
        for q in range(N_CHIPS):
            copies.append(_remote(big_ref.at[2 * q + 1 - c], rbig.at[q], send_sems.at[q], recv_sems.at[q], sibling))
            copies.append(_remote(small_ref.at[2 * q + 1 - c], rsmall.at[q], send_sems.at[N_CHIPS + q],
                                  recv_sems.at[N_CHIPS + q], sibling))
        for cp in copies:
            cp.start()
        for cp in copies:
            cp.wait_recv()
        for cp in copies:
            cp.wait_send()

    return pl.pallas_call(
        body, name=name,
        in_specs=[pl.BlockSpec(memory_space=pl.ANY), pl.BlockSpec(memory_space=pl.ANY)],
        out_specs=(pl.BlockSpec(memory_space=pl.ANY), pl.BlockSpec(memory_space=pl.ANY)),
        out_shape=(jax.ShapeDtypeStruct((N_CHIPS,) + big.shape[1:], big.dtype),
                   jax.ShapeDtypeStruct((N_CHIPS,) + small.shape[1:], small.dtype)),
        scratch_shapes=[pltpu.SemaphoreType.DMA((2 * N_CHIPS,)), pltpu.SemaphoreType.DMA((2 * N_CHIPS,))],
    )(big, small)


def _pair_sum(slabs, recv, core, out_dtype, rows_per_step, name):
    _, r, w = slabs.shape

    def body(core_ref, mine, theirs, out):
        out[...] = (mine[...] + theirs[...]).astype(out.dtype)

    grid_spec = pltpu.PrefetchScalarGridSpec(
        num_scalar_prefetch=1, grid=(N_CHIPS, r // rows_per_step),
        in_specs=[pl.BlockSpec((None, rows_per_step, w), lambda q, i, core_ref: (2 * q + core_ref[0], i, 0)),
                  pl.BlockSpec((None, rows_per_step, w), lambda q, i, core_ref: (q, i, 0))],
        out_specs=pl.BlockSpec((None, rows_per_step, w), lambda q, i, core_ref: (q, i, 0)))
    return pl.pallas_call(body, name=name, grid_spec=grid_spec,
                          out_shape=jax.ShapeDtypeStruct((N_CHIPS, r, w), out_dtype),
                          compiler_params=_cparams(("parallel", "parallel")))(core, slabs, recv)


def _chip_exchange(big, small, name):
    def body(big_ref, small_ref, rbig, rsmall, send_sems, recv_sems, local_sems):
        x, y, c, _ = _my_place()
        chip = 2 * x + y
        own = [pltpu.make_async_copy(big_ref.at[chip], rbig.at[chip], local_sems.at[0]),
               pltpu.make_async_copy(small_ref.at[chip], rsmall.at[chip], local_sems.at[1])]
        for cp in own:
            cp.start()
        copies = []
        for k in range(1, N_CHIPS):
            px, py = _chip_peer(x, y, k)
            copies.append(_remote(big_ref.at[chip ^ k], rbig.at[chip], send_sems.at[k], recv_sems.at[k], (px, py, c)))
            copies.append(_remote(small_ref.at[chip ^ k], rsmall.at[chip], send_sems.at[N_CHIPS + k],
                                  recv_sems.at[N_CHIPS + k], (px, py, c)))
        for cp in copies:
            cp.start()
        for k in range(1, N_CHIPS):
            px, py = _chip_peer(x, y, k)
            _remote(big_ref.at[chip], rbig.at[chip ^ k], send_sems.at[k], recv_sems.at[k], (px, py, c)).wait_recv()
            _remote(small_ref.at[chip], rsmall.at[chip ^ k], send_sems.at[N_CHIPS + k], recv_sems.at[N_CHIPS + k],
                    (px, py, c)).wait_recv()
        for cp in copies:
            cp.wait_send()
        for cp in own:
            cp.wait()

    return pl.pallas_call(
        body, name=name,
        in_specs=[pl.BlockSpec(memory_space=pl.ANY), pl.BlockSpec(memory_space=pl.ANY)],
        out_specs=(pl.BlockSpec(memory_space=pl.ANY), pl.BlockSpec(memory_space=pl.ANY)),
        out_shape=(jax.ShapeDtypeStruct(big.shape, big.dtype), jax.ShapeDtypeStruct(small.shape, small.dtype)),
        scratch_shapes=[pltpu.SemaphoreType.DMA((2 * N_CHIPS,)), pltpu.SemaphoreType.DMA((2 * N_CHIPS,)),
                        pltpu.SemaphoreType.DMA((2,))],
    )(big, small)


def _reduce_adamw(parts, w, m, v, rows_per_step, name):
    n_parts, r, wd = parts.shape
    c1 = 1.0 / (1.0 - ADAM_B1 ** ADAM_STEP)
    c2 = 1.0 / (1.0 - ADAM_B2 ** ADAM_STEP)

    def body(p_ref, w_ref, m_ref, v_ref, g_out, d_out, m_out, v_out):
        g = p_ref[0].astype(F32)
        for s in range(1, n_parts):
            g = g + p_ref[s].astype(F32)
        mn = ADAM_B1 * m_ref[...] + (1.0 - ADAM_B1) * g
        vn = ADAM_B2 * v_ref[...] + (1.0 - ADAM_B2) * (g * g)
        g_out[...] = g
        m_out[...] = mn
        v_out[...] = vn
        d_out[...] = -ADAM_LR * ((mn * c1) / (jnp.sqrt(vn * c2) + ADAM_EPS) + ADAM_WD * w_ref[...])

    blk = pl.BlockSpec((rows_per_step, wd), lambda i: (i, 0))
    shp = jax.ShapeDtypeStruct((r, wd), F32)
    return pl.pallas_call(
        body, name=name, grid=(r // rows_per_step,),
        in_specs=[pl.BlockSpec((n_parts, rows_per_step, wd), lambda i: (0, i, 0)), blk, blk, blk],
        out_specs=(blk, blk, blk, blk), out_shape=(shp, shp, shp, shp),
        compiler_params=_cparams(("parallel",)),
    )(parts, w, m, v)


PACK_W = 1024


def _pad_heads(a, slots):
    lead = a.shape[:-1]
    a = a.reshape(lead + (slots, RWKV_HEAD_DIM))
    a = jnp.pad(a, [(0, 0)] * (len(lead) + 1) + [(0, LANES - RWKV_HEAD_DIM)])
    return a.reshape(lead + (slots * LANES,))


def _unpad_heads(a, slots):
    lead = a.shape[:-1]
    return a.reshape(lead + (slots, LANES))[..., :RWKV_HEAD_DIM].reshape(lead + (slots * RWKV_HEAD_DIM,))


def _flat_pack(arrs, dtype, row_mult):
    flat = jnp.concatenate([a.reshape(-1).astype(dtype) for a in arrs])
    n = flat.shape[0]
    rows = -(-n // PACK_W)
    rows = -(-rows // row_mult) * row_mult
    return jnp.pad(flat, (0, rows * PACK_W - n)).reshape(rows, PACK_W)


def _unpack_gathered(g, names, shard_shapes):
    flat = g.reshape(N_DEV, -1)
    out, off = {}, 0
    for n in names:
        s = shard_shapes[n]
        size = s[0] * s[1]
        seg = flat[:, off:off + size].reshape((N_DEV,) + tuple(s))
        off += size
        if SHARD_AXIS[n] == 1:
            out[n] = jnp.transpose(seg, (1, 0, 2)).reshape(s[0], N_DEV * s[1])
        else:
            out[n] = seg.reshape(N_DEV * s[0], s[1])
    return out


def _shard_major(full, axis):
    a, b = full.shape
    if axis == 1:
        return jnp.transpose(full.reshape(a, N_DEV, b // N_DEV), (1, 0, 2)).reshape(N_DEV, -1)
    return full.reshape(N_DEV, -1)


def _prepare_weights(full, rep):
    w = full['w_in']
    d = w.shape[1]
    rkv = jnp.pad(w[0:1536].reshape(3 * RWKV_HEADS, RWKV_HEAD_DIM, d), ((0, 0), (0, LANES - RWKV_HEAD_DIM), (0, 0)))
    w_cat = jnp.concatenate([
        w[3848:4872], w[4872:5896], rkv.reshape(3 * RWKV_HEADS * LANES, d), w[1792:3328], w[3328:3840],
        w[1536:1792], jnp.pad(w[3840:3848], ((0, LANES - 8), (0, 0))), jnp.zeros((LANES, d), w.dtype)], axis=0)
    assert w_cat.shape[0] == CAT_W
    rp = full['rwkv_proj']
    rproj = jnp.pad(rp.reshape(RWKV_HEADS, RWKV_HEAD_DIM, -1), ((0, 0), (0, LANES - RWKV_HEAD_DIM), (0, 0))).reshape(RWKV_HEADS * LANES, -1)
    mu = rep['rwkv_mu']
    vecs = [mu[0:512], mu[512:1024], mu[1024:1536], rep['rwkv_w0'], rep['rwkv_a0'], rep['rwkv_k_k'], rep['rwkv_k_a'],
            rep['rwkv_ln_w'], rep['rwkv_ln_b'], rep['rwkv_r_k'].reshape(-1)]
    ppack = jnp.stack([jnp.pad(v.reshape(RWKV_HEADS, RWKV_HEAD_DIM), ((0, 0), (0, LANES - RWKV_HEAD_DIM))) for v in vecs], axis=1)
    ppack = jnp.pad(ppack, ((0, 0), (0, 16 - len(vecs)), (0, 0)))
    mulo = mu[1536:1792].reshape(1, 2 * LANES)
    wl = jnp.zeros((3, 2 * LANES, RWKV_HEADS * LANES), F32)
    wl = wl.at[0, 0:64].set(_pad_heads(full['rwkv_w2'], RWKV_HEADS))
    wl = wl.at[1, 64:128].set(_pad_heads(full['rwkv_a2'], RWKV_HEADS))
    wl = wl.at[2, 128:256].set(_pad_heads(full['rwkv_g2'], RWKV_HEADS))
    wl = jnp.transpose(wl.reshape(3, 2 * LANES, RWKV_HEADS, LANES), (2, 0, 1, 3))
    cw = full['gdn_conv_w'].reshape(GDN_CONV, 3, GDN_HEADS, LANES)
    cwpack = jnp.pad(jnp.transpose(cw, (2, 1, 0, 3)), ((0, 0), (0, 0), (0, SUBLANES - GDN_CONV), (0, 0)))
    gpar = jnp.zeros((SUBLANES, LANES), F32)
    gpar = gpar.at[0, 0:GDN_HEADS].set(rep['gdn_a_log']).at[1, 0:GDN_HEADS].set(rep['gdn_dt_bias']).at[2].set(rep['gdn_norm_w'])
    return dict(w_cat=w_cat, rproj=rproj, gproj=full['gdn_proj'], w_out=full['w_out'], ffn_up=full['ffn_up'],
                ffn_down=full['ffn_down'], ffn_cw=full['ffn_conv_w'], ppack=ppack, mulo=mulo, wl=wl, cwpack=cwpack, gpar=gpar,
                g1=rep['norm1_g'].reshape(1, -1), g2=rep['norm2_g'].reshape(1, -1), gf=rep['final_g'].reshape(1, -1))


def _local_step(x, target, p):
    d = x.shape[1]
    full_w = lambda a: (a, a.shape[1], 0)
    (u,) = _rw_forward(_norm_fn, [full_w(x)], [p['g1']], [(d, BF16)], "norm1")
    p_cat = _matmul(u, p['w_cat'], 'nt', F32, "proj_in")
    ya_pre, st_r = _rwkv_forward(p_cat, p['ppack'], p['mulo'], p['wl'])
    yb_pre, st_g = _gdn_forward(p_cat, p['cwpack'], p['gpar'])
    ya = _matmul(ya_pre, p['rproj'], 'nn', F32, "rwkv_proj")
    yb = _matmul(yb_pre, p['gproj'], 'nn', F32, "gdn_proj")
    gates = [(p_cat, d, OFF_GA // d), (p_cat, d, OFF_GB // d)]
    (mixed,) = _rw_forward(_merge_fn, gates + [full_w(ya), full_w(yb)], [], [(d, BF16)], "merge")
    mo = _matmul(mixed, p['w_out'], 'nn', F32, "out_proj")
    x1, n2 = _rw_forward(_res_norm_fn, [full_w(x), full_w(mo)], [p['g2']], [(d, F32), (d, BF16)], "res_norm2")
    hpre = _matmul(n2, p['ffn_up'], 'nt', F32, "ffn_up")
    act = _ffn_act_forward(hpre, p['ffn_cw'])
    fo = _matmul(act, p['ffn_down'], 'nn', F32, "ffn_down")
    loss_vec, dx2, dx2b, dgf = _loss_head(x1, fo, p['gf'], target, "loss_head")

    dact = _matmul(dx2b, p['ffn_down'], 'nt', F32, "d_act")
    dw_down = _matmul(act, dx2b, 'tn', F32, "dw_ffn_down")
    dconv, dcw_f = _ffn_act_backward(hpre, p['ffn_cw'], dact)
    dh = _ffn_conv_backward(dconv, p['ffn_cw'])
    dn2 = _matmul(dh, p['ffn_up'], 'nn', F32, "d_norm2")
    dw_up = _matmul(dh, n2, 'tn', F32, "dw_ffn_up")
    dx1, dx1b, dg2 = _rw_backward(_res_norm_fn, [full_w(x), full_w(mo)], [p['g2']], [full_w(dx2), full_w(dn2)],
                                  [F32, BF16], "res_norm2_bwd")
    dmixed = _matmul(dx1b, p['w_out'], 'nt', F32, "d_mixed")
    dw_out = _matmul(mixed, dx1b, 'tn', F32, "dw_out")
    dga, dgb, dya, dyb = _rw_backward(_merge_fn, gates + [full_w(ya), full_w(yb)], [], [full_w(dmixed)],
                                      [BF16, BF16, BF16, BF16], "merge_bwd")
    d_ya_pre = _matmul(dya, p['rproj'], 'nt', F32, "d_rwkv_out")
    dw_rproj = _matmul(ya_pre, dya, 'tn', F32, "dw_rwkv_proj")
    d_yb_pre = _matmul(dyb, p['gproj'], 'nt', F32, "d_gdn_out")
    dw_gproj = _matmul(yb_pre, dyb, 'tn', F32, "dw_gdn_proj")
    dpr, dpk, dpv, dplo, dpp, dml, dwl = _rwkv_backward(p_cat, p['ppack'], p['mulo'], p['wl'], st_r, d_ya_pre)
    dq, dk, dv, dz, dab, dcw_g, dgp = _gdn_backward(p_cat, p['cwpack'], p['gpar'], st_g, d_yb_pre)
    t = x.shape[0]
    dp_cat = jnp.concatenate([dga, dgb, dpr, dpk, dpv, dq, dk, dv, dz, dplo.astype(BF16), dab.astype(BF16),
                              jnp.zeros((t, LANES), BF16)], axis=1)
    du = _matmul(dp_cat, p['w_cat'], 'nn', F32, "d_norm1")
    dw_cat = _matmul(dp_cat, u, 'tn', F32, "dw_in")
    grad_x, dg1 = _rw_backward(_norm_skip_fn, [full_w(x)], [p['g1']], [full_w(du), full_w(dx1)], [F32], "norm1_bwd")

    heads = lambda row: dpp[:, row, :RWKV_HEAD_DIM].reshape(-1)
    lora = lambda j, lo_, hi_: jnp.transpose(dwl[:, j, lo_:hi_, :RWKV_HEAD_DIM], (1, 0, 2)).reshape(hi_ - lo_, RWKV_WIDTH)
    grads = {
        'norm1_g': dg1[0],
        'w_in': jnp.concatenate([dw_cat[OFF_RKV:OFF_QKV].reshape(3 * RWKV_HEADS, LANES, d)[:, :RWKV_HEAD_DIM].reshape(-1, d),
                                 dw_cat[OFF_LO:OFF_AB], dw_cat[OFF_QKV:OFF_Z], dw_cat[OFF_Z:OFF_LO], dw_cat[OFF_AB:OFF_AB + 8],
                                 dw_cat[OFF_GA:OFF_GB], dw_cat[OFF_GB:OFF_RKV]], axis=0),
        'rwkv_mu': jnp.concatenate([heads(0), heads(1), heads(2), jnp.sum(dml[:, 0, :], axis=0)]),
        'rwkv_w0': heads(3), 'rwkv_a0': heads(4), 'rwkv_k_k': heads(5), 'rwkv_k_a': heads(6),
        'rwkv_ln_w': heads(7), 'rwkv_ln_b': heads(8), 'rwkv_r_k': heads(9).reshape(RWKV_HEADS, RWKV_HEAD_DIM),
        'rwkv_w2': lora(0, 0, 64), 'rwkv_a2': lora(1, 64, 128), 'rwkv_g2': lora(2, 128, 256),
        'rwkv_proj': dw_rproj.reshape(RWKV_HEADS, LANES, -1)[:, :RWKV_HEAD_DIM].reshape(RWKV_WIDTH, -1),
        'gdn_conv_w': jnp.transpose(dcw_g[:, :, :GDN_CONV, :], (2, 1, 0, 3)).reshape(GDN_CONV, 3 * GDN_WIDTH),
        'gdn_a_log': jnp.sum(dgp[:, 0, :GDN_HEADS], axis=0), 'gdn_dt_bias': jnp.sum(dgp[:, 1, :GDN_HEADS], axis=0),
        'gdn_norm_w': jnp.sum(dgp[:, 2, :], axis=0),
        'gdn_proj': dw_gproj, 'w_out': dw_out, 'norm2_g': dg2[0], 'ffn_up': dw_up, 'ffn_conv_w': dcw_f,
        'ffn_down': dw_down, 'final_g': dgf[0],
    }
    return loss_vec, grad_x, grads


def kernel(x, norm1_g, w_in, rwkv_mu, rwkv_w0, rwkv_w2, rwkv_a0, rwkv_a2, rwkv_g2, rwkv_k_k, rwkv_k_a, rwkv_r_k, rwkv_ln_w, rwkv_ln_b, rwkv_proj, gdn_conv_w, gdn_a_log, gdn_dt_bias, gdn_norm_w, gdn_proj, w_out, norm2_g, ffn_up, ffn_conv_w, ffn_down, final_g, loss_target, m_norm1_g, m_w_in, m_rwkv_mu, m_rwkv_w0, m_rwkv_w2, m_rwkv_a0, m_rwkv_a2, m_rwkv_g2, m_rwkv_k_k, m_rwkv_k_a, m_rwkv_r_k, m_rwkv_ln_w, m_rwkv_ln_b, m_rwkv_proj, m_gdn_conv_w, m_gdn_a_log, m_gdn_dt_bias, m_gdn_norm_w, m_gdn_proj, m_w_out, m_norm2_g, m_ffn_up, m_ffn_conv_w, m_ffn_down, m_final_g, v_norm1_g, v_w_in, v_rwkv_mu, v_rwkv_w0, v_rwkv_w2, v_rwkv_a0, v_rwkv_a2, v_rwkv_g2, v_rwkv_k_k, v_rwkv_k_a, v_rwkv_r_k, v_rwkv_ln_w, v_rwkv_ln_b, v_rwkv_proj, v_gdn_conv_w, v_gdn_a_log, v_gdn_dt_bias, v_gdn_norm_w, v_gdn_proj, v_w_out, v_norm2_g, v_ffn_up, v_ffn_conv_w, v_ffn_down, v_final_g):
    given = dict(zip(WEIGHT_NAMES, (norm1_g, w_in, rwkv_mu, rwkv_w0, rwkv_w2, rwkv_a0, rwkv_a2, rwkv_g2, rwkv_k_k, rwkv_k_a, rwkv_r_k,
                                    rwkv_ln_w, rwkv_ln_b, rwkv_proj, gdn_conv_w, gdn_a_log, gdn_dt_bias, gdn_norm_w, gdn_proj, w_out,
                                    norm2_g, ffn_up, ffn_conv_w, ffn_down, final_g)))
    mom1 = dict(zip(WEIGHT_NAMES, (m_norm1_g, m_w_in, m_rwkv_mu, m_rwkv_w0, m_rwkv_w2, m_rwkv_a0, m_rwkv_a2, m_rwkv_g2, m_rwkv_k_k,
                                   m_rwkv_k_a, m_rwkv_r_k, m_rwkv_ln_w, m_rwkv_ln_b, m_rwkv_proj, m_gdn_conv_w, m_gdn_a_log,
                                   m_gdn_dt_bias, m_gdn_norm_w, m_gdn_proj, m_w_out, m_norm2_g, m_ffn_up, m_ffn_conv_w, m_ffn_down,
                                   m_final_g)))
    mom2 = dict(zip(WEIGHT_NAMES, (v_norm1_g, v_w_in, v_rwkv_mu, v_rwkv_w0, v_rwkv_w2, v_rwkv_a0, v_rwkv_a2, v_rwkv_g2, v_rwkv_k_k,
                                   v_rwkv_k_a, v_rwkv_r_k, v_rwkv_ln_w, v_rwkv_ln_b, v_rwkv_proj, v_gdn_conv_w, v_gdn_a_log,
                                   v_gdn_dt_bias, v_gdn_norm_w, v_gdn_proj, v_w_out, v_norm2_g, v_ffn_up, v_ffn_conv_w, v_ffn_down,
                                   v_final_g)))
    def strip(n, a):
        a = a if n == 'final_g' else a.reshape(a.shape[1:])
        return a.T if n in TRANSPOSED else a

    local = {n: strip(n, a) for n, a in given.items()}
    shard_shapes = {n: local[n].shape for n in SHARD_AXIS}
    sharded = BIG_SHARDED + SMALL_SHARDED

    g_big = _all_gather_two_level(_flat_pack([local[n] for n in BIG_SHARDED], BF16, 16), "gather_big")
    g_small = _all_gather_two_level(_flat_pack([local[n] for n in SMALL_SHARDED], F32, SUBLANES), "gather_small")
    full = _unpack_gathered(g_big, BIG_SHARDED, shard_shapes)
    full.update(_unpack_gathered(g_small, SMALL_SHARDED, shard_shapes))
    rep = {n: local[n] for n in REPLICATED}

    loss_vec, grad_x, grads = _local_step(x[0], loss_target[0], _prepare_weights(full, rep))

    small_names = SMALL_SHARDED + REPLICATED
    rep_vec = jnp.concatenate([grads[n].reshape(-1) for n in REPLICATED] + [loss_vec[0, 0:1]])
    slab_big = jnp.concatenate([_shard_major(grads[n], SHARD_AXIS[n]) for n in BIG_SHARDED], axis=1)
    slab_small = jnp.concatenate([_shard_major(grads[n], SHARD_AXIS[n]) for n in SMALL_SHARDED] +
                                 [jnp.broadcast_to(rep_vec[None], (N_DEV, rep_vec.shape[0]))], axis=1)
    big_step, small_rows = 416, 32
    big_rows = -(-slab_big.shape[1] // (PACK_W * big_step)) * big_step
    assert slab_small.shape[1] <= small_rows * PACK_W
    to_rows = lambda a, rows: jnp.pad(a, ((0, 0), (0, rows * PACK_W - a.shape[1]))).reshape(a.shape[0], rows, PACK_W)
    slab_big, slab_small = to_rows(slab_big, big_rows), to_rows(slab_small, small_rows)
    core = lax.axis_index("c").astype(jnp.int32).reshape(1)
    sib_big, sib_small = _pair_exchange(slab_big, slab_small, "grad_pair_exchange")
    chip_big = _pair_sum(slab_big, sib_big, core, BF16, big_step, "grad_pair_sum_big")
    chip_small = _pair_sum(slab_small, sib_small, core, F32, small_rows, "grad_pair_sum_small")
    parts_big, parts_small = _chip_exchange(chip_big, chip_small, "grad_chip_exchange")

    def pack_local(src, names, rows):
        flat = jnp.concatenate([strip(n, src[n]).reshape(-1) for n in names])
        return jnp.pad(flat, (0, rows * PACK_W - flat.shape[0])).reshape(rows, PACK_W)

    packs_big = _reduce_adamw(parts_big, *[pack_local(s, BIG_SHARDED, big_rows) for s in (given, mom1, mom2)], big_step,
                              "reduce_adamw_big")
    packs_small = _reduce_adamw(parts_small, *[pack_local(s, small_names, small_rows) for s in (given, mom1, mom2)],
                                small_rows, "reduce_adamw_small")

    def unpack(pk, names):
        flat = pk.reshape(-1)
        out, off = {}, 0
        for n in names:
            size = int(np.prod(given[n].shape))
            seg = flat[off:off + size].reshape(local[n].shape)
            out[n] = (seg.T if n in TRANSPOSED else seg).reshape(given[n].shape)
            off += size
        return out, flat[off]

    results = []
    for pb, ps in zip(packs_big, packs_small):
        out, _ = unpack(pb, BIG_SHARDED)
        out_small, tail = unpack(ps, small_names)
        out.update(out_small)
        results.append((out, tail))
    (g_out, loss), (d_out, _), (m_out, _), (v_out, _) = results
    return (loss, grad_x[None], *[g_out[n] for n in WEIGHT_NAMES], *[d_out[n] for n in WEIGHT_NAMES],
            *[m_out[n] for n in WEIGHT_NAMES], *[v_out[n] for n in WEIGHT_NAMES])
```

```python
import functools

import jax
import jax.numpy as jnp
import numpy as np
from jax import lax
from jax.experimental import pallas as pl
from jax.experimental.pallas import tpu as pltpu

F32 = jnp.float32
BF16 = jnp.bfloat16
HI = lax.Precision.HIGHEST

N_DEV = 8
D_MODEL = 1024
CHUNK = 64
RWKV_HEADS = 8
RWKV_HEAD_DIM = 64
RWKV_WIDTH = 512
GDN_HEADS = 4
GDN_HEAD_DIM = 128
GDN_WIDTH = 512
GDN_CONV = 4
FFN_HIDDEN = 2816
FFN_CONV = 3
NORM_EPS = 1e-6
L2_EPS = 1e-6
RWKV_GN_EPS = 64e-5
LANES = 128
SUBLANES = 8
VMEM_LIMIT = 56 * 1024 * 1024

ADAM_LR = 0.001
ADAM_B1 = 0.9
ADAM_B2 = 0.999
ADAM_EPS = 1e-08
ADAM_WD = 0.01
ADAM_STEP = 10

OFF_GA, OFF_GB, OFF_RKV, OFF_QKV, OFF_Z, OFF_LO, OFF_AB, CAT_W = 0, 1024, 2048, 5120, 6656, 7168, 7424, 7680
RWKV_HB = 8
GDN_HB = 4

WEIGHT_NAMES = ['norm1_g', 'w_in', 'rwkv_mu', 'rwkv_w0', 'rwkv_w2', 'rwkv_a0', 'rwkv_a2', 'rwkv_g2', 'rwkv_k_k', 'rwkv_k_a',
                'rwkv_r_k', 'rwkv_ln_w', 'rwkv_ln_b', 'rwkv_proj', 'gdn_conv_w', 'gdn_a_log', 'gdn_dt_bias', 'gdn_norm_w',
                'gdn_proj', 'w_out', 'norm2_g', 'ffn_up', 'ffn_conv_w', 'ffn_down', 'final_g']
BIG_SHARDED = ['w_in', 'ffn_up', 'ffn_down', 'w_out', 'rwkv_proj', 'gdn_proj']
SMALL_SHARDED = ['rwkv_w2', 'rwkv_a2', 'rwkv_g2', 'gdn_conv_w', 'ffn_conv_w']
TRANSPOSED = ('w_in', 'ffn_up')
SHARD_AXIS = {'w_in': 0, 'ffn_up': 0, 'ffn_down': 0, 'w_out': 0, 'rwkv_proj': 1, 'gdn_proj': 1,
              'rwkv_w2': 1, 'rwkv_a2': 1, 'rwkv_g2': 1, 'gdn_conv_w': 1, 'ffn_conv_w': 1}
REPLICATED = [n for n in WEIGHT_NAMES if n not in SHARD_AXIS]
ROW_SHARDED = ['w_in', 'ffn_up', 'ffn_down', 'w_out']


def _cparams(sem=None):
    kw = dict(vmem_limit_bytes=VMEM_LIMIT)
    if sem is not None:
        kw['dimension_semantics'] = sem
    return pltpu.CompilerParams(**kw)


_NN, _NT, _TN = 'nn', 'nt', 'tn'
_DIMS_2D = {'nn': (((1,), (0,)), ((), ())), 'nt': (((1,), (1,)), ((), ())), 'tn': (((0,), (0,)), ((), ()))}
_DIMS_3D = {'nn': (((2,), (1,)), ((0,), (0,))), 'nt': (((2,), (2,)), ((0,), (0,))), 'tn': (((1,), (1,)), ((0,), (0,)))}


def _dg(a, b, kind):
    return lax.dot_general(a, b, (_DIMS_2D if a.ndim == 2 else _DIMS_3D)[kind], preferred_element_type=F32)


def _dot1(a, b, kind):
    return _dg(a.astype(BF16), b.astype(BF16), kind)


@jax.custom_vjp
def _dhi(a, b):
    return _dot1(a, b, _NN)


_dhi.defvjp(lambda a, b: (_dot1(a, b, _NN), (a, b)),
            lambda res, ct: (_dot1(ct, res[1], _NT), _dot1(res[0], ct, _TN)))


@jax.custom_vjp
def _dnt(a, b):
    return _dot1(a, b, _NT)


_dnt.defvjp(lambda a, b: (_dot1(a, b, _NT), (a, b)),
            lambda res, ct: (_dot1(ct, res[1], _NN), _dot1(ct, res[0], _TN)))


@jax.custom_vjp
def _dtn(a, b):
    return _dot1(a, b, _TN)


_dtn.defvjp(lambda a, b: (_dot1(a, b, _TN), (a, b)),
            lambda res, ct: (_dot1(res[1], ct, _NT), _dot1(res[0], ct, _NN)))


def _split3(x):
    x1 = x.astype(BF16)
    r1 = x - x1.astype(F32)
    x2 = r1.astype(BF16)
    return x1, x2, (r1 - x2.astype(F32)).astype(BF16)


def _dot_exact_lhs(sel, x, kind):
    parts = [_dg(sel, xi, kind) for xi in _split3(x)]
    return parts[0] + parts[1] + parts[2]


def _tril_ones(like):
    c = like.shape[-2]
    ri, ci = _iotas(c)
    return jnp.broadcast_to((ri >= ci).astype(BF16), like.shape[:-2] + (c, c))


@jax.custom_vjp
def _cumsum_rows(x):
    return _dot_exact_lhs(_tril_ones(x), x, _NN)


_cumsum_rows.defvjp(lambda x: (_dot_exact_lhs(_tril_ones(x), x, _NN), None),
                    lambda _, ct: (_dot_exact_lhs(_tril_ones(ct), ct, _TN),))


@jax.custom_vjp
def _lane_sum_as_row(x):
    return _dot_exact_lhs(jnp.ones(x.shape, BF16), x, _NT)


def _lane_sum_as_row_bwd(_, ct):
    ones = jnp.ones(ct.shape[:-1] + (LANES,), BF16)
    parts = [_dg(ci, ones, _TN) for ci in _split3(ct)]
    return (parts[0] + parts[1] + parts[2],)


_lane_sum_as_row.defvjp(lambda x: (_dot_exact_lhs(jnp.ones(x.shape, BF16), x, _NT), None), _lane_sum_as_row_bwd)


def _shift_rows(x, halo, s):
    rows = lax.broadcasted_iota(jnp.int32, x.shape, 0)
    out = pltpu.roll(x, s, 0)
    for i in range(s):
        out = jnp.where(rows == i, halo[SUBLANES - s + i:SUBLANES - s + i + 1, :], out)
    return out


def _unshift_rows(g, carry, s):
    c = g.shape[0]
    rows = lax.broadcasted_iota(jnp.int32, g.shape, 0)
    out = pltpu.roll(g, c - s, 0)
    for i in range(s):
        out = jnp.where(rows == c - s + i, carry[i:i + 1, :], out)
    return out


def _sigmoid(z):
    return 1.0 / (1.0 + jnp.exp(-z))


def _silu(z):
    return z * _sigmoid(z)


def _softplus(z):
    return jnp.maximum(z, 0.0) + jnp.log(1.0 + jnp.exp(-jnp.abs(z)))


def _rms(t, gain):
    return t * lax.rsqrt(jnp.mean(t * t, axis=-1, keepdims=True) + NORM_EPS) * gain


def _iotas(c):
    return lax.broadcasted_iota(jnp.int32, (c, c), 0), lax.broadcasted_iota(jnp.int32, (c, c), 1)


def _unit_lower_inverse(xm, eye):
    t = eye + xm
    p = xm
    for _ in range(5):
        p = _dhi(p, p)
        t = t + _dhi(t, p)
    return t


def _rwkv_head(pr, pk, pv, plo, qr, qk, qv, qlo, s0, pp, mulo, wl):
    c = pr.shape[1]
    ri, ci = _iotas(c)

    def mix(p, q, mu):
        return p + (q - p) * mu

    r = mix(pr, qr, pp[:, 0:1])
    k = mix(pk, qk, pp[:, 1:2])
    v = mix(pv, qv, pp[:, 2:3])
    lo = mix(plo, qlo, mulo)
    w0, a0, k_k, k_a, ln_w, ln_b, r_k = (pp[:, i:i + 1] for i in range(3, 10))
    per_head = lambda t: jnp.broadcast_to(t, (pr.shape[0],) + t.shape)
    zw = _dhi(per_head(jnp.tanh(lo)), wl[:, 0])
    za = _dhi(per_head(lo), wl[:, 1])
    g = _dhi(per_head(_sigmoid(lo)), wl[:, 2])
    w_log = -_softplus(-(w0 + zw)) - 0.5
    lw = -jnp.exp(w_log)
    a = _sigmoid(a0 + za)
    kk = k * k_k
    kk = kk * lax.rsqrt(jnp.sum(kk * kk, axis=-1, keepdims=True) + L2_EPS)
    k2 = k * (1.0 + (a - 1.0) * k_a)
    an = -kk
    b = kk * a
    causal = ri >= ci
    strict = ri > ci
    eye = (ri == ci).astype(F32)
    cl = _cumsum_rows(lw)
    ecl = jnp.exp(-cl)
    at = an * jnp.exp(cl - lw)
    bt = b * ecl
    kt = k2 * ecl
    rt = r * jnp.exp(cl)
    a_ab = jnp.where(strict, _dnt(at, bt), 0.0)
    a_ak = jnp.where(strict, _dnt(at, kt), 0.0)
    tinv = _unit_lower_inverse(a_ab, eye)
    u = _dhi(tinv, _dnt(at, s0) + _dhi(a_ak, v))
    y = _dnt(rt, s0) + _dhi(jnp.where(causal, _dnt(rt, bt), 0.0), u) + _dhi(jnp.where(causal, _dnt(rt, kt), 0.0), v)
    cl_end = jnp.sum(lw, axis=1, keepdims=True)
    dec_end = jnp.exp(cl_end - cl)
    s1 = s0 * jnp.exp(cl_end) + _dtn(u, b * dec_end) + _dtn(v, k2 * dec_end)
    m = (lax.broadcasted_iota(jnp.int32, (1, LANES), 1) < RWKV_HEAD_DIM).astype(F32)
    mean = jnp.sum(y, axis=-1, keepdims=True) * (1.0 / RWKV_HEAD_DIM)
    yc = (y - mean) * m
    var = jnp.sum(yc * yc, axis=-1, keepdims=True) * (1.0 / RWKV_HEAD_DIM)
    yn = yc * lax.rsqrt(var + RWKV_GN_EPS) * ln_w + ln_b
    y2 = yn + jnp.sum(r * k2 * r_k, axis=-1, keepdims=True) * v
    return y2 * g, s1


def _gdn_head(xq, xk, xv, z, ab, s0, cw, gp, oha, ohb):
    c = z.shape[1]
    ri, ci = _iotas(c)

    def conv(xs, w):
        out = xs[0] * w[:, GDN_CONV - 1:GDN_CONV]
        for s in range(1, GDN_CONV):
            out = out + xs[s] * w[:, GDN_CONV - 1 - s:GDN_CONV - s]
        return out

    q = _silu(conv(xq, cw[:, 0]))
    k = _silu(conv(xk, cw[:, 1]))
    v = _silu(conv(xv, cw[:, 2]))
    q = q * lax.rsqrt(jnp.sum(q * q, axis=-1, keepdims=True) + L2_EPS) * (GDN_HEAD_DIM ** -0.5)
    k = k * lax.rsqrt(jnp.sum(k * k, axis=-1, keepdims=True) + L2_EPS)
    gg = -jnp.exp(gp[0:1]) * _softplus(ab + gp[1:2])
    beta = jnp.sum(_sigmoid(ab) * ohb, axis=-1, keepdims=True)
    causal = ri >= ci
    strict = ri > ci
    eye = (ri == ci).astype(F32)
    gcm = _cumsum_rows(gg * oha)
    gc = jnp.sum(gcm, axis=-1, keepdims=True)
    gc_row = _lane_sum_as_row(gcm)
    dec = jnp.where(causal, jnp.exp(jnp.where(causal, gc - gc_row, 0.0)), 0.0)
    kb = k * beta
    vb = v * beta
    lm = jnp.where(strict, _dnt(kb, k) * dec, 0.0)
    tinv = _unit_lower_inverse(-lm, eye)
    egc = jnp.exp(gc)
    u = _dhi(tinv, vb)
    wk = _dhi(tinv, kb * egc)
    attn = jnp.where(causal, _dnt(q, k) * dec, 0.0)
    g_last = gc[:, c - 1:c, :]
    v_new = u - _dhi(wk, s0)
    o = _dhi(q * egc, s0) + _dhi(attn, v_new)
    s1 = s0 * jnp.exp(g_last) + _dtn(k * jnp.exp(g_last - gc), v_new)
    return _rms(o, gp[2:3]) * _silu(z), s1


def _head_id(grp, i, per_step, heads):
    return i if per_step == heads else grp * per_step + i


def _head_range(grp, per_step, heads):
    return slice(None) if per_step == heads else pl.ds(grp * per_step, per_step)


def _stack_heads(ref, n, fn=None):
    parts = []
    for i in range(n):
        sl = slice(i * LANES, (i + 1) * LANES)
        v = ref[:, sl]
        parts.append(v if fn is None else fn(v, sl))
    return jnp.stack(parts)


def _prev_rows_spec(width, col):
    per = CHUNK // SUBLANES
    return pl.BlockSpec((SUBLANES, width), lambda n, h: (jnp.maximum(n * per - 1, 0), col(h)))


def _rwkv_specs(nmap):
    hb, groups = RWKV_HB, RWKV_HEADS // RWKV_HB
    cb = OFF_RKV // (hb * LANES)
    specs = []
    for j in range(3):
        specs.append(pl.BlockSpec((CHUNK, hb * LANES), lambda n, g, j=j: (nmap(n), cb + j * groups + g)))
    specs.append(pl.BlockSpec((CHUNK, 2 * LANES), lambda n, g: (nmap(n), OFF_LO // (2 * LANES))))
    per = CHUNK // SUBLANES
    for j in range(3):
        specs.append(pl.BlockSpec((SUBLANES, hb * LANES),
                                  lambda n, g, j=j: (jnp.maximum(nmap(n) * per - 1, 0), cb + j * groups + g)))
    specs.append(pl.BlockSpec((SUBLANES, 2 * LANES), lambda n, g: (jnp.maximum(nmap(n) * per - 1, 0), OFF_LO // (2 * LANES))))
    specs.append(pl.BlockSpec((hb, 16, LANES), lambda n, g: (g, 0, 0)))
    specs.append(pl.BlockSpec((1, 2 * LANES), lambda n, g: (0, 0)))
    specs.append(pl.BlockSpec((hb, 3, 2 * LANES, LANES), lambda n, g: (g, 0, 0, 0)))
    return specs


def _rwkv_forward(p_cat, ppack, mulo, wl):
    t = p_cat.shape[0]
    n_chunks = t // CHUNK

    def body(pr, pk, pv, plo, hr, hk, hv, hlo, pp, ml, w, out, st_out, s_scr):
        n, grp = pl.program_id(0), pl.program_id(1)

        hsl = _head_range(grp, RWKV_HB, RWKV_HEADS)

        @pl.when(n == 0)
        def _():
            s_scr[hsl] = jnp.zeros((RWKV_HB, LANES, LANES), F32)

        live = (n > 0).astype(F32)
        lo = plo[...]
        lo_prev = _shift_rows(lo, hlo[...] * live, 1)
        cur = [_stack_heads(x, RWKV_HB) for x in (pr, pk, pv)]
        prev = [_stack_heads(x, RWKV_HB, lambda v, sl, hx=hx: _shift_rows(v, hx[:, sl] * live, 1))
                for x, hx in ((pr, hr), (pk, hk), (pv, hv))]
        s0 = s_scr[hsl]
        st_out[...] = s0
        o, s1 = _rwkv_head(*cur, lo, *prev, lo_prev, s0, pp[...], ml[...], w[...])
        for i in range(RWKV_HB):
            out[:, i * LANES:(i + 1) * LANES] = o[i].astype(out.dtype)
        s_scr[hsl] = s1

    return pl.pallas_call(
        body, name="rwkv_fwd", grid=(n_chunks, RWKV_HEADS // RWKV_HB),
        in_specs=_rwkv_specs(lambda n: n),
        out_specs=(pl.BlockSpec((CHUNK, RWKV_HB * LANES), lambda n, g: (n, g)),
                   pl.BlockSpec((None, RWKV_HB, LANES, LANES), lambda n, g: (n, g, 0, 0))),
        out_shape=(jax.ShapeDtypeStruct((t, RWKV_HEADS * LANES), BF16),
                   jax.ShapeDtypeStruct((n_chunks, RWKV_HEADS, LANES, LANES), F32)),
        scratch_shapes=[pltpu.VMEM((RWKV_HEADS, LANES, LANES), F32)],
        compiler_params=_cparams(("arbitrary", "arbitrary")),
    )(p_cat, p_cat, p_cat, p_cat, p_cat, p_cat, p_cat, p_cat, ppack, mulo, wl)


def _rwkv_backward(p_cat, ppack, mulo, wl, states, d_out):
    t = p_cat.shape[0]
    n_chunks = t // CHUNK
    last = n_chunks - 1

    def body(pr, pk, pv, plo, hr, hk, hv, hlo, pp, ml, w, st, dy, dpr, dpk, dpv, dplo, dpp, dml, dw, ds_scr, car_scr, carlo_scr):
        n, grp = pl.program_id(0), pl.program_id(1)

        hsl = _head_range(grp, RWKV_HB, RWKV_HEADS)
        gi = _head_id(grp, 0, 1, RWKV_HEADS // RWKV_HB)

        @pl.when(n == 0)
        def _():
            ds_scr[hsl] = jnp.zeros((RWKV_HB, LANES, LANES), F32)
            car_scr[hsl] = jnp.zeros((RWKV_HB, 3 * SUBLANES, LANES), F32)
            carlo_scr[gi] = jnp.zeros((SUBLANES, 2 * LANES), F32)

        @pl.when((n == 0) & (grp == 0))
        def _():
            dpp[...] = jnp.zeros(dpp.shape, F32)
            dml[...] = jnp.zeros(dml.shape, F32)
            dw[...] = jnp.zeros(dw.shape, F32)

        live = (n < last).astype(F32)
        lo = plo[...]
        lo_prev = _shift_rows(lo, hlo[...] * live, 1)
        cur = [_stack_heads(x, RWKV_HB) for x in (pr, pk, pv)]
        prev = [_stack_heads(x, RWKV_HB, lambda v, sl, hx=hx: _shift_rows(v, hx[:, sl] * live, 1))
                for x, hx in ((pr, hr), (pk, hk), (pv, hv))]
        _, vjp = jax.vjp(_rwkv_head, *cur, lo, *prev, lo_prev, st[...], pp[...], ml[...], w[...])
        g = vjp((_stack_heads(dy, RWKV_HB), ds_scr[hsl]))
        outs = (dpr, dpk, dpv)
        for i in range(RWKV_HB):
            sl = slice(i * LANES, (i + 1) * LANES)
            h = _head_id(grp, i, RWKV_HB, RWKV_HEADS)
            car = car_scr[h]
            for j in range(3):
                tot = g[j][i] + _unshift_rows(g[4 + j][i], car[SUBLANES * j:SUBLANES * (j + 1), :], 1)
                outs[j][:, sl] = tot.astype(outs[j].dtype)
                car_scr[h, SUBLANES * j:SUBLANES * (j + 1), :] = g[4 + j][i][0:SUBLANES, :]
        dlo = g[3] + _unshift_rows(g[7], carlo_scr[gi], 1)
        carlo_scr[gi] = g[7][0:SUBLANES, :]
        ds_scr[hsl] = g[8]
        dpp[hsl] += g[9]
        dml[0, 0:1, :] += g[10]
        dw[hsl] += g[11]

        @pl.when(grp == 0)
        def _():
            dplo[...] = dlo

        @pl.when(grp > 0)
        def _():
            dplo[...] += dlo

    rev = lambda n: last - n
    in_specs = _rwkv_specs(rev) + [
        pl.BlockSpec((None, RWKV_HB, LANES, LANES), lambda n, g: (rev(n), g, 0, 0)),
        pl.BlockSpec((CHUNK, RWKV_HB * LANES), lambda n, g: (rev(n), g)),
    ]
    hw = RWKV_HEADS * LANES
    return pl.pallas_call(
        body, name="rwkv_bwd", grid=(n_chunks, RWKV_HEADS // RWKV_HB),
        in_specs=in_specs,
        out_specs=(pl.BlockSpec((CHUNK, RWKV_HB * LANES), lambda n, g: (rev(n), g)),
                   pl.BlockSpec((CHUNK, RWKV_HB * LANES), lambda n, g: (rev(n), g)),
                   pl.BlockSpec((CHUNK, RWKV_HB * LANES), lambda n, g: (rev(n), g)),
                   pl.BlockSpec((CHUNK, 2 * LANES), lambda n, h: (rev(n), 0)),
                   pl.BlockSpec((RWKV_HEADS, 16, LANES), lambda n, h: (0, 0, 0)),
                   pl.BlockSpec((RWKV_HEADS, SUBLANES, 2 * LANES), lambda n, h: (0, 0, 0)),
                   pl.BlockSpec((RWKV_HEADS, 3, 2 * LANES, LANES), lambda n, h: (0, 0, 0, 0))),
        out_shape=(jax.ShapeDtypeStruct((t, hw), BF16), jax.ShapeDtypeStruct((t, hw), BF16), jax.ShapeDtypeStruct((t, hw), BF16),
                   jax.ShapeDtypeStruct((t, 2 * LANES), F32),
                   jax.ShapeDtypeStruct((RWKV_HEADS, 16, LANES), F32),
                   jax.ShapeDtypeStruct((RWKV_HEADS, SUBLANES, 2 * LANES), F32),
                   jax.ShapeDtypeStruct((RWKV_HEADS, 3, 2 * LANES, LANES), F32)),
        scratch_shapes=[pltpu.VMEM((RWKV_HEADS, LANES, LANES), F32),
                        pltpu.VMEM((RWKV_HEADS, 3 * SUBLANES, LANES), F32),
                        pltpu.VMEM((RWKV_HEADS, SUBLANES, 2 * LANES), F32)],
        compiler_params=_cparams(("arbitrary", "arbitrary")),
    )(p_cat, p_cat, p_cat, p_cat, p_cat, p_cat, p_cat, p_cat, ppack, mulo, wl, states, d_out)


def _gdn_specs(nmap):
    cb = OFF_QKV // LANES
    per = CHUNK // SUBLANES
    hb, groups = GDN_HB, GDN_HEADS // GDN_HB
    cb = OFF_QKV // (hb * LANES)
    specs = []
    for j in range(3):
        specs.append(pl.BlockSpec((CHUNK, hb * LANES), lambda n, g, j=j: (nmap(n), cb + j * groups + g)))
    for j in range(3):
        specs.append(pl.BlockSpec((SUBLANES, hb * LANES),
                                  lambda n, g, j=j: (jnp.maximum(nmap(n) * per - 1, 0), cb + j * groups + g)))
    specs.append(pl.BlockSpec((CHUNK, hb * LANES), lambda n, g: (nmap(n), OFF_Z // (hb * LANES) + g)))
    specs.append(pl.BlockSpec((CHUNK, LANES), lambda n, g: (nmap(n), OFF_AB // LANES)))
    specs.append(pl.BlockSpec((hb, 3, SUBLANES, LANES), lambda n, g: (g, 0, 0, 0)))
    specs.append(pl.BlockSpec((SUBLANES, LANES), lambda n, g: (0, 0)))
    return specs


def _conv_taps(x, halo):
    return (x,) + tuple(_shift_rows(x, halo, s) for s in range(1, GDN_CONV))


def _onehots(grp):
    lane = lax.broadcasted_iota(jnp.int32, (GDN_HB, 1, LANES), 2)
    head = lax.broadcasted_iota(jnp.int32, (GDN_HB, 1, LANES), 0) + _head_id(grp, 0, GDN_HB, GDN_HEADS)
    return (lane == head).astype(F32), (lane == GDN_HEADS + head).astype(F32)


def _gdn_taps(refs, halos, live):
    out = []
    for x, hx in zip(refs, halos):
        per_head = [_conv_taps(x[:, i * LANES:(i + 1) * LANES], hx[:, i * LANES:(i + 1) * LANES] * live) for i in range(GDN_HB)]
        out.append(tuple(jnp.stack([per_head[i][s] for i in range(GDN_HB)]) for s in range(GDN_CONV)))
    return out


def _gdn_forward(p_cat, cwpack, gpar):
    t = p_cat.shape[0]
    n_chunks = t // CHUNK

    def body(xq, xk, xv, hq, hk, hv, z, ab, cw, gp, out, st_out, s_scr):
        n, grp = pl.program_id(0), pl.program_id(1)

        hsl = _head_range(grp, GDN_HB, GDN_HEADS)

        @pl.when(n == 0)
        def _():
            s_scr[hsl] = jnp.zeros((GDN_HB, LANES, LANES), F32)

        live = (n > 0).astype(F32)
        oha, ohb = _onehots(grp)
        s0 = s_scr[hsl]
        st_out[...] = s0
        taps = _gdn_taps((xq, xk, xv), (hq, hk, hv), live)
        o, s1 = _gdn_head(*taps, _stack_heads(z, GDN_HB), ab[...], s0, cw[...], gp[...], oha, ohb)
        for i in range(GDN_HB):
            out[:, i * LANES:(i + 1) * LANES] = o[i].astype(out.dtype)
        s_scr[hsl] = s1

    return pl.pallas_call(
        body, name="gdn_fwd", grid=(n_chunks, GDN_HEADS // GDN_HB),
        in_specs=_gdn_specs(lambda n: n),
        out_specs=(pl.BlockSpec((CHUNK, GDN_HB * LANES), lambda n, g: (n, g)),
                   pl.BlockSpec((None, GDN_HB, LANES, LANES), lambda n, g: (n, g, 0, 0))),
        out_shape=(jax.ShapeDtypeStruct((t, GDN_WIDTH), BF16),
                   jax.ShapeDtypeStruct((n_chunks, GDN_HEADS, LANES, LANES), F32)),
        scratch_shapes=[pltpu.VMEM((GDN_HEADS, LANES, LANES), F32)],
        compiler_params=_cparams(("arbitrary", "arbitrary")),
    )(p_cat, p_cat, p_cat, p_cat, p_cat, p_cat, p_cat, p_cat, cwpack, gpar)


def _gdn_backward(p_cat, cwpack, gpar, states, d_out):
    t = p_cat.shape[0]
    n_chunks = t // CHUNK
    last = n_chunks - 1

    def body(xq, xk, xv, hq, hk, hv, z, ab, cw, gp, st, dy, dq, dk, dv, dz, dab, dcw, dgp, ds_scr, car_scr):
        n, grp = pl.program_id(0), pl.program_id(1)

        hsl = _head_range(grp, GDN_HB, GDN_HEADS)

        @pl.when(n == 0)
        def _():
            ds_scr[hsl] = jnp.zeros((GDN_HB, LANES, LANES), F32)
            car_scr[hsl] = jnp.zeros((GDN_HB, 3 * GDN_CONV, SUBLANES, LANES), F32)

        @pl.when((n == 0) & (grp == 0))
        def _():
            dcw[...] = jnp.zeros(dcw.shape, F32)
            dgp[...] = jnp.zeros(dgp.shape, F32)

        live = (n < last).astype(F32)
        oha, ohb = _onehots(grp)
        fn = functools.partial(_gdn_head, oha=oha, ohb=ohb)
        taps = _gdn_taps((xq, xk, xv), (hq, hk, hv), live)
        _, vjp = jax.vjp(fn, *taps, _stack_heads(z, GDN_HB), ab[...], st[...], cw[...], gp[...])
        g = vjp((_stack_heads(dy, GDN_HB), ds_scr[hsl]))
        outs = (dq, dk, dv)
        for i in range(GDN_HB):
            sl = slice(i * LANES, (i + 1) * LANES)
            h = _head_id(grp, i, GDN_HB, GDN_HEADS)
            for j in range(3):
                tot = g[j][0][i]
                for s in range(1, GDN_CONV):
                    slot = j * GDN_CONV + s
                    tot = tot + _unshift_rows(g[j][s][i], car_scr[h, slot], s)
                    car_scr[h, slot] = g[j][s][i][0:SUBLANES, :]
                outs[j][:, sl] = tot.astype(outs[j].dtype)
            dz[:, sl] = g[3][i].astype(dz.dtype)
        dab_sum = g[4]
        ds_scr[hsl] = g[5]
        dcw[hsl] += g[6]
        dgp[0] += g[7]

        @pl.when(grp == 0)
        def _():
            dab[...] = dab_sum

        @pl.when(grp > 0)
        def _():
            dab[...] += dab_sum

    rev = lambda n: last - n
    in_specs = _gdn_specs(rev) + [
        pl.BlockSpec((None, GDN_HB, LANES, LANES), lambda n, g: (rev(n), g, 0, 0)),
        pl.BlockSpec((CHUNK, GDN_HB * LANES), lambda n, g: (rev(n), g)),
    ]
    blk = pl.BlockSpec((CHUNK, GDN_HB * LANES), lambda n, g: (rev(n), g))
    return pl.pallas_call(
        body, name="gdn_bwd", grid=(n_chunks, GDN_HEADS // GDN_HB),
        in_specs=in_specs,
        out_specs=(blk, blk, blk, blk,
                   pl.BlockSpec((CHUNK, LANES), lambda n, h: (rev(n), 0)),
                   pl.BlockSpec((GDN_HEADS, 3, SUBLANES, LANES), lambda n, h: (0, 0, 0, 0)),
                   pl.BlockSpec((GDN_HEADS, SUBLANES, LANES), lambda n, h: (0, 0, 0))),
        out_shape=(jax.ShapeDtypeStruct((t, GDN_WIDTH), BF16), jax.ShapeDtypeStruct((t, GDN_WIDTH), BF16),
                   jax.ShapeDtypeStruct((t, GDN_WIDTH), BF16), jax.ShapeDtypeStruct((t, GDN_WIDTH), BF16),
                   jax.ShapeDtypeStruct((t, LANES), F32),
                   jax.ShapeDtypeStruct((GDN_HEADS, 3, SUBLANES, LANES), F32),
                   jax.ShapeDtypeStruct((GDN_HEADS, SUBLANES, LANES), F32)),
        scratch_shapes=[pltpu.VMEM((GDN_HEADS, LANES, LANES), F32),
                        pltpu.VMEM((GDN_HEADS, 3 * GDN_CONV, SUBLANES, LANES), F32)],
        compiler_params=_cparams(("arbitrary", "arbitrary")),
    )(p_cat, p_cat, p_cat, p_cat, p_cat, p_cat, p_cat, p_cat, cwpack, gpar, states, d_out)


def _pick(n, options):
    for o in options:
        if n % o == 0:
            return o
    raise ValueError(f"no tile for {n}")


_MM_DIMS = {'nn': (((1,), (0,)), ((), ())), 'nt': (((1,), (1,)), ((), ())), 'tn': (((0,), (0,)), ((), ()))}


def _matmul(a, b, mode, out_dtype, name):
    if mode == 'nn':
        (m, k), (k2, n) = a.shape, b.shape
    elif mode == 'nt':
        (m, k), (n, k2) = a.shape, b.shape
    else:
        (k, m), (k2, n) = a.shape, b.shape
    assert k == k2, (a.shape, b.shape, mode)
    tm = _pick(m, (512, 256, 128, 64) if mode == 'tn' else (1024, 512, 256, 128, 64))
    tn = _pick(n, (1024, 512, 256, 128) if mode == 'tn' else (512, 256, 128))
    tk = _pick(k, (1920, 1408, 1024, 512, 256, 128, 64))
    nk = k // tk
    dims = _MM_DIMS[mode]

    def body(a_ref, b_ref, o_ref, acc_ref):
        kk = pl.program_id(2)
        part = lax.dot_general(a_ref[...], b_ref[...], dims, preferred_element_type=F32)
        if nk == 1:
            o_ref[...] = part.astype(o_ref.dtype)
            return

        @pl.when(kk == 0)
        def _():
            acc_ref[...] = part

        @pl.when((kk > 0) & (kk < nk - 1))
        def _():
            acc_ref[...] += part

        @pl.when(kk == nk - 1)
        def _():
            o_ref[...] = (acc_ref[...] + part).astype(o_ref.dtype)

    a_spec = pl.BlockSpec((tk, tm), lambda i, j, kk: (kk, i)) if mode == 'tn' else pl.BlockSpec((tm, tk), lambda i, j, kk: (i, kk))
    b_spec = pl.BlockSpec((tn, tk), lambda i, j, kk: (j, kk)) if mode == 'nt' else pl.BlockSpec((tk, tn), lambda i, j, kk: (kk, j))
    return pl.pallas_call(
        body, name=name, grid=(m // tm, n // tn, nk),
        in_specs=[a_spec, b_spec],
        out_specs=pl.BlockSpec((tm, tn), lambda i, j, kk: (i, j)),
        out_shape=jax.ShapeDtypeStruct((m, n), out_dtype),
        scratch_shapes=[pltpu.VMEM((tm, tn), F32)],
        compiler_params=_cparams(("parallel", "parallel", "arbitrary")),
    )(a, b)


ROW_TILE = 256


def _row_specs(rows, tm):
    return [pl.BlockSpec((tm, w), lambda i, ci=ci: (i, ci)) for (_, w, ci) in rows]


def _rw_forward(fn, rows, pars, outs, name):
    t = rows[0][0].shape[0]
    tm = min(ROW_TILE, t)
    nr, npar = len(rows), len(pars)

    def body(*refs):
        vals = [r[...].astype(F32) for r in refs[:nr]] + [p[...] for p in refs[nr:nr + npar]]
        res = fn(*vals)
        for o, v in zip(refs[nr + npar:], res):
            o[...] = v.astype(o.dtype)

    return pl.pallas_call(
        body, name=name, grid=(t // tm,),
        in_specs=_row_specs(rows, tm) + [pl.BlockSpec(p.shape, lambda i: (0, 0)) for p in pars],
        out_specs=tuple(pl.BlockSpec((tm, w), lambda i: (i, 0)) for (w, _) in outs),
        out_shape=tuple(jax.ShapeDtypeStruct((t, w), dt) for (w, dt) in outs),
        compiler_params=_cparams(("parallel",)),
    )(*[r[0] for r in rows], *pars)


def _rw_backward(fn, rows, pars, cots, drow_dtypes, name):
    t = rows[0][0].shape[0]
    tm = min(ROW_TILE, t)
    nr, npar, nc = len(rows), len(pars), len(cots)
    keep = [i for i, dt in enumerate(drow_dtypes) if dt is not None]

    def body(*refs):
        vals = [r[...].astype(F32) for r in refs[:nr]] + [p[...] for p in refs[nr:nr + npar]]
        cvals = tuple(c[...].astype(F32) for c in refs[nr + npar:nr + npar + nc])
        orefs = refs[nr + npar + nc:]
        _, vjp = jax.vjp(fn, *vals)
        g = vjp(cvals)
        for o, i in zip(orefs[:len(keep)], keep):
            o[...] = g[i].astype(o.dtype)
        first = pl.program_id(0) == 0
        for o, gi in zip(orefs[len(keep):], g[nr:]):
            @pl.when(first)
            def _(o=o, gi=gi):
                o[...] = gi

            @pl.when(jnp.logical_not(first))
            def _(o=o, gi=gi):
                o[...] += gi

    out_specs = [pl.BlockSpec((tm, rows[i][1]), lambda i_: (i_, 0)) for i in keep] + \
                [pl.BlockSpec(p.shape, lambda i_: (0, 0)) for p in pars]
    out_shape = [jax.ShapeDtypeStruct((t, rows[i][1]), drow_dtypes[i]) for i in keep] + \
                [jax.ShapeDtypeStruct(p.shape, F32) for p in pars]
    return pl.pallas_call(
        body, name=name, grid=(t // tm,),
        in_specs=_row_specs(rows, tm) + [pl.BlockSpec(p.shape, lambda i: (0, 0)) for p in pars] + _row_specs(cots, tm),
        out_specs=tuple(out_specs), out_shape=tuple(out_shape),
        compiler_params=_cparams(("arbitrary",)),
    )(*[r[0] for r in rows], *pars, *[c[0] for c in cots])


def _norm_fn(x, g):
    return (_rms(x, g),)


def _norm_skip_fn(x, g):
    return _rms(x, g), x


def _merge_fn(ga, gb, ya, yb):
    return (_sigmoid(ga) * ya + _sigmoid(gb) * yb,)


def _res_norm_fn(x, mo, g):
    x1 = x + mo
    return x1, _rms(x1, g)


def _loss_head(x1, fo, gf, target, name):
    t, d = x1.shape
    tm = min(ROW_TILE, t)

    def tile_loss(x2, g, tgt):
        err = _rms(x2, g) - tgt
        per_row = jnp.sum(err * err, axis=-1, keepdims=True) * (0.5 / d)
        return jnp.sum(per_row, axis=0, keepdims=True)

    def body(x1_ref, fo_ref, g_ref, t_ref, loss_ref, dx_ref, dxb_ref, dg_ref):
        x2 = x1_ref[...] + fo_ref[...]
        val, vjp = jax.vjp(functools.partial(tile_loss, tgt=t_ref[...]), x2, g_ref[...])
        dx2, dg = vjp(jnp.ones((1, 1), F32))
        dx_ref[...] = dx2
        dxb_ref[...] = dx2.astype(BF16)
        first = pl.program_id(0) == 0

        @pl.when(first)
        def _():
            loss_ref[...] = jnp.broadcast_to(val, loss_ref.shape)
            dg_ref[...] = dg

        @pl.when(jnp.logical_not(first))
        def _():
            loss_ref[...] += jnp.broadcast_to(val, loss_ref.shape)
            dg_ref[...] += dg

    row = pl.BlockSpec((tm, d), lambda i: (i, 0))
    vec = pl.BlockSpec((1, d), lambda i: (0, 0))
    return pl.pallas_call(
        body, name=name, grid=(t // tm,),
        in_specs=[row, row, vec, row],
        out_specs=(pl.BlockSpec((1, LANES), lambda i: (0, 0)), row, row, vec),
        out_shape=(jax.ShapeDtypeStruct((1, LANES), F32), jax.ShapeDtypeStruct((t, d), F32),
                   jax.ShapeDtypeStruct((t, d), BF16), jax.ShapeDtypeStruct((1, d), F32)),
        compiler_params=_cparams(("arbitrary",)),
    )(x1, fo, gf, target)


FFN_TILE_ROWS = 512
FFN_TILE_COLS = 256
FFN_COL_BLOCKS = FFN_HIDDEN // FFN_TILE_COLS


def _conv3_past(x, halo, w):
    rows = lax.broadcasted_iota(jnp.int32, x.shape, 0)
    x1 = jnp.where(rows == 0, halo[7:8, :], pltpu.roll(x, 1, 0))
    x2 = jnp.where(rows == 0, halo[6:7, :], jnp.where(rows == 1, halo[7:8, :], pltpu.roll(x, 2, 0)))
    return x * w[2:3] + x1 * w[1:2] + x2 * w[0:1], x1, x2


def _ffn_in_specs(tm, imap, jmap):
    per = tm // SUBLANES
    tile = lambda off: pl.BlockSpec((tm, FFN_TILE_COLS), lambda *g: (imap(*g), off + jmap(*g) % FFN_COL_BLOCKS))
    halo = lambda off: pl.BlockSpec((SUBLANES, FFN_TILE_COLS),
                                    lambda *g: (jnp.maximum(imap(*g) * per - 1, 0), off + jmap(*g) % FFN_COL_BLOCKS))
    wsp = lambda off: pl.BlockSpec((FFN_CONV, FFN_TILE_COLS), lambda *g: (0, off + jmap(*g) % FFN_COL_BLOCKS))
    return [tile(0), halo(0), wsp(0), tile(FFN_COL_BLOCKS), halo(FFN_COL_BLOCKS), wsp(FFN_COL_BLOCKS)]


def _ffn_act_forward(hpre, cw):
    t = hpre.shape[0]
    tm = min(FFN_TILE_ROWS, t)

    def body(hg, pg, wg, hu, pu, wu, out):
        live = (pl.program_id(0) > 0).astype(F32)
        cg, _, _ = _conv3_past(hg[...], pg[...] * live, wg[...])
        cu, _, _ = _conv3_past(hu[...], pu[...] * live, wu[...])
        out[...] = (_silu(cg) * cu).astype(out.dtype)

    return pl.pallas_call(
        body, name="ffn_act_fwd", grid=(t // tm, FFN_COL_BLOCKS),
        in_specs=_ffn_in_specs(tm, lambda i, j: i, lambda i, j: j),
        out_specs=pl.BlockSpec((tm, FFN_TILE_COLS), lambda i, j: (i, j)),
        out_shape=jax.ShapeDtypeStruct((t, FFN_HIDDEN), BF16),
        compiler_params=_cparams(("parallel", "parallel")),
    )(hpre, hpre, cw, hpre, hpre, cw)


def _ffn_act_backward(hpre, cw, dact):
    t = hpre.shape[0]
    tm = min(FFN_TILE_ROWS, t)

    def body(hg, pg, wg, hu, pu, wu, da, dconv, dw):
        j, i = pl.program_id(0), pl.program_id(1)
        live = (i > 0).astype(F32)
        cg, g1, g2 = _conv3_past(hg[...], pg[...] * live, wg[...])
        cu, u1, u2 = _conv3_past(hu[...], pu[...] * live, wu[...])
        s = _sigmoid(cg)
        d = da[...]
        d_gate = d * cu * s * (1.0 + cg * (1.0 - s))
        d_up = d * cg * s
        is_gate = j < FFN_COL_BLOCKS
        dc = jnp.where(is_gate, d_gate, d_up)
        dconv[...] = dc
        taps = (jnp.where(is_gate, g2, u2), jnp.where(is_gate, g1, u1), jnp.where(is_gate, hg[...], hu[...]))
        sums = [jnp.sum(xs * dc, axis=0, keepdims=True) for xs in taps]

        @pl.when(i == 0)
        def _():
            for r_ in range(FFN_CONV):
                dw[r_:r_ + 1, :] = sums[r_]

        @pl.when(i > 0)
        def _():
            for r_ in range(FFN_CONV):
                dw[r_:r_ + 1, :] += sums[r_]

    return pl.pallas_call(
        body, name="ffn_act_bwd", grid=(2 * FFN_COL_BLOCKS, t // tm),
        in_specs=_ffn_in_specs(tm, lambda j, i: i, lambda j, i: j) +
        [pl.BlockSpec((tm, FFN_TILE_COLS), lambda j, i: (i, j % FFN_COL_BLOCKS))],
        out_specs=(pl.BlockSpec((tm, FFN_TILE_COLS), lambda j, i: (i, j)),
                   pl.BlockSpec((FFN_CONV, FFN_TILE_COLS), lambda j, i: (0, j))),
        out_shape=(jax.ShapeDtypeStruct((t, 2 * FFN_HIDDEN), F32), jax.ShapeDtypeStruct((FFN_CONV, 2 * FFN_HIDDEN), F32)),
        compiler_params=_cparams(("parallel", "arbitrary")),
    )(hpre, hpre, cw, hpre, hpre, cw, dact)


def _ffn_conv_backward(dconv, cw):
    t = dconv.shape[0]
    tm = min(FFN_TILE_ROWS, t)
    n_tiles = t // tm
    per = tm // SUBLANES

    def body(d_ref, nx_ref, w_ref, out):
        live = (pl.program_id(0) < n_tiles - 1).astype(F32)
        d = d_ref[...]
        nx = nx_ref[...] * live
        w = w_ref[...]
        rows = lax.broadcasted_iota(jnp.int32, d.shape, 0)
        d1 = jnp.where(rows == tm - 1, nx[0:1, :], pltpu.roll(d, tm - 1, 0))
        d2 = jnp.where(rows == tm - 1, nx[1:2, :], jnp.where(rows == tm - 2, nx[0:1, :], pltpu.roll(d, tm - 2, 0)))
        out[...] = (d * w[2:3] + d1 * w[1:2] + d2 * w[0:1]).astype(out.dtype)

    return pl.pallas_call(
        body, name="ffn_conv_bwd", grid=(n_tiles, 2 * FFN_COL_BLOCKS),
        in_specs=[pl.BlockSpec((tm, FFN_TILE_COLS), lambda i, j: (i, j)),
                  pl.BlockSpec((SUBLANES, FFN_TILE_COLS), lambda i, j: (jnp.minimum((i + 1) * per, t // SUBLANES - 1), j)),
                  pl.BlockSpec((FFN_CONV, FFN_TILE_COLS), lambda i, j: (0, j))],
        out_specs=pl.BlockSpec((tm, FFN_TILE_COLS), lambda i, j: (i, j)),
        out_shape=jax.ShapeDtypeStruct((t, 2 * FFN_HIDDEN), BF16),
        compiler_params=_cparams(("parallel", "parallel")),
    )(dconv, dconv, cw)


def _my_place():
    x, y, c = lax.axis_index("x"), lax.axis_index("y"), lax.axis_index("c")
    return x, y, c, 4 * x + 2 * y + c


N_CHIPS = 4


def _remote(src, dst, send_sem, recv_sem, dev):
    return pltpu.make_async_remote_copy(src_ref=src, dst_ref=dst, send_sem=send_sem, recv_sem=recv_sem, device_id=dev,
                                        device_id_type=pl.DeviceIdType.MESH)


def _chip_peer(x, y, k):
    return x ^ ((k >> 1) & 1), y ^ (k & 1)


def _all_gather_two_level(shard, name):
    r, w = shard.shape

    def body(src, out, send_sems, recv_sems, local_sem):
        x, y, c, me = _my_place()
        sibling = (x, y, 1 - c)
        mine = pltpu.make_async_copy(src, out.at[me], local_sem)
        mine.start()
        first = [_remote(src, out.at[me], send_sems.at[0], recv_sems.at[0], sibling)]
        for k in range(1, N_CHIPS):
            px, py = _chip_peer(x, y, k)
            first.append(_remote(src, out.at[me], send_sems.at[k], recv_sems.at[k], (px, py, c)))
        for cp in first:
            cp.start()
        passed = []
        for k in range(1, N_CHIPS):
            px, py = _chip_peer(x, y, k)
            landed = out.at[me ^ (2 * k)]
            _remote(src, landed, send_sems.at[k], recv_sems.at[k], (px, py, c)).wait_recv()
            fwd = _remote(landed, landed, send_sems.at[N_CHIPS - 1 + k], recv_sems.at[N_CHIPS - 1 + k], sibling)
            fwd.start()
            passed.append(fwd)
        _remote(src, out.at[me ^ 1], send_sems.at[0], recv_sems.at[0], sibling).wait_recv()
        for k in range(1, N_CHIPS):
            got = out.at[(me ^ 1) ^ (2 * k)]
            _remote(got, got, send_sems.at[N_CHIPS - 1 + k], recv_sems.at[N_CHIPS - 1 + k], sibling).wait_recv()
        for cp in first + passed:
            cp.wait_send()
        mine.wait()

    return pl.pallas_call(
        body, name=name,
        in_specs=[pl.BlockSpec(memory_space=pl.ANY)],
        out_specs=pl.BlockSpec(memory_space=pl.ANY),
        out_shape=jax.ShapeDtypeStruct((N_DEV, r, w), shard.dtype),
        scratch_shapes=[pltpu.SemaphoreType.DMA((N_DEV - 1,)), pltpu.SemaphoreType.DMA((N_DEV - 1,)), pltpu.SemaphoreType.DMA],
    )(shard)


def _pair_exchange(arrays, name):
    na = len(arrays)

    def body(*refs):
        srcs, dsts, (send_sems, recv_sems) = refs[:na], refs[na:2 * na], refs[2 * na:]
        x, y, c, _ = _my_place()
        sibling = (x, y, 1 - c)
        copies = []
        for i in range(na):
            for q in range(N_CHIPS):
                s = i * N_CHIPS + q
                copies.append(_remote(srcs[i].at[2 * q + 1 - c], dsts[i].at[q], send_sems.at[s], recv_sems.at[s], sibling))
        for cp in copies:
            cp.start()
        for cp in copies:
            cp.wait_recv()
        for cp in copies:
            cp.wait_send()

    hbm = pl.BlockSpec(memory_space=pl.ANY)
    return pl.pallas_call(
        body, name=name, in_specs=[hbm] * na, out_specs=tuple([hbm] * na),
        out_shape=tuple(jax.ShapeDtypeStruct((N_CHIPS,) + a.shape[1:], a.dtype) for a in arrays),
        scratch_shapes=[pltpu.SemaphoreType.DMA((na * N_CHIPS,)), pltpu.SemaphoreType.DMA((na * N_CHIPS,))],
    )(*arrays)


ELEMENTWISE_COLS = 256


def _pair_sum(slabs, recv, core, out_dtype, name):
    _, r, w = slabs.shape
    tc = ELEMENTWISE_COLS

    def body(core_ref, mine, theirs, out):
        out[...] = (mine[...] + theirs[...]).astype(out.dtype)

    grid_spec = pltpu.PrefetchScalarGridSpec(
        num_scalar_prefetch=1, grid=(N_CHIPS, w // tc),
        in_specs=[pl.BlockSpec((None, r, tc), lambda q, j, core_ref: (2 * q + core_ref[0], 0, j)),
                  pl.BlockSpec((None, r, tc), lambda q, j, core_ref: (q, 0, j))],
        out_specs=pl.BlockSpec((None, r, tc), lambda q, j, core_ref: (q, 0, j)))
    return pl.pallas_call(body, name=name, grid_spec=grid_spec,
                          out_shape=jax.ShapeDtypeStruct((N_CHIPS, r, w), out_dtype),
                          compiler_params=_cparams(("parallel", "parallel")))(core, slabs, recv)


def _chip_exchange(arrays, name):
    na = len(arrays)

    def body(*refs):
        srcs, dsts, (send_sems, recv_sems, local_sems) = refs[:na], refs[na:2 * na], refs[2 * na:]
        x, y, c, _ = _my_place()
        chip = 2 * x + y
        own = [pltpu.make_async_copy(srcs[i].at[chip], dsts[i].at[chip], local_sems.at[i]) for i in range(na)]
        for cp in own:
            cp.start()
        sends, arrivals = [], []
        for i in range(na):
            for k in range(1, N_CHIPS):
                px, py = _chip_peer(x, y, k)
                s = i * N_CHIPS + k
                sends.append(_remote(srcs[i].at[chip ^ k], dsts[i].at[chip], send_sems.at[s], recv_sems.at[s], (px, py, c)))
                arrivals.append(_remote(srcs[i].at[chip], dsts[i].at[chip ^ k], send_sems.at[s], recv_sems.at[s], (px, py, c)))
        for cp in sends:
            cp.start()
        for cp in arrivals:
            cp.wait_recv()
        for cp in sends:
            cp.wait_send()
        for cp in own:
            cp.wait()

    hbm = pl.BlockSpec(memory_space=pl.ANY)
    return pl.pallas_call(
        body, name=name, in_specs=[hbm] * na, out_specs=tuple([hbm] * na),
        out_shape=tuple(jax.ShapeDtypeStruct(a.shape, a.dtype) for a in arrays),
        scratch_shapes=[pltpu.SemaphoreType.DMA((na * N_CHIPS,)), pltpu.SemaphoreType.DMA((na * N_CHIPS,)),
                        pltpu.SemaphoreType.DMA((na,))],
    )(*arrays)


def _reduce_adamw(parts, w, m, v, name):
    n_parts, r, wd = parts.shape
    tc = ELEMENTWISE_COLS
    c1 = 1.0 / (1.0 - ADAM_B1 ** ADAM_STEP)
    c2 = 1.0 / (1.0 - ADAM_B2 ** ADAM_STEP)

    def body(p_ref, w_ref, m_ref, v_ref, g_out, d_out, m_out, v_out):
        g = p_ref[0].astype(F32)
        for s in range(1, n_parts):
            g = g + p_ref[s].astype(F32)
        mn = ADAM_B1 * m_ref[...] + (1.0 - ADAM_B1) * g
        vn = ADAM_B2 * v_ref[...] + (1.0 - ADAM_B2) * (g * g)
        g_out[...] = g
        m_out[...] = mn
        v_out[...] = vn
        d_out[...] = -ADAM_LR * ((mn * c1) / (jnp.sqrt(vn * c2) + ADAM_EPS) + ADAM_WD * w_ref[...])

    blk = pl.BlockSpec((r, tc), lambda j: (0, j))
    shp = jax.ShapeDtypeStruct((r, wd), F32)
    return pl.pallas_call(
        body, name=name, grid=(wd // tc,),
        in_specs=[pl.BlockSpec((n_parts, r, tc), lambda j: (0, 0, j)), blk, blk, blk],
        out_specs=(blk, blk, blk, blk), out_shape=(shp, shp, shp, shp),
        compiler_params=_cparams(("parallel",)),
    )(parts, w, m, v)


PACK_W = 1024


def _pad_heads(a, slots):
    lead = a.shape[:-1]
    a = a.reshape(lead + (slots, RWKV_HEAD_DIM))
    a = jnp.pad(a, [(0, 0)] * (len(lead) + 1) + [(0, LANES - RWKV_HEAD_DIM)])
    return a.reshape(lead + (slots * LANES,))


def _unpad_heads(a, slots):
    lead = a.shape[:-1]
    return a.reshape(lead + (slots, LANES))[..., :RWKV_HEAD_DIM].reshape(lead + (slots * RWKV_HEAD_DIM,))


def _flat_pack(arrs, dtype, row_mult):
    flat = jnp.concatenate([a.reshape(-1).astype(dtype) for a in arrs])
    n = flat.shape[0]
    rows = -(-n // PACK_W)
    rows = -(-rows // row_mult) * row_mult
    return jnp.pad(flat, (0, rows * PACK_W - n)).reshape(rows, PACK_W)


def _unpack_gathered(g, names, shard_shapes):
    flat = g.reshape(N_DEV, -1)
    out, off = {}, 0
    for n in names:
        s = shard_shapes[n]
        size = s[0] * s[1]
        seg = flat[:, off:off + size].reshape((N_DEV,) + tuple(s))
        off += size
        if SHARD_AXIS[n] == 1:
            out[n] = jnp.transpose(seg, (1, 0, 2)).reshape(s[0], N_DEV * s[1])
        else:
            out[n] = seg.reshape(N_DEV * s[0], s[1])
    return out


def _shard_major(full, axis):
    a, b = full.shape
    if axis == 1:
        return jnp.transpose(full.reshape(a, N_DEV, b // N_DEV), (1, 0, 2)).reshape(N_DEV, -1)
    return full.reshape(N_DEV, -1)


def _prepare_weights(full, rep):
    w = full['w_in']
    d = w.shape[1]
    rkv = jnp.pad(w[0:1536].reshape(3 * RWKV_HEADS, RWKV_HEAD_DIM, d), ((0, 0), (0, LANES - RWKV_HEAD_DIM), (0, 0)))
    w_cat = jnp.concatenate([
        w[3848:4872], w[4872:5896], rkv.reshape(3 * RWKV_HEADS * LANES, d), w[1792:3328], w[3328:3840],
        w[1536:1792], jnp.pad(w[3840:3848], ((0, LANES - 8), (0, 0))), jnp.zeros((LANES, d), w.dtype)], axis=0)
    assert w_cat.shape[0] == CAT_W
    rp = full['rwkv_proj']
    rproj = jnp.pad(rp.reshape(RWKV_HEADS, RWKV_HEAD_DIM, -1), ((0, 0), (0, LANES - RWKV_HEAD_DIM), (0, 0))).reshape(RWKV_HEADS * LANES, -1)
    mu = rep['rwkv_mu']
    vecs = [mu[0:512], mu[512:1024], mu[1024:1536], rep['rwkv_w0'], rep['rwkv_a0'], rep['rwkv_k_k'], rep['rwkv_k_a'],
            rep['rwkv_ln_w'], rep['rwkv_ln_b'], rep['rwkv_r_k'].reshape(-1)]
    ppack = jnp.stack([jnp.pad(v.reshape(RWKV_HEADS, RWKV_HEAD_DIM), ((0, 0), (0, LANES - RWKV_HEAD_DIM))) for v in vecs], axis=1)
    ppack = jnp.pad(ppack, ((0, 0), (0, 16 - len(vecs)), (0, 0)))
    mulo = mu[1536:1792].reshape(1, 2 * LANES)
    wl = jnp.zeros((3, 2 * LANES, RWKV_HEADS * LANES), F32)
    wl = wl.at[0, 0:64].set(_pad_heads(full['rwkv_w2'], RWKV_HEADS))
    wl = wl.at[1, 64:128].set(_pad_heads(full['rwkv_a2'], RWKV_HEADS))
    wl = wl.at[2, 128:256].set(_pad_heads(full['rwkv_g2'], RWKV_HEADS))
    wl = jnp.transpose(wl.reshape(3, 2 * LANES, RWKV_HEADS, LANES), (2, 0, 1, 3))
    cw = full['gdn_conv_w'].reshape(GDN_CONV, 3, GDN_HEADS, LANES)
    cwpack = jnp.pad(jnp.transpose(cw, (2, 1, 0, 3)), ((0, 0), (0, 0), (0, SUBLANES - GDN_CONV), (0, 0)))
    gpar = jnp.zeros((SUBLANES, LANES), F32)
    gpar = gpar.at[0, 0:GDN_HEADS].set(rep['gdn_a_log']).at[1, 0:GDN_HEADS].set(rep['gdn_dt_bias']).at[2].set(rep['gdn_norm_w'])
    return dict(w_cat=w_cat, rproj=rproj, gproj=full['gdn_proj'], w_out=full['w_out'], ffn_up=full['ffn_up'],
                ffn_down=full['ffn_down'], ffn_cw=full['ffn_conv_w'], ppack=ppack, mulo=mulo, wl=wl, cwpack=cwpack, gpar=gpar,
                g1=rep['norm1_g'].reshape(1, -1), g2=rep['norm2_g'].reshape(1, -1), gf=rep['final_g'].reshape(1, -1))


def _local_step(x, target, p):
    d = x.shape[1]
    full_w = lambda a: (a, a.shape[1], 0)
    (u,) = _rw_forward(_norm_fn, [full_w(x)], [p['g1']], [(d, BF16)], "norm1")
    p_cat = _matmul(u, p['w_cat'], 'nt', F32, "proj_in")
    ya_pre, st_r = _rwkv_forward(p_cat, p['ppack'], p['mulo'], p['wl'])
    yb_pre, st_g = _gdn_forward(p_cat, p['cwpack'], p['gpar'])
    ya = _matmul(ya_pre, p['rproj'], 'nn', F32, "rwkv_proj")
    yb = _matmul(yb_pre, p['gproj'], 'nn', F32, "gdn_proj")
    gates = [(p_cat, d, OFF_GA // d), (p_cat, d, OFF_GB // d)]
    (mixed,) = _rw_forward(_merge_fn, gates + [full_w(ya), full_w(yb)], [], [(d, BF16)], "merge")
    mo = _matmul(mixed, p['w_out'], 'nn', F32, "out_proj")
    x1, n2 = _rw_forward(_res_norm_fn, [full_w(x), full_w(mo)], [p['g2']], [(d, F32), (d, BF16)], "res_norm2")
    hpre = _matmul(n2, p['ffn_up'], 'nt', F32, "ffn_up")
    act = _ffn_act_forward(hpre, p['ffn_cw'])
    fo = _matmul(act, p['ffn_down'], 'nn', F32, "ffn_down")
    loss_vec, dx2, dx2b, dgf = _loss_head(x1, fo, p['gf'], target, "loss_head")

    dact = _matmul(dx2b, p['ffn_down'], 'nt', F32, "d_act")
    dw_down = _matmul(act, dx2b, 'tn', F32, "dw_ffn_down")
    dconv, dcw_f = _ffn_act_backward(hpre, p['ffn_cw'], dact)
    dh = _ffn_conv_backward(dconv, p['ffn_cw'])
    dn2 = _matmul(dh, p['ffn_up'], 'nn', F32, "d_norm2")
    dw_up = _matmul(dh, n2, 'tn', F32, "dw_ffn_up")
    dx1, dx1b, dg2 = _rw_backward(_res_norm_fn, [full_w(x), full_w(mo)], [p['g2']], [full_w(dx2), full_w(dn2)],
                                  [F32, BF16], "res_norm2_bwd")
    dmixed = _matmul(dx1b, p['w_out'], 'nt', F32, "d_mixed")
    dw_out = _matmul(mixed, dx1b, 'tn', F32, "dw_out")
    dga, dgb, dya, dyb = _rw_backward(_merge_fn, gates + [full_w(ya), full_w(yb)], [], [full_w(dmixed)],
                                      [BF16, BF16, BF16, BF16], "merge_bwd")
    d_ya_pre = _matmul(dya, p['rproj'], 'nt', F32, "d_rwkv_out")
    dw_rproj = _matmul(ya_pre, dya, 'tn', F32, "dw_rwkv_proj")
    d_yb_pre = _matmul(dyb, p['gproj'], 'nt', F32, "d_gdn_out")
    dw_gproj = _matmul(yb_pre, dyb, 'tn', F32, "dw_gdn_proj")
    dpr, dpk, dpv, dplo, dpp, dml, dwl = _rwkv_backward(p_cat, p['ppack'], p['mulo'], p['wl'], st_r, d_ya_pre)
    dq, dk, dv, dz, dab, dcw_g, dgp = _gdn_backward(p_cat, p['cwpack'], p['gpar'], st_g, d_yb_pre)
    t = x.shape[0]
    dp_cat = jnp.concatenate([dga, dgb, dpr, dpk, dpv, dq, dk, dv, dz, dplo.astype(BF16), dab.astype(BF16),
                              jnp.zeros((t, LANES), BF16)], axis=1)
    du = _matmul(dp_cat, p['w_cat'], 'nn', F32, "d_norm1")
    dw_cat = _matmul(dp_cat, u, 'tn', F32, "dw_in")
    grad_x, dg1 = _rw_backward(_norm_skip_fn, [full_w(x)], [p['g1']], [full_w(du), full_w(dx1)], [F32], "norm1_bwd")

    heads = lambda row: dpp[:, row, :RWKV_HEAD_DIM].reshape(-1)
    lora = lambda j, lo_, hi_: jnp.transpose(dwl[:, j, lo_:hi_, :RWKV_HEAD_DIM], (1, 0, 2)).reshape(hi_ - lo_, RWKV_WIDTH)
    grads = {
        'norm1_g': dg1[0],
        'w_in': jnp.concatenate([dw_cat[OFF_RKV:OFF_QKV].reshape(3 * RWKV_HEADS, LANES, d)[:, :RWKV_HEAD_DIM].reshape(-1, d),
                                 dw_cat[OFF_LO:OFF_AB], dw_cat[OFF_QKV:OFF_Z], dw_cat[OFF_Z:OFF_LO], dw_cat[OFF_AB:OFF_AB + 8],
                                 dw_cat[OFF_GA:OFF_GB], dw_cat[OFF_GB:OFF_RKV]], axis=0),
        'rwkv_mu': jnp.concatenate([heads(0), heads(1), heads(2), jnp.sum(dml[:, 0, :], axis=0)]),
        'rwkv_w0': heads(3), 'rwkv_a0': heads(4), 'rwkv_k_k': heads(5), 'rwkv_k_a': heads(6),
        'rwkv_ln_w': heads(7), 'rwkv_ln_b': heads(8), 'rwkv_r_k': heads(9).reshape(RWKV_HEADS, RWKV_HEAD_DIM),
        'rwkv_w2': lora(0, 0, 64), 'rwkv_a2': lora(1, 64, 128), 'rwkv_g2': lora(2, 128, 256),
        'rwkv_proj': dw_rproj.reshape(RWKV_HEADS, LANES, -1)[:, :RWKV_HEAD_DIM].reshape(RWKV_WIDTH, -1),
        'gdn_conv_w': jnp.transpose(dcw_g[:, :, :GDN_CONV, :], (2, 1, 0, 3)).reshape(GDN_CONV, 3 * GDN_WIDTH),
        'gdn_a_log': jnp.sum(dgp[:, 0, :GDN_HEADS], axis=0), 'gdn_dt_bias': jnp.sum(dgp[:, 1, :GDN_HEADS], axis=0),
        'gdn_norm_w': jnp.sum(dgp[:, 2, :], axis=0),
        'gdn_proj': dw_gproj, 'w_out': dw_out, 'norm2_g': dg2[0], 'ffn_up': dw_up, 'ffn_conv_w': dcw_f,
        'ffn_down': dw_down, 'final_g': dgf[0],
    }
    return loss_vec, grad_x, grads


def kernel(x, norm1_g, w_in, rwkv_mu, rwkv_w0, rwkv_w2, rwkv_a0, rwkv_a2, rwkv_g2, rwkv_k_k, rwkv_k_a, rwkv_r_k, rwkv_ln_w, rwkv_ln_b, rwkv_proj, gdn_conv_w, gdn_a_log, gdn_dt_bias, gdn_norm_w, gdn_proj, w_out, norm2_g, ffn_up, ffn_conv_w, ffn_down, final_g, loss_target, m_norm1_g, m_w_in, m_rwkv_mu, m_rwkv_w0, m_rwkv_w2, m_rwkv_a0, m_rwkv_a2, m_rwkv_g2, m_rwkv_k_k, m_rwkv_k_a, m_rwkv_r_k, m_rwkv_ln_w, m_rwkv_ln_b, m_rwkv_proj, m_gdn_conv_w, m_gdn_a_log, m_gdn_dt_bias, m_gdn_norm_w, m_gdn_proj, m_w_out, m_norm2_g, m_ffn_up, m_ffn_conv_w, m_ffn_down, m_final_g, v_norm1_g, v_w_in, v_rwkv_mu, v_rwkv_w0, v_rwkv_w2, v_rwkv_a0, v_rwkv_a2, v_rwkv_g2, v_rwkv_k_k, v_rwkv_k_a, v_rwkv_r_k, v_rwkv_ln_w, v_rwkv_ln_b, v_rwkv_proj, v_gdn_conv_w, v_gdn_a_log, v_gdn_dt_bias, v_gdn_norm_w, v_gdn_proj, v_w_out, v_norm2_g, v_ffn_up, v_ffn_conv_w, v_ffn_down, v_final_g):
    given = dict(zip(WEIGHT_NAMES, (norm1_g, w_in, rwkv_mu, rwkv_w0, rwkv_w2, rwkv_a0, rwkv_a2, rwkv_g2, rwkv_k_k, rwkv_k_a, rwkv_r_k,
                                    rwkv_ln_w, rwkv_ln_b, rwkv_proj, gdn_conv_w, gdn_a_log, gdn_dt_bias, gdn_norm_w, gdn_proj, w_out,
                                    norm2_g, ffn_up, ffn_conv_w, ffn_down, final_g)))
    mom1 = dict(zip(WEIGHT_NAMES, (m_norm1_g, m_w_in, m_rwkv_mu, m_rwkv_w0, m_rwkv_w2, m_rwkv_a0, m_rwkv_a2, m_rwkv_g2, m_rwkv_k_k,
                                   m_rwkv_k_a, m_rwkv_r_k, m_rwkv_ln_w, m_rwkv_ln_b, m_rwkv_proj, m_gdn_conv_w, m_gdn_a_log,
                                   m_gdn_dt_bias, m_gdn_norm_w, m_gdn_proj, m_w_out, m_norm2_g, m_ffn_up, m_ffn_conv_w, m_ffn_down,
                                   m_final_g)))
    mom2 = dict(zip(WEIGHT_NAMES, (v_norm1_g, v_w_in, v_rwkv_mu, v_rwkv_w0, v_rwkv_w2, v_rwkv_a0, v_rwkv_a2, v_rwkv_g2, v_rwkv_k_k,
                                   v_rwkv_k_a, v_rwkv_r_k, v_rwkv_ln_w, v_rwkv_ln_b, v_rwkv_proj, v_gdn_conv_w, v_gdn_a_log,
                                   v_gdn_dt_bias, v_gdn_norm_w, v_gdn_proj, v_w_out, v_norm2_g, v_ffn_up, v_ffn_conv_w, v_ffn_down,
                                   v_final_g)))
    def strip(n, a):
        a = a if n == 'final_g' else a.reshape(a.shape[1:])
        return a.T if n in TRANSPOSED else a

    local = {n: strip(n, a) for n, a in given.items()}
    shard_shapes = {n: local[n].shape for n in SHARD_AXIS}
    sharded = BIG_SHARDED + SMALL_SHARDED

    g_big = _all_gather_two_level(_flat_pack([local[n] for n in BIG_SHARDED], BF16, 16), "gather_big")
    g_small = _all_gather_two_level(_flat_pack([local[n] for n in SMALL_SHARDED], F32, SUBLANES), "gather_small")
    full = _unpack_gathered(g_big, BIG_SHARDED, shard_shapes)
    full.update(_unpack_gathered(g_small, SMALL_SHARDED, shard_shapes))
    rep = {n: local[n] for n in REPLICATED}

    loss_vec, grad_x, grads = _local_step(x[0], loss_target[0], _prepare_weights(full, rep))

    small_sharded = ['rwkv_proj', 'gdn_proj'] + SMALL_SHARDED
    small_names = small_sharded + REPLICATED
    rep_vec = jnp.concatenate([grads[n].reshape(-1) for n in REPLICATED] + [loss_vec[0, 0:1]])
    slabs = [grads[n].reshape(N_DEV, -1, grads[n].shape[1]) for n in ROW_SHARDED]
    slab_small = jnp.concatenate([_shard_major(grads[n], SHARD_AXIS[n]) for n in small_sharded] +
                                 [jnp.broadcast_to(rep_vec[None], (N_DEV, rep_vec.shape[0]))], axis=1)
    small_rows = -(-slab_small.shape[1] // (PACK_W * SUBLANES)) * SUBLANES
    slab_small = jnp.pad(slab_small, ((0, 0), (0, small_rows * PACK_W - slab_small.shape[1]))).reshape(N_DEV, small_rows, PACK_W)
    core = lax.axis_index("c").astype(jnp.int32).reshape(1)
    from_sibling = _pair_exchange(slabs + [slab_small], "grad_pair_exchange")
    chip_parts = [_pair_sum(s, r, core, BF16, "grad_pair_sum_" + n) for s, r, n in zip(slabs, from_sibling, ROW_SHARDED)]
    chip_parts.append(_pair_sum(slab_small, from_sibling[-1], core, F32, "grad_pair_sum_small"))
    parts = _chip_exchange(chip_parts, "grad_chip_exchange")

    def pack_local(src):
        flat = jnp.concatenate([strip(n, src[n]).reshape(-1) for n in small_names])
        return jnp.pad(flat, (0, small_rows * PACK_W - flat.shape[0])).reshape(small_rows, PACK_W)

    results = [({}, None) for _ in range(4)]
    for part, n in zip(parts, ROW_SHARDED):
        packs = _reduce_adamw(part, local[n], strip(n, mom1[n]), strip(n, mom2[n]), "adamw_" + n)
        for (out, _), pk in zip(results, packs):
            out[n] = (pk.T if n in TRANSPOSED else pk).reshape(given[n].shape)
    packs = _reduce_adamw(parts[-1], pack_local(given), pack_local(mom1), pack_local(mom2), "adamw_small")
    for i, pk in enumerate(packs):
        flat, off = pk.reshape(-1), 0
        for n in small_names:
            size = int(np.prod(given[n].shape))
            results[i][0][n] = flat[off:off + size].reshape(given[n].shape)
            off += size
        results[i] = (results[i][0], flat[off])
    (g_out, loss), (d_out, _), (m_out, _), (v_out, _) = results
    return (loss, grad_x[None], *[g_out[n] for n in WEIGHT_NAMES], *[d_out[n] for n in WEIGHT_NAMES],
            *[m_out[n] for n in WEIGHT_NAMES], *[v_out[n] for n in WEIGHT_NAMES])
```

```python
import functools

import jax
import jax.numpy as jnp
import numpy as np
from jax import lax
from jax.experimental import pallas as pl
from jax.experimental.pallas import tpu as pltpu

F32 = jnp.float32
BF16 = jnp.bfloat16
HI = lax.Precision.HIGHEST

N_DEV = 8
D_MODEL = 1024
CHUNK = 64
RWKV_HEADS = 8
RWKV_HEAD_DIM = 64
RWKV_WIDTH = 512
GDN_HEADS = 4
GDN_HEAD_DIM = 128
GDN_WIDTH = 512
GDN_CONV = 4
FFN_HIDDEN = 2816
FFN_CONV = 3
NORM_EPS = 1e-6
L2_EPS = 1e-6
RWKV_GN_EPS = 64e-5
LANES = 128
SUBLANES = 8
VMEM_LIMIT = 56 * 1024 * 1024

ADAM_LR = 0.001
ADAM_B1 = 0.9
ADAM_B2 = 0.999
ADAM_EPS = 1e-08
ADAM_WD = 0.01
ADAM_STEP = 10

OFF_GA, OFF_GB, OFF_RKV, OFF_QKV, OFF_Z, OFF_LO, OFF_AB, CAT_W = 0, 1024, 2048, 5120, 6656, 7168, 7424, 7680
RWKV_HB = 8
GDN_HB = 4

WEIGHT_NAMES = ['norm1_g', 'w_in', 'rwkv_mu', 'rwkv_w0', 'rwkv_w2', 'rwkv_a0', 'rwkv_a2', 'rwkv_g2', 'rwkv_k_k', 'rwkv_k_a',
                'rwkv_r_k', 'rwkv_ln_w', 'rwkv_ln_b', 'rwkv_proj', 'gdn_conv_w', 'gdn_a_log', 'gdn_dt_bias', 'gdn_norm_w',
                'gdn_proj', 'w_out', 'norm2_g', 'ffn_up', 'ffn_conv_w', 'ffn_down', 'final_g']
BIG_SHARDED = ['w_in', 'ffn_up', 'ffn_down', 'w_out', 'rwkv_proj', 'gdn_proj']
SMALL_SHARDED = ['rwkv_w2', 'rwkv_a2', 'rwkv_g2', 'gdn_conv_w', 'ffn_conv_w']
TRANSPOSED = ('w_in', 'ffn_up')
SHARD_AXIS = {'w_in': 0, 'ffn_up': 0, 'ffn_down': 0, 'w_out': 0, 'rwkv_proj': 1, 'gdn_proj': 1,
              'rwkv_w2': 1, 'rwkv_a2': 1, 'rwkv_g2': 1, 'gdn_conv_w': 1, 'ffn_conv_w': 1}
REPLICATED = [n for n in WEIGHT_NAMES if n not in SHARD_AXIS]
ROW_SHARDED = ['w_in', 'ffn_up', 'ffn_down', 'w_out']


def _cparams(sem=None):
    kw = dict(vmem_limit_bytes=VMEM_LIMIT)
    if sem is not None:
        kw['dimension_semantics'] = sem
    return pltpu.CompilerParams(**kw)


_NN, _NT, _TN = 'nn', 'nt', 'tn'
_DIMS_2D = {'nn': (((1,), (0,)), ((), ())), 'nt': (((1,), (1,)), ((), ())), 'tn': (((0,), (0,)), ((), ()))}
_DIMS_3D = {'nn': (((2,), (1,)), ((0,), (0,))), 'nt': (((2,), (2,)), ((0,), (0,))), 'tn': (((1,), (1,)), ((0,), (0,)))}


def _dg(a, b, kind):
    return lax.dot_general(a, b, (_DIMS_2D if a.ndim == 2 else _DIMS_3D)[kind], preferred_element_type=F32)


def _dot1(a, b, kind):
    return _dg(a.astype(BF16), b.astype(BF16), kind)


@jax.custom_vjp
def _dhi(a, b):
    return _dot1(a, b, _NN)


_dhi.defvjp(lambda a, b: (_dot1(a, b, _NN), (a, b)),
            lambda res, ct: (_dot1(ct, res[1], _NT), _dot1(res[0], ct, _TN)))


@jax.custom_vjp
def _dnt(a, b):
    return _dot1(a, b, _NT)


_dnt.defvjp(lambda a, b: (_dot1(a, b, _NT), (a, b)),
            lambda res, ct: (_dot1(ct, res[1], _NN), _dot1(ct, res[0], _TN)))


@jax.custom_vjp
def _dtn(a, b):
    return _dot1(a, b, _TN)


_dtn.defvjp(lambda a, b: (_dot1(a, b, _TN), (a, b)),
            lambda res, ct: (_dot1(res[1], ct, _NT), _dot1(res[0], ct, _NN)))


def _split3(x):
    x1 = x.astype(BF16)
    r1 = x - x1.astype(F32)
    x2 = r1.astype(BF16)
    return x1, x2, (r1 - x2.astype(F32)).astype(BF16)


def _dot_exact_lhs(sel, x, kind):
    parts = [_dg(sel, xi, kind) for xi in _split3(x)]
    return parts[0] + parts[1] + parts[2]


def _tril_ones(like):
    c = like.shape[-2]
    ri, ci = _iotas(c)
    return jnp.broadcast_to((ri >= ci).astype(BF16), like.shape[:-2] + (c, c))


@jax.custom_vjp
def _cumsum_rows(x):
    return _dot_exact_lhs(_tril_ones(x), x, _NN)


_cumsum_rows.defvjp(lambda x: (_dot_exact_lhs(_tril_ones(x), x, _NN), None),
                    lambda _, ct: (_dot_exact_lhs(_tril_ones(ct), ct, _TN),))


@jax.custom_vjp
def _lane_sum_as_row(x):
    return _dot_exact_lhs(jnp.ones(x.shape, BF16), x, _NT)


def _lane_sum_as_row_bwd(_, ct):
    ones = jnp.ones(ct.shape[:-1] + (LANES,), BF16)
    parts = [_dg(ci, ones, _TN) for ci in _split3(ct)]
    return (parts[0] + parts[1] + parts[2],)


_lane_sum_as_row.defvjp(lambda x: (_dot_exact_lhs(jnp.ones(x.shape, BF16), x, _NT), None), _lane_sum_as_row_bwd)


def _shift_rows(x, halo, s):
    rows = lax.broadcasted_iota(jnp.int32, x.shape, 0)
    out = pltpu.roll(x, s, 0)
    for i in range(s):
        out = jnp.where(rows == i, halo[SUBLANES - s + i:SUBLANES - s + i + 1, :], out)
    return out


def _unshift_rows(g, carry, s):
    c = g.shape[0]
    rows = lax.broadcasted_iota(jnp.int32, g.shape, 0)
    out = pltpu.roll(g, c - s, 0)
    for i in range(s):
        out = jnp.where(rows == c - s + i, carry[i:i + 1, :], out)
    return out


def _sigmoid(z):
    return 1.0 / (1.0 + jnp.exp(-z))


def _silu(z):
    return z * _sigmoid(z)


def _softplus(z):
    return jnp.maximum(z, 0.0) + jnp.log(1.0 + jnp.exp(-jnp.abs(z)))


def _rms(t, gain):
    return t * lax.rsqrt(jnp.mean(t * t, axis=-1, keepdims=True) + NORM_EPS) * gain


def _iotas(c):
    return lax.broadcasted_iota(jnp.int32, (c, c), 0), lax.broadcasted_iota(jnp.int32, (c, c), 1)


def _unit_lower_inverse(xm, eye):
    t = eye + xm
    p = xm
    for _ in range(5):
        p = _dhi(p, p)
        t = t + _dhi(t, p)
    return t


def _rwkv_head(pr, pk, pv, plo, qr, qk, qv, qlo, s0, pp, mulo, wl):
    c = pr.shape[1]
    ri, ci = _iotas(c)

    def mix(p, q, mu):
        return p + (q - p) * mu

    r = mix(pr, qr, pp[:, 0:1])
    k = mix(pk, qk, pp[:, 1:2])
    v = mix(pv, qv, pp[:, 2:3])
    lo = mix(plo, qlo, mulo)
    w0, a0, k_k, k_a, ln_w, ln_b, r_k = (pp[:, i:i + 1] for i in range(3, 10))
    per_head = lambda t: jnp.broadcast_to(t, (pr.shape[0],) + t.shape)
    zw = _dhi(per_head(jnp.tanh(lo)), wl[:, 0])
    za = _dhi(per_head(lo), wl[:, 1])
    g = _dhi(per_head(_sigmoid(lo)), wl[:, 2])
    w_log = -_softplus(-(w0 + zw)) - 0.5
    lw = -jnp.exp(w_log)
    a = _sigmoid(a0 + za)
    kk = k * k_k
    kk = kk * lax.rsqrt(jnp.sum(kk * kk, axis=-1, keepdims=True) + L2_EPS)
    k2 = k * (1.0 + (a - 1.0) * k_a)
    an = -kk
    b = kk * a
    causal = ri >= ci
    strict = ri > ci
    eye = (ri == ci).astype(F32)
    cl = _cumsum_rows(lw)
    ecl = jnp.exp(-cl)
    at = an * jnp.exp(cl - lw)
    bt = b * ecl
    kt = k2 * ecl
    rt = r * jnp.exp(cl)
    a_ab = jnp.where(strict, _dnt(at, bt), 0.0)
    a_ak = jnp.where(strict, _dnt(at, kt), 0.0)
    tinv = _unit_lower_inverse(a_ab, eye)
    u = _dhi(tinv, _dnt(at, s0) + _dhi(a_ak, v))
    y = _dnt(rt, s0) + _dhi(jnp.where(causal, _dnt(rt, bt), 0.0), u) + _dhi(jnp.where(causal, _dnt(rt, kt), 0.0), v)
    cl_end = jnp.sum(lw, axis=1, keepdims=True)
    dec_end = jnp.exp(cl_end - cl)
    s1 = s0 * jnp.exp(cl_end) + _dtn(u, b * dec_end) + _dtn(v, k2 * dec_end)
    m = (lax.broadcasted_iota(jnp.int32, (1, LANES), 1) < RWKV_HEAD_DIM).astype(F32)
    mean = jnp.sum(y, axis=-1, keepdims=True) * (1.0 / RWKV_HEAD_DIM)
    yc = (y - mean) * m
    var = jnp.sum(yc * yc, axis=-1, keepdims=True) * (1.0 / RWKV_HEAD_DIM)
    yn = yc * lax.rsqrt(var + RWKV_GN_EPS) * ln_w + ln_b
    y2 = yn + jnp.sum(r * k2 * r_k, axis=-1, keepdims=True) * v
    return y2 * g, s1


def _gdn_head(xq, xk, xv, z, ab, s0, cw, gp, oha, ohb):
    c = z.shape[1]
    ri, ci = _iotas(c)

    def conv(xs, w):
        out = xs[0] * w[:, GDN_CONV - 1:GDN_CONV]
        for s in range(1, GDN_CONV):
            out = out + xs[s] * w[:, GDN_CONV - 1 - s:GDN_CONV - s]
        return out

    q = _silu(conv(xq, cw[:, 0]))
    k = _silu(conv(xk, cw[:, 1]))
    v = _silu(conv(xv, cw[:, 2]))
    q = q * lax.rsqrt(jnp.sum(q * q, axis=-1, keepdims=True) + L2_EPS) * (GDN_HEAD_DIM ** -0.5)
    k = k * lax.rsqrt(jnp.sum(k * k, axis=-1, keepdims=True) + L2_EPS)
    gg = -jnp.exp(gp[0:1]) * _softplus(ab + gp[1:2])
    beta = jnp.sum(_sigmoid(ab) * ohb, axis=-1, keepdims=True)
    causal = ri >= ci
    strict = ri > ci
    eye = (ri == ci).astype(F32)
    gcm = _cumsum_rows(gg * oha)
    gc = jnp.sum(gcm, axis=-1, keepdims=True)
    gc_row = _lane_sum_as_row(gcm)
    dec = jnp.where(causal, jnp.exp(jnp.where(causal, gc - gc_row, 0.0)), 0.0)
    kb = k * beta
    vb = v * beta
    lm = jnp.where(strict, _dnt(kb, k) * dec, 0.0)
    tinv = _unit_lower_inverse(-lm, eye)
    egc = jnp.exp(gc)
    u = _dhi(tinv, vb)
    wk = _dhi(tinv, kb * egc)
    attn = jnp.where(causal, _dnt(q, k) * dec, 0.0)
    g_last = gc[:, c - 1:c, :]
    v_new = u - _dhi(wk, s0)
    o = _dhi(q * egc, s0) + _dhi(attn, v_new)
    s1 = s0 * jnp.exp(g_last) + _dtn(k * jnp.exp(g_last - gc), v_new)
    return _rms(o, gp[2:3]) * _silu(z), s1


def _head_id(grp, i, per_step, heads):
    return i if per_step == heads else grp * per_step + i


def _head_range(grp, per_step, heads):
    return slice(None) if per_step == heads else pl.ds(grp * per_step, per_step)


def _stack_heads(ref, n, fn=None):
    parts = []
    for i in range(n):
        sl = slice(i * LANES, (i + 1) * LANES)
        v = ref[:, sl]
        parts.append(v if fn is None else fn(v, sl))
    return jnp.stack(parts)


def _prev_rows_spec(width, col):
    per = CHUNK // SUBLANES
    return pl.BlockSpec((SUBLANES, width), lambda n, h: (jnp.maximum(n * per - 1, 0), col(h)))


def _rwkv_specs(nmap):
    hb, groups = RWKV_HB, RWKV_HEADS // RWKV_HB
    cb = OFF_RKV // (hb * LANES)
    specs = []
    for j in range(3):
        specs.append(pl.BlockSpec((CHUNK, hb * LANES), lambda n, g, j=j: (nmap(n), cb + j * groups + g)))
    specs.append(pl.BlockSpec((CHUNK, 2 * LANES), lambda n, g: (nmap(n), OFF_LO // (2 * LANES))))
    per = CHUNK // SUBLANES
    for j in range(3):
        specs.append(pl.BlockSpec((SUBLANES, hb * LANES),
                                  lambda n, g, j=j: (jnp.maximum(nmap(n) * per - 1, 0), cb + j * groups + g)))
    specs.append(pl.BlockSpec((SUBLANES, 2 * LANES), lambda n, g: (jnp.maximum(nmap(n) * per - 1, 0), OFF_LO // (2 * LANES))))
    specs.append(pl.BlockSpec((hb, 16, LANES), lambda n, g: (g, 0, 0)))
    specs.append(pl.BlockSpec((1, 2 * LANES), lambda n, g: (0, 0)))
    specs.append(pl.BlockSpec((hb, 3, 2 * LANES, LANES), lambda n, g: (g, 0, 0, 0)))
    return specs


def _rwkv_forward(p_cat, ppack, mulo, wl):
    t = p_cat.shape[0]
    n_chunks = t // CHUNK

    def body(pr, pk, pv, plo, hr, hk, hv, hlo, pp, ml, w, out, st_out, s_scr):
        n, grp = pl.program_id(0), pl.program_id(1)

        hsl = _head_range(grp, RWKV_HB, RWKV_HEADS)

        @pl.when(n == 0)
        def _():
            s_scr[hsl] = jnp.zeros((RWKV_HB, LANES, LANES), F32)

        live = (n > 0).astype(F32)
        lo = plo[...]
        lo_prev = _shift_rows(lo, hlo[...] * live, 1)
        cur = [_stack_heads(x, RWKV_HB) for x in (pr, pk, pv)]
        prev = [_stack_heads(x, RWKV_HB, lambda v, sl, hx=hx: _shift_rows(v, hx[:, sl] * live, 1))
                for x, hx in ((pr, hr), (pk, hk), (pv, hv))]
        s0 = s_scr[hsl]
        st_out[...] = s0
        o, s1 = _rwkv_head(*cur, lo, *prev, lo_prev, s0, pp[...], ml[...], w[...])
        for i in range(RWKV_HB):
            out[:, i * LANES:(i + 1) * LANES] = o[i].astype(out.dtype)
        s_scr[hsl] = s1

    return pl.pallas_call(
        body, name="rwkv_fwd", grid=(n_chunks, RWKV_HEADS // RWKV_HB),
        in_specs=_rwkv_specs(lambda n: n),
        out_specs=(pl.BlockSpec((CHUNK, RWKV_HB * LANES), lambda n, g: (n, g)),
                   pl.BlockSpec((None, RWKV_HB, LANES, LANES), lambda n, g: (n, g, 0, 0))),
        out_shape=(jax.ShapeDtypeStruct((t, RWKV_HEADS * LANES), BF16),
                   jax.ShapeDtypeStruct((n_chunks, RWKV_HEADS, LANES, LANES), F32)),
        scratch_shapes=[pltpu.VMEM((RWKV_HEADS, LANES, LANES), F32)],
        compiler_params=_cparams(("arbitrary", "arbitrary")),
    )(p_cat, p_cat, p_cat, p_cat, p_cat, p_cat, p_cat, p_cat, ppack, mulo, wl)


def _rwkv_backward(p_cat, ppack, mulo, wl, states, d_out):
    t = p_cat.shape[0]
    n_chunks = t // CHUNK
    last = n_chunks - 1

    def body(pr, pk, pv, plo, hr, hk, hv, hlo, pp, ml, w, st, dy, dpr, dpk, dpv, dplo, dpp, dml, dw, ds_scr, car_scr, carlo_scr):
        n, grp = pl.program_id(0), pl.program_id(1)

        hsl = _head_range(grp, RWKV_HB, RWKV_HEADS)
        gi = _head_id(grp, 0, 1, RWKV_HEADS // RWKV_HB)

        @pl.when(n == 0)
        def _():
            ds_scr[hsl] = jnp.zeros((RWKV_HB, LANES, LANES), F32)
            car_scr[hsl] = jnp.zeros((RWKV_HB, 3 * SUBLANES, LANES), F32)
            carlo_scr[gi] = jnp.zeros((SUBLANES, 2 * LANES), F32)

        @pl.when((n == 0) & (grp == 0))
        def _():
            dpp[...] = jnp.zeros(dpp.shape, F32)
            dml[...] = jnp.zeros(dml.shape, F32)
            dw[...] = jnp.zeros(dw.shape, F32)

        live = (n < last).astype(F32)
        lo = plo[...]
        lo_prev = _shift_rows(lo, hlo[...] * live, 1)
        cur = [_stack_heads(x, RWKV_HB) for x in (pr, pk, pv)]
        prev = [_stack_heads(x, RWKV_HB, lambda v, sl, hx=hx: _shift_rows(v, hx[:, sl] * live, 1))
                for x, hx in ((pr, hr), (pk, hk), (pv, hv))]
        _, vjp = jax.vjp(_rwkv_head, *cur, lo, *prev, lo_prev, st[...], pp[...], ml[...], w[...])
        g = vjp((_stack_heads(dy, RWKV_HB), ds_scr[hsl]))
        outs = (dpr, dpk, dpv)
        for i in range(RWKV_HB):
            sl = slice(i * LANES, (i + 1) * LANES)
            h = _head_id(grp, i, RWKV_HB, RWKV_HEADS)
            car = car_scr[h]
            for j in range(3):
                tot = g[j][i] + _unshift_rows(g[4 + j][i], car[SUBLANES * j:SUBLANES * (j + 1), :], 1)
                outs[j][:, sl] = tot.astype(outs[j].dtype)
                car_scr[h, SUBLANES * j:SUBLANES * (j + 1), :] = g[4 + j][i][0:SUBLANES, :]
        dlo = g[3] + _unshift_rows(g[7], carlo_scr[gi], 1)
        carlo_scr[gi] = g[7][0:SUBLANES, :]
        ds_scr[hsl] = g[8]
        dpp[hsl] += g[9]
        dml[0, 0:1, :] += g[10]
        dw[hsl] += g[11]

        @pl.when(grp == 0)
        def _():
            dplo[...] = dlo

        @pl.when(grp > 0)
        def _():
            dplo[...] += dlo

    rev = lambda n: last - n
    in_specs = _rwkv_specs(rev) + [
        pl.BlockSpec((None, RWKV_HB, LANES, LANES), lambda n, g: (rev(n), g, 0, 0)),
        pl.BlockSpec((CHUNK, RWKV_HB * LANES), lambda n, g: (rev(n), g)),
    ]
    hw = RWKV_HEADS * LANES
    return pl.pallas_call(
        body, name="rwkv_bwd", grid=(n_chunks, RWKV_HEADS // RWKV_HB),
        in_specs=in_specs,
        out_specs=(pl.BlockSpec((CHUNK, RWKV_HB * LANES), lambda n, g: (rev(n), g)),
                   pl.BlockSpec((CHUNK, RWKV_HB * LANES), lambda n, g: (rev(n), g)),
                   pl.BlockSpec((CHUNK, RWKV_HB * LANES), lambda n, g: (rev(n), g)),
                   pl.BlockSpec((CHUNK, 2 * LANES), lambda n, h: (rev(n), 0)),
                   pl.BlockSpec((RWKV_HEADS, 16, LANES), lambda n, h: (0, 0, 0)),
                   pl.BlockSpec((RWKV_HEADS, SUBLANES, 2 * LANES), lambda n, h: (0, 0, 0)),
                   pl.BlockSpec((RWKV_HEADS, 3, 2 * LANES, LANES), lambda n, h: (0, 0, 0, 0))),
        out_shape=(jax.ShapeDtypeStruct((t, hw), BF16), jax.ShapeDtypeStruct((t, hw), BF16), jax.ShapeDtypeStruct((t, hw), BF16),
                   jax.ShapeDtypeStruct((t, 2 * LANES), F32),
                   jax.ShapeDtypeStruct((RWKV_HEADS, 16, LANES), F32),
                   jax.ShapeDtypeStruct((RWKV_HEADS, SUBLANES, 2 * LANES), F32),
                   jax.ShapeDtypeStruct((RWKV_HEADS, 3, 2 * LANES, LANES), F32)),
        scratch_shapes=[pltpu.VMEM((RWKV_HEADS, LANES, LANES), F32),
                        pltpu.VMEM((RWKV_HEADS, 3 * SUBLANES, LANES), F32),
                        pltpu.VMEM((RWKV_HEADS, SUBLANES, 2 * LANES), F32)],
        compiler_params=_cparams(("arbitrary", "arbitrary")),
    )(p_cat, p_cat, p_cat, p_cat, p_cat, p_cat, p_cat, p_cat, ppack, mulo, wl, states, d_out)


def _gdn_specs(nmap):
    cb = OFF_QKV // LANES
    per = CHUNK // SUBLANES
    hb, groups = GDN_HB, GDN_HEADS // GDN_HB
    cb = OFF_QKV // (hb * LANES)
    specs = []
    for j in range(3):
        specs.append(pl.BlockSpec((CHUNK, hb * LANES), lambda n, g, j=j: (nmap(n), cb + j * groups + g)))
    for j in range(3):
        specs.append(pl.BlockSpec((SUBLANES, hb * LANES),
                                  lambda n, g, j=j: (jnp.maximum(nmap(n) * per - 1, 0), cb + j * groups + g)))
    specs.append(pl.BlockSpec((CHUNK, hb * LANES), lambda n, g: (nmap(n), OFF_Z // (hb * LANES) + g)))
    specs.append(pl.BlockSpec((CHUNK, LANES), lambda n, g: (nmap(n), OFF_AB // LANES)))
    specs.append(pl.BlockSpec((hb, 3, SUBLANES, LANES), lambda n, g: (g, 0, 0, 0)))
    specs.append(pl.BlockSpec((SUBLANES, LANES), lambda n, g: (0, 0)))
    return specs


def _conv_taps(x, halo):
    return (x,) + tuple(_shift_rows(x, halo, s) for s in range(1, GDN_CONV))


def _onehots(grp):
    lane = lax.broadcasted_iota(jnp.int32, (GDN_HB, 1, LANES), 2)
    head = lax.broadcasted_iota(jnp.int32, (GDN_HB, 1, LANES), 0) + _head_id(grp, 0, GDN_HB, GDN_HEADS)
    return (lane == head).astype(F32), (lane == GDN_HEADS + head).astype(F32)


def _gdn_taps(refs, halos, live):
    out = []
    for x, hx in zip(refs, halos):
        per_head = [_conv_taps(x[:, i * LANES:(i + 1) * LANES], hx[:, i * LANES:(i + 1) * LANES] * live) for i in range(GDN_HB)]
        out.append(tuple(jnp.stack([per_head[i][s] for i in range(GDN_HB)]) for s in range(GDN_CONV)))
    return out


def _gdn_forward(p_cat, cwpack, gpar):
    t = p_cat.shape[0]
    n_chunks = t // CHUNK

    def body(xq, xk, xv, hq, hk, hv, z, ab, cw, gp, out, st_out, s_scr):
        n, grp = pl.program_id(0), pl.program_id(1)

        hsl = _head_range(grp, GDN_HB, GDN_HEADS)

        @pl.when(n == 0)
        def _():
            s_scr[hsl] = jnp.zeros((GDN_HB, LANES, LANES), F32)

        live = (n > 0).astype(F32)
        oha, ohb = _onehots(grp)
        s0 = s_scr[hsl]
        st_out[...] = s0
        taps = _gdn_taps((xq, xk, xv), (hq, hk, hv), live)
        o, s1 = _gdn_head(*taps, _stack_heads(z, GDN_HB), ab[...], s0, cw[...], gp[...], oha, ohb)
        for i in range(GDN_HB):
            out[:, i * LANES:(i + 1) * LANES] = o[i].astype(out.dtype)
        s_scr[hsl] = s1

    return pl.pallas_call(
        body, name="gdn_fwd", grid=(n_chunks, GDN_HEADS // GDN_HB),
        in_specs=_gdn_specs(lambda n: n),
        out_specs=(pl.BlockSpec((CHUNK, GDN_HB * LANES), lambda n, g: (n, g)),
                   pl.BlockSpec((None, GDN_HB, LANES, LANES), lambda n, g: (n, g, 0, 0))),
        out_shape=(jax.ShapeDtypeStruct((t, GDN_WIDTH), BF16),
                   jax.ShapeDtypeStruct((n_chunks, GDN_HEADS, LANES, LANES), F32)),
        scratch_shapes=[pltpu.VMEM((GDN_HEADS, LANES, LANES), F32)],
        compiler_params=_cparams(("arbitrary", "arbitrary")),
    )(p_cat, p_cat, p_cat, p_cat, p_cat, p_cat, p_cat, p_cat, cwpack, gpar)


def _gdn_backward(p_cat, cwpack, gpar, states, d_out):
    t = p_cat.shape[0]
    n_chunks = t // CHUNK
    last = n_chunks - 1

    def body(xq, xk, xv, hq, hk, hv, z, ab, cw, gp, st, dy, dq, dk, dv, dz, dab, dcw, dgp, ds_scr, car_scr):
        n, grp = pl.program_id(0), pl.program_id(1)

        hsl = _head_range(grp, GDN_HB, GDN_HEADS)

        @pl.when(n == 0)
        def _():
            ds_scr[hsl] = jnp.zeros((GDN_HB, LANES, LANES), F32)
            car_scr[hsl] = jnp.zeros((GDN_HB, 3 * GDN_CONV, SUBLANES, LANES), F32)

        @pl.when((n == 0) & (grp == 0))
        def _():
            dcw[...] = jnp.zeros(dcw.shape, F32)
            dgp[...] = jnp.zeros(dgp.shape, F32)

        live = (n < last).astype(F32)
        oha, ohb = _onehots(grp)
        fn = functools.partial(_gdn_head, oha=oha, ohb=ohb)
        taps = _gdn_taps((xq, xk, xv), (hq, hk, hv), live)
        _, vjp = jax.vjp(fn, *taps, _stack_heads(z, GDN_HB), ab[...], st[...], cw[...], gp[...])
        g = vjp((_stack_heads(dy, GDN_HB), ds_scr[hsl]))
        outs = (dq, dk, dv)
        for i in range(GDN_HB):
            sl = slice(i * LANES, (i + 1) * LANES)
            h = _head_id(grp, i, GDN_HB, GDN_HEADS)
            for j in range(3):
                tot = g[j][0][i]
                for s in range(1, GDN_CONV):
                    slot = j * GDN_CONV + s
                    tot = tot + _unshift_rows(g[j][s][i], car_scr[h, slot], s)
                    car_scr[h, slot] = g[j][s][i][0:SUBLANES, :]
                outs[j][:, sl] = tot.astype(outs[j].dtype)
            dz[:, sl] = g[3][i].astype(dz.dtype)
        dab_sum = g[4]
        ds_scr[hsl] = g[5]
        dcw[hsl] += g[6]
        dgp[0] += g[7]

        @pl.when(grp == 0)
        def _():
            dab[...] = dab_sum

        @pl.when(grp > 0)
        def _():
            dab[...] += dab_sum

    rev = lambda n: last - n
    in_specs = _gdn_specs(rev) + [
        pl.BlockSpec((None, GDN_HB, LANES, LANES), lambda n, g: (rev(n), g, 0, 0)),
        pl.BlockSpec((CHUNK, GDN_HB * LANES), lambda n, g: (rev(n), g)),
    ]
    blk = pl.BlockSpec((CHUNK, GDN_HB * LANES), lambda n, g: (rev(n), g))
    return pl.pallas_call(
        body, name="gdn_bwd", grid=(n_chunks, GDN_HEADS // GDN_HB),
        in_specs=in_specs,
        out_specs=(blk, blk, blk, blk,
                   pl.BlockSpec((CHUNK, LANES), lambda n, h: (rev(n), 0)),
                   pl.BlockSpec((GDN_HEADS, 3, SUBLANES, LANES), lambda n, h: (0, 0, 0, 0)),
                   pl.BlockSpec((GDN_HEADS, SUBLANES, LANES), lambda n, h: (0, 0, 0))),
        out_shape=(jax.ShapeDtypeStruct((t, GDN_WIDTH), BF16), jax.ShapeDtypeStruct((t, GDN_WIDTH), BF16),
                   jax.ShapeDtypeStruct((t, GDN_WIDTH), BF16), jax.ShapeDtypeStruct((t, GDN_WIDTH), BF16),
                   jax.ShapeDtypeStruct((t, LANES), F32),
                   jax.ShapeDtypeStruct((GDN_HEADS, 3, SUBLANES, LANES), F32),
                   jax.ShapeDtypeStruct((GDN_HEADS, SUBLANES, LANES), F32)),
        scratch_shapes=[pltpu.VMEM((GDN_HEADS, LANES, LANES), F32),
                        pltpu.VMEM((GDN_HEADS, 3 * GDN_CONV, SUBLANES, LANES), F32)],
        compiler_params=_cparams(("arbitrary", "arbitrary")),
    )(p_cat, p_cat, p_cat, p_cat, p_cat, p_cat, p_cat, p_cat, cwpack, gpar, states, d_out)


def _pick(n, options):
    for o in options:
        if n % o == 0:
            return o
    raise ValueError(f"no tile for {n}")


MM_VMEM_BUDGET = 30 * 1024 * 1024
MM_MIN_STEPS = 8


def _mm_tiles(mode, m, n, k, out_bytes):
    tms = [t for t in (2048, 1024, 768, 512, 256, 128, 64) if m % t == 0 and (mode != 'tn' or t % LANES == 0)]
    tns = [t for t in (1408, 1024, 768, 512, 256, 128) if n % t == 0]
    tks = [t for t in (2048, 1920, 1408, 1024, 512, 256, 128, 64) if k % t == 0]
    best, best_key = None, None
    for tm in tms:
        for tn in tns:
            for tk in tks:
                nk = k // tk
                vmem = 2 * (tm * tk * 2 + tk * tn * 2 + tm * tn * out_bytes) + (tm * tn * 4 if nk > 1 else 0)
                steps = (m // tm) * (n // tn) * nk
                if vmem > MM_VMEM_BUDGET:
                    continue
                key = (steps >= MM_MIN_STEPS, tn if mode == 'tn' else 0, tm * tn * tk, -nk)
                if best_key is None or key > best_key:
                    best, best_key = (tm, tn, tk), key
    if best is None:
        raise ValueError(f"no matmul tile for {mode} {m}x{n}x{k}")
    return best


_MM_DIMS = {'nn': (((1,), (0,)), ((), ())), 'nt': (((1,), (1,)), ((), ())), 'tn': (((0,), (0,)), ((), ()))}


def _matmul(a, b, mode, out_dtype, name):
    if mode == 'nn':
        (m, k), (k2, n) = a.shape, b.shape
    elif mode == 'nt':
        (m, k), (n, k2) = a.shape, b.shape
    else:
        (k, m), (k2, n) = a.shape, b.shape
    assert k == k2, (a.shape, b.shape, mode)
    tm, tn, tk = _mm_tiles(mode, m, n, k, jnp.dtype(out_dtype).itemsize)
    nk = k // tk
    dims = _MM_DIMS[mode]

    def body(a_ref, b_ref, o_ref, acc_ref):
        kk = pl.program_id(2)
        part = lax.dot_general(a_ref[...], b_ref[...], dims, preferred_element_type=F32)
        if nk == 1:
            o_ref[...] = part.astype(o_ref.dtype)
            return

        @pl.when(kk == 0)
        def _():
            acc_ref[...] = part

        @pl.when((kk > 0) & (kk < nk - 1))
        def _():
            acc_ref[...] += part

        @pl.when(kk == nk - 1)
        def _():
            o_ref[...] = (acc_ref[...] + part).astype(o_ref.dtype)

    a_spec = pl.BlockSpec((tk, tm), lambda i, j, kk: (kk, i)) if mode == 'tn' else pl.BlockSpec((tm, tk), lambda i, j, kk: (i, kk))
    b_spec = pl.BlockSpec((tn, tk), lambda i, j, kk: (j, kk)) if mode == 'nt' else pl.BlockSpec((tk, tn), lambda i, j, kk: (kk, j))
    return pl.pallas_call(
        body, name=name, grid=(m // tm, n // tn, nk),
        in_specs=[a_spec, b_spec],
        out_specs=pl.BlockSpec((tm, tn), lambda i, j, kk: (i, j)),
        out_shape=jax.ShapeDtypeStruct((m, n), out_dtype),
        scratch_shapes=[pltpu.VMEM((tm, tn), F32)],
        compiler_params=_cparams(("parallel", "parallel", "arbitrary")),
    )(a, b)


ROW_TILE = 256


def _row_specs(rows, tm):
    return [pl.BlockSpec((tm, w), lambda i, ci=ci: (i, ci)) for (_, w, ci) in rows]


def _rw_forward(fn, rows, pars, outs, name):
    t = rows[0][0].shape[0]
    tm = min(ROW_TILE, t)
    nr, npar = len(rows), len(pars)

    def body(*refs):
        vals = [r[...].astype(F32) for r in refs[:nr]] + [p[...] for p in refs[nr:nr + npar]]
        res = fn(*vals)
        for o, v in zip(refs[nr + npar:], res):
            o[...] = v.astype(o.dtype)

    return pl.pallas_call(
        body, name=name, grid=(t // tm,),
        in_specs=_row_specs(rows, tm) + [pl.BlockSpec(p.shape, lambda i: (0, 0)) for p in pars],
        out_specs=tuple(pl.BlockSpec((tm, w), lambda i: (i, 0)) for (w, _) in outs),
        out_shape=tuple(jax.ShapeDtypeStruct((t, w), dt) for (w, dt) in outs),
        compiler_params=_cparams(("parallel",)),
    )(*[r[0] for r in rows], *pars)


def _rw_backward(fn, rows, pars, cots, drow_dtypes, name):
    t = rows[0][0].shape[0]
    tm = min(ROW_TILE, t)
    nr, npar, nc = len(rows), len(pars), len(cots)
    keep = [i for i, dt in enumerate(drow_dtypes) if dt is not None]

    def body(*refs):
        vals = [r[...].astype(F32) for r in refs[:nr]] + [p[...] for p in refs[nr:nr + npar]]
        cvals = tuple(c[...].astype(F32) for c in refs[nr + npar:nr + npar + nc])
        orefs = refs[nr + npar + nc:]
        _, vjp = jax.vjp(fn, *vals)
        g = vjp(cvals)
        for o, i in zip(orefs[:len(keep)], keep):
            o[...] = g[i].astype(o.dtype)
        first = pl.program_id(0) == 0
        for o, gi in zip(orefs[len(keep):], g[nr:]):
            @pl.when(first)
            def _(o=o, gi=gi):
                o[...] = gi

            @pl.when(jnp.logical_not(first))
            def _(o=o, gi=gi):
                o[...] += gi

    out_specs = [pl.BlockSpec((tm, rows[i][1]), lambda i_: (i_, 0)) for i in keep] + \
                [pl.BlockSpec(p.shape, lambda i_: (0, 0)) for p in pars]
    out_shape = [jax.ShapeDtypeStruct((t, rows[i][1]), drow_dtypes[i]) for i in keep] + \
                [jax.ShapeDtypeStruct(p.shape, F32) for p in pars]
    return pl.pallas_call(
        body, name=name, grid=(t // tm,),
        in_specs=_row_specs(rows, tm) + [pl.BlockSpec(p.shape, lambda i: (0, 0)) for p in pars] + _row_specs(cots, tm),
        out_specs=tuple(out_specs), out_shape=tuple(out_shape),
        compiler_params=_cparams(("arbitrary",)),
    )(*[r[0] for r in rows], *pars, *[c[0] for c in cots])


def _norm_fn(x, g):
    return (_rms(x, g),)


def _norm_skip_fn(x, g):
    return _rms(x, g), x


def _merge_fn(ga, gb, ya, yb):
    return (_sigmoid(ga) * ya + _sigmoid(gb) * yb,)


def _res_norm_fn(x, mo, g):
    x1 = x + mo
    return x1, _rms(x1, g)


def _loss_head(x1, fo, gf, target, name):
    t, d = x1.shape
    tm = min(ROW_TILE, t)

    def tile_loss(x2, g, tgt):
        err = _rms(x2, g) - tgt
        per_row = jnp.sum(err * err, axis=-1, keepdims=True) * (0.5 / d)
        return jnp.sum(per_row, axis=0, keepdims=True)

    def body(x1_ref, fo_ref, g_ref, t_ref, loss_ref, dx_ref, dxb_ref, dg_ref):
        x2 = x1_ref[...] + fo_ref[...]
        val, vjp = jax.vjp(functools.partial(tile_loss, tgt=t_ref[...]), x2, g_ref[...])
        dx2, dg = vjp(jnp.ones((1, 1), F32))
        dx_ref[...] = dx2
        dxb_ref[...] = dx2.astype(BF16)
        first = pl.program_id(0) == 0

        @pl.when(first)
        def _():
            loss_ref[...] = jnp.broadcast_to(val, loss_ref.shape)
            dg_ref[...] = dg

        @pl.when(jnp.logical_not(first))
        def _():
            loss_ref[...] += jnp.broadcast_to(val, loss_ref.shape)
            dg_ref[...] += dg

    row = pl.BlockSpec((tm, d), lambda i: (i, 0))
    vec = pl.BlockSpec((1, d), lambda i: (0, 0))
    return pl.pallas_call(
        body, name=name, grid=(t // tm,),
        in_specs=[row, row, vec, row],
        out_specs=(pl.BlockSpec((1, LANES), lambda i: (0, 0)), row, row, vec),
        out_shape=(jax.ShapeDtypeStruct((1, LANES), F32), jax.ShapeDtypeStruct((t, d), F32),
                   jax.ShapeDtypeStruct((t, d), BF16), jax.ShapeDtypeStruct((1, d), F32)),
        compiler_params=_cparams(("arbitrary",)),
    )(x1, fo, gf, target)


FFN_TILE_ROWS = 512
FFN_TILE_COLS = 256
FFN_COL_BLOCKS = FFN_HIDDEN // FFN_TILE_COLS


def _conv3_past(x, halo, w):
    rows = lax.broadcasted_iota(jnp.int32, x.shape, 0)
    x1 = jnp.where(rows == 0, halo[7:8, :], pltpu.roll(x, 1, 0))
    x2 = jnp.where(rows == 0, halo[6:7, :], jnp.where(rows == 1, halo[7:8, :], pltpu.roll(x, 2, 0)))
    return x * w[2:3] + x1 * w[1:2] + x2 * w[0:1], x1, x2


def _ffn_in_specs(tm, imap, jmap):
    per = tm // SUBLANES
    tile = lambda off: pl.BlockSpec((tm, FFN_TILE_COLS), lambda *g: (imap(*g), off + jmap(*g) % FFN_COL_BLOCKS))
    halo = lambda off: pl.BlockSpec((SUBLANES, FFN_TILE_COLS),
                                    lambda *g: (jnp.maximum(imap(*g) * per - 1, 0), off + jmap(*g) % FFN_COL_BLOCKS))
    wsp = lambda off: pl.BlockSpec((FFN_CONV, FFN_TILE_COLS), lambda *g: (0, off + jmap(*g) % FFN_COL_BLOCKS))
    return [tile(0), halo(0), wsp(0), tile(FFN_COL_BLOCKS), halo(FFN_COL_BLOCKS), wsp(FFN_COL_BLOCKS)]


def _ffn_act_forward(hpre, cw):
    t = hpre.shape[0]
    tm = min(FFN_TILE_ROWS, t)

    def body(hg, pg, wg, hu, pu, wu, out):
        live = (pl.program_id(0) > 0).astype(F32)
        cg, _, _ = _conv3_past(hg[...], pg[...] * live, wg[...])
        cu, _, _ = _conv3_past(hu[...], pu[...] * live, wu[...])
        out[...] = (_silu(cg) * cu).astype(out.dtype)

    return pl.pallas_call(
        body, name="ffn_act_fwd", grid=(t // tm, FFN_COL_BLOCKS),
        in_specs=_ffn_in_specs(tm, lambda i, j: i, lambda i, j: j),
        out_specs=pl.BlockSpec((tm, FFN_TILE_COLS), lambda i, j: (i, j)),
        out_shape=jax.ShapeDtypeStruct((t, FFN_HIDDEN), BF16),
        compiler_params=_cparams(("parallel", "parallel")),
    )(hpre, hpre, cw, hpre, hpre, cw)


def _conv3_future(d, nxt, w):
    tm = d.shape[0]
    rows = lax.broadcasted_iota(jnp.int32, d.shape, 0)
    d1 = jnp.where(rows == tm - 1, nxt[0:1, :], pltpu.roll(d, tm - 1, 0))
    d2 = jnp.where(rows == tm - 1, nxt[1:2, :], jnp.where(rows == tm - 2, nxt[0:1, :], pltpu.roll(d, tm - 2, 0)))
    return d * w[2:3] + d1 * w[1:2] + d2 * w[0:1]


def _ffn_backward(hpre, cw, dact):
    t = hpre.shape[0]
    tm = min(FFN_TILE_ROWS, t)
    n_tiles = t // tm
    per = tm // SUBLANES

    def d_conv_out(cg, cu, d):
        s = _sigmoid(cg)
        return d * cu * s * (1.0 + cg * (1.0 - s)), d * cg * s

    def body(hg, pg, ng, wg, hu, pu, nu, wu, da, dan, dhg, dhu, dwg, dwu):
        i = pl.program_id(1)
        live_prev = (i > 0).astype(F32)
        live_next = (i < n_tiles - 1).astype(F32)
        xg, xu = hg[...], hu[...]
        cg, g1, g2 = _conv3_past(xg, pg[...] * live_prev, wg[...])
        cu, u1, u2 = _conv3_past(xu, pu[...] * live_prev, wu[...])
        dg, du = d_conv_out(cg, cu, da[...])
        cgn, _, _ = _conv3_past(ng[...], hg[tm - SUBLANES:tm, :], wg[...])
        cun, _, _ = _conv3_past(nu[...], hu[tm - SUBLANES:tm, :], wu[...])
        dgn, dun = d_conv_out(cgn, cun, dan[...] * live_next)
        dhg[...] = _conv3_future(dg, dgn, wg[...]).astype(dhg.dtype)
        dhu[...] = _conv3_future(du, dun, wu[...]).astype(dhu.dtype)
        sums_g = [jnp.sum(xs * dg, axis=0, keepdims=True) for xs in (g2, g1, xg)]
        sums_u = [jnp.sum(xs * du, axis=0, keepdims=True) for xs in (u2, u1, xu)]

        @pl.when(i == 0)
        def _():
            for r_ in range(FFN_CONV):
                dwg[r_:r_ + 1, :] = sums_g[r_]
                dwu[r_:r_ + 1, :] = sums_u[r_]

        @pl.when(i > 0)
        def _():
            for r_ in range(FFN_CONV):
                dwg[r_:r_ + 1, :] += sums_g[r_]
                dwu[r_:r_ + 1, :] += sums_u[r_]

    nb = FFN_COL_BLOCKS
    nxt = lambda i: jnp.minimum((i + 1) * per, t // SUBLANES - 1)
    prv = lambda i: jnp.maximum(i * per - 1, 0)
    half = lambda off: [pl.BlockSpec((tm, FFN_TILE_COLS), lambda j, i: (i, off + j)),
                        pl.BlockSpec((SUBLANES, FFN_TILE_COLS), lambda j, i: (prv(i), off + j)),
                        pl.BlockSpec((SUBLANES, FFN_TILE_COLS), lambda j, i: (nxt(i), off + j)),
                        pl.BlockSpec((FFN_CONV, FFN_TILE_COLS), lambda j, i: (0, off + j))]
    tile = pl.BlockSpec((tm, FFN_TILE_COLS), lambda j, i: (i, j))
    taps = pl.BlockSpec((FFN_CONV, FFN_TILE_COLS), lambda j, i: (0, j))
    return pl.pallas_call(
        body, name="ffn_bwd", grid=(nb, n_tiles),
        in_specs=half(0) + half(nb) + [tile, pl.BlockSpec((SUBLANES, FFN_TILE_COLS), lambda j, i: (nxt(i), j))],
        out_specs=(tile, tile, taps, taps),
        out_shape=(jax.ShapeDtypeStruct((t, FFN_HIDDEN), BF16), jax.ShapeDtypeStruct((t, FFN_HIDDEN), BF16),
                   jax.ShapeDtypeStruct((FFN_CONV, FFN_HIDDEN), F32), jax.ShapeDtypeStruct((FFN_CONV, FFN_HIDDEN), F32)),
        compiler_params=_cparams(("parallel", "arbitrary")),
    )(hpre, hpre, hpre, cw, hpre, hpre, hpre, cw, dact, dact)


def _my_place():
    x, y, c = lax.axis_index("x"), lax.axis_index("y"), lax.axis_index("c")
    return x, y, c, 4 * x + 2 * y + c


N_CHIPS = 4


def _remote(src, dst, send_sem, recv_sem, dev):
    return pltpu.make_async_remote_copy(src_ref=src, dst_ref=dst, send_sem=send_sem, recv_sem=recv_sem, device_id=dev,
                                        device_id_type=pl.DeviceIdType.MESH)


def _chip_peer(x, y, k):
    return x ^ ((k >> 1) & 1), y ^ (k & 1)


def _all_gather_two_level(shard, name):
    r, w = shard.shape

    def body(src, out, send_sems, recv_sems, local_sem):
        x, y, c, me = _my_place()
        sibling = (x, y, 1 - c)
        mine = pltpu.make_async_copy(src, out.at[me], local_sem)
        mine.start()
        first = [_remote(src, out.at[me], send_sems.at[0], recv_sems.at[0], sibling)]
        for k in range(1, N_CHIPS):
            px, py = _chip_peer(x, y, k)
            first.append(_remote(src, out.at[me], send_sems.at[k], recv_sems.at[k], (px, py, c)))
        for cp in first:
            cp.start()
        passed = []
        for k in range(1, N_CHIPS):
            px, py = _chip_peer(x, y, k)
            landed = out.at[me ^ (2 * k)]
            _remote(src, landed, send_sems.at[k], recv_sems.at[k], (px, py, c)).wait_recv()
            fwd = _remote(landed, landed, send_sems.at[N_CHIPS - 1 + k], recv_sems.at[N_CHIPS - 1 + k], sibling)
            fwd.start()
            passed.append(fwd)
        _remote(src, out.at[me ^ 1], send_sems.at[0], recv_sems.at[0], sibling).wait_recv()
        for k in range(1, N_CHIPS):
            got = out.at[(me ^ 1) ^ (2 * k)]
            _remote(got, got, send_sems.at[N_CHIPS - 1 + k], recv_sems.at[N_CHIPS - 1 + k], sibling).wait_recv()
        for cp in first + passed:
            cp.wait_send()
        mine.wait()

    return pl.pallas_call(
        body, name=name,
        in_specs=[pl.BlockSpec(memory_space=pl.ANY)],
        out_specs=pl.BlockSpec(memory_space=pl.ANY),
        out_shape=jax.ShapeDtypeStruct((N_DEV, r, w), shard.dtype),
        scratch_shapes=[pltpu.SemaphoreType.DMA((N_DEV - 1,)), pltpu.SemaphoreType.DMA((N_DEV - 1,)), pltpu.SemaphoreType.DMA],
    )(shard)


def _pair_exchange(arrays, name):
    na = len(arrays)

    def body(*refs):
        srcs, dsts, (send_sems, recv_sems) = refs[:na], refs[na:2 * na], refs[2 * na:]
        x, y, c, _ = _my_place()
        sibling = (x, y, 1 - c)
        copies = []
        for i in range(na):
            for q in range(N_CHIPS):
                s = i * N_CHIPS + q
                copies.append(_remote(srcs[i].at[2 * q + 1 - c], dsts[i].at[q], send_sems.at[s], recv_sems.at[s], sibling))
        for cp in copies:
            cp.start()
        for cp in copies:
            cp.wait_recv()
        for cp in copies:
            cp.wait_send()

    hbm = pl.BlockSpec(memory_space=pl.ANY)
    return pl.pallas_call(
        body, name=name, in_specs=[hbm] * na, out_specs=tuple([hbm] * na),
        out_shape=tuple(jax.ShapeDtypeStruct((N_CHIPS,) + a.shape[1:], a.dtype) for a in arrays),
        scratch_shapes=[pltpu.SemaphoreType.DMA((na * N_CHIPS,)), pltpu.SemaphoreType.DMA((na * N_CHIPS,))],
    )(*arrays)


ELEMENTWISE_COLS = 256


def _pair_sum(slabs, recv, core, out_dtype, name):
    _, r, w = slabs.shape
    tc = ELEMENTWISE_COLS

    def body(core_ref, mine, theirs, out):
        out[...] = (mine[...] + theirs[...]).astype(out.dtype)

    grid_spec = pltpu.PrefetchScalarGridSpec(
        num_scalar_prefetch=1, grid=(N_CHIPS, w // tc),
        in_specs=[pl.BlockSpec((None, r, tc), lambda q, j, core_ref: (2 * q + core_ref[0], 0, j)),
                  pl.BlockSpec((None, r, tc), lambda q, j, core_ref: (q, 0, j))],
        out_specs=pl.BlockSpec((None, r, tc), lambda q, j, core_ref: (q, 0, j)))
    return pl.pallas_call(body, name=name, grid_spec=grid_spec,
                          out_shape=jax.ShapeDtypeStruct((N_CHIPS, r, w), out_dtype),
                          compiler_params=_cparams(("parallel", "parallel")))(core, slabs, recv)


def _chip_exchange(arrays, name):
    na = len(arrays)

    def body(*refs):
        srcs, dsts, (send_sems, recv_sems, local_sems) = refs[:na], refs[na:2 * na], refs[2 * na:]
        x, y, c, _ = _my_place()
        chip = 2 * x + y
        own = [pltpu.make_async_copy(srcs[i].at[chip], dsts[i].at[chip], local_sems.at[i]) for i in range(na)]
        for cp in own:
            cp.start()
        sends, arrivals = [], []
        for i in range(na):
            for k in range(1, N_CHIPS):
                px, py = _chip_peer(x, y, k)
                s = i * N_CHIPS + k
                sends.append(_remote(srcs[i].at[chip ^ k], dsts[i].at[chip], send_sems.at[s], recv_sems.at[s], (px, py, c)))
                arrivals.append(_remote(srcs[i].at[chip], dsts[i].at[chip ^ k], send_sems.at[s], recv_sems.at[s], (px, py, c)))
        for cp in sends:
            cp.start()
        for cp in arrivals:
            cp.wait_recv()
        for cp in sends:
            cp.wait_send()
        for cp in own:
            cp.wait()

    hbm = pl.BlockSpec(memory_space=pl.ANY)
    return pl.pallas_call(
        body, name=name, in_specs=[hbm] * na, out_specs=tuple([hbm] * na),
        out_shape=tuple(jax.ShapeDtypeStruct(a.shape, a.dtype) for a in arrays),
        scratch_shapes=[pltpu.SemaphoreType.DMA((na * N_CHIPS,)), pltpu.SemaphoreType.DMA((na * N_CHIPS,)),
                        pltpu.SemaphoreType.DMA((na,))],
    )(*arrays)


def _reduce_adamw(parts, w, m, v, name):
    n_parts, r, wd = parts.shape
    tc = ELEMENTWISE_COLS
    c1 = 1.0 / (1.0 - ADAM_B1 ** ADAM_STEP)
    c2 = 1.0 / (1.0 - ADAM_B2 ** ADAM_STEP)

    def body(p_ref, w_ref, m_ref, v_ref, g_out, d_out, m_out, v_out):
        g = p_ref[0].astype(F32)
        for s in range(1, n_parts):
            g = g + p_ref[s].astype(F32)
        mn = ADAM_B1 * m_ref[...] + (1.0 - ADAM_B1) * g
        vn = ADAM_B2 * v_ref[...] + (1.0 - ADAM_B2) * (g * g)
        g_out[...] = g
        m_out[...] = mn
        v_out[...] = vn
        d_out[...] = -ADAM_LR * ((mn * c1) / (jnp.sqrt(vn * c2) + ADAM_EPS) + ADAM_WD * w_ref[...])

    blk = pl.BlockSpec((r, tc), lambda j: (0, j))
    shp = jax.ShapeDtypeStruct((r, wd), F32)
    return pl.pallas_call(
        body, name=name, grid=(wd // tc,),
        in_specs=[pl.BlockSpec((n_parts, r, tc), lambda j: (0, 0, j)), blk, blk, blk],
        out_specs=(blk, blk, blk, blk), out_shape=(shp, shp, shp, shp),
        compiler_params=_cparams(("parallel",)),
    )(parts, w, m, v)


PACK_W = 1024


def _pad_heads(a, slots):
    lead = a.shape[:-1]
    a = a.reshape(lead + (slots, RWKV_HEAD_DIM))
    a = jnp.pad(a, [(0, 0)] * (len(lead) + 1) + [(0, LANES - RWKV_HEAD_DIM)])
    return a.reshape(lead + (slots * LANES,))


def _unpad_heads(a, slots):
    lead = a.shape[:-1]
    return a.reshape(lead + (slots, LANES))[..., :RWKV_HEAD_DIM].reshape(lead + (slots * RWKV_HEAD_DIM,))


def _flat_pack(arrs, dtype, row_mult):
    flat = jnp.concatenate([a.reshape(-1).astype(dtype) for a in arrs])
    n = flat.shape[0]
    rows = -(-n // PACK_W)
    rows = -(-rows // row_mult) * row_mult
    return jnp.pad(flat, (0, rows * PACK_W - n)).reshape(rows, PACK_W)


def _unpack_gathered(g, names, shard_shapes):
    flat = g.reshape(N_DEV, -1)
    out, off = {}, 0
    for n in names:
        s = shard_shapes[n]
        size = s[0] * s[1]
        seg = flat[:, off:off + size].reshape((N_DEV,) + tuple(s))
        off += size
        if SHARD_AXIS[n] == 1:
            out[n] = jnp.transpose(seg, (1, 0, 2)).reshape(s[0], N_DEV * s[1])
        else:
            out[n] = seg.reshape(N_DEV * s[0], s[1])
    return out


def _shard_major(full, axis):
    a, b = full.shape
    if axis == 1:
        return jnp.transpose(full.reshape(a, N_DEV, b // N_DEV), (1, 0, 2)).reshape(N_DEV, -1)
    return full.reshape(N_DEV, -1)


def _prepare_weights(full, rep):
    w = full['w_in']
    d = w.shape[1]
    rkv = jnp.pad(w[0:1536].reshape(3 * RWKV_HEADS, RWKV_HEAD_DIM, d), ((0, 0), (0, LANES - RWKV_HEAD_DIM), (0, 0)))
    w_cat = jnp.concatenate([
        w[3848:4872], w[4872:5896], rkv.reshape(3 * RWKV_HEADS * LANES, d), w[1792:3328], w[3328:3840],
        w[1536:1792], jnp.pad(w[3840:3848], ((0, LANES - 8), (0, 0))), jnp.zeros((LANES, d), w.dtype)], axis=0)
    assert w_cat.shape[0] == CAT_W
    rp = full['rwkv_proj']
    rproj = jnp.pad(rp.reshape(RWKV_HEADS, RWKV_HEAD_DIM, -1), ((0, 0), (0, LANES - RWKV_HEAD_DIM), (0, 0))).reshape(RWKV_HEADS * LANES, -1)
    mu = rep['rwkv_mu']
    vecs = [mu[0:512], mu[512:1024], mu[1024:1536], rep['rwkv_w0'], rep['rwkv_a0'], rep['rwkv_k_k'], rep['rwkv_k_a'],
            rep['rwkv_ln_w'], rep['rwkv_ln_b'], rep['rwkv_r_k'].reshape(-1)]
    ppack = jnp.stack([jnp.pad(v.reshape(RWKV_HEADS, RWKV_HEAD_DIM), ((0, 0), (0, LANES - RWKV_HEAD_DIM))) for v in vecs], axis=1)
    ppack = jnp.pad(ppack, ((0, 0), (0, 16 - len(vecs)), (0, 0)))
    mulo = mu[1536:1792].reshape(1, 2 * LANES)
    wl = jnp.zeros((3, 2 * LANES, RWKV_HEADS * LANES), F32)
    wl = wl.at[0, 0:64].set(_pad_heads(full['rwkv_w2'], RWKV_HEADS))
    wl = wl.at[1, 64:128].set(_pad_heads(full['rwkv_a2'], RWKV_HEADS))
    wl = wl.at[2, 128:256].set(_pad_heads(full['rwkv_g2'], RWKV_HEADS))
    wl = jnp.transpose(wl.reshape(3, 2 * LANES, RWKV_HEADS, LANES), (2, 0, 1, 3))
    cw = full['gdn_conv_w'].reshape(GDN_CONV, 3, GDN_HEADS, LANES)
    cwpack = jnp.pad(jnp.transpose(cw, (2, 1, 0, 3)), ((0, 0), (0, 0), (0, SUBLANES - GDN_CONV), (0, 0)))
    gpar = jnp.zeros((SUBLANES, LANES), F32)
    gpar = gpar.at[0, 0:GDN_HEADS].set(rep['gdn_a_log']).at[1, 0:GDN_HEADS].set(rep['gdn_dt_bias']).at[2].set(rep['gdn_norm_w'])
    return dict(w_cat=w_cat, rproj=rproj, gproj=full['gdn_proj'], w_out=full['w_out'], ffn_up=full['ffn_up'],
                ffn_down=full['ffn_down'], ffn_cw=full['ffn_conv_w'], ppack=ppack, mulo=mulo, wl=wl, cwpack=cwpack, gpar=gpar,
                g1=rep['norm1_g'].reshape(1, -1), g2=rep['norm2_g'].reshape(1, -1), gf=rep['final_g'].reshape(1, -1))


def _local_step(x, target, p):
    d = x.shape[1]
    full_w = lambda a: (a, a.shape[1], 0)
    (u,) = _rw_forward(_norm_fn, [full_w(x)], [p['g1']], [(d, BF16)], "norm1")
    p_cat = _matmul(u, p['w_cat'], 'nt', F32, "proj_in")
    ya_pre, st_r = _rwkv_forward(p_cat, p['ppack'], p['mulo'], p['wl'])
    yb_pre, st_g = _gdn_forward(p_cat, p['cwpack'], p['gpar'])
    ya = _matmul(ya_pre, p['rproj'], 'nn', F32, "rwkv_proj")
    yb = _matmul(yb_pre, p['gproj'], 'nn', F32, "gdn_proj")
    gates = [(p_cat, d, OFF_GA // d), (p_cat, d, OFF_GB // d)]
    (mixed,) = _rw_forward(_merge_fn, gates + [full_w(ya), full_w(yb)], [], [(d, BF16)], "merge")
    mo = _matmul(mixed, p['w_out'], 'nn', F32, "out_proj")
    x1, n2 = _rw_forward(_res_norm_fn, [full_w(x), full_w(mo)], [p['g2']], [(d, F32), (d, BF16)], "res_norm2")
    hpre = _matmul(n2, p['ffn_up'], 'nt', F32, "ffn_up")
    act = _ffn_act_forward(hpre, p['ffn_cw'])
    fo = _matmul(act, p['ffn_down'], 'nn', F32, "ffn_down")
    loss_vec, dx2, dx2b, dgf = _loss_head(x1, fo, p['gf'], target, "loss_head")

    dact = _matmul(dx2b, p['ffn_down'], 'nt', F32, "d_act")
    dw_down = _matmul(act, dx2b, 'tn', F32, "dw_ffn_down")
    dh_gate, dh_up, dcw_gate, dcw_up = _ffn_backward(hpre, p['ffn_cw'], dact)
    dh = jnp.concatenate([dh_gate, dh_up], axis=1)
    dcw_f = jnp.concatenate([dcw_gate, dcw_up], axis=1)
    dn2 = _matmul(dh, p['ffn_up'], 'nn', F32, "d_norm2")
    dw_up = _matmul(dh, n2, 'tn', F32, "dw_ffn_up")
    dx1, dx1b, dg2 = _rw_backward(_res_norm_fn, [full_w(x), full_w(mo)], [p['g2']], [full_w(dx2), full_w(dn2)],
                                  [F32, BF16], "res_norm2_bwd")
    dmixed = _matmul(dx1b, p['w_out'], 'nt', F32, "d_mixed")
    dw_out = _matmul(mixed, dx1b, 'tn', F32, "dw_out")
    dga, dgb, dya, dyb = _rw_backward(_merge_fn, gates + [full_w(ya), full_w(yb)], [], [full_w(dmixed)],
                                      [BF16, BF16, BF16, BF16], "merge_bwd")
    d_ya_pre = _matmul(dya, p['rproj'], 'nt', F32, "d_rwkv_out")
    dw_rproj = _matmul(ya_pre, dya, 'tn', F32, "dw_rwkv_proj")
    d_yb_pre = _matmul(dyb, p['gproj'], 'nt', F32, "d_gdn_out")
    dw_gproj = _matmul(yb_pre, dyb, 'tn', F32, "dw_gdn_proj")
    dpr, dpk, dpv, dplo, dpp, dml, dwl = _rwkv_backward(p_cat, p['ppack'], p['mulo'], p['wl'], st_r, d_ya_pre)
    dq, dk, dv, dz, dab, dcw_g, dgp = _gdn_backward(p_cat, p['cwpack'], p['gpar'], st_g, d_yb_pre)
    t = x.shape[0]
    dp_cat = jnp.concatenate([dga, dgb, dpr, dpk, dpv, dq, dk, dv, dz, dplo.astype(BF16), dab.astype(BF16),
                              jnp.zeros((t, LANES), BF16)], axis=1)
    du = _matmul(dp_cat, p['w_cat'], 'nn', F32, "d_norm1")
    dw_cat = _matmul(dp_cat, u, 'tn', F32, "dw_in")
    grad_x, dg1 = _rw_backward(_norm_skip_fn, [full_w(x)], [p['g1']], [full_w(du), full_w(dx1)], [F32], "norm1_bwd")

    heads = lambda row: dpp[:, row, :RWKV_HEAD_DIM].reshape(-1)
    lora = lambda j, lo_, hi_: jnp.transpose(dwl[:, j, lo_:hi_, :RWKV_HEAD_DIM], (1, 0, 2)).reshape(hi_ - lo_, RWKV_WIDTH)
    grads = {
        'norm1_g': dg1[0],
        'w_in': jnp.concatenate([dw_cat[OFF_RKV:OFF_QKV].reshape(3 * RWKV_HEADS, LANES, d)[:, :RWKV_HEAD_DIM].reshape(-1, d),
                                 dw_cat[OFF_LO:OFF_AB], dw_cat[OFF_QKV:OFF_Z], dw_cat[OFF_Z:OFF_LO], dw_cat[OFF_AB:OFF_AB + 8],
                                 dw_cat[OFF_GA:OFF_GB], dw_cat[OFF_GB:OFF_RKV]], axis=0),
        'rwkv_mu': jnp.concatenate([heads(0), heads(1), heads(2), jnp.sum(dml[:, 0, :], axis=0)]),
        'rwkv_w0': heads(3), 'rwkv_a0': heads(4), 'rwkv_k_k': heads(5), 'rwkv_k_a': heads(6),
        'rwkv_ln_w': heads(7), 'rwkv_ln_b': heads(8), 'rwkv_r_k': heads(9).reshape(RWKV_HEADS, RWKV_HEAD_DIM),
        'rwkv_w2': lora(0, 0, 64), 'rwkv_a2': lora(1, 64, 128), 'rwkv_g2': lora(2, 128, 256),
        'rwkv_proj': dw_rproj.reshape(RWKV_HEADS, LANES, -1)[:, :RWKV_HEAD_DIM].reshape(RWKV_WIDTH, -1),
        'gdn_conv_w': jnp.transpose(dcw_g[:, :, :GDN_CONV, :], (2, 1, 0, 3)).reshape(GDN_CONV, 3 * GDN_WIDTH),
        'gdn_a_log': jnp.sum(dgp[:, 0, :GDN_HEADS], axis=0), 'gdn_dt_bias': jnp.sum(dgp[:, 1, :GDN_HEADS], axis=0),
        'gdn_norm_w': jnp.sum(dgp[:, 2, :], axis=0),
        'gdn_proj': dw_gproj, 'w_out': dw_out, 'norm2_g': dg2[0], 'ffn_up': dw_up, 'ffn_conv_w': dcw_f,
        'ffn_down': dw_down, 'final_g': dgf[0],
    }
    return loss_vec, grad_x, grads


def kernel(x, norm1_g, w_in, rwkv_mu, rwkv_w0, rwkv_w2, rwkv_a0, rwkv_a2, rwkv_g2, rwkv_k_k, rwkv_k_a, rwkv_r_k, rwkv_ln_w, rwkv_ln_b, rwkv_proj, gdn_conv_w, gdn_a_log, gdn_dt_bias, gdn_norm_w, gdn_proj, w_out, norm2_g, ffn_up, ffn_conv_w, ffn_down, final_g, loss_target, m_norm1_g, m_w_in, m_rwkv_mu, m_rwkv_w0, m_rwkv_w2, m_rwkv_a0, m_rwkv_a2, m_rwkv_g2, m_rwkv_k_k, m_rwkv_k_a, m_rwkv_r_k, m_rwkv_ln_w, m_rwkv_ln_b, m_rwkv_proj, m_gdn_conv_w, m_gdn_a_log, m_gdn_dt_bias, m_gdn_norm_w, m_gdn_proj, m_w_out, m_norm2_g, m_ffn_up, m_ffn_conv_w, m_ffn_down, m_final_g, v_norm1_g, v_w_in, v_rwkv_mu, v_rwkv_w0, v_rwkv_w2, v_rwkv_a0, v_rwkv_a2, v_rwkv_g2, v_rwkv_k_k, v_rwkv_k_a, v_rwkv_r_k, v_rwkv_ln_w, v_rwkv_ln_b, v_rwkv_proj, v_gdn_conv_w, v_gdn_a_log, v_gdn_dt_bias, v_gdn_norm_w, v_gdn_proj, v_w_out, v_norm2_g, v_ffn_up, v_ffn_conv_w, v_ffn_down, v_final_g):
    given = dict(zip(WEIGHT_NAMES, (norm1_g, w_in, rwkv_mu, rwkv_w0, rwkv_w2, rwkv_a0, rwkv_a2, rwkv_g2, rwkv_k_k, rwkv_k_a, rwkv_r_k,
                                    rwkv_ln_w, rwkv_ln_b, rwkv_proj, gdn_conv_w, gdn_a_log, gdn_dt_bias, gdn_norm_w, gdn_proj, w_out,
                                    norm2_g, ffn_up, ffn_conv_w, ffn_down, final_g)))
    mom1 = dict(zip(WEIGHT_NAMES, (m_norm1_g, m_w_in, m_rwkv_mu, m_rwkv_w0, m_rwkv_w2, m_rwkv_a0, m_rwkv_a2, m_rwkv_g2, m_rwkv_k_k,
                                   m_rwkv_k_a, m_rwkv_r_k, m_rwkv_ln_w, m_rwkv_ln_b, m_rwkv_proj, m_gdn_conv_w, m_gdn_a_log,
                                   m_gdn_dt_bias, m_gdn_norm_w, m_gdn_proj, m_w_out, m_norm2_g, m_ffn_up, m_ffn_conv_w, m_ffn_down,
                                   m_final_g)))
    mom2 = dict(zip(WEIGHT_NAMES, (v_norm1_g, v_w_in, v_rwkv_mu, v_rwkv_w0, v_rwkv_w2, v_rwkv_a0, v_rwkv_a2, v_rwkv_g2, v_rwkv_k_k,
                                   v_rwkv_k_a, v_rwkv_r_k, v_rwkv_ln_w, v_rwkv_ln_b, v_rwkv_proj, v_gdn_conv_w, v_gdn_a_log,
                                   v_gdn_dt_bias, v_gdn_norm_w, v_gdn_proj, v_w_out, v_norm2_g, v_ffn_up, v_ffn_conv_w, v_ffn_down,
                                   v_final_g)))
    def strip(n, a):
        a = a if n == 'final_g' else a.reshape(a.shape[1:])
        return a.T if n in TRANSPOSED else a

    local = {n: strip(n, a) for n, a in given.items()}
    shard_shapes = {n: local[n].shape for n in SHARD_AXIS}
    sharded = BIG_SHARDED + SMALL_SHARDED

    g_big = _all_gather_two_level(_flat_pack([local[n] for n in BIG_SHARDED], BF16, 16), "gather_big")
    g_small = _all_gather_two_level(_flat_pack([local[n] for n in SMALL_SHARDED], F32, SUBLANES), "gather_small")
    full = _unpack_gathered(g_big, BIG_SHARDED, shard_shapes)
    full.update(_unpack_gathered(g_small, SMALL_SHARDED, shard_shapes))
    rep = {n: local[n] for n in REPLICATED}

    loss_vec, grad_x, grads = _local_step(x[0], loss_target[0], _prepare_weights(full, rep))

    small_sharded = ['rwkv_proj', 'gdn_proj'] + SMALL_SHARDED
    small_names = small_sharded + REPLICATED
    rep_vec = jnp.concatenate([grads[n].reshape(-1) for n in REPLICATED] + [loss_vec[0, 0:1]])
    slabs = [grads[n].reshape(N_DEV, -1, grads[n].shape[1]) for n in ROW_SHARDED]
    slab_small = jnp.concatenate([_shard_major(grads[n], SHARD_AXIS[n]) for n in small_sharded] +
                                 [jnp.broadcast_to(rep_vec[None], (N_DEV, rep_vec.shape[0]))], axis=1)
    small_rows = -(-slab_small.shape[1] // (PACK_W * SUBLANES)) * SUBLANES
    slab_small = jnp.pad(slab_small, ((0, 0), (0, small_rows * PACK_W - slab_small.shape[1]))).reshape(N_DEV, small_rows, PACK_W)
    core = lax.axis_index("c").astype(jnp.int32).reshape(1)
    from_sibling = _pair_exchange(slabs + [slab_small], "grad_pair_exchange")
    chip_parts = [_pair_sum(s, r, core, BF16, "grad_pair_sum_" + n) for s, r, n in zip(slabs, from_sibling, ROW_SHARDED)]
    chip_parts.append(_pair_sum(slab_small, from_sibling[-1], core, F32, "grad_pair_sum_small"))
    parts = _chip_exchange(chip_parts, "grad_chip_exchange")

    def pack_local(src):
        flat = jnp.concatenate([strip(n, src[n]).reshape(-1) for n in small_names])
        return jnp.pad(flat, (0, small_rows * PACK_W - flat.shape[0])).reshape(small_rows, PACK_W)

    results = [({}, None) for _ in range(4)]
    for part, n in zip(parts, ROW_SHARDED):
        packs = _reduce_adamw(part, local[n], strip(n, mom1[n]), strip(n, mom2[n]), "adamw_" + n)
        for (out, _), pk in zip(results, packs):
            out[n] = (pk.T if n in TRANSPOSED else pk).reshape(given[n].shape)
    packs = _reduce_adamw(parts[-1], pack_local(given), pack_local(mom1), pack_local(mom2), "adamw_small")
    for i, pk in enumerate(packs):
        flat, off = pk.reshape(-1), 0
        for n in small_names:
            size = int(np.prod(given[n].shape))
            results[i][0][n] = flat[off:off + size].reshape(given[n].shape)
            off += size
        results[i] = (results[i][0], flat[off])
    (g_out, loss), (d_out, _), (m_out, _), (v_out, _) = results
    return (loss, grad_x[None], *[g_out[n] for n in WEIGHT_NAMES], *[d_out[n] for n in WEIGHT_NAMES],
            *[m_out[n] for n in WEIGHT_NAMES], *[v_out[n] for n in WEIGHT_NAMES])
```

```python
import functools

import jax
import jax.numpy as jnp
import numpy as np
from jax import lax
from jax.experimental import pallas as pl
from jax.experimental.pallas import tpu as pltpu

F32 = jnp.float32
BF16 = jnp.bfloat16
HI = lax.Precision.HIGHEST

N_DEV = 8
D_MODEL = 1024
CHUNK = 64
RWKV_HEADS = 8
RWKV_HEAD_DIM = 64
RWKV_WIDTH = 512
GDN_HEADS = 4
GDN_HEAD_DIM = 128
GDN_WIDTH = 512
GDN_CONV = 4
FFN_HIDDEN = 2816
FFN_CONV = 3
NORM_EPS = 1e-6
L2_EPS = 1e-6
RWKV_GN_EPS = 64e-5
LANES = 128
SUBLANES = 8
VMEM_LIMIT = 56 * 1024 * 1024

ADAM_LR = 0.001
ADAM_B1 = 0.9
ADAM_B2 = 0.999
ADAM_EPS = 1e-08
ADAM_WD = 0.01
ADAM_STEP = 10

OFF_GA, OFF_GB, OFF_RKV, OFF_QKV, OFF_Z, OFF_LO, OFF_AB, CAT_W = 0, 1024, 2048, 5120, 6656, 7168, 7424, 7680
RWKV_HB = 8
GDN_HB = 4

WEIGHT_NAMES = ['norm1_g', 'w_in', 'rwkv_mu', 'rwkv_w0', 'rwkv_w2', 'rwkv_a0', 'rwkv_a2', 'rwkv_g2', 'rwkv_k_k', 'rwkv_k_a',
                'rwkv_r_k', 'rwkv_ln_w', 'rwkv_ln_b', 'rwkv_proj', 'gdn_conv_w', 'gdn_a_log', 'gdn_dt_bias', 'gdn_norm_w',
                'gdn_proj', 'w_out', 'norm2_g', 'ffn_up', 'ffn_conv_w', 'ffn_down', 'final_g']
BIG_SHARDED = ['w_in', 'ffn_up', 'ffn_down', 'w_out', 'rwkv_proj', 'gdn_proj']
SMALL_SHARDED = ['rwkv_w2', 'rwkv_a2', 'rwkv_g2', 'gdn_conv_w', 'ffn_conv_w']
TRANSPOSED = ('w_in', 'ffn_up')
SHARD_AXIS = {'w_in': 0, 'ffn_up': 0, 'ffn_down': 0, 'w_out': 0, 'rwkv_proj': 1, 'gdn_proj': 1,
              'rwkv_w2': 1, 'rwkv_a2': 1, 'rwkv_g2': 1, 'gdn_conv_w': 1, 'ffn_conv_w': 1}
REPLICATED = [n for n in WEIGHT_NAMES if n not in SHARD_AXIS]
ROW_SHARDED = ['w_in', 'ffn_up', 'ffn_down', 'w_out']


def _cparams(sem=None):
    kw = dict(vmem_limit_bytes=VMEM_LIMIT)
    if sem is not None:
        kw['dimension_semantics'] = sem
    return pltpu.CompilerParams(**kw)


_NN, _NT, _TN = 'nn', 'nt', 'tn'
_DIMS_2D = {'nn': (((1,), (0,)), ((), ())), 'nt': (((1,), (1,)), ((), ())), 'tn': (((0,), (0,)), ((), ()))}
_DIMS_3D = {'nn': (((2,), (1,)), ((0,), (0,))), 'nt': (((2,), (2,)), ((0,), (0,))), 'tn': (((1,), (1,)), ((0,), (0,)))}


def _dg(a, b, kind):
    return lax.dot_general(a, b, (_DIMS_2D if a.ndim == 2 else _DIMS_3D)[kind], preferred_element_type=F32)


def _dot1(a, b, kind):
    return _dg(a.astype(BF16), b.astype(BF16), kind)


@jax.custom_vjp
def _dhi(a, b):
    return _dot1(a, b, _NN)


_dhi.defvjp(lambda a, b: (_dot1(a, b, _NN), (a, b)),
            lambda res, ct: (_dot1(ct, res[1], _NT), _dot1(res[0], ct, _TN)))


@jax.custom_vjp
def _dnt(a, b):
    return _dot1(a, b, _NT)


_dnt.defvjp(lambda a, b: (_dot1(a, b, _NT), (a, b)),
            lambda res, ct: (_dot1(ct, res[1], _NN), _dot1(ct, res[0], _TN)))


@jax.custom_vjp
def _dtn(a, b):
    return _dot1(a, b, _TN)


_dtn.defvjp(lambda a, b: (_dot1(a, b, _TN), (a, b)),
            lambda res, ct: (_dot1(res[1], ct, _NT), _dot1(res[0], ct, _NN)))


def _split3(x):
    x1 = x.astype(BF16)
    r1 = x - x1.astype(F32)
    x2 = r1.astype(BF16)
    return x1, x2, (r1 - x2.astype(F32)).astype(BF16)


def _dot_exact_lhs(sel, x, kind):
    parts = [_dg(sel, xi, kind) for xi in _split3(x)]
    return parts[0] + parts[1] + parts[2]


def _tril_ones(like):
    c = like.shape[-2]
    ri, ci = _iotas(c)
    return jnp.broadcast_to((ri >= ci).astype(BF16), like.shape[:-2] + (c, c))


@jax.custom_vjp
def _cumsum_rows(x):
    return _dot_exact_lhs(_tril_ones(x), x, _NN)


_cumsum_rows.defvjp(lambda x: (_dot_exact_lhs(_tril_ones(x), x, _NN), None),
                    lambda _, ct: (_dot_exact_lhs(_tril_ones(ct), ct, _TN),))


@jax.custom_vjp
def _lane_sum_as_row(x):
    return _dot_exact_lhs(jnp.ones(x.shape, BF16), x, _NT)


def _lane_sum_as_row_bwd(_, ct):
    ones = jnp.ones(ct.shape[:-1] + (LANES,), BF16)
    parts = [_dg(ci, ones, _TN) for ci in _split3(ct)]
    return (parts[0] + parts[1] + parts[2],)


_lane_sum_as_row.defvjp(lambda x: (_dot_exact_lhs(jnp.ones(x.shape, BF16), x, _NT), None), _lane_sum_as_row_bwd)


def _shift_rows(x, halo, s):
    rows = lax.broadcasted_iota(jnp.int32, x.shape, 0)
    out = pltpu.roll(x, s, 0)
    for i in range(s):
        out = jnp.where(rows == i, halo[SUBLANES - s + i:SUBLANES - s + i + 1, :], out)
    return out


def _unshift_rows(g, carry, s):
    c = g.shape[0]
    rows = lax.broadcasted_iota(jnp.int32, g.shape, 0)
    out = pltpu.roll(g, c - s, 0)
    for i in range(s):
        out = jnp.where(rows == c - s + i, carry[i:i + 1, :], out)
    return out


def _sigmoid(z):
    return 1.0 / (1.0 + jnp.exp(-z))


def _silu(z):
    return z * _sigmoid(z)


def _softplus(z):
    return jnp.maximum(z, 0.0) + jnp.log(1.0 + jnp.exp(-jnp.abs(z)))


def _rms(t, gain):
    return t * lax.rsqrt(jnp.mean(t * t, axis=-1, keepdims=True) + NORM_EPS) * gain


def _iotas(c):
    return lax.broadcasted_iota(jnp.int32, (c, c), 0), lax.broadcasted_iota(jnp.int32, (c, c), 1)


def _unit_lower_inverse(xm, eye):
    t = eye + xm
    p = xm
    for _ in range(5):
        p = _dhi(p, p)
        t = t + _dhi(t, p)
    return t


def _rwkv_head(pr, pk, pv, plo, qr, qk, qv, qlo, s0, pp, mulo, wl):
    c = pr.shape[1]
    ri, ci = _iotas(c)

    def mix(p, q, mu):
        return p + (q - p) * mu

    r = mix(pr, qr, pp[:, 0:1])
    k = mix(pk, qk, pp[:, 1:2])
    v = mix(pv, qv, pp[:, 2:3])
    lo = mix(plo, qlo, mulo)
    w0, a0, k_k, k_a, ln_w, ln_b, r_k = (pp[:, i:i + 1] for i in range(3, 10))
    per_head = lambda t: jnp.broadcast_to(t, (pr.shape[0],) + t.shape)
    zw = _dhi(per_head(jnp.tanh(lo)), wl[:, 0])
    za = _dhi(per_head(lo), wl[:, 1])
    g = _dhi(per_head(_sigmoid(lo)), wl[:, 2])
    w_log = -_softplus(-(w0 + zw)) - 0.5
    lw = -jnp.exp(w_log)
    a = _sigmoid(a0 + za)
    kk = k * k_k
    kk = kk * lax.rsqrt(jnp.sum(kk * kk, axis=-1, keepdims=True) + L2_EPS)
    k2 = k * (1.0 + (a - 1.0) * k_a)
    an = -kk
    b = kk * a
    causal = ri >= ci
    strict = ri > ci
    eye = (ri == ci).astype(F32)
    cl = _cumsum_rows(lw)
    ecl = jnp.exp(-cl)
    at = an * jnp.exp(cl - lw)
    bt = b * ecl
    kt = k2 * ecl
    rt = r * jnp.exp(cl)
    a_ab = jnp.where(strict, _dnt(at, bt), 0.0)
    a_ak = jnp.where(strict, _dnt(at, kt), 0.0)
    tinv = _unit_lower_inverse(a_ab, eye)
    u = _dhi(tinv, _dnt(at, s0) + _dhi(a_ak, v))
    y = _dnt(rt, s0) + _dhi(jnp.where(causal, _dnt(rt, bt), 0.0), u) + _dhi(jnp.where(causal, _dnt(rt, kt), 0.0), v)
    cl_end = jnp.sum(lw, axis=1, keepdims=True)
    dec_end = jnp.exp(cl_end - cl)
    s1 = s0 * jnp.exp(cl_end) + _dtn(u, b * dec_end) + _dtn(v, k2 * dec_end)
    m = (lax.broadcasted_iota(jnp.int32, (1, LANES), 1) < RWKV_HEAD_DIM).astype(F32)
    mean = jnp.sum(y, axis=-1, keepdims=True) * (1.0 / RWKV_HEAD_DIM)
    yc = (y - mean) * m
    var = jnp.sum(yc * yc, axis=-1, keepdims=True) * (1.0 / RWKV_HEAD_DIM)
    yn = yc * lax.rsqrt(var + RWKV_GN_EPS) * ln_w + ln_b
    y2 = yn + jnp.sum(r * k2 * r_k, axis=-1, keepdims=True) * v
    return y2 * g, s1


def _gdn_head(xq, xk, xv, z, ab, s0, cw, gp, oha, ohb):
    c = z.shape[1]
    ri, ci = _iotas(c)

    def conv(xs, w):
        out = xs[0] * w[:, GDN_CONV - 1:GDN_CONV]
        for s in range(1, GDN_CONV):
            out = out + xs[s] * w[:, GDN_CONV - 1 - s:GDN_CONV - s]
        return out

    q = _silu(conv(xq, cw[:, 0]))
    k = _silu(conv(xk, cw[:, 1]))
    v = _silu(conv(xv, cw[:, 2]))
    q = q * lax.rsqrt(jnp.sum(q * q, axis=-1, keepdims=True) + L2_EPS) * (GDN_HEAD_DIM ** -0.5)
    k = k * lax.rsqrt(jnp.sum(k * k, axis=-1, keepdims=True) + L2_EPS)
    gg = -jnp.exp(gp[0:1]) * _softplus(ab + gp[1:2])
    beta = jnp.sum(_sigmoid(ab) * ohb, axis=-1, keepdims=True)
    causal = ri >= ci
    strict = ri > ci
    eye = (ri == ci).astype(F32)
    gcm = _cumsum_rows(gg * oha)
    gc = jnp.sum(gcm, axis=-1, keepdims=True)
    gc_row = _lane_sum_as_row(gcm)
    dec = jnp.where(causal, jnp.exp(jnp.where(causal, gc - gc_row, 0.0)), 0.0)
    kb = k * beta
    vb = v * beta
    lm = jnp.where(strict, _dnt(kb, k) * dec, 0.0)
    tinv = _unit_lower_inverse(-lm, eye)
    egc = jnp.exp(gc)
    u = _dhi(tinv, vb)
    wk = _dhi(tinv, kb * egc)
    attn = jnp.where(causal, _dnt(q, k) * dec, 0.0)
    g_last = gc[:, c - 1:c, :]
    v_new = u - _dhi(wk, s0)
    o = _dhi(q * egc, s0) + _dhi(attn, v_new)
    s1 = s0 * jnp.exp(g_last) + _dtn(k * jnp.exp(g_last - gc), v_new)
    return _rms(o, gp[2:3]) * _silu(z), s1


def _head_id(grp, i, per_step, heads):
    return i if per_step == heads else grp * per_step + i


def _head_range(grp, per_step, heads):
    return slice(None) if per_step == heads else pl.ds(grp * per_step, per_step)


def _stack_heads(ref, n, fn=None):
    parts = []
    for i in range(n):
        sl = slice(i * LANES, (i + 1) * LANES)
        v = ref[:, sl]
        parts.append(v if fn is None else fn(v, sl))
    return jnp.stack(parts)


def _prev_rows_spec(width, col):
    per = CHUNK // SUBLANES
    return pl.BlockSpec((SUBLANES, width), lambda n, h: (jnp.maximum(n * per - 1, 0), col(h)))


def _rwkv_specs(nmap):
    hb, groups = RWKV_HB, RWKV_HEADS // RWKV_HB
    cb = OFF_RKV // (hb * LANES)
    specs = []
    for j in range(3):
        specs.append(pl.BlockSpec((CHUNK, hb * LANES), lambda n, g, j=j: (nmap(n), cb + j * groups + g)))
    specs.append(pl.BlockSpec((CHUNK, 2 * LANES), lambda n, g: (nmap(n), OFF_LO // (2 * LANES))))
    per = CHUNK // SUBLANES
    for j in range(3):
        specs.append(pl.BlockSpec((SUBLANES, hb * LANES),
                                  lambda n, g, j=j: (jnp.maximum(nmap(n) * per - 1, 0), cb + j * groups + g)))
    specs.append(pl.BlockSpec((SUBLANES, 2 * LANES), lambda n, g: (jnp.maximum(nmap(n) * per - 1, 0), OFF_LO // (2 * LANES))))
    specs.append(pl.BlockSpec((hb, 16, LANES), lambda n, g: (g, 0, 0)))
    specs.append(pl.BlockSpec((1, 2 * LANES), lambda n, g: (0, 0)))
    specs.append(pl.BlockSpec((hb, 3, 2 * LANES, LANES), lambda n, g: (g, 0, 0, 0)))
    return specs


def _rwkv_forward(p_cat, ppack, mulo, wl):
    t = p_cat.shape[0]
    n_chunks = t // CHUNK

    def body(pr, pk, pv, plo, hr, hk, hv, hlo, pp, ml, w, out, st_out, s_scr):
        n, grp = pl.program_id(0), pl.program_id(1)

        hsl = _head_range(grp, RWKV_HB, RWKV_HEADS)

        @pl.when(n == 0)
        def _():
            s_scr[hsl] = jnp.zeros((RWKV_HB, LANES, LANES), F32)

        live = (n > 0).astype(F32)
        lo = plo[...]
        lo_prev = _shift_rows(lo, hlo[...] * live, 1)
        cur = [_stack_heads(x, RWKV_HB) for x in (pr, pk, pv)]
        prev = [_stack_heads(x, RWKV_HB, lambda v, sl, hx=hx: _shift_rows(v, hx[:, sl] * live, 1))
                for x, hx in ((pr, hr), (pk, hk), (pv, hv))]
        s0 = s_scr[hsl]
        st_out[...] = s0
        o, s1 = _rwkv_head(*cur, lo, *prev, lo_prev, s0, pp[...], ml[...], w[...])
        for i in range(RWKV_HB):
            out[:, i * LANES:(i + 1) * LANES] = o[i].astype(out.dtype)
        s_scr[hsl] = s1

    return pl.pallas_call(
        body, name="rwkv_fwd", grid=(n_chunks, RWKV_HEADS // RWKV_HB),
        in_specs=_rwkv_specs(lambda n: n),
        out_specs=(pl.BlockSpec((CHUNK, RWKV_HB * LANES), lambda n, g: (n, g)),
                   pl.BlockSpec((None, RWKV_HB, LANES, LANES), lambda n, g: (n, g, 0, 0))),
        out_shape=(jax.ShapeDtypeStruct((t, RWKV_HEADS * LANES), BF16),
                   jax.ShapeDtypeStruct((n_chunks, RWKV_HEADS, LANES, LANES), F32)),
        scratch_shapes=[pltpu.VMEM((RWKV_HEADS, LANES, LANES), F32)],
        compiler_params=_cparams(("arbitrary", "arbitrary")),
    )(p_cat, p_cat, p_cat, p_cat, p_cat, p_cat, p_cat, p_cat, ppack, mulo, wl)


def _rwkv_backward(p_cat, ppack, mulo, wl, states, d_out):
    t = p_cat.shape[0]
    n_chunks = t // CHUNK
    last = n_chunks - 1

    def body(pr, pk, pv, plo, hr, hk, hv, hlo, pp, ml, w, st, dy, dpr, dpk, dpv, dplo, dpp, dml, dw, ds_scr, car_scr, carlo_scr):
        n, grp = pl.program_id(0), pl.program_id(1)

        hsl = _head_range(grp, RWKV_HB, RWKV_HEADS)
        gi = _head_id(grp, 0, 1, RWKV_HEADS // RWKV_HB)

        @pl.when(n == 0)
        def _():
            ds_scr[hsl] = jnp.zeros((RWKV_HB, LANES, LANES), F32)
            car_scr[hsl] = jnp.zeros((RWKV_HB, 3 * SUBLANES, LANES), F32)
            carlo_scr[gi] = jnp.zeros((SUBLANES, 2 * LANES), F32)

        @pl.when((n == 0) & (grp == 0))
        def _():
            dpp[...] = jnp.zeros(dpp.shape, F32)
            dml[...] = jnp.zeros(dml.shape, F32)
            dw[...] = jnp.zeros(dw.shape, F32)

        live = (n < last).astype(F32)
        lo = plo[...]
        lo_prev = _shift_rows(lo, hlo[...] * live, 1)
        cur = [_stack_heads(x, RWKV_HB) for x in (pr, pk, pv)]
        prev = [_stack_heads(x, RWKV_HB, lambda v, sl, hx=hx: _shift_rows(v, hx[:, sl] * live, 1))
                for x, hx in ((pr, hr), (pk, hk), (pv, hv))]
        _, vjp = jax.vjp(_rwkv_head, *cur, lo, *prev, lo_prev, st[...], pp[...], ml[...], w[...])
        g = vjp((_stack_heads(dy, RWKV_HB), ds_scr[hsl]))
        outs = (dpr, dpk, dpv)
        for i in range(RWKV_HB):
            sl = slice(i * LANES, (i + 1) * LANES)
            h = _head_id(grp, i, RWKV_HB, RWKV_HEADS)
            car = car_scr[h]
            for j in range(3):
                tot = g[j][i] + _unshift_rows(g[4 + j][i], car[SUBLANES * j:SUBLANES * (j + 1), :], 1)
                outs[j][:, sl] = tot.astype(outs[j].dtype)
                car_scr[h, SUBLANES * j:SUBLANES * (j + 1), :] = g[4 + j][i][0:SUBLANES, :]
        dlo = g[3] + _unshift_rows(g[7], carlo_scr[gi], 1)
        carlo_scr[gi] = g[7][0:SUBLANES, :]
        ds_scr[hsl] = g[8]
        dpp[hsl] += g[9]
        dml[0, 0:1, :] += g[10]
        dw[hsl] += g[11]

        @pl.when(grp == 0)
        def _():
            dplo[...] = dlo

        @pl.when(grp > 0)
        def _():
            dplo[...] += dlo

    rev = lambda n: last - n
    in_specs = _rwkv_specs(rev) + [
        pl.BlockSpec((None, RWKV_HB, LANES, LANES), lambda n, g: (rev(n), g, 0, 0)),
        pl.BlockSpec((CHUNK, RWKV_HB * LANES), lambda n, g: (rev(n), g)),
    ]
    hw = RWKV_HEADS * LANES
    return pl.pallas_call(
        body, name="rwkv_bwd", grid=(n_chunks, RWKV_HEADS // RWKV_HB),
        in_specs=in_specs,
        out_specs=(pl.BlockSpec((CHUNK, RWKV_HB * LANES), lambda n, g: (rev(n), g)),
                   pl.BlockSpec((CHUNK, RWKV_HB * LANES), lambda n, g: (rev(n), g)),
                   pl.BlockSpec((CHUNK, RWKV_HB * LANES), lambda n, g: (rev(n), g)),
                   pl.BlockSpec((CHUNK, 2 * LANES), lambda n, h: (rev(n), 0)),
                   pl.BlockSpec((RWKV_HEADS, 16, LANES), lambda n, h: (0, 0, 0)),
                   pl.BlockSpec((RWKV_HEADS, SUBLANES, 2 * LANES), lambda n, h: (0, 0, 0)),
                   pl.BlockSpec((RWKV_HEADS, 3, 2 * LANES, LANES), lambda n, h: (0, 0, 0, 0))),
        out_shape=(jax.ShapeDtypeStruct((t, hw), BF16), jax.ShapeDtypeStruct((t, hw), BF16), jax.ShapeDtypeStruct((t, hw), BF16),
                   jax.ShapeDtypeStruct((t, 2 * LANES), F32),
                   jax.ShapeDtypeStruct((RWKV_HEADS, 16, LANES), F32),
                   jax.ShapeDtypeStruct((RWKV_HEADS, SUBLANES, 2 * LANES), F32),
                   jax.ShapeDtypeStruct((RWKV_HEADS, 3, 2 * LANES, LANES), F32)),
        scratch_shapes=[pltpu.VMEM((RWKV_HEADS, LANES, LANES), F32),
                        pltpu.VMEM((RWKV_HEADS, 3 * SUBLANES, LANES), F32),
                        pltpu.VMEM((RWKV_HEADS, SUBLANES, 2 * LANES), F32)],
        compiler_params=_cparams(("arbitrary", "arbitrary")),
    )(p_cat, p_cat, p_cat, p_cat, p_cat, p_cat, p_cat, p_cat, ppack, mulo, wl, states, d_out)


def _gdn_specs(nmap):
    cb = OFF_QKV // LANES
    per = CHUNK // SUBLANES
    hb, groups = GDN_HB, GDN_HEADS // GDN_HB
    cb = OFF_QKV // (hb * LANES)
    specs = []
    for j in range(3):
        specs.append(pl.BlockSpec((CHUNK, hb * LANES), lambda n, g, j=j: (nmap(n), cb + j * groups + g)))
    for j in range(3):
        specs.append(pl.BlockSpec((SUBLANES, hb * LANES),
                                  lambda n, g, j=j: (jnp.maximum(nmap(n) * per - 1, 0), cb + j * groups + g)))
    specs.append(pl.BlockSpec((CHUNK, hb * LANES), lambda n, g: (nmap(n), OFF_Z // (hb * LANES) + g)))
    specs.append(pl.BlockSpec((CHUNK, LANES), lambda n, g: (nmap(n), OFF_AB // LANES)))
    specs.append(pl.BlockSpec((hb, 3, SUBLANES, LANES), lambda n, g: (g, 0, 0, 0)))
    specs.append(pl.BlockSpec((SUBLANES, LANES), lambda n, g: (0, 0)))
    return specs


def _conv_taps(x, halo):
    return (x,) + tuple(_shift_rows(x, halo, s) for s in range(1, GDN_CONV))


def _onehots(grp):
    lane = lax.broadcasted_iota(jnp.int32, (GDN_HB, 1, LANES), 2)
    head = lax.broadcasted_iota(jnp.int32, (GDN_HB, 1, LANES), 0) + _head_id(grp, 0, GDN_HB, GDN_HEADS)
    return (lane == head).astype(F32), (lane == GDN_HEADS + head).astype(F32)


def _gdn_taps(refs, halos, live):
    out = []
    for x, hx in zip(refs, halos):
        per_head = [_conv_taps(x[:, i * LANES:(i + 1) * LANES], hx[:, i * LANES:(i + 1) * LANES] * live) for i in range(GDN_HB)]
        out.append(tuple(jnp.stack([per_head[i][s] for i in range(GDN_HB)]) for s in range(GDN_CONV)))
    return out


def _gdn_forward(p_cat, cwpack, gpar):
    t = p_cat.shape[0]
    n_chunks = t // CHUNK

    def body(xq, xk, xv, hq, hk, hv, z, ab, cw, gp, out, st_out, s_scr):
        n, grp = pl.program_id(0), pl.program_id(1)

        hsl = _head_range(grp, GDN_HB, GDN_HEADS)

        @pl.when(n == 0)
        def _():
            s_scr[hsl] = jnp.zeros((GDN_HB, LANES, LANES), F32)

        live = (n > 0).astype(F32)
        oha, ohb = _onehots(grp)
        s0 = s_scr[hsl]
        st_out[...] = s0
        taps = _gdn_taps((xq, xk, xv), (hq, hk, hv), live)
        o, s1 = _gdn_head(*taps, _stack_heads(z, GDN_HB), ab[...], s0, cw[...], gp[...], oha, ohb)
        for i in range(GDN_HB):
            out[:, i * LANES:(i + 1) * LANES] = o[i].astype(out.dtype)
        s_scr[hsl] = s1

    return pl.pallas_call(
        body, name="gdn_fwd", grid=(n_chunks, GDN_HEADS // GDN_HB),
        in_specs=_gdn_specs(lambda n: n),
        out_specs=(pl.BlockSpec((CHUNK, GDN_HB * LANES), lambda n, g: (n, g)),
                   pl.BlockSpec((None, GDN_HB, LANES, LANES), lambda n, g: (n, g, 0, 0))),
        out_shape=(jax.ShapeDtypeStruct((t, GDN_WIDTH), BF16),
                   jax.ShapeDtypeStruct((n_chunks, GDN_HEADS, LANES, LANES), F32)),
        scratch_shapes=[pltpu.VMEM((GDN_HEADS, LANES, LANES), F32)],
        compiler_params=_cparams(("arbitrary", "arbitrary")),
    )(p_cat, p_cat, p_cat, p_cat, p_cat, p_cat, p_cat, p_cat, cwpack, gpar)


def _gdn_backward(p_cat, cwpack, gpar, states, d_out):
    t = p_cat.shape[0]
    n_chunks = t // CHUNK
    last = n_chunks - 1

    def body(xq, xk, xv, hq, hk, hv, z, ab, cw, gp, st, dy, dq, dk, dv, dz, dab, dcw, dgp, ds_scr, car_scr):
        n, grp = pl.program_id(0), pl.program_id(1)

        hsl = _head_range(grp, GDN_HB, GDN_HEADS)

        @pl.when(n == 0)
        def _():
            ds_scr[hsl] = jnp.zeros((GDN_HB, LANES, LANES), F32)
            car_scr[hsl] = jnp.zeros((GDN_HB, 3 * GDN_CONV, SUBLANES, LANES), F32)

        @pl.when((n == 0) & (grp == 0))
        def _():
            dcw[...] = jnp.zeros(dcw.shape, F32)
            dgp[...] = jnp.zeros(dgp.shape, F32)

        live = (n < last).astype(F32)
        oha, ohb = _onehots(grp)
        fn = functools.partial(_gdn_head, oha=oha, ohb=ohb)
        taps = _gdn_taps((xq, xk, xv), (hq, hk, hv), live)
        _, vjp = jax.vjp(fn, *taps, _stack_heads(z, GDN_HB), ab[...], st[...], cw[...], gp[...])
        g = vjp((_stack_heads(dy, GDN_HB), ds_scr[hsl]))
        outs = (dq, dk, dv)
        for i in range(GDN_HB):
            sl = slice(i * LANES, (i + 1) * LANES)
            h = _head_id(grp, i, GDN_HB, GDN_HEADS)
            for j in range(3):
                tot = g[j][0][i]
                for s in range(1, GDN_CONV):
                    slot = j * GDN_CONV + s
                    tot = tot + _unshift_rows(g[j][s][i], car_scr[h, slot], s)
                    car_scr[h, slot] = g[j][s][i][0:SUBLANES, :]
                outs[j][:, sl] = tot.astype(outs[j].dtype)
            dz[:, sl] = g[3][i].astype(dz.dtype)
        dab_sum = g[4]
        ds_scr[hsl] = g[5]
        dcw[hsl] += g[6]
        dgp[0] += g[7]

        @pl.when(grp == 0)
        def _():
            dab[...] = dab_sum

        @pl.when(grp > 0)
        def _():
            dab[...] += dab_sum

    rev = lambda n: last - n
    in_specs = _gdn_specs(rev) + [
        pl.BlockSpec((None, GDN_HB, LANES, LANES), lambda n, g: (rev(n), g, 0, 0)),
        pl.BlockSpec((CHUNK, GDN_HB * LANES), lambda n, g: (rev(n), g)),
    ]
    blk = pl.BlockSpec((CHUNK, GDN_HB * LANES), lambda n, g: (rev(n), g))
    return pl.pallas_call(
        body, name="gdn_bwd", grid=(n_chunks, GDN_HEADS // GDN_HB),
        in_specs=in_specs,
        out_specs=(blk, blk, blk, blk,
                   pl.BlockSpec((CHUNK, LANES), lambda n, h: (rev(n), 0)),
                   pl.BlockSpec((GDN_HEADS, 3, SUBLANES, LANES), lambda n, h: (0, 0, 0, 0)),
                   pl.BlockSpec((GDN_HEADS, SUBLANES, LANES), lambda n, h: (0, 0, 0))),
        out_shape=(jax.ShapeDtypeStruct((t, GDN_WIDTH), BF16), jax.ShapeDtypeStruct((t, GDN_WIDTH), BF16),
                   jax.ShapeDtypeStruct((t, GDN_WIDTH), BF16), jax.ShapeDtypeStruct((t, GDN_WIDTH), BF16),
                   jax.ShapeDtypeStruct((t, LANES), F32),
                   jax.ShapeDtypeStruct((GDN_HEADS, 3, SUBLANES, LANES), F32),
                   jax.ShapeDtypeStruct((GDN_HEADS, SUBLANES, LANES), F32)),
        scratch_shapes=[pltpu.VMEM((GDN_HEADS, LANES, LANES), F32),
                        pltpu.VMEM((GDN_HEADS, 3 * GDN_CONV, SUBLANES, LANES), F32)],
        compiler_params=_cparams(("arbitrary", "arbitrary")),
    )(p_cat, p_cat, p_cat, p_cat, p_cat, p_cat, p_cat, p_cat, cwpack, gpar, states, d_out)


def _pick(n, options):
    for o in options:
        if n % o == 0:
            return o
    raise ValueError(f"no tile for {n}")


MM_VMEM_BUDGET = 30 * 1024 * 1024
MM_MIN_STEPS = 8


def _mm_tiles(mode, m, n, k, out_bytes):
    tms = [t for t in (2048, 1024, 768, 512, 256, 128, 64) if m % t == 0 and (mode != 'tn' or t % LANES == 0)]
    tns = [t for t in (1408, 1024, 768, 512, 256, 128) if n % t == 0]
    tks = [t for t in (2048, 1920, 1408, 1024, 512, 256, 128, 64) if k % t == 0]
    best, best_key = None, None
    for tm in tms:
        for tn in tns:
            for tk in tks:
                nk = k // tk
                vmem = 2 * (tm * tk * 2 + tk * tn * 2 + tm * tn * out_bytes) + (tm * tn * 4 if nk > 1 else 0)
                steps = (m // tm) * (n // tn) * nk
                if vmem > MM_VMEM_BUDGET:
                    continue
                key = (steps >= MM_MIN_STEPS, tn if mode == 'tn' else 0, tm * tn * tk, -nk)
                if best_key is None or key > best_key:
                    best, best_key = (tm, tn, tk), key
    if best is None:
        raise ValueError(f"no matmul tile for {mode} {m}x{n}x{k}")
    return best


_MM_DIMS = {'nn': (((1,), (0,)), ((), ())), 'nt': (((1,), (1,)), ((), ())), 'tn': (((0,), (0,)), ((), ()))}


def _matmul(a, b, mode, out_dtype, name):
    if mode == 'nn':
        (m, k), (k2, n) = a.shape, b.shape
    elif mode == 'nt':
        (m, k), (n, k2) = a.shape, b.shape
    else:
        (k, m), (k2, n) = a.shape, b.shape
    assert k == k2, (a.shape, b.shape, mode)
    tm, tn, tk = _mm_tiles(mode, m, n, k, jnp.dtype(out_dtype).itemsize)
    nk = k // tk
    dims = _MM_DIMS[mode]

    def body(a_ref, b_ref, o_ref, acc_ref):
        kk = pl.program_id(2)
        part = lax.dot_general(a_ref[...], b_ref[...], dims, preferred_element_type=F32)
        if nk == 1:
            o_ref[...] = part.astype(o_ref.dtype)
            return

        @pl.when(kk == 0)
        def _():
            acc_ref[...] = part

        @pl.when((kk > 0) & (kk < nk - 1))
        def _():
            acc_ref[...] += part

        @pl.when(kk == nk - 1)
        def _():
            o_ref[...] = (acc_ref[...] + part).astype(o_ref.dtype)

    a_spec = pl.BlockSpec((tk, tm), lambda i, j, kk: (kk, i)) if mode == 'tn' else pl.BlockSpec((tm, tk), lambda i, j, kk: (i, kk))
    b_spec = pl.BlockSpec((tn, tk), lambda i, j, kk: (j, kk)) if mode == 'nt' else pl.BlockSpec((tk, tn), lambda i, j, kk: (kk, j))
    return pl.pallas_call(
        body, name=name, grid=(m // tm, n // tn, nk),
        in_specs=[a_spec, b_spec],
        out_specs=pl.BlockSpec((tm, tn), lambda i, j, kk: (i, j)),
        out_shape=jax.ShapeDtypeStruct((m, n), out_dtype),
        scratch_shapes=[pltpu.VMEM((tm, tn), F32)],
        compiler_params=_cparams(("parallel", "parallel", "arbitrary")),
    )(a, b)


ROW_TILE = 256


def _row_specs(rows, tm):
    return [pl.BlockSpec((tm, w), lambda i, ci=ci: (i, ci)) for (_, w, ci) in rows]


def _rw_forward(fn, rows, pars, outs, name):
    t = rows[0][0].shape[0]
    tm = min(ROW_TILE, t)
    nr, npar = len(rows), len(pars)

    def body(*refs):
        vals = [r[...].astype(F32) for r in refs[:nr]] + [p[...] for p in refs[nr:nr + npar]]
        res = fn(*vals)
        for o, v in zip(refs[nr + npar:], res):
            o[...] = v.astype(o.dtype)

    return pl.pallas_call(
        body, name=name, grid=(t // tm,),
        in_specs=_row_specs(rows, tm) + [pl.BlockSpec(p.shape, lambda i: (0, 0)) for p in pars],
        out_specs=tuple(pl.BlockSpec((tm, w), lambda i: (i, 0)) for (w, _) in outs),
        out_shape=tuple(jax.ShapeDtypeStruct((t, w), dt) for (w, dt) in outs),
        compiler_params=_cparams(("parallel",)),
    )(*[r[0] for r in rows], *pars)


def _rw_backward(fn, rows, pars, cots, drow_dtypes, name):
    t = rows[0][0].shape[0]
    tm = min(ROW_TILE, t)
    nr, npar, nc = len(rows), len(pars), len(cots)
    keep = [i for i, dt in enumerate(drow_dtypes) if dt is not None]

    def body(*refs):
        vals = [r[...].astype(F32) for r in refs[:nr]] + [p[...] for p in refs[nr:nr + npar]]
        cvals = tuple(c[...].astype(F32) for c in refs[nr + npar:nr + npar + nc])
        orefs = refs[nr + npar + nc:]
        _, vjp = jax.vjp(fn, *vals)
        g = vjp(cvals)
        for o, i in zip(orefs[:len(keep)], keep):
            o[...] = g[i].astype(o.dtype)
        first = pl.program_id(0) == 0
        for o, gi in zip(orefs[len(keep):], g[nr:]):
            @pl.when(first)
            def _(o=o, gi=gi):
                o[...] = gi

            @pl.when(jnp.logical_not(first))
            def _(o=o, gi=gi):
                o[...] += gi

    out_specs = [pl.BlockSpec((tm, rows[i][1]), lambda i_: (i_, 0)) for i in keep] + \
                [pl.BlockSpec(p.shape, lambda i_: (0, 0)) for p in pars]
    out_shape = [jax.ShapeDtypeStruct((t, rows[i][1]), drow_dtypes[i]) for i in keep] + \
                [jax.ShapeDtypeStruct(p.shape, F32) for p in pars]
    return pl.pallas_call(
        body, name=name, grid=(t // tm,),
        in_specs=_row_specs(rows, tm) + [pl.BlockSpec(p.shape, lambda i: (0, 0)) for p in pars] + _row_specs(cots, tm),
        out_specs=tuple(out_specs), out_shape=tuple(out_shape),
        compiler_params=_cparams(("arbitrary",)),
    )(*[r[0] for r in rows], *pars, *[c[0] for c in cots])


def _norm_fn(x, g):
    return (_rms(x, g),)


def _norm_skip_fn(x, g):
    return _rms(x, g), x


def _merge_fn(ga, gb, ya, yb):
    return (_sigmoid(ga) * ya + _sigmoid(gb) * yb,)


def _res_norm_fn(x, mo, g):
    x1 = x + mo
    return x1, _rms(x1, g)


def _loss_head(x1, fo, gf, target, name):
    t, d = x1.shape
    tm = min(ROW_TILE, t)

    def tile_loss(x2, g, tgt):
        err = _rms(x2, g) - tgt
        per_row = jnp.sum(err * err, axis=-1, keepdims=True) * (0.5 / d)
        return jnp.sum(per_row, axis=0, keepdims=True)

    def body(x1_ref, fo_ref, g_ref, t_ref, loss_ref, dx_ref, dxb_ref, dg_ref):
        x2 = x1_ref[...] + fo_ref[...]
        val, vjp = jax.vjp(functools.partial(tile_loss, tgt=t_ref[...]), x2, g_ref[...])
        dx2, dg = vjp(jnp.ones((1, 1), F32))
        dx_ref[...] = dx2
        dxb_ref[...] = dx2.astype(BF16)
        first = pl.program_id(0) == 0

        @pl.when(first)
        def _():
            loss_ref[...] = jnp.broadcast_to(val, loss_ref.shape)
            dg_ref[...] = dg

        @pl.when(jnp.logical_not(first))
        def _():
            loss_ref[...] += jnp.broadcast_to(val, loss_ref.shape)
            dg_ref[...] += dg

    row = pl.BlockSpec((tm, d), lambda i: (i, 0))
    vec = pl.BlockSpec((1, d), lambda i: (0, 0))
    return pl.pallas_call(
        body, name=name, grid=(t // tm,),
        in_specs=[row, row, vec, row],
        out_specs=(pl.BlockSpec((1, LANES), lambda i: (0, 0)), row, row, vec),
        out_shape=(jax.ShapeDtypeStruct((1, LANES), F32), jax.ShapeDtypeStruct((t, d), F32),
                   jax.ShapeDtypeStruct((t, d), BF16), jax.ShapeDtypeStruct((1, d), F32)),
        compiler_params=_cparams(("arbitrary",)),
    )(x1, fo, gf, target)


FFN_TILE_ROWS = 512
FFN_TILE_COLS = 256
FFN_COL_BLOCKS = FFN_HIDDEN // FFN_TILE_COLS


def _conv3_past(x, halo, w):
    rows = lax.broadcasted_iota(jnp.int32, x.shape, 0)
    x1 = jnp.where(rows == 0, halo[7:8, :], pltpu.roll(x, 1, 0))
    x2 = jnp.where(rows == 0, halo[6:7, :], jnp.where(rows == 1, halo[7:8, :], pltpu.roll(x, 2, 0)))
    return x * w[2:3] + x1 * w[1:2] + x2 * w[0:1], x1, x2


def _ffn_in_specs(tm, imap, jmap):
    per = tm // SUBLANES
    tile = lambda off: pl.BlockSpec((tm, FFN_TILE_COLS), lambda *g: (imap(*g), off + jmap(*g) % FFN_COL_BLOCKS))
    halo = lambda off: pl.BlockSpec((SUBLANES, FFN_TILE_COLS),
                                    lambda *g: (jnp.maximum(imap(*g) * per - 1, 0), off + jmap(*g) % FFN_COL_BLOCKS))
    wsp = lambda off: pl.BlockSpec((FFN_CONV, FFN_TILE_COLS), lambda *g: (0, off + jmap(*g) % FFN_COL_BLOCKS))
    return [tile(0), halo(0), wsp(0), tile(FFN_COL_BLOCKS), halo(FFN_COL_BLOCKS), wsp(FFN_COL_BLOCKS)]


def _ffn_act_forward(hpre, cw):
    t = hpre.shape[0]
    tm = min(FFN_TILE_ROWS, t)

    def body(hg, pg, wg, hu, pu, wu, out):
        live = (pl.program_id(0) > 0).astype(F32)
        cg, _, _ = _conv3_past(hg[...], pg[...] * live, wg[...])
        cu, _, _ = _conv3_past(hu[...], pu[...] * live, wu[...])
        out[...] = (_silu(cg) * cu).astype(out.dtype)

    return pl.pallas_call(
        body, name="ffn_act_fwd", grid=(t // tm, FFN_COL_BLOCKS),
        in_specs=_ffn_in_specs(tm, lambda i, j: i, lambda i, j: j),
        out_specs=pl.BlockSpec((tm, FFN_TILE_COLS), lambda i, j: (i, j)),
        out_shape=jax.ShapeDtypeStruct((t, FFN_HIDDEN), BF16),
        compiler_params=_cparams(("parallel", "parallel")),
    )(hpre, hpre, cw, hpre, hpre, cw)


def _conv3_future(d, nxt, w):
    tm = d.shape[0]
    rows = lax.broadcasted_iota(jnp.int32, d.shape, 0)
    d1 = jnp.where(rows == tm - 1, nxt[0:1, :], pltpu.roll(d, tm - 1, 0))
    d2 = jnp.where(rows == tm - 1, nxt[1:2, :], jnp.where(rows == tm - 2, nxt[0:1, :], pltpu.roll(d, tm - 2, 0)))
    return d * w[2:3] + d1 * w[1:2] + d2 * w[0:1]


def _ffn_backward(hpre, cw, dact):
    t = hpre.shape[0]
    tm = min(FFN_TILE_ROWS, t)
    n_tiles = t // tm
    per = tm // SUBLANES

    def d_conv_out(cg, cu, d):
        s = _sigmoid(cg)
        return d * cu * s * (1.0 + cg * (1.0 - s)), d * cg * s

    def body(hg, pg, ng, wg, hu, pu, nu, wu, da, dan, dhg, dhu, dwg, dwu):
        i = pl.program_id(1)
        live_prev = (i > 0).astype(F32)
        live_next = (i < n_tiles - 1).astype(F32)
        xg, xu = hg[...], hu[...]
        cg, g1, g2 = _conv3_past(xg, pg[...] * live_prev, wg[...])
        cu, u1, u2 = _conv3_past(xu, pu[...] * live_prev, wu[...])
        dg, du = d_conv_out(cg, cu, da[...])
        cgn, _, _ = _conv3_past(ng[...], hg[tm - SUBLANES:tm, :], wg[...])
        cun, _, _ = _conv3_past(nu[...], hu[tm - SUBLANES:tm, :], wu[...])
        dgn, dun = d_conv_out(cgn, cun, dan[...] * live_next)
        dhg[...] = _conv3_future(dg, dgn, wg[...]).astype(dhg.dtype)
        dhu[...] = _conv3_future(du, dun, wu[...]).astype(dhu.dtype)
        sums_g = [jnp.sum(xs * dg, axis=0, keepdims=True) for xs in (g2, g1, xg)]
        sums_u = [jnp.sum(xs * du, axis=0, keepdims=True) for xs in (u2, u1, xu)]

        @pl.when(i == 0)
        def _():
            for r_ in range(FFN_CONV):
                dwg[r_:r_ + 1, :] = sums_g[r_]
                dwu[r_:r_ + 1, :] = sums_u[r_]

        @pl.when(i > 0)
        def _():
            for r_ in range(FFN_CONV):
                dwg[r_:r_ + 1, :] += sums_g[r_]
                dwu[r_:r_ + 1, :] += sums_u[r_]

    nb = FFN_COL_BLOCKS
    nxt = lambda i: jnp.minimum((i + 1) * per, t // SUBLANES - 1)
    prv = lambda i: jnp.maximum(i * per - 1, 0)
    half = lambda off: [pl.BlockSpec((tm, FFN_TILE_COLS), lambda j, i: (i, off + j)),
                        pl.BlockSpec((SUBLANES, FFN_TILE_COLS), lambda j, i: (prv(i), off + j)),
                        pl.BlockSpec((SUBLANES, FFN_TILE_COLS), lambda j, i: (nxt(i), off + j)),
                        pl.BlockSpec((FFN_CONV, FFN_TILE_COLS), lambda j, i: (0, off + j))]
    tile = pl.BlockSpec((tm, FFN_TILE_COLS), lambda j, i: (i, j))
    taps = pl.BlockSpec((FFN_CONV, FFN_TILE_COLS), lambda j, i: (0, j))
    return pl.pallas_call(
        body, name="ffn_bwd", grid=(nb, n_tiles),
        in_specs=half(0) + half(nb) + [tile, pl.BlockSpec((SUBLANES, FFN_TILE_COLS), lambda j, i: (nxt(i), j))],
        out_specs=(tile, tile, taps, taps),
        out_shape=(jax.ShapeDtypeStruct((t, FFN_HIDDEN), BF16), jax.ShapeDtypeStruct((t, FFN_HIDDEN), BF16),
                   jax.ShapeDtypeStruct((FFN_CONV, FFN_HIDDEN), F32), jax.ShapeDtypeStruct((FFN_CONV, FFN_HIDDEN), F32)),
        compiler_params=_cparams(("parallel", "arbitrary")),
    )(hpre, hpre, hpre, cw, hpre, hpre, hpre, cw, dact, dact)


def _my_place():
    x, y, c = lax.axis_index("x"), lax.axis_index("y"), lax.axis_index("c")
    return x, y, c, 4 * x + 2 * y + c


N_CHIPS = 4


def _remote(src, dst, send_sem, recv_sem, dev):
    return pltpu.make_async_remote_copy(src_ref=src, dst_ref=dst, send_sem=send_sem, recv_sem=recv_sem, device_id=dev,
                                        device_id_type=pl.DeviceIdType.MESH)


def _chip_peer(x, y, k):
    return x ^ ((k >> 1) & 1), y ^ (k & 1)


def _all_gather_two_level(shard, name):
    r, w = shard.shape

    def body(src, out, send_sems, recv_sems, local_sem):
        x, y, c, me = _my_place()
        sibling = (x, y, 1 - c)
        mine = pltpu.make_async_copy(src, out.at[me], local_sem)
        mine.start()
        first = [_remote(src, out.at[me], send_sems.at[0], recv_sems.at[0], sibling)]
        for k in range(1, N_CHIPS):
            px, py = _chip_peer(x, y, k)
            first.append(_remote(src, out.at[me], send_sems.at[k], recv_sems.at[k], (px, py, c)))
        for cp in first:
            cp.start()
        passed = []
        for k in range(1, N_CHIPS):
            px, py = _chip_peer(x, y, k)
            landed = out.at[me ^ (2 * k)]
            _remote(src, landed, send_sems.at[k], recv_sems.at[k], (px, py, c)).wait_recv()
            fwd = _remote(landed, landed, send_sems.at[N_CHIPS - 1 + k], recv_sems.at[N_CHIPS - 1 + k], sibling)
            fwd.start()
            passed.append(fwd)
        _remote(src, out.at[me ^ 1], send_sems.at[0], recv_sems.at[0], sibling).wait_recv()
        for k in range(1, N_CHIPS):
            got = out.at[(me ^ 1) ^ (2 * k)]
            _remote(got, got, send_sems.at[N_CHIPS - 1 + k], recv_sems.at[N_CHIPS - 1 + k], sibling).wait_recv()
        for cp in first + passed:
            cp.wait_send()
        mine.wait()

    return pl.pallas_call(
        body, name=name,
        in_specs=[pl.BlockSpec(memory_space=pl.ANY)],
        out_specs=pl.BlockSpec(memory_space=pl.ANY),
        out_shape=jax.ShapeDtypeStruct((N_DEV, r, w), shard.dtype),
        scratch_shapes=[pltpu.SemaphoreType.DMA((N_DEV - 1,)), pltpu.SemaphoreType.DMA((N_DEV - 1,)), pltpu.SemaphoreType.DMA],
    )(shard)


def _device_peer(x, y, c, k):
    px, py, pc = x ^ ((k >> 2) & 1), y ^ ((k >> 1) & 1), c ^ (k & 1)
    return (px, py, pc), 4 * px + 2 * py + pc


_HBM = pl.BlockSpec(memory_space=pltpu.HBM)
_SEM = pl.BlockSpec(memory_space=pltpu.SEMAPHORE)


def _gather_start(shard, name):
    def body(src, land, send_sems, recv_sems, src_thru, land_thru, token):
        x, y, c, me = _my_place()
        for k in range(1, N_DEV):
            dev, _ = _device_peer(x, y, c, k)
            _remote(src, land.at[me], send_sems.at[k], recv_sems.at[k], dev).start()
        token[...] = jnp.zeros_like(token)

    landing = lax.empty((N_DEV,) + shard.shape, shard.dtype)
    return pl.pallas_call(
        body, name=name,
        out_shape=(pltpu.SemaphoreType.DMA((N_DEV,)), pltpu.SemaphoreType.DMA((N_DEV,)), pltpu.HBM(shard.shape, shard.dtype),
                   pltpu.HBM(landing.shape, landing.dtype), jax.ShapeDtypeStruct((SUBLANES, LANES), F32)),
        in_specs=(_HBM, _HBM), out_specs=(_SEM, _SEM, _HBM, _HBM, pl.BlockSpec(memory_space=pltpu.VMEM)),
        input_output_aliases={0: 2, 1: 3},
        compiler_params=pltpu.CompilerParams(has_side_effects=pltpu.SideEffectType.DATAFLOW_SIDE_EFFECTING),
    )(pltpu.with_memory_space_constraint(shard, pltpu.HBM), pltpu.with_memory_space_constraint(landing, pltpu.HBM))


def _gather_wait(send_sems, recv_sems, shard, landing, after, name):
    n_after = len(after)

    def body(*refs):
        src, land, send_sems, recv_sems = refs[:4]
        x, y, c, _ = _my_place()
        for k in range(1, N_DEV):
            dev, idx = _device_peer(x, y, c, k)
            cp = _remote(src, land.at[idx], send_sems.at[k], recv_sems.at[k], dev)
            cp.wait_send()
            cp.wait_recv()

    return pl.pallas_call(
        body, name=name,
        out_shape=(pltpu.HBM(shard.shape, shard.dtype), pltpu.HBM(landing.shape, landing.dtype)),
        in_specs=(_HBM, _HBM, _SEM, _SEM) + (pl.BlockSpec(memory_space=pl.ANY),) * n_after, out_specs=(_HBM, _HBM),
        input_output_aliases={0: 0, 1: 1},
        compiler_params=pltpu.CompilerParams(has_side_effects=pltpu.SideEffectType.DATAFLOW_SIDE_EFFECTING),
    )(shard, landing, send_sems, recv_sems, *after)[1]


def _pair_exchange(arrays, name):
    na = len(arrays)

    def body(*refs):
        srcs, dsts, (send_sems, recv_sems) = refs[:na], refs[na:2 * na], refs[2 * na:]
        x, y, c, _ = _my_place()
        sibling = (x, y, 1 - c)
        copies = []
        for i in range(na):
            for q in range(N_CHIPS):
                s = i * N_CHIPS + q
                copies.append(_remote(srcs[i].at[2 * q + 1 - c], dsts[i].at[q], send_sems.at[s], recv_sems.at[s], sibling))
        for cp in copies:
            cp.start()
        for cp in copies:
            cp.wait_recv()
        for cp in copies:
            cp.wait_send()

    hbm = pl.BlockSpec(memory_space=pl.ANY)
    return pl.pallas_call(
        body, name=name, in_specs=[hbm] * na, out_specs=tuple([hbm] * na),
        out_shape=tuple(jax.ShapeDtypeStruct((N_CHIPS,) + a.shape[1:], a.dtype) for a in arrays),
        scratch_shapes=[pltpu.SemaphoreType.DMA((na * N_CHIPS,)), pltpu.SemaphoreType.DMA((na * N_CHIPS,))],
    )(*arrays)


ELEMENTWISE_COLS = 256


def _pair_sum(slabs, recv, core, out_dtype, name):
    _, r, w = slabs.shape
    tc = ELEMENTWISE_COLS

    def body(core_ref, mine, theirs, out):
        out[...] = (mine[...] + theirs[...]).astype(out.dtype)

    grid_spec = pltpu.PrefetchScalarGridSpec(
        num_scalar_prefetch=1, grid=(N_CHIPS, w // tc),
        in_specs=[pl.BlockSpec((None, r, tc), lambda q, j, core_ref: (2 * q + core_ref[0], 0, j)),
                  pl.BlockSpec((None, r, tc), lambda q, j, core_ref: (q, 0, j))],
        out_specs=pl.BlockSpec((None, r, tc), lambda q, j, core_ref: (q, 0, j)))
    return pl.pallas_call(body, name=name, grid_spec=grid_spec,
                          out_shape=jax.ShapeDtypeStruct((N_CHIPS, r, w), out_dtype),
                          compiler_params=_cparams(("parallel", "parallel")))(core, slabs, recv)


def _chip_exchange(arrays, name):
    na = len(arrays)

    def body(*refs):
        srcs, dsts, (send_sems, recv_sems, local_sems) = refs[:na], refs[na:2 * na], refs[2 * na:]
        x, y, c, _ = _my_place()
        chip = 2 * x + y
        own = [pltpu.make_async_copy(srcs[i].at[chip], dsts[i].at[chip], local_sems.at[i]) for i in range(na)]
        for cp in own:
            cp.start()
        sends, arrivals = [], []
        for i in range(na):
            for k in range(1, N_CHIPS):
                px, py = _chip_peer(x, y, k)
                s = i * N_CHIPS + k
                sends.append(_remote(srcs[i].at[chip ^ k], dsts[i].at[chip], send_sems.at[s], recv_sems.at[s], (px, py, c)))
                arrivals.append(_remote(srcs[i].at[chip], dsts[i].at[chip ^ k], send_sems.at[s], recv_sems.at[s], (px, py, c)))
        for cp in sends:
            cp.start()
        for cp in arrivals:
            cp.wait_recv()
        for cp in sends:
            cp.wait_send()
        for cp in own:
            cp.wait()

    hbm = pl.BlockSpec(memory_space=pl.ANY)
    return pl.pallas_call(
        body, name=name, in_specs=[hbm] * na, out_specs=tuple([hbm] * na),
        out_shape=tuple(jax.ShapeDtypeStruct(a.shape, a.dtype) for a in arrays),
        scratch_shapes=[pltpu.SemaphoreType.DMA((na * N_CHIPS,)), pltpu.SemaphoreType.DMA((na * N_CHIPS,)),
                        pltpu.SemaphoreType.DMA((na,))],
    )(*arrays)


def _reduce_adamw(parts, w, m, v, name):
    n_parts, r, wd = parts.shape
    tc = ELEMENTWISE_COLS
    c1 = 1.0 / (1.0 - ADAM_B1 ** ADAM_STEP)
    c2 = 1.0 / (1.0 - ADAM_B2 ** ADAM_STEP)

    def body(p_ref, w_ref, m_ref, v_ref, g_out, d_out, m_out, v_out):
        g = p_ref[0].astype(F32)
        for s in range(1, n_parts):
            g = g + p_ref[s].astype(F32)
        mn = ADAM_B1 * m_ref[...] + (1.0 - ADAM_B1) * g
        vn = ADAM_B2 * v_ref[...] + (1.0 - ADAM_B2) * (g * g)
        g_out[...] = g
        m_out[...] = mn
        v_out[...] = vn
        d_out[...] = -ADAM_LR * ((mn * c1) / (jnp.sqrt(vn * c2) + ADAM_EPS) + ADAM_WD * w_ref[...])

    blk = pl.BlockSpec((r, tc), lambda j: (0, j))
    shp = jax.ShapeDtypeStruct((r, wd), F32)
    return pl.pallas_call(
        body, name=name, grid=(wd // tc,),
        in_specs=[pl.BlockSpec((n_parts, r, tc), lambda j: (0, 0, j)), blk, blk, blk],
        out_specs=(blk, blk, blk, blk), out_shape=(shp, shp, shp, shp),
        compiler_params=_cparams(("parallel",)),
    )(parts, w, m, v)


PACK_W = 1024


def _pad_heads(a, slots):
    lead = a.shape[:-1]
    a = a.reshape(lead + (slots, RWKV_HEAD_DIM))
    a = jnp.pad(a, [(0, 0)] * (len(lead) + 1) + [(0, LANES - RWKV_HEAD_DIM)])
    return a.reshape(lead + (slots * LANES,))


def _unpad_heads(a, slots):
    lead = a.shape[:-1]
    return a.reshape(lead + (slots, LANES))[..., :RWKV_HEAD_DIM].reshape(lead + (slots * RWKV_HEAD_DIM,))


def _flat_pack(arrs, dtype, row_mult):
    flat = jnp.concatenate([a.reshape(-1).astype(dtype) for a in arrs])
    n = flat.shape[0]
    rows = -(-n // PACK_W)
    rows = -(-rows // row_mult) * row_mult
    return jnp.pad(flat, (0, rows * PACK_W - n)).reshape(rows, PACK_W)


def _row_pack(arrs, dtype, row_mult):
    parts = [a.astype(dtype) if a.shape[1] == PACK_W else a.astype(dtype).reshape(-1, PACK_W) for a in arrs]
    rows = sum(p.shape[0] for p in parts)
    pad = -(-rows // row_mult) * row_mult - rows
    return jnp.concatenate(parts + ([jnp.zeros((pad, PACK_W), dtype)] if pad else []), axis=0)


def _unpack_row_gathered(g, names, shard_shapes):
    out, r0 = {}, 0
    for n in names:
        s = shard_shapes[n]
        rows = s[0] * s[1] // PACK_W
        seg = g[:, r0:r0 + rows, :]
        r0 += rows
        if s[1] == PACK_W:
            assert SHARD_AXIS[n] == 0
            out[n] = seg.reshape(N_DEV * s[0], s[1])
        else:
            assert SHARD_AXIS[n] == 1
            out[n] = jnp.transpose(seg.reshape((N_DEV,) + tuple(s)), (1, 0, 2)).reshape(s[0], N_DEV * s[1])
    return out


def _unpack_gathered(g, names, shard_shapes):
    flat = g.reshape(N_DEV, -1)
    out, off = {}, 0
    for n in names:
        s = shard_shapes[n]
        size = s[0] * s[1]
        seg = flat[:, off:off + size].reshape((N_DEV,) + tuple(s))
        off += size
        if SHARD_AXIS[n] == 1:
            out[n] = jnp.transpose(seg, (1, 0, 2)).reshape(s[0], N_DEV * s[1])
        else:
            out[n] = seg.reshape(N_DEV * s[0], s[1])
    return out


def _shard_major(full, axis):
    a, b = full.shape
    if axis == 1:
        return jnp.transpose(full.reshape(a, N_DEV, b // N_DEV), (1, 0, 2)).reshape(N_DEV, -1)
    return full.reshape(N_DEV, -1)


def _prepare_weights(full, rep):
    w = full['w_in']
    d = w.shape[1]
    rkv = jnp.pad(w[0:1536].reshape(3 * RWKV_HEADS, RWKV_HEAD_DIM, d), ((0, 0), (0, LANES - RWKV_HEAD_DIM), (0, 0)))
    w_cat = jnp.concatenate([
        w[3848:4872], w[4872:5896], rkv.reshape(3 * RWKV_HEADS * LANES, d), w[1792:3328], w[3328:3840],
        w[1536:1792], jnp.pad(w[3840:3848], ((0, LANES - 8), (0, 0))), jnp.zeros((LANES, d), w.dtype)], axis=0)
    assert w_cat.shape[0] == CAT_W
    mu = rep['rwkv_mu']
    vecs = [mu[0:512], mu[512:1024], mu[1024:1536], rep['rwkv_w0'], rep['rwkv_a0'], rep['rwkv_k_k'], rep['rwkv_k_a'],
            rep['rwkv_ln_w'], rep['rwkv_ln_b'], rep['rwkv_r_k'].reshape(-1)]
    ppack = jnp.stack([jnp.pad(v.reshape(RWKV_HEADS, RWKV_HEAD_DIM), ((0, 0), (0, LANES - RWKV_HEAD_DIM))) for v in vecs], axis=1)
    ppack = jnp.pad(ppack, ((0, 0), (0, 16 - len(vecs)), (0, 0)))
    mulo = mu[1536:1792].reshape(1, 2 * LANES)
    wl = jnp.zeros((3, 2 * LANES, RWKV_HEADS * LANES), F32)
    wl = wl.at[0, 0:64].set(_pad_heads(full['rwkv_w2'], RWKV_HEADS))
    wl = wl.at[1, 64:128].set(_pad_heads(full['rwkv_a2'], RWKV_HEADS))
    wl = wl.at[2, 128:256].set(_pad_heads(full['rwkv_g2'], RWKV_HEADS))
    wl = jnp.transpose(wl.reshape(3, 2 * LANES, RWKV_HEADS, LANES), (2, 0, 1, 3))
    cw = full['gdn_conv_w'].reshape(GDN_CONV, 3, GDN_HEADS, LANES)
    cwpack = jnp.pad(jnp.transpose(cw, (2, 1, 0, 3)), ((0, 0), (0, 0), (0, SUBLANES - GDN_CONV), (0, 0)))
    gpar = jnp.zeros((SUBLANES, LANES), F32)
    gpar = gpar.at[0, 0:GDN_HEADS].set(rep['gdn_a_log']).at[1, 0:GDN_HEADS].set(rep['gdn_dt_bias']).at[2].set(rep['gdn_norm_w'])
    return dict(w_cat=w_cat, ffn_cw=full['ffn_conv_w'], ppack=ppack, mulo=mulo, wl=wl, cwpack=cwpack, gpar=gpar,
                g1=rep['norm1_g'].reshape(1, -1), g2=rep['norm2_g'].reshape(1, -1), gf=rep['final_g'].reshape(1, -1))


def _prepare_late_weights(full):
    rp = full['rwkv_proj']
    rproj = jnp.pad(rp.reshape(RWKV_HEADS, RWKV_HEAD_DIM, -1), ((0, 0), (0, LANES - RWKV_HEAD_DIM), (0, 0))).reshape(RWKV_HEADS * LANES, -1)
    return dict(rproj=rproj, gproj=full['gdn_proj'], w_out=full['w_out'], ffn_up=full['ffn_up'], ffn_down=full['ffn_down'])


def _local_step(x, target, p, late_weights):
    d = x.shape[1]
    full_w = lambda a: (a, a.shape[1], 0)
    (u,) = _rw_forward(_norm_fn, [full_w(x)], [p['g1']], [(d, BF16)], "norm1")
    p_cat = _matmul(u, p['w_cat'], 'nt', F32, "proj_in")
    ya_pre, st_r = _rwkv_forward(p_cat, p['ppack'], p['mulo'], p['wl'])
    yb_pre, st_g = _gdn_forward(p_cat, p['cwpack'], p['gpar'])
    p = {**p, **late_weights((ya_pre, yb_pre))}
    ya = _matmul(ya_pre, p['rproj'], 'nn', F32, "rwkv_proj")
    yb = _matmul(yb_pre, p['gproj'], 'nn', F32, "gdn_proj")
    gates = [(p_cat, d, OFF_GA // d), (p_cat, d, OFF_GB // d)]
    (mixed,) = _rw_forward(_merge_fn, gates + [full_w(ya), full_w(yb)], [], [(d, BF16)], "merge")
    mo = _matmul(mixed, p['w_out'], 'nn', F32, "out_proj")
    x1, n2 = _rw_forward(_res_norm_fn, [full_w(x), full_w(mo)], [p['g2']], [(d, F32), (d, BF16)], "res_norm2")
    hpre = _matmul(n2, p['ffn_up'], 'nt', F32, "ffn_up")
    act = _ffn_act_forward(hpre, p['ffn_cw'])
    fo = _matmul(act, p['ffn_down'], 'nn', F32, "ffn_down")
    loss_vec, dx2, dx2b, dgf = _loss_head(x1, fo, p['gf'], target, "loss_head")

    dact = _matmul(dx2b, p['ffn_down'], 'nt', F32, "d_act")
    dw_down = _matmul(act, dx2b, 'tn', F32, "dw_ffn_down")
    dh_gate, dh_up, dcw_gate, dcw_up = _ffn_backward(hpre, p['ffn_cw'], dact)
    dh = jnp.concatenate([dh_gate, dh_up], axis=1)
    dcw_f = jnp.concatenate([dcw_gate, dcw_up], axis=1)
    dn2 = _matmul(dh, p['ffn_up'], 'nn', F32, "d_norm2")
    dw_up = _matmul(dh, n2, 'tn', F32, "dw_ffn_up")
    dx1, dx1b, dg2 = _rw_backward(_res_norm_fn, [full_w(x), full_w(mo)], [p['g2']], [full_w(dx2), full_w(dn2)],
                                  [F32, BF16], "res_norm2_bwd")
    dmixed = _matmul(dx1b, p['w_out'], 'nt', F32, "d_mixed")
    dw_out = _matmul(mixed, dx1b, 'tn', F32, "dw_out")
    dga, dgb, dya, dyb = _rw_backward(_merge_fn, gates + [full_w(ya), full_w(yb)], [], [full_w(dmixed)],
                                      [BF16, BF16, BF16, BF16], "merge_bwd")
    d_ya_pre = _matmul(dya, p['rproj'], 'nt', F32, "d_rwkv_out")
    dw_rproj = _matmul(ya_pre, dya, 'tn', F32, "dw_rwkv_proj")
    d_yb_pre = _matmul(dyb, p['gproj'], 'nt', F32, "d_gdn_out")
    dw_gproj = _matmul(yb_pre, dyb, 'tn', F32, "dw_gdn_proj")
    dpr, dpk, dpv, dplo, dpp, dml, dwl = _rwkv_backward(p_cat, p['ppack'], p['mulo'], p['wl'], st_r, d_ya_pre)
    dq, dk, dv, dz, dab, dcw_g, dgp = _gdn_backward(p_cat, p['cwpack'], p['gpar'], st_g, d_yb_pre)
    t = x.shape[0]
    dp_cat = jnp.concatenate([dga, dgb, dpr, dpk, dpv, dq, dk, dv, dz, dplo.astype(BF16), dab.astype(BF16),
                              jnp.zeros((t, LANES), BF16)], axis=1)
    du = _matmul(dp_cat, p['w_cat'], 'nn', F32, "d_norm1")
    dw_cat = _matmul(dp_cat, u, 'tn', F32, "dw_in")
    grad_x, dg1 = _rw_backward(_norm_skip_fn, [full_w(x)], [p['g1']], [full_w(du), full_w(dx1)], [F32], "norm1_bwd")

    heads = lambda row: dpp[:, row, :RWKV_HEAD_DIM].reshape(-1)
    lora = lambda j, lo_, hi_: jnp.transpose(dwl[:, j, lo_:hi_, :RWKV_HEAD_DIM], (1, 0, 2)).reshape(hi_ - lo_, RWKV_WIDTH)
    grads = {
        'norm1_g': dg1[0],
        'w_in': jnp.concatenate([dw_cat[OFF_RKV:OFF_QKV].reshape(3 * RWKV_HEADS, LANES, d)[:, :RWKV_HEAD_DIM].reshape(-1, d),
                                 dw_cat[OFF_LO:OFF_AB], dw_cat[OFF_QKV:OFF_Z], dw_cat[OFF_Z:OFF_LO], dw_cat[OFF_AB:OFF_AB + 8],
                                 dw_cat[OFF_GA:OFF_GB], dw_cat[OFF_GB:OFF_RKV]], axis=0),
        'rwkv_mu': jnp.concatenate([heads(0), heads(1), heads(2), jnp.sum(dml[:, 0, :], axis=0)]),
        'rwkv_w0': heads(3), 'rwkv_a0': heads(4), 'rwkv_k_k': heads(5), 'rwkv_k_a': heads(6),
        'rwkv_ln_w': heads(7), 'rwkv_ln_b': heads(8), 'rwkv_r_k': heads(9).reshape(RWKV_HEADS, RWKV_HEAD_DIM),
        'rwkv_w2': lora(0, 0, 64), 'rwkv_a2': lora(1, 64, 128), 'rwkv_g2': lora(2, 128, 256),
        'rwkv_proj': dw_rproj.reshape(RWKV_HEADS, LANES, -1)[:, :RWKV_HEAD_DIM].reshape(RWKV_WIDTH, -1),
        'gdn_conv_w': jnp.transpose(dcw_g[:, :, :GDN_CONV, :], (2, 1, 0, 3)).reshape(GDN_CONV, 3 * GDN_WIDTH),
        'gdn_a_log': jnp.sum(dgp[:, 0, :GDN_HEADS], axis=0), 'gdn_dt_bias': jnp.sum(dgp[:, 1, :GDN_HEADS], axis=0),
        'gdn_norm_w': jnp.sum(dgp[:, 2, :], axis=0),
        'gdn_proj': dw_gproj, 'w_out': dw_out, 'norm2_g': dg2[0], 'ffn_up': dw_up, 'ffn_conv_w': dcw_f,
        'ffn_down': dw_down, 'final_g': dgf[0],
    }
    return loss_vec, grad_x, grads


def kernel(x, norm1_g, w_in, rwkv_mu, rwkv_w0, rwkv_w2, rwkv_a0, rwkv_a2, rwkv_g2, rwkv_k_k, rwkv_k_a, rwkv_r_k, rwkv_ln_w, rwkv_ln_b, rwkv_proj, gdn_conv_w, gdn_a_log, gdn_dt_bias, gdn_norm_w, gdn_proj, w_out, norm2_g, ffn_up, ffn_conv_w, ffn_down, final_g, loss_target, m_norm1_g, m_w_in, m_rwkv_mu, m_rwkv_w0, m_rwkv_w2, m_rwkv_a0, m_rwkv_a2, m_rwkv_g2, m_rwkv_k_k, m_rwkv_k_a, m_rwkv_r_k, m_rwkv_ln_w, m_rwkv_ln_b, m_rwkv_proj, m_gdn_conv_w, m_gdn_a_log, m_gdn_dt_bias, m_gdn_norm_w, m_gdn_proj, m_w_out, m_norm2_g, m_ffn_up, m_ffn_conv_w, m_ffn_down, m_final_g, v_norm1_g, v_w_in, v_rwkv_mu, v_rwkv_w0, v_rwkv_w2, v_rwkv_a0, v_rwkv_a2, v_rwkv_g2, v_rwkv_k_k, v_rwkv_k_a, v_rwkv_r_k, v_rwkv_ln_w, v_rwkv_ln_b, v_rwkv_proj, v_gdn_conv_w, v_gdn_a_log, v_gdn_dt_bias, v_gdn_norm_w, v_gdn_proj, v_w_out, v_norm2_g, v_ffn_up, v_ffn_conv_w, v_ffn_down, v_final_g):
    given = dict(zip(WEIGHT_NAMES, (norm1_g, w_in, rwkv_mu, rwkv_w0, rwkv_w2, rwkv_a0, rwkv_a2, rwkv_g2, rwkv_k_k, rwkv_k_a, rwkv_r_k,
                                    rwkv_ln_w, rwkv_ln_b, rwkv_proj, gdn_conv_w, gdn_a_log, gdn_dt_bias, gdn_norm_w, gdn_proj, w_out,
                                    norm2_g, ffn_up, ffn_conv_w, ffn_down, final_g)))
    mom1 = dict(zip(WEIGHT_NAMES, (m_norm1_g, m_w_in, m_rwkv_mu, m_rwkv_w0, m_rwkv_w2, m_rwkv_a0, m_rwkv_a2, m_rwkv_g2, m_rwkv_k_k,
                                   m_rwkv_k_a, m_rwkv_r_k, m_rwkv_ln_w, m_rwkv_ln_b, m_rwkv_proj, m_gdn_conv_w, m_gdn_a_log,
                                   m_gdn_dt_bias, m_gdn_norm_w, m_gdn_proj, m_w_out, m_norm2_g, m_ffn_up, m_ffn_conv_w, m_ffn_down,
                                   m_final_g)))
    mom2 = dict(zip(WEIGHT_NAMES, (v_norm1_g, v_w_in, v_rwkv_mu, v_rwkv_w0, v_rwkv_w2, v_rwkv_a0, v_rwkv_a2, v_rwkv_g2, v_rwkv_k_k,
                                   v_rwkv_k_a, v_rwkv_r_k, v_rwkv_ln_w, v_rwkv_ln_b, v_rwkv_proj, v_gdn_conv_w, v_gdn_a_log,
                                   v_gdn_dt_bias, v_gdn_norm_w, v_gdn_proj, v_w_out, v_norm2_g, v_ffn_up, v_ffn_conv_w, v_ffn_down,
                                   v_final_g)))
    def strip(n, a):
        a = a if n == 'final_g' else a.reshape(a.shape[1:])
        return a.T if n in TRANSPOSED else a

    local = {n: strip(n, a) for n, a in given.items()}
    shard_shapes = {n: local[n].shape for n in SHARD_AXIS}
    sharded = BIG_SHARDED + SMALL_SHARDED

    late_names = [n for n in BIG_SHARDED if n != 'w_in']
    late_pack = _row_pack([local[n] for n in late_names], BF16, 16)
    send_sems, recv_sems, late_pack, landing, token = _gather_start(late_pack, "gather_late_start")
    g_in = _all_gather_two_level(_row_pack([local['w_in']], BF16, 16), "gather_w_in")
    g_small = _all_gather_two_level(_flat_pack([local[n] for n in SMALL_SHARDED], F32, SUBLANES), "gather_small")
    full = _unpack_row_gathered(g_in, ['w_in'], shard_shapes)
    full.update(_unpack_gathered(g_small, SMALL_SHARDED, shard_shapes))
    rep = {n: local[n] for n in REPLICATED}
    rep['norm1_g'] = rep['norm1_g'] + token[0, 0]

    def late_weights(after):
        got = _gather_wait(send_sems, recv_sems, late_pack, landing, after, "gather_late_wait")
        me = 4 * lax.axis_index("x") + 2 * lax.axis_index("y") + lax.axis_index("c")
        slot = lax.broadcasted_iota(jnp.int32, (N_DEV, 1, 1), 0)
        got = jnp.where(slot == me, late_pack[None], got)
        return _prepare_late_weights(_unpack_row_gathered(got, late_names, shard_shapes))

    loss_vec, grad_x, grads = _local_step(x[0], loss_target[0], _prepare_weights(full, rep), late_weights)

    small_sharded = ['rwkv_proj', 'gdn_proj'] + SMALL_SHARDED
    small_names = small_sharded + REPLICATED
    rep_vec = jnp.concatenate([grads[n].reshape(-1) for n in REPLICATED] + [loss_vec[0, 0:1]])
    slabs = [grads[n].reshape(N_DEV, -1, grads[n].shape[1]) for n in ROW_SHARDED]
    slab_small = jnp.concatenate([_shard_major(grads[n], SHARD_AXIS[n]) for n in small_sharded] +
                                 [jnp.broadcast_to(rep_vec[None], (N_DEV, rep_vec.shape[0]))], axis=1)
    small_rows = -(-slab_small.shape[1] // (PACK_W * SUBLANES)) * SUBLANES
    slab_small = jnp.pad(slab_small, ((0, 0), (0, small_rows * PACK_W - slab_small.shape[1]))).reshape(N_DEV, small_rows, PACK_W)
    core = lax.axis_index("c").astype(jnp.int32).reshape(1)
    from_sibling = _pair_exchange(slabs + [slab_small], "grad_pair_exchange")
    chip_parts = [_pair_sum(s, r, core, BF16, "grad_pair_sum_" + n) for s, r, n in zip(slabs, from_sibling, ROW_SHARDED)]
    chip_parts.append(_pair_sum(slab_small, from_sibling[-1], core, F32, "grad_pair_sum_small"))
    parts = _chip_exchange(chip_parts, "grad_chip_exchange")

    def pack_local(src):
        flat = jnp.concatenate([strip(n, src[n]).reshape(-1) for n in small_names])
        return jnp.pad(flat, (0, small_rows * PACK_W - flat.shape[0])).reshape(small_rows, PACK_W)

    results = [({}, None) for _ in range(4)]
    for part, n in zip(parts, ROW_SHARDED):
        packs = _reduce_adamw(part, local[n], strip(n, mom1[n]), strip(n, mom2[n]), "adamw_" + n)
        for (out, _), pk in zip(results, packs):
            out[n] = (pk.T if n in TRANSPOSED else pk).reshape(given[n].shape)
    packs = _reduce_adamw(parts[-1], pack_local(given), pack_local(mom1), pack_local(mom2), "adamw_small")
    for i, pk in enumerate(packs):
        flat, off = pk.reshape(-1), 0
        for n in small_names:
            size = int(np.prod(given[n].shape))
            results[i][0][n] = flat[off:off + size].reshape(given[n].shape)
            off += size
        results[i] = (results[i][0], flat[off])
    (g_out, loss), (d_out, _), (m_out, _), (v_out, _) = results
    return (loss, grad_x[None], *[g_out[n] for n in WEIGHT_NAMES], *[d_out[n] for n in WEIGHT_NAMES],
            *[m_out[n] for n in WEIGHT_NAMES], *[v_out[n] for n in WEIGHT_NAMES])
```

```python
import functools

import jax
import jax.numpy as jnp
import numpy as np
from jax import lax
from jax.experimental import pallas as pl
from jax.experimental.pallas import tpu as pltpu

F32 = jnp.float32
BF16 = jnp.bfloat16
HI = lax.Precision.HIGHEST

N_DEV = 8
D_MODEL = 1024
CHUNK = 64
RWKV_HEADS = 8
RWKV_HEAD_DIM = 64
RWKV_WIDTH = 512
GDN_HEADS = 4
GDN_HEAD_DIM = 128
GDN_WIDTH = 512
GDN_CONV = 4
FFN_HIDDEN = 2816
FFN_CONV = 3
NORM_EPS = 1e-6
L2_EPS = 1e-6
RWKV_GN_EPS = 64e-5
LANES = 128
SUBLANES = 8
VMEM_LIMIT = 56 * 1024 * 1024

ADAM_LR = 0.001
ADAM_B1 = 0.9
ADAM_B2 = 0.999
ADAM_EPS = 1e-08
ADAM_WD = 0.01
ADAM_STEP = 10

OFF_GA, OFF_GB, OFF_RKV, OFF_QKV, OFF_Z, OFF_LO, OFF_AB, CAT_W = 0, 1024, 2048, 5120, 6656, 7168, 7424, 7680
RWKV_HB = 8
GDN_HB = 4

WEIGHT_NAMES = ['norm1_g', 'w_in', 'rwkv_mu', 'rwkv_w0', 'rwkv_w2', 'rwkv_a0', 'rwkv_a2', 'rwkv_g2', 'rwkv_k_k', 'rwkv_k_a',
                'rwkv_r_k', 'rwkv_ln_w', 'rwkv_ln_b', 'rwkv_proj', 'gdn_conv_w', 'gdn_a_log', 'gdn_dt_bias', 'gdn_norm_w',
                'gdn_proj', 'w_out', 'norm2_g', 'ffn_up', 'ffn_conv_w', 'ffn_down', 'final_g']
BIG_SHARDED = ['w_in', 'ffn_up', 'ffn_down', 'w_out', 'rwkv_proj', 'gdn_proj']
SMALL_SHARDED = ['rwkv_w2', 'rwkv_a2', 'rwkv_g2', 'gdn_conv_w', 'ffn_conv_w']
TRANSPOSED = ('w_in', 'ffn_up')
SHARD_AXIS = {'w_in': 0, 'ffn_up': 0, 'ffn_down': 0, 'w_out': 0, 'rwkv_proj': 1, 'gdn_proj': 1,
              'rwkv_w2': 1, 'rwkv_a2': 1, 'rwkv_g2': 1, 'gdn_conv_w': 1, 'ffn_conv_w': 1}
REPLICATED = [n for n in WEIGHT_NAMES if n not in SHARD_AXIS]
ROW_SHARDED = ['w_in', 'ffn_up', 'ffn_down', 'w_out']


def _cparams(sem=None):
    kw = dict(vmem_limit_bytes=VMEM_LIMIT)
    if sem is not None:
        kw['dimension_semantics'] = sem
    return pltpu.CompilerParams(**kw)


_NN, _NT, _TN = 'nn', 'nt', 'tn'
_DIMS_2D = {'nn': (((1,), (0,)), ((), ())), 'nt': (((1,), (1,)), ((), ())), 'tn': (((0,), (0,)), ((), ()))}
_DIMS_3D = {'nn': (((2,), (1,)), ((0,), (0,))), 'nt': (((2,), (2,)), ((0,), (0,))), 'tn': (((1,), (1,)), ((0,), (0,)))}


def _dg(a, b, kind):
    return lax.dot_general(a, b, (_DIMS_2D if a.ndim == 2 else _DIMS_3D)[kind], preferred_element_type=F32)


def _dot1(a, b, kind):
    return _dg(a.astype(BF16), b.astype(BF16), kind)


@jax.custom_vjp
def _dhi(a, b):
    return _dot1(a, b, _NN)


_dhi.defvjp(lambda a, b: (_dot1(a, b, _NN), (a, b)),
            lambda res, ct: (_dot1(ct, res[1], _NT), _dot1(res[0], ct, _TN)))


@jax.custom_vjp
def _dnt(a, b):
    return _dot1(a, b, _NT)


_dnt.defvjp(lambda a, b: (_dot1(a, b, _NT), (a, b)),
            lambda res, ct: (_dot1(ct, res[1], _NN), _dot1(ct, res[0], _TN)))


@jax.custom_vjp
def _dtn(a, b):
    return _dot1(a, b, _TN)


_dtn.defvjp(lambda a, b: (_dot1(a, b, _TN), (a, b)),
            lambda res, ct: (_dot1(res[1], ct, _NT), _dot1(res[0], ct, _NN)))


def _split3(x):
    x1 = x.astype(BF16)
    r1 = x - x1.astype(F32)
    x2 = r1.astype(BF16)
    return x1, x2, (r1 - x2.astype(F32)).astype(BF16)


def _dot_exact_lhs(sel, x, kind):
    parts = [_dg(sel, xi, kind) for xi in _split3(x)]
    return parts[0] + parts[1] + parts[2]


def _tril_ones(like):
    c = like.shape[-2]
    ri, ci = _iotas(c)
    return jnp.broadcast_to((ri >= ci).astype(BF16), like.shape[:-2] + (c, c))


@jax.custom_vjp
def _cumsum_rows(x):
    return _dot_exact_lhs(_tril_ones(x), x, _NN)


_cumsum_rows.defvjp(lambda x: (_dot_exact_lhs(_tril_ones(x), x, _NN), None),
                    lambda _, ct: (_dot_exact_lhs(_tril_ones(ct), ct, _TN),))


@jax.custom_vjp
def _lane_sum_as_row(x):
    return _dot_exact_lhs(jnp.ones(x.shape, BF16), x, _NT)


def _lane_sum_as_row_bwd(_, ct):
    ones = jnp.ones(ct.shape[:-1] + (LANES,), BF16)
    parts = [_dg(ci, ones, _TN) for ci in _split3(ct)]
    return (parts[0] + parts[1] + parts[2],)


_lane_sum_as_row.defvjp(lambda x: (_dot_exact_lhs(jnp.ones(x.shape, BF16), x, _NT), None), _lane_sum_as_row_bwd)


def _shift_rows(x, halo, s):
    rows = lax.broadcasted_iota(jnp.int32, x.shape, 0)
    out = pltpu.roll(x, s, 0)
    for i in range(s):
        out = jnp.where(rows == i, halo[SUBLANES - s + i:SUBLANES - s + i + 1, :], out)
    return out


def _unshift_rows(g, carry, s):
    c = g.shape[0]
    rows = lax.broadcasted_iota(jnp.int32, g.shape, 0)
    out = pltpu.roll(g, c - s, 0)
    for i in range(s):
        out = jnp.where(rows == c - s + i, carry[i:i + 1, :], out)
    return out


def _sigmoid(z):
    return 1.0 / (1.0 + jnp.exp(-z))


def _silu(z):
    return z * _sigmoid(z)


def _softplus(z):
    return jnp.maximum(z, 0.0) + jnp.log(1.0 + jnp.exp(-jnp.abs(z)))


def _rms(t, gain):
    return t * lax.rsqrt(jnp.mean(t * t, axis=-1, keepdims=True) + NORM_EPS) * gain


def _iotas(c):
    return lax.broadcasted_iota(jnp.int32, (c, c), 0), lax.broadcasted_iota(jnp.int32, (c, c), 1)


def _unit_lower_inverse(xm, eye):
    t = eye + xm
    p = xm
    for _ in range(5):
        p = _dhi(p, p)
        t = t + _dhi(t, p)
    return t


def _rwkv_head(pr, pk, pv, plo, qr, qk, qv, qlo, s0, pp, mulo, wl):
    c = pr.shape[1]
    ri, ci = _iotas(c)

    def mix(p, q, mu):
        return p + (q - p) * mu

    r = mix(pr, qr, pp[:, 0:1])
    k = mix(pk, qk, pp[:, 1:2])
    v = mix(pv, qv, pp[:, 2:3])
    lo = mix(plo, qlo, mulo)
    w0, a0, k_k, k_a, ln_w, ln_b, r_k = (pp[:, i:i + 1] for i in range(3, 10))
    per_head = lambda t: jnp.broadcast_to(t, (pr.shape[0],) + t.shape)
    zw = _dhi(per_head(jnp.tanh(lo)), wl[:, 0])
    za = _dhi(per_head(lo), wl[:, 1])
    g = _dhi(per_head(_sigmoid(lo)), wl[:, 2])
    w_log = -_softplus(-(w0 + zw)) - 0.5
    lw = -jnp.exp(w_log)
    a = _sigmoid(a0 + za)
    kk = k * k_k
    kk = kk * lax.rsqrt(jnp.sum(kk * kk, axis=-1, keepdims=True) + L2_EPS)
    k2 = k * (1.0 + (a - 1.0) * k_a)
    an = -kk
    b = kk * a
    causal = ri >= ci
    strict = ri > ci
    eye = (ri == ci).astype(F32)
    cl = _cumsum_rows(lw)
    ecl = jnp.exp(-cl)
    at = an * jnp.exp(cl - lw)
    bt = b * ecl
    kt = k2 * ecl
    rt = r * jnp.exp(cl)
    a_ab = jnp.where(strict, _dnt(at, bt), 0.0)
    a_ak = jnp.where(strict, _dnt(at, kt), 0.0)
    tinv = _unit_lower_inverse(a_ab, eye)
    u = _dhi(tinv, _dnt(at, s0) + _dhi(a_ak, v))
    y = _dnt(rt, s0) + _dhi(jnp.where(causal, _dnt(rt, bt), 0.0), u) + _dhi(jnp.where(causal, _dnt(rt, kt), 0.0), v)
    cl_end = jnp.sum(lw, axis=1, keepdims=True)
    dec_end = jnp.exp(cl_end - cl)
    s1 = s0 * jnp.exp(cl_end) + _dtn(u, b * dec_end) + _dtn(v, k2 * dec_end)
    m = (lax.broadcasted_iota(jnp.int32, (1, LANES), 1) < RWKV_HEAD_DIM).astype(F32)
    mean = jnp.sum(y, axis=-1, keepdims=True) * (1.0 / RWKV_HEAD_DIM)
    yc = (y - mean) * m
    var = jnp.sum(yc * yc, axis=-1, keepdims=True) * (1.0 / RWKV_HEAD_DIM)
    yn = yc * lax.rsqrt(var + RWKV_GN_EPS) * ln_w + ln_b
    y2 = yn + jnp.sum(r * k2 * r_k, axis=-1, keepdims=True) * v
    return y2 * g, s1


def _gdn_head(xq, xk, xv, z, ab, s0, cw, gp, oha, ohb):
    c = z.shape[1]
    ri, ci = _iotas(c)

    def conv(xs, w):
        out = xs[0] * w[:, GDN_CONV - 1:GDN_CONV]
        for s in range(1, GDN_CONV):
            out = out + xs[s] * w[:, GDN_CONV - 1 - s:GDN_CONV - s]
        return out

    q = _silu(conv(xq, cw[:, 0]))
    k = _silu(conv(xk, cw[:, 1]))
    v = _silu(conv(xv, cw[:, 2]))
    q = q * lax.rsqrt(jnp.sum(q * q, axis=-1, keepdims=True) + L2_EPS) * (GDN_HEAD_DIM ** -0.5)
    k = k * lax.rsqrt(jnp.sum(k * k, axis=-1, keepdims=True) + L2_EPS)
    gg = -jnp.exp(gp[0:1]) * _softplus(ab + gp[1:2])
    beta = jnp.sum(_sigmoid(ab) * ohb, axis=-1, keepdims=True)
    causal = ri >= ci
    strict = ri > ci
    eye = (ri == ci).astype(F32)
    gcm = _cumsum_rows(gg * oha)
    gc = jnp.sum(gcm, axis=-1, keepdims=True)
    gc_row = _lane_sum_as_row(gcm)
    dec = jnp.where(causal, jnp.exp(jnp.where(causal, gc - gc_row, 0.0)), 0.0)
    kb = k * beta
    vb = v * beta
    lm = jnp.where(strict, _dnt(kb, k) * dec, 0.0)
    tinv = _unit_lower_inverse(-lm, eye)
    egc = jnp.exp(gc)
    u = _dhi(tinv, vb)
    wk = _dhi(tinv, kb * egc)
    attn = jnp.where(causal, _dnt(q, k) * dec, 0.0)
    g_last = gc[:, c - 1:c, :]
    v_new = u - _dhi(wk, s0)
    o = _dhi(q * egc, s0) + _dhi(attn, v_new)
    s1 = s0 * jnp.exp(g_last) + _dtn(k * jnp.exp(g_last - gc), v_new)
    return _rms(o, gp[2:3]) * _silu(z), s1


def _head_id(grp, i, per_step, heads):
    return i if per_step == heads else grp * per_step + i


def _head_range(grp, per_step, heads):
    return slice(None) if per_step == heads else pl.ds(grp * per_step, per_step)


def _stack_heads(ref, n, fn=None):
    parts = []
    for i in range(n):
        sl = slice(i * LANES, (i + 1) * LANES)
        v = ref[:, sl]
        parts.append(v if fn is None else fn(v, sl))
    return jnp.stack(parts)


def _prev_rows_spec(width, col):
    per = CHUNK // SUBLANES
    return pl.BlockSpec((SUBLANES, width), lambda n, h: (jnp.maximum(n * per - 1, 0), col(h)))


def _rwkv_specs(nmap):
    hb, groups = RWKV_HB, RWKV_HEADS // RWKV_HB
    cb = OFF_RKV // (hb * LANES)
    specs = []
    for j in range(3):
        specs.append(pl.BlockSpec((CHUNK, hb * LANES), lambda n, g, j=j: (nmap(n), cb + j * groups + g)))
    specs.append(pl.BlockSpec((CHUNK, 2 * LANES), lambda n, g: (nmap(n), OFF_LO // (2 * LANES))))
    per = CHUNK // SUBLANES
    for j in range(3):
        specs.append(pl.BlockSpec((SUBLANES, hb * LANES),
                                  lambda n, g, j=j: (jnp.maximum(nmap(n) * per - 1, 0), cb + j * groups + g)))
    specs.append(pl.BlockSpec((SUBLANES, 2 * LANES), lambda n, g: (jnp.maximum(nmap(n) * per - 1, 0), OFF_LO // (2 * LANES))))
    specs.append(pl.BlockSpec((hb, 16, LANES), lambda n, g: (g, 0, 0)))
    specs.append(pl.BlockSpec((1, 2 * LANES), lambda n, g: (0, 0)))
    specs.append(pl.BlockSpec((hb, 3, 2 * LANES, LANES), lambda n, g: (g, 0, 0, 0)))
    return specs


def _rwkv_forward(p_cat, ppack, mulo, wl):
    t = p_cat.shape[0]
    n_chunks = t // CHUNK

    def body(pr, pk, pv, plo, hr, hk, hv, hlo, pp, ml, w, out, st_out, s_scr):
        n, grp = pl.program_id(0), pl.program_id(1)

        hsl = _head_range(grp, RWKV_HB, RWKV_HEADS)

        @pl.when(n == 0)
        def _():
            s_scr[hsl] = jnp.zeros((RWKV_HB, LANES, LANES), F32)

        live = (n > 0).astype(F32)
        lo = plo[...]
        lo_prev = _shift_rows(lo, hlo[...] * live, 1)
        cur = [_stack_heads(x, RWKV_HB) for x in (pr, pk, pv)]
        prev = [_stack_heads(x, RWKV_HB, lambda v, sl, hx=hx: _shift_rows(v, hx[:, sl] * live, 1))
                for x, hx in ((pr, hr), (pk, hk), (pv, hv))]
        s0 = s_scr[hsl]
        st_out[...] = s0
        o, s1 = _rwkv_head(*cur, lo, *prev, lo_prev, s0, pp[...], ml[...], w[...])
        for i in range(RWKV_HB):
            out[:, i * LANES:(i + 1) * LANES] = o[i].astype(out.dtype)
        s_scr[hsl] = s1

    return pl.pallas_call(
        body, name="rwkv_fwd", grid=(n_chunks, RWKV_HEADS // RWKV_HB),
        in_specs=_rwkv_specs(lambda n: n),
        out_specs=(pl.BlockSpec((CHUNK, RWKV_HB * LANES), lambda n, g: (n, g)),
                   pl.BlockSpec((None, RWKV_HB, LANES, LANES), lambda n, g: (n, g, 0, 0))),
        out_shape=(jax.ShapeDtypeStruct((t, RWKV_HEADS * LANES), BF16),
                   jax.ShapeDtypeStruct((n_chunks, RWKV_HEADS, LANES, LANES), F32)),
        scratch_shapes=[pltpu.VMEM((RWKV_HEADS, LANES, LANES), F32)],
        compiler_params=_cparams(("arbitrary", "arbitrary")),
    )(p_cat, p_cat, p_cat, p_cat, p_cat, p_cat, p_cat, p_cat, ppack, mulo, wl)


def _rwkv_backward(p_cat, ppack, mulo, wl, states, d_out):
    t = p_cat.shape[0]
    n_chunks = t // CHUNK
    last = n_chunks - 1

    def body(pr, pk, pv, plo, hr, hk, hv, hlo, pp, ml, w, st, dy, dpr, dpk, dpv, dplo, dpp, dml, dw, ds_scr, car_scr, carlo_scr):
        n, grp = pl.program_id(0), pl.program_id(1)

        hsl = _head_range(grp, RWKV_HB, RWKV_HEADS)
        gi = _head_id(grp, 0, 1, RWKV_HEADS // RWKV_HB)

        @pl.when(n == 0)
        def _():
            ds_scr[hsl] = jnp.zeros((RWKV_HB, LANES, LANES), F32)
            car_scr[hsl] = jnp.zeros((RWKV_HB, 3 * SUBLANES, LANES), F32)
            carlo_scr[gi] = jnp.zeros((SUBLANES, 2 * LANES), F32)

        @pl.when((n == 0) & (grp == 0))
        def _():
            dpp[...] = jnp.zeros(dpp.shape, F32)
            dml[...] = jnp.zeros(dml.shape, F32)
            dw[...] = jnp.zeros(dw.shape, F32)

        live = (n < last).astype(F32)
        lo = plo[...]
        lo_prev = _shift_rows(lo, hlo[...] * live, 1)
        cur = [_stack_heads(x, RWKV_HB) for x in (pr, pk, pv)]
        prev = [_stack_heads(x, RWKV_HB, lambda v, sl, hx=hx: _shift_rows(v, hx[:, sl] * live, 1))
                for x, hx in ((pr, hr), (pk, hk), (pv, hv))]
        _, vjp = jax.vjp(_rwkv_head, *cur, lo, *prev, lo_prev, st[...], pp[...], ml[...], w[...])
        g = vjp((_stack_heads(dy, RWKV_HB), ds_scr[hsl]))
        outs = (dpr, dpk, dpv)
        for i in range(RWKV_HB):
            sl = slice(i * LANES, (i + 1) * LANES)
            h = _head_id(grp, i, RWKV_HB, RWKV_HEADS)
            car = car_scr[h]
            for j in range(3):
                tot = g[j][i] + _unshift_rows(g[4 + j][i], car[SUBLANES * j:SUBLANES * (j + 1), :], 1)
                outs[j][:, sl] = tot.astype(outs[j].dtype)
                car_scr[h, SUBLANES * j:SUBLANES * (j + 1), :] = g[4 + j][i][0:SUBLANES, :]
        dlo = g[3] + _unshift_rows(g[7], carlo_scr[gi], 1)
        carlo_scr[gi] = g[7][0:SUBLANES, :]
        ds_scr[hsl] = g[8]
        dpp[hsl] += g[9]
        dml[0, 0:1, :] += g[10]
        dw[hsl] += g[11]

        @pl.when(grp == 0)
        def _():
            dplo[...] = dlo

        @pl.when(grp > 0)
        def _():
            dplo[...] += dlo

    rev = lambda n: last - n
    in_specs = _rwkv_specs(rev) + [
        pl.BlockSpec((None, RWKV_HB, LANES, LANES), lambda n, g: (rev(n), g, 0, 0)),
        pl.BlockSpec((CHUNK, RWKV_HB * LANES), lambda n, g: (rev(n), g)),
    ]
    hw = RWKV_HEADS * LANES
    return pl.pallas_call(
        body, name="rwkv_bwd", grid=(n_chunks, RWKV_HEADS // RWKV_HB),
        in_specs=in_specs,
        out_specs=(pl.BlockSpec((CHUNK, RWKV_HB * LANES), lambda n, g: (rev(n), g)),
                   pl.BlockSpec((CHUNK, RWKV_HB * LANES), lambda n, g: (rev(n), g)),
                   pl.BlockSpec((CHUNK, RWKV_HB * LANES), lambda n, g: (rev(n), g)),
                   pl.BlockSpec((CHUNK, 2 * LANES), lambda n, h: (rev(n), 0)),
                   pl.BlockSpec((RWKV_HEADS, 16, LANES), lambda n, h: (0, 0, 0)),
                   pl.BlockSpec((RWKV_HEADS, SUBLANES, 2 * LANES), lambda n, h: (0, 0, 0)),
                   pl.BlockSpec((RWKV_HEADS, 3, 2 * LANES, LANES), lambda n, h: (0, 0, 0, 0))),
        out_shape=(jax.ShapeDtypeStruct((t, hw), BF16), jax.ShapeDtypeStruct((t, hw), BF16), jax.ShapeDtypeStruct((t, hw), BF16),
                   jax.ShapeDtypeStruct((t, 2 * LANES), F32),
                   jax.ShapeDtypeStruct((RWKV_HEADS, 16, LANES), F32),
                   jax.ShapeDtypeStruct((RWKV_HEADS, SUBLANES, 2 * LANES), F32),
                   jax.ShapeDtypeStruct((RWKV_HEADS, 3, 2 * LANES, LANES), F32)),
        scratch_shapes=[pltpu.VMEM((RWKV_HEADS, LANES, LANES), F32),
                        pltpu.VMEM((RWKV_HEADS, 3 * SUBLANES, LANES), F32),
                        pltpu.VMEM((RWKV_HEADS, SUBLANES, 2 * LANES), F32)],
        compiler_params=_cparams(("arbitrary", "arbitrary")),
    )(p_cat, p_cat, p_cat, p_cat, p_cat, p_cat, p_cat, p_cat, ppack, mulo, wl, states, d_out)


def _gdn_specs(nmap):
    cb = OFF_QKV // LANES
    per = CHUNK // SUBLANES
    hb, groups = GDN_HB, GDN_HEADS // GDN_HB
    cb = OFF_QKV // (hb * LANES)
    specs = []
    for j in range(3):
        specs.append(pl.BlockSpec((CHUNK, hb * LANES), lambda n, g, j=j: (nmap(n), cb + j * groups + g)))
    for j in range(3):
        specs.append(pl.BlockSpec((SUBLANES, hb * LANES),
                                  lambda n, g, j=j: (jnp.maximum(nmap(n) * per - 1, 0), cb + j * groups + g)))
    specs.append(pl.BlockSpec((CHUNK, hb * LANES), lambda n, g: (nmap(n), OFF_Z // (hb * LANES) + g)))
    specs.append(pl.BlockSpec((CHUNK, LANES), lambda n, g: (nmap(n), OFF_AB // LANES)))
    specs.append(pl.BlockSpec((hb, 3, SUBLANES, LANES), lambda n, g: (g, 0, 0, 0)))
    specs.append(pl.BlockSpec((SUBLANES, LANES), lambda n, g: (0, 0)))
    return specs


def _conv_taps(x, halo):
    return (x,) + tuple(_shift_rows(x, halo, s) for s in range(1, GDN_CONV))


def _onehots(grp):
    lane = lax.broadcasted_iota(jnp.int32, (GDN_HB, 1, LANES), 2)
    head = lax.broadcasted_iota(jnp.int32, (GDN_HB, 1, LANES), 0) + _head_id(grp, 0, GDN_HB, GDN_HEADS)
    return (lane == head).astype(F32), (lane == GDN_HEADS + head).astype(F32)


def _gdn_taps(refs, halos, live):
    out = []
    for x, hx in zip(refs, halos):
        per_head = [_conv_taps(x[:, i * LANES:(i + 1) * LANES], hx[:, i * LANES:(i + 1) * LANES] * live) for i in range(GDN_HB)]
        out.append(tuple(jnp.stack([per_head[i][s] for i in range(GDN_HB)]) for s in range(GDN_CONV)))
    return out


def _gdn_forward(p_cat, cwpack, gpar):
    t = p_cat.shape[0]
    n_chunks = t // CHUNK

    def body(xq, xk, xv, hq, hk, hv, z, ab, cw, gp, out, st_out, s_scr):
        n, grp = pl.program_id(0), pl.program_id(1)

        hsl = _head_range(grp, GDN_HB, GDN_HEADS)

        @pl.when(n == 0)
        def _():
            s_scr[hsl] = jnp.zeros((GDN_HB, LANES, LANES), F32)

        live = (n > 0).astype(F32)
        oha, ohb = _onehots(grp)
        s0 = s_scr[hsl]
        st_out[...] = s0
        taps = _gdn_taps((xq, xk, xv), (hq, hk, hv), live)
        o, s1 = _gdn_head(*taps, _stack_heads(z, GDN_HB), ab[...], s0, cw[...], gp[...], oha, ohb)
        for i in range(GDN_HB):
            out[:, i * LANES:(i + 1) * LANES] = o[i].astype(out.dtype)
        s_scr[hsl] = s1

    return pl.pallas_call(
        body, name="gdn_fwd", grid=(n_chunks, GDN_HEADS // GDN_HB),
        in_specs=_gdn_specs(lambda n: n),
        out_specs=(pl.BlockSpec((CHUNK, GDN_HB * LANES), lambda n, g: (n, g)),
                   pl.BlockSpec((None, GDN_HB, LANES, LANES), lambda n, g: (n, g, 0, 0))),
        out_shape=(jax.ShapeDtypeStruct((t, GDN_WIDTH), BF16),
                   jax.ShapeDtypeStruct((n_chunks, GDN_HEADS, LANES, LANES), F32)),
        scratch_shapes=[pltpu.VMEM((GDN_HEADS, LANES, LANES), F32)],
        compiler_params=_cparams(("arbitrary", "arbitrary")),
    )(p_cat, p_cat, p_cat, p_cat, p_cat, p_cat, p_cat, p_cat, cwpack, gpar)


def _gdn_backward(p_cat, cwpack, gpar, states, d_out):
    t = p_cat.shape[0]
    n_chunks = t // CHUNK
    last = n_chunks - 1

    def body(xq, xk, xv, hq, hk, hv, z, ab, cw, gp, st, dy, dq, dk, dv, dz, dab, dcw, dgp, ds_scr, car_scr):
        n, grp = pl.program_id(0), pl.program_id(1)

        hsl = _head_range(grp, GDN_HB, GDN_HEADS)

        @pl.when(n == 0)
        def _():
            ds_scr[hsl] = jnp.zeros((GDN_HB, LANES, LANES), F32)
            car_scr[hsl] = jnp.zeros((GDN_HB, 3 * GDN_CONV, SUBLANES, LANES), F32)

        @pl.when((n == 0) & (grp == 0))
        def _():
            dcw[...] = jnp.zeros(dcw.shape, F32)
            dgp[...] = jnp.zeros(dgp.shape, F32)

        live = (n < last).astype(F32)
        oha, ohb = _onehots(grp)
        fn = functools.partial(_gdn_head, oha=oha, ohb=ohb)
        taps = _gdn_taps((xq, xk, xv), (hq, hk, hv), live)
        _, vjp = jax.vjp(fn, *taps, _stack_heads(z, GDN_HB), ab[...], st[...], cw[...], gp[...])
        g = vjp((_stack_heads(dy, GDN_HB), ds_scr[hsl]))
        outs = (dq, dk, dv)
        for i in range(GDN_HB):
            sl = slice(i * LANES, (i + 1) * LANES)
            h = _head_id(grp, i, GDN_HB, GDN_HEADS)
            for j in range(3):
                tot = g[j][0][i]
                for s in range(1, GDN_CONV):
                    slot = j * GDN_CONV + s
                    tot = tot + _unshift_rows(g[j][s][i], car_scr[h, slot], s)
                    car_scr[h, slot] = g[j][s][i][0:SUBLANES, :]
                outs[j][:, sl] = tot.astype(outs[j].dtype)
            dz[:, sl] = g[3][i].astype(dz.dtype)
        dab_sum = g[4]
        ds_scr[hsl] = g[5]
        dcw[hsl] += g[6]
        dgp[0] += g[7]

        @pl.when(grp == 0)
        def _():
            dab[...] = dab_sum

        @pl.when(grp > 0)
        def _():
            dab[...] += dab_sum

    rev = lambda n: last - n
    in_specs = _gdn_specs(rev) + [
        pl.BlockSpec((None, GDN_HB, LANES, LANES), lambda n, g: (rev(n), g, 0, 0)),
        pl.BlockSpec((CHUNK, GDN_HB * LANES), lambda n, g: (rev(n), g)),
    ]
    blk = pl.BlockSpec((CHUNK, GDN_HB * LANES), lambda n, g: (rev(n), g))
    return pl.pallas_call(
        body, name="gdn_bwd", grid=(n_chunks, GDN_HEADS // GDN_HB),
        in_specs=in_specs,
        out_specs=(blk, blk, blk, blk,
                   pl.BlockSpec((CHUNK, LANES), lambda n, h: (rev(n), 0)),
                   pl.BlockSpec((GDN_HEADS, 3, SUBLANES, LANES), lambda n, h: (0, 0, 0, 0)),
                   pl.BlockSpec((GDN_HEADS, SUBLANES, LANES), lambda n, h: (0, 0, 0))),
        out_shape=(jax.ShapeDtypeStruct((t, GDN_WIDTH), BF16), jax.ShapeDtypeStruct((t, GDN_WIDTH), BF16),
                   jax.ShapeDtypeStruct((t, GDN_WIDTH), BF16), jax.ShapeDtypeStruct((t, GDN_WIDTH), BF16),
                   jax.ShapeDtypeStruct((t, LANES), F32),
                   jax.ShapeDtypeStruct((GDN_HEADS, 3, SUBLANES, LANES), F32),
                   jax.ShapeDtypeStruct((GDN_HEADS, SUBLANES, LANES), F32)),
        scratch_shapes=[pltpu.VMEM((GDN_HEADS, LANES, LANES), F32),
                        pltpu.VMEM((GDN_HEADS, 3 * GDN_CONV, SUBLANES, LANES), F32)],
        compiler_params=_cparams(("arbitrary", "arbitrary")),
    )(p_cat, p_cat, p_cat, p_cat, p_cat, p_cat, p_cat, p_cat, cwpack, gpar, states, d_out)


def _pick(n, options):
    for o in options:
        if n % o == 0:
            return o
    raise ValueError(f"no tile for {n}")


MM_VMEM_BUDGET = 30 * 1024 * 1024
MM_MIN_STEPS = 8


def _mm_tiles(mode, m, n, k, out_bytes):
    tms = [t for t in (2048, 1024, 768, 512, 256, 128, 64) if m % t == 0 and (mode != 'tn' or t % LANES == 0)]
    tns = [t for t in (1408, 1024, 768, 512, 256, 128) if n % t == 0]
    tks = [t for t in (2048, 1920, 1408, 1024, 512, 256, 128, 64) if k % t == 0]
    best, best_key = None, None
    for tm in tms:
        for tn in tns:
            for tk in tks:
                nk = k // tk
                vmem = 2 * (tm * tk * 2 + tk * tn * 2 + tm * tn * out_bytes) + (tm * tn * 4 if nk > 1 else 0)
                steps = (m // tm) * (n // tn) * nk
                if vmem > MM_VMEM_BUDGET:
                    continue
                key = (steps >= MM_MIN_STEPS, tn if mode == 'tn' else 0, tm * tn * tk, -nk)
                if best_key is None or key > best_key:
                    best, best_key = (tm, tn, tk), key
    if best is None:
        raise ValueError(f"no matmul tile for {mode} {m}x{n}x{k}")
    return best


_MM_DIMS = {'nn': (((1,), (0,)), ((), ())), 'nt': (((1,), (1,)), ((), ())), 'tn': (((0,), (0,)), ((), ()))}


def _matmul(a, b, mode, out_dtype, name):
    if mode == 'nn':
        (m, k), (k2, n) = a.shape, b.shape
    elif mode == 'nt':
        (m, k), (n, k2) = a.shape, b.shape
    else:
        (k, m), (k2, n) = a.shape, b.shape
    assert k == k2, (a.shape, b.shape, mode)
    tm, tn, tk = _mm_tiles(mode, m, n, k, jnp.dtype(out_dtype).itemsize)
    nk = k // tk
    dims = _MM_DIMS[mode]

    def body(a_ref, b_ref, o_ref, acc_ref):
        kk = pl.program_id(2)
        part = lax.dot_general(a_ref[...], b_ref[...], dims, preferred_element_type=F32)
        if nk == 1:
            o_ref[...] = part.astype(o_ref.dtype)
            return

        @pl.when(kk == 0)
        def _():
            acc_ref[...] = part

        @pl.when((kk > 0) & (kk < nk - 1))
        def _():
            acc_ref[...] += part

        @pl.when(kk == nk - 1)
        def _():
            o_ref[...] = (acc_ref[...] + part).astype(o_ref.dtype)

    a_spec = pl.BlockSpec((tk, tm), lambda i, j, kk: (kk, i)) if mode == 'tn' else pl.BlockSpec((tm, tk), lambda i, j, kk: (i, kk))
    b_spec = pl.BlockSpec((tn, tk), lambda i, j, kk: (j, kk)) if mode == 'nt' else pl.BlockSpec((tk, tn), lambda i, j, kk: (kk, j))
    return pl.pallas_call(
        body, name=name, grid=(m // tm, n // tn, nk),
        in_specs=[a_spec, b_spec],
        out_specs=pl.BlockSpec((tm, tn), lambda i, j, kk: (i, j)),
        out_shape=jax.ShapeDtypeStruct((m, n), out_dtype),
        scratch_shapes=[pltpu.VMEM((tm, tn), F32)],
        compiler_params=_cparams(("parallel", "parallel", "arbitrary")),
    )(a, b)


ROW_TILE = 256


def _row_specs(rows, tm):
    return [pl.BlockSpec((tm, w), lambda i, ci=ci: (i, ci)) for (_, w, ci) in rows]


def _rw_forward(fn, rows, pars, outs, name):
    t = rows[0][0].shape[0]
    tm = min(ROW_TILE, t)
    nr, npar = len(rows), len(pars)

    def body(*refs):
        vals = [r[...].astype(F32) for r in refs[:nr]] + [p[...] for p in refs[nr:nr + npar]]
        res = fn(*vals)
        for o, v in zip(refs[nr + npar:], res):
            o[...] = v.astype(o.dtype)

    return pl.pallas_call(
        body, name=name, grid=(t // tm,),
        in_specs=_row_specs(rows, tm) + [pl.BlockSpec(p.shape, lambda i: (0, 0)) for p in pars],
        out_specs=tuple(pl.BlockSpec((tm, w), lambda i: (i, 0)) for (w, _) in outs),
        out_shape=tuple(jax.ShapeDtypeStruct((t, w), dt) for (w, dt) in outs),
        compiler_params=_cparams(("parallel",)),
    )(*[r[0] for r in rows], *pars)


def _rw_backward(fn, rows, pars, cots, drow_dtypes, name):
    t = rows[0][0].shape[0]
    tm = min(ROW_TILE, t)
    nr, npar, nc = len(rows), len(pars), len(cots)
    keep = [i for i, dt in enumerate(drow_dtypes) if dt is not None]

    def body(*refs):
        vals = [r[...].astype(F32) for r in refs[:nr]] + [p[...] for p in refs[nr:nr + npar]]
        cvals = tuple(c[...].astype(F32) for c in refs[nr + npar:nr + npar + nc])
        orefs = refs[nr + npar + nc:]
        _, vjp = jax.vjp(fn, *vals)
        g = vjp(cvals)
        for o, i in zip(orefs[:len(keep)], keep):
            o[...] = g[i].astype(o.dtype)
        first = pl.program_id(0) == 0
        for o, gi in zip(orefs[len(keep):], g[nr:]):
            @pl.when(first)
            def _(o=o, gi=gi):
                o[...] = gi

            @pl.when(jnp.logical_not(first))
            def _(o=o, gi=gi):
                o[...] += gi

    out_specs = [pl.BlockSpec((tm, rows[i][1]), lambda i_: (i_, 0)) for i in keep] + \
                [pl.BlockSpec(p.shape, lambda i_: (0, 0)) for p in pars]
    out_shape = [jax.ShapeDtypeStruct((t, rows[i][1]), drow_dtypes[i]) for i in keep] + \
                [jax.ShapeDtypeStruct(p.shape, F32) for p in pars]
    return pl.pallas_call(
        body, name=name, grid=(t // tm,),
        in_specs=_row_specs(rows, tm) + [pl.BlockSpec(p.shape, lambda i: (0, 0)) for p in pars] + _row_specs(cots, tm),
        out_specs=tuple(out_specs), out_shape=tuple(out_shape),
        compiler_params=_cparams(("arbitrary",)),
    )(*[r[0] for r in rows], *pars, *[c[0] for c in cots])


def _norm_fn(x, g):
    return (_rms(x, g),)


def _norm_skip_fn(x, g):
    return _rms(x, g), x


def _merge_fn(ga, gb, ya, yb):
    return (_sigmoid(ga) * ya + _sigmoid(gb) * yb,)


def _res_norm_fn(x, mo, g):
    x1 = x + mo
    return x1, _rms(x1, g)


def _loss_head(x1, fo, gf, target, name):
    t, d = x1.shape
    tm = min(ROW_TILE, t)

    def tile_loss(x2, g, tgt):
        err = _rms(x2, g) - tgt
        per_row = jnp.sum(err * err, axis=-1, keepdims=True) * (0.5 / d)
        return jnp.sum(per_row, axis=0, keepdims=True)

    def body(x1_ref, fo_ref, g_ref, t_ref, loss_ref, dx_ref, dxb_ref, dg_ref):
        x2 = x1_ref[...] + fo_ref[...]
        val, vjp = jax.vjp(functools.partial(tile_loss, tgt=t_ref[...]), x2, g_ref[...])
        dx2, dg = vjp(jnp.ones((1, 1), F32))
        dx_ref[...] = dx2
        dxb_ref[...] = dx2.astype(BF16)
        first = pl.program_id(0) == 0

        @pl.when(first)
        def _():
            loss_ref[...] = jnp.broadcast_to(val, loss_ref.shape)
            dg_ref[...] = dg

        @pl.when(jnp.logical_not(first))
        def _():
            loss_ref[...] += jnp.broadcast_to(val, loss_ref.shape)
            dg_ref[...] += dg

    row = pl.BlockSpec((tm, d), lambda i: (i, 0))
    vec = pl.BlockSpec((1, d), lambda i: (0, 0))
    return pl.pallas_call(
        body, name=name, grid=(t // tm,),
        in_specs=[row, row, vec, row],
        out_specs=(pl.BlockSpec((1, LANES), lambda i: (0, 0)), row, row, vec),
        out_shape=(jax.ShapeDtypeStruct((1, LANES), F32), jax.ShapeDtypeStruct((t, d), F32),
                   jax.ShapeDtypeStruct((t, d), BF16), jax.ShapeDtypeStruct((1, d), F32)),
        compiler_params=_cparams(("arbitrary",)),
    )(x1, fo, gf, target)


FFN_TILE_ROWS = 512
FFN_TILE_COLS = 256
FFN_COL_BLOCKS = FFN_HIDDEN // FFN_TILE_COLS


def _conv3_past(x, halo, w):
    rows = lax.broadcasted_iota(jnp.int32, x.shape, 0)
    x1 = jnp.where(rows == 0, halo[7:8, :], pltpu.roll(x, 1, 0))
    x2 = jnp.where(rows == 0, halo[6:7, :], jnp.where(rows == 1, halo[7:8, :], pltpu.roll(x, 2, 0)))
    return x * w[2:3] + x1 * w[1:2] + x2 * w[0:1], x1, x2


def _ffn_in_specs(tm, imap, jmap):
    per = tm // SUBLANES
    tile = lambda off: pl.BlockSpec((tm, FFN_TILE_COLS), lambda *g: (imap(*g), off + jmap(*g) % FFN_COL_BLOCKS))
    halo = lambda off: pl.BlockSpec((SUBLANES, FFN_TILE_COLS),
                                    lambda *g: (jnp.maximum(imap(*g) * per - 1, 0), off + jmap(*g) % FFN_COL_BLOCKS))
    wsp = lambda off: pl.BlockSpec((FFN_CONV, FFN_TILE_COLS), lambda *g: (0, off + jmap(*g) % FFN_COL_BLOCKS))
    return [tile(0), halo(0), wsp(0), tile(FFN_COL_BLOCKS), halo(FFN_COL_BLOCKS), wsp(FFN_COL_BLOCKS)]


def _ffn_act_forward(hpre, cw):
    t = hpre.shape[0]
    tm = min(FFN_TILE_ROWS, t)

    def body(hg, pg, wg, hu, pu, wu, out):
        live = (pl.program_id(0) > 0).astype(F32)
        cg, _, _ = _conv3_past(hg[...], pg[...] * live, wg[...])
        cu, _, _ = _conv3_past(hu[...], pu[...] * live, wu[...])
        out[...] = (_silu(cg) * cu).astype(out.dtype)

    return pl.pallas_call(
        body, name="ffn_act_fwd", grid=(t // tm, FFN_COL_BLOCKS),
        in_specs=_ffn_in_specs(tm, lambda i, j: i, lambda i, j: j),
        out_specs=pl.BlockSpec((tm, FFN_TILE_COLS), lambda i, j: (i, j)),
        out_shape=jax.ShapeDtypeStruct((t, FFN_HIDDEN), BF16),
        compiler_params=_cparams(("parallel", "parallel")),
    )(hpre, hpre, cw, hpre, hpre, cw)


def _conv3_future(d, nxt, w):
    tm = d.shape[0]
    rows = lax.broadcasted_iota(jnp.int32, d.shape, 0)
    d1 = jnp.where(rows == tm - 1, nxt[0:1, :], pltpu.roll(d, tm - 1, 0))
    d2 = jnp.where(rows == tm - 1, nxt[1:2, :], jnp.where(rows == tm - 2, nxt[0:1, :], pltpu.roll(d, tm - 2, 0)))
    return d * w[2:3] + d1 * w[1:2] + d2 * w[0:1]


def _ffn_backward(hpre, cw, dact):
    t = hpre.shape[0]
    tm = min(FFN_TILE_ROWS, t)
    n_tiles = t // tm
    per = tm // SUBLANES

    def d_conv_out(cg, cu, d):
        s = _sigmoid(cg)
        return d * cu * s * (1.0 + cg * (1.0 - s)), d * cg * s

    def body(hg, pg, ng, wg, hu, pu, nu, wu, da, dan, dhg, dhu, dwg, dwu):
        i = pl.program_id(1)
        live_prev = (i > 0).astype(F32)
        live_next = (i < n_tiles - 1).astype(F32)
        xg, xu = hg[...], hu[...]
        cg, g1, g2 = _conv3_past(xg, pg[...] * live_prev, wg[...])
        cu, u1, u2 = _conv3_past(xu, pu[...] * live_prev, wu[...])
        dg, du = d_conv_out(cg, cu, da[...])
        cgn, _, _ = _conv3_past(ng[...], hg[tm - SUBLANES:tm, :], wg[...])
        cun, _, _ = _conv3_past(nu[...], hu[tm - SUBLANES:tm, :], wu[...])
        dgn, dun = d_conv_out(cgn, cun, dan[...] * live_next)
        dhg[...] = _conv3_future(dg, dgn, wg[...]).astype(dhg.dtype)
        dhu[...] = _conv3_future(du, dun, wu[...]).astype(dhu.dtype)
        sums_g = [jnp.sum(xs * dg, axis=0, keepdims=True) for xs in (g2, g1, xg)]
        sums_u = [jnp.sum(xs * du, axis=0, keepdims=True) for xs in (u2, u1, xu)]

        @pl.when(i == 0)
        def _():
            for r_ in range(FFN_CONV):
                dwg[r_:r_ + 1, :] = sums_g[r_]
                dwu[r_:r_ + 1, :] = sums_u[r_]

        @pl.when(i > 0)
        def _():
            for r_ in range(FFN_CONV):
                dwg[r_:r_ + 1, :] += sums_g[r_]
                dwu[r_:r_ + 1, :] += sums_u[r_]

    nb = FFN_COL_BLOCKS
    nxt = lambda i: jnp.minimum((i + 1) * per, t // SUBLANES - 1)
    prv = lambda i: jnp.maximum(i * per - 1, 0)
    half = lambda off: [pl.BlockSpec((tm, FFN_TILE_COLS), lambda j, i: (i, off + j)),
                        pl.BlockSpec((SUBLANES, FFN_TILE_COLS), lambda j, i: (prv(i), off + j)),
                        pl.BlockSpec((SUBLANES, FFN_TILE_COLS), lambda j, i: (nxt(i), off + j)),
                        pl.BlockSpec((FFN_CONV, FFN_TILE_COLS), lambda j, i: (0, off + j))]
    tile = pl.BlockSpec((tm, FFN_TILE_COLS), lambda j, i: (i, j))
    taps = pl.BlockSpec((FFN_CONV, FFN_TILE_COLS), lambda j, i: (0, j))
    return pl.pallas_call(
        body, name="ffn_bwd", grid=(nb, n_tiles),
        in_specs=half(0) + half(nb) + [tile, pl.BlockSpec((SUBLANES, FFN_TILE_COLS), lambda j, i: (nxt(i), j))],
        out_specs=(tile, tile, taps, taps),
        out_shape=(jax.ShapeDtypeStruct((t, FFN_HIDDEN), BF16), jax.ShapeDtypeStruct((t, FFN_HIDDEN), BF16),
                   jax.ShapeDtypeStruct((FFN_CONV, FFN_HIDDEN), F32), jax.ShapeDtypeStruct((FFN_CONV, FFN_HIDDEN), F32)),
        compiler_params=_cparams(("parallel", "arbitrary")),
    )(hpre, hpre, hpre, cw, hpre, hpre, hpre, cw, dact, dact)


def _my_place():
    x, y, c = lax.axis_index("x"), lax.axis_index("y"), lax.axis_index("c")
    return x, y, c, 4 * x + 2 * y + c


N_CHIPS = 4


def _remote(src, dst, send_sem, recv_sem, dev):
    return pltpu.make_async_remote_copy(src_ref=src, dst_ref=dst, send_sem=send_sem, recv_sem=recv_sem, device_id=dev,
                                        device_id_type=pl.DeviceIdType.MESH)


def _chip_peer(x, y, k):
    return x ^ ((k >> 1) & 1), y ^ (k & 1)


def _all_gather_two_level(shard, name):
    r, w = shard.shape

    def body(src, out, send_sems, recv_sems, local_sem):
        x, y, c, me = _my_place()
        sibling = (x, y, 1 - c)
        mine = pltpu.make_async_copy(src, out.at[me], local_sem)
        mine.start()
        first = [_remote(src, out.at[me], send_sems.at[0], recv_sems.at[0], sibling)]
        for k in range(1, N_CHIPS):
            px, py = _chip_peer(x, y, k)
            first.append(_remote(src, out.at[me], send_sems.at[k], recv_sems.at[k], (px, py, c)))
        for cp in first:
            cp.start()
        passed = []
        for k in range(1, N_CHIPS):
            px, py = _chip_peer(x, y, k)
            landed = out.at[me ^ (2 * k)]
            _remote(src, landed, send_sems.at[k], recv_sems.at[k], (px, py, c)).wait_recv()
            fwd = _remote(landed, landed, send_sems.at[N_CHIPS - 1 + k], recv_sems.at[N_CHIPS - 1 + k], sibling)
            fwd.start()
            passed.append(fwd)
        _remote(src, out.at[me ^ 1], send_sems.at[0], recv_sems.at[0], sibling).wait_recv()
        for k in range(1, N_CHIPS):
            got = out.at[(me ^ 1) ^ (2 * k)]
            _remote(got, got, send_sems.at[N_CHIPS - 1 + k], recv_sems.at[N_CHIPS - 1 + k], sibling).wait_recv()
        for cp in first + passed:
            cp.wait_send()
        mine.wait()

    return pl.pallas_call(
        body, name=name,
        in_specs=[pl.BlockSpec(memory_space=pl.ANY)],
        out_specs=pl.BlockSpec(memory_space=pl.ANY),
        out_shape=jax.ShapeDtypeStruct((N_DEV, r, w), shard.dtype),
        scratch_shapes=[pltpu.SemaphoreType.DMA((N_DEV - 1,)), pltpu.SemaphoreType.DMA((N_DEV - 1,)), pltpu.SemaphoreType.DMA],
    )(shard)


def _device_peer(x, y, c, k):
    px, py, pc = x ^ ((k >> 2) & 1), y ^ ((k >> 1) & 1), c ^ (k & 1)
    return (px, py, pc), 4 * px + 2 * py + pc


_HBM = pl.BlockSpec(memory_space=pltpu.HBM)
_SEM = pl.BlockSpec(memory_space=pltpu.SEMAPHORE)


def _gather_start(shard, name):
    def body(src, land, send_sems, recv_sems, src_thru, land_thru, token):
        x, y, c, me = _my_place()
        for k in range(1, N_DEV):
            dev, _ = _device_peer(x, y, c, k)
            _remote(src, land.at[me], send_sems.at[k], recv_sems.at[k], dev).start()
        token[...] = jnp.zeros_like(token)

    landing = lax.empty((N_DEV,) + shard.shape, shard.dtype)
    return pl.pallas_call(
        body, name=name,
        out_shape=(pltpu.SemaphoreType.DMA((N_DEV,)), pltpu.SemaphoreType.DMA((N_DEV,)), pltpu.HBM(shard.shape, shard.dtype),
                   pltpu.HBM(landing.shape, landing.dtype), jax.ShapeDtypeStruct((SUBLANES, LANES), F32)),
        in_specs=(_HBM, _HBM), out_specs=(_SEM, _SEM, _HBM, _HBM, pl.BlockSpec(memory_space=pltpu.VMEM)),
        input_output_aliases={0: 2, 1: 3},
        compiler_params=pltpu.CompilerParams(has_side_effects=pltpu.SideEffectType.DATAFLOW_SIDE_EFFECTING),
    )(pltpu.with_memory_space_constraint(shard, pltpu.HBM), pltpu.with_memory_space_constraint(landing, pltpu.HBM))


def _gather_wait(send_sems, recv_sems, shard, landing, after, name):
    n_after = len(after)

    def body(*refs):
        src, land, send_sems, recv_sems = refs[:4]
        x, y, c, _ = _my_place()
        for k in range(1, N_DEV):
            dev, idx = _device_peer(x, y, c, k)
            cp = _remote(src, land.at[idx], send_sems.at[k], recv_sems.at[k], dev)
            cp.wait_send()
            cp.wait_recv()

    return pl.pallas_call(
        body, name=name,
        out_shape=(pltpu.HBM(shard.shape, shard.dtype), pltpu.HBM(landing.shape, landing.dtype)),
        in_specs=(_HBM, _HBM, _SEM, _SEM) + (pl.BlockSpec(memory_space=pl.ANY),) * n_after, out_specs=(_HBM, _HBM),
        input_output_aliases={0: 0, 1: 1},
        compiler_params=pltpu.CompilerParams(has_side_effects=pltpu.SideEffectType.DATAFLOW_SIDE_EFFECTING),
    )(shard, landing, send_sems, recv_sems, *after)[1]


def _slab_push_start(slabs, name):
    na = len(slabs)

    def body(*refs):
        srcs, lands = refs[:na], refs[na:2 * na]
        send_sems, recv_sems = refs[2 * na], refs[2 * na + 1]
        token = refs[-1]
        x, y, c, me = _my_place()
        for i in range(na):
            for k in range(1, N_DEV):
                dev, idx = _device_peer(x, y, c, k)
                s = i * N_DEV + k
                _remote(srcs[i].at[idx], lands[i].at[me], send_sems.at[s], recv_sems.at[s], dev).start()
        token[...] = jnp.zeros_like(token)

    hbm_shapes = [pltpu.HBM(a.shape, a.dtype) for a in slabs]
    ins = [pltpu.with_memory_space_constraint(a, pltpu.HBM) for a in slabs]
    ins += [pltpu.with_memory_space_constraint(lax.empty(a.shape, a.dtype), pltpu.HBM) for a in slabs]
    out = pl.pallas_call(
        body, name=name,
        out_shape=(pltpu.SemaphoreType.DMA((na * N_DEV,)), pltpu.SemaphoreType.DMA((na * N_DEV,)), *hbm_shapes, *hbm_shapes,
                   jax.ShapeDtypeStruct((SUBLANES, LANES), F32)),
        in_specs=(_HBM,) * (2 * na), out_specs=(_SEM, _SEM) + (_HBM,) * (2 * na) + (pl.BlockSpec(memory_space=pltpu.VMEM),),
        input_output_aliases={i: 2 + i for i in range(2 * na)},
        compiler_params=pltpu.CompilerParams(has_side_effects=pltpu.SideEffectType.DATAFLOW_SIDE_EFFECTING),
    )(*ins)
    return out[0], out[1], list(out[2:2 + na]), list(out[2 + na:2 + 2 * na]), out[-1]


def _slab_push_wait(send_sems, recv_sems, slabs, landings, after, name):
    na = len(slabs)

    def body(*refs):
        srcs, lands = refs[:na], refs[na:2 * na]
        send_sems, recv_sems = refs[2 * na], refs[2 * na + 1]
        x, y, c, me = _my_place()
        for i in range(na):
            for k in range(1, N_DEV):
                dev, idx = _device_peer(x, y, c, k)
                s = i * N_DEV + k
                cp = _remote(srcs[i].at[idx], lands[i].at[idx], send_sems.at[s], recv_sems.at[s], dev)
                cp.wait_send()
                cp.wait_recv()

    hbm_shapes = tuple(pltpu.HBM(a.shape, a.dtype) for a in slabs)
    out = pl.pallas_call(
        body, name=name, out_shape=hbm_shapes + hbm_shapes,
        in_specs=(_HBM,) * (2 * na) + (_SEM, _SEM) + (pl.BlockSpec(memory_space=pl.ANY),) * len(after),
        out_specs=(_HBM,) * (2 * na), input_output_aliases={i: i for i in range(2 * na)},
        compiler_params=pltpu.CompilerParams(has_side_effects=pltpu.SideEffectType.DATAFLOW_SIDE_EFFECTING),
    )(*slabs, *landings, send_sems, recv_sems, *after)
    return list(out[na:])


def _pair_exchange(arrays, name):
    na = len(arrays)

    def body(*refs):
        srcs, dsts, (send_sems, recv_sems) = refs[:na], refs[na:2 * na], refs[2 * na:]
        x, y, c, _ = _my_place()
        sibling = (x, y, 1 - c)
        copies = []
        for i in range(na):
            for q in range(N_CHIPS):
                s = i * N_CHIPS + q
                copies.append(_remote(srcs[i].at[2 * q + 1 - c], dsts[i].at[q], send_sems.at[s], recv_sems.at[s], sibling))
        for cp in copies:
            cp.start()
        for cp in copies:
            cp.wait_recv()
        for cp in copies:
            cp.wait_send()

    hbm = pl.BlockSpec(memory_space=pl.ANY)
    return pl.pallas_call(
        body, name=name, in_specs=[hbm] * na, out_specs=tuple([hbm] * na),
        out_shape=tuple(jax.ShapeDtypeStruct((N_CHIPS,) + a.shape[1:], a.dtype) for a in arrays),
        scratch_shapes=[pltpu.SemaphoreType.DMA((na * N_CHIPS,)), pltpu.SemaphoreType.DMA((na * N_CHIPS,))],
    )(*arrays)


ELEMENTWISE_COLS = 256


def _pair_sum(slabs, recv, core, out_dtype, name):
    _, r, w = slabs.shape
    tc = ELEMENTWISE_COLS

    def body(core_ref, mine, theirs, out):
        out[...] = (mine[...] + theirs[...]).astype(out.dtype)

    grid_spec = pltpu.PrefetchScalarGridSpec(
        num_scalar_prefetch=1, grid=(N_CHIPS, w // tc),
        in_specs=[pl.BlockSpec((None, r, tc), lambda q, j, core_ref: (2 * q + core_ref[0], 0, j)),
                  pl.BlockSpec((None, r, tc), lambda q, j, core_ref: (q, 0, j))],
        out_specs=pl.BlockSpec((None, r, tc), lambda q, j, core_ref: (q, 0, j)))
    return pl.pallas_call(body, name=name, grid_spec=grid_spec,
                          out_shape=jax.ShapeDtypeStruct((N_CHIPS, r, w), out_dtype),
                          compiler_params=_cparams(("parallel", "parallel")))(core, slabs, recv)


def _chip_exchange(arrays, name):
    na = len(arrays)

    def body(*refs):
        srcs, dsts, (send_sems, recv_sems, local_sems) = refs[:na], refs[na:2 * na], refs[2 * na:]
        x, y, c, _ = _my_place()
        chip = 2 * x + y
        own = [pltpu.make_async_copy(srcs[i].at[chip], dsts[i].at[chip], local_sems.at[i]) for i in range(na)]
        for cp in own:
            cp.start()
        sends, arrivals = [], []
        for i in range(na):
            for k in range(1, N_CHIPS):
                px, py = _chip_peer(x, y, k)
                s = i * N_CHIPS + k
                sends.append(_remote(srcs[i].at[chip ^ k], dsts[i].at[chip], send_sems.at[s], recv_sems.at[s], (px, py, c)))
                arrivals.append(_remote(srcs[i].at[chip], dsts[i].at[chip ^ k], send_sems.at[s], recv_sems.at[s], (px, py, c)))
        for cp in sends:
            cp.start()
        for cp in arrivals:
            cp.wait_recv()
        for cp in sends:
            cp.wait_send()
        for cp in own:
            cp.wait()

    hbm = pl.BlockSpec(memory_space=pl.ANY)
    return pl.pallas_call(
        body, name=name, in_specs=[hbm] * na, out_specs=tuple([hbm] * na),
        out_shape=tuple(jax.ShapeDtypeStruct(a.shape, a.dtype) for a in arrays),
        scratch_shapes=[pltpu.SemaphoreType.DMA((na * N_CHIPS,)), pltpu.SemaphoreType.DMA((na * N_CHIPS,)),
                        pltpu.SemaphoreType.DMA((na,))],
    )(*arrays)


def _reduce_adamw(parts, w, m, v, name):
    n_parts, r, wd = parts.shape
    tc = ELEMENTWISE_COLS
    c1 = 1.0 / (1.0 - ADAM_B1 ** ADAM_STEP)
    c2 = 1.0 / (1.0 - ADAM_B2 ** ADAM_STEP)

    def body(p_ref, w_ref, m_ref, v_ref, g_out, d_out, m_out, v_out):
        g = p_ref[0].astype(F32)
        for s in range(1, n_parts):
            g = g + p_ref[s].astype(F32)
        mn = ADAM_B1 * m_ref[...] + (1.0 - ADAM_B1) * g
        vn = ADAM_B2 * v_ref[...] + (1.0 - ADAM_B2) * (g * g)
        g_out[...] = g
        m_out[...] = mn
        v_out[...] = vn
        d_out[...] = -ADAM_LR * ((mn * c1) / (jnp.sqrt(vn * c2) + ADAM_EPS) + ADAM_WD * w_ref[...])

    blk = pl.BlockSpec((r, tc), lambda j: (0, j))
    shp = jax.ShapeDtypeStruct((r, wd), F32)
    return pl.pallas_call(
        body, name=name, grid=(wd // tc,),
        in_specs=[pl.BlockSpec((n_parts, r, tc), lambda j: (0, 0, j)), blk, blk, blk],
        out_specs=(blk, blk, blk, blk), out_shape=(shp, shp, shp, shp),
        compiler_params=_cparams(("parallel",)),
    )(parts, w, m, v)


PACK_W = 1024


def _pad_heads(a, slots):
    lead = a.shape[:-1]
    a = a.reshape(lead + (slots, RWKV_HEAD_DIM))
    a = jnp.pad(a, [(0, 0)] * (len(lead) + 1) + [(0, LANES - RWKV_HEAD_DIM)])
    return a.reshape(lead + (slots * LANES,))


def _unpad_heads(a, slots):
    lead = a.shape[:-1]
    return a.reshape(lead + (slots, LANES))[..., :RWKV_HEAD_DIM].reshape(lead + (slots * RWKV_HEAD_DIM,))


def _flat_pack(arrs, dtype, row_mult):
    flat = jnp.concatenate([a.reshape(-1).astype(dtype) for a in arrs])
    n = flat.shape[0]
    rows = -(-n // PACK_W)
    rows = -(-rows // row_mult) * row_mult
    return jnp.pad(flat, (0, rows * PACK_W - n)).reshape(rows, PACK_W)


def _row_pack(arrs, dtype, row_mult):
    parts = [a.astype(dtype) if a.shape[1] == PACK_W else a.astype(dtype).reshape(-1, PACK_W) for a in arrs]
    rows = sum(p.shape[0] for p in parts)
    pad = -(-rows // row_mult) * row_mult - rows
    return jnp.concatenate(parts + ([jnp.zeros((pad, PACK_W), dtype)] if pad else []), axis=0)


def _unpack_row_gathered(g, names, shard_shapes):
    out, r0 = {}, 0
    for n in names:
        s = shard_shapes[n]
        rows = s[0] * s[1] // PACK_W
        seg = g[:, r0:r0 + rows, :]
        r0 += rows
        if s[1] == PACK_W:
            assert SHARD_AXIS[n] == 0
            out[n] = seg.reshape(N_DEV * s[0], s[1])
        else:
            assert SHARD_AXIS[n] == 1
            out[n] = jnp.transpose(seg.reshape((N_DEV,) + tuple(s)), (1, 0, 2)).reshape(s[0], N_DEV * s[1])
    return out


def _unpack_gathered(g, names, shard_shapes):
    flat = g.reshape(N_DEV, -1)
    out, off = {}, 0
    for n in names:
        s = shard_shapes[n]
        size = s[0] * s[1]
        seg = flat[:, off:off + size].reshape((N_DEV,) + tuple(s))
        off += size
        if SHARD_AXIS[n] == 1:
            out[n] = jnp.transpose(seg, (1, 0, 2)).reshape(s[0], N_DEV * s[1])
        else:
            out[n] = seg.reshape(N_DEV * s[0], s[1])
    return out


def _shard_major(full, axis):
    a, b = full.shape
    if axis == 1:
        return jnp.transpose(full.reshape(a, N_DEV, b // N_DEV), (1, 0, 2)).reshape(N_DEV, -1)
    return full.reshape(N_DEV, -1)


def _prepare_weights(full, rep):
    w = full['w_in']
    d = w.shape[1]
    rkv = jnp.pad(w[0:1536].reshape(3 * RWKV_HEADS, RWKV_HEAD_DIM, d), ((0, 0), (0, LANES - RWKV_HEAD_DIM), (0, 0)))
    w_cat = jnp.concatenate([
        w[3848:4872], w[4872:5896], rkv.reshape(3 * RWKV_HEADS * LANES, d), w[1792:3328], w[3328:3840],
        w[1536:1792], jnp.pad(w[3840:3848], ((0, LANES - 8), (0, 0))), jnp.zeros((LANES, d), w.dtype)], axis=0)
    assert w_cat.shape[0] == CAT_W
    mu = rep['rwkv_mu']
    vecs = [mu[0:512], mu[512:1024], mu[1024:1536], rep['rwkv_w0'], rep['rwkv_a0'], rep['rwkv_k_k'], rep['rwkv_k_a'],
            rep['rwkv_ln_w'], rep['rwkv_ln_b'], rep['rwkv_r_k'].reshape(-1)]
    ppack = jnp.stack([jnp.pad(v.reshape(RWKV_HEADS, RWKV_HEAD_DIM), ((0, 0), (0, LANES - RWKV_HEAD_DIM))) for v in vecs], axis=1)
    ppack = jnp.pad(ppack, ((0, 0), (0, 16 - len(vecs)), (0, 0)))
    mulo = mu[1536:1792].reshape(1, 2 * LANES)
    wl = jnp.zeros((3, 2 * LANES, RWKV_HEADS * LANES), F32)
    wl = wl.at[0, 0:64].set(_pad_heads(full['rwkv_w2'], RWKV_HEADS))
    wl = wl.at[1, 64:128].set(_pad_heads(full['rwkv_a2'], RWKV_HEADS))
    wl = wl.at[2, 128:256].set(_pad_heads(full['rwkv_g2'], RWKV_HEADS))
    wl = jnp.transpose(wl.reshape(3, 2 * LANES, RWKV_HEADS, LANES), (2, 0, 1, 3))
    cw = full['gdn_conv_w'].reshape(GDN_CONV, 3, GDN_HEADS, LANES)
    cwpack = jnp.pad(jnp.transpose(cw, (2, 1, 0, 3)), ((0, 0), (0, 0), (0, SUBLANES - GDN_CONV), (0, 0)))
    gpar = jnp.zeros((SUBLANES, LANES), F32)
    gpar = gpar.at[0, 0:GDN_HEADS].set(rep['gdn_a_log']).at[1, 0:GDN_HEADS].set(rep['gdn_dt_bias']).at[2].set(rep['gdn_norm_w'])
    return dict(w_cat=w_cat, ffn_cw=full['ffn_conv_w'], ppack=ppack, mulo=mulo, wl=wl, cwpack=cwpack, gpar=gpar,
                g1=rep['norm1_g'].reshape(1, -1), g2=rep['norm2_g'].reshape(1, -1), gf=rep['final_g'].reshape(1, -1))


def _prepare_late_weights(full):
    rp = full['rwkv_proj']
    rproj = jnp.pad(rp.reshape(RWKV_HEADS, RWKV_HEAD_DIM, -1), ((0, 0), (0, LANES - RWKV_HEAD_DIM), (0, 0))).reshape(RWKV_HEADS * LANES, -1)
    return dict(rproj=rproj, gproj=full['gdn_proj'], w_out=full['w_out'], ffn_up=full['ffn_up'], ffn_down=full['ffn_down'])


def _local_step(x, target, p, late_weights, push_grads):
    d = x.shape[1]
    full_w = lambda a: (a, a.shape[1], 0)
    (u,) = _rw_forward(_norm_fn, [full_w(x)], [p['g1']], [(d, BF16)], "norm1")
    p_cat = _matmul(u, p['w_cat'], 'nt', F32, "proj_in")
    ya_pre, st_r = _rwkv_forward(p_cat, p['ppack'], p['mulo'], p['wl'])
    yb_pre, st_g = _gdn_forward(p_cat, p['cwpack'], p['gpar'])
    p = {**p, **late_weights((ya_pre, yb_pre))}
    ya = _matmul(ya_pre, p['rproj'], 'nn', F32, "rwkv_proj")
    yb = _matmul(yb_pre, p['gproj'], 'nn', F32, "gdn_proj")
    gates = [(p_cat, d, OFF_GA // d), (p_cat, d, OFF_GB // d)]
    (mixed,) = _rw_forward(_merge_fn, gates + [full_w(ya), full_w(yb)], [], [(d, BF16)], "merge")
    mo = _matmul(mixed, p['w_out'], 'nn', F32, "out_proj")
    x1, n2 = _rw_forward(_res_norm_fn, [full_w(x), full_w(mo)], [p['g2']], [(d, F32), (d, BF16)], "res_norm2")
    hpre = _matmul(n2, p['ffn_up'], 'nt', F32, "ffn_up")
    act = _ffn_act_forward(hpre, p['ffn_cw'])
    fo = _matmul(act, p['ffn_down'], 'nn', F32, "ffn_down")
    loss_vec, dx2, dx2b, dgf = _loss_head(x1, fo, p['gf'], target, "loss_head")

    dact = _matmul(dx2b, p['ffn_down'], 'nt', F32, "d_act")
    dw_down = _matmul(act, dx2b, 'tn', BF16, "dw_ffn_down")
    dh_gate, dh_up, dcw_gate, dcw_up = _ffn_backward(hpre, p['ffn_cw'], dact)
    dh = jnp.concatenate([dh_gate, dh_up], axis=1)
    dcw_f = jnp.concatenate([dcw_gate, dcw_up], axis=1)
    dn2 = _matmul(dh, p['ffn_up'], 'nn', F32, "d_norm2")
    dw_up = _matmul(dh, n2, 'tn', BF16, "dw_ffn_up")
    token = push_grads({'ffn_down': dw_down, 'ffn_up': dw_up})
    dx1, dx1b, dg2 = _rw_backward(_res_norm_fn, [full_w(x), full_w(mo)], [p['g2'] + token], [full_w(dx2), full_w(dn2)],
                                  [F32, BF16], "res_norm2_bwd")
    dmixed = _matmul(dx1b, p['w_out'], 'nt', F32, "d_mixed")
    dw_out = _matmul(mixed, dx1b, 'tn', BF16, "dw_out")
    dga, dgb, dya, dyb = _rw_backward(_merge_fn, gates + [full_w(ya), full_w(yb)], [], [full_w(dmixed)],
                                      [BF16, BF16, BF16, BF16], "merge_bwd")
    d_ya_pre = _matmul(dya, p['rproj'], 'nt', F32, "d_rwkv_out")
    dw_rproj = _matmul(ya_pre, dya, 'tn', F32, "dw_rwkv_proj")
    d_yb_pre = _matmul(dyb, p['gproj'], 'nt', F32, "d_gdn_out")
    dw_gproj = _matmul(yb_pre, dyb, 'tn', F32, "dw_gdn_proj")
    dpr, dpk, dpv, dplo, dpp, dml, dwl = _rwkv_backward(p_cat, p['ppack'], p['mulo'], p['wl'], st_r, d_ya_pre)
    dq, dk, dv, dz, dab, dcw_g, dgp = _gdn_backward(p_cat, p['cwpack'], p['gpar'], st_g, d_yb_pre)
    t = x.shape[0]
    dp_cat = jnp.concatenate([dga, dgb, dpr, dpk, dpv, dq, dk, dv, dz, dplo.astype(BF16), dab.astype(BF16),
                              jnp.zeros((t, LANES), BF16)], axis=1)
    dw_cat = _matmul(dp_cat, u, 'tn', BF16, "dw_in")
    dw_in = jnp.concatenate([dw_cat[OFF_RKV:OFF_QKV].reshape(3 * RWKV_HEADS, LANES, d)[:, :RWKV_HEAD_DIM].reshape(-1, d),
                             dw_cat[OFF_LO:OFF_AB], dw_cat[OFF_QKV:OFF_Z], dw_cat[OFF_Z:OFF_LO], dw_cat[OFF_AB:OFF_AB + 8],
                             dw_cat[OFF_GA:OFF_GB], dw_cat[OFF_GB:OFF_RKV]], axis=0)
    token = push_grads({'w_out': dw_out, 'w_in': dw_in})
    du = _matmul(dp_cat, p['w_cat'], 'nn', F32, "d_norm1")
    grad_x, dg1 = _rw_backward(_norm_skip_fn, [full_w(x)], [p['g1'] + token], [full_w(du), full_w(dx1)], [F32], "norm1_bwd")

    heads = lambda row: dpp[:, row, :RWKV_HEAD_DIM].reshape(-1)
    lora = lambda j, lo_, hi_: jnp.transpose(dwl[:, j, lo_:hi_, :RWKV_HEAD_DIM], (1, 0, 2)).reshape(hi_ - lo_, RWKV_WIDTH)
    grads = {
        'norm1_g': dg1[0],
        'w_in': dw_in,
        'rwkv_mu': jnp.concatenate([heads(0), heads(1), heads(2), jnp.sum(dml[:, 0, :], axis=0)]),
        'rwkv_w0': heads(3), 'rwkv_a0': heads(4), 'rwkv_k_k': heads(5), 'rwkv_k_a': heads(6),
        'rwkv_ln_w': heads(7), 'rwkv_ln_b': heads(8), 'rwkv_r_k': heads(9).reshape(RWKV_HEADS, RWKV_HEAD_DIM),
        'rwkv_w2': lora(0, 0, 64), 'rwkv_a2': lora(1, 64, 128), 'rwkv_g2': lora(2, 128, 256),
        'rwkv_proj': dw_rproj.reshape(RWKV_HEADS, LANES, -1)[:, :RWKV_HEAD_DIM].reshape(RWKV_WIDTH, -1),
        'gdn_conv_w': jnp.transpose(dcw_g[:, :, :GDN_CONV, :], (2, 1, 0, 3)).reshape(GDN_CONV, 3 * GDN_WIDTH),
        'gdn_a_log': jnp.sum(dgp[:, 0, :GDN_HEADS], axis=0), 'gdn_dt_bias': jnp.sum(dgp[:, 1, :GDN_HEADS], axis=0),
        'gdn_norm_w': jnp.sum(dgp[:, 2, :], axis=0),
        'gdn_proj': dw_gproj, 'w_out': dw_out, 'norm2_g': dg2[0], 'ffn_up': dw_up, 'ffn_conv_w': dcw_f,
        'ffn_down': dw_down, 'final_g': dgf[0],
    }
    return loss_vec, grad_x, grads


def kernel(x, norm1_g, w_in, rwkv_mu, rwkv_w0, rwkv_w2, rwkv_a0, rwkv_a2, rwkv_g2, rwkv_k_k, rwkv_k_a, rwkv_r_k, rwkv_ln_w, rwkv_ln_b, rwkv_proj, gdn_conv_w, gdn_a_log, gdn_dt_bias, gdn_norm_w, gdn_proj, w_out, norm2_g, ffn_up, ffn_conv_w, ffn_down, final_g, loss_target, m_norm1_g, m_w_in, m_rwkv_mu, m_rwkv_w0, m_rwkv_w2, m_rwkv_a0, m_rwkv_a2, m_rwkv_g2, m_rwkv_k_k, m_rwkv_k_a, m_rwkv_r_k, m_rwkv_ln_w, m_rwkv_ln_b, m_rwkv_proj, m_gdn_conv_w, m_gdn_a_log, m_gdn_dt_bias, m_gdn_norm_w, m_gdn_proj, m_w_out, m_norm2_g, m_ffn_up, m_ffn_conv_w, m_ffn_down, m_final_g, v_norm1_g, v_w_in, v_rwkv_mu, v_rwkv_w0, v_rwkv_w2, v_rwkv_a0, v_rwkv_a2, v_rwkv_g2, v_rwkv_k_k, v_rwkv_k_a, v_rwkv_r_k, v_rwkv_ln_w, v_rwkv_ln_b, v_rwkv_proj, v_gdn_conv_w, v_gdn_a_log, v_gdn_dt_bias, v_gdn_norm_w, v_gdn_proj, v_w_out, v_norm2_g, v_ffn_up, v_ffn_conv_w, v_ffn_down, v_final_g):
    given = dict(zip(WEIGHT_NAMES, (norm1_g, w_in, rwkv_mu, rwkv_w0, rwkv_w2, rwkv_a0, rwkv_a2, rwkv_g2, rwkv_k_k, rwkv_k_a, rwkv_r_k,
                                    rwkv_ln_w, rwkv_ln_b, rwkv_proj, gdn_conv_w, gdn_a_log, gdn_dt_bias, gdn_norm_w, gdn_proj, w_out,
                                    norm2_g, ffn_up, ffn_conv_w, ffn_down, final_g)))
    mom1 = dict(zip(WEIGHT_NAMES, (m_norm1_g, m_w_in, m_rwkv_mu, m_rwkv_w0, m_rwkv_w2, m_rwkv_a0, m_rwkv_a2, m_rwkv_g2, m_rwkv_k_k,
                                   m_rwkv_k_a, m_rwkv_r_k, m_rwkv_ln_w, m_rwkv_ln_b, m_rwkv_proj, m_gdn_conv_w, m_gdn_a_log,
                                   m_gdn_dt_bias, m_gdn_norm_w, m_gdn_proj, m_w_out, m_norm2_g, m_ffn_up, m_ffn_conv_w, m_ffn_down,
                                   m_final_g)))
    mom2 = dict(zip(WEIGHT_NAMES, (v_norm1_g, v_w_in, v_rwkv_mu, v_rwkv_w0, v_rwkv_w2, v_rwkv_a0, v_rwkv_a2, v_rwkv_g2, v_rwkv_k_k,
                                   v_rwkv_k_a, v_rwkv_r_k, v_rwkv_ln_w, v_rwkv_ln_b, v_rwkv_proj, v_gdn_conv_w, v_gdn_a_log,
                                   v_gdn_dt_bias, v_gdn_norm_w, v_gdn_proj, v_w_out, v_norm2_g, v_ffn_up, v_ffn_conv_w, v_ffn_down,
                                   v_final_g)))
    def strip(n, a):
        a = a if n == 'final_g' else a.reshape(a.shape[1:])
        return a.T if n in TRANSPOSED else a

    local = {n: strip(n, a) for n, a in given.items()}
    shard_shapes = {n: local[n].shape for n in SHARD_AXIS}
    sharded = BIG_SHARDED + SMALL_SHARDED

    late_names = [n for n in BIG_SHARDED if n != 'w_in']
    late_pack = _row_pack([local[n] for n in late_names], BF16, 16)
    send_sems, recv_sems, late_pack, landing, token = _gather_start(late_pack, "gather_late_start")
    g_in = _all_gather_two_level(_row_pack([local['w_in']], BF16, 16), "gather_w_in")
    g_small = _all_gather_two_level(_flat_pack([local[n] for n in SMALL_SHARDED], F32, SUBLANES), "gather_small")
    full = _unpack_row_gathered(g_in, ['w_in'], shard_shapes)
    full.update(_unpack_gathered(g_small, SMALL_SHARDED, shard_shapes))
    rep = {n: local[n] for n in REPLICATED}
    rep['norm1_g'] = rep['norm1_g'] + token[0, 0]

    def late_weights(after):
        got = _gather_wait(send_sems, recv_sems, late_pack, landing, after, "gather_late_wait")
        me = 4 * lax.axis_index("x") + 2 * lax.axis_index("y") + lax.axis_index("c")
        slot = lax.broadcasted_iota(jnp.int32, (N_DEV, 1, 1), 0)
        got = jnp.where(slot == me, late_pack[None], got)
        return _prepare_late_weights(_unpack_row_gathered(got, late_names, shard_shapes))

    pushes = []
    me = 4 * lax.axis_index("x") + 2 * lax.axis_index("y") + lax.axis_index("c")
    slot = lax.broadcasted_iota(jnp.int32, (N_DEV, 1, 1), 0)

    def push_grads(group):
        names = list(group)
        slabs = [group[n].reshape(N_DEV, -1, group[n].shape[1]) for n in names]
        send_sems, recv_sems, slabs, landings, token = _slab_push_start(slabs, "grad_push_start_" + "_".join(names))
        pushes.append((names, send_sems, recv_sems, slabs, landings))
        return token[0, 0]

    loss_vec, grad_x, grads = _local_step(x[0], loss_target[0], _prepare_weights(full, rep), late_weights, push_grads)

    parts = {}
    for names, send_sems, recv_sems, slabs, landings in pushes:
        got = _slab_push_wait(send_sems, recv_sems, slabs, landings, (grad_x,), "grad_push_wait_" + "_".join(names))
        for n, slab, land in zip(names, slabs, got):
            parts[n] = jnp.where(slot == me, slab, land)

    small_sharded = ['rwkv_proj', 'gdn_proj'] + SMALL_SHARDED
    small_names = small_sharded + REPLICATED
    rep_vec = jnp.concatenate([grads[n].reshape(-1) for n in REPLICATED] + [loss_vec[0, 0:1]])
    slab_small = jnp.concatenate([_shard_major(grads[n], SHARD_AXIS[n]) for n in small_sharded] +
                                 [jnp.broadcast_to(rep_vec[None], (N_DEV, rep_vec.shape[0]))], axis=1)
    small_rows = -(-slab_small.shape[1] // (PACK_W * SUBLANES)) * SUBLANES
    slab_small = jnp.pad(slab_small, ((0, 0), (0, small_rows * PACK_W - slab_small.shape[1]))).reshape(N_DEV, small_rows, PACK_W)
    core = lax.axis_index("c").astype(jnp.int32).reshape(1)
    (from_sibling,) = _pair_exchange([slab_small], "grad_pair_exchange")
    chip_small = _pair_sum(slab_small, from_sibling, core, F32, "grad_pair_sum_small")
    (parts_small,) = _chip_exchange([chip_small], "grad_chip_exchange")

    def pack_local(src):
        flat = jnp.concatenate([strip(n, src[n]).reshape(-1) for n in small_names])
        return jnp.pad(flat, (0, small_rows * PACK_W - flat.shape[0])).reshape(small_rows, PACK_W)

    results = [({}, None) for _ in range(4)]
    for n in ROW_SHARDED:
        packs = _reduce_adamw(parts[n], local[n], strip(n, mom1[n]), strip(n, mom2[n]), "adamw_" + n)
        for (out, _), pk in zip(results, packs):
            out[n] = (pk.T if n in TRANSPOSED else pk).reshape(given[n].shape)
    packs = _reduce_adamw(parts_small, pack_local(given), pack_local(mom1), pack_local(mom2), "adamw_small")
    for i, pk in enumerate(packs):
        flat, off = pk.reshape(-1), 0
        for n in small_names:
            size = int(np.prod(given[n].shape))
            results[i][0][n] = flat[off:off + size].reshape(given[n].shape)
            off += size
        results[i] = (results[i][0], flat[off])
    (g_out, loss), (d_out, _), (m_out, _), (v_out, _) = results
    return (loss, grad_x[None], *[g_out[n] for n in WEIGHT_NAMES], *[d_out[n] for n in WEIGHT_NAMES],
            *[m_out[n] for n in WEIGHT_NAMES], *[v_out[n] for n in WEIGHT_NAMES])
```

```python
import functools

import jax
import jax.numpy as jnp
import numpy as np
from jax import lax
from jax.experimental import pallas as pl
from jax.experimental.pallas import tpu as pltpu

F32 = jnp.float32
BF16 = jnp.bfloat16
HI = lax.Precision.HIGHEST

N_DEV = 8
D_MODEL = 1024
CHUNK = 64
RWKV_HEADS = 8
RWKV_HEAD_DIM = 64
RWKV_WIDTH = 512
GDN_HEADS = 4
GDN_HEAD_DIM = 128
GDN_WIDTH = 512
GDN_CONV = 4
FFN_HIDDEN = 2816
FFN_CONV = 3
NORM_EPS = 1e-6
L2_EPS = 1e-6
RWKV_GN_EPS = 64e-5
LANES = 128
SUBLANES = 8
VMEM_LIMIT = 56 * 1024 * 1024

ADAM_LR = 0.001
ADAM_B1 = 0.9
ADAM_B2 = 0.999
ADAM_EPS = 1e-08
ADAM_WD = 0.01
ADAM_STEP = 10

OFF_GA, OFF_GB, OFF_RKV, OFF_QKV, OFF_Z, OFF_LO, OFF_AB, CAT_W = 0, 1024, 2048, 5120, 6656, 7168, 7424, 7680
RWKV_HB = 8
GDN_HB = 4

WEIGHT_NAMES = ['norm1_g', 'w_in', 'rwkv_mu', 'rwkv_w0', 'rwkv_w2', 'rwkv_a0', 'rwkv_a2', 'rwkv_g2', 'rwkv_k_k', 'rwkv_k_a',
                'rwkv_r_k', 'rwkv_ln_w', 'rwkv_ln_b', 'rwkv_proj', 'gdn_conv_w', 'gdn_a_log', 'gdn_dt_bias', 'gdn_norm_w',
                'gdn_proj', 'w_out', 'norm2_g', 'ffn_up', 'ffn_conv_w', 'ffn_down', 'final_g']
BIG_SHARDED = ['w_in', 'ffn_up', 'ffn_down', 'w_out', 'rwkv_proj', 'gdn_proj']
SMALL_SHARDED = ['rwkv_w2', 'rwkv_a2', 'rwkv_g2', 'gdn_conv_w', 'ffn_conv_w']
TRANSPOSED = ('w_in', 'ffn_up')
SHARD_AXIS = {'w_in': 0, 'ffn_up': 0, 'ffn_down': 0, 'w_out': 0, 'rwkv_proj': 1, 'gdn_proj': 1,
              'rwkv_w2': 1, 'rwkv_a2': 1, 'rwkv_g2': 1, 'gdn_conv_w': 1, 'ffn_conv_w': 1}
REPLICATED = [n for n in WEIGHT_NAMES if n not in SHARD_AXIS]
ROW_SHARDED = ['w_in', 'ffn_up', 'ffn_down', 'w_out']


def _cparams(sem=None):
    kw = dict(vmem_limit_bytes=VMEM_LIMIT)
    if sem is not None:
        kw['dimension_semantics'] = sem
    return pltpu.CompilerParams(**kw)


_NN, _NT, _TN = 'nn', 'nt', 'tn'
_DIMS_2D = {'nn': (((1,), (0,)), ((), ())), 'nt': (((1,), (1,)), ((), ())), 'tn': (((0,), (0,)), ((), ()))}
_DIMS_3D = {'nn': (((2,), (1,)), ((0,), (0,))), 'nt': (((2,), (2,)), ((0,), (0,))), 'tn': (((1,), (1,)), ((0,), (0,)))}


def _dg(a, b, kind):
    return lax.dot_general(a, b, (_DIMS_2D if a.ndim == 2 else _DIMS_3D)[kind], preferred_element_type=F32)


def _dot1(a, b, kind):
    return _dg(a.astype(BF16), b.astype(BF16), kind)


@jax.custom_vjp
def _dhi(a, b):
    return _dot1(a, b, _NN)


_dhi.defvjp(lambda a, b: (_dot1(a, b, _NN), (a, b)),
            lambda res, ct: (_dot1(ct, res[1], _NT), _dot1(res[0], ct, _TN)))


@jax.custom_vjp
def _dnt(a, b):
    return _dot1(a, b, _NT)


_dnt.defvjp(lambda a, b: (_dot1(a, b, _NT), (a, b)),
            lambda res, ct: (_dot1(ct, res[1], _NN), _dot1(ct, res[0], _TN)))


@jax.custom_vjp
def _dtn(a, b):
    return _dot1(a, b, _TN)


_dtn.defvjp(lambda a, b: (_dot1(a, b, _TN), (a, b)),
            lambda res, ct: (_dot1(res[1], ct, _NT), _dot1(res[0], ct, _NN)))


def _split3(x):
    x1 = x.astype(BF16)
    r1 = x - x1.astype(F32)
    x2 = r1.astype(BF16)
    return x1, x2, (r1 - x2.astype(F32)).astype(BF16)


def _dot_exact_lhs(sel, x, kind):
    parts = [_dg(sel, xi, kind) for xi in _split3(x)]
    return parts[0] + parts[1] + parts[2]


def _tril_ones(like):
    c = like.shape[-2]
    ri, ci = _iotas(c)
    return jnp.broadcast_to((ri >= ci).astype(BF16), like.shape[:-2] + (c, c))


@jax.custom_vjp
def _cumsum_rows(x):
    return _dot_exact_lhs(_tril_ones(x), x, _NN)


_cumsum_rows.defvjp(lambda x: (_dot_exact_lhs(_tril_ones(x), x, _NN), None),
                    lambda _, ct: (_dot_exact_lhs(_tril_ones(ct), ct, _TN),))


@jax.custom_vjp
def _lane_sum_as_row(x):
    return _dot_exact_lhs(jnp.ones(x.shape, BF16), x, _NT)


def _lane_sum_as_row_bwd(_, ct):
    ones = jnp.ones(ct.shape[:-1] + (LANES,), BF16)
    parts = [_dg(ci, ones, _TN) for ci in _split3(ct)]
    return (parts[0] + parts[1] + parts[2],)


_lane_sum_as_row.defvjp(lambda x: (_dot_exact_lhs(jnp.ones(x.shape, BF16), x, _NT), None), _lane_sum_as_row_bwd)


def _shift_rows(x, halo, s):
    rows = lax.broadcasted_iota(jnp.int32, x.shape, 0)
    out = pltpu.roll(x, s, 0)
    for i in range(s):
        out = jnp.where(rows == i, halo[SUBLANES - s + i:SUBLANES - s + i + 1, :], out)
    return out


def _unshift_rows(g, carry, s):
    c = g.shape[0]
    rows = lax.broadcasted_iota(jnp.int32, g.shape, 0)
    out = pltpu.roll(g, c - s, 0)
    for i in range(s):
        out = jnp.where(rows == c - s + i, carry[i:i + 1, :], out)
    return out


def _sigmoid(z):
    return 1.0 / (1.0 + jnp.exp(-z))


def _silu(z):
    return z * _sigmoid(z)


def _softplus(z):
    return jnp.maximum(z, 0.0) + jnp.log(1.0 + jnp.exp(-jnp.abs(z)))


def _rms(t, gain):
    return t * lax.rsqrt(jnp.mean(t * t, axis=-1, keepdims=True) + NORM_EPS) * gain


def _iotas(c):
    return lax.broadcasted_iota(jnp.int32, (c, c), 0), lax.broadcasted_iota(jnp.int32, (c, c), 1)


def _unit_lower_inverse(xm, eye):
    t = eye + xm
    p = xm
    for _ in range(5):
        p = _dhi(p, p)
        t = t + _dhi(t, p)
    return t


def _rwkv_head(pr, pk, pv, plo, qr, qk, qv, qlo, s0, pp, mulo, wl):
    c = pr.shape[1]
    ri, ci = _iotas(c)

    def mix(p, q, mu):
        return p + (q - p) * mu

    r = mix(pr, qr, pp[:, 0:1])
    k = mix(pk, qk, pp[:, 1:2])
    v = mix(pv, qv, pp[:, 2:3])
    lo = mix(plo, qlo, mulo)
    w0, a0, k_k, k_a, ln_w, ln_b, r_k = (pp[:, i:i + 1] for i in range(3, 10))
    per_head = lambda t: jnp.broadcast_to(t, (pr.shape[0],) + t.shape)
    zw = _dhi(per_head(jnp.tanh(lo)), wl[:, 0])
    za = _dhi(per_head(lo), wl[:, 1])
    g = _dhi(per_head(_sigmoid(lo)), wl[:, 2])
    w_log = -_softplus(-(w0 + zw)) - 0.5
    lw = -jnp.exp(w_log)
    a = _sigmoid(a0 + za)
    kk = k * k_k
    kk = kk * lax.rsqrt(jnp.sum(kk * kk, axis=-1, keepdims=True) + L2_EPS)
    k2 = k * (1.0 + (a - 1.0) * k_a)
    an = -kk
    b = kk * a
    causal = ri >= ci
    strict = ri > ci
    eye = (ri == ci).astype(F32)
    cl = _cumsum_rows(lw)
    ecl = jnp.exp(-cl)
    at = an * jnp.exp(cl - lw)
    bt = b * ecl
    kt = k2 * ecl
    rt = r * jnp.exp(cl)
    a_ab = jnp.where(strict, _dnt(at, bt), 0.0)
    a_ak = jnp.where(strict, _dnt(at, kt), 0.0)
    tinv = _unit_lower_inverse(a_ab, eye)
    u = _dhi(tinv, _dnt(at, s0) + _dhi(a_ak, v))
    y = _dnt(rt, s0) + _dhi(jnp.where(causal, _dnt(rt, bt), 0.0), u) + _dhi(jnp.where(causal, _dnt(rt, kt), 0.0), v)
    cl_end = jnp.sum(lw, axis=1, keepdims=True)
    dec_end = jnp.exp(cl_end - cl)
    s1 = s0 * jnp.exp(cl_end) + _dtn(u, b * dec_end) + _dtn(v, k2 * dec_end)
    m = (lax.broadcasted_iota(jnp.int32, (1, LANES), 1) < RWKV_HEAD_DIM).astype(F32)
    mean = jnp.sum(y, axis=-1, keepdims=True) * (1.0 / RWKV_HEAD_DIM)
    yc = (y - mean) * m
    var = jnp.sum(yc * yc, axis=-1, keepdims=True) * (1.0 / RWKV_HEAD_DIM)
    yn = yc * lax.rsqrt(var + RWKV_GN_EPS) * ln_w + ln_b
    y2 = yn + jnp.sum(r * k2 * r_k, axis=-1, keepdims=True) * v
    return y2 * g, s1


def _gdn_head(xq, xk, xv, z, ab, s0, cw, gp, oha, ohb):
    c = z.shape[1]
    ri, ci = _iotas(c)

    def conv(xs, w):
        out = xs[0] * w[:, GDN_CONV - 1:GDN_CONV]
        for s in range(1, GDN_CONV):
            out = out + xs[s] * w[:, GDN_CONV - 1 - s:GDN_CONV - s]
        return out

    q = _silu(conv(xq, cw[:, 0]))
    k = _silu(conv(xk, cw[:, 1]))
    v = _silu(conv(xv, cw[:, 2]))
    q = q * lax.rsqrt(jnp.sum(q * q, axis=-1, keepdims=True) + L2_EPS) * (GDN_HEAD_DIM ** -0.5)
    k = k * lax.rsqrt(jnp.sum(k * k, axis=-1, keepdims=True) + L2_EPS)
    gg = -jnp.exp(gp[0:1]) * _softplus(ab + gp[1:2])
    beta = jnp.sum(_sigmoid(ab) * ohb, axis=-1, keepdims=True)
    causal = ri >= ci
    strict = ri > ci
    eye = (ri == ci).astype(F32)
    gcm = _cumsum_rows(gg * oha)
    gc = jnp.sum(gcm, axis=-1, keepdims=True)
    gc_row = _lane_sum_as_row(gcm)
    dec = jnp.where(causal, jnp.exp(jnp.where(causal, gc - gc_row, 0.0)), 0.0)
    kb = k * beta
    vb = v * beta
    lm = jnp.where(strict, _dnt(kb, k) * dec, 0.0)
    tinv = _unit_lower_inverse(-lm, eye)
    egc = jnp.exp(gc)
    u = _dhi(tinv, vb)
    wk = _dhi(tinv, kb * egc)
    attn = jnp.where(causal, _dnt(q, k) * dec, 0.0)
    g_last = gc[:, c - 1:c, :]
    v_new = u - _dhi(wk, s0)
    o = _dhi(q * egc, s0) + _dhi(attn, v_new)
    s1 = s0 * jnp.exp(g_last) + _dtn(k * jnp.exp(g_last - gc), v_new)
    return _rms(o, gp[2:3]) * _silu(z), s1


def _head_id(grp, i, per_step, heads):
    return i if per_step == heads else grp * per_step + i


def _head_range(grp, per_step, heads):
    return slice(None) if per_step == heads else pl.ds(grp * per_step, per_step)


def _stack_heads(ref, n, fn=None):
    parts = []
    for i in range(n):
        sl = slice(i * LANES, (i + 1) * LANES)
        v = ref[:, sl]
        parts.append(v if fn is None else fn(v, sl))
    return jnp.stack(parts)


def _prev_rows_spec(width, col):
    per = CHUNK // SUBLANES
    return pl.BlockSpec((SUBLANES, width), lambda n, h: (jnp.maximum(n * per - 1, 0), col(h)))


def _rwkv_specs(nmap):
    hb, groups = RWKV_HB, RWKV_HEADS // RWKV_HB
    cb = OFF_RKV // (hb * LANES)
    specs = []
    for j in range(3):
        specs.append(pl.BlockSpec((CHUNK, hb * LANES), lambda n, g, j=j: (nmap(n), cb + j * groups + g)))
    specs.append(pl.BlockSpec((CHUNK, 2 * LANES), lambda n, g: (nmap(n), OFF_LO // (2 * LANES))))
    per = CHUNK // SUBLANES
    for j in range(3):
        specs.append(pl.BlockSpec((SUBLANES, hb * LANES),
                                  lambda n, g, j=j: (jnp.maximum(nmap(n) * per - 1, 0), cb + j * groups + g)))
    specs.append(pl.BlockSpec((SUBLANES, 2 * LANES), lambda n, g: (jnp.maximum(nmap(n) * per - 1, 0), OFF_LO // (2 * LANES))))
    specs.append(pl.BlockSpec((hb, 16, LANES), lambda n, g: (g, 0, 0)))
    specs.append(pl.BlockSpec((1, 2 * LANES), lambda n, g: (0, 0)))
    specs.append(pl.BlockSpec((hb, 3, 2 * LANES, LANES), lambda n, g: (g, 0, 0, 0)))
    return specs


def _rwkv_forward(p_cat, ppack, mulo, wl):
    t = p_cat.shape[0]
    n_chunks = t // CHUNK

    def body(pr, pk, pv, plo, hr, hk, hv, hlo, pp, ml, w, out, st_out, s_scr):
        n, grp = pl.program_id(0), pl.program_id(1)

        hsl = _head_range(grp, RWKV_HB, RWKV_HEADS)

        @pl.when(n == 0)
        def _():
            s_scr[hsl] = jnp.zeros((RWKV_HB, LANES, LANES), F32)

        live = (n > 0).astype(F32)
        lo = plo[...]
        lo_prev = _shift_rows(lo, hlo[...] * live, 1)
        cur = [_stack_heads(x, RWKV_HB) for x in (pr, pk, pv)]
        prev = [_stack_heads(x, RWKV_HB, lambda v, sl, hx=hx: _shift_rows(v, hx[:, sl] * live, 1))
                for x, hx in ((pr, hr), (pk, hk), (pv, hv))]
        s0 = s_scr[hsl]
        st_out[...] = s0
        o, s1 = _rwkv_head(*cur, lo, *prev, lo_prev, s0, pp[...], ml[...], w[...])
        for i in range(RWKV_HB):
            out[:, i * LANES:(i + 1) * LANES] = o[i].astype(out.dtype)
        s_scr[hsl] = s1

    return pl.pallas_call(
        body, name="rwkv_fwd", grid=(n_chunks, RWKV_HEADS // RWKV_HB),
        in_specs=_rwkv_specs(lambda n: n),
        out_specs=(pl.BlockSpec((CHUNK, RWKV_HB * LANES), lambda n, g: (n, g)),
                   pl.BlockSpec((None, RWKV_HB, LANES, LANES), lambda n, g: (n, g, 0, 0))),
        out_shape=(jax.ShapeDtypeStruct((t, RWKV_HEADS * LANES), BF16),
                   jax.ShapeDtypeStruct((n_chunks, RWKV_HEADS, LANES, LANES), F32)),
        scratch_shapes=[pltpu.VMEM((RWKV_HEADS, LANES, LANES), F32)],
        compiler_params=_cparams(("arbitrary", "arbitrary")),
    )(p_cat, p_cat, p_cat, p_cat, p_cat, p_cat, p_cat, p_cat, ppack, mulo, wl)


def _rwkv_backward(p_cat, ppack, mulo, wl, states, d_out):
    t = p_cat.shape[0]
    n_chunks = t // CHUNK
    last = n_chunks - 1

    def body(pr, pk, pv, plo, hr, hk, hv, hlo, pp, ml, w, st, dy, dpr, dpk, dpv, dplo, dpp, dml, dw, ds_scr, car_scr, carlo_scr):
        n, grp = pl.program_id(0), pl.program_id(1)

        hsl = _head_range(grp, RWKV_HB, RWKV_HEADS)
        gi = _head_id(grp, 0, 1, RWKV_HEADS // RWKV_HB)

        @pl.when(n == 0)
        def _():
            ds_scr[hsl] = jnp.zeros((RWKV_HB, LANES, LANES), F32)
            car_scr[hsl] = jnp.zeros((RWKV_HB, 3 * SUBLANES, LANES), F32)
            carlo_scr[gi] = jnp.zeros((SUBLANES, 2 * LANES), F32)

        @pl.when((n == 0) & (grp == 0))
        def _():
            dpp[...] = jnp.zeros(dpp.shape, F32)
            dml[...] = jnp.zeros(dml.shape, F32)
            dw[...] = jnp.zeros(dw.shape, F32)

        live = (n < last).astype(F32)
        lo = plo[...]
        lo_prev = _shift_rows(lo, hlo[...] * live, 1)
        cur = [_stack_heads(x, RWKV_HB) for x in (pr, pk, pv)]
        prev = [_stack_heads(x, RWKV_HB, lambda v, sl, hx=hx: _shift_rows(v, hx[:, sl] * live, 1))
                for x, hx in ((pr, hr), (pk, hk), (pv, hv))]
        _, vjp = jax.vjp(_rwkv_head, *cur, lo, *prev, lo_prev, st[...], pp[...], ml[...], w[...])
        g = vjp((_stack_heads(dy, RWKV_HB), ds_scr[hsl]))
        outs = (dpr, dpk, dpv)
        for i in range(RWKV_HB):
            sl = slice(i * LANES, (i + 1) * LANES)
            h = _head_id(grp, i, RWKV_HB, RWKV_HEADS)
            car = car_scr[h]
            for j in range(3):
                tot = g[j][i] + _unshift_rows(g[4 + j][i], car[SUBLANES * j:SUBLANES * (j + 1), :], 1)
                outs[j][:, sl] = tot.astype(outs[j].dtype)
                car_scr[h, SUBLANES * j:SUBLANES * (j + 1), :] = g[4 + j][i][0:SUBLANES, :]
        dlo = g[3] + _unshift_rows(g[7], carlo_scr[gi], 1)
        carlo_scr[gi] = g[7][0:SUBLANES, :]
        ds_scr[hsl] = g[8]
        dpp[hsl] += g[9]
        dml[0, 0:1, :] += g[10]
        dw[hsl] += g[11]

        @pl.when(grp == 0)
        def _():
            dplo[...] = dlo

        @pl.when(grp > 0)
        def _():
            dplo[...] += dlo

    rev = lambda n: last - n
    in_specs = _rwkv_specs(rev) + [
        pl.BlockSpec((None, RWKV_HB, LANES, LANES), lambda n, g: (rev(n), g, 0, 0)),
        pl.BlockSpec((CHUNK, RWKV_HB * LANES), lambda n, g: (rev(n), g)),
    ]
    hw = RWKV_HEADS * LANES
    return pl.pallas_call(
        body, name="rwkv_bwd", grid=(n_chunks, RWKV_HEADS // RWKV_HB),
        in_specs=in_specs,
        out_specs=(pl.BlockSpec((CHUNK, RWKV_HB * LANES), lambda n, g: (rev(n), g)),
                   pl.BlockSpec((CHUNK, RWKV_HB * LANES), lambda n, g: (rev(n), g)),
                   pl.BlockSpec((CHUNK, RWKV_HB * LANES), lambda n, g: (rev(n), g)),
                   pl.BlockSpec((CHUNK, 2 * LANES), lambda n, h: (rev(n), 0)),
                   pl.BlockSpec((RWKV_HEADS, 16, LANES), lambda n, h: (0, 0, 0)),
                   pl.BlockSpec((RWKV_HEADS, SUBLANES, 2 * LANES), lambda n, h: (0, 0, 0)),
                   pl.BlockSpec((RWKV_HEADS, 3, 2 * LANES, LANES), lambda n, h: (0, 0, 0, 0))),
        out_shape=(jax.ShapeDtypeStruct((t, hw), BF16), jax.ShapeDtypeStruct((t, hw), BF16), jax.ShapeDtypeStruct((t, hw), BF16),
                   jax.ShapeDtypeStruct((t, 2 * LANES), F32),
                   jax.ShapeDtypeStruct((RWKV_HEADS, 16, LANES), F32),
                   jax.ShapeDtypeStruct((RWKV_HEADS, SUBLANES, 2 * LANES), F32),
                   jax.ShapeDtypeStruct((RWKV_HEADS, 3, 2 * LANES, LANES), F32)),
        scratch_shapes=[pltpu.VMEM((RWKV_HEADS, LANES, LANES), F32),
                        pltpu.VMEM((RWKV_HEADS, 3 * SUBLANES, LANES), F32),
                        pltpu.VMEM((RWKV_HEADS, SUBLANES, 2 * LANES), F32)],
        compiler_params=_cparams(("arbitrary", "arbitrary")),
    )(p_cat, p_cat, p_cat, p_cat, p_cat, p_cat, p_cat, p_cat, ppack, mulo, wl, states, d_out)


def _gdn_specs(nmap):
    cb = OFF_QKV // LANES
    per = CHUNK // SUBLANES
    hb, groups = GDN_HB, GDN_HEADS // GDN_HB
    cb = OFF_QKV // (hb * LANES)
    specs = []
    for j in range(3):
        specs.append(pl.BlockSpec((CHUNK, hb * LANES), lambda n, g, j=j: (nmap(n), cb + j * groups + g)))
    for j in range(3):
        specs.append(pl.BlockSpec((SUBLANES, hb * LANES),
                                  lambda n, g, j=j: (jnp.maximum(nmap(n) * per - 1, 0), cb + j * groups + g)))
    specs.append(pl.BlockSpec((CHUNK, hb * LANES), lambda n, g: (nmap(n), OFF_Z // (hb * LANES) + g)))
    specs.append(pl.BlockSpec((CHUNK, LANES), lambda n, g: (nmap(n), OFF_AB // LANES)))
    specs.append(pl.BlockSpec((hb, 3, SUBLANES, LANES), lambda n, g: (g, 0, 0, 0)))
    specs.append(pl.BlockSpec((SUBLANES, LANES), lambda n, g: (0, 0)))
    return specs


def _conv_taps(x, halo):
    return (x,) + tuple(_shift_rows(x, halo, s) for s in range(1, GDN_CONV))


def _onehots(grp):
    lane = lax.broadcasted_iota(jnp.int32, (GDN_HB, 1, LANES), 2)
    head = lax.broadcasted_iota(jnp.int32, (GDN_HB, 1, LANES), 0) + _head_id(grp, 0, GDN_HB, GDN_HEADS)
    return (lane == head).astype(F32), (lane == GDN_HEADS + head).astype(F32)


def _gdn_taps(refs, halos, live):
    out = []
    for x, hx in zip(refs, halos):
        per_head = [_conv_taps(x[:, i * LANES:(i + 1) * LANES], hx[:, i * LANES:(i + 1) * LANES] * live) for i in range(GDN_HB)]
        out.append(tuple(jnp.stack([per_head[i][s] for i in range(GDN_HB)]) for s in range(GDN_CONV)))
    return out


def _gdn_forward(p_cat, cwpack, gpar):
    t = p_cat.shape[0]
    n_chunks = t // CHUNK

    def body(xq, xk, xv, hq, hk, hv, z, ab, cw, gp, out, st_out, s_scr):
        n, grp = pl.program_id(0), pl.program_id(1)

        hsl = _head_range(grp, GDN_HB, GDN_HEADS)

        @pl.when(n == 0)
        def _():
            s_scr[hsl] = jnp.zeros((GDN_HB, LANES, LANES), F32)

        live = (n > 0).astype(F32)
        oha, ohb = _onehots(grp)
        s0 = s_scr[hsl]
        st_out[...] = s0
        taps = _gdn_taps((xq, xk, xv), (hq, hk, hv), live)
        o, s1 = _gdn_head(*taps, _stack_heads(z, GDN_HB), ab[...], s0, cw[...], gp[...], oha, ohb)
        for i in range(GDN_HB):
            out[:, i * LANES:(i + 1) * LANES] = o[i].astype(out.dtype)
        s_scr[hsl] = s1

    return pl.pallas_call(
        body, name="gdn_fwd", grid=(n_chunks, GDN_HEADS // GDN_HB),
        in_specs=_gdn_specs(lambda n: n),
        out_specs=(pl.BlockSpec((CHUNK, GDN_HB * LANES), lambda n, g: (n, g)),
                   pl.BlockSpec((None, GDN_HB, LANES, LANES), lambda n, g: (n, g, 0, 0))),
        out_shape=(jax.ShapeDtypeStruct((t, GDN_WIDTH), BF16),
                   jax.ShapeDtypeStruct((n_chunks, GDN_HEADS, LANES, LANES), F32)),
        scratch_shapes=[pltpu.VMEM((GDN_HEADS, LANES, LANES), F32)],
        compiler_params=_cparams(("arbitrary", "arbitrary")),
    )(p_cat, p_cat, p_cat, p_cat, p_cat, p_cat, p_cat, p_cat, cwpack, gpar)


def _gdn_backward(p_cat, cwpack, gpar, states, d_out):
    t = p_cat.shape[0]
    n_chunks = t // CHUNK
    last = n_chunks - 1

    def body(xq, xk, xv, hq, hk, hv, z, ab, cw, gp, st, dy, dq, dk, dv, dz, dab, dcw, dgp, ds_scr, car_scr):
        n, grp = pl.program_id(0), pl.program_id(1)

        hsl = _head_range(grp, GDN_HB, GDN_HEADS)

        @pl.when(n == 0)
        def _():
            ds_scr[hsl] = jnp.zeros((GDN_HB, LANES, LANES), F32)
            car_scr[hsl] = jnp.zeros((GDN_HB, 3 * GDN_CONV, SUBLANES, LANES), F32)

        @pl.when((n == 0) & (grp == 0))
        def _():
            dcw[...] = jnp.zeros(dcw.shape, F32)
            dgp[...] = jnp.zeros(dgp.shape, F32)

        live = (n < last).astype(F32)
        oha, ohb = _onehots(grp)
        fn = functools.partial(_gdn_head, oha=oha, ohb=ohb)
        taps = _gdn_taps((xq, xk, xv), (hq, hk, hv), live)
        _, vjp = jax.vjp(fn, *taps, _stack_heads(z, GDN_HB), ab[...], st[...], cw[...], gp[...])
        g = vjp((_stack_heads(dy, GDN_HB), ds_scr[hsl]))
        outs = (dq, dk, dv)
        for i in range(GDN_HB):
            sl = slice(i * LANES, (i + 1) * LANES)
            h = _head_id(grp, i, GDN_HB, GDN_HEADS)
            for j in range(3):
                tot = g[j][0][i]
                for s in range(1, GDN_CONV):
                    slot = j * GDN_CONV + s
                    tot = tot + _unshift_rows(g[j][s][i], car_scr[h, slot], s)
                    car_scr[h, slot] = g[j][s][i][0:SUBLANES, :]
                outs[j][:, sl] = tot.astype(outs[j].dtype)
            dz[:, sl] = g[3][i].astype(dz.dtype)
        dab_sum = g[4]
        ds_scr[hsl] = g[5]
        dcw[hsl] += g[6]
        dgp[0] += g[7]

        @pl.when(grp == 0)
        def _():
            dab[...] = dab_sum

        @pl.when(grp > 0)
        def _():
            dab[...] += dab_sum

    rev = lambda n: last - n
    in_specs = _gdn_specs(rev) + [
        pl.BlockSpec((None, GDN_HB, LANES, LANES), lambda n, g: (rev(n), g, 0, 0)),
        pl.BlockSpec((CHUNK, GDN_HB * LANES), lambda n, g: (rev(n), g)),
    ]
    blk = pl.BlockSpec((CHUNK, GDN_HB * LANES), lambda n, g: (rev(n), g))
    return pl.pallas_call(
        body, name="gdn_bwd", grid=(n_chunks, GDN_HEADS // GDN_HB),
        in_specs=in_specs,
        out_specs=(blk, blk, blk, blk,
                   pl.BlockSpec((CHUNK, LANES), lambda n, h: (rev(n), 0)),
                   pl.BlockSpec((GDN_HEADS, 3, SUBLANES, LANES), lambda n, h: (0, 0, 0, 0)),
                   pl.BlockSpec((GDN_HEADS, SUBLANES, LANES), lambda n, h: (0, 0, 0))),
        out_shape=(jax.ShapeDtypeStruct((t, GDN_WIDTH), BF16), jax.ShapeDtypeStruct((t, GDN_WIDTH), BF16),
                   jax.ShapeDtypeStruct((t, GDN_WIDTH), BF16), jax.ShapeDtypeStruct((t, GDN_WIDTH), BF16),
                   jax.ShapeDtypeStruct((t, LANES), F32),
                   jax.ShapeDtypeStruct((GDN_HEADS, 3, SUBLANES, LANES), F32),
                   jax.ShapeDtypeStruct((GDN_HEADS, SUBLANES, LANES), F32)),
        scratch_shapes=[pltpu.VMEM((GDN_HEADS, LANES, LANES), F32),
                        pltpu.VMEM((GDN_HEADS, 3 * GDN_CONV, SUBLANES, LANES), F32)],
        compiler_params=_cparams(("arbitrary", "arbitrary")),
    )(p_cat, p_cat, p_cat, p_cat, p_cat, p_cat, p_cat, p_cat, cwpack, gpar, states, d_out)


def _pick(n, options):
    for o in options:
        if n % o == 0:
            return o
    raise ValueError(f"no tile for {n}")


MM_VMEM_BUDGET = 30 * 1024 * 1024
MM_MIN_STEPS = 8


def _mm_tiles(mode, m, n, k, out_bytes):
    tms = [t for t in (2048, 1024, 768, 512, 256, 128, 64) if m % t == 0 and (mode != 'tn' or t % LANES == 0)]
    tns = [t for t in (1408, 1024, 768, 512, 256, 128) if n % t == 0]
    tks = [t for t in (2048, 1920, 1408, 1024, 512, 256, 128, 64) if k % t == 0]
    best, best_key = None, None
    for tm in tms:
        for tn in tns:
            for tk in tks:
                nk = k // tk
                vmem = 2 * (tm * tk * 2 + tk * tn * 2 + tm * tn * out_bytes) + (tm * tn * 4 if nk > 1 else 0)
                steps = (m // tm) * (n // tn) * nk
                if vmem > MM_VMEM_BUDGET:
                    continue
                key = (steps >= MM_MIN_STEPS, tn if mode == 'tn' else 0, tm * tn * tk, -nk)
                if best_key is None or key > best_key:
                    best, best_key = (tm, tn, tk), key
    if best is None:
        raise ValueError(f"no matmul tile for {mode} {m}x{n}x{k}")
    return best


_MM_DIMS = {'nn': (((1,), (0,)), ((), ())), 'nt': (((1,), (1,)), ((), ())), 'tn': (((0,), (0,)), ((), ()))}


def _matmul(a, b, mode, out_dtype, name):
    if mode == 'nn':
        (m, k), (k2, n) = a.shape, b.shape
    elif mode == 'nt':
        (m, k), (n, k2) = a.shape, b.shape
    else:
        (k, m), (k2, n) = a.shape, b.shape
    assert k == k2, (a.shape, b.shape, mode)
    tm, tn, tk = _mm_tiles(mode, m, n, k, jnp.dtype(out_dtype).itemsize)
    nk = k // tk
    dims = _MM_DIMS[mode]

    def body(a_ref, b_ref, o_ref, acc_ref):
        kk = pl.program_id(2)
        part = lax.dot_general(a_ref[...], b_ref[...], dims, preferred_element_type=F32)
        if nk == 1:
            o_ref[...] = part.astype(o_ref.dtype)
            return

        @pl.when(kk == 0)
        def _():
            acc_ref[...] = part

        @pl.when((kk > 0) & (kk < nk - 1))
        def _():
            acc_ref[...] += part

        @pl.when(kk == nk - 1)
        def _():
            o_ref[...] = (acc_ref[...] + part).astype(o_ref.dtype)

    a_spec = pl.BlockSpec((tk, tm), lambda i, j, kk: (kk, i)) if mode == 'tn' else pl.BlockSpec((tm, tk), lambda i, j, kk: (i, kk))
    b_spec = pl.BlockSpec((tn, tk), lambda i, j, kk: (j, kk)) if mode == 'nt' else pl.BlockSpec((tk, tn), lambda i, j, kk: (kk, j))
    return pl.pallas_call(
        body, name=name, grid=(m // tm, n // tn, nk),
        in_specs=[a_spec, b_spec],
        out_specs=pl.BlockSpec((tm, tn), lambda i, j, kk: (i, j)),
        out_shape=jax.ShapeDtypeStruct((m, n), out_dtype),
        scratch_shapes=[pltpu.VMEM((tm, tn), F32)],
        compiler_params=_cparams(("parallel", "parallel", "arbitrary")),
    )(a, b)


ROW_TILE = 256


def _row_specs(rows, tm):
    return [pl.BlockSpec((tm, w), lambda i, ci=ci: (i, ci)) for (_, w, ci) in rows]


def _rw_forward(fn, rows, pars, outs, name):
    t = rows[0][0].shape[0]
    tm = min(ROW_TILE, t)
    nr, npar = len(rows), len(pars)

    def body(*refs):
        vals = [r[...].astype(F32) for r in refs[:nr]] + [p[...] for p in refs[nr:nr + npar]]
        res = fn(*vals)
        for o, v in zip(refs[nr + npar:], res):
            o[...] = v.astype(o.dtype)

    return pl.pallas_call(
        body, name=name, grid=(t // tm,),
        in_specs=_row_specs(rows, tm) + [pl.BlockSpec(p.shape, lambda i: (0, 0)) for p in pars],
        out_specs=tuple(pl.BlockSpec((tm, w), lambda i: (i, 0)) for (w, _) in outs),
        out_shape=tuple(jax.ShapeDtypeStruct((t, w), dt) for (w, dt) in outs),
        compiler_params=_cparams(("parallel",)),
    )(*[r[0] for r in rows], *pars)


def _rw_backward(fn, rows, pars, cots, drow_dtypes, name):
    t = rows[0][0].shape[0]
    tm = min(ROW_TILE, t)
    nr, npar, nc = len(rows), len(pars), len(cots)
    keep = [i for i, dt in enumerate(drow_dtypes) if dt is not None]

    def body(*refs):
        vals = [r[...].astype(F32) for r in refs[:nr]] + [p[...] for p in refs[nr:nr + npar]]
        cvals = tuple(c[...].astype(F32) for c in refs[nr + npar:nr + npar + nc])
        orefs = refs[nr + npar + nc:]
        _, vjp = jax.vjp(fn, *vals)
        g = vjp(cvals)
        for o, i in zip(orefs[:len(keep)], keep):
            o[...] = g[i].astype(o.dtype)
        first = pl.program_id(0) == 0
        for o, gi in zip(orefs[len(keep):], g[nr:]):
            @pl.when(first)
            def _(o=o, gi=gi):
                o[...] = gi

            @pl.when(jnp.logical_not(first))
            def _(o=o, gi=gi):
                o[...] += gi

    out_specs = [pl.BlockSpec((tm, rows[i][1]), lambda i_: (i_, 0)) for i in keep] + \
                [pl.BlockSpec(p.shape, lambda i_: (0, 0)) for p in pars]
    out_shape = [jax.ShapeDtypeStruct((t, rows[i][1]), drow_dtypes[i]) for i in keep] + \
                [jax.ShapeDtypeStruct(p.shape, F32) for p in pars]
    return pl.pallas_call(
        body, name=name, grid=(t // tm,),
        in_specs=_row_specs(rows, tm) + [pl.BlockSpec(p.shape, lambda i: (0, 0)) for p in pars] + _row_specs(cots, tm),
        out_specs=tuple(out_specs), out_shape=tuple(out_shape),
        compiler_params=_cparams(("arbitrary",)),
    )(*[r[0] for r in rows], *pars, *[c[0] for c in cots])


def _norm_fn(x, g):
    return (_rms(x, g),)


def _norm_skip_fn(x, g):
    return _rms(x, g), x


def _merge_fn(ga, gb, ya, yb):
    return (_sigmoid(ga) * ya + _sigmoid(gb) * yb,)


def _res_norm_fn(x, mo, g):
    x1 = x + mo
    return x1, _rms(x1, g)


def _loss_head(x1, fo, gf, target, name):
    t, d = x1.shape
    tm = min(ROW_TILE, t)

    def tile_loss(x2, g, tgt):
        err = _rms(x2, g) - tgt
        per_row = jnp.sum(err * err, axis=-1, keepdims=True) * (0.5 / d)
        return jnp.sum(per_row, axis=0, keepdims=True)

    def body(x1_ref, fo_ref, g_ref, t_ref, loss_ref, dx_ref, dxb_ref, dg_ref):
        x2 = x1_ref[...] + fo_ref[...]
        val, vjp = jax.vjp(functools.partial(tile_loss, tgt=t_ref[...]), x2, g_ref[...])
        dx2, dg = vjp(jnp.ones((1, 1), F32))
        dx_ref[...] = dx2
        dxb_ref[...] = dx2.astype(BF16)
        first = pl.program_id(0) == 0

        @pl.when(first)
        def _():
            loss_ref[...] = jnp.broadcast_to(val, loss_ref.shape)
            dg_ref[...] = dg

        @pl.when(jnp.logical_not(first))
        def _():
            loss_ref[...] += jnp.broadcast_to(val, loss_ref.shape)
            dg_ref[...] += dg

    row = pl.BlockSpec((tm, d), lambda i: (i, 0))
    vec = pl.BlockSpec((1, d), lambda i: (0, 0))
    return pl.pallas_call(
        body, name=name, grid=(t // tm,),
        in_specs=[row, row, vec, row],
        out_specs=(pl.BlockSpec((1, LANES), lambda i: (0, 0)), row, row, vec),
        out_shape=(jax.ShapeDtypeStruct((1, LANES), F32), jax.ShapeDtypeStruct((t, d), F32),
                   jax.ShapeDtypeStruct((t, d), BF16), jax.ShapeDtypeStruct((1, d), F32)),
        compiler_params=_cparams(("arbitrary",)),
    )(x1, fo, gf, target)


FFN_TILE_ROWS = 512
FFN_TILE_COLS = 256
FFN_COL_BLOCKS = FFN_HIDDEN // FFN_TILE_COLS


def _conv3_past(x, halo, w):
    rows = lax.broadcasted_iota(jnp.int32, x.shape, 0)
    x1 = jnp.where(rows == 0, halo[7:8, :], pltpu.roll(x, 1, 0))
    x2 = jnp.where(rows == 0, halo[6:7, :], jnp.where(rows == 1, halo[7:8, :], pltpu.roll(x, 2, 0)))
    return x * w[2:3] + x1 * w[1:2] + x2 * w[0:1], x1, x2


def _ffn_in_specs(tm, imap, jmap):
    per = tm // SUBLANES
    tile = lambda off: pl.BlockSpec((tm, FFN_TILE_COLS), lambda *g: (imap(*g), off + jmap(*g) % FFN_COL_BLOCKS))
    halo = lambda off: pl.BlockSpec((SUBLANES, FFN_TILE_COLS),
                                    lambda *g: (jnp.maximum(imap(*g) * per - 1, 0), off + jmap(*g) % FFN_COL_BLOCKS))
    wsp = lambda off: pl.BlockSpec((FFN_CONV, FFN_TILE_COLS), lambda *g: (0, off + jmap(*g) % FFN_COL_BLOCKS))
    return [tile(0), halo(0), wsp(0), tile(FFN_COL_BLOCKS), halo(FFN_COL_BLOCKS), wsp(FFN_COL_BLOCKS)]


def _ffn_act_forward(hpre, cw):
    t = hpre.shape[0]
    tm = min(FFN_TILE_ROWS, t)

    def body(hg, pg, wg, hu, pu, wu, out):
        live = (pl.program_id(0) > 0).astype(F32)
        cg, _, _ = _conv3_past(hg[...], pg[...] * live, wg[...])
        cu, _, _ = _conv3_past(hu[...], pu[...] * live, wu[...])
        out[...] = (_silu(cg) * cu).astype(out.dtype)

    return pl.pallas_call(
        body, name="ffn_act_fwd", grid=(t // tm, FFN_COL_BLOCKS),
        in_specs=_ffn_in_specs(tm, lambda i, j: i, lambda i, j: j),
        out_specs=pl.BlockSpec((tm, FFN_TILE_COLS), lambda i, j: (i, j)),
        out_shape=jax.ShapeDtypeStruct((t, FFN_HIDDEN), BF16),
        compiler_params=_cparams(("parallel", "parallel")),
    )(hpre, hpre, cw, hpre, hpre, cw)


def _conv3_future(d, nxt, w):
    tm = d.shape[0]
    rows = lax.broadcasted_iota(jnp.int32, d.shape, 0)
    d1 = jnp.where(rows == tm - 1, nxt[0:1, :], pltpu.roll(d, tm - 1, 0))
    d2 = jnp.where(rows == tm - 1, nxt[1:2, :], jnp.where(rows == tm - 2, nxt[0:1, :], pltpu.roll(d, tm - 2, 0)))
    return d * w[2:3] + d1 * w[1:2] + d2 * w[0:1]


def _ffn_backward(hpre, cw, dact):
    t = hpre.shape[0]
    tm = min(FFN_TILE_ROWS, t)
    n_tiles = t // tm
    per = tm // SUBLANES

    def d_conv_out(cg, cu, d):
        s = _sigmoid(cg)
        return d * cu * s * (1.0 + cg * (1.0 - s)), d * cg * s

    def body(hg, pg, ng, wg, hu, pu, nu, wu, da, dan, dhg, dhu, dwg, dwu):
        i = pl.program_id(1)
        live_prev = (i > 0).astype(F32)
        live_next = (i < n_tiles - 1).astype(F32)
        xg, xu = hg[...], hu[...]
        cg, g1, g2 = _conv3_past(xg, pg[...] * live_prev, wg[...])
        cu, u1, u2 = _conv3_past(xu, pu[...] * live_prev, wu[...])
        dg, du = d_conv_out(cg, cu, da[...])
        cgn, _, _ = _conv3_past(ng[...], hg[tm - SUBLANES:tm, :], wg[...])
        cun, _, _ = _conv3_past(nu[...], hu[tm - SUBLANES:tm, :], wu[...])
        dgn, dun = d_conv_out(cgn, cun, dan[...] * live_next)
        dhg[...] = _conv3_future(dg, dgn, wg[...]).astype(dhg.dtype)
        dhu[...] = _conv3_future(du, dun, wu[...]).astype(dhu.dtype)
        sums_g = [jnp.sum(xs * dg, axis=0, keepdims=True) for xs in (g2, g1, xg)]
        sums_u = [jnp.sum(xs * du, axis=0, keepdims=True) for xs in (u2, u1, xu)]

        @pl.when(i == 0)
        def _():
            for r_ in range(FFN_CONV):
                dwg[r_:r_ + 1, :] = sums_g[r_]
                dwu[r_:r_ + 1, :] = sums_u[r_]

        @pl.when(i > 0)
        def _():
            for r_ in range(FFN_CONV):
                dwg[r_:r_ + 1, :] += sums_g[r_]
                dwu[r_:r_ + 1, :] += sums_u[r_]

    nb = FFN_COL_BLOCKS
    nxt = lambda i: jnp.minimum((i + 1) * per, t // SUBLANES - 1)
    prv = lambda i: jnp.maximum(i * per - 1, 0)
    half = lambda off: [pl.BlockSpec((tm, FFN_TILE_COLS), lambda j, i: (i, off + j)),
                        pl.BlockSpec((SUBLANES, FFN_TILE_COLS), lambda j, i: (prv(i), off + j)),
                        pl.BlockSpec((SUBLANES, FFN_TILE_COLS), lambda j, i: (nxt(i), off + j)),
                        pl.BlockSpec((FFN_CONV, FFN_TILE_COLS), lambda j, i: (0, off + j))]
    tile = pl.BlockSpec((tm, FFN_TILE_COLS), lambda j, i: (i, j))
    taps = pl.BlockSpec((FFN_CONV, FFN_TILE_COLS), lambda j, i: (0, j))
    return pl.pallas_call(
        body, name="ffn_bwd", grid=(nb, n_tiles),
        in_specs=half(0) + half(nb) + [tile, pl.BlockSpec((SUBLANES, FFN_TILE_COLS), lambda j, i: (nxt(i), j))],
        out_specs=(tile, tile, taps, taps),
        out_shape=(jax.ShapeDtypeStruct((t, FFN_HIDDEN), BF16), jax.ShapeDtypeStruct((t, FFN_HIDDEN), BF16),
                   jax.ShapeDtypeStruct((FFN_CONV, FFN_HIDDEN), F32), jax.ShapeDtypeStruct((FFN_CONV, FFN_HIDDEN), F32)),
        compiler_params=_cparams(("parallel", "arbitrary")),
    )(hpre, hpre, hpre, cw, hpre, hpre, hpre, cw, dact, dact)


def _my_place():
    x, y, c = lax.axis_index("x"), lax.axis_index("y"), lax.axis_index("c")
    return x, y, c, 4 * x + 2 * y + c


N_CHIPS = 4


def _remote(src, dst, send_sem, recv_sem, dev):
    return pltpu.make_async_remote_copy(src_ref=src, dst_ref=dst, send_sem=send_sem, recv_sem=recv_sem, device_id=dev,
                                        device_id_type=pl.DeviceIdType.MESH)


def _chip_peer(x, y, k):
    return x ^ ((k >> 1) & 1), y ^ (k & 1)


def _all_gather_two_level(shard, name):
    r, w = shard.shape

    def body(src, out, send_sems, recv_sems, local_sem):
        x, y, c, me = _my_place()
        sibling = (x, y, 1 - c)
        mine = pltpu.make_async_copy(src, out.at[me], local_sem)
        mine.start()
        first = [_remote(src, out.at[me], send_sems.at[0], recv_sems.at[0], sibling)]
        for k in range(1, N_CHIPS):
            px, py = _chip_peer(x, y, k)
            first.append(_remote(src, out.at[me], send_sems.at[k], recv_sems.at[k], (px, py, c)))
        for cp in first:
            cp.start()
        passed = []
        for k in range(1, N_CHIPS):
            px, py = _chip_peer(x, y, k)
            landed = out.at[me ^ (2 * k)]
            _remote(src, landed, send_sems.at[k], recv_sems.at[k], (px, py, c)).wait_recv()
            fwd = _remote(landed, landed, send_sems.at[N_CHIPS - 1 + k], recv_sems.at[N_CHIPS - 1 + k], sibling)
            fwd.start()
            passed.append(fwd)
        _remote(src, out.at[me ^ 1], send_sems.at[0], recv_sems.at[0], sibling).wait_recv()
        for k in range(1, N_CHIPS):
            got = out.at[(me ^ 1) ^ (2 * k)]
            _remote(got, got, send_sems.at[N_CHIPS - 1 + k], recv_sems.at[N_CHIPS - 1 + k], sibling).wait_recv()
        for cp in first + passed:
            cp.wait_send()
        mine.wait()

    return pl.pallas_call(
        body, name=name,
        in_specs=[pl.BlockSpec(memory_space=pl.ANY)],
        out_specs=pl.BlockSpec(memory_space=pl.ANY),
        out_shape=jax.ShapeDtypeStruct((N_DEV, r, w), shard.dtype),
        scratch_shapes=[pltpu.SemaphoreType.DMA((N_DEV - 1,)), pltpu.SemaphoreType.DMA((N_DEV - 1,)), pltpu.SemaphoreType.DMA],
    )(shard)


def _device_peer(x, y, c, k):
    px, py, pc = x ^ ((k >> 2) & 1), y ^ ((k >> 1) & 1), c ^ (k & 1)
    return (px, py, pc), 4 * px + 2 * py + pc


_HBM = pl.BlockSpec(memory_space=pltpu.HBM)
_SEM = pl.BlockSpec(memory_space=pltpu.SEMAPHORE)


def _gather_start(shard, name):
    def body(src, land, send_sems, recv_sems, src_thru, land_thru, token):
        x, y, c, me = _my_place()
        for k in range(1, N_DEV):
            dev, _ = _device_peer(x, y, c, k)
            _remote(src, land.at[me], send_sems.at[k], recv_sems.at[k], dev).start()
        token[...] = jnp.zeros_like(token)

    landing = lax.empty((N_DEV,) + shard.shape, shard.dtype)
    return pl.pallas_call(
        body, name=name,
        out_shape=(pltpu.SemaphoreType.DMA((N_DEV,)), pltpu.SemaphoreType.DMA((N_DEV,)), pltpu.HBM(shard.shape, shard.dtype),
                   pltpu.HBM(landing.shape, landing.dtype), jax.ShapeDtypeStruct((SUBLANES, LANES), F32)),
        in_specs=(_HBM, _HBM), out_specs=(_SEM, _SEM, _HBM, _HBM, pl.BlockSpec(memory_space=pltpu.VMEM)),
        input_output_aliases={0: 2, 1: 3},
        compiler_params=pltpu.CompilerParams(has_side_effects=pltpu.SideEffectType.DATAFLOW_SIDE_EFFECTING),
    )(pltpu.with_memory_space_constraint(shard, pltpu.HBM), pltpu.with_memory_space_constraint(landing, pltpu.HBM))


def _gather_wait(send_sems, recv_sems, shard, landing, after, name):
    n_after = len(after)

    def body(*refs):
        src, land, send_sems, recv_sems = refs[:4]
        x, y, c, _ = _my_place()
        for k in range(1, N_DEV):
            dev, idx = _device_peer(x, y, c, k)
            cp = _remote(src, land.at[idx], send_sems.at[k], recv_sems.at[k], dev)
            cp.wait_send()
            cp.wait_recv()

    return pl.pallas_call(
        body, name=name,
        out_shape=(pltpu.HBM(shard.shape, shard.dtype), pltpu.HBM(landing.shape, landing.dtype)),
        in_specs=(_HBM, _HBM, _SEM, _SEM) + (pl.BlockSpec(memory_space=pl.ANY),) * n_after, out_specs=(_HBM, _HBM),
        input_output_aliases={0: 0, 1: 1},
        compiler_params=pltpu.CompilerParams(has_side_effects=pltpu.SideEffectType.DATAFLOW_SIDE_EFFECTING),
    )(shard, landing, send_sems, recv_sems, *after)[1]


def _slab_push_start(slabs, name):
    na = len(slabs)

    def body(*refs):
        srcs, lands = refs[:na], refs[na:2 * na]
        send_sems, recv_sems = refs[2 * na], refs[2 * na + 1]
        token = refs[-1]
        x, y, c, me = _my_place()
        for i in range(na):
            for k in range(1, N_DEV):
                dev, idx = _device_peer(x, y, c, k)
                s = i * N_DEV + k
                _remote(srcs[i].at[idx], lands[i].at[me], send_sems.at[s], recv_sems.at[s], dev).start()
        token[...] = jnp.zeros_like(token)

    hbm_shapes = [pltpu.HBM(a.shape, a.dtype) for a in slabs]
    ins = [pltpu.with_memory_space_constraint(a, pltpu.HBM) for a in slabs]
    ins += [pltpu.with_memory_space_constraint(lax.empty(a.shape, a.dtype), pltpu.HBM) for a in slabs]
    out = pl.pallas_call(
        body, name=name,
        out_shape=(pltpu.SemaphoreType.DMA((na * N_DEV,)), pltpu.SemaphoreType.DMA((na * N_DEV,)), *hbm_shapes, *hbm_shapes,
                   jax.ShapeDtypeStruct((SUBLANES, LANES), F32)),
        in_specs=(_HBM,) * (2 * na), out_specs=(_SEM, _SEM) + (_HBM,) * (2 * na) + (pl.BlockSpec(memory_space=pltpu.VMEM),),
        input_output_aliases={i: 2 + i for i in range(2 * na)},
        compiler_params=pltpu.CompilerParams(has_side_effects=pltpu.SideEffectType.DATAFLOW_SIDE_EFFECTING),
    )(*ins)
    return out[0], out[1], list(out[2:2 + na]), list(out[2 + na:2 + 2 * na]), out[-1]


def _slab_push_wait(send_sems, recv_sems, slabs, landings, after, name):
    na = len(slabs)

    def body(*refs):
        srcs, lands = refs[:na], refs[na:2 * na]
        send_sems, recv_sems = refs[2 * na], refs[2 * na + 1]
        x, y, c, me = _my_place()
        for i in range(na):
            for k in range(1, N_DEV):
                dev, idx = _device_peer(x, y, c, k)
                s = i * N_DEV + k
                cp = _remote(srcs[i].at[idx], lands[i].at[idx], send_sems.at[s], recv_sems.at[s], dev)
                cp.wait_send()
                cp.wait_recv()

    hbm_shapes = tuple(pltpu.HBM(a.shape, a.dtype) for a in slabs)
    out = pl.pallas_call(
        body, name=name, out_shape=hbm_shapes + hbm_shapes,
        in_specs=(_HBM,) * (2 * na) + (_SEM, _SEM) + (pl.BlockSpec(memory_space=pl.ANY),) * len(after),
        out_specs=(_HBM,) * (2 * na), input_output_aliases={i: i for i in range(2 * na)},
        compiler_params=pltpu.CompilerParams(has_side_effects=pltpu.SideEffectType.DATAFLOW_SIDE_EFFECTING),
    )(*slabs, *landings, send_sems, recv_sems, *after)
    return list(out[na:])


def _pair_exchange(arrays, name):
    na = len(arrays)

    def body(*refs):
        srcs, dsts, (send_sems, recv_sems) = refs[:na], refs[na:2 * na], refs[2 * na:]
        x, y, c, _ = _my_place()
        sibling = (x, y, 1 - c)
        copies = []
        for i in range(na):
            for q in range(N_CHIPS):
                s = i * N_CHIPS + q
                copies.append(_remote(srcs[i].at[2 * q + 1 - c], dsts[i].at[q], send_sems.at[s], recv_sems.at[s], sibling))
        for cp in copies:
            cp.start()
        for cp in copies:
            cp.wait_recv()
        for cp in copies:
            cp.wait_send()

    hbm = pl.BlockSpec(memory_space=pl.ANY)
    return pl.pallas_call(
        body, name=name, in_specs=[hbm] * na, out_specs=tuple([hbm] * na),
        out_shape=tuple(jax.ShapeDtypeStruct((N_CHIPS,) + a.shape[1:], a.dtype) for a in arrays),
        scratch_shapes=[pltpu.SemaphoreType.DMA((na * N_CHIPS,)), pltpu.SemaphoreType.DMA((na * N_CHIPS,))],
    )(*arrays)


ELEMENTWISE_COLS = 256


def _pair_sum(slabs, recv, core, out_dtype, name):
    _, r, w = slabs.shape
    tc = ELEMENTWISE_COLS

    def body(core_ref, mine, theirs, out):
        out[...] = (mine[...] + theirs[...]).astype(out.dtype)

    grid_spec = pltpu.PrefetchScalarGridSpec(
        num_scalar_prefetch=1, grid=(N_CHIPS, w // tc),
        in_specs=[pl.BlockSpec((None, r, tc), lambda q, j, core_ref: (2 * q + core_ref[0], 0, j)),
                  pl.BlockSpec((None, r, tc), lambda q, j, core_ref: (q, 0, j))],
        out_specs=pl.BlockSpec((None, r, tc), lambda q, j, core_ref: (q, 0, j)))
    return pl.pallas_call(body, name=name, grid_spec=grid_spec,
                          out_shape=jax.ShapeDtypeStruct((N_CHIPS, r, w), out_dtype),
                          compiler_params=_cparams(("parallel", "parallel")))(core, slabs, recv)


def _chip_exchange(arrays, name):
    na = len(arrays)

    def body(*refs):
        srcs, dsts, (send_sems, recv_sems, local_sems) = refs[:na], refs[na:2 * na], refs[2 * na:]
        x, y, c, _ = _my_place()
        chip = 2 * x + y
        own = [pltpu.make_async_copy(srcs[i].at[chip], dsts[i].at[chip], local_sems.at[i]) for i in range(na)]
        for cp in own:
            cp.start()
        sends, arrivals = [], []
        for i in range(na):
            for k in range(1, N_CHIPS):
                px, py = _chip_peer(x, y, k)
                s = i * N_CHIPS + k
                sends.append(_remote(srcs[i].at[chip ^ k], dsts[i].at[chip], send_sems.at[s], recv_sems.at[s], (px, py, c)))
                arrivals.append(_remote(srcs[i].at[chip], dsts[i].at[chip ^ k], send_sems.at[s], recv_sems.at[s], (px, py, c)))
        for cp in sends:
            cp.start()
        for cp in arrivals:
            cp.wait_recv()
        for cp in sends:
            cp.wait_send()
        for cp in own:
            cp.wait()

    hbm = pl.BlockSpec(memory_space=pl.ANY)
    return pl.pallas_call(
        body, name=name, in_specs=[hbm] * na, out_specs=tuple([hbm] * na),
        out_shape=tuple(jax.ShapeDtypeStruct(a.shape, a.dtype) for a in arrays),
        scratch_shapes=[pltpu.SemaphoreType.DMA((na * N_CHIPS,)), pltpu.SemaphoreType.DMA((na * N_CHIPS,)),
                        pltpu.SemaphoreType.DMA((na,))],
    )(*arrays)


def _adamw_update(g, w, m, v):
    c1 = 1.0 / (1.0 - ADAM_B1 ** ADAM_STEP)
    c2 = 1.0 / (1.0 - ADAM_B2 ** ADAM_STEP)
    mn = ADAM_B1 * m + (1.0 - ADAM_B1) * g
    vn = ADAM_B2 * v + (1.0 - ADAM_B2) * (g * g)
    return -ADAM_LR * ((mn * c1) / (jnp.sqrt(vn * c2) + ADAM_EPS) + ADAM_WD * w), mn, vn


def _reduce_adamw(parts, w, m, v, name):
    n_parts, r, wd = parts.shape
    tc = ELEMENTWISE_COLS

    def body(p_ref, w_ref, m_ref, v_ref, g_out, d_out, m_out, v_out):
        g = p_ref[0].astype(F32)
        for s in range(1, n_parts):
            g = g + p_ref[s].astype(F32)
        g_out[...] = g
        d_out[...], m_out[...], v_out[...] = _adamw_update(g, w_ref[...], m_ref[...], v_ref[...])

    blk = pl.BlockSpec((r, tc), lambda j: (0, j))
    shp = jax.ShapeDtypeStruct((r, wd), F32)
    return pl.pallas_call(
        body, name=name, grid=(wd // tc,),
        in_specs=[pl.BlockSpec((n_parts, r, tc), lambda j: (0, 0, j)), blk, blk, blk],
        out_specs=(blk, blk, blk, blk), out_shape=(shp, shp, shp, shp),
        compiler_params=_cparams(("parallel",)),
    )(parts, w, m, v)


def _reduce_landed_adamw(landing, own, me, w, m, v, name):
    n_parts, r, wd = landing.shape
    tc = ELEMENTWISE_COLS

    def body(me_ref, land_ref, own_ref, w_ref, m_ref, v_ref, g_out, d_out, m_out, v_out):
        mine = own_ref[...].astype(F32)
        g = None
        for s in range(n_parts):
            part = jnp.where(me_ref[0] == s, mine, land_ref[s].astype(F32))
            g = part if g is None else g + part
        g_out[...] = g
        d_out[...], m_out[...], v_out[...] = _adamw_update(g, w_ref[...], m_ref[...], v_ref[...])

    blk = pl.BlockSpec((r, tc), lambda j, me_ref: (0, j))
    shp = jax.ShapeDtypeStruct((r, wd), F32)
    grid_spec = pltpu.PrefetchScalarGridSpec(
        num_scalar_prefetch=1, grid=(wd // tc,),
        in_specs=[pl.BlockSpec((n_parts, r, tc), lambda j, me_ref: (0, 0, j)),
                  pl.BlockSpec((None, r, tc), lambda j, me_ref: (me_ref[0], 0, j)), blk, blk, blk],
        out_specs=(blk, blk, blk, blk))
    return pl.pallas_call(body, name=name, grid_spec=grid_spec, out_shape=(shp, shp, shp, shp),
                          compiler_params=_cparams(("parallel",)))(me, landing, own, w, m, v)


PACK_W = 1024


def _pad_heads(a, slots):
    lead = a.shape[:-1]
    a = a.reshape(lead + (slots, RWKV_HEAD_DIM))
    a = jnp.pad(a, [(0, 0)] * (len(lead) + 1) + [(0, LANES - RWKV_HEAD_DIM)])
    return a.reshape(lead + (slots * LANES,))


def _unpad_heads(a, slots):
    lead = a.shape[:-1]
    return a.reshape(lead + (slots, LANES))[..., :RWKV_HEAD_DIM].reshape(lead + (slots * RWKV_HEAD_DIM,))


def _flat_pack(arrs, dtype, row_mult):
    flat = jnp.concatenate([a.reshape(-1).astype(dtype) for a in arrs])
    n = flat.shape[0]
    rows = -(-n // PACK_W)
    rows = -(-rows // row_mult) * row_mult
    return jnp.pad(flat, (0, rows * PACK_W - n)).reshape(rows, PACK_W)


def _row_pack(arrs, dtype, row_mult):
    parts = [a.astype(dtype) if a.shape[1] == PACK_W else a.astype(dtype).reshape(-1, PACK_W) for a in arrs]
    rows = sum(p.shape[0] for p in parts)
    pad = -(-rows // row_mult) * row_mult - rows
    return jnp.concatenate(parts + ([jnp.zeros((pad, PACK_W), dtype)] if pad else []), axis=0)


def _unpack_row_gathered(g, names, shard_shapes):
    out, r0 = {}, 0
    for n in names:
        s = shard_shapes[n]
        rows = s[0] * s[1] // PACK_W
        seg = g[:, r0:r0 + rows, :]
        r0 += rows
        if s[1] == PACK_W:
            assert SHARD_AXIS[n] == 0
            out[n] = seg.reshape(N_DEV * s[0], s[1])
        else:
            assert SHARD_AXIS[n] == 1
            out[n] = jnp.transpose(seg.reshape((N_DEV,) + tuple(s)), (1, 0, 2)).reshape(s[0], N_DEV * s[1])
    return out


def _unpack_gathered(g, names, shard_shapes):
    flat = g.reshape(N_DEV, -1)
    out, off = {}, 0
    for n in names:
        s = shard_shapes[n]
        size = s[0] * s[1]
        seg = flat[:, off:off + size].reshape((N_DEV,) + tuple(s))
        off += size
        if SHARD_AXIS[n] == 1:
            out[n] = jnp.transpose(seg, (1, 0, 2)).reshape(s[0], N_DEV * s[1])
        else:
            out[n] = seg.reshape(N_DEV * s[0], s[1])
    return out


def _shard_major(full, axis):
    a, b = full.shape
    if axis == 1:
        return jnp.transpose(full.reshape(a, N_DEV, b // N_DEV), (1, 0, 2)).reshape(N_DEV, -1)
    return full.reshape(N_DEV, -1)


def _prepare_weights(full, rep):
    w = full['w_in']
    d = w.shape[1]
    rkv = jnp.pad(w[0:1536].reshape(3 * RWKV_HEADS, RWKV_HEAD_DIM, d), ((0, 0), (0, LANES - RWKV_HEAD_DIM), (0, 0)))
    w_cat = jnp.concatenate([
        w[3848:4872], w[4872:5896], rkv.reshape(3 * RWKV_HEADS * LANES, d), w[1792:3328], w[3328:3840],
        w[1536:1792], jnp.pad(w[3840:3848], ((0, LANES - 8), (0, 0))), jnp.zeros((LANES, d), w.dtype)], axis=0)
    assert w_cat.shape[0] == CAT_W
    mu = rep['rwkv_mu']
    vecs = [mu[0:512], mu[512:1024], mu[1024:1536], rep['rwkv_w0'], rep['rwkv_a0'], rep['rwkv_k_k'], rep['rwkv_k_a'],
            rep['rwkv_ln_w'], rep['rwkv_ln_b'], rep['rwkv_r_k'].reshape(-1)]
    ppack = jnp.stack([jnp.pad(v.reshape(RWKV_HEADS, RWKV_HEAD_DIM), ((0, 0), (0, LANES - RWKV_HEAD_DIM))) for v in vecs], axis=1)
    ppack = jnp.pad(ppack, ((0, 0), (0, 16 - len(vecs)), (0, 0)))
    mulo = mu[1536:1792].reshape(1, 2 * LANES)
    wl = jnp.zeros((3, 2 * LANES, RWKV_HEADS * LANES), F32)
    wl = wl.at[0, 0:64].set(_pad_heads(full['rwkv_w2'], RWKV_HEADS))
    wl = wl.at[1, 64:128].set(_pad_heads(full['rwkv_a2'], RWKV_HEADS))
    wl = wl.at[2, 128:256].set(_pad_heads(full['rwkv_g2'], RWKV_HEADS))
    wl = jnp.transpose(wl.reshape(3, 2 * LANES, RWKV_HEADS, LANES), (2, 0, 1, 3))
    cw = full['gdn_conv_w'].reshape(GDN_CONV, 3, GDN_HEADS, LANES)
    cwpack = jnp.pad(jnp.transpose(cw, (2, 1, 0, 3)), ((0, 0), (0, 0), (0, SUBLANES - GDN_CONV), (0, 0)))
    gpar = jnp.zeros((SUBLANES, LANES), F32)
    gpar = gpar.at[0, 0:GDN_HEADS].set(rep['gdn_a_log']).at[1, 0:GDN_HEADS].set(rep['gdn_dt_bias']).at[2].set(rep['gdn_norm_w'])
    return dict(w_cat=w_cat, ffn_cw=full['ffn_conv_w'], ppack=ppack, mulo=mulo, wl=wl, cwpack=cwpack, gpar=gpar,
                g1=rep['norm1_g'].reshape(1, -1), g2=rep['norm2_g'].reshape(1, -1), gf=rep['final_g'].reshape(1, -1))


def _prepare_late_weights(full):
    rp = full['rwkv_proj']
    rproj = jnp.pad(rp.reshape(RWKV_HEADS, RWKV_HEAD_DIM, -1), ((0, 0), (0, LANES - RWKV_HEAD_DIM), (0, 0))).reshape(RWKV_HEADS * LANES, -1)
    return dict(rproj=rproj, gproj=full['gdn_proj'], w_out=full['w_out'], ffn_up=full['ffn_up'], ffn_down=full['ffn_down'])


def _local_step(x, target, p, late_weights, push_grads):
    d = x.shape[1]
    full_w = lambda a: (a, a.shape[1], 0)
    (u,) = _rw_forward(_norm_fn, [full_w(x)], [p['g1']], [(d, BF16)], "norm1")
    p_cat = _matmul(u, p['w_cat'], 'nt', F32, "proj_in")
    ya_pre, st_r = _rwkv_forward(p_cat, p['ppack'], p['mulo'], p['wl'])
    yb_pre, st_g = _gdn_forward(p_cat, p['cwpack'], p['gpar'])
    p = {**p, **late_weights((ya_pre, yb_pre))}
    ya = _matmul(ya_pre, p['rproj'], 'nn', F32, "rwkv_proj")
    yb = _matmul(yb_pre, p['gproj'], 'nn', F32, "gdn_proj")
    gates = [(p_cat, d, OFF_GA // d), (p_cat, d, OFF_GB // d)]
    (mixed,) = _rw_forward(_merge_fn, gates + [full_w(ya), full_w(yb)], [], [(d, BF16)], "merge")
    mo = _matmul(mixed, p['w_out'], 'nn', F32, "out_proj")
    x1, n2 = _rw_forward(_res_norm_fn, [full_w(x), full_w(mo)], [p['g2']], [(d, F32), (d, BF16)], "res_norm2")
    hpre = _matmul(n2, p['ffn_up'], 'nt', F32, "ffn_up")
    act = _ffn_act_forward(hpre, p['ffn_cw'])
    fo = _matmul(act, p['ffn_down'], 'nn', F32, "ffn_down")
    loss_vec, dx2, dx2b, dgf = _loss_head(x1, fo, p['gf'], target, "loss_head")

    dact = _matmul(dx2b, p['ffn_down'], 'nt', F32, "d_act")
    dw_down = _matmul(act, dx2b, 'tn', BF16, "dw_ffn_down")
    dh_gate, dh_up, dcw_gate, dcw_up = _ffn_backward(hpre, p['ffn_cw'], dact)
    dh = jnp.concatenate([dh_gate, dh_up], axis=1)
    dcw_f = jnp.concatenate([dcw_gate, dcw_up], axis=1)
    dn2 = _matmul(dh, p['ffn_up'], 'nn', F32, "d_norm2")
    dw_up = _matmul(dh, n2, 'tn', BF16, "dw_ffn_up")
    token = push_grads({'ffn_down': dw_down, 'ffn_up': dw_up})
    dx1, dx1b, dg2 = _rw_backward(_res_norm_fn, [full_w(x), full_w(mo)], [p['g2'] + token], [full_w(dx2), full_w(dn2)],
                                  [F32, BF16], "res_norm2_bwd")
    dmixed = _matmul(dx1b, p['w_out'], 'nt', F32, "d_mixed")
    dw_out = _matmul(mixed, dx1b, 'tn', BF16, "dw_out")
    dga, dgb, dya, dyb = _rw_backward(_merge_fn, gates + [full_w(ya), full_w(yb)], [], [full_w(dmixed)],
                                      [BF16, BF16, BF16, BF16], "merge_bwd")
    d_ya_pre = _matmul(dya, p['rproj'], 'nt', F32, "d_rwkv_out")
    dw_rproj = _matmul(ya_pre, dya, 'tn', F32, "dw_rwkv_proj")
    d_yb_pre = _matmul(dyb, p['gproj'], 'nt', F32, "d_gdn_out")
    dw_gproj = _matmul(yb_pre, dyb, 'tn', F32, "dw_gdn_proj")
    dpr, dpk, dpv, dplo, dpp, dml, dwl = _rwkv_backward(p_cat, p['ppack'], p['mulo'], p['wl'], st_r, d_ya_pre)
    dq, dk, dv, dz, dab, dcw_g, dgp = _gdn_backward(p_cat, p['cwpack'], p['gpar'], st_g, d_yb_pre)
    t = x.shape[0]
    dp_cat = jnp.concatenate([dga, dgb, dpr, dpk, dpv, dq, dk, dv, dz, dplo.astype(BF16), dab.astype(BF16),
                              jnp.zeros((t, LANES), BF16)], axis=1)
    dw_cat = _matmul(dp_cat, u, 'tn', BF16, "dw_in")
    dw_in = jnp.concatenate([dw_cat[OFF_RKV:OFF_QKV].reshape(3 * RWKV_HEADS, LANES, d)[:, :RWKV_HEAD_DIM].reshape(-1, d),
                             dw_cat[OFF_LO:OFF_AB], dw_cat[OFF_QKV:OFF_Z], dw_cat[OFF_Z:OFF_LO], dw_cat[OFF_AB:OFF_AB + 8],
                             dw_cat[OFF_GA:OFF_GB], dw_cat[OFF_GB:OFF_RKV]], axis=0)
    token = push_grads({'w_out': dw_out, 'w_in': dw_in})
    du = _matmul(dp_cat, p['w_cat'], 'nn', F32, "d_norm1")
    grad_x, dg1 = _rw_backward(_norm_skip_fn, [full_w(x)], [p['g1'] + token], [full_w(du), full_w(dx1)], [F32], "norm1_bwd")

    heads = lambda row: dpp[:, row, :RWKV_HEAD_DIM].reshape(-1)
    lora = lambda j, lo_, hi_: jnp.transpose(dwl[:, j, lo_:hi_, :RWKV_HEAD_DIM], (1, 0, 2)).reshape(hi_ - lo_, RWKV_WIDTH)
    grads = {
        'norm1_g': dg1[0],
        'w_in': dw_in,
        'rwkv_mu': jnp.concatenate([heads(0), heads(1), heads(2), jnp.sum(dml[:, 0, :], axis=0)]),
        'rwkv_w0': heads(3), 'rwkv_a0': heads(4), 'rwkv_k_k': heads(5), 'rwkv_k_a': heads(6),
        'rwkv_ln_w': heads(7), 'rwkv_ln_b': heads(8), 'rwkv_r_k': heads(9).reshape(RWKV_HEADS, RWKV_HEAD_DIM),
        'rwkv_w2': lora(0, 0, 64), 'rwkv_a2': lora(1, 64, 128), 'rwkv_g2': lora(2, 128, 256),
        'rwkv_proj': dw_rproj.reshape(RWKV_HEADS, LANES, -1)[:, :RWKV_HEAD_DIM].reshape(RWKV_WIDTH, -1),
        'gdn_conv_w': jnp.transpose(dcw_g[:, :, :GDN_CONV, :], (2, 1, 0, 3)).reshape(GDN_CONV, 3 * GDN_WIDTH),
        'gdn_a_log': jnp.sum(dgp[:, 0, :GDN_HEADS], axis=0), 'gdn_dt_bias': jnp.sum(dgp[:, 1, :GDN_HEADS], axis=0),
        'gdn_norm_w': jnp.sum(dgp[:, 2, :], axis=0),
        'gdn_proj': dw_gproj, 'w_out': dw_out, 'norm2_g': dg2[0], 'ffn_up': dw_up, 'ffn_conv_w': dcw_f,
        'ffn_down': dw_down, 'final_g': dgf[0],
    }
    return loss_vec, grad_x, grads


def kernel(x, norm1_g, w_in, rwkv_mu, rwkv_w0, rwkv_w2, rwkv_a0, rwkv_a2, rwkv_g2, rwkv_k_k, rwkv_k_a, rwkv_r_k, rwkv_ln_w, rwkv_ln_b, rwkv_proj, gdn_conv_w, gdn_a_log, gdn_dt_bias, gdn_norm_w, gdn_proj, w_out, norm2_g, ffn_up, ffn_conv_w, ffn_down, final_g, loss_target, m_norm1_g, m_w_in, m_rwkv_mu, m_rwkv_w0, m_rwkv_w2, m_rwkv_a0, m_rwkv_a2, m_rwkv_g2, m_rwkv_k_k, m_rwkv_k_a, m_rwkv_r_k, m_rwkv_ln_w, m_rwkv_ln_b, m_rwkv_proj, m_gdn_conv_w, m_gdn_a_log, m_gdn_dt_bias, m_gdn_norm_w, m_gdn_proj, m_w_out, m_norm2_g, m_ffn_up, m_ffn_conv_w, m_ffn_down, m_final_g, v_norm1_g, v_w_in, v_rwkv_mu, v_rwkv_w0, v_rwkv_w2, v_rwkv_a0, v_rwkv_a2, v_rwkv_g2, v_rwkv_k_k, v_rwkv_k_a, v_rwkv_r_k, v_rwkv_ln_w, v_rwkv_ln_b, v_rwkv_proj, v_gdn_conv_w, v_gdn_a_log, v_gdn_dt_bias, v_gdn_norm_w, v_gdn_proj, v_w_out, v_norm2_g, v_ffn_up, v_ffn_conv_w, v_ffn_down, v_final_g):
    given = dict(zip(WEIGHT_NAMES, (norm1_g, w_in, rwkv_mu, rwkv_w0, rwkv_w2, rwkv_a0, rwkv_a2, rwkv_g2, rwkv_k_k, rwkv_k_a, rwkv_r_k,
                                    rwkv_ln_w, rwkv_ln_b, rwkv_proj, gdn_conv_w, gdn_a_log, gdn_dt_bias, gdn_norm_w, gdn_proj, w_out,
                                    norm2_g, ffn_up, ffn_conv_w, ffn_down, final_g)))
    mom1 = dict(zip(WEIGHT_NAMES, (m_norm1_g, m_w_in, m_rwkv_mu, m_rwkv_w0, m_rwkv_w2, m_rwkv_a0, m_rwkv_a2, m_rwkv_g2, m_rwkv_k_k,
                                   m_rwkv_k_a, m_rwkv_r_k, m_rwkv_ln_w, m_rwkv_ln_b, m_rwkv_proj, m_gdn_conv_w, m_gdn_a_log,
                                   m_gdn_dt_bias, m_gdn_norm_w, m_gdn_proj, m_w_out, m_norm2_g, m_ffn_up, m_ffn_conv_w, m_ffn_down,
                                   m_final_g)))
    mom2 = dict(zip(WEIGHT_NAMES, (v_norm1_g, v_w_in, v_rwkv_mu, v_rwkv_w0, v_rwkv_w2, v_rwkv_a0, v_rwkv_a2, v_rwkv_g2, v_rwkv_k_k,
                                   v_rwkv_k_a, v_rwkv_r_k, v_rwkv_ln_w, v_rwkv_ln_b, v_rwkv_proj, v_gdn_conv_w, v_gdn_a_log,
                                   v_gdn_dt_bias, v_gdn_norm_w, v_gdn_proj, v_w_out, v_norm2_g, v_ffn_up, v_ffn_conv_w, v_ffn_down,
                                   v_final_g)))
    def strip(n, a):
        a = a if n == 'final_g' else a.reshape(a.shape[1:])
        return a.T if n in TRANSPOSED else a

    local = {n: strip(n, a) for n, a in given.items()}
    shard_shapes = {n: local[n].shape for n in SHARD_AXIS}
    sharded = BIG_SHARDED + SMALL_SHARDED

    late_names = [n for n in BIG_SHARDED if n != 'w_in']
    g_in = _all_gather_two_level(_row_pack([local['w_in']], BF16, 16), "gather_w_in")
    g_small = _all_gather_two_level(_flat_pack([local[n] for n in SMALL_SHARDED], F32, SUBLANES), "gather_small")
    late_pack, g_in, g_small = lax.optimization_barrier((_row_pack([local[n] for n in late_names], BF16, 16), g_in, g_small))
    send_sems, recv_sems, late_pack, landing, token = _gather_start(late_pack, "gather_late_start")
    full = _unpack_row_gathered(g_in, ['w_in'], shard_shapes)
    full.update(_unpack_gathered(g_small, SMALL_SHARDED, shard_shapes))
    rep = {n: local[n] for n in REPLICATED}
    rep['norm1_g'] = rep['norm1_g'] + token[0, 0]

    def late_weights(after):
        got = _gather_wait(send_sems, recv_sems, late_pack, landing, after, "gather_late_wait")
        me = 4 * lax.axis_index("x") + 2 * lax.axis_index("y") + lax.axis_index("c")
        slot = lax.broadcasted_iota(jnp.int32, (N_DEV, 1, 1), 0)
        got = jnp.where(slot == me, late_pack[None], got)
        return _prepare_late_weights(_unpack_row_gathered(got, late_names, shard_shapes))

    pushes = []
    me = 4 * lax.axis_index("x") + 2 * lax.axis_index("y") + lax.axis_index("c")
    slot = lax.broadcasted_iota(jnp.int32, (N_DEV, 1, 1), 0)

    def push_grads(group):
        names = list(group)
        slabs = [group[n].reshape(N_DEV, -1, group[n].shape[1]) for n in names]
        send_sems, recv_sems, slabs, landings, token = _slab_push_start(slabs, "grad_push_start_" + "_".join(names))
        pushes.append((names, send_sems, recv_sems, slabs, landings))
        return token[0, 0]

    loss_vec, grad_x, grads = _local_step(x[0], loss_target[0], _prepare_weights(full, rep), late_weights, push_grads)

    landed = {}
    for names, send_sems, recv_sems, slabs, landings in pushes:
        got = _slab_push_wait(send_sems, recv_sems, slabs, landings, (grad_x,), "grad_push_wait_" + "_".join(names))
        for n, slab, land in zip(names, slabs, got):
            landed[n] = (land, slab)

    small_sharded = ['rwkv_proj', 'gdn_proj'] + SMALL_SHARDED
    small_names = small_sharded + REPLICATED
    rep_vec = jnp.concatenate([grads[n].reshape(-1) for n in REPLICATED] + [loss_vec[0, 0:1]])
    slab_small = jnp.concatenate([_shard_major(grads[n], SHARD_AXIS[n]) for n in small_sharded] +
                                 [jnp.broadcast_to(rep_vec[None], (N_DEV, rep_vec.shape[0]))], axis=1)
    small_rows = -(-slab_small.shape[1] // (PACK_W * SUBLANES)) * SUBLANES
    slab_small = jnp.pad(slab_small, ((0, 0), (0, small_rows * PACK_W - slab_small.shape[1]))).reshape(N_DEV, small_rows, PACK_W)
    core = lax.axis_index("c").astype(jnp.int32).reshape(1)
    (from_sibling,) = _pair_exchange([slab_small], "grad_pair_exchange")
    chip_small = _pair_sum(slab_small, from_sibling, core, F32, "grad_pair_sum_small")
    (parts_small,) = _chip_exchange([chip_small], "grad_chip_exchange")

    def pack_local(src):
        flat = jnp.concatenate([strip(n, src[n]).reshape(-1) for n in small_names])
        return jnp.pad(flat, (0, small_rows * PACK_W - flat.shape[0])).reshape(small_rows, PACK_W)

    results = [({}, None) for _ in range(4)]
    me_arr = me.astype(jnp.int32).reshape(1)
    for n in ROW_SHARDED:
        packs = _reduce_landed_adamw(*landed[n], me_arr, local[n], strip(n, mom1[n]), strip(n, mom2[n]), "adamw_" + n)
        for (out, _), pk in zip(results, packs):
            out[n] = (pk.T if n in TRANSPOSED else pk).reshape(given[n].shape)
    packs = _reduce_adamw(parts_small, pack_local(given), pack_local(mom1), pack_local(mom2), "adamw_small")
    for i, pk in enumerate(packs):
        flat, off = pk.reshape(-1), 0
        for n in small_names:
            size = int(np.prod(given[n].shape))
            results[i][0][n] = flat[off:off + size].reshape(given[n].shape)
            off += size
        results[i] = (results[i][0], flat[off])
    (g_out, loss), (d_out, _), (m_out, _), (v_out, _) = results
    return (loss, grad_x[None], *[g_out[n] for n in WEIGHT_NAMES], *[d_out[n] for n in WEIGHT_NAMES],
            *[m_out[n] for n in WEIGHT_NAMES], *[v_out[n] for n in WEIGHT_NAMES])
```

```python
import functools

import jax
import jax.numpy as jnp
import numpy as np
from jax import lax
from jax.experimental import pallas as pl
from jax.experimental.pallas import tpu as pltpu

F32 = jnp.float32
BF16 = jnp.bfloat16
HI = lax.Precision.HIGHEST

N_DEV = 8
D_MODEL = 1024
CHUNK = 64
RWKV_HEADS = 8
RWKV_HEAD_DIM = 64
RWKV_WIDTH = 512
GDN_HEADS = 4
GDN_HEAD_DIM = 128
GDN_WIDTH = 512
GDN_CONV = 4
FFN_HIDDEN = 2816
FFN_CONV = 3
NORM_EPS = 1e-6
L2_EPS = 1e-6
RWKV_GN_EPS = 64e-5
LANES = 128
SUBLANES = 8
VMEM_LIMIT = 56 * 1024 * 1024

ADAM_LR = 0.001
ADAM_B1 = 0.9
ADAM_B2 = 0.999
ADAM_EPS = 1e-08
ADAM_WD = 0.01
ADAM_STEP = 10

OFF_GA, OFF_GB, OFF_RKV, OFF_QKV, OFF_Z, OFF_LO, OFF_AB, CAT_W = 0, 1024, 2048, 5120, 6656, 7168, 7424, 7680
RWKV_HB = 8
GDN_HB = 4
GDN_STEP_CHUNKS = 2
GDN_TILE = GDN_STEP_CHUNKS * CHUNK

WEIGHT_NAMES = ['norm1_g', 'w_in', 'rwkv_mu', 'rwkv_w0', 'rwkv_w2', 'rwkv_a0', 'rwkv_a2', 'rwkv_g2', 'rwkv_k_k', 'rwkv_k_a',
                'rwkv_r_k', 'rwkv_ln_w', 'rwkv_ln_b', 'rwkv_proj', 'gdn_conv_w', 'gdn_a_log', 'gdn_dt_bias', 'gdn_norm_w',
                'gdn_proj', 'w_out', 'norm2_g', 'ffn_up', 'ffn_conv_w', 'ffn_down', 'final_g']
BIG_SHARDED = ['w_in', 'ffn_up', 'ffn_down', 'w_out', 'rwkv_proj', 'gdn_proj']
SMALL_SHARDED = ['rwkv_w2', 'rwkv_a2', 'rwkv_g2', 'gdn_conv_w', 'ffn_conv_w']
TRANSPOSED = ('w_in', 'ffn_up')
SHARD_AXIS = {'w_in': 0, 'ffn_up': 0, 'ffn_down': 0, 'w_out': 0, 'rwkv_proj': 1, 'gdn_proj': 1,
              'rwkv_w2': 1, 'rwkv_a2': 1, 'rwkv_g2': 1, 'gdn_conv_w': 1, 'ffn_conv_w': 1}
REPLICATED = [n for n in WEIGHT_NAMES if n not in SHARD_AXIS]
ROW_SHARDED = ['w_in', 'ffn_up', 'ffn_down', 'w_out']


def _cparams(sem=None):
    kw = dict(vmem_limit_bytes=VMEM_LIMIT)
    if sem is not None:
        kw['dimension_semantics'] = sem
    return pltpu.CompilerParams(**kw)


_NN, _NT, _TN = 'nn', 'nt', 'tn'
_DIMS_2D = {'nn': (((1,), (0,)), ((), ())), 'nt': (((1,), (1,)), ((), ())), 'tn': (((0,), (0,)), ((), ()))}
_DIMS_3D = {'nn': (((2,), (1,)), ((0,), (0,))), 'nt': (((2,), (2,)), ((0,), (0,))), 'tn': (((1,), (1,)), ((0,), (0,)))}


def _dg(a, b, kind):
    return lax.dot_general(a, b, (_DIMS_2D if a.ndim == 2 else _DIMS_3D)[kind], preferred_element_type=F32)


def _dot1(a, b, kind):
    return _dg(a.astype(BF16), b.astype(BF16), kind)


@jax.custom_vjp
def _dhi(a, b):
    return _dot1(a, b, _NN)


_dhi.defvjp(lambda a, b: (_dot1(a, b, _NN), (a, b)),
            lambda res, ct: (_dot1(ct, res[1], _NT), _dot1(res[0], ct, _TN)))


@jax.custom_vjp
def _dnt(a, b):
    return _dot1(a, b, _NT)


_dnt.defvjp(lambda a, b: (_dot1(a, b, _NT), (a, b)),
            lambda res, ct: (_dot1(ct, res[1], _NN), _dot1(ct, res[0], _TN)))


@jax.custom_vjp
def _dtn(a, b):
    return _dot1(a, b, _TN)


_dtn.defvjp(lambda a, b: (_dot1(a, b, _TN), (a, b)),
            lambda res, ct: (_dot1(res[1], ct, _NT), _dot1(res[0], ct, _NN)))


def _split3(x):
    x1 = x.astype(BF16)
    r1 = x - x1.astype(F32)
    x2 = r1.astype(BF16)
    return x1, x2, (r1 - x2.astype(F32)).astype(BF16)


def _dot_exact_lhs(sel, x, kind):
    parts = [_dg(sel, xi, kind) for xi in _split3(x)]
    return parts[0] + parts[1] + parts[2]


def _tril_ones(like):
    c = like.shape[-2]
    ri, ci = _iotas(c)
    return jnp.broadcast_to((ri >= ci).astype(BF16), like.shape[:-2] + (c, c))


@jax.custom_vjp
def _cumsum_rows(x):
    return _dot_exact_lhs(_tril_ones(x), x, _NN)


_cumsum_rows.defvjp(lambda x: (_dot_exact_lhs(_tril_ones(x), x, _NN), None),
                    lambda _, ct: (_dot_exact_lhs(_tril_ones(ct), ct, _TN),))


@jax.custom_vjp
def _lane_sum_as_row(x):
    return _dot_exact_lhs(jnp.ones(x.shape, BF16), x, _NT)


def _lane_sum_as_row_bwd(_, ct):
    ones = jnp.ones(ct.shape[:-1] + (LANES,), BF16)
    parts = [_dg(ci, ones, _TN) for ci in _split3(ct)]
    return (parts[0] + parts[1] + parts[2],)


_lane_sum_as_row.defvjp(lambda x: (_dot_exact_lhs(jnp.ones(x.shape, BF16), x, _NT), None), _lane_sum_as_row_bwd)


def _shift_rows(x, halo, s):
    rows = lax.broadcasted_iota(jnp.int32, x.shape, 0)
    out = pltpu.roll(x, s, 0)
    for i in range(s):
        out = jnp.where(rows == i, halo[SUBLANES - s + i:SUBLANES - s + i + 1, :], out)
    return out


def _unshift_rows(g, carry, s):
    c = g.shape[0]
    rows = lax.broadcasted_iota(jnp.int32, g.shape, 0)
    out = pltpu.roll(g, c - s, 0)
    for i in range(s):
        out = jnp.where(rows == c - s + i, carry[i:i + 1, :], out)
    return out


def _sigmoid(z):
    return 1.0 / (1.0 + jnp.exp(-z))


def _silu(z):
    return z * _sigmoid(z)


def _softplus(z):
    return jnp.maximum(z, 0.0) + jnp.log(1.0 + jnp.exp(-jnp.abs(z)))


def _rms(t, gain):
    return t * lax.rsqrt(jnp.mean(t * t, axis=-1, keepdims=True) + NORM_EPS) * gain


def _iotas(c):
    return lax.broadcasted_iota(jnp.int32, (c, c), 0), lax.broadcasted_iota(jnp.int32, (c, c), 1)


def _unit_lower_inverse(xm, eye):
    t = eye + xm
    p = xm
    for _ in range(5):
        p = _dhi(p, p)
        t = t + _dhi(t, p)
    return t


def _rwkv_head(pr, pk, pv, plo, qr, qk, qv, qlo, s0, pp, mulo, wl):
    c = pr.shape[1]
    ri, ci = _iotas(c)

    def mix(p, q, mu):
        return p + (q - p) * mu

    r = mix(pr, qr, pp[:, 0:1])
    k = mix(pk, qk, pp[:, 1:2])
    v = mix(pv, qv, pp[:, 2:3])
    lo = mix(plo, qlo, mulo)
    w0, a0, k_k, k_a, ln_w, ln_b, r_k = (pp[:, i:i + 1] for i in range(3, 10))
    per_head = lambda t: jnp.broadcast_to(t, (pr.shape[0],) + t.shape)
    zw = _dhi(per_head(jnp.tanh(lo)), wl[:, 0])
    za = _dhi(per_head(lo), wl[:, 1])
    g = _dhi(per_head(_sigmoid(lo)), wl[:, 2])
    w_log = -_softplus(-(w0 + zw)) - 0.5
    lw = -jnp.exp(w_log)
    a = _sigmoid(a0 + za)
    kk = k * k_k
    kk = kk * lax.rsqrt(jnp.sum(kk * kk, axis=-1, keepdims=True) + L2_EPS)
    k2 = k * (1.0 + (a - 1.0) * k_a)
    an = -kk
    b = kk * a
    causal = ri >= ci
    strict = ri > ci
    eye = (ri == ci).astype(F32)
    cl = _cumsum_rows(lw)
    ecl = jnp.exp(-cl)
    at = an * jnp.exp(cl - lw)
    bt = b * ecl
    kt = k2 * ecl
    rt = r * jnp.exp(cl)
    a_ab = jnp.where(strict, _dnt(at, bt), 0.0)
    a_ak = jnp.where(strict, _dnt(at, kt), 0.0)
    tinv = _unit_lower_inverse(a_ab, eye)
    u = _dhi(tinv, _dnt(at, s0) + _dhi(a_ak, v))
    y = _dnt(rt, s0) + _dhi(jnp.where(causal, _dnt(rt, bt), 0.0), u) + _dhi(jnp.where(causal, _dnt(rt, kt), 0.0), v)
    cl_end = jnp.sum(lw, axis=1, keepdims=True)
    dec_end = jnp.exp(cl_end - cl)
    s1 = s0 * jnp.exp(cl_end) + _dtn(u, b * dec_end) + _dtn(v, k2 * dec_end)
    m = (lax.broadcasted_iota(jnp.int32, (1, LANES), 1) < RWKV_HEAD_DIM).astype(F32)
    mean = jnp.sum(y, axis=-1, keepdims=True) * (1.0 / RWKV_HEAD_DIM)
    yc = (y - mean) * m
    var = jnp.sum(yc * yc, axis=-1, keepdims=True) * (1.0 / RWKV_HEAD_DIM)
    yn = yc * lax.rsqrt(var + RWKV_GN_EPS) * ln_w + ln_b
    y2 = yn + jnp.sum(r * k2 * r_k, axis=-1, keepdims=True) * v
    return y2 * g, s1


def _gdn_head(xq, xk, xv, z, ab, s0, cw, gp, oha, ohb):
    c = z.shape[1]
    n_heads = s0.shape[0]
    n_chunks = z.shape[0] // n_heads
    ri, ci = _iotas(c)
    cw = jnp.concatenate([cw] * n_chunks, axis=0) if n_chunks > 1 else cw

    def conv(xs, w):
        out = xs[0] * w[:, GDN_CONV - 1:GDN_CONV]
        for s in range(1, GDN_CONV):
            out = out + xs[s] * w[:, GDN_CONV - 1 - s:GDN_CONV - s]
        return out

    q = _silu(conv(xq, cw[:, 0]))
    k = _silu(conv(xk, cw[:, 1]))
    v = _silu(conv(xv, cw[:, 2]))
    q = q * lax.rsqrt(jnp.sum(q * q, axis=-1, keepdims=True) + L2_EPS) * (GDN_HEAD_DIM ** -0.5)
    k = k * lax.rsqrt(jnp.sum(k * k, axis=-1, keepdims=True) + L2_EPS)
    gg = -jnp.exp(gp[0:1]) * _softplus(ab + gp[1:2])
    beta = jnp.sum(_sigmoid(ab) * ohb, axis=-1, keepdims=True)
    causal = ri >= ci
    strict = ri > ci
    eye = (ri == ci).astype(F32)
    gcm = _cumsum_rows(gg * oha)
    gc = jnp.sum(gcm, axis=-1, keepdims=True)
    gc_row = _lane_sum_as_row(gcm)
    dec = jnp.where(causal, jnp.exp(jnp.where(causal, gc - gc_row, 0.0)), 0.0)
    kb = k * beta
    vb = v * beta
    lm = jnp.where(strict, _dnt(kb, k) * dec, 0.0)
    tinv = _unit_lower_inverse(-lm, eye)
    egc = jnp.exp(gc)
    u = _dhi(tinv, vb)
    wk = _dhi(tinv, kb * egc)
    attn = jnp.where(causal, _dnt(q, k) * dec, 0.0)
    g_last = gc[:, c - 1:c, :]
    q_dec = q * egc
    k_dec = k * jnp.exp(g_last - gc)
    e_last = jnp.exp(g_last)
    state, outs = s0, []
    for i in range(n_chunks):
        sl = slice(i * n_heads, (i + 1) * n_heads)
        v_new = u[sl] - _dhi(wk[sl], state)
        outs.append(_dhi(q_dec[sl], state) + _dhi(attn[sl], v_new))
        state = state * e_last[sl] + _dtn(k_dec[sl], v_new)
    o = jnp.concatenate(outs, axis=0) if n_chunks > 1 else outs[0]
    return _rms(o, gp[2:3]) * _silu(z), state


def _head_id(grp, i, per_step, heads):
    return i if per_step == heads else grp * per_step + i


def _head_range(grp, per_step, heads):
    return slice(None) if per_step == heads else pl.ds(grp * per_step, per_step)


def _stack_heads(ref, n, fn=None):
    parts = []
    for i in range(n):
        sl = slice(i * LANES, (i + 1) * LANES)
        v = ref[:, sl]
        parts.append(v if fn is None else fn(v, sl))
    return jnp.stack(parts)


def _prev_rows_spec(width, col):
    per = CHUNK // SUBLANES
    return pl.BlockSpec((SUBLANES, width), lambda n, h: (jnp.maximum(n * per - 1, 0), col(h)))


def _rwkv_specs(nmap):
    hb, groups = RWKV_HB, RWKV_HEADS // RWKV_HB
    cb = OFF_RKV // (hb * LANES)
    specs = []
    for j in range(3):
        specs.append(pl.BlockSpec((CHUNK, hb * LANES), lambda n, g, j=j: (nmap(n), cb + j * groups + g)))
    specs.append(pl.BlockSpec((CHUNK, 2 * LANES), lambda n, g: (nmap(n), OFF_LO // (2 * LANES))))
    per = CHUNK // SUBLANES
    for j in range(3):
        specs.append(pl.BlockSpec((SUBLANES, hb * LANES),
                                  lambda n, g, j=j: (jnp.maximum(nmap(n) * per - 1, 0), cb + j * groups + g)))
    specs.append(pl.BlockSpec((SUBLANES, 2 * LANES), lambda n, g: (jnp.maximum(nmap(n) * per - 1, 0), OFF_LO // (2 * LANES))))
    specs.append(pl.BlockSpec((hb, 16, LANES), lambda n, g: (g, 0, 0)))
    specs.append(pl.BlockSpec((1, 2 * LANES), lambda n, g: (0, 0)))
    specs.append(pl.BlockSpec((hb, 3, 2 * LANES, LANES), lambda n, g: (g, 0, 0, 0)))
    return specs


def _rwkv_forward(p_cat, ppack, mulo, wl):
    t = p_cat.shape[0]
    n_chunks = t // CHUNK

    def body(pr, pk, pv, plo, hr, hk, hv, hlo, pp, ml, w, out, st_out, s_scr):
        n, grp = pl.program_id(0), pl.program_id(1)

        hsl = _head_range(grp, RWKV_HB, RWKV_HEADS)

        @pl.when(n == 0)
        def _():
            s_scr[hsl] = jnp.zeros((RWKV_HB, LANES, LANES), F32)

        live = (n > 0).astype(F32)
        lo = plo[...]
        lo_prev = _shift_rows(lo, hlo[...] * live, 1)
        cur = [_stack_heads(x, RWKV_HB) for x in (pr, pk, pv)]
        prev = [_stack_heads(x, RWKV_HB, lambda v, sl, hx=hx: _shift_rows(v, hx[:, sl] * live, 1))
                for x, hx in ((pr, hr), (pk, hk), (pv, hv))]
        s0 = s_scr[hsl]
        st_out[...] = s0
        o, s1 = _rwkv_head(*cur, lo, *prev, lo_prev, s0, pp[...], ml[...], w[...])
        for i in range(RWKV_HB):
            out[:, i * LANES:(i + 1) * LANES] = o[i].astype(out.dtype)
        s_scr[hsl] = s1

    return pl.pallas_call(
        body, name="rwkv_fwd", grid=(n_chunks, RWKV_HEADS // RWKV_HB),
        in_specs=_rwkv_specs(lambda n: n),
        out_specs=(pl.BlockSpec((CHUNK, RWKV_HB * LANES), lambda n, g: (n, g)),
                   pl.BlockSpec((None, RWKV_HB, LANES, LANES), lambda n, g: (n, g, 0, 0))),
        out_shape=(jax.ShapeDtypeStruct((t, RWKV_HEADS * LANES), BF16),
                   jax.ShapeDtypeStruct((n_chunks, RWKV_HEADS, LANES, LANES), F32)),
        scratch_shapes=[pltpu.VMEM((RWKV_HEADS, LANES, LANES), F32)],
        compiler_params=_cparams(("arbitrary", "arbitrary")),
    )(p_cat, p_cat, p_cat, p_cat, p_cat, p_cat, p_cat, p_cat, ppack, mulo, wl)


def _rwkv_backward(p_cat, ppack, mulo, wl, states, d_out):
    t = p_cat.shape[0]
    n_chunks = t // CHUNK
    last = n_chunks - 1

    def body(pr, pk, pv, plo, hr, hk, hv, hlo, pp, ml, w, st, dy, dpr, dpk, dpv, dplo, dpp, dml, dw, ds_scr, car_scr, carlo_scr):
        n, grp = pl.program_id(0), pl.program_id(1)

        hsl = _head_range(grp, RWKV_HB, RWKV_HEADS)
        gi = _head_id(grp, 0, 1, RWKV_HEADS // RWKV_HB)

        @pl.when(n == 0)
        def _():
            ds_scr[hsl] = jnp.zeros((RWKV_HB, LANES, LANES), F32)
            car_scr[hsl] = jnp.zeros((RWKV_HB, 3 * SUBLANES, LANES), F32)
            carlo_scr[gi] = jnp.zeros((SUBLANES, 2 * LANES), F32)

        @pl.when((n == 0) & (grp == 0))
        def _():
            dpp[...] = jnp.zeros(dpp.shape, F32)
            dml[...] = jnp.zeros(dml.shape, F32)
            dw[...] = jnp.zeros(dw.shape, F32)

        live = (n < last).astype(F32)
        lo = plo[...]
        lo_prev = _shift_rows(lo, hlo[...] * live, 1)
        cur = [_stack_heads(x, RWKV_HB) for x in (pr, pk, pv)]
        prev = [_stack_heads(x, RWKV_HB, lambda v, sl, hx=hx: _shift_rows(v, hx[:, sl] * live, 1))
                for x, hx in ((pr, hr), (pk, hk), (pv, hv))]
        _, vjp = jax.vjp(_rwkv_head, *cur, lo, *prev, lo_prev, st[...], pp[...], ml[...], w[...])
        g = vjp((_stack_heads(dy, RWKV_HB), ds_scr[hsl]))
        outs = (dpr, dpk, dpv)
        for i in range(RWKV_HB):
            sl = slice(i * LANES, (i + 1) * LANES)
            h = _head_id(grp, i, RWKV_HB, RWKV_HEADS)
            car = car_scr[h]
            for j in range(3):
                tot = g[j][i] + _unshift_rows(g[4 + j][i], car[SUBLANES * j:SUBLANES * (j + 1), :], 1)
                outs[j][:, sl] = tot.astype(outs[j].dtype)
                car_scr[h, SUBLANES * j:SUBLANES * (j + 1), :] = g[4 + j][i][0:SUBLANES, :]
        dlo = g[3] + _unshift_rows(g[7], carlo_scr[gi], 1)
        carlo_scr[gi] = g[7][0:SUBLANES, :]
        ds_scr[hsl] = g[8]
        dpp[hsl] += g[9]
        dml[0, 0:1, :] += g[10]
        dw[hsl] += g[11]

        @pl.when(grp == 0)
        def _():
            dplo[...] = dlo

        @pl.when(grp > 0)
        def _():
            dplo[...] += dlo

    rev = lambda n: last - n
    in_specs = _rwkv_specs(rev) + [
        pl.BlockSpec((None, RWKV_HB, LANES, LANES), lambda n, g: (rev(n), g, 0, 0)),
        pl.BlockSpec((CHUNK, RWKV_HB * LANES), lambda n, g: (rev(n), g)),
    ]
    hw = RWKV_HEADS * LANES
    return pl.pallas_call(
        body, name="rwkv_bwd", grid=(n_chunks, RWKV_HEADS // RWKV_HB),
        in_specs=in_specs,
        out_specs=(pl.BlockSpec((CHUNK, RWKV_HB * LANES), lambda n, g: (rev(n), g)),
                   pl.BlockSpec((CHUNK, RWKV_HB * LANES), lambda n, g: (rev(n), g)),
                   pl.BlockSpec((CHUNK, RWKV_HB * LANES), lambda n, g: (rev(n), g)),
                   pl.BlockSpec((CHUNK, 2 * LANES), lambda n, h: (rev(n), 0)),
                   pl.BlockSpec((RWKV_HEADS, 16, LANES), lambda n, h: (0, 0, 0)),
                   pl.BlockSpec((RWKV_HEADS, SUBLANES, 2 * LANES), lambda n, h: (0, 0, 0)),
                   pl.BlockSpec((RWKV_HEADS, 3, 2 * LANES, LANES), lambda n, h: (0, 0, 0, 0))),
        out_shape=(jax.ShapeDtypeStruct((t, hw), BF16), jax.ShapeDtypeStruct((t, hw), BF16), jax.ShapeDtypeStruct((t, hw), BF16),
                   jax.ShapeDtypeStruct((t, 2 * LANES), F32),
                   jax.ShapeDtypeStruct((RWKV_HEADS, 16, LANES), F32),
                   jax.ShapeDtypeStruct((RWKV_HEADS, SUBLANES, 2 * LANES), F32),
                   jax.ShapeDtypeStruct((RWKV_HEADS, 3, 2 * LANES, LANES), F32)),
        scratch_shapes=[pltpu.VMEM((RWKV_HEADS, LANES, LANES), F32),
                        pltpu.VMEM((RWKV_HEADS, 3 * SUBLANES, LANES), F32),
                        pltpu.VMEM((RWKV_HEADS, SUBLANES, 2 * LANES), F32)],
        compiler_params=_cparams(("arbitrary", "arbitrary")),
    )(p_cat, p_cat, p_cat, p_cat, p_cat, p_cat, p_cat, p_cat, ppack, mulo, wl, states, d_out)


def _gdn_specs(nmap):
    per = GDN_TILE // SUBLANES
    hb, groups = GDN_HB, GDN_HEADS // GDN_HB
    cb = OFF_QKV // (hb * LANES)
    specs = []
    for j in range(3):
        specs.append(pl.BlockSpec((GDN_TILE, hb * LANES), lambda n, g, j=j: (nmap(n), cb + j * groups + g)))
    for j in range(3):
        specs.append(pl.BlockSpec((SUBLANES, hb * LANES),
                                  lambda n, g, j=j: (jnp.maximum(nmap(n) * per - 1, 0), cb + j * groups + g)))
    specs.append(pl.BlockSpec((GDN_TILE, hb * LANES), lambda n, g: (nmap(n), OFF_Z // (hb * LANES) + g)))
    specs.append(pl.BlockSpec((GDN_TILE, LANES), lambda n, g: (nmap(n), OFF_AB // LANES)))
    specs.append(pl.BlockSpec((hb, 3, SUBLANES, LANES), lambda n, g: (g, 0, 0, 0)))
    specs.append(pl.BlockSpec((SUBLANES, LANES), lambda n, g: (0, 0)))
    return specs


def _conv_taps(x, halo):
    return (x,) + tuple(_shift_rows(x, halo, s) for s in range(1, GDN_CONV))


def _onehots(grp):
    nb = GDN_STEP_CHUNKS * GDN_HB
    lane = lax.broadcasted_iota(jnp.int32, (nb, 1, LANES), 2)
    head = lax.broadcasted_iota(jnp.int32, (nb, 1, LANES), 0) % GDN_HB + _head_id(grp, 0, GDN_HB, GDN_HEADS)
    return (lane == head).astype(F32), (lane == GDN_HEADS + head).astype(F32)


def _chunk_batch(tiles):
    return jnp.stack([t_[i * CHUNK:(i + 1) * CHUNK, :] for i in range(GDN_STEP_CHUNKS) for t_ in tiles])


def _head_tiles(batch):
    return [jnp.concatenate([batch[i * GDN_HB + h] for i in range(GDN_STEP_CHUNKS)], axis=0) for h in range(GDN_HB)]


def _lane_block(ref, h):
    return ref[:, h * LANES:(h + 1) * LANES]


def _gdn_taps(refs, halos, live):
    out = []
    for x, hx in zip(refs, halos):
        per_head = [_conv_taps(_lane_block(x, h), _lane_block(hx, h) * live) for h in range(GDN_HB)]
        out.append(tuple(_chunk_batch([per_head[h][s] for h in range(GDN_HB)]) for s in range(GDN_CONV)))
    return out


def _gdn_forward(p_cat, cwpack, gpar):
    t = p_cat.shape[0]
    n_chunks = t // GDN_TILE

    def body(xq, xk, xv, hq, hk, hv, z, ab, cw, gp, out, st_out, s_scr):
        n, grp = pl.program_id(0), pl.program_id(1)

        hsl = _head_range(grp, GDN_HB, GDN_HEADS)

        @pl.when(n == 0)
        def _():
            s_scr[hsl] = jnp.zeros((GDN_HB, LANES, LANES), F32)

        live = (n > 0).astype(F32)
        oha, ohb = _onehots(grp)
        s0 = s_scr[hsl]
        st_out[...] = s0
        taps = _gdn_taps((xq, xk, xv), (hq, hk, hv), live)
        zb = _chunk_batch([_lane_block(z, h) for h in range(GDN_HB)])
        abb = _chunk_batch([ab[...]] * GDN_HB)
        o, s1 = _gdn_head(*taps, zb, abb, s0, cw[...], gp[...], oha, ohb)
        for h, tile in enumerate(_head_tiles(o)):
            out[:, h * LANES:(h + 1) * LANES] = tile.astype(out.dtype)
        s_scr[hsl] = s1

    return pl.pallas_call(
        body, name="gdn_fwd", grid=(n_chunks, GDN_HEADS // GDN_HB),
        in_specs=_gdn_specs(lambda n: n),
        out_specs=(pl.BlockSpec((GDN_TILE, GDN_HB * LANES), lambda n, g: (n, g)),
                   pl.BlockSpec((None, GDN_HB, LANES, LANES), lambda n, g: (n, g, 0, 0))),
        out_shape=(jax.ShapeDtypeStruct((t, GDN_WIDTH), BF16),
                   jax.ShapeDtypeStruct((n_chunks, GDN_HEADS, LANES, LANES), F32)),
        scratch_shapes=[pltpu.VMEM((GDN_HEADS, LANES, LANES), F32)],
        compiler_params=_cparams(("arbitrary", "arbitrary")),
    )(p_cat, p_cat, p_cat, p_cat, p_cat, p_cat, p_cat, p_cat, cwpack, gpar)


def _gdn_backward(p_cat, cwpack, gpar, states, d_out):
    t = p_cat.shape[0]
    n_chunks = t // GDN_TILE
    last = n_chunks - 1

    def body(xq, xk, xv, hq, hk, hv, z, ab, cw, gp, st, dy, dq, dk, dv, dz, dab, dcw, dgp, ds_scr, car_scr):
        n, grp = pl.program_id(0), pl.program_id(1)

        hsl = _head_range(grp, GDN_HB, GDN_HEADS)

        @pl.when(n == 0)
        def _():
            ds_scr[hsl] = jnp.zeros((GDN_HB, LANES, LANES), F32)
            car_scr[hsl] = jnp.zeros((GDN_HB, 3 * GDN_CONV, SUBLANES, LANES), F32)

        @pl.when((n == 0) & (grp == 0))
        def _():
            dcw[...] = jnp.zeros(dcw.shape, F32)
            dgp[...] = jnp.zeros(dgp.shape, F32)

        live = (n < last).astype(F32)
        oha, ohb = _onehots(grp)
        fn = functools.partial(_gdn_head, oha=oha, ohb=ohb)
        taps = _gdn_taps((xq, xk, xv), (hq, hk, hv), live)
        zb = _chunk_batch([_lane_block(z, h) for h in range(GDN_HB)])
        abb = _chunk_batch([ab[...]] * GDN_HB)
        _, vjp = jax.vjp(fn, *taps, zb, abb, st[...], cw[...], gp[...])
        g = vjp((_chunk_batch([_lane_block(dy, h) for h in range(GDN_HB)]), ds_scr[hsl]))
        outs = (dq, dk, dv)
        tap_tiles = [[_head_tiles(g[j][s]) for s in range(GDN_CONV)] for j in range(3)]
        dz_tiles = _head_tiles(g[3])
        for i in range(GDN_HB):
            sl = slice(i * LANES, (i + 1) * LANES)
            h = _head_id(grp, i, GDN_HB, GDN_HEADS)
            for j in range(3):
                tot = tap_tiles[j][0][i]
                for s in range(1, GDN_CONV):
                    slot = j * GDN_CONV + s
                    tot = tot + _unshift_rows(tap_tiles[j][s][i], car_scr[h, slot], s)
                    car_scr[h, slot] = tap_tiles[j][s][i][0:SUBLANES, :]
                outs[j][:, sl] = tot.astype(outs[j].dtype)
            dz[:, sl] = dz_tiles[i].astype(dz.dtype)
        dab_tiles = _head_tiles(g[4])
        dab_sum = dab_tiles[0]
        for h in range(1, GDN_HB):
            dab_sum = dab_sum + dab_tiles[h]
        ds_scr[hsl] = g[5]
        dcw[hsl] += g[6]
        dgp[0] += g[7]

        @pl.when(grp == 0)
        def _():
            dab[...] = dab_sum

        @pl.when(grp > 0)
        def _():
            dab[...] += dab_sum

    rev = lambda n: last - n
    in_specs = _gdn_specs(rev) + [
        pl.BlockSpec((None, GDN_HB, LANES, LANES), lambda n, g: (rev(n), g, 0, 0)),
        pl.BlockSpec((GDN_TILE, GDN_HB * LANES), lambda n, g: (rev(n), g)),
    ]
    blk = pl.BlockSpec((GDN_TILE, GDN_HB * LANES), lambda n, g: (rev(n), g))
    return pl.pallas_call(
        body, name="gdn_bwd", grid=(n_chunks, GDN_HEADS // GDN_HB),
        in_specs=in_specs,
        out_specs=(blk, blk, blk, blk,
                   pl.BlockSpec((GDN_TILE, LANES), lambda n, h: (rev(n), 0)),
                   pl.BlockSpec((GDN_HEADS, 3, SUBLANES, LANES), lambda n, h: (0, 0, 0, 0)),
                   pl.BlockSpec((GDN_HEADS, SUBLANES, LANES), lambda n, h: (0, 0, 0))),
        out_shape=(jax.ShapeDtypeStruct((t, GDN_WIDTH), BF16), jax.ShapeDtypeStruct((t, GDN_WIDTH), BF16),
                   jax.ShapeDtypeStruct((t, GDN_WIDTH), BF16), jax.ShapeDtypeStruct((t, GDN_WIDTH), BF16),
                   jax.ShapeDtypeStruct((t, LANES), F32),
                   jax.ShapeDtypeStruct((GDN_HEADS, 3, SUBLANES, LANES), F32),
                   jax.ShapeDtypeStruct((GDN_HEADS, SUBLANES, LANES), F32)),
        scratch_shapes=[pltpu.VMEM((GDN_HEADS, LANES, LANES), F32),
                        pltpu.VMEM((GDN_HEADS, 3 * GDN_CONV, SUBLANES, LANES), F32)],
        compiler_params=_cparams(("arbitrary", "arbitrary")),
    )(p_cat, p_cat, p_cat, p_cat, p_cat, p_cat, p_cat, p_cat, cwpack, gpar, states, d_out)


def _pick(n, options):
    for o in options:
        if n % o == 0:
            return o
    raise ValueError(f"no tile for {n}")


MM_VMEM_BUDGET = 30 * 1024 * 1024
MM_MIN_STEPS = 8


def _mm_tiles(mode, m, n, k, out_bytes):
    tms = [t for t in (2048, 1024, 768, 512, 256, 128, 64) if m % t == 0 and (mode != 'tn' or t % LANES == 0)]
    tns = [t for t in (1408, 1024, 768, 512, 256, 128) if n % t == 0]
    tks = [t for t in (2048, 1920, 1408, 1024, 512, 256, 128, 64) if k % t == 0]
    best, best_key = None, None
    for tm in tms:
        for tn in tns:
            for tk in tks:
                nk = k // tk
                vmem = 2 * (tm * tk * 2 + tk * tn * 2 + tm * tn * out_bytes) + (tm * tn * 4 if nk > 1 else 0)
                steps = (m // tm) * (n // tn) * nk
                if vmem > MM_VMEM_BUDGET:
                    continue
                key = (steps >= MM_MIN_STEPS, tn if mode == 'tn' else 0, tm * tn * tk, -nk)
                if best_key is None or key > best_key:
                    best, best_key = (tm, tn, tk), key
    if best is None:
        raise ValueError(f"no matmul tile for {mode} {m}x{n}x{k}")
    return best


_MM_DIMS = {'nn': (((1,), (0,)), ((), ())), 'nt': (((1,), (1,)), ((), ())), 'tn': (((0,), (0,)), ((), ()))}


def _matmul(a, b, mode, out_dtype, name):
    if mode == 'nn':
        (m, k), (k2, n) = a.shape, b.shape
    elif mode == 'nt':
        (m, k), (n, k2) = a.shape, b.shape
    else:
        (k, m), (k2, n) = a.shape, b.shape
    assert k == k2, (a.shape, b.shape, mode)
    tm, tn, tk = _mm_tiles(mode, m, n, k, jnp.dtype(out_dtype).itemsize)
    nk = k // tk
    dims = _MM_DIMS[mode]

    def body(a_ref, b_ref, o_ref, acc_ref):
        kk = pl.program_id(2)
        part = lax.dot_general(a_ref[...], b_ref[...], dims, preferred_element_type=F32)
        if nk == 1:
            o_ref[...] = part.astype(o_ref.dtype)
            return

        @pl.when(kk == 0)
        def _():
            acc_ref[...] = part

        @pl.when((kk > 0) & (kk < nk - 1))
        def _():
            acc_ref[...] += part

        @pl.when(kk == nk - 1)
        def _():
            o_ref[...] = (acc_ref[...] + part).astype(o_ref.dtype)

    a_spec = pl.BlockSpec((tk, tm), lambda i, j, kk: (kk, i)) if mode == 'tn' else pl.BlockSpec((tm, tk), lambda i, j, kk: (i, kk))
    b_spec = pl.BlockSpec((tn, tk), lambda i, j, kk: (j, kk)) if mode == 'nt' else pl.BlockSpec((tk, tn), lambda i, j, kk: (kk, j))
    return pl.pallas_call(
        body, name=name, grid=(m // tm, n // tn, nk),
        in_specs=[a_spec, b_spec],
        out_specs=pl.BlockSpec((tm, tn), lambda i, j, kk: (i, j)),
        out_shape=jax.ShapeDtypeStruct((m, n), out_dtype),
        scratch_shapes=[pltpu.VMEM((tm, tn), F32)],
        compiler_params=_cparams(("parallel", "parallel", "arbitrary")),
    )(a, b)


ROW_TILE = 256


def _row_specs(rows, tm):
    return [pl.BlockSpec((tm, w), lambda i, ci=ci: (i, ci)) for (_, w, ci) in rows]


def _rw_forward(fn, rows, pars, outs, name):
    t = rows[0][0].shape[0]
    tm = min(ROW_TILE, t)
    nr, npar = len(rows), len(pars)

    def body(*refs):
        vals = [r[...].astype(F32) for r in refs[:nr]] + [p[...] for p in refs[nr:nr + npar]]
        res = fn(*vals)
        for o, v in zip(refs[nr + npar:], res):
            o[...] = v.astype(o.dtype)

    return pl.pallas_call(
        body, name=name, grid=(t // tm,),
        in_specs=_row_specs(rows, tm) + [pl.BlockSpec(p.shape, lambda i: (0, 0)) for p in pars],
        out_specs=tuple(pl.BlockSpec((tm, w), lambda i: (i, 0)) for (w, _) in outs),
        out_shape=tuple(jax.ShapeDtypeStruct((t, w), dt) for (w, dt) in outs),
        compiler_params=_cparams(("parallel",)),
    )(*[r[0] for r in rows], *pars)


def _rw_backward(fn, rows, pars, cots, drow_dtypes, name):
    t = rows[0][0].shape[0]
    tm = min(ROW_TILE, t)
    nr, npar, nc = len(rows), len(pars), len(cots)
    keep = [i for i, dt in enumerate(drow_dtypes) if dt is not None]

    def body(*refs):
        vals = [r[...].astype(F32) for r in refs[:nr]] + [p[...] for p in refs[nr:nr + npar]]
        cvals = tuple(c[...].astype(F32) for c in refs[nr + npar:nr + npar + nc])
        orefs = refs[nr + npar + nc:]
        _, vjp = jax.vjp(fn, *vals)
        g = vjp(cvals)
        for o, i in zip(orefs[:len(keep)], keep):
            o[...] = g[i].astype(o.dtype)
        first = pl.program_id(0) == 0
        for o, gi in zip(orefs[len(keep):], g[nr:]):
            @pl.when(first)
            def _(o=o, gi=gi):
                o[...] = gi

            @pl.when(jnp.logical_not(first))
            def _(o=o, gi=gi):
                o[...] += gi

    out_specs = [pl.BlockSpec((tm, rows[i][1]), lambda i_: (i_, 0)) for i in keep] + \
                [pl.BlockSpec(p.shape, lambda i_: (0, 0)) for p in pars]
    out_shape = [jax.ShapeDtypeStruct((t, rows[i][1]), drow_dtypes[i]) for i in keep] + \
                [jax.ShapeDtypeStruct(p.shape, F32) for p in pars]
    return pl.pallas_call(
        body, name=name, grid=(t // tm,),
        in_specs=_row_specs(rows, tm) + [pl.BlockSpec(p.shape, lambda i: (0, 0)) for p in pars] + _row_specs(cots, tm),
        out_specs=tuple(out_specs), out_shape=tuple(out_shape),
        compiler_params=_cparams(("arbitrary",)),
    )(*[r[0] for r in rows], *pars, *[c[0] for c in cots])


def _norm_fn(x, g):
    return (_rms(x, g),)


def _norm_skip_fn(x, g):
    return _rms(x, g), x


def _merge_fn(ga, gb, ya, yb):
    return (_sigmoid(ga) * ya + _sigmoid(gb) * yb,)


def _res_norm_fn(x, mo, g):
    x1 = x + mo
    return x1, _rms(x1, g)


def _loss_head(x1, fo, gf, target, name):
    t, d = x1.shape
    tm = min(ROW_TILE, t)

    def tile_loss(x2, g, tgt):
        err = _rms(x2, g) - tgt
        per_row = jnp.sum(err * err, axis=-1, keepdims=True) * (0.5 / d)
        return jnp.sum(per_row, axis=0, keepdims=True)

    def body(x1_ref, fo_ref, g_ref, t_ref, loss_ref, dx_ref, dxb_ref, dg_ref):
        x2 = x1_ref[...] + fo_ref[...]
        val, vjp = jax.vjp(functools.partial(tile_loss, tgt=t_ref[...]), x2, g_ref[...])
        dx2, dg = vjp(jnp.ones((1, 1), F32))
        dx_ref[...] = dx2
        dxb_ref[...] = dx2.astype(BF16)
        first = pl.program_id(0) == 0

        @pl.when(first)
        def _():
            loss_ref[...] = jnp.broadcast_to(val, loss_ref.shape)
            dg_ref[...] = dg

        @pl.when(jnp.logical_not(first))
        def _():
            loss_ref[...] += jnp.broadcast_to(val, loss_ref.shape)
            dg_ref[...] += dg

    row = pl.BlockSpec((tm, d), lambda i: (i, 0))
    vec = pl.BlockSpec((1, d), lambda i: (0, 0))
    return pl.pallas_call(
        body, name=name, grid=(t // tm,),
        in_specs=[row, row, vec, row],
        out_specs=(pl.BlockSpec((1, LANES), lambda i: (0, 0)), row, row, vec),
        out_shape=(jax.ShapeDtypeStruct((1, LANES), F32), jax.ShapeDtypeStruct((t, d), F32),
                   jax.ShapeDtypeStruct((t, d), BF16), jax.ShapeDtypeStruct((1, d), F32)),
        compiler_params=_cparams(("arbitrary",)),
    )(x1, fo, gf, target)


FFN_TILE_ROWS = 512
FFN_TILE_COLS = 256
FFN_COL_BLOCKS = FFN_HIDDEN // FFN_TILE_COLS


def _conv3_past(x, halo, w):
    rows = lax.broadcasted_iota(jnp.int32, x.shape, 0)
    x1 = jnp.where(rows == 0, halo[7:8, :], pltpu.roll(x, 1, 0))
    x2 = jnp.where(rows == 0, halo[6:7, :], jnp.where(rows == 1, halo[7:8, :], pltpu.roll(x, 2, 0)))
    return x * w[2:3] + x1 * w[1:2] + x2 * w[0:1], x1, x2


def _ffn_in_specs(tm, imap, jmap):
    per = tm // SUBLANES
    tile = lambda off: pl.BlockSpec((tm, FFN_TILE_COLS), lambda *g: (imap(*g), off + jmap(*g) % FFN_COL_BLOCKS))
    halo = lambda off: pl.BlockSpec((SUBLANES, FFN_TILE_COLS),
                                    lambda *g: (jnp.maximum(imap(*g) * per - 1, 0), off + jmap(*g) % FFN_COL_BLOCKS))
    wsp = lambda off: pl.BlockSpec((FFN_CONV, FFN_TILE_COLS), lambda *g: (0, off + jmap(*g) % FFN_COL_BLOCKS))
    return [tile(0), halo(0), wsp(0), tile(FFN_COL_BLOCKS), halo(FFN_COL_BLOCKS), wsp(FFN_COL_BLOCKS)]


def _ffn_act_forward(hpre, cw):
    t = hpre.shape[0]
    tm = min(FFN_TILE_ROWS, t)

    def body(hg, pg, wg, hu, pu, wu, out):
        live = (pl.program_id(0) > 0).astype(F32)
        cg, _, _ = _conv3_past(hg[...], pg[...] * live, wg[...])
        cu, _, _ = _conv3_past(hu[...], pu[...] * live, wu[...])
        out[...] = (_silu(cg) * cu).astype(out.dtype)

    return pl.pallas_call(
        body, name="ffn_act_fwd", grid=(t // tm, FFN_COL_BLOCKS),
        in_specs=_ffn_in_specs(tm, lambda i, j: i, lambda i, j: j),
        out_specs=pl.BlockSpec((tm, FFN_TILE_COLS), lambda i, j: (i, j)),
        out_shape=jax.ShapeDtypeStruct((t, FFN_HIDDEN), BF16),
        compiler_params=_cparams(("parallel", "parallel")),
    )(hpre, hpre, cw, hpre, hpre, cw)


def _conv3_future(d, nxt, w):
    tm = d.shape[0]
    rows = lax.broadcasted_iota(jnp.int32, d.shape, 0)
    d1 = jnp.where(rows == tm - 1, nxt[0:1, :], pltpu.roll(d, tm - 1, 0))
    d2 = jnp.where(rows == tm - 1, nxt[1:2, :], jnp.where(rows == tm - 2, nxt[0:1, :], pltpu.roll(d, tm - 2, 0)))
    return d * w[2:3] + d1 * w[1:2] + d2 * w[0:1]


def _ffn_backward(hpre, cw, dact):
    t = hpre.shape[0]
    tm = min(FFN_TILE_ROWS, t)
    n_tiles = t // tm
    per = tm // SUBLANES

    def d_conv_out(cg, cu, d):
        s = _sigmoid(cg)
        return d * cu * s * (1.0 + cg * (1.0 - s)), d * cg * s

    def body(hg, pg, ng, wg, hu, pu, nu, wu, da, dan, dhg, dhu, dwg, dwu):
        i = pl.program_id(1)
        live_prev = (i > 0).astype(F32)
        live_next = (i < n_tiles - 1).astype(F32)
        xg, xu = hg[...], hu[...]
        cg, g1, g2 = _conv3_past(xg, pg[...] * live_prev, wg[...])
        cu, u1, u2 = _conv3_past(xu, pu[...] * live_prev, wu[...])
        dg, du = d_conv_out(cg, cu, da[...])
        cgn, _, _ = _conv3_past(ng[...], hg[tm - SUBLANES:tm, :], wg[...])
        cun, _, _ = _conv3_past(nu[...], hu[tm - SUBLANES:tm, :], wu[...])
        dgn, dun = d_conv_out(cgn, cun, dan[...] * live_next)
        dhg[...] = _conv3_future(dg, dgn, wg[...]).astype(dhg.dtype)
        dhu[...] = _conv3_future(du, dun, wu[...]).astype(dhu.dtype)
        sums_g = [jnp.sum(xs * dg, axis=0, keepdims=True) for xs in (g2, g1, xg)]
        sums_u = [jnp.sum(xs * du, axis=0, keepdims=True) for xs in (u2, u1, xu)]

        @pl.when(i == 0)
        def _():
            for r_ in range(FFN_CONV):
                dwg[r_:r_ + 1, :] = sums_g[r_]
                dwu[r_:r_ + 1, :] = sums_u[r_]

        @pl.when(i > 0)
        def _():
            for r_ in range(FFN_CONV):
                dwg[r_:r_ + 1, :] += sums_g[r_]
                dwu[r_:r_ + 1, :] += sums_u[r_]

    nb = FFN_COL_BLOCKS
    nxt = lambda i: jnp.minimum((i + 1) * per, t // SUBLANES - 1)
    prv = lambda i: jnp.maximum(i * per - 1, 0)
    half = lambda off: [pl.BlockSpec((tm, FFN_TILE_COLS), lambda j, i: (i, off + j)),
                        pl.BlockSpec((SUBLANES, FFN_TILE_COLS), lambda j, i: (prv(i), off + j)),
                        pl.BlockSpec((SUBLANES, FFN_TILE_COLS), lambda j, i: (nxt(i), off + j)),
                        pl.BlockSpec((FFN_CONV, FFN_TILE_COLS), lambda j, i: (0, off + j))]
    tile = pl.BlockSpec((tm, FFN_TILE_COLS), lambda j, i: (i, j))
    taps = pl.BlockSpec((FFN_CONV, FFN_TILE_COLS), lambda j, i: (0, j))
    return pl.pallas_call(
        body, name="ffn_bwd", grid=(nb, n_tiles),
        in_specs=half(0) + half(nb) + [tile, pl.BlockSpec((SUBLANES, FFN_TILE_COLS), lambda j, i: (nxt(i), j))],
        out_specs=(tile, tile, taps, taps),
        out_shape=(jax.ShapeDtypeStruct((t, FFN_HIDDEN), BF16), jax.ShapeDtypeStruct((t, FFN_HIDDEN), BF16),
                   jax.ShapeDtypeStruct((FFN_CONV, FFN_HIDDEN), F32), jax.ShapeDtypeStruct((FFN_CONV, FFN_HIDDEN), F32)),
        compiler_params=_cparams(("parallel", "arbitrary")),
    )(hpre, hpre, hpre, cw, hpre, hpre, hpre, cw, dact, dact)


def _my_place():
    x, y, c = lax.axis_index("x"), lax.axis_index("y"), lax.axis_index("c")
    return x, y, c, 4 * x + 2 * y + c


N_CHIPS = 4


def _remote(src, dst, send_sem, recv_sem, dev):
    return pltpu.make_async_remote_copy(src_ref=src, dst_ref=dst, send_sem=send_sem, recv_sem=recv_sem, device_id=dev,
                                        device_id_type=pl.DeviceIdType.MESH)


def _chip_peer(x, y, k):
    return x ^ ((k >> 1) & 1), y ^ (k & 1)


def _all_gather_two_level(shard, name):
    r, w = shard.shape

    def body(src, out, send_sems, recv_sems, local_sem):
        x, y, c, me = _my_place()
        sibling = (x, y, 1 - c)
        mine = pltpu.make_async_copy(src, out.at[me], local_sem)
        mine.start()
        first = [_remote(src, out.at[me], send_sems.at[0], recv_sems.at[0], sibling)]
        for k in range(1, N_CHIPS):
            px, py = _chip_peer(x, y, k)
            first.append(_remote(src, out.at[me], send_sems.at[k], recv_sems.at[k], (px, py, c)))
        for cp in first:
            cp.start()
        passed = []
        for k in range(1, N_CHIPS):
            px, py = _chip_peer(x, y, k)
            landed = out.at[me ^ (2 * k)]
            _remote(src, landed, send_sems.at[k], recv_sems.at[k], (px, py, c)).wait_recv()
            fwd = _remote(landed, landed, send_sems.at[N_CHIPS - 1 + k], recv_sems.at[N_CHIPS - 1 + k], sibling)
            fwd.start()
            passed.append(fwd)
        _remote(src, out.at[me ^ 1], send_sems.at[0], recv_sems.at[0], sibling).wait_recv()
        for k in range(1, N_CHIPS):
            got = out.at[(me ^ 1) ^ (2 * k)]
            _remote(got, got, send_sems.at[N_CHIPS - 1 + k], recv_sems.at[N_CHIPS - 1 + k], sibling).wait_recv()
        for cp in first + passed:
            cp.wait_send()
        mine.wait()

    return pl.pallas_call(
        body, name=name,
        in_specs=[pl.BlockSpec(memory_space=pl.ANY)],
        out_specs=pl.BlockSpec(memory_space=pl.ANY),
        out_shape=jax.ShapeDtypeStruct((N_DEV, r, w), shard.dtype),
        scratch_shapes=[pltpu.SemaphoreType.DMA((N_DEV - 1,)), pltpu.SemaphoreType.DMA((N_DEV - 1,)), pltpu.SemaphoreType.DMA],
    )(shard)


def _device_peer(x, y, c, k):
    px, py, pc = x ^ ((k >> 2) & 1), y ^ ((k >> 1) & 1), c ^ (k & 1)
    return (px, py, pc), 4 * px + 2 * py + pc


_HBM = pl.BlockSpec(memory_space=pltpu.HBM)
_SEM = pl.BlockSpec(memory_space=pltpu.SEMAPHORE)


def _gather_start(shard, name):
    def body(src, land, send_sems, recv_sems, src_thru, land_thru, token):
        x, y, c, me = _my_place()
        for k in range(1, N_DEV):
            dev, _ = _device_peer(x, y, c, k)
            _remote(src, land.at[me], send_sems.at[k], recv_sems.at[k], dev).start()
        token[...] = jnp.zeros_like(token)

    landing = lax.empty((N_DEV,) + shard.shape, shard.dtype)
    return pl.pallas_call(
        body, name=name,
        out_shape=(pltpu.SemaphoreType.DMA((N_DEV,)), pltpu.SemaphoreType.DMA((N_DEV,)), pltpu.HBM(shard.shape, shard.dtype),
                   pltpu.HBM(landing.shape, landing.dtype), jax.ShapeDtypeStruct((SUBLANES, LANES), F32)),
        in_specs=(_HBM, _HBM), out_specs=(_SEM, _SEM, _HBM, _HBM, pl.BlockSpec(memory_space=pltpu.VMEM)),
        input_output_aliases={0: 2, 1: 3},
        compiler_params=pltpu.CompilerParams(has_side_effects=pltpu.SideEffectType.DATAFLOW_SIDE_EFFECTING),
    )(pltpu.with_memory_space_constraint(shard, pltpu.HBM), pltpu.with_memory_space_constraint(landing, pltpu.HBM))


def _gather_wait(send_sems, recv_sems, shard, landing, after, name):
    n_after = len(after)

    def body(*refs):
        src, land, send_sems, recv_sems = refs[:4]
        x, y, c, _ = _my_place()
        for k in range(1, N_DEV):
            dev, idx = _device_peer(x, y, c, k)
            cp = _remote(src, land.at[idx], send_sems.at[k], recv_sems.at[k], dev)
            cp.wait_send()
            cp.wait_recv()

    return pl.pallas_call(
        body, name=name,
        out_shape=(pltpu.HBM(shard.shape, shard.dtype), pltpu.HBM(landing.shape, landing.dtype)),
        in_specs=(_HBM, _HBM, _SEM, _SEM) + (pl.BlockSpec(memory_space=pl.ANY),) * n_after, out_specs=(_HBM, _HBM),
        input_output_aliases={0: 0, 1: 1},
        compiler_params=pltpu.CompilerParams(has_side_effects=pltpu.SideEffectType.DATAFLOW_SIDE_EFFECTING),
    )(shard, landing, send_sems, recv_sems, *after)[1]


def _slab_push_start(slabs, name):
    na = len(slabs)

    def body(*refs):
        srcs, lands = refs[:na], refs[na:2 * na]
        send_sems, recv_sems = refs[2 * na], refs[2 * na + 1]
        token = refs[-1]
        x, y, c, me = _my_place()
        for i in range(na):
            for k in range(1, N_DEV):
                dev, idx = _device_peer(x, y, c, k)
                s = i * N_DEV + k
                _remote(srcs[i].at[idx], lands[i].at[me], send_sems.at[s], recv_sems.at[s], dev).start()
        token[...] = jnp.zeros_like(token)

    hbm_shapes = [pltpu.HBM(a.shape, a.dtype) for a in slabs]
    ins = [pltpu.with_memory_space_constraint(a, pltpu.HBM) for a in slabs]
    ins += [pltpu.with_memory_space_constraint(lax.empty(a.shape, a.dtype), pltpu.HBM) for a in slabs]
    out = pl.pallas_call(
        body, name=name,
        out_shape=(pltpu.SemaphoreType.DMA((na * N_DEV,)), pltpu.SemaphoreType.DMA((na * N_DEV,)), *hbm_shapes, *hbm_shapes,
                   jax.ShapeDtypeStruct((SUBLANES, LANES), F32)),
        in_specs=(_HBM,) * (2 * na), out_specs=(_SEM, _SEM) + (_HBM,) * (2 * na) + (pl.BlockSpec(memory_space=pltpu.VMEM),),
        input_output_aliases={i: 2 + i for i in range(2 * na)},
        compiler_params=pltpu.CompilerParams(has_side_effects=pltpu.SideEffectType.DATAFLOW_SIDE_EFFECTING),
    )(*ins)
    return out[0], out[1], list(out[2:2 + na]), list(out[2 + na:2 + 2 * na]), out[-1]


def _slab_push_wait(send_sems, recv_sems, slabs, landings, after, name):
    na = len(slabs)

    def body(*refs):
        srcs, lands = refs[:na], refs[na:2 * na]
        send_sems, recv_sems = refs[2 * na], refs[2 * na + 1]
        x, y, c, me = _my_place()
        for i in range(na):
            for k in range(1, N_DEV):
                dev, idx = _device_peer(x, y, c, k)
                s = i * N_DEV + k
                cp = _remote(srcs[i].at[idx], lands[i].at[idx], send_sems.at[s], recv_sems.at[s], dev)
                cp.wait_send()
                cp.wait_recv()

    hbm_shapes = tuple(pltpu.HBM(a.shape, a.dtype) for a in slabs)
    out = pl.pallas_call(
        body, name=name, out_shape=hbm_shapes + hbm_shapes,
        in_specs=(_HBM,) * (2 * na) + (_SEM, _SEM) + (pl.BlockSpec(memory_space=pl.ANY),) * len(after),
        out_specs=(_HBM,) * (2 * na), input_output_aliases={i: i for i in range(2 * na)},
        compiler_params=pltpu.CompilerParams(has_side_effects=pltpu.SideEffectType.DATAFLOW_SIDE_EFFECTING),
    )(*slabs, *landings, send_sems, recv_sems, *after)
    return list(out[na:])


def _pair_exchange(arrays, name):
    na = len(arrays)

    def body(*refs):
        srcs, dsts, (send_sems, recv_sems) = refs[:na], refs[na:2 * na], refs[2 * na:]
        x, y, c, _ = _my_place()
        sibling = (x, y, 1 - c)
        copies = []
        for i in range(na):
            for q in range(N_CHIPS):
                s = i * N_CHIPS + q
                copies.append(_remote(srcs[i].at[2 * q + 1 - c], dsts[i].at[q], send_sems.at[s], recv_sems.at[s], sibling))
        for cp in copies:
            cp.start()
        for cp in copies:
            cp.wait_recv()
        for cp in copies:
            cp.wait_send()

    hbm = pl.BlockSpec(memory_space=pl.ANY)
    return pl.pallas_call(
        body, name=name, in_specs=[hbm] * na, out_specs=tuple([hbm] * na),
        out_shape=tuple(jax.ShapeDtypeStruct((N_CHIPS,) + a.shape[1:], a.dtype) for a in arrays),
        scratch_shapes=[pltpu.SemaphoreType.DMA((na * N_CHIPS,)), pltpu.SemaphoreType.DMA((na * N_CHIPS,))],
    )(*arrays)


ELEMENTWISE_COLS = 256


def _pair_sum(slabs, recv, core, out_dtype, name):
    _, r, w = slabs.shape
    tc = ELEMENTWISE_COLS

    def body(core_ref, mine, theirs, out):
        out[...] = (mine[...] + theirs[...]).astype(out.dtype)

    grid_spec = pltpu.PrefetchScalarGridSpec(
        num_scalar_prefetch=1, grid=(N_CHIPS, w // tc),
        in_specs=[pl.BlockSpec((None, r, tc), lambda q, j, core_ref: (2 * q + core_ref[0], 0, j)),
                  pl.BlockSpec((None, r, tc), lambda q, j, core_ref: (q, 0, j))],
        out_specs=pl.BlockSpec((None, r, tc), lambda q, j, core_ref: (q, 0, j)))
    return pl.pallas_call(body, name=name, grid_spec=grid_spec,
                          out_shape=jax.ShapeDtypeStruct((N_CHIPS, r, w), out_dtype),
                          compiler_params=_cparams(("parallel", "parallel")))(core, slabs, recv)


def _chip_exchange(arrays, name):
    na = len(arrays)

    def body(*refs):
        srcs, dsts, (send_sems, recv_sems, local_sems) = refs[:na], refs[na:2 * na], refs[2 * na:]
        x, y, c, _ = _my_place()
        chip = 2 * x + y
        own = [pltpu.make_async_copy(srcs[i].at[chip], dsts[i].at[chip], local_sems.at[i]) for i in range(na)]
        for cp in own:
            cp.start()
        sends, arrivals = [], []
        for i in range(na):
            for k in range(1, N_CHIPS):
                px, py = _chip_peer(x, y, k)
                s = i * N_CHIPS + k
                sends.append(_remote(srcs[i].at[chip ^ k], dsts[i].at[chip], send_sems.at[s], recv_sems.at[s], (px, py, c)))
                arrivals.append(_remote(srcs[i].at[chip], dsts[i].at[chip ^ k], send_sems.at[s], recv_sems.at[s], (px, py, c)))
        for cp in sends:
            cp.start()
        for cp in arrivals:
            cp.wait_recv()
        for cp in sends:
            cp.wait_send()
        for cp in own:
            cp.wait()

    hbm = pl.BlockSpec(memory_space=pl.ANY)
    return pl.pallas_call(
        body, name=name, in_specs=[hbm] * na, out_specs=tuple([hbm] * na),
        out_shape=tuple(jax.ShapeDtypeStruct(a.shape, a.dtype) for a in arrays),
        scratch_shapes=[pltpu.SemaphoreType.DMA((na * N_CHIPS,)), pltpu.SemaphoreType.DMA((na * N_CHIPS,)),
                        pltpu.SemaphoreType.DMA((na,))],
    )(*arrays)


def _adamw_update(g, w, m, v):
    c1 = 1.0 / (1.0 - ADAM_B1 ** ADAM_STEP)
    c2 = 1.0 / (1.0 - ADAM_B2 ** ADAM_STEP)
    mn = ADAM_B1 * m + (1.0 - ADAM_B1) * g
    vn = ADAM_B2 * v + (1.0 - ADAM_B2) * (g * g)
    return -ADAM_LR * ((mn * c1) / (jnp.sqrt(vn * c2) + ADAM_EPS) + ADAM_WD * w), mn, vn


def _reduce_adamw(parts, w, m, v, name):
    n_parts, r, wd = parts.shape
    tc = ELEMENTWISE_COLS

    def body(p_ref, w_ref, m_ref, v_ref, g_out, d_out, m_out, v_out):
        g = p_ref[0].astype(F32)
        for s in range(1, n_parts):
            g = g + p_ref[s].astype(F32)
        g_out[...] = g
        d_out[...], m_out[...], v_out[...] = _adamw_update(g, w_ref[...], m_ref[...], v_ref[...])

    blk = pl.BlockSpec((r, tc), lambda j: (0, j))
    shp = jax.ShapeDtypeStruct((r, wd), F32)
    return pl.pallas_call(
        body, name=name, grid=(wd // tc,),
        in_specs=[pl.BlockSpec((n_parts, r, tc), lambda j: (0, 0, j)), blk, blk, blk],
        out_specs=(blk, blk, blk, blk), out_shape=(shp, shp, shp, shp),
        compiler_params=_cparams(("parallel",)),
    )(parts, w, m, v)


def _reduce_landed_adamw(landing, own, me, w, m, v, name):
    n_parts, r, wd = landing.shape
    tc = ELEMENTWISE_COLS

    def body(me_ref, land_ref, own_ref, w_ref, m_ref, v_ref, g_out, d_out, m_out, v_out):
        mine = own_ref[...].astype(F32)
        g = None
        for s in range(n_parts):
            part = jnp.where(me_ref[0] == s, mine, land_ref[s].astype(F32))
            g = part if g is None else g + part
        g_out[...] = g
        d_out[...], m_out[...], v_out[...] = _adamw_update(g, w_ref[...], m_ref[...], v_ref[...])

    blk = pl.BlockSpec((r, tc), lambda j, me_ref: (0, j))
    shp = jax.ShapeDtypeStruct((r, wd), F32)
    grid_spec = pltpu.PrefetchScalarGridSpec(
        num_scalar_prefetch=1, grid=(wd // tc,),
        in_specs=[pl.BlockSpec((n_parts, r, tc), lambda j, me_ref: (0, 0, j)),
                  pl.BlockSpec((None, r, tc), lambda j, me_ref: (me_ref[0], 0, j)), blk, blk, blk],
        out_specs=(blk, blk, blk, blk))
    return pl.pallas_call(body, name=name, grid_spec=grid_spec, out_shape=(shp, shp, shp, shp),
                          compiler_params=_cparams(("parallel",)))(me, landing, own, w, m, v)


PACK_W = 1024


def _pad_heads(a, slots):
    lead = a.shape[:-1]
    a = a.reshape(lead + (slots, RWKV_HEAD_DIM))
    a = jnp.pad(a, [(0, 0)] * (len(lead) + 1) + [(0, LANES - RWKV_HEAD_DIM)])
    return a.reshape(lead + (slots * LANES,))


def _unpad_heads(a, slots):
    lead = a.shape[:-1]
    return a.reshape(lead + (slots, LANES))[..., :RWKV_HEAD_DIM].reshape(lead + (slots * RWKV_HEAD_DIM,))


def _flat_pack(arrs, dtype, row_mult):
    flat = jnp.concatenate([a.reshape(-1).astype(dtype) for a in arrs])
    n = flat.shape[0]
    rows = -(-n // PACK_W)
    rows = -(-rows // row_mult) * row_mult
    return jnp.pad(flat, (0, rows * PACK_W - n)).reshape(rows, PACK_W)


def _row_pack(arrs, dtype, row_mult):
    parts = [a.astype(dtype) if a.shape[1] == PACK_W else a.astype(dtype).reshape(-1, PACK_W) for a in arrs]
    rows = sum(p.shape[0] for p in parts)
    pad = -(-rows // row_mult) * row_mult - rows
    return jnp.concatenate(parts + ([jnp.zeros((pad, PACK_W), dtype)] if pad else []), axis=0)


def _unpack_row_gathered(g, names, shard_shapes):
    out, r0 = {}, 0
    for n in names:
        s = shard_shapes[n]
        rows = s[0] * s[1] // PACK_W
        seg = g[:, r0:r0 + rows, :]
        r0 += rows
        if s[1] == PACK_W:
            assert SHARD_AXIS[n] == 0
            out[n] = seg.reshape(N_DEV * s[0], s[1])
        else:
            assert SHARD_AXIS[n] == 1
            out[n] = jnp.transpose(seg.reshape((N_DEV,) + tuple(s)), (1, 0, 2)).reshape(s[0], N_DEV * s[1])
    return out


def _unpack_gathered(g, names, shard_shapes):
    flat = g.reshape(N_DEV, -1)
    out, off = {}, 0
    for n in names:
        s = shard_shapes[n]
        size = s[0] * s[1]
        seg = flat[:, off:off + size].reshape((N_DEV,) + tuple(s))
        off += size
        if SHARD_AXIS[n] == 1:
            out[n] = jnp.transpose(seg, (1, 0, 2)).reshape(s[0], N_DEV * s[1])
        else:
            out[n] = seg.reshape(N_DEV * s[0], s[1])
    return out


def _shard_major(full, axis):
    a, b = full.shape
    if axis == 1:
        return jnp.transpose(full.reshape(a, N_DEV, b // N_DEV), (1, 0, 2)).reshape(N_DEV, -1)
    return full.reshape(N_DEV, -1)


def _prepare_weights(full, rep):
    w = full['w_in']
    d = w.shape[1]
    rkv = jnp.pad(w[0:1536].reshape(3 * RWKV_HEADS, RWKV_HEAD_DIM, d), ((0, 0), (0, LANES - RWKV_HEAD_DIM), (0, 0)))
    w_cat = jnp.concatenate([
        w[3848:4872], w[4872:5896], rkv.reshape(3 * RWKV_HEADS * LANES, d), w[1792:3328], w[3328:3840],
        w[1536:1792], jnp.pad(w[3840:3848], ((0, LANES - 8), (0, 0))), jnp.zeros((LANES, d), w.dtype)], axis=0)
    assert w_cat.shape[0] == CAT_W
    mu = rep['rwkv_mu']
    vecs = [mu[0:512], mu[512:1024], mu[1024:1536], rep['rwkv_w0'], rep['rwkv_a0'], rep['rwkv_k_k'], rep['rwkv_k_a'],
            rep['rwkv_ln_w'], rep['rwkv_ln_b'], rep['rwkv_r_k'].reshape(-1)]
    ppack = jnp.stack([jnp.pad(v.reshape(RWKV_HEADS, RWKV_HEAD_DIM), ((0, 0), (0, LANES - RWKV_HEAD_DIM))) for v in vecs], axis=1)
    ppack = jnp.pad(ppack, ((0, 0), (0, 16 - len(vecs)), (0, 0)))
    mulo = mu[1536:1792].reshape(1, 2 * LANES)
    wl = jnp.zeros((3, 2 * LANES, RWKV_HEADS * LANES), F32)
    wl = wl.at[0, 0:64].set(_pad_heads(full['rwkv_w2'], RWKV_HEADS))
    wl = wl.at[1, 64:128].set(_pad_heads(full['rwkv_a2'], RWKV_HEADS))
    wl = wl.at[2, 128:256].set(_pad_heads(full['rwkv_g2'], RWKV_HEADS))
    wl = jnp.transpose(wl.reshape(3, 2 * LANES, RWKV_HEADS, LANES), (2, 0, 1, 3))
    cw = full['gdn_conv_w'].reshape(GDN_CONV, 3, GDN_HEADS, LANES)
    cwpack = jnp.pad(jnp.transpose(cw, (2, 1, 0, 3)), ((0, 0), (0, 0), (0, SUBLANES - GDN_CONV), (0, 0)))
    gpar = jnp.zeros((SUBLANES, LANES), F32)
    gpar = gpar.at[0, 0:GDN_HEADS].set(rep['gdn_a_log']).at[1, 0:GDN_HEADS].set(rep['gdn_dt_bias']).at[2].set(rep['gdn_norm_w'])
    return dict(w_cat=w_cat, ffn_cw=full['ffn_conv_w'], ppack=ppack, mulo=mulo, wl=wl, cwpack=cwpack, gpar=gpar,
                g1=rep['norm1_g'].reshape(1, -1), g2=rep['norm2_g'].reshape(1, -1), gf=rep['final_g'].reshape(1, -1))


def _prepare_late_weights(full):
    rp = full['rwkv_proj']
    rproj = jnp.pad(rp.reshape(RWKV_HEADS, RWKV_HEAD_DIM, -1), ((0, 0), (0, LANES - RWKV_HEAD_DIM), (0, 0))).reshape(RWKV_HEADS * LANES, -1)
    return dict(rproj=rproj, gproj=full['gdn_proj'], w_out=full['w_out'], ffn_up=full['ffn_up'], ffn_down=full['ffn_down'])


def _local_step(x, target, p, late_weights, push_grads):
    d = x.shape[1]
    full_w = lambda a: (a, a.shape[1], 0)
    (u,) = _rw_forward(_norm_fn, [full_w(x)], [p['g1']], [(d, BF16)], "norm1")
    p_cat = _matmul(u, p['w_cat'], 'nt', F32, "proj_in")
    ya_pre, st_r = _rwkv_forward(p_cat, p['ppack'], p['mulo'], p['wl'])
    yb_pre, st_g = _gdn_forward(p_cat, p['cwpack'], p['gpar'])
    p = {**p, **late_weights((ya_pre, yb_pre))}
    ya = _matmul(ya_pre, p['rproj'], 'nn', F32, "rwkv_proj")
    yb = _matmul(yb_pre, p['gproj'], 'nn', F32, "gdn_proj")
    gates = [(p_cat, d, OFF_GA // d), (p_cat, d, OFF_GB // d)]
    (mixed,) = _rw_forward(_merge_fn, gates + [full_w(ya), full_w(yb)], [], [(d, BF16)], "merge")
    mo = _matmul(mixed, p['w_out'], 'nn', F32, "out_proj")
    x1, n2 = _rw_forward(_res_norm_fn, [full_w(x), full_w(mo)], [p['g2']], [(d, F32), (d, BF16)], "res_norm2")
    hpre = _matmul(n2, p['ffn_up'], 'nt', F32, "ffn_up")
    act = _ffn_act_forward(hpre, p['ffn_cw'])
    fo = _matmul(act, p['ffn_down'], 'nn', F32, "ffn_down")
    loss_vec, dx2, dx2b, dgf = _loss_head(x1, fo, p['gf'], target, "loss_head")

    dact = _matmul(dx2b, p['ffn_down'], 'nt', F32, "d_act")
    dw_down = _matmul(act, dx2b, 'tn', BF16, "dw_ffn_down")
    dh_gate, dh_up, dcw_gate, dcw_up = _ffn_backward(hpre, p['ffn_cw'], dact)
    dh = jnp.concatenate([dh_gate, dh_up], axis=1)
    dcw_f = jnp.concatenate([dcw_gate, dcw_up], axis=1)
    dn2 = _matmul(dh, p['ffn_up'], 'nn', F32, "d_norm2")
    dw_up = _matmul(dh, n2, 'tn', BF16, "dw_ffn_up")
    token = push_grads({'ffn_down': dw_down, 'ffn_up': dw_up})
    dx1, dx1b, dg2 = _rw_backward(_res_norm_fn, [full_w(x), full_w(mo)], [p['g2'] + token], [full_w(dx2), full_w(dn2)],
                                  [F32, BF16], "res_norm2_bwd")
    dmixed = _matmul(dx1b, p['w_out'], 'nt', F32, "d_mixed")
    dw_out = _matmul(mixed, dx1b, 'tn', BF16, "dw_out")
    dga, dgb, dya, dyb = _rw_backward(_merge_fn, gates + [full_w(ya), full_w(yb)], [], [full_w(dmixed)],
                                      [BF16, BF16, BF16, BF16], "merge_bwd")
    d_ya_pre = _matmul(dya, p['rproj'], 'nt', F32, "d_rwkv_out")
    dw_rproj = _matmul(ya_pre, dya, 'tn', F32, "dw_rwkv_proj")
    d_yb_pre = _matmul(dyb, p['gproj'], 'nt', F32, "d_gdn_out")
    dw_gproj = _matmul(yb_pre, dyb, 'tn', F32, "dw_gdn_proj")
    dpr, dpk, dpv, dplo, dpp, dml, dwl = _rwkv_backward(p_cat, p['ppack'], p['mulo'], p['wl'], st_r, d_ya_pre)
    dq, dk, dv, dz, dab, dcw_g, dgp = _gdn_backward(p_cat, p['cwpack'], p['gpar'], st_g, d_yb_pre)
    t = x.shape[0]
    dp_cat = jnp.concatenate([dga, dgb, dpr, dpk, dpv, dq, dk, dv, dz, dplo.astype(BF16), dab.astype(BF16),
                              jnp.zeros((t, LANES), BF16)], axis=1)
    dw_cat = _matmul(dp_cat, u, 'tn', BF16, "dw_in")
    dw_in = jnp.concatenate([dw_cat[OFF_RKV:OFF_QKV].reshape(3 * RWKV_HEADS, LANES, d)[:, :RWKV_HEAD_DIM].reshape(-1, d),
                             dw_cat[OFF_LO:OFF_AB], dw_cat[OFF_QKV:OFF_Z], dw_cat[OFF_Z:OFF_LO], dw_cat[OFF_AB:OFF_AB + 8],
                             dw_cat[OFF_GA:OFF_GB], dw_cat[OFF_GB:OFF_RKV]], axis=0)
    token = push_grads({'w_out': dw_out, 'w_in': dw_in})
    du = _matmul(dp_cat, p['w_cat'], 'nn', F32, "d_norm1")
    grad_x, dg1 = _rw_backward(_norm_skip_fn, [full_w(x)], [p['g1'] + token], [full_w(du), full_w(dx1)], [F32], "norm1_bwd")

    heads = lambda row: dpp[:, row, :RWKV_HEAD_DIM].reshape(-1)
    lora = lambda j, lo_, hi_: jnp.transpose(dwl[:, j, lo_:hi_, :RWKV_HEAD_DIM], (1, 0, 2)).reshape(hi_ - lo_, RWKV_WIDTH)
    grads = {
        'norm1_g': dg1[0],
        'w_in': dw_in,
        'rwkv_mu': jnp.concatenate([heads(0), heads(1), heads(2), jnp.sum(dml[:, 0, :], axis=0)]),
        'rwkv_w0': heads(3), 'rwkv_a0': heads(4), 'rwkv_k_k': heads(5), 'rwkv_k_a': heads(6),
        'rwkv_ln_w': heads(7), 'rwkv_ln_b': heads(8), 'rwkv_r_k': heads(9).reshape(RWKV_HEADS, RWKV_HEAD_DIM),
        'rwkv_w2': lora(0, 0, 64), 'rwkv_a2': lora(1, 64, 128), 'rwkv_g2': lora(2, 128, 256),
        'rwkv_proj': dw_rproj.reshape(RWKV_HEADS, LANES, -1)[:, :RWKV_HEAD_DIM].reshape(RWKV_WIDTH, -1),
        'gdn_conv_w': jnp.transpose(dcw_g[:, :, :GDN_CONV, :], (2, 1, 0, 3)).reshape(GDN_CONV, 3 * GDN_WIDTH),
        'gdn_a_log': jnp.sum(dgp[:, 0, :GDN_HEADS], axis=0), 'gdn_dt_bias': jnp.sum(dgp[:, 1, :GDN_HEADS], axis=0),
        'gdn_norm_w': jnp.sum(dgp[:, 2, :], axis=0),
        'gdn_proj': dw_gproj, 'w_out': dw_out, 'norm2_g': dg2[0], 'ffn_up': dw_up, 'ffn_conv_w': dcw_f,
        'ffn_down': dw_down, 'final_g': dgf[0],
    }
    return loss_vec, grad_x, grads


def kernel(x, norm1_g, w_in, rwkv_mu, rwkv_w0, rwkv_w2, rwkv_a0, rwkv_a2, rwkv_g2, rwkv_k_k, rwkv_k_a, rwkv_r_k, rwkv_ln_w, rwkv_ln_b, rwkv_proj, gdn_conv_w, gdn_a_log, gdn_dt_bias, gdn_norm_w, gdn_proj, w_out, norm2_g, ffn_up, ffn_conv_w, ffn_down, final_g, loss_target, m_norm1_g, m_w_in, m_rwkv_mu, m_rwkv_w0, m_rwkv_w2, m_rwkv_a0, m_rwkv_a2, m_rwkv_g2, m_rwkv_k_k, m_rwkv_k_a, m_rwkv_r_k, m_rwkv_ln_w, m_rwkv_ln_b, m_rwkv_proj, m_gdn_conv_w, m_gdn_a_log, m_gdn_dt_bias, m_gdn_norm_w, m_gdn_proj, m_w_out, m_norm2_g, m_ffn_up, m_ffn_conv_w, m_ffn_down, m_final_g, v_norm1_g, v_w_in, v_rwkv_mu, v_rwkv_w0, v_rwkv_w2, v_rwkv_a0, v_rwkv_a2, v_rwkv_g2, v_rwkv_k_k, v_rwkv_k_a, v_rwkv_r_k, v_rwkv_ln_w, v_rwkv_ln_b, v_rwkv_proj, v_gdn_conv_w, v_gdn_a_log, v_gdn_dt_bias, v_gdn_norm_w, v_gdn_proj, v_w_out, v_norm2_g, v_ffn_up, v_ffn_conv_w, v_ffn_down, v_final_g):
    given = dict(zip(WEIGHT_NAMES, (norm1_g, w_in, rwkv_mu, rwkv_w0, rwkv_w2, rwkv_a0, rwkv_a2, rwkv_g2, rwkv_k_k, rwkv_k_a, rwkv_r_k,
                                    rwkv_ln_w, rwkv_ln_b, rwkv_proj, gdn_conv_w, gdn_a_log, gdn_dt_bias, gdn_norm_w, gdn_proj, w_out,
                                    norm2_g, ffn_up, ffn_conv_w, ffn_down, final_g)))
    mom1 = dict(zip(WEIGHT_NAMES, (m_norm1_g, m_w_in, m_rwkv_mu, m_rwkv_w0, m_rwkv_w2, m_rwkv_a0, m_rwkv_a2, m_rwkv_g2, m_rwkv_k_k,
                                   m_rwkv_k_a, m_rwkv_r_k, m_rwkv_ln_w, m_rwkv_ln_b, m_rwkv_proj, m_gdn_conv_w, m_gdn_a_log,
                                   m_gdn_dt_bias, m_gdn_norm_w, m_gdn_proj, m_w_out, m_norm2_g, m_ffn_up, m_ffn_conv_w, m_ffn_down,
                                   m_final_g)))
    mom2 = dict(zip(WEIGHT_NAMES, (v_norm1_g, v_w_in, v_rwkv_mu, v_rwkv_w0, v_rwkv_w2, v_rwkv_a0, v_rwkv_a2, v_rwkv_g2, v_rwkv_k_k,
                                   v_rwkv_k_a, v_rwkv_r_k, v_rwkv_ln_w, v_rwkv_ln_b, v_rwkv_proj, v_gdn_conv_w, v_gdn_a_log,
                                   v_gdn_dt_bias, v_gdn_norm_w, v_gdn_proj, v_w_out, v_norm2_g, v_ffn_up, v_ffn_conv_w, v_ffn_down,
                                   v_final_g)))
    def strip(n, a):
        a = a if n == 'final_g' else a.reshape(a.shape[1:])
        return a.T if n in TRANSPOSED else a

    local = {n: strip(n, a) for n, a in given.items()}
    shard_shapes = {n: local[n].shape for n in SHARD_AXIS}
    sharded = BIG_SHARDED + SMALL_SHARDED

    late_names = [n for n in BIG_SHARDED if n != 'w_in']
    g_in = _all_gather_two_level(_row_pack([local['w_in']], BF16, 16), "gather_w_in")
    g_small = _all_gather_two_level(_flat_pack([local[n] for n in SMALL_SHARDED], F32, SUBLANES), "gather_small")
    late_pack, g_in, g_small = lax.optimization_barrier((_row_pack([local[n] for n in late_names], BF16, 16), g_in, g_small))
    send_sems, recv_sems, late_pack, landing, token = _gather_start(late_pack, "gather_late_start")
    full = _unpack_row_gathered(g_in, ['w_in'], shard_shapes)
    full.update(_unpack_gathered(g_small, SMALL_SHARDED, shard_shapes))
    rep = {n: local[n] for n in REPLICATED}
    rep['norm1_g'] = rep['norm1_g'] + token[0, 0]

    def late_weights(after):
        got = _gather_wait(send_sems, recv_sems, late_pack, landing, after, "gather_late_wait")
        me = 4 * lax.axis_index("x") + 2 * lax.axis_index("y") + lax.axis_index("c")
        slot = lax.broadcasted_iota(jnp.int32, (N_DEV, 1, 1), 0)
        got = jnp.where(slot == me, late_pack[None], got)
        return _prepare_late_weights(_unpack_row_gathered(got, late_names, shard_shapes))

    pushes = []
    me = 4 * lax.axis_index("x") + 2 * lax.axis_index("y") + lax.axis_index("c")
    slot = lax.broadcasted_iota(jnp.int32, (N_DEV, 1, 1), 0)

    def push_grads(group):
        names = list(group)
        slabs = [group[n].reshape(N_DEV, -1, group[n].shape[1]) for n in names]
        send_sems, recv_sems, slabs, landings, token = _slab_push_start(slabs, "grad_push_start_" + "_".join(names))
        pushes.append((names, send_sems, recv_sems, slabs, landings))
        return token[0, 0]

    loss_vec, grad_x, grads = _local_step(x[0], loss_target[0], _prepare_weights(full, rep), late_weights, push_grads)

    landed = {}
    for names, send_sems, recv_sems, slabs, landings in pushes:
        got = _slab_push_wait(send_sems, recv_sems, slabs, landings, (grad_x,), "grad_push_wait_" + "_".join(names))
        for n, slab, land in zip(names, slabs, got):
            landed[n] = (land, slab)

    small_sharded = ['rwkv_proj', 'gdn_proj'] + SMALL_SHARDED
    small_names = small_sharded + REPLICATED
    rep_vec = jnp.concatenate([grads[n].reshape(-1) for n in REPLICATED] + [loss_vec[0, 0:1]])
    slab_small = jnp.concatenate([_shard_major(grads[n], SHARD_AXIS[n]) for n in small_sharded] +
                                 [jnp.broadcast_to(rep_vec[None], (N_DEV, rep_vec.shape[0]))], axis=1)
    small_rows = -(-slab_small.shape[1] // (PACK_W * SUBLANES)) * SUBLANES
    slab_small = jnp.pad(slab_small, ((0, 0), (0, small_rows * PACK_W - slab_small.shape[1]))).reshape(N_DEV, small_rows, PACK_W)
    core = lax.axis_index("c").astype(jnp.int32).reshape(1)
    (from_sibling,) = _pair_exchange([slab_small], "grad_pair_exchange")
    chip_small = _pair_sum(slab_small, from_sibling, core, F32, "grad_pair_sum_small")
    (parts_small,) = _chip_exchange([chip_small], "grad_chip_exchange")

    def pack_local(src):
        flat = jnp.concatenate([strip(n, src[n]).reshape(-1) for n in small_names])
        return jnp.pad(flat, (0, small_rows * PACK_W - flat.shape[0])).reshape(small_rows, PACK_W)

    results = [({}, None) for _ in range(4)]
    me_arr = me.astype(jnp.int32).reshape(1)
    for n in ROW_SHARDED:
        packs = _reduce_landed_adamw(*landed[n], me_arr, local[n], strip(n, mom1[n]), strip(n, mom2[n]), "adamw_" + n)
        for (out, _), pk in zip(results, packs):
            out[n] = (pk.T if n in TRANSPOSED else pk).reshape(given[n].shape)
    packs = _reduce_adamw(parts_small, pack_local(given), pack_local(mom1), pack_local(mom2), "adamw_small")
    for i, pk in enumerate(packs):
        flat, off = pk.reshape(-1), 0
        for n in small_names:
            size = int(np.prod(given[n].shape))
            results[i][0][n] = flat[off:off + size].reshape(given[n].shape)
            off += size
        results[i] = (results[i][0], flat[off])
    (g_out, loss), (d_out, _), (m_out, _), (v_out, _) = results
    return (loss, grad_x[None], *[g_out[n] for n in WEIGHT_NAMES], *[d_out[n] for n in WEIGHT_NAMES],
            *[m_out[n] for n in WEIGHT_NAMES], *[v_out[n] for n in WEIGHT_NAMES])
```

```python
import functools

import jax
import jax.numpy as jnp
import numpy as np
from jax import lax
from jax.experimental import pallas as pl
from jax.experimental.pallas import tpu as pltpu

F32 = jnp.float32
BF16 = jnp.bfloat16
HI = lax.Precision.HIGHEST

N_DEV = 8
D_MODEL = 1024
CHUNK = 64
RWKV_HEADS = 8
RWKV_HEAD_DIM = 64
RWKV_WIDTH = 512
GDN_HEADS = 4
GDN_HEAD_DIM = 128
GDN_WIDTH = 512
GDN_CONV = 4
FFN_HIDDEN = 2816
FFN_CONV = 3
NORM_EPS = 1e-6
L2_EPS = 1e-6
RWKV_GN_EPS = 64e-5
LANES = 128
SUBLANES = 8
VMEM_LIMIT = 56 * 1024 * 1024

ADAM_LR = 0.001
ADAM_B1 = 0.9
ADAM_B2 = 0.999
ADAM_EPS = 1e-08
ADAM_WD = 0.01
ADAM_STEP = 10

OFF_GA, OFF_GB, OFF_RKV, OFF_QKV, OFF_Z, OFF_LO, OFF_AB, CAT_W = 0, 1024, 2048, 5120, 6656, 7168, 7424, 7680
RWKV_HB = 8
RWKV_STEP_CHUNKS = 2
RWKV_TILE = RWKV_STEP_CHUNKS * CHUNK
GDN_HB = 4
GDN_STEP_CHUNKS = 4
GDN_TILE = GDN_STEP_CHUNKS * CHUNK

WEIGHT_NAMES = ['norm1_g', 'w_in', 'rwkv_mu', 'rwkv_w0', 'rwkv_w2', 'rwkv_a0', 'rwkv_a2', 'rwkv_g2', 'rwkv_k_k', 'rwkv_k_a',
                'rwkv_r_k', 'rwkv_ln_w', 'rwkv_ln_b', 'rwkv_proj', 'gdn_conv_w', 'gdn_a_log', 'gdn_dt_bias', 'gdn_norm_w',
                'gdn_proj', 'w_out', 'norm2_g', 'ffn_up', 'ffn_conv_w', 'ffn_down', 'final_g']
BIG_SHARDED = ['w_in', 'ffn_up', 'ffn_down', 'w_out', 'rwkv_proj', 'gdn_proj']
SMALL_SHARDED = ['rwkv_w2', 'rwkv_a2', 'rwkv_g2', 'gdn_conv_w', 'ffn_conv_w']
TRANSPOSED = ('w_in', 'ffn_up')
SHARD_AXIS = {'w_in': 0, 'ffn_up': 0, 'ffn_down': 0, 'w_out': 0, 'rwkv_proj': 1, 'gdn_proj': 1,
              'rwkv_w2': 1, 'rwkv_a2': 1, 'rwkv_g2': 1, 'gdn_conv_w': 1, 'ffn_conv_w': 1}
REPLICATED = [n for n in WEIGHT_NAMES if n not in SHARD_AXIS]
ROW_SHARDED = ['w_in', 'ffn_up', 'ffn_down', 'w_out']


def _cparams(sem=None):
    kw = dict(vmem_limit_bytes=VMEM_LIMIT)
    if sem is not None:
        kw['dimension_semantics'] = sem
    return pltpu.CompilerParams(**kw)


_NN, _NT, _TN = 'nn', 'nt', 'tn'
_DIMS_2D = {'nn': (((1,), (0,)), ((), ())), 'nt': (((1,), (1,)), ((), ())), 'tn': (((0,), (0,)), ((), ()))}
_DIMS_3D = {'nn': (((2,), (1,)), ((0,), (0,))), 'nt': (((2,), (2,)), ((0,), (0,))), 'tn': (((1,), (1,)), ((0,), (0,)))}


def _dg(a, b, kind):
    return lax.dot_general(a, b, (_DIMS_2D if a.ndim == 2 else _DIMS_3D)[kind], preferred_element_type=F32)


def _dot1(a, b, kind):
    return _dg(a.astype(BF16), b.astype(BF16), kind)


@jax.custom_vjp
def _dhi(a, b):
    return _dot1(a, b, _NN)


_dhi.defvjp(lambda a, b: (_dot1(a, b, _NN), (a, b)),
            lambda res, ct: (_dot1(ct, res[1], _NT), _dot1(res[0], ct, _TN)))


@jax.custom_vjp
def _dnt(a, b):
    return _dot1(a, b, _NT)


_dnt.defvjp(lambda a, b: (_dot1(a, b, _NT), (a, b)),
            lambda res, ct: (_dot1(ct, res[1], _NN), _dot1(ct, res[0], _TN)))


@jax.custom_vjp
def _dtn(a, b):
    return _dot1(a, b, _TN)


_dtn.defvjp(lambda a, b: (_dot1(a, b, _TN), (a, b)),
            lambda res, ct: (_dot1(res[1], ct, _NT), _dot1(res[0], ct, _NN)))


def _split3(x):
    x1 = x.astype(BF16)
    r1 = x - x1.astype(F32)
    x2 = r1.astype(BF16)
    return x1, x2, (r1 - x2.astype(F32)).astype(BF16)


def _dot_exact_lhs(sel, x, kind):
    parts = [_dg(sel, xi, kind) for xi in _split3(x)]
    return parts[0] + parts[1] + parts[2]


def _tril_ones(like):
    c = like.shape[-2]
    ri, ci = _iotas(c)
    return jnp.broadcast_to((ri >= ci).astype(BF16), like.shape[:-2] + (c, c))


@jax.custom_vjp
def _cumsum_rows(x):
    return _dot_exact_lhs(_tril_ones(x), x, _NN)


_cumsum_rows.defvjp(lambda x: (_dot_exact_lhs(_tril_ones(x), x, _NN), None),
                    lambda _, ct: (_dot_exact_lhs(_tril_ones(ct), ct, _TN),))


@jax.custom_vjp
def _lane_sum_as_row(x):
    return _dot_exact_lhs(jnp.ones(x.shape, BF16), x, _NT)


def _lane_sum_as_row_bwd(_, ct):
    ones = jnp.ones(ct.shape[:-1] + (LANES,), BF16)
    parts = [_dg(ci, ones, _TN) for ci in _split3(ct)]
    return (parts[0] + parts[1] + parts[2],)


_lane_sum_as_row.defvjp(lambda x: (_dot_exact_lhs(jnp.ones(x.shape, BF16), x, _NT), None), _lane_sum_as_row_bwd)


def _shift_rows(x, halo, s):
    rows = lax.broadcasted_iota(jnp.int32, x.shape, 0)
    out = pltpu.roll(x, s, 0)
    for i in range(s):
        out = jnp.where(rows == i, halo[SUBLANES - s + i:SUBLANES - s + i + 1, :], out)
    return out


def _unshift_rows(g, carry, s):
    c = g.shape[0]
    rows = lax.broadcasted_iota(jnp.int32, g.shape, 0)
    out = pltpu.roll(g, c - s, 0)
    for i in range(s):
        out = jnp.where(rows == c - s + i, carry[i:i + 1, :], out)
    return out


def _sigmoid(z):
    return 1.0 / (1.0 + jnp.exp(-z))


def _silu(z):
    return z * _sigmoid(z)


def _softplus(z):
    return jnp.maximum(z, 0.0) + jnp.log(1.0 + jnp.exp(-jnp.abs(z)))


def _rms(t, gain):
    return t * lax.rsqrt(jnp.mean(t * t, axis=-1, keepdims=True) + NORM_EPS) * gain


def _iotas(c):
    return lax.broadcasted_iota(jnp.int32, (c, c), 0), lax.broadcasted_iota(jnp.int32, (c, c), 1)


def _unit_lower_inverse(xm, eye):
    t = eye + xm
    p = xm
    for _ in range(5):
        p = _dhi(p, p)
        t = t + _dhi(t, p)
    return t


def _rwkv_head(pr, pk, pv, plo, qr, qk, qv, qlo, s0, pp, mulo, wl):
    c = pr.shape[1]
    n_heads = s0.shape[0]
    n_chunks = pr.shape[0] // n_heads
    ri, ci = _iotas(c)
    if n_chunks > 1:
        pp = jnp.concatenate([pp] * n_chunks, axis=0)
        wl = jnp.concatenate([wl] * n_chunks, axis=0)

    def mix(p, q, mu):
        return p + (q - p) * mu

    r = mix(pr, qr, pp[:, 0:1])
    k = mix(pk, qk, pp[:, 1:2])
    v = mix(pv, qv, pp[:, 2:3])
    lo = mix(plo, qlo, mulo)
    w0, a0, k_k, k_a, ln_w, ln_b, r_k = (pp[:, i:i + 1] for i in range(3, 10))

    def per_head(t):
        return jnp.concatenate([jnp.broadcast_to(t[i], (n_heads,) + t.shape[1:]) for i in range(n_chunks)], axis=0)

    zw = _dhi(per_head(jnp.tanh(lo)), wl[:, 0])
    za = _dhi(per_head(lo), wl[:, 1])
    g = _dhi(per_head(_sigmoid(lo)), wl[:, 2])
    w_log = -_softplus(-(w0 + zw)) - 0.5
    lw = -jnp.exp(w_log)
    a = _sigmoid(a0 + za)
    kk = k * k_k
    kk = kk * lax.rsqrt(jnp.sum(kk * kk, axis=-1, keepdims=True) + L2_EPS)
    k2 = k * (1.0 + (a - 1.0) * k_a)
    an = -kk
    b = kk * a
    causal = ri >= ci
    strict = ri > ci
    eye = (ri == ci).astype(F32)
    cl = _cumsum_rows(lw)
    ecl = jnp.exp(-cl)
    at = an * jnp.exp(cl - lw)
    bt = b * ecl
    kt = k2 * ecl
    rt = r * jnp.exp(cl)
    a_ab = jnp.where(strict, _dnt(at, bt), 0.0)
    a_ak = jnp.where(strict, _dnt(at, kt), 0.0)
    tinv = _unit_lower_inverse(a_ab, eye)
    akv = _dhi(a_ak, v)
    r_b = jnp.where(causal, _dnt(rt, bt), 0.0)
    rkv = _dhi(jnp.where(causal, _dnt(rt, kt), 0.0), v)
    cl_end = jnp.sum(lw, axis=1, keepdims=True)
    dec_end = jnp.exp(cl_end - cl)
    b_end = b * dec_end
    sv = _dtn(v, k2 * dec_end)
    e_end = jnp.exp(cl_end)
    state, ys = s0, []
    for i in range(n_chunks):
        sl = slice(i * n_heads, (i + 1) * n_heads)
        u = _dhi(tinv[sl], _dnt(at[sl], state) + akv[sl])
        ys.append(_dnt(rt[sl], state) + _dhi(r_b[sl], u) + rkv[sl])
        state = state * e_end[sl] + _dtn(u, b_end[sl]) + sv[sl]
    y = jnp.concatenate(ys, axis=0) if n_chunks > 1 else ys[0]
    s1 = state
    m = (lax.broadcasted_iota(jnp.int32, (1, LANES), 1) < RWKV_HEAD_DIM).astype(F32)
    mean = jnp.sum(y, axis=-1, keepdims=True) * (1.0 / RWKV_HEAD_DIM)
    yc = (y - mean) * m
    var = jnp.sum(yc * yc, axis=-1, keepdims=True) * (1.0 / RWKV_HEAD_DIM)
    yn = yc * lax.rsqrt(var + RWKV_GN_EPS) * ln_w + ln_b
    y2 = yn + jnp.sum(r * k2 * r_k, axis=-1, keepdims=True) * v
    return y2 * g, s1


def _gdn_head(xq, xk, xv, z, ab, s0, cw, gp, oha, ohb):
    c = z.shape[1]
    n_heads = s0.shape[0]
    n_chunks = z.shape[0] // n_heads
    ri, ci = _iotas(c)
    cw = jnp.concatenate([cw] * n_chunks, axis=0) if n_chunks > 1 else cw

    def conv(xs, w):
        out = xs[0] * w[:, GDN_CONV - 1:GDN_CONV]
        for s in range(1, GDN_CONV):
            out = out + xs[s] * w[:, GDN_CONV - 1 - s:GDN_CONV - s]
        return out

    q = _silu(conv(xq, cw[:, 0]))
    k = _silu(conv(xk, cw[:, 1]))
    v = _silu(conv(xv, cw[:, 2]))
    q = q * lax.rsqrt(jnp.sum(q * q, axis=-1, keepdims=True) + L2_EPS) * (GDN_HEAD_DIM ** -0.5)
    k = k * lax.rsqrt(jnp.sum(k * k, axis=-1, keepdims=True) + L2_EPS)
    gg = -jnp.exp(gp[0:1]) * _softplus(ab + gp[1:2])
    beta = jnp.sum(_sigmoid(ab) * ohb, axis=-1, keepdims=True)
    causal = ri >= ci
    strict = ri > ci
    eye = (ri == ci).astype(F32)
    gcm = _cumsum_rows(gg * oha)
    gc = jnp.sum(gcm, axis=-1, keepdims=True)
    gc_row = _lane_sum_as_row(gcm)
    dec = jnp.where(causal, jnp.exp(jnp.where(causal, gc - gc_row, 0.0)), 0.0)
    kb = k * beta
    vb = v * beta
    lm = jnp.where(strict, _dnt(kb, k) * dec, 0.0)
    tinv = _unit_lower_inverse(-lm, eye)
    egc = jnp.exp(gc)
    u = _dhi(tinv, vb)
    wk = _dhi(tinv, kb * egc)
    attn = jnp.where(causal, _dnt(q, k) * dec, 0.0)
    g_last = gc[:, c - 1:c, :]
    q_dec = q * egc
    k_dec = k * jnp.exp(g_last - gc)
    e_last = jnp.exp(g_last)
    state, outs = s0, []
    for i in range(n_chunks):
        sl = slice(i * n_heads, (i + 1) * n_heads)
        v_new = u[sl] - _dhi(wk[sl], state)
        outs.append(_dhi(q_dec[sl], state) + _dhi(attn[sl], v_new))
        state = state * e_last[sl] + _dtn(k_dec[sl], v_new)
    o = jnp.concatenate(outs, axis=0) if n_chunks > 1 else outs[0]
    return _rms(o, gp[2:3]) * _silu(z), state


def _head_id(grp, i, per_step, heads):
    return i if per_step == heads else grp * per_step + i


def _head_range(grp, per_step, heads):
    return slice(None) if per_step == heads else pl.ds(grp * per_step, per_step)


def _stack_heads(ref, n, fn=None):
    parts = []
    for i in range(n):
        sl = slice(i * LANES, (i + 1) * LANES)
        v = ref[:, sl]
        parts.append(v if fn is None else fn(v, sl))
    return jnp.stack(parts)


def _prev_rows_spec(width, col):
    per = CHUNK // SUBLANES
    return pl.BlockSpec((SUBLANES, width), lambda n, h: (jnp.maximum(n * per - 1, 0), col(h)))


def _rwkv_specs(nmap):
    hb, groups = RWKV_HB, RWKV_HEADS // RWKV_HB
    cb = OFF_RKV // (hb * LANES)
    specs = []
    for j in range(3):
        specs.append(pl.BlockSpec((RWKV_TILE, hb * LANES), lambda n, g, j=j: (nmap(n), cb + j * groups + g)))
    specs.append(pl.BlockSpec((RWKV_TILE, 2 * LANES), lambda n, g: (nmap(n), OFF_LO // (2 * LANES))))
    per = RWKV_TILE // SUBLANES
    for j in range(3):
        specs.append(pl.BlockSpec((SUBLANES, hb * LANES),
                                  lambda n, g, j=j: (jnp.maximum(nmap(n) * per - 1, 0), cb + j * groups + g)))
    specs.append(pl.BlockSpec((SUBLANES, 2 * LANES), lambda n, g: (jnp.maximum(nmap(n) * per - 1, 0), OFF_LO // (2 * LANES))))
    specs.append(pl.BlockSpec((hb, 16, LANES), lambda n, g: (g, 0, 0)))
    specs.append(pl.BlockSpec((1, 2 * LANES), lambda n, g: (0, 0)))
    specs.append(pl.BlockSpec((hb, 3, 2 * LANES, LANES), lambda n, g: (g, 0, 0, 0)))
    return specs


def _rwkv_operands(refs, halos, live):
    pr, pk, pv, plo = refs
    hr, hk, hv, hlo = halos
    cur, prev = [], []
    for x, hx in ((pr, hr), (pk, hk), (pv, hv)):
        tiles = [_lane_block(x, h) for h in range(RWKV_HB)]
        cur.append(_chunk_batch(tiles))
        prev.append(_chunk_batch([_shift_rows(t_, _lane_block(hx, h) * live, 1) for h, t_ in enumerate(tiles)]))
    lo = plo[...]
    cur.append(_chunk_batch([lo]))
    prev.append(_chunk_batch([_shift_rows(lo, hlo[...] * live, 1)]))
    return cur, prev


def _rwkv_forward(p_cat, ppack, mulo, wl):
    t = p_cat.shape[0]
    n_chunks = t // RWKV_TILE

    def body(pr, pk, pv, plo, hr, hk, hv, hlo, pp, ml, w, out, st_out, s_scr):
        n, grp = pl.program_id(0), pl.program_id(1)

        hsl = _head_range(grp, RWKV_HB, RWKV_HEADS)

        @pl.when(n == 0)
        def _():
            s_scr[hsl] = jnp.zeros((RWKV_HB, LANES, LANES), F32)

        live = (n > 0).astype(F32)
        cur, prev = _rwkv_operands((pr, pk, pv, plo), (hr, hk, hv, hlo), live)
        s0 = s_scr[hsl]
        st_out[...] = s0
        o, s1 = _rwkv_head(*cur, *prev, s0, pp[...], ml[...], w[...])
        for h, tile in enumerate(_head_tiles(o, RWKV_HB)):
            out[:, h * LANES:(h + 1) * LANES] = tile.astype(out.dtype)
        s_scr[hsl] = s1

    return pl.pallas_call(
        body, name="rwkv_fwd", grid=(n_chunks, RWKV_HEADS // RWKV_HB),
        in_specs=_rwkv_specs(lambda n: n),
        out_specs=(pl.BlockSpec((RWKV_TILE, RWKV_HB * LANES), lambda n, g: (n, g)),
                   pl.BlockSpec((None, RWKV_HB, LANES, LANES), lambda n, g: (n, g, 0, 0))),
        out_shape=(jax.ShapeDtypeStruct((t, RWKV_HEADS * LANES), BF16),
                   jax.ShapeDtypeStruct((n_chunks, RWKV_HEADS, LANES, LANES), F32)),
        scratch_shapes=[pltpu.VMEM((RWKV_HEADS, LANES, LANES), F32)],
        compiler_params=_cparams(("arbitrary", "arbitrary")),
    )(p_cat, p_cat, p_cat, p_cat, p_cat, p_cat, p_cat, p_cat, ppack, mulo, wl)


def _rwkv_backward(p_cat, ppack, mulo, wl, states, d_out):
    t = p_cat.shape[0]
    n_chunks = t // RWKV_TILE
    last = n_chunks - 1

    def body(pr, pk, pv, plo, hr, hk, hv, hlo, pp, ml, w, st, dy, dpr, dpk, dpv, dplo, dpp, dml, dw, ds_scr, car_scr, carlo_scr):
        n, grp = pl.program_id(0), pl.program_id(1)

        hsl = _head_range(grp, RWKV_HB, RWKV_HEADS)
        gi = _head_id(grp, 0, 1, RWKV_HEADS // RWKV_HB)

        @pl.when(n == 0)
        def _():
            ds_scr[hsl] = jnp.zeros((RWKV_HB, LANES, LANES), F32)
            car_scr[hsl] = jnp.zeros((RWKV_HB, 3 * SUBLANES, LANES), F32)
            carlo_scr[gi] = jnp.zeros((SUBLANES, 2 * LANES), F32)

        @pl.when((n == 0) & (grp == 0))
        def _():
            dpp[...] = jnp.zeros(dpp.shape, F32)
            dml[...] = jnp.zeros(dml.shape, F32)
            dw[...] = jnp.zeros(dw.shape, F32)

        live = (n < last).astype(F32)
        cur, prev = _rwkv_operands((pr, pk, pv, plo), (hr, hk, hv, hlo), live)
        _, vjp = jax.vjp(_rwkv_head, *cur, *prev, st[...], pp[...], ml[...], w[...])
        g = vjp((_chunk_batch([_lane_block(dy, h) for h in range(RWKV_HB)]), ds_scr[hsl]))
        outs = (dpr, dpk, dpv)
        d_cur = [_head_tiles(g[j], RWKV_HB) for j in range(3)]
        d_prev = [_head_tiles(g[4 + j], RWKV_HB) for j in range(3)]
        for i in range(RWKV_HB):
            sl = slice(i * LANES, (i + 1) * LANES)
            h = _head_id(grp, i, RWKV_HB, RWKV_HEADS)
            car = car_scr[h]
            for j in range(3):
                tot = d_cur[j][i] + _unshift_rows(d_prev[j][i], car[SUBLANES * j:SUBLANES * (j + 1), :], 1)
                outs[j][:, sl] = tot.astype(outs[j].dtype)
                car_scr[h, SUBLANES * j:SUBLANES * (j + 1), :] = d_prev[j][i][0:SUBLANES, :]
        (dlo_cur,), (dlo_prev,) = _head_tiles(g[3], 1), _head_tiles(g[7], 1)
        dlo = dlo_cur + _unshift_rows(dlo_prev, carlo_scr[gi], 1)
        carlo_scr[gi] = dlo_prev[0:SUBLANES, :]
        ds_scr[hsl] = g[8]
        dpp[hsl] += g[9]
        dml[0, 0:1, :] += g[10]
        dw[hsl] += g[11]

        @pl.when(grp == 0)
        def _():
            dplo[...] = dlo

        @pl.when(grp > 0)
        def _():
            dplo[...] += dlo

    rev = lambda n: last - n
    in_specs = _rwkv_specs(rev) + [
        pl.BlockSpec((None, RWKV_HB, LANES, LANES), lambda n, g: (rev(n), g, 0, 0)),
        pl.BlockSpec((RWKV_TILE, RWKV_HB * LANES), lambda n, g: (rev(n), g)),
    ]
    hw = RWKV_HEADS * LANES
    return pl.pallas_call(
        body, name="rwkv_bwd", grid=(n_chunks, RWKV_HEADS // RWKV_HB),
        in_specs=in_specs,
        out_specs=(pl.BlockSpec((RWKV_TILE, RWKV_HB * LANES), lambda n, g: (rev(n), g)),
                   pl.BlockSpec((RWKV_TILE, RWKV_HB * LANES), lambda n, g: (rev(n), g)),
                   pl.BlockSpec((RWKV_TILE, RWKV_HB * LANES), lambda n, g: (rev(n), g)),
                   pl.BlockSpec((RWKV_TILE, 2 * LANES), lambda n, h: (rev(n), 0)),
                   pl.BlockSpec((RWKV_HEADS, 16, LANES), lambda n, h: (0, 0, 0)),
                   pl.BlockSpec((RWKV_HEADS, SUBLANES, 2 * LANES), lambda n, h: (0, 0, 0)),
                   pl.BlockSpec((RWKV_HEADS, 3, 2 * LANES, LANES), lambda n, h: (0, 0, 0, 0))),
        out_shape=(jax.ShapeDtypeStruct((t, hw), BF16), jax.ShapeDtypeStruct((t, hw), BF16), jax.ShapeDtypeStruct((t, hw), BF16),
                   jax.ShapeDtypeStruct((t, 2 * LANES), F32),
                   jax.ShapeDtypeStruct((RWKV_HEADS, 16, LANES), F32),
                   jax.ShapeDtypeStruct((RWKV_HEADS, SUBLANES, 2 * LANES), F32),
                   jax.ShapeDtypeStruct((RWKV_HEADS, 3, 2 * LANES, LANES), F32)),
        scratch_shapes=[pltpu.VMEM((RWKV_HEADS, LANES, LANES), F32),
                        pltpu.VMEM((RWKV_HEADS, 3 * SUBLANES, LANES), F32),
                        pltpu.VMEM((RWKV_HEADS, SUBLANES, 2 * LANES), F32)],
        compiler_params=_cparams(("arbitrary", "arbitrary")),
    )(p_cat, p_cat, p_cat, p_cat, p_cat, p_cat, p_cat, p_cat, ppack, mulo, wl, states, d_out)


def _gdn_specs(nmap):
    per = GDN_TILE // SUBLANES
    hb, groups = GDN_HB, GDN_HEADS // GDN_HB
    cb = OFF_QKV // (hb * LANES)
    specs = []
    for j in range(3):
        specs.append(pl.BlockSpec((GDN_TILE, hb * LANES), lambda n, g, j=j: (nmap(n), cb + j * groups + g)))
    for j in range(3):
        specs.append(pl.BlockSpec((SUBLANES, hb * LANES),
                                  lambda n, g, j=j: (jnp.maximum(nmap(n) * per - 1, 0), cb + j * groups + g)))
    specs.append(pl.BlockSpec((GDN_TILE, hb * LANES), lambda n, g: (nmap(n), OFF_Z // (hb * LANES) + g)))
    specs.append(pl.BlockSpec((GDN_TILE, LANES), lambda n, g: (nmap(n), OFF_AB // LANES)))
    specs.append(pl.BlockSpec((hb, 3, SUBLANES, LANES), lambda n, g: (g, 0, 0, 0)))
    specs.append(pl.BlockSpec((SUBLANES, LANES), lambda n, g: (0, 0)))
    return specs


def _conv_taps(x, halo):
    return (x,) + tuple(_shift_rows(x, halo, s) for s in range(1, GDN_CONV))


def _onehots(grp):
    nb = GDN_STEP_CHUNKS * GDN_HB
    lane = lax.broadcasted_iota(jnp.int32, (nb, 1, LANES), 2)
    head = lax.broadcasted_iota(jnp.int32, (nb, 1, LANES), 0) % GDN_HB + _head_id(grp, 0, GDN_HB, GDN_HEADS)
    return (lane == head).astype(F32), (lane == GDN_HEADS + head).astype(F32)


def _chunk_batch(tiles):
    n_chunks = tiles[0].shape[0] // CHUNK
    return jnp.stack([t_[i * CHUNK:(i + 1) * CHUNK, :] for i in range(n_chunks) for t_ in tiles])


def _head_tiles(batch, n_heads=GDN_HB):
    n_chunks = batch.shape[0] // n_heads
    return [jnp.concatenate([batch[i * n_heads + h] for i in range(n_chunks)], axis=0) for h in range(n_heads)]


def _lane_block(ref, h):
    return ref[:, h * LANES:(h + 1) * LANES]


def _gdn_taps(refs, halos, live):
    out = []
    for x, hx in zip(refs, halos):
        per_head = [_conv_taps(_lane_block(x, h), _lane_block(hx, h) * live) for h in range(GDN_HB)]
        out.append(tuple(_chunk_batch([per_head[h][s] for h in range(GDN_HB)]) for s in range(GDN_CONV)))
    return out


def _gdn_forward(p_cat, cwpack, gpar):
    t = p_cat.shape[0]
    n_chunks = t // GDN_TILE

    def body(xq, xk, xv, hq, hk, hv, z, ab, cw, gp, out, st_out, s_scr):
        n, grp = pl.program_id(0), pl.program_id(1)

        hsl = _head_range(grp, GDN_HB, GDN_HEADS)

        @pl.when(n == 0)
        def _():
            s_scr[hsl] = jnp.zeros((GDN_HB, LANES, LANES), F32)

        live = (n > 0).astype(F32)
        oha, ohb = _onehots(grp)
        s0 = s_scr[hsl]
        st_out[...] = s0
        taps = _gdn_taps((xq, xk, xv), (hq, hk, hv), live)
        zb = _chunk_batch([_lane_block(z, h) for h in range(GDN_HB)])
        abb = _chunk_batch([ab[...]] * GDN_HB)
        o, s1 = _gdn_head(*taps, zb, abb, s0, cw[...], gp[...], oha, ohb)
        for h, tile in enumerate(_head_tiles(o)):
            out[:, h * LANES:(h + 1) * LANES] = tile.astype(out.dtype)
        s_scr[hsl] = s1

    return pl.pallas_call(
        body, name="gdn_fwd", grid=(n_chunks, GDN_HEADS // GDN_HB),
        in_specs=_gdn_specs(lambda n: n),
        out_specs=(pl.BlockSpec((GDN_TILE, GDN_HB * LANES), lambda n, g: (n, g)),
                   pl.BlockSpec((None, GDN_HB, LANES, LANES), lambda n, g: (n, g, 0, 0))),
        out_shape=(jax.ShapeDtypeStruct((t, GDN_WIDTH), BF16),
                   jax.ShapeDtypeStruct((n_chunks, GDN_HEADS, LANES, LANES), F32)),
        scratch_shapes=[pltpu.VMEM((GDN_HEADS, LANES, LANES), F32)],
        compiler_params=_cparams(("arbitrary", "arbitrary")),
    )(p_cat, p_cat, p_cat, p_cat, p_cat, p_cat, p_cat, p_cat, cwpack, gpar)


def _gdn_backward(p_cat, cwpack, gpar, states, d_out):
    t = p_cat.shape[0]
    n_chunks = t // GDN_TILE
    last = n_chunks - 1

    def body(xq, xk, xv, hq, hk, hv, z, ab, cw, gp, st, dy, dq, dk, dv, dz, dab, dcw, dgp, ds_scr, car_scr):
        n, grp = pl.program_id(0), pl.program_id(1)

        hsl = _head_range(grp, GDN_HB, GDN_HEADS)

        @pl.when(n == 0)
        def _():
            ds_scr[hsl] = jnp.zeros((GDN_HB, LANES, LANES), F32)
            car_scr[hsl] = jnp.zeros((GDN_HB, 3 * GDN_CONV, SUBLANES, LANES), F32)

        @pl.when((n == 0) & (grp == 0))
        def _():
            dcw[...] = jnp.zeros(dcw.shape, F32)
            dgp[...] = jnp.zeros(dgp.shape, F32)

        live = (n < last).astype(F32)
        oha, ohb = _onehots(grp)
        fn = functools.partial(_gdn_head, oha=oha, ohb=ohb)
        taps = _gdn_taps((xq, xk, xv), (hq, hk, hv), live)
        zb = _chunk_batch([_lane_block(z, h) for h in range(GDN_HB)])
        abb = _chunk_batch([ab[...]] * GDN_HB)
        _, vjp = jax.vjp(fn, *taps, zb, abb, st[...], cw[...], gp[...])
        g = vjp((_chunk_batch([_lane_block(dy, h) for h in range(GDN_HB)]), ds_scr[hsl]))
        outs = (dq, dk, dv)
        tap_tiles = [[_head_tiles(g[j][s]) for s in range(GDN_CONV)] for j in range(3)]
        dz_tiles = _head_tiles(g[3])
        for i in range(GDN_HB):
            sl = slice(i * LANES, (i + 1) * LANES)
            h = _head_id(grp, i, GDN_HB, GDN_HEADS)
            for j in range(3):
                tot = tap_tiles[j][0][i]
                for s in range(1, GDN_CONV):
                    slot = j * GDN_CONV + s
                    tot = tot + _unshift_rows(tap_tiles[j][s][i], car_scr[h, slot], s)
                    car_scr[h, slot] = tap_tiles[j][s][i][0:SUBLANES, :]
                outs[j][:, sl] = tot.astype(outs[j].dtype)
            dz[:, sl] = dz_tiles[i].astype(dz.dtype)
        dab_tiles = _head_tiles(g[4])
        dab_sum = dab_tiles[0]
        for h in range(1, GDN_HB):
            dab_sum = dab_sum + dab_tiles[h]
        ds_scr[hsl] = g[5]
        dcw[hsl] += g[6]
        dgp[0] += g[7]

        @pl.when(grp == 0)
        def _():
            dab[...] = dab_sum

        @pl.when(grp > 0)
        def _():
            dab[...] += dab_sum

    rev = lambda n: last - n
    in_specs = _gdn_specs(rev) + [
        pl.BlockSpec((None, GDN_HB, LANES, LANES), lambda n, g: (rev(n), g, 0, 0)),
        pl.BlockSpec((GDN_TILE, GDN_HB * LANES), lambda n, g: (rev(n), g)),
    ]
    blk = pl.BlockSpec((GDN_TILE, GDN_HB * LANES), lambda n, g: (rev(n), g))
    return pl.pallas_call(
        body, name="gdn_bwd", grid=(n_chunks, GDN_HEADS // GDN_HB),
        in_specs=in_specs,
        out_specs=(blk, blk, blk, blk,
                   pl.BlockSpec((GDN_TILE, LANES), lambda n, h: (rev(n), 0)),
                   pl.BlockSpec((GDN_HEADS, 3, SUBLANES, LANES), lambda n, h: (0, 0, 0, 0)),
                   pl.BlockSpec((GDN_HEADS, SUBLANES, LANES), lambda n, h: (0, 0, 0))),
        out_shape=(jax.ShapeDtypeStruct((t, GDN_WIDTH), BF16), jax.ShapeDtypeStruct((t, GDN_WIDTH), BF16),
                   jax.ShapeDtypeStruct((t, GDN_WIDTH), BF16), jax.ShapeDtypeStruct((t, GDN_WIDTH), BF16),
                   jax.ShapeDtypeStruct((t, LANES), F32),
                   jax.ShapeDtypeStruct((GDN_HEADS, 3, SUBLANES, LANES), F32),
                   jax.ShapeDtypeStruct((GDN_HEADS, SUBLANES, LANES), F32)),
        scratch_shapes=[pltpu.VMEM((GDN_HEADS, LANES, LANES), F32),
                        pltpu.VMEM((GDN_HEADS, 3 * GDN_CONV, SUBLANES, LANES), F32)],
        compiler_params=_cparams(("arbitrary", "arbitrary")),
    )(p_cat, p_cat, p_cat, p_cat, p_cat, p_cat, p_cat, p_cat, cwpack, gpar, states, d_out)


def _pick(n, options):
    for o in options:
        if n % o == 0:
            return o
    raise ValueError(f"no tile for {n}")


MM_VMEM_BUDGET = 30 * 1024 * 1024
MM_MIN_STEPS = 8


def _mm_tiles(mode, m, n, k, out_bytes):
    tms = [t for t in (2048, 1024, 768, 512, 256, 128, 64) if m % t == 0 and (mode != 'tn' or t % LANES == 0)]
    tns = [t for t in (1408, 1024, 768, 512, 256, 128) if n % t == 0]
    tks = [t for t in (2048, 1920, 1408, 1024, 512, 256, 128, 64) if k % t == 0]
    best, best_key = None, None
    for tm in tms:
        for tn in tns:
            for tk in tks:
                nk = k // tk
                vmem = 2 * (tm * tk * 2 + tk * tn * 2 + tm * tn * out_bytes) + (tm * tn * 4 if nk > 1 else 0)
                steps = (m // tm) * (n // tn) * nk
                if vmem > MM_VMEM_BUDGET:
                    continue
                key = (steps >= MM_MIN_STEPS, tn if mode == 'tn' else 0, tm * tn * tk, -nk)
                if best_key is None or key > best_key:
                    best, best_key = (tm, tn, tk), key
    if best is None:
        raise ValueError(f"no matmul tile for {mode} {m}x{n}x{k}")
    return best


_MM_DIMS = {'nn': (((1,), (0,)), ((), ())), 'nt': (((1,), (1,)), ((), ())), 'tn': (((0,), (0,)), ((), ()))}


def _matmul(a, b, mode, out_dtype, name):
    if mode == 'nn':
        (m, k), (k2, n) = a.shape, b.shape
    elif mode == 'nt':
        (m, k), (n, k2) = a.shape, b.shape
    else:
        (k, m), (k2, n) = a.shape, b.shape
    assert k == k2, (a.shape, b.shape, mode)
    tm, tn, tk = _mm_tiles(mode, m, n, k, jnp.dtype(out_dtype).itemsize)
    nk = k // tk
    dims = _MM_DIMS[mode]

    def body(a_ref, b_ref, o_ref, acc_ref):
        kk = pl.program_id(2)
        part = lax.dot_general(a_ref[...], b_ref[...], dims, preferred_element_type=F32)
        if nk == 1:
            o_ref[...] = part.astype(o_ref.dtype)
            return

        @pl.when(kk == 0)
        def _():
            acc_ref[...] = part

        @pl.when((kk > 0) & (kk < nk - 1))
        def _():
            acc_ref[...] += part

        @pl.when(kk == nk - 1)
        def _():
            o_ref[...] = (acc_ref[...] + part).astype(o_ref.dtype)

    a_spec = pl.BlockSpec((tk, tm), lambda i, j, kk: (kk, i)) if mode == 'tn' else pl.BlockSpec((tm, tk), lambda i, j, kk: (i, kk))
    b_spec = pl.BlockSpec((tn, tk), lambda i, j, kk: (j, kk)) if mode == 'nt' else pl.BlockSpec((tk, tn), lambda i, j, kk: (kk, j))
    return pl.pallas_call(
        body, name=name, grid=(m // tm, n // tn, nk),
        in_specs=[a_spec, b_spec],
        out_specs=pl.BlockSpec((tm, tn), lambda i, j, kk: (i, j)),
        out_shape=jax.ShapeDtypeStruct((m, n), out_dtype),
        scratch_shapes=[pltpu.VMEM((tm, tn), F32)],
        compiler_params=_cparams(("parallel", "parallel", "arbitrary")),
    )(a, b)


ROW_TILE = 256


def _row_specs(rows, tm):
    return [pl.BlockSpec((tm, w), lambda i, ci=ci: (i, ci)) for (_, w, ci) in rows]


def _rw_forward(fn, rows, pars, outs, name):
    t = rows[0][0].shape[0]
    tm = min(ROW_TILE, t)
    nr, npar = len(rows), len(pars)

    def body(*refs):
        vals = [r[...].astype(F32) for r in refs[:nr]] + [p[...] for p in refs[nr:nr + npar]]
        res = fn(*vals)
        for o, v in zip(refs[nr + npar:], res):
            o[...] = v.astype(o.dtype)

    return pl.pallas_call(
        body, name=name, grid=(t // tm,),
        in_specs=_row_specs(rows, tm) + [pl.BlockSpec(p.shape, lambda i: (0, 0)) for p in pars],
        out_specs=tuple(pl.BlockSpec((tm, w), lambda i: (i, 0)) for (w, _) in outs),
        out_shape=tuple(jax.ShapeDtypeStruct((t, w), dt) for (w, dt) in outs),
        compiler_params=_cparams(("parallel",)),
    )(*[r[0] for r in rows], *pars)


def _rw_backward(fn, rows, pars, cots, drow_dtypes, name):
    t = rows[0][0].shape[0]
    tm = min(ROW_TILE, t)
    nr, npar, nc = len(rows), len(pars), len(cots)
    keep = [i for i, dt in enumerate(drow_dtypes) if dt is not None]

    def body(*refs):
        vals = [r[...].astype(F32) for r in refs[:nr]] + [p[...] for p in refs[nr:nr + npar]]
        cvals = tuple(c[...].astype(F32) for c in refs[nr + npar:nr + npar + nc])
        orefs = refs[nr + npar + nc:]
        _, vjp = jax.vjp(fn, *vals)
        g = vjp(cvals)
        for o, i in zip(orefs[:len(keep)], keep):
            o[...] = g[i].astype(o.dtype)
        first = pl.program_id(0) == 0
        for o, gi in zip(orefs[len(keep):], g[nr:]):
            @pl.when(first)
            def _(o=o, gi=gi):
                o[...] = gi

            @pl.when(jnp.logical_not(first))
            def _(o=o, gi=gi):
                o[...] += gi

    out_specs = [pl.BlockSpec((tm, rows[i][1]), lambda i_: (i_, 0)) for i in keep] + \
                [pl.BlockSpec(p.shape, lambda i_: (0, 0)) for p in pars]
    out_shape = [jax.ShapeDtypeStruct((t, rows[i][1]), drow_dtypes[i]) for i in keep] + \
                [jax.ShapeDtypeStruct(p.shape, F32) for p in pars]
    return pl.pallas_call(
        body, name=name, grid=(t // tm,),
        in_specs=_row_specs(rows, tm) + [pl.BlockSpec(p.shape, lambda i: (0, 0)) for p in pars] + _row_specs(cots, tm),
        out_specs=tuple(out_specs), out_shape=tuple(out_shape),
        compiler_params=_cparams(("arbitrary",)),
    )(*[r[0] for r in rows], *pars, *[c[0] for c in cots])


def _norm_fn(x, g):
    return (_rms(x, g),)


def _norm_skip_fn(x, g):
    return _rms(x, g), x


def _merge_fn(ga, gb, ya, yb):
    return (_sigmoid(ga) * ya + _sigmoid(gb) * yb,)


def _res_norm_fn(x, mo, g):
    x1 = x + mo
    return x1, _rms(x1, g)


def _loss_head(x1, fo, gf, target, name):
    t, d = x1.shape
    tm = min(ROW_TILE, t)

    def tile_loss(x2, g, tgt):
        err = _rms(x2, g) - tgt
        per_row = jnp.sum(err * err, axis=-1, keepdims=True) * (0.5 / d)
        return jnp.sum(per_row, axis=0, keepdims=True)

    def body(x1_ref, fo_ref, g_ref, t_ref, loss_ref, dx_ref, dxb_ref, dg_ref):
        x2 = x1_ref[...] + fo_ref[...]
        val, vjp = jax.vjp(functools.partial(tile_loss, tgt=t_ref[...]), x2, g_ref[...])
        dx2, dg = vjp(jnp.ones((1, 1), F32))
        dx_ref[...] = dx2
        dxb_ref[...] = dx2.astype(BF16)
        first = pl.program_id(0) == 0

        @pl.when(first)
        def _():
            loss_ref[...] = jnp.broadcast_to(val, loss_ref.shape)
            dg_ref[...] = dg

        @pl.when(jnp.logical_not(first))
        def _():
            loss_ref[...] += jnp.broadcast_to(val, loss_ref.shape)
            dg_ref[...] += dg

    row = pl.BlockSpec((tm, d), lambda i: (i, 0))
    vec = pl.BlockSpec((1, d), lambda i: (0, 0))
    return pl.pallas_call(
        body, name=name, grid=(t // tm,),
        in_specs=[row, row, vec, row],
        out_specs=(pl.BlockSpec((1, LANES), lambda i: (0, 0)), row, row, vec),
        out_shape=(jax.ShapeDtypeStruct((1, LANES), F32), jax.ShapeDtypeStruct((t, d), F32),
                   jax.ShapeDtypeStruct((t, d), BF16), jax.ShapeDtypeStruct((1, d), F32)),
        compiler_params=_cparams(("arbitrary",)),
    )(x1, fo, gf, target)


FFN_TILE_ROWS = 512
FFN_TILE_COLS = 256
FFN_COL_BLOCKS = FFN_HIDDEN // FFN_TILE_COLS


def _conv3_past(x, halo, w):
    rows = lax.broadcasted_iota(jnp.int32, x.shape, 0)
    x1 = jnp.where(rows == 0, halo[7:8, :], pltpu.roll(x, 1, 0))
    x2 = jnp.where(rows == 0, halo[6:7, :], jnp.where(rows == 1, halo[7:8, :], pltpu.roll(x, 2, 0)))
    return x * w[2:3] + x1 * w[1:2] + x2 * w[0:1], x1, x2


def _ffn_in_specs(tm, imap, jmap):
    per = tm // SUBLANES
    tile = lambda off: pl.BlockSpec((tm, FFN_TILE_COLS), lambda *g: (imap(*g), off + jmap(*g) % FFN_COL_BLOCKS))
    halo = lambda off: pl.BlockSpec((SUBLANES, FFN_TILE_COLS),
                                    lambda *g: (jnp.maximum(imap(*g) * per - 1, 0), off + jmap(*g) % FFN_COL_BLOCKS))
    wsp = lambda off: pl.BlockSpec((FFN_CONV, FFN_TILE_COLS), lambda *g: (0, off + jmap(*g) % FFN_COL_BLOCKS))
    return [tile(0), halo(0), wsp(0), tile(FFN_COL_BLOCKS), halo(FFN_COL_BLOCKS), wsp(FFN_COL_BLOCKS)]


def _ffn_act_forward(hpre, cw):
    t = hpre.shape[0]
    tm = min(FFN_TILE_ROWS, t)

    def body(hg, pg, wg, hu, pu, wu, out):
        live = (pl.program_id(0) > 0).astype(F32)
        cg, _, _ = _conv3_past(hg[...], pg[...] * live, wg[...])
        cu, _, _ = _conv3_past(hu[...], pu[...] * live, wu[...])
        out[...] = (_silu(cg) * cu).astype(out.dtype)

    return pl.pallas_call(
        body, name="ffn_act_fwd", grid=(t // tm, FFN_COL_BLOCKS),
        in_specs=_ffn_in_specs(tm, lambda i, j: i, lambda i, j: j),
        out_specs=pl.BlockSpec((tm, FFN_TILE_COLS), lambda i, j: (i, j)),
        out_shape=jax.ShapeDtypeStruct((t, FFN_HIDDEN), BF16),
        compiler_params=_cparams(("parallel", "parallel")),
    )(hpre, hpre, cw, hpre, hpre, cw)


def _conv3_future(d, nxt, w):
    tm = d.shape[0]
    rows = lax.broadcasted_iota(jnp.int32, d.shape, 0)
    d1 = jnp.where(rows == tm - 1, nxt[0:1, :], pltpu.roll(d, tm - 1, 0))
    d2 = jnp.where(rows == tm - 1, nxt[1:2, :], jnp.where(rows == tm - 2, nxt[0:1, :], pltpu.roll(d, tm - 2, 0)))
    return d * w[2:3] + d1 * w[1:2] + d2 * w[0:1]


def _ffn_backward(hpre, cw, dact):
    t = hpre.shape[0]
    tm = min(FFN_TILE_ROWS, t)
    n_tiles = t // tm
    per = tm // SUBLANES

    def d_conv_out(cg, cu, d):
        s = _sigmoid(cg)
        return d * cu * s * (1.0 + cg * (1.0 - s)), d * cg * s

    def body(hg, pg, ng, wg, hu, pu, nu, wu, da, dan, dhg, dhu, dwg, dwu):
        i = pl.program_id(1)
        live_prev = (i > 0).astype(F32)
        live_next = (i < n_tiles - 1).astype(F32)
        xg, xu = hg[...], hu[...]
        cg, g1, g2 = _conv3_past(xg, pg[...] * live_prev, wg[...])
        cu, u1, u2 = _conv3_past(xu, pu[...] * live_prev, wu[...])
        dg, du = d_conv_out(cg, cu, da[...])
        cgn, _, _ = _conv3_past(ng[...], hg[tm - SUBLANES:tm, :], wg[...])
        cun, _, _ = _conv3_past(nu[...], hu[tm - SUBLANES:tm, :], wu[...])
        dgn, dun = d_conv_out(cgn, cun, dan[...] * live_next)
        dhg[...] = _conv3_future(dg, dgn, wg[...]).astype(dhg.dtype)
        dhu[...] = _conv3_future(du, dun, wu[...]).astype(dhu.dtype)
        sums_g = [jnp.sum(xs * dg, axis=0, keepdims=True) for xs in (g2, g1, xg)]
        sums_u = [jnp.sum(xs * du, axis=0, keepdims=True) for xs in (u2, u1, xu)]

        @pl.when(i == 0)
        def _():
            for r_ in range(FFN_CONV):
                dwg[r_:r_ + 1, :] = sums_g[r_]
                dwu[r_:r_ + 1, :] = sums_u[r_]

        @pl.when(i > 0)
        def _():
            for r_ in range(FFN_CONV):
                dwg[r_:r_ + 1, :] += sums_g[r_]
                dwu[r_:r_ + 1, :] += sums_u[r_]

    nb = FFN_COL_BLOCKS
    nxt = lambda i: jnp.minimum((i + 1) * per, t // SUBLANES - 1)
    prv = lambda i: jnp.maximum(i * per - 1, 0)
    half = lambda off: [pl.BlockSpec((tm, FFN_TILE_COLS), lambda j, i: (i, off + j)),
                        pl.BlockSpec((SUBLANES, FFN_TILE_COLS), lambda j, i: (prv(i), off + j)),
                        pl.BlockSpec((SUBLANES, FFN_TILE_COLS), lambda j, i: (nxt(i), off + j)),
                        pl.BlockSpec((FFN_CONV, FFN_TILE_COLS), lambda j, i: (0, off + j))]
    tile = pl.BlockSpec((tm, FFN_TILE_COLS), lambda j, i: (i, j))
    taps = pl.BlockSpec((FFN_CONV, FFN_TILE_COLS), lambda j, i: (0, j))
    return pl.pallas_call(
        body, name="ffn_bwd", grid=(nb, n_tiles),
        in_specs=half(0) + half(nb) + [tile, pl.BlockSpec((SUBLANES, FFN_TILE_COLS), lambda j, i: (nxt(i), j))],
        out_specs=(tile, tile, taps, taps),
        out_shape=(jax.ShapeDtypeStruct((t, FFN_HIDDEN), BF16), jax.ShapeDtypeStruct((t, FFN_HIDDEN), BF16),
                   jax.ShapeDtypeStruct((FFN_CONV, FFN_HIDDEN), F32), jax.ShapeDtypeStruct((FFN_CONV, FFN_HIDDEN), F32)),
        compiler_params=_cparams(("parallel", "arbitrary")),
    )(hpre, hpre, hpre, cw, hpre, hpre, hpre, cw, dact, dact)


def _my_place():
    x, y, c = lax.axis_index("x"), lax.axis_index("y"), lax.axis_index("c")
    return x, y, c, 4 * x + 2 * y + c


N_CHIPS = 4


def _remote(src, dst, send_sem, recv_sem, dev):
    return pltpu.make_async_remote_copy(src_ref=src, dst_ref=dst, send_sem=send_sem, recv_sem=recv_sem, device_id=dev,
                                        device_id_type=pl.DeviceIdType.MESH)


def _chip_peer(x, y, k):
    return x ^ ((k >> 1) & 1), y ^ (k & 1)


def _all_gather_two_level(shard, name):
    r, w = shard.shape

    def body(src, out, send_sems, recv_sems, local_sem):
        x, y, c, me = _my_place()
        sibling = (x, y, 1 - c)
        mine = pltpu.make_async_copy(src, out.at[me], local_sem)
        mine.start()
        first = [_remote(src, out.at[me], send_sems.at[0], recv_sems.at[0], sibling)]
        for k in range(1, N_CHIPS):
            px, py = _chip_peer(x, y, k)
            first.append(_remote(src, out.at[me], send_sems.at[k], recv_sems.at[k], (px, py, c)))
        for cp in first:
            cp.start()
        passed = []
        for k in range(1, N_CHIPS):
            px, py = _chip_peer(x, y, k)
            landed = out.at[me ^ (2 * k)]
            _remote(src, landed, send_sems.at[k], recv_sems.at[k], (px, py, c)).wait_recv()
            fwd = _remote(landed, landed, send_sems.at[N_CHIPS - 1 + k], recv_sems.at[N_CHIPS - 1 + k], sibling)
            fwd.start()
            passed.append(fwd)
        _remote(src, out.at[me ^ 1], send_sems.at[0], recv_sems.at[0], sibling).wait_recv()
        for k in range(1, N_CHIPS):
            got = out.at[(me ^ 1) ^ (2 * k)]
            _remote(got, got, send_sems.at[N_CHIPS - 1 + k], recv_sems.at[N_CHIPS - 1 + k], sibling).wait_recv()
        for cp in first + passed:
            cp.wait_send()
        mine.wait()

    return pl.pallas_call(
        body, name=name,
        in_specs=[pl.BlockSpec(memory_space=pl.ANY)],
        out_specs=pl.BlockSpec(memory_space=pl.ANY),
        out_shape=jax.ShapeDtypeStruct((N_DEV, r, w), shard.dtype),
        scratch_shapes=[pltpu.SemaphoreType.DMA((N_DEV - 1,)), pltpu.SemaphoreType.DMA((N_DEV - 1,)), pltpu.SemaphoreType.DMA],
    )(shard)


def _device_peer(x, y, c, k):
    px, py, pc = x ^ ((k >> 2) & 1), y ^ ((k >> 1) & 1), c ^ (k & 1)
    return (px, py, pc), 4 * px + 2 * py + pc


_HBM = pl.BlockSpec(memory_space=pltpu.HBM)
_SEM = pl.BlockSpec(memory_space=pltpu.SEMAPHORE)


def _gather_start(shard, name):
    def body(src, land, send_sems, recv_sems, src_thru, land_thru, token):
        x, y, c, me = _my_place()
        for k in range(1, N_DEV):
            dev, _ = _device_peer(x, y, c, k)
            _remote(src, land.at[me], send_sems.at[k], recv_sems.at[k], dev).start()
        token[...] = jnp.zeros_like(token)

    landing = lax.empty((N_DEV,) + shard.shape, shard.dtype)
    return pl.pallas_call(
        body, name=name,
        out_shape=(pltpu.SemaphoreType.DMA((N_DEV,)), pltpu.SemaphoreType.DMA((N_DEV,)), pltpu.HBM(shard.shape, shard.dtype),
                   pltpu.HBM(landing.shape, landing.dtype), jax.ShapeDtypeStruct((SUBLANES, LANES), F32)),
        in_specs=(_HBM, _HBM), out_specs=(_SEM, _SEM, _HBM, _HBM, pl.BlockSpec(memory_space=pltpu.VMEM)),
        input_output_aliases={0: 2, 1: 3},
        compiler_params=pltpu.CompilerParams(has_side_effects=pltpu.SideEffectType.DATAFLOW_SIDE_EFFECTING),
    )(pltpu.with_memory_space_constraint(shard, pltpu.HBM), pltpu.with_memory_space_constraint(landing, pltpu.HBM))


def _gather_wait(send_sems, recv_sems, shard, landing, after, name):
    n_after = len(after)

    def body(*refs):
        src, land, send_sems, recv_sems = refs[:4]
        x, y, c, _ = _my_place()
        for k in range(1, N_DEV):
            dev, idx = _device_peer(x, y, c, k)
            cp = _remote(src, land.at[idx], send_sems.at[k], recv_sems.at[k], dev)
            cp.wait_send()
            cp.wait_recv()

    return pl.pallas_call(
        body, name=name,
        out_shape=(pltpu.HBM(shard.shape, shard.dtype), pltpu.HBM(landing.shape, landing.dtype)),
        in_specs=(_HBM, _HBM, _SEM, _SEM) + (pl.BlockSpec(memory_space=pl.ANY),) * n_after, out_specs=(_HBM, _HBM),
        input_output_aliases={0: 0, 1: 1},
        compiler_params=pltpu.CompilerParams(has_side_effects=pltpu.SideEffectType.DATAFLOW_SIDE_EFFECTING),
    )(shard, landing, send_sems, recv_sems, *after)[1]


def _slab_push_start(slabs, name):
    na = len(slabs)

    def body(*refs):
        srcs, lands = refs[:na], refs[na:2 * na]
        send_sems, recv_sems = refs[2 * na], refs[2 * na + 1]
        token = refs[-1]
        x, y, c, me = _my_place()
        for i in range(na):
            for k in range(1, N_DEV):
                dev, idx = _device_peer(x, y, c, k)
                s = i * N_DEV + k
                _remote(srcs[i].at[idx], lands[i].at[me], send_sems.at[s], recv_sems.at[s], dev).start()
        token[...] = jnp.zeros_like(token)

    hbm_shapes = [pltpu.HBM(a.shape, a.dtype) for a in slabs]
    ins = [pltpu.with_memory_space_constraint(a, pltpu.HBM) for a in slabs]
    ins += [pltpu.with_memory_space_constraint(lax.empty(a.shape, a.dtype), pltpu.HBM) for a in slabs]
    out = pl.pallas_call(
        body, name=name,
        out_shape=(pltpu.SemaphoreType.DMA((na * N_DEV,)), pltpu.SemaphoreType.DMA((na * N_DEV,)), *hbm_shapes, *hbm_shapes,
                   jax.ShapeDtypeStruct((SUBLANES, LANES), F32)),
        in_specs=(_HBM,) * (2 * na), out_specs=(_SEM, _SEM) + (_HBM,) * (2 * na) + (pl.BlockSpec(memory_space=pltpu.VMEM),),
        input_output_aliases={i: 2 + i for i in range(2 * na)},
        compiler_params=pltpu.CompilerParams(has_side_effects=pltpu.SideEffectType.DATAFLOW_SIDE_EFFECTING),
    )(*ins)
    return out[0], out[1], list(out[2:2 + na]), list(out[2 + na:2 + 2 * na]), out[-1]


def _slab_push_wait(send_sems, recv_sems, slabs, landings, after, name):
    na = len(slabs)

    def body(*refs):
        srcs, lands = refs[:na], refs[na:2 * na]
        send_sems, recv_sems = refs[2 * na], refs[2 * na + 1]
        x, y, c, me = _my_place()
        for i in range(na):
            for k in range(1, N_DEV):
                dev, idx = _device_peer(x, y, c, k)
                s = i * N_DEV + k
                cp = _remote(srcs[i].at[idx], lands[i].at[idx], send_sems.at[s], recv_sems.at[s], dev)
                cp.wait_send()
                cp.wait_recv()

    hbm_shapes = tuple(pltpu.HBM(a.shape, a.dtype) for a in slabs)
    out = pl.pallas_call(
        body, name=name, out_shape=hbm_shapes + hbm_shapes,
        in_specs=(_HBM,) * (2 * na) + (_SEM, _SEM) + (pl.BlockSpec(memory_space=pl.ANY),) * len(after),
        out_specs=(_HBM,) * (2 * na), input_output_aliases={i: i for i in range(2 * na)},
        compiler_params=pltpu.CompilerParams(has_side_effects=pltpu.SideEffectType.DATAFLOW_SIDE_EFFECTING),
    )(*slabs, *landings, send_sems, recv_sems, *after)
    return list(out[na:])


def _pair_exchange(arrays, name):
    na = len(arrays)

    def body(*refs):
        srcs, dsts, (send_sems, recv_sems) = refs[:na], refs[na:2 * na], refs[2 * na:]
        x, y, c, _ = _my_place()
        sibling = (x, y, 1 - c)
        copies = []
        for i in range(na):
            for q in range(N_CHIPS):
                s = i * N_CHIPS + q
                copies.append(_remote(srcs[i].at[2 * q + 1 - c], dsts[i].at[q], send_sems.at[s], recv_sems.at[s], sibling))
        for cp in copies:
            cp.start()
        for cp in copies:
            cp.wait_recv()
        for cp in copies:
            cp.wait_send()

    hbm = pl.BlockSpec(memory_space=pl.ANY)
    return pl.pallas_call(
        body, name=name, in_specs=[hbm] * na, out_specs=tuple([hbm] * na),
        out_shape=tuple(jax.ShapeDtypeStruct((N_CHIPS,) + a.shape[1:], a.dtype) for a in arrays),
        scratch_shapes=[pltpu.SemaphoreType.DMA((na * N_CHIPS,)), pltpu.SemaphoreType.DMA((na * N_CHIPS,))],
    )(*arrays)


ELEMENTWISE_COLS = 256


def _pair_sum(slabs, recv, core, out_dtype, name):
    _, r, w = slabs.shape
    tc = ELEMENTWISE_COLS

    def body(core_ref, mine, theirs, out):
        out[...] = (mine[...] + theirs[...]).astype(out.dtype)

    grid_spec = pltpu.PrefetchScalarGridSpec(
        num_scalar_prefetch=1, grid=(N_CHIPS, w // tc),
        in_specs=[pl.BlockSpec((None, r, tc), lambda q, j, core_ref: (2 * q + core_ref[0], 0, j)),
                  pl.BlockSpec((None, r, tc), lambda q, j, core_ref: (q, 0, j))],
        out_specs=pl.BlockSpec((None, r, tc), lambda q, j, core_ref: (q, 0, j)))
    return pl.pallas_call(body, name=name, grid_spec=grid_spec,
                          out_shape=jax.ShapeDtypeStruct((N_CHIPS, r, w), out_dtype),
                          compiler_params=_cparams(("parallel", "parallel")))(core, slabs, recv)


def _chip_exchange(arrays, name):
    na = len(arrays)

    def body(*refs):
        srcs, dsts, (send_sems, recv_sems, local_sems) = refs[:na], refs[na:2 * na], refs[2 * na:]
        x, y, c, _ = _my_place()
        chip = 2 * x + y
        own = [pltpu.make_async_copy(srcs[i].at[chip], dsts[i].at[chip], local_sems.at[i]) for i in range(na)]
        for cp in own:
            cp.start()
        sends, arrivals = [], []
        for i in range(na):
            for k in range(1, N_CHIPS):
                px, py = _chip_peer(x, y, k)
                s = i * N_CHIPS + k
                sends.append(_remote(srcs[i].at[chip ^ k], dsts[i].at[chip], send_sems.at[s], recv_sems.at[s], (px, py, c)))
                arrivals.append(_remote(srcs[i].at[chip], dsts[i].at[chip ^ k], send_sems.at[s], recv_sems.at[s], (px, py, c)))
        for cp in sends:
            cp.start()
        for cp in arrivals:
            cp.wait_recv()
        for cp in sends:
            cp.wait_send()
        for cp in own:
            cp.wait()

    hbm = pl.BlockSpec(memory_space=pl.ANY)
    return pl.pallas_call(
        body, name=name, in_specs=[hbm] * na, out_specs=tuple([hbm] * na),
        out_shape=tuple(jax.ShapeDtypeStruct(a.shape, a.dtype) for a in arrays),
        scratch_shapes=[pltpu.SemaphoreType.DMA((na * N_CHIPS,)), pltpu.SemaphoreType.DMA((na * N_CHIPS,)),
                        pltpu.SemaphoreType.DMA((na,))],
    )(*arrays)


def _adamw_update(g, w, m, v):
    c1 = 1.0 / (1.0 - ADAM_B1 ** ADAM_STEP)
    c2 = 1.0 / (1.0 - ADAM_B2 ** ADAM_STEP)
    mn = ADAM_B1 * m + (1.0 - ADAM_B1) * g
    vn = ADAM_B2 * v + (1.0 - ADAM_B2) * (g * g)
    return -ADAM_LR * ((mn * c1) / (jnp.sqrt(vn * c2) + ADAM_EPS) + ADAM_WD * w), mn, vn


def _reduce_adamw(parts, w, m, v, name):
    n_parts, r, wd = parts.shape
    tc = ELEMENTWISE_COLS

    def body(p_ref, w_ref, m_ref, v_ref, g_out, d_out, m_out, v_out):
        g = p_ref[0].astype(F32)
        for s in range(1, n_parts):
            g = g + p_ref[s].astype(F32)
        g_out[...] = g
        d_out[...], m_out[...], v_out[...] = _adamw_update(g, w_ref[...], m_ref[...], v_ref[...])

    blk = pl.BlockSpec((r, tc), lambda j: (0, j))
    shp = jax.ShapeDtypeStruct((r, wd), F32)
    return pl.pallas_call(
        body, name=name, grid=(wd // tc,),
        in_specs=[pl.BlockSpec((n_parts, r, tc), lambda j: (0, 0, j)), blk, blk, blk],
        out_specs=(blk, blk, blk, blk), out_shape=(shp, shp, shp, shp),
        compiler_params=_cparams(("parallel",)),
    )(parts, w, m, v)


def _reduce_landed_adamw(landing, own, me, w, m, v, name):
    n_parts, r, wd = landing.shape
    tc = ELEMENTWISE_COLS

    def body(me_ref, land_ref, own_ref, w_ref, m_ref, v_ref, g_out, d_out, m_out, v_out):
        mine = own_ref[...].astype(F32)
        g = None
        for s in range(n_parts):
            part = jnp.where(me_ref[0] == s, mine, land_ref[s].astype(F32))
            g = part if g is None else g + part
        g_out[...] = g
        d_out[...], m_out[...], v_out[...] = _adamw_update(g, w_ref[...], m_ref[...], v_ref[...])

    blk = pl.BlockSpec((r, tc), lambda j, me_ref: (0, j))
    shp = jax.ShapeDtypeStruct((r, wd), F32)
    grid_spec = pltpu.PrefetchScalarGridSpec(
        num_scalar_prefetch=1, grid=(wd // tc,),
        in_specs=[pl.BlockSpec((n_parts, r, tc), lambda j, me_ref: (0, 0, j)),
                  pl.BlockSpec((None, r, tc), lambda j, me_ref: (me_ref[0], 0, j)), blk, blk, blk],
        out_specs=(blk, blk, blk, blk))
    return pl.pallas_call(body, name=name, grid_spec=grid_spec, out_shape=(shp, shp, shp, shp),
                          compiler_params=_cparams(("parallel",)))(me, landing, own, w, m, v)


PACK_W = 1024


def _pad_heads(a, slots):
    lead = a.shape[:-1]
    a = a.reshape(lead + (slots, RWKV_HEAD_DIM))
    a = jnp.pad(a, [(0, 0)] * (len(lead) + 1) + [(0, LANES - RWKV_HEAD_DIM)])
    return a.reshape(lead + (slots * LANES,))


def _unpad_heads(a, slots):
    lead = a.shape[:-1]
    return a.reshape(lead + (slots, LANES))[..., :RWKV_HEAD_DIM].reshape(lead + (slots * RWKV_HEAD_DIM,))


def _flat_pack(arrs, dtype, row_mult):
    flat = jnp.concatenate([a.reshape(-1).astype(dtype) for a in arrs])
    n = flat.shape[0]
    rows = -(-n // PACK_W)
    rows = -(-rows // row_mult) * row_mult
    return jnp.pad(flat, (0, rows * PACK_W - n)).reshape(rows, PACK_W)


def _row_pack(arrs, dtype, row_mult):
    parts = [a.astype(dtype) if a.shape[1] == PACK_W else a.astype(dtype).reshape(-1, PACK_W) for a in arrs]
    rows = sum(p.shape[0] for p in parts)
    pad = -(-rows // row_mult) * row_mult - rows
    return jnp.concatenate(parts + ([jnp.zeros((pad, PACK_W), dtype)] if pad else []), axis=0)


def _unpack_row_gathered(g, names, shard_shapes):
    out, r0 = {}, 0
    for n in names:
        s = shard_shapes[n]
        rows = s[0] * s[1] // PACK_W
        seg = g[:, r0:r0 + rows, :]
        r0 += rows
        if s[1] == PACK_W:
            assert SHARD_AXIS[n] == 0
            out[n] = seg.reshape(N_DEV * s[0], s[1])
        else:
            assert SHARD_AXIS[n] == 1
            out[n] = jnp.transpose(seg.reshape((N_DEV,) + tuple(s)), (1, 0, 2)).reshape(s[0], N_DEV * s[1])
    return out


def _unpack_gathered(g, names, shard_shapes):
    flat = g.reshape(N_DEV, -1)
    out, off = {}, 0
    for n in names:
        s = shard_shapes[n]
        size = s[0] * s[1]
        seg = flat[:, off:off + size].reshape((N_DEV,) + tuple(s))
        off += size
        if SHARD_AXIS[n] == 1:
            out[n] = jnp.transpose(seg, (1, 0, 2)).reshape(s[0], N_DEV * s[1])
        else:
            out[n] = seg.reshape(N_DEV * s[0], s[1])
    return out


def _shard_major(full, axis):
    a, b = full.shape
    if axis == 1:
        return jnp.transpose(full.reshape(a, N_DEV, b // N_DEV), (1, 0, 2)).reshape(N_DEV, -1)
    return full.reshape(N_DEV, -1)


def _prepare_weights(full, rep):
    w = full['w_in']
    d = w.shape[1]
    rkv = jnp.pad(w[0:1536].reshape(3 * RWKV_HEADS, RWKV_HEAD_DIM, d), ((0, 0), (0, LANES - RWKV_HEAD_DIM), (0, 0)))
    w_cat = jnp.concatenate([
        w[3848:4872], w[4872:5896], rkv.reshape(3 * RWKV_HEADS * LANES, d), w[1792:3328], w[3328:3840],
        w[1536:1792], jnp.pad(w[3840:3848], ((0, LANES - 8), (0, 0))), jnp.zeros((LANES, d), w.dtype)], axis=0)
    assert w_cat.shape[0] == CAT_W
    mu = rep['rwkv_mu']
    vecs = [mu[0:512], mu[512:1024], mu[1024:1536], rep['rwkv_w0'], rep['rwkv_a0'], rep['rwkv_k_k'], rep['rwkv_k_a'],
            rep['rwkv_ln_w'], rep['rwkv_ln_b'], rep['rwkv_r_k'].reshape(-1)]
    ppack = jnp.stack([jnp.pad(v.reshape(RWKV_HEADS, RWKV_HEAD_DIM), ((0, 0), (0, LANES - RWKV_HEAD_DIM))) for v in vecs], axis=1)
    ppack = jnp.pad(ppack, ((0, 0), (0, 16 - len(vecs)), (0, 0)))
    mulo = mu[1536:1792].reshape(1, 2 * LANES)
    wl = jnp.zeros((3, 2 * LANES, RWKV_HEADS * LANES), F32)
    wl = wl.at[0, 0:64].set(_pad_heads(full['rwkv_w2'], RWKV_HEADS))
    wl = wl.at[1, 64:128].set(_pad_heads(full['rwkv_a2'], RWKV_HEADS))
    wl = wl.at[2, 128:256].set(_pad_heads(full['rwkv_g2'], RWKV_HEADS))
    wl = jnp.transpose(wl.reshape(3, 2 * LANES, RWKV_HEADS, LANES), (2, 0, 1, 3))
    cw = full['gdn_conv_w'].reshape(GDN_CONV, 3, GDN_HEADS, LANES)
    cwpack = jnp.pad(jnp.transpose(cw, (2, 1, 0, 3)), ((0, 0), (0, 0), (0, SUBLANES - GDN_CONV), (0, 0)))
    gpar = jnp.zeros((SUBLANES, LANES), F32)
    gpar = gpar.at[0, 0:GDN_HEADS].set(rep['gdn_a_log']).at[1, 0:GDN_HEADS].set(rep['gdn_dt_bias']).at[2].set(rep['gdn_norm_w'])
    return dict(w_cat=w_cat, ffn_cw=full['ffn_conv_w'], ppack=ppack, mulo=mulo, wl=wl, cwpack=cwpack, gpar=gpar,
                g1=rep['norm1_g'].reshape(1, -1), g2=rep['norm2_g'].reshape(1, -1), gf=rep['final_g'].reshape(1, -1))


def _prepare_late_weights(full):
    rp = full['rwkv_proj']
    rproj = jnp.pad(rp.reshape(RWKV_HEADS, RWKV_HEAD_DIM, -1), ((0, 0), (0, LANES - RWKV_HEAD_DIM), (0, 0))).reshape(RWKV_HEADS * LANES, -1)
    return dict(rproj=rproj, gproj=full['gdn_proj'], w_out=full['w_out'], ffn_up=full['ffn_up'], ffn_down=full['ffn_down'])


def _local_step(x, target, p, late_weights, push_grads):
    d = x.shape[1]
    full_w = lambda a: (a, a.shape[1], 0)
    (u,) = _rw_forward(_norm_fn, [full_w(x)], [p['g1']], [(d, BF16)], "norm1")
    p_cat = _matmul(u, p['w_cat'], 'nt', F32, "proj_in")
    ya_pre, st_r = _rwkv_forward(p_cat, p['ppack'], p['mulo'], p['wl'])
    yb_pre, st_g = _gdn_forward(p_cat, p['cwpack'], p['gpar'])
    p = {**p, **late_weights((ya_pre, yb_pre))}
    ya = _matmul(ya_pre, p['rproj'], 'nn', F32, "rwkv_proj")
    yb = _matmul(yb_pre, p['gproj'], 'nn', F32, "gdn_proj")
    gates = [(p_cat, d, OFF_GA // d), (p_cat, d, OFF_GB // d)]
    (mixed,) = _rw_forward(_merge_fn, gates + [full_w(ya), full_w(yb)], [], [(d, BF16)], "merge")
    mo = _matmul(mixed, p['w_out'], 'nn', F32, "out_proj")
    x1, n2 = _rw_forward(_res_norm_fn, [full_w(x), full_w(mo)], [p['g2']], [(d, F32), (d, BF16)], "res_norm2")
    hpre = _matmul(n2, p['ffn_up'], 'nt', F32, "ffn_up")
    act = _ffn_act_forward(hpre, p['ffn_cw'])
    fo = _matmul(act, p['ffn_down'], 'nn', F32, "ffn_down")
    loss_vec, dx2, dx2b, dgf = _loss_head(x1, fo, p['gf'], target, "loss_head")

    dact = _matmul(dx2b, p['ffn_down'], 'nt', F32, "d_act")
    dw_down = _matmul(act, dx2b, 'tn', BF16, "dw_ffn_down")
    dh_gate, dh_up, dcw_gate, dcw_up = _ffn_backward(hpre, p['ffn_cw'], dact)
    dh = jnp.concatenate([dh_gate, dh_up], axis=1)
    dcw_f = jnp.concatenate([dcw_gate, dcw_up], axis=1)
    dn2 = _matmul(dh, p['ffn_up'], 'nn', F32, "d_norm2")
    dw_up = _matmul(dh, n2, 'tn', BF16, "dw_ffn_up")
    token = push_grads({'ffn_down': dw_down, 'ffn_up': dw_up})
    dx1, dx1b, dg2 = _rw_backward(_res_norm_fn, [full_w(x), full_w(mo)], [p['g2'] + token], [full_w(dx2), full_w(dn2)],
                                  [F32, BF16], "res_norm2_bwd")
    dmixed = _matmul(dx1b, p['w_out'], 'nt', F32, "d_mixed")
    dw_out = _matmul(mixed, dx1b, 'tn', BF16, "dw_out")
    dga, dgb, dya, dyb = _rw_backward(_merge_fn, gates + [full_w(ya), full_w(yb)], [], [full_w(dmixed)],
                                      [BF16, BF16, BF16, BF16], "merge_bwd")
    d_ya_pre = _matmul(dya, p['rproj'], 'nt', F32, "d_rwkv_out")
    dw_rproj = _matmul(ya_pre, dya, 'tn', F32, "dw_rwkv_proj")
    d_yb_pre = _matmul(dyb, p['gproj'], 'nt', F32, "d_gdn_out")
    dw_gproj = _matmul(yb_pre, dyb, 'tn', F32, "dw_gdn_proj")
    dpr, dpk, dpv, dplo, dpp, dml, dwl = _rwkv_backward(p_cat, p['ppack'], p['mulo'], p['wl'], st_r, d_ya_pre)
    dq, dk, dv, dz, dab, dcw_g, dgp = _gdn_backward(p_cat, p['cwpack'], p['gpar'], st_g, d_yb_pre)
    t = x.shape[0]
    dp_cat = jnp.concatenate([dga, dgb, dpr, dpk, dpv, dq, dk, dv, dz, dplo.astype(BF16), dab.astype(BF16),
                              jnp.zeros((t, LANES), BF16)], axis=1)
    dw_cat = _matmul(dp_cat, u, 'tn', BF16, "dw_in")
    dw_in = jnp.concatenate([dw_cat[OFF_RKV:OFF_QKV].reshape(3 * RWKV_HEADS, LANES, d)[:, :RWKV_HEAD_DIM].reshape(-1, d),
                             dw_cat[OFF_LO:OFF_AB], dw_cat[OFF_QKV:OFF_Z], dw_cat[OFF_Z:OFF_LO], dw_cat[OFF_AB:OFF_AB + 8],
                             dw_cat[OFF_GA:OFF_GB], dw_cat[OFF_GB:OFF_RKV]], axis=0)
    token = push_grads({'w_out': dw_out, 'w_in': dw_in})
    du = _matmul(dp_cat, p['w_cat'], 'nn', F32, "d_norm1")
    grad_x, dg1 = _rw_backward(_norm_skip_fn, [full_w(x)], [p['g1'] + token], [full_w(du), full_w(dx1)], [F32], "norm1_bwd")

    heads = lambda row: dpp[:, row, :RWKV_HEAD_DIM].reshape(-1)
    lora = lambda j, lo_, hi_: jnp.transpose(dwl[:, j, lo_:hi_, :RWKV_HEAD_DIM], (1, 0, 2)).reshape(hi_ - lo_, RWKV_WIDTH)
    grads = {
        'norm1_g': dg1[0],
        'w_in': dw_in,
        'rwkv_mu': jnp.concatenate([heads(0), heads(1), heads(2), jnp.sum(dml[:, 0, :], axis=0)]),
        'rwkv_w0': heads(3), 'rwkv_a0': heads(4), 'rwkv_k_k': heads(5), 'rwkv_k_a': heads(6),
        'rwkv_ln_w': heads(7), 'rwkv_ln_b': heads(8), 'rwkv_r_k': heads(9).reshape(RWKV_HEADS, RWKV_HEAD_DIM),
        'rwkv_w2': lora(0, 0, 64), 'rwkv_a2': lora(1, 64, 128), 'rwkv_g2': lora(2, 128, 256),
        'rwkv_proj': dw_rproj.reshape(RWKV_HEADS, LANES, -1)[:, :RWKV_HEAD_DIM].reshape(RWKV_WIDTH, -1),
        'gdn_conv_w': jnp.transpose(dcw_g[:, :, :GDN_CONV, :], (2, 1, 0, 3)).reshape(GDN_CONV, 3 * GDN_WIDTH),
        'gdn_a_log': jnp.sum(dgp[:, 0, :GDN_HEADS], axis=0), 'gdn_dt_bias': jnp.sum(dgp[:, 1, :GDN_HEADS], axis=0),
        'gdn_norm_w': jnp.sum(dgp[:, 2, :], axis=0),
        'gdn_proj': dw_gproj, 'w_out': dw_out, 'norm2_g': dg2[0], 'ffn_up': dw_up, 'ffn_conv_w': dcw_f,
        'ffn_down': dw_down, 'final_g': dgf[0],
    }
    return loss_vec, grad_x, grads


def kernel(x, norm1_g, w_in, rwkv_mu, rwkv_w0, rwkv_w2, rwkv_a0, rwkv_a2, rwkv_g2, rwkv_k_k, rwkv_k_a, rwkv_r_k, rwkv_ln_w, rwkv_ln_b, rwkv_proj, gdn_conv_w, gdn_a_log, gdn_dt_bias, gdn_norm_w, gdn_proj, w_out, norm2_g, ffn_up, ffn_conv_w, ffn_down, final_g, loss_target, m_norm1_g, m_w_in, m_rwkv_mu, m_rwkv_w0, m_rwkv_w2, m_rwkv_a0, m_rwkv_a2, m_rwkv_g2, m_rwkv_k_k, m_rwkv_k_a, m_rwkv_r_k, m_rwkv_ln_w, m_rwkv_ln_b, m_rwkv_proj, m_gdn_conv_w, m_gdn_a_log, m_gdn_dt_bias, m_gdn_norm_w, m_gdn_proj, m_w_out, m_norm2_g, m_ffn_up, m_ffn_conv_w, m_ffn_down, m_final_g, v_norm1_g, v_w_in, v_rwkv_mu, v_rwkv_w0, v_rwkv_w2, v_rwkv_a0, v_rwkv_a2, v_rwkv_g2, v_rwkv_k_k, v_rwkv_k_a, v_rwkv_r_k, v_rwkv_ln_w, v_rwkv_ln_b, v_rwkv_proj, v_gdn_conv_w, v_gdn_a_log, v_gdn_dt_bias, v_gdn_norm_w, v_gdn_proj, v_w_out, v_norm2_g, v_ffn_up, v_ffn_conv_w, v_ffn_down, v_final_g):
    given = dict(zip(WEIGHT_NAMES, (norm1_g, w_in, rwkv_mu, rwkv_w0, rwkv_w2, rwkv_a0, rwkv_a2, rwkv_g2, rwkv_k_k, rwkv_k_a, rwkv_r_k,
                                    rwkv_ln_w, rwkv_ln_b, rwkv_proj, gdn_conv_w, gdn_a_log, gdn_dt_bias, gdn_norm_w, gdn_proj, w_out,
                                    norm2_g, ffn_up, ffn_conv_w, ffn_down, final_g)))
    mom1 = dict(zip(WEIGHT_NAMES, (m_norm1_g, m_w_in, m_rwkv_mu, m_rwkv_w0, m_rwkv_w2, m_rwkv_a0, m_rwkv_a2, m_rwkv_g2, m_rwkv_k_k,
                                   m_rwkv_k_a, m_rwkv_r_k, m_rwkv_ln_w, m_rwkv_ln_b, m_rwkv_proj, m_gdn_conv_w, m_gdn_a_log,
                                   m_gdn_dt_bias, m_gdn_norm_w, m_gdn_proj, m_w_out, m_norm2_g, m_ffn_up, m_ffn_conv_w, m_ffn_down,
                                   m_final_g)))
    mom2 = dict(zip(WEIGHT_NAMES, (v_norm1_g, v_w_in, v_rwkv_mu, v_rwkv_w0, v_rwkv_w2, v_rwkv_a0, v_rwkv_a2, v_rwkv_g2, v_rwkv_k_k,
                                   v_rwkv_k_a, v_rwkv_r_k, v_rwkv_ln_w, v_rwkv_ln_b, v_rwkv_proj, v_gdn_conv_w, v_gdn_a_log,
                                   v_gdn_dt_bias, v_gdn_norm_w, v_gdn_proj, v_w_out, v_norm2_g, v_ffn_up, v_ffn_conv_w, v_ffn_down,
                                   v_final_g)))
    def strip(n, a):
        a = a if n == 'final_g' else a.reshape(a.shape[1:])
        return a.T if n in TRANSPOSED else a

    local = {n: strip(n, a) for n, a in given.items()}
    shard_shapes = {n: local[n].shape for n in SHARD_AXIS}
    sharded = BIG_SHARDED + SMALL_SHARDED

    late_names = [n for n in BIG_SHARDED if n != 'w_in']
    g_in = _all_gather_two_level(_row_pack([local['w_in']], BF16, 16), "gather_w_in")
    g_small = _all_gather_two_level(_flat_pack([local[n] for n in SMALL_SHARDED], F32, SUBLANES), "gather_small")
    late_pack, g_in, g_small = lax.optimization_barrier((_row_pack([local[n] for n in late_names], BF16, 16), g_in, g_small))
    send_sems, recv_sems, late_pack, landing, token = _gather_start(late_pack, "gather_late_start")
    full = _unpack_row_gathered(g_in, ['w_in'], shard_shapes)
    full.update(_unpack_gathered(g_small, SMALL_SHARDED, shard_shapes))
    rep = {n: local[n] for n in REPLICATED}
    rep['norm1_g'] = rep['norm1_g'] + token[0, 0]

    def late_weights(after):
        got = _gather_wait(send_sems, recv_sems, late_pack, landing, after, "gather_late_wait")
        me = 4 * lax.axis_index("x") + 2 * lax.axis_index("y") + lax.axis_index("c")
        slot = lax.broadcasted_iota(jnp.int32, (N_DEV, 1, 1), 0)
        got = jnp.where(slot == me, late_pack[None], got)
        return _prepare_late_weights(_unpack_row_gathered(got, late_names, shard_shapes))

    pushes = []
    me = 4 * lax.axis_index("x") + 2 * lax.axis_index("y") + lax.axis_index("c")
    slot = lax.broadcasted_iota(jnp.int32, (N_DEV, 1, 1), 0)

    def push_grads(group):
        names = list(group)
        slabs = [group[n].reshape(N_DEV, -1, group[n].shape[1]) for n in names]
        send_sems, recv_sems, slabs, landings, token = _slab_push_start(slabs, "grad_push_start_" + "_".join(names))
        pushes.append((names, send_sems, recv_sems, slabs, landings))
        return token[0, 0]

    loss_vec, grad_x, grads = _local_step(x[0], loss_target[0], _prepare_weights(full, rep), late_weights, push_grads)

    landed = {}
    for names, send_sems, recv_sems, slabs, landings in pushes:
        got = _slab_push_wait(send_sems, recv_sems, slabs, landings, (grad_x,), "grad_push_wait_" + "_".join(names))
        for n, slab, land in zip(names, slabs, got):
            landed[n] = (land, slab)

    small_sharded = ['rwkv_proj', 'gdn_proj'] + SMALL_SHARDED
    small_names = small_sharded + REPLICATED
    rep_vec = jnp.concatenate([grads[n].reshape(-1) for n in REPLICATED] + [loss_vec[0, 0:1]])
    slab_small = jnp.concatenate([_shard_major(grads[n], SHARD_AXIS[n]) for n in small_sharded] +
                                 [jnp.broadcast_to(rep_vec[None], (N_DEV, rep_vec.shape[0]))], axis=1)
    small_rows = -(-slab_small.shape[1] // (PACK_W * SUBLANES)) * SUBLANES
    slab_small = jnp.pad(slab_small, ((0, 0), (0, small_rows * PACK_W - slab_small.shape[1]))).reshape(N_DEV, small_rows, PACK_W)
    core = lax.axis_index("c").astype(jnp.int32).reshape(1)
    (from_sibling,) = _pair_exchange([slab_small], "grad_pair_exchange")
    chip_small = _pair_sum(slab_small, from_sibling, core, F32, "grad_pair_sum_small")
    (parts_small,) = _chip_exchange([chip_small], "grad_chip_exchange")

    def pack_local(src):
        flat = jnp.concatenate([strip(n, src[n]).reshape(-1) for n in small_names])
        return jnp.pad(flat, (0, small_rows * PACK_W - flat.shape[0])).reshape(small_rows, PACK_W)

    results = [({}, None) for _ in range(4)]
    me_arr = me.astype(jnp.int32).reshape(1)
    for n in ROW_SHARDED:
        packs = _reduce_landed_adamw(*landed[n], me_arr, local[n], strip(n, mom1[n]), strip(n, mom2[n]), "adamw_" + n)
        for (out, _), pk in zip(results, packs):
            out[n] = (pk.T if n in TRANSPOSED else pk).reshape(given[n].shape)
    packs = _reduce_adamw(parts_small, pack_local(given), pack_local(mom1), pack_local(mom2), "adamw_small")
    for i, pk in enumerate(packs):
        flat, off = pk.reshape(-1), 0
        for n in small_names:
            size = int(np.prod(given[n].shape))
            results[i][0][n] = flat[off:off + size].reshape(given[n].shape)
            off += size
        results[i] = (results[i][0], flat[off])
    (g_out, loss), (d_out, _), (m_out, _), (v_out, _) = results
    return (loss, grad_x[None], *[g_out[n] for n in WEIGHT_NAMES], *[d_out[n] for n in WEIGHT_NAMES],
            *[m_out[n] for n in WEIGHT_NAMES], *[v_out[n] for n in WEIGHT_NAMES])
```

```python
import functools

import jax
import jax.numpy as jnp
import numpy as np
from jax import lax
from jax.experimental import pallas as pl
from jax.experimental.pallas import tpu as pltpu

F32 = jnp.float32
BF16 = jnp.bfloat16

N_DEV = 8
D_MODEL = 1024
CHUNK = 64
RWKV_HEADS = 8
RWKV_HEAD_DIM = 64
RWKV_WIDTH = 512
GDN_HEADS = 4
GDN_HEAD_DIM = 128
GDN_WIDTH = 512
GDN_CONV = 4
FFN_HIDDEN = 2816
FFN_CONV = 3
NORM_EPS = 1e-6
L2_EPS = 1e-6
RWKV_GN_EPS = 64e-5
LANES = 128
SUBLANES = 8
VMEM_LIMIT = 56 * 1024 * 1024

ADAM_LR = 0.001
ADAM_B1 = 0.9
ADAM_B2 = 0.999
ADAM_EPS = 1e-08
ADAM_WD = 0.01
ADAM_STEP = 10

OFF_GA, OFF_GB, OFF_RKV, OFF_QKV, OFF_Z, OFF_LO, OFF_AB, CAT_W = 0, 1024, 2048, 5120, 6656, 7168, 7424, 7680
RWKV_HB = 8
RWKV_STEP_CHUNKS = 2
RWKV_TILE = RWKV_STEP_CHUNKS * CHUNK
GDN_HB = 4
GDN_STEP_CHUNKS = 4
GDN_TILE = GDN_STEP_CHUNKS * CHUNK

WEIGHT_NAMES = ['norm1_g', 'w_in', 'rwkv_mu', 'rwkv_w0', 'rwkv_w2', 'rwkv_a0', 'rwkv_a2', 'rwkv_g2', 'rwkv_k_k', 'rwkv_k_a',
                'rwkv_r_k', 'rwkv_ln_w', 'rwkv_ln_b', 'rwkv_proj', 'gdn_conv_w', 'gdn_a_log', 'gdn_dt_bias', 'gdn_norm_w',
                'gdn_proj', 'w_out', 'norm2_g', 'ffn_up', 'ffn_conv_w', 'ffn_down', 'final_g']
BIG_SHARDED = ['w_in', 'ffn_up', 'ffn_down', 'w_out', 'rwkv_proj', 'gdn_proj']
SMALL_SHARDED = ['rwkv_w2', 'rwkv_a2', 'rwkv_g2', 'gdn_conv_w', 'ffn_conv_w']
TRANSPOSED = ('w_in', 'ffn_up')
SHARD_AXIS = {'w_in': 0, 'ffn_up': 0, 'ffn_down': 0, 'w_out': 0, 'rwkv_proj': 1, 'gdn_proj': 1,
              'rwkv_w2': 1, 'rwkv_a2': 1, 'rwkv_g2': 1, 'gdn_conv_w': 1, 'ffn_conv_w': 1}
REPLICATED = [n for n in WEIGHT_NAMES if n not in SHARD_AXIS]
ROW_SHARDED = ['w_in', 'ffn_up', 'ffn_down', 'w_out']


def _cparams(sem=None):
    kw = dict(vmem_limit_bytes=VMEM_LIMIT)
    if sem is not None:
        kw['dimension_semantics'] = sem
    return pltpu.CompilerParams(**kw)


_NN, _NT, _TN = 'nn', 'nt', 'tn'
_DIMS_2D = {'nn': (((1,), (0,)), ((), ())), 'nt': (((1,), (1,)), ((), ())), 'tn': (((0,), (0,)), ((), ()))}
_DIMS_3D = {'nn': (((2,), (1,)), ((0,), (0,))), 'nt': (((2,), (2,)), ((0,), (0,))), 'tn': (((1,), (1,)), ((0,), (0,)))}


def _dg(a, b, kind):
    return lax.dot_general(a, b, (_DIMS_2D if a.ndim == 2 else _DIMS_3D)[kind], preferred_element_type=F32)


def _dot1(a, b, kind):
    return _dg(a.astype(BF16), b.astype(BF16), kind)


@jax.custom_vjp
def _dhi(a, b):
    return _dot1(a, b, _NN)


_dhi.defvjp(lambda a, b: (_dot1(a, b, _NN), (a, b)),
            lambda res, ct: (_dot1(ct, res[1], _NT), _dot1(res[0], ct, _TN)))


@jax.custom_vjp
def _dnt(a, b):
    return _dot1(a, b, _NT)


_dnt.defvjp(lambda a, b: (_dot1(a, b, _NT), (a, b)),
            lambda res, ct: (_dot1(ct, res[1], _NN), _dot1(ct, res[0], _TN)))


@jax.custom_vjp
def _dtn(a, b):
    return _dot1(a, b, _TN)


_dtn.defvjp(lambda a, b: (_dot1(a, b, _TN), (a, b)),
            lambda res, ct: (_dot1(res[1], ct, _NT), _dot1(res[0], ct, _NN)))


def _split3(x):
    x1 = x.astype(BF16)
    r1 = x - x1.astype(F32)
    x2 = r1.astype(BF16)
    return x1, x2, (r1 - x2.astype(F32)).astype(BF16)


def _dot_exact_lhs(sel, x, kind):
    parts = [_dg(sel, xi, kind) for xi in _split3(x)]
    return parts[0] + parts[1] + parts[2]


def _tril_ones(like):
    c = like.shape[-2]
    ri, ci = _iotas(c)
    return jnp.broadcast_to((ri >= ci).astype(BF16), like.shape[:-2] + (c, c))


@jax.custom_vjp
def _cumsum_rows(x):
    return _dot_exact_lhs(_tril_ones(x), x, _NN)


_cumsum_rows.defvjp(lambda x: (_dot_exact_lhs(_tril_ones(x), x, _NN), None),
                    lambda _, ct: (_dot_exact_lhs(_tril_ones(ct), ct, _TN),))


@jax.custom_vjp
def _lane_sum_as_row(x):
    return _dot_exact_lhs(jnp.ones(x.shape, BF16), x, _NT)


def _lane_sum_as_row_bwd(_, ct):
    ones = jnp.ones(ct.shape[:-1] + (LANES,), BF16)
    parts = [_dg(ci, ones, _TN) for ci in _split3(ct)]
    return (parts[0] + parts[1] + parts[2],)


_lane_sum_as_row.defvjp(lambda x: (_dot_exact_lhs(jnp.ones(x.shape, BF16), x, _NT), None), _lane_sum_as_row_bwd)


def _shift_rows(x, halo, s):
    rows = lax.broadcasted_iota(jnp.int32, x.shape, 0)
    out = pltpu.roll(x, s, 0)
    for i in range(s):
        out = jnp.where(rows == i, halo[SUBLANES - s + i:SUBLANES - s + i + 1, :], out)
    return out


def _unshift_rows(g, carry, s):
    c = g.shape[0]
    rows = lax.broadcasted_iota(jnp.int32, g.shape, 0)
    out = pltpu.roll(g, c - s, 0)
    for i in range(s):
        out = jnp.where(rows == c - s + i, carry[i:i + 1, :], out)
    return out


def _sigmoid(z):
    return 1.0 / (1.0 + jnp.exp(-z))


def _silu(z):
    return z * _sigmoid(z)


def _softplus(z):
    return jnp.maximum(z, 0.0) + jnp.log(1.0 + jnp.exp(-jnp.abs(z)))


def _rms(t, gain):
    return t * lax.rsqrt(jnp.mean(t * t, axis=-1, keepdims=True) + NORM_EPS) * gain


def _iotas(c):
    return lax.broadcasted_iota(jnp.int32, (c, c), 0), lax.broadcasted_iota(jnp.int32, (c, c), 1)


def _unit_lower_inverse(xm, eye):
    t = eye + xm
    p = xm
    for _ in range(5):
        p = _dhi(p, p)
        t = t + _dhi(t, p)
    return t


def _rwkv_head(pr, pk, pv, plo, qr, qk, qv, qlo, s0, pp, mulo, wl):
    c = pr.shape[1]
    n_heads = s0.shape[0]
    n_chunks = pr.shape[0] // n_heads
    ri, ci = _iotas(c)
    if n_chunks > 1:
        pp = jnp.concatenate([pp] * n_chunks, axis=0)
        wl = jnp.concatenate([wl] * n_chunks, axis=0)

    def mix(p, q, mu):
        return p + (q - p) * mu

    r = mix(pr, qr, pp[:, 0:1])
    k = mix(pk, qk, pp[:, 1:2])
    v = mix(pv, qv, pp[:, 2:3])
    lo = mix(plo, qlo, mulo)
    w0, a0, k_k, k_a, ln_w, ln_b, r_k = (pp[:, i:i + 1] for i in range(3, 10))

    def per_head(t):
        return jnp.concatenate([jnp.broadcast_to(t[i], (n_heads,) + t.shape[1:]) for i in range(n_chunks)], axis=0)

    zw = _dhi(per_head(jnp.tanh(lo)), wl[:, 0])
    za = _dhi(per_head(lo), wl[:, 1])
    g = _dhi(per_head(_sigmoid(lo)), wl[:, 2])
    w_log = -_softplus(-(w0 + zw)) - 0.5
    lw = -jnp.exp(w_log)
    a = _sigmoid(a0 + za)
    kk = k * k_k
    kk = kk * lax.rsqrt(jnp.sum(kk * kk, axis=-1, keepdims=True) + L2_EPS)
    k2 = k * (1.0 + (a - 1.0) * k_a)
    an = -kk
    b = kk * a
    causal = ri >= ci
    strict = ri > ci
    eye = (ri == ci).astype(F32)
    cl = _cumsum_rows(lw)
    ecl = jnp.exp(-cl)
    at = an * jnp.exp(cl - lw)
    bt = b * ecl
    kt = k2 * ecl
    rt = r * jnp.exp(cl)
    a_ab = jnp.where(strict, _dnt(at, bt), 0.0)
    a_ak = jnp.where(strict, _dnt(at, kt), 0.0)
    tinv = _unit_lower_inverse(a_ab, eye)
    akv = _dhi(a_ak, v)
    r_b = jnp.where(causal, _dnt(rt, bt), 0.0)
    rkv = _dhi(jnp.where(causal, _dnt(rt, kt), 0.0), v)
    cl_end = jnp.sum(lw, axis=1, keepdims=True)
    dec_end = jnp.exp(cl_end - cl)
    b_end = b * dec_end
    sv = _dtn(v, k2 * dec_end)
    e_end = jnp.exp(cl_end)
    state, ys = s0, []
    for i in range(n_chunks):
        sl = slice(i * n_heads, (i + 1) * n_heads)
        u = _dhi(tinv[sl], _dnt(at[sl], state) + akv[sl])
        ys.append(_dnt(rt[sl], state) + _dhi(r_b[sl], u) + rkv[sl])
        state = state * e_end[sl] + _dtn(u, b_end[sl]) + sv[sl]
    y = jnp.concatenate(ys, axis=0) if n_chunks > 1 else ys[0]
    s1 = state
    m = (lax.broadcasted_iota(jnp.int32, (1, LANES), 1) < RWKV_HEAD_DIM).astype(F32)
    mean = jnp.sum(y, axis=-1, keepdims=True) * (1.0 / RWKV_HEAD_DIM)
    yc = (y - mean) * m
    var = jnp.sum(yc * yc, axis=-1, keepdims=True) * (1.0 / RWKV_HEAD_DIM)
    yn = yc * lax.rsqrt(var + RWKV_GN_EPS) * ln_w + ln_b
    y2 = yn + jnp.sum(r * k2 * r_k, axis=-1, keepdims=True) * v
    return y2 * g, s1


def _gdn_head(xq, xk, xv, z, ab, s0, cw, gp, oha, ohb):
    c = z.shape[1]
    n_heads = s0.shape[0]
    n_chunks = z.shape[0] // n_heads
    ri, ci = _iotas(c)
    cw = jnp.concatenate([cw] * n_chunks, axis=0) if n_chunks > 1 else cw

    def conv(xs, w):
        out = xs[0] * w[:, GDN_CONV - 1:GDN_CONV]
        for s in range(1, GDN_CONV):
            out = out + xs[s] * w[:, GDN_CONV - 1 - s:GDN_CONV - s]
        return out

    q = _silu(conv(xq, cw[:, 0]))
    k = _silu(conv(xk, cw[:, 1]))
    v = _silu(conv(xv, cw[:, 2]))
    q = q * lax.rsqrt(jnp.sum(q * q, axis=-1, keepdims=True) + L2_EPS) * (GDN_HEAD_DIM ** -0.5)
    k = k * lax.rsqrt(jnp.sum(k * k, axis=-1, keepdims=True) + L2_EPS)
    gg = -jnp.exp(gp[0:1]) * _softplus(ab + gp[1:2])
    beta = jnp.sum(_sigmoid(ab) * ohb, axis=-1, keepdims=True)
    causal = ri >= ci
    strict = ri > ci
    eye = (ri == ci).astype(F32)
    gcm = _cumsum_rows(gg * oha)
    gc = jnp.sum(gcm, axis=-1, keepdims=True)
    gc_row = _lane_sum_as_row(gcm)
    dec = jnp.where(causal, jnp.exp(jnp.where(causal, gc - gc_row, 0.0)), 0.0)
    kb = k * beta
    vb = v * beta
    lm = jnp.where(strict, _dnt(kb, k) * dec, 0.0)
    tinv = _unit_lower_inverse(-lm, eye)
    egc = jnp.exp(gc)
    u = _dhi(tinv, vb)
    wk = _dhi(tinv, kb * egc)
    attn = jnp.where(causal, _dnt(q, k) * dec, 0.0)
    g_last = gc[:, c - 1:c, :]
    q_dec = q * egc
    k_dec = k * jnp.exp(g_last - gc)
    e_last = jnp.exp(g_last)
    state, outs = s0, []
    for i in range(n_chunks):
        sl = slice(i * n_heads, (i + 1) * n_heads)
        v_new = u[sl] - _dhi(wk[sl], state)
        outs.append(_dhi(q_dec[sl], state) + _dhi(attn[sl], v_new))
        state = state * e_last[sl] + _dtn(k_dec[sl], v_new)
    o = jnp.concatenate(outs, axis=0) if n_chunks > 1 else outs[0]
    return _rms(o, gp[2:3]) * _silu(z), state


def _head_id(grp, i, per_step, heads):
    return i if per_step == heads else grp * per_step + i


def _head_range(grp, per_step, heads):
    return slice(None) if per_step == heads else pl.ds(grp * per_step, per_step)


def _rwkv_specs(nmap):
    hb, groups = RWKV_HB, RWKV_HEADS // RWKV_HB
    cb = OFF_RKV // (hb * LANES)
    specs = []
    for j in range(3):
        specs.append(pl.BlockSpec((RWKV_TILE, hb * LANES), lambda n, g, j=j: (nmap(n), cb + j * groups + g)))
    specs.append(pl.BlockSpec((RWKV_TILE, 2 * LANES), lambda n, g: (nmap(n), OFF_LO // (2 * LANES))))
    per = RWKV_TILE // SUBLANES
    for j in range(3):
        specs.append(pl.BlockSpec((SUBLANES, hb * LANES),
                                  lambda n, g, j=j: (jnp.maximum(nmap(n) * per - 1, 0), cb + j * groups + g)))
    specs.append(pl.BlockSpec((SUBLANES, 2 * LANES), lambda n, g: (jnp.maximum(nmap(n) * per - 1, 0), OFF_LO // (2 * LANES))))
    specs.append(pl.BlockSpec((hb, 16, LANES), lambda n, g: (g, 0, 0)))
    specs.append(pl.BlockSpec((1, 2 * LANES), lambda n, g: (0, 0)))
    specs.append(pl.BlockSpec((hb, 3, 2 * LANES, LANES), lambda n, g: (g, 0, 0, 0)))
    return specs


def _rwkv_operands(refs, halos, live):
    pr, pk, pv, plo = refs
    hr, hk, hv, hlo = halos
    cur, prev = [], []
    for x, hx in ((pr, hr), (pk, hk), (pv, hv)):
        tiles = [_lane_block(x, h) for h in range(RWKV_HB)]
        cur.append(_chunk_batch(tiles))
        prev.append(_chunk_batch([_shift_rows(t_, _lane_block(hx, h) * live, 1) for h, t_ in enumerate(tiles)]))
    lo = plo[...]
    cur.append(_chunk_batch([lo]))
    prev.append(_chunk_batch([_shift_rows(lo, hlo[...] * live, 1)]))
    return cur, prev


def _rwkv_forward(p_cat, ppack, mulo, wl):
    t = p_cat.shape[0]
    n_chunks = t // RWKV_TILE

    def body(pr, pk, pv, plo, hr, hk, hv, hlo, pp, ml, w, out, st_out, s_scr):
        n, grp = pl.program_id(0), pl.program_id(1)

        hsl = _head_range(grp, RWKV_HB, RWKV_HEADS)

        @pl.when(n == 0)
        def _():
            s_scr[hsl] = jnp.zeros((RWKV_HB, LANES, LANES), F32)

        live = (n > 0).astype(F32)
        cur, prev = _rwkv_operands((pr, pk, pv, plo), (hr, hk, hv, hlo), live)
        s0 = s_scr[hsl]
        st_out[...] = s0
        o, s1 = _rwkv_head(*cur, *prev, s0, pp[...], ml[...], w[...])
        for h, tile in enumerate(_head_tiles(o, RWKV_HB)):
            out[:, h * LANES:(h + 1) * LANES] = tile.astype(out.dtype)
        s_scr[hsl] = s1

    return pl.pallas_call(
        body, name="rwkv_fwd", grid=(n_chunks, RWKV_HEADS // RWKV_HB),
        in_specs=_rwkv_specs(lambda n: n),
        out_specs=(pl.BlockSpec((RWKV_TILE, RWKV_HB * LANES), lambda n, g: (n, g)),
                   pl.BlockSpec((None, RWKV_HB, LANES, LANES), lambda n, g: (n, g, 0, 0))),
        out_shape=(jax.ShapeDtypeStruct((t, RWKV_HEADS * LANES), BF16),
                   jax.ShapeDtypeStruct((n_chunks, RWKV_HEADS, LANES, LANES), F32)),
        scratch_shapes=[pltpu.VMEM((RWKV_HEADS, LANES, LANES), F32)],
        compiler_params=_cparams(("arbitrary", "arbitrary")),
    )(p_cat, p_cat, p_cat, p_cat, p_cat, p_cat, p_cat, p_cat, ppack, mulo, wl)


def _rwkv_backward(p_cat, ppack, mulo, wl, states, d_out):
    t = p_cat.shape[0]
    n_chunks = t // RWKV_TILE
    last = n_chunks - 1

    def body(pr, pk, pv, plo, hr, hk, hv, hlo, pp, ml, w, st, dy, dpr, dpk, dpv, dplo, dpp, dml, dw, ds_scr, car_scr, carlo_scr):
        n, grp = pl.program_id(0), pl.program_id(1)

        hsl = _head_range(grp, RWKV_HB, RWKV_HEADS)
        gi = _head_id(grp, 0, 1, RWKV_HEADS // RWKV_HB)

        @pl.when(n == 0)
        def _():
            ds_scr[hsl] = jnp.zeros((RWKV_HB, LANES, LANES), F32)
            car_scr[hsl] = jnp.zeros((RWKV_HB, 3 * SUBLANES, LANES), F32)
            carlo_scr[gi] = jnp.zeros((SUBLANES, 2 * LANES), F32)

        @pl.when((n == 0) & (grp == 0))
        def _():
            dpp[...] = jnp.zeros(dpp.shape, F32)
            dml[...] = jnp.zeros(dml.shape, F32)
            dw[...] = jnp.zeros(dw.shape, F32)

        live = (n < last).astype(F32)
        cur, prev = _rwkv_operands((pr, pk, pv, plo), (hr, hk, hv, hlo), live)
        _, vjp = jax.vjp(_rwkv_head, *cur, *prev, st[...], pp[...], ml[...], w[...])
        g = vjp((_chunk_batch([_lane_block(dy, h) for h in range(RWKV_HB)]), ds_scr[hsl]))
        outs = (dpr, dpk, dpv)
        d_cur = [_head_tiles(g[j], RWKV_HB) for j in range(3)]
        d_prev = [_head_tiles(g[4 + j], RWKV_HB) for j in range(3)]
        for i in range(RWKV_HB):
            sl = slice(i * LANES, (i + 1) * LANES)
            h = _head_id(grp, i, RWKV_HB, RWKV_HEADS)
            car = car_scr[h]
            for j in range(3):
                tot = d_cur[j][i] + _unshift_rows(d_prev[j][i], car[SUBLANES * j:SUBLANES * (j + 1), :], 1)
                outs[j][:, sl] = tot.astype(outs[j].dtype)
                car_scr[h, SUBLANES * j:SUBLANES * (j + 1), :] = d_prev[j][i][0:SUBLANES, :]
        (dlo_cur,), (dlo_prev,) = _head_tiles(g[3], 1), _head_tiles(g[7], 1)
        dlo = dlo_cur + _unshift_rows(dlo_prev, carlo_scr[gi], 1)
        carlo_scr[gi] = dlo_prev[0:SUBLANES, :]
        ds_scr[hsl] = g[8]
        dpp[hsl] += g[9]
        dml[0, 0:1, :] += g[10]
        dw[hsl] += g[11]

        @pl.when(grp == 0)
        def _():
            dplo[...] = dlo

        @pl.when(grp > 0)
        def _():
            dplo[...] += dlo

    rev = lambda n: last - n
    in_specs = _rwkv_specs(rev) + [
        pl.BlockSpec((None, RWKV_HB, LANES, LANES), lambda n, g: (rev(n), g, 0, 0)),
        pl.BlockSpec((RWKV_TILE, RWKV_HB * LANES), lambda n, g: (rev(n), g)),
    ]
    hw = RWKV_HEADS * LANES
    return pl.pallas_call(
        body, name="rwkv_bwd", grid=(n_chunks, RWKV_HEADS // RWKV_HB),
        in_specs=in_specs,
        out_specs=(pl.BlockSpec((RWKV_TILE, RWKV_HB * LANES), lambda n, g: (rev(n), g)),
                   pl.BlockSpec((RWKV_TILE, RWKV_HB * LANES), lambda n, g: (rev(n), g)),
                   pl.BlockSpec((RWKV_TILE, RWKV_HB * LANES), lambda n, g: (rev(n), g)),
                   pl.BlockSpec((RWKV_TILE, 2 * LANES), lambda n, h: (rev(n), 0)),
                   pl.BlockSpec((RWKV_HEADS, 16, LANES), lambda n, h: (0, 0, 0)),
                   pl.BlockSpec((RWKV_HEADS, SUBLANES, 2 * LANES), lambda n, h: (0, 0, 0)),
                   pl.BlockSpec((RWKV_HEADS, 3, 2 * LANES, LANES), lambda n, h: (0, 0, 0, 0))),
        out_shape=(jax.ShapeDtypeStruct((t, hw), BF16), jax.ShapeDtypeStruct((t, hw), BF16), jax.ShapeDtypeStruct((t, hw), BF16),
                   jax.ShapeDtypeStruct((t, 2 * LANES), F32),
                   jax.ShapeDtypeStruct((RWKV_HEADS, 16, LANES), F32),
                   jax.ShapeDtypeStruct((RWKV_HEADS, SUBLANES, 2 * LANES), F32),
                   jax.ShapeDtypeStruct((RWKV_HEADS, 3, 2 * LANES, LANES), F32)),
        scratch_shapes=[pltpu.VMEM((RWKV_HEADS, LANES, LANES), F32),
                        pltpu.VMEM((RWKV_HEADS, 3 * SUBLANES, LANES), F32),
                        pltpu.VMEM((RWKV_HEADS, SUBLANES, 2 * LANES), F32)],
        compiler_params=_cparams(("arbitrary", "arbitrary")),
    )(p_cat, p_cat, p_cat, p_cat, p_cat, p_cat, p_cat, p_cat, ppack, mulo, wl, states, d_out)


def _gdn_specs(nmap):
    per = GDN_TILE // SUBLANES
    hb, groups = GDN_HB, GDN_HEADS // GDN_HB
    cb = OFF_QKV // (hb * LANES)
    specs = []
    for j in range(3):
        specs.append(pl.BlockSpec((GDN_TILE, hb * LANES), lambda n, g, j=j: (nmap(n), cb + j * groups + g)))
    for j in range(3):
        specs.append(pl.BlockSpec((SUBLANES, hb * LANES),
                                  lambda n, g, j=j: (jnp.maximum(nmap(n) * per - 1, 0), cb + j * groups + g)))
    specs.append(pl.BlockSpec((GDN_TILE, hb * LANES), lambda n, g: (nmap(n), OFF_Z // (hb * LANES) + g)))
    specs.append(pl.BlockSpec((GDN_TILE, LANES), lambda n, g: (nmap(n), OFF_AB // LANES)))
    specs.append(pl.BlockSpec((hb, 3, SUBLANES, LANES), lambda n, g: (g, 0, 0, 0)))
    specs.append(pl.BlockSpec((SUBLANES, LANES), lambda n, g: (0, 0)))
    return specs


def _conv_taps(x, halo):
    return (x,) + tuple(_shift_rows(x, halo, s) for s in range(1, GDN_CONV))


def _onehots(grp):
    nb = GDN_STEP_CHUNKS * GDN_HB
    lane = lax.broadcasted_iota(jnp.int32, (nb, 1, LANES), 2)
    head = lax.broadcasted_iota(jnp.int32, (nb, 1, LANES), 0) % GDN_HB + _head_id(grp, 0, GDN_HB, GDN_HEADS)
    return (lane == head).astype(F32), (lane == GDN_HEADS + head).astype(F32)


def _chunk_batch(tiles):
    n_chunks = tiles[0].shape[0] // CHUNK
    return jnp.stack([t_[i * CHUNK:(i + 1) * CHUNK, :] for i in range(n_chunks) for t_ in tiles])


def _head_tiles(batch, n_heads=GDN_HB):
    n_chunks = batch.shape[0] // n_heads
    return [jnp.concatenate([batch[i * n_heads + h] for i in range(n_chunks)], axis=0) for h in range(n_heads)]


def _lane_block(ref, h):
    return ref[:, h * LANES:(h + 1) * LANES]


def _gdn_taps(refs, halos, live):
    out = []
    for x, hx in zip(refs, halos):
        per_head = [_conv_taps(_lane_block(x, h), _lane_block(hx, h) * live) for h in range(GDN_HB)]
        out.append(tuple(_chunk_batch([per_head[h][s] for h in range(GDN_HB)]) for s in range(GDN_CONV)))
    return out


def _gdn_forward(p_cat, cwpack, gpar):
    t = p_cat.shape[0]
    n_chunks = t // GDN_TILE

    def body(xq, xk, xv, hq, hk, hv, z, ab, cw, gp, out, st_out, s_scr):
        n, grp = pl.program_id(0), pl.program_id(1)

        hsl = _head_range(grp, GDN_HB, GDN_HEADS)

        @pl.when(n == 0)
        def _():
            s_scr[hsl] = jnp.zeros((GDN_HB, LANES, LANES), F32)

        live = (n > 0).astype(F32)
        oha, ohb = _onehots(grp)
        s0 = s_scr[hsl]
        st_out[...] = s0
        taps = _gdn_taps((xq, xk, xv), (hq, hk, hv), live)
        zb = _chunk_batch([_lane_block(z, h) for h in range(GDN_HB)])
        abb = _chunk_batch([ab[...]] * GDN_HB)
        o, s1 = _gdn_head(*taps, zb, abb, s0, cw[...], gp[...], oha, ohb)
        for h, tile in enumerate(_head_tiles(o)):
            out[:, h * LANES:(h + 1) * LANES] = tile.astype(out.dtype)
        s_scr[hsl] = s1

    return pl.pallas_call(
        body, name="gdn_fwd", grid=(n_chunks, GDN_HEADS // GDN_HB),
        in_specs=_gdn_specs(lambda n: n),
        out_specs=(pl.BlockSpec((GDN_TILE, GDN_HB * LANES), lambda n, g: (n, g)),
                   pl.BlockSpec((None, GDN_HB, LANES, LANES), lambda n, g: (n, g, 0, 0))),
        out_shape=(jax.ShapeDtypeStruct((t, GDN_WIDTH), BF16),
                   jax.ShapeDtypeStruct((n_chunks, GDN_HEADS, LANES, LANES), F32)),
        scratch_shapes=[pltpu.VMEM((GDN_HEADS, LANES, LANES), F32)],
        compiler_params=_cparams(("arbitrary", "arbitrary")),
    )(p_cat, p_cat, p_cat, p_cat, p_cat, p_cat, p_cat, p_cat, cwpack, gpar)


def _gdn_backward(p_cat, cwpack, gpar, states, d_out):
    t = p_cat.shape[0]
    n_chunks = t // GDN_TILE
    last = n_chunks - 1

    def body(xq, xk, xv, hq, hk, hv, z, ab, cw, gp, st, dy, dq, dk, dv, dz, dab, dcw, dgp, ds_scr, car_scr):
        n, grp = pl.program_id(0), pl.program_id(1)

        hsl = _head_range(grp, GDN_HB, GDN_HEADS)

        @pl.when(n == 0)
        def _():
            ds_scr[hsl] = jnp.zeros((GDN_HB, LANES, LANES), F32)
            car_scr[hsl] = jnp.zeros((GDN_HB, 3 * GDN_CONV, SUBLANES, LANES), F32)

        @pl.when((n == 0) & (grp == 0))
        def _():
            dcw[...] = jnp.zeros(dcw.shape, F32)
            dgp[...] = jnp.zeros(dgp.shape, F32)

        live = (n < last).astype(F32)
        oha, ohb = _onehots(grp)
        fn = functools.partial(_gdn_head, oha=oha, ohb=ohb)
        taps = _gdn_taps((xq, xk, xv), (hq, hk, hv), live)
        zb = _chunk_batch([_lane_block(z, h) for h in range(GDN_HB)])
        abb = _chunk_batch([ab[...]] * GDN_HB)
        _, vjp = jax.vjp(fn, *taps, zb, abb, st[...], cw[...], gp[...])
        g = vjp((_chunk_batch([_lane_block(dy, h) for h in range(GDN_HB)]), ds_scr[hsl]))
        outs = (dq, dk, dv)
        tap_tiles = [[_head_tiles(g[j][s]) for s in range(GDN_CONV)] for j in range(3)]
        dz_tiles = _head_tiles(g[3])
        for i in range(GDN_HB):
            sl = slice(i * LANES, (i + 1) * LANES)
            h = _head_id(grp, i, GDN_HB, GDN_HEADS)
            for j in range(3):
                tot = tap_tiles[j][0][i]
                for s in range(1, GDN_CONV):
                    slot = j * GDN_CONV + s
                    tot = tot + _unshift_rows(tap_tiles[j][s][i], car_scr[h, slot], s)
                    car_scr[h, slot] = tap_tiles[j][s][i][0:SUBLANES, :]
                outs[j][:, sl] = tot.astype(outs[j].dtype)
            dz[:, sl] = dz_tiles[i].astype(dz.dtype)
        dab_tiles = _head_tiles(g[4])
        dab_sum = dab_tiles[0]
        for h in range(1, GDN_HB):
            dab_sum = dab_sum + dab_tiles[h]
        ds_scr[hsl] = g[5]
        dcw[hsl] += g[6]
        dgp[0] += g[7]

        @pl.when(grp == 0)
        def _():
            dab[...] = dab_sum

        @pl.when(grp > 0)
        def _():
            dab[...] += dab_sum

    rev = lambda n: last - n
    in_specs = _gdn_specs(rev) + [
        pl.BlockSpec((None, GDN_HB, LANES, LANES), lambda n, g: (rev(n), g, 0, 0)),
        pl.BlockSpec((GDN_TILE, GDN_HB * LANES), lambda n, g: (rev(n), g)),
    ]
    blk = pl.BlockSpec((GDN_TILE, GDN_HB * LANES), lambda n, g: (rev(n), g))
    return pl.pallas_call(
        body, name="gdn_bwd", grid=(n_chunks, GDN_HEADS // GDN_HB),
        in_specs=in_specs,
        out_specs=(blk, blk, blk, blk,
                   pl.BlockSpec((GDN_TILE, LANES), lambda n, h: (rev(n), 0)),
                   pl.BlockSpec((GDN_HEADS, 3, SUBLANES, LANES), lambda n, h: (0, 0, 0, 0)),
                   pl.BlockSpec((GDN_HEADS, SUBLANES, LANES), lambda n, h: (0, 0, 0))),
        out_shape=(jax.ShapeDtypeStruct((t, GDN_WIDTH), BF16), jax.ShapeDtypeStruct((t, GDN_WIDTH), BF16),
                   jax.ShapeDtypeStruct((t, GDN_WIDTH), BF16), jax.ShapeDtypeStruct((t, GDN_WIDTH), BF16),
                   jax.ShapeDtypeStruct((t, LANES), F32),
                   jax.ShapeDtypeStruct((GDN_HEADS, 3, SUBLANES, LANES), F32),
                   jax.ShapeDtypeStruct((GDN_HEADS, SUBLANES, LANES), F32)),
        scratch_shapes=[pltpu.VMEM((GDN_HEADS, LANES, LANES), F32),
                        pltpu.VMEM((GDN_HEADS, 3 * GDN_CONV, SUBLANES, LANES), F32)],
        compiler_params=_cparams(("arbitrary", "arbitrary")),
    )(p_cat, p_cat, p_cat, p_cat, p_cat, p_cat, p_cat, p_cat, cwpack, gpar, states, d_out)


MM_VMEM_BUDGET = 30 * 1024 * 1024
MM_MIN_STEPS = 8


def _mm_tiles(mode, m, n, k, out_bytes):
    tms = [t for t in (2048, 1024, 768, 512, 256, 128, 64) if m % t == 0 and (mode != 'tn' or t % LANES == 0)]
    tns = [t for t in (1408, 1024, 768, 512, 256, 128) if n % t == 0]
    tks = [t for t in (2048, 1920, 1408, 1024, 512, 256, 128, 64) if k % t == 0]
    best, best_key = None, None
    for tm in tms:
        for tn in tns:
            for tk in tks:
                nk = k // tk
                vmem = 2 * (tm * tk * 2 + tk * tn * 2 + tm * tn * out_bytes) + (tm * tn * 4 if nk > 1 else 0)
                steps = (m // tm) * (n // tn) * nk
                if vmem > MM_VMEM_BUDGET:
                    continue
                key = (steps >= MM_MIN_STEPS, tn if mode == 'tn' else 0, tm * tn * tk, -nk)
                if best_key is None or key > best_key:
                    best, best_key = (tm, tn, tk), key
    if best is None:
        raise ValueError(f"no matmul tile for {mode} {m}x{n}x{k}")
    return best


_MM_DIMS = {'nn': (((1,), (0,)), ((), ())), 'nt': (((1,), (1,)), ((), ())), 'tn': (((0,), (0,)), ((), ()))}


def _matmul(a, b, mode, out_dtype, name):
    if mode == 'nn':
        (m, k), (k2, n) = a.shape, b.shape
    elif mode == 'nt':
        (m, k), (n, k2) = a.shape, b.shape
    else:
        (k, m), (k2, n) = a.shape, b.shape
    assert k == k2, (a.shape, b.shape, mode)
    tm, tn, tk = _mm_tiles(mode, m, n, k, jnp.dtype(out_dtype).itemsize)
    nk = k // tk
    dims = _MM_DIMS[mode]

    def body(a_ref, b_ref, o_ref, acc_ref):
        kk = pl.program_id(2)
        part = lax.dot_general(a_ref[...], b_ref[...], dims, preferred_element_type=F32)
        if nk == 1:
            o_ref[...] = part.astype(o_ref.dtype)
            return

        @pl.when(kk == 0)
        def _():
            acc_ref[...] = part

        @pl.when((kk > 0) & (kk < nk - 1))
        def _():
            acc_ref[...] += part

        @pl.when(kk == nk - 1)
        def _():
            o_ref[...] = (acc_ref[...] + part).astype(o_ref.dtype)

    a_spec = pl.BlockSpec((tk, tm), lambda i, j, kk: (kk, i)) if mode == 'tn' else pl.BlockSpec((tm, tk), lambda i, j, kk: (i, kk))
    b_spec = pl.BlockSpec((tn, tk), lambda i, j, kk: (j, kk)) if mode == 'nt' else pl.BlockSpec((tk, tn), lambda i, j, kk: (kk, j))
    return pl.pallas_call(
        body, name=name, grid=(m // tm, n // tn, nk),
        in_specs=[a_spec, b_spec],
        out_specs=pl.BlockSpec((tm, tn), lambda i, j, kk: (i, j)),
        out_shape=jax.ShapeDtypeStruct((m, n), out_dtype),
        scratch_shapes=[pltpu.VMEM((tm, tn), F32)],
        compiler_params=_cparams(("parallel", "parallel", "arbitrary")),
    )(a, b)


ROW_TILE = 256


def _row_specs(rows, tm):
    return [pl.BlockSpec((tm, w), lambda i, ci=ci: (i, ci)) for (_, w, ci) in rows]


def _rw_forward(fn, rows, pars, outs, name):
    t = rows[0][0].shape[0]
    tm = min(ROW_TILE, t)
    nr, npar = len(rows), len(pars)

    def body(*refs):
        vals = [r[...].astype(F32) for r in refs[:nr]] + [p[...] for p in refs[nr:nr + npar]]
        res = fn(*vals)
        for o, v in zip(refs[nr + npar:], res):
            o[...] = v.astype(o.dtype)

    return pl.pallas_call(
        body, name=name, grid=(t // tm,),
        in_specs=_row_specs(rows, tm) + [pl.BlockSpec(p.shape, lambda i: (0, 0)) for p in pars],
        out_specs=tuple(pl.BlockSpec((tm, w), lambda i: (i, 0)) for (w, _) in outs),
        out_shape=tuple(jax.ShapeDtypeStruct((t, w), dt) for (w, dt) in outs),
        compiler_params=_cparams(("parallel",)),
    )(*[r[0] for r in rows], *pars)


def _rw_backward(fn, rows, pars, cots, drow_dtypes, name):
    t = rows[0][0].shape[0]
    tm = min(ROW_TILE, t)
    nr, npar, nc = len(rows), len(pars), len(cots)
    keep = [i for i, dt in enumerate(drow_dtypes) if dt is not None]

    def body(*refs):
        vals = [r[...].astype(F32) for r in refs[:nr]] + [p[...] for p in refs[nr:nr + npar]]
        cvals = tuple(c[...].astype(F32) for c in refs[nr + npar:nr + npar + nc])
        orefs = refs[nr + npar + nc:]
        _, vjp = jax.vjp(fn, *vals)
        g = vjp(cvals)
        for o, i in zip(orefs[:len(keep)], keep):
            o[...] = g[i].astype(o.dtype)
        first = pl.program_id(0) == 0
        for o, gi in zip(orefs[len(keep):], g[nr:]):
            @pl.when(first)
            def _(o=o, gi=gi):
                o[...] = gi

            @pl.when(jnp.logical_not(first))
            def _(o=o, gi=gi):
                o[...] += gi

    out_specs = [pl.BlockSpec((tm, rows[i][1]), lambda i_: (i_, 0)) for i in keep] + \
                [pl.BlockSpec(p.shape, lambda i_: (0, 0)) for p in pars]
    out_shape = [jax.ShapeDtypeStruct((t, rows[i][1]), drow_dtypes[i]) for i in keep] + \
                [jax.ShapeDtypeStruct(p.shape, F32) for p in pars]
    return pl.pallas_call(
        body, name=name, grid=(t // tm,),
        in_specs=_row_specs(rows, tm) + [pl.BlockSpec(p.shape, lambda i: (0, 0)) for p in pars] + _row_specs(cots, tm),
        out_specs=tuple(out_specs), out_shape=tuple(out_shape),
        compiler_params=_cparams(("arbitrary",)),
    )(*[r[0] for r in rows], *pars, *[c[0] for c in cots])


def _norm_fn(x, g):
    return (_rms(x, g),)


def _norm_skip_fn(x, g):
    return _rms(x, g), x


def _merge_fn(ga, gb, ya, yb):
    return (_sigmoid(ga) * ya + _sigmoid(gb) * yb,)


def _res_norm_fn(x, mo, g):
    x1 = x + mo
    return x1, _rms(x1, g)


def _loss_head(x1, fo, gf, target, name):
    t, d = x1.shape
    tm = min(ROW_TILE, t)

    def tile_loss(x2, g, tgt):
        err = _rms(x2, g) - tgt
        per_row = jnp.sum(err * err, axis=-1, keepdims=True) * (0.5 / d)
        return jnp.sum(per_row, axis=0, keepdims=True)

    def body(x1_ref, fo_ref, g_ref, t_ref, loss_ref, dx_ref, dxb_ref, dg_ref):
        x2 = x1_ref[...] + fo_ref[...]
        val, vjp = jax.vjp(functools.partial(tile_loss, tgt=t_ref[...]), x2, g_ref[...])
        dx2, dg = vjp(jnp.ones((1, 1), F32))
        dx_ref[...] = dx2
        dxb_ref[...] = dx2.astype(BF16)
        first = pl.program_id(0) == 0

        @pl.when(first)
        def _():
            loss_ref[...] = jnp.broadcast_to(val, loss_ref.shape)
            dg_ref[...] = dg

        @pl.when(jnp.logical_not(first))
        def _():
            loss_ref[...] += jnp.broadcast_to(val, loss_ref.shape)
            dg_ref[...] += dg

    row = pl.BlockSpec((tm, d), lambda i: (i, 0))
    vec = pl.BlockSpec((1, d), lambda i: (0, 0))
    return pl.pallas_call(
        body, name=name, grid=(t // tm,),
        in_specs=[row, row, vec, row],
        out_specs=(pl.BlockSpec((1, LANES), lambda i: (0, 0)), row, row, vec),
        out_shape=(jax.ShapeDtypeStruct((1, LANES), F32), jax.ShapeDtypeStruct((t, d), F32),
                   jax.ShapeDtypeStruct((t, d), BF16), jax.ShapeDtypeStruct((1, d), F32)),
        compiler_params=_cparams(("arbitrary",)),
    )(x1, fo, gf, target)


FFN_TILE_ROWS = 512
FFN_TILE_COLS = 256
FFN_COL_BLOCKS = FFN_HIDDEN // FFN_TILE_COLS


def _conv3_past(x, halo, w):
    rows = lax.broadcasted_iota(jnp.int32, x.shape, 0)
    x1 = jnp.where(rows == 0, halo[7:8, :], pltpu.roll(x, 1, 0))
    x2 = jnp.where(rows == 0, halo[6:7, :], jnp.where(rows == 1, halo[7:8, :], pltpu.roll(x, 2, 0)))
    return x * w[2:3] + x1 * w[1:2] + x2 * w[0:1], x1, x2


def _ffn_in_specs(tm, imap, jmap):
    per = tm // SUBLANES
    tile = lambda off: pl.BlockSpec((tm, FFN_TILE_COLS), lambda *g: (imap(*g), off + jmap(*g) % FFN_COL_BLOCKS))
    halo = lambda off: pl.BlockSpec((SUBLANES, FFN_TILE_COLS),
                                    lambda *g: (jnp.maximum(imap(*g) * per - 1, 0), off + jmap(*g) % FFN_COL_BLOCKS))
    wsp = lambda off: pl.BlockSpec((FFN_CONV, FFN_TILE_COLS), lambda *g: (0, off + jmap(*g) % FFN_COL_BLOCKS))
    return [tile(0), halo(0), wsp(0), tile(FFN_COL_BLOCKS), halo(FFN_COL_BLOCKS), wsp(FFN_COL_BLOCKS)]


def _ffn_act_forward(hpre, cw):
    t = hpre.shape[0]
    tm = min(FFN_TILE_ROWS, t)

    def body(hg, pg, wg, hu, pu, wu, out):
        live = (pl.program_id(0) > 0).astype(F32)
        cg, _, _ = _conv3_past(hg[...], pg[...] * live, wg[...])
        cu, _, _ = _conv3_past(hu[...], pu[...] * live, wu[...])
        out[...] = (_silu(cg) * cu).astype(out.dtype)

    return pl.pallas_call(
        body, name="ffn_act_fwd", grid=(t // tm, FFN_COL_BLOCKS),
        in_specs=_ffn_in_specs(tm, lambda i, j: i, lambda i, j: j),
        out_specs=pl.BlockSpec((tm, FFN_TILE_COLS), lambda i, j: (i, j)),
        out_shape=jax.ShapeDtypeStruct((t, FFN_HIDDEN), BF16),
        compiler_params=_cparams(("parallel", "parallel")),
    )(hpre, hpre, cw, hpre, hpre, cw)


def _conv3_future(d, nxt, w):
    tm = d.shape[0]
    rows = lax.broadcasted_iota(jnp.int32, d.shape, 0)
    d1 = jnp.where(rows == tm - 1, nxt[0:1, :], pltpu.roll(d, tm - 1, 0))
    d2 = jnp.where(rows == tm - 1, nxt[1:2, :], jnp.where(rows == tm - 2, nxt[0:1, :], pltpu.roll(d, tm - 2, 0)))
    return d * w[2:3] + d1 * w[1:2] + d2 * w[0:1]


def _ffn_backward(hpre, cw, dact):
    t = hpre.shape[0]
    tm = min(FFN_TILE_ROWS, t)
    n_tiles = t // tm
    per = tm // SUBLANES

    def d_conv_out(cg, cu, d):
        s = _sigmoid(cg)
        return d * cu * s * (1.0 + cg * (1.0 - s)), d * cg * s

    def body(hg, pg, ng, wg, hu, pu, nu, wu, da, dan, dhg, dhu, dwg, dwu):
        i = pl.program_id(1)
        live_prev = (i > 0).astype(F32)
        live_next = (i < n_tiles - 1).astype(F32)
        xg, xu = hg[...], hu[...]
        cg, g1, g2 = _conv3_past(xg, pg[...] * live_prev, wg[...])
        cu, u1, u2 = _conv3_past(xu, pu[...] * live_prev, wu[...])
        dg, du = d_conv_out(cg, cu, da[...])
        cgn, _, _ = _conv3_past(ng[...], hg[tm - SUBLANES:tm, :], wg[...])
        cun, _, _ = _conv3_past(nu[...], hu[tm - SUBLANES:tm, :], wu[...])
        dgn, dun = d_conv_out(cgn, cun, dan[...] * live_next)
        dhg[...] = _conv3_future(dg, dgn, wg[...]).astype(dhg.dtype)
        dhu[...] = _conv3_future(du, dun, wu[...]).astype(dhu.dtype)
        sums_g = [jnp.sum(xs * dg, axis=0, keepdims=True) for xs in (g2, g1, xg)]
        sums_u = [jnp.sum(xs * du, axis=0, keepdims=True) for xs in (u2, u1, xu)]

        @pl.when(i == 0)
        def _():
            for r_ in range(FFN_CONV):
                dwg[r_:r_ + 1, :] = sums_g[r_]
                dwu[r_:r_ + 1, :] = sums_u[r_]

        @pl.when(i > 0)
        def _():
            for r_ in range(FFN_CONV):
                dwg[r_:r_ + 1, :] += sums_g[r_]
                dwu[r_:r_ + 1, :] += sums_u[r_]

    nb = FFN_COL_BLOCKS
    nxt = lambda i: jnp.minimum((i + 1) * per, t // SUBLANES - 1)
    prv = lambda i: jnp.maximum(i * per - 1, 0)
    half = lambda off: [pl.BlockSpec((tm, FFN_TILE_COLS), lambda j, i: (i, off + j)),
                        pl.BlockSpec((SUBLANES, FFN_TILE_COLS), lambda j, i: (prv(i), off + j)),
                        pl.BlockSpec((SUBLANES, FFN_TILE_COLS), lambda j, i: (nxt(i), off + j)),
                        pl.BlockSpec((FFN_CONV, FFN_TILE_COLS), lambda j, i: (0, off + j))]
    tile = pl.BlockSpec((tm, FFN_TILE_COLS), lambda j, i: (i, j))
    taps = pl.BlockSpec((FFN_CONV, FFN_TILE_COLS), lambda j, i: (0, j))
    return pl.pallas_call(
        body, name="ffn_bwd", grid=(nb, n_tiles),
        in_specs=half(0) + half(nb) + [tile, pl.BlockSpec((SUBLANES, FFN_TILE_COLS), lambda j, i: (nxt(i), j))],
        out_specs=(tile, tile, taps, taps),
        out_shape=(jax.ShapeDtypeStruct((t, FFN_HIDDEN), BF16), jax.ShapeDtypeStruct((t, FFN_HIDDEN), BF16),
                   jax.ShapeDtypeStruct((FFN_CONV, FFN_HIDDEN), F32), jax.ShapeDtypeStruct((FFN_CONV, FFN_HIDDEN), F32)),
        compiler_params=_cparams(("parallel", "arbitrary")),
    )(hpre, hpre, hpre, cw, hpre, hpre, hpre, cw, dact, dact)


def _my_place():
    x, y, c = lax.axis_index("x"), lax.axis_index("y"), lax.axis_index("c")
    return x, y, c, 4 * x + 2 * y + c


N_CHIPS = 4


def _remote(src, dst, send_sem, recv_sem, dev):
    return pltpu.make_async_remote_copy(src_ref=src, dst_ref=dst, send_sem=send_sem, recv_sem=recv_sem, device_id=dev,
                                        device_id_type=pl.DeviceIdType.MESH)


def _chip_peer(x, y, k):
    return x ^ ((k >> 1) & 1), y ^ (k & 1)


def _all_gather_two_level(shard, name):
    r, w = shard.shape

    def body(src, out, send_sems, recv_sems, local_sem):
        x, y, c, me = _my_place()
        sibling = (x, y, 1 - c)
        mine = pltpu.make_async_copy(src, out.at[me], local_sem)
        mine.start()
        first = [_remote(src, out.at[me], send_sems.at[0], recv_sems.at[0], sibling)]
        for k in range(1, N_CHIPS):
            px, py = _chip_peer(x, y, k)
            first.append(_remote(src, out.at[me], send_sems.at[k], recv_sems.at[k], (px, py, c)))
        for cp in first:
            cp.start()
        passed = []
        for k in range(1, N_CHIPS):
            px, py = _chip_peer(x, y, k)
            landed = out.at[me ^ (2 * k)]
            _remote(src, landed, send_sems.at[k], recv_sems.at[k], (px, py, c)).wait_recv()
            fwd = _remote(landed, landed, send_sems.at[N_CHIPS - 1 + k], recv_sems.at[N_CHIPS - 1 + k], sibling)
            fwd.start()
            passed.append(fwd)
        _remote(src, out.at[me ^ 1], send_sems.at[0], recv_sems.at[0], sibling).wait_recv()
        for k in range(1, N_CHIPS):
            got = out.at[(me ^ 1) ^ (2 * k)]
            _remote(got, got, send_sems.at[N_CHIPS - 1 + k], recv_sems.at[N_CHIPS - 1 + k], sibling).wait_recv()
        for cp in first + passed:
            cp.wait_send()
        mine.wait()

    return pl.pallas_call(
        body, name=name,
        in_specs=[pl.BlockSpec(memory_space=pl.ANY)],
        out_specs=pl.BlockSpec(memory_space=pl.ANY),
        out_shape=jax.ShapeDtypeStruct((N_DEV, r, w), shard.dtype),
        scratch_shapes=[pltpu.SemaphoreType.DMA((N_DEV - 1,)), pltpu.SemaphoreType.DMA((N_DEV - 1,)), pltpu.SemaphoreType.DMA],
    )(shard)


def _device_peer(x, y, c, k):
    px, py, pc = x ^ ((k >> 2) & 1), y ^ ((k >> 1) & 1), c ^ (k & 1)
    return (px, py, pc), 4 * px + 2 * py + pc


_HBM = pl.BlockSpec(memory_space=pltpu.HBM)
_SEM = pl.BlockSpec(memory_space=pltpu.SEMAPHORE)


def _gather_start(shard, name):
    def body(src, land, send_sems, recv_sems, src_thru, land_thru, token):
        x, y, c, me = _my_place()
        for k in range(1, N_DEV):
            dev, _ = _device_peer(x, y, c, k)
            _remote(src, land.at[me], send_sems.at[k], recv_sems.at[k], dev).start()
        token[...] = jnp.zeros_like(token)

    landing = lax.empty((N_DEV,) + shard.shape, shard.dtype)
    return pl.pallas_call(
        body, name=name,
        out_shape=(pltpu.SemaphoreType.DMA((N_DEV,)), pltpu.SemaphoreType.DMA((N_DEV,)), pltpu.HBM(shard.shape, shard.dtype),
                   pltpu.HBM(landing.shape, landing.dtype), jax.ShapeDtypeStruct((SUBLANES, LANES), F32)),
        in_specs=(_HBM, _HBM), out_specs=(_SEM, _SEM, _HBM, _HBM, pl.BlockSpec(memory_space=pltpu.VMEM)),
        input_output_aliases={0: 2, 1: 3},
        compiler_params=pltpu.CompilerParams(has_side_effects=pltpu.SideEffectType.DATAFLOW_SIDE_EFFECTING),
    )(pltpu.with_memory_space_constraint(shard, pltpu.HBM), pltpu.with_memory_space_constraint(landing, pltpu.HBM))


def _gather_wait(send_sems, recv_sems, shard, landing, after, name):
    n_after = len(after)

    def body(*refs):
        src, land, send_sems, recv_sems = refs[:4]
        x, y, c, _ = _my_place()
        for k in range(1, N_DEV):
            dev, idx = _device_peer(x, y, c, k)
            cp = _remote(src, land.at[idx], send_sems.at[k], recv_sems.at[k], dev)
            cp.wait_send()
            cp.wait_recv()

    return pl.pallas_call(
        body, name=name,
        out_shape=(pltpu.HBM(shard.shape, shard.dtype), pltpu.HBM(landing.shape, landing.dtype)),
        in_specs=(_HBM, _HBM, _SEM, _SEM) + (pl.BlockSpec(memory_space=pl.ANY),) * n_after, out_specs=(_HBM, _HBM),
        input_output_aliases={0: 0, 1: 1},
        compiler_params=pltpu.CompilerParams(has_side_effects=pltpu.SideEffectType.DATAFLOW_SIDE_EFFECTING),
    )(shard, landing, send_sems, recv_sems, *after)[1]


def _slab_push_start(slabs, name):
    na = len(slabs)

    def body(*refs):
        srcs, lands = refs[:na], refs[na:2 * na]
        send_sems, recv_sems = refs[2 * na], refs[2 * na + 1]
        token = refs[-1]
        x, y, c, me = _my_place()
        for i in range(na):
            for k in range(1, N_DEV):
                dev, idx = _device_peer(x, y, c, k)
                s = i * N_DEV + k
                _remote(srcs[i].at[idx], lands[i].at[me], send_sems.at[s], recv_sems.at[s], dev).start()
        token[...] = jnp.zeros_like(token)

    hbm_shapes = [pltpu.HBM(a.shape, a.dtype) for a in slabs]
    ins = [pltpu.with_memory_space_constraint(a, pltpu.HBM) for a in slabs]
    ins += [pltpu.with_memory_space_constraint(lax.empty(a.shape, a.dtype), pltpu.HBM) for a in slabs]
    out = pl.pallas_call(
        body, name=name,
        out_shape=(pltpu.SemaphoreType.DMA((na * N_DEV,)), pltpu.SemaphoreType.DMA((na * N_DEV,)), *hbm_shapes, *hbm_shapes,
                   jax.ShapeDtypeStruct((SUBLANES, LANES), F32)),
        in_specs=(_HBM,) * (2 * na), out_specs=(_SEM, _SEM) + (_HBM,) * (2 * na) + (pl.BlockSpec(memory_space=pltpu.VMEM),),
        input_output_aliases={i: 2 + i for i in range(2 * na)},
        compiler_params=pltpu.CompilerParams(has_side_effects=pltpu.SideEffectType.DATAFLOW_SIDE_EFFECTING),
    )(*ins)
    return out[0], out[1], list(out[2:2 + na]), list(out[2 + na:2 + 2 * na]), out[-1]


def _slab_push_wait(send_sems, recv_sems, slabs, landings, after, name):
    na = len(slabs)

    def body(*refs):
        srcs, lands = refs[:na], refs[na:2 * na]
        send_sems, recv_sems = refs[2 * na], refs[2 * na + 1]
        x, y, c, me = _my_place()
        for i in range(na):
            for k in range(1, N_DEV):
                dev, idx = _device_peer(x, y, c, k)
                s = i * N_DEV + k
                cp = _remote(srcs[i].at[idx], lands[i].at[idx], send_sems.at[s], recv_sems.at[s], dev)
                cp.wait_send()
                cp.wait_recv()

    hbm_shapes = tuple(pltpu.HBM(a.shape, a.dtype) for a in slabs)
    out = pl.pallas_call(
        body, name=name, out_shape=hbm_shapes + hbm_shapes,
        in_specs=(_HBM,) * (2 * na) + (_SEM, _SEM) + (pl.BlockSpec(memory_space=pl.ANY),) * len(after),
        out_specs=(_HBM,) * (2 * na), input_output_aliases={i: i for i in range(2 * na)},
        compiler_params=pltpu.CompilerParams(has_side_effects=pltpu.SideEffectType.DATAFLOW_SIDE_EFFECTING),
    )(*slabs, *landings, send_sems, recv_sems, *after)
    return list(out[na:])


def _pair_exchange(arrays, name):
    na = len(arrays)

    def body(*refs):
        srcs, dsts, (send_sems, recv_sems) = refs[:na], refs[na:2 * na], refs[2 * na:]
        x, y, c, _ = _my_place()
        sibling = (x, y, 1 - c)
        copies = []
        for i in range(na):
            for q in range(N_CHIPS):
                s = i * N_CHIPS + q
                copies.append(_remote(srcs[i].at[2 * q + 1 - c], dsts[i].at[q], send_sems.at[s], recv_sems.at[s], sibling))
        for cp in copies:
            cp.start()
        for cp in copies:
            cp.wait_recv()
        for cp in copies:
            cp.wait_send()

    hbm = pl.BlockSpec(memory_space=pl.ANY)
    return pl.pallas_call(
        body, name=name, in_specs=[hbm] * na, out_specs=tuple([hbm] * na),
        out_shape=tuple(jax.ShapeDtypeStruct((N_CHIPS,) + a.shape[1:], a.dtype) for a in arrays),
        scratch_shapes=[pltpu.SemaphoreType.DMA((na * N_CHIPS,)), pltpu.SemaphoreType.DMA((na * N_CHIPS,))],
    )(*arrays)


ELEMENTWISE_COLS = 256


def _pair_sum(slabs, recv, core, out_dtype, name):
    _, r, w = slabs.shape
    tc = ELEMENTWISE_COLS

    def body(core_ref, mine, theirs, out):
        out[...] = (mine[...] + theirs[...]).astype(out.dtype)

    grid_spec = pltpu.PrefetchScalarGridSpec(
        num_scalar_prefetch=1, grid=(N_CHIPS, w // tc),
        in_specs=[pl.BlockSpec((None, r, tc), lambda q, j, core_ref: (2 * q + core_ref[0], 0, j)),
                  pl.BlockSpec((None, r, tc), lambda q, j, core_ref: (q, 0, j))],
        out_specs=pl.BlockSpec((None, r, tc), lambda q, j, core_ref: (q, 0, j)))
    return pl.pallas_call(body, name=name, grid_spec=grid_spec,
                          out_shape=jax.ShapeDtypeStruct((N_CHIPS, r, w), out_dtype),
                          compiler_params=_cparams(("parallel", "parallel")))(core, slabs, recv)


def _chip_exchange(arrays, name):
    na = len(arrays)

    def body(*refs):
        srcs, dsts, (send_sems, recv_sems, local_sems) = refs[:na], refs[na:2 * na], refs[2 * na:]
        x, y, c, _ = _my_place()
        chip = 2 * x + y
        own = [pltpu.make_async_copy(srcs[i].at[chip], dsts[i].at[chip], local_sems.at[i]) for i in range(na)]
        for cp in own:
            cp.start()
        sends, arrivals = [], []
        for i in range(na):
            for k in range(1, N_CHIPS):
                px, py = _chip_peer(x, y, k)
                s = i * N_CHIPS + k
                sends.append(_remote(srcs[i].at[chip ^ k], dsts[i].at[chip], send_sems.at[s], recv_sems.at[s], (px, py, c)))
                arrivals.append(_remote(srcs[i].at[chip], dsts[i].at[chip ^ k], send_sems.at[s], recv_sems.at[s], (px, py, c)))
        for cp in sends:
            cp.start()
        for cp in arrivals:
            cp.wait_recv()
        for cp in sends:
            cp.wait_send()
        for cp in own:
            cp.wait()

    hbm = pl.BlockSpec(memory_space=pl.ANY)
    return pl.pallas_call(
        body, name=name, in_specs=[hbm] * na, out_specs=tuple([hbm] * na),
        out_shape=tuple(jax.ShapeDtypeStruct(a.shape, a.dtype) for a in arrays),
        scratch_shapes=[pltpu.SemaphoreType.DMA((na * N_CHIPS,)), pltpu.SemaphoreType.DMA((na * N_CHIPS,)),
                        pltpu.SemaphoreType.DMA((na,))],
    )(*arrays)


def _adamw_update(g, w, m, v):
    c1 = 1.0 / (1.0 - ADAM_B1 ** ADAM_STEP)
    c2 = 1.0 / (1.0 - ADAM_B2 ** ADAM_STEP)
    mn = ADAM_B1 * m + (1.0 - ADAM_B1) * g
    vn = ADAM_B2 * v + (1.0 - ADAM_B2) * (g * g)
    return -ADAM_LR * ((mn * c1) / (jnp.sqrt(vn * c2) + ADAM_EPS) + ADAM_WD * w), mn, vn


def _reduce_adamw(parts, w, m, v, name):
    n_parts, r, wd = parts.shape
    tc = ELEMENTWISE_COLS

    def body(p_ref, w_ref, m_ref, v_ref, g_out, d_out, m_out, v_out):
        g = p_ref[0].astype(F32)
        for s in range(1, n_parts):
            g = g + p_ref[s].astype(F32)
        g_out[...] = g
        d_out[...], m_out[...], v_out[...] = _adamw_update(g, w_ref[...], m_ref[...], v_ref[...])

    blk = pl.BlockSpec((r, tc), lambda j: (0, j))
    shp = jax.ShapeDtypeStruct((r, wd), F32)
    return pl.pallas_call(
        body, name=name, grid=(wd // tc,),
        in_specs=[pl.BlockSpec((n_parts, r, tc), lambda j: (0, 0, j)), blk, blk, blk],
        out_specs=(blk, blk, blk, blk), out_shape=(shp, shp, shp, shp),
        compiler_params=_cparams(("parallel",)),
    )(parts, w, m, v)


def _reduce_landed(landing, own, me, name):
    n_parts, r, wd = landing.shape
    tc = ELEMENTWISE_COLS

    def body(me_ref, land_ref, own_ref, g_out):
        mine = own_ref[...].astype(F32)
        g = None
        for s in range(n_parts):
            part = jnp.where(me_ref[0] == s, mine, land_ref[s].astype(F32))
            g = part if g is None else g + part
        g_out[...] = g

    grid_spec = pltpu.PrefetchScalarGridSpec(
        num_scalar_prefetch=1, grid=(wd // tc,),
        in_specs=[pl.BlockSpec((n_parts, r, tc), lambda j, me_ref: (0, 0, j)),
                  pl.BlockSpec((None, r, tc), lambda j, me_ref: (me_ref[0], 0, j))],
        out_specs=pl.BlockSpec((r, tc), lambda j, me_ref: (0, j)))
    return pl.pallas_call(body, name=name, grid_spec=grid_spec, out_shape=jax.ShapeDtypeStruct((r, wd), F32),
                          compiler_params=_cparams(("parallel",)))(me, landing, own)


def _adamw(g, w, m, v, name):
    r, wd = g.shape
    tr = _mm_pick_rows(r)

    def body(g_ref, w_ref, m_ref, v_ref, d_out, m_out, v_out):
        d_out[...], m_out[...], v_out[...] = _adamw_update(g_ref[...], w_ref[...], m_ref[...], v_ref[...])

    blk = pl.BlockSpec((tr, wd), lambda i: (i, 0))
    shp = jax.ShapeDtypeStruct((r, wd), F32)
    return pl.pallas_call(body, name=name, grid=(r // tr,), in_specs=[blk] * 4, out_specs=(blk,) * 3,
                          out_shape=(shp,) * 3, compiler_params=_cparams(("parallel",)))(g, w, m, v)


def _mm_pick_rows(r):
    for t_ in (256, 128, 64, 32, 16, 8):
        if r % t_ == 0:
            return t_
    return r


def _reduce_landed_adamw(landing, own, me, w, m, v, name):
    n_parts, r, wd = landing.shape
    tc = ELEMENTWISE_COLS

    def body(me_ref, land_ref, own_ref, w_ref, m_ref, v_ref, g_out, d_out, m_out, v_out):
        mine = own_ref[...].astype(F32)
        g = None
        for s in range(n_parts):
            part = jnp.where(me_ref[0] == s, mine, land_ref[s].astype(F32))
            g = part if g is None else g + part
        g_out[...] = g
        d_out[...], m_out[...], v_out[...] = _adamw_update(g, w_ref[...], m_ref[...], v_ref[...])

    blk = pl.BlockSpec((r, tc), lambda j, me_ref: (0, j))
    shp = jax.ShapeDtypeStruct((r, wd), F32)
    grid_spec = pltpu.PrefetchScalarGridSpec(
        num_scalar_prefetch=1, grid=(wd // tc,),
        in_specs=[pl.BlockSpec((n_parts, r, tc), lambda j, me_ref: (0, 0, j)),
                  pl.BlockSpec((None, r, tc), lambda j, me_ref: (me_ref[0], 0, j)), blk, blk, blk],
        out_specs=(blk, blk, blk, blk))
    return pl.pallas_call(body, name=name, grid_spec=grid_spec, out_shape=(shp, shp, shp, shp),
                          compiler_params=_cparams(("parallel",)))(me, landing, own, w, m, v)


PACK_W = 1024


def _pad_heads(a, slots):
    lead = a.shape[:-1]
    a = a.reshape(lead + (slots, RWKV_HEAD_DIM))
    a = jnp.pad(a, [(0, 0)] * (len(lead) + 1) + [(0, LANES - RWKV_HEAD_DIM)])
    return a.reshape(lead + (slots * LANES,))


def _flat_pack(arrs, dtype, row_mult):
    flat = jnp.concatenate([a.reshape(-1).astype(dtype) for a in arrs])
    n = flat.shape[0]
    rows = -(-n // PACK_W)
    rows = -(-rows // row_mult) * row_mult
    return jnp.pad(flat, (0, rows * PACK_W - n)).reshape(rows, PACK_W)


def _row_pack(arrs, dtype, row_mult):
    parts = [a.astype(dtype) if a.shape[1] == PACK_W else a.astype(dtype).reshape(-1, PACK_W) for a in arrs]
    rows = sum(p.shape[0] for p in parts)
    pad = -(-rows // row_mult) * row_mult - rows
    return jnp.concatenate(parts + ([jnp.zeros((pad, PACK_W), dtype)] if pad else []), axis=0)


def _unpack_row_gathered(g, names, shard_shapes):
    out, r0 = {}, 0
    for n in names:
        s = shard_shapes[n]
        rows = s[0] * s[1] // PACK_W
        seg = g[:, r0:r0 + rows, :]
        r0 += rows
        if s[1] == PACK_W:
            assert SHARD_AXIS[n] == 0
            out[n] = seg.reshape(N_DEV * s[0], s[1])
        else:
            assert SHARD_AXIS[n] == 1
            out[n] = jnp.transpose(seg.reshape((N_DEV,) + tuple(s)), (1, 0, 2)).reshape(s[0], N_DEV * s[1])
    return out


def _unpack_gathered(g, names, shard_shapes):
    flat = g.reshape(N_DEV, -1)
    out, off = {}, 0
    for n in names:
        s = shard_shapes[n]
        size = s[0] * s[1]
        seg = flat[:, off:off + size].reshape((N_DEV,) + tuple(s))
        off += size
        if SHARD_AXIS[n] == 1:
            out[n] = jnp.transpose(seg, (1, 0, 2)).reshape(s[0], N_DEV * s[1])
        else:
            out[n] = seg.reshape(N_DEV * s[0], s[1])
    return out


def _shard_major(full, axis):
    a, b = full.shape
    if axis == 1:
        return jnp.transpose(full.reshape(a, N_DEV, b // N_DEV), (1, 0, 2)).reshape(N_DEV, -1)
    return full.reshape(N_DEV, -1)


def _prepare_weights(full, rep):
    w = full['w_in']
    d = w.shape[1]
    rkv = jnp.pad(w[0:1536].reshape(3 * RWKV_HEADS, RWKV_HEAD_DIM, d), ((0, 0), (0, LANES - RWKV_HEAD_DIM), (0, 0)))
    w_cat = jnp.concatenate([
        w[3848:4872], w[4872:5896], rkv.reshape(3 * RWKV_HEADS * LANES, d), w[1792:3328], w[3328:3840],
        w[1536:1792], jnp.pad(w[3840:3848], ((0, LANES - 8), (0, 0))), jnp.zeros((LANES, d), w.dtype)], axis=0)
    assert w_cat.shape[0] == CAT_W
    mu = rep['rwkv_mu']
    vecs = [mu[0:512], mu[512:1024], mu[1024:1536], rep['rwkv_w0'], rep['rwkv_a0'], rep['rwkv_k_k'], rep['rwkv_k_a'],
            rep['rwkv_ln_w'], rep['rwkv_ln_b'], rep['rwkv_r_k'].reshape(-1)]
    ppack = jnp.stack([jnp.pad(v.reshape(RWKV_HEADS, RWKV_HEAD_DIM), ((0, 0), (0, LANES - RWKV_HEAD_DIM))) for v in vecs], axis=1)
    ppack = jnp.pad(ppack, ((0, 0), (0, 16 - len(vecs)), (0, 0)))
    mulo = mu[1536:1792].reshape(1, 2 * LANES)
    wl = jnp.zeros((3, 2 * LANES, RWKV_HEADS * LANES), F32)
    wl = wl.at[0, 0:64].set(_pad_heads(full['rwkv_w2'], RWKV_HEADS))
    wl = wl.at[1, 64:128].set(_pad_heads(full['rwkv_a2'], RWKV_HEADS))
    wl = wl.at[2, 128:256].set(_pad_heads(full['rwkv_g2'], RWKV_HEADS))
    wl = jnp.transpose(wl.reshape(3, 2 * LANES, RWKV_HEADS, LANES), (2, 0, 1, 3))
    cw = full['gdn_conv_w'].reshape(GDN_CONV, 3, GDN_HEADS, LANES)
    cwpack = jnp.pad(jnp.transpose(cw, (2, 1, 0, 3)), ((0, 0), (0, 0), (0, SUBLANES - GDN_CONV), (0, 0)))
    gpar = jnp.zeros((SUBLANES, LANES), F32)
    gpar = gpar.at[0, 0:GDN_HEADS].set(rep['gdn_a_log']).at[1, 0:GDN_HEADS].set(rep['gdn_dt_bias']).at[2].set(rep['gdn_norm_w'])
    return dict(w_cat=w_cat, ffn_cw=full['ffn_conv_w'], ppack=ppack, mulo=mulo, wl=wl, cwpack=cwpack, gpar=gpar,
                g1=rep['norm1_g'].reshape(1, -1), g2=rep['norm2_g'].reshape(1, -1), gf=rep['final_g'].reshape(1, -1))


def _prepare_late_weights(full):
    rp = full['rwkv_proj']
    rproj = jnp.pad(rp.reshape(RWKV_HEADS, RWKV_HEAD_DIM, -1), ((0, 0), (0, LANES - RWKV_HEAD_DIM), (0, 0))).reshape(RWKV_HEADS * LANES, -1)
    return dict(rproj=rproj, gproj=full['gdn_proj'], w_out=full['w_out'], ffn_up=full['ffn_up'], ffn_down=full['ffn_down'])


def _local_step(x, target, p, late_weights, push_grads):
    d = x.shape[1]
    full_w = lambda a: (a, a.shape[1], 0)
    (u,) = _rw_forward(_norm_fn, [full_w(x)], [p['g1']], [(d, BF16)], "norm1")
    p_cat = _matmul(u, p['w_cat'], 'nt', F32, "proj_in")
    ya_pre, st_r = _rwkv_forward(p_cat, p['ppack'], p['mulo'], p['wl'])
    yb_pre, st_g = _gdn_forward(p_cat, p['cwpack'], p['gpar'])
    p = {**p, **late_weights((ya_pre, yb_pre))}
    ya = _matmul(ya_pre, p['rproj'], 'nn', F32, "rwkv_proj")
    yb = _matmul(yb_pre, p['gproj'], 'nn', F32, "gdn_proj")
    gates = [(p_cat, d, OFF_GA // d), (p_cat, d, OFF_GB // d)]
    (mixed,) = _rw_forward(_merge_fn, gates + [full_w(ya), full_w(yb)], [], [(d, BF16)], "merge")
    mo = _matmul(mixed, p['w_out'], 'nn', F32, "out_proj")
    x1, n2 = _rw_forward(_res_norm_fn, [full_w(x), full_w(mo)], [p['g2']], [(d, F32), (d, BF16)], "res_norm2")
    hpre = _matmul(n2, p['ffn_up'], 'nt', F32, "ffn_up")
    act = _ffn_act_forward(hpre, p['ffn_cw'])
    fo = _matmul(act, p['ffn_down'], 'nn', F32, "ffn_down")
    loss_vec, dx2, dx2b, dgf = _loss_head(x1, fo, p['gf'], target, "loss_head")

    dact = _matmul(dx2b, p['ffn_down'], 'nt', F32, "d_act")
    dw_down = _matmul(act, dx2b, 'tn', BF16, "dw_ffn_down")
    dh_gate, dh_up, dcw_gate, dcw_up = _ffn_backward(hpre, p['ffn_cw'], dact)
    dh = jnp.concatenate([dh_gate, dh_up], axis=1)
    dcw_f = jnp.concatenate([dcw_gate, dcw_up], axis=1)
    dn2 = _matmul(dh, p['ffn_up'], 'nn', F32, "d_norm2")
    dw_up = _matmul(dh, n2, 'tn', BF16, "dw_ffn_up")
    token = push_grads({'ffn_down': dw_down, 'ffn_up': dw_up})
    dx1, dx1b, dg2 = _rw_backward(_res_norm_fn, [full_w(x), full_w(mo)], [p['g2'] + token], [full_w(dx2), full_w(dn2)],
                                  [F32, BF16], "res_norm2_bwd")
    dmixed = _matmul(dx1b, p['w_out'], 'nt', F32, "d_mixed")
    dw_out = _matmul(mixed, dx1b, 'tn', BF16, "dw_out")
    dga, dgb, dya, dyb = _rw_backward(_merge_fn, gates + [full_w(ya), full_w(yb)], [], [full_w(dmixed)],
                                      [BF16, BF16, BF16, BF16], "merge_bwd")
    d_ya_pre = _matmul(dya, p['rproj'], 'nt', F32, "d_rwkv_out")
    dw_rproj = _matmul(ya_pre, dya, 'tn', F32, "dw_rwkv_proj")
    d_yb_pre = _matmul(dyb, p['gproj'], 'nt', F32, "d_gdn_out")
    dw_gproj = _matmul(yb_pre, dyb, 'tn', F32, "dw_gdn_proj")
    dpr, dpk, dpv, dplo, dpp, dml, dwl = _rwkv_backward(p_cat, p['ppack'], p['mulo'], p['wl'], st_r, d_ya_pre)
    dq, dk, dv, dz, dab, dcw_g, dgp = _gdn_backward(p_cat, p['cwpack'], p['gpar'], st_g, d_yb_pre)
    t = x.shape[0]
    dp_cat = jnp.concatenate([dga, dgb, dpr, dpk, dpv, dq, dk, dv, dz, dplo.astype(BF16), dab.astype(BF16),
                              jnp.zeros((t, LANES), BF16)], axis=1)
    dw_cat = _matmul(dp_cat, u, 'tn', BF16, "dw_in")
    dw_in = jnp.concatenate([dw_cat[OFF_RKV:OFF_QKV].reshape(3 * RWKV_HEADS, LANES, d)[:, :RWKV_HEAD_DIM].reshape(-1, d),
                             dw_cat[OFF_LO:OFF_AB], dw_cat[OFF_QKV:OFF_Z], dw_cat[OFF_Z:OFF_LO], dw_cat[OFF_AB:OFF_AB + 8],
                             dw_cat[OFF_GA:OFF_GB], dw_cat[OFF_GB:OFF_RKV]], axis=0)
    token = push_grads({'w_out': dw_out, 'w_in': dw_in})
    du = _matmul(dp_cat, p['w_cat'], 'nn', F32, "d_norm1")
    grad_x, dg1 = _rw_backward(_norm_skip_fn, [full_w(x)], [p['g1'] + token], [full_w(du), full_w(dx1)], [F32], "norm1_bwd")

    heads = lambda row: dpp[:, row, :RWKV_HEAD_DIM].reshape(-1)
    lora = lambda j, lo_, hi_: jnp.transpose(dwl[:, j, lo_:hi_, :RWKV_HEAD_DIM], (1, 0, 2)).reshape(hi_ - lo_, RWKV_WIDTH)
    grads = {
        'norm1_g': dg1[0],
        'w_in': dw_in,
        'rwkv_mu': jnp.concatenate([heads(0), heads(1), heads(2), jnp.sum(dml[:, 0, :], axis=0)]),
        'rwkv_w0': heads(3), 'rwkv_a0': heads(4), 'rwkv_k_k': heads(5), 'rwkv_k_a': heads(6),
        'rwkv_ln_w': heads(7), 'rwkv_ln_b': heads(8), 'rwkv_r_k': heads(9).reshape(RWKV_HEADS, RWKV_HEAD_DIM),
        'rwkv_w2': lora(0, 0, 64), 'rwkv_a2': lora(1, 64, 128), 'rwkv_g2': lora(2, 128, 256),
        'rwkv_proj': dw_rproj.reshape(RWKV_HEADS, LANES, -1)[:, :RWKV_HEAD_DIM].reshape(RWKV_WIDTH, -1),
        'gdn_conv_w': jnp.transpose(dcw_g[:, :, :GDN_CONV, :], (2, 1, 0, 3)).reshape(GDN_CONV, 3 * GDN_WIDTH),
        'gdn_a_log': jnp.sum(dgp[:, 0, :GDN_HEADS], axis=0), 'gdn_dt_bias': jnp.sum(dgp[:, 1, :GDN_HEADS], axis=0),
        'gdn_norm_w': jnp.sum(dgp[:, 2, :], axis=0),
        'gdn_proj': dw_gproj, 'w_out': dw_out, 'norm2_g': dg2[0], 'ffn_up': dw_up, 'ffn_conv_w': dcw_f,
        'ffn_down': dw_down, 'final_g': dgf[0],
    }
    return loss_vec, grad_x, grads


def kernel(x, norm1_g, w_in, rwkv_mu, rwkv_w0, rwkv_w2, rwkv_a0, rwkv_a2, rwkv_g2, rwkv_k_k, rwkv_k_a, rwkv_r_k, rwkv_ln_w, rwkv_ln_b, rwkv_proj, gdn_conv_w, gdn_a_log, gdn_dt_bias, gdn_norm_w, gdn_proj, w_out, norm2_g, ffn_up, ffn_conv_w, ffn_down, final_g, loss_target, m_norm1_g, m_w_in, m_rwkv_mu, m_rwkv_w0, m_rwkv_w2, m_rwkv_a0, m_rwkv_a2, m_rwkv_g2, m_rwkv_k_k, m_rwkv_k_a, m_rwkv_r_k, m_rwkv_ln_w, m_rwkv_ln_b, m_rwkv_proj, m_gdn_conv_w, m_gdn_a_log, m_gdn_dt_bias, m_gdn_norm_w, m_gdn_proj, m_w_out, m_norm2_g, m_ffn_up, m_ffn_conv_w, m_ffn_down, m_final_g, v_norm1_g, v_w_in, v_rwkv_mu, v_rwkv_w0, v_rwkv_w2, v_rwkv_a0, v_rwkv_a2, v_rwkv_g2, v_rwkv_k_k, v_rwkv_k_a, v_rwkv_r_k, v_rwkv_ln_w, v_rwkv_ln_b, v_rwkv_proj, v_gdn_conv_w, v_gdn_a_log, v_gdn_dt_bias, v_gdn_norm_w, v_gdn_proj, v_w_out, v_norm2_g, v_ffn_up, v_ffn_conv_w, v_ffn_down, v_final_g):
    given = dict(zip(WEIGHT_NAMES, (norm1_g, w_in, rwkv_mu, rwkv_w0, rwkv_w2, rwkv_a0, rwkv_a2, rwkv_g2, rwkv_k_k, rwkv_k_a, rwkv_r_k,
                                    rwkv_ln_w, rwkv_ln_b, rwkv_proj, gdn_conv_w, gdn_a_log, gdn_dt_bias, gdn_norm_w, gdn_proj, w_out,
                                    norm2_g, ffn_up, ffn_conv_w, ffn_down, final_g)))
    mom1 = dict(zip(WEIGHT_NAMES, (m_norm1_g, m_w_in, m_rwkv_mu, m_rwkv_w0, m_rwkv_w2, m_rwkv_a0, m_rwkv_a2, m_rwkv_g2, m_rwkv_k_k,
                                   m_rwkv_k_a, m_rwkv_r_k, m_rwkv_ln_w, m_rwkv_ln_b, m_rwkv_proj, m_gdn_conv_w, m_gdn_a_log,
                                   m_gdn_dt_bias, m_gdn_norm_w, m_gdn_proj, m_w_out, m_norm2_g, m_ffn_up, m_ffn_conv_w, m_ffn_down,
                                   m_final_g)))
    mom2 = dict(zip(WEIGHT_NAMES, (v_norm1_g, v_w_in, v_rwkv_mu, v_rwkv_w0, v_rwkv_w2, v_rwkv_a0, v_rwkv_a2, v_rwkv_g2, v_rwkv_k_k,
                                   v_rwkv_k_a, v_rwkv_r_k, v_rwkv_ln_w, v_rwkv_ln_b, v_rwkv_proj, v_gdn_conv_w, v_gdn_a_log,
                                   v_gdn_dt_bias, v_gdn_norm_w, v_gdn_proj, v_w_out, v_norm2_g, v_ffn_up, v_ffn_conv_w, v_ffn_down,
                                   v_final_g)))
    def strip(n, a):
        a = a if n == 'final_g' else a.reshape(a.shape[1:])
        return a.T if n in TRANSPOSED else a

    local = {n: strip(n, a) for n, a in given.items()}
    shard_shapes = {n: local[n].shape for n in SHARD_AXIS}
    sharded = BIG_SHARDED + SMALL_SHARDED

    late_names = [n for n in BIG_SHARDED if n != 'w_in']
    g_in = _all_gather_two_level(_row_pack([local['w_in']], BF16, 16), "gather_w_in")
    g_small = _all_gather_two_level(_flat_pack([local[n] for n in SMALL_SHARDED], F32, SUBLANES), "gather_small")
    late_pack, g_in, g_small = lax.optimization_barrier((_row_pack([local[n] for n in late_names], BF16, 16), g_in, g_small))
    send_sems, recv_sems, late_pack, landing, token = _gather_start(late_pack, "gather_late_start")
    full = _unpack_row_gathered(g_in, ['w_in'], shard_shapes)
    full.update(_unpack_gathered(g_small, SMALL_SHARDED, shard_shapes))
    rep = {n: local[n] for n in REPLICATED}
    rep['norm1_g'] = rep['norm1_g'] + token[0, 0]

    def late_weights(after):
        got = _gather_wait(send_sems, recv_sems, late_pack, landing, after, "gather_late_wait")
        me = 4 * lax.axis_index("x") + 2 * lax.axis_index("y") + lax.axis_index("c")
        slot = lax.broadcasted_iota(jnp.int32, (N_DEV, 1, 1), 0)
        got = jnp.where(slot == me, late_pack[None], got)
        return _prepare_late_weights(_unpack_row_gathered(got, late_names, shard_shapes))

    pushes = []
    me = 4 * lax.axis_index("x") + 2 * lax.axis_index("y") + lax.axis_index("c")
    slot = lax.broadcasted_iota(jnp.int32, (N_DEV, 1, 1), 0)

    def push_grads(group):
        names = list(group)
        slabs = [group[n].reshape(N_DEV, -1, group[n].shape[1]) for n in names]
        send_sems, recv_sems, slabs, landings, token = _slab_push_start(slabs, "grad_push_start_" + "_".join(names))
        pushes.append((names, send_sems, recv_sems, slabs, landings))
        return token[0, 0]

    loss_vec, grad_x, grads = _local_step(x[0], loss_target[0], _prepare_weights(full, rep), late_weights, push_grads)

    landed = {}
    for names, send_sems, recv_sems, slabs, landings in pushes:
        got = _slab_push_wait(send_sems, recv_sems, slabs, landings, (grad_x,), "grad_push_wait_" + "_".join(names))
        for n, slab, land in zip(names, slabs, got):
            landed[n] = (land, slab)

    small_sharded = ['rwkv_proj', 'gdn_proj'] + SMALL_SHARDED
    small_names = small_sharded + REPLICATED
    rep_vec = jnp.concatenate([grads[n].reshape(-1) for n in REPLICATED] + [loss_vec[0, 0:1]])
    slab_small = jnp.concatenate([_shard_major(grads[n], SHARD_AXIS[n]) for n in small_sharded] +
                                 [jnp.broadcast_to(rep_vec[None], (N_DEV, rep_vec.shape[0]))], axis=1)
    small_rows = -(-slab_small.shape[1] // (PACK_W * SUBLANES)) * SUBLANES
    slab_small = jnp.pad(slab_small, ((0, 0), (0, small_rows * PACK_W - slab_small.shape[1]))).reshape(N_DEV, small_rows, PACK_W)
    core = lax.axis_index("c").astype(jnp.int32).reshape(1)
    (from_sibling,) = _pair_exchange([slab_small], "grad_pair_exchange")
    chip_small = _pair_sum(slab_small, from_sibling, core, F32, "grad_pair_sum_small")
    (parts_small,) = _chip_exchange([chip_small], "grad_chip_exchange")

    def pack_local(src):
        flat = jnp.concatenate([strip(n, src[n]).reshape(-1) for n in small_names])
        return jnp.pad(flat, (0, small_rows * PACK_W - flat.shape[0])).reshape(small_rows, PACK_W)

    results = [({}, None) for _ in range(4)]
    me_arr = me.astype(jnp.int32).reshape(1)
    for n in ROW_SHARDED:
        if n in TRANSPOSED:
            natural = lambda a: a.reshape(a.shape[1:])
            grad = _reduce_landed(*landed[n], me_arr, "grad_sum_" + n).T
            packs = (grad,) + tuple(_adamw(grad, natural(given[n]), natural(mom1[n]), natural(mom2[n]), "adamw_" + n))
        else:
            packs = _reduce_landed_adamw(*landed[n], me_arr, local[n], strip(n, mom1[n]), strip(n, mom2[n]), "adamw_" + n)
        for (out, _), pk in zip(results, packs):
            out[n] = pk.reshape(given[n].shape)
    packs = _reduce_adamw(parts_small, pack_local(given), pack_local(mom1), pack_local(mom2), "adamw_small")
    for i, pk in enumerate(packs):
        flat, off = pk.reshape(-1), 0
        for n in small_names:
            size = int(np.prod(given[n].shape))
            results[i][0][n] = flat[off:off + size].reshape(given[n].shape)
            off += size
        results[i] = (results[i][0], flat[off])
    (g_out, loss), (d_out, _), (m_out, _), (v_out, _) = results
    return (loss, grad_x[None], *[g_out[n] for n in WEIGHT_NAMES], *[d_out[n] for n in WEIGHT_NAMES],
            *[m_out[n] for n in WEIGHT_NAMES], *[v_out[n] for n in WEIGHT_NAMES])
```

```python
import functools

import jax
import jax.numpy as jnp
import numpy as np
from jax import lax
from jax.experimental import pallas as pl
from jax.experimental.pallas import tpu as pltpu

F32 = jnp.float32
BF16 = jnp.bfloat16

N_DEV = 8
D_MODEL = 1024
CHUNK = 64
RWKV_HEADS = 8
RWKV_HEAD_DIM = 64
RWKV_WIDTH = 512
GDN_HEADS = 4
GDN_HEAD_DIM = 128
GDN_WIDTH = 512
GDN_CONV = 4
FFN_HIDDEN = 2816
FFN_CONV = 3
NORM_EPS = 1e-6
L2_EPS = 1e-6
RWKV_GN_EPS = 64e-5
LANES = 128
SUBLANES = 8
VMEM_LIMIT = 56 * 1024 * 1024

ADAM_LR = 0.001
ADAM_B1 = 0.9
ADAM_B2 = 0.999
ADAM_EPS = 1e-08
ADAM_WD = 0.01
ADAM_STEP = 10

OFF_GA, OFF_GB, OFF_RKV, OFF_QKV, OFF_Z, OFF_LO, OFF_AB, CAT_W = 0, 1024, 2048, 5120, 6656, 7168, 7424, 7680
RWKV_HB = 8
RWKV_STEP_CHUNKS = 2
RWKV_TILE = RWKV_STEP_CHUNKS * CHUNK
GDN_HB = 4
GDN_STEP_CHUNKS = 4
GDN_TILE = GDN_STEP_CHUNKS * CHUNK

WEIGHT_NAMES = ['norm1_g', 'w_in', 'rwkv_mu', 'rwkv_w0', 'rwkv_w2', 'rwkv_a0', 'rwkv_a2', 'rwkv_g2', 'rwkv_k_k', 'rwkv_k_a',
                'rwkv_r_k', 'rwkv_ln_w', 'rwkv_ln_b', 'rwkv_proj', 'gdn_conv_w', 'gdn_a_log', 'gdn_dt_bias', 'gdn_norm_w',
                'gdn_proj', 'w_out', 'norm2_g', 'ffn_up', 'ffn_conv_w', 'ffn_down', 'final_g']
BIG_SHARDED = ['w_in', 'ffn_up', 'ffn_down', 'w_out', 'rwkv_proj', 'gdn_proj']
SMALL_SHARDED = ['rwkv_w2', 'rwkv_a2', 'rwkv_g2', 'gdn_conv_w', 'ffn_conv_w']
TRANSPOSED = ('w_in', 'ffn_up')
SHARD_AXIS = {'w_in': 0, 'ffn_up': 0, 'ffn_down': 0, 'w_out': 0, 'rwkv_proj': 1, 'gdn_proj': 1,
              'rwkv_w2': 1, 'rwkv_a2': 1, 'rwkv_g2': 1, 'gdn_conv_w': 1, 'ffn_conv_w': 1}
REPLICATED = [n for n in WEIGHT_NAMES if n not in SHARD_AXIS]
ROW_SHARDED = ['w_in', 'ffn_up', 'ffn_down', 'w_out']


def _cparams(sem=None):
    kw = dict(vmem_limit_bytes=VMEM_LIMIT)
    if sem is not None:
        kw['dimension_semantics'] = sem
    return pltpu.CompilerParams(**kw)


_NN, _NT, _TN = 'nn', 'nt', 'tn'
_DIMS_2D = {'nn': (((1,), (0,)), ((), ())), 'nt': (((1,), (1,)), ((), ())), 'tn': (((0,), (0,)), ((), ()))}
_DIMS_3D = {'nn': (((2,), (1,)), ((0,), (0,))), 'nt': (((2,), (2,)), ((0,), (0,))), 'tn': (((1,), (1,)), ((0,), (0,)))}


def _dg(a, b, kind):
    return lax.dot_general(a, b, (_DIMS_2D if a.ndim == 2 else _DIMS_3D)[kind], preferred_element_type=F32)


def _dot1(a, b, kind):
    return _dg(a.astype(BF16), b.astype(BF16), kind)


@jax.custom_vjp
def _dhi(a, b):
    return _dot1(a, b, _NN)


_dhi.defvjp(lambda a, b: (_dot1(a, b, _NN), (a, b)),
            lambda res, ct: (_dot1(ct, res[1], _NT), _dot1(res[0], ct, _TN)))


@jax.custom_vjp
def _dnt(a, b):
    return _dot1(a, b, _NT)


_dnt.defvjp(lambda a, b: (_dot1(a, b, _NT), (a, b)),
            lambda res, ct: (_dot1(ct, res[1], _NN), _dot1(ct, res[0], _TN)))


@jax.custom_vjp
def _dtn(a, b):
    return _dot1(a, b, _TN)


_dtn.defvjp(lambda a, b: (_dot1(a, b, _TN), (a, b)),
            lambda res, ct: (_dot1(res[1], ct, _NT), _dot1(res[0], ct, _NN)))


def _split3(x):
    x1 = x.astype(BF16)
    r1 = x - x1.astype(F32)
    x2 = r1.astype(BF16)
    return x1, x2, (r1 - x2.astype(F32)).astype(BF16)


def _dot_exact_lhs(sel, x, kind):
    parts = [_dg(sel, xi, kind) for xi in _split3(x)]
    return parts[0] + parts[1] + parts[2]


def _tril_ones(like):
    c = like.shape[-2]
    ri, ci = _iotas(c)
    return jnp.broadcast_to((ri >= ci).astype(BF16), like.shape[:-2] + (c, c))


@jax.custom_vjp
def _cumsum_rows(x):
    return _dot_exact_lhs(_tril_ones(x), x, _NN)


_cumsum_rows.defvjp(lambda x: (_dot_exact_lhs(_tril_ones(x), x, _NN), None),
                    lambda _, ct: (_dot_exact_lhs(_tril_ones(ct), ct, _TN),))


@jax.custom_vjp
def _lane_sum_as_row(x):
    return _dot_exact_lhs(jnp.ones(x.shape, BF16), x, _NT)


def _lane_sum_as_row_bwd(_, ct):
    ones = jnp.ones(ct.shape[:-1] + (LANES,), BF16)
    parts = [_dg(ci, ones, _TN) for ci in _split3(ct)]
    return (parts[0] + parts[1] + parts[2],)


_lane_sum_as_row.defvjp(lambda x: (_dot_exact_lhs(jnp.ones(x.shape, BF16), x, _NT), None), _lane_sum_as_row_bwd)


def _shift_rows(x, halo, s):
    rows = lax.broadcasted_iota(jnp.int32, x.shape, 0)
    out = pltpu.roll(x, s, 0)
    for i in range(s):
        out = jnp.where(rows == i, halo[SUBLANES - s + i:SUBLANES - s + i + 1, :], out)
    return out


def _unshift_rows(g, carry, s):
    c = g.shape[0]
    rows = lax.broadcasted_iota(jnp.int32, g.shape, 0)
    out = pltpu.roll(g, c - s, 0)
    for i in range(s):
        out = jnp.where(rows == c - s + i, carry[i:i + 1, :], out)
    return out


def _sigmoid(z):
    return 1.0 / (1.0 + jnp.exp(-z))


def _silu(z):
    return z * _sigmoid(z)


def _softplus(z):
    return jnp.maximum(z, 0.0) + jnp.log(1.0 + jnp.exp(-jnp.abs(z)))


def _rms(t, gain):
    return t * lax.rsqrt(jnp.mean(t * t, axis=-1, keepdims=True) + NORM_EPS) * gain


def _iotas(c):
    return lax.broadcasted_iota(jnp.int32, (c, c), 0), lax.broadcasted_iota(jnp.int32, (c, c), 1)


def _unit_lower_inverse(xm, eye):
    t = eye + xm
    p = xm
    for _ in range(5):
        p = _dhi(p, p)
        t = t + _dhi(t, p)
    return t


def _rwkv_head(pr, pk, pv, plo, qr, qk, qv, qlo, s0, pp, mulo, wl):
    c = pr.shape[1]
    n_heads = s0.shape[0]
    n_chunks = pr.shape[0] // n_heads
    ri, ci = _iotas(c)
    if n_chunks > 1:
        pp = jnp.concatenate([pp] * n_chunks, axis=0)
        wl = jnp.concatenate([wl] * n_chunks, axis=0)

    def mix(p, q, mu):
        return p + (q - p) * mu

    r = mix(pr, qr, pp[:, 0:1])
    k = mix(pk, qk, pp[:, 1:2])
    v = mix(pv, qv, pp[:, 2:3])
    lo = mix(plo, qlo, mulo)
    w0, a0, k_k, k_a, ln_w, ln_b, r_k = (pp[:, i:i + 1] for i in range(3, 10))

    def per_head(t):
        return jnp.concatenate([jnp.broadcast_to(t[i], (n_heads,) + t.shape[1:]) for i in range(n_chunks)], axis=0)

    zw = _dhi(per_head(jnp.tanh(lo)), wl[:, 0])
    za = _dhi(per_head(lo), wl[:, 1])
    g = _dhi(per_head(_sigmoid(lo)), wl[:, 2])
    w_log = -_softplus(-(w0 + zw)) - 0.5
    lw = -jnp.exp(w_log)
    a = _sigmoid(a0 + za)
    kk = k * k_k
    kk = kk * lax.rsqrt(jnp.sum(kk * kk, axis=-1, keepdims=True) + L2_EPS)
    k2 = k * (1.0 + (a - 1.0) * k_a)
    an = -kk
    b = kk * a
    causal = ri >= ci
    strict = ri > ci
    eye = (ri == ci).astype(F32)
    cl = _cumsum_rows(lw)
    ecl = jnp.exp(-cl)
    at = an * jnp.exp(cl - lw)
    bt = b * ecl
    kt = k2 * ecl
    rt = r * jnp.exp(cl)
    a_ab = jnp.where(strict, _dnt(at, bt), 0.0)
    a_ak = jnp.where(strict, _dnt(at, kt), 0.0)
    tinv = _unit_lower_inverse(a_ab, eye)
    akv = _dhi(a_ak, v)
    r_b = jnp.where(causal, _dnt(rt, bt), 0.0)
    rkv = _dhi(jnp.where(causal, _dnt(rt, kt), 0.0), v)
    cl_end = jnp.sum(lw, axis=1, keepdims=True)
    dec_end = jnp.exp(cl_end - cl)
    b_end = b * dec_end
    sv = _dtn(v, k2 * dec_end)
    e_end = jnp.exp(cl_end)
    state, ys = s0, []
    for i in range(n_chunks):
        sl = slice(i * n_heads, (i + 1) * n_heads)
        u = _dhi(tinv[sl], _dnt(at[sl], state) + akv[sl])
        ys.append(_dnt(rt[sl], state) + _dhi(r_b[sl], u) + rkv[sl])
        state = state * e_end[sl] + _dtn(u, b_end[sl]) + sv[sl]
    y = jnp.concatenate(ys, axis=0) if n_chunks > 1 else ys[0]
    s1 = state
    m = (lax.broadcasted_iota(jnp.int32, (1, LANES), 1) < RWKV_HEAD_DIM).astype(F32)
    mean = jnp.sum(y, axis=-1, keepdims=True) * (1.0 / RWKV_HEAD_DIM)
    yc = (y - mean) * m
    var = jnp.sum(yc * yc, axis=-1, keepdims=True) * (1.0 / RWKV_HEAD_DIM)
    yn = yc * lax.rsqrt(var + RWKV_GN_EPS) * ln_w + ln_b
    y2 = yn + jnp.sum(r * k2 * r_k, axis=-1, keepdims=True) * v
    return y2 * g, s1


def _gdn_head(xq, xk, xv, z, ab, s0, cw, gp, oha, ohb):
    c = z.shape[1]
    n_heads = s0.shape[0]
    n_chunks = z.shape[0] // n_heads
    ri, ci = _iotas(c)
    cw = jnp.concatenate([cw] * n_chunks, axis=0) if n_chunks > 1 else cw

    def conv(xs, w):
        out = xs[0] * w[:, GDN_CONV - 1:GDN_CONV]
        for s in range(1, GDN_CONV):
            out = out + xs[s] * w[:, GDN_CONV - 1 - s:GDN_CONV - s]
        return out

    q = _silu(conv(xq, cw[:, 0]))
    k = _silu(conv(xk, cw[:, 1]))
    v = _silu(conv(xv, cw[:, 2]))
    q = q * lax.rsqrt(jnp.sum(q * q, axis=-1, keepdims=True) + L2_EPS) * (GDN_HEAD_DIM ** -0.5)
    k = k * lax.rsqrt(jnp.sum(k * k, axis=-1, keepdims=True) + L2_EPS)
    gg = -jnp.exp(gp[0:1]) * _softplus(ab + gp[1:2])
    beta = jnp.sum(_sigmoid(ab) * ohb, axis=-1, keepdims=True)
    causal = ri >= ci
    strict = ri > ci
    eye = (ri == ci).astype(F32)
    gcm = _cumsum_rows(gg * oha)
    gc = jnp.sum(gcm, axis=-1, keepdims=True)
    gc_row = _lane_sum_as_row(gcm)
    dec = jnp.where(causal, jnp.exp(jnp.where(causal, gc - gc_row, 0.0)), 0.0)
    kb = k * beta
    vb = v * beta
    lm = jnp.where(strict, _dnt(kb, k) * dec, 0.0)
    tinv = _unit_lower_inverse(-lm, eye)
    egc = jnp.exp(gc)
    u = _dhi(tinv, vb)
    wk = _dhi(tinv, kb * egc)
    attn = jnp.where(causal, _dnt(q, k) * dec, 0.0)
    g_last = gc[:, c - 1:c, :]
    q_dec = q * egc
    k_dec = k * jnp.exp(g_last - gc)
    e_last = jnp.exp(g_last)
    state, outs = s0, []
    for i in range(n_chunks):
        sl = slice(i * n_heads, (i + 1) * n_heads)
        v_new = u[sl] - _dhi(wk[sl], state)
        outs.append(_dhi(q_dec[sl], state) + _dhi(attn[sl], v_new))
        state = state * e_last[sl] + _dtn(k_dec[sl], v_new)
    o = jnp.concatenate(outs, axis=0) if n_chunks > 1 else outs[0]
    return _rms(o, gp[2:3]) * _silu(z), state


def _head_id(grp, i, per_step, heads):
    return i if per_step == heads else grp * per_step + i


def _head_range(grp, per_step, heads):
    return slice(None) if per_step == heads else pl.ds(grp * per_step, per_step)


def _rwkv_specs(nmap):
    hb, groups = RWKV_HB, RWKV_HEADS // RWKV_HB
    cb = OFF_RKV // (hb * LANES)
    specs = []
    for j in range(3):
        specs.append(pl.BlockSpec((RWKV_TILE, hb * LANES), lambda n, g, j=j: (nmap(n), cb + j * groups + g)))
    specs.append(pl.BlockSpec((RWKV_TILE, 2 * LANES), lambda n, g: (nmap(n), OFF_LO // (2 * LANES))))
    per = RWKV_TILE // SUBLANES
    for j in range(3):
        specs.append(pl.BlockSpec((SUBLANES, hb * LANES),
                                  lambda n, g, j=j: (jnp.maximum(nmap(n) * per - 1, 0), cb + j * groups + g)))
    specs.append(pl.BlockSpec((SUBLANES, 2 * LANES), lambda n, g: (jnp.maximum(nmap(n) * per - 1, 0), OFF_LO // (2 * LANES))))
    specs.append(pl.BlockSpec((hb, 16, LANES), lambda n, g: (g, 0, 0)))
    specs.append(pl.BlockSpec((1, 2 * LANES), lambda n, g: (0, 0)))
    specs.append(pl.BlockSpec((hb, 3, 2 * LANES, LANES), lambda n, g: (g, 0, 0, 0)))
    return specs


def _rwkv_operands(refs, halos, live):
    pr, pk, pv, plo = refs
    hr, hk, hv, hlo = halos
    cur, prev = [], []
    for x, hx in ((pr, hr), (pk, hk), (pv, hv)):
        tiles = [_lane_block(x, h) for h in range(RWKV_HB)]
        cur.append(_chunk_batch(tiles))
        prev.append(_chunk_batch([_shift_rows(t_, _lane_block(hx, h) * live, 1) for h, t_ in enumerate(tiles)]))
    lo = plo[...]
    cur.append(_chunk_batch([lo]))
    prev.append(_chunk_batch([_shift_rows(lo, hlo[...] * live, 1)]))
    return cur, prev


def _rwkv_forward(p_cat, ppack, mulo, wl):
    t = p_cat.shape[0]
    n_chunks = t // RWKV_TILE

    def body(pr, pk, pv, plo, hr, hk, hv, hlo, pp, ml, w, out, st_out, s_scr):
        n, grp = pl.program_id(0), pl.program_id(1)

        hsl = _head_range(grp, RWKV_HB, RWKV_HEADS)

        @pl.when(n == 0)
        def _():
            s_scr[hsl] = jnp.zeros((RWKV_HB, LANES, LANES), F32)

        live = (n > 0).astype(F32)
        cur, prev = _rwkv_operands((pr, pk, pv, plo), (hr, hk, hv, hlo), live)
        s0 = s_scr[hsl]
        st_out[...] = s0
        o, s1 = _rwkv_head(*cur, *prev, s0, pp[...], ml[...], w[...])
        for h, tile in enumerate(_head_tiles(o, RWKV_HB)):
            out[:, h * LANES:(h + 1) * LANES] = tile.astype(out.dtype)
        s_scr[hsl] = s1

    return pl.pallas_call(
        body, name="rwkv_fwd", grid=(n_chunks, RWKV_HEADS // RWKV_HB),
        in_specs=_rwkv_specs(lambda n: n),
        out_specs=(pl.BlockSpec((RWKV_TILE, RWKV_HB * LANES), lambda n, g: (n, g)),
                   pl.BlockSpec((None, RWKV_HB, LANES, LANES), lambda n, g: (n, g, 0, 0))),
        out_shape=(jax.ShapeDtypeStruct((t, RWKV_HEADS * LANES), BF16),
                   jax.ShapeDtypeStruct((n_chunks, RWKV_HEADS, LANES, LANES), F32)),
        scratch_shapes=[pltpu.VMEM((RWKV_HEADS, LANES, LANES), F32)],
        compiler_params=_cparams(("arbitrary", "arbitrary")),
    )(p_cat, p_cat, p_cat, p_cat, p_cat, p_cat, p_cat, p_cat, ppack, mulo, wl)


def _rwkv_backward(p_cat, ppack, mulo, wl, states, d_out):
    t = p_cat.shape[0]
    n_chunks = t // RWKV_TILE
    last = n_chunks - 1

    def body(pr, pk, pv, plo, hr, hk, hv, hlo, pp, ml, w, st, dy, dpr, dpk, dpv, dplo, dpp, dml, dw, ds_scr, car_scr, carlo_scr):
        n, grp = pl.program_id(0), pl.program_id(1)

        hsl = _head_range(grp, RWKV_HB, RWKV_HEADS)
        gi = _head_id(grp, 0, 1, RWKV_HEADS // RWKV_HB)

        @pl.when(n == 0)
        def _():
            ds_scr[hsl] = jnp.zeros((RWKV_HB, LANES, LANES), F32)
            car_scr[hsl] = jnp.zeros((RWKV_HB, 3 * SUBLANES, LANES), F32)
            carlo_scr[gi] = jnp.zeros((SUBLANES, 2 * LANES), F32)

        @pl.when((n == 0) & (grp == 0))
        def _():
            dpp[...] = jnp.zeros(dpp.shape, F32)
            dml[...] = jnp.zeros(dml.shape, F32)
            dw[...] = jnp.zeros(dw.shape, F32)

        live = (n < last).astype(F32)
        cur, prev = _rwkv_operands((pr, pk, pv, plo), (hr, hk, hv, hlo), live)
        _, vjp = jax.vjp(_rwkv_head, *cur, *prev, st[...], pp[...], ml[...], w[...])
        g = vjp((_chunk_batch([_lane_block(dy, h) for h in range(RWKV_HB)]), ds_scr[hsl]))
        outs = (dpr, dpk, dpv)
        d_cur = [_head_tiles(g[j], RWKV_HB) for j in range(3)]
        d_prev = [_head_tiles(g[4 + j], RWKV_HB) for j in range(3)]
        for i in range(RWKV_HB):
            sl = slice(i * LANES, (i + 1) * LANES)
            h = _head_id(grp, i, RWKV_HB, RWKV_HEADS)
            car = car_scr[h]
            for j in range(3):
                tot = d_cur[j][i] + _unshift_rows(d_prev[j][i], car[SUBLANES * j:SUBLANES * (j + 1), :], 1)
                outs[j][:, sl] = tot.astype(outs[j].dtype)
                car_scr[h, SUBLANES * j:SUBLANES * (j + 1), :] = d_prev[j][i][0:SUBLANES, :]
        (dlo_cur,), (dlo_prev,) = _head_tiles(g[3], 1), _head_tiles(g[7], 1)
        dlo = dlo_cur + _unshift_rows(dlo_prev, carlo_scr[gi], 1)
        carlo_scr[gi] = dlo_prev[0:SUBLANES, :]
        ds_scr[hsl] = g[8]
        dpp[hsl] += g[9]
        dml[0, 0:1, :] += g[10]
        dw[hsl] += g[11]

        @pl.when(grp == 0)
        def _():
            dplo[...] = dlo

        @pl.when(grp > 0)
        def _():
            dplo[...] += dlo

    rev = lambda n: last - n
    in_specs = _rwkv_specs(rev) + [
        pl.BlockSpec((None, RWKV_HB, LANES, LANES), lambda n, g: (rev(n), g, 0, 0)),
        pl.BlockSpec((RWKV_TILE, RWKV_HB * LANES), lambda n, g: (rev(n), g)),
    ]
    hw = RWKV_HEADS * LANES
    return pl.pallas_call(
        body, name="rwkv_bwd", grid=(n_chunks, RWKV_HEADS // RWKV_HB),
        in_specs=in_specs,
        out_specs=(pl.BlockSpec((RWKV_TILE, RWKV_HB * LANES), lambda n, g: (rev(n), g)),
                   pl.BlockSpec((RWKV_TILE, RWKV_HB * LANES), lambda n, g: (rev(n), g)),
                   pl.BlockSpec((RWKV_TILE, RWKV_HB * LANES), lambda n, g: (rev(n), g)),
                   pl.BlockSpec((RWKV_TILE, 2 * LANES), lambda n, h: (rev(n), 0)),
                   pl.BlockSpec((RWKV_HEADS, 16, LANES), lambda n, h: (0, 0, 0)),
                   pl.BlockSpec((RWKV_HEADS, SUBLANES, 2 * LANES), lambda n, h: (0, 0, 0)),
                   pl.BlockSpec((RWKV_HEADS, 3, 2 * LANES, LANES), lambda n, h: (0, 0, 0, 0))),
        out_shape=(jax.ShapeDtypeStruct((t, hw), BF16), jax.ShapeDtypeStruct((t, hw), BF16), jax.ShapeDtypeStruct((t, hw), BF16),
                   jax.ShapeDtypeStruct((t, 2 * LANES), F32),
                   jax.ShapeDtypeStruct((RWKV_HEADS, 16, LANES), F32),
                   jax.ShapeDtypeStruct((RWKV_HEADS, SUBLANES, 2 * LANES), F32),
                   jax.ShapeDtypeStruct((RWKV_HEADS, 3, 2 * LANES, LANES), F32)),
        scratch_shapes=[pltpu.VMEM((RWKV_HEADS, LANES, LANES), F32),
                        pltpu.VMEM((RWKV_HEADS, 3 * SUBLANES, LANES), F32),
                        pltpu.VMEM((RWKV_HEADS, SUBLANES, 2 * LANES), F32)],
        compiler_params=_cparams(("arbitrary", "arbitrary")),
    )(p_cat, p_cat, p_cat, p_cat, p_cat, p_cat, p_cat, p_cat, ppack, mulo, wl, states, d_out)


def _gdn_specs(nmap):
    per = GDN_TILE // SUBLANES
    hb, groups = GDN_HB, GDN_HEADS // GDN_HB
    cb = OFF_QKV // (hb * LANES)
    specs = []
    for j in range(3):
        specs.append(pl.BlockSpec((GDN_TILE, hb * LANES), lambda n, g, j=j: (nmap(n), cb + j * groups + g)))
    for j in range(3):
        specs.append(pl.BlockSpec((SUBLANES, hb * LANES),
                                  lambda n, g, j=j: (jnp.maximum(nmap(n) * per - 1, 0), cb + j * groups + g)))
    specs.append(pl.BlockSpec((GDN_TILE, hb * LANES), lambda n, g: (nmap(n), OFF_Z // (hb * LANES) + g)))
    specs.append(pl.BlockSpec((GDN_TILE, LANES), lambda n, g: (nmap(n), OFF_AB // LANES)))
    specs.append(pl.BlockSpec((hb, 3, SUBLANES, LANES), lambda n, g: (g, 0, 0, 0)))
    specs.append(pl.BlockSpec((SUBLANES, LANES), lambda n, g: (0, 0)))
    return specs


def _conv_taps(x, halo):
    return (x,) + tuple(_shift_rows(x, halo, s) for s in range(1, GDN_CONV))


def _onehots(grp):
    nb = GDN_STEP_CHUNKS * GDN_HB
    lane = lax.broadcasted_iota(jnp.int32, (nb, 1, LANES), 2)
    head = lax.broadcasted_iota(jnp.int32, (nb, 1, LANES), 0) % GDN_HB + _head_id(grp, 0, GDN_HB, GDN_HEADS)
    return (lane == head).astype(F32), (lane == GDN_HEADS + head).astype(F32)


def _chunk_batch(tiles):
    n_chunks = tiles[0].shape[0] // CHUNK
    return jnp.stack([t_[i * CHUNK:(i + 1) * CHUNK, :] for i in range(n_chunks) for t_ in tiles])


def _head_tiles(batch, n_heads=GDN_HB):
    n_chunks = batch.shape[0] // n_heads
    return [jnp.concatenate([batch[i * n_heads + h] for i in range(n_chunks)], axis=0) for h in range(n_heads)]


def _lane_block(ref, h):
    return ref[:, h * LANES:(h + 1) * LANES]


def _gdn_taps(refs, halos, live):
    out = []
    for x, hx in zip(refs, halos):
        per_head = [_conv_taps(_lane_block(x, h), _lane_block(hx, h) * live) for h in range(GDN_HB)]
        out.append(tuple(_chunk_batch([per_head[h][s] for h in range(GDN_HB)]) for s in range(GDN_CONV)))
    return out


def _gdn_forward(p_cat, cwpack, gpar):
    t = p_cat.shape[0]
    n_chunks = t // GDN_TILE

    def body(xq, xk, xv, hq, hk, hv, z, ab, cw, gp, out, st_out, s_scr):
        n, grp = pl.program_id(0), pl.program_id(1)

        hsl = _head_range(grp, GDN_HB, GDN_HEADS)

        @pl.when(n == 0)
        def _():
            s_scr[hsl] = jnp.zeros((GDN_HB, LANES, LANES), F32)

        live = (n > 0).astype(F32)
        oha, ohb = _onehots(grp)
        s0 = s_scr[hsl]
        st_out[...] = s0
        taps = _gdn_taps((xq, xk, xv), (hq, hk, hv), live)
        zb = _chunk_batch([_lane_block(z, h) for h in range(GDN_HB)])
        abb = _chunk_batch([ab[...]] * GDN_HB)
        o, s1 = _gdn_head(*taps, zb, abb, s0, cw[...], gp[...], oha, ohb)
        for h, tile in enumerate(_head_tiles(o)):
            out[:, h * LANES:(h + 1) * LANES] = tile.astype(out.dtype)
        s_scr[hsl] = s1

    return pl.pallas_call(
        body, name="gdn_fwd", grid=(n_chunks, GDN_HEADS // GDN_HB),
        in_specs=_gdn_specs(lambda n: n),
        out_specs=(pl.BlockSpec((GDN_TILE, GDN_HB * LANES), lambda n, g: (n, g)),
                   pl.BlockSpec((None, GDN_HB, LANES, LANES), lambda n, g: (n, g, 0, 0))),
        out_shape=(jax.ShapeDtypeStruct((t, GDN_WIDTH), BF16),
                   jax.ShapeDtypeStruct((n_chunks, GDN_HEADS, LANES, LANES), F32)),
        scratch_shapes=[pltpu.VMEM((GDN_HEADS, LANES, LANES), F32)],
        compiler_params=_cparams(("arbitrary", "arbitrary")),
    )(p_cat, p_cat, p_cat, p_cat, p_cat, p_cat, p_cat, p_cat, cwpack, gpar)


def _gdn_backward(p_cat, cwpack, gpar, states, d_out):
    t = p_cat.shape[0]
    n_chunks = t // GDN_TILE
    last = n_chunks - 1

    def body(xq, xk, xv, hq, hk, hv, z, ab, cw, gp, st, dy, dq, dk, dv, dz, dab, dcw, dgp, ds_scr, car_scr):
        n, grp = pl.program_id(0), pl.program_id(1)

        hsl = _head_range(grp, GDN_HB, GDN_HEADS)

        @pl.when(n == 0)
        def _():
            ds_scr[hsl] = jnp.zeros((GDN_HB, LANES, LANES), F32)
            car_scr[hsl] = jnp.zeros((GDN_HB, 3 * GDN_CONV, SUBLANES, LANES), F32)

        @pl.when((n == 0) & (grp == 0))
        def _():
            dcw[...] = jnp.zeros(dcw.shape, F32)
            dgp[...] = jnp.zeros(dgp.shape, F32)

        live = (n < last).astype(F32)
        oha, ohb = _onehots(grp)
        fn = functools.partial(_gdn_head, oha=oha, ohb=ohb)
        taps = _gdn_taps((xq, xk, xv), (hq, hk, hv), live)
        zb = _chunk_batch([_lane_block(z, h) for h in range(GDN_HB)])
        abb = _chunk_batch([ab[...]] * GDN_HB)
        _, vjp = jax.vjp(fn, *taps, zb, abb, st[...], cw[...], gp[...])
        g = vjp((_chunk_batch([_lane_block(dy, h) for h in range(GDN_HB)]), ds_scr[hsl]))
        outs = (dq, dk, dv)
        tap_tiles = [[_head_tiles(g[j][s]) for s in range(GDN_CONV)] for j in range(3)]
        dz_tiles = _head_tiles(g[3])
        for i in range(GDN_HB):
            sl = slice(i * LANES, (i + 1) * LANES)
            h = _head_id(grp, i, GDN_HB, GDN_HEADS)
            for j in range(3):
                tot = tap_tiles[j][0][i]
                for s in range(1, GDN_CONV):
                    slot = j * GDN_CONV + s
                    tot = tot + _unshift_rows(tap_tiles[j][s][i], car_scr[h, slot], s)
                    car_scr[h, slot] = tap_tiles[j][s][i][0:SUBLANES, :]
                outs[j][:, sl] = tot.astype(outs[j].dtype)
            dz[:, sl] = dz_tiles[i].astype(dz.dtype)
        dab_tiles = _head_tiles(g[4])
        dab_sum = dab_tiles[0]
        for h in range(1, GDN_HB):
            dab_sum = dab_sum + dab_tiles[h]
        ds_scr[hsl] = g[5]
        dcw[hsl] += g[6]
        dgp[0] += g[7]

        @pl.when(grp == 0)
        def _():
            dab[...] = dab_sum

        @pl.when(grp > 0)
        def _():
            dab[...] += dab_sum

    rev = lambda n: last - n
    in_specs = _gdn_specs(rev) + [
        pl.BlockSpec((None, GDN_HB, LANES, LANES), lambda n, g: (rev(n), g, 0, 0)),
        pl.BlockSpec((GDN_TILE, GDN_HB * LANES), lambda n, g: (rev(n), g)),
    ]
    blk = pl.BlockSpec((GDN_TILE, GDN_HB * LANES), lambda n, g: (rev(n), g))
    return pl.pallas_call(
        body, name="gdn_bwd", grid=(n_chunks, GDN_HEADS // GDN_HB),
        in_specs=in_specs,
        out_specs=(blk, blk, blk, blk,
                   pl.BlockSpec((GDN_TILE, LANES), lambda n, h: (rev(n), 0)),
                   pl.BlockSpec((GDN_HEADS, 3, SUBLANES, LANES), lambda n, h: (0, 0, 0, 0)),
                   pl.BlockSpec((GDN_HEADS, SUBLANES, LANES), lambda n, h: (0, 0, 0))),
        out_shape=(jax.ShapeDtypeStruct((t, GDN_WIDTH), BF16), jax.ShapeDtypeStruct((t, GDN_WIDTH), BF16),
                   jax.ShapeDtypeStruct((t, GDN_WIDTH), BF16), jax.ShapeDtypeStruct((t, GDN_WIDTH), BF16),
                   jax.ShapeDtypeStruct((t, LANES), F32),
                   jax.ShapeDtypeStruct((GDN_HEADS, 3, SUBLANES, LANES), F32),
                   jax.ShapeDtypeStruct((GDN_HEADS, SUBLANES, LANES), F32)),
        scratch_shapes=[pltpu.VMEM((GDN_HEADS, LANES, LANES), F32),
                        pltpu.VMEM((GDN_HEADS, 3 * GDN_CONV, SUBLANES, LANES), F32)],
        compiler_params=_cparams(("arbitrary", "arbitrary")),
    )(p_cat, p_cat, p_cat, p_cat, p_cat, p_cat, p_cat, p_cat, cwpack, gpar, states, d_out)


MM_VMEM_BUDGET = 44 * 1024 * 1024
MM_MIN_STEPS = 4


def _mm_tiles(mode, m, n, k, out_bytes):
    tms = [t for t in (2048, 1024, 768, 512, 256, 128, 64) if m % t == 0 and (mode != 'tn' or t % LANES == 0)]
    tns = [t for t in (1408, 1024, 768, 512, 256, 128) if n % t == 0]
    tks = [t for t in (2048, 1920, 1408, 1024, 512, 256, 128, 64) if k % t == 0]
    best, best_key = None, None
    for tm in tms:
        for tn in tns:
            for tk in tks:
                nk = k // tk
                vmem = 2 * (tm * tk * 2 + tk * tn * 2 + tm * tn * out_bytes) + (tm * tn * 4 if nk > 1 else 0)
                steps = (m // tm) * (n // tn) * nk
                if vmem > MM_VMEM_BUDGET:
                    continue
                key = (steps >= MM_MIN_STEPS, tn if mode == 'tn' else 0, tm * tn * tk, -nk)
                if best_key is None or key > best_key:
                    best, best_key = (tm, tn, tk), key
    if best is None:
        raise ValueError(f"no matmul tile for {mode} {m}x{n}x{k}")
    return best


_MM_DIMS = {'nn': (((1,), (0,)), ((), ())), 'nt': (((1,), (1,)), ((), ())), 'tn': (((0,), (0,)), ((), ()))}


def _matmul(a, b, mode, out_dtype, name):
    if mode == 'nn':
        (m, k), (k2, n) = a.shape, b.shape
    elif mode == 'nt':
        (m, k), (n, k2) = a.shape, b.shape
    else:
        (k, m), (k2, n) = a.shape, b.shape
    assert k == k2, (a.shape, b.shape, mode)
    tm, tn, tk = _mm_tiles(mode, m, n, k, jnp.dtype(out_dtype).itemsize)
    nk = k // tk
    dims = _MM_DIMS[mode]

    def body(a_ref, b_ref, o_ref, acc_ref):
        kk = pl.program_id(2)
        part = lax.dot_general(a_ref[...], b_ref[...], dims, preferred_element_type=F32)
        if nk == 1:
            o_ref[...] = part.astype(o_ref.dtype)
            return

        @pl.when(kk == 0)
        def _():
            acc_ref[...] = part

        @pl.when((kk > 0) & (kk < nk - 1))
        def _():
            acc_ref[...] += part

        @pl.when(kk == nk - 1)
        def _():
            o_ref[...] = (acc_ref[...] + part).astype(o_ref.dtype)

    a_spec = pl.BlockSpec((tk, tm), lambda i, j, kk: (kk, i)) if mode == 'tn' else pl.BlockSpec((tm, tk), lambda i, j, kk: (i, kk))
    b_spec = pl.BlockSpec((tn, tk), lambda i, j, kk: (j, kk)) if mode == 'nt' else pl.BlockSpec((tk, tn), lambda i, j, kk: (kk, j))
    return pl.pallas_call(
        body, name=name, grid=(m // tm, n // tn, nk),
        in_specs=[a_spec, b_spec],
        out_specs=pl.BlockSpec((tm, tn), lambda i, j, kk: (i, j)),
        out_shape=jax.ShapeDtypeStruct((m, n), out_dtype),
        scratch_shapes=[pltpu.VMEM((tm, tn), F32)],
        compiler_params=_cparams(("parallel", "parallel", "arbitrary")),
    )(a, b)


ROW_TILE = 256


def _row_specs(rows, tm):
    return [pl.BlockSpec((tm, w), lambda i, ci=ci: (i, ci)) for (_, w, ci) in rows]


def _rw_forward(fn, rows, pars, outs, name):
    t = rows[0][0].shape[0]
    tm = min(ROW_TILE, t)
    nr, npar = len(rows), len(pars)

    def body(*refs):
        vals = [r[...].astype(F32) for r in refs[:nr]] + [p[...] for p in refs[nr:nr + npar]]
        res = fn(*vals)
        for o, v in zip(refs[nr + npar:], res):
            o[...] = v.astype(o.dtype)

    return pl.pallas_call(
        body, name=name, grid=(t // tm,),
        in_specs=_row_specs(rows, tm) + [pl.BlockSpec(p.shape, lambda i: (0, 0)) for p in pars],
        out_specs=tuple(pl.BlockSpec((tm, w), lambda i: (i, 0)) for (w, _) in outs),
        out_shape=tuple(jax.ShapeDtypeStruct((t, w), dt) for (w, dt) in outs),
        compiler_params=_cparams(("parallel",)),
    )(*[r[0] for r in rows], *pars)


def _rw_backward(fn, rows, pars, cots, drow_dtypes, name):
    t = rows[0][0].shape[0]
    tm = min(ROW_TILE, t)
    nr, npar, nc = len(rows), len(pars), len(cots)
    keep = [i for i, dt in enumerate(drow_dtypes) if dt is not None]

    def body(*refs):
        vals = [r[...].astype(F32) for r in refs[:nr]] + [p[...] for p in refs[nr:nr + npar]]
        cvals = tuple(c[...].astype(F32) for c in refs[nr + npar:nr + npar + nc])
        orefs = refs[nr + npar + nc:]
        _, vjp = jax.vjp(fn, *vals)
        g = vjp(cvals)
        for o, i in zip(orefs[:len(keep)], keep):
            o[...] = g[i].astype(o.dtype)
        first = pl.program_id(0) == 0
        for o, gi in zip(orefs[len(keep):], g[nr:]):
            @pl.when(first)
            def _(o=o, gi=gi):
                o[...] = gi

            @pl.when(jnp.logical_not(first))
            def _(o=o, gi=gi):
                o[...] += gi

    out_specs = [pl.BlockSpec((tm, rows[i][1]), lambda i_: (i_, 0)) for i in keep] + \
                [pl.BlockSpec(p.shape, lambda i_: (0, 0)) for p in pars]
    out_shape = [jax.ShapeDtypeStruct((t, rows[i][1]), drow_dtypes[i]) for i in keep] + \
                [jax.ShapeDtypeStruct(p.shape, F32) for p in pars]
    return pl.pallas_call(
        body, name=name, grid=(t // tm,),
        in_specs=_row_specs(rows, tm) + [pl.BlockSpec(p.shape, lambda i: (0, 0)) for p in pars] + _row_specs(cots, tm),
        out_specs=tuple(out_specs), out_shape=tuple(out_shape),
        compiler_params=_cparams(("arbitrary",)),
    )(*[r[0] for r in rows], *pars, *[c[0] for c in cots])


def _norm_fn(x, g):
    return (_rms(x, g),)


def _norm_skip_fn(x, g):
    return _rms(x, g), x


def _merge_fn(ga, gb, ya, yb):
    return (_sigmoid(ga) * ya + _sigmoid(gb) * yb,)


def _res_norm_fn(x, mo, g):
    x1 = x + mo
    return x1, _rms(x1, g)


def _loss_head(x1, fo, gf, target, name):
    t, d = x1.shape
    tm = min(ROW_TILE, t)

    def tile_loss(x2, g, tgt):
        err = _rms(x2, g) - tgt
        per_row = jnp.sum(err * err, axis=-1, keepdims=True) * (0.5 / d)
        return jnp.sum(per_row, axis=0, keepdims=True)

    def body(x1_ref, fo_ref, g_ref, t_ref, loss_ref, dx_ref, dxb_ref, dg_ref):
        x2 = x1_ref[...] + fo_ref[...]
        val, vjp = jax.vjp(functools.partial(tile_loss, tgt=t_ref[...]), x2, g_ref[...])
        dx2, dg = vjp(jnp.ones((1, 1), F32))
        dx_ref[...] = dx2
        dxb_ref[...] = dx2.astype(BF16)
        first = pl.program_id(0) == 0

        @pl.when(first)
        def _():
            loss_ref[...] = jnp.broadcast_to(val, loss_ref.shape)
            dg_ref[...] = dg

        @pl.when(jnp.logical_not(first))
        def _():
            loss_ref[...] += jnp.broadcast_to(val, loss_ref.shape)
            dg_ref[...] += dg

    row = pl.BlockSpec((tm, d), lambda i: (i, 0))
    vec = pl.BlockSpec((1, d), lambda i: (0, 0))
    return pl.pallas_call(
        body, name=name, grid=(t // tm,),
        in_specs=[row, row, vec, row],
        out_specs=(pl.BlockSpec((1, LANES), lambda i: (0, 0)), row, row, vec),
        out_shape=(jax.ShapeDtypeStruct((1, LANES), F32), jax.ShapeDtypeStruct((t, d), F32),
                   jax.ShapeDtypeStruct((t, d), BF16), jax.ShapeDtypeStruct((1, d), F32)),
        compiler_params=_cparams(("arbitrary",)),
    )(x1, fo, gf, target)


FFN_TILE_ROWS = 512
FFN_TILE_COLS = 256
FFN_COL_BLOCKS = FFN_HIDDEN // FFN_TILE_COLS


def _conv3_past(x, halo, w):
    rows = lax.broadcasted_iota(jnp.int32, x.shape, 0)
    x1 = jnp.where(rows == 0, halo[7:8, :], pltpu.roll(x, 1, 0))
    x2 = jnp.where(rows == 0, halo[6:7, :], jnp.where(rows == 1, halo[7:8, :], pltpu.roll(x, 2, 0)))
    return x * w[2:3] + x1 * w[1:2] + x2 * w[0:1], x1, x2


def _ffn_in_specs(tm, imap, jmap):
    per = tm // SUBLANES
    tile = lambda off: pl.BlockSpec((tm, FFN_TILE_COLS), lambda *g: (imap(*g), off + jmap(*g) % FFN_COL_BLOCKS))
    halo = lambda off: pl.BlockSpec((SUBLANES, FFN_TILE_COLS),
                                    lambda *g: (jnp.maximum(imap(*g) * per - 1, 0), off + jmap(*g) % FFN_COL_BLOCKS))
    wsp = lambda off: pl.BlockSpec((FFN_CONV, FFN_TILE_COLS), lambda *g: (0, off + jmap(*g) % FFN_COL_BLOCKS))
    return [tile(0), halo(0), wsp(0), tile(FFN_COL_BLOCKS), halo(FFN_COL_BLOCKS), wsp(FFN_COL_BLOCKS)]


def _ffn_act_forward(hpre, cw):
    t = hpre.shape[0]
    tm = min(FFN_TILE_ROWS, t)

    def body(hg, pg, wg, hu, pu, wu, out):
        live = (pl.program_id(0) > 0).astype(F32)
        cg, _, _ = _conv3_past(hg[...], pg[...] * live, wg[...])
        cu, _, _ = _conv3_past(hu[...], pu[...] * live, wu[...])
        out[...] = (_silu(cg) * cu).astype(out.dtype)

    return pl.pallas_call(
        body, name="ffn_act_fwd", grid=(t // tm, FFN_COL_BLOCKS),
        in_specs=_ffn_in_specs(tm, lambda i, j: i, lambda i, j: j),
        out_specs=pl.BlockSpec((tm, FFN_TILE_COLS), lambda i, j: (i, j)),
        out_shape=jax.ShapeDtypeStruct((t, FFN_HIDDEN), BF16),
        compiler_params=_cparams(("parallel", "parallel")),
    )(hpre, hpre, cw, hpre, hpre, cw)


def _conv3_future(d, nxt, w):
    tm = d.shape[0]
    rows = lax.broadcasted_iota(jnp.int32, d.shape, 0)
    d1 = jnp.where(rows == tm - 1, nxt[0:1, :], pltpu.roll(d, tm - 1, 0))
    d2 = jnp.where(rows == tm - 1, nxt[1:2, :], jnp.where(rows == tm - 2, nxt[0:1, :], pltpu.roll(d, tm - 2, 0)))
    return d * w[2:3] + d1 * w[1:2] + d2 * w[0:1]


def _ffn_backward(hpre, cw, dact):
    t = hpre.shape[0]
    tm = min(FFN_TILE_ROWS, t)
    n_tiles = t // tm
    per = tm // SUBLANES

    def d_conv_out(cg, cu, d):
        s = _sigmoid(cg)
        return d * cu * s * (1.0 + cg * (1.0 - s)), d * cg * s

    def body(hg, pg, ng, wg, hu, pu, nu, wu, da, dan, dhg, dhu, dwg, dwu):
        i = pl.program_id(1)
        live_prev = (i > 0).astype(F32)
        live_next = (i < n_tiles - 1).astype(F32)
        xg, xu = hg[...], hu[...]
        cg, g1, g2 = _conv3_past(xg, pg[...] * live_prev, wg[...])
        cu, u1, u2 = _conv3_past(xu, pu[...] * live_prev, wu[...])
        dg, du = d_conv_out(cg, cu, da[...])
        cgn, _, _ = _conv3_past(ng[...], hg[tm - SUBLANES:tm, :], wg[...])
        cun, _, _ = _conv3_past(nu[...], hu[tm - SUBLANES:tm, :], wu[...])
        dgn, dun = d_conv_out(cgn, cun, dan[...] * live_next)
        dhg[...] = _conv3_future(dg, dgn, wg[...]).astype(dhg.dtype)
        dhu[...] = _conv3_future(du, dun, wu[...]).astype(dhu.dtype)
        sums_g = [jnp.sum(xs * dg, axis=0, keepdims=True) for xs in (g2, g1, xg)]
        sums_u = [jnp.sum(xs * du, axis=0, keepdims=True) for xs in (u2, u1, xu)]

        @pl.when(i == 0)
        def _():
            for r_ in range(FFN_CONV):
                dwg[r_:r_ + 1, :] = sums_g[r_]
                dwu[r_:r_ + 1, :] = sums_u[r_]

        @pl.when(i > 0)
        def _():
            for r_ in range(FFN_CONV):
                dwg[r_:r_ + 1, :] += sums_g[r_]
                dwu[r_:r_ + 1, :] += sums_u[r_]

    nb = FFN_COL_BLOCKS
    nxt = lambda i: jnp.minimum((i + 1) * per, t // SUBLANES - 1)
    prv = lambda i: jnp.maximum(i * per - 1, 0)
    half = lambda off: [pl.BlockSpec((tm, FFN_TILE_COLS), lambda j, i: (i, off + j)),
                        pl.BlockSpec((SUBLANES, FFN_TILE_COLS), lambda j, i: (prv(i), off + j)),
                        pl.BlockSpec((SUBLANES, FFN_TILE_COLS), lambda j, i: (nxt(i), off + j)),
                        pl.BlockSpec((FFN_CONV, FFN_TILE_COLS), lambda j, i: (0, off + j))]
    tile = pl.BlockSpec((tm, FFN_TILE_COLS), lambda j, i: (i, j))
    taps = pl.BlockSpec((FFN_CONV, FFN_TILE_COLS), lambda j, i: (0, j))
    return pl.pallas_call(
        body, name="ffn_bwd", grid=(nb, n_tiles),
        in_specs=half(0) + half(nb) + [tile, pl.BlockSpec((SUBLANES, FFN_TILE_COLS), lambda j, i: (nxt(i), j))],
        out_specs=(tile, tile, taps, taps),
        out_shape=(jax.ShapeDtypeStruct((t, FFN_HIDDEN), BF16), jax.ShapeDtypeStruct((t, FFN_HIDDEN), BF16),
                   jax.ShapeDtypeStruct((FFN_CONV, FFN_HIDDEN), F32), jax.ShapeDtypeStruct((FFN_CONV, FFN_HIDDEN), F32)),
        compiler_params=_cparams(("parallel", "arbitrary")),
    )(hpre, hpre, hpre, cw, hpre, hpre, hpre, cw, dact, dact)


def _my_place():
    x, y, c = lax.axis_index("x"), lax.axis_index("y"), lax.axis_index("c")
    return x, y, c, 4 * x + 2 * y + c


N_CHIPS = 4


def _remote(src, dst, send_sem, recv_sem, dev):
    return pltpu.make_async_remote_copy(src_ref=src, dst_ref=dst, send_sem=send_sem, recv_sem=recv_sem, device_id=dev,
                                        device_id_type=pl.DeviceIdType.MESH)


def _chip_peer(x, y, k):
    return x ^ ((k >> 1) & 1), y ^ (k & 1)


def _all_gather_two_level(shard, name):
    r, w = shard.shape

    def body(src, out, send_sems, recv_sems, local_sem):
        x, y, c, me = _my_place()
        sibling = (x, y, 1 - c)
        mine = pltpu.make_async_copy(src, out.at[me], local_sem)
        mine.start()
        first = [_remote(src, out.at[me], send_sems.at[0], recv_sems.at[0], sibling)]
        for k in range(1, N_CHIPS):
            px, py = _chip_peer(x, y, k)
            first.append(_remote(src, out.at[me], send_sems.at[k], recv_sems.at[k], (px, py, c)))
        for cp in first:
            cp.start()
        passed = []
        for k in range(1, N_CHIPS):
            px, py = _chip_peer(x, y, k)
            landed = out.at[me ^ (2 * k)]
            _remote(src, landed, send_sems.at[k], recv_sems.at[k], (px, py, c)).wait_recv()
            fwd = _remote(landed, landed, send_sems.at[N_CHIPS - 1 + k], recv_sems.at[N_CHIPS - 1 + k], sibling)
            fwd.start()
            passed.append(fwd)
        _remote(src, out.at[me ^ 1], send_sems.at[0], recv_sems.at[0], sibling).wait_recv()
        for k in range(1, N_CHIPS):
            got = out.at[(me ^ 1) ^ (2 * k)]
            _remote(got, got, send_sems.at[N_CHIPS - 1 + k], recv_sems.at[N_CHIPS - 1 + k], sibling).wait_recv()
        for cp in first + passed:
            cp.wait_send()
        mine.wait()

    return pl.pallas_call(
        body, name=name,
        in_specs=[pl.BlockSpec(memory_space=pl.ANY)],
        out_specs=pl.BlockSpec(memory_space=pl.ANY),
        out_shape=jax.ShapeDtypeStruct((N_DEV, r, w), shard.dtype),
        scratch_shapes=[pltpu.SemaphoreType.DMA((N_DEV - 1,)), pltpu.SemaphoreType.DMA((N_DEV - 1,)), pltpu.SemaphoreType.DMA],
    )(shard)


def _device_peer(x, y, c, k):
    px, py, pc = x ^ ((k >> 2) & 1), y ^ ((k >> 1) & 1), c ^ (k & 1)
    return (px, py, pc), 4 * px + 2 * py + pc


_HBM = pl.BlockSpec(memory_space=pltpu.HBM)
_SEM = pl.BlockSpec(memory_space=pltpu.SEMAPHORE)


def _gather_start(shard, name):
    def body(src, land, send_sems, recv_sems, src_thru, land_thru, token):
        x, y, c, me = _my_place()
        for k in range(1, N_DEV):
            dev, _ = _device_peer(x, y, c, k)
            _remote(src, land.at[me], send_sems.at[k], recv_sems.at[k], dev).start()
        token[...] = jnp.zeros_like(token)

    landing = lax.empty((N_DEV,) + shard.shape, shard.dtype)
    return pl.pallas_call(
        body, name=name,
        out_shape=(pltpu.SemaphoreType.DMA((N_DEV,)), pltpu.SemaphoreType.DMA((N_DEV,)), pltpu.HBM(shard.shape, shard.dtype),
                   pltpu.HBM(landing.shape, landing.dtype), jax.ShapeDtypeStruct((SUBLANES, LANES), F32)),
        in_specs=(_HBM, _HBM), out_specs=(_SEM, _SEM, _HBM, _HBM, pl.BlockSpec(memory_space=pltpu.VMEM)),
        input_output_aliases={0: 2, 1: 3},
        compiler_params=pltpu.CompilerParams(has_side_effects=pltpu.SideEffectType.DATAFLOW_SIDE_EFFECTING),
    )(pltpu.with_memory_space_constraint(shard, pltpu.HBM), pltpu.with_memory_space_constraint(landing, pltpu.HBM))


def _gather_wait(send_sems, recv_sems, shard, landing, after, name):
    n_after = len(after)

    def body(*refs):
        src, land, send_sems, recv_sems = refs[:4]
        x, y, c, _ = _my_place()
        for k in range(1, N_DEV):
            dev, idx = _device_peer(x, y, c, k)
            cp = _remote(src, land.at[idx], send_sems.at[k], recv_sems.at[k], dev)
            cp.wait_send()
            cp.wait_recv()

    return pl.pallas_call(
        body, name=name,
        out_shape=(pltpu.HBM(shard.shape, shard.dtype), pltpu.HBM(landing.shape, landing.dtype)),
        in_specs=(_HBM, _HBM, _SEM, _SEM) + (pl.BlockSpec(memory_space=pl.ANY),) * n_after, out_specs=(_HBM, _HBM),
        input_output_aliases={0: 0, 1: 1},
        compiler_params=pltpu.CompilerParams(has_side_effects=pltpu.SideEffectType.DATAFLOW_SIDE_EFFECTING),
    )(shard, landing, send_sems, recv_sems, *after)[1]


def _slab_push_start(slabs, name):
    na = len(slabs)

    def body(*refs):
        srcs, lands = refs[:na], refs[na:2 * na]
        send_sems, recv_sems = refs[2 * na], refs[2 * na + 1]
        token = refs[-1]
        x, y, c, me = _my_place()
        for i in range(na):
            for k in range(1, N_DEV):
                dev, idx = _device_peer(x, y, c, k)
                s = i * N_DEV + k
                _remote(srcs[i].at[idx], lands[i].at[me], send_sems.at[s], recv_sems.at[s], dev).start()
        token[...] = jnp.zeros_like(token)

    hbm_shapes = [pltpu.HBM(a.shape, a.dtype) for a in slabs]
    ins = [pltpu.with_memory_space_constraint(a, pltpu.HBM) for a in slabs]
    ins += [pltpu.with_memory_space_constraint(lax.empty(a.shape, a.dtype), pltpu.HBM) for a in slabs]
    out = pl.pallas_call(
        body, name=name,
        out_shape=(pltpu.SemaphoreType.DMA((na * N_DEV,)), pltpu.SemaphoreType.DMA((na * N_DEV,)), *hbm_shapes, *hbm_shapes,
                   jax.ShapeDtypeStruct((SUBLANES, LANES), F32)),
        in_specs=(_HBM,) * (2 * na), out_specs=(_SEM, _SEM) + (_HBM,) * (2 * na) + (pl.BlockSpec(memory_space=pltpu.VMEM),),
        input_output_aliases={i: 2 + i for i in range(2 * na)},
        compiler_params=pltpu.CompilerParams(has_side_effects=pltpu.SideEffectType.DATAFLOW_SIDE_EFFECTING),
    )(*ins)
    return out[0], out[1], list(out[2:2 + na]), list(out[2 + na:2 + 2 * na]), out[-1]


def _slab_push_wait(send_sems, recv_sems, slabs, landings, after, name):
    na = len(slabs)

    def body(*refs):
        srcs, lands = refs[:na], refs[na:2 * na]
        send_sems, recv_sems = refs[2 * na], refs[2 * na + 1]
        x, y, c, me = _my_place()
        for i in range(na):
            for k in range(1, N_DEV):
                dev, idx = _device_peer(x, y, c, k)
                s = i * N_DEV + k
                cp = _remote(srcs[i].at[idx], lands[i].at[idx], send_sems.at[s], recv_sems.at[s], dev)
                cp.wait_send()
                cp.wait_recv()

    hbm_shapes = tuple(pltpu.HBM(a.shape, a.dtype) for a in slabs)
    out = pl.pallas_call(
        body, name=name, out_shape=hbm_shapes + hbm_shapes,
        in_specs=(_HBM,) * (2 * na) + (_SEM, _SEM) + (pl.BlockSpec(memory_space=pl.ANY),) * len(after),
        out_specs=(_HBM,) * (2 * na), input_output_aliases={i: i for i in range(2 * na)},
        compiler_params=pltpu.CompilerParams(has_side_effects=pltpu.SideEffectType.DATAFLOW_SIDE_EFFECTING),
    )(*slabs, *landings, send_sems, recv_sems, *after)
    return list(out[na:])


def _pair_exchange(arrays, name):
    na = len(arrays)

    def body(*refs):
        srcs, dsts, (send_sems, recv_sems) = refs[:na], refs[na:2 * na], refs[2 * na:]
        x, y, c, _ = _my_place()
        sibling = (x, y, 1 - c)
        copies = []
        for i in range(na):
            for q in range(N_CHIPS):
                s = i * N_CHIPS + q
                copies.append(_remote(srcs[i].at[2 * q + 1 - c], dsts[i].at[q], send_sems.at[s], recv_sems.at[s], sibling))
        for cp in copies:
            cp.start()
        for cp in copies:
            cp.wait_recv()
        for cp in copies:
            cp.wait_send()

    hbm = pl.BlockSpec(memory_space=pl.ANY)
    return pl.pallas_call(
        body, name=name, in_specs=[hbm] * na, out_specs=tuple([hbm] * na),
        out_shape=tuple(jax.ShapeDtypeStruct((N_CHIPS,) + a.shape[1:], a.dtype) for a in arrays),
        scratch_shapes=[pltpu.SemaphoreType.DMA((na * N_CHIPS,)), pltpu.SemaphoreType.DMA((na * N_CHIPS,))],
    )(*arrays)


ELEMENTWISE_COLS = 256


def _pair_sum(slabs, recv, core, out_dtype, name):
    _, r, w = slabs.shape
    tc = ELEMENTWISE_COLS

    def body(core_ref, mine, theirs, out):
        out[...] = (mine[...] + theirs[...]).astype(out.dtype)

    grid_spec = pltpu.PrefetchScalarGridSpec(
        num_scalar_prefetch=1, grid=(N_CHIPS, w // tc),
        in_specs=[pl.BlockSpec((None, r, tc), lambda q, j, core_ref: (2 * q + core_ref[0], 0, j)),
                  pl.BlockSpec((None, r, tc), lambda q, j, core_ref: (q, 0, j))],
        out_specs=pl.BlockSpec((None, r, tc), lambda q, j, core_ref: (q, 0, j)))
    return pl.pallas_call(body, name=name, grid_spec=grid_spec,
                          out_shape=jax.ShapeDtypeStruct((N_CHIPS, r, w), out_dtype),
                          compiler_params=_cparams(("parallel", "parallel")))(core, slabs, recv)


def _chip_exchange(arrays, name):
    na = len(arrays)

    def body(*refs):
        srcs, dsts, (send_sems, recv_sems, local_sems) = refs[:na], refs[na:2 * na], refs[2 * na:]
        x, y, c, _ = _my_place()
        chip = 2 * x + y
        own = [pltpu.make_async_copy(srcs[i].at[chip], dsts[i].at[chip], local_sems.at[i]) for i in range(na)]
        for cp in own:
            cp.start()
        sends, arrivals = [], []
        for i in range(na):
            for k in range(1, N_CHIPS):
                px, py = _chip_peer(x, y, k)
                s = i * N_CHIPS + k
                sends.append(_remote(srcs[i].at[chip ^ k], dsts[i].at[chip], send_sems.at[s], recv_sems.at[s], (px, py, c)))
                arrivals.append(_remote(srcs[i].at[chip], dsts[i].at[chip ^ k], send_sems.at[s], recv_sems.at[s], (px, py, c)))
        for cp in sends:
            cp.start()
        for cp in arrivals:
            cp.wait_recv()
        for cp in sends:
            cp.wait_send()
        for cp in own:
            cp.wait()

    hbm = pl.BlockSpec(memory_space=pl.ANY)
    return pl.pallas_call(
        body, name=name, in_specs=[hbm] * na, out_specs=tuple([hbm] * na),
        out_shape=tuple(jax.ShapeDtypeStruct(a.shape, a.dtype) for a in arrays),
        scratch_shapes=[pltpu.SemaphoreType.DMA((na * N_CHIPS,)), pltpu.SemaphoreType.DMA((na * N_CHIPS,)),
                        pltpu.SemaphoreType.DMA((na,))],
    )(*arrays)


def _adamw_update(g, w, m, v):
    c1 = 1.0 / (1.0 - ADAM_B1 ** ADAM_STEP)
    c2 = 1.0 / (1.0 - ADAM_B2 ** ADAM_STEP)
    mn = ADAM_B1 * m + (1.0 - ADAM_B1) * g
    vn = ADAM_B2 * v + (1.0 - ADAM_B2) * (g * g)
    return -ADAM_LR * ((mn * c1) / (jnp.sqrt(vn * c2) + ADAM_EPS) + ADAM_WD * w), mn, vn


def _reduce_adamw(parts, w, m, v, name):
    n_parts, r, wd = parts.shape
    tc = ELEMENTWISE_COLS

    def body(p_ref, w_ref, m_ref, v_ref, g_out, d_out, m_out, v_out):
        g = p_ref[0].astype(F32)
        for s in range(1, n_parts):
            g = g + p_ref[s].astype(F32)
        g_out[...] = g
        d_out[...], m_out[...], v_out[...] = _adamw_update(g, w_ref[...], m_ref[...], v_ref[...])

    blk = pl.BlockSpec((r, tc), lambda j: (0, j))
    shp = jax.ShapeDtypeStruct((r, wd), F32)
    return pl.pallas_call(
        body, name=name, grid=(wd // tc,),
        in_specs=[pl.BlockSpec((n_parts, r, tc), lambda j: (0, 0, j)), blk, blk, blk],
        out_specs=(blk, blk, blk, blk), out_shape=(shp, shp, shp, shp),
        compiler_params=_cparams(("parallel",)),
    )(parts, w, m, v)


def _reduce_landed_adamw(landing, own, me, w, m, v, name):
    n_parts, r, wd = landing.shape
    tc = ELEMENTWISE_COLS

    def body(me_ref, land_ref, own_ref, w_ref, m_ref, v_ref, g_out, d_out, m_out, v_out):
        mine = own_ref[...].astype(F32)
        g = None
        for s in range(n_parts):
            part = jnp.where(me_ref[0] == s, mine, land_ref[s].astype(F32))
            g = part if g is None else g + part
        g_out[...] = g
        d_out[...], m_out[...], v_out[...] = _adamw_update(g, w_ref[...], m_ref[...], v_ref[...])

    blk = pl.BlockSpec((r, tc), lambda j, me_ref: (0, j))
    shp = jax.ShapeDtypeStruct((r, wd), F32)
    grid_spec = pltpu.PrefetchScalarGridSpec(
        num_scalar_prefetch=1, grid=(wd // tc,),
        in_specs=[pl.BlockSpec((n_parts, r, tc), lambda j, me_ref: (0, 0, j)),
                  pl.BlockSpec((None, r, tc), lambda j, me_ref: (me_ref[0], 0, j)), blk, blk, blk],
        out_specs=(blk, blk, blk, blk))
    return pl.pallas_call(body, name=name, grid_spec=grid_spec, out_shape=(shp, shp, shp, shp),
                          compiler_params=_cparams(("parallel",)))(me, landing, own, w, m, v)


PACK_W = 1024


def _pad_heads(a, slots):
    lead = a.shape[:-1]
    a = a.reshape(lead + (slots, RWKV_HEAD_DIM))
    a = jnp.pad(a, [(0, 0)] * (len(lead) + 1) + [(0, LANES - RWKV_HEAD_DIM)])
    return a.reshape(lead + (slots * LANES,))


def _flat_pack(arrs, dtype, row_mult):
    flat = jnp.concatenate([a.reshape(-1).astype(dtype) for a in arrs])
    n = flat.shape[0]
    rows = -(-n // PACK_W)
    rows = -(-rows // row_mult) * row_mult
    return jnp.pad(flat, (0, rows * PACK_W - n)).reshape(rows, PACK_W)


def _row_pack(arrs, dtype, row_mult):
    parts = [a.astype(dtype) if a.shape[1] == PACK_W else a.astype(dtype).reshape(-1, PACK_W) for a in arrs]
    rows = sum(p.shape[0] for p in parts)
    pad = -(-rows // row_mult) * row_mult - rows
    return jnp.concatenate(parts + ([jnp.zeros((pad, PACK_W), dtype)] if pad else []), axis=0)


def _unpack_row_gathered(g, names, shard_shapes):
    out, r0 = {}, 0
    for n in names:
        s = shard_shapes[n]
        rows = s[0] * s[1] // PACK_W
        seg = g[:, r0:r0 + rows, :]
        r0 += rows
        if s[1] == PACK_W:
            assert SHARD_AXIS[n] == 0
            out[n] = seg.reshape(N_DEV * s[0], s[1])
        else:
            assert SHARD_AXIS[n] == 1
            out[n] = jnp.transpose(seg.reshape((N_DEV,) + tuple(s)), (1, 0, 2)).reshape(s[0], N_DEV * s[1])
    return out


def _unpack_gathered(g, names, shard_shapes):
    flat = g.reshape(N_DEV, -1)
    out, off = {}, 0
    for n in names:
        s = shard_shapes[n]
        size = s[0] * s[1]
        seg = flat[:, off:off + size].reshape((N_DEV,) + tuple(s))
        off += size
        if SHARD_AXIS[n] == 1:
            out[n] = jnp.transpose(seg, (1, 0, 2)).reshape(s[0], N_DEV * s[1])
        else:
            out[n] = seg.reshape(N_DEV * s[0], s[1])
    return out


def _shard_major(full, axis):
    a, b = full.shape
    if axis == 1:
        return jnp.transpose(full.reshape(a, N_DEV, b // N_DEV), (1, 0, 2)).reshape(N_DEV, -1)
    return full.reshape(N_DEV, -1)


def _prepare_weights(full, rep):
    w = full['w_in']
    d = w.shape[1]
    rkv = jnp.pad(w[0:1536].reshape(3 * RWKV_HEADS, RWKV_HEAD_DIM, d), ((0, 0), (0, LANES - RWKV_HEAD_DIM), (0, 0)))
    w_cat = jnp.concatenate([
        w[3848:4872], w[4872:5896], rkv.reshape(3 * RWKV_HEADS * LANES, d), w[1792:3328], w[3328:3840],
        w[1536:1792], jnp.pad(w[3840:3848], ((0, LANES - 8), (0, 0))), jnp.zeros((LANES, d), w.dtype)], axis=0)
    assert w_cat.shape[0] == CAT_W
    mu = rep['rwkv_mu']
    vecs = [mu[0:512], mu[512:1024], mu[1024:1536], rep['rwkv_w0'], rep['rwkv_a0'], rep['rwkv_k_k'], rep['rwkv_k_a'],
            rep['rwkv_ln_w'], rep['rwkv_ln_b'], rep['rwkv_r_k'].reshape(-1)]
    ppack = jnp.stack([jnp.pad(v.reshape(RWKV_HEADS, RWKV_HEAD_DIM), ((0, 0), (0, LANES - RWKV_HEAD_DIM))) for v in vecs], axis=1)
    ppack = jnp.pad(ppack, ((0, 0), (0, 16 - len(vecs)), (0, 0)))
    mulo = mu[1536:1792].reshape(1, 2 * LANES)
    wl = jnp.zeros((3, 2 * LANES, RWKV_HEADS * LANES), F32)
    wl = wl.at[0, 0:64].set(_pad_heads(full['rwkv_w2'], RWKV_HEADS))
    wl = wl.at[1, 64:128].set(_pad_heads(full['rwkv_a2'], RWKV_HEADS))
    wl = wl.at[2, 128:256].set(_pad_heads(full['rwkv_g2'], RWKV_HEADS))
    wl = jnp.transpose(wl.reshape(3, 2 * LANES, RWKV_HEADS, LANES), (2, 0, 1, 3))
    cw = full['gdn_conv_w'].reshape(GDN_CONV, 3, GDN_HEADS, LANES)
    cwpack = jnp.pad(jnp.transpose(cw, (2, 1, 0, 3)), ((0, 0), (0, 0), (0, SUBLANES - GDN_CONV), (0, 0)))
    gpar = jnp.zeros((SUBLANES, LANES), F32)
    gpar = gpar.at[0, 0:GDN_HEADS].set(rep['gdn_a_log']).at[1, 0:GDN_HEADS].set(rep['gdn_dt_bias']).at[2].set(rep['gdn_norm_w'])
    return dict(w_cat=w_cat, ffn_cw=full['ffn_conv_w'], ppack=ppack, mulo=mulo, wl=wl, cwpack=cwpack, gpar=gpar,
                g1=rep['norm1_g'].reshape(1, -1), g2=rep['norm2_g'].reshape(1, -1), gf=rep['final_g'].reshape(1, -1))


def _prepare_late_weights(full):
    rp = full['rwkv_proj']
    rproj = jnp.pad(rp.reshape(RWKV_HEADS, RWKV_HEAD_DIM, -1), ((0, 0), (0, LANES - RWKV_HEAD_DIM), (0, 0))).reshape(RWKV_HEADS * LANES, -1)
    return dict(rproj=rproj, gproj=full['gdn_proj'], w_out=full['w_out'], ffn_up=full['ffn_up'], ffn_down=full['ffn_down'])


def _local_step(x, target, p, late_weights, push_grads):
    d = x.shape[1]
    full_w = lambda a: (a, a.shape[1], 0)
    (u,) = _rw_forward(_norm_fn, [full_w(x)], [p['g1']], [(d, BF16)], "norm1")
    p_cat = _matmul(u, p['w_cat'], 'nt', F32, "proj_in")
    ya_pre, st_r = _rwkv_forward(p_cat, p['ppack'], p['mulo'], p['wl'])
    yb_pre, st_g = _gdn_forward(p_cat, p['cwpack'], p['gpar'])
    p = {**p, **late_weights((ya_pre, yb_pre))}
    ya = _matmul(ya_pre, p['rproj'], 'nn', F32, "rwkv_proj")
    yb = _matmul(yb_pre, p['gproj'], 'nn', F32, "gdn_proj")
    gates = [(p_cat, d, OFF_GA // d), (p_cat, d, OFF_GB // d)]
    (mixed,) = _rw_forward(_merge_fn, gates + [full_w(ya), full_w(yb)], [], [(d, BF16)], "merge")
    mo = _matmul(mixed, p['w_out'], 'nn', F32, "out_proj")
    x1, n2 = _rw_forward(_res_norm_fn, [full_w(x), full_w(mo)], [p['g2']], [(d, F32), (d, BF16)], "res_norm2")
    hpre = _matmul(n2, p['ffn_up'], 'nt', F32, "ffn_up")
    act = _ffn_act_forward(hpre, p['ffn_cw'])
    fo = _matmul(act, p['ffn_down'], 'nn', F32, "ffn_down")
    loss_vec, dx2, dx2b, dgf = _loss_head(x1, fo, p['gf'], target, "loss_head")

    dact = _matmul(dx2b, p['ffn_down'], 'nt', F32, "d_act")
    dw_down = _matmul(act, dx2b, 'tn', BF16, "dw_ffn_down")
    dh_gate, dh_up, dcw_gate, dcw_up = _ffn_backward(hpre, p['ffn_cw'], dact)
    dh = jnp.concatenate([dh_gate, dh_up], axis=1)
    dcw_f = jnp.concatenate([dcw_gate, dcw_up], axis=1)
    dn2 = _matmul(dh, p['ffn_up'], 'nn', F32, "d_norm2")
    dw_up = _matmul(dh, n2, 'tn', BF16, "dw_ffn_up")
    token = push_grads({'ffn_down': dw_down, 'ffn_up': dw_up})
    dx1, dx1b, dg2 = _rw_backward(_res_norm_fn, [full_w(x), full_w(mo)], [p['g2'] + token], [full_w(dx2), full_w(dn2)],
                                  [F32, BF16], "res_norm2_bwd")
    dmixed = _matmul(dx1b, p['w_out'], 'nt', F32, "d_mixed")
    dw_out = _matmul(mixed, dx1b, 'tn', BF16, "dw_out")
    dga, dgb, dya, dyb = _rw_backward(_merge_fn, gates + [full_w(ya), full_w(yb)], [], [full_w(dmixed)],
                                      [BF16, BF16, BF16, BF16], "merge_bwd")
    d_ya_pre = _matmul(dya, p['rproj'], 'nt', F32, "d_rwkv_out")
    dw_rproj = _matmul(ya_pre, dya, 'tn', F32, "dw_rwkv_proj")
    d_yb_pre = _matmul(dyb, p['gproj'], 'nt', F32, "d_gdn_out")
    dw_gproj = _matmul(yb_pre, dyb, 'tn', F32, "dw_gdn_proj")
    dpr, dpk, dpv, dplo, dpp, dml, dwl = _rwkv_backward(p_cat, p['ppack'], p['mulo'], p['wl'], st_r, d_ya_pre)
    dq, dk, dv, dz, dab, dcw_g, dgp = _gdn_backward(p_cat, p['cwpack'], p['gpar'], st_g, d_yb_pre)
    t = x.shape[0]
    dp_cat = jnp.concatenate([dga, dgb, dpr, dpk, dpv, dq, dk, dv, dz, dplo.astype(BF16), dab.astype(BF16),
                              jnp.zeros((t, LANES), BF16)], axis=1)
    dw_cat = _matmul(dp_cat, u, 'tn', BF16, "dw_in")
    dw_in = jnp.concatenate([dw_cat[OFF_RKV:OFF_QKV].reshape(3 * RWKV_HEADS, LANES, d)[:, :RWKV_HEAD_DIM].reshape(-1, d),
                             dw_cat[OFF_LO:OFF_AB], dw_cat[OFF_QKV:OFF_Z], dw_cat[OFF_Z:OFF_LO], dw_cat[OFF_AB:OFF_AB + 8],
                             dw_cat[OFF_GA:OFF_GB], dw_cat[OFF_GB:OFF_RKV]], axis=0)
    token = push_grads({'w_out': dw_out, 'w_in': dw_in})
    du = _matmul(dp_cat, p['w_cat'], 'nn', F32, "d_norm1")
    grad_x, dg1 = _rw_backward(_norm_skip_fn, [full_w(x)], [p['g1'] + token], [full_w(du), full_w(dx1)], [F32], "norm1_bwd")

    heads = lambda row: dpp[:, row, :RWKV_HEAD_DIM].reshape(-1)
    lora = lambda j, lo_, hi_: jnp.transpose(dwl[:, j, lo_:hi_, :RWKV_HEAD_DIM], (1, 0, 2)).reshape(hi_ - lo_, RWKV_WIDTH)
    grads = {
        'norm1_g': dg1[0],
        'w_in': dw_in,
        'rwkv_mu': jnp.concatenate([heads(0), heads(1), heads(2), jnp.sum(dml[:, 0, :], axis=0)]),
        'rwkv_w0': heads(3), 'rwkv_a0': heads(4), 'rwkv_k_k': heads(5), 'rwkv_k_a': heads(6),
        'rwkv_ln_w': heads(7), 'rwkv_ln_b': heads(8), 'rwkv_r_k': heads(9).reshape(RWKV_HEADS, RWKV_HEAD_DIM),
        'rwkv_w2': lora(0, 0, 64), 'rwkv_a2': lora(1, 64, 128), 'rwkv_g2': lora(2, 128, 256),
        'rwkv_proj': dw_rproj.reshape(RWKV_HEADS, LANES, -1)[:, :RWKV_HEAD_DIM].reshape(RWKV_WIDTH, -1),
        'gdn_conv_w': jnp.transpose(dcw_g[:, :, :GDN_CONV, :], (2, 1, 0, 3)).reshape(GDN_CONV, 3 * GDN_WIDTH),
        'gdn_a_log': jnp.sum(dgp[:, 0, :GDN_HEADS], axis=0), 'gdn_dt_bias': jnp.sum(dgp[:, 1, :GDN_HEADS], axis=0),
        'gdn_norm_w': jnp.sum(dgp[:, 2, :], axis=0),
        'gdn_proj': dw_gproj, 'w_out': dw_out, 'norm2_g': dg2[0], 'ffn_up': dw_up, 'ffn_conv_w': dcw_f,
        'ffn_down': dw_down, 'final_g': dgf[0],
    }
    return loss_vec, grad_x, grads


def kernel(x, norm1_g, w_in, rwkv_mu, rwkv_w0, rwkv_w2, rwkv_a0, rwkv_a2, rwkv_g2, rwkv_k_k, rwkv_k_a, rwkv_r_k, rwkv_ln_w, rwkv_ln_b, rwkv_proj, gdn_conv_w, gdn_a_log, gdn_dt_bias, gdn_norm_w, gdn_proj, w_out, norm2_g, ffn_up, ffn_conv_w, ffn_down, final_g, loss_target, m_norm1_g, m_w_in, m_rwkv_mu, m_rwkv_w0, m_rwkv_w2, m_rwkv_a0, m_rwkv_a2, m_rwkv_g2, m_rwkv_k_k, m_rwkv_k_a, m_rwkv_r_k, m_rwkv_ln_w, m_rwkv_ln_b, m_rwkv_proj, m_gdn_conv_w, m_gdn_a_log, m_gdn_dt_bias, m_gdn_norm_w, m_gdn_proj, m_w_out, m_norm2_g, m_ffn_up, m_ffn_conv_w, m_ffn_down, m_final_g, v_norm1_g, v_w_in, v_rwkv_mu, v_rwkv_w0, v_rwkv_w2, v_rwkv_a0, v_rwkv_a2, v_rwkv_g2, v_rwkv_k_k, v_rwkv_k_a, v_rwkv_r_k, v_rwkv_ln_w, v_rwkv_ln_b, v_rwkv_proj, v_gdn_conv_w, v_gdn_a_log, v_gdn_dt_bias, v_gdn_norm_w, v_gdn_proj, v_w_out, v_norm2_g, v_ffn_up, v_ffn_conv_w, v_ffn_down, v_final_g):
    given = dict(zip(WEIGHT_NAMES, (norm1_g, w_in, rwkv_mu, rwkv_w0, rwkv_w2, rwkv_a0, rwkv_a2, rwkv_g2, rwkv_k_k, rwkv_k_a, rwkv_r_k,
                                    rwkv_ln_w, rwkv_ln_b, rwkv_proj, gdn_conv_w, gdn_a_log, gdn_dt_bias, gdn_norm_w, gdn_proj, w_out,
                                    norm2_g, ffn_up, ffn_conv_w, ffn_down, final_g)))
    mom1 = dict(zip(WEIGHT_NAMES, (m_norm1_g, m_w_in, m_rwkv_mu, m_rwkv_w0, m_rwkv_w2, m_rwkv_a0, m_rwkv_a2, m_rwkv_g2, m_rwkv_k_k,
                                   m_rwkv_k_a, m_rwkv_r_k, m_rwkv_ln_w, m_rwkv_ln_b, m_rwkv_proj, m_gdn_conv_w, m_gdn_a_log,
                                   m_gdn_dt_bias, m_gdn_norm_w, m_gdn_proj, m_w_out, m_norm2_g, m_ffn_up, m_ffn_conv_w, m_ffn_down,
                                   m_final_g)))
    mom2 = dict(zip(WEIGHT_NAMES, (v_norm1_g, v_w_in, v_rwkv_mu, v_rwkv_w0, v_rwkv_w2, v_rwkv_a0, v_rwkv_a2, v_rwkv_g2, v_rwkv_k_k,
                                   v_rwkv_k_a, v_rwkv_r_k, v_rwkv_ln_w, v_rwkv_ln_b, v_rwkv_proj, v_gdn_conv_w, v_gdn_a_log,
                                   v_gdn_dt_bias, v_gdn_norm_w, v_gdn_proj, v_w_out, v_norm2_g, v_ffn_up, v_ffn_conv_w, v_ffn_down,
                                   v_final_g)))
    def strip(n, a):
        a = a if n == 'final_g' else a.reshape(a.shape[1:])
        return a.T if n in TRANSPOSED else a

    local = {n: strip(n, a) for n, a in given.items()}
    shard_shapes = {n: local[n].shape for n in SHARD_AXIS}
    sharded = BIG_SHARDED + SMALL_SHARDED

    late_names = [n for n in BIG_SHARDED if n != 'w_in']
    g_in = _all_gather_two_level(_row_pack([local['w_in']], BF16, 16), "gather_w_in")
    g_small = _all_gather_two_level(_flat_pack([local[n] for n in SMALL_SHARDED], F32, SUBLANES), "gather_small")
    late_pack, g_in, g_small = lax.optimization_barrier((_row_pack([local[n] for n in late_names], BF16, 16), g_in, g_small))
    send_sems, recv_sems, late_pack, landing, token = _gather_start(late_pack, "gather_late_start")
    full = _unpack_row_gathered(g_in, ['w_in'], shard_shapes)
    full.update(_unpack_gathered(g_small, SMALL_SHARDED, shard_shapes))
    rep = {n: local[n] for n in REPLICATED}
    rep['norm1_g'] = rep['norm1_g'] + token[0, 0]

    def late_weights(after):
        got = _gather_wait(send_sems, recv_sems, late_pack, landing, after, "gather_late_wait")
        me = 4 * lax.axis_index("x") + 2 * lax.axis_index("y") + lax.axis_index("c")
        slot = lax.broadcasted_iota(jnp.int32, (N_DEV, 1, 1), 0)
        got = jnp.where(slot == me, late_pack[None], got)
        return _prepare_late_weights(_unpack_row_gathered(got, late_names, shard_shapes))

    pushes = []
    me = 4 * lax.axis_index("x") + 2 * lax.axis_index("y") + lax.axis_index("c")
    slot = lax.broadcasted_iota(jnp.int32, (N_DEV, 1, 1), 0)

    def push_grads(group):
        names = list(group)
        slabs = [group[n].reshape(N_DEV, -1, group[n].shape[1]) for n in names]
        send_sems, recv_sems, slabs, landings, token = _slab_push_start(slabs, "grad_push_start_" + "_".join(names))
        pushes.append((names, send_sems, recv_sems, slabs, landings))
        return token[0, 0]

    loss_vec, grad_x, grads = _local_step(x[0], loss_target[0], _prepare_weights(full, rep), late_weights, push_grads)

    landed = {}
    for names, send_sems, recv_sems, slabs, landings in pushes:
        got = _slab_push_wait(send_sems, recv_sems, slabs, landings, (grad_x,), "grad_push_wait_" + "_".join(names))
        for n, slab, land in zip(names, slabs, got):
            landed[n] = (land, slab)

    small_sharded = ['rwkv_proj', 'gdn_proj'] + SMALL_SHARDED
    small_names = small_sharded + REPLICATED
    rep_vec = jnp.concatenate([grads[n].reshape(-1) for n in REPLICATED] + [loss_vec[0, 0:1]])
    slab_small = jnp.concatenate([_shard_major(grads[n], SHARD_AXIS[n]) for n in small_sharded] +
                                 [jnp.broadcast_to(rep_vec[None], (N_DEV, rep_vec.shape[0]))], axis=1)
    small_rows = -(-slab_small.shape[1] // (PACK_W * SUBLANES)) * SUBLANES
    slab_small = jnp.pad(slab_small, ((0, 0), (0, small_rows * PACK_W - slab_small.shape[1]))).reshape(N_DEV, small_rows, PACK_W)
    core = lax.axis_index("c").astype(jnp.int32).reshape(1)
    (from_sibling,) = _pair_exchange([slab_small], "grad_pair_exchange")
    chip_small = _pair_sum(slab_small, from_sibling, core, F32, "grad_pair_sum_small")
    (parts_small,) = _chip_exchange([chip_small], "grad_chip_exchange")

    def pack_local(src):
        flat = jnp.concatenate([strip(n, src[n]).reshape(-1) for n in small_names])
        return jnp.pad(flat, (0, small_rows * PACK_W - flat.shape[0])).reshape(small_rows, PACK_W)

    results = [({}, None) for _ in range(4)]
    me_arr = me.astype(jnp.int32).reshape(1)
    for n in ROW_SHARDED:
        packs = _reduce_landed_adamw(*landed[n], me_arr, local[n], strip(n, mom1[n]), strip(n, mom2[n]), "adamw_" + n)
        for (out, _), pk in zip(results, packs):
            out[n] = (pk.T if n in TRANSPOSED else pk).reshape(given[n].shape)
    packs = _reduce_adamw(parts_small, pack_local(given), pack_local(mom1), pack_local(mom2), "adamw_small")
    for i, pk in enumerate(packs):
        flat, off = pk.reshape(-1), 0
        for n in small_names:
            size = int(np.prod(given[n].shape))
            results[i][0][n] = flat[off:off + size].reshape(given[n].shape)
            off += size
        results[i] = (results[i][0], flat[off])
    (g_out, loss), (d_out, _), (m_out, _), (v_out, _) = results
    return (loss, grad_x[None], *[g_out[n] for n in WEIGHT_NAMES], *[d_out[n] for n in WEIGHT_NAMES],
            *[m_out[n] for n in WEIGHT_NAMES], *[v_out[n] for n in WEIGHT_NAMES])
```

```python
import functools

import jax
import jax.numpy as jnp
import numpy as np
from jax import lax
from jax.experimental import pallas as pl
from jax.experimental.pallas import tpu as pltpu

F32 = jnp.float32
BF16 = jnp.bfloat16

N_DEV = 8
D_MODEL = 1024
CHUNK = 64
RWKV_HEADS = 8
RWKV_HEAD_DIM = 64
RWKV_WIDTH = 512
GDN_HEADS = 4
GDN_HEAD_DIM = 128
GDN_WIDTH = 512
GDN_CONV = 4
FFN_HIDDEN = 2816
FFN_CONV = 3
NORM_EPS = 1e-6
L2_EPS = 1e-6
RWKV_GN_EPS = 64e-5
LANES = 128
SUBLANES = 8
VMEM_LIMIT = 56 * 1024 * 1024

ADAM_LR = 0.001
ADAM_B1 = 0.9
ADAM_B2 = 0.999
ADAM_EPS = 1e-08
ADAM_WD = 0.01
ADAM_STEP = 10

OFF_GA, OFF_GB, OFF_RKV, OFF_QKV, OFF_Z, OFF_LO, OFF_AB, CAT_W = 0, 1024, 2048, 5120, 6656, 7168, 7424, 7680
RWKV_HB = 8
RWKV_STEP_CHUNKS = 2
RWKV_TILE = RWKV_STEP_CHUNKS * CHUNK
GDN_HB = 4
GDN_STEP_CHUNKS = 4
GDN_TILE = GDN_STEP_CHUNKS * CHUNK

WEIGHT_NAMES = ['norm1_g', 'w_in', 'rwkv_mu', 'rwkv_w0', 'rwkv_w2', 'rwkv_a0', 'rwkv_a2', 'rwkv_g2', 'rwkv_k_k', 'rwkv_k_a',
                'rwkv_r_k', 'rwkv_ln_w', 'rwkv_ln_b', 'rwkv_proj', 'gdn_conv_w', 'gdn_a_log', 'gdn_dt_bias', 'gdn_norm_w',
                'gdn_proj', 'w_out', 'norm2_g', 'ffn_up', 'ffn_conv_w', 'ffn_down', 'final_g']
BIG_SHARDED = ['w_in', 'ffn_up', 'ffn_down', 'w_out', 'rwkv_proj', 'gdn_proj']
SMALL_SHARDED = ['rwkv_w2', 'rwkv_a2', 'rwkv_g2', 'gdn_conv_w', 'ffn_conv_w']
TRANSPOSED = ('w_in', 'ffn_up')
SHARD_AXIS = {'w_in': 0, 'ffn_up': 0, 'ffn_down': 0, 'w_out': 0, 'rwkv_proj': 1, 'gdn_proj': 1,
              'rwkv_w2': 1, 'rwkv_a2': 1, 'rwkv_g2': 1, 'gdn_conv_w': 1, 'ffn_conv_w': 1}
REPLICATED = [n for n in WEIGHT_NAMES if n not in SHARD_AXIS]
ROW_SHARDED = ['w_in', 'ffn_up', 'ffn_down', 'w_out']


def _cparams(sem=None):
    kw = dict(vmem_limit_bytes=VMEM_LIMIT)
    if sem is not None:
        kw['dimension_semantics'] = sem
    return pltpu.CompilerParams(**kw)


_NN, _NT, _TN = 'nn', 'nt', 'tn'
_DIMS_2D = {'nn': (((1,), (0,)), ((), ())), 'nt': (((1,), (1,)), ((), ())), 'tn': (((0,), (0,)), ((), ()))}
_DIMS_3D = {'nn': (((2,), (1,)), ((0,), (0,))), 'nt': (((2,), (2,)), ((0,), (0,))), 'tn': (((1,), (1,)), ((0,), (0,)))}


def _dg(a, b, kind):
    return lax.dot_general(a, b, (_DIMS_2D if a.ndim == 2 else _DIMS_3D)[kind], preferred_element_type=F32)


def _dot1(a, b, kind):
    return _dg(a.astype(BF16), b.astype(BF16), kind)


@jax.custom_vjp
def _dhi(a, b):
    return _dot1(a, b, _NN)


_dhi.defvjp(lambda a, b: (_dot1(a, b, _NN), (a, b)),
            lambda res, ct: (_dot1(ct, res[1], _NT), _dot1(res[0], ct, _TN)))


@jax.custom_vjp
def _dnt(a, b):
    return _dot1(a, b, _NT)


_dnt.defvjp(lambda a, b: (_dot1(a, b, _NT), (a, b)),
            lambda res, ct: (_dot1(ct, res[1], _NN), _dot1(ct, res[0], _TN)))


@jax.custom_vjp
def _dtn(a, b):
    return _dot1(a, b, _TN)


_dtn.defvjp(lambda a, b: (_dot1(a, b, _TN), (a, b)),
            lambda res, ct: (_dot1(res[1], ct, _NT), _dot1(res[0], ct, _NN)))


def _split3(x):
    x1 = x.astype(BF16)
    r1 = x - x1.astype(F32)
    x2 = r1.astype(BF16)
    return x1, x2, (r1 - x2.astype(F32)).astype(BF16)


def _dot_exact_lhs(sel, x, kind):
    parts = [_dg(sel, xi, kind) for xi in _split3(x)]
    return parts[0] + parts[1] + parts[2]


def _tril_ones(like):
    c = like.shape[-2]
    ri, ci = _iotas(c)
    return jnp.broadcast_to((ri >= ci).astype(BF16), like.shape[:-2] + (c, c))


@jax.custom_vjp
def _cumsum_rows(x):
    return _dot_exact_lhs(_tril_ones(x), x, _NN)


_cumsum_rows.defvjp(lambda x: (_dot_exact_lhs(_tril_ones(x), x, _NN), None),
                    lambda _, ct: (_dot_exact_lhs(_tril_ones(ct), ct, _TN),))


@jax.custom_vjp
def _lane_sum_as_row(x):
    return _dot_exact_lhs(jnp.ones(x.shape, BF16), x, _NT)


def _lane_sum_as_row_bwd(_, ct):
    ones = jnp.ones(ct.shape[:-1] + (LANES,), BF16)
    parts = [_dg(ci, ones, _TN) for ci in _split3(ct)]
    return (parts[0] + parts[1] + parts[2],)


_lane_sum_as_row.defvjp(lambda x: (_dot_exact_lhs(jnp.ones(x.shape, BF16), x, _NT), None), _lane_sum_as_row_bwd)


def _shift_rows(x, halo, s):
    rows = lax.broadcasted_iota(jnp.int32, x.shape, 0)
    out = pltpu.roll(x, s, 0)
    for i in range(s):
        out = jnp.where(rows == i, halo[SUBLANES - s + i:SUBLANES - s + i + 1, :], out)
    return out


def _unshift_rows(g, carry, s):
    c = g.shape[0]
    rows = lax.broadcasted_iota(jnp.int32, g.shape, 0)
    out = pltpu.roll(g, c - s, 0)
    for i in range(s):
        out = jnp.where(rows == c - s + i, carry[i:i + 1, :], out)
    return out


def _sigmoid_plain(z):
    return 1.0 / (1.0 + jnp.exp(-z))


def _sigmoid_value(z):
    t = jnp.exp(-jnp.abs(z))
    r = 1.0 / (1.0 + t)
    return jnp.where(z >= 0, r, t * r)


@jax.custom_vjp
def _sigmoid(z):
    return _sigmoid_value(z)


def _sigmoid_fwd(z):
    s = _sigmoid_value(z)
    return s, s


_sigmoid.defvjp(_sigmoid_fwd, lambda s, ct: (ct * s * (1.0 - s),))


def _silu(z):
    return z * _sigmoid(z)


def _softplus(z):
    return jnp.maximum(z, 0.0) + jnp.log(1.0 + jnp.exp(-jnp.abs(z)))


def _rms(t, gain):
    return t * lax.rsqrt(jnp.mean(t * t, axis=-1, keepdims=True) + NORM_EPS) * gain


def _iotas(c):
    return lax.broadcasted_iota(jnp.int32, (c, c), 0), lax.broadcasted_iota(jnp.int32, (c, c), 1)


def _unit_lower_inverse(xm, eye):
    t = eye + xm
    p = xm
    for _ in range(5):
        p = _dhi(p, p)
        t = t + _dhi(t, p)
    return t


def _rwkv_head(pr, pk, pv, plo, qr, qk, qv, qlo, s0, pp, mulo, wl):
    c = pr.shape[1]
    n_heads = s0.shape[0]
    n_chunks = pr.shape[0] // n_heads
    ri, ci = _iotas(c)
    if n_chunks > 1:
        pp = jnp.concatenate([pp] * n_chunks, axis=0)
        wl = jnp.concatenate([wl] * n_chunks, axis=0)

    def mix(p, q, mu):
        return p + (q - p) * mu

    r = mix(pr, qr, pp[:, 0:1])
    k = mix(pk, qk, pp[:, 1:2])
    v = mix(pv, qv, pp[:, 2:3])
    lo = mix(plo, qlo, mulo)
    w0, a0, k_k, k_a, ln_w, ln_b, r_k = (pp[:, i:i + 1] for i in range(3, 10))

    def per_head(t):
        return jnp.concatenate([jnp.broadcast_to(t[i], (n_heads,) + t.shape[1:]) for i in range(n_chunks)], axis=0)

    zw = _dhi(per_head(jnp.tanh(lo)), wl[:, 0])
    za = _dhi(per_head(lo), wl[:, 1])
    g = _dhi(per_head(_sigmoid(lo)), wl[:, 2])
    w_log = -_softplus(-(w0 + zw)) - 0.5
    lw = -jnp.exp(w_log)
    a = _sigmoid(a0 + za)
    kk = k * k_k
    kk = kk * lax.rsqrt(jnp.sum(kk * kk, axis=-1, keepdims=True) + L2_EPS)
    k2 = k * (1.0 + (a - 1.0) * k_a)
    an = -kk
    b = kk * a
    causal = ri >= ci
    strict = ri > ci
    eye = (ri == ci).astype(F32)
    cl = _cumsum_rows(lw)
    ecl = jnp.exp(-cl)
    at = an * jnp.exp(cl - lw)
    bt = b * ecl
    kt = k2 * ecl
    rt = r * jnp.exp(cl)
    a_ab = jnp.where(strict, _dnt(at, bt), 0.0)
    a_ak = jnp.where(strict, _dnt(at, kt), 0.0)
    tinv = _unit_lower_inverse(a_ab, eye)
    akv = _dhi(a_ak, v)
    r_b = jnp.where(causal, _dnt(rt, bt), 0.0)
    rkv = _dhi(jnp.where(causal, _dnt(rt, kt), 0.0), v)
    cl_end = jnp.sum(lw, axis=1, keepdims=True)
    dec_end = jnp.exp(cl_end - cl)
    b_end = b * dec_end
    sv = _dtn(v, k2 * dec_end)
    e_end = jnp.exp(cl_end)
    state, ys = s0, []
    for i in range(n_chunks):
        sl = slice(i * n_heads, (i + 1) * n_heads)
        u = _dhi(tinv[sl], _dnt(at[sl], state) + akv[sl])
        ys.append(_dnt(rt[sl], state) + _dhi(r_b[sl], u) + rkv[sl])
        state = state * e_end[sl] + _dtn(u, b_end[sl]) + sv[sl]
    y = jnp.concatenate(ys, axis=0) if n_chunks > 1 else ys[0]
    s1 = state
    m = (lax.broadcasted_iota(jnp.int32, (1, LANES), 1) < RWKV_HEAD_DIM).astype(F32)
    mean = jnp.sum(y, axis=-1, keepdims=True) * (1.0 / RWKV_HEAD_DIM)
    yc = (y - mean) * m
    var = jnp.sum(yc * yc, axis=-1, keepdims=True) * (1.0 / RWKV_HEAD_DIM)
    yn = yc * lax.rsqrt(var + RWKV_GN_EPS) * ln_w + ln_b
    y2 = yn + jnp.sum(r * k2 * r_k, axis=-1, keepdims=True) * v
    return y2 * g, s1


def _gdn_head(xq, xk, xv, z, ab, s0, cw, gp, oha, ohb):
    c = z.shape[1]
    n_heads = s0.shape[0]
    n_chunks = z.shape[0] // n_heads
    ri, ci = _iotas(c)
    cw = jnp.concatenate([cw] * n_chunks, axis=0) if n_chunks > 1 else cw

    def conv(xs, w):
        out = xs[0] * w[:, GDN_CONV - 1:GDN_CONV]
        for s in range(1, GDN_CONV):
            out = out + xs[s] * w[:, GDN_CONV - 1 - s:GDN_CONV - s]
        return out

    q = _silu(conv(xq, cw[:, 0]))
    k = _silu(conv(xk, cw[:, 1]))
    v = _silu(conv(xv, cw[:, 2]))
    q = q * lax.rsqrt(jnp.sum(q * q, axis=-1, keepdims=True) + L2_EPS) * (GDN_HEAD_DIM ** -0.5)
    k = k * lax.rsqrt(jnp.sum(k * k, axis=-1, keepdims=True) + L2_EPS)
    gg = -jnp.exp(gp[0:1]) * _softplus(ab + gp[1:2])
    beta = jnp.sum(_sigmoid(ab) * ohb, axis=-1, keepdims=True)
    causal = ri >= ci
    strict = ri > ci
    eye = (ri == ci).astype(F32)
    gcm = _cumsum_rows(gg * oha)
    gc = jnp.sum(gcm, axis=-1, keepdims=True)
    gc_row = _lane_sum_as_row(gcm)
    dec = jnp.where(causal, jnp.exp(jnp.where(causal, gc - gc_row, 0.0)), 0.0)
    kb = k * beta
    vb = v * beta
    lm = jnp.where(strict, _dnt(kb, k) * dec, 0.0)
    tinv = _unit_lower_inverse(-lm, eye)
    egc = jnp.exp(gc)
    u = _dhi(tinv, vb)
    wk = _dhi(tinv, kb * egc)
    attn = jnp.where(causal, _dnt(q, k) * dec, 0.0)
    g_last = gc[:, c - 1:c, :]
    q_dec = q * egc
    k_dec = k * jnp.exp(g_last - gc)
    e_last = jnp.exp(g_last)
    state, outs = s0, []
    for i in range(n_chunks):
        sl = slice(i * n_heads, (i + 1) * n_heads)
        v_new = u[sl] - _dhi(wk[sl], state)
        outs.append(_dhi(q_dec[sl], state) + _dhi(attn[sl], v_new))
        state = state * e_last[sl] + _dtn(k_dec[sl], v_new)
    o = jnp.concatenate(outs, axis=0) if n_chunks > 1 else outs[0]
    return _rms(o, gp[2:3]) * _silu(z), state


def _head_id(grp, i, per_step, heads):
    return i if per_step == heads else grp * per_step + i


def _head_range(grp, per_step, heads):
    return slice(None) if per_step == heads else pl.ds(grp * per_step, per_step)


def _rwkv_specs(nmap):
    hb, groups = RWKV_HB, RWKV_HEADS // RWKV_HB
    cb = OFF_RKV // (hb * LANES)
    specs = []
    for j in range(3):
        specs.append(pl.BlockSpec((RWKV_TILE, hb * LANES), lambda n, g, j=j: (nmap(n), cb + j * groups + g)))
    specs.append(pl.BlockSpec((RWKV_TILE, 2 * LANES), lambda n, g: (nmap(n), OFF_LO // (2 * LANES))))
    per = RWKV_TILE // SUBLANES
    for j in range(3):
        specs.append(pl.BlockSpec((SUBLANES, hb * LANES),
                                  lambda n, g, j=j: (jnp.maximum(nmap(n) * per - 1, 0), cb + j * groups + g)))
    specs.append(pl.BlockSpec((SUBLANES, 2 * LANES), lambda n, g: (jnp.maximum(nmap(n) * per - 1, 0), OFF_LO // (2 * LANES))))
    specs.append(pl.BlockSpec((hb, 16, LANES), lambda n, g: (g, 0, 0)))
    specs.append(pl.BlockSpec((1, 2 * LANES), lambda n, g: (0, 0)))
    specs.append(pl.BlockSpec((hb, 3, 2 * LANES, LANES), lambda n, g: (g, 0, 0, 0)))
    return specs


def _rwkv_operands(refs, halos, live):
    pr, pk, pv, plo = refs
    hr, hk, hv, hlo = halos
    cur, prev = [], []
    for x, hx in ((pr, hr), (pk, hk), (pv, hv)):
        tiles = [_lane_block(x, h) for h in range(RWKV_HB)]
        cur.append(_chunk_batch(tiles))
        prev.append(_chunk_batch([_shift_rows(t_, _lane_block(hx, h) * live, 1) for h, t_ in enumerate(tiles)]))
    lo = plo[...]
    cur.append(_chunk_batch([lo]))
    prev.append(_chunk_batch([_shift_rows(lo, hlo[...] * live, 1)]))
    return cur, prev


def _rwkv_forward(p_cat, ppack, mulo, wl):
    t = p_cat.shape[0]
    n_chunks = t // RWKV_TILE

    def body(pr, pk, pv, plo, hr, hk, hv, hlo, pp, ml, w, out, st_out, s_scr):
        n, grp = pl.program_id(0), pl.program_id(1)

        hsl = _head_range(grp, RWKV_HB, RWKV_HEADS)

        @pl.when(n == 0)
        def _():
            s_scr[hsl] = jnp.zeros((RWKV_HB, LANES, LANES), F32)

        live = (n > 0).astype(F32)
        cur, prev = _rwkv_operands((pr, pk, pv, plo), (hr, hk, hv, hlo), live)
        s0 = s_scr[hsl]
        st_out[...] = s0
        o, s1 = _rwkv_head(*cur, *prev, s0, pp[...], ml[...], w[...])
        for h, tile in enumerate(_head_tiles(o, RWKV_HB)):
            out[:, h * LANES:(h + 1) * LANES] = tile.astype(out.dtype)
        s_scr[hsl] = s1

    return pl.pallas_call(
        body, name="rwkv_fwd", grid=(n_chunks, RWKV_HEADS // RWKV_HB),
        in_specs=_rwkv_specs(lambda n: n),
        out_specs=(pl.BlockSpec((RWKV_TILE, RWKV_HB * LANES), lambda n, g: (n, g)),
                   pl.BlockSpec((None, RWKV_HB, LANES, LANES), lambda n, g: (n, g, 0, 0))),
        out_shape=(jax.ShapeDtypeStruct((t, RWKV_HEADS * LANES), BF16),
                   jax.ShapeDtypeStruct((n_chunks, RWKV_HEADS, LANES, LANES), F32)),
        scratch_shapes=[pltpu.VMEM((RWKV_HEADS, LANES, LANES), F32)],
        compiler_params=_cparams(("arbitrary", "arbitrary")),
    )(p_cat, p_cat, p_cat, p_cat, p_cat, p_cat, p_cat, p_cat, ppack, mulo, wl)


def _rwkv_backward(p_cat, ppack, mulo, wl, states, d_out):
    t = p_cat.shape[0]
    n_chunks = t // RWKV_TILE
    last = n_chunks - 1

    def body(pr, pk, pv, plo, hr, hk, hv, hlo, pp, ml, w, st, dy, dpr, dpk, dpv, dplo, dpp, dml, dw, ds_scr, car_scr, carlo_scr):
        n, grp = pl.program_id(0), pl.program_id(1)

        hsl = _head_range(grp, RWKV_HB, RWKV_HEADS)
        gi = _head_id(grp, 0, 1, RWKV_HEADS // RWKV_HB)

        @pl.when(n == 0)
        def _():
            ds_scr[hsl] = jnp.zeros((RWKV_HB, LANES, LANES), F32)
            car_scr[hsl] = jnp.zeros((RWKV_HB, 3 * SUBLANES, LANES), F32)
            carlo_scr[gi] = jnp.zeros((SUBLANES, 2 * LANES), F32)

        @pl.when((n == 0) & (grp == 0))
        def _():
            dpp[...] = jnp.zeros(dpp.shape, F32)
            dml[...] = jnp.zeros(dml.shape, F32)
            dw[...] = jnp.zeros(dw.shape, F32)

        live = (n < last).astype(F32)
        cur, prev = _rwkv_operands((pr, pk, pv, plo), (hr, hk, hv, hlo), live)
        _, vjp = jax.vjp(_rwkv_head, *cur, *prev, st[...], pp[...], ml[...], w[...])
        g = vjp((_chunk_batch([_lane_block(dy, h) for h in range(RWKV_HB)]), ds_scr[hsl]))
        outs = (dpr, dpk, dpv)
        d_cur = [_head_tiles(g[j], RWKV_HB) for j in range(3)]
        d_prev = [_head_tiles(g[4 + j], RWKV_HB) for j in range(3)]
        for i in range(RWKV_HB):
            sl = slice(i * LANES, (i + 1) * LANES)
            h = _head_id(grp, i, RWKV_HB, RWKV_HEADS)
            car = car_scr[h]
            for j in range(3):
                tot = d_cur[j][i] + _unshift_rows(d_prev[j][i], car[SUBLANES * j:SUBLANES * (j + 1), :], 1)
                outs[j][:, sl] = tot.astype(outs[j].dtype)
                car_scr[h, SUBLANES * j:SUBLANES * (j + 1), :] = d_prev[j][i][0:SUBLANES, :]
        (dlo_cur,), (dlo_prev,) = _head_tiles(g[3], 1), _head_tiles(g[7], 1)
        dlo = dlo_cur + _unshift_rows(dlo_prev, carlo_scr[gi], 1)
        carlo_scr[gi] = dlo_prev[0:SUBLANES, :]
        ds_scr[hsl] = g[8]
        dpp[hsl] += g[9]
        dml[0, 0:1, :] += g[10]
        dw[hsl] += g[11]

        @pl.when(grp == 0)
        def _():
            dplo[...] = dlo

        @pl.when(grp > 0)
        def _():
            dplo[...] += dlo

    rev = lambda n: last - n
    in_specs = _rwkv_specs(rev) + [
        pl.BlockSpec((None, RWKV_HB, LANES, LANES), lambda n, g: (rev(n), g, 0, 0)),
        pl.BlockSpec((RWKV_TILE, RWKV_HB * LANES), lambda n, g: (rev(n), g)),
    ]
    hw = RWKV_HEADS * LANES
    return pl.pallas_call(
        body, name="rwkv_bwd", grid=(n_chunks, RWKV_HEADS // RWKV_HB),
        in_specs=in_specs,
        out_specs=(pl.BlockSpec((RWKV_TILE, RWKV_HB * LANES), lambda n, g: (rev(n), g)),
                   pl.BlockSpec((RWKV_TILE, RWKV_HB * LANES), lambda n, g: (rev(n), g)),
                   pl.BlockSpec((RWKV_TILE, RWKV_HB * LANES), lambda n, g: (rev(n), g)),
                   pl.BlockSpec((RWKV_TILE, 2 * LANES), lambda n, h: (rev(n), 0)),
                   pl.BlockSpec((RWKV_HEADS, 16, LANES), lambda n, h: (0, 0, 0)),
                   pl.BlockSpec((RWKV_HEADS, SUBLANES, 2 * LANES), lambda n, h: (0, 0, 0)),
                   pl.BlockSpec((RWKV_HEADS, 3, 2 * LANES, LANES), lambda n, h: (0, 0, 0, 0))),
        out_shape=(jax.ShapeDtypeStruct((t, hw), BF16), jax.ShapeDtypeStruct((t, hw), BF16), jax.ShapeDtypeStruct((t, hw), BF16),
                   jax.ShapeDtypeStruct((t, 2 * LANES), F32),
                   jax.ShapeDtypeStruct((RWKV_HEADS, 16, LANES), F32),
                   jax.ShapeDtypeStruct((RWKV_HEADS, SUBLANES, 2 * LANES), F32),
                   jax.ShapeDtypeStruct((RWKV_HEADS, 3, 2 * LANES, LANES), F32)),
        scratch_shapes=[pltpu.VMEM((RWKV_HEADS, LANES, LANES), F32),
                        pltpu.VMEM((RWKV_HEADS, 3 * SUBLANES, LANES), F32),
                        pltpu.VMEM((RWKV_HEADS, SUBLANES, 2 * LANES), F32)],
        compiler_params=_cparams(("arbitrary", "arbitrary")),
    )(p_cat, p_cat, p_cat, p_cat, p_cat, p_cat, p_cat, p_cat, ppack, mulo, wl, states, d_out)


def _gdn_specs(nmap):
    per = GDN_TILE // SUBLANES
    hb, groups = GDN_HB, GDN_HEADS // GDN_HB
    cb = OFF_QKV // (hb * LANES)
    specs = []
    for j in range(3):
        specs.append(pl.BlockSpec((GDN_TILE, hb * LANES), lambda n, g, j=j: (nmap(n), cb + j * groups + g)))
    for j in range(3):
        specs.append(pl.BlockSpec((SUBLANES, hb * LANES),
                                  lambda n, g, j=j: (jnp.maximum(nmap(n) * per - 1, 0), cb + j * groups + g)))
    specs.append(pl.BlockSpec((GDN_TILE, hb * LANES), lambda n, g: (nmap(n), OFF_Z // (hb * LANES) + g)))
    specs.append(pl.BlockSpec((GDN_TILE, LANES), lambda n, g: (nmap(n), OFF_AB // LANES)))
    specs.append(pl.BlockSpec((hb, 3, SUBLANES, LANES), lambda n, g: (g, 0, 0, 0)))
    specs.append(pl.BlockSpec((SUBLANES, LANES), lambda n, g: (0, 0)))
    return specs


def _conv_taps(x, halo):
    return (x,) + tuple(_shift_rows(x, halo, s) for s in range(1, GDN_CONV))


def _onehots(grp):
    nb = GDN_STEP_CHUNKS * GDN_HB
    lane = lax.broadcasted_iota(jnp.int32, (nb, 1, LANES), 2)
    head = lax.broadcasted_iota(jnp.int32, (nb, 1, LANES), 0) % GDN_HB + _head_id(grp, 0, GDN_HB, GDN_HEADS)
    return (lane == head).astype(F32), (lane == GDN_HEADS + head).astype(F32)


def _chunk_batch(tiles):
    n_chunks = tiles[0].shape[0] // CHUNK
    return jnp.stack([t_[i * CHUNK:(i + 1) * CHUNK, :] for i in range(n_chunks) for t_ in tiles])


def _head_tiles(batch, n_heads=GDN_HB):
    n_chunks = batch.shape[0] // n_heads
    return [jnp.concatenate([batch[i * n_heads + h] for i in range(n_chunks)], axis=0) for h in range(n_heads)]


def _lane_block(ref, h):
    return ref[:, h * LANES:(h + 1) * LANES]


def _gdn_taps(refs, halos, live):
    out = []
    for x, hx in zip(refs, halos):
        per_head = [_conv_taps(_lane_block(x, h), _lane_block(hx, h) * live) for h in range(GDN_HB)]
        out.append(tuple(_chunk_batch([per_head[h][s] for h in range(GDN_HB)]) for s in range(GDN_CONV)))
    return out


def _gdn_forward(p_cat, cwpack, gpar):
    t = p_cat.shape[0]
    n_chunks = t // GDN_TILE

    def body(xq, xk, xv, hq, hk, hv, z, ab, cw, gp, out, st_out, s_scr):
        n, grp = pl.program_id(0), pl.program_id(1)

        hsl = _head_range(grp, GDN_HB, GDN_HEADS)

        @pl.when(n == 0)
        def _():
            s_scr[hsl] = jnp.zeros((GDN_HB, LANES, LANES), F32)

        live = (n > 0).astype(F32)
        oha, ohb = _onehots(grp)
        s0 = s_scr[hsl]
        st_out[...] = s0
        taps = _gdn_taps((xq, xk, xv), (hq, hk, hv), live)
        zb = _chunk_batch([_lane_block(z, h) for h in range(GDN_HB)])
        abb = _chunk_batch([ab[...]] * GDN_HB)
        o, s1 = _gdn_head(*taps, zb, abb, s0, cw[...], gp[...], oha, ohb)
        for h, tile in enumerate(_head_tiles(o)):
            out[:, h * LANES:(h + 1) * LANES] = tile.astype(out.dtype)
        s_scr[hsl] = s1

    return pl.pallas_call(
        body, name="gdn_fwd", grid=(n_chunks, GDN_HEADS // GDN_HB),
        in_specs=_gdn_specs(lambda n: n),
        out_specs=(pl.BlockSpec((GDN_TILE, GDN_HB * LANES), lambda n, g: (n, g)),
                   pl.BlockSpec((None, GDN_HB, LANES, LANES), lambda n, g: (n, g, 0, 0))),
        out_shape=(jax.ShapeDtypeStruct((t, GDN_WIDTH), BF16),
                   jax.ShapeDtypeStruct((n_chunks, GDN_HEADS, LANES, LANES), F32)),
        scratch_shapes=[pltpu.VMEM((GDN_HEADS, LANES, LANES), F32)],
        compiler_params=_cparams(("arbitrary", "arbitrary")),
    )(p_cat, p_cat, p_cat, p_cat, p_cat, p_cat, p_cat, p_cat, cwpack, gpar)


def _gdn_backward(p_cat, cwpack, gpar, states, d_out):
    t = p_cat.shape[0]
    n_chunks = t // GDN_TILE
    last = n_chunks - 1

    def body(xq, xk, xv, hq, hk, hv, z, ab, cw, gp, st, dy, dq, dk, dv, dz, dab, dcw, dgp, ds_scr, car_scr):
        n, grp = pl.program_id(0), pl.program_id(1)

        hsl = _head_range(grp, GDN_HB, GDN_HEADS)

        @pl.when(n == 0)
        def _():
            ds_scr[hsl] = jnp.zeros((GDN_HB, LANES, LANES), F32)
            car_scr[hsl] = jnp.zeros((GDN_HB, 3 * GDN_CONV, SUBLANES, LANES), F32)

        @pl.when((n == 0) & (grp == 0))
        def _():
            dcw[...] = jnp.zeros(dcw.shape, F32)
            dgp[...] = jnp.zeros(dgp.shape, F32)

        live = (n < last).astype(F32)
        oha, ohb = _onehots(grp)
        fn = functools.partial(_gdn_head, oha=oha, ohb=ohb)
        taps = _gdn_taps((xq, xk, xv), (hq, hk, hv), live)
        zb = _chunk_batch([_lane_block(z, h) for h in range(GDN_HB)])
        abb = _chunk_batch([ab[...]] * GDN_HB)
        _, vjp = jax.vjp(fn, *taps, zb, abb, st[...], cw[...], gp[...])
        g = vjp((_chunk_batch([_lane_block(dy, h) for h in range(GDN_HB)]), ds_scr[hsl]))
        outs = (dq, dk, dv)
        tap_tiles = [[_head_tiles(g[j][s]) for s in range(GDN_CONV)] for j in range(3)]
        dz_tiles = _head_tiles(g[3])
        for i in range(GDN_HB):
            sl = slice(i * LANES, (i + 1) * LANES)
            h = _head_id(grp, i, GDN_HB, GDN_HEADS)
            for j in range(3):
                tot = tap_tiles[j][0][i]
                for s in range(1, GDN_CONV):
                    slot = j * GDN_CONV + s
                    tot = tot + _unshift_rows(tap_tiles[j][s][i], car_scr[h, slot], s)
                    car_scr[h, slot] = tap_tiles[j][s][i][0:SUBLANES, :]
                outs[j][:, sl] = tot.astype(outs[j].dtype)
            dz[:, sl] = dz_tiles[i].astype(dz.dtype)
        dab_tiles = _head_tiles(g[4])
        dab_sum = dab_tiles[0]
        for h in range(1, GDN_HB):
            dab_sum = dab_sum + dab_tiles[h]
        ds_scr[hsl] = g[5]
        dcw[hsl] += g[6]
        dgp[0] += g[7]

        @pl.when(grp == 0)
        def _():
            dab[...] = dab_sum

        @pl.when(grp > 0)
        def _():
            dab[...] += dab_sum

    rev = lambda n: last - n
    in_specs = _gdn_specs(rev) + [
        pl.BlockSpec((None, GDN_HB, LANES, LANES), lambda n, g: (rev(n), g, 0, 0)),
        pl.BlockSpec((GDN_TILE, GDN_HB * LANES), lambda n, g: (rev(n), g)),
    ]
    blk = pl.BlockSpec((GDN_TILE, GDN_HB * LANES), lambda n, g: (rev(n), g))
    return pl.pallas_call(
        body, name="gdn_bwd", grid=(n_chunks, GDN_HEADS // GDN_HB),
        in_specs=in_specs,
        out_specs=(blk, blk, blk, blk,
                   pl.BlockSpec((GDN_TILE, LANES), lambda n, h: (rev(n), 0)),
                   pl.BlockSpec((GDN_HEADS, 3, SUBLANES, LANES), lambda n, h: (0, 0, 0, 0)),
                   pl.BlockSpec((GDN_HEADS, SUBLANES, LANES), lambda n, h: (0, 0, 0))),
        out_shape=(jax.ShapeDtypeStruct((t, GDN_WIDTH), BF16), jax.ShapeDtypeStruct((t, GDN_WIDTH), BF16),
                   jax.ShapeDtypeStruct((t, GDN_WIDTH), BF16), jax.ShapeDtypeStruct((t, GDN_WIDTH), BF16),
                   jax.ShapeDtypeStruct((t, LANES), F32),
                   jax.ShapeDtypeStruct((GDN_HEADS, 3, SUBLANES, LANES), F32),
                   jax.ShapeDtypeStruct((GDN_HEADS, SUBLANES, LANES), F32)),
        scratch_shapes=[pltpu.VMEM((GDN_HEADS, LANES, LANES), F32),
                        pltpu.VMEM((GDN_HEADS, 3 * GDN_CONV, SUBLANES, LANES), F32)],
        compiler_params=_cparams(("arbitrary", "arbitrary")),
    )(p_cat, p_cat, p_cat, p_cat, p_cat, p_cat, p_cat, p_cat, cwpack, gpar, states, d_out)


MM_VMEM_BUDGET = 44 * 1024 * 1024
MM_MIN_STEPS = 4


def _mm_tiles(mode, m, n, k, out_bytes):
    tms = [t for t in (2048, 1024, 768, 512, 256, 128, 64) if m % t == 0 and (mode != 'tn' or t % LANES == 0)]
    tns = [t for t in (1408, 1024, 768, 512, 256, 128) if n % t == 0]
    tks = [t for t in (2048, 1920, 1408, 1024, 512, 256, 128, 64) if k % t == 0]
    best, best_key = None, None
    for tm in tms:
        for tn in tns:
            for tk in tks:
                nk = k // tk
                vmem = 2 * (tm * tk * 2 + tk * tn * 2 + tm * tn * out_bytes) + (tm * tn * 4 if nk > 1 else 0)
                steps = (m // tm) * (n // tn) * nk
                if vmem > MM_VMEM_BUDGET:
                    continue
                key = (steps >= MM_MIN_STEPS, tn if mode == 'tn' else 0, tm * tn * tk, -nk)
                if best_key is None or key > best_key:
                    best, best_key = (tm, tn, tk), key
    if best is None:
        raise ValueError(f"no matmul tile for {mode} {m}x{n}x{k}")
    return best


_MM_DIMS = {'nn': (((1,), (0,)), ((), ())), 'nt': (((1,), (1,)), ((), ())), 'tn': (((0,), (0,)), ((), ()))}


def _matmul(a, b, mode, out_dtype, name):
    if mode == 'nn':
        (m, k), (k2, n) = a.shape, b.shape
    elif mode == 'nt':
        (m, k), (n, k2) = a.shape, b.shape
    else:
        (k, m), (k2, n) = a.shape, b.shape
    assert k == k2, (a.shape, b.shape, mode)
    tm, tn, tk = _mm_tiles(mode, m, n, k, jnp.dtype(out_dtype).itemsize)
    nk = k // tk
    dims = _MM_DIMS[mode]

    def body(a_ref, b_ref, o_ref, acc_ref):
        kk = pl.program_id(2)
        part = lax.dot_general(a_ref[...], b_ref[...], dims, preferred_element_type=F32)
        if nk == 1:
            o_ref[...] = part.astype(o_ref.dtype)
            return

        @pl.when(kk == 0)
        def _():
            acc_ref[...] = part

        @pl.when((kk > 0) & (kk < nk - 1))
        def _():
            acc_ref[...] += part

        @pl.when(kk == nk - 1)
        def _():
            o_ref[...] = (acc_ref[...] + part).astype(o_ref.dtype)

    a_spec = pl.BlockSpec((tk, tm), lambda i, j, kk: (kk, i)) if mode == 'tn' else pl.BlockSpec((tm, tk), lambda i, j, kk: (i, kk))
    b_spec = pl.BlockSpec((tn, tk), lambda i, j, kk: (j, kk)) if mode == 'nt' else pl.BlockSpec((tk, tn), lambda i, j, kk: (kk, j))
    return pl.pallas_call(
        body, name=name, grid=(m // tm, n // tn, nk),
        in_specs=[a_spec, b_spec],
        out_specs=pl.BlockSpec((tm, tn), lambda i, j, kk: (i, j)),
        out_shape=jax.ShapeDtypeStruct((m, n), out_dtype),
        scratch_shapes=[pltpu.VMEM((tm, tn), F32)],
        compiler_params=_cparams(("parallel", "parallel", "arbitrary")),
    )(a, b)


ROW_TILE = 256


def _row_specs(rows, tm):
    return [pl.BlockSpec((tm, w), lambda i, ci=ci: (i, ci)) for (_, w, ci) in rows]


def _rw_forward(fn, rows, pars, outs, name):
    t = rows[0][0].shape[0]
    tm = min(ROW_TILE, t)
    nr, npar = len(rows), len(pars)

    def body(*refs):
        vals = [r[...].astype(F32) for r in refs[:nr]] + [p[...] for p in refs[nr:nr + npar]]
        res = fn(*vals)
        for o, v in zip(refs[nr + npar:], res):
            o[...] = v.astype(o.dtype)

    return pl.pallas_call(
        body, name=name, grid=(t // tm,),
        in_specs=_row_specs(rows, tm) + [pl.BlockSpec(p.shape, lambda i: (0, 0)) for p in pars],
        out_specs=tuple(pl.BlockSpec((tm, w), lambda i: (i, 0)) for (w, _) in outs),
        out_shape=tuple(jax.ShapeDtypeStruct((t, w), dt) for (w, dt) in outs),
        compiler_params=_cparams(("parallel",)),
    )(*[r[0] for r in rows], *pars)


def _rw_backward(fn, rows, pars, cots, drow_dtypes, name):
    t = rows[0][0].shape[0]
    tm = min(ROW_TILE, t)
    nr, npar, nc = len(rows), len(pars), len(cots)
    keep = [i for i, dt in enumerate(drow_dtypes) if dt is not None]

    def body(*refs):
        vals = [r[...].astype(F32) for r in refs[:nr]] + [p[...] for p in refs[nr:nr + npar]]
        cvals = tuple(c[...].astype(F32) for c in refs[nr + npar:nr + npar + nc])
        orefs = refs[nr + npar + nc:]
        _, vjp = jax.vjp(fn, *vals)
        g = vjp(cvals)
        for o, i in zip(orefs[:len(keep)], keep):
            o[...] = g[i].astype(o.dtype)
        first = pl.program_id(0) == 0
        for o, gi in zip(orefs[len(keep):], g[nr:]):
            @pl.when(first)
            def _(o=o, gi=gi):
                o[...] = gi

            @pl.when(jnp.logical_not(first))
            def _(o=o, gi=gi):
                o[...] += gi

    out_specs = [pl.BlockSpec((tm, rows[i][1]), lambda i_: (i_, 0)) for i in keep] + \
                [pl.BlockSpec(p.shape, lambda i_: (0, 0)) for p in pars]
    out_shape = [jax.ShapeDtypeStruct((t, rows[i][1]), drow_dtypes[i]) for i in keep] + \
                [jax.ShapeDtypeStruct(p.shape, F32) for p in pars]
    return pl.pallas_call(
        body, name=name, grid=(t // tm,),
        in_specs=_row_specs(rows, tm) + [pl.BlockSpec(p.shape, lambda i: (0, 0)) for p in pars] + _row_specs(cots, tm),
        out_specs=tuple(out_specs), out_shape=tuple(out_shape),
        compiler_params=_cparams(("arbitrary",)),
    )(*[r[0] for r in rows], *pars, *[c[0] for c in cots])


def _norm_fn(x, g):
    return (_rms(x, g),)


def _norm_skip_fn(x, g):
    return _rms(x, g), x


def _merge_fn(ga, gb, ya, yb):
    return (_sigmoid(ga) * ya + _sigmoid(gb) * yb,)


def _res_norm_fn(x, mo, g):
    x1 = x + mo
    return x1, _rms(x1, g)


def _loss_head(x1, fo, gf, target, name):
    t, d = x1.shape
    tm = min(ROW_TILE, t)

    def tile_loss(x2, g, tgt):
        err = _rms(x2, g) - tgt
        per_row = jnp.sum(err * err, axis=-1, keepdims=True) * (0.5 / d)
        return jnp.sum(per_row, axis=0, keepdims=True)

    def body(x1_ref, fo_ref, g_ref, t_ref, loss_ref, dx_ref, dxb_ref, dg_ref):
        x2 = x1_ref[...] + fo_ref[...]
        val, vjp = jax.vjp(functools.partial(tile_loss, tgt=t_ref[...]), x2, g_ref[...])
        dx2, dg = vjp(jnp.ones((1, 1), F32))
        dx_ref[...] = dx2
        dxb_ref[...] = dx2.astype(BF16)
        first = pl.program_id(0) == 0

        @pl.when(first)
        def _():
            loss_ref[...] = jnp.broadcast_to(val, loss_ref.shape)
            dg_ref[...] = dg

        @pl.when(jnp.logical_not(first))
        def _():
            loss_ref[...] += jnp.broadcast_to(val, loss_ref.shape)
            dg_ref[...] += dg

    row = pl.BlockSpec((tm, d), lambda i: (i, 0))
    vec = pl.BlockSpec((1, d), lambda i: (0, 0))
    return pl.pallas_call(
        body, name=name, grid=(t // tm,),
        in_specs=[row, row, vec, row],
        out_specs=(pl.BlockSpec((1, LANES), lambda i: (0, 0)), row, row, vec),
        out_shape=(jax.ShapeDtypeStruct((1, LANES), F32), jax.ShapeDtypeStruct((t, d), F32),
                   jax.ShapeDtypeStruct((t, d), BF16), jax.ShapeDtypeStruct((1, d), F32)),
        compiler_params=_cparams(("arbitrary",)),
    )(x1, fo, gf, target)


FFN_TILE_ROWS = 512
FFN_TILE_COLS = 256
FFN_COL_BLOCKS = FFN_HIDDEN // FFN_TILE_COLS


def _conv3_past(x, halo, w):
    rows = lax.broadcasted_iota(jnp.int32, x.shape, 0)
    x1 = jnp.where(rows == 0, halo[7:8, :], pltpu.roll(x, 1, 0))
    x2 = jnp.where(rows == 0, halo[6:7, :], jnp.where(rows == 1, halo[7:8, :], pltpu.roll(x, 2, 0)))
    return x * w[2:3] + x1 * w[1:2] + x2 * w[0:1], x1, x2


def _ffn_in_specs(tm, imap, jmap):
    per = tm // SUBLANES
    tile = lambda off: pl.BlockSpec((tm, FFN_TILE_COLS), lambda *g: (imap(*g), off + jmap(*g) % FFN_COL_BLOCKS))
    halo = lambda off: pl.BlockSpec((SUBLANES, FFN_TILE_COLS),
                                    lambda *g: (jnp.maximum(imap(*g) * per - 1, 0), off + jmap(*g) % FFN_COL_BLOCKS))
    wsp = lambda off: pl.BlockSpec((FFN_CONV, FFN_TILE_COLS), lambda *g: (0, off + jmap(*g) % FFN_COL_BLOCKS))
    return [tile(0), halo(0), wsp(0), tile(FFN_COL_BLOCKS), halo(FFN_COL_BLOCKS), wsp(FFN_COL_BLOCKS)]


def _ffn_act_forward(hpre, cw):
    t = hpre.shape[0]
    tm = min(FFN_TILE_ROWS, t)

    def body(hg, pg, wg, hu, pu, wu, out):
        live = (pl.program_id(0) > 0).astype(F32)
        cg, _, _ = _conv3_past(hg[...], pg[...] * live, wg[...])
        cu, _, _ = _conv3_past(hu[...], pu[...] * live, wu[...])
        out[...] = (cg * _sigmoid_plain(cg) * cu).astype(out.dtype)

    return pl.pallas_call(
        body, name="ffn_act_fwd", grid=(t // tm, FFN_COL_BLOCKS),
        in_specs=_ffn_in_specs(tm, lambda i, j: i, lambda i, j: j),
        out_specs=pl.BlockSpec((tm, FFN_TILE_COLS), lambda i, j: (i, j)),
        out_shape=jax.ShapeDtypeStruct((t, FFN_HIDDEN), BF16),
        compiler_params=_cparams(("parallel", "parallel")),
    )(hpre, hpre, cw, hpre, hpre, cw)


def _conv3_future(d, nxt, w):
    tm = d.shape[0]
    rows = lax.broadcasted_iota(jnp.int32, d.shape, 0)
    d1 = jnp.where(rows == tm - 1, nxt[0:1, :], pltpu.roll(d, tm - 1, 0))
    d2 = jnp.where(rows == tm - 1, nxt[1:2, :], jnp.where(rows == tm - 2, nxt[0:1, :], pltpu.roll(d, tm - 2, 0)))
    return d * w[2:3] + d1 * w[1:2] + d2 * w[0:1]


def _ffn_backward(hpre, cw, dact):
    t = hpre.shape[0]
    tm = min(FFN_TILE_ROWS, t)
    n_tiles = t // tm
    per = tm // SUBLANES

    def d_conv_out(cg, cu, d):
        s = _sigmoid_plain(cg)
        return d * cu * s * (1.0 + cg * (1.0 - s)), d * cg * s

    def body(hg, pg, ng, wg, hu, pu, nu, wu, da, dan, dhg, dhu, dwg, dwu):
        i = pl.program_id(1)
        live_prev = (i > 0).astype(F32)
        live_next = (i < n_tiles - 1).astype(F32)
        xg, xu = hg[...], hu[...]
        cg, g1, g2 = _conv3_past(xg, pg[...] * live_prev, wg[...])
        cu, u1, u2 = _conv3_past(xu, pu[...] * live_prev, wu[...])
        dg, du = d_conv_out(cg, cu, da[...])
        cgn, _, _ = _conv3_past(ng[...], hg[tm - SUBLANES:tm, :], wg[...])
        cun, _, _ = _conv3_past(nu[...], hu[tm - SUBLANES:tm, :], wu[...])
        dgn, dun = d_conv_out(cgn, cun, dan[...] * live_next)
        dhg[...] = _conv3_future(dg, dgn, wg[...]).astype(dhg.dtype)
        dhu[...] = _conv3_future(du, dun, wu[...]).astype(dhu.dtype)
        sums_g = [jnp.sum(xs * dg, axis=0, keepdims=True) for xs in (g2, g1, xg)]
        sums_u = [jnp.sum(xs * du, axis=0, keepdims=True) for xs in (u2, u1, xu)]

        @pl.when(i == 0)
        def _():
            for r_ in range(FFN_CONV):
                dwg[r_:r_ + 1, :] = sums_g[r_]
                dwu[r_:r_ + 1, :] = sums_u[r_]

        @pl.when(i > 0)
        def _():
            for r_ in range(FFN_CONV):
                dwg[r_:r_ + 1, :] += sums_g[r_]
                dwu[r_:r_ + 1, :] += sums_u[r_]

    nb = FFN_COL_BLOCKS
    nxt = lambda i: jnp.minimum((i + 1) * per, t // SUBLANES - 1)
    prv = lambda i: jnp.maximum(i * per - 1, 0)
    half = lambda off: [pl.BlockSpec((tm, FFN_TILE_COLS), lambda j, i: (i, off + j)),
                        pl.BlockSpec((SUBLANES, FFN_TILE_COLS), lambda j, i: (prv(i), off + j)),
                        pl.BlockSpec((SUBLANES, FFN_TILE_COLS), lambda j, i: (nxt(i), off + j)),
                        pl.BlockSpec((FFN_CONV, FFN_TILE_COLS), lambda j, i: (0, off + j))]
    tile = pl.BlockSpec((tm, FFN_TILE_COLS), lambda j, i: (i, j))
    taps = pl.BlockSpec((FFN_CONV, FFN_TILE_COLS), lambda j, i: (0, j))
    return pl.pallas_call(
        body, name="ffn_bwd", grid=(nb, n_tiles),
        in_specs=half(0) + half(nb) + [tile, pl.BlockSpec((SUBLANES, FFN_TILE_COLS), lambda j, i: (nxt(i), j))],
        out_specs=(tile, tile, taps, taps),
        out_shape=(jax.ShapeDtypeStruct((t, FFN_HIDDEN), BF16), jax.ShapeDtypeStruct((t, FFN_HIDDEN), BF16),
                   jax.ShapeDtypeStruct((FFN_CONV, FFN_HIDDEN), F32), jax.ShapeDtypeStruct((FFN_CONV, FFN_HIDDEN), F32)),
        compiler_params=_cparams(("parallel", "arbitrary")),
    )(hpre, hpre, hpre, cw, hpre, hpre, hpre, cw, dact, dact)


def _my_place():
    x, y, c = lax.axis_index("x"), lax.axis_index("y"), lax.axis_index("c")
    return x, y, c, 4 * x + 2 * y + c


N_CHIPS = 4


def _remote(src, dst, send_sem, recv_sem, dev):
    return pltpu.make_async_remote_copy(src_ref=src, dst_ref=dst, send_sem=send_sem, recv_sem=recv_sem, device_id=dev,
                                        device_id_type=pl.DeviceIdType.MESH)


def _chip_peer(x, y, k):
    return x ^ ((k >> 1) & 1), y ^ (k & 1)


def _all_gather_two_level(shard, name):
    r, w = shard.shape

    def body(src, out, send_sems, recv_sems, local_sem):
        x, y, c, me = _my_place()
        sibling = (x, y, 1 - c)
        mine = pltpu.make_async_copy(src, out.at[me], local_sem)
        mine.start()
        first = [_remote(src, out.at[me], send_sems.at[0], recv_sems.at[0], sibling)]
        for k in range(1, N_CHIPS):
            px, py = _chip_peer(x, y, k)
            first.append(_remote(src, out.at[me], send_sems.at[k], recv_sems.at[k], (px, py, c)))
        for cp in first:
            cp.start()
        passed = []
        for k in range(1, N_CHIPS):
            px, py = _chip_peer(x, y, k)
            landed = out.at[me ^ (2 * k)]
            _remote(src, landed, send_sems.at[k], recv_sems.at[k], (px, py, c)).wait_recv()
            fwd = _remote(landed, landed, send_sems.at[N_CHIPS - 1 + k], recv_sems.at[N_CHIPS - 1 + k], sibling)
            fwd.start()
            passed.append(fwd)
        _remote(src, out.at[me ^ 1], send_sems.at[0], recv_sems.at[0], sibling).wait_recv()
        for k in range(1, N_CHIPS):
            got = out.at[(me ^ 1) ^ (2 * k)]
            _remote(got, got, send_sems.at[N_CHIPS - 1 + k], recv_sems.at[N_CHIPS - 1 + k], sibling).wait_recv()
        for cp in first + passed:
            cp.wait_send()
        mine.wait()

    return pl.pallas_call(
        body, name=name,
        in_specs=[pl.BlockSpec(memory_space=pl.ANY)],
        out_specs=pl.BlockSpec(memory_space=pl.ANY),
        out_shape=jax.ShapeDtypeStruct((N_DEV, r, w), shard.dtype),
        scratch_shapes=[pltpu.SemaphoreType.DMA((N_DEV - 1,)), pltpu.SemaphoreType.DMA((N_DEV - 1,)), pltpu.SemaphoreType.DMA],
    )(shard)


def _device_peer(x, y, c, k):
    px, py, pc = x ^ ((k >> 2) & 1), y ^ ((k >> 1) & 1), c ^ (k & 1)
    return (px, py, pc), 4 * px + 2 * py + pc


_HBM = pl.BlockSpec(memory_space=pltpu.HBM)
_SEM = pl.BlockSpec(memory_space=pltpu.SEMAPHORE)


def _gather_start(shard, name):
    def body(src, land, send_sems, recv_sems, src_thru, land_thru, token):
        x, y, c, me = _my_place()
        for k in range(1, N_DEV):
            dev, _ = _device_peer(x, y, c, k)
            _remote(src, land.at[me], send_sems.at[k], recv_sems.at[k], dev).start()
        token[...] = jnp.zeros_like(token)

    landing = lax.empty((N_DEV,) + shard.shape, shard.dtype)
    return pl.pallas_call(
        body, name=name,
        out_shape=(pltpu.SemaphoreType.DMA((N_DEV,)), pltpu.SemaphoreType.DMA((N_DEV,)), pltpu.HBM(shard.shape, shard.dtype),
                   pltpu.HBM(landing.shape, landing.dtype), jax.ShapeDtypeStruct((SUBLANES, LANES), F32)),
        in_specs=(_HBM, _HBM), out_specs=(_SEM, _SEM, _HBM, _HBM, pl.BlockSpec(memory_space=pltpu.VMEM)),
        input_output_aliases={0: 2, 1: 3},
        compiler_params=pltpu.CompilerParams(has_side_effects=pltpu.SideEffectType.DATAFLOW_SIDE_EFFECTING),
    )(pltpu.with_memory_space_constraint(shard, pltpu.HBM), pltpu.with_memory_space_constraint(landing, pltpu.HBM))


def _gather_wait(send_sems, recv_sems, shard, landing, after, name):
    n_after = len(after)

    def body(*refs):
        src, land, send_sems, recv_sems = refs[:4]
        x, y, c, _ = _my_place()
        for k in range(1, N_DEV):
            dev, idx = _device_peer(x, y, c, k)
            cp = _remote(src, land.at[idx], send_sems.at[k], recv_sems.at[k], dev)
            cp.wait_send()
            cp.wait_recv()

    return pl.pallas_call(
        body, name=name,
        out_shape=(pltpu.HBM(shard.shape, shard.dtype), pltpu.HBM(landing.shape, landing.dtype)),
        in_specs=(_HBM, _HBM, _SEM, _SEM) + (pl.BlockSpec(memory_space=pl.ANY),) * n_after, out_specs=(_HBM, _HBM),
        input_output_aliases={0: 0, 1: 1},
        compiler_params=pltpu.CompilerParams(has_side_effects=pltpu.SideEffectType.DATAFLOW_SIDE_EFFECTING),
    )(shard, landing, send_sems, recv_sems, *after)[1]


def _slab_push_start(slabs, name):
    na = len(slabs)

    def body(*refs):
        srcs, lands = refs[:na], refs[na:2 * na]
        send_sems, recv_sems = refs[2 * na], refs[2 * na + 1]
        token = refs[-1]
        x, y, c, me = _my_place()
        for i in range(na):
            for k in range(1, N_DEV):
                dev, idx = _device_peer(x, y, c, k)
                s = i * N_DEV + k
                _remote(srcs[i].at[idx], lands[i].at[me], send_sems.at[s], recv_sems.at[s], dev).start()
        token[...] = jnp.zeros_like(token)

    hbm_shapes = [pltpu.HBM(a.shape, a.dtype) for a in slabs]
    ins = [pltpu.with_memory_space_constraint(a, pltpu.HBM) for a in slabs]
    ins += [pltpu.with_memory_space_constraint(lax.empty(a.shape, a.dtype), pltpu.HBM) for a in slabs]
    out = pl.pallas_call(
        body, name=name,
        out_shape=(pltpu.SemaphoreType.DMA((na * N_DEV,)), pltpu.SemaphoreType.DMA((na * N_DEV,)), *hbm_shapes, *hbm_shapes,
                   jax.ShapeDtypeStruct((SUBLANES, LANES), F32)),
        in_specs=(_HBM,) * (2 * na), out_specs=(_SEM, _SEM) + (_HBM,) * (2 * na) + (pl.BlockSpec(memory_space=pltpu.VMEM),),
        input_output_aliases={i: 2 + i for i in range(2 * na)},
        compiler_params=pltpu.CompilerParams(has_side_effects=pltpu.SideEffectType.DATAFLOW_SIDE_EFFECTING),
    )(*ins)
    return out[0], out[1], list(out[2:2 + na]), list(out[2 + na:2 + 2 * na]), out[-1]


def _slab_push_wait(send_sems, recv_sems, slabs, landings, after, name):
    na = len(slabs)

    def body(*refs):
        srcs, lands = refs[:na], refs[na:2 * na]
        send_sems, recv_sems = refs[2 * na], refs[2 * na + 1]
        x, y, c, me = _my_place()
        for i in range(na):
            for k in range(1, N_DEV):
                dev, idx = _device_peer(x, y, c, k)
                s = i * N_DEV + k
                cp = _remote(srcs[i].at[idx], lands[i].at[idx], send_sems.at[s], recv_sems.at[s], dev)
                cp.wait_send()
                cp.wait_recv()

    hbm_shapes = tuple(pltpu.HBM(a.shape, a.dtype) for a in slabs)
    out = pl.pallas_call(
        body, name=name, out_shape=hbm_shapes + hbm_shapes,
        in_specs=(_HBM,) * (2 * na) + (_SEM, _SEM) + (pl.BlockSpec(memory_space=pl.ANY),) * len(after),
        out_specs=(_HBM,) * (2 * na), input_output_aliases={i: i for i in range(2 * na)},
        compiler_params=pltpu.CompilerParams(has_side_effects=pltpu.SideEffectType.DATAFLOW_SIDE_EFFECTING),
    )(*slabs, *landings, send_sems, recv_sems, *after)
    return list(out[na:])


def _pair_exchange(arrays, name):
    na = len(arrays)

    def body(*refs):
        srcs, dsts, (send_sems, recv_sems) = refs[:na], refs[na:2 * na], refs[2 * na:]
        x, y, c, _ = _my_place()
        sibling = (x, y, 1 - c)
        copies = []
        for i in range(na):
            for q in range(N_CHIPS):
                s = i * N_CHIPS + q
                copies.append(_remote(srcs[i].at[2 * q + 1 - c], dsts[i].at[q], send_sems.at[s], recv_sems.at[s], sibling))
        for cp in copies:
            cp.start()
        for cp in copies:
            cp.wait_recv()
        for cp in copies:
            cp.wait_send()

    hbm = pl.BlockSpec(memory_space=pl.ANY)
    return pl.pallas_call(
        body, name=name, in_specs=[hbm] * na, out_specs=tuple([hbm] * na),
        out_shape=tuple(jax.ShapeDtypeStruct((N_CHIPS,) + a.shape[1:], a.dtype) for a in arrays),
        scratch_shapes=[pltpu.SemaphoreType.DMA((na * N_CHIPS,)), pltpu.SemaphoreType.DMA((na * N_CHIPS,))],
    )(*arrays)


ELEMENTWISE_COLS = 256


def _pair_sum(slabs, recv, core, out_dtype, name):
    _, r, w = slabs.shape
    tc = ELEMENTWISE_COLS

    def body(core_ref, mine, theirs, out):
        out[...] = (mine[...] + theirs[...]).astype(out.dtype)

    grid_spec = pltpu.PrefetchScalarGridSpec(
        num_scalar_prefetch=1, grid=(N_CHIPS, w // tc),
        in_specs=[pl.BlockSpec((None, r, tc), lambda q, j, core_ref: (2 * q + core_ref[0], 0, j)),
                  pl.BlockSpec((None, r, tc), lambda q, j, core_ref: (q, 0, j))],
        out_specs=pl.BlockSpec((None, r, tc), lambda q, j, core_ref: (q, 0, j)))
    return pl.pallas_call(body, name=name, grid_spec=grid_spec,
                          out_shape=jax.ShapeDtypeStruct((N_CHIPS, r, w), out_dtype),
                          compiler_params=_cparams(("parallel", "parallel")))(core, slabs, recv)


def _chip_exchange(arrays, name):
    na = len(arrays)

    def body(*refs):
        srcs, dsts, (send_sems, recv_sems, local_sems) = refs[:na], refs[na:2 * na], refs[2 * na:]
        x, y, c, _ = _my_place()
        chip = 2 * x + y
        own = [pltpu.make_async_copy(srcs[i].at[chip], dsts[i].at[chip], local_sems.at[i]) for i in range(na)]
        for cp in own:
            cp.start()
        sends, arrivals = [], []
        for i in range(na):
            for k in range(1, N_CHIPS):
                px, py = _chip_peer(x, y, k)
                s = i * N_CHIPS + k
                sends.append(_remote(srcs[i].at[chip ^ k], dsts[i].at[chip], send_sems.at[s], recv_sems.at[s], (px, py, c)))
                arrivals.append(_remote(srcs[i].at[chip], dsts[i].at[chip ^ k], send_sems.at[s], recv_sems.at[s], (px, py, c)))
        for cp in sends:
            cp.start()
        for cp in arrivals:
            cp.wait_recv()
        for cp in sends:
            cp.wait_send()
        for cp in own:
            cp.wait()

    hbm = pl.BlockSpec(memory_space=pl.ANY)
    return pl.pallas_call(
        body, name=name, in_specs=[hbm] * na, out_specs=tuple([hbm] * na),
        out_shape=tuple(jax.ShapeDtypeStruct(a.shape, a.dtype) for a in arrays),
        scratch_shapes=[pltpu.SemaphoreType.DMA((na * N_CHIPS,)), pltpu.SemaphoreType.DMA((na * N_CHIPS,)),
                        pltpu.SemaphoreType.DMA((na,))],
    )(*arrays)


def _adamw_update(g, w, m, v):
    c1 = 1.0 / (1.0 - ADAM_B1 ** ADAM_STEP)
    c2 = 1.0 / (1.0 - ADAM_B2 ** ADAM_STEP)
    mn = ADAM_B1 * m + (1.0 - ADAM_B1) * g
    vn = ADAM_B2 * v + (1.0 - ADAM_B2) * (g * g)
    return -ADAM_LR * ((mn * c1) / (jnp.sqrt(vn * c2) + ADAM_EPS) + ADAM_WD * w), mn, vn


def _reduce_adamw(parts, w, m, v, name):
    n_parts, r, wd = parts.shape
    tc = ELEMENTWISE_COLS

    def body(p_ref, w_ref, m_ref, v_ref, g_out, d_out, m_out, v_out):
        g = p_ref[0].astype(F32)
        for s in range(1, n_parts):
            g = g + p_ref[s].astype(F32)
        g_out[...] = g
        d_out[...], m_out[...], v_out[...] = _adamw_update(g, w_ref[...], m_ref[...], v_ref[...])

    blk = pl.BlockSpec((r, tc), lambda j: (0, j))
    shp = jax.ShapeDtypeStruct((r, wd), F32)
    return pl.pallas_call(
        body, name=name, grid=(wd // tc,),
        in_specs=[pl.BlockSpec((n_parts, r, tc), lambda j: (0, 0, j)), blk, blk, blk],
        out_specs=(blk, blk, blk, blk), out_shape=(shp, shp, shp, shp),
        compiler_params=_cparams(("parallel",)),
    )(parts, w, m, v)


def _reduce_landed_adamw(landing, own, me, w, m, v, name):
    n_parts, r, wd = landing.shape
    tc = ELEMENTWISE_COLS

    def body(me_ref, land_ref, own_ref, w_ref, m_ref, v_ref, g_out, d_out, m_out, v_out):
        mine = own_ref[...].astype(F32)
        g = None
        for s in range(n_parts):
            part = jnp.where(me_ref[0] == s, mine, land_ref[s].astype(F32))
            g = part if g is None else g + part
        g_out[...] = g
        d_out[...], m_out[...], v_out[...] = _adamw_update(g, w_ref[...], m_ref[...], v_ref[...])

    blk = pl.BlockSpec((r, tc), lambda j, me_ref: (0, j))
    shp = jax.ShapeDtypeStruct((r, wd), F32)
    grid_spec = pltpu.PrefetchScalarGridSpec(
        num_scalar_prefetch=1, grid=(wd // tc,),
        in_specs=[pl.BlockSpec((n_parts, r, tc), lambda j, me_ref: (0, 0, j)),
                  pl.BlockSpec((None, r, tc), lambda j, me_ref: (me_ref[0], 0, j)), blk, blk, blk],
        out_specs=(blk, blk, blk, blk))
    return pl.pallas_call(body, name=name, grid_spec=grid_spec, out_shape=(shp, shp, shp, shp),
                          compiler_params=_cparams(("parallel",)))(me, landing, own, w, m, v)


PACK_W = 1024


def _pad_heads(a, slots):
    lead = a.shape[:-1]
    a = a.reshape(lead + (slots, RWKV_HEAD_DIM))
    a = jnp.pad(a, [(0, 0)] * (len(lead) + 1) + [(0, LANES - RWKV_HEAD_DIM)])
    return a.reshape(lead + (slots * LANES,))


def _flat_pack(arrs, dtype, row_mult):
    flat = jnp.concatenate([a.reshape(-1).astype(dtype) for a in arrs])
    n = flat.shape[0]
    rows = -(-n // PACK_W)
    rows = -(-rows // row_mult) * row_mult
    return jnp.pad(flat, (0, rows * PACK_W - n)).reshape(rows, PACK_W)


def _row_pack(arrs, dtype, row_mult):
    parts = [a.astype(dtype) if a.shape[1] == PACK_W else a.astype(dtype).reshape(-1, PACK_W) for a in arrs]
    rows = sum(p.shape[0] for p in parts)
    pad = -(-rows // row_mult) * row_mult - rows
    return jnp.concatenate(parts + ([jnp.zeros((pad, PACK_W), dtype)] if pad else []), axis=0)


def _unpack_row_gathered(g, names, shard_shapes):
    out, r0 = {}, 0
    for n in names:
        s = shard_shapes[n]
        rows = s[0] * s[1] // PACK_W
        seg = g[:, r0:r0 + rows, :]
        r0 += rows
        if s[1] == PACK_W:
            assert SHARD_AXIS[n] == 0
            out[n] = seg.reshape(N_DEV * s[0], s[1])
        else:
            assert SHARD_AXIS[n] == 1
            out[n] = jnp.transpose(seg.reshape((N_DEV,) + tuple(s)), (1, 0, 2)).reshape(s[0], N_DEV * s[1])
    return out


def _unpack_gathered(g, names, shard_shapes):
    flat = g.reshape(N_DEV, -1)
    out, off = {}, 0
    for n in names:
        s = shard_shapes[n]
        size = s[0] * s[1]
        seg = flat[:, off:off + size].reshape((N_DEV,) + tuple(s))
        off += size
        if SHARD_AXIS[n] == 1:
            out[n] = jnp.transpose(seg, (1, 0, 2)).reshape(s[0], N_DEV * s[1])
        else:
            out[n] = seg.reshape(N_DEV * s[0], s[1])
    return out


def _shard_major(full, axis):
    a, b = full.shape
    if axis == 1:
        return jnp.transpose(full.reshape(a, N_DEV, b // N_DEV), (1, 0, 2)).reshape(N_DEV, -1)
    return full.reshape(N_DEV, -1)


def _prepare_weights(full, rep):
    w = full['w_in']
    d = w.shape[1]
    rkv = jnp.pad(w[0:1536].reshape(3 * RWKV_HEADS, RWKV_HEAD_DIM, d), ((0, 0), (0, LANES - RWKV_HEAD_DIM), (0, 0)))
    w_cat = jnp.concatenate([
        w[3848:4872], w[4872:5896], rkv.reshape(3 * RWKV_HEADS * LANES, d), w[1792:3328], w[3328:3840],
        w[1536:1792], jnp.pad(w[3840:3848], ((0, LANES - 8), (0, 0))), jnp.zeros((LANES, d), w.dtype)], axis=0)
    assert w_cat.shape[0] == CAT_W
    mu = rep['rwkv_mu']
    vecs = [mu[0:512], mu[512:1024], mu[1024:1536], rep['rwkv_w0'], rep['rwkv_a0'], rep['rwkv_k_k'], rep['rwkv_k_a'],
            rep['rwkv_ln_w'], rep['rwkv_ln_b'], rep['rwkv_r_k'].reshape(-1)]
    ppack = jnp.stack([jnp.pad(v.reshape(RWKV_HEADS, RWKV_HEAD_DIM), ((0, 0), (0, LANES - RWKV_HEAD_DIM))) for v in vecs], axis=1)
    ppack = jnp.pad(ppack, ((0, 0), (0, 16 - len(vecs)), (0, 0)))
    mulo = mu[1536:1792].reshape(1, 2 * LANES)
    wl = jnp.zeros((3, 2 * LANES, RWKV_HEADS * LANES), F32)
    wl = wl.at[0, 0:64].set(_pad_heads(full['rwkv_w2'], RWKV_HEADS))
    wl = wl.at[1, 64:128].set(_pad_heads(full['rwkv_a2'], RWKV_HEADS))
    wl = wl.at[2, 128:256].set(_pad_heads(full['rwkv_g2'], RWKV_HEADS))
    wl = jnp.transpose(wl.reshape(3, 2 * LANES, RWKV_HEADS, LANES), (2, 0, 1, 3))
    cw = full['gdn_conv_w'].reshape(GDN_CONV, 3, GDN_HEADS, LANES)
    cwpack = jnp.pad(jnp.transpose(cw, (2, 1, 0, 3)), ((0, 0), (0, 0), (0, SUBLANES - GDN_CONV), (0, 0)))
    gpar = jnp.zeros((SUBLANES, LANES), F32)
    gpar = gpar.at[0, 0:GDN_HEADS].set(rep['gdn_a_log']).at[1, 0:GDN_HEADS].set(rep['gdn_dt_bias']).at[2].set(rep['gdn_norm_w'])
    return dict(w_cat=w_cat, ffn_cw=full['ffn_conv_w'], ppack=ppack, mulo=mulo, wl=wl, cwpack=cwpack, gpar=gpar,
                g1=rep['norm1_g'].reshape(1, -1), g2=rep['norm2_g'].reshape(1, -1), gf=rep['final_g'].reshape(1, -1))


def _prepare_late_weights(full):
    rp = full['rwkv_proj']
    rproj = jnp.pad(rp.reshape(RWKV_HEADS, RWKV_HEAD_DIM, -1), ((0, 0), (0, LANES - RWKV_HEAD_DIM), (0, 0))).reshape(RWKV_HEADS * LANES, -1)
    return dict(rproj=rproj, gproj=full['gdn_proj'], w_out=full['w_out'], ffn_up=full['ffn_up'], ffn_down=full['ffn_down'])


def _local_step(x, target, p, late_weights, push_grads):
    d = x.shape[1]
    full_w = lambda a: (a, a.shape[1], 0)
    (u,) = _rw_forward(_norm_fn, [full_w(x)], [p['g1']], [(d, BF16)], "norm1")
    p_cat = _matmul(u, p['w_cat'], 'nt', F32, "proj_in")
    ya_pre, st_r = _rwkv_forward(p_cat, p['ppack'], p['mulo'], p['wl'])
    yb_pre, st_g = _gdn_forward(p_cat, p['cwpack'], p['gpar'])
    p = {**p, **late_weights((ya_pre, yb_pre))}
    ya = _matmul(ya_pre, p['rproj'], 'nn', F32, "rwkv_proj")
    yb = _matmul(yb_pre, p['gproj'], 'nn', F32, "gdn_proj")
    gates = [(p_cat, d, OFF_GA // d), (p_cat, d, OFF_GB // d)]
    (mixed,) = _rw_forward(_merge_fn, gates + [full_w(ya), full_w(yb)], [], [(d, BF16)], "merge")
    mo = _matmul(mixed, p['w_out'], 'nn', F32, "out_proj")
    x1, n2 = _rw_forward(_res_norm_fn, [full_w(x), full_w(mo)], [p['g2']], [(d, F32), (d, BF16)], "res_norm2")
    hpre = _matmul(n2, p['ffn_up'], 'nt', F32, "ffn_up")
    act = _ffn_act_forward(hpre, p['ffn_cw'])
    fo = _matmul(act, p['ffn_down'], 'nn', F32, "ffn_down")
    loss_vec, dx2, dx2b, dgf = _loss_head(x1, fo, p['gf'], target, "loss_head")

    dact = _matmul(dx2b, p['ffn_down'], 'nt', F32, "d_act")
    dw_down = _matmul(act, dx2b, 'tn', BF16, "dw_ffn_down")
    dh_gate, dh_up, dcw_gate, dcw_up = _ffn_backward(hpre, p['ffn_cw'], dact)
    dh = jnp.concatenate([dh_gate, dh_up], axis=1)
    dcw_f = jnp.concatenate([dcw_gate, dcw_up], axis=1)
    dn2 = _matmul(dh, p['ffn_up'], 'nn', F32, "d_norm2")
    dw_up = _matmul(dh, n2, 'tn', BF16, "dw_ffn_up")
    token = push_grads({'ffn_down': dw_down, 'ffn_up': dw_up})
    dx1, dx1b, dg2 = _rw_backward(_res_norm_fn, [full_w(x), full_w(mo)], [p['g2'] + token], [full_w(dx2), full_w(dn2)],
                                  [F32, BF16], "res_norm2_bwd")
    dmixed = _matmul(dx1b, p['w_out'], 'nt', F32, "d_mixed")
    dw_out = _matmul(mixed, dx1b, 'tn', BF16, "dw_out")
    dga, dgb, dya, dyb = _rw_backward(_merge_fn, gates + [full_w(ya), full_w(yb)], [], [full_w(dmixed)],
                                      [BF16, BF16, BF16, BF16], "merge_bwd")
    d_ya_pre = _matmul(dya, p['rproj'], 'nt', F32, "d_rwkv_out")
    dw_rproj = _matmul(ya_pre, dya, 'tn', F32, "dw_rwkv_proj")
    d_yb_pre = _matmul(dyb, p['gproj'], 'nt', F32, "d_gdn_out")
    dw_gproj = _matmul(yb_pre, dyb, 'tn', F32, "dw_gdn_proj")
    dpr, dpk, dpv, dplo, dpp, dml, dwl = _rwkv_backward(p_cat, p['ppack'], p['mulo'], p['wl'], st_r, d_ya_pre)
    dq, dk, dv, dz, dab, dcw_g, dgp = _gdn_backward(p_cat, p['cwpack'], p['gpar'], st_g, d_yb_pre)
    t = x.shape[0]
    dp_cat = jnp.concatenate([dga, dgb, dpr, dpk, dpv, dq, dk, dv, dz, dplo.astype(BF16), dab.astype(BF16),
                              jnp.zeros((t, LANES), BF16)], axis=1)
    dw_cat = _matmul(dp_cat, u, 'tn', BF16, "dw_in")
    dw_in = jnp.concatenate([dw_cat[OFF_RKV:OFF_QKV].reshape(3 * RWKV_HEADS, LANES, d)[:, :RWKV_HEAD_DIM].reshape(-1, d),
                             dw_cat[OFF_LO:OFF_AB], dw_cat[OFF_QKV:OFF_Z], dw_cat[OFF_Z:OFF_LO], dw_cat[OFF_AB:OFF_AB + 8],
                             dw_cat[OFF_GA:OFF_GB], dw_cat[OFF_GB:OFF_RKV]], axis=0)
    token = push_grads({'w_out': dw_out, 'w_in': dw_in})
    du = _matmul(dp_cat, p['w_cat'], 'nn', F32, "d_norm1")
    grad_x, dg1 = _rw_backward(_norm_skip_fn, [full_w(x)], [p['g1'] + token], [full_w(du), full_w(dx1)], [F32], "norm1_bwd")

    heads = lambda row: dpp[:, row, :RWKV_HEAD_DIM].reshape(-1)
    lora = lambda j, lo_, hi_: jnp.transpose(dwl[:, j, lo_:hi_, :RWKV_HEAD_DIM], (1, 0, 2)).reshape(hi_ - lo_, RWKV_WIDTH)
    grads = {
        'norm1_g': dg1[0],
        'w_in': dw_in,
        'rwkv_mu': jnp.concatenate([heads(0), heads(1), heads(2), jnp.sum(dml[:, 0, :], axis=0)]),
        'rwkv_w0': heads(3), 'rwkv_a0': heads(4), 'rwkv_k_k': heads(5), 'rwkv_k_a': heads(6),
        'rwkv_ln_w': heads(7), 'rwkv_ln_b': heads(8), 'rwkv_r_k': heads(9).reshape(RWKV_HEADS, RWKV_HEAD_DIM),
        'rwkv_w2': lora(0, 0, 64), 'rwkv_a2': lora(1, 64, 128), 'rwkv_g2': lora(2, 128, 256),
        'rwkv_proj': dw_rproj.reshape(RWKV_HEADS, LANES, -1)[:, :RWKV_HEAD_DIM].reshape(RWKV_WIDTH, -1),
        'gdn_conv_w': jnp.transpose(dcw_g[:, :, :GDN_CONV, :], (2, 1, 0, 3)).reshape(GDN_CONV, 3 * GDN_WIDTH),
        'gdn_a_log': jnp.sum(dgp[:, 0, :GDN_HEADS], axis=0), 'gdn_dt_bias': jnp.sum(dgp[:, 1, :GDN_HEADS], axis=0),
        'gdn_norm_w': jnp.sum(dgp[:, 2, :], axis=0),
        'gdn_proj': dw_gproj, 'w_out': dw_out, 'norm2_g': dg2[0], 'ffn_up': dw_up, 'ffn_conv_w': dcw_f,
        'ffn_down': dw_down, 'final_g': dgf[0],
    }
    return loss_vec, grad_x, grads


def kernel(x, norm1_g, w_in, rwkv_mu, rwkv_w0, rwkv_w2, rwkv_a0, rwkv_a2, rwkv_g2, rwkv_k_k, rwkv_k_a, rwkv_r_k, rwkv_ln_w, rwkv_ln_b, rwkv_proj, gdn_conv_w, gdn_a_log, gdn_dt_bias, gdn_norm_w, gdn_proj, w_out, norm2_g, ffn_up, ffn_conv_w, ffn_down, final_g, loss_target, m_norm1_g, m_w_in, m_rwkv_mu, m_rwkv_w0, m_rwkv_w2, m_rwkv_a0, m_rwkv_a2, m_rwkv_g2, m_rwkv_k_k, m_rwkv_k_a, m_rwkv_r_k, m_rwkv_ln_w, m_rwkv_ln_b, m_rwkv_proj, m_gdn_conv_w, m_gdn_a_log, m_gdn_dt_bias, m_gdn_norm_w, m_gdn_proj, m_w_out, m_norm2_g, m_ffn_up, m_ffn_conv_w, m_ffn_down, m_final_g, v_norm1_g, v_w_in, v_rwkv_mu, v_rwkv_w0, v_rwkv_w2, v_rwkv_a0, v_rwkv_a2, v_rwkv_g2, v_rwkv_k_k, v_rwkv_k_a, v_rwkv_r_k, v_rwkv_ln_w, v_rwkv_ln_b, v_rwkv_proj, v_gdn_conv_w, v_gdn_a_log, v_gdn_dt_bias, v_gdn_norm_w, v_gdn_proj, v_w_out, v_norm2_g, v_ffn_up, v_ffn_conv_w, v_ffn_down, v_final_g):
    given = dict(zip(WEIGHT_NAMES, (norm1_g, w_in, rwkv_mu, rwkv_w0, rwkv_w2, rwkv_a0, rwkv_a2, rwkv_g2, rwkv_k_k, rwkv_k_a, rwkv_r_k,
                                    rwkv_ln_w, rwkv_ln_b, rwkv_proj, gdn_conv_w, gdn_a_log, gdn_dt_bias, gdn_norm_w, gdn_proj, w_out,
                                    norm2_g, ffn_up, ffn_conv_w, ffn_down, final_g)))
    mom1 = dict(zip(WEIGHT_NAMES, (m_norm1_g, m_w_in, m_rwkv_mu, m_rwkv_w0, m_rwkv_w2, m_rwkv_a0, m_rwkv_a2, m_rwkv_g2, m_rwkv_k_k,
                                   m_rwkv_k_a, m_rwkv_r_k, m_rwkv_ln_w, m_rwkv_ln_b, m_rwkv_proj, m_gdn_conv_w, m_gdn_a_log,
                                   m_gdn_dt_bias, m_gdn_norm_w, m_gdn_proj, m_w_out, m_norm2_g, m_ffn_up, m_ffn_conv_w, m_ffn_down,
                                   m_final_g)))
    mom2 = dict(zip(WEIGHT_NAMES, (v_norm1_g, v_w_in, v_rwkv_mu, v_rwkv_w0, v_rwkv_w2, v_rwkv_a0, v_rwkv_a2, v_rwkv_g2, v_rwkv_k_k,
                                   v_rwkv_k_a, v_rwkv_r_k, v_rwkv_ln_w, v_rwkv_ln_b, v_rwkv_proj, v_gdn_conv_w, v_gdn_a_log,
                                   v_gdn_dt_bias, v_gdn_norm_w, v_gdn_proj, v_w_out, v_norm2_g, v_ffn_up, v_ffn_conv_w, v_ffn_down,
                                   v_final_g)))
    def strip(n, a):
        a = a if n == 'final_g' else a.reshape(a.shape[1:])
        return a.T if n in TRANSPOSED else a

    local = {n: strip(n, a) for n, a in given.items()}
    shard_shapes = {n: local[n].shape for n in SHARD_AXIS}
    sharded = BIG_SHARDED + SMALL_SHARDED

    late_names = [n for n in BIG_SHARDED if n != 'w_in']
    g_in = _all_gather_two_level(_row_pack([local['w_in']], BF16, 16), "gather_w_in")
    g_small = _all_gather_two_level(_flat_pack([local[n] for n in SMALL_SHARDED], F32, SUBLANES), "gather_small")
    late_pack, g_in, g_small = lax.optimization_barrier((_row_pack([local[n] for n in late_names], BF16, 16), g_in, g_small))
    send_sems, recv_sems, late_pack, landing, token = _gather_start(late_pack, "gather_late_start")
    full = _unpack_row_gathered(g_in, ['w_in'], shard_shapes)
    full.update(_unpack_gathered(g_small, SMALL_SHARDED, shard_shapes))
    rep = {n: local[n] for n in REPLICATED}
    rep['norm1_g'] = rep['norm1_g'] + token[0, 0]

    def late_weights(after):
        got = _gather_wait(send_sems, recv_sems, late_pack, landing, after, "gather_late_wait")
        me = 4 * lax.axis_index("x") + 2 * lax.axis_index("y") + lax.axis_index("c")
        slot = lax.broadcasted_iota(jnp.int32, (N_DEV, 1, 1), 0)
        got = jnp.where(slot == me, late_pack[None], got)
        return _prepare_late_weights(_unpack_row_gathered(got, late_names, shard_shapes))

    pushes = []
    me = 4 * lax.axis_index("x") + 2 * lax.axis_index("y") + lax.axis_index("c")
    slot = lax.broadcasted_iota(jnp.int32, (N_DEV, 1, 1), 0)

    def push_grads(group):
        names = list(group)
        slabs = [group[n].reshape(N_DEV, -1, group[n].shape[1]) for n in names]
        send_sems, recv_sems, slabs, landings, token = _slab_push_start(slabs, "grad_push_start_" + "_".join(names))
        pushes.append((names, send_sems, recv_sems, slabs, landings))
        return token[0, 0]

    loss_vec, grad_x, grads = _local_step(x[0], loss_target[0], _prepare_weights(full, rep), late_weights, push_grads)

    landed = {}
    for names, send_sems, recv_sems, slabs, landings in pushes:
        got = _slab_push_wait(send_sems, recv_sems, slabs, landings, (grad_x,), "grad_push_wait_" + "_".join(names))
        for n, slab, land in zip(names, slabs, got):
            landed[n] = (land, slab)

    small_sharded = ['rwkv_proj', 'gdn_proj'] + SMALL_SHARDED
    small_names = small_sharded + REPLICATED
    rep_vec = jnp.concatenate([grads[n].reshape(-1) for n in REPLICATED] + [loss_vec[0, 0:1]])
    slab_small = jnp.concatenate([_shard_major(grads[n], SHARD_AXIS[n]) for n in small_sharded] +
                                 [jnp.broadcast_to(rep_vec[None], (N_DEV, rep_vec.shape[0]))], axis=1)
    small_rows = -(-slab_small.shape[1] // (PACK_W * SUBLANES)) * SUBLANES
    slab_small = jnp.pad(slab_small, ((0, 0), (0, small_rows * PACK_W - slab_small.shape[1]))).reshape(N_DEV, small_rows, PACK_W)
    core = lax.axis_index("c").astype(jnp.int32).reshape(1)
    (from_sibling,) = _pair_exchange([slab_small], "grad_pair_exchange")
    chip_small = _pair_sum(slab_small, from_sibling, core, F32, "grad_pair_sum_small")
    (parts_small,) = _chip_exchange([chip_small], "grad_chip_exchange")

    def pack_local(src):
        flat = jnp.concatenate([strip(n, src[n]).reshape(-1) for n in small_names])
        return jnp.pad(flat, (0, small_rows * PACK_W - flat.shape[0])).reshape(small_rows, PACK_W)

    results = [({}, None) for _ in range(4)]
    me_arr = me.astype(jnp.int32).reshape(1)
    for n in ROW_SHARDED:
        packs = _reduce_landed_adamw(*landed[n], me_arr, local[n], strip(n, mom1[n]), strip(n, mom2[n]), "adamw_" + n)
        for (out, _), pk in zip(results, packs):
            out[n] = (pk.T if n in TRANSPOSED else pk).reshape(given[n].shape)
    packs = _reduce_adamw(parts_small, pack_local(given), pack_local(mom1), pack_local(mom2), "adamw_small")
    for i, pk in enumerate(packs):
        flat, off = pk.reshape(-1), 0
        for n in small_names:
            size = int(np.prod(given[n].shape))
            results[i][0][n] = flat[off:off + size].reshape(given[n].shape)
            off += size
        results[i] = (results[i][0], flat[off])
    (g_out, loss), (d_out, _), (m_out, _), (v_out, _) = results
    return (loss, grad_x[None], *[g_out[n] for n in WEIGHT_NAMES], *[d_out[n] for n in WEIGHT_NAMES],
            *[m_out[n] for n in WEIGHT_NAMES], *[v_out[n] for n in WEIGHT_NAMES])
```

```python
import functools

import jax
import jax.numpy as jnp
import numpy as np
from jax import lax
from jax.experimental import pallas as pl
from jax.experimental.pallas import tpu as pltpu

F32 = jnp.float32
BF16 = jnp.bfloat16

N_DEV = 8
D_MODEL = 1024
CHUNK = 64
RWKV_HEADS = 8
RWKV_HEAD_DIM = 64
RWKV_WIDTH = 512
GDN_HEADS = 4
GDN_HEAD_DIM = 128
GDN_WIDTH = 512
GDN_CONV = 4
FFN_HIDDEN = 2816
FFN_CONV = 3
NORM_EPS = 1e-6
L2_EPS = 1e-6
RWKV_GN_EPS = 64e-5
LANES = 128
SUBLANES = 8
VMEM_LIMIT = 56 * 1024 * 1024

ADAM_LR = 0.001
ADAM_B1 = 0.9
ADAM_B2 = 0.999
ADAM_EPS = 1e-08
ADAM_WD = 0.01
ADAM_STEP = 10

OFF_GA, OFF_GB, OFF_RKV, OFF_QKV, OFF_Z, OFF_LO, OFF_AB, CAT_W = 0, 1024, 2048, 5120, 6656, 7168, 7424, 7680
RWKV_HB = 8
RWKV_STEP_CHUNKS = 2
RWKV_TILE = RWKV_STEP_CHUNKS * CHUNK
GDN_HB = 4
GDN_STEP_CHUNKS = 4
GDN_TILE = GDN_STEP_CHUNKS * CHUNK

WEIGHT_NAMES = ['norm1_g', 'w_in', 'rwkv_mu', 'rwkv_w0', 'rwkv_w2', 'rwkv_a0', 'rwkv_a2', 'rwkv_g2', 'rwkv_k_k', 'rwkv_k_a',
                'rwkv_r_k', 'rwkv_ln_w', 'rwkv_ln_b', 'rwkv_proj', 'gdn_conv_w', 'gdn_a_log', 'gdn_dt_bias', 'gdn_norm_w',
                'gdn_proj', 'w_out', 'norm2_g', 'ffn_up', 'ffn_conv_w', 'ffn_down', 'final_g']
BIG_SHARDED = ['w_in', 'ffn_up', 'ffn_down', 'w_out', 'rwkv_proj', 'gdn_proj']
SMALL_SHARDED = ['rwkv_w2', 'rwkv_a2', 'rwkv_g2', 'gdn_conv_w', 'ffn_conv_w']
TRANSPOSED = ('w_in', 'ffn_up')
SHARD_AXIS = {'w_in': 0, 'ffn_up': 0, 'ffn_down': 0, 'w_out': 0, 'rwkv_proj': 1, 'gdn_proj': 1,
              'rwkv_w2': 1, 'rwkv_a2': 1, 'rwkv_g2': 1, 'gdn_conv_w': 1, 'ffn_conv_w': 1}
REPLICATED = [n for n in WEIGHT_NAMES if n not in SHARD_AXIS]
ROW_SHARDED = ['w_in', 'ffn_up', 'ffn_down', 'w_out']


def _cparams(sem=None):
    kw = dict(vmem_limit_bytes=VMEM_LIMIT)
    if sem is not None:
        kw['dimension_semantics'] = sem
    return pltpu.CompilerParams(**kw)


_NN, _NT, _TN = 'nn', 'nt', 'tn'
_DIMS_2D = {'nn': (((1,), (0,)), ((), ())), 'nt': (((1,), (1,)), ((), ())), 'tn': (((0,), (0,)), ((), ()))}
_DIMS_3D = {'nn': (((2,), (1,)), ((0,), (0,))), 'nt': (((2,), (2,)), ((0,), (0,))), 'tn': (((1,), (1,)), ((0,), (0,)))}


def _dg(a, b, kind):
    return lax.dot_general(a, b, (_DIMS_2D if a.ndim == 2 else _DIMS_3D)[kind], preferred_element_type=F32)


def _dot1(a, b, kind):
    return _dg(a.astype(BF16), b.astype(BF16), kind)


@jax.custom_vjp
def _dhi(a, b):
    return _dot1(a, b, _NN)


_dhi.defvjp(lambda a, b: (_dot1(a, b, _NN), (a, b)),
            lambda res, ct: (_dot1(ct, res[1], _NT), _dot1(res[0], ct, _TN)))


@jax.custom_vjp
def _dnt(a, b):
    return _dot1(a, b, _NT)


_dnt.defvjp(lambda a, b: (_dot1(a, b, _NT), (a, b)),
            lambda res, ct: (_dot1(ct, res[1], _NN), _dot1(ct, res[0], _TN)))


@jax.custom_vjp
def _dtn(a, b):
    return _dot1(a, b, _TN)


_dtn.defvjp(lambda a, b: (_dot1(a, b, _TN), (a, b)),
            lambda res, ct: (_dot1(res[1], ct, _NT), _dot1(res[0], ct, _NN)))


def _split3(x):
    x1 = x.astype(BF16)
    r1 = x - x1.astype(F32)
    x2 = r1.astype(BF16)
    return x1, x2, (r1 - x2.astype(F32)).astype(BF16)


def _dot_exact_lhs(sel, x, kind):
    parts = [_dg(sel, xi, kind) for xi in _split3(x)]
    return parts[0] + parts[1] + parts[2]


def _tril_ones(like):
    c = like.shape[-2]
    ri, ci = _iotas(c)
    return jnp.broadcast_to((ri >= ci).astype(BF16), like.shape[:-2] + (c, c))


@jax.custom_vjp
def _cumsum_rows(x):
    return _dot_exact_lhs(_tril_ones(x), x, _NN)


_cumsum_rows.defvjp(lambda x: (_dot_exact_lhs(_tril_ones(x), x, _NN), None),
                    lambda _, ct: (_dot_exact_lhs(_tril_ones(ct), ct, _TN),))


@jax.custom_vjp
def _lane_sum_as_row(x):
    return _dot_exact_lhs(jnp.ones(x.shape, BF16), x, _NT)


def _lane_sum_as_row_bwd(_, ct):
    ones = jnp.ones(ct.shape[:-1] + (LANES,), BF16)
    parts = [_dg(ci, ones, _TN) for ci in _split3(ct)]
    return (parts[0] + parts[1] + parts[2],)


_lane_sum_as_row.defvjp(lambda x: (_dot_exact_lhs(jnp.ones(x.shape, BF16), x, _NT), None), _lane_sum_as_row_bwd)


def _shift_rows(x, halo, s):
    rows = lax.broadcasted_iota(jnp.int32, x.shape, 0)
    out = pltpu.roll(x, s, 0)
    for i in range(s):
        out = jnp.where(rows == i, halo[SUBLANES - s + i:SUBLANES - s + i + 1, :], out)
    return out


def _unshift_rows(g, carry, s):
    c = g.shape[0]
    rows = lax.broadcasted_iota(jnp.int32, g.shape, 0)
    out = pltpu.roll(g, c - s, 0)
    for i in range(s):
        out = jnp.where(rows == c - s + i, carry[i:i + 1, :], out)
    return out


def _sigmoid_plain(z):
    return 1.0 / (1.0 + jnp.exp(-z))


def _sigmoid_value(z):
    t = jnp.exp(-jnp.abs(z))
    r = 1.0 / (1.0 + t)
    return jnp.where(z >= 0, r, t * r)


@jax.custom_vjp
def _sigmoid(z):
    return _sigmoid_value(z)


def _sigmoid_fwd(z):
    s = _sigmoid_value(z)
    return s, s


_sigmoid.defvjp(_sigmoid_fwd, lambda s, ct: (ct * s * (1.0 - s),))


def _silu(z):
    return z * _sigmoid(z)


def _softplus(z):
    return jnp.maximum(z, 0.0) + jnp.log(1.0 + jnp.exp(-jnp.abs(z)))


def _rms(t, gain):
    return t * lax.rsqrt(jnp.mean(t * t, axis=-1, keepdims=True) + NORM_EPS) * gain


def _iotas(c):
    return lax.broadcasted_iota(jnp.int32, (c, c), 0), lax.broadcasted_iota(jnp.int32, (c, c), 1)


def _unit_lower_inverse(xm, eye):
    t = eye + xm
    p = xm
    for _ in range(5):
        p = _dhi(p, p)
        t = t + _dhi(t, p)
    return t


def _rwkv_head(pr, pk, pv, plo, qr, qk, qv, qlo, s0, pp, mulo, wl):
    c = pr.shape[1]
    n_heads = s0.shape[0]
    n_chunks = pr.shape[0] // n_heads
    ri, ci = _iotas(c)
    if n_chunks > 1:
        pp = jnp.concatenate([pp] * n_chunks, axis=0)
        wl = jnp.concatenate([wl] * n_chunks, axis=0)

    def mix(p, q, mu):
        return p + (q - p) * mu

    r = mix(pr, qr, pp[:, 0:1])
    k = mix(pk, qk, pp[:, 1:2])
    v = mix(pv, qv, pp[:, 2:3])
    lo = mix(plo, qlo, mulo)
    w0, a0, k_k, k_a, ln_w, ln_b, r_k = (pp[:, i:i + 1] for i in range(3, 10))

    def per_head(t):
        return jnp.concatenate([jnp.broadcast_to(t[i], (n_heads,) + t.shape[1:]) for i in range(n_chunks)], axis=0)

    zw = _dhi(per_head(jnp.tanh(lo)), wl[:, 0])
    za = _dhi(per_head(lo), wl[:, 1])
    g = _dhi(per_head(_sigmoid(lo)), wl[:, 2])
    w_log = -_softplus(-(w0 + zw)) - 0.5
    lw = -jnp.exp(w_log)
    a = _sigmoid(a0 + za)
    kk = k * k_k
    kk = kk * lax.rsqrt(jnp.sum(kk * kk, axis=-1, keepdims=True) + L2_EPS)
    k2 = k * (1.0 + (a - 1.0) * k_a)
    an = -kk
    b = kk * a
    causal = ri >= ci
    strict = ri > ci
    eye = (ri == ci).astype(F32)
    cl = _cumsum_rows(lw)
    ecl = jnp.exp(-cl)
    at = an * jnp.exp(cl - lw)
    bt = b * ecl
    kt = k2 * ecl
    rt = r * jnp.exp(cl)
    a_ab = jnp.where(strict, _dnt(at, bt), 0.0)
    a_ak = jnp.where(strict, _dnt(at, kt), 0.0)
    tinv = _unit_lower_inverse(a_ab, eye)
    akv = _dhi(a_ak, v)
    r_b = jnp.where(causal, _dnt(rt, bt), 0.0)
    rkv = _dhi(jnp.where(causal, _dnt(rt, kt), 0.0), v)
    cl_end = jnp.sum(lw, axis=1, keepdims=True)
    dec_end = jnp.exp(cl_end - cl)
    b_end = b * dec_end
    sv = _dtn(v, k2 * dec_end)
    e_end = jnp.exp(cl_end)
    state, ys = s0, []
    for i in range(n_chunks):
        sl = slice(i * n_heads, (i + 1) * n_heads)
        u = _dhi(tinv[sl], _dnt(at[sl], state) + akv[sl])
        ys.append(_dnt(rt[sl], state) + _dhi(r_b[sl], u) + rkv[sl])
        state = state * e_end[sl] + _dtn(u, b_end[sl]) + sv[sl]
    y = jnp.concatenate(ys, axis=0) if n_chunks > 1 else ys[0]
    s1 = state
    m = (lax.broadcasted_iota(jnp.int32, (1, LANES), 1) < RWKV_HEAD_DIM).astype(F32)
    mean = jnp.sum(y, axis=-1, keepdims=True) * (1.0 / RWKV_HEAD_DIM)
    yc = (y - mean) * m
    var = jnp.sum(yc * yc, axis=-1, keepdims=True) * (1.0 / RWKV_HEAD_DIM)
    yn = yc * lax.rsqrt(var + RWKV_GN_EPS) * ln_w + ln_b
    y2 = yn + jnp.sum(r * k2 * r_k, axis=-1, keepdims=True) * v
    return y2 * g, s1


def _gdn_head(xq, xk, xv, z, ab, s0, cw, gp, oha, ohb):
    c = z.shape[1]
    n_heads = s0.shape[0]
    n_chunks = z.shape[0] // n_heads
    ri, ci = _iotas(c)
    cw = jnp.concatenate([cw] * n_chunks, axis=0) if n_chunks > 1 else cw

    def conv(xs, w):
        out = xs[0] * w[:, GDN_CONV - 1:GDN_CONV]
        for s in range(1, GDN_CONV):
            out = out + xs[s] * w[:, GDN_CONV - 1 - s:GDN_CONV - s]
        return out

    q = _silu(conv(xq, cw[:, 0]))
    k = _silu(conv(xk, cw[:, 1]))
    v = _silu(conv(xv, cw[:, 2]))
    q = q * lax.rsqrt(jnp.sum(q * q, axis=-1, keepdims=True) + L2_EPS) * (GDN_HEAD_DIM ** -0.5)
    k = k * lax.rsqrt(jnp.sum(k * k, axis=-1, keepdims=True) + L2_EPS)
    gg = -jnp.exp(gp[0:1]) * _softplus(ab + gp[1:2])
    beta = jnp.sum(_sigmoid(ab) * ohb, axis=-1, keepdims=True)
    causal = ri >= ci
    strict = ri > ci
    eye = (ri == ci).astype(F32)
    gcm = _cumsum_rows(gg * oha)
    gc = jnp.sum(gcm, axis=-1, keepdims=True)
    gc_row = _lane_sum_as_row(gcm)
    dec = jnp.where(causal, jnp.exp(jnp.where(causal, gc - gc_row, 0.0)), 0.0)
    kb = k * beta
    vb = v * beta
    lm = jnp.where(strict, _dnt(kb, k) * dec, 0.0)
    tinv = _unit_lower_inverse(-lm, eye)
    egc = jnp.exp(gc)
    u = _dhi(tinv, vb)
    wk = _dhi(tinv, kb * egc)
    attn = jnp.where(causal, _dnt(q, k) * dec, 0.0)
    g_last = gc[:, c - 1:c, :]
    q_dec = q * egc
    k_dec = k * jnp.exp(g_last - gc)
    e_last = jnp.exp(g_last)
    state, outs = s0, []
    for i in range(n_chunks):
        sl = slice(i * n_heads, (i + 1) * n_heads)
        v_new = u[sl] - _dhi(wk[sl], state)
        outs.append(_dhi(q_dec[sl], state) + _dhi(attn[sl], v_new))
        state = state * e_last[sl] + _dtn(k_dec[sl], v_new)
    o = jnp.concatenate(outs, axis=0) if n_chunks > 1 else outs[0]
    return _rms(o, gp[2:3]) * _silu(z), state


def _head_id(grp, i, per_step, heads):
    return i if per_step == heads else grp * per_step + i


def _head_range(grp, per_step, heads):
    return slice(None) if per_step == heads else pl.ds(grp * per_step, per_step)


def _rwkv_specs(nmap):
    hb, groups = RWKV_HB, RWKV_HEADS // RWKV_HB
    cb = OFF_RKV // (hb * LANES)
    specs = []
    for j in range(3):
        specs.append(pl.BlockSpec((RWKV_TILE, hb * LANES), lambda n, g, j=j: (nmap(n), cb + j * groups + g)))
    specs.append(pl.BlockSpec((RWKV_TILE, 2 * LANES), lambda n, g: (nmap(n), OFF_LO // (2 * LANES))))
    per = RWKV_TILE // SUBLANES
    for j in range(3):
        specs.append(pl.BlockSpec((SUBLANES, hb * LANES),
                                  lambda n, g, j=j: (jnp.maximum(nmap(n) * per - 1, 0), cb + j * groups + g)))
    specs.append(pl.BlockSpec((SUBLANES, 2 * LANES), lambda n, g: (jnp.maximum(nmap(n) * per - 1, 0), OFF_LO // (2 * LANES))))
    specs.append(pl.BlockSpec((hb, 16, LANES), lambda n, g: (g, 0, 0)))
    specs.append(pl.BlockSpec((1, 2 * LANES), lambda n, g: (0, 0)))
    specs.append(pl.BlockSpec((hb, 3, 2 * LANES, LANES), lambda n, g: (g, 0, 0, 0)))
    return specs


def _rwkv_operands(refs, halos, live):
    pr, pk, pv, plo = refs
    hr, hk, hv, hlo = halos
    cur, prev = [], []
    for x, hx in ((pr, hr), (pk, hk), (pv, hv)):
        tiles = [_lane_block(x, h) for h in range(RWKV_HB)]
        cur.append(_chunk_batch(tiles))
        prev.append(_chunk_batch([_shift_rows(t_, _lane_block(hx, h) * live, 1) for h, t_ in enumerate(tiles)]))
    lo = plo[...]
    cur.append(_chunk_batch([lo]))
    prev.append(_chunk_batch([_shift_rows(lo, hlo[...] * live, 1)]))
    return cur, prev


def _rwkv_forward(p_cat, ppack, mulo, wl):
    t = p_cat.shape[0]
    n_chunks = t // RWKV_TILE

    def body(pr, pk, pv, plo, hr, hk, hv, hlo, pp, ml, w, out, st_out, s_scr):
        n, grp = pl.program_id(0), pl.program_id(1)

        hsl = _head_range(grp, RWKV_HB, RWKV_HEADS)

        @pl.when(n == 0)
        def _():
            s_scr[hsl] = jnp.zeros((RWKV_HB, LANES, LANES), F32)

        live = (n > 0).astype(F32)
        cur, prev = _rwkv_operands((pr, pk, pv, plo), (hr, hk, hv, hlo), live)
        s0 = s_scr[hsl]
        st_out[...] = s0
        o, s1 = _rwkv_head(*cur, *prev, s0, pp[...], ml[...], w[...])
        for h, tile in enumerate(_head_tiles(o, RWKV_HB)):
            out[:, h * LANES:(h + 1) * LANES] = tile.astype(out.dtype)
        s_scr[hsl] = s1

    return pl.pallas_call(
        body, name="rwkv_fwd", grid=(n_chunks, RWKV_HEADS // RWKV_HB),
        in_specs=_rwkv_specs(lambda n: n),
        out_specs=(pl.BlockSpec((RWKV_TILE, RWKV_HB * LANES), lambda n, g: (n, g)),
                   pl.BlockSpec((None, RWKV_HB, LANES, LANES), lambda n, g: (n, g, 0, 0))),
        out_shape=(jax.ShapeDtypeStruct((t, RWKV_HEADS * LANES), BF16),
                   jax.ShapeDtypeStruct((n_chunks, RWKV_HEADS, LANES, LANES), F32)),
        scratch_shapes=[pltpu.VMEM((RWKV_HEADS, LANES, LANES), F32)],
        compiler_params=_cparams(("arbitrary", "arbitrary")),
    )(p_cat, p_cat, p_cat, p_cat, p_cat, p_cat, p_cat, p_cat, ppack, mulo, wl)


def _rwkv_backward(p_cat, ppack, mulo, wl, states, d_out):
    t = p_cat.shape[0]
    n_chunks = t // RWKV_TILE
    last = n_chunks - 1

    def body(pr, pk, pv, plo, hr, hk, hv, hlo, pp, ml, w, st, dy, dpr, dpk, dpv, dplo, dpp, dml, dw, ds_scr, car_scr, carlo_scr):
        n, grp = pl.program_id(0), pl.program_id(1)

        hsl = _head_range(grp, RWKV_HB, RWKV_HEADS)
        gi = _head_id(grp, 0, 1, RWKV_HEADS // RWKV_HB)

        @pl.when(n == 0)
        def _():
            ds_scr[hsl] = jnp.zeros((RWKV_HB, LANES, LANES), F32)
            car_scr[hsl] = jnp.zeros((RWKV_HB, 3 * SUBLANES, LANES), F32)
            carlo_scr[gi] = jnp.zeros((SUBLANES, 2 * LANES), F32)

        @pl.when((n == 0) & (grp == 0))
        def _():
            dpp[...] = jnp.zeros(dpp.shape, F32)
            dml[...] = jnp.zeros(dml.shape, F32)
            dw[...] = jnp.zeros(dw.shape, F32)

        live = (n < last).astype(F32)
        cur, prev = _rwkv_operands((pr, pk, pv, plo), (hr, hk, hv, hlo), live)
        _, vjp = jax.vjp(_rwkv_head, *cur, *prev, st[...], pp[...], ml[...], w[...])
        g = vjp((_chunk_batch([_lane_block(dy, h) for h in range(RWKV_HB)]), ds_scr[hsl]))
        outs = (dpr, dpk, dpv)
        d_cur = [_head_tiles(g[j], RWKV_HB) for j in range(3)]
        d_prev = [_head_tiles(g[4 + j], RWKV_HB) for j in range(3)]
        for i in range(RWKV_HB):
            sl = slice(i * LANES, (i + 1) * LANES)
            h = _head_id(grp, i, RWKV_HB, RWKV_HEADS)
            car = car_scr[h]
            for j in range(3):
                tot = d_cur[j][i] + _unshift_rows(d_prev[j][i], car[SUBLANES * j:SUBLANES * (j + 1), :], 1)
                outs[j][:, sl] = tot.astype(outs[j].dtype)
                car_scr[h, SUBLANES * j:SUBLANES * (j + 1), :] = d_prev[j][i][0:SUBLANES, :]
        (dlo_cur,), (dlo_prev,) = _head_tiles(g[3], 1), _head_tiles(g[7], 1)
        dlo = dlo_cur + _unshift_rows(dlo_prev, carlo_scr[gi], 1)
        carlo_scr[gi] = dlo_prev[0:SUBLANES, :]
        ds_scr[hsl] = g[8]
        dpp[hsl] += g[9]
        dml[0, 0:1, :] += g[10]
        dw[hsl] += g[11]

        @pl.when(grp == 0)
        def _():
            dplo[...] = dlo

        @pl.when(grp > 0)
        def _():
            dplo[...] += dlo

    rev = lambda n: last - n
    in_specs = _rwkv_specs(rev) + [
        pl.BlockSpec((None, RWKV_HB, LANES, LANES), lambda n, g: (rev(n), g, 0, 0)),
        pl.BlockSpec((RWKV_TILE, RWKV_HB * LANES), lambda n, g: (rev(n), g)),
    ]
    hw = RWKV_HEADS * LANES
    return pl.pallas_call(
        body, name="rwkv_bwd", grid=(n_chunks, RWKV_HEADS // RWKV_HB),
        in_specs=in_specs,
        out_specs=(pl.BlockSpec((RWKV_TILE, RWKV_HB * LANES), lambda n, g: (rev(n), g)),
                   pl.BlockSpec((RWKV_TILE, RWKV_HB * LANES), lambda n, g: (rev(n), g)),
                   pl.BlockSpec((RWKV_TILE, RWKV_HB * LANES), lambda n, g: (rev(n), g)),
                   pl.BlockSpec((RWKV_TILE, 2 * LANES), lambda n, h: (rev(n), 0)),
                   pl.BlockSpec((RWKV_HEADS, 16, LANES), lambda n, h: (0, 0, 0)),
                   pl.BlockSpec((RWKV_HEADS, SUBLANES, 2 * LANES), lambda n, h: (0, 0, 0)),
                   pl.BlockSpec((RWKV_HEADS, 3, 2 * LANES, LANES), lambda n, h: (0, 0, 0, 0))),
        out_shape=(jax.ShapeDtypeStruct((t, hw), BF16), jax.ShapeDtypeStruct((t, hw), BF16), jax.ShapeDtypeStruct((t, hw), BF16),
                   jax.ShapeDtypeStruct((t, 2 * LANES), F32),
                   jax.ShapeDtypeStruct((RWKV_HEADS, 16, LANES), F32),
                   jax.ShapeDtypeStruct((RWKV_HEADS, SUBLANES, 2 * LANES), F32),
                   jax.ShapeDtypeStruct((RWKV_HEADS, 3, 2 * LANES, LANES), F32)),
        scratch_shapes=[pltpu.VMEM((RWKV_HEADS, LANES, LANES), F32),
                        pltpu.VMEM((RWKV_HEADS, 3 * SUBLANES, LANES), F32),
                        pltpu.VMEM((RWKV_HEADS, SUBLANES, 2 * LANES), F32)],
        compiler_params=_cparams(("arbitrary", "arbitrary")),
    )(p_cat, p_cat, p_cat, p_cat, p_cat, p_cat, p_cat, p_cat, ppack, mulo, wl, states, d_out)


def _gdn_specs(nmap):
    per = GDN_TILE // SUBLANES
    hb, groups = GDN_HB, GDN_HEADS // GDN_HB
    cb = OFF_QKV // (hb * LANES)
    specs = []
    for j in range(3):
        specs.append(pl.BlockSpec((GDN_TILE, hb * LANES), lambda n, g, j=j: (nmap(n), cb + j * groups + g)))
    for j in range(3):
        specs.append(pl.BlockSpec((SUBLANES, hb * LANES),
                                  lambda n, g, j=j: (jnp.maximum(nmap(n) * per - 1, 0), cb + j * groups + g)))
    specs.append(pl.BlockSpec((GDN_TILE, hb * LANES), lambda n, g: (nmap(n), OFF_Z // (hb * LANES) + g)))
    specs.append(pl.BlockSpec((GDN_TILE, LANES), lambda n, g: (nmap(n), OFF_AB // LANES)))
    specs.append(pl.BlockSpec((hb, 3, SUBLANES, LANES), lambda n, g: (g, 0, 0, 0)))
    specs.append(pl.BlockSpec((SUBLANES, LANES), lambda n, g: (0, 0)))
    return specs


def _conv_taps(x, halo):
    return (x,) + tuple(_shift_rows(x, halo, s) for s in range(1, GDN_CONV))


def _onehots(grp):
    nb = GDN_STEP_CHUNKS * GDN_HB
    lane = lax.broadcasted_iota(jnp.int32, (nb, 1, LANES), 2)
    head = lax.broadcasted_iota(jnp.int32, (nb, 1, LANES), 0) % GDN_HB + _head_id(grp, 0, GDN_HB, GDN_HEADS)
    return (lane == head).astype(F32), (lane == GDN_HEADS + head).astype(F32)


def _chunk_batch(tiles):
    n_chunks = tiles[0].shape[0] // CHUNK
    return jnp.stack([t_[i * CHUNK:(i + 1) * CHUNK, :] for i in range(n_chunks) for t_ in tiles])


def _head_tiles(batch, n_heads=GDN_HB):
    n_chunks = batch.shape[0] // n_heads
    return [jnp.concatenate([batch[i * n_heads + h] for i in range(n_chunks)], axis=0) for h in range(n_heads)]


def _lane_block(ref, h):
    return ref[:, h * LANES:(h + 1) * LANES]


def _gdn_taps(refs, halos, live):
    out = []
    for x, hx in zip(refs, halos):
        per_head = [_conv_taps(_lane_block(x, h), _lane_block(hx, h) * live) for h in range(GDN_HB)]
        out.append(tuple(_chunk_batch([per_head[h][s] for h in range(GDN_HB)]) for s in range(GDN_CONV)))
    return out


def _gdn_forward(p_cat, cwpack, gpar):
    t = p_cat.shape[0]
    n_chunks = t // GDN_TILE

    def body(xq, xk, xv, hq, hk, hv, z, ab, cw, gp, out, st_out, s_scr):
        n, grp = pl.program_id(0), pl.program_id(1)

        hsl = _head_range(grp, GDN_HB, GDN_HEADS)

        @pl.when(n == 0)
        def _():
            s_scr[hsl] = jnp.zeros((GDN_HB, LANES, LANES), F32)

        live = (n > 0).astype(F32)
        oha, ohb = _onehots(grp)
        s0 = s_scr[hsl]
        st_out[...] = s0
        taps = _gdn_taps((xq, xk, xv), (hq, hk, hv), live)
        zb = _chunk_batch([_lane_block(z, h) for h in range(GDN_HB)])
        abb = _chunk_batch([ab[...]] * GDN_HB)
        o, s1 = _gdn_head(*taps, zb, abb, s0, cw[...], gp[...], oha, ohb)
        for h, tile in enumerate(_head_tiles(o)):
            out[:, h * LANES:(h + 1) * LANES] = tile.astype(out.dtype)
        s_scr[hsl] = s1

    return pl.pallas_call(
        body, name="gdn_fwd", grid=(n_chunks, GDN_HEADS // GDN_HB),
        in_specs=_gdn_specs(lambda n: n),
        out_specs=(pl.BlockSpec((GDN_TILE, GDN_HB * LANES), lambda n, g: (n, g)),
                   pl.BlockSpec((None, GDN_HB, LANES, LANES), lambda n, g: (n, g, 0, 0))),
        out_shape=(jax.ShapeDtypeStruct((t, GDN_WIDTH), BF16),
                   jax.ShapeDtypeStruct((n_chunks, GDN_HEADS, LANES, LANES), F32)),
        scratch_shapes=[pltpu.VMEM((GDN_HEADS, LANES, LANES), F32)],
        compiler_params=_cparams(("arbitrary", "arbitrary")),
    )(p_cat, p_cat, p_cat, p_cat, p_cat, p_cat, p_cat, p_cat, cwpack, gpar)


def _gdn_backward(p_cat, cwpack, gpar, states, d_out):
    t = p_cat.shape[0]
    n_chunks = t // GDN_TILE
    last = n_chunks - 1

    def body(xq, xk, xv, hq, hk, hv, z, ab, cw, gp, st, dy, dq, dk, dv, dz, dab, dcw, dgp, ds_scr, car_scr):
        n, grp = pl.program_id(0), pl.program_id(1)

        hsl = _head_range(grp, GDN_HB, GDN_HEADS)

        @pl.when(n == 0)
        def _():
            ds_scr[hsl] = jnp.zeros((GDN_HB, LANES, LANES), F32)
            car_scr[hsl] = jnp.zeros((GDN_HB, 3 * GDN_CONV, SUBLANES, LANES), F32)

        @pl.when((n == 0) & (grp == 0))
        def _():
            dcw[...] = jnp.zeros(dcw.shape, F32)
            dgp[...] = jnp.zeros(dgp.shape, F32)

        live = (n < last).astype(F32)
        oha, ohb = _onehots(grp)
        fn = functools.partial(_gdn_head, oha=oha, ohb=ohb)
        taps = _gdn_taps((xq, xk, xv), (hq, hk, hv), live)
        zb = _chunk_batch([_lane_block(z, h) for h in range(GDN_HB)])
        abb = _chunk_batch([ab[...]] * GDN_HB)
        _, vjp = jax.vjp(fn, *taps, zb, abb, st[...], cw[...], gp[...])
        g = vjp((_chunk_batch([_lane_block(dy, h) for h in range(GDN_HB)]), ds_scr[hsl]))
        outs = (dq, dk, dv)
        tap_tiles = [[_head_tiles(g[j][s]) for s in range(GDN_CONV)] for j in range(3)]
        dz_tiles = _head_tiles(g[3])
        for i in range(GDN_HB):
            sl = slice(i * LANES, (i + 1) * LANES)
            h = _head_id(grp, i, GDN_HB, GDN_HEADS)
            for j in range(3):
                tot = tap_tiles[j][0][i]
                for s in range(1, GDN_CONV):
                    slot = j * GDN_CONV + s
                    tot = tot + _unshift_rows(tap_tiles[j][s][i], car_scr[h, slot], s)
                    car_scr[h, slot] = tap_tiles[j][s][i][0:SUBLANES, :]
                outs[j][:, sl] = tot.astype(outs[j].dtype)
            dz[:, sl] = dz_tiles[i].astype(dz.dtype)
        dab_tiles = _head_tiles(g[4])
        dab_sum = dab_tiles[0]
        for h in range(1, GDN_HB):
            dab_sum = dab_sum + dab_tiles[h]
        ds_scr[hsl] = g[5]
        dcw[hsl] += g[6]
        dgp[0] += g[7]

        @pl.when(grp == 0)
        def _():
            dab[...] = dab_sum

        @pl.when(grp > 0)
        def _():
            dab[...] += dab_sum

    rev = lambda n: last - n
    in_specs = _gdn_specs(rev) + [
        pl.BlockSpec((None, GDN_HB, LANES, LANES), lambda n, g: (rev(n), g, 0, 0)),
        pl.BlockSpec((GDN_TILE, GDN_HB * LANES), lambda n, g: (rev(n), g)),
    ]
    blk = pl.BlockSpec((GDN_TILE, GDN_HB * LANES), lambda n, g: (rev(n), g))
    return pl.pallas_call(
        body, name="gdn_bwd", grid=(n_chunks, GDN_HEADS // GDN_HB),
        in_specs=in_specs,
        out_specs=(blk, blk, blk, blk,
                   pl.BlockSpec((GDN_TILE, LANES), lambda n, h: (rev(n), 0)),
                   pl.BlockSpec((GDN_HEADS, 3, SUBLANES, LANES), lambda n, h: (0, 0, 0, 0)),
                   pl.BlockSpec((GDN_HEADS, SUBLANES, LANES), lambda n, h: (0, 0, 0))),
        out_shape=(jax.ShapeDtypeStruct((t, GDN_WIDTH), BF16), jax.ShapeDtypeStruct((t, GDN_WIDTH), BF16),
                   jax.ShapeDtypeStruct((t, GDN_WIDTH), BF16), jax.ShapeDtypeStruct((t, GDN_WIDTH), BF16),
                   jax.ShapeDtypeStruct((t, LANES), F32),
                   jax.ShapeDtypeStruct((GDN_HEADS, 3, SUBLANES, LANES), F32),
                   jax.ShapeDtypeStruct((GDN_HEADS, SUBLANES, LANES), F32)),
        scratch_shapes=[pltpu.VMEM((GDN_HEADS, LANES, LANES), F32),
                        pltpu.VMEM((GDN_HEADS, 3 * GDN_CONV, SUBLANES, LANES), F32)],
        compiler_params=_cparams(("arbitrary", "arbitrary")),
    )(p_cat, p_cat, p_cat, p_cat, p_cat, p_cat, p_cat, p_cat, cwpack, gpar, states, d_out)


MM_VMEM_BUDGET = 44 * 1024 * 1024
MM_MIN_STEPS = 4


def _mm_tiles(mode, m, n, k, out_bytes):
    tms = [t for t in (2048, 1024, 768, 512, 256, 128, 64) if m % t == 0 and (mode != 'tn' or t % LANES == 0)]
    tns = [t for t in (1408, 1024, 768, 512, 256, 128) if n % t == 0]
    tks = [t for t in (2048, 1920, 1408, 1024, 512, 256, 128, 64) if k % t == 0]
    best, best_key = None, None
    for tm in tms:
        for tn in tns:
            for tk in tks:
                nk = k // tk
                vmem = 2 * (tm * tk * 2 + tk * tn * 2 + tm * tn * out_bytes) + (tm * tn * 4 if nk > 1 else 0)
                steps = (m // tm) * (n // tn) * nk
                if vmem > MM_VMEM_BUDGET:
                    continue
                reread = m * k * (n // tn) + k * n * (m // tm)
                key = (steps >= MM_MIN_STEPS, tn if mode == 'tn' else 0, tm * tn * tk, -nk, -reread)
                if best_key is None or key > best_key:
                    best, best_key = (tm, tn, tk), key
    if best is None:
        raise ValueError(f"no matmul tile for {mode} {m}x{n}x{k}")
    return best


_MM_DIMS = {'nn': (((1,), (0,)), ((), ())), 'nt': (((1,), (1,)), ((), ())), 'tn': (((0,), (0,)), ((), ()))}


def _matmul(a, b, mode, out_dtype, name):
    if mode == 'nn':
        (m, k), (k2, n) = a.shape, b.shape
    elif mode == 'nt':
        (m, k), (n, k2) = a.shape, b.shape
    else:
        (k, m), (k2, n) = a.shape, b.shape
    assert k == k2, (a.shape, b.shape, mode)
    tm, tn, tk = _mm_tiles(mode, m, n, k, jnp.dtype(out_dtype).itemsize)
    nk = k // tk
    dims = _MM_DIMS[mode]

    def body(a_ref, b_ref, o_ref, acc_ref):
        kk = pl.program_id(2)
        part = lax.dot_general(a_ref[...], b_ref[...], dims, preferred_element_type=F32)
        if nk == 1:
            o_ref[...] = part.astype(o_ref.dtype)
            return

        @pl.when(kk == 0)
        def _():
            acc_ref[...] = part

        @pl.when((kk > 0) & (kk < nk - 1))
        def _():
            acc_ref[...] += part

        @pl.when(kk == nk - 1)
        def _():
            o_ref[...] = (acc_ref[...] + part).astype(o_ref.dtype)

    a_spec = pl.BlockSpec((tk, tm), lambda i, j, kk: (kk, i)) if mode == 'tn' else pl.BlockSpec((tm, tk), lambda i, j, kk: (i, kk))
    b_spec = pl.BlockSpec((tn, tk), lambda i, j, kk: (j, kk)) if mode == 'nt' else pl.BlockSpec((tk, tn), lambda i, j, kk: (kk, j))
    return pl.pallas_call(
        body, name=name, grid=(m // tm, n // tn, nk),
        in_specs=[a_spec, b_spec],
        out_specs=pl.BlockSpec((tm, tn), lambda i, j, kk: (i, j)),
        out_shape=jax.ShapeDtypeStruct((m, n), out_dtype),
        scratch_shapes=[pltpu.VMEM((tm, tn), F32)],
        compiler_params=_cparams(("parallel", "parallel", "arbitrary")),
    )(a, b)


ROW_TILE = 256


def _row_specs(rows, tm):
    return [pl.BlockSpec((tm, w), lambda i, ci=ci: (i, ci)) for (_, w, ci) in rows]


def _rw_forward(fn, rows, pars, outs, name):
    t = rows[0][0].shape[0]
    tm = min(ROW_TILE, t)
    nr, npar = len(rows), len(pars)

    def body(*refs):
        vals = [r[...].astype(F32) for r in refs[:nr]] + [p[...] for p in refs[nr:nr + npar]]
        res = fn(*vals)
        for o, v in zip(refs[nr + npar:], res):
            o[...] = v.astype(o.dtype)

    return pl.pallas_call(
        body, name=name, grid=(t // tm,),
        in_specs=_row_specs(rows, tm) + [pl.BlockSpec(p.shape, lambda i: (0, 0)) for p in pars],
        out_specs=tuple(pl.BlockSpec((tm, w), lambda i: (i, 0)) for (w, _) in outs),
        out_shape=tuple(jax.ShapeDtypeStruct((t, w), dt) for (w, dt) in outs),
        compiler_params=_cparams(("parallel",)),
    )(*[r[0] for r in rows], *pars)


def _rw_backward(fn, rows, pars, cots, drow_dtypes, name):
    t = rows[0][0].shape[0]
    tm = min(ROW_TILE, t)
    nr, npar, nc = len(rows), len(pars), len(cots)
    keep = [i for i, dt in enumerate(drow_dtypes) if dt is not None]

    def body(*refs):
        vals = [r[...].astype(F32) for r in refs[:nr]] + [p[...] for p in refs[nr:nr + npar]]
        cvals = tuple(c[...].astype(F32) for c in refs[nr + npar:nr + npar + nc])
        orefs = refs[nr + npar + nc:]
        _, vjp = jax.vjp(fn, *vals)
        g = vjp(cvals)
        for o, i in zip(orefs[:len(keep)], keep):
            o[...] = g[i].astype(o.dtype)
        first = pl.program_id(0) == 0
        for o, gi in zip(orefs[len(keep):], g[nr:]):
            @pl.when(first)
            def _(o=o, gi=gi):
                o[...] = gi

            @pl.when(jnp.logical_not(first))
            def _(o=o, gi=gi):
                o[...] += gi

    out_specs = [pl.BlockSpec((tm, rows[i][1]), lambda i_: (i_, 0)) for i in keep] + \
                [pl.BlockSpec(p.shape, lambda i_: (0, 0)) for p in pars]
    out_shape = [jax.ShapeDtypeStruct((t, rows[i][1]), drow_dtypes[i]) for i in keep] + \
                [jax.ShapeDtypeStruct(p.shape, F32) for p in pars]
    return pl.pallas_call(
        body, name=name, grid=(t // tm,),
        in_specs=_row_specs(rows, tm) + [pl.BlockSpec(p.shape, lambda i: (0, 0)) for p in pars] + _row_specs(cots, tm),
        out_specs=tuple(out_specs), out_shape=tuple(out_shape),
        compiler_params=_cparams(("arbitrary",)),
    )(*[r[0] for r in rows], *pars, *[c[0] for c in cots])


def _norm_fn(x, g):
    return (_rms(x, g),)


def _norm_skip_fn(x, g):
    return _rms(x, g), x


def _merge_fn(ga, gb, ya, yb):
    return (_sigmoid(ga) * ya + _sigmoid(gb) * yb,)


def _res_norm_fn(x, mo, g):
    x1 = x + mo
    return x1, _rms(x1, g)


def _loss_head(x1, fo, gf, target, name):
    t, d = x1.shape
    tm = min(ROW_TILE, t)

    def tile_loss(x2, g, tgt):
        err = _rms(x2, g) - tgt
        per_row = jnp.sum(err * err, axis=-1, keepdims=True) * (0.5 / d)
        return jnp.sum(per_row, axis=0, keepdims=True)

    def body(x1_ref, fo_ref, g_ref, t_ref, loss_ref, dx_ref, dxb_ref, dg_ref):
        x2 = x1_ref[...] + fo_ref[...]
        val, vjp = jax.vjp(functools.partial(tile_loss, tgt=t_ref[...]), x2, g_ref[...])
        dx2, dg = vjp(jnp.ones((1, 1), F32))
        dx_ref[...] = dx2
        dxb_ref[...] = dx2.astype(BF16)
        first = pl.program_id(0) == 0

        @pl.when(first)
        def _():
            loss_ref[...] = jnp.broadcast_to(val, loss_ref.shape)
            dg_ref[...] = dg

        @pl.when(jnp.logical_not(first))
        def _():
            loss_ref[...] += jnp.broadcast_to(val, loss_ref.shape)
            dg_ref[...] += dg

    row = pl.BlockSpec((tm, d), lambda i: (i, 0))
    vec = pl.BlockSpec((1, d), lambda i: (0, 0))
    return pl.pallas_call(
        body, name=name, grid=(t // tm,),
        in_specs=[row, row, vec, row],
        out_specs=(pl.BlockSpec((1, LANES), lambda i: (0, 0)), row, row, vec),
        out_shape=(jax.ShapeDtypeStruct((1, LANES), F32), jax.ShapeDtypeStruct((t, d), F32),
                   jax.ShapeDtypeStruct((t, d), BF16), jax.ShapeDtypeStruct((1, d), F32)),
        compiler_params=_cparams(("arbitrary",)),
    )(x1, fo, gf, target)


FFN_TILE_ROWS = 512
FFN_TILE_COLS = 256
FFN_COL_BLOCKS = FFN_HIDDEN // FFN_TILE_COLS


def _conv3_past(x, halo, w):
    rows = lax.broadcasted_iota(jnp.int32, x.shape, 0)
    x1 = jnp.where(rows == 0, halo[7:8, :], pltpu.roll(x, 1, 0))
    x2 = jnp.where(rows == 0, halo[6:7, :], jnp.where(rows == 1, halo[7:8, :], pltpu.roll(x, 2, 0)))
    return x * w[2:3] + x1 * w[1:2] + x2 * w[0:1], x1, x2


def _ffn_in_specs(tm, imap, jmap):
    per = tm // SUBLANES
    tile = lambda off: pl.BlockSpec((tm, FFN_TILE_COLS), lambda *g: (imap(*g), off + jmap(*g) % FFN_COL_BLOCKS))
    halo = lambda off: pl.BlockSpec((SUBLANES, FFN_TILE_COLS),
                                    lambda *g: (jnp.maximum(imap(*g) * per - 1, 0), off + jmap(*g) % FFN_COL_BLOCKS))
    wsp = lambda off: pl.BlockSpec((FFN_CONV, FFN_TILE_COLS), lambda *g: (0, off + jmap(*g) % FFN_COL_BLOCKS))
    return [tile(0), halo(0), wsp(0), tile(FFN_COL_BLOCKS), halo(FFN_COL_BLOCKS), wsp(FFN_COL_BLOCKS)]


def _ffn_act_forward(hpre, cw):
    t = hpre.shape[0]
    tm = min(FFN_TILE_ROWS, t)

    def body(hg, pg, wg, hu, pu, wu, out):
        live = (pl.program_id(0) > 0).astype(F32)
        cg, _, _ = _conv3_past(hg[...], pg[...] * live, wg[...])
        cu, _, _ = _conv3_past(hu[...], pu[...] * live, wu[...])
        out[...] = (cg * _sigmoid_plain(cg) * cu).astype(out.dtype)

    return pl.pallas_call(
        body, name="ffn_act_fwd", grid=(t // tm, FFN_COL_BLOCKS),
        in_specs=_ffn_in_specs(tm, lambda i, j: i, lambda i, j: j),
        out_specs=pl.BlockSpec((tm, FFN_TILE_COLS), lambda i, j: (i, j)),
        out_shape=jax.ShapeDtypeStruct((t, FFN_HIDDEN), BF16),
        compiler_params=_cparams(("parallel", "parallel")),
    )(hpre, hpre, cw, hpre, hpre, cw)


def _conv3_future(d, nxt, w):
    tm = d.shape[0]
    rows = lax.broadcasted_iota(jnp.int32, d.shape, 0)
    d1 = jnp.where(rows == tm - 1, nxt[0:1, :], pltpu.roll(d, tm - 1, 0))
    d2 = jnp.where(rows == tm - 1, nxt[1:2, :], jnp.where(rows == tm - 2, nxt[0:1, :], pltpu.roll(d, tm - 2, 0)))
    return d * w[2:3] + d1 * w[1:2] + d2 * w[0:1]


def _ffn_backward(hpre, cw, dact):
    t = hpre.shape[0]
    tm = min(FFN_TILE_ROWS, t)
    n_tiles = t // tm
    per = tm // SUBLANES

    def d_conv_out(cg, cu, d):
        s = _sigmoid_plain(cg)
        return d * cu * s * (1.0 + cg * (1.0 - s)), d * cg * s

    def body(hg, pg, ng, wg, hu, pu, nu, wu, da, dan, dhg, dhu, dwg, dwu):
        i = pl.program_id(1)
        live_prev = (i > 0).astype(F32)
        live_next = (i < n_tiles - 1).astype(F32)
        xg, xu = hg[...], hu[...]
        cg, g1, g2 = _conv3_past(xg, pg[...] * live_prev, wg[...])
        cu, u1, u2 = _conv3_past(xu, pu[...] * live_prev, wu[...])
        dg, du = d_conv_out(cg, cu, da[...])
        cgn, _, _ = _conv3_past(ng[...], hg[tm - SUBLANES:tm, :], wg[...])
        cun, _, _ = _conv3_past(nu[...], hu[tm - SUBLANES:tm, :], wu[...])
        dgn, dun = d_conv_out(cgn, cun, dan[...] * live_next)
        dhg[...] = _conv3_future(dg, dgn, wg[...]).astype(dhg.dtype)
        dhu[...] = _conv3_future(du, dun, wu[...]).astype(dhu.dtype)
        sums_g = [jnp.sum(xs * dg, axis=0, keepdims=True) for xs in (g2, g1, xg)]
        sums_u = [jnp.sum(xs * du, axis=0, keepdims=True) for xs in (u2, u1, xu)]

        @pl.when(i == 0)
        def _():
            for r_ in range(FFN_CONV):
                dwg[r_:r_ + 1, :] = sums_g[r_]
                dwu[r_:r_ + 1, :] = sums_u[r_]

        @pl.when(i > 0)
        def _():
            for r_ in range(FFN_CONV):
                dwg[r_:r_ + 1, :] += sums_g[r_]
                dwu[r_:r_ + 1, :] += sums_u[r_]

    nb = FFN_COL_BLOCKS
    nxt = lambda i: jnp.minimum((i + 1) * per, t // SUBLANES - 1)
    prv = lambda i: jnp.maximum(i * per - 1, 0)
    half = lambda off: [pl.BlockSpec((tm, FFN_TILE_COLS), lambda j, i: (i, off + j)),
                        pl.BlockSpec((SUBLANES, FFN_TILE_COLS), lambda j, i: (prv(i), off + j)),
                        pl.BlockSpec((SUBLANES, FFN_TILE_COLS), lambda j, i: (nxt(i), off + j)),
                        pl.BlockSpec((FFN_CONV, FFN_TILE_COLS), lambda j, i: (0, off + j))]
    tile = pl.BlockSpec((tm, FFN_TILE_COLS), lambda j, i: (i, j))
    taps = pl.BlockSpec((FFN_CONV, FFN_TILE_COLS), lambda j, i: (0, j))
    return pl.pallas_call(
        body, name="ffn_bwd", grid=(nb, n_tiles),
        in_specs=half(0) + half(nb) + [tile, pl.BlockSpec((SUBLANES, FFN_TILE_COLS), lambda j, i: (nxt(i), j))],
        out_specs=(tile, tile, taps, taps),
        out_shape=(jax.ShapeDtypeStruct((t, FFN_HIDDEN), BF16), jax.ShapeDtypeStruct((t, FFN_HIDDEN), BF16),
                   jax.ShapeDtypeStruct((FFN_CONV, FFN_HIDDEN), F32), jax.ShapeDtypeStruct((FFN_CONV, FFN_HIDDEN), F32)),
        compiler_params=_cparams(("parallel", "arbitrary")),
    )(hpre, hpre, hpre, cw, hpre, hpre, hpre, cw, dact, dact)


def _my_place():
    x, y, c = lax.axis_index("x"), lax.axis_index("y"), lax.axis_index("c")
    return x, y, c, 4 * x + 2 * y + c


N_CHIPS = 4


def _remote(src, dst, send_sem, recv_sem, dev):
    return pltpu.make_async_remote_copy(src_ref=src, dst_ref=dst, send_sem=send_sem, recv_sem=recv_sem, device_id=dev,
                                        device_id_type=pl.DeviceIdType.MESH)


def _chip_peer(x, y, k):
    return x ^ ((k >> 1) & 1), y ^ (k & 1)


def _all_gather_two_level(shard, name):
    r, w = shard.shape

    def body(src, out, send_sems, recv_sems, local_sem):
        x, y, c, me = _my_place()
        sibling = (x, y, 1 - c)
        mine = pltpu.make_async_copy(src, out.at[me], local_sem)
        mine.start()
        first = [_remote(src, out.at[me], send_sems.at[0], recv_sems.at[0], sibling)]
        for k in range(1, N_CHIPS):
            px, py = _chip_peer(x, y, k)
            first.append(_remote(src, out.at[me], send_sems.at[k], recv_sems.at[k], (px, py, c)))
        for cp in first:
            cp.start()
        passed = []
        for k in range(1, N_CHIPS):
            px, py = _chip_peer(x, y, k)
            landed = out.at[me ^ (2 * k)]
            _remote(src, landed, send_sems.at[k], recv_sems.at[k], (px, py, c)).wait_recv()
            fwd = _remote(landed, landed, send_sems.at[N_CHIPS - 1 + k], recv_sems.at[N_CHIPS - 1 + k], sibling)
            fwd.start()
            passed.append(fwd)
        _remote(src, out.at[me ^ 1], send_sems.at[0], recv_sems.at[0], sibling).wait_recv()
        for k in range(1, N_CHIPS):
            got = out.at[(me ^ 1) ^ (2 * k)]
            _remote(got, got, send_sems.at[N_CHIPS - 1 + k], recv_sems.at[N_CHIPS - 1 + k], sibling).wait_recv()
        for cp in first + passed:
            cp.wait_send()
        mine.wait()

    return pl.pallas_call(
        body, name=name,
        in_specs=[pl.BlockSpec(memory_space=pl.ANY)],
        out_specs=pl.BlockSpec(memory_space=pl.ANY),
        out_shape=jax.ShapeDtypeStruct((N_DEV, r, w), shard.dtype),
        scratch_shapes=[pltpu.SemaphoreType.DMA((N_DEV - 1,)), pltpu.SemaphoreType.DMA((N_DEV - 1,)), pltpu.SemaphoreType.DMA],
    )(shard)


def _device_peer(x, y, c, k):
    px, py, pc = x ^ ((k >> 2) & 1), y ^ ((k >> 1) & 1), c ^ (k & 1)
    return (px, py, pc), 4 * px + 2 * py + pc


_HBM = pl.BlockSpec(memory_space=pltpu.HBM)
_SEM = pl.BlockSpec(memory_space=pltpu.SEMAPHORE)


def _gather_start(shard, name):
    def body(src, land, send_sems, recv_sems, src_thru, land_thru, token):
        x, y, c, me = _my_place()
        for k in range(1, N_DEV):
            dev, _ = _device_peer(x, y, c, k)
            _remote(src, land.at[me], send_sems.at[k], recv_sems.at[k], dev).start()
        token[...] = jnp.zeros_like(token)

    landing = lax.empty((N_DEV,) + shard.shape, shard.dtype)
    return pl.pallas_call(
        body, name=name,
        out_shape=(pltpu.SemaphoreType.DMA((N_DEV,)), pltpu.SemaphoreType.DMA((N_DEV,)), pltpu.HBM(shard.shape, shard.dtype),
                   pltpu.HBM(landing.shape, landing.dtype), jax.ShapeDtypeStruct((SUBLANES, LANES), F32)),
        in_specs=(_HBM, _HBM), out_specs=(_SEM, _SEM, _HBM, _HBM, pl.BlockSpec(memory_space=pltpu.VMEM)),
        input_output_aliases={0: 2, 1: 3},
        compiler_params=pltpu.CompilerParams(has_side_effects=pltpu.SideEffectType.DATAFLOW_SIDE_EFFECTING),
    )(pltpu.with_memory_space_constraint(shard, pltpu.HBM), pltpu.with_memory_space_constraint(landing, pltpu.HBM))


def _gather_wait(send_sems, recv_sems, shard, landing, after, name):
    n_after = len(after)

    def body(*refs):
        src, land, send_sems, recv_sems = refs[:4]
        x, y, c, _ = _my_place()
        for k in range(1, N_DEV):
            dev, idx = _device_peer(x, y, c, k)
            cp = _remote(src, land.at[idx], send_sems.at[k], recv_sems.at[k], dev)
            cp.wait_send()
            cp.wait_recv()

    return pl.pallas_call(
        body, name=name,
        out_shape=(pltpu.HBM(shard.shape, shard.dtype), pltpu.HBM(landing.shape, landing.dtype)),
        in_specs=(_HBM, _HBM, _SEM, _SEM) + (pl.BlockSpec(memory_space=pl.ANY),) * n_after, out_specs=(_HBM, _HBM),
        input_output_aliases={0: 0, 1: 1},
        compiler_params=pltpu.CompilerParams(has_side_effects=pltpu.SideEffectType.DATAFLOW_SIDE_EFFECTING),
    )(shard, landing, send_sems, recv_sems, *after)[1]


def _slab_push_start(slabs, name):
    na = len(slabs)

    def body(*refs):
        srcs, lands = refs[:na], refs[na:2 * na]
        send_sems, recv_sems = refs[2 * na], refs[2 * na + 1]
        token = refs[-1]
        x, y, c, me = _my_place()
        for i in range(na):
            for k in range(1, N_DEV):
                dev, idx = _device_peer(x, y, c, k)
                s = i * N_DEV + k
                _remote(srcs[i].at[idx], lands[i].at[me], send_sems.at[s], recv_sems.at[s], dev).start()
        token[...] = jnp.zeros_like(token)

    hbm_shapes = [pltpu.HBM(a.shape, a.dtype) for a in slabs]
    ins = [pltpu.with_memory_space_constraint(a, pltpu.HBM) for a in slabs]
    ins += [pltpu.with_memory_space_constraint(lax.empty(a.shape, a.dtype), pltpu.HBM) for a in slabs]
    out = pl.pallas_call(
        body, name=name,
        out_shape=(pltpu.SemaphoreType.DMA((na * N_DEV,)), pltpu.SemaphoreType.DMA((na * N_DEV,)), *hbm_shapes, *hbm_shapes,
                   jax.ShapeDtypeStruct((SUBLANES, LANES), F32)),
        in_specs=(_HBM,) * (2 * na), out_specs=(_SEM, _SEM) + (_HBM,) * (2 * na) + (pl.BlockSpec(memory_space=pltpu.VMEM),),
        input_output_aliases={i: 2 + i for i in range(2 * na)},
        compiler_params=pltpu.CompilerParams(has_side_effects=pltpu.SideEffectType.DATAFLOW_SIDE_EFFECTING),
    )(*ins)
    return out[0], out[1], list(out[2:2 + na]), list(out[2 + na:2 + 2 * na]), out[-1]


def _slab_push_wait(send_sems, recv_sems, slabs, landings, after, name):
    na = len(slabs)

    def body(*refs):
        srcs, lands = refs[:na], refs[na:2 * na]
        send_sems, recv_sems = refs[2 * na], refs[2 * na + 1]
        x, y, c, me = _my_place()
        for i in range(na):
            for k in range(1, N_DEV):
                dev, idx = _device_peer(x, y, c, k)
                s = i * N_DEV + k
                cp = _remote(srcs[i].at[idx], lands[i].at[idx], send_sems.at[s], recv_sems.at[s], dev)
                cp.wait_send()
                cp.wait_recv()

    hbm_shapes = tuple(pltpu.HBM(a.shape, a.dtype) for a in slabs)
    out = pl.pallas_call(
        body, name=name, out_shape=hbm_shapes + hbm_shapes,
        in_specs=(_HBM,) * (2 * na) + (_SEM, _SEM) + (pl.BlockSpec(memory_space=pl.ANY),) * len(after),
        out_specs=(_HBM,) * (2 * na), input_output_aliases={i: i for i in range(2 * na)},
        compiler_params=pltpu.CompilerParams(has_side_effects=pltpu.SideEffectType.DATAFLOW_SIDE_EFFECTING),
    )(*slabs, *landings, send_sems, recv_sems, *after)
    return list(out[na:])


def _pair_exchange(arrays, name):
    na = len(arrays)

    def body(*refs):
        srcs, dsts, (send_sems, recv_sems) = refs[:na], refs[na:2 * na], refs[2 * na:]
        x, y, c, _ = _my_place()
        sibling = (x, y, 1 - c)
        copies = []
        for i in range(na):
            for q in range(N_CHIPS):
                s = i * N_CHIPS + q
                copies.append(_remote(srcs[i].at[2 * q + 1 - c], dsts[i].at[q], send_sems.at[s], recv_sems.at[s], sibling))
        for cp in copies:
            cp.start()
        for cp in copies:
            cp.wait_recv()
        for cp in copies:
            cp.wait_send()

    hbm = pl.BlockSpec(memory_space=pl.ANY)
    return pl.pallas_call(
        body, name=name, in_specs=[hbm] * na, out_specs=tuple([hbm] * na),
        out_shape=tuple(jax.ShapeDtypeStruct((N_CHIPS,) + a.shape[1:], a.dtype) for a in arrays),
        scratch_shapes=[pltpu.SemaphoreType.DMA((na * N_CHIPS,)), pltpu.SemaphoreType.DMA((na * N_CHIPS,))],
    )(*arrays)


ELEMENTWISE_COLS = 256


def _pair_sum(slabs, recv, core, out_dtype, name):
    _, r, w = slabs.shape
    tc = ELEMENTWISE_COLS

    def body(core_ref, mine, theirs, out):
        out[...] = (mine[...] + theirs[...]).astype(out.dtype)

    grid_spec = pltpu.PrefetchScalarGridSpec(
        num_scalar_prefetch=1, grid=(N_CHIPS, w // tc),
        in_specs=[pl.BlockSpec((None, r, tc), lambda q, j, core_ref: (2 * q + core_ref[0], 0, j)),
                  pl.BlockSpec((None, r, tc), lambda q, j, core_ref: (q, 0, j))],
        out_specs=pl.BlockSpec((None, r, tc), lambda q, j, core_ref: (q, 0, j)))
    return pl.pallas_call(body, name=name, grid_spec=grid_spec,
                          out_shape=jax.ShapeDtypeStruct((N_CHIPS, r, w), out_dtype),
                          compiler_params=_cparams(("parallel", "parallel")))(core, slabs, recv)


def _chip_exchange(arrays, name):
    na = len(arrays)

    def body(*refs):
        srcs, dsts, (send_sems, recv_sems, local_sems) = refs[:na], refs[na:2 * na], refs[2 * na:]
        x, y, c, _ = _my_place()
        chip = 2 * x + y
        own = [pltpu.make_async_copy(srcs[i].at[chip], dsts[i].at[chip], local_sems.at[i]) for i in range(na)]
        for cp in own:
            cp.start()
        sends, arrivals = [], []
        for i in range(na):
            for k in range(1, N_CHIPS):
                px, py = _chip_peer(x, y, k)
                s = i * N_CHIPS + k
                sends.append(_remote(srcs[i].at[chip ^ k], dsts[i].at[chip], send_sems.at[s], recv_sems.at[s], (px, py, c)))
                arrivals.append(_remote(srcs[i].at[chip], dsts[i].at[chip ^ k], send_sems.at[s], recv_sems.at[s], (px, py, c)))
        for cp in sends:
            cp.start()
        for cp in arrivals:
            cp.wait_recv()
        for cp in sends:
            cp.wait_send()
        for cp in own:
            cp.wait()

    hbm = pl.BlockSpec(memory_space=pl.ANY)
    return pl.pallas_call(
        body, name=name, in_specs=[hbm] * na, out_specs=tuple([hbm] * na),
        out_shape=tuple(jax.ShapeDtypeStruct(a.shape, a.dtype) for a in arrays),
        scratch_shapes=[pltpu.SemaphoreType.DMA((na * N_CHIPS,)), pltpu.SemaphoreType.DMA((na * N_CHIPS,)),
                        pltpu.SemaphoreType.DMA((na,))],
    )(*arrays)


def _adamw_update(g, w, m, v):
    c1 = 1.0 / (1.0 - ADAM_B1 ** ADAM_STEP)
    c2 = 1.0 / (1.0 - ADAM_B2 ** ADAM_STEP)
    mn = ADAM_B1 * m + (1.0 - ADAM_B1) * g
    vn = ADAM_B2 * v + (1.0 - ADAM_B2) * (g * g)
    return -ADAM_LR * ((mn * c1) / (jnp.sqrt(vn * c2) + ADAM_EPS) + ADAM_WD * w), mn, vn


def _reduce_adamw(parts, w, m, v, name):
    n_parts, r, wd = parts.shape
    tc = ELEMENTWISE_COLS

    def body(p_ref, w_ref, m_ref, v_ref, g_out, d_out, m_out, v_out):
        g = p_ref[0].astype(F32)
        for s in range(1, n_parts):
            g = g + p_ref[s].astype(F32)
        g_out[...] = g
        d_out[...], m_out[...], v_out[...] = _adamw_update(g, w_ref[...], m_ref[...], v_ref[...])

    blk = pl.BlockSpec((r, tc), lambda j: (0, j))
    shp = jax.ShapeDtypeStruct((r, wd), F32)
    return pl.pallas_call(
        body, name=name, grid=(wd // tc,),
        in_specs=[pl.BlockSpec((n_parts, r, tc), lambda j: (0, 0, j)), blk, blk, blk],
        out_specs=(blk, blk, blk, blk), out_shape=(shp, shp, shp, shp),
        compiler_params=_cparams(("parallel",)),
    )(parts, w, m, v)


def _reduce_landed_adamw(landing, own, me, w, m, v, name):
    n_parts, r, wd = landing.shape
    tc = ELEMENTWISE_COLS

    def body(me_ref, land_ref, own_ref, w_ref, m_ref, v_ref, g_out, d_out, m_out, v_out):
        mine = own_ref[...].astype(F32)
        g = None
        for s in range(n_parts):
            part = jnp.where(me_ref[0] == s, mine, land_ref[s].astype(F32))
            g = part if g is None else g + part
        g_out[...] = g
        d_out[...], m_out[...], v_out[...] = _adamw_update(g, w_ref[...], m_ref[...], v_ref[...])

    blk = pl.BlockSpec((r, tc), lambda j, me_ref: (0, j))
    shp = jax.ShapeDtypeStruct((r, wd), F32)
    grid_spec = pltpu.PrefetchScalarGridSpec(
        num_scalar_prefetch=1, grid=(wd // tc,),
        in_specs=[pl.BlockSpec((n_parts, r, tc), lambda j, me_ref: (0, 0, j)),
                  pl.BlockSpec((None, r, tc), lambda j, me_ref: (me_ref[0], 0, j)), blk, blk, blk],
        out_specs=(blk, blk, blk, blk))
    return pl.pallas_call(body, name=name, grid_spec=grid_spec, out_shape=(shp, shp, shp, shp),
                          compiler_params=_cparams(("parallel",)))(me, landing, own, w, m, v)


PACK_W = 1024


def _pad_heads(a, slots):
    lead = a.shape[:-1]
    a = a.reshape(lead + (slots, RWKV_HEAD_DIM))
    a = jnp.pad(a, [(0, 0)] * (len(lead) + 1) + [(0, LANES - RWKV_HEAD_DIM)])
    return a.reshape(lead + (slots * LANES,))


def _flat_pack(arrs, dtype, row_mult):
    flat = jnp.concatenate([a.reshape(-1).astype(dtype) for a in arrs])
    n = flat.shape[0]
    rows = -(-n // PACK_W)
    rows = -(-rows // row_mult) * row_mult
    return jnp.pad(flat, (0, rows * PACK_W - n)).reshape(rows, PACK_W)


def _row_pack(arrs, dtype, row_mult):
    parts = [a.astype(dtype) if a.shape[1] == PACK_W else a.astype(dtype).reshape(-1, PACK_W) for a in arrs]
    rows = sum(p.shape[0] for p in parts)
    pad = -(-rows // row_mult) * row_mult - rows
    return jnp.concatenate(parts + ([jnp.zeros((pad, PACK_W), dtype)] if pad else []), axis=0)


def _unpack_row_gathered(g, names, shard_shapes):
    out, r0 = {}, 0
    for n in names:
        s = shard_shapes[n]
        rows = s[0] * s[1] // PACK_W
        seg = g[:, r0:r0 + rows, :]
        r0 += rows
        if s[1] == PACK_W:
            assert SHARD_AXIS[n] == 0
            out[n] = seg.reshape(N_DEV * s[0], s[1])
        else:
            assert SHARD_AXIS[n] == 1
            out[n] = jnp.transpose(seg.reshape((N_DEV,) + tuple(s)), (1, 0, 2)).reshape(s[0], N_DEV * s[1])
    return out


def _unpack_gathered(g, names, shard_shapes):
    flat = g.reshape(N_DEV, -1)
    out, off = {}, 0
    for n in names:
        s = shard_shapes[n]
        size = s[0] * s[1]
        seg = flat[:, off:off + size].reshape((N_DEV,) + tuple(s))
        off += size
        if SHARD_AXIS[n] == 1:
            out[n] = jnp.transpose(seg, (1, 0, 2)).reshape(s[0], N_DEV * s[1])
        else:
            out[n] = seg.reshape(N_DEV * s[0], s[1])
    return out


def _shard_major(full, axis):
    a, b = full.shape
    if axis == 1:
        return jnp.transpose(full.reshape(a, N_DEV, b // N_DEV), (1, 0, 2)).reshape(N_DEV, -1)
    return full.reshape(N_DEV, -1)


def _prepare_weights(full, rep):
    w = full['w_in']
    d = w.shape[1]
    rkv = jnp.pad(w[0:1536].reshape(3 * RWKV_HEADS, RWKV_HEAD_DIM, d), ((0, 0), (0, LANES - RWKV_HEAD_DIM), (0, 0)))
    w_cat = jnp.concatenate([
        w[3848:4872], w[4872:5896], rkv.reshape(3 * RWKV_HEADS * LANES, d), w[1792:3328], w[3328:3840],
        w[1536:1792], jnp.pad(w[3840:3848], ((0, LANES - 8), (0, 0))), jnp.zeros((LANES, d), w.dtype)], axis=0)
    assert w_cat.shape[0] == CAT_W
    mu = rep['rwkv_mu']
    vecs = [mu[0:512], mu[512:1024], mu[1024:1536], rep['rwkv_w0'], rep['rwkv_a0'], rep['rwkv_k_k'], rep['rwkv_k_a'],
            rep['rwkv_ln_w'], rep['rwkv_ln_b'], rep['rwkv_r_k'].reshape(-1)]
    ppack = jnp.stack([jnp.pad(v.reshape(RWKV_HEADS, RWKV_HEAD_DIM), ((0, 0), (0, LANES - RWKV_HEAD_DIM))) for v in vecs], axis=1)
    ppack = jnp.pad(ppack, ((0, 0), (0, 16 - len(vecs)), (0, 0)))
    mulo = mu[1536:1792].reshape(1, 2 * LANES)
    wl = jnp.zeros((3, 2 * LANES, RWKV_HEADS * LANES), F32)
    wl = wl.at[0, 0:64].set(_pad_heads(full['rwkv_w2'], RWKV_HEADS))
    wl = wl.at[1, 64:128].set(_pad_heads(full['rwkv_a2'], RWKV_HEADS))
    wl = wl.at[2, 128:256].set(_pad_heads(full['rwkv_g2'], RWKV_HEADS))
    wl = jnp.transpose(wl.reshape(3, 2 * LANES, RWKV_HEADS, LANES), (2, 0, 1, 3))
    cw = full['gdn_conv_w'].reshape(GDN_CONV, 3, GDN_HEADS, LANES)
    cwpack = jnp.pad(jnp.transpose(cw, (2, 1, 0, 3)), ((0, 0), (0, 0), (0, SUBLANES - GDN_CONV), (0, 0)))
    gpar = jnp.zeros((SUBLANES, LANES), F32)
    gpar = gpar.at[0, 0:GDN_HEADS].set(rep['gdn_a_log']).at[1, 0:GDN_HEADS].set(rep['gdn_dt_bias']).at[2].set(rep['gdn_norm_w'])
    return dict(w_cat=w_cat, ffn_cw=full['ffn_conv_w'], ppack=ppack, mulo=mulo, wl=wl, cwpack=cwpack, gpar=gpar,
                g1=rep['norm1_g'].reshape(1, -1), g2=rep['norm2_g'].reshape(1, -1), gf=rep['final_g'].reshape(1, -1))


def _prepare_late_weights(full):
    rp = full['rwkv_proj']
    rproj = jnp.pad(rp.reshape(RWKV_HEADS, RWKV_HEAD_DIM, -1), ((0, 0), (0, LANES - RWKV_HEAD_DIM), (0, 0))).reshape(RWKV_HEADS * LANES, -1)
    return dict(rproj=rproj, gproj=full['gdn_proj'], w_out=full['w_out'], ffn_up=full['ffn_up'], ffn_down=full['ffn_down'])


def _local_step(x, target, p, late_weights, push_grads):
    d = x.shape[1]
    full_w = lambda a: (a, a.shape[1], 0)
    (u,) = _rw_forward(_norm_fn, [full_w(x)], [p['g1']], [(d, BF16)], "norm1")
    p_cat = _matmul(u, p['w_cat'], 'nt', F32, "proj_in")
    ya_pre, st_r = _rwkv_forward(p_cat, p['ppack'], p['mulo'], p['wl'])
    yb_pre, st_g = _gdn_forward(p_cat, p['cwpack'], p['gpar'])
    p = {**p, **late_weights((ya_pre, yb_pre))}
    ya = _matmul(ya_pre, p['rproj'], 'nn', F32, "rwkv_proj")
    yb = _matmul(yb_pre, p['gproj'], 'nn', F32, "gdn_proj")
    gates = [(p_cat, d, OFF_GA // d), (p_cat, d, OFF_GB // d)]
    (mixed,) = _rw_forward(_merge_fn, gates + [full_w(ya), full_w(yb)], [], [(d, BF16)], "merge")
    mo = _matmul(mixed, p['w_out'], 'nn', F32, "out_proj")
    x1, n2 = _rw_forward(_res_norm_fn, [full_w(x), full_w(mo)], [p['g2']], [(d, F32), (d, BF16)], "res_norm2")
    hpre = _matmul(n2, p['ffn_up'], 'nt', F32, "ffn_up")
    act = _ffn_act_forward(hpre, p['ffn_cw'])
    fo = _matmul(act, p['ffn_down'], 'nn', F32, "ffn_down")
    loss_vec, dx2, dx2b, dgf = _loss_head(x1, fo, p['gf'], target, "loss_head")

    dact = _matmul(dx2b, p['ffn_down'], 'nt', F32, "d_act")
    dw_down = _matmul(act, dx2b, 'tn', BF16, "dw_ffn_down")
    dh_gate, dh_up, dcw_gate, dcw_up = _ffn_backward(hpre, p['ffn_cw'], dact)
    dh = jnp.concatenate([dh_gate, dh_up], axis=1)
    dcw_f = jnp.concatenate([dcw_gate, dcw_up], axis=1)
    dn2 = _matmul(dh, p['ffn_up'], 'nn', F32, "d_norm2")
    dw_up = _matmul(dh, n2, 'tn', BF16, "dw_ffn_up")
    token = push_grads({'ffn_down': dw_down, 'ffn_up': dw_up})
    dx1, dx1b, dg2 = _rw_backward(_res_norm_fn, [full_w(x), full_w(mo)], [p['g2'] + token], [full_w(dx2), full_w(dn2)],
                                  [F32, BF16], "res_norm2_bwd")
    dmixed = _matmul(dx1b, p['w_out'], 'nt', F32, "d_mixed")
    dw_out = _matmul(mixed, dx1b, 'tn', BF16, "dw_out")
    dga, dgb, dya, dyb = _rw_backward(_merge_fn, gates + [full_w(ya), full_w(yb)], [], [full_w(dmixed)],
                                      [BF16, BF16, BF16, BF16], "merge_bwd")
    d_ya_pre = _matmul(dya, p['rproj'], 'nt', F32, "d_rwkv_out")
    dw_rproj = _matmul(ya_pre, dya, 'tn', F32, "dw_rwkv_proj")
    d_yb_pre = _matmul(dyb, p['gproj'], 'nt', F32, "d_gdn_out")
    dw_gproj = _matmul(yb_pre, dyb, 'tn', F32, "dw_gdn_proj")
    dpr, dpk, dpv, dplo, dpp, dml, dwl = _rwkv_backward(p_cat, p['ppack'], p['mulo'], p['wl'], st_r, d_ya_pre)
    dq, dk, dv, dz, dab, dcw_g, dgp = _gdn_backward(p_cat, p['cwpack'], p['gpar'], st_g, d_yb_pre)
    t = x.shape[0]
    dp_cat = jnp.concatenate([dga, dgb, dpr, dpk, dpv, dq, dk, dv, dz, dplo.astype(BF16), dab.astype(BF16),
                              jnp.zeros((t, LANES), BF16)], axis=1)
    dw_cat = _matmul(dp_cat, u, 'tn', BF16, "dw_in")
    dw_in = jnp.concatenate([dw_cat[OFF_RKV:OFF_QKV].reshape(3 * RWKV_HEADS, LANES, d)[:, :RWKV_HEAD_DIM].reshape(-1, d),
                             dw_cat[OFF_LO:OFF_AB], dw_cat[OFF_QKV:OFF_Z], dw_cat[OFF_Z:OFF_LO], dw_cat[OFF_AB:OFF_AB + 8],
                             dw_cat[OFF_GA:OFF_GB], dw_cat[OFF_GB:OFF_RKV]], axis=0)
    token = push_grads({'w_out': dw_out, 'w_in': dw_in})
    du = _matmul(dp_cat, p['w_cat'], 'nn', F32, "d_norm1")
    grad_x, dg1 = _rw_backward(_norm_skip_fn, [full_w(x)], [p['g1'] + token], [full_w(du), full_w(dx1)], [F32], "norm1_bwd")

    heads = lambda row: dpp[:, row, :RWKV_HEAD_DIM].reshape(-1)
    lora = lambda j, lo_, hi_: jnp.transpose(dwl[:, j, lo_:hi_, :RWKV_HEAD_DIM], (1, 0, 2)).reshape(hi_ - lo_, RWKV_WIDTH)
    grads = {
        'norm1_g': dg1[0],
        'w_in': dw_in,
        'rwkv_mu': jnp.concatenate([heads(0), heads(1), heads(2), jnp.sum(dml[:, 0, :], axis=0)]),
        'rwkv_w0': heads(3), 'rwkv_a0': heads(4), 'rwkv_k_k': heads(5), 'rwkv_k_a': heads(6),
        'rwkv_ln_w': heads(7), 'rwkv_ln_b': heads(8), 'rwkv_r_k': heads(9).reshape(RWKV_HEADS, RWKV_HEAD_DIM),
        'rwkv_w2': lora(0, 0, 64), 'rwkv_a2': lora(1, 64, 128), 'rwkv_g2': lora(2, 128, 256),
        'rwkv_proj': dw_rproj.reshape(RWKV_HEADS, LANES, -1)[:, :RWKV_HEAD_DIM].reshape(RWKV_WIDTH, -1),
        'gdn_conv_w': jnp.transpose(dcw_g[:, :, :GDN_CONV, :], (2, 1, 0, 3)).reshape(GDN_CONV, 3 * GDN_WIDTH),
        'gdn_a_log': jnp.sum(dgp[:, 0, :GDN_HEADS], axis=0), 'gdn_dt_bias': jnp.sum(dgp[:, 1, :GDN_HEADS], axis=0),
        'gdn_norm_w': jnp.sum(dgp[:, 2, :], axis=0),
        'gdn_proj': dw_gproj, 'w_out': dw_out, 'norm2_g': dg2[0], 'ffn_up': dw_up, 'ffn_conv_w': dcw_f,
        'ffn_down': dw_down, 'final_g': dgf[0],
    }
    return loss_vec, grad_x, grads


def kernel(x, norm1_g, w_in, rwkv_mu, rwkv_w0, rwkv_w2, rwkv_a0, rwkv_a2, rwkv_g2, rwkv_k_k, rwkv_k_a, rwkv_r_k, rwkv_ln_w, rwkv_ln_b, rwkv_proj, gdn_conv_w, gdn_a_log, gdn_dt_bias, gdn_norm_w, gdn_proj, w_out, norm2_g, ffn_up, ffn_conv_w, ffn_down, final_g, loss_target, m_norm1_g, m_w_in, m_rwkv_mu, m_rwkv_w0, m_rwkv_w2, m_rwkv_a0, m_rwkv_a2, m_rwkv_g2, m_rwkv_k_k, m_rwkv_k_a, m_rwkv_r_k, m_rwkv_ln_w, m_rwkv_ln_b, m_rwkv_proj, m_gdn_conv_w, m_gdn_a_log, m_gdn_dt_bias, m_gdn_norm_w, m_gdn_proj, m_w_out, m_norm2_g, m_ffn_up, m_ffn_conv_w, m_ffn_down, m_final_g, v_norm1_g, v_w_in, v_rwkv_mu, v_rwkv_w0, v_rwkv_w2, v_rwkv_a0, v_rwkv_a2, v_rwkv_g2, v_rwkv_k_k, v_rwkv_k_a, v_rwkv_r_k, v_rwkv_ln_w, v_rwkv_ln_b, v_rwkv_proj, v_gdn_conv_w, v_gdn_a_log, v_gdn_dt_bias, v_gdn_norm_w, v_gdn_proj, v_w_out, v_norm2_g, v_ffn_up, v_ffn_conv_w, v_ffn_down, v_final_g):
    given = dict(zip(WEIGHT_NAMES, (norm1_g, w_in, rwkv_mu, rwkv_w0, rwkv_w2, rwkv_a0, rwkv_a2, rwkv_g2, rwkv_k_k, rwkv_k_a, rwkv_r_k,
                                    rwkv_ln_w, rwkv_ln_b, rwkv_proj, gdn_conv_w, gdn_a_log, gdn_dt_bias, gdn_norm_w, gdn_proj, w_out,
                                    norm2_g, ffn_up, ffn_conv_w, ffn_down, final_g)))
    mom1 = dict(zip(WEIGHT_NAMES, (m_norm1_g, m_w_in, m_rwkv_mu, m_rwkv_w0, m_rwkv_w2, m_rwkv_a0, m_rwkv_a2, m_rwkv_g2, m_rwkv_k_k,
                                   m_rwkv_k_a, m_rwkv_r_k, m_rwkv_ln_w, m_rwkv_ln_b, m_rwkv_proj, m_gdn_conv_w, m_gdn_a_log,
                                   m_gdn_dt_bias, m_gdn_norm_w, m_gdn_proj, m_w_out, m_norm2_g, m_ffn_up, m_ffn_conv_w, m_ffn_down,
                                   m_final_g)))
    mom2 = dict(zip(WEIGHT_NAMES, (v_norm1_g, v_w_in, v_rwkv_mu, v_rwkv_w0, v_rwkv_w2, v_rwkv_a0, v_rwkv_a2, v_rwkv_g2, v_rwkv_k_k,
                                   v_rwkv_k_a, v_rwkv_r_k, v_rwkv_ln_w, v_rwkv_ln_b, v_rwkv_proj, v_gdn_conv_w, v_gdn_a_log,
                                   v_gdn_dt_bias, v_gdn_norm_w, v_gdn_proj, v_w_out, v_norm2_g, v_ffn_up, v_ffn_conv_w, v_ffn_down,
                                   v_final_g)))
    def strip(n, a):
        a = a if n == 'final_g' else a.reshape(a.shape[1:])
        return a.T if n in TRANSPOSED else a

    local = {n: strip(n, a) for n, a in given.items()}
    shard_shapes = {n: local[n].shape for n in SHARD_AXIS}
    sharded = BIG_SHARDED + SMALL_SHARDED

    late_names = [n for n in BIG_SHARDED if n != 'w_in']
    g_in = _all_gather_two_level(_row_pack([local['w_in']], BF16, 16), "gather_w_in")
    g_small = _all_gather_two_level(_flat_pack([local[n] for n in SMALL_SHARDED], F32, SUBLANES), "gather_small")
    late_pack, g_in, g_small = lax.optimization_barrier((_row_pack([local[n] for n in late_names], BF16, 16), g_in, g_small))
    send_sems, recv_sems, late_pack, landing, token = _gather_start(late_pack, "gather_late_start")
    full = _unpack_row_gathered(g_in, ['w_in'], shard_shapes)
    full.update(_unpack_gathered(g_small, SMALL_SHARDED, shard_shapes))
    rep = {n: local[n] for n in REPLICATED}
    rep['norm1_g'] = rep['norm1_g'] + token[0, 0]

    def late_weights(after):
        got = _gather_wait(send_sems, recv_sems, late_pack, landing, after, "gather_late_wait")
        me = 4 * lax.axis_index("x") + 2 * lax.axis_index("y") + lax.axis_index("c")
        slot = lax.broadcasted_iota(jnp.int32, (N_DEV, 1, 1), 0)
        got = jnp.where(slot == me, late_pack[None], got)
        return _prepare_late_weights(_unpack_row_gathered(got, late_names, shard_shapes))

    pushes = []
    me = 4 * lax.axis_index("x") + 2 * lax.axis_index("y") + lax.axis_index("c")
    slot = lax.broadcasted_iota(jnp.int32, (N_DEV, 1, 1), 0)

    def push_grads(group):
        names = list(group)
        slabs = [group[n].reshape(N_DEV, -1, group[n].shape[1]) for n in names]
        send_sems, recv_sems, slabs, landings, token = _slab_push_start(slabs, "grad_push_start_" + "_".join(names))
        pushes.append((names, send_sems, recv_sems, slabs, landings))
        return token[0, 0]

    loss_vec, grad_x, grads = _local_step(x[0], loss_target[0], _prepare_weights(full, rep), late_weights, push_grads)

    landed = {}
    for names, send_sems, recv_sems, slabs, landings in pushes:
        got = _slab_push_wait(send_sems, recv_sems, slabs, landings, (grad_x,), "grad_push_wait_" + "_".join(names))
        for n, slab, land in zip(names, slabs, got):
            landed[n] = (land, slab)

    small_sharded = ['rwkv_proj', 'gdn_proj'] + SMALL_SHARDED
    small_names = small_sharded + REPLICATED
    rep_vec = jnp.concatenate([grads[n].reshape(-1) for n in REPLICATED] + [loss_vec[0, 0:1]])
    slab_small = jnp.concatenate([_shard_major(grads[n], SHARD_AXIS[n]) for n in small_sharded] +
                                 [jnp.broadcast_to(rep_vec[None], (N_DEV, rep_vec.shape[0]))], axis=1)
    small_rows = -(-slab_small.shape[1] // (PACK_W * SUBLANES)) * SUBLANES
    slab_small = jnp.pad(slab_small, ((0, 0), (0, small_rows * PACK_W - slab_small.shape[1]))).reshape(N_DEV, small_rows, PACK_W)
    core = lax.axis_index("c").astype(jnp.int32).reshape(1)
    (from_sibling,) = _pair_exchange([slab_small], "grad_pair_exchange")
    chip_small = _pair_sum(slab_small, from_sibling, core, F32, "grad_pair_sum_small")
    (parts_small,) = _chip_exchange([chip_small], "grad_chip_exchange")

    def pack_local(src):
        flat = jnp.concatenate([strip(n, src[n]).reshape(-1) for n in small_names])
        return jnp.pad(flat, (0, small_rows * PACK_W - flat.shape[0])).reshape(small_rows, PACK_W)

    results = [({}, None) for _ in range(4)]
    me_arr = me.astype(jnp.int32).reshape(1)
    for n in ROW_SHARDED:
        packs = _reduce_landed_adamw(*landed[n], me_arr, local[n], strip(n, mom1[n]), strip(n, mom2[n]), "adamw_" + n)
        for (out, _), pk in zip(results, packs):
            out[n] = (pk.T if n in TRANSPOSED else pk).reshape(given[n].shape)
    packs = _reduce_adamw(parts_small, pack_local(given), pack_local(mom1), pack_local(mom2), "adamw_small")
    for i, pk in enumerate(packs):
        flat, off = pk.reshape(-1), 0
        for n in small_names:
            size = int(np.prod(given[n].shape))
            results[i][0][n] = flat[off:off + size].reshape(given[n].shape)
            off += size
        results[i] = (results[i][0], flat[off])
    (g_out, loss), (d_out, _), (m_out, _), (v_out, _) = results
    return (loss, grad_x[None], *[g_out[n] for n in WEIGHT_NAMES], *[d_out[n] for n in WEIGHT_NAMES],
            *[m_out[n] for n in WEIGHT_NAMES], *[v_out[n] for n in WEIGHT_NAMES])
```

```python
import functools

import jax
import jax.numpy as jnp
import numpy as np
from jax import lax
from jax.experimental import pallas as pl
from jax.experimental.pallas import tpu as pltpu

F32 = jnp.float32
BF16 = jnp.bfloat16

N_DEV = 8
D_MODEL = 1024
CHUNK = 64
RWKV_HEADS = 8
RWKV_HEAD_DIM = 64
RWKV_WIDTH = 512
GDN_HEADS = 4
GDN_HEAD_DIM = 128
GDN_WIDTH = 512
GDN_CONV = 4
FFN_HIDDEN = 2816
FFN_CONV = 3
NORM_EPS = 1e-6
L2_EPS = 1e-6
RWKV_GN_EPS = 64e-5
LANES = 128
SUBLANES = 8
VMEM_LIMIT = 56 * 1024 * 1024

ADAM_LR = 0.001
ADAM_B1 = 0.9
ADAM_B2 = 0.999
ADAM_EPS = 1e-08
ADAM_WD = 0.01
ADAM_STEP = 10

OFF_GA, OFF_GB, OFF_RKV, OFF_QKV, OFF_Z, OFF_LO, OFF_AB, CAT_W = 0, 1024, 2048, 5120, 6656, 7168, 7424, 7680
RWKV_HB = 8
RWKV_STEP_CHUNKS = 2
RWKV_TILE = RWKV_STEP_CHUNKS * CHUNK
GDN_HB = 4
GDN_STEP_CHUNKS = 4
GDN_TILE = GDN_STEP_CHUNKS * CHUNK

WEIGHT_NAMES = ['norm1_g', 'w_in', 'rwkv_mu', 'rwkv_w0', 'rwkv_w2', 'rwkv_a0', 'rwkv_a2', 'rwkv_g2', 'rwkv_k_k', 'rwkv_k_a',
                'rwkv_r_k', 'rwkv_ln_w', 'rwkv_ln_b', 'rwkv_proj', 'gdn_conv_w', 'gdn_a_log', 'gdn_dt_bias', 'gdn_norm_w',
                'gdn_proj', 'w_out', 'norm2_g', 'ffn_up', 'ffn_conv_w', 'ffn_down', 'final_g']
BIG_SHARDED = ['w_in', 'ffn_up', 'ffn_down', 'w_out', 'rwkv_proj', 'gdn_proj']
SMALL_SHARDED = ['rwkv_w2', 'rwkv_a2', 'rwkv_g2', 'gdn_conv_w', 'ffn_conv_w']
TRANSPOSED = ('w_in', 'ffn_up')
SHARD_AXIS = {'w_in': 0, 'ffn_up': 0, 'ffn_down': 0, 'w_out': 0, 'rwkv_proj': 1, 'gdn_proj': 1,
              'rwkv_w2': 1, 'rwkv_a2': 1, 'rwkv_g2': 1, 'gdn_conv_w': 1, 'ffn_conv_w': 1}
REPLICATED = [n for n in WEIGHT_NAMES if n not in SHARD_AXIS]
ROW_SHARDED = ['w_in', 'ffn_up', 'ffn_down', 'w_out']


def _cparams(sem=None):
    kw = dict(vmem_limit_bytes=VMEM_LIMIT)
    if sem is not None:
        kw['dimension_semantics'] = sem
    return pltpu.CompilerParams(**kw)


_NN, _NT, _TN = 'nn', 'nt', 'tn'
_DIMS_2D = {'nn': (((1,), (0,)), ((), ())), 'nt': (((1,), (1,)), ((), ())), 'tn': (((0,), (0,)), ((), ()))}
_DIMS_3D = {'nn': (((2,), (1,)), ((0,), (0,))), 'nt': (((2,), (2,)), ((0,), (0,))), 'tn': (((1,), (1,)), ((0,), (0,)))}


def _dg(a, b, kind):
    return lax.dot_general(a, b, (_DIMS_2D if a.ndim == 2 else _DIMS_3D)[kind], preferred_element_type=F32)


def _dot1(a, b, kind):
    return _dg(a.astype(BF16), b.astype(BF16), kind)


@jax.custom_vjp
def _dhi(a, b):
    return _dot1(a, b, _NN)


_dhi.defvjp(lambda a, b: (_dot1(a, b, _NN), (a, b)),
            lambda res, ct: (_dot1(ct, res[1], _NT), _dot1(res[0], ct, _TN)))


@jax.custom_vjp
def _dnt(a, b):
    return _dot1(a, b, _NT)


_dnt.defvjp(lambda a, b: (_dot1(a, b, _NT), (a, b)),
            lambda res, ct: (_dot1(ct, res[1], _NN), _dot1(ct, res[0], _TN)))


@jax.custom_vjp
def _dtn(a, b):
    return _dot1(a, b, _TN)


_dtn.defvjp(lambda a, b: (_dot1(a, b, _TN), (a, b)),
            lambda res, ct: (_dot1(res[1], ct, _NT), _dot1(res[0], ct, _NN)))


def _split3(x):
    x1 = x.astype(BF16)
    r1 = x - x1.astype(F32)
    x2 = r1.astype(BF16)
    return x1, x2, (r1 - x2.astype(F32)).astype(BF16)


def _dot_exact_lhs(sel, x, kind):
    parts = [_dg(sel, xi, kind) for xi in _split3(x)]
    return parts[0] + parts[1] + parts[2]


def _tril_ones(like):
    c = like.shape[-2]
    ri, ci = _iotas(c)
    return jnp.broadcast_to((ri >= ci).astype(BF16), like.shape[:-2] + (c, c))


@jax.custom_vjp
def _cumsum_rows(x):
    return _dot_exact_lhs(_tril_ones(x), x, _NN)


_cumsum_rows.defvjp(lambda x: (_dot_exact_lhs(_tril_ones(x), x, _NN), None),
                    lambda _, ct: (_dot_exact_lhs(_tril_ones(ct), ct, _TN),))


@jax.custom_vjp
def _lane_sum_as_row(x):
    return _dot_exact_lhs(jnp.ones(x.shape, BF16), x, _NT)


def _lane_sum_as_row_bwd(_, ct):
    ones = jnp.ones(ct.shape[:-1] + (LANES,), BF16)
    parts = [_dg(ci, ones, _TN) for ci in _split3(ct)]
    return (parts[0] + parts[1] + parts[2],)


_lane_sum_as_row.defvjp(lambda x: (_dot_exact_lhs(jnp.ones(x.shape, BF16), x, _NT), None), _lane_sum_as_row_bwd)


def _shift_rows(x, halo, s):
    rows = lax.broadcasted_iota(jnp.int32, x.shape, 0)
    out = pltpu.roll(x, s, 0)
    for i in range(s):
        out = jnp.where(rows == i, halo[SUBLANES - s + i:SUBLANES - s + i + 1, :], out)
    return out


def _unshift_rows(g, carry, s):
    c = g.shape[0]
    rows = lax.broadcasted_iota(jnp.int32, g.shape, 0)
    out = pltpu.roll(g, c - s, 0)
    for i in range(s):
        out = jnp.where(rows == c - s + i, carry[i:i + 1, :], out)
    return out


def _sigmoid_plain(z):
    return 1.0 / (1.0 + jnp.exp(-z))


def _sigmoid_value(z):
    t = jnp.exp(-jnp.abs(z))
    r = 1.0 / (1.0 + t)
    return jnp.where(z >= 0, r, t * r)


@jax.custom_vjp
def _sigmoid(z):
    return _sigmoid_value(z)


def _sigmoid_fwd(z):
    s = _sigmoid_value(z)
    return s, s


_sigmoid.defvjp(_sigmoid_fwd, lambda s, ct: (ct * s * (1.0 - s),))


def _silu(z):
    return z * _sigmoid(z)


def _softplus(z):
    return jnp.maximum(z, 0.0) + jnp.log(1.0 + jnp.exp(-jnp.abs(z)))


def _rms(t, gain):
    return t * lax.rsqrt(jnp.mean(t * t, axis=-1, keepdims=True) + NORM_EPS) * gain


def _iotas(c):
    return lax.broadcasted_iota(jnp.int32, (c, c), 0), lax.broadcasted_iota(jnp.int32, (c, c), 1)


def _unit_lower_inverse(xm, eye):
    t = eye + xm
    p = xm
    for _ in range(5):
        p = _dhi(p, p)
        t = t + _dhi(t, p)
    return t


def _rwkv_head(pr, pk, pv, plo, qr, qk, qv, qlo, s0, pp, mulo, wl):
    c = pr.shape[1]
    n_heads = s0.shape[0]
    n_chunks = pr.shape[0] // n_heads
    ri, ci = _iotas(c)
    if n_chunks > 1:
        pp = jnp.concatenate([pp] * n_chunks, axis=0)
        wl = jnp.concatenate([wl] * n_chunks, axis=0)

    def mix(p, q, mu):
        return p + (q - p) * mu

    r = mix(pr, qr, pp[:, 0:1])
    k = mix(pk, qk, pp[:, 1:2])
    v = mix(pv, qv, pp[:, 2:3])
    lo = mix(plo, qlo, mulo)
    w0, a0, k_k, k_a, ln_w, ln_b, r_k = (pp[:, i:i + 1] for i in range(3, 10))

    def per_head(t):
        return jnp.concatenate([jnp.broadcast_to(t[i], (n_heads,) + t.shape[1:]) for i in range(n_chunks)], axis=0)

    zw = _dhi(per_head(jnp.tanh(lo)), wl[:, 0])
    za = _dhi(per_head(lo), wl[:, 1])
    g = _dhi(per_head(_sigmoid(lo)), wl[:, 2])
    w_log = -_softplus(-(w0 + zw)) - 0.5
    lw = -jnp.exp(w_log)
    a = _sigmoid(a0 + za)
    kk = k * k_k
    kk = kk * lax.rsqrt(jnp.sum(kk * kk, axis=-1, keepdims=True) + L2_EPS)
    k2 = k * (1.0 + (a - 1.0) * k_a)
    an = -kk
    b = kk * a
    causal = ri >= ci
    strict = ri > ci
    eye = (ri == ci).astype(F32)
    cl = _cumsum_rows(lw)
    ecl = jnp.exp(-cl)
    at = an * jnp.exp(cl - lw)
    bt = b * ecl
    kt = k2 * ecl
    rt = r * jnp.exp(cl)
    a_ab = jnp.where(strict, _dnt(at, bt), 0.0)
    a_ak = jnp.where(strict, _dnt(at, kt), 0.0)
    tinv = _unit_lower_inverse(a_ab, eye)
    akv = _dhi(a_ak, v)
    r_b = jnp.where(causal, _dnt(rt, bt), 0.0)
    rkv = _dhi(jnp.where(causal, _dnt(rt, kt), 0.0), v)
    cl_end = jnp.sum(lw, axis=1, keepdims=True)
    dec_end = jnp.exp(cl_end - cl)
    b_end = b * dec_end
    sv = _dtn(v, k2 * dec_end)
    e_end = jnp.exp(cl_end)
    state, ys = s0, []
    for i in range(n_chunks):
        sl = slice(i * n_heads, (i + 1) * n_heads)
        u = _dhi(tinv[sl], _dnt(at[sl], state) + akv[sl])
        ys.append(_dnt(rt[sl], state) + _dhi(r_b[sl], u) + rkv[sl])
        state = state * e_end[sl] + _dtn(u, b_end[sl]) + sv[sl]
    y = jnp.concatenate(ys, axis=0) if n_chunks > 1 else ys[0]
    s1 = state
    m = (lax.broadcasted_iota(jnp.int32, (1, LANES), 1) < RWKV_HEAD_DIM).astype(F32)
    mean = jnp.sum(y, axis=-1, keepdims=True) * (1.0 / RWKV_HEAD_DIM)
    yc = (y - mean) * m
    var = jnp.sum(yc * yc, axis=-1, keepdims=True) * (1.0 / RWKV_HEAD_DIM)
    yn = yc * lax.rsqrt(var + RWKV_GN_EPS) * ln_w + ln_b
    y2 = yn + jnp.sum(r * k2 * r_k, axis=-1, keepdims=True) * v
    return y2 * g, s1


def _gdn_head(xq, xk, xv, z, ab, s0, cw, gp, oha, ohb):
    c = z.shape[1]
    n_heads = s0.shape[0]
    n_chunks = z.shape[0] // n_heads
    ri, ci = _iotas(c)
    cw = jnp.concatenate([cw] * n_chunks, axis=0) if n_chunks > 1 else cw

    def conv(xs, w):
        out = xs[0] * w[:, GDN_CONV - 1:GDN_CONV]
        for s in range(1, GDN_CONV):
            out = out + xs[s] * w[:, GDN_CONV - 1 - s:GDN_CONV - s]
        return out

    q = _silu(conv(xq, cw[:, 0]))
    k = _silu(conv(xk, cw[:, 1]))
    v = _silu(conv(xv, cw[:, 2]))
    q = q * lax.rsqrt(jnp.sum(q * q, axis=-1, keepdims=True) + L2_EPS) * (GDN_HEAD_DIM ** -0.5)
    k = k * lax.rsqrt(jnp.sum(k * k, axis=-1, keepdims=True) + L2_EPS)
    gg = -jnp.exp(gp[0:1]) * _softplus(ab + gp[1:2])
    beta = jnp.sum(_sigmoid(ab) * ohb, axis=-1, keepdims=True)
    causal = ri >= ci
    strict = ri > ci
    eye = (ri == ci).astype(F32)
    gcm = _cumsum_rows(gg * oha)
    gc = jnp.sum(gcm, axis=-1, keepdims=True)
    gc_row = _lane_sum_as_row(gcm)
    dec = jnp.where(causal, jnp.exp(jnp.where(causal, gc - gc_row, 0.0)), 0.0)
    kb = k * beta
    vb = v * beta
    lm = jnp.where(strict, _dnt(kb, k) * dec, 0.0)
    tinv = _unit_lower_inverse(-lm, eye)
    egc = jnp.exp(gc)
    u = _dhi(tinv, vb)
    wk = _dhi(tinv, kb * egc)
    attn = jnp.where(causal, _dnt(q, k) * dec, 0.0)
    g_last = gc[:, c - 1:c, :]
    q_dec = q * egc
    k_dec = k * jnp.exp(g_last - gc)
    e_last = jnp.exp(g_last)
    state, outs = s0, []
    for i in range(n_chunks):
        sl = slice(i * n_heads, (i + 1) * n_heads)
        v_new = u[sl] - _dhi(wk[sl], state)
        outs.append(_dhi(q_dec[sl], state) + _dhi(attn[sl], v_new))
        state = state * e_last[sl] + _dtn(k_dec[sl], v_new)
    o = jnp.concatenate(outs, axis=0) if n_chunks > 1 else outs[0]
    return _rms(o, gp[2:3]) * _silu(z), state


def _head_id(grp, i, per_step, heads):
    return i if per_step == heads else grp * per_step + i


def _head_range(grp, per_step, heads):
    return slice(None) if per_step == heads else pl.ds(grp * per_step, per_step)


def _rwkv_specs(nmap):
    hb, groups = RWKV_HB, RWKV_HEADS // RWKV_HB
    cb = OFF_RKV // (hb * LANES)
    specs = []
    for j in range(3):
        specs.append(pl.BlockSpec((RWKV_TILE, hb * LANES), lambda n, g, j=j: (nmap(n), cb + j * groups + g)))
    specs.append(pl.BlockSpec((RWKV_TILE, 2 * LANES), lambda n, g: (nmap(n), OFF_LO // (2 * LANES))))
    per = RWKV_TILE // SUBLANES
    for j in range(3):
        specs.append(pl.BlockSpec((SUBLANES, hb * LANES),
                                  lambda n, g, j=j: (jnp.maximum(nmap(n) * per - 1, 0), cb + j * groups + g)))
    specs.append(pl.BlockSpec((SUBLANES, 2 * LANES), lambda n, g: (jnp.maximum(nmap(n) * per - 1, 0), OFF_LO // (2 * LANES))))
    specs.append(pl.BlockSpec((hb, 16, LANES), lambda n, g: (g, 0, 0)))
    specs.append(pl.BlockSpec((1, 2 * LANES), lambda n, g: (0, 0)))
    specs.append(pl.BlockSpec((hb, 3, 2 * LANES, LANES), lambda n, g: (g, 0, 0, 0)))
    return specs


def _rwkv_operands(refs, halos, live):
    pr, pk, pv, plo = refs
    hr, hk, hv, hlo = halos
    cur, prev = [], []
    for x, hx in ((pr, hr), (pk, hk), (pv, hv)):
        tiles = [_lane_block(x, h) for h in range(RWKV_HB)]
        cur.append(_chunk_batch(tiles))
        prev.append(_chunk_batch([_shift_rows(t_, _lane_block(hx, h) * live, 1) for h, t_ in enumerate(tiles)]))
    lo = plo[...]
    cur.append(_chunk_batch([lo]))
    prev.append(_chunk_batch([_shift_rows(lo, hlo[...] * live, 1)]))
    return cur, prev


def _rwkv_forward(p_cat, ppack, mulo, wl):
    t = p_cat.shape[0]
    n_chunks = t // RWKV_TILE

    def body(pr, pk, pv, plo, hr, hk, hv, hlo, pp, ml, w, out, st_out, s_scr):
        n, grp = pl.program_id(0), pl.program_id(1)

        hsl = _head_range(grp, RWKV_HB, RWKV_HEADS)

        @pl.when(n == 0)
        def _():
            s_scr[hsl] = jnp.zeros((RWKV_HB, LANES, LANES), F32)

        live = (n > 0).astype(F32)
        cur, prev = _rwkv_operands((pr, pk, pv, plo), (hr, hk, hv, hlo), live)
        s0 = s_scr[hsl]
        st_out[...] = s0
        o, s1 = _rwkv_head(*cur, *prev, s0, pp[...], ml[...], w[...])
        for h, tile in enumerate(_head_tiles(o, RWKV_HB)):
            out[:, h * LANES:(h + 1) * LANES] = tile.astype(out.dtype)
        s_scr[hsl] = s1

    return pl.pallas_call(
        body, name="rwkv_fwd", grid=(n_chunks, RWKV_HEADS // RWKV_HB),
        in_specs=_rwkv_specs(lambda n: n),
        out_specs=(pl.BlockSpec((RWKV_TILE, RWKV_HB * LANES), lambda n, g: (n, g)),
                   pl.BlockSpec((None, RWKV_HB, LANES, LANES), lambda n, g: (n, g, 0, 0))),
        out_shape=(jax.ShapeDtypeStruct((t, RWKV_HEADS * LANES), BF16),
                   jax.ShapeDtypeStruct((n_chunks, RWKV_HEADS, LANES, LANES), F32)),
        scratch_shapes=[pltpu.VMEM((RWKV_HEADS, LANES, LANES), F32)],
        compiler_params=_cparams(("arbitrary", "arbitrary")),
    )(p_cat, p_cat, p_cat, p_cat, p_cat, p_cat, p_cat, p_cat, ppack, mulo, wl)


def _rwkv_backward(p_cat, ppack, mulo, wl, states, d_out):
    t = p_cat.shape[0]
    n_chunks = t // RWKV_TILE
    last = n_chunks - 1

    def body(pr, pk, pv, plo, hr, hk, hv, hlo, pp, ml, w, st, dy, dpr, dpk, dpv, dplo, dpp, dml, dw, ds_scr, car_scr, carlo_scr):
        n, grp = pl.program_id(0), pl.program_id(1)

        hsl = _head_range(grp, RWKV_HB, RWKV_HEADS)
        gi = _head_id(grp, 0, 1, RWKV_HEADS // RWKV_HB)

        @pl.when(n == 0)
        def _():
            ds_scr[hsl] = jnp.zeros((RWKV_HB, LANES, LANES), F32)
            car_scr[hsl] = jnp.zeros((RWKV_HB, 3 * SUBLANES, LANES), F32)
            carlo_scr[gi] = jnp.zeros((SUBLANES, 2 * LANES), F32)

        @pl.when((n == 0) & (grp == 0))
        def _():
            dpp[...] = jnp.zeros(dpp.shape, F32)
            dml[...] = jnp.zeros(dml.shape, F32)
            dw[...] = jnp.zeros(dw.shape, F32)

        live = (n < last).astype(F32)
        cur, prev = _rwkv_operands((pr, pk, pv, plo), (hr, hk, hv, hlo), live)
        _, vjp = jax.vjp(_rwkv_head, *cur, *prev, st[...], pp[...], ml[...], w[...])
        g = vjp((_chunk_batch([_lane_block(dy, h) for h in range(RWKV_HB)]), ds_scr[hsl]))
        outs = (dpr, dpk, dpv)
        d_cur = [_head_tiles(g[j], RWKV_HB) for j in range(3)]
        d_prev = [_head_tiles(g[4 + j], RWKV_HB) for j in range(3)]
        for i in range(RWKV_HB):
            sl = slice(i * LANES, (i + 1) * LANES)
            h = _head_id(grp, i, RWKV_HB, RWKV_HEADS)
            car = car_scr[h]
            for j in range(3):
                tot = d_cur[j][i] + _unshift_rows(d_prev[j][i], car[SUBLANES * j:SUBLANES * (j + 1), :], 1)
                outs[j][:, sl] = tot.astype(outs[j].dtype)
                car_scr[h, SUBLANES * j:SUBLANES * (j + 1), :] = d_prev[j][i][0:SUBLANES, :]
        (dlo_cur,), (dlo_prev,) = _head_tiles(g[3], 1), _head_tiles(g[7], 1)
        dlo = dlo_cur + _unshift_rows(dlo_prev, carlo_scr[gi], 1)
        carlo_scr[gi] = dlo_prev[0:SUBLANES, :]
        ds_scr[hsl] = g[8]
        dpp[hsl] += g[9]
        dml[0, 0:1, :] += g[10]
        dw[hsl] += g[11]

        @pl.when(grp == 0)
        def _():
            dplo[...] = dlo

        @pl.when(grp > 0)
        def _():
            dplo[...] += dlo

    rev = lambda n: last - n
    in_specs = _rwkv_specs(rev) + [
        pl.BlockSpec((None, RWKV_HB, LANES, LANES), lambda n, g: (rev(n), g, 0, 0)),
        pl.BlockSpec((RWKV_TILE, RWKV_HB * LANES), lambda n, g: (rev(n), g)),
    ]
    hw = RWKV_HEADS * LANES
    return pl.pallas_call(
        body, name="rwkv_bwd", grid=(n_chunks, RWKV_HEADS // RWKV_HB),
        in_specs=in_specs,
        out_specs=(pl.BlockSpec((RWKV_TILE, RWKV_HB * LANES), lambda n, g: (rev(n), g)),
                   pl.BlockSpec((RWKV_TILE, RWKV_HB * LANES), lambda n, g: (rev(n), g)),
                   pl.BlockSpec((RWKV_TILE, RWKV_HB * LANES), lambda n, g: (rev(n), g)),
                   pl.BlockSpec((RWKV_TILE, 2 * LANES), lambda n, h: (rev(n), 0)),
                   pl.BlockSpec((RWKV_HEADS, 16, LANES), lambda n, h: (0, 0, 0)),
                   pl.BlockSpec((RWKV_HEADS, SUBLANES, 2 * LANES), lambda n, h: (0, 0, 0)),
                   pl.BlockSpec((RWKV_HEADS, 3, 2 * LANES, LANES), lambda n, h: (0, 0, 0, 0))),
        out_shape=(jax.ShapeDtypeStruct((t, hw), BF16), jax.ShapeDtypeStruct((t, hw), BF16), jax.ShapeDtypeStruct((t, hw), BF16),
                   jax.ShapeDtypeStruct((t, 2 * LANES), F32),
                   jax.ShapeDtypeStruct((RWKV_HEADS, 16, LANES), F32),
                   jax.ShapeDtypeStruct((RWKV_HEADS, SUBLANES, 2 * LANES), F32),
                   jax.ShapeDtypeStruct((RWKV_HEADS, 3, 2 * LANES, LANES), F32)),
        scratch_shapes=[pltpu.VMEM((RWKV_HEADS, LANES, LANES), F32),
                        pltpu.VMEM((RWKV_HEADS, 3 * SUBLANES, LANES), F32),
                        pltpu.VMEM((RWKV_HEADS, SUBLANES, 2 * LANES), F32)],
        compiler_params=_cparams(("arbitrary", "arbitrary")),
    )(p_cat, p_cat, p_cat, p_cat, p_cat, p_cat, p_cat, p_cat, ppack, mulo, wl, states, d_out)


def _gdn_specs(nmap):
    per = GDN_TILE // SUBLANES
    hb, groups = GDN_HB, GDN_HEADS // GDN_HB
    cb = OFF_QKV // (hb * LANES)
    specs = []
    for j in range(3):
        specs.append(pl.BlockSpec((GDN_TILE, hb * LANES), lambda n, g, j=j: (nmap(n), cb + j * groups + g)))
    for j in range(3):
        specs.append(pl.BlockSpec((SUBLANES, hb * LANES),
                                  lambda n, g, j=j: (jnp.maximum(nmap(n) * per - 1, 0), cb + j * groups + g)))
    specs.append(pl.BlockSpec((GDN_TILE, hb * LANES), lambda n, g: (nmap(n), OFF_Z // (hb * LANES) + g)))
    specs.append(pl.BlockSpec((GDN_TILE, LANES), lambda n, g: (nmap(n), OFF_AB // LANES)))
    specs.append(pl.BlockSpec((hb, 3, SUBLANES, LANES), lambda n, g: (g, 0, 0, 0)))
    specs.append(pl.BlockSpec((SUBLANES, LANES), lambda n, g: (0, 0)))
    return specs


def _conv_taps(x, halo):
    return (x,) + tuple(_shift_rows(x, halo, s) for s in range(1, GDN_CONV))


def _onehots(grp):
    nb = GDN_STEP_CHUNKS * GDN_HB
    lane = lax.broadcasted_iota(jnp.int32, (nb, 1, LANES), 2)
    head = lax.broadcasted_iota(jnp.int32, (nb, 1, LANES), 0) % GDN_HB + _head_id(grp, 0, GDN_HB, GDN_HEADS)
    return (lane == head).astype(F32), (lane == GDN_HEADS + head).astype(F32)


def _chunk_batch(tiles):
    n_chunks = tiles[0].shape[0] // CHUNK
    return jnp.stack([t_[i * CHUNK:(i + 1) * CHUNK, :] for i in range(n_chunks) for t_ in tiles])


def _head_tiles(batch, n_heads=GDN_HB):
    n_chunks = batch.shape[0] // n_heads
    return [jnp.concatenate([batch[i * n_heads + h] for i in range(n_chunks)], axis=0) for h in range(n_heads)]


def _lane_block(ref, h):
    return ref[:, h * LANES:(h + 1) * LANES]


def _gdn_taps(refs, halos, live):
    out = []
    for x, hx in zip(refs, halos):
        per_head = [_conv_taps(_lane_block(x, h), _lane_block(hx, h) * live) for h in range(GDN_HB)]
        out.append(tuple(_chunk_batch([per_head[h][s] for h in range(GDN_HB)]) for s in range(GDN_CONV)))
    return out


def _gdn_forward(p_cat, cwpack, gpar):
    t = p_cat.shape[0]
    n_chunks = t // GDN_TILE

    def body(xq, xk, xv, hq, hk, hv, z, ab, cw, gp, out, st_out, s_scr):
        n, grp = pl.program_id(0), pl.program_id(1)

        hsl = _head_range(grp, GDN_HB, GDN_HEADS)

        @pl.when(n == 0)
        def _():
            s_scr[hsl] = jnp.zeros((GDN_HB, LANES, LANES), F32)

        live = (n > 0).astype(F32)
        oha, ohb = _onehots(grp)
        s0 = s_scr[hsl]
        st_out[...] = s0
        taps = _gdn_taps((xq, xk, xv), (hq, hk, hv), live)
        zb = _chunk_batch([_lane_block(z, h) for h in range(GDN_HB)])
        abb = _chunk_batch([ab[...]] * GDN_HB)
        o, s1 = _gdn_head(*taps, zb, abb, s0, cw[...], gp[...], oha, ohb)
        for h, tile in enumerate(_head_tiles(o)):
            out[:, h * LANES:(h + 1) * LANES] = tile.astype(out.dtype)
        s_scr[hsl] = s1

    return pl.pallas_call(
        body, name="gdn_fwd", grid=(n_chunks, GDN_HEADS // GDN_HB),
        in_specs=_gdn_specs(lambda n: n),
        out_specs=(pl.BlockSpec((GDN_TILE, GDN_HB * LANES), lambda n, g: (n, g)),
                   pl.BlockSpec((None, GDN_HB, LANES, LANES), lambda n, g: (n, g, 0, 0))),
        out_shape=(jax.ShapeDtypeStruct((t, GDN_WIDTH), BF16),
                   jax.ShapeDtypeStruct((n_chunks, GDN_HEADS, LANES, LANES), F32)),
        scratch_shapes=[pltpu.VMEM((GDN_HEADS, LANES, LANES), F32)],
        compiler_params=_cparams(("arbitrary", "arbitrary")),
    )(p_cat, p_cat, p_cat, p_cat, p_cat, p_cat, p_cat, p_cat, cwpack, gpar)


def _gdn_backward(p_cat, cwpack, gpar, states, d_out):
    t = p_cat.shape[0]
    n_chunks = t // GDN_TILE
    last = n_chunks - 1

    def body(xq, xk, xv, hq, hk, hv, z, ab, cw, gp, st, dy, dq, dk, dv, dz, dab, dcw, dgp, ds_scr, car_scr):
        n, grp = pl.program_id(0), pl.program_id(1)

        hsl = _head_range(grp, GDN_HB, GDN_HEADS)

        @pl.when(n == 0)
        def _():
            ds_scr[hsl] = jnp.zeros((GDN_HB, LANES, LANES), F32)
            car_scr[hsl] = jnp.zeros((GDN_HB, 3 * GDN_CONV, SUBLANES, LANES), F32)

        @pl.when((n == 0) & (grp == 0))
        def _():
            dcw[...] = jnp.zeros(dcw.shape, F32)
            dgp[...] = jnp.zeros(dgp.shape, F32)

        live = (n < last).astype(F32)
        oha, ohb = _onehots(grp)
        fn = functools.partial(_gdn_head, oha=oha, ohb=ohb)
        taps = _gdn_taps((xq, xk, xv), (hq, hk, hv), live)
        zb = _chunk_batch([_lane_block(z, h) for h in range(GDN_HB)])
        abb = _chunk_batch([ab[...]] * GDN_HB)
        _, vjp = jax.vjp(fn, *taps, zb, abb, st[...], cw[...], gp[...])
        g = vjp((_chunk_batch([_lane_block(dy, h) for h in range(GDN_HB)]), ds_scr[hsl]))
        outs = (dq, dk, dv)
        tap_tiles = [[_head_tiles(g[j][s]) for s in range(GDN_CONV)] for j in range(3)]
        dz_tiles = _head_tiles(g[3])
        for i in range(GDN_HB):
            sl = slice(i * LANES, (i + 1) * LANES)
            h = _head_id(grp, i, GDN_HB, GDN_HEADS)
            for j in range(3):
                tot = tap_tiles[j][0][i]
                for s in range(1, GDN_CONV):
                    slot = j * GDN_CONV + s
                    tot = tot + _unshift_rows(tap_tiles[j][s][i], car_scr[h, slot], s)
                    car_scr[h, slot] = tap_tiles[j][s][i][0:SUBLANES, :]
                outs[j][:, sl] = tot.astype(outs[j].dtype)
            dz[:, sl] = dz_tiles[i].astype(dz.dtype)
        dab_tiles = _head_tiles(g[4])
        dab_sum = dab_tiles[0]
        for h in range(1, GDN_HB):
            dab_sum = dab_sum + dab_tiles[h]
        ds_scr[hsl] = g[5]
        dcw[hsl] += g[6]
        dgp[0] += g[7]

        @pl.when(grp == 0)
        def _():
            dab[...] = dab_sum

        @pl.when(grp > 0)
        def _():
            dab[...] += dab_sum

    rev = lambda n: last - n
    in_specs = _gdn_specs(rev) + [
        pl.BlockSpec((None, GDN_HB, LANES, LANES), lambda n, g: (rev(n), g, 0, 0)),
        pl.BlockSpec((GDN_TILE, GDN_HB * LANES), lambda n, g: (rev(n), g)),
    ]
    blk = pl.BlockSpec((GDN_TILE, GDN_HB * LANES), lambda n, g: (rev(n), g))
    return pl.pallas_call(
        body, name="gdn_bwd", grid=(n_chunks, GDN_HEADS // GDN_HB),
        in_specs=in_specs,
        out_specs=(blk, blk, blk, blk,
                   pl.BlockSpec((GDN_TILE, LANES), lambda n, h: (rev(n), 0)),
                   pl.BlockSpec((GDN_HEADS, 3, SUBLANES, LANES), lambda n, h: (0, 0, 0, 0)),
                   pl.BlockSpec((GDN_HEADS, SUBLANES, LANES), lambda n, h: (0, 0, 0))),
        out_shape=(jax.ShapeDtypeStruct((t, GDN_WIDTH), BF16), jax.ShapeDtypeStruct((t, GDN_WIDTH), BF16),
                   jax.ShapeDtypeStruct((t, GDN_WIDTH), BF16), jax.ShapeDtypeStruct((t, GDN_WIDTH), BF16),
                   jax.ShapeDtypeStruct((t, LANES), F32),
                   jax.ShapeDtypeStruct((GDN_HEADS, 3, SUBLANES, LANES), F32),
                   jax.ShapeDtypeStruct((GDN_HEADS, SUBLANES, LANES), F32)),
        scratch_shapes=[pltpu.VMEM((GDN_HEADS, LANES, LANES), F32),
                        pltpu.VMEM((GDN_HEADS, 3 * GDN_CONV, SUBLANES, LANES), F32)],
        compiler_params=_cparams(("arbitrary", "arbitrary")),
    )(p_cat, p_cat, p_cat, p_cat, p_cat, p_cat, p_cat, p_cat, cwpack, gpar, states, d_out)


MM_VMEM_BUDGET = 44 * 1024 * 1024
MM_MIN_STEPS = 4


def _mm_tiles(mode, m, n, k, out_bytes):
    tms = [t for t in (2048, 1024, 768, 512, 256, 128, 64) if m % t == 0 and (mode != 'tn' or t % LANES == 0)]
    tns = [t for t in (1408, 1024, 768, 512, 256, 128) if n % t == 0]
    tks = [t for t in (2048, 1920, 1408, 1024, 512, 256, 128, 64) if k % t == 0]
    best, best_key = None, None
    for tm in tms:
        for tn in tns:
            for tk in tks:
                nk = k // tk
                vmem = 2 * (tm * tk * 2 + tk * tn * 2 + tm * tn * out_bytes) + (tm * tn * 4 if nk > 1 else 0)
                steps = (m // tm) * (n // tn) * nk
                if vmem > MM_VMEM_BUDGET:
                    continue
                reread = m * k * (n // tn) + k * n * (m // tm)
                key = (steps >= MM_MIN_STEPS, tn if mode == 'tn' else 0, tm * tn * tk, -nk, -reread)
                if best_key is None or key > best_key:
                    best, best_key = (tm, tn, tk), key
    if best is None:
        raise ValueError(f"no matmul tile for {mode} {m}x{n}x{k}")
    return best


_MM_DIMS = {'nn': (((1,), (0,)), ((), ())), 'nt': (((1,), (1,)), ((), ())), 'tn': (((0,), (0,)), ((), ()))}


def _matmul(a, b, mode, out_dtype, name):
    if mode == 'nn':
        (m, k), (k2, n) = a.shape, b.shape
    elif mode == 'nt':
        (m, k), (n, k2) = a.shape, b.shape
    else:
        (k, m), (k2, n) = a.shape, b.shape
    assert k == k2, (a.shape, b.shape, mode)
    tm, tn, tk = _mm_tiles(mode, m, n, k, jnp.dtype(out_dtype).itemsize)
    nk = k // tk
    dims = _MM_DIMS[mode]

    def body(a_ref, b_ref, o_ref, acc_ref):
        kk = pl.program_id(2)
        part = lax.dot_general(a_ref[...], b_ref[...], dims, preferred_element_type=F32)
        if nk == 1:
            o_ref[...] = part.astype(o_ref.dtype)
            return

        @pl.when(kk == 0)
        def _():
            acc_ref[...] = part

        @pl.when((kk > 0) & (kk < nk - 1))
        def _():
            acc_ref[...] += part

        @pl.when(kk == nk - 1)
        def _():
            o_ref[...] = (acc_ref[...] + part).astype(o_ref.dtype)

    a_spec = pl.BlockSpec((tk, tm), lambda i, j, kk: (kk, i)) if mode == 'tn' else pl.BlockSpec((tm, tk), lambda i, j, kk: (i, kk))
    b_spec = pl.BlockSpec((tn, tk), lambda i, j, kk: (j, kk)) if mode == 'nt' else pl.BlockSpec((tk, tn), lambda i, j, kk: (kk, j))
    return pl.pallas_call(
        body, name=name, grid=(m // tm, n // tn, nk),
        in_specs=[a_spec, b_spec],
        out_specs=pl.BlockSpec((tm, tn), lambda i, j, kk: (i, j)),
        out_shape=jax.ShapeDtypeStruct((m, n), out_dtype),
        scratch_shapes=[pltpu.VMEM((tm, tn), F32)],
        compiler_params=_cparams(("parallel", "parallel", "arbitrary")),
    )(a, b)


ROW_TILE = 512


def _row_specs(rows, tm):
    return [pl.BlockSpec((tm, w), lambda i, ci=ci: (i, ci)) for (_, w, ci) in rows]


def _rw_forward(fn, rows, pars, outs, name):
    t = rows[0][0].shape[0]
    tm = min(ROW_TILE, t)
    nr, npar = len(rows), len(pars)

    def body(*refs):
        vals = [r[...].astype(F32) for r in refs[:nr]] + [p[...] for p in refs[nr:nr + npar]]
        res = fn(*vals)
        for o, v in zip(refs[nr + npar:], res):
            o[...] = v.astype(o.dtype)

    return pl.pallas_call(
        body, name=name, grid=(t // tm,),
        in_specs=_row_specs(rows, tm) + [pl.BlockSpec(p.shape, lambda i: (0, 0)) for p in pars],
        out_specs=tuple(pl.BlockSpec((tm, w), lambda i: (i, 0)) for (w, _) in outs),
        out_shape=tuple(jax.ShapeDtypeStruct((t, w), dt) for (w, dt) in outs),
        compiler_params=_cparams(("parallel",)),
    )(*[r[0] for r in rows], *pars)


def _rw_backward(fn, rows, pars, cots, drow_dtypes, name):
    t = rows[0][0].shape[0]
    tm = min(ROW_TILE, t)
    nr, npar, nc = len(rows), len(pars), len(cots)
    keep = [i for i, dt in enumerate(drow_dtypes) if dt is not None]

    def body(*refs):
        vals = [r[...].astype(F32) for r in refs[:nr]] + [p[...] for p in refs[nr:nr + npar]]
        cvals = tuple(c[...].astype(F32) for c in refs[nr + npar:nr + npar + nc])
        orefs = refs[nr + npar + nc:]
        _, vjp = jax.vjp(fn, *vals)
        g = vjp(cvals)
        for o, i in zip(orefs[:len(keep)], keep):
            o[...] = g[i].astype(o.dtype)
        first = pl.program_id(0) == 0
        for o, gi in zip(orefs[len(keep):], g[nr:]):
            @pl.when(first)
            def _(o=o, gi=gi):
                o[...] = gi

            @pl.when(jnp.logical_not(first))
            def _(o=o, gi=gi):
                o[...] += gi

    out_specs = [pl.BlockSpec((tm, rows[i][1]), lambda i_: (i_, 0)) for i in keep] + \
                [pl.BlockSpec(p.shape, lambda i_: (0, 0)) for p in pars]
    out_shape = [jax.ShapeDtypeStruct((t, rows[i][1]), drow_dtypes[i]) for i in keep] + \
                [jax.ShapeDtypeStruct(p.shape, F32) for p in pars]
    return pl.pallas_call(
        body, name=name, grid=(t // tm,),
        in_specs=_row_specs(rows, tm) + [pl.BlockSpec(p.shape, lambda i: (0, 0)) for p in pars] + _row_specs(cots, tm),
        out_specs=tuple(out_specs), out_shape=tuple(out_shape),
        compiler_params=_cparams(("arbitrary",)),
    )(*[r[0] for r in rows], *pars, *[c[0] for c in cots])


def _norm_fn(x, g):
    return (_rms(x, g),)


def _norm_skip_fn(x, g):
    return _rms(x, g), x


def _merge_fn(ga, gb, ya, yb):
    return (_sigmoid(ga) * ya + _sigmoid(gb) * yb,)


def _res_norm_fn(x, mo, g):
    x1 = x + mo
    return x1, _rms(x1, g)


def _loss_head(x1, fo, gf, target, name):
    t, d = x1.shape
    tm = min(ROW_TILE, t)

    def tile_loss(x2, g, tgt):
        err = _rms(x2, g) - tgt
        per_row = jnp.sum(err * err, axis=-1, keepdims=True) * (0.5 / d)
        return jnp.sum(per_row, axis=0, keepdims=True)

    def body(x1_ref, fo_ref, g_ref, t_ref, loss_ref, dx_ref, dxb_ref, dg_ref):
        x2 = x1_ref[...] + fo_ref[...]
        val, vjp = jax.vjp(functools.partial(tile_loss, tgt=t_ref[...]), x2, g_ref[...])
        dx2, dg = vjp(jnp.ones((1, 1), F32))
        dx_ref[...] = dx2
        dxb_ref[...] = dx2.astype(BF16)
        first = pl.program_id(0) == 0

        @pl.when(first)
        def _():
            loss_ref[...] = jnp.broadcast_to(val, loss_ref.shape)
            dg_ref[...] = dg

        @pl.when(jnp.logical_not(first))
        def _():
            loss_ref[...] += jnp.broadcast_to(val, loss_ref.shape)
            dg_ref[...] += dg

    row = pl.BlockSpec((tm, d), lambda i: (i, 0))
    vec = pl.BlockSpec((1, d), lambda i: (0, 0))
    return pl.pallas_call(
        body, name=name, grid=(t // tm,),
        in_specs=[row, row, vec, row],
        out_specs=(pl.BlockSpec((1, LANES), lambda i: (0, 0)), row, row, vec),
        out_shape=(jax.ShapeDtypeStruct((1, LANES), F32), jax.ShapeDtypeStruct((t, d), F32),
                   jax.ShapeDtypeStruct((t, d), BF16), jax.ShapeDtypeStruct((1, d), F32)),
        compiler_params=_cparams(("arbitrary",)),
    )(x1, fo, gf, target)


FFN_TILE_ROWS = 1024
FFN_TILE_COLS = 256
FFN_COL_BLOCKS = FFN_HIDDEN // FFN_TILE_COLS


def _conv3_past(x, halo, w):
    rows = lax.broadcasted_iota(jnp.int32, x.shape, 0)
    x1 = jnp.where(rows == 0, halo[7:8, :], pltpu.roll(x, 1, 0))
    x2 = jnp.where(rows == 0, halo[6:7, :], jnp.where(rows == 1, halo[7:8, :], pltpu.roll(x, 2, 0)))
    return x * w[2:3] + x1 * w[1:2] + x2 * w[0:1], x1, x2


def _ffn_in_specs(tm, imap, jmap):
    per = tm // SUBLANES
    tile = lambda off: pl.BlockSpec((tm, FFN_TILE_COLS), lambda *g: (imap(*g), off + jmap(*g) % FFN_COL_BLOCKS))
    halo = lambda off: pl.BlockSpec((SUBLANES, FFN_TILE_COLS),
                                    lambda *g: (jnp.maximum(imap(*g) * per - 1, 0), off + jmap(*g) % FFN_COL_BLOCKS))
    wsp = lambda off: pl.BlockSpec((FFN_CONV, FFN_TILE_COLS), lambda *g: (0, off + jmap(*g) % FFN_COL_BLOCKS))
    return [tile(0), halo(0), wsp(0), tile(FFN_COL_BLOCKS), halo(FFN_COL_BLOCKS), wsp(FFN_COL_BLOCKS)]


def _ffn_act_forward(hpre, cw):
    t = hpre.shape[0]
    tm = min(FFN_TILE_ROWS, t)

    def body(hg, pg, wg, hu, pu, wu, out):
        live = (pl.program_id(0) > 0).astype(F32)
        cg, _, _ = _conv3_past(hg[...], pg[...] * live, wg[...])
        cu, _, _ = _conv3_past(hu[...], pu[...] * live, wu[...])
        out[...] = (cg * _sigmoid_plain(cg) * cu).astype(out.dtype)

    return pl.pallas_call(
        body, name="ffn_act_fwd", grid=(t // tm, FFN_COL_BLOCKS),
        in_specs=_ffn_in_specs(tm, lambda i, j: i, lambda i, j: j),
        out_specs=pl.BlockSpec((tm, FFN_TILE_COLS), lambda i, j: (i, j)),
        out_shape=jax.ShapeDtypeStruct((t, FFN_HIDDEN), BF16),
        compiler_params=_cparams(("parallel", "parallel")),
    )(hpre, hpre, cw, hpre, hpre, cw)


def _conv3_future(d, nxt, w):
    tm = d.shape[0]
    rows = lax.broadcasted_iota(jnp.int32, d.shape, 0)
    d1 = jnp.where(rows == tm - 1, nxt[0:1, :], pltpu.roll(d, tm - 1, 0))
    d2 = jnp.where(rows == tm - 1, nxt[1:2, :], jnp.where(rows == tm - 2, nxt[0:1, :], pltpu.roll(d, tm - 2, 0)))
    return d * w[2:3] + d1 * w[1:2] + d2 * w[0:1]


def _ffn_backward(hpre, cw, dact):
    t = hpre.shape[0]
    tm = min(FFN_TILE_ROWS, t)
    n_tiles = t // tm
    per = tm // SUBLANES

    def d_conv_out(cg, cu, d):
        s = _sigmoid_plain(cg)
        return d * cu * s * (1.0 + cg * (1.0 - s)), d * cg * s

    def body(hg, pg, ng, wg, hu, pu, nu, wu, da, dan, dhg, dhu, dwg, dwu):
        i = pl.program_id(1)
        live_prev = (i > 0).astype(F32)
        live_next = (i < n_tiles - 1).astype(F32)
        xg, xu = hg[...], hu[...]
        cg, g1, g2 = _conv3_past(xg, pg[...] * live_prev, wg[...])
        cu, u1, u2 = _conv3_past(xu, pu[...] * live_prev, wu[...])
        dg, du = d_conv_out(cg, cu, da[...])
        cgn, _, _ = _conv3_past(ng[...], hg[tm - SUBLANES:tm, :], wg[...])
        cun, _, _ = _conv3_past(nu[...], hu[tm - SUBLANES:tm, :], wu[...])
        dgn, dun = d_conv_out(cgn, cun, dan[...] * live_next)
        dhg[...] = _conv3_future(dg, dgn, wg[...]).astype(dhg.dtype)
        dhu[...] = _conv3_future(du, dun, wu[...]).astype(dhu.dtype)
        sums_g = [jnp.sum(xs * dg, axis=0, keepdims=True) for xs in (g2, g1, xg)]
        sums_u = [jnp.sum(xs * du, axis=0, keepdims=True) for xs in (u2, u1, xu)]

        @pl.when(i == 0)
        def _():
            for r_ in range(FFN_CONV):
                dwg[r_:r_ + 1, :] = sums_g[r_]
                dwu[r_:r_ + 1, :] = sums_u[r_]

        @pl.when(i > 0)
        def _():
            for r_ in range(FFN_CONV):
                dwg[r_:r_ + 1, :] += sums_g[r_]
                dwu[r_:r_ + 1, :] += sums_u[r_]

    nb = FFN_COL_BLOCKS
    nxt = lambda i: jnp.minimum((i + 1) * per, t // SUBLANES - 1)
    prv = lambda i: jnp.maximum(i * per - 1, 0)
    half = lambda off: [pl.BlockSpec((tm, FFN_TILE_COLS), lambda j, i: (i, off + j)),
                        pl.BlockSpec((SUBLANES, FFN_TILE_COLS), lambda j, i: (prv(i), off + j)),
                        pl.BlockSpec((SUBLANES, FFN_TILE_COLS), lambda j, i: (nxt(i), off + j)),
                        pl.BlockSpec((FFN_CONV, FFN_TILE_COLS), lambda j, i: (0, off + j))]
    tile = pl.BlockSpec((tm, FFN_TILE_COLS), lambda j, i: (i, j))
    taps = pl.BlockSpec((FFN_CONV, FFN_TILE_COLS), lambda j, i: (0, j))
    return pl.pallas_call(
        body, name="ffn_bwd", grid=(nb, n_tiles),
        in_specs=half(0) + half(nb) + [tile, pl.BlockSpec((SUBLANES, FFN_TILE_COLS), lambda j, i: (nxt(i), j))],
        out_specs=(tile, tile, taps, taps),
        out_shape=(jax.ShapeDtypeStruct((t, FFN_HIDDEN), BF16), jax.ShapeDtypeStruct((t, FFN_HIDDEN), BF16),
                   jax.ShapeDtypeStruct((FFN_CONV, FFN_HIDDEN), F32), jax.ShapeDtypeStruct((FFN_CONV, FFN_HIDDEN), F32)),
        compiler_params=_cparams(("parallel", "arbitrary")),
    )(hpre, hpre, hpre, cw, hpre, hpre, hpre, cw, dact, dact)


def _my_place():
    x, y, c = lax.axis_index("x"), lax.axis_index("y"), lax.axis_index("c")
    return x, y, c, 4 * x + 2 * y + c


N_CHIPS = 4


def _remote(src, dst, send_sem, recv_sem, dev):
    return pltpu.make_async_remote_copy(src_ref=src, dst_ref=dst, send_sem=send_sem, recv_sem=recv_sem, device_id=dev,
                                        device_id_type=pl.DeviceIdType.MESH)


def _chip_peer(x, y, k):
    return x ^ ((k >> 1) & 1), y ^ (k & 1)


def _all_gather_two_level(shard, name):
    r, w = shard.shape

    def body(src, out, send_sems, recv_sems, local_sem):
        x, y, c, me = _my_place()
        sibling = (x, y, 1 - c)
        mine = pltpu.make_async_copy(src, out.at[me], local_sem)
        mine.start()
        first = [_remote(src, out.at[me], send_sems.at[0], recv_sems.at[0], sibling)]
        for k in range(1, N_CHIPS):
            px, py = _chip_peer(x, y, k)
            first.append(_remote(src, out.at[me], send_sems.at[k], recv_sems.at[k], (px, py, c)))
        for cp in first:
            cp.start()
        passed = []
        for k in range(1, N_CHIPS):
            px, py = _chip_peer(x, y, k)
            landed = out.at[me ^ (2 * k)]
            _remote(src, landed, send_sems.at[k], recv_sems.at[k], (px, py, c)).wait_recv()
            fwd = _remote(landed, landed, send_sems.at[N_CHIPS - 1 + k], recv_sems.at[N_CHIPS - 1 + k], sibling)
            fwd.start()
            passed.append(fwd)
        _remote(src, out.at[me ^ 1], send_sems.at[0], recv_sems.at[0], sibling).wait_recv()
        for k in range(1, N_CHIPS):
            got = out.at[(me ^ 1) ^ (2 * k)]
            _remote(got, got, send_sems.at[N_CHIPS - 1 + k], recv_sems.at[N_CHIPS - 1 + k], sibling).wait_recv()
        for cp in first + passed:
            cp.wait_send()
        mine.wait()

    return pl.pallas_call(
        body, name=name,
        in_specs=[pl.BlockSpec(memory_space=pl.ANY)],
        out_specs=pl.BlockSpec(memory_space=pl.ANY),
        out_shape=jax.ShapeDtypeStruct((N_DEV, r, w), shard.dtype),
        scratch_shapes=[pltpu.SemaphoreType.DMA((N_DEV - 1,)), pltpu.SemaphoreType.DMA((N_DEV - 1,)), pltpu.SemaphoreType.DMA],
    )(shard)


def _device_peer(x, y, c, k):
    px, py, pc = x ^ ((k >> 2) & 1), y ^ ((k >> 1) & 1), c ^ (k & 1)
    return (px, py, pc), 4 * px + 2 * py + pc


_HBM = pl.BlockSpec(memory_space=pltpu.HBM)
_SEM = pl.BlockSpec(memory_space=pltpu.SEMAPHORE)


def _gather_start(shard, name):
    def body(src, land, send_sems, recv_sems, src_thru, land_thru, token):
        x, y, c, me = _my_place()
        for k in range(1, N_DEV):
            dev, _ = _device_peer(x, y, c, k)
            _remote(src, land.at[me], send_sems.at[k], recv_sems.at[k], dev).start()
        token[...] = jnp.zeros_like(token)

    landing = lax.empty((N_DEV,) + shard.shape, shard.dtype)
    return pl.pallas_call(
        body, name=name,
        out_shape=(pltpu.SemaphoreType.DMA((N_DEV,)), pltpu.SemaphoreType.DMA((N_DEV,)), pltpu.HBM(shard.shape, shard.dtype),
                   pltpu.HBM(landing.shape, landing.dtype), jax.ShapeDtypeStruct((SUBLANES, LANES), F32)),
        in_specs=(_HBM, _HBM), out_specs=(_SEM, _SEM, _HBM, _HBM, pl.BlockSpec(memory_space=pltpu.VMEM)),
        input_output_aliases={0: 2, 1: 3},
        compiler_params=pltpu.CompilerParams(has_side_effects=pltpu.SideEffectType.DATAFLOW_SIDE_EFFECTING),
    )(pltpu.with_memory_space_constraint(shard, pltpu.HBM), pltpu.with_memory_space_constraint(landing, pltpu.HBM))


def _gather_wait(send_sems, recv_sems, shard, landing, after, name):
    n_after = len(after)

    def body(*refs):
        src, land, send_sems, recv_sems = refs[:4]
        x, y, c, _ = _my_place()
        for k in range(1, N_DEV):
            dev, idx = _device_peer(x, y, c, k)
            cp = _remote(src, land.at[idx], send_sems.at[k], recv_sems.at[k], dev)
            cp.wait_send()
            cp.wait_recv()

    return pl.pallas_call(
        body, name=name,
        out_shape=(pltpu.HBM(shard.shape, shard.dtype), pltpu.HBM(landing.shape, landing.dtype)),
        in_specs=(_HBM, _HBM, _SEM, _SEM) + (pl.BlockSpec(memory_space=pl.ANY),) * n_after, out_specs=(_HBM, _HBM),
        input_output_aliases={0: 0, 1: 1},
        compiler_params=pltpu.CompilerParams(has_side_effects=pltpu.SideEffectType.DATAFLOW_SIDE_EFFECTING),
    )(shard, landing, send_sems, recv_sems, *after)[1]


def _slab_push_start(slabs, name):
    na = len(slabs)

    def body(*refs):
        srcs, lands = refs[:na], refs[na:2 * na]
        send_sems, recv_sems = refs[2 * na], refs[2 * na + 1]
        token = refs[-1]
        x, y, c, me = _my_place()
        for i in range(na):
            for k in range(1, N_DEV):
                dev, idx = _device_peer(x, y, c, k)
                s = i * N_DEV + k
                _remote(srcs[i].at[idx], lands[i].at[me], send_sems.at[s], recv_sems.at[s], dev).start()
        token[...] = jnp.zeros_like(token)

    hbm_shapes = [pltpu.HBM(a.shape, a.dtype) for a in slabs]
    ins = [pltpu.with_memory_space_constraint(a, pltpu.HBM) for a in slabs]
    ins += [pltpu.with_memory_space_constraint(lax.empty(a.shape, a.dtype), pltpu.HBM) for a in slabs]
    out = pl.pallas_call(
        body, name=name,
        out_shape=(pltpu.SemaphoreType.DMA((na * N_DEV,)), pltpu.SemaphoreType.DMA((na * N_DEV,)), *hbm_shapes, *hbm_shapes,
                   jax.ShapeDtypeStruct((SUBLANES, LANES), F32)),
        in_specs=(_HBM,) * (2 * na), out_specs=(_SEM, _SEM) + (_HBM,) * (2 * na) + (pl.BlockSpec(memory_space=pltpu.VMEM),),
        input_output_aliases={i: 2 + i for i in range(2 * na)},
        compiler_params=pltpu.CompilerParams(has_side_effects=pltpu.SideEffectType.DATAFLOW_SIDE_EFFECTING),
    )(*ins)
    return out[0], out[1], list(out[2:2 + na]), list(out[2 + na:2 + 2 * na]), out[-1]


def _slab_push_wait(send_sems, recv_sems, slabs, landings, after, name):
    na = len(slabs)

    def body(*refs):
        srcs, lands = refs[:na], refs[na:2 * na]
        send_sems, recv_sems = refs[2 * na], refs[2 * na + 1]
        x, y, c, me = _my_place()
        for i in range(na):
            for k in range(1, N_DEV):
                dev, idx = _device_peer(x, y, c, k)
                s = i * N_DEV + k
                cp = _remote(srcs[i].at[idx], lands[i].at[idx], send_sems.at[s], recv_sems.at[s], dev)
                cp.wait_send()
                cp.wait_recv()

    hbm_shapes = tuple(pltpu.HBM(a.shape, a.dtype) for a in slabs)
    out = pl.pallas_call(
        body, name=name, out_shape=hbm_shapes + hbm_shapes,
        in_specs=(_HBM,) * (2 * na) + (_SEM, _SEM) + (pl.BlockSpec(memory_space=pl.ANY),) * len(after),
        out_specs=(_HBM,) * (2 * na), input_output_aliases={i: i for i in range(2 * na)},
        compiler_params=pltpu.CompilerParams(has_side_effects=pltpu.SideEffectType.DATAFLOW_SIDE_EFFECTING),
    )(*slabs, *landings, send_sems, recv_sems, *after)
    return list(out[na:])


def _pair_exchange(arrays, name):
    na = len(arrays)

    def body(*refs):
        srcs, dsts, (send_sems, recv_sems) = refs[:na], refs[na:2 * na], refs[2 * na:]
        x, y, c, _ = _my_place()
        sibling = (x, y, 1 - c)
        copies = []
        for i in range(na):
            for q in range(N_CHIPS):
                s = i * N_CHIPS + q
                copies.append(_remote(srcs[i].at[2 * q + 1 - c], dsts[i].at[q], send_sems.at[s], recv_sems.at[s], sibling))
        for cp in copies:
            cp.start()
        for cp in copies:
            cp.wait_recv()
        for cp in copies:
            cp.wait_send()

    hbm = pl.BlockSpec(memory_space=pl.ANY)
    return pl.pallas_call(
        body, name=name, in_specs=[hbm] * na, out_specs=tuple([hbm] * na),
        out_shape=tuple(jax.ShapeDtypeStruct((N_CHIPS,) + a.shape[1:], a.dtype) for a in arrays),
        scratch_shapes=[pltpu.SemaphoreType.DMA((na * N_CHIPS,)), pltpu.SemaphoreType.DMA((na * N_CHIPS,))],
    )(*arrays)


ELEMENTWISE_COLS = 256


def _pair_sum(slabs, recv, core, out_dtype, name):
    _, r, w = slabs.shape
    tc = ELEMENTWISE_COLS

    def body(core_ref, mine, theirs, out):
        out[...] = (mine[...] + theirs[...]).astype(out.dtype)

    grid_spec = pltpu.PrefetchScalarGridSpec(
        num_scalar_prefetch=1, grid=(N_CHIPS, w // tc),
        in_specs=[pl.BlockSpec((None, r, tc), lambda q, j, core_ref: (2 * q + core_ref[0], 0, j)),
                  pl.BlockSpec((None, r, tc), lambda q, j, core_ref: (q, 0, j))],
        out_specs=pl.BlockSpec((None, r, tc), lambda q, j, core_ref: (q, 0, j)))
    return pl.pallas_call(body, name=name, grid_spec=grid_spec,
                          out_shape=jax.ShapeDtypeStruct((N_CHIPS, r, w), out_dtype),
                          compiler_params=_cparams(("parallel", "parallel")))(core, slabs, recv)


def _chip_exchange(arrays, name):
    na = len(arrays)

    def body(*refs):
        srcs, dsts, (send_sems, recv_sems, local_sems) = refs[:na], refs[na:2 * na], refs[2 * na:]
        x, y, c, _ = _my_place()
        chip = 2 * x + y
        own = [pltpu.make_async_copy(srcs[i].at[chip], dsts[i].at[chip], local_sems.at[i]) for i in range(na)]
        for cp in own:
            cp.start()
        sends, arrivals = [], []
        for i in range(na):
            for k in range(1, N_CHIPS):
                px, py = _chip_peer(x, y, k)
                s = i * N_CHIPS + k
                sends.append(_remote(srcs[i].at[chip ^ k], dsts[i].at[chip], send_sems.at[s], recv_sems.at[s], (px, py, c)))
                arrivals.append(_remote(srcs[i].at[chip], dsts[i].at[chip ^ k], send_sems.at[s], recv_sems.at[s], (px, py, c)))
        for cp in sends:
            cp.start()
        for cp in arrivals:
            cp.wait_recv()
        for cp in sends:
            cp.wait_send()
        for cp in own:
            cp.wait()

    hbm = pl.BlockSpec(memory_space=pl.ANY)
    return pl.pallas_call(
        body, name=name, in_specs=[hbm] * na, out_specs=tuple([hbm] * na),
        out_shape=tuple(jax.ShapeDtypeStruct(a.shape, a.dtype) for a in arrays),
        scratch_shapes=[pltpu.SemaphoreType.DMA((na * N_CHIPS,)), pltpu.SemaphoreType.DMA((na * N_CHIPS,)),
                        pltpu.SemaphoreType.DMA((na,))],
    )(*arrays)


def _adamw_update(g, w, m, v):
    c1 = 1.0 / (1.0 - ADAM_B1 ** ADAM_STEP)
    c2 = 1.0 / (1.0 - ADAM_B2 ** ADAM_STEP)
    mn = ADAM_B1 * m + (1.0 - ADAM_B1) * g
    vn = ADAM_B2 * v + (1.0 - ADAM_B2) * (g * g)
    return -ADAM_LR * ((mn * c1) / (jnp.sqrt(vn * c2) + ADAM_EPS) + ADAM_WD * w), mn, vn


def _reduce_adamw(parts, w, m, v, name):
    n_parts, r, wd = parts.shape
    tc = ELEMENTWISE_COLS

    def body(p_ref, w_ref, m_ref, v_ref, g_out, d_out, m_out, v_out):
        g = p_ref[0].astype(F32)
        for s in range(1, n_parts):
            g = g + p_ref[s].astype(F32)
        g_out[...] = g
        d_out[...], m_out[...], v_out[...] = _adamw_update(g, w_ref[...], m_ref[...], v_ref[...])

    blk = pl.BlockSpec((r, tc), lambda j: (0, j))
    shp = jax.ShapeDtypeStruct((r, wd), F32)
    return pl.pallas_call(
        body, name=name, grid=(wd // tc,),
        in_specs=[pl.BlockSpec((n_parts, r, tc), lambda j: (0, 0, j)), blk, blk, blk],
        out_specs=(blk, blk, blk, blk), out_shape=(shp, shp, shp, shp),
        compiler_params=_cparams(("parallel",)),
    )(parts, w, m, v)


def _reduce_landed_adamw(landing, own, me, w, m, v, name):
    n_parts, r, wd = landing.shape
    tc = ELEMENTWISE_COLS

    def body(me_ref, land_ref, own_ref, w_ref, m_ref, v_ref, g_out, d_out, m_out, v_out):
        mine = own_ref[...].astype(F32)
        g = None
        for s in range(n_parts):
            part = jnp.where(me_ref[0] == s, mine, land_ref[s].astype(F32))
            g = part if g is None else g + part
        g_out[...] = g
        d_out[...], m_out[...], v_out[...] = _adamw_update(g, w_ref[...], m_ref[...], v_ref[...])

    blk = pl.BlockSpec((r, tc), lambda j, me_ref: (0, j))
    shp = jax.ShapeDtypeStruct((r, wd), F32)
    grid_spec = pltpu.PrefetchScalarGridSpec(
        num_scalar_prefetch=1, grid=(wd // tc,),
        in_specs=[pl.BlockSpec((n_parts, r, tc), lambda j, me_ref: (0, 0, j)),
                  pl.BlockSpec((None, r, tc), lambda j, me_ref: (me_ref[0], 0, j)), blk, blk, blk],
        out_specs=(blk, blk, blk, blk))
    return pl.pallas_call(body, name=name, grid_spec=grid_spec, out_shape=(shp, shp, shp, shp),
                          compiler_params=_cparams(("parallel",)))(me, landing, own, w, m, v)


PACK_W = 1024


def _pad_heads(a, slots):
    lead = a.shape[:-1]
    a = a.reshape(lead + (slots, RWKV_HEAD_DIM))
    a = jnp.pad(a, [(0, 0)] * (len(lead) + 1) + [(0, LANES - RWKV_HEAD_DIM)])
    return a.reshape(lead + (slots * LANES,))


def _flat_pack(arrs, dtype, row_mult):
    flat = jnp.concatenate([a.reshape(-1).astype(dtype) for a in arrs])
    n = flat.shape[0]
    rows = -(-n // PACK_W)
    rows = -(-rows // row_mult) * row_mult
    return jnp.pad(flat, (0, rows * PACK_W - n)).reshape(rows, PACK_W)


def _row_pack(arrs, dtype, row_mult):
    parts = [a.astype(dtype) if a.shape[1] == PACK_W else a.astype(dtype).reshape(-1, PACK_W) for a in arrs]
    rows = sum(p.shape[0] for p in parts)
    pad = -(-rows // row_mult) * row_mult - rows
    return jnp.concatenate(parts + ([jnp.zeros((pad, PACK_W), dtype)] if pad else []), axis=0)


def _unpack_row_gathered(g, names, shard_shapes):
    out, r0 = {}, 0
    for n in names:
        s = shard_shapes[n]
        rows = s[0] * s[1] // PACK_W
        seg = g[:, r0:r0 + rows, :]
        r0 += rows
        if s[1] == PACK_W:
            assert SHARD_AXIS[n] == 0
            out[n] = seg.reshape(N_DEV * s[0], s[1])
        else:
            assert SHARD_AXIS[n] == 1
            out[n] = jnp.transpose(seg.reshape((N_DEV,) + tuple(s)), (1, 0, 2)).reshape(s[0], N_DEV * s[1])
    return out


def _unpack_gathered(g, names, shard_shapes):
    flat = g.reshape(N_DEV, -1)
    out, off = {}, 0
    for n in names:
        s = shard_shapes[n]
        size = s[0] * s[1]
        seg = flat[:, off:off + size].reshape((N_DEV,) + tuple(s))
        off += size
        if SHARD_AXIS[n] == 1:
            out[n] = jnp.transpose(seg, (1, 0, 2)).reshape(s[0], N_DEV * s[1])
        else:
            out[n] = seg.reshape(N_DEV * s[0], s[1])
    return out


def _shard_major(full, axis):
    a, b = full.shape
    if axis == 1:
        return jnp.transpose(full.reshape(a, N_DEV, b // N_DEV), (1, 0, 2)).reshape(N_DEV, -1)
    return full.reshape(N_DEV, -1)


def _prepare_weights(full, rep):
    w = full['w_in']
    d = w.shape[1]
    rkv = jnp.pad(w[0:1536].reshape(3 * RWKV_HEADS, RWKV_HEAD_DIM, d), ((0, 0), (0, LANES - RWKV_HEAD_DIM), (0, 0)))
    w_cat = jnp.concatenate([
        w[3848:4872], w[4872:5896], rkv.reshape(3 * RWKV_HEADS * LANES, d), w[1792:3328], w[3328:3840],
        w[1536:1792], jnp.pad(w[3840:3848], ((0, LANES - 8), (0, 0))), jnp.zeros((LANES, d), w.dtype)], axis=0)
    assert w_cat.shape[0] == CAT_W
    mu = rep['rwkv_mu']
    vecs = [mu[0:512], mu[512:1024], mu[1024:1536], rep['rwkv_w0'], rep['rwkv_a0'], rep['rwkv_k_k'], rep['rwkv_k_a'],
            rep['rwkv_ln_w'], rep['rwkv_ln_b'], rep['rwkv_r_k'].reshape(-1)]
    ppack = jnp.stack([jnp.pad(v.reshape(RWKV_HEADS, RWKV_HEAD_DIM), ((0, 0), (0, LANES - RWKV_HEAD_DIM))) for v in vecs], axis=1)
    ppack = jnp.pad(ppack, ((0, 0), (0, 16 - len(vecs)), (0, 0)))
    mulo = mu[1536:1792].reshape(1, 2 * LANES)
    wl = jnp.zeros((3, 2 * LANES, RWKV_HEADS * LANES), F32)
    wl = wl.at[0, 0:64].set(_pad_heads(full['rwkv_w2'], RWKV_HEADS))
    wl = wl.at[1, 64:128].set(_pad_heads(full['rwkv_a2'], RWKV_HEADS))
    wl = wl.at[2, 128:256].set(_pad_heads(full['rwkv_g2'], RWKV_HEADS))
    wl = jnp.transpose(wl.reshape(3, 2 * LANES, RWKV_HEADS, LANES), (2, 0, 1, 3))
    cw = full['gdn_conv_w'].reshape(GDN_CONV, 3, GDN_HEADS, LANES)
    cwpack = jnp.pad(jnp.transpose(cw, (2, 1, 0, 3)), ((0, 0), (0, 0), (0, SUBLANES - GDN_CONV), (0, 0)))
    gpar = jnp.zeros((SUBLANES, LANES), F32)
    gpar = gpar.at[0, 0:GDN_HEADS].set(rep['gdn_a_log']).at[1, 0:GDN_HEADS].set(rep['gdn_dt_bias']).at[2].set(rep['gdn_norm_w'])
    return dict(w_cat=w_cat, ffn_cw=full['ffn_conv_w'], ppack=ppack, mulo=mulo, wl=wl, cwpack=cwpack, gpar=gpar,
                g1=rep['norm1_g'].reshape(1, -1), g2=rep['norm2_g'].reshape(1, -1), gf=rep['final_g'].reshape(1, -1))


def _prepare_late_weights(full):
    rp = full['rwkv_proj']
    rproj = jnp.pad(rp.reshape(RWKV_HEADS, RWKV_HEAD_DIM, -1), ((0, 0), (0, LANES - RWKV_HEAD_DIM), (0, 0))).reshape(RWKV_HEADS * LANES, -1)
    return dict(rproj=rproj, gproj=full['gdn_proj'], w_out=full['w_out'], ffn_up=full['ffn_up'], ffn_down=full['ffn_down'])


def _local_step(x, target, p, late_weights, push_grads):
    d = x.shape[1]
    full_w = lambda a: (a, a.shape[1], 0)
    (u,) = _rw_forward(_norm_fn, [full_w(x)], [p['g1']], [(d, BF16)], "norm1")
    p_cat = _matmul(u, p['w_cat'], 'nt', F32, "proj_in")
    ya_pre, st_r = _rwkv_forward(p_cat, p['ppack'], p['mulo'], p['wl'])
    yb_pre, st_g = _gdn_forward(p_cat, p['cwpack'], p['gpar'])
    p = {**p, **late_weights((ya_pre, yb_pre))}
    ya = _matmul(ya_pre, p['rproj'], 'nn', F32, "rwkv_proj")
    yb = _matmul(yb_pre, p['gproj'], 'nn', F32, "gdn_proj")
    gates = [(p_cat, d, OFF_GA // d), (p_cat, d, OFF_GB // d)]
    (mixed,) = _rw_forward(_merge_fn, gates + [full_w(ya), full_w(yb)], [], [(d, BF16)], "merge")
    mo = _matmul(mixed, p['w_out'], 'nn', F32, "out_proj")
    x1, n2 = _rw_forward(_res_norm_fn, [full_w(x), full_w(mo)], [p['g2']], [(d, F32), (d, BF16)], "res_norm2")
    hpre = _matmul(n2, p['ffn_up'], 'nt', F32, "ffn_up")
    act = _ffn_act_forward(hpre, p['ffn_cw'])
    fo = _matmul(act, p['ffn_down'], 'nn', F32, "ffn_down")
    loss_vec, dx2, dx2b, dgf = _loss_head(x1, fo, p['gf'], target, "loss_head")

    dact = _matmul(dx2b, p['ffn_down'], 'nt', F32, "d_act")
    dw_down = _matmul(act, dx2b, 'tn', BF16, "dw_ffn_down")
    dh_gate, dh_up, dcw_gate, dcw_up = _ffn_backward(hpre, p['ffn_cw'], dact)
    dh = jnp.concatenate([dh_gate, dh_up], axis=1)
    dcw_f = jnp.concatenate([dcw_gate, dcw_up], axis=1)
    dn2 = _matmul(dh, p['ffn_up'], 'nn', F32, "d_norm2")
    dw_up = _matmul(dh, n2, 'tn', BF16, "dw_ffn_up")
    token = push_grads({'ffn_down': dw_down, 'ffn_up': dw_up})
    dx1, dx1b, dg2 = _rw_backward(_res_norm_fn, [full_w(x), full_w(mo)], [p['g2'] + token], [full_w(dx2), full_w(dn2)],
                                  [F32, BF16], "res_norm2_bwd")
    dmixed = _matmul(dx1b, p['w_out'], 'nt', F32, "d_mixed")
    dw_out = _matmul(mixed, dx1b, 'tn', BF16, "dw_out")
    dga, dgb, dya, dyb = _rw_backward(_merge_fn, gates + [full_w(ya), full_w(yb)], [], [full_w(dmixed)],
                                      [BF16, BF16, BF16, BF16], "merge_bwd")
    d_ya_pre = _matmul(dya, p['rproj'], 'nt', F32, "d_rwkv_out")
    dw_rproj = _matmul(ya_pre, dya, 'tn', F32, "dw_rwkv_proj")
    d_yb_pre = _matmul(dyb, p['gproj'], 'nt', F32, "d_gdn_out")
    dw_gproj = _matmul(yb_pre, dyb, 'tn', F32, "dw_gdn_proj")
    dpr, dpk, dpv, dplo, dpp, dml, dwl = _rwkv_backward(p_cat, p['ppack'], p['mulo'], p['wl'], st_r, d_ya_pre)
    dq, dk, dv, dz, dab, dcw_g, dgp = _gdn_backward(p_cat, p['cwpack'], p['gpar'], st_g, d_yb_pre)
    t = x.shape[0]
    dp_cat = jnp.concatenate([dga, dgb, dpr, dpk, dpv, dq, dk, dv, dz, dplo.astype(BF16), dab.astype(BF16),
                              jnp.zeros((t, LANES), BF16)], axis=1)
    dw_cat = _matmul(dp_cat, u, 'tn', BF16, "dw_in")
    dw_in = jnp.concatenate([dw_cat[OFF_RKV:OFF_QKV].reshape(3 * RWKV_HEADS, LANES, d)[:, :RWKV_HEAD_DIM].reshape(-1, d),
                             dw_cat[OFF_LO:OFF_AB], dw_cat[OFF_QKV:OFF_Z], dw_cat[OFF_Z:OFF_LO], dw_cat[OFF_AB:OFF_AB + 8],
                             dw_cat[OFF_GA:OFF_GB], dw_cat[OFF_GB:OFF_RKV]], axis=0)
    token = push_grads({'w_out': dw_out, 'w_in': dw_in})
    du = _matmul(dp_cat, p['w_cat'], 'nn', F32, "d_norm1")
    grad_x, dg1 = _rw_backward(_norm_skip_fn, [full_w(x)], [p['g1'] + token], [full_w(du), full_w(dx1)], [F32], "norm1_bwd")

    heads = lambda row: dpp[:, row, :RWKV_HEAD_DIM].reshape(-1)
    lora = lambda j, lo_, hi_: jnp.transpose(dwl[:, j, lo_:hi_, :RWKV_HEAD_DIM], (1, 0, 2)).reshape(hi_ - lo_, RWKV_WIDTH)
    grads = {
        'norm1_g': dg1[0],
        'w_in': dw_in,
        'rwkv_mu': jnp.concatenate([heads(0), heads(1), heads(2), jnp.sum(dml[:, 0, :], axis=0)]),
        'rwkv_w0': heads(3), 'rwkv_a0': heads(4), 'rwkv_k_k': heads(5), 'rwkv_k_a': heads(6),
        'rwkv_ln_w': heads(7), 'rwkv_ln_b': heads(8), 'rwkv_r_k': heads(9).reshape(RWKV_HEADS, RWKV_HEAD_DIM),
        'rwkv_w2': lora(0, 0, 64), 'rwkv_a2': lora(1, 64, 128), 'rwkv_g2': lora(2, 128, 256),
        'rwkv_proj': dw_rproj.reshape(RWKV_HEADS, LANES, -1)[:, :RWKV_HEAD_DIM].reshape(RWKV_WIDTH, -1),
        'gdn_conv_w': jnp.transpose(dcw_g[:, :, :GDN_CONV, :], (2, 1, 0, 3)).reshape(GDN_CONV, 3 * GDN_WIDTH),
        'gdn_a_log': jnp.sum(dgp[:, 0, :GDN_HEADS], axis=0), 'gdn_dt_bias': jnp.sum(dgp[:, 1, :GDN_HEADS], axis=0),
        'gdn_norm_w': jnp.sum(dgp[:, 2, :], axis=0),
        'gdn_proj': dw_gproj, 'w_out': dw_out, 'norm2_g': dg2[0], 'ffn_up': dw_up, 'ffn_conv_w': dcw_f,
        'ffn_down': dw_down, 'final_g': dgf[0],
    }
    return loss_vec, grad_x, grads


def kernel(x, norm1_g, w_in, rwkv_mu, rwkv_w0, rwkv_w2, rwkv_a0, rwkv_a2, rwkv_g2, rwkv_k_k, rwkv_k_a, rwkv_r_k, rwkv_ln_w, rwkv_ln_b, rwkv_proj, gdn_conv_w, gdn_a_log, gdn_dt_bias, gdn_norm_w, gdn_proj, w_out, norm2_g, ffn_up, ffn_conv_w, ffn_down, final_g, loss_target, m_norm1_g, m_w_in, m_rwkv_mu, m_rwkv_w0, m_rwkv_w2, m_rwkv_a0, m_rwkv_a2, m_rwkv_g2, m_rwkv_k_k, m_rwkv_k_a, m_rwkv_r_k, m_rwkv_ln_w, m_rwkv_ln_b, m_rwkv_proj, m_gdn_conv_w, m_gdn_a_log, m_gdn_dt_bias, m_gdn_norm_w, m_gdn_proj, m_w_out, m_norm2_g, m_ffn_up, m_ffn_conv_w, m_ffn_down, m_final_g, v_norm1_g, v_w_in, v_rwkv_mu, v_rwkv_w0, v_rwkv_w2, v_rwkv_a0, v_rwkv_a2, v_rwkv_g2, v_rwkv_k_k, v_rwkv_k_a, v_rwkv_r_k, v_rwkv_ln_w, v_rwkv_ln_b, v_rwkv_proj, v_gdn_conv_w, v_gdn_a_log, v_gdn_dt_bias, v_gdn_norm_w, v_gdn_proj, v_w_out, v_norm2_g, v_ffn_up, v_ffn_conv_w, v_ffn_down, v_final_g):
    given = dict(zip(WEIGHT_NAMES, (norm1_g, w_in, rwkv_mu, rwkv_w0, rwkv_w2, rwkv_a0, rwkv_a2, rwkv_g2, rwkv_k_k, rwkv_k_a, rwkv_r_k,
                                    rwkv_ln_w, rwkv_ln_b, rwkv_proj, gdn_conv_w, gdn_a_log, gdn_dt_bias, gdn_norm_w, gdn_proj, w_out,
                                    norm2_g, ffn_up, ffn_conv_w, ffn_down, final_g)))
    mom1 = dict(zip(WEIGHT_NAMES, (m_norm1_g, m_w_in, m_rwkv_mu, m_rwkv_w0, m_rwkv_w2, m_rwkv_a0, m_rwkv_a2, m_rwkv_g2, m_rwkv_k_k,
                                   m_rwkv_k_a, m_rwkv_r_k, m_rwkv_ln_w, m_rwkv_ln_b, m_rwkv_proj, m_gdn_conv_w, m_gdn_a_log,
                                   m_gdn_dt_bias, m_gdn_norm_w, m_gdn_proj, m_w_out, m_norm2_g, m_ffn_up, m_ffn_conv_w, m_ffn_down,
                                   m_final_g)))
    mom2 = dict(zip(WEIGHT_NAMES, (v_norm1_g, v_w_in, v_rwkv_mu, v_rwkv_w0, v_rwkv_w2, v_rwkv_a0, v_rwkv_a2, v_rwkv_g2, v_rwkv_k_k,
                                   v_rwkv_k_a, v_rwkv_r_k, v_rwkv_ln_w, v_rwkv_ln_b, v_rwkv_proj, v_gdn_conv_w, v_gdn_a_log,
                                   v_gdn_dt_bias, v_gdn_norm_w, v_gdn_proj, v_w_out, v_norm2_g, v_ffn_up, v_ffn_conv_w, v_ffn_down,
                                   v_final_g)))
    def strip(n, a):
        a = a if n == 'final_g' else a.reshape(a.shape[1:])
        return a.T if n in TRANSPOSED else a

    local = {n: strip(n, a) for n, a in given.items()}
    shard_shapes = {n: local[n].shape for n in SHARD_AXIS}
    sharded = BIG_SHARDED + SMALL_SHARDED

    late_names = [n for n in BIG_SHARDED if n != 'w_in']
    g_in = _all_gather_two_level(_row_pack([local['w_in']], BF16, 16), "gather_w_in")
    g_small = _all_gather_two_level(_flat_pack([local[n] for n in SMALL_SHARDED], F32, SUBLANES), "gather_small")
    late_pack, g_in, g_small = lax.optimization_barrier((_row_pack([local[n] for n in late_names], BF16, 16), g_in, g_small))
    send_sems, recv_sems, late_pack, landing, token = _gather_start(late_pack, "gather_late_start")
    full = _unpack_row_gathered(g_in, ['w_in'], shard_shapes)
    full.update(_unpack_gathered(g_small, SMALL_SHARDED, shard_shapes))
    rep = {n: local[n] for n in REPLICATED}
    rep['norm1_g'] = rep['norm1_g'] + token[0, 0]

    def late_weights(after):
        got = _gather_wait(send_sems, recv_sems, late_pack, landing, after, "gather_late_wait")
        me = 4 * lax.axis_index("x") + 2 * lax.axis_index("y") + lax.axis_index("c")
        slot = lax.broadcasted_iota(jnp.int32, (N_DEV, 1, 1), 0)
        got = jnp.where(slot == me, late_pack[None], got)
        return _prepare_late_weights(_unpack_row_gathered(got, late_names, shard_shapes))

    pushes = []
    me = 4 * lax.axis_index("x") + 2 * lax.axis_index("y") + lax.axis_index("c")
    slot = lax.broadcasted_iota(jnp.int32, (N_DEV, 1, 1), 0)

    def push_grads(group):
        names = list(group)
        slabs = [group[n].reshape(N_DEV, -1, group[n].shape[1]) for n in names]
        send_sems, recv_sems, slabs, landings, token = _slab_push_start(slabs, "grad_push_start_" + "_".join(names))
        pushes.append((names, send_sems, recv_sems, slabs, landings))
        return token[0, 0]

    loss_vec, grad_x, grads = _local_step(x[0], loss_target[0], _prepare_weights(full, rep), late_weights, push_grads)

    landed = {}
    for names, send_sems, recv_sems, slabs, landings in pushes:
        got = _slab_push_wait(send_sems, recv_sems, slabs, landings, (grad_x,), "grad_push_wait_" + "_".join(names))
        for n, slab, land in zip(names, slabs, got):
            landed[n] = (land, slab)

    small_sharded = ['rwkv_proj', 'gdn_proj'] + SMALL_SHARDED
    small_names = small_sharded + REPLICATED
    rep_vec = jnp.concatenate([grads[n].reshape(-1) for n in REPLICATED] + [loss_vec[0, 0:1]])
    slab_small = jnp.concatenate([_shard_major(grads[n], SHARD_AXIS[n]) for n in small_sharded] +
                                 [jnp.broadcast_to(rep_vec[None], (N_DEV, rep_vec.shape[0]))], axis=1)
    small_rows = -(-slab_small.shape[1] // (PACK_W * SUBLANES)) * SUBLANES
    slab_small = jnp.pad(slab_small, ((0, 0), (0, small_rows * PACK_W - slab_small.shape[1]))).reshape(N_DEV, small_rows, PACK_W)
    core = lax.axis_index("c").astype(jnp.int32).reshape(1)
    (from_sibling,) = _pair_exchange([slab_small], "grad_pair_exchange")
    chip_small = _pair_sum(slab_small, from_sibling, core, F32, "grad_pair_sum_small")
    (parts_small,) = _chip_exchange([chip_small], "grad_chip_exchange")

    def pack_local(src):
        flat = jnp.concatenate([strip(n, src[n]).reshape(-1) for n in small_names])
        return jnp.pad(flat, (0, small_rows * PACK_W - flat.shape[0])).reshape(small_rows, PACK_W)

    results = [({}, None) for _ in range(4)]
    me_arr = me.astype(jnp.int32).reshape(1)
    for n in ROW_SHARDED:
        packs = _reduce_landed_adamw(*landed[n], me_arr, local[n], strip(n, mom1[n]), strip(n, mom2[n]), "adamw_" + n)
        for (out, _), pk in zip(results, packs):
            out[n] = (pk.T if n in TRANSPOSED else pk).reshape(given[n].shape)
    packs = _reduce_adamw(parts_small, pack_local(given), pack_local(mom1), pack_local(mom2), "adamw_small")
    for i, pk in enumerate(packs):
        flat, off = pk.reshape(-1), 0
        for n in small_names:
            size = int(np.prod(given[n].shape))
            results[i][0][n] = flat[off:off + size].reshape(given[n].shape)
            off += size
        results[i] = (results[i][0], flat[off])
    (g_out, loss), (d_out, _), (m_out, _), (v_out, _) = results
    return (loss, grad_x[None], *[g_out[n] for n in WEIGHT_NAMES], *[d_out[n] for n in WEIGHT_NAMES],
            *[m_out[n] for n in WEIGHT_NAMES], *[v_out[n] for n in WEIGHT_NAMES])
```

```python
import functools

import jax
import jax.numpy as jnp
import numpy as np
from jax import lax
from jax.experimental import pallas as pl
from jax.experimental.pallas import tpu as pltpu

F32 = jnp.float32
BF16 = jnp.bfloat16

N_DEV = 8
D_MODEL = 1024
CHUNK = 64
RWKV_HEADS = 8
RWKV_HEAD_DIM = 64
RWKV_WIDTH = 512
GDN_HEADS = 4
GDN_HEAD_DIM = 128
GDN_WIDTH = 512
GDN_CONV = 4
FFN_HIDDEN = 2816
FFN_CONV = 3
NORM_EPS = 1e-6
L2_EPS = 1e-6
RWKV_GN_EPS = 64e-5
LANES = 128
SUBLANES = 8
VMEM_LIMIT = 56 * 1024 * 1024

ADAM_LR = 0.001
ADAM_B1 = 0.9
ADAM_B2 = 0.999
ADAM_EPS = 1e-08
ADAM_WD = 0.01
ADAM_STEP = 10

OFF_GA, OFF_GB, OFF_RKV, OFF_QKV, OFF_Z, OFF_LO, OFF_AB, CAT_W = 0, 1024, 2048, 5120, 6656, 7168, 7424, 7680
RWKV_HB = 8
RWKV_STEP_CHUNKS = 2
RWKV_TILE = RWKV_STEP_CHUNKS * CHUNK
GDN_HB = 4
GDN_STEP_CHUNKS = 4
GDN_TILE = GDN_STEP_CHUNKS * CHUNK

WEIGHT_NAMES = ['norm1_g', 'w_in', 'rwkv_mu', 'rwkv_w0', 'rwkv_w2', 'rwkv_a0', 'rwkv_a2', 'rwkv_g2', 'rwkv_k_k', 'rwkv_k_a',
                'rwkv_r_k', 'rwkv_ln_w', 'rwkv_ln_b', 'rwkv_proj', 'gdn_conv_w', 'gdn_a_log', 'gdn_dt_bias', 'gdn_norm_w',
                'gdn_proj', 'w_out', 'norm2_g', 'ffn_up', 'ffn_conv_w', 'ffn_down', 'final_g']
BIG_SHARDED = ['w_in', 'ffn_up', 'ffn_down', 'w_out', 'rwkv_proj', 'gdn_proj']
SMALL_SHARDED = ['rwkv_w2', 'rwkv_a2', 'rwkv_g2', 'gdn_conv_w', 'ffn_conv_w']
TRANSPOSED = ('w_in', 'ffn_up')
SHARD_AXIS = {'w_in': 0, 'ffn_up': 0, 'ffn_down': 0, 'w_out': 0, 'rwkv_proj': 1, 'gdn_proj': 1,
              'rwkv_w2': 1, 'rwkv_a2': 1, 'rwkv_g2': 1, 'gdn_conv_w': 1, 'ffn_conv_w': 1}
REPLICATED = [n for n in WEIGHT_NAMES if n not in SHARD_AXIS]
ROW_SHARDED = ['w_in', 'ffn_up', 'ffn_down', 'w_out']


def _cparams(sem=None):
    kw = dict(vmem_limit_bytes=VMEM_LIMIT)
    if sem is not None:
        kw['dimension_semantics'] = sem
    return pltpu.CompilerParams(**kw)


_NN, _NT, _TN = 'nn', 'nt', 'tn'
_DIMS_2D = {'nn': (((1,), (0,)), ((), ())), 'nt': (((1,), (1,)), ((), ())), 'tn': (((0,), (0,)), ((), ()))}
_DIMS_3D = {'nn': (((2,), (1,)), ((0,), (0,))), 'nt': (((2,), (2,)), ((0,), (0,))), 'tn': (((1,), (1,)), ((0,), (0,)))}


def _dg(a, b, kind):
    return lax.dot_general(a, b, (_DIMS_2D if a.ndim == 2 else _DIMS_3D)[kind], preferred_element_type=F32)


def _dot1(a, b, kind):
    return _dg(a.astype(BF16), b.astype(BF16), kind)


@jax.custom_vjp
def _dhi(a, b):
    return _dot1(a, b, _NN)


_dhi.defvjp(lambda a, b: (_dot1(a, b, _NN), (a, b)),
            lambda res, ct: (_dot1(ct, res[1], _NT), _dot1(res[0], ct, _TN)))


@jax.custom_vjp
def _dnt(a, b):
    return _dot1(a, b, _NT)


_dnt.defvjp(lambda a, b: (_dot1(a, b, _NT), (a, b)),
            lambda res, ct: (_dot1(ct, res[1], _NN), _dot1(ct, res[0], _TN)))


@jax.custom_vjp
def _dtn(a, b):
    return _dot1(a, b, _TN)


_dtn.defvjp(lambda a, b: (_dot1(a, b, _TN), (a, b)),
            lambda res, ct: (_dot1(res[1], ct, _NT), _dot1(res[0], ct, _NN)))


def _split3(x):
    x1 = x.astype(BF16)
    r1 = x - x1.astype(F32)
    x2 = r1.astype(BF16)
    return x1, x2, (r1 - x2.astype(F32)).astype(BF16)


def _dot_exact_lhs(sel, x, kind):
    parts = [_dg(sel, xi, kind) for xi in _split3(x)]
    return parts[0] + parts[1] + parts[2]


def _tril_ones(like):
    c = like.shape[-2]
    ri, ci = _iotas(c)
    return jnp.broadcast_to((ri >= ci).astype(BF16), like.shape[:-2] + (c, c))


@jax.custom_vjp
def _cumsum_rows(x):
    return _dot_exact_lhs(_tril_ones(x), x, _NN)


_cumsum_rows.defvjp(lambda x: (_dot_exact_lhs(_tril_ones(x), x, _NN), None),
                    lambda _, ct: (_dot_exact_lhs(_tril_ones(ct), ct, _TN),))


@jax.custom_vjp
def _lane_sum_as_row(x):
    return _dot_exact_lhs(jnp.ones(x.shape, BF16), x, _NT)


def _lane_sum_as_row_bwd(_, ct):
    ones = jnp.ones(ct.shape[:-1] + (LANES,), BF16)
    parts = [_dg(ci, ones, _TN) for ci in _split3(ct)]
    return (parts[0] + parts[1] + parts[2],)


_lane_sum_as_row.defvjp(lambda x: (_dot_exact_lhs(jnp.ones(x.shape, BF16), x, _NT), None), _lane_sum_as_row_bwd)


def _shift_rows(x, halo, s):
    rows = lax.broadcasted_iota(jnp.int32, x.shape, 0)
    out = pltpu.roll(x, s, 0)
    for i in range(s):
        out = jnp.where(rows == i, halo[SUBLANES - s + i:SUBLANES - s + i + 1, :], out)
    return out


def _unshift_rows(g, carry, s):
    c = g.shape[0]
    rows = lax.broadcasted_iota(jnp.int32, g.shape, 0)
    out = pltpu.roll(g, c - s, 0)
    for i in range(s):
        out = jnp.where(rows == c - s + i, carry[i:i + 1, :], out)
    return out


def _sigmoid_plain(z):
    return 1.0 / (1.0 + jnp.exp(-z))


def _sigmoid_value(z):
    t = jnp.exp(-jnp.abs(z))
    r = 1.0 / (1.0 + t)
    return jnp.where(z >= 0, r, t * r)


@jax.custom_vjp
def _sigmoid(z):
    return _sigmoid_value(z)


def _sigmoid_fwd(z):
    s = _sigmoid_value(z)
    return s, s


_sigmoid.defvjp(_sigmoid_fwd, lambda s, ct: (ct * s * (1.0 - s),))


def _silu(z):
    return z * _sigmoid(z)


def _softplus(z):
    return jnp.maximum(z, 0.0) + jnp.log(1.0 + jnp.exp(-jnp.abs(z)))


def _rms(t, gain):
    return t * lax.rsqrt(jnp.mean(t * t, axis=-1, keepdims=True) + NORM_EPS) * gain


def _iotas(c):
    return lax.broadcasted_iota(jnp.int32, (c, c), 0), lax.broadcasted_iota(jnp.int32, (c, c), 1)


def _unit_lower_inverse(xm, eye):
    t = eye + xm
    p = xm
    for _ in range(5):
        p = _dhi(p, p)
        t = t + _dhi(t, p)
    return t


def _rwkv_head(pr, pk, pv, plo, qr, qk, qv, qlo, s0, pp, mulo, wl):
    c = pr.shape[1]
    n_heads = s0.shape[0]
    n_chunks = pr.shape[0] // n_heads
    ri, ci = _iotas(c)
    if n_chunks > 1:
        pp = jnp.concatenate([pp] * n_chunks, axis=0)
        wl = jnp.concatenate([wl] * n_chunks, axis=0)

    def mix(p, q, mu):
        return p + (q - p) * mu

    r = mix(pr, qr, pp[:, 0:1])
    k = mix(pk, qk, pp[:, 1:2])
    v = mix(pv, qv, pp[:, 2:3])
    lo = mix(plo, qlo, mulo)
    w0, a0, k_k, k_a, ln_w, ln_b, r_k = (pp[:, i:i + 1] for i in range(3, 10))

    def per_head(t):
        return jnp.concatenate([jnp.broadcast_to(t[i], (n_heads,) + t.shape[1:]) for i in range(n_chunks)], axis=0)

    zw = _dhi(per_head(jnp.tanh(lo)), wl[:, 0])
    za = _dhi(per_head(lo), wl[:, 1])
    g = _dhi(per_head(_sigmoid(lo)), wl[:, 2])
    w_log = -_softplus(-(w0 + zw)) - 0.5
    lw = -jnp.exp(w_log)
    a = _sigmoid(a0 + za)
    kk = k * k_k
    kk = kk * lax.rsqrt(jnp.sum(kk * kk, axis=-1, keepdims=True) + L2_EPS)
    k2 = k * (1.0 + (a - 1.0) * k_a)
    an = -kk
    b = kk * a
    causal = ri >= ci
    strict = ri > ci
    eye = (ri == ci).astype(F32)
    cl = _cumsum_rows(lw)
    ecl = jnp.exp(-cl)
    at = an * jnp.exp(cl - lw)
    bt = b * ecl
    kt = k2 * ecl
    rt = r * jnp.exp(cl)
    a_ab = jnp.where(strict, _dnt(at, bt), 0.0)
    a_ak = jnp.where(strict, _dnt(at, kt), 0.0)
    tinv = _unit_lower_inverse(a_ab, eye)
    akv = _dhi(a_ak, v)
    r_b = jnp.where(causal, _dnt(rt, bt), 0.0)
    rkv = _dhi(jnp.where(causal, _dnt(rt, kt), 0.0), v)
    cl_end = jnp.sum(lw, axis=1, keepdims=True)
    dec_end = jnp.exp(cl_end - cl)
    b_end = b * dec_end
    sv = _dtn(v, k2 * dec_end)
    e_end = jnp.exp(cl_end)
    state, ys = s0, []
    for i in range(n_chunks):
        sl = slice(i * n_heads, (i + 1) * n_heads)
        u = _dhi(tinv[sl], _dnt(at[sl], state) + akv[sl])
        ys.append(_dnt(rt[sl], state) + _dhi(r_b[sl], u) + rkv[sl])
        state = state * e_end[sl] + _dtn(u, b_end[sl]) + sv[sl]
    y = jnp.concatenate(ys, axis=0) if n_chunks > 1 else ys[0]
    s1 = state
    m = (lax.broadcasted_iota(jnp.int32, (1, LANES), 1) < RWKV_HEAD_DIM).astype(F32)
    mean = jnp.sum(y, axis=-1, keepdims=True) * (1.0 / RWKV_HEAD_DIM)
    yc = (y - mean) * m
    var = jnp.sum(yc * yc, axis=-1, keepdims=True) * (1.0 / RWKV_HEAD_DIM)
    yn = yc * lax.rsqrt(var + RWKV_GN_EPS) * ln_w + ln_b
    y2 = yn + jnp.sum(r * k2 * r_k, axis=-1, keepdims=True) * v
    return y2 * g, s1


def _gdn_head(xq, xk, xv, z, ab, s0, cw, gp, oha, ohb):
    c = z.shape[1]
    n_heads = s0.shape[0]
    n_chunks = z.shape[0] // n_heads
    ri, ci = _iotas(c)
    cw = jnp.concatenate([cw] * n_chunks, axis=0) if n_chunks > 1 else cw

    def conv(xs, w):
        out = xs[0] * w[:, GDN_CONV - 1:GDN_CONV]
        for s in range(1, GDN_CONV):
            out = out + xs[s] * w[:, GDN_CONV - 1 - s:GDN_CONV - s]
        return out

    q = _silu(conv(xq, cw[:, 0]))
    k = _silu(conv(xk, cw[:, 1]))
    v = _silu(conv(xv, cw[:, 2]))
    q = q * lax.rsqrt(jnp.sum(q * q, axis=-1, keepdims=True) + L2_EPS) * (GDN_HEAD_DIM ** -0.5)
    k = k * lax.rsqrt(jnp.sum(k * k, axis=-1, keepdims=True) + L2_EPS)
    gg = -jnp.exp(gp[0:1]) * _softplus(ab + gp[1:2])
    beta = jnp.sum(_sigmoid(ab) * ohb, axis=-1, keepdims=True)
    causal = ri >= ci
    strict = ri > ci
    eye = (ri == ci).astype(F32)
    gcm = _cumsum_rows(gg * oha)
    gc = jnp.sum(gcm, axis=-1, keepdims=True)
    gc_row = _lane_sum_as_row(gcm)
    dec = jnp.where(causal, jnp.exp(jnp.where(causal, gc - gc_row, 0.0)), 0.0)
    kb = k * beta
    vb = v * beta
    lm = jnp.where(strict, _dnt(kb, k) * dec, 0.0)
    tinv = _unit_lower_inverse(-lm, eye)
    egc = jnp.exp(gc)
    u = _dhi(tinv, vb)
    wk = _dhi(tinv, kb * egc)
    attn = jnp.where(causal, _dnt(q, k) * dec, 0.0)
    g_last = gc[:, c - 1:c, :]
    q_dec = q * egc
    k_dec = k * jnp.exp(g_last - gc)
    e_last = jnp.exp(g_last)
    state, outs = s0, []
    for i in range(n_chunks):
        sl = slice(i * n_heads, (i + 1) * n_heads)
        v_new = u[sl] - _dhi(wk[sl], state)
        outs.append(_dhi(q_dec[sl], state) + _dhi(attn[sl], v_new))
        state = state * e_last[sl] + _dtn(k_dec[sl], v_new)
    o = jnp.concatenate(outs, axis=0) if n_chunks > 1 else outs[0]
    return _rms(o, gp[2:3]) * _silu(z), state


def _head_id(grp, i, per_step, heads):
    return i if per_step == heads else grp * per_step + i


def _head_range(grp, per_step, heads):
    return slice(None) if per_step == heads else pl.ds(grp * per_step, per_step)


def _rwkv_specs(nmap):
    hb, groups = RWKV_HB, RWKV_HEADS // RWKV_HB
    cb = OFF_RKV // (hb * LANES)
    specs = []
    for j in range(3):
        specs.append(pl.BlockSpec((RWKV_TILE, hb * LANES), lambda n, g, j=j: (nmap(n), cb + j * groups + g)))
    specs.append(pl.BlockSpec((RWKV_TILE, 2 * LANES), lambda n, g: (nmap(n), OFF_LO // (2 * LANES))))
    per = RWKV_TILE // SUBLANES
    for j in range(3):
        specs.append(pl.BlockSpec((SUBLANES, hb * LANES),
                                  lambda n, g, j=j: (jnp.maximum(nmap(n) * per - 1, 0), cb + j * groups + g)))
    specs.append(pl.BlockSpec((SUBLANES, 2 * LANES), lambda n, g: (jnp.maximum(nmap(n) * per - 1, 0), OFF_LO // (2 * LANES))))
    specs.append(pl.BlockSpec((hb, 16, LANES), lambda n, g: (g, 0, 0)))
    specs.append(pl.BlockSpec((1, 2 * LANES), lambda n, g: (0, 0)))
    specs.append(pl.BlockSpec((hb, 3, 2 * LANES, LANES), lambda n, g: (g, 0, 0, 0)))
    return specs


def _rwkv_operands(refs, halos, live):
    pr, pk, pv, plo = refs
    hr, hk, hv, hlo = halos
    cur, prev = [], []
    for x, hx in ((pr, hr), (pk, hk), (pv, hv)):
        tiles = [_lane_block(x, h) for h in range(RWKV_HB)]
        cur.append(_chunk_batch(tiles))
        prev.append(_chunk_batch([_shift_rows(t_, _lane_block(hx, h) * live, 1) for h, t_ in enumerate(tiles)]))
    lo = plo[...]
    cur.append(_chunk_batch([lo]))
    prev.append(_chunk_batch([_shift_rows(lo, hlo[...] * live, 1)]))
    return cur, prev


def _rwkv_forward(p_cat, ppack, mulo, wl):
    t = p_cat.shape[0]
    n_chunks = t // RWKV_TILE

    def body(pr, pk, pv, plo, hr, hk, hv, hlo, pp, ml, w, out, st_out, s_scr):
        n, grp = pl.program_id(0), pl.program_id(1)

        hsl = _head_range(grp, RWKV_HB, RWKV_HEADS)

        @pl.when(n == 0)
        def _():
            s_scr[hsl] = jnp.zeros((RWKV_HB, LANES, LANES), F32)

        live = (n > 0).astype(F32)
        cur, prev = _rwkv_operands((pr, pk, pv, plo), (hr, hk, hv, hlo), live)
        s0 = s_scr[hsl]
        st_out[...] = s0
        o, s1 = _rwkv_head(*cur, *prev, s0, pp[...], ml[...], w[...])
        for h, tile in enumerate(_head_tiles(o, RWKV_HB)):
            out[:, h * LANES:(h + 1) * LANES] = tile.astype(out.dtype)
        s_scr[hsl] = s1

    return pl.pallas_call(
        body, name="rwkv_fwd", grid=(n_chunks, RWKV_HEADS // RWKV_HB),
        in_specs=_rwkv_specs(lambda n: n),
        out_specs=(pl.BlockSpec((RWKV_TILE, RWKV_HB * LANES), lambda n, g: (n, g)),
                   pl.BlockSpec((None, RWKV_HB, LANES, LANES), lambda n, g: (n, g, 0, 0))),
        out_shape=(jax.ShapeDtypeStruct((t, RWKV_HEADS * LANES), BF16),
                   jax.ShapeDtypeStruct((n_chunks, RWKV_HEADS, LANES, LANES), F32)),
        scratch_shapes=[pltpu.VMEM((RWKV_HEADS, LANES, LANES), F32)],
        compiler_params=_cparams(("arbitrary", "arbitrary")),
    )(p_cat, p_cat, p_cat, p_cat, p_cat, p_cat, p_cat, p_cat, ppack, mulo, wl)


def _rwkv_backward(p_cat, ppack, mulo, wl, states, d_out):
    t = p_cat.shape[0]
    n_chunks = t // RWKV_TILE
    last = n_chunks - 1

    def body(pr, pk, pv, plo, hr, hk, hv, hlo, pp, ml, w, st, dy, dpr, dpk, dpv, dplo, dpp, dml, dw, ds_scr, car_scr, carlo_scr):
        n, grp = pl.program_id(0), pl.program_id(1)

        hsl = _head_range(grp, RWKV_HB, RWKV_HEADS)
        gi = _head_id(grp, 0, 1, RWKV_HEADS // RWKV_HB)

        @pl.when(n == 0)
        def _():
            ds_scr[hsl] = jnp.zeros((RWKV_HB, LANES, LANES), F32)
            car_scr[hsl] = jnp.zeros((RWKV_HB, 3 * SUBLANES, LANES), F32)
            carlo_scr[gi] = jnp.zeros((SUBLANES, 2 * LANES), F32)

        @pl.when((n == 0) & (grp == 0))
        def _():
            dpp[...] = jnp.zeros(dpp.shape, F32)
            dml[...] = jnp.zeros(dml.shape, F32)
            dw[...] = jnp.zeros(dw.shape, F32)

        live = (n < last).astype(F32)
        cur, prev = _rwkv_operands((pr, pk, pv, plo), (hr, hk, hv, hlo), live)
        _, vjp = jax.vjp(_rwkv_head, *cur, *prev, st[...], pp[...], ml[...], w[...])
        g = vjp((_chunk_batch([_lane_block(dy, h) for h in range(RWKV_HB)]), ds_scr[hsl]))
        outs = (dpr, dpk, dpv)
        d_cur = [_head_tiles(g[j], RWKV_HB) for j in range(3)]
        d_prev = [_head_tiles(g[4 + j], RWKV_HB) for j in range(3)]
        for i in range(RWKV_HB):
            sl = slice(i * LANES, (i + 1) * LANES)
            h = _head_id(grp, i, RWKV_HB, RWKV_HEADS)
            car = car_scr[h]
            for j in range(3):
                tot = d_cur[j][i] + _unshift_rows(d_prev[j][i], car[SUBLANES * j:SUBLANES * (j + 1), :], 1)
                outs[j][:, sl] = tot.astype(outs[j].dtype)
                car_scr[h, SUBLANES * j:SUBLANES * (j + 1), :] = d_prev[j][i][0:SUBLANES, :]
        (dlo_cur,), (dlo_prev,) = _head_tiles(g[3], 1), _head_tiles(g[7], 1)
        dlo = dlo_cur + _unshift_rows(dlo_prev, carlo_scr[gi], 1)
        carlo_scr[gi] = dlo_prev[0:SUBLANES, :]
        ds_scr[hsl] = g[8]
        dpp[hsl] += g[9]
        dml[0, 0:1, :] += g[10]
        dw[hsl] += g[11]

        @pl.when(grp == 0)
        def _():
            dplo[...] = dlo

        @pl.when(grp > 0)
        def _():
            dplo[...] += dlo

    rev = lambda n: last - n
    in_specs = _rwkv_specs(rev) + [
        pl.BlockSpec((None, RWKV_HB, LANES, LANES), lambda n, g: (rev(n), g, 0, 0)),
        pl.BlockSpec((RWKV_TILE, RWKV_HB * LANES), lambda n, g: (rev(n), g)),
    ]
    hw = RWKV_HEADS * LANES
    return pl.pallas_call(
        body, name="rwkv_bwd", grid=(n_chunks, RWKV_HEADS // RWKV_HB),
        in_specs=in_specs,
        out_specs=(pl.BlockSpec((RWKV_TILE, RWKV_HB * LANES), lambda n, g: (rev(n), g)),
                   pl.BlockSpec((RWKV_TILE, RWKV_HB * LANES), lambda n, g: (rev(n), g)),
                   pl.BlockSpec((RWKV_TILE, RWKV_HB * LANES), lambda n, g: (rev(n), g)),
                   pl.BlockSpec((RWKV_TILE, 2 * LANES), lambda n, h: (rev(n), 0)),
                   pl.BlockSpec((RWKV_HEADS, 16, LANES), lambda n, h: (0, 0, 0)),
                   pl.BlockSpec((RWKV_HEADS, SUBLANES, 2 * LANES), lambda n, h: (0, 0, 0)),
                   pl.BlockSpec((RWKV_HEADS, 3, 2 * LANES, LANES), lambda n, h: (0, 0, 0, 0))),
        out_shape=(jax.ShapeDtypeStruct((t, hw), BF16), jax.ShapeDtypeStruct((t, hw), BF16), jax.ShapeDtypeStruct((t, hw), BF16),
                   jax.ShapeDtypeStruct((t, 2 * LANES), F32),
                   jax.ShapeDtypeStruct((RWKV_HEADS, 16, LANES), F32),
                   jax.ShapeDtypeStruct((RWKV_HEADS, SUBLANES, 2 * LANES), F32),
                   jax.ShapeDtypeStruct((RWKV_HEADS, 3, 2 * LANES, LANES), F32)),
        scratch_shapes=[pltpu.VMEM((RWKV_HEADS, LANES, LANES), F32),
                        pltpu.VMEM((RWKV_HEADS, 3 * SUBLANES, LANES), F32),
                        pltpu.VMEM((RWKV_HEADS, SUBLANES, 2 * LANES), F32)],
        compiler_params=_cparams(("arbitrary", "arbitrary")),
    )(p_cat, p_cat, p_cat, p_cat, p_cat, p_cat, p_cat, p_cat, ppack, mulo, wl, states, d_out)


def _gdn_specs(nmap):
    per = GDN_TILE // SUBLANES
    hb, groups = GDN_HB, GDN_HEADS // GDN_HB
    cb = OFF_QKV // (hb * LANES)
    specs = []
    for j in range(3):
        specs.append(pl.BlockSpec((GDN_TILE, hb * LANES), lambda n, g, j=j: (nmap(n), cb + j * groups + g)))
    for j in range(3):
        specs.append(pl.BlockSpec((SUBLANES, hb * LANES),
                                  lambda n, g, j=j: (jnp.maximum(nmap(n) * per - 1, 0), cb + j * groups + g)))
    specs.append(pl.BlockSpec((GDN_TILE, hb * LANES), lambda n, g: (nmap(n), OFF_Z // (hb * LANES) + g)))
    specs.append(pl.BlockSpec((GDN_TILE, LANES), lambda n, g: (nmap(n), OFF_AB // LANES)))
    specs.append(pl.BlockSpec((hb, 3, SUBLANES, LANES), lambda n, g: (g, 0, 0, 0)))
    specs.append(pl.BlockSpec((SUBLANES, LANES), lambda n, g: (0, 0)))
    return specs


def _conv_taps(x, halo):
    return (x,) + tuple(_shift_rows(x, halo, s) for s in range(1, GDN_CONV))


def _onehots(grp):
    nb = GDN_STEP_CHUNKS * GDN_HB
    lane = lax.broadcasted_iota(jnp.int32, (nb, 1, LANES), 2)
    head = lax.broadcasted_iota(jnp.int32, (nb, 1, LANES), 0) % GDN_HB + _head_id(grp, 0, GDN_HB, GDN_HEADS)
    return (lane == head).astype(F32), (lane == GDN_HEADS + head).astype(F32)


def _chunk_batch(tiles):
    n_chunks = tiles[0].shape[0] // CHUNK
    return jnp.stack([t_[i * CHUNK:(i + 1) * CHUNK, :] for i in range(n_chunks) for t_ in tiles])


def _head_tiles(batch, n_heads=GDN_HB):
    n_chunks = batch.shape[0] // n_heads
    return [jnp.concatenate([batch[i * n_heads + h] for i in range(n_chunks)], axis=0) for h in range(n_heads)]


def _lane_block(ref, h):
    return ref[:, h * LANES:(h + 1) * LANES]


def _gdn_taps(refs, halos, live):
    out = []
    for x, hx in zip(refs, halos):
        per_head = [_conv_taps(_lane_block(x, h), _lane_block(hx, h) * live) for h in range(GDN_HB)]
        out.append(tuple(_chunk_batch([per_head[h][s] for h in range(GDN_HB)]) for s in range(GDN_CONV)))
    return out


def _gdn_forward(p_cat, cwpack, gpar):
    t = p_cat.shape[0]
    n_chunks = t // GDN_TILE

    def body(xq, xk, xv, hq, hk, hv, z, ab, cw, gp, out, st_out, s_scr):
        n, grp = pl.program_id(0), pl.program_id(1)

        hsl = _head_range(grp, GDN_HB, GDN_HEADS)

        @pl.when(n == 0)
        def _():
            s_scr[hsl] = jnp.zeros((GDN_HB, LANES, LANES), F32)

        live = (n > 0).astype(F32)
        oha, ohb = _onehots(grp)
        s0 = s_scr[hsl]
        st_out[...] = s0
        taps = _gdn_taps((xq, xk, xv), (hq, hk, hv), live)
        zb = _chunk_batch([_lane_block(z, h) for h in range(GDN_HB)])
        abb = _chunk_batch([ab[...]] * GDN_HB)
        o, s1 = _gdn_head(*taps, zb, abb, s0, cw[...], gp[...], oha, ohb)
        for h, tile in enumerate(_head_tiles(o)):
            out[:, h * LANES:(h + 1) * LANES] = tile.astype(out.dtype)
        s_scr[hsl] = s1

    return pl.pallas_call(
        body, name="gdn_fwd", grid=(n_chunks, GDN_HEADS // GDN_HB),
        in_specs=_gdn_specs(lambda n: n),
        out_specs=(pl.BlockSpec((GDN_TILE, GDN_HB * LANES), lambda n, g: (n, g)),
                   pl.BlockSpec((None, GDN_HB, LANES, LANES), lambda n, g: (n, g, 0, 0))),
        out_shape=(jax.ShapeDtypeStruct((t, GDN_WIDTH), BF16),
                   jax.ShapeDtypeStruct((n_chunks, GDN_HEADS, LANES, LANES), F32)),
        scratch_shapes=[pltpu.VMEM((GDN_HEADS, LANES, LANES), F32)],
        compiler_params=_cparams(("arbitrary", "arbitrary")),
    )(p_cat, p_cat, p_cat, p_cat, p_cat, p_cat, p_cat, p_cat, cwpack, gpar)


def _gdn_backward(p_cat, cwpack, gpar, states, d_out):
    t = p_cat.shape[0]
    n_chunks = t // GDN_TILE
    last = n_chunks - 1

    def body(xq, xk, xv, hq, hk, hv, z, ab, cw, gp, st, dy, dq, dk, dv, dz, dab, dcw, dgp, ds_scr, car_scr):
        n, grp = pl.program_id(0), pl.program_id(1)

        hsl = _head_range(grp, GDN_HB, GDN_HEADS)

        @pl.when(n == 0)
        def _():
            ds_scr[hsl] = jnp.zeros((GDN_HB, LANES, LANES), F32)
            car_scr[hsl] = jnp.zeros((GDN_HB, 3 * GDN_CONV, SUBLANES, LANES), F32)

        @pl.when((n == 0) & (grp == 0))
        def _():
            dcw[...] = jnp.zeros(dcw.shape, F32)
            dgp[...] = jnp.zeros(dgp.shape, F32)

        live = (n < last).astype(F32)
        oha, ohb = _onehots(grp)
        fn = functools.partial(_gdn_head, oha=oha, ohb=ohb)
        taps = _gdn_taps((xq, xk, xv), (hq, hk, hv), live)
        zb = _chunk_batch([_lane_block(z, h) for h in range(GDN_HB)])
        abb = _chunk_batch([ab[...]] * GDN_HB)
        _, vjp = jax.vjp(fn, *taps, zb, abb, st[...], cw[...], gp[...])
        g = vjp((_chunk_batch([_lane_block(dy, h) for h in range(GDN_HB)]), ds_scr[hsl]))
        outs = (dq, dk, dv)
        tap_tiles = [[_head_tiles(g[j][s]) for s in range(GDN_CONV)] for j in range(3)]
        dz_tiles = _head_tiles(g[3])
        for i in range(GDN_HB):
            sl = slice(i * LANES, (i + 1) * LANES)
            h = _head_id(grp, i, GDN_HB, GDN_HEADS)
            for j in range(3):
                tot = tap_tiles[j][0][i]
                for s in range(1, GDN_CONV):
                    slot = j * GDN_CONV + s
                    tot = tot + _unshift_rows(tap_tiles[j][s][i], car_scr[h, slot], s)
                    car_scr[h, slot] = tap_tiles[j][s][i][0:SUBLANES, :]
                outs[j][:, sl] = tot.astype(outs[j].dtype)
            dz[:, sl] = dz_tiles[i].astype(dz.dtype)
        dab_tiles = _head_tiles(g[4])
        dab_sum = dab_tiles[0]
        for h in range(1, GDN_HB):
            dab_sum = dab_sum + dab_tiles[h]
        ds_scr[hsl] = g[5]
        dcw[hsl] += g[6]
        dgp[0] += g[7]

        @pl.when(grp == 0)
        def _():
            dab[...] = dab_sum

        @pl.when(grp > 0)
        def _():
            dab[...] += dab_sum

    rev = lambda n: last - n
    in_specs = _gdn_specs(rev) + [
        pl.BlockSpec((None, GDN_HB, LANES, LANES), lambda n, g: (rev(n), g, 0, 0)),
        pl.BlockSpec((GDN_TILE, GDN_HB * LANES), lambda n, g: (rev(n), g)),
    ]
    blk = pl.BlockSpec((GDN_TILE, GDN_HB * LANES), lambda n, g: (rev(n), g))
    return pl.pallas_call(
        body, name="gdn_bwd", grid=(n_chunks, GDN_HEADS // GDN_HB),
        in_specs=in_specs,
        out_specs=(blk, blk, blk, blk,
                   pl.BlockSpec((GDN_TILE, LANES), lambda n, h: (rev(n), 0)),
                   pl.BlockSpec((GDN_HEADS, 3, SUBLANES, LANES), lambda n, h: (0, 0, 0, 0)),
                   pl.BlockSpec((GDN_HEADS, SUBLANES, LANES), lambda n, h: (0, 0, 0))),
        out_shape=(jax.ShapeDtypeStruct((t, GDN_WIDTH), BF16), jax.ShapeDtypeStruct((t, GDN_WIDTH), BF16),
                   jax.ShapeDtypeStruct((t, GDN_WIDTH), BF16), jax.ShapeDtypeStruct((t, GDN_WIDTH), BF16),
                   jax.ShapeDtypeStruct((t, LANES), F32),
                   jax.ShapeDtypeStruct((GDN_HEADS, 3, SUBLANES, LANES), F32),
                   jax.ShapeDtypeStruct((GDN_HEADS, SUBLANES, LANES), F32)),
        scratch_shapes=[pltpu.VMEM((GDN_HEADS, LANES, LANES), F32),
                        pltpu.VMEM((GDN_HEADS, 3 * GDN_CONV, SUBLANES, LANES), F32)],
        compiler_params=_cparams(("arbitrary", "arbitrary")),
    )(p_cat, p_cat, p_cat, p_cat, p_cat, p_cat, p_cat, p_cat, cwpack, gpar, states, d_out)


MM_VMEM_BUDGET = 44 * 1024 * 1024
MM_MIN_STEPS = 4


def _mm_tiles(mode, m, n, k, out_bytes):
    tms = [t for t in (2048, 1024, 768, 512, 256, 128, 64) if m % t == 0 and (mode != 'tn' or t % LANES == 0)]
    tns = [t for t in (1408, 1024, 768, 512, 256, 128) if n % t == 0]
    tks = [t for t in (2048, 1920, 1408, 1024, 512, 256, 128, 64) if k % t == 0]
    best, best_key = None, None
    for tm in tms:
        for tn in tns:
            for tk in tks:
                nk = k // tk
                vmem = 2 * (tm * tk * 2 + tk * tn * 2 + tm * tn * out_bytes) + (tm * tn * 4 if nk > 1 else 0)
                steps = (m // tm) * (n // tn) * nk
                if vmem > MM_VMEM_BUDGET:
                    continue
                reread = m * k * (n // tn) + k * n * (m // tm)
                key = (steps >= MM_MIN_STEPS, tn if mode == 'tn' else 0, tm * tn * tk, -nk, -reread)
                if best_key is None or key > best_key:
                    best, best_key = (tm, tn, tk), key
    if best is None:
        raise ValueError(f"no matmul tile for {mode} {m}x{n}x{k}")
    return best


_MM_DIMS = {'nn': (((1,), (0,)), ((), ())), 'nt': (((1,), (1,)), ((), ())), 'tn': (((0,), (0,)), ((), ()))}


def _matmul(a, b, mode, out_dtype, name):
    if mode == 'nn':
        (m, k), (k2, n) = a.shape, b.shape
    elif mode == 'nt':
        (m, k), (n, k2) = a.shape, b.shape
    else:
        (k, m), (k2, n) = a.shape, b.shape
    assert k == k2, (a.shape, b.shape, mode)
    tm, tn, tk = _mm_tiles(mode, m, n, k, jnp.dtype(out_dtype).itemsize)
    nk = k // tk
    dims = _MM_DIMS[mode]

    def body(a_ref, b_ref, o_ref, acc_ref):
        kk = pl.program_id(2)
        part = lax.dot_general(a_ref[...], b_ref[...], dims, preferred_element_type=F32)
        if nk == 1:
            o_ref[...] = part.astype(o_ref.dtype)
            return

        @pl.when(kk == 0)
        def _():
            acc_ref[...] = part

        @pl.when((kk > 0) & (kk < nk - 1))
        def _():
            acc_ref[...] += part

        @pl.when(kk == nk - 1)
        def _():
            o_ref[...] = (acc_ref[...] + part).astype(o_ref.dtype)

    a_spec = pl.BlockSpec((tk, tm), lambda i, j, kk: (kk, i)) if mode == 'tn' else pl.BlockSpec((tm, tk), lambda i, j, kk: (i, kk))
    b_spec = pl.BlockSpec((tn, tk), lambda i, j, kk: (j, kk)) if mode == 'nt' else pl.BlockSpec((tk, tn), lambda i, j, kk: (kk, j))
    return pl.pallas_call(
        body, name=name, grid=(m // tm, n // tn, nk),
        in_specs=[a_spec, b_spec],
        out_specs=pl.BlockSpec((tm, tn), lambda i, j, kk: (i, j)),
        out_shape=jax.ShapeDtypeStruct((m, n), out_dtype),
        scratch_shapes=[pltpu.VMEM((tm, tn), F32)],
        compiler_params=_cparams(("parallel", "parallel", "arbitrary")),
    )(a, b)


ROW_TILE = 512


def _row_specs(rows, tm):
    return [pl.BlockSpec((tm, w), lambda i, ci=ci: (i, ci)) for (_, w, ci) in rows]


def _rw_forward(fn, rows, pars, outs, name):
    t = rows[0][0].shape[0]
    tm = min(ROW_TILE, t)
    nr, npar = len(rows), len(pars)

    def body(*refs):
        vals = [r[...].astype(F32) for r in refs[:nr]] + [p[...] for p in refs[nr:nr + npar]]
        res = fn(*vals)
        for o, v in zip(refs[nr + npar:], res):
            o[...] = v.astype(o.dtype)

    return pl.pallas_call(
        body, name=name, grid=(t // tm,),
        in_specs=_row_specs(rows, tm) + [pl.BlockSpec(p.shape, lambda i: (0, 0)) for p in pars],
        out_specs=tuple(pl.BlockSpec((tm, w), lambda i: (i, 0)) for (w, _) in outs),
        out_shape=tuple(jax.ShapeDtypeStruct((t, w), dt) for (w, dt) in outs),
        compiler_params=_cparams(("parallel",)),
    )(*[r[0] for r in rows], *pars)


def _rw_backward(fn, rows, pars, cots, drow_dtypes, name):
    t = rows[0][0].shape[0]
    tm = min(ROW_TILE, t)
    nr, npar, nc = len(rows), len(pars), len(cots)
    keep = [i for i, dt in enumerate(drow_dtypes) if dt is not None]

    def body(*refs):
        vals = [r[...].astype(F32) for r in refs[:nr]] + [p[...] for p in refs[nr:nr + npar]]
        cvals = tuple(c[...].astype(F32) for c in refs[nr + npar:nr + npar + nc])
        orefs = refs[nr + npar + nc:]
        _, vjp = jax.vjp(fn, *vals)
        g = vjp(cvals)
        for o, i in zip(orefs[:len(keep)], keep):
            o[...] = g[i].astype(o.dtype)
        first = pl.program_id(0) == 0
        for o, gi in zip(orefs[len(keep):], g[nr:]):
            @pl.when(first)
            def _(o=o, gi=gi):
                o[...] = gi

            @pl.when(jnp.logical_not(first))
            def _(o=o, gi=gi):
                o[...] += gi

    out_specs = [pl.BlockSpec((tm, rows[i][1]), lambda i_: (i_, 0)) for i in keep] + \
                [pl.BlockSpec(p.shape, lambda i_: (0, 0)) for p in pars]
    out_shape = [jax.ShapeDtypeStruct((t, rows[i][1]), drow_dtypes[i]) for i in keep] + \
                [jax.ShapeDtypeStruct(p.shape, F32) for p in pars]
    return pl.pallas_call(
        body, name=name, grid=(t // tm,),
        in_specs=_row_specs(rows, tm) + [pl.BlockSpec(p.shape, lambda i: (0, 0)) for p in pars] + _row_specs(cots, tm),
        out_specs=tuple(out_specs), out_shape=tuple(out_shape),
        compiler_params=_cparams(("arbitrary",)),
    )(*[r[0] for r in rows], *pars, *[c[0] for c in cots])


def _norm_fn(x, g):
    return (_rms(x, g),)


def _norm_skip_fn(x, g):
    return _rms(x, g), x


def _merge_fn(ga, gb, ya, yb):
    return (_sigmoid(ga) * ya + _sigmoid(gb) * yb,)


def _res_norm_fn(x, mo, g):
    x1 = x + mo
    return x1, _rms(x1, g)


def _loss_head(x1, fo, gf, target, name):
    t, d = x1.shape
    tm = min(ROW_TILE, t)

    def tile_loss(x2, g, tgt):
        err = _rms(x2, g) - tgt
        per_row = jnp.sum(err * err, axis=-1, keepdims=True) * (0.5 / d)
        return jnp.sum(per_row, axis=0, keepdims=True)

    def body(x1_ref, fo_ref, g_ref, t_ref, loss_ref, dx_ref, dxb_ref, dg_ref):
        x2 = x1_ref[...] + fo_ref[...]
        val, vjp = jax.vjp(functools.partial(tile_loss, tgt=t_ref[...]), x2, g_ref[...])
        dx2, dg = vjp(jnp.ones((1, 1), F32))
        dx_ref[...] = dx2
        dxb_ref[...] = dx2.astype(BF16)
        first = pl.program_id(0) == 0

        @pl.when(first)
        def _():
            loss_ref[...] = jnp.broadcast_to(val, loss_ref.shape)
            dg_ref[...] = dg

        @pl.when(jnp.logical_not(first))
        def _():
            loss_ref[...] += jnp.broadcast_to(val, loss_ref.shape)
            dg_ref[...] += dg

    row = pl.BlockSpec((tm, d), lambda i: (i, 0))
    vec = pl.BlockSpec((1, d), lambda i: (0, 0))
    return pl.pallas_call(
        body, name=name, grid=(t // tm,),
        in_specs=[row, row, vec, row],
        out_specs=(pl.BlockSpec((1, LANES), lambda i: (0, 0)), row, row, vec),
        out_shape=(jax.ShapeDtypeStruct((1, LANES), F32), jax.ShapeDtypeStruct((t, d), F32),
                   jax.ShapeDtypeStruct((t, d), BF16), jax.ShapeDtypeStruct((1, d), F32)),
        compiler_params=_cparams(("arbitrary",)),
    )(x1, fo, gf, target)


FFN_TILE_ROWS = 2048
FFN_TILE_COLS = 256
FFN_COL_BLOCKS = FFN_HIDDEN // FFN_TILE_COLS


def _conv3_past(x, halo, w):
    rows = lax.broadcasted_iota(jnp.int32, x.shape, 0)
    x1 = jnp.where(rows == 0, halo[7:8, :], pltpu.roll(x, 1, 0))
    x2 = jnp.where(rows == 0, halo[6:7, :], jnp.where(rows == 1, halo[7:8, :], pltpu.roll(x, 2, 0)))
    return x * w[2:3] + x1 * w[1:2] + x2 * w[0:1], x1, x2


def _ffn_in_specs(tm, imap, jmap):
    per = tm // SUBLANES
    tile = lambda off: pl.BlockSpec((tm, FFN_TILE_COLS), lambda *g: (imap(*g), off + jmap(*g) % FFN_COL_BLOCKS))
    halo = lambda off: pl.BlockSpec((SUBLANES, FFN_TILE_COLS),
                                    lambda *g: (jnp.maximum(imap(*g) * per - 1, 0), off + jmap(*g) % FFN_COL_BLOCKS))
    wsp = lambda off: pl.BlockSpec((FFN_CONV, FFN_TILE_COLS), lambda *g: (0, off + jmap(*g) % FFN_COL_BLOCKS))
    return [tile(0), halo(0), wsp(0), tile(FFN_COL_BLOCKS), halo(FFN_COL_BLOCKS), wsp(FFN_COL_BLOCKS)]


def _ffn_act_forward(hpre, cw):
    t = hpre.shape[0]
    tm = min(FFN_TILE_ROWS, t)

    def body(hg, pg, wg, hu, pu, wu, out):
        live = (pl.program_id(0) > 0).astype(F32)
        cg, _, _ = _conv3_past(hg[...], pg[...] * live, wg[...])
        cu, _, _ = _conv3_past(hu[...], pu[...] * live, wu[...])
        out[...] = (cg * _sigmoid_plain(cg) * cu).astype(out.dtype)

    return pl.pallas_call(
        body, name="ffn_act_fwd", grid=(t // tm, FFN_COL_BLOCKS),
        in_specs=_ffn_in_specs(tm, lambda i, j: i, lambda i, j: j),
        out_specs=pl.BlockSpec((tm, FFN_TILE_COLS), lambda i, j: (i, j)),
        out_shape=jax.ShapeDtypeStruct((t, FFN_HIDDEN), BF16),
        compiler_params=_cparams(("parallel", "parallel")),
    )(hpre, hpre, cw, hpre, hpre, cw)


def _conv3_future(d, nxt, w):
    tm = d.shape[0]
    rows = lax.broadcasted_iota(jnp.int32, d.shape, 0)
    d1 = jnp.where(rows == tm - 1, nxt[0:1, :], pltpu.roll(d, tm - 1, 0))
    d2 = jnp.where(rows == tm - 1, nxt[1:2, :], jnp.where(rows == tm - 2, nxt[0:1, :], pltpu.roll(d, tm - 2, 0)))
    return d * w[2:3] + d1 * w[1:2] + d2 * w[0:1]


def _ffn_backward(hpre, cw, dact):
    t = hpre.shape[0]
    tm = min(FFN_TILE_ROWS, t)
    n_tiles = t // tm
    per = tm // SUBLANES

    def d_conv_out(cg, cu, d):
        s = _sigmoid_plain(cg)
        return d * cu * s * (1.0 + cg * (1.0 - s)), d * cg * s

    def body(hg, pg, ng, wg, hu, pu, nu, wu, da, dan, dhg, dhu, dwg, dwu):
        i = pl.program_id(1)
        live_prev = (i > 0).astype(F32)
        live_next = (i < n_tiles - 1).astype(F32)
        xg, xu = hg[...], hu[...]
        cg, g1, g2 = _conv3_past(xg, pg[...] * live_prev, wg[...])
        cu, u1, u2 = _conv3_past(xu, pu[...] * live_prev, wu[...])
        dg, du = d_conv_out(cg, cu, da[...])
        cgn, _, _ = _conv3_past(ng[...], hg[tm - SUBLANES:tm, :], wg[...])
        cun, _, _ = _conv3_past(nu[...], hu[tm - SUBLANES:tm, :], wu[...])
        dgn, dun = d_conv_out(cgn, cun, dan[...] * live_next)
        dhg[...] = _conv3_future(dg, dgn, wg[...]).astype(dhg.dtype)
        dhu[...] = _conv3_future(du, dun, wu[...]).astype(dhu.dtype)
        sums_g = [jnp.sum(xs * dg, axis=0, keepdims=True) for xs in (g2, g1, xg)]
        sums_u = [jnp.sum(xs * du, axis=0, keepdims=True) for xs in (u2, u1, xu)]

        @pl.when(i == 0)
        def _():
            for r_ in range(FFN_CONV):
                dwg[r_:r_ + 1, :] = sums_g[r_]
                dwu[r_:r_ + 1, :] = sums_u[r_]

        @pl.when(i > 0)
        def _():
            for r_ in range(FFN_CONV):
                dwg[r_:r_ + 1, :] += sums_g[r_]
                dwu[r_:r_ + 1, :] += sums_u[r_]

    nb = FFN_COL_BLOCKS
    nxt = lambda i: jnp.minimum((i + 1) * per, t // SUBLANES - 1)
    prv = lambda i: jnp.maximum(i * per - 1, 0)
    half = lambda off: [pl.BlockSpec((tm, FFN_TILE_COLS), lambda j, i: (i, off + j)),
                        pl.BlockSpec((SUBLANES, FFN_TILE_COLS), lambda j, i: (prv(i), off + j)),
                        pl.BlockSpec((SUBLANES, FFN_TILE_COLS), lambda j, i: (nxt(i), off + j)),
                        pl.BlockSpec((FFN_CONV, FFN_TILE_COLS), lambda j, i: (0, off + j))]
    tile = pl.BlockSpec((tm, FFN_TILE_COLS), lambda j, i: (i, j))
    taps = pl.BlockSpec((FFN_CONV, FFN_TILE_COLS), lambda j, i: (0, j))
    return pl.pallas_call(
        body, name="ffn_bwd", grid=(nb, n_tiles),
        in_specs=half(0) + half(nb) + [tile, pl.BlockSpec((SUBLANES, FFN_TILE_COLS), lambda j, i: (nxt(i), j))],
        out_specs=(tile, tile, taps, taps),
        out_shape=(jax.ShapeDtypeStruct((t, FFN_HIDDEN), BF16), jax.ShapeDtypeStruct((t, FFN_HIDDEN), BF16),
                   jax.ShapeDtypeStruct((FFN_CONV, FFN_HIDDEN), F32), jax.ShapeDtypeStruct((FFN_CONV, FFN_HIDDEN), F32)),
        compiler_params=_cparams(("parallel", "arbitrary")),
    )(hpre, hpre, hpre, cw, hpre, hpre, hpre, cw, dact, dact)


def _my_place():
    x, y, c = lax.axis_index("x"), lax.axis_index("y"), lax.axis_index("c")
    return x, y, c, 4 * x + 2 * y + c


N_CHIPS = 4


def _remote(src, dst, send_sem, recv_sem, dev):
    return pltpu.make_async_remote_copy(src_ref=src, dst_ref=dst, send_sem=send_sem, recv_sem=recv_sem, device_id=dev,
                                        device_id_type=pl.DeviceIdType.MESH)


def _chip_peer(x, y, k):
    return x ^ ((k >> 1) & 1), y ^ (k & 1)


def _all_gather_two_level(shard, name):
    r, w = shard.shape

    def body(src, out, send_sems, recv_sems, local_sem):
        x, y, c, me = _my_place()
        sibling = (x, y, 1 - c)
        mine = pltpu.make_async_copy(src, out.at[me], local_sem)
        mine.start()
        first = [_remote(src, out.at[me], send_sems.at[0], recv_sems.at[0], sibling)]
        for k in range(1, N_CHIPS):
            px, py = _chip_peer(x, y, k)
            first.append(_remote(src, out.at[me], send_sems.at[k], recv_sems.at[k], (px, py, c)))
        for cp in first:
            cp.start()
        passed = []
        for k in range(1, N_CHIPS):
            px, py = _chip_peer(x, y, k)
            landed = out.at[me ^ (2 * k)]
            _remote(src, landed, send_sems.at[k], recv_sems.at[k], (px, py, c)).wait_recv()
            fwd = _remote(landed, landed, send_sems.at[N_CHIPS - 1 + k], recv_sems.at[N_CHIPS - 1 + k], sibling)
            fwd.start()
            passed.append(fwd)
        _remote(src, out.at[me ^ 1], send_sems.at[0], recv_sems.at[0], sibling).wait_recv()
        for k in range(1, N_CHIPS):
            got = out.at[(me ^ 1) ^ (2 * k)]
            _remote(got, got, send_sems.at[N_CHIPS - 1 + k], recv_sems.at[N_CHIPS - 1 + k], sibling).wait_recv()
        for cp in first + passed:
            cp.wait_send()
        mine.wait()

    return pl.pallas_call(
        body, name=name,
        in_specs=[pl.BlockSpec(memory_space=pl.ANY)],
        out_specs=pl.BlockSpec(memory_space=pl.ANY),
        out_shape=jax.ShapeDtypeStruct((N_DEV, r, w), shard.dtype),
        scratch_shapes=[pltpu.SemaphoreType.DMA((N_DEV - 1,)), pltpu.SemaphoreType.DMA((N_DEV - 1,)), pltpu.SemaphoreType.DMA],
    )(shard)


def _device_peer(x, y, c, k):
    px, py, pc = x ^ ((k >> 2) & 1), y ^ ((k >> 1) & 1), c ^ (k & 1)
    return (px, py, pc), 4 * px + 2 * py + pc


_HBM = pl.BlockSpec(memory_space=pltpu.HBM)
_SEM = pl.BlockSpec(memory_space=pltpu.SEMAPHORE)


def _gather_start(shard, name):
    def body(src, land, send_sems, recv_sems, src_thru, land_thru, token):
        x, y, c, me = _my_place()
        for k in range(1, N_DEV):
            dev, _ = _device_peer(x, y, c, k)
            _remote(src, land.at[me], send_sems.at[k], recv_sems.at[k], dev).start()
        token[...] = jnp.zeros_like(token)

    landing = lax.empty((N_DEV,) + shard.shape, shard.dtype)
    return pl.pallas_call(
        body, name=name,
        out_shape=(pltpu.SemaphoreType.DMA((N_DEV,)), pltpu.SemaphoreType.DMA((N_DEV,)), pltpu.HBM(shard.shape, shard.dtype),
                   pltpu.HBM(landing.shape, landing.dtype), jax.ShapeDtypeStruct((SUBLANES, LANES), F32)),
        in_specs=(_HBM, _HBM), out_specs=(_SEM, _SEM, _HBM, _HBM, pl.BlockSpec(memory_space=pltpu.VMEM)),
        input_output_aliases={0: 2, 1: 3},
        compiler_params=pltpu.CompilerParams(has_side_effects=pltpu.SideEffectType.DATAFLOW_SIDE_EFFECTING),
    )(pltpu.with_memory_space_constraint(shard, pltpu.HBM), pltpu.with_memory_space_constraint(landing, pltpu.HBM))


def _gather_wait(send_sems, recv_sems, shard, landing, after, name):
    n_after = len(after)

    def body(*refs):
        src, land, send_sems, recv_sems = refs[:4]
        x, y, c, _ = _my_place()
        for k in range(1, N_DEV):
            dev, idx = _device_peer(x, y, c, k)
            cp = _remote(src, land.at[idx], send_sems.at[k], recv_sems.at[k], dev)
            cp.wait_send()
            cp.wait_recv()

    return pl.pallas_call(
        body, name=name,
        out_shape=(pltpu.HBM(shard.shape, shard.dtype), pltpu.HBM(landing.shape, landing.dtype)),
        in_specs=(_HBM, _HBM, _SEM, _SEM) + (pl.BlockSpec(memory_space=pl.ANY),) * n_after, out_specs=(_HBM, _HBM),
        input_output_aliases={0: 0, 1: 1},
        compiler_params=pltpu.CompilerParams(has_side_effects=pltpu.SideEffectType.DATAFLOW_SIDE_EFFECTING),
    )(shard, landing, send_sems, recv_sems, *after)[1]


def _slab_push_start(slabs, name):
    na = len(slabs)

    def body(*refs):
        srcs, lands = refs[:na], refs[na:2 * na]
        send_sems, recv_sems = refs[2 * na], refs[2 * na + 1]
        token = refs[-1]
        x, y, c, me = _my_place()
        for i in range(na):
            for k in range(1, N_DEV):
                dev, idx = _device_peer(x, y, c, k)
                s = i * N_DEV + k
                _remote(srcs[i].at[idx], lands[i].at[me], send_sems.at[s], recv_sems.at[s], dev).start()
        token[...] = jnp.zeros_like(token)

    hbm_shapes = [pltpu.HBM(a.shape, a.dtype) for a in slabs]
    ins = [pltpu.with_memory_space_constraint(a, pltpu.HBM) for a in slabs]
    ins += [pltpu.with_memory_space_constraint(lax.empty(a.shape, a.dtype), pltpu.HBM) for a in slabs]
    out = pl.pallas_call(
        body, name=name,
        out_shape=(pltpu.SemaphoreType.DMA((na * N_DEV,)), pltpu.SemaphoreType.DMA((na * N_DEV,)), *hbm_shapes, *hbm_shapes,
                   jax.ShapeDtypeStruct((SUBLANES, LANES), F32)),
        in_specs=(_HBM,) * (2 * na), out_specs=(_SEM, _SEM) + (_HBM,) * (2 * na) + (pl.BlockSpec(memory_space=pltpu.VMEM),),
        input_output_aliases={i: 2 + i for i in range(2 * na)},
        compiler_params=pltpu.CompilerParams(has_side_effects=pltpu.SideEffectType.DATAFLOW_SIDE_EFFECTING),
    )(*ins)
    return out[0], out[1], list(out[2:2 + na]), list(out[2 + na:2 + 2 * na]), out[-1]


def _slab_push_wait(send_sems, recv_sems, slabs, landings, after, name):
    na = len(slabs)

    def body(*refs):
        srcs, lands = refs[:na], refs[na:2 * na]
        send_sems, recv_sems = refs[2 * na], refs[2 * na + 1]
        x, y, c, me = _my_place()
        for i in range(na):
            for k in range(1, N_DEV):
                dev, idx = _device_peer(x, y, c, k)
                s = i * N_DEV + k
                cp = _remote(srcs[i].at[idx], lands[i].at[idx], send_sems.at[s], recv_sems.at[s], dev)
                cp.wait_send()
                cp.wait_recv()

    hbm_shapes = tuple(pltpu.HBM(a.shape, a.dtype) for a in slabs)
    out = pl.pallas_call(
        body, name=name, out_shape=hbm_shapes + hbm_shapes,
        in_specs=(_HBM,) * (2 * na) + (_SEM, _SEM) + (pl.BlockSpec(memory_space=pl.ANY),) * len(after),
        out_specs=(_HBM,) * (2 * na), input_output_aliases={i: i for i in range(2 * na)},
        compiler_params=pltpu.CompilerParams(has_side_effects=pltpu.SideEffectType.DATAFLOW_SIDE_EFFECTING),
    )(*slabs, *landings, send_sems, recv_sems, *after)
    return list(out[na:])


def _pair_exchange(arrays, name):
    na = len(arrays)

    def body(*refs):
        srcs, dsts, (send_sems, recv_sems) = refs[:na], refs[na:2 * na], refs[2 * na:]
        x, y, c, _ = _my_place()
        sibling = (x, y, 1 - c)
        copies = []
        for i in range(na):
            for q in range(N_CHIPS):
                s = i * N_CHIPS + q
                copies.append(_remote(srcs[i].at[2 * q + 1 - c], dsts[i].at[q], send_sems.at[s], recv_sems.at[s], sibling))
        for cp in copies:
            cp.start()
        for cp in copies:
            cp.wait_recv()
        for cp in copies:
            cp.wait_send()

    hbm = pl.BlockSpec(memory_space=pl.ANY)
    return pl.pallas_call(
        body, name=name, in_specs=[hbm] * na, out_specs=tuple([hbm] * na),
        out_shape=tuple(jax.ShapeDtypeStruct((N_CHIPS,) + a.shape[1:], a.dtype) for a in arrays),
        scratch_shapes=[pltpu.SemaphoreType.DMA((na * N_CHIPS,)), pltpu.SemaphoreType.DMA((na * N_CHIPS,))],
    )(*arrays)


ELEMENTWISE_COLS = 256


def _pair_sum(slabs, recv, core, out_dtype, name):
    _, r, w = slabs.shape
    tc = ELEMENTWISE_COLS

    def body(core_ref, mine, theirs, out):
        out[...] = (mine[...] + theirs[...]).astype(out.dtype)

    grid_spec = pltpu.PrefetchScalarGridSpec(
        num_scalar_prefetch=1, grid=(N_CHIPS, w // tc),
        in_specs=[pl.BlockSpec((None, r, tc), lambda q, j, core_ref: (2 * q + core_ref[0], 0, j)),
                  pl.BlockSpec((None, r, tc), lambda q, j, core_ref: (q, 0, j))],
        out_specs=pl.BlockSpec((None, r, tc), lambda q, j, core_ref: (q, 0, j)))
    return pl.pallas_call(body, name=name, grid_spec=grid_spec,
                          out_shape=jax.ShapeDtypeStruct((N_CHIPS, r, w), out_dtype),
                          compiler_params=_cparams(("parallel", "parallel")))(core, slabs, recv)


def _chip_exchange(arrays, name):
    na = len(arrays)

    def body(*refs):
        srcs, dsts, (send_sems, recv_sems, local_sems) = refs[:na], refs[na:2 * na], refs[2 * na:]
        x, y, c, _ = _my_place()
        chip = 2 * x + y
        own = [pltpu.make_async_copy(srcs[i].at[chip], dsts[i].at[chip], local_sems.at[i]) for i in range(na)]
        for cp in own:
            cp.start()
        sends, arrivals = [], []
        for i in range(na):
            for k in range(1, N_CHIPS):
                px, py = _chip_peer(x, y, k)
                s = i * N_CHIPS + k
                sends.append(_remote(srcs[i].at[chip ^ k], dsts[i].at[chip], send_sems.at[s], recv_sems.at[s], (px, py, c)))
                arrivals.append(_remote(srcs[i].at[chip], dsts[i].at[chip ^ k], send_sems.at[s], recv_sems.at[s], (px, py, c)))
        for cp in sends:
            cp.start()
        for cp in arrivals:
            cp.wait_recv()
        for cp in sends:
            cp.wait_send()
        for cp in own:
            cp.wait()

    hbm = pl.BlockSpec(memory_space=pl.ANY)
    return pl.pallas_call(
        body, name=name, in_specs=[hbm] * na, out_specs=tuple([hbm] * na),
        out_shape=tuple(jax.ShapeDtypeStruct(a.shape, a.dtype) for a in arrays),
        scratch_shapes=[pltpu.SemaphoreType.DMA((na * N_CHIPS,)), pltpu.SemaphoreType.DMA((na * N_CHIPS,)),
                        pltpu.SemaphoreType.DMA((na,))],
    )(*arrays)


def _adamw_update(g, w, m, v):
    c1 = 1.0 / (1.0 - ADAM_B1 ** ADAM_STEP)
    c2 = 1.0 / (1.0 - ADAM_B2 ** ADAM_STEP)
    mn = ADAM_B1 * m + (1.0 - ADAM_B1) * g
    vn = ADAM_B2 * v + (1.0 - ADAM_B2) * (g * g)
    return -ADAM_LR * ((mn * c1) / (jnp.sqrt(vn * c2) + ADAM_EPS) + ADAM_WD * w), mn, vn


def _reduce_adamw(parts, w, m, v, name):
    n_parts, r, wd = parts.shape
    tc = ELEMENTWISE_COLS

    def body(p_ref, w_ref, m_ref, v_ref, g_out, d_out, m_out, v_out):
        g = p_ref[0].astype(F32)
        for s in range(1, n_parts):
            g = g + p_ref[s].astype(F32)
        g_out[...] = g
        d_out[...], m_out[...], v_out[...] = _adamw_update(g, w_ref[...], m_ref[...], v_ref[...])

    blk = pl.BlockSpec((r, tc), lambda j: (0, j))
    shp = jax.ShapeDtypeStruct((r, wd), F32)
    return pl.pallas_call(
        body, name=name, grid=(wd // tc,),
        in_specs=[pl.BlockSpec((n_parts, r, tc), lambda j: (0, 0, j)), blk, blk, blk],
        out_specs=(blk, blk, blk, blk), out_shape=(shp, shp, shp, shp),
        compiler_params=_cparams(("parallel",)),
    )(parts, w, m, v)


def _reduce_landed_adamw(landing, own, me, w, m, v, name):
    n_parts, r, wd = landing.shape
    tc = ELEMENTWISE_COLS

    def body(me_ref, land_ref, own_ref, w_ref, m_ref, v_ref, g_out, d_out, m_out, v_out):
        mine = own_ref[...].astype(F32)
        g = None
        for s in range(n_parts):
            part = jnp.where(me_ref[0] == s, mine, land_ref[s].astype(F32))
            g = part if g is None else g + part
        g_out[...] = g
        d_out[...], m_out[...], v_out[...] = _adamw_update(g, w_ref[...], m_ref[...], v_ref[...])

    blk = pl.BlockSpec((r, tc), lambda j, me_ref: (0, j))
    shp = jax.ShapeDtypeStruct((r, wd), F32)
    grid_spec = pltpu.PrefetchScalarGridSpec(
        num_scalar_prefetch=1, grid=(wd // tc,),
        in_specs=[pl.BlockSpec((n_parts, r, tc), lambda j, me_ref: (0, 0, j)),
                  pl.BlockSpec((None, r, tc), lambda j, me_ref: (me_ref[0], 0, j)), blk, blk, blk],
        out_specs=(blk, blk, blk, blk))
    return pl.pallas_call(body, name=name, grid_spec=grid_spec, out_shape=(shp, shp, shp, shp),
                          compiler_params=_cparams(("parallel",)))(me, landing, own, w, m, v)


PACK_W = 1024


def _pad_heads(a, slots):
    lead = a.shape[:-1]
    a = a.reshape(lead + (slots, RWKV_HEAD_DIM))
    a = jnp.pad(a, [(0, 0)] * (len(lead) + 1) + [(0, LANES - RWKV_HEAD_DIM)])
    return a.reshape(lead + (slots * LANES,))


def _flat_pack(arrs, dtype, row_mult):
    flat = jnp.concatenate([a.reshape(-1).astype(dtype) for a in arrs])
    n = flat.shape[0]
    rows = -(-n // PACK_W)
    rows = -(-rows // row_mult) * row_mult
    return jnp.pad(flat, (0, rows * PACK_W - n)).reshape(rows, PACK_W)


def _row_pack(arrs, dtype, row_mult):
    parts = [a.astype(dtype) if a.shape[1] == PACK_W else a.astype(dtype).reshape(-1, PACK_W) for a in arrs]
    rows = sum(p.shape[0] for p in parts)
    pad = -(-rows // row_mult) * row_mult - rows
    return jnp.concatenate(parts + ([jnp.zeros((pad, PACK_W), dtype)] if pad else []), axis=0)


def _unpack_row_gathered(g, names, shard_shapes):
    out, r0 = {}, 0
    for n in names:
        s = shard_shapes[n]
        rows = s[0] * s[1] // PACK_W
        seg = g[:, r0:r0 + rows, :]
        r0 += rows
        if s[1] == PACK_W:
            assert SHARD_AXIS[n] == 0
            out[n] = seg.reshape(N_DEV * s[0], s[1])
        else:
            assert SHARD_AXIS[n] == 1
            out[n] = jnp.transpose(seg.reshape((N_DEV,) + tuple(s)), (1, 0, 2)).reshape(s[0], N_DEV * s[1])
    return out


def _unpack_gathered(g, names, shard_shapes):
    flat = g.reshape(N_DEV, -1)
    out, off = {}, 0
    for n in names:
        s = shard_shapes[n]
        size = s[0] * s[1]
        seg = flat[:, off:off + size].reshape((N_DEV,) + tuple(s))
        off += size
        if SHARD_AXIS[n] == 1:
            out[n] = jnp.transpose(seg, (1, 0, 2)).reshape(s[0], N_DEV * s[1])
        else:
            out[n] = seg.reshape(N_DEV * s[0], s[1])
    return out


def _shard_major(full, axis):
    a, b = full.shape
    if axis == 1:
        return jnp.transpose(full.reshape(a, N_DEV, b // N_DEV), (1, 0, 2)).reshape(N_DEV, -1)
    return full.reshape(N_DEV, -1)


def _prepare_weights(full, rep):
    w = full['w_in']
    d = w.shape[1]
    rkv = jnp.pad(w[0:1536].reshape(3 * RWKV_HEADS, RWKV_HEAD_DIM, d), ((0, 0), (0, LANES - RWKV_HEAD_DIM), (0, 0)))
    w_cat = jnp.concatenate([
        w[3848:4872], w[4872:5896], rkv.reshape(3 * RWKV_HEADS * LANES, d), w[1792:3328], w[3328:3840],
        w[1536:1792], jnp.pad(w[3840:3848], ((0, LANES - 8), (0, 0))), jnp.zeros((LANES, d), w.dtype)], axis=0)
    assert w_cat.shape[0] == CAT_W
    mu = rep['rwkv_mu']
    vecs = [mu[0:512], mu[512:1024], mu[1024:1536], rep['rwkv_w0'], rep['rwkv_a0'], rep['rwkv_k_k'], rep['rwkv_k_a'],
            rep['rwkv_ln_w'], rep['rwkv_ln_b'], rep['rwkv_r_k'].reshape(-1)]
    ppack = jnp.stack([jnp.pad(v.reshape(RWKV_HEADS, RWKV_HEAD_DIM), ((0, 0), (0, LANES - RWKV_HEAD_DIM))) for v in vecs], axis=1)
    ppack = jnp.pad(ppack, ((0, 0), (0, 16 - len(vecs)), (0, 0)))
    mulo = mu[1536:1792].reshape(1, 2 * LANES)
    wl = jnp.zeros((3, 2 * LANES, RWKV_HEADS * LANES), F32)
    wl = wl.at[0, 0:64].set(_pad_heads(full['rwkv_w2'], RWKV_HEADS))
    wl = wl.at[1, 64:128].set(_pad_heads(full['rwkv_a2'], RWKV_HEADS))
    wl = wl.at[2, 128:256].set(_pad_heads(full['rwkv_g2'], RWKV_HEADS))
    wl = jnp.transpose(wl.reshape(3, 2 * LANES, RWKV_HEADS, LANES), (2, 0, 1, 3))
    cw = full['gdn_conv_w'].reshape(GDN_CONV, 3, GDN_HEADS, LANES)
    cwpack = jnp.pad(jnp.transpose(cw, (2, 1, 0, 3)), ((0, 0), (0, 0), (0, SUBLANES - GDN_CONV), (0, 0)))
    gpar = jnp.zeros((SUBLANES, LANES), F32)
    gpar = gpar.at[0, 0:GDN_HEADS].set(rep['gdn_a_log']).at[1, 0:GDN_HEADS].set(rep['gdn_dt_bias']).at[2].set(rep['gdn_norm_w'])
    return dict(w_cat=w_cat, ffn_cw=full['ffn_conv_w'], ppack=ppack, mulo=mulo, wl=wl, cwpack=cwpack, gpar=gpar,
                g1=rep['norm1_g'].reshape(1, -1), g2=rep['norm2_g'].reshape(1, -1), gf=rep['final_g'].reshape(1, -1))


def _prepare_late_weights(full):
    rp = full['rwkv_proj']
    rproj = jnp.pad(rp.reshape(RWKV_HEADS, RWKV_HEAD_DIM, -1), ((0, 0), (0, LANES - RWKV_HEAD_DIM), (0, 0))).reshape(RWKV_HEADS * LANES, -1)
    return dict(rproj=rproj, gproj=full['gdn_proj'], w_out=full['w_out'], ffn_up=full['ffn_up'], ffn_down=full['ffn_down'])


def _local_step(x, target, p, late_weights, push_grads):
    d = x.shape[1]
    full_w = lambda a: (a, a.shape[1], 0)
    (u,) = _rw_forward(_norm_fn, [full_w(x)], [p['g1']], [(d, BF16)], "norm1")
    p_cat = _matmul(u, p['w_cat'], 'nt', F32, "proj_in")
    ya_pre, st_r = _rwkv_forward(p_cat, p['ppack'], p['mulo'], p['wl'])
    yb_pre, st_g = _gdn_forward(p_cat, p['cwpack'], p['gpar'])
    p = {**p, **late_weights((ya_pre, yb_pre))}
    ya = _matmul(ya_pre, p['rproj'], 'nn', F32, "rwkv_proj")
    yb = _matmul(yb_pre, p['gproj'], 'nn', F32, "gdn_proj")
    gates = [(p_cat, d, OFF_GA // d), (p_cat, d, OFF_GB // d)]
    (mixed,) = _rw_forward(_merge_fn, gates + [full_w(ya), full_w(yb)], [], [(d, BF16)], "merge")
    mo = _matmul(mixed, p['w_out'], 'nn', F32, "out_proj")
    x1, n2 = _rw_forward(_res_norm_fn, [full_w(x), full_w(mo)], [p['g2']], [(d, F32), (d, BF16)], "res_norm2")
    hpre = _matmul(n2, p['ffn_up'], 'nt', F32, "ffn_up")
    act = _ffn_act_forward(hpre, p['ffn_cw'])
    fo = _matmul(act, p['ffn_down'], 'nn', F32, "ffn_down")
    loss_vec, dx2, dx2b, dgf = _loss_head(x1, fo, p['gf'], target, "loss_head")

    dact = _matmul(dx2b, p['ffn_down'], 'nt', F32, "d_act")
    dw_down = _matmul(act, dx2b, 'tn', BF16, "dw_ffn_down")
    dh_gate, dh_up, dcw_gate, dcw_up = _ffn_backward(hpre, p['ffn_cw'], dact)
    dh = jnp.concatenate([dh_gate, dh_up], axis=1)
    dcw_f = jnp.concatenate([dcw_gate, dcw_up], axis=1)
    dn2 = _matmul(dh, p['ffn_up'], 'nn', F32, "d_norm2")
    dw_up = _matmul(dh, n2, 'tn', BF16, "dw_ffn_up")
    token = push_grads({'ffn_down': dw_down, 'ffn_up': dw_up})
    dx1, dx1b, dg2 = _rw_backward(_res_norm_fn, [full_w(x), full_w(mo)], [p['g2'] + token], [full_w(dx2), full_w(dn2)],
                                  [F32, BF16], "res_norm2_bwd")
    dmixed = _matmul(dx1b, p['w_out'], 'nt', F32, "d_mixed")
    dw_out = _matmul(mixed, dx1b, 'tn', BF16, "dw_out")
    dga, dgb, dya, dyb = _rw_backward(_merge_fn, gates + [full_w(ya), full_w(yb)], [], [full_w(dmixed)],
                                      [BF16, BF16, BF16, BF16], "merge_bwd")
    d_ya_pre = _matmul(dya, p['rproj'], 'nt', F32, "d_rwkv_out")
    dw_rproj = _matmul(ya_pre, dya, 'tn', F32, "dw_rwkv_proj")
    d_yb_pre = _matmul(dyb, p['gproj'], 'nt', F32, "d_gdn_out")
    dw_gproj = _matmul(yb_pre, dyb, 'tn', F32, "dw_gdn_proj")
    dpr, dpk, dpv, dplo, dpp, dml, dwl = _rwkv_backward(p_cat, p['ppack'], p['mulo'], p['wl'], st_r, d_ya_pre)
    dq, dk, dv, dz, dab, dcw_g, dgp = _gdn_backward(p_cat, p['cwpack'], p['gpar'], st_g, d_yb_pre)
    t = x.shape[0]
    dp_cat = jnp.concatenate([dga, dgb, dpr, dpk, dpv, dq, dk, dv, dz, dplo.astype(BF16), dab.astype(BF16),
                              jnp.zeros((t, LANES), BF16)], axis=1)
    dw_cat = _matmul(dp_cat, u, 'tn', BF16, "dw_in")
    dw_in = jnp.concatenate([dw_cat[OFF_RKV:OFF_QKV].reshape(3 * RWKV_HEADS, LANES, d)[:, :RWKV_HEAD_DIM].reshape(-1, d),
                             dw_cat[OFF_LO:OFF_AB], dw_cat[OFF_QKV:OFF_Z], dw_cat[OFF_Z:OFF_LO], dw_cat[OFF_AB:OFF_AB + 8],
                             dw_cat[OFF_GA:OFF_GB], dw_cat[OFF_GB:OFF_RKV]], axis=0)
    token = push_grads({'w_out': dw_out, 'w_in': dw_in})
    du = _matmul(dp_cat, p['w_cat'], 'nn', F32, "d_norm1")
    grad_x, dg1 = _rw_backward(_norm_skip_fn, [full_w(x)], [p['g1'] + token], [full_w(du), full_w(dx1)], [F32], "norm1_bwd")

    heads = lambda row: dpp[:, row, :RWKV_HEAD_DIM].reshape(-1)
    lora = lambda j, lo_, hi_: jnp.transpose(dwl[:, j, lo_:hi_, :RWKV_HEAD_DIM], (1, 0, 2)).reshape(hi_ - lo_, RWKV_WIDTH)
    grads = {
        'norm1_g': dg1[0],
        'w_in': dw_in,
        'rwkv_mu': jnp.concatenate([heads(0), heads(1), heads(2), jnp.sum(dml[:, 0, :], axis=0)]),
        'rwkv_w0': heads(3), 'rwkv_a0': heads(4), 'rwkv_k_k': heads(5), 'rwkv_k_a': heads(6),
        'rwkv_ln_w': heads(7), 'rwkv_ln_b': heads(8), 'rwkv_r_k': heads(9).reshape(RWKV_HEADS, RWKV_HEAD_DIM),
        'rwkv_w2': lora(0, 0, 64), 'rwkv_a2': lora(1, 64, 128), 'rwkv_g2': lora(2, 128, 256),
        'rwkv_proj': dw_rproj.reshape(RWKV_HEADS, LANES, -1)[:, :RWKV_HEAD_DIM].reshape(RWKV_WIDTH, -1),
        'gdn_conv_w': jnp.transpose(dcw_g[:, :, :GDN_CONV, :], (2, 1, 0, 3)).reshape(GDN_CONV, 3 * GDN_WIDTH),
        'gdn_a_log': jnp.sum(dgp[:, 0, :GDN_HEADS], axis=0), 'gdn_dt_bias': jnp.sum(dgp[:, 1, :GDN_HEADS], axis=0),
        'gdn_norm_w': jnp.sum(dgp[:, 2, :], axis=0),
        'gdn_proj': dw_gproj, 'w_out': dw_out, 'norm2_g': dg2[0], 'ffn_up': dw_up, 'ffn_conv_w': dcw_f,
        'ffn_down': dw_down, 'final_g': dgf[0],
    }
    return loss_vec, grad_x, grads


def kernel(x, norm1_g, w_in, rwkv_mu, rwkv_w0, rwkv_w2, rwkv_a0, rwkv_a2, rwkv_g2, rwkv_k_k, rwkv_k_a, rwkv_r_k, rwkv_ln_w, rwkv_ln_b, rwkv_proj, gdn_conv_w, gdn_a_log, gdn_dt_bias, gdn_norm_w, gdn_proj, w_out, norm2_g, ffn_up, ffn_conv_w, ffn_down, final_g, loss_target, m_norm1_g, m_w_in, m_rwkv_mu, m_rwkv_w0, m_rwkv_w2, m_rwkv_a0, m_rwkv_a2, m_rwkv_g2, m_rwkv_k_k, m_rwkv_k_a, m_rwkv_r_k, m_rwkv_ln_w, m_rwkv_ln_b, m_rwkv_proj, m_gdn_conv_w, m_gdn_a_log, m_gdn_dt_bias, m_gdn_norm_w, m_gdn_proj, m_w_out, m_norm2_g, m_ffn_up, m_ffn_conv_w, m_ffn_down, m_final_g, v_norm1_g, v_w_in, v_rwkv_mu, v_rwkv_w0, v_rwkv_w2, v_rwkv_a0, v_rwkv_a2, v_rwkv_g2, v_rwkv_k_k, v_rwkv_k_a, v_rwkv_r_k, v_rwkv_ln_w, v_rwkv_ln_b, v_rwkv_proj, v_gdn_conv_w, v_gdn_a_log, v_gdn_dt_bias, v_gdn_norm_w, v_gdn_proj, v_w_out, v_norm2_g, v_ffn_up, v_ffn_conv_w, v_ffn_down, v_final_g):
    given = dict(zip(WEIGHT_NAMES, (norm1_g, w_in, rwkv_mu, rwkv_w0, rwkv_w2, rwkv_a0, rwkv_a2, rwkv_g2, rwkv_k_k, rwkv_k_a, rwkv_r_k,
                                    rwkv_ln_w, rwkv_ln_b, rwkv_proj, gdn_conv_w, gdn_a_log, gdn_dt_bias, gdn_norm_w, gdn_proj, w_out,
                                    norm2_g, ffn_up, ffn_conv_w, ffn_down, final_g)))
    mom1 = dict(zip(WEIGHT_NAMES, (m_norm1_g, m_w_in, m_rwkv_mu, m_rwkv_w0, m_rwkv_w2, m_rwkv_a0, m_rwkv_a2, m_rwkv_g2, m_rwkv_k_k,
                                   m_rwkv_k_a, m_rwkv_r_k, m_rwkv_ln_w, m_rwkv_ln_b, m_rwkv_proj, m_gdn_conv_w, m_gdn_a_log,
                                   m_gdn_dt_bias, m_gdn_norm_w, m_gdn_proj, m_w_out, m_norm2_g, m_ffn_up, m_ffn_conv_w, m_ffn_down,
                                   m_final_g)))
    mom2 = dict(zip(WEIGHT_NAMES, (v_norm1_g, v_w_in, v_rwkv_mu, v_rwkv_w0, v_rwkv_w2, v_rwkv_a0, v_rwkv_a2, v_rwkv_g2, v_rwkv_k_k,
                                   v_rwkv_k_a, v_rwkv_r_k, v_rwkv_ln_w, v_rwkv_ln_b, v_rwkv_proj, v_gdn_conv_w, v_gdn_a_log,
                                   v_gdn_dt_bias, v_gdn_norm_w, v_gdn_proj, v_w_out, v_norm2_g, v_ffn_up, v_ffn_conv_w, v_ffn_down,
                                   v_final_g)))
    def strip(n, a):
        a = a if n == 'final_g' else a.reshape(a.shape[1:])
        return a.T if n in TRANSPOSED else a

    local = {n: strip(n, a) for n, a in given.items()}
    shard_shapes = {n: local[n].shape for n in SHARD_AXIS}
    sharded = BIG_SHARDED + SMALL_SHARDED

    late_names = [n for n in BIG_SHARDED if n != 'w_in']
    g_in = _all_gather_two_level(_row_pack([local['w_in']], BF16, 16), "gather_w_in")
    g_small = _all_gather_two_level(_flat_pack([local[n] for n in SMALL_SHARDED], F32, SUBLANES), "gather_small")
    late_pack, g_in, g_small = lax.optimization_barrier((_row_pack([local[n] for n in late_names], BF16, 16), g_in, g_small))
    send_sems, recv_sems, late_pack, landing, token = _gather_start(late_pack, "gather_late_start")
    full = _unpack_row_gathered(g_in, ['w_in'], shard_shapes)
    full.update(_unpack_gathered(g_small, SMALL_SHARDED, shard_shapes))
    rep = {n: local[n] for n in REPLICATED}
    rep['norm1_g'] = rep['norm1_g'] + token[0, 0]

    def late_weights(after):
        got = _gather_wait(send_sems, recv_sems, late_pack, landing, after, "gather_late_wait")
        me = 4 * lax.axis_index("x") + 2 * lax.axis_index("y") + lax.axis_index("c")
        slot = lax.broadcasted_iota(jnp.int32, (N_DEV, 1, 1), 0)
        got = jnp.where(slot == me, late_pack[None], got)
        return _prepare_late_weights(_unpack_row_gathered(got, late_names, shard_shapes))

    pushes = []
    me = 4 * lax.axis_index("x") + 2 * lax.axis_index("y") + lax.axis_index("c")
    slot = lax.broadcasted_iota(jnp.int32, (N_DEV, 1, 1), 0)

    def push_grads(group):
        names = list(group)
        slabs = [group[n].reshape(N_DEV, -1, group[n].shape[1]) for n in names]
        send_sems, recv_sems, slabs, landings, token = _slab_push_start(slabs, "grad_push_start_" + "_".join(names))
        pushes.append((names, send_sems, recv_sems, slabs, landings))
        return token[0, 0]

    loss_vec, grad_x, grads = _local_step(x[0], loss_target[0], _prepare_weights(full, rep), late_weights, push_grads)

    landed = {}
    for names, send_sems, recv_sems, slabs, landings in pushes:
        got = _slab_push_wait(send_sems, recv_sems, slabs, landings, (grad_x,), "grad_push_wait_" + "_".join(names))
        for n, slab, land in zip(names, slabs, got):
            landed[n] = (land, slab)

    small_sharded = ['rwkv_proj', 'gdn_proj'] + SMALL_SHARDED
    small_names = small_sharded + REPLICATED
    rep_vec = jnp.concatenate([grads[n].reshape(-1) for n in REPLICATED] + [loss_vec[0, 0:1]])
    slab_small = jnp.concatenate([_shard_major(grads[n], SHARD_AXIS[n]) for n in small_sharded] +
                                 [jnp.broadcast_to(rep_vec[None], (N_DEV, rep_vec.shape[0]))], axis=1)
    small_rows = -(-slab_small.shape[1] // (PACK_W * SUBLANES)) * SUBLANES
    slab_small = jnp.pad(slab_small, ((0, 0), (0, small_rows * PACK_W - slab_small.shape[1]))).reshape(N_DEV, small_rows, PACK_W)
    core = lax.axis_index("c").astype(jnp.int32).reshape(1)
    (from_sibling,) = _pair_exchange([slab_small], "grad_pair_exchange")
    chip_small = _pair_sum(slab_small, from_sibling, core, F32, "grad_pair_sum_small")
    (parts_small,) = _chip_exchange([chip_small], "grad_chip_exchange")

    def pack_local(src):
        flat = jnp.concatenate([strip(n, src[n]).reshape(-1) for n in small_names])
        return jnp.pad(flat, (0, small_rows * PACK_W - flat.shape[0])).reshape(small_rows, PACK_W)

    results = [({}, None) for _ in range(4)]
    me_arr = me.astype(jnp.int32).reshape(1)
    for n in ROW_SHARDED:
        packs = _reduce_landed_adamw(*landed[n], me_arr, local[n], strip(n, mom1[n]), strip(n, mom2[n]), "adamw_" + n)
        for (out, _), pk in zip(results, packs):
            out[n] = (pk.T if n in TRANSPOSED else pk).reshape(given[n].shape)
    packs = _reduce_adamw(parts_small, pack_local(given), pack_local(mom1), pack_local(mom2), "adamw_small")
    for i, pk in enumerate(packs):
        flat, off = pk.reshape(-1), 0
        for n in small_names:
            size = int(np.prod(given[n].shape))
            results[i][0][n] = flat[off:off + size].reshape(given[n].shape)
            off += size
        results[i] = (results[i][0], flat[off])
    (g_out, loss), (d_out, _), (m_out, _), (v_out, _) = results
    return (loss, grad_x[None], *[g_out[n] for n in WEIGHT_NAMES], *[d_out[n] for n in WEIGHT_NAMES],
            *[m_out[n] for n in WEIGHT_NAMES], *[v_out[n] for n in WEIGHT_NAMES])
```

```python
import functools

import jax
import jax.numpy as jnp
import numpy as np
from jax import lax
from jax.experimental import pallas as pl
from jax.experimental.pallas import tpu as pltpu

F32 = jnp.float32
BF16 = jnp.bfloat16

N_DEV = 8
D_MODEL = 1024
CHUNK = 64
RWKV_HEADS = 8
RWKV_HEAD_DIM = 64
RWKV_WIDTH = 512
GDN_HEADS = 4
GDN_HEAD_DIM = 128
GDN_WIDTH = 512
GDN_CONV = 4
FFN_HIDDEN = 2816
FFN_CONV = 3
NORM_EPS = 1e-6
L2_EPS = 1e-6
RWKV_GN_EPS = 64e-5
LANES = 128
SUBLANES = 8
VMEM_LIMIT = 56 * 1024 * 1024

ADAM_LR = 0.001
ADAM_B1 = 0.9
ADAM_B2 = 0.999
ADAM_EPS = 1e-08
ADAM_WD = 0.01
ADAM_STEP = 10

OFF_GA, OFF_GB, OFF_RKV, OFF_QKV, OFF_Z, OFF_LO, OFF_AB, CAT_W = 0, 1024, 2048, 5120, 6656, 7168, 7424, 7680
RWKV_HB = 8
RWKV_STEP_CHUNKS = 2
RWKV_TILE = RWKV_STEP_CHUNKS * CHUNK
GDN_HB = 4
GDN_STEP_CHUNKS = 4
GDN_TILE = GDN_STEP_CHUNKS * CHUNK

WEIGHT_NAMES = ['norm1_g', 'w_in', 'rwkv_mu', 'rwkv_w0', 'rwkv_w2', 'rwkv_a0', 'rwkv_a2', 'rwkv_g2', 'rwkv_k_k', 'rwkv_k_a',
                'rwkv_r_k', 'rwkv_ln_w', 'rwkv_ln_b', 'rwkv_proj', 'gdn_conv_w', 'gdn_a_log', 'gdn_dt_bias', 'gdn_norm_w',
                'gdn_proj', 'w_out', 'norm2_g', 'ffn_up', 'ffn_conv_w', 'ffn_down', 'final_g']
BIG_SHARDED = ['w_in', 'ffn_up', 'ffn_down', 'w_out', 'rwkv_proj', 'gdn_proj']
SMALL_SHARDED = ['rwkv_w2', 'rwkv_a2', 'rwkv_g2', 'gdn_conv_w', 'ffn_conv_w']
TRANSPOSED = ('w_in', 'ffn_up')
SHARD_AXIS = {'w_in': 0, 'ffn_up': 0, 'ffn_down': 0, 'w_out': 0, 'rwkv_proj': 1, 'gdn_proj': 1,
              'rwkv_w2': 1, 'rwkv_a2': 1, 'rwkv_g2': 1, 'gdn_conv_w': 1, 'ffn_conv_w': 1}
REPLICATED = [n for n in WEIGHT_NAMES if n not in SHARD_AXIS]
ROW_SHARDED = ['w_in', 'ffn_up', 'ffn_down', 'w_out']


def _cparams(sem=None):
    kw = dict(vmem_limit_bytes=VMEM_LIMIT)
    if sem is not None:
        kw['dimension_semantics'] = sem
    return pltpu.CompilerParams(**kw)


_NN, _NT, _TN = 'nn', 'nt', 'tn'
_DIMS_2D = {'nn': (((1,), (0,)), ((), ())), 'nt': (((1,), (1,)), ((), ())), 'tn': (((0,), (0,)), ((), ()))}
_DIMS_3D = {'nn': (((2,), (1,)), ((0,), (0,))), 'nt': (((2,), (2,)), ((0,), (0,))), 'tn': (((1,), (1,)), ((0,), (0,)))}


def _dg(a, b, kind):
    return lax.dot_general(a, b, (_DIMS_2D if a.ndim == 2 else _DIMS_3D)[kind], preferred_element_type=F32)


def _dot1(a, b, kind):
    return _dg(a.astype(BF16), b.astype(BF16), kind)


@jax.custom_vjp
def _dhi(a, b):
    return _dot1(a, b, _NN)


_dhi.defvjp(lambda a, b: (_dot1(a, b, _NN), (a, b)),
            lambda res, ct: (_dot1(ct, res[1], _NT), _dot1(res[0], ct, _TN)))


@jax.custom_vjp
def _dnt(a, b):
    return _dot1(a, b, _NT)


_dnt.defvjp(lambda a, b: (_dot1(a, b, _NT), (a, b)),
            lambda res, ct: (_dot1(ct, res[1], _NN), _dot1(ct, res[0], _TN)))


@jax.custom_vjp
def _dtn(a, b):
    return _dot1(a, b, _TN)


_dtn.defvjp(lambda a, b: (_dot1(a, b, _TN), (a, b)),
            lambda res, ct: (_dot1(res[1], ct, _NT), _dot1(res[0], ct, _NN)))


def _split3(x):
    x1 = x.astype(BF16)
    r1 = x - x1.astype(F32)
    x2 = r1.astype(BF16)
    return x1, x2, (r1 - x2.astype(F32)).astype(BF16)


def _dot_exact_lhs(sel, x, kind):
    parts = [_dg(sel, xi, kind) for xi in _split3(x)]
    return parts[0] + parts[1] + parts[2]


def _tril_ones(like):
    c = like.shape[-2]
    ri, ci = _iotas(c)
    return jnp.broadcast_to((ri >= ci).astype(BF16), like.shape[:-2] + (c, c))


@jax.custom_vjp
def _cumsum_rows(x):
    return _dot_exact_lhs(_tril_ones(x), x, _NN)


_cumsum_rows.defvjp(lambda x: (_dot_exact_lhs(_tril_ones(x), x, _NN), None),
                    lambda _, ct: (_dot_exact_lhs(_tril_ones(ct), ct, _TN),))


@jax.custom_vjp
def _lane_sum_as_row(x):
    return _dot_exact_lhs(jnp.ones(x.shape, BF16), x, _NT)


def _lane_sum_as_row_bwd(_, ct):
    ones = jnp.ones(ct.shape[:-1] + (LANES,), BF16)
    parts = [_dg(ci, ones, _TN) for ci in _split3(ct)]
    return (parts[0] + parts[1] + parts[2],)


_lane_sum_as_row.defvjp(lambda x: (_dot_exact_lhs(jnp.ones(x.shape, BF16), x, _NT), None), _lane_sum_as_row_bwd)


def _shift_rows(x, halo, s):
    rows = lax.broadcasted_iota(jnp.int32, x.shape, 0)
    out = pltpu.roll(x, s, 0)
    for i in range(s):
        out = jnp.where(rows == i, halo[SUBLANES - s + i:SUBLANES - s + i + 1, :], out)
    return out


def _unshift_rows(g, carry, s):
    c = g.shape[0]
    rows = lax.broadcasted_iota(jnp.int32, g.shape, 0)
    out = pltpu.roll(g, c - s, 0)
    for i in range(s):
        out = jnp.where(rows == c - s + i, carry[i:i + 1, :], out)
    return out


def _sigmoid_plain(z):
    return 1.0 / (1.0 + jnp.exp(-z))


def _sigmoid_value(z):
    t = jnp.exp(-jnp.abs(z))
    r = 1.0 / (1.0 + t)
    return jnp.where(z >= 0, r, t * r)


@jax.custom_vjp
def _sigmoid(z):
    return _sigmoid_value(z)


def _sigmoid_fwd(z):
    s = _sigmoid_value(z)
    return s, s


_sigmoid.defvjp(_sigmoid_fwd, lambda s, ct: (ct * s * (1.0 - s),))


def _silu(z):
    return z * _sigmoid(z)


def _softplus(z):
    return jnp.maximum(z, 0.0) + jnp.log(1.0 + jnp.exp(-jnp.abs(z)))


def _rms(t, gain):
    return t * lax.rsqrt(jnp.mean(t * t, axis=-1, keepdims=True) + NORM_EPS) * gain


def _iotas(c):
    return lax.broadcasted_iota(jnp.int32, (c, c), 0), lax.broadcasted_iota(jnp.int32, (c, c), 1)


def _unit_lower_inverse(xm, eye):
    t = eye + xm
    p = xm
    for _ in range(5):
        p = _dhi(p, p)
        t = t + _dhi(t, p)
    return t


def _rwkv_head(pr, pk, pv, plo, qr, qk, qv, qlo, s0, pp, mulo, wl):
    c = pr.shape[1]
    n_heads = s0.shape[0]
    n_chunks = pr.shape[0] // n_heads
    ri, ci = _iotas(c)
    if n_chunks > 1:
        pp = jnp.concatenate([pp] * n_chunks, axis=0)
        wl = jnp.concatenate([wl] * n_chunks, axis=0)

    def mix(p, q, mu):
        return p + (q - p) * mu

    r = mix(pr, qr, pp[:, 0:1])
    k = mix(pk, qk, pp[:, 1:2])
    v = mix(pv, qv, pp[:, 2:3])
    lo = mix(plo, qlo, mulo)
    w0, a0, k_k, k_a, ln_w, ln_b, r_k = (pp[:, i:i + 1] for i in range(3, 10))

    def per_head(t):
        return jnp.concatenate([jnp.broadcast_to(t[i], (n_heads,) + t.shape[1:]) for i in range(n_chunks)], axis=0)

    zw = _dhi(per_head(jnp.tanh(lo)), wl[:, 0])
    za = _dhi(per_head(lo), wl[:, 1])
    g = _dhi(per_head(_sigmoid(lo)), wl[:, 2])
    w_log = -_softplus(-(w0 + zw)) - 0.5
    lw = -jnp.exp(w_log)
    a = _sigmoid(a0 + za)
    kk = k * k_k
    kk = kk * lax.rsqrt(jnp.sum(kk * kk, axis=-1, keepdims=True) + L2_EPS)
    k2 = k * (1.0 + (a - 1.0) * k_a)
    an = -kk
    b = kk * a
    causal = ri >= ci
    strict = ri > ci
    eye = (ri == ci).astype(F32)
    cl = _cumsum_rows(lw)
    ecl = jnp.exp(-cl)
    at = an * jnp.exp(cl - lw)
    bt = b * ecl
    kt = k2 * ecl
    rt = r * jnp.exp(cl)
    a_ab = jnp.where(strict, _dnt(at, bt), 0.0)
    a_ak = jnp.where(strict, _dnt(at, kt), 0.0)
    tinv = _unit_lower_inverse(a_ab, eye)
    akv = _dhi(a_ak, v)
    r_b = jnp.where(causal, _dnt(rt, bt), 0.0)
    rkv = _dhi(jnp.where(causal, _dnt(rt, kt), 0.0), v)
    cl_end = jnp.sum(lw, axis=1, keepdims=True)
    dec_end = jnp.exp(cl_end - cl)
    b_end = b * dec_end
    sv = _dtn(v, k2 * dec_end)
    e_end = jnp.exp(cl_end)
    state, ys = s0, []
    for i in range(n_chunks):
        sl = slice(i * n_heads, (i + 1) * n_heads)
        u = _dhi(tinv[sl], _dnt(at[sl], state) + akv[sl])
        ys.append(_dnt(rt[sl], state) + _dhi(r_b[sl], u) + rkv[sl])
        state = state * e_end[sl] + _dtn(u, b_end[sl]) + sv[sl]
    y = jnp.concatenate(ys, axis=0) if n_chunks > 1 else ys[0]
    s1 = state
    m = (lax.broadcasted_iota(jnp.int32, (1, LANES), 1) < RWKV_HEAD_DIM).astype(F32)
    mean = jnp.sum(y, axis=-1, keepdims=True) * (1.0 / RWKV_HEAD_DIM)
    yc = (y - mean) * m
    var = jnp.sum(yc * yc, axis=-1, keepdims=True) * (1.0 / RWKV_HEAD_DIM)
    yn = yc * lax.rsqrt(var + RWKV_GN_EPS) * ln_w + ln_b
    y2 = yn + jnp.sum(r * k2 * r_k, axis=-1, keepdims=True) * v
    return y2 * g, s1


def _gdn_head(xq, xk, xv, z, ab, s0, cw, gp, oha, ohb):
    c = z.shape[1]
    n_heads = s0.shape[0]
    n_chunks = z.shape[0] // n_heads
    ri, ci = _iotas(c)
    cw = jnp.concatenate([cw] * n_chunks, axis=0) if n_chunks > 1 else cw

    def conv(xs, w):
        out = xs[0] * w[:, GDN_CONV - 1:GDN_CONV]
        for s in range(1, GDN_CONV):
            out = out + xs[s] * w[:, GDN_CONV - 1 - s:GDN_CONV - s]
        return out

    q = _silu(conv(xq, cw[:, 0]))
    k = _silu(conv(xk, cw[:, 1]))
    v = _silu(conv(xv, cw[:, 2]))
    q = q * lax.rsqrt(jnp.sum(q * q, axis=-1, keepdims=True) + L2_EPS) * (GDN_HEAD_DIM ** -0.5)
    k = k * lax.rsqrt(jnp.sum(k * k, axis=-1, keepdims=True) + L2_EPS)
    gg = -jnp.exp(gp[0:1]) * _softplus(ab + gp[1:2])
    beta = jnp.sum(_sigmoid(ab) * ohb, axis=-1, keepdims=True)
    causal = ri >= ci
    strict = ri > ci
    eye = (ri == ci).astype(F32)
    gcm = _cumsum_rows(gg * oha)
    gc = jnp.sum(gcm, axis=-1, keepdims=True)
    gc_row = _lane_sum_as_row(gcm)
    dec = jnp.where(causal, jnp.exp(jnp.where(causal, gc - gc_row, 0.0)), 0.0)
    kb = k * beta
    vb = v * beta
    lm = jnp.where(strict, _dnt(kb, k) * dec, 0.0)
    tinv = _unit_lower_inverse(-lm, eye)
    egc = jnp.exp(gc)
    u = _dhi(tinv, vb)
    wk = _dhi(tinv, kb * egc)
    attn = jnp.where(causal, _dnt(q, k) * dec, 0.0)
    g_last = gc[:, c - 1:c, :]
    q_dec = q * egc
    k_dec = k * jnp.exp(g_last - gc)
    e_last = jnp.exp(g_last)
    state, outs = s0, []
    for i in range(n_chunks):
        sl = slice(i * n_heads, (i + 1) * n_heads)
        v_new = u[sl] - _dhi(wk[sl], state)
        outs.append(_dhi(q_dec[sl], state) + _dhi(attn[sl], v_new))
        state = state * e_last[sl] + _dtn(k_dec[sl], v_new)
    o = jnp.concatenate(outs, axis=0) if n_chunks > 1 else outs[0]
    return _rms(o, gp[2:3]) * _silu(z), state


def _head_id(grp, i, per_step, heads):
    return i if per_step == heads else grp * per_step + i


def _head_range(grp, per_step, heads):
    return slice(None) if per_step == heads else pl.ds(grp * per_step, per_step)


def _rwkv_specs(nmap):
    hb, groups = RWKV_HB, RWKV_HEADS // RWKV_HB
    cb = OFF_RKV // (hb * LANES)
    specs = []
    for j in range(3):
        specs.append(pl.BlockSpec((RWKV_TILE, hb * LANES), lambda n, g, j=j: (nmap(n), cb + j * groups + g)))
    specs.append(pl.BlockSpec((RWKV_TILE, 2 * LANES), lambda n, g: (nmap(n), OFF_LO // (2 * LANES))))
    per = RWKV_TILE // SUBLANES
    for j in range(3):
        specs.append(pl.BlockSpec((SUBLANES, hb * LANES),
                                  lambda n, g, j=j: (jnp.maximum(nmap(n) * per - 1, 0), cb + j * groups + g)))
    specs.append(pl.BlockSpec((SUBLANES, 2 * LANES), lambda n, g: (jnp.maximum(nmap(n) * per - 1, 0), OFF_LO // (2 * LANES))))
    specs.append(pl.BlockSpec((hb, 16, LANES), lambda n, g: (g, 0, 0)))
    specs.append(pl.BlockSpec((1, 2 * LANES), lambda n, g: (0, 0)))
    specs.append(pl.BlockSpec((hb, 3, 2 * LANES, LANES), lambda n, g: (g, 0, 0, 0)))
    return specs


def _rwkv_operands(refs, halos, live):
    pr, pk, pv, plo = refs
    hr, hk, hv, hlo = halos
    cur, prev = [], []
    for x, hx in ((pr, hr), (pk, hk), (pv, hv)):
        tiles = [_lane_block(x, h) for h in range(RWKV_HB)]
        cur.append(_chunk_batch(tiles))
        prev.append(_chunk_batch([_shift_rows(t_, _lane_block(hx, h) * live, 1) for h, t_ in enumerate(tiles)]))
    lo = plo[...]
    cur.append(_chunk_batch([lo]))
    prev.append(_chunk_batch([_shift_rows(lo, hlo[...] * live, 1)]))
    return cur, prev


def _rwkv_forward(p_cat, ppack, mulo, wl):
    t = p_cat.shape[0]
    n_chunks = t // RWKV_TILE

    def body(pr, pk, pv, plo, hr, hk, hv, hlo, pp, ml, w, out, st_out, s_scr):
        n, grp = pl.program_id(0), pl.program_id(1)

        hsl = _head_range(grp, RWKV_HB, RWKV_HEADS)

        @pl.when(n == 0)
        def _():
            s_scr[hsl] = jnp.zeros((RWKV_HB, LANES, LANES), F32)

        live = (n > 0).astype(F32)
        cur, prev = _rwkv_operands((pr, pk, pv, plo), (hr, hk, hv, hlo), live)
        s0 = s_scr[hsl]
        st_out[...] = s0
        o, s1 = _rwkv_head(*cur, *prev, s0, pp[...], ml[...], w[...])
        for h, tile in enumerate(_head_tiles(o, RWKV_HB)):
            out[:, h * LANES:(h + 1) * LANES] = tile.astype(out.dtype)
        s_scr[hsl] = s1

    return pl.pallas_call(
        body, name="rwkv_fwd", grid=(n_chunks, RWKV_HEADS // RWKV_HB),
        in_specs=_rwkv_specs(lambda n: n),
        out_specs=(pl.BlockSpec((RWKV_TILE, RWKV_HB * LANES), lambda n, g: (n, g)),
                   pl.BlockSpec((None, RWKV_HB, LANES, LANES), lambda n, g: (n, g, 0, 0))),
        out_shape=(jax.ShapeDtypeStruct((t, RWKV_HEADS * LANES), BF16),
                   jax.ShapeDtypeStruct((n_chunks, RWKV_HEADS, LANES, LANES), F32)),
        scratch_shapes=[pltpu.VMEM((RWKV_HEADS, LANES, LANES), F32)],
        compiler_params=_cparams(("arbitrary", "arbitrary")),
    )(p_cat, p_cat, p_cat, p_cat, p_cat, p_cat, p_cat, p_cat, ppack, mulo, wl)


def _rwkv_backward(p_cat, ppack, mulo, wl, states, d_out):
    t = p_cat.shape[0]
    n_chunks = t // RWKV_TILE
    last = n_chunks - 1

    def body(pr, pk, pv, plo, hr, hk, hv, hlo, pp, ml, w, st, dy, dpr, dpk, dpv, dplo, dpp, dml, dw, ds_scr, car_scr, carlo_scr):
        n, grp = pl.program_id(0), pl.program_id(1)

        hsl = _head_range(grp, RWKV_HB, RWKV_HEADS)
        gi = _head_id(grp, 0, 1, RWKV_HEADS // RWKV_HB)

        @pl.when(n == 0)
        def _():
            ds_scr[hsl] = jnp.zeros((RWKV_HB, LANES, LANES), F32)
            car_scr[hsl] = jnp.zeros((RWKV_HB, 3 * SUBLANES, LANES), F32)
            carlo_scr[gi] = jnp.zeros((SUBLANES, 2 * LANES), F32)

        @pl.when((n == 0) & (grp == 0))
        def _():
            dpp[...] = jnp.zeros(dpp.shape, F32)
            dml[...] = jnp.zeros(dml.shape, F32)
            dw[...] = jnp.zeros(dw.shape, F32)

        live = (n < last).astype(F32)
        cur, prev = _rwkv_operands((pr, pk, pv, plo), (hr, hk, hv, hlo), live)
        _, vjp = jax.vjp(_rwkv_head, *cur, *prev, st[...], pp[...], ml[...], w[...])
        g = vjp((_chunk_batch([_lane_block(dy, h) for h in range(RWKV_HB)]), ds_scr[hsl]))
        outs = (dpr, dpk, dpv)
        d_cur = [_head_tiles(g[j], RWKV_HB) for j in range(3)]
        d_prev = [_head_tiles(g[4 + j], RWKV_HB) for j in range(3)]
        for i in range(RWKV_HB):
            sl = slice(i * LANES, (i + 1) * LANES)
            h = _head_id(grp, i, RWKV_HB, RWKV_HEADS)
            car = car_scr[h]
            for j in range(3):
                tot = d_cur[j][i] + _unshift_rows(d_prev[j][i], car[SUBLANES * j:SUBLANES * (j + 1), :], 1)
                outs[j][:, sl] = tot.astype(outs[j].dtype)
                car_scr[h, SUBLANES * j:SUBLANES * (j + 1), :] = d_prev[j][i][0:SUBLANES, :]
        (dlo_cur,), (dlo_prev,) = _head_tiles(g[3], 1), _head_tiles(g[7], 1)
        dlo = dlo_cur + _unshift_rows(dlo_prev, carlo_scr[gi], 1)
        carlo_scr[gi] = dlo_prev[0:SUBLANES, :]
        ds_scr[hsl] = g[8]
        dpp[hsl] += g[9]
        dml[0, 0:1, :] += g[10]
        dw[hsl] += g[11]

        @pl.when(grp == 0)
        def _():
            dplo[...] = dlo

        @pl.when(grp > 0)
        def _():
            dplo[...] += dlo

    rev = lambda n: last - n
    in_specs = _rwkv_specs(rev) + [
        pl.BlockSpec((None, RWKV_HB, LANES, LANES), lambda n, g: (rev(n), g, 0, 0)),
        pl.BlockSpec((RWKV_TILE, RWKV_HB * LANES), lambda n, g: (rev(n), g)),
    ]
    hw = RWKV_HEADS * LANES
    return pl.pallas_call(
        body, name="rwkv_bwd", grid=(n_chunks, RWKV_HEADS // RWKV_HB),
        in_specs=in_specs,
        out_specs=(pl.BlockSpec((RWKV_TILE, RWKV_HB * LANES), lambda n, g: (rev(n), g)),
                   pl.BlockSpec((RWKV_TILE, RWKV_HB * LANES), lambda n, g: (rev(n), g)),
                   pl.BlockSpec((RWKV_TILE, RWKV_HB * LANES), lambda n, g: (rev(n), g)),
                   pl.BlockSpec((RWKV_TILE, 2 * LANES), lambda n, h: (rev(n), 0)),
                   pl.BlockSpec((RWKV_HEADS, 16, LANES), lambda n, h: (0, 0, 0)),
                   pl.BlockSpec((RWKV_HEADS, SUBLANES, 2 * LANES), lambda n, h: (0, 0, 0)),
                   pl.BlockSpec((RWKV_HEADS, 3, 2 * LANES, LANES), lambda n, h: (0, 0, 0, 0))),
        out_shape=(jax.ShapeDtypeStruct((t, hw), BF16), jax.ShapeDtypeStruct((t, hw), BF16), jax.ShapeDtypeStruct((t, hw), BF16),
                   jax.ShapeDtypeStruct((t, 2 * LANES), F32),
                   jax.ShapeDtypeStruct((RWKV_HEADS, 16, LANES), F32),
                   jax.ShapeDtypeStruct((RWKV_HEADS, SUBLANES, 2 * LANES), F32),
                   jax.ShapeDtypeStruct((RWKV_HEADS, 3, 2 * LANES, LANES), F32)),
        scratch_shapes=[pltpu.VMEM((RWKV_HEADS, LANES, LANES), F32),
                        pltpu.VMEM((RWKV_HEADS, 3 * SUBLANES, LANES), F32),
                        pltpu.VMEM((RWKV_HEADS, SUBLANES, 2 * LANES), F32)],
        compiler_params=_cparams(("arbitrary", "arbitrary")),
    )(p_cat, p_cat, p_cat, p_cat, p_cat, p_cat, p_cat, p_cat, ppack, mulo, wl, states, d_out)


def _gdn_specs(nmap):
    per = GDN_TILE // SUBLANES
    hb, groups = GDN_HB, GDN_HEADS // GDN_HB
    cb = OFF_QKV // (hb * LANES)
    specs = []
    for j in range(3):
        specs.append(pl.BlockSpec((GDN_TILE, hb * LANES), lambda n, g, j=j: (nmap(n), cb + j * groups + g)))
    for j in range(3):
        specs.append(pl.BlockSpec((SUBLANES, hb * LANES),
                                  lambda n, g, j=j: (jnp.maximum(nmap(n) * per - 1, 0), cb + j * groups + g)))
    specs.append(pl.BlockSpec((GDN_TILE, hb * LANES), lambda n, g: (nmap(n), OFF_Z // (hb * LANES) + g)))
    specs.append(pl.BlockSpec((GDN_TILE, LANES), lambda n, g: (nmap(n), OFF_AB // LANES)))
    specs.append(pl.BlockSpec((hb, 3, SUBLANES, LANES), lambda n, g: (g, 0, 0, 0)))
    specs.append(pl.BlockSpec((SUBLANES, LANES), lambda n, g: (0, 0)))
    return specs


def _conv_taps(x, halo):
    return (x,) + tuple(_shift_rows(x, halo, s) for s in range(1, GDN_CONV))


def _onehots(grp):
    nb = GDN_STEP_CHUNKS * GDN_HB
    lane = lax.broadcasted_iota(jnp.int32, (nb, 1, LANES), 2)
    head = lax.broadcasted_iota(jnp.int32, (nb, 1, LANES), 0) % GDN_HB + _head_id(grp, 0, GDN_HB, GDN_HEADS)
    return (lane == head).astype(F32), (lane == GDN_HEADS + head).astype(F32)


def _chunk_batch(tiles):
    n_chunks = tiles[0].shape[0] // CHUNK
    return jnp.stack([t_[i * CHUNK:(i + 1) * CHUNK, :] for i in range(n_chunks) for t_ in tiles])


def _head_tiles(batch, n_heads=GDN_HB):
    n_chunks = batch.shape[0] // n_heads
    return [jnp.concatenate([batch[i * n_heads + h] for i in range(n_chunks)], axis=0) for h in range(n_heads)]


def _lane_block(ref, h):
    return ref[:, h * LANES:(h + 1) * LANES]


def _gdn_taps(refs, halos, live):
    out = []
    for x, hx in zip(refs, halos):
        per_head = [_conv_taps(_lane_block(x, h), _lane_block(hx, h) * live) for h in range(GDN_HB)]
        out.append(tuple(_chunk_batch([per_head[h][s] for h in range(GDN_HB)]) for s in range(GDN_CONV)))
    return out


def _gdn_forward(p_cat, cwpack, gpar):
    t = p_cat.shape[0]
    n_chunks = t // GDN_TILE

    def body(xq, xk, xv, hq, hk, hv, z, ab, cw, gp, out, st_out, s_scr):
        n, grp = pl.program_id(0), pl.program_id(1)

        hsl = _head_range(grp, GDN_HB, GDN_HEADS)

        @pl.when(n == 0)
        def _():
            s_scr[hsl] = jnp.zeros((GDN_HB, LANES, LANES), F32)

        live = (n > 0).astype(F32)
        oha, ohb = _onehots(grp)
        s0 = s_scr[hsl]
        st_out[...] = s0
        taps = _gdn_taps((xq, xk, xv), (hq, hk, hv), live)
        zb = _chunk_batch([_lane_block(z, h) for h in range(GDN_HB)])
        abb = _chunk_batch([ab[...]] * GDN_HB)
        o, s1 = _gdn_head(*taps, zb, abb, s0, cw[...], gp[...], oha, ohb)
        for h, tile in enumerate(_head_tiles(o)):
            out[:, h * LANES:(h + 1) * LANES] = tile.astype(out.dtype)
        s_scr[hsl] = s1

    return pl.pallas_call(
        body, name="gdn_fwd", grid=(n_chunks, GDN_HEADS // GDN_HB),
        in_specs=_gdn_specs(lambda n: n),
        out_specs=(pl.BlockSpec((GDN_TILE, GDN_HB * LANES), lambda n, g: (n, g)),
                   pl.BlockSpec((None, GDN_HB, LANES, LANES), lambda n, g: (n, g, 0, 0))),
        out_shape=(jax.ShapeDtypeStruct((t, GDN_WIDTH), BF16),
                   jax.ShapeDtypeStruct((n_chunks, GDN_HEADS, LANES, LANES), F32)),
        scratch_shapes=[pltpu.VMEM((GDN_HEADS, LANES, LANES), F32)],
        compiler_params=_cparams(("arbitrary", "arbitrary")),
    )(p_cat, p_cat, p_cat, p_cat, p_cat, p_cat, p_cat, p_cat, cwpack, gpar)


def _gdn_backward(p_cat, cwpack, gpar, states, d_out):
    t = p_cat.shape[0]
    n_chunks = t // GDN_TILE
    last = n_chunks - 1

    def body(xq, xk, xv, hq, hk, hv, z, ab, cw, gp, st, dy, dq, dk, dv, dz, dab, dcw, dgp, ds_scr, car_scr):
        n, grp = pl.program_id(0), pl.program_id(1)

        hsl = _head_range(grp, GDN_HB, GDN_HEADS)

        @pl.when(n == 0)
        def _():
            ds_scr[hsl] = jnp.zeros((GDN_HB, LANES, LANES), F32)
            car_scr[hsl] = jnp.zeros((GDN_HB, 3 * GDN_CONV, SUBLANES, LANES), F32)

        @pl.when((n == 0) & (grp == 0))
        def _():
            dcw[...] = jnp.zeros(dcw.shape, F32)
            dgp[...] = jnp.zeros(dgp.shape, F32)

        live = (n < last).astype(F32)
        oha, ohb = _onehots(grp)
        fn = functools.partial(_gdn_head, oha=oha, ohb=ohb)
        taps = _gdn_taps((xq, xk, xv), (hq, hk, hv), live)
        zb = _chunk_batch([_lane_block(z, h) for h in range(GDN_HB)])
        abb = _chunk_batch([ab[...]] * GDN_HB)
        _, vjp = jax.vjp(fn, *taps, zb, abb, st[...], cw[...], gp[...])
        g = vjp((_chunk_batch([_lane_block(dy, h) for h in range(GDN_HB)]), ds_scr[hsl]))
        outs = (dq, dk, dv)
        tap_tiles = [[_head_tiles(g[j][s]) for s in range(GDN_CONV)] for j in range(3)]
        dz_tiles = _head_tiles(g[3])
        for i in range(GDN_HB):
            sl = slice(i * LANES, (i + 1) * LANES)
            h = _head_id(grp, i, GDN_HB, GDN_HEADS)
            for j in range(3):
                tot = tap_tiles[j][0][i]
                for s in range(1, GDN_CONV):
                    slot = j * GDN_CONV + s
                    tot = tot + _unshift_rows(tap_tiles[j][s][i], car_scr[h, slot], s)
                    car_scr[h, slot] = tap_tiles[j][s][i][0:SUBLANES, :]
                outs[j][:, sl] = tot.astype(outs[j].dtype)
            dz[:, sl] = dz_tiles[i].astype(dz.dtype)
        dab_tiles = _head_tiles(g[4])
        dab_sum = dab_tiles[0]
        for h in range(1, GDN_HB):
            dab_sum = dab_sum + dab_tiles[h]
        ds_scr[hsl] = g[5]
        dcw[hsl] += g[6]
        dgp[0] += g[7]

        @pl.when(grp == 0)
        def _():
            dab[...] = dab_sum

        @pl.when(grp > 0)
        def _():
            dab[...] += dab_sum

    rev = lambda n: last - n
    in_specs = _gdn_specs(rev) + [
        pl.BlockSpec((None, GDN_HB, LANES, LANES), lambda n, g: (rev(n), g, 0, 0)),
        pl.BlockSpec((GDN_TILE, GDN_HB * LANES), lambda n, g: (rev(n), g)),
    ]
    blk = pl.BlockSpec((GDN_TILE, GDN_HB * LANES), lambda n, g: (rev(n), g))
    return pl.pallas_call(
        body, name="gdn_bwd", grid=(n_chunks, GDN_HEADS // GDN_HB),
        in_specs=in_specs,
        out_specs=(blk, blk, blk, blk,
                   pl.BlockSpec((GDN_TILE, LANES), lambda n, h: (rev(n), 0)),
                   pl.BlockSpec((GDN_HEADS, 3, SUBLANES, LANES), lambda n, h: (0, 0, 0, 0)),
                   pl.BlockSpec((GDN_HEADS, SUBLANES, LANES), lambda n, h: (0, 0, 0))),
        out_shape=(jax.ShapeDtypeStruct((t, GDN_WIDTH), BF16), jax.ShapeDtypeStruct((t, GDN_WIDTH), BF16),
                   jax.ShapeDtypeStruct((t, GDN_WIDTH), BF16), jax.ShapeDtypeStruct((t, GDN_WIDTH), BF16),
                   jax.ShapeDtypeStruct((t, LANES), F32),
                   jax.ShapeDtypeStruct((GDN_HEADS, 3, SUBLANES, LANES), F32),
                   jax.ShapeDtypeStruct((GDN_HEADS, SUBLANES, LANES), F32)),
        scratch_shapes=[pltpu.VMEM((GDN_HEADS, LANES, LANES), F32),
                        pltpu.VMEM((GDN_HEADS, 3 * GDN_CONV, SUBLANES, LANES), F32)],
        compiler_params=_cparams(("arbitrary", "arbitrary")),
    )(p_cat, p_cat, p_cat, p_cat, p_cat, p_cat, p_cat, p_cat, cwpack, gpar, states, d_out)


MM_VMEM_BUDGET = 44 * 1024 * 1024
MM_MIN_STEPS = 4


def _mm_tiles(mode, m, n, k, out_bytes):
    tms = [t for t in (2048, 1024, 768, 512, 256, 128, 64) if m % t == 0 and (mode != 'tn' or t % LANES == 0)]
    tns = [t for t in (1408, 1024, 768, 512, 256, 128) if n % t == 0]
    tks = [t for t in (2048, 1920, 1408, 1024, 512, 256, 128, 64) if k % t == 0]
    best, best_key = None, None
    for tm in tms:
        for tn in tns:
            for tk in tks:
                nk = k // tk
                vmem = 2 * (tm * tk * 2 + tk * tn * 2 + tm * tn * out_bytes) + (tm * tn * 4 if nk > 1 else 0)
                steps = (m // tm) * (n // tn) * nk
                if vmem > MM_VMEM_BUDGET:
                    continue
                reread = m * k * (n // tn) + k * n * (m // tm)
                key = (steps >= MM_MIN_STEPS, tn if mode == 'tn' else 0, tm * tn * tk, -nk, -reread)
                if best_key is None or key > best_key:
                    best, best_key = (tm, tn, tk), key
    if best is None:
        raise ValueError(f"no matmul tile for {mode} {m}x{n}x{k}")
    return best


_MM_DIMS = {'nn': (((1,), (0,)), ((), ())), 'nt': (((1,), (1,)), ((), ())), 'tn': (((0,), (0,)), ((), ()))}


def _matmul(a, b, mode, out_dtype, name):
    if mode == 'nn':
        (m, k), (k2, n) = a.shape, b.shape
    elif mode == 'nt':
        (m, k), (n, k2) = a.shape, b.shape
    else:
        (k, m), (k2, n) = a.shape, b.shape
    assert k == k2, (a.shape, b.shape, mode)
    tm, tn, tk = _mm_tiles(mode, m, n, k, jnp.dtype(out_dtype).itemsize)
    nk = k // tk
    dims = _MM_DIMS[mode]

    def body(a_ref, b_ref, o_ref, acc_ref):
        kk = pl.program_id(2)
        part = lax.dot_general(a_ref[...], b_ref[...], dims, preferred_element_type=F32)
        if nk == 1:
            o_ref[...] = part.astype(o_ref.dtype)
            return

        @pl.when(kk == 0)
        def _():
            acc_ref[...] = part

        @pl.when((kk > 0) & (kk < nk - 1))
        def _():
            acc_ref[...] += part

        @pl.when(kk == nk - 1)
        def _():
            o_ref[...] = (acc_ref[...] + part).astype(o_ref.dtype)

    a_spec = pl.BlockSpec((tk, tm), lambda i, j, kk: (kk, i)) if mode == 'tn' else pl.BlockSpec((tm, tk), lambda i, j, kk: (i, kk))
    b_spec = pl.BlockSpec((tn, tk), lambda i, j, kk: (j, kk)) if mode == 'nt' else pl.BlockSpec((tk, tn), lambda i, j, kk: (kk, j))
    return pl.pallas_call(
        body, name=name, grid=(m // tm, n // tn, nk),
        in_specs=[a_spec, b_spec],
        out_specs=pl.BlockSpec((tm, tn), lambda i, j, kk: (i, j)),
        out_shape=jax.ShapeDtypeStruct((m, n), out_dtype),
        scratch_shapes=[pltpu.VMEM((tm, tn), F32)],
        compiler_params=_cparams(("parallel", "parallel", "arbitrary")),
    )(a, b)


ROW_TILE = 512


def _row_specs(rows, tm):
    return [pl.BlockSpec((tm, w), lambda i, ci=ci: (i, ci)) for (_, w, ci) in rows]


def _rw_forward(fn, rows, pars, outs, name):
    t = rows[0][0].shape[0]
    tm = min(ROW_TILE, t)
    nr, npar = len(rows), len(pars)

    def body(*refs):
        vals = [r[...].astype(F32) for r in refs[:nr]] + [p[...] for p in refs[nr:nr + npar]]
        res = fn(*vals)
        for o, v in zip(refs[nr + npar:], res):
            o[...] = v.astype(o.dtype)

    return pl.pallas_call(
        body, name=name, grid=(t // tm,),
        in_specs=_row_specs(rows, tm) + [pl.BlockSpec(p.shape, lambda i: (0, 0)) for p in pars],
        out_specs=tuple(pl.BlockSpec((tm, w), lambda i: (i, 0)) for (w, _) in outs),
        out_shape=tuple(jax.ShapeDtypeStruct((t, w), dt) for (w, dt) in outs),
        compiler_params=_cparams(("parallel",)),
    )(*[r[0] for r in rows], *pars)


def _rw_backward(fn, rows, pars, cots, drow_dtypes, name):
    t = rows[0][0].shape[0]
    tm = min(ROW_TILE, t)
    nr, npar, nc = len(rows), len(pars), len(cots)
    keep = [i for i, dt in enumerate(drow_dtypes) if dt is not None]

    def body(*refs):
        vals = [r[...].astype(F32) for r in refs[:nr]] + [p[...] for p in refs[nr:nr + npar]]
        cvals = tuple(c[...].astype(F32) for c in refs[nr + npar:nr + npar + nc])
        orefs = refs[nr + npar + nc:]
        _, vjp = jax.vjp(fn, *vals)
        g = vjp(cvals)
        for o, i in zip(orefs[:len(keep)], keep):
            o[...] = g[i].astype(o.dtype)
        first = pl.program_id(0) == 0
        for o, gi in zip(orefs[len(keep):], g[nr:]):
            @pl.when(first)
            def _(o=o, gi=gi):
                o[...] = gi

            @pl.when(jnp.logical_not(first))
            def _(o=o, gi=gi):
                o[...] += gi

    out_specs = [pl.BlockSpec((tm, rows[i][1]), lambda i_: (i_, 0)) for i in keep] + \
                [pl.BlockSpec(p.shape, lambda i_: (0, 0)) for p in pars]
    out_shape = [jax.ShapeDtypeStruct((t, rows[i][1]), drow_dtypes[i]) for i in keep] + \
                [jax.ShapeDtypeStruct(p.shape, F32) for p in pars]
    return pl.pallas_call(
        body, name=name, grid=(t // tm,),
        in_specs=_row_specs(rows, tm) + [pl.BlockSpec(p.shape, lambda i: (0, 0)) for p in pars] + _row_specs(cots, tm),
        out_specs=tuple(out_specs), out_shape=tuple(out_shape),
        compiler_params=_cparams(("arbitrary",)),
    )(*[r[0] for r in rows], *pars, *[c[0] for c in cots])


def _norm_fn(x, g):
    return (_rms(x, g),)


def _norm_skip_fn(x, g):
    return _rms(x, g), x


def _merge_fn(ga, gb, ya, yb):
    return (_sigmoid(ga) * ya + _sigmoid(gb) * yb,)


def _res_norm_fn(x, mo, g):
    x1 = x + mo
    return x1, _rms(x1, g)


def _loss_head(x1, fo, gf, target, name):
    t, d = x1.shape
    tm = min(ROW_TILE, t)

    def tile_loss(x2, g, tgt):
        err = _rms(x2, g) - tgt
        per_row = jnp.sum(err * err, axis=-1, keepdims=True) * (0.5 / d)
        return jnp.sum(per_row, axis=0, keepdims=True)

    def body(x1_ref, fo_ref, g_ref, t_ref, loss_ref, dx_ref, dxb_ref, dg_ref):
        x2 = x1_ref[...] + fo_ref[...]
        val, vjp = jax.vjp(functools.partial(tile_loss, tgt=t_ref[...]), x2, g_ref[...])
        dx2, dg = vjp(jnp.ones((1, 1), F32))
        dx_ref[...] = dx2
        dxb_ref[...] = dx2.astype(BF16)
        first = pl.program_id(0) == 0

        @pl.when(first)
        def _():
            loss_ref[...] = jnp.broadcast_to(val, loss_ref.shape)
            dg_ref[...] = dg

        @pl.when(jnp.logical_not(first))
        def _():
            loss_ref[...] += jnp.broadcast_to(val, loss_ref.shape)
            dg_ref[...] += dg

    row = pl.BlockSpec((tm, d), lambda i: (i, 0))
    vec = pl.BlockSpec((1, d), lambda i: (0, 0))
    return pl.pallas_call(
        body, name=name, grid=(t // tm,),
        in_specs=[row, row, vec, row],
        out_specs=(pl.BlockSpec((1, LANES), lambda i: (0, 0)), row, row, vec),
        out_shape=(jax.ShapeDtypeStruct((1, LANES), F32), jax.ShapeDtypeStruct((t, d), F32),
                   jax.ShapeDtypeStruct((t, d), BF16), jax.ShapeDtypeStruct((1, d), F32)),
        compiler_params=_cparams(("arbitrary",)),
    )(x1, fo, gf, target)


FFN_TILE_ROWS = 2048
FFN_TILE_COLS = 256
FFN_COL_BLOCKS = FFN_HIDDEN // FFN_TILE_COLS


def _conv3_past(x, halo, w):
    rows = lax.broadcasted_iota(jnp.int32, x.shape, 0)
    x1 = jnp.where(rows == 0, halo[7:8, :], pltpu.roll(x, 1, 0))
    x2 = jnp.where(rows == 0, halo[6:7, :], jnp.where(rows == 1, halo[7:8, :], pltpu.roll(x, 2, 0)))
    return x * w[2:3] + x1 * w[1:2] + x2 * w[0:1], x1, x2


def _ffn_in_specs(tm, imap, jmap):
    per = tm // SUBLANES
    tile = lambda off: pl.BlockSpec((tm, FFN_TILE_COLS), lambda *g: (imap(*g), off + jmap(*g) % FFN_COL_BLOCKS))
    halo = lambda off: pl.BlockSpec((SUBLANES, FFN_TILE_COLS),
                                    lambda *g: (jnp.maximum(imap(*g) * per - 1, 0), off + jmap(*g) % FFN_COL_BLOCKS))
    wsp = lambda off: pl.BlockSpec((FFN_CONV, FFN_TILE_COLS), lambda *g: (0, off + jmap(*g) % FFN_COL_BLOCKS))
    return [tile(0), halo(0), wsp(0), tile(FFN_COL_BLOCKS), halo(FFN_COL_BLOCKS), wsp(FFN_COL_BLOCKS)]


def _ffn_act_forward(hpre, cw):
    t = hpre.shape[0]
    tm = min(FFN_TILE_ROWS, t)

    def body(hg, pg, wg, hu, pu, wu, out):
        live = (pl.program_id(0) > 0).astype(F32)
        cg, _, _ = _conv3_past(hg[...], pg[...] * live, wg[...])
        cu, _, _ = _conv3_past(hu[...], pu[...] * live, wu[...])
        out[...] = (cg * _sigmoid_plain(cg) * cu).astype(out.dtype)

    return pl.pallas_call(
        body, name="ffn_act_fwd", grid=(t // tm, FFN_COL_BLOCKS),
        in_specs=_ffn_in_specs(tm, lambda i, j: i, lambda i, j: j),
        out_specs=pl.BlockSpec((tm, FFN_TILE_COLS), lambda i, j: (i, j)),
        out_shape=jax.ShapeDtypeStruct((t, FFN_HIDDEN), BF16),
        compiler_params=_cparams(("parallel", "parallel")),
    )(hpre, hpre, cw, hpre, hpre, cw)


def _conv3_future(d, nxt, w):
    tm = d.shape[0]
    rows = lax.broadcasted_iota(jnp.int32, d.shape, 0)
    d1 = jnp.where(rows == tm - 1, nxt[0:1, :], pltpu.roll(d, tm - 1, 0))
    d2 = jnp.where(rows == tm - 1, nxt[1:2, :], jnp.where(rows == tm - 2, nxt[0:1, :], pltpu.roll(d, tm - 2, 0)))
    return d * w[2:3] + d1 * w[1:2] + d2 * w[0:1]


def _ffn_backward(hpre, cw, dact):
    t = hpre.shape[0]
    tm = min(FFN_TILE_ROWS, t)
    n_tiles = t // tm
    per = tm // SUBLANES

    def d_conv_out(cg, cu, d):
        s = _sigmoid_plain(cg)
        return d * cu * s * (1.0 + cg * (1.0 - s)), d * cg * s

    def body(hg, pg, ng, wg, hu, pu, nu, wu, da, dan, dhg, dhu, dwg, dwu):
        i = pl.program_id(1)
        live_prev = (i > 0).astype(F32)
        live_next = (i < n_tiles - 1).astype(F32)
        xg, xu = hg[...], hu[...]
        cg, g1, g2 = _conv3_past(xg, pg[...] * live_prev, wg[...])
        cu, u1, u2 = _conv3_past(xu, pu[...] * live_prev, wu[...])
        dg, du = d_conv_out(cg, cu, da[...])
        cgn, _, _ = _conv3_past(ng[...], hg[tm - SUBLANES:tm, :], wg[...])
        cun, _, _ = _conv3_past(nu[...], hu[tm - SUBLANES:tm, :], wu[...])
        dgn, dun = d_conv_out(cgn, cun, dan[...] * live_next)
        dhg[...] = _conv3_future(dg, dgn, wg[...]).astype(dhg.dtype)
        dhu[...] = _conv3_future(du, dun, wu[...]).astype(dhu.dtype)
        sums_g = [jnp.sum(xs * dg, axis=0, keepdims=True) for xs in (g2, g1, xg)]
        sums_u = [jnp.sum(xs * du, axis=0, keepdims=True) for xs in (u2, u1, xu)]

        @pl.when(i == 0)
        def _():
            for r_ in range(FFN_CONV):
                dwg[r_:r_ + 1, :] = sums_g[r_]
                dwu[r_:r_ + 1, :] = sums_u[r_]

        @pl.when(i > 0)
        def _():
            for r_ in range(FFN_CONV):
                dwg[r_:r_ + 1, :] += sums_g[r_]
                dwu[r_:r_ + 1, :] += sums_u[r_]

    nb = FFN_COL_BLOCKS
    nxt = lambda i: jnp.minimum((i + 1) * per, t // SUBLANES - 1)
    prv = lambda i: jnp.maximum(i * per - 1, 0)
    half = lambda off: [pl.BlockSpec((tm, FFN_TILE_COLS), lambda j, i: (i, off + j)),
                        pl.BlockSpec((SUBLANES, FFN_TILE_COLS), lambda j, i: (prv(i), off + j)),
                        pl.BlockSpec((SUBLANES, FFN_TILE_COLS), lambda j, i: (nxt(i), off + j)),
                        pl.BlockSpec((FFN_CONV, FFN_TILE_COLS), lambda j, i: (0, off + j))]
    tile = pl.BlockSpec((tm, FFN_TILE_COLS), lambda j, i: (i, j))
    taps = pl.BlockSpec((FFN_CONV, FFN_TILE_COLS), lambda j, i: (0, j))
    return pl.pallas_call(
        body, name="ffn_bwd", grid=(nb, n_tiles),
        in_specs=half(0) + half(nb) + [tile, pl.BlockSpec((SUBLANES, FFN_TILE_COLS), lambda j, i: (nxt(i), j))],
        out_specs=(tile, tile, taps, taps),
        out_shape=(jax.ShapeDtypeStruct((t, FFN_HIDDEN), BF16), jax.ShapeDtypeStruct((t, FFN_HIDDEN), BF16),
                   jax.ShapeDtypeStruct((FFN_CONV, FFN_HIDDEN), F32), jax.ShapeDtypeStruct((FFN_CONV, FFN_HIDDEN), F32)),
        compiler_params=_cparams(("parallel", "arbitrary")),
    )(hpre, hpre, hpre, cw, hpre, hpre, hpre, cw, dact, dact)


def _my_place():
    x, y, c = lax.axis_index("x"), lax.axis_index("y"), lax.axis_index("c")
    return x, y, c, 4 * x + 2 * y + c


N_CHIPS = 4


def _remote(src, dst, send_sem, recv_sem, dev):
    return pltpu.make_async_remote_copy(src_ref=src, dst_ref=dst, send_sem=send_sem, recv_sem=recv_sem, device_id=dev,
                                        device_id_type=pl.DeviceIdType.MESH)


def _chip_peer(x, y, k):
    return x ^ ((k >> 1) & 1), y ^ (k & 1)


def _all_gather_two_level(shard, name):
    r, w = shard.shape

    def body(src, out, send_sems, recv_sems, local_sem):
        x, y, c, me = _my_place()
        sibling = (x, y, 1 - c)
        mine = pltpu.make_async_copy(src, out.at[me], local_sem)
        mine.start()
        first = [_remote(src, out.at[me], send_sems.at[0], recv_sems.at[0], sibling)]
        for k in range(1, N_CHIPS):
            px, py = _chip_peer(x, y, k)
            first.append(_remote(src, out.at[me], send_sems.at[k], recv_sems.at[k], (px, py, c)))
        for cp in first:
            cp.start()
        passed = []
        for k in range(1, N_CHIPS):
            px, py = _chip_peer(x, y, k)
            landed = out.at[me ^ (2 * k)]
            _remote(src, landed, send_sems.at[k], recv_sems.at[k], (px, py, c)).wait_recv()
            fwd = _remote(landed, landed, send_sems.at[N_CHIPS - 1 + k], recv_sems.at[N_CHIPS - 1 + k], sibling)
            fwd.start()
            passed.append(fwd)
        _remote(src, out.at[me ^ 1], send_sems.at[0], recv_sems.at[0], sibling).wait_recv()
        for k in range(1, N_CHIPS):
            got = out.at[(me ^ 1) ^ (2 * k)]
            _remote(got, got, send_sems.at[N_CHIPS - 1 + k], recv_sems.at[N_CHIPS - 1 + k], sibling).wait_recv()
        for cp in first + passed:
            cp.wait_send()
        mine.wait()

    return pl.pallas_call(
        body, name=name,
        in_specs=[pl.BlockSpec(memory_space=pl.ANY)],
        out_specs=pl.BlockSpec(memory_space=pl.ANY),
        out_shape=jax.ShapeDtypeStruct((N_DEV, r, w), shard.dtype),
        scratch_shapes=[pltpu.SemaphoreType.DMA((N_DEV - 1,)), pltpu.SemaphoreType.DMA((N_DEV - 1,)), pltpu.SemaphoreType.DMA],
    )(shard)


def _device_peer(x, y, c, k):
    px, py, pc = x ^ ((k >> 2) & 1), y ^ ((k >> 1) & 1), c ^ (k & 1)
    return (px, py, pc), 4 * px + 2 * py + pc


_HBM = pl.BlockSpec(memory_space=pltpu.HBM)
_SEM = pl.BlockSpec(memory_space=pltpu.SEMAPHORE)


def _gather_start(shard, name):
    def body(src, land, send_sems, recv_sems, src_thru, land_thru, token):
        x, y, c, me = _my_place()
        for k in range(1, N_DEV):
            dev, _ = _device_peer(x, y, c, k)
            _remote(src, land.at[me], send_sems.at[k], recv_sems.at[k], dev).start()
        token[...] = jnp.zeros_like(token)

    landing = lax.empty((N_DEV,) + shard.shape, shard.dtype)
    return pl.pallas_call(
        body, name=name,
        out_shape=(pltpu.SemaphoreType.DMA((N_DEV,)), pltpu.SemaphoreType.DMA((N_DEV,)), pltpu.HBM(shard.shape, shard.dtype),
                   pltpu.HBM(landing.shape, landing.dtype), jax.ShapeDtypeStruct((SUBLANES, LANES), F32)),
        in_specs=(_HBM, _HBM), out_specs=(_SEM, _SEM, _HBM, _HBM, pl.BlockSpec(memory_space=pltpu.VMEM)),
        input_output_aliases={0: 2, 1: 3},
        compiler_params=pltpu.CompilerParams(has_side_effects=pltpu.SideEffectType.DATAFLOW_SIDE_EFFECTING),
    )(pltpu.with_memory_space_constraint(shard, pltpu.HBM), pltpu.with_memory_space_constraint(landing, pltpu.HBM))


def _gather_wait(send_sems, recv_sems, shard, landing, after, name):
    n_after = len(after)

    def body(*refs):
        src, land, send_sems, recv_sems = refs[:4]
        x, y, c, _ = _my_place()
        for k in range(1, N_DEV):
            dev, idx = _device_peer(x, y, c, k)
            cp = _remote(src, land.at[idx], send_sems.at[k], recv_sems.at[k], dev)
            cp.wait_send()
            cp.wait_recv()

    return pl.pallas_call(
        body, name=name,
        out_shape=(pltpu.HBM(shard.shape, shard.dtype), pltpu.HBM(landing.shape, landing.dtype)),
        in_specs=(_HBM, _HBM, _SEM, _SEM) + (pl.BlockSpec(memory_space=pl.ANY),) * n_after, out_specs=(_HBM, _HBM),
        input_output_aliases={0: 0, 1: 1},
        compiler_params=pltpu.CompilerParams(has_side_effects=pltpu.SideEffectType.DATAFLOW_SIDE_EFFECTING),
    )(shard, landing, send_sems, recv_sems, *after)[1]


def _slab_push_start(slabs, name):
    na = len(slabs)

    def body(*refs):
        srcs, lands = refs[:na], refs[na:2 * na]
        send_sems, recv_sems = refs[2 * na], refs[2 * na + 1]
        token = refs[-1]
        x, y, c, me = _my_place()
        for i in range(na):
            for k in range(1, N_DEV):
                dev, idx = _device_peer(x, y, c, k)
                s = i * N_DEV + k
                _remote(srcs[i].at[idx], lands[i].at[me], send_sems.at[s], recv_sems.at[s], dev).start()
        token[...] = jnp.zeros_like(token)

    hbm_shapes = [pltpu.HBM(a.shape, a.dtype) for a in slabs]
    ins = [pltpu.with_memory_space_constraint(a, pltpu.HBM) for a in slabs]
    ins += [pltpu.with_memory_space_constraint(lax.empty(a.shape, a.dtype), pltpu.HBM) for a in slabs]
    out = pl.pallas_call(
        body, name=name,
        out_shape=(pltpu.SemaphoreType.DMA((na * N_DEV,)), pltpu.SemaphoreType.DMA((na * N_DEV,)), *hbm_shapes, *hbm_shapes,
                   jax.ShapeDtypeStruct((SUBLANES, LANES), F32)),
        in_specs=(_HBM,) * (2 * na), out_specs=(_SEM, _SEM) + (_HBM,) * (2 * na) + (pl.BlockSpec(memory_space=pltpu.VMEM),),
        input_output_aliases={i: 2 + i for i in range(2 * na)},
        compiler_params=pltpu.CompilerParams(has_side_effects=pltpu.SideEffectType.DATAFLOW_SIDE_EFFECTING),
    )(*ins)
    return out[0], out[1], list(out[2:2 + na]), list(out[2 + na:2 + 2 * na]), out[-1]


def _slab_push_wait(send_sems, recv_sems, slabs, landings, after, name):
    na = len(slabs)

    def body(*refs):
        srcs, lands = refs[:na], refs[na:2 * na]
        send_sems, recv_sems = refs[2 * na], refs[2 * na + 1]
        x, y, c, me = _my_place()
        for i in range(na):
            for k in range(1, N_DEV):
                dev, idx = _device_peer(x, y, c, k)
                s = i * N_DEV + k
                cp = _remote(srcs[i].at[idx], lands[i].at[idx], send_sems.at[s], recv_sems.at[s], dev)
                cp.wait_send()
                cp.wait_recv()

    hbm_shapes = tuple(pltpu.HBM(a.shape, a.dtype) for a in slabs)
    out = pl.pallas_call(
        body, name=name, out_shape=hbm_shapes + hbm_shapes,
        in_specs=(_HBM,) * (2 * na) + (_SEM, _SEM) + (pl.BlockSpec(memory_space=pl.ANY),) * len(after),
        out_specs=(_HBM,) * (2 * na), input_output_aliases={i: i for i in range(2 * na)},
        compiler_params=pltpu.CompilerParams(has_side_effects=pltpu.SideEffectType.DATAFLOW_SIDE_EFFECTING),
    )(*slabs, *landings, send_sems, recv_sems, *after)
    return list(out[na:])


def _pair_exchange(arrays, name):
    na = len(arrays)

    def body(*refs):
        srcs, dsts, (send_sems, recv_sems) = refs[:na], refs[na:2 * na], refs[2 * na:]
        x, y, c, _ = _my_place()
        sibling = (x, y, 1 - c)
        copies = []
        for i in range(na):
            for q in range(N_CHIPS):
                s = i * N_CHIPS + q
                copies.append(_remote(srcs[i].at[2 * q + 1 - c], dsts[i].at[q], send_sems.at[s], recv_sems.at[s], sibling))
        for cp in copies:
            cp.start()
        for cp in copies:
            cp.wait_recv()
        for cp in copies:
            cp.wait_send()

    hbm = pl.BlockSpec(memory_space=pl.ANY)
    return pl.pallas_call(
        body, name=name, in_specs=[hbm] * na, out_specs=tuple([hbm] * na),
        out_shape=tuple(jax.ShapeDtypeStruct((N_CHIPS,) + a.shape[1:], a.dtype) for a in arrays),
        scratch_shapes=[pltpu.SemaphoreType.DMA((na * N_CHIPS,)), pltpu.SemaphoreType.DMA((na * N_CHIPS,))],
    )(*arrays)


ELEMENTWISE_COLS = 512


def _pair_sum(slabs, recv, core, out_dtype, name):
    _, r, w = slabs.shape
    tc = ELEMENTWISE_COLS

    def body(core_ref, mine, theirs, out):
        out[...] = (mine[...] + theirs[...]).astype(out.dtype)

    grid_spec = pltpu.PrefetchScalarGridSpec(
        num_scalar_prefetch=1, grid=(N_CHIPS, w // tc),
        in_specs=[pl.BlockSpec((None, r, tc), lambda q, j, core_ref: (2 * q + core_ref[0], 0, j)),
                  pl.BlockSpec((None, r, tc), lambda q, j, core_ref: (q, 0, j))],
        out_specs=pl.BlockSpec((None, r, tc), lambda q, j, core_ref: (q, 0, j)))
    return pl.pallas_call(body, name=name, grid_spec=grid_spec,
                          out_shape=jax.ShapeDtypeStruct((N_CHIPS, r, w), out_dtype),
                          compiler_params=_cparams(("parallel", "parallel")))(core, slabs, recv)


def _chip_exchange(arrays, name):
    na = len(arrays)

    def body(*refs):
        srcs, dsts, (send_sems, recv_sems, local_sems) = refs[:na], refs[na:2 * na], refs[2 * na:]
        x, y, c, _ = _my_place()
        chip = 2 * x + y
        own = [pltpu.make_async_copy(srcs[i].at[chip], dsts[i].at[chip], local_sems.at[i]) for i in range(na)]
        for cp in own:
            cp.start()
        sends, arrivals = [], []
        for i in range(na):
            for k in range(1, N_CHIPS):
                px, py = _chip_peer(x, y, k)
                s = i * N_CHIPS + k
                sends.append(_remote(srcs[i].at[chip ^ k], dsts[i].at[chip], send_sems.at[s], recv_sems.at[s], (px, py, c)))
                arrivals.append(_remote(srcs[i].at[chip], dsts[i].at[chip ^ k], send_sems.at[s], recv_sems.at[s], (px, py, c)))
        for cp in sends:
            cp.start()
        for cp in arrivals:
            cp.wait_recv()
        for cp in sends:
            cp.wait_send()
        for cp in own:
            cp.wait()

    hbm = pl.BlockSpec(memory_space=pl.ANY)
    return pl.pallas_call(
        body, name=name, in_specs=[hbm] * na, out_specs=tuple([hbm] * na),
        out_shape=tuple(jax.ShapeDtypeStruct(a.shape, a.dtype) for a in arrays),
        scratch_shapes=[pltpu.SemaphoreType.DMA((na * N_CHIPS,)), pltpu.SemaphoreType.DMA((na * N_CHIPS,)),
                        pltpu.SemaphoreType.DMA((na,))],
    )(*arrays)


def _adamw_update(g, w, m, v):
    c1 = 1.0 / (1.0 - ADAM_B1 ** ADAM_STEP)
    c2 = 1.0 / (1.0 - ADAM_B2 ** ADAM_STEP)
    mn = ADAM_B1 * m + (1.0 - ADAM_B1) * g
    vn = ADAM_B2 * v + (1.0 - ADAM_B2) * (g * g)
    return -ADAM_LR * ((mn * c1) / (jnp.sqrt(vn * c2) + ADAM_EPS) + ADAM_WD * w), mn, vn


def _reduce_adamw(parts, w, m, v, name):
    n_parts, r, wd = parts.shape
    tc = ELEMENTWISE_COLS

    def body(p_ref, w_ref, m_ref, v_ref, g_out, d_out, m_out, v_out):
        g = p_ref[0].astype(F32)
        for s in range(1, n_parts):
            g = g + p_ref[s].astype(F32)
        g_out[...] = g
        d_out[...], m_out[...], v_out[...] = _adamw_update(g, w_ref[...], m_ref[...], v_ref[...])

    blk = pl.BlockSpec((r, tc), lambda j: (0, j))
    shp = jax.ShapeDtypeStruct((r, wd), F32)
    return pl.pallas_call(
        body, name=name, grid=(wd // tc,),
        in_specs=[pl.BlockSpec((n_parts, r, tc), lambda j: (0, 0, j)), blk, blk, blk],
        out_specs=(blk, blk, blk, blk), out_shape=(shp, shp, shp, shp),
        compiler_params=_cparams(("parallel",)),
    )(parts, w, m, v)


def _reduce_landed_adamw(landing, own, me, w, m, v, name):
    n_parts, r, wd = landing.shape
    tc = ELEMENTWISE_COLS

    def body(me_ref, land_ref, own_ref, w_ref, m_ref, v_ref, g_out, d_out, m_out, v_out):
        mine = own_ref[...].astype(F32)
        g = None
        for s in range(n_parts):
            part = jnp.where(me_ref[0] == s, mine, land_ref[s].astype(F32))
            g = part if g is None else g + part
        g_out[...] = g
        d_out[...], m_out[...], v_out[...] = _adamw_update(g, w_ref[...], m_ref[...], v_ref[...])

    blk = pl.BlockSpec((r, tc), lambda j, me_ref: (0, j))
    shp = jax.ShapeDtypeStruct((r, wd), F32)
    grid_spec = pltpu.PrefetchScalarGridSpec(
        num_scalar_prefetch=1, grid=(wd // tc,),
        in_specs=[pl.BlockSpec((n_parts, r, tc), lambda j, me_ref: (0, 0, j)),
                  pl.BlockSpec((None, r, tc), lambda j, me_ref: (me_ref[0], 0, j)), blk, blk, blk],
        out_specs=(blk, blk, blk, blk))
    return pl.pallas_call(body, name=name, grid_spec=grid_spec, out_shape=(shp, shp, shp, shp),
                          compiler_params=_cparams(("parallel",)))(me, landing, own, w, m, v)


PACK_W = 1024


def _pad_heads(a, slots):
    lead = a.shape[:-1]
    a = a.reshape(lead + (slots, RWKV_HEAD_DIM))
    a = jnp.pad(a, [(0, 0)] * (len(lead) + 1) + [(0, LANES - RWKV_HEAD_DIM)])
    return a.reshape(lead + (slots * LANES,))


def _flat_pack(arrs, dtype, row_mult):
    flat = jnp.concatenate([a.reshape(-1).astype(dtype) for a in arrs])
    n = flat.shape[0]
    rows = -(-n // PACK_W)
    rows = -(-rows // row_mult) * row_mult
    return jnp.pad(flat, (0, rows * PACK_W - n)).reshape(rows, PACK_W)


def _row_pack(arrs, dtype, row_mult):
    parts = [a.astype(dtype) if a.shape[1] == PACK_W else a.astype(dtype).reshape(-1, PACK_W) for a in arrs]
    rows = sum(p.shape[0] for p in parts)
    pad = -(-rows // row_mult) * row_mult - rows
    return jnp.concatenate(parts + ([jnp.zeros((pad, PACK_W), dtype)] if pad else []), axis=0)


def _unpack_row_gathered(g, names, shard_shapes):
    out, r0 = {}, 0
    for n in names:
        s = shard_shapes[n]
        rows = s[0] * s[1] // PACK_W
        seg = g[:, r0:r0 + rows, :]
        r0 += rows
        if s[1] == PACK_W:
            assert SHARD_AXIS[n] == 0
            out[n] = seg.reshape(N_DEV * s[0], s[1])
        else:
            assert SHARD_AXIS[n] == 1
            out[n] = jnp.transpose(seg.reshape((N_DEV,) + tuple(s)), (1, 0, 2)).reshape(s[0], N_DEV * s[1])
    return out


def _unpack_gathered(g, names, shard_shapes):
    flat = g.reshape(N_DEV, -1)
    out, off = {}, 0
    for n in names:
        s = shard_shapes[n]
        size = s[0] * s[1]
        seg = flat[:, off:off + size].reshape((N_DEV,) + tuple(s))
        off += size
        if SHARD_AXIS[n] == 1:
            out[n] = jnp.transpose(seg, (1, 0, 2)).reshape(s[0], N_DEV * s[1])
        else:
            out[n] = seg.reshape(N_DEV * s[0], s[1])
    return out


def _shard_major(full, axis):
    a, b = full.shape
    if axis == 1:
        return jnp.transpose(full.reshape(a, N_DEV, b // N_DEV), (1, 0, 2)).reshape(N_DEV, -1)
    return full.reshape(N_DEV, -1)


def _prepare_weights(full, rep):
    w = full['w_in']
    d = w.shape[1]
    rkv = jnp.pad(w[0:1536].reshape(3 * RWKV_HEADS, RWKV_HEAD_DIM, d), ((0, 0), (0, LANES - RWKV_HEAD_DIM), (0, 0)))
    w_cat = jnp.concatenate([
        w[3848:4872], w[4872:5896], rkv.reshape(3 * RWKV_HEADS * LANES, d), w[1792:3328], w[3328:3840],
        w[1536:1792], jnp.pad(w[3840:3848], ((0, LANES - 8), (0, 0))), jnp.zeros((LANES, d), w.dtype)], axis=0)
    assert w_cat.shape[0] == CAT_W
    mu = rep['rwkv_mu']
    vecs = [mu[0:512], mu[512:1024], mu[1024:1536], rep['rwkv_w0'], rep['rwkv_a0'], rep['rwkv_k_k'], rep['rwkv_k_a'],
            rep['rwkv_ln_w'], rep['rwkv_ln_b'], rep['rwkv_r_k'].reshape(-1)]
    ppack = jnp.stack([jnp.pad(v.reshape(RWKV_HEADS, RWKV_HEAD_DIM), ((0, 0), (0, LANES - RWKV_HEAD_DIM))) for v in vecs], axis=1)
    ppack = jnp.pad(ppack, ((0, 0), (0, 16 - len(vecs)), (0, 0)))
    mulo = mu[1536:1792].reshape(1, 2 * LANES)
    wl = jnp.zeros((3, 2 * LANES, RWKV_HEADS * LANES), F32)
    wl = wl.at[0, 0:64].set(_pad_heads(full['rwkv_w2'], RWKV_HEADS))
    wl = wl.at[1, 64:128].set(_pad_heads(full['rwkv_a2'], RWKV_HEADS))
    wl = wl.at[2, 128:256].set(_pad_heads(full['rwkv_g2'], RWKV_HEADS))
    wl = jnp.transpose(wl.reshape(3, 2 * LANES, RWKV_HEADS, LANES), (2, 0, 1, 3))
    cw = full['gdn_conv_w'].reshape(GDN_CONV, 3, GDN_HEADS, LANES)
    cwpack = jnp.pad(jnp.transpose(cw, (2, 1, 0, 3)), ((0, 0), (0, 0), (0, SUBLANES - GDN_CONV), (0, 0)))
    gpar = jnp.zeros((SUBLANES, LANES), F32)
    gpar = gpar.at[0, 0:GDN_HEADS].set(rep['gdn_a_log']).at[1, 0:GDN_HEADS].set(rep['gdn_dt_bias']).at[2].set(rep['gdn_norm_w'])
    return dict(w_cat=w_cat, ffn_cw=full['ffn_conv_w'], ppack=ppack, mulo=mulo, wl=wl, cwpack=cwpack, gpar=gpar,
                g1=rep['norm1_g'].reshape(1, -1), g2=rep['norm2_g'].reshape(1, -1), gf=rep['final_g'].reshape(1, -1))


def _prepare_late_weights(full):
    rp = full['rwkv_proj']
    rproj = jnp.pad(rp.reshape(RWKV_HEADS, RWKV_HEAD_DIM, -1), ((0, 0), (0, LANES - RWKV_HEAD_DIM), (0, 0))).reshape(RWKV_HEADS * LANES, -1)
    return dict(rproj=rproj, gproj=full['gdn_proj'], w_out=full['w_out'], ffn_up=full['ffn_up'], ffn_down=full['ffn_down'])


def _local_step(x, target, p, late_weights, push_grads):
    d = x.shape[1]
    full_w = lambda a: (a, a.shape[1], 0)
    (u,) = _rw_forward(_norm_fn, [full_w(x)], [p['g1']], [(d, BF16)], "norm1")
    p_cat = _matmul(u, p['w_cat'], 'nt', F32, "proj_in")
    ya_pre, st_r = _rwkv_forward(p_cat, p['ppack'], p['mulo'], p['wl'])
    yb_pre, st_g = _gdn_forward(p_cat, p['cwpack'], p['gpar'])
    p = {**p, **late_weights((ya_pre, yb_pre))}
    ya = _matmul(ya_pre, p['rproj'], 'nn', F32, "rwkv_proj")
    yb = _matmul(yb_pre, p['gproj'], 'nn', F32, "gdn_proj")
    gates = [(p_cat, d, OFF_GA // d), (p_cat, d, OFF_GB // d)]
    (mixed,) = _rw_forward(_merge_fn, gates + [full_w(ya), full_w(yb)], [], [(d, BF16)], "merge")
    mo = _matmul(mixed, p['w_out'], 'nn', F32, "out_proj")
    x1, n2 = _rw_forward(_res_norm_fn, [full_w(x), full_w(mo)], [p['g2']], [(d, F32), (d, BF16)], "res_norm2")
    hpre = _matmul(n2, p['ffn_up'], 'nt', F32, "ffn_up")
    act = _ffn_act_forward(hpre, p['ffn_cw'])
    fo = _matmul(act, p['ffn_down'], 'nn', F32, "ffn_down")
    loss_vec, dx2, dx2b, dgf = _loss_head(x1, fo, p['gf'], target, "loss_head")

    dact = _matmul(dx2b, p['ffn_down'], 'nt', F32, "d_act")
    dw_down = _matmul(act, dx2b, 'tn', BF16, "dw_ffn_down")
    dh_gate, dh_up, dcw_gate, dcw_up = _ffn_backward(hpre, p['ffn_cw'], dact)
    dh = jnp.concatenate([dh_gate, dh_up], axis=1)
    dcw_f = jnp.concatenate([dcw_gate, dcw_up], axis=1)
    dn2 = _matmul(dh, p['ffn_up'], 'nn', F32, "d_norm2")
    dw_up = _matmul(dh, n2, 'tn', BF16, "dw_ffn_up")
    token = push_grads({'ffn_down': dw_down, 'ffn_up': dw_up})
    dx1, dx1b, dg2 = _rw_backward(_res_norm_fn, [full_w(x), full_w(mo)], [p['g2'] + token], [full_w(dx2), full_w(dn2)],
                                  [F32, BF16], "res_norm2_bwd")
    dmixed = _matmul(dx1b, p['w_out'], 'nt', F32, "d_mixed")
    dw_out = _matmul(mixed, dx1b, 'tn', BF16, "dw_out")
    dga, dgb, dya, dyb = _rw_backward(_merge_fn, gates + [full_w(ya), full_w(yb)], [], [full_w(dmixed)],
                                      [BF16, BF16, BF16, BF16], "merge_bwd")
    d_ya_pre = _matmul(dya, p['rproj'], 'nt', F32, "d_rwkv_out")
    dw_rproj = _matmul(ya_pre, dya, 'tn', F32, "dw_rwkv_proj")
    d_yb_pre = _matmul(dyb, p['gproj'], 'nt', F32, "d_gdn_out")
    dw_gproj = _matmul(yb_pre, dyb, 'tn', F32, "dw_gdn_proj")
    dpr, dpk, dpv, dplo, dpp, dml, dwl = _rwkv_backward(p_cat, p['ppack'], p['mulo'], p['wl'], st_r, d_ya_pre)
    dq, dk, dv, dz, dab, dcw_g, dgp = _gdn_backward(p_cat, p['cwpack'], p['gpar'], st_g, d_yb_pre)
    t = x.shape[0]
    dp_cat = jnp.concatenate([dga, dgb, dpr, dpk, dpv, dq, dk, dv, dz, dplo.astype(BF16), dab.astype(BF16),
                              jnp.zeros((t, LANES), BF16)], axis=1)
    dw_cat = _matmul(dp_cat, u, 'tn', BF16, "dw_in")
    dw_in = jnp.concatenate([dw_cat[OFF_RKV:OFF_QKV].reshape(3 * RWKV_HEADS, LANES, d)[:, :RWKV_HEAD_DIM].reshape(-1, d),
                             dw_cat[OFF_LO:OFF_AB], dw_cat[OFF_QKV:OFF_Z], dw_cat[OFF_Z:OFF_LO], dw_cat[OFF_AB:OFF_AB + 8],
                             dw_cat[OFF_GA:OFF_GB], dw_cat[OFF_GB:OFF_RKV]], axis=0)
    token = push_grads({'w_out': dw_out, 'w_in': dw_in})
    du = _matmul(dp_cat, p['w_cat'], 'nn', F32, "d_norm1")
    grad_x, dg1 = _rw_backward(_norm_skip_fn, [full_w(x)], [p['g1'] + token], [full_w(du), full_w(dx1)], [F32], "norm1_bwd")

    heads = lambda row: dpp[:, row, :RWKV_HEAD_DIM].reshape(-1)
    lora = lambda j, lo_, hi_: jnp.transpose(dwl[:, j, lo_:hi_, :RWKV_HEAD_DIM], (1, 0, 2)).reshape(hi_ - lo_, RWKV_WIDTH)
    grads = {
        'norm1_g': dg1[0],
        'w_in': dw_in,
        'rwkv_mu': jnp.concatenate([heads(0), heads(1), heads(2), jnp.sum(dml[:, 0, :], axis=0)]),
        'rwkv_w0': heads(3), 'rwkv_a0': heads(4), 'rwkv_k_k': heads(5), 'rwkv_k_a': heads(6),
        'rwkv_ln_w': heads(7), 'rwkv_ln_b': heads(8), 'rwkv_r_k': heads(9).reshape(RWKV_HEADS, RWKV_HEAD_DIM),
        'rwkv_w2': lora(0, 0, 64), 'rwkv_a2': lora(1, 64, 128), 'rwkv_g2': lora(2, 128, 256),
        'rwkv_proj': dw_rproj.reshape(RWKV_HEADS, LANES, -1)[:, :RWKV_HEAD_DIM].reshape(RWKV_WIDTH, -1),
        'gdn_conv_w': jnp.transpose(dcw_g[:, :, :GDN_CONV, :], (2, 1, 0, 3)).reshape(GDN_CONV, 3 * GDN_WIDTH),
        'gdn_a_log': jnp.sum(dgp[:, 0, :GDN_HEADS], axis=0), 'gdn_dt_bias': jnp.sum(dgp[:, 1, :GDN_HEADS], axis=0),
        'gdn_norm_w': jnp.sum(dgp[:, 2, :], axis=0),
        'gdn_proj': dw_gproj, 'w_out': dw_out, 'norm2_g': dg2[0], 'ffn_up': dw_up, 'ffn_conv_w': dcw_f,
        'ffn_down': dw_down, 'final_g': dgf[0],
    }
    return loss_vec, grad_x, grads


def kernel(x, norm1_g, w_in, rwkv_mu, rwkv_w0, rwkv_w2, rwkv_a0, rwkv_a2, rwkv_g2, rwkv_k_k, rwkv_k_a, rwkv_r_k, rwkv_ln_w, rwkv_ln_b, rwkv_proj, gdn_conv_w, gdn_a_log, gdn_dt_bias, gdn_norm_w, gdn_proj, w_out, norm2_g, ffn_up, ffn_conv_w, ffn_down, final_g, loss_target, m_norm1_g, m_w_in, m_rwkv_mu, m_rwkv_w0, m_rwkv_w2, m_rwkv_a0, m_rwkv_a2, m_rwkv_g2, m_rwkv_k_k, m_rwkv_k_a, m_rwkv_r_k, m_rwkv_ln_w, m_rwkv_ln_b, m_rwkv_proj, m_gdn_conv_w, m_gdn_a_log, m_gdn_dt_bias, m_gdn_norm_w, m_gdn_proj, m_w_out, m_norm2_g, m_ffn_up, m_ffn_conv_w, m_ffn_down, m_final_g, v_norm1_g, v_w_in, v_rwkv_mu, v_rwkv_w0, v_rwkv_w2, v_rwkv_a0, v_rwkv_a2, v_rwkv_g2, v_rwkv_k_k, v_rwkv_k_a, v_rwkv_r_k, v_rwkv_ln_w, v_rwkv_ln_b, v_rwkv_proj, v_gdn_conv_w, v_gdn_a_log, v_gdn_dt_bias, v_gdn_norm_w, v_gdn_proj, v_w_out, v_norm2_g, v_ffn_up, v_ffn_conv_w, v_ffn_down, v_final_g):
    given = dict(zip(WEIGHT_NAMES, (norm1_g, w_in, rwkv_mu, rwkv_w0, rwkv_w2, rwkv_a0, rwkv_a2, rwkv_g2, rwkv_k_k, rwkv_k_a, rwkv_r_k,
                                    rwkv_ln_w, rwkv_ln_b, rwkv_proj, gdn_conv_w, gdn_a_log, gdn_dt_bias, gdn_norm_w, gdn_proj, w_out,
                                    norm2_g, ffn_up, ffn_conv_w, ffn_down, final_g)))
    mom1 = dict(zip(WEIGHT_NAMES, (m_norm1_g, m_w_in, m_rwkv_mu, m_rwkv_w0, m_rwkv_w2, m_rwkv_a0, m_rwkv_a2, m_rwkv_g2, m_rwkv_k_k,
                                   m_rwkv_k_a, m_rwkv_r_k, m_rwkv_ln_w, m_rwkv_ln_b, m_rwkv_proj, m_gdn_conv_w, m_gdn_a_log,
                                   m_gdn_dt_bias, m_gdn_norm_w, m_gdn_proj, m_w_out, m_norm2_g, m_ffn_up, m_ffn_conv_w, m_ffn_down,
                                   m_final_g)))
    mom2 = dict(zip(WEIGHT_NAMES, (v_norm1_g, v_w_in, v_rwkv_mu, v_rwkv_w0, v_rwkv_w2, v_rwkv_a0, v_rwkv_a2, v_rwkv_g2, v_rwkv_k_k,
                                   v_rwkv_k_a, v_rwkv_r_k, v_rwkv_ln_w, v_rwkv_ln_b, v_rwkv_proj, v_gdn_conv_w, v_gdn_a_log,
                                   v_gdn_dt_bias, v_gdn_norm_w, v_gdn_proj, v_w_out, v_norm2_g, v_ffn_up, v_ffn_conv_w, v_ffn_down,
                                   v_final_g)))
    def strip(n, a):
        a = a if n == 'final_g' else a.reshape(a.shape[1:])
        return a.T if n in TRANSPOSED else a

    local = {n: strip(n, a) for n, a in given.items()}
    shard_shapes = {n: local[n].shape for n in SHARD_AXIS}
    sharded = BIG_SHARDED + SMALL_SHARDED

    late_names = [n for n in BIG_SHARDED if n != 'w_in']
    g_in = _all_gather_two_level(_row_pack([local['w_in']], BF16, 16), "gather_w_in")
    g_small = _all_gather_two_level(_flat_pack([local[n] for n in SMALL_SHARDED], F32, SUBLANES), "gather_small")
    late_pack, g_in, g_small = lax.optimization_barrier((_row_pack([local[n] for n in late_names], BF16, 16), g_in, g_small))
    send_sems, recv_sems, late_pack, landing, token = _gather_start(late_pack, "gather_late_start")
    full = _unpack_row_gathered(g_in, ['w_in'], shard_shapes)
    full.update(_unpack_gathered(g_small, SMALL_SHARDED, shard_shapes))
    rep = {n: local[n] for n in REPLICATED}
    rep['norm1_g'] = rep['norm1_g'] + token[0, 0]

    def late_weights(after):
        got = _gather_wait(send_sems, recv_sems, late_pack, landing, after, "gather_late_wait")
        me = 4 * lax.axis_index("x") + 2 * lax.axis_index("y") + lax.axis_index("c")
        slot = lax.broadcasted_iota(jnp.int32, (N_DEV, 1, 1), 0)
        got = jnp.where(slot == me, late_pack[None], got)
        return _prepare_late_weights(_unpack_row_gathered(got, late_names, shard_shapes))

    pushes = []
    me = 4 * lax.axis_index("x") + 2 * lax.axis_index("y") + lax.axis_index("c")
    slot = lax.broadcasted_iota(jnp.int32, (N_DEV, 1, 1), 0)

    def push_grads(group):
        names = list(group)
        slabs = [group[n].reshape(N_DEV, -1, group[n].shape[1]) for n in names]
        send_sems, recv_sems, slabs, landings, token = _slab_push_start(slabs, "grad_push_start_" + "_".join(names))
        pushes.append((names, send_sems, recv_sems, slabs, landings))
        return token[0, 0]

    loss_vec, grad_x, grads = _local_step(x[0], loss_target[0], _prepare_weights(full, rep), late_weights, push_grads)

    landed = {}
    for names, send_sems, recv_sems, slabs, landings in pushes:
        got = _slab_push_wait(send_sems, recv_sems, slabs, landings, (grad_x,), "grad_push_wait_" + "_".join(names))
        for n, slab, land in zip(names, slabs, got):
            landed[n] = (land, slab)

    small_sharded = ['rwkv_proj', 'gdn_proj'] + SMALL_SHARDED
    small_names = small_sharded + REPLICATED
    rep_vec = jnp.concatenate([grads[n].reshape(-1) for n in REPLICATED] + [loss_vec[0, 0:1]])
    slab_small = jnp.concatenate([_shard_major(grads[n], SHARD_AXIS[n]) for n in small_sharded] +
                                 [jnp.broadcast_to(rep_vec[None], (N_DEV, rep_vec.shape[0]))], axis=1)
    small_rows = -(-slab_small.shape[1] // (PACK_W * SUBLANES)) * SUBLANES
    slab_small = jnp.pad(slab_small, ((0, 0), (0, small_rows * PACK_W - slab_small.shape[1]))).reshape(N_DEV, small_rows, PACK_W)
    core = lax.axis_index("c").astype(jnp.int32).reshape(1)
    (from_sibling,) = _pair_exchange([slab_small], "grad_pair_exchange")
    chip_small = _pair_sum(slab_small, from_sibling, core, F32, "grad_pair_sum_small")
    (parts_small,) = _chip_exchange([chip_small], "grad_chip_exchange")

    def pack_local(src):
        flat = jnp.concatenate([strip(n, src[n]).reshape(-1) for n in small_names])
        return jnp.pad(flat, (0, small_rows * PACK_W - flat.shape[0])).reshape(small_rows, PACK_W)

    results = [({}, None) for _ in range(4)]
    me_arr = me.astype(jnp.int32).reshape(1)
    for n in ROW_SHARDED:
        packs = _reduce_landed_adamw(*landed[n], me_arr, local[n], strip(n, mom1[n]), strip(n, mom2[n]), "adamw_" + n)
        for (out, _), pk in zip(results, packs):
            out[n] = (pk.T if n in TRANSPOSED else pk).reshape(given[n].shape)
    packs = _reduce_adamw(parts_small, pack_local(given), pack_local(mom1), pack_local(mom2), "adamw_small")
    for i, pk in enumerate(packs):
        flat, off = pk.reshape(-1), 0
        for n in small_names:
            size = int(np.prod(given[n].shape))
            results[i][0][n] = flat[off:off + size].reshape(given[n].shape)
            off += size
        results[i] = (results[i][0], flat[off])
    (g_out, loss), (d_out, _), (m_out, _), (v_out, _) = results
    return (loss, grad_x[None], *[g_out[n] for n in WEIGHT_NAMES], *[d_out[n] for n in WEIGHT_NAMES],
            *[m_out[n] for n in WEIGHT_NAMES], *[v_out[n] for n in WEIGHT_NAMES])
```

```python
import functools

import jax
import jax.numpy as jnp
import numpy as np
from jax import lax
from jax.experimental import pallas as pl
from jax.experimental.pallas import tpu as pltpu

F32 = jnp.float32
BF16 = jnp.bfloat16

N_DEV = 8
D_MODEL = 1024
CHUNK = 64
RWKV_HEADS = 8
RWKV_HEAD_DIM = 64
RWKV_WIDTH = 512
GDN_HEADS = 4
GDN_HEAD_DIM = 128
GDN_WIDTH = 512
GDN_CONV = 4
FFN_HIDDEN = 2816
FFN_CONV = 3
NORM_EPS = 1e-6
L2_EPS = 1e-6
RWKV_GN_EPS = 64e-5
LANES = 128
SUBLANES = 8
VMEM_LIMIT = 56 * 1024 * 1024

ADAM_LR = 0.001
ADAM_B1 = 0.9
ADAM_B2 = 0.999
ADAM_EPS = 1e-08
ADAM_WD = 0.01
ADAM_STEP = 10

OFF_GA, OFF_GB, OFF_RKV, OFF_QKV, OFF_Z, OFF_LO, OFF_AB, CAT_W = 0, 1024, 2048, 5120, 6656, 7168, 7424, 7680
RWKV_HB = 8
RWKV_STEP_CHUNKS = 2
RWKV_TILE = RWKV_STEP_CHUNKS * CHUNK
GDN_HB = 4
GDN_STEP_CHUNKS = 4
GDN_TILE = GDN_STEP_CHUNKS * CHUNK

WEIGHT_NAMES = ['norm1_g', 'w_in', 'rwkv_mu', 'rwkv_w0', 'rwkv_w2', 'rwkv_a0', 'rwkv_a2', 'rwkv_g2', 'rwkv_k_k', 'rwkv_k_a',
                'rwkv_r_k', 'rwkv_ln_w', 'rwkv_ln_b', 'rwkv_proj', 'gdn_conv_w', 'gdn_a_log', 'gdn_dt_bias', 'gdn_norm_w',
                'gdn_proj', 'w_out', 'norm2_g', 'ffn_up', 'ffn_conv_w', 'ffn_down', 'final_g']
BIG_SHARDED = ['w_in', 'ffn_up', 'ffn_down', 'w_out', 'rwkv_proj', 'gdn_proj']
SMALL_SHARDED = ['rwkv_w2', 'rwkv_a2', 'rwkv_g2', 'gdn_conv_w', 'ffn_conv_w']
TRANSPOSED = ('w_in', 'ffn_up')
SHARD_AXIS = {'w_in': 0, 'ffn_up': 0, 'ffn_down': 0, 'w_out': 0, 'rwkv_proj': 1, 'gdn_proj': 1,
              'rwkv_w2': 1, 'rwkv_a2': 1, 'rwkv_g2': 1, 'gdn_conv_w': 1, 'ffn_conv_w': 1}
REPLICATED = [n for n in WEIGHT_NAMES if n not in SHARD_AXIS]
ROW_SHARDED = ['w_in', 'ffn_up', 'ffn_down', 'w_out']


def _cparams(sem=None):
    kw = dict(vmem_limit_bytes=VMEM_LIMIT)
    if sem is not None:
        kw['dimension_semantics'] = sem
    return pltpu.CompilerParams(**kw)


_NN, _NT, _TN = 'nn', 'nt', 'tn'
_DIMS_2D = {'nn': (((1,), (0,)), ((), ())), 'nt': (((1,), (1,)), ((), ())), 'tn': (((0,), (0,)), ((), ()))}
_DIMS_3D = {'nn': (((2,), (1,)), ((0,), (0,))), 'nt': (((2,), (2,)), ((0,), (0,))), 'tn': (((1,), (1,)), ((0,), (0,)))}


def _dg(a, b, kind):
    return lax.dot_general(a, b, (_DIMS_2D if a.ndim == 2 else _DIMS_3D)[kind], preferred_element_type=F32)


def _dot1(a, b, kind):
    return _dg(a.astype(BF16), b.astype(BF16), kind)


@jax.custom_vjp
def _dhi(a, b):
    return _dot1(a, b, _NN)


_dhi.defvjp(lambda a, b: (_dot1(a, b, _NN), (a, b)),
            lambda res, ct: (_dot1(ct, res[1], _NT), _dot1(res[0], ct, _TN)))


@jax.custom_vjp
def _dnt(a, b):
    return _dot1(a, b, _NT)


_dnt.defvjp(lambda a, b: (_dot1(a, b, _NT), (a, b)),
            lambda res, ct: (_dot1(ct, res[1], _NN), _dot1(ct, res[0], _TN)))


@jax.custom_vjp
def _dtn(a, b):
    return _dot1(a, b, _TN)


_dtn.defvjp(lambda a, b: (_dot1(a, b, _TN), (a, b)),
            lambda res, ct: (_dot1(res[1], ct, _NT), _dot1(res[0], ct, _NN)))


def _split3(x):
    x1 = x.astype(BF16)
    r1 = x - x1.astype(F32)
    x2 = r1.astype(BF16)
    return x1, x2, (r1 - x2.astype(F32)).astype(BF16)


def _dot_exact_lhs(sel, x, kind):
    parts = [_dg(sel, xi, kind) for xi in _split3(x)]
    return parts[0] + parts[1] + parts[2]


def _tril_ones(like):
    c = like.shape[-2]
    ri, ci = _iotas(c)
    return jnp.broadcast_to((ri >= ci).astype(BF16), like.shape[:-2] + (c, c))


@jax.custom_vjp
def _cumsum_rows(x):
    return _dot_exact_lhs(_tril_ones(x), x, _NN)


_cumsum_rows.defvjp(lambda x: (_dot_exact_lhs(_tril_ones(x), x, _NN), None),
                    lambda _, ct: (_dot_exact_lhs(_tril_ones(ct), ct, _TN),))


@jax.custom_vjp
def _lane_sum_as_row(x):
    return _dot_exact_lhs(jnp.ones(x.shape, BF16), x, _NT)


def _lane_sum_as_row_bwd(_, ct):
    ones = jnp.ones(ct.shape[:-1] + (LANES,), BF16)
    parts = [_dg(ci, ones, _TN) for ci in _split3(ct)]
    return (parts[0] + parts[1] + parts[2],)


_lane_sum_as_row.defvjp(lambda x: (_dot_exact_lhs(jnp.ones(x.shape, BF16), x, _NT), None), _lane_sum_as_row_bwd)


def _shift_rows(x, halo, s):
    rows = lax.broadcasted_iota(jnp.int32, x.shape, 0)
    out = pltpu.roll(x, s, 0)
    for i in range(s):
        out = jnp.where(rows == i, halo[SUBLANES - s + i:SUBLANES - s + i + 1, :], out)
    return out


def _unshift_rows(g, carry, s):
    c = g.shape[0]
    rows = lax.broadcasted_iota(jnp.int32, g.shape, 0)
    out = pltpu.roll(g, c - s, 0)
    for i in range(s):
        out = jnp.where(rows == c - s + i, carry[i:i + 1, :], out)
    return out


def _sigmoid_plain(z):
    return 1.0 / (1.0 + jnp.exp(-z))


def _sigmoid_value(z):
    t = jnp.exp(-jnp.abs(z))
    r = 1.0 / (1.0 + t)
    return jnp.where(z >= 0, r, t * r)


@jax.custom_vjp
def _sigmoid(z):
    return _sigmoid_value(z)


def _sigmoid_fwd(z):
    s = _sigmoid_value(z)
    return s, s


_sigmoid.defvjp(_sigmoid_fwd, lambda s, ct: (ct * s * (1.0 - s),))


def _silu(z):
    return z * _sigmoid(z)


def _softplus(z):
    return jnp.maximum(z, 0.0) + jnp.log(1.0 + jnp.exp(-jnp.abs(z)))


def _rms(t, gain):
    return t * lax.rsqrt(jnp.mean(t * t, axis=-1, keepdims=True) + NORM_EPS) * gain


def _iotas(c):
    return lax.broadcasted_iota(jnp.int32, (c, c), 0), lax.broadcasted_iota(jnp.int32, (c, c), 1)


def _unit_lower_inverse(xm, eye):
    t = eye + xm
    p = xm
    for _ in range(5):
        p = _dhi(p, p)
        t = t + _dhi(t, p)
    return t


def _rwkv_head(pr, pk, pv, plo, qr, qk, qv, qlo, s0, pp, mulo, wl):
    c = pr.shape[1]
    n_heads = s0.shape[0]
    n_chunks = pr.shape[0] // n_heads
    ri, ci = _iotas(c)
    if n_chunks > 1:
        pp = jnp.concatenate([pp] * n_chunks, axis=0)
        wl = jnp.concatenate([wl] * n_chunks, axis=0)

    def mix(p, q, mu):
        return p + (q - p) * mu

    r = mix(pr, qr, pp[:, 0:1])
    k = mix(pk, qk, pp[:, 1:2])
    v = mix(pv, qv, pp[:, 2:3])
    lo = mix(plo, qlo, mulo)
    w0, a0, k_k, k_a, ln_w, ln_b, r_k = (pp[:, i:i + 1] for i in range(3, 10))

    def per_head(t):
        return jnp.concatenate([jnp.broadcast_to(t[i], (n_heads,) + t.shape[1:]) for i in range(n_chunks)], axis=0)

    zw = _dhi(per_head(jnp.tanh(lo)), wl[:, 0])
    za = _dhi(per_head(lo), wl[:, 1])
    g = _dhi(per_head(_sigmoid(lo)), wl[:, 2])
    w_log = -_softplus(-(w0 + zw)) - 0.5
    lw = -jnp.exp(w_log)
    a = _sigmoid(a0 + za)
    kk = k * k_k
    kk = kk * lax.rsqrt(jnp.sum(kk * kk, axis=-1, keepdims=True) + L2_EPS)
    k2 = k * (1.0 + (a - 1.0) * k_a)
    an = -kk
    b = kk * a
    causal = ri >= ci
    strict = ri > ci
    eye = (ri == ci).astype(F32)
    cl = _cumsum_rows(lw)
    ecl = jnp.exp(-cl)
    at = an * jnp.exp(cl - lw)
    bt = b * ecl
    kt = k2 * ecl
    rt = r * jnp.exp(cl)
    a_ab = jnp.where(strict, _dnt(at, bt), 0.0)
    a_ak = jnp.where(strict, _dnt(at, kt), 0.0)
    tinv = _unit_lower_inverse(a_ab, eye)
    akv = _dhi(a_ak, v)
    r_b = jnp.where(causal, _dnt(rt, bt), 0.0)
    rkv = _dhi(jnp.where(causal, _dnt(rt, kt), 0.0), v)
    cl_end = jnp.sum(lw, axis=1, keepdims=True)
    dec_end = jnp.exp(cl_end - cl)
    b_end = b * dec_end
    sv = _dtn(v, k2 * dec_end)
    e_end = jnp.exp(cl_end)
    state, ys = s0, []
    for i in range(n_chunks):
        sl = slice(i * n_heads, (i + 1) * n_heads)
        u = _dhi(tinv[sl], _dnt(at[sl], state) + akv[sl])
        ys.append(_dnt(rt[sl], state) + _dhi(r_b[sl], u) + rkv[sl])
        state = state * e_end[sl] + _dtn(u, b_end[sl]) + sv[sl]
    y = jnp.concatenate(ys, axis=0) if n_chunks > 1 else ys[0]
    s1 = state
    m = (lax.broadcasted_iota(jnp.int32, (1, LANES), 1) < RWKV_HEAD_DIM).astype(F32)
    mean = jnp.sum(y, axis=-1, keepdims=True) * (1.0 / RWKV_HEAD_DIM)
    yc = (y - mean) * m
    var = jnp.sum(yc * yc, axis=-1, keepdims=True) * (1.0 / RWKV_HEAD_DIM)
    yn = yc * lax.rsqrt(var + RWKV_GN_EPS) * ln_w + ln_b
    y2 = yn + jnp.sum(r * k2 * r_k, axis=-1, keepdims=True) * v
    return y2 * g, s1


def _gdn_head(xq, xk, xv, z, ab, s0, cw, gp, oha, ohb):
    c = z.shape[1]
    n_heads = s0.shape[0]
    n_chunks = z.shape[0] // n_heads
    ri, ci = _iotas(c)
    cw = jnp.concatenate([cw] * n_chunks, axis=0) if n_chunks > 1 else cw

    def conv(xs, w):
        out = xs[0] * w[:, GDN_CONV - 1:GDN_CONV]
        for s in range(1, GDN_CONV):
            out = out + xs[s] * w[:, GDN_CONV - 1 - s:GDN_CONV - s]
        return out

    q = _silu(conv(xq, cw[:, 0]))
    k = _silu(conv(xk, cw[:, 1]))
    v = _silu(conv(xv, cw[:, 2]))
    q = q * lax.rsqrt(jnp.sum(q * q, axis=-1, keepdims=True) + L2_EPS) * (GDN_HEAD_DIM ** -0.5)
    k = k * lax.rsqrt(jnp.sum(k * k, axis=-1, keepdims=True) + L2_EPS)
    gg = -jnp.exp(gp[0:1]) * _softplus(ab + gp[1:2])
    beta = jnp.sum(_sigmoid(ab) * ohb, axis=-1, keepdims=True)
    causal = ri >= ci
    strict = ri > ci
    eye = (ri == ci).astype(F32)
    gcm = _cumsum_rows(gg * oha)
    gc = jnp.sum(gcm, axis=-1, keepdims=True)
    gc_row = _lane_sum_as_row(gcm)
    dec = jnp.where(causal, jnp.exp(jnp.where(causal, gc - gc_row, 0.0)), 0.0)
    kb = k * beta
    vb = v * beta
    lm = jnp.where(strict, _dnt(kb, k) * dec, 0.0)
    tinv = _unit_lower_inverse(-lm, eye)
    egc = jnp.exp(gc)
    u = _dhi(tinv, vb)
    wk = _dhi(tinv, kb * egc)
    attn = jnp.where(causal, _dnt(q, k) * dec, 0.0)
    g_last = gc[:, c - 1:c, :]
    q_dec = q * egc
    k_dec = k * jnp.exp(g_last - gc)
    e_last = jnp.exp(g_last)
    state, outs = s0, []
    for i in range(n_chunks):
        sl = slice(i * n_heads, (i + 1) * n_heads)
        v_new = u[sl] - _dhi(wk[sl], state)
        outs.append(_dhi(q_dec[sl], state) + _dhi(attn[sl], v_new))
        state = state * e_last[sl] + _dtn(k_dec[sl], v_new)
    o = jnp.concatenate(outs, axis=0) if n_chunks > 1 else outs[0]
    return _rms(o, gp[2:3]) * _silu(z), state


def _head_id(grp, i, per_step, heads):
    return i if per_step == heads else grp * per_step + i


def _head_range(grp, per_step, heads):
    return slice(None) if per_step == heads else pl.ds(grp * per_step, per_step)


def _rwkv_specs(nmap):
    hb, groups = RWKV_HB, RWKV_HEADS // RWKV_HB
    cb = OFF_RKV // (hb * LANES)
    specs = []
    for j in range(3):
        specs.append(pl.BlockSpec((RWKV_TILE, hb * LANES), lambda n, g, j=j: (nmap(n), cb + j * groups + g)))
    specs.append(pl.BlockSpec((RWKV_TILE, 2 * LANES), lambda n, g: (nmap(n), OFF_LO // (2 * LANES))))
    per = RWKV_TILE // SUBLANES
    for j in range(3):
        specs.append(pl.BlockSpec((SUBLANES, hb * LANES),
                                  lambda n, g, j=j: (jnp.maximum(nmap(n) * per - 1, 0), cb + j * groups + g)))
    specs.append(pl.BlockSpec((SUBLANES, 2 * LANES), lambda n, g: (jnp.maximum(nmap(n) * per - 1, 0), OFF_LO // (2 * LANES))))
    specs.append(pl.BlockSpec((hb, 16, LANES), lambda n, g: (g, 0, 0)))
    specs.append(pl.BlockSpec((1, 2 * LANES), lambda n, g: (0, 0)))
    specs.append(pl.BlockSpec((hb, 3, 2 * LANES, LANES), lambda n, g: (g, 0, 0, 0)))
    return specs


def _rwkv_operands(refs, halos, live):
    pr, pk, pv, plo = refs
    hr, hk, hv, hlo = halos
    cur, prev = [], []
    for x, hx in ((pr, hr), (pk, hk), (pv, hv)):
        tiles = [_lane_block(x, h) for h in range(RWKV_HB)]
        cur.append(_chunk_batch(tiles))
        prev.append(_chunk_batch([_shift_rows(t_, _lane_block(hx, h) * live, 1) for h, t_ in enumerate(tiles)]))
    lo = plo[...]
    cur.append(_chunk_batch([lo]))
    prev.append(_chunk_batch([_shift_rows(lo, hlo[...] * live, 1)]))
    return cur, prev


def _rwkv_forward(p_cat, ppack, mulo, wl):
    t = p_cat.shape[0]
    n_chunks = t // RWKV_TILE

    def body(pr, pk, pv, plo, hr, hk, hv, hlo, pp, ml, w, out, st_out, s_scr):
        n, grp = pl.program_id(0), pl.program_id(1)

        hsl = _head_range(grp, RWKV_HB, RWKV_HEADS)

        @pl.when(n == 0)
        def _():
            s_scr[hsl] = jnp.zeros((RWKV_HB, LANES, LANES), F32)

        live = (n > 0).astype(F32)
        cur, prev = _rwkv_operands((pr, pk, pv, plo), (hr, hk, hv, hlo), live)
        s0 = s_scr[hsl]
        st_out[...] = s0
        o, s1 = _rwkv_head(*cur, *prev, s0, pp[...], ml[...], w[...])
        for h, tile in enumerate(_head_tiles(o, RWKV_HB)):
            out[:, h * LANES:(h + 1) * LANES] = tile.astype(out.dtype)
        s_scr[hsl] = s1

    return pl.pallas_call(
        body, name="rwkv_fwd", grid=(n_chunks, RWKV_HEADS // RWKV_HB),
        in_specs=_rwkv_specs(lambda n: n),
        out_specs=(pl.BlockSpec((RWKV_TILE, RWKV_HB * LANES), lambda n, g: (n, g)),
                   pl.BlockSpec((None, RWKV_HB, LANES, LANES), lambda n, g: (n, g, 0, 0))),
        out_shape=(jax.ShapeDtypeStruct((t, RWKV_HEADS * LANES), BF16),
                   jax.ShapeDtypeStruct((n_chunks, RWKV_HEADS, LANES, LANES), F32)),
        scratch_shapes=[pltpu.VMEM((RWKV_HEADS, LANES, LANES), F32)],
        compiler_params=_cparams(("arbitrary", "arbitrary")),
    )(p_cat, p_cat, p_cat, p_cat, p_cat, p_cat, p_cat, p_cat, ppack, mulo, wl)


def _rwkv_backward(p_cat, ppack, mulo, wl, states, d_out):
    t = p_cat.shape[0]
    n_chunks = t // RWKV_TILE
    last = n_chunks - 1

    def body(pr, pk, pv, plo, hr, hk, hv, hlo, pp, ml, w, st, dy, dpr, dpk, dpv, dplo, dpp, dml, dw, ds_scr, car_scr, carlo_scr):
        n, grp = pl.program_id(0), pl.program_id(1)

        hsl = _head_range(grp, RWKV_HB, RWKV_HEADS)
        gi = _head_id(grp, 0, 1, RWKV_HEADS // RWKV_HB)

        @pl.when(n == 0)
        def _():
            ds_scr[hsl] = jnp.zeros((RWKV_HB, LANES, LANES), F32)
            car_scr[hsl] = jnp.zeros((RWKV_HB, 3 * SUBLANES, LANES), F32)
            carlo_scr[gi] = jnp.zeros((SUBLANES, 2 * LANES), F32)

        @pl.when((n == 0) & (grp == 0))
        def _():
            dpp[...] = jnp.zeros(dpp.shape, F32)
            dml[...] = jnp.zeros(dml.shape, F32)
            dw[...] = jnp.zeros(dw.shape, F32)

        live = (n < last).astype(F32)
        cur, prev = _rwkv_operands((pr, pk, pv, plo), (hr, hk, hv, hlo), live)
        _, vjp = jax.vjp(_rwkv_head, *cur, *prev, st[...], pp[...], ml[...], w[...])
        g = vjp((_chunk_batch([_lane_block(dy, h) for h in range(RWKV_HB)]), ds_scr[hsl]))
        outs = (dpr, dpk, dpv)
        d_cur = [_head_tiles(g[j], RWKV_HB) for j in range(3)]
        d_prev = [_head_tiles(g[4 + j], RWKV_HB) for j in range(3)]
        for i in range(RWKV_HB):
            sl = slice(i * LANES, (i + 1) * LANES)
            h = _head_id(grp, i, RWKV_HB, RWKV_HEADS)
            car = car_scr[h]
            for j in range(3):
                tot = d_cur[j][i] + _unshift_rows(d_prev[j][i], car[SUBLANES * j:SUBLANES * (j + 1), :], 1)
                outs[j][:, sl] = tot.astype(outs[j].dtype)
                car_scr[h, SUBLANES * j:SUBLANES * (j + 1), :] = d_prev[j][i][0:SUBLANES, :]
        (dlo_cur,), (dlo_prev,) = _head_tiles(g[3], 1), _head_tiles(g[7], 1)
        dlo = dlo_cur + _unshift_rows(dlo_prev, carlo_scr[gi], 1)
        carlo_scr[gi] = dlo_prev[0:SUBLANES, :]
        ds_scr[hsl] = g[8]
        dpp[hsl] += g[9]
        dml[0, 0:1, :] += g[10]
        dw[hsl] += g[11]

        @pl.when(grp == 0)
        def _():
            dplo[...] = dlo

        @pl.when(grp > 0)
        def _():
            dplo[...] += dlo

    rev = lambda n: last - n
    in_specs = _rwkv_specs(rev) + [
        pl.BlockSpec((None, RWKV_HB, LANES, LANES), lambda n, g: (rev(n), g, 0, 0)),
        pl.BlockSpec((RWKV_TILE, RWKV_HB * LANES), lambda n, g: (rev(n), g)),
    ]
    hw = RWKV_HEADS * LANES
    return pl.pallas_call(
        body, name="rwkv_bwd", grid=(n_chunks, RWKV_HEADS // RWKV_HB),
        in_specs=in_specs,
        out_specs=(pl.BlockSpec((RWKV_TILE, RWKV_HB * LANES), lambda n, g: (rev(n), g)),
                   pl.BlockSpec((RWKV_TILE, RWKV_HB * LANES), lambda n, g: (rev(n), g)),
                   pl.BlockSpec((RWKV_TILE, RWKV_HB * LANES), lambda n, g: (rev(n), g)),
                   pl.BlockSpec((RWKV_TILE, 2 * LANES), lambda n, h: (rev(n), 0)),
                   pl.BlockSpec((RWKV_HEADS, 16, LANES), lambda n, h: (0, 0, 0)),
                   pl.BlockSpec((RWKV_HEADS, SUBLANES, 2 * LANES), lambda n, h: (0, 0, 0)),
                   pl.BlockSpec((RWKV_HEADS, 3, 2 * LANES, LANES), lambda n, h: (0, 0, 0, 0))),
        out_shape=(jax.ShapeDtypeStruct((t, hw), BF16), jax.ShapeDtypeStruct((t, hw), BF16), jax.ShapeDtypeStruct((t, hw), BF16),
                   jax.ShapeDtypeStruct((t, 2 * LANES), F32),
                   jax.ShapeDtypeStruct((RWKV_HEADS, 16, LANES), F32),
                   jax.ShapeDtypeStruct((RWKV_HEADS, SUBLANES, 2 * LANES), F32),
                   jax.ShapeDtypeStruct((RWKV_HEADS, 3, 2 * LANES, LANES), F32)),
        scratch_shapes=[pltpu.VMEM((RWKV_HEADS, LANES, LANES), F32),
                        pltpu.VMEM((RWKV_HEADS, 3 * SUBLANES, LANES), F32),
                        pltpu.VMEM((RWKV_HEADS, SUBLANES, 2 * LANES), F32)],
        compiler_params=_cparams(("arbitrary", "arbitrary")),
    )(p_cat, p_cat, p_cat, p_cat, p_cat, p_cat, p_cat, p_cat, ppack, mulo, wl, states, d_out)


def _gdn_specs(nmap):
    per = GDN_TILE // SUBLANES
    hb, groups = GDN_HB, GDN_HEADS // GDN_HB
    cb = OFF_QKV // (hb * LANES)
    specs = []
    for j in range(3):
        specs.append(pl.BlockSpec((GDN_TILE, hb * LANES), lambda n, g, j=j: (nmap(n), cb + j * groups + g)))
    for j in range(3):
        specs.append(pl.BlockSpec((SUBLANES, hb * LANES),
                                  lambda n, g, j=j: (jnp.maximum(nmap(n) * per - 1, 0), cb + j * groups + g)))
    specs.append(pl.BlockSpec((GDN_TILE, hb * LANES), lambda n, g: (nmap(n), OFF_Z // (hb * LANES) + g)))
    specs.append(pl.BlockSpec((GDN_TILE, LANES), lambda n, g: (nmap(n), OFF_AB // LANES)))
    specs.append(pl.BlockSpec((hb, 3, SUBLANES, LANES), lambda n, g: (g, 0, 0, 0)))
    specs.append(pl.BlockSpec((SUBLANES, LANES), lambda n, g: (0, 0)))
    return specs


def _conv_taps(x, halo):
    return (x,) + tuple(_shift_rows(x, halo, s) for s in range(1, GDN_CONV))


def _onehots(grp):
    nb = GDN_STEP_CHUNKS * GDN_HB
    lane = lax.broadcasted_iota(jnp.int32, (nb, 1, LANES), 2)
    head = lax.broadcasted_iota(jnp.int32, (nb, 1, LANES), 0) % GDN_HB + _head_id(grp, 0, GDN_HB, GDN_HEADS)
    return (lane == head).astype(F32), (lane == GDN_HEADS + head).astype(F32)


def _chunk_batch(tiles):
    n_chunks = tiles[0].shape[0] // CHUNK
    return jnp.stack([t_[i * CHUNK:(i + 1) * CHUNK, :] for i in range(n_chunks) for t_ in tiles])


def _head_tiles(batch, n_heads=GDN_HB):
    n_chunks = batch.shape[0] // n_heads
    return [jnp.concatenate([batch[i * n_heads + h] for i in range(n_chunks)], axis=0) for h in range(n_heads)]


def _lane_block(ref, h):
    return ref[:, h * LANES:(h + 1) * LANES]


def _gdn_taps(refs, halos, live):
    out = []
    for x, hx in zip(refs, halos):
        per_head = [_conv_taps(_lane_block(x, h), _lane_block(hx, h) * live) for h in range(GDN_HB)]
        out.append(tuple(_chunk_batch([per_head[h][s] for h in range(GDN_HB)]) for s in range(GDN_CONV)))
    return out


def _gdn_forward(p_cat, cwpack, gpar):
    t = p_cat.shape[0]
    n_chunks = t // GDN_TILE

    def body(xq, xk, xv, hq, hk, hv, z, ab, cw, gp, out, st_out, s_scr):
        n, grp = pl.program_id(0), pl.program_id(1)

        hsl = _head_range(grp, GDN_HB, GDN_HEADS)

        @pl.when(n == 0)
        def _():
            s_scr[hsl] = jnp.zeros((GDN_HB, LANES, LANES), F32)

        live = (n > 0).astype(F32)
        oha, ohb = _onehots(grp)
        s0 = s_scr[hsl]
        st_out[...] = s0
        taps = _gdn_taps((xq, xk, xv), (hq, hk, hv), live)
        zb = _chunk_batch([_lane_block(z, h) for h in range(GDN_HB)])
        abb = _chunk_batch([ab[...]] * GDN_HB)
        o, s1 = _gdn_head(*taps, zb, abb, s0, cw[...], gp[...], oha, ohb)
        for h, tile in enumerate(_head_tiles(o)):
            out[:, h * LANES:(h + 1) * LANES] = tile.astype(out.dtype)
        s_scr[hsl] = s1

    return pl.pallas_call(
        body, name="gdn_fwd", grid=(n_chunks, GDN_HEADS // GDN_HB),
        in_specs=_gdn_specs(lambda n: n),
        out_specs=(pl.BlockSpec((GDN_TILE, GDN_HB * LANES), lambda n, g: (n, g)),
                   pl.BlockSpec((None, GDN_HB, LANES, LANES), lambda n, g: (n, g, 0, 0))),
        out_shape=(jax.ShapeDtypeStruct((t, GDN_WIDTH), BF16),
                   jax.ShapeDtypeStruct((n_chunks, GDN_HEADS, LANES, LANES), F32)),
        scratch_shapes=[pltpu.VMEM((GDN_HEADS, LANES, LANES), F32)],
        compiler_params=_cparams(("arbitrary", "arbitrary")),
    )(p_cat, p_cat, p_cat, p_cat, p_cat, p_cat, p_cat, p_cat, cwpack, gpar)


def _gdn_backward(p_cat, cwpack, gpar, states, d_out):
    t = p_cat.shape[0]
    n_chunks = t // GDN_TILE
    last = n_chunks - 1

    def body(xq, xk, xv, hq, hk, hv, z, ab, cw, gp, st, dy, dq, dk, dv, dz, dab, dcw, dgp, ds_scr, car_scr):
        n, grp = pl.program_id(0), pl.program_id(1)

        hsl = _head_range(grp, GDN_HB, GDN_HEADS)

        @pl.when(n == 0)
        def _():
            ds_scr[hsl] = jnp.zeros((GDN_HB, LANES, LANES), F32)
            car_scr[hsl] = jnp.zeros((GDN_HB, 3 * GDN_CONV, SUBLANES, LANES), F32)

        @pl.when((n == 0) & (grp == 0))
        def _():
            dcw[...] = jnp.zeros(dcw.shape, F32)
            dgp[...] = jnp.zeros(dgp.shape, F32)

        live = (n < last).astype(F32)
        oha, ohb = _onehots(grp)
        fn = functools.partial(_gdn_head, oha=oha, ohb=ohb)
        taps = _gdn_taps((xq, xk, xv), (hq, hk, hv), live)
        zb = _chunk_batch([_lane_block(z, h) for h in range(GDN_HB)])
        abb = _chunk_batch([ab[...]] * GDN_HB)
        _, vjp = jax.vjp(fn, *taps, zb, abb, st[...], cw[...], gp[...])
        g = vjp((_chunk_batch([_lane_block(dy, h) for h in range(GDN_HB)]), ds_scr[hsl]))
        outs = (dq, dk, dv)
        tap_tiles = [[_head_tiles(g[j][s]) for s in range(GDN_CONV)] for j in range(3)]
        dz_tiles = _head_tiles(g[3])
        for i in range(GDN_HB):
            sl = slice(i * LANES, (i + 1) * LANES)
            h = _head_id(grp, i, GDN_HB, GDN_HEADS)
            for j in range(3):
                tot = tap_tiles[j][0][i]
                for s in range(1, GDN_CONV):
                    slot = j * GDN_CONV + s
                    tot = tot + _unshift_rows(tap_tiles[j][s][i], car_scr[h, slot], s)
                    car_scr[h, slot] = tap_tiles[j][s][i][0:SUBLANES, :]
                outs[j][:, sl] = tot.astype(outs[j].dtype)
            dz[:, sl] = dz_tiles[i].astype(dz.dtype)
        dab_tiles = _head_tiles(g[4])
        dab_sum = dab_tiles[0]
        for h in range(1, GDN_HB):
            dab_sum = dab_sum + dab_tiles[h]
        ds_scr[hsl] = g[5]
        dcw[hsl] += g[6]
        dgp[0] += g[7]

        @pl.when(grp == 0)
        def _():
            dab[...] = dab_sum

        @pl.when(grp > 0)
        def _():
            dab[...] += dab_sum

    rev = lambda n: last - n
    in_specs = _gdn_specs(rev) + [
        pl.BlockSpec((None, GDN_HB, LANES, LANES), lambda n, g: (rev(n), g, 0, 0)),
        pl.BlockSpec((GDN_TILE, GDN_HB * LANES), lambda n, g: (rev(n), g)),
    ]
    blk = pl.BlockSpec((GDN_TILE, GDN_HB * LANES), lambda n, g: (rev(n), g))
    return pl.pallas_call(
        body, name="gdn_bwd", grid=(n_chunks, GDN_HEADS // GDN_HB),
        in_specs=in_specs,
        out_specs=(blk, blk, blk, blk,
                   pl.BlockSpec((GDN_TILE, LANES), lambda n, h: (rev(n), 0)),
                   pl.BlockSpec((GDN_HEADS, 3, SUBLANES, LANES), lambda n, h: (0, 0, 0, 0)),
                   pl.BlockSpec((GDN_HEADS, SUBLANES, LANES), lambda n, h: (0, 0, 0))),
        out_shape=(jax.ShapeDtypeStruct((t, GDN_WIDTH), BF16), jax.ShapeDtypeStruct((t, GDN_WIDTH), BF16),
                   jax.ShapeDtypeStruct((t, GDN_WIDTH), BF16), jax.ShapeDtypeStruct((t, GDN_WIDTH), BF16),
                   jax.ShapeDtypeStruct((t, LANES), F32),
                   jax.ShapeDtypeStruct((GDN_HEADS, 3, SUBLANES, LANES), F32),
                   jax.ShapeDtypeStruct((GDN_HEADS, SUBLANES, LANES), F32)),
        scratch_shapes=[pltpu.VMEM((GDN_HEADS, LANES, LANES), F32),
                        pltpu.VMEM((GDN_HEADS, 3 * GDN_CONV, SUBLANES, LANES), F32)],
        compiler_params=_cparams(("arbitrary", "arbitrary")),
    )(p_cat, p_cat, p_cat, p_cat, p_cat, p_cat, p_cat, p_cat, cwpack, gpar, states, d_out)


MM_VMEM_BUDGET = 44 * 1024 * 1024
MM_MIN_STEPS = 4


def _mm_tiles(mode, m, n, k, out_bytes):
    tms = [t for t in (2048, 1024, 768, 512, 256, 128, 64) if m % t == 0 and (mode != 'tn' or t % LANES == 0)]
    tns = [t for t in (1408, 1024, 768, 512, 256, 128) if n % t == 0]
    tks = [t for t in (2048, 1920, 1408, 1024, 512, 256, 128, 64) if k % t == 0]
    best, best_key = None, None
    for tm in tms:
        for tn in tns:
            for tk in tks:
                nk = k // tk
                vmem = 2 * (tm * tk * 2 + tk * tn * 2 + tm * tn * out_bytes) + (tm * tn * 4 if nk > 1 else 0)
                steps = (m // tm) * (n // tn) * nk
                if vmem > MM_VMEM_BUDGET:
                    continue
                reread = m * k * (n // tn) + k * n * (m // tm)
                key = (steps >= MM_MIN_STEPS, tn if mode == 'tn' else 0, tm * tn * tk, -nk, -reread)
                if best_key is None or key > best_key:
                    best, best_key = (tm, tn, tk), key
    if best is None:
        raise ValueError(f"no matmul tile for {mode} {m}x{n}x{k}")
    return best


_MM_DIMS = {'nn': (((1,), (0,)), ((), ())), 'nt': (((1,), (1,)), ((), ())), 'tn': (((0,), (0,)), ((), ()))}


def _matmul(a, b, mode, out_dtype, name):
    if mode == 'nn':
        (m, k), (k2, n) = a.shape, b.shape
    elif mode == 'nt':
        (m, k), (n, k2) = a.shape, b.shape
    else:
        (k, m), (k2, n) = a.shape, b.shape
    assert k == k2, (a.shape, b.shape, mode)
    tm, tn, tk = _mm_tiles(mode, m, n, k, jnp.dtype(out_dtype).itemsize)
    nk = k // tk
    dims = _MM_DIMS[mode]

    def body(a_ref, b_ref, o_ref, acc_ref):
        kk = pl.program_id(2)
        part = lax.dot_general(a_ref[...], b_ref[...], dims, preferred_element_type=F32)
        if nk == 1:
            o_ref[...] = part.astype(o_ref.dtype)
            return

        @pl.when(kk == 0)
        def _():
            acc_ref[...] = part

        @pl.when((kk > 0) & (kk < nk - 1))
        def _():
            acc_ref[...] += part

        @pl.when(kk == nk - 1)
        def _():
            o_ref[...] = (acc_ref[...] + part).astype(o_ref.dtype)

    a_spec = pl.BlockSpec((tk, tm), lambda i, j, kk: (kk, i)) if mode == 'tn' else pl.BlockSpec((tm, tk), lambda i, j, kk: (i, kk))
    b_spec = pl.BlockSpec((tn, tk), lambda i, j, kk: (j, kk)) if mode == 'nt' else pl.BlockSpec((tk, tn), lambda i, j, kk: (kk, j))
    return pl.pallas_call(
        body, name=name, grid=(m // tm, n // tn, nk),
        in_specs=[a_spec, b_spec],
        out_specs=pl.BlockSpec((tm, tn), lambda i, j, kk: (i, j)),
        out_shape=jax.ShapeDtypeStruct((m, n), out_dtype),
        scratch_shapes=[pltpu.VMEM((tm, tn), F32)],
        compiler_params=_cparams(("parallel", "parallel", "arbitrary")),
    )(a, b)


ROW_TILE = 512


def _row_specs(rows, tm):
    return [pl.BlockSpec((tm, w), lambda i, ci=ci: (i, ci)) for (_, w, ci) in rows]


def _rw_forward(fn, rows, pars, outs, name):
    t = rows[0][0].shape[0]
    tm = min(ROW_TILE, t)
    nr, npar = len(rows), len(pars)

    def body(*refs):
        vals = [r[...].astype(F32) for r in refs[:nr]] + [p[...] for p in refs[nr:nr + npar]]
        res = fn(*vals)
        for o, v in zip(refs[nr + npar:], res):
            o[...] = v.astype(o.dtype)

    return pl.pallas_call(
        body, name=name, grid=(t // tm,),
        in_specs=_row_specs(rows, tm) + [pl.BlockSpec(p.shape, lambda i: (0, 0)) for p in pars],
        out_specs=tuple(pl.BlockSpec((tm, w), lambda i: (i, 0)) for (w, _) in outs),
        out_shape=tuple(jax.ShapeDtypeStruct((t, w), dt) for (w, dt) in outs),
        compiler_params=_cparams(("parallel",)),
    )(*[r[0] for r in rows], *pars)


def _rw_backward(fn, rows, pars, cots, drow_dtypes, name):
    t = rows[0][0].shape[0]
    tm = min(ROW_TILE, t)
    nr, npar, nc = len(rows), len(pars), len(cots)
    keep = [i for i, dt in enumerate(drow_dtypes) if dt is not None]

    def body(*refs):
        vals = [r[...].astype(F32) for r in refs[:nr]] + [p[...] for p in refs[nr:nr + npar]]
        cvals = tuple(c[...].astype(F32) for c in refs[nr + npar:nr + npar + nc])
        orefs = refs[nr + npar + nc:]
        _, vjp = jax.vjp(fn, *vals)
        g = vjp(cvals)
        for o, i in zip(orefs[:len(keep)], keep):
            o[...] = g[i].astype(o.dtype)
        first = pl.program_id(0) == 0
        for o, gi in zip(orefs[len(keep):], g[nr:]):
            @pl.when(first)
            def _(o=o, gi=gi):
                o[...] = gi

            @pl.when(jnp.logical_not(first))
            def _(o=o, gi=gi):
                o[...] += gi

    out_specs = [pl.BlockSpec((tm, rows[i][1]), lambda i_: (i_, 0)) for i in keep] + \
                [pl.BlockSpec(p.shape, lambda i_: (0, 0)) for p in pars]
    out_shape = [jax.ShapeDtypeStruct((t, rows[i][1]), drow_dtypes[i]) for i in keep] + \
                [jax.ShapeDtypeStruct(p.shape, F32) for p in pars]
    return pl.pallas_call(
        body, name=name, grid=(t // tm,),
        in_specs=_row_specs(rows, tm) + [pl.BlockSpec(p.shape, lambda i: (0, 0)) for p in pars] + _row_specs(cots, tm),
        out_specs=tuple(out_specs), out_shape=tuple(out_shape),
        compiler_params=_cparams(("arbitrary",)),
    )(*[r[0] for r in rows], *pars, *[c[0] for c in cots])


def _norm_fn(x, g):
    return (_rms(x, g),)


def _norm_skip_fn(x, g):
    return _rms(x, g), x


def _merge_fn(ga, gb, ya, yb):
    return (_sigmoid(ga) * ya + _sigmoid(gb) * yb,)


def _res_norm_fn(x, mo, g):
    x1 = x + mo
    return x1, _rms(x1, g)


def _loss_head(x1, fo, gf, target, name):
    t, d = x1.shape
    tm = min(ROW_TILE, t)

    def tile_loss(x2, g, tgt):
        err = _rms(x2, g) - tgt
        per_row = jnp.sum(err * err, axis=-1, keepdims=True) * (0.5 / d)
        return jnp.sum(per_row, axis=0, keepdims=True)

    def body(x1_ref, fo_ref, g_ref, t_ref, loss_ref, dx_ref, dxb_ref, dg_ref):
        x2 = x1_ref[...] + fo_ref[...]
        val, vjp = jax.vjp(functools.partial(tile_loss, tgt=t_ref[...]), x2, g_ref[...])
        dx2, dg = vjp(jnp.ones((1, 1), F32))
        dx_ref[...] = dx2
        dxb_ref[...] = dx2.astype(BF16)
        first = pl.program_id(0) == 0

        @pl.when(first)
        def _():
            loss_ref[...] = jnp.broadcast_to(val, loss_ref.shape)
            dg_ref[...] = dg

        @pl.when(jnp.logical_not(first))
        def _():
            loss_ref[...] += jnp.broadcast_to(val, loss_ref.shape)
            dg_ref[...] += dg

    row = pl.BlockSpec((tm, d), lambda i: (i, 0))
    vec = pl.BlockSpec((1, d), lambda i: (0, 0))
    return pl.pallas_call(
        body, name=name, grid=(t // tm,),
        in_specs=[row, row, vec, row],
        out_specs=(pl.BlockSpec((1, LANES), lambda i: (0, 0)), row, row, vec),
        out_shape=(jax.ShapeDtypeStruct((1, LANES), F32), jax.ShapeDtypeStruct((t, d), F32),
                   jax.ShapeDtypeStruct((t, d), BF16), jax.ShapeDtypeStruct((1, d), F32)),
        compiler_params=_cparams(("arbitrary",)),
    )(x1, fo, gf, target)


FFN_TILE_ROWS = 2048
FFN_TILE_COLS = 256
FFN_COL_BLOCKS = FFN_HIDDEN // FFN_TILE_COLS


def _conv3_past(x, halo, w):
    rows = lax.broadcasted_iota(jnp.int32, x.shape, 0)
    x1 = jnp.where(rows == 0, halo[7:8, :], pltpu.roll(x, 1, 0))
    x2 = jnp.where(rows == 0, halo[6:7, :], jnp.where(rows == 1, halo[7:8, :], pltpu.roll(x, 2, 0)))
    return x * w[2:3] + x1 * w[1:2] + x2 * w[0:1], x1, x2


def _ffn_in_specs(tm, imap, jmap):
    per = tm // SUBLANES
    tile = lambda off: pl.BlockSpec((tm, FFN_TILE_COLS), lambda *g: (imap(*g), off + jmap(*g) % FFN_COL_BLOCKS))
    halo = lambda off: pl.BlockSpec((SUBLANES, FFN_TILE_COLS),
                                    lambda *g: (jnp.maximum(imap(*g) * per - 1, 0), off + jmap(*g) % FFN_COL_BLOCKS))
    wsp = lambda off: pl.BlockSpec((FFN_CONV, FFN_TILE_COLS), lambda *g: (0, off + jmap(*g) % FFN_COL_BLOCKS))
    return [tile(0), halo(0), wsp(0), tile(FFN_COL_BLOCKS), halo(FFN_COL_BLOCKS), wsp(FFN_COL_BLOCKS)]


def _ffn_act_forward(hpre, cw):
    t = hpre.shape[0]
    tm = min(FFN_TILE_ROWS, t)

    def body(hg, pg, wg, hu, pu, wu, out):
        live = (pl.program_id(0) > 0).astype(F32)
        cg, _, _ = _conv3_past(hg[...], pg[...] * live, wg[...])
        cu, _, _ = _conv3_past(hu[...], pu[...] * live, wu[...])
        out[...] = (cg * _sigmoid_plain(cg) * cu).astype(out.dtype)

    return pl.pallas_call(
        body, name="ffn_act_fwd", grid=(t // tm, FFN_COL_BLOCKS),
        in_specs=_ffn_in_specs(tm, lambda i, j: i, lambda i, j: j),
        out_specs=pl.BlockSpec((tm, FFN_TILE_COLS), lambda i, j: (i, j)),
        out_shape=jax.ShapeDtypeStruct((t, FFN_HIDDEN), BF16),
        compiler_params=_cparams(("parallel", "parallel")),
    )(hpre, hpre, cw, hpre, hpre, cw)


def _conv3_future(d, nxt, w):
    tm = d.shape[0]
    rows = lax.broadcasted_iota(jnp.int32, d.shape, 0)
    d1 = jnp.where(rows == tm - 1, nxt[0:1, :], pltpu.roll(d, tm - 1, 0))
    d2 = jnp.where(rows == tm - 1, nxt[1:2, :], jnp.where(rows == tm - 2, nxt[0:1, :], pltpu.roll(d, tm - 2, 0)))
    return d * w[2:3] + d1 * w[1:2] + d2 * w[0:1]


def _ffn_backward(hpre, cw, dact):
    t = hpre.shape[0]
    tm = min(FFN_TILE_ROWS, t)
    n_tiles = t // tm
    per = tm // SUBLANES

    def d_conv_out(cg, cu, d):
        s = _sigmoid_plain(cg)
        return d * cu * s * (1.0 + cg * (1.0 - s)), d * cg * s

    def body(hg, pg, ng, wg, hu, pu, nu, wu, da, dan, dhg, dhu, dwg, dwu):
        i = pl.program_id(1)
        live_prev = (i > 0).astype(F32)
        live_next = (i < n_tiles - 1).astype(F32)
        xg, xu = hg[...], hu[...]
        cg, g1, g2 = _conv3_past(xg, pg[...] * live_prev, wg[...])
        cu, u1, u2 = _conv3_past(xu, pu[...] * live_prev, wu[...])
        dg, du = d_conv_out(cg, cu, da[...])
        cgn, _, _ = _conv3_past(ng[...], hg[tm - SUBLANES:tm, :], wg[...])
        cun, _, _ = _conv3_past(nu[...], hu[tm - SUBLANES:tm, :], wu[...])
        dgn, dun = d_conv_out(cgn, cun, dan[...] * live_next)
        dhg[...] = _conv3_future(dg, dgn, wg[...]).astype(dhg.dtype)
        dhu[...] = _conv3_future(du, dun, wu[...]).astype(dhu.dtype)
        sums_g = [jnp.sum(xs * dg, axis=0, keepdims=True) for xs in (g2, g1, xg)]
        sums_u = [jnp.sum(xs * du, axis=0, keepdims=True) for xs in (u2, u1, xu)]

        @pl.when(i == 0)
        def _():
            for r_ in range(FFN_CONV):
                dwg[r_:r_ + 1, :] = sums_g[r_]
                dwu[r_:r_ + 1, :] = sums_u[r_]

        @pl.when(i > 0)
        def _():
            for r_ in range(FFN_CONV):
                dwg[r_:r_ + 1, :] += sums_g[r_]
                dwu[r_:r_ + 1, :] += sums_u[r_]

    nb = FFN_COL_BLOCKS
    nxt = lambda i: jnp.minimum((i + 1) * per, t // SUBLANES - 1)
    prv = lambda i: jnp.maximum(i * per - 1, 0)
    half = lambda off: [pl.BlockSpec((tm, FFN_TILE_COLS), lambda j, i: (i, off + j)),
                        pl.BlockSpec((SUBLANES, FFN_TILE_COLS), lambda j, i: (prv(i), off + j)),
                        pl.BlockSpec((SUBLANES, FFN_TILE_COLS), lambda j, i: (nxt(i), off + j)),
                        pl.BlockSpec((FFN_CONV, FFN_TILE_COLS), lambda j, i: (0, off + j))]
    tile = pl.BlockSpec((tm, FFN_TILE_COLS), lambda j, i: (i, j))
    taps = pl.BlockSpec((FFN_CONV, FFN_TILE_COLS), lambda j, i: (0, j))
    return pl.pallas_call(
        body, name="ffn_bwd", grid=(nb, n_tiles),
        in_specs=half(0) + half(nb) + [tile, pl.BlockSpec((SUBLANES, FFN_TILE_COLS), lambda j, i: (nxt(i), j))],
        out_specs=(tile, tile, taps, taps),
        out_shape=(jax.ShapeDtypeStruct((t, FFN_HIDDEN), BF16), jax.ShapeDtypeStruct((t, FFN_HIDDEN), BF16),
                   jax.ShapeDtypeStruct((FFN_CONV, FFN_HIDDEN), F32), jax.ShapeDtypeStruct((FFN_CONV, FFN_HIDDEN), F32)),
        compiler_params=_cparams(("parallel", "arbitrary")),
    )(hpre, hpre, hpre, cw, hpre, hpre, hpre, cw, dact, dact)


def _my_place():
    x, y, c = lax.axis_index("x"), lax.axis_index("y"), lax.axis_index("c")
    return x, y, c, 4 * x + 2 * y + c


N_CHIPS = 4


def _remote(src, dst, send_sem, recv_sem, dev):
    return pltpu.make_async_remote_copy(src_ref=src, dst_ref=dst, send_sem=send_sem, recv_sem=recv_sem, device_id=dev,
                                        device_id_type=pl.DeviceIdType.MESH)


def _chip_peer(x, y, k):
    return x ^ ((k >> 1) & 1), y ^ (k & 1)


def _all_gather_two_level(shard, name):
    r, w = shard.shape

    def body(src, out, send_sems, recv_sems, local_sem):
        x, y, c, me = _my_place()
        sibling = (x, y, 1 - c)
        mine = pltpu.make_async_copy(src, out.at[me], local_sem)
        mine.start()
        first = [_remote(src, out.at[me], send_sems.at[0], recv_sems.at[0], sibling)]
        for k in range(1, N_CHIPS):
            px, py = _chip_peer(x, y, k)
            first.append(_remote(src, out.at[me], send_sems.at[k], recv_sems.at[k], (px, py, c)))
        for cp in first:
            cp.start()
        passed = []
        for k in range(1, N_CHIPS):
            px, py = _chip_peer(x, y, k)
            landed = out.at[me ^ (2 * k)]
            _remote(src, landed, send_sems.at[k], recv_sems.at[k], (px, py, c)).wait_recv()
            fwd = _remote(landed, landed, send_sems.at[N_CHIPS - 1 + k], recv_sems.at[N_CHIPS - 1 + k], sibling)
            fwd.start()
            passed.append(fwd)
        _remote(src, out.at[me ^ 1], send_sems.at[0], recv_sems.at[0], sibling).wait_recv()
        for k in range(1, N_CHIPS):
            got = out.at[(me ^ 1) ^ (2 * k)]
            _remote(got, got, send_sems.at[N_CHIPS - 1 + k], recv_sems.at[N_CHIPS - 1 + k], sibling).wait_recv()
        for cp in first + passed:
            cp.wait_send()
        mine.wait()

    return pl.pallas_call(
        body, name=name,
        in_specs=[pl.BlockSpec(memory_space=pl.ANY)],
        out_specs=pl.BlockSpec(memory_space=pl.ANY),
        out_shape=jax.ShapeDtypeStruct((N_DEV, r, w), shard.dtype),
        scratch_shapes=[pltpu.SemaphoreType.DMA((N_DEV - 1,)), pltpu.SemaphoreType.DMA((N_DEV - 1,)), pltpu.SemaphoreType.DMA],
    )(shard)


def _device_peer(x, y, c, k):
    px, py, pc = x ^ ((k >> 2) & 1), y ^ ((k >> 1) & 1), c ^ (k & 1)
    return (px, py, pc), 4 * px + 2 * py + pc


_HBM = pl.BlockSpec(memory_space=pltpu.HBM)
_SEM = pl.BlockSpec(memory_space=pltpu.SEMAPHORE)


def _gather_start(shard, name):
    def body(src, land, send_sems, recv_sems, src_thru, land_thru, token):
        x, y, c, me = _my_place()
        for k in range(1, N_DEV):
            dev, _ = _device_peer(x, y, c, k)
            _remote(src, land.at[me], send_sems.at[k], recv_sems.at[k], dev).start()
        token[...] = jnp.zeros_like(token)

    landing = lax.empty((N_DEV,) + shard.shape, shard.dtype)
    return pl.pallas_call(
        body, name=name,
        out_shape=(pltpu.SemaphoreType.DMA((N_DEV,)), pltpu.SemaphoreType.DMA((N_DEV,)), pltpu.HBM(shard.shape, shard.dtype),
                   pltpu.HBM(landing.shape, landing.dtype), jax.ShapeDtypeStruct((SUBLANES, LANES), F32)),
        in_specs=(_HBM, _HBM), out_specs=(_SEM, _SEM, _HBM, _HBM, pl.BlockSpec(memory_space=pltpu.VMEM)),
        input_output_aliases={0: 2, 1: 3},
        compiler_params=pltpu.CompilerParams(has_side_effects=pltpu.SideEffectType.DATAFLOW_SIDE_EFFECTING),
    )(pltpu.with_memory_space_constraint(shard, pltpu.HBM), pltpu.with_memory_space_constraint(landing, pltpu.HBM))


def _gather_wait(send_sems, recv_sems, shard, landing, after, name):
    n_after = len(after)

    def body(*refs):
        src, land, send_sems, recv_sems = refs[:4]
        x, y, c, _ = _my_place()
        for k in range(1, N_DEV):
            dev, idx = _device_peer(x, y, c, k)
            cp = _remote(src, land.at[idx], send_sems.at[k], recv_sems.at[k], dev)
            cp.wait_send()
            cp.wait_recv()

    return pl.pallas_call(
        body, name=name,
        out_shape=(pltpu.HBM(shard.shape, shard.dtype), pltpu.HBM(landing.shape, landing.dtype)),
        in_specs=(_HBM, _HBM, _SEM, _SEM) + (pl.BlockSpec(memory_space=pl.ANY),) * n_after, out_specs=(_HBM, _HBM),
        input_output_aliases={0: 0, 1: 1},
        compiler_params=pltpu.CompilerParams(has_side_effects=pltpu.SideEffectType.DATAFLOW_SIDE_EFFECTING),
    )(shard, landing, send_sems, recv_sems, *after)[1]


def _slab_push_start(slabs, name):
    na = len(slabs)

    def body(*refs):
        srcs, lands = refs[:na], refs[na:2 * na]
        send_sems, recv_sems = refs[2 * na], refs[2 * na + 1]
        token = refs[-1]
        x, y, c, me = _my_place()
        for i in range(na):
            for k in range(1, N_DEV):
                dev, idx = _device_peer(x, y, c, k)
                s = i * N_DEV + k
                _remote(srcs[i].at[idx], lands[i].at[me], send_sems.at[s], recv_sems.at[s], dev).start()
        token[...] = jnp.zeros_like(token)

    hbm_shapes = [pltpu.HBM(a.shape, a.dtype) for a in slabs]
    ins = [pltpu.with_memory_space_constraint(a, pltpu.HBM) for a in slabs]
    ins += [pltpu.with_memory_space_constraint(lax.empty(a.shape, a.dtype), pltpu.HBM) for a in slabs]
    out = pl.pallas_call(
        body, name=name,
        out_shape=(pltpu.SemaphoreType.DMA((na * N_DEV,)), pltpu.SemaphoreType.DMA((na * N_DEV,)), *hbm_shapes, *hbm_shapes,
                   jax.ShapeDtypeStruct((SUBLANES, LANES), F32)),
        in_specs=(_HBM,) * (2 * na), out_specs=(_SEM, _SEM) + (_HBM,) * (2 * na) + (pl.BlockSpec(memory_space=pltpu.VMEM),),
        input_output_aliases={i: 2 + i for i in range(2 * na)},
        compiler_params=pltpu.CompilerParams(has_side_effects=pltpu.SideEffectType.DATAFLOW_SIDE_EFFECTING),
    )(*ins)
    return out[0], out[1], list(out[2:2 + na]), list(out[2 + na:2 + 2 * na]), out[-1]


def _slab_push_wait(send_sems, recv_sems, slabs, landings, after, name):
    na = len(slabs)

    def body(*refs):
        srcs, lands = refs[:na], refs[na:2 * na]
        send_sems, recv_sems = refs[2 * na], refs[2 * na + 1]
        x, y, c, me = _my_place()
        for i in range(na):
            for k in range(1, N_DEV):
                dev, idx = _device_peer(x, y, c, k)
                s = i * N_DEV + k
                cp = _remote(srcs[i].at[idx], lands[i].at[idx], send_sems.at[s], recv_sems.at[s], dev)
                cp.wait_send()
                cp.wait_recv()

    hbm_shapes = tuple(pltpu.HBM(a.shape, a.dtype) for a in slabs)
    out = pl.pallas_call(
        body, name=name, out_shape=hbm_shapes + hbm_shapes,
        in_specs=(_HBM,) * (2 * na) + (_SEM, _SEM) + (pl.BlockSpec(memory_space=pl.ANY),) * len(after),
        out_specs=(_HBM,) * (2 * na), input_output_aliases={i: i for i in range(2 * na)},
        compiler_params=pltpu.CompilerParams(has_side_effects=pltpu.SideEffectType.DATAFLOW_SIDE_EFFECTING),
    )(*slabs, *landings, send_sems, recv_sems, *after)
    return list(out[na:])


def _pair_exchange(arrays, name):
    na = len(arrays)

    def body(*refs):
        srcs, dsts, (send_sems, recv_sems) = refs[:na], refs[na:2 * na], refs[2 * na:]
        x, y, c, _ = _my_place()
        sibling = (x, y, 1 - c)
        copies = []
        for i in range(na):
            for q in range(N_CHIPS):
                s = i * N_CHIPS + q
                copies.append(_remote(srcs[i].at[2 * q + 1 - c], dsts[i].at[q], send_sems.at[s], recv_sems.at[s], sibling))
        for cp in copies:
            cp.start()
        for cp in copies:
            cp.wait_recv()
        for cp in copies:
            cp.wait_send()

    hbm = pl.BlockSpec(memory_space=pl.ANY)
    return pl.pallas_call(
        body, name=name, in_specs=[hbm] * na, out_specs=tuple([hbm] * na),
        out_shape=tuple(jax.ShapeDtypeStruct((N_CHIPS,) + a.shape[1:], a.dtype) for a in arrays),
        scratch_shapes=[pltpu.SemaphoreType.DMA((na * N_CHIPS,)), pltpu.SemaphoreType.DMA((na * N_CHIPS,))],
    )(*arrays)


ELEMENTWISE_COLS = 512


def _pair_sum(slabs, recv, core, out_dtype, name):
    _, r, w = slabs.shape
    tc = ELEMENTWISE_COLS

    def body(core_ref, mine, theirs, out):
        out[...] = (mine[...] + theirs[...]).astype(out.dtype)

    grid_spec = pltpu.PrefetchScalarGridSpec(
        num_scalar_prefetch=1, grid=(N_CHIPS, w // tc),
        in_specs=[pl.BlockSpec((None, r, tc), lambda q, j, core_ref: (2 * q + core_ref[0], 0, j)),
                  pl.BlockSpec((None, r, tc), lambda q, j, core_ref: (q, 0, j))],
        out_specs=pl.BlockSpec((None, r, tc), lambda q, j, core_ref: (q, 0, j)))
    return pl.pallas_call(body, name=name, grid_spec=grid_spec,
                          out_shape=jax.ShapeDtypeStruct((N_CHIPS, r, w), out_dtype),
                          compiler_params=_cparams(("parallel", "parallel")))(core, slabs, recv)


def _chip_exchange(arrays, name):
    na = len(arrays)

    def body(*refs):
        srcs, dsts, (send_sems, recv_sems, local_sems) = refs[:na], refs[na:2 * na], refs[2 * na:]
        x, y, c, _ = _my_place()
        chip = 2 * x + y
        own = [pltpu.make_async_copy(srcs[i].at[chip], dsts[i].at[chip], local_sems.at[i]) for i in range(na)]
        for cp in own:
            cp.start()
        sends, arrivals = [], []
        for i in range(na):
            for k in range(1, N_CHIPS):
                px, py = _chip_peer(x, y, k)
                s = i * N_CHIPS + k
                sends.append(_remote(srcs[i].at[chip ^ k], dsts[i].at[chip], send_sems.at[s], recv_sems.at[s], (px, py, c)))
                arrivals.append(_remote(srcs[i].at[chip], dsts[i].at[chip ^ k], send_sems.at[s], recv_sems.at[s], (px, py, c)))
        for cp in sends:
            cp.start()
        for cp in arrivals:
            cp.wait_recv()
        for cp in sends:
            cp.wait_send()
        for cp in own:
            cp.wait()

    hbm = pl.BlockSpec(memory_space=pl.ANY)
    return pl.pallas_call(
        body, name=name, in_specs=[hbm] * na, out_specs=tuple([hbm] * na),
        out_shape=tuple(jax.ShapeDtypeStruct(a.shape, a.dtype) for a in arrays),
        scratch_shapes=[pltpu.SemaphoreType.DMA((na * N_CHIPS,)), pltpu.SemaphoreType.DMA((na * N_CHIPS,)),
                        pltpu.SemaphoreType.DMA((na,))],
    )(*arrays)


def _adamw_update(g, w, m, v):
    c1 = 1.0 / (1.0 - ADAM_B1 ** ADAM_STEP)
    c2 = 1.0 / (1.0 - ADAM_B2 ** ADAM_STEP)
    mn = ADAM_B1 * m + (1.0 - ADAM_B1) * g
    vn = ADAM_B2 * v + (1.0 - ADAM_B2) * (g * g)
    return -ADAM_LR * ((mn * c1) / (jnp.sqrt(vn * c2) + ADAM_EPS) + ADAM_WD * w), mn, vn


def _reduce_adamw(parts, w, m, v, name):
    n_parts, r, wd = parts.shape
    tc = ELEMENTWISE_COLS

    def body(p_ref, w_ref, m_ref, v_ref, g_out, d_out, m_out, v_out):
        g = p_ref[0].astype(F32)
        for s in range(1, n_parts):
            g = g + p_ref[s].astype(F32)
        g_out[...] = g
        d_out[...], m_out[...], v_out[...] = _adamw_update(g, w_ref[...], m_ref[...], v_ref[...])

    blk = pl.BlockSpec((r, tc), lambda j: (0, j))
    shp = jax.ShapeDtypeStruct((r, wd), F32)
    return pl.pallas_call(
        body, name=name, grid=(wd // tc,),
        in_specs=[pl.BlockSpec((n_parts, r, tc), lambda j: (0, 0, j)), blk, blk, blk],
        out_specs=(blk, blk, blk, blk), out_shape=(shp, shp, shp, shp),
        compiler_params=_cparams(("parallel",)),
    )(parts, w, m, v)


def _reduce_landed_adamw(landing, own, me, w, m, v, name):
    n_parts, r, wd = landing.shape
    tc = ELEMENTWISE_COLS

    def body(me_ref, land_ref, own_ref, w_ref, m_ref, v_ref, g_out, d_out, m_out, v_out):
        mine = own_ref[...].astype(F32)
        g = None
        for s in range(n_parts):
            part = jnp.where(me_ref[0] == s, mine, land_ref[s].astype(F32))
            g = part if g is None else g + part
        g_out[...] = g
        d_out[...], m_out[...], v_out[...] = _adamw_update(g, w_ref[...], m_ref[...], v_ref[...])

    blk = pl.BlockSpec((r, tc), lambda j, me_ref: (0, j))
    shp = jax.ShapeDtypeStruct((r, wd), F32)
    grid_spec = pltpu.PrefetchScalarGridSpec(
        num_scalar_prefetch=1, grid=(wd // tc,),
        in_specs=[pl.BlockSpec((n_parts, r, tc), lambda j, me_ref: (0, 0, j)),
                  pl.BlockSpec((None, r, tc), lambda j, me_ref: (me_ref[0], 0, j)), blk, blk, blk],
        out_specs=(blk, blk, blk, blk))
    return pl.pallas_call(body, name=name, grid_spec=grid_spec, out_shape=(shp, shp, shp, shp),
                          compiler_params=_cparams(("parallel",)))(me, landing, own, w, m, v)


PACK_W = 1024


def _pad_heads(a, slots):
    lead = a.shape[:-1]
    a = a.reshape(lead + (slots, RWKV_HEAD_DIM))
    a = jnp.pad(a, [(0, 0)] * (len(lead) + 1) + [(0, LANES - RWKV_HEAD_DIM)])
    return a.reshape(lead + (slots * LANES,))


def _flat_pack(arrs, dtype, row_mult):
    flat = jnp.concatenate([a.reshape(-1).astype(dtype) for a in arrs])
    n = flat.shape[0]
    rows = -(-n // PACK_W)
    rows = -(-rows // row_mult) * row_mult
    return jnp.pad(flat, (0, rows * PACK_W - n)).reshape(rows, PACK_W)


def _row_pack(arrs, dtype, row_mult):
    parts = [a.astype(dtype) if a.shape[1] == PACK_W else a.astype(dtype).reshape(-1, PACK_W) for a in arrs]
    rows = sum(p.shape[0] for p in parts)
    pad = -(-rows // row_mult) * row_mult - rows
    return jnp.concatenate(parts + ([jnp.zeros((pad, PACK_W), dtype)] if pad else []), axis=0)


def _unpack_row_gathered(g, names, shard_shapes):
    out, r0 = {}, 0
    for n in names:
        s = shard_shapes[n]
        rows = s[0] * s[1] // PACK_W
        seg = g[:, r0:r0 + rows, :]
        r0 += rows
        if s[1] == PACK_W:
            assert SHARD_AXIS[n] == 0
            out[n] = seg.reshape(N_DEV * s[0], s[1])
        else:
            assert SHARD_AXIS[n] == 1
            out[n] = jnp.transpose(seg.reshape((N_DEV,) + tuple(s)), (1, 0, 2)).reshape(s[0], N_DEV * s[1])
    return out


def _unpack_gathered(g, names, shard_shapes):
    flat = g.reshape(N_DEV, -1)
    out, off = {}, 0
    for n in names:
        s = shard_shapes[n]
        size = s[0] * s[1]
        seg = flat[:, off:off + size].reshape((N_DEV,) + tuple(s))
        off += size
        if SHARD_AXIS[n] == 1:
            out[n] = jnp.transpose(seg, (1, 0, 2)).reshape(s[0], N_DEV * s[1])
        else:
            out[n] = seg.reshape(N_DEV * s[0], s[1])
    return out


def _shard_major(full, axis):
    a, b = full.shape
    if axis == 1:
        return jnp.transpose(full.reshape(a, N_DEV, b // N_DEV), (1, 0, 2)).reshape(N_DEV, -1)
    return full.reshape(N_DEV, -1)


def _prepare_weights(full, rep):
    w = full['w_in']
    d = w.shape[1]
    rkv = jnp.pad(w[0:1536].reshape(3 * RWKV_HEADS, RWKV_HEAD_DIM, d), ((0, 0), (0, LANES - RWKV_HEAD_DIM), (0, 0)))
    w_cat = jnp.concatenate([
        w[3848:4872], w[4872:5896], rkv.reshape(3 * RWKV_HEADS * LANES, d), w[1792:3328], w[3328:3840],
        w[1536:1792], jnp.pad(w[3840:3848], ((0, LANES - 8), (0, 0))), jnp.zeros((LANES, d), w.dtype)], axis=0)
    assert w_cat.shape[0] == CAT_W
    mu = rep['rwkv_mu']
    vecs = [mu[0:512], mu[512:1024], mu[1024:1536], rep['rwkv_w0'], rep['rwkv_a0'], rep['rwkv_k_k'], rep['rwkv_k_a'],
            rep['rwkv_ln_w'], rep['rwkv_ln_b'], rep['rwkv_r_k'].reshape(-1)]
    ppack = jnp.stack([jnp.pad(v.reshape(RWKV_HEADS, RWKV_HEAD_DIM), ((0, 0), (0, LANES - RWKV_HEAD_DIM))) for v in vecs], axis=1)
    ppack = jnp.pad(ppack, ((0, 0), (0, 16 - len(vecs)), (0, 0)))
    mulo = mu[1536:1792].reshape(1, 2 * LANES)
    wl = jnp.zeros((3, 2 * LANES, RWKV_HEADS * LANES), F32)
    wl = wl.at[0, 0:64].set(_pad_heads(full['rwkv_w2'], RWKV_HEADS))
    wl = wl.at[1, 64:128].set(_pad_heads(full['rwkv_a2'], RWKV_HEADS))
    wl = wl.at[2, 128:256].set(_pad_heads(full['rwkv_g2'], RWKV_HEADS))
    wl = jnp.transpose(wl.reshape(3, 2 * LANES, RWKV_HEADS, LANES), (2, 0, 1, 3))
    cw = full['gdn_conv_w'].reshape(GDN_CONV, 3, GDN_HEADS, LANES)
    cwpack = jnp.pad(jnp.transpose(cw, (2, 1, 0, 3)), ((0, 0), (0, 0), (0, SUBLANES - GDN_CONV), (0, 0)))
    gpar = jnp.zeros((SUBLANES, LANES), F32)
    gpar = gpar.at[0, 0:GDN_HEADS].set(rep['gdn_a_log']).at[1, 0:GDN_HEADS].set(rep['gdn_dt_bias']).at[2].set(rep['gdn_norm_w'])
    return dict(w_cat=w_cat, ffn_cw=full['ffn_conv_w'], ppack=ppack, mulo=mulo, wl=wl, cwpack=cwpack, gpar=gpar,
                g1=rep['norm1_g'].reshape(1, -1), g2=rep['norm2_g'].reshape(1, -1), gf=rep['final_g'].reshape(1, -1))


def _prepare_late_weights(full):
    rp = full['rwkv_proj']
    rproj = jnp.pad(rp.reshape(RWKV_HEADS, RWKV_HEAD_DIM, -1), ((0, 0), (0, LANES - RWKV_HEAD_DIM), (0, 0))).reshape(RWKV_HEADS * LANES, -1)
    return dict(rproj=rproj, gproj=full['gdn_proj'], w_out=full['w_out'], ffn_up=full['ffn_up'], ffn_down=full['ffn_down'])


def _local_step(x, target, p, late_weights, push_grads):
    d = x.shape[1]
    full_w = lambda a: (a, a.shape[1], 0)
    (u,) = _rw_forward(_norm_fn, [full_w(x)], [p['g1']], [(d, BF16)], "norm1")
    p_cat = _matmul(u, p['w_cat'], 'nt', F32, "proj_in")
    ya_pre, st_r = _rwkv_forward(p_cat, p['ppack'], p['mulo'], p['wl'])
    yb_pre, st_g = _gdn_forward(p_cat, p['cwpack'], p['gpar'])
    p = {**p, **late_weights((ya_pre, yb_pre))}
    ya = _matmul(ya_pre, p['rproj'], 'nn', F32, "rwkv_proj")
    yb = _matmul(yb_pre, p['gproj'], 'nn', F32, "gdn_proj")
    gates = [(p_cat, d, OFF_GA // d), (p_cat, d, OFF_GB // d)]
    (mixed,) = _rw_forward(_merge_fn, gates + [full_w(ya), full_w(yb)], [], [(d, BF16)], "merge")
    mo = _matmul(mixed, p['w_out'], 'nn', F32, "out_proj")
    x1, n2 = _rw_forward(_res_norm_fn, [full_w(x), full_w(mo)], [p['g2']], [(d, F32), (d, BF16)], "res_norm2")
    hpre = _matmul(n2, p['ffn_up'], 'nt', F32, "ffn_up")
    act = _ffn_act_forward(hpre, p['ffn_cw'])
    fo = _matmul(act, p['ffn_down'], 'nn', F32, "ffn_down")
    loss_vec, dx2, dx2b, dgf = _loss_head(x1, fo, p['gf'], target, "loss_head")

    dact = _matmul(dx2b, p['ffn_down'], 'nt', F32, "d_act")
    dw_down = _matmul(act, dx2b, 'tn', BF16, "dw_ffn_down")
    dh_gate, dh_up, dcw_gate, dcw_up = _ffn_backward(hpre, p['ffn_cw'], dact)
    dh = jnp.concatenate([dh_gate, dh_up], axis=1)
    dcw_f = jnp.concatenate([dcw_gate, dcw_up], axis=1)
    dn2 = _matmul(dh, p['ffn_up'], 'nn', F32, "d_norm2")
    dw_up = _matmul(dh, n2, 'tn', BF16, "dw_ffn_up")
    token = push_grads({'ffn_down': dw_down, 'ffn_up': dw_up})
    dx1, dx1b, dg2 = _rw_backward(_res_norm_fn, [full_w(x), full_w(mo)], [p['g2'] + token], [full_w(dx2), full_w(dn2)],
                                  [F32, BF16], "res_norm2_bwd")
    dmixed = _matmul(dx1b, p['w_out'], 'nt', F32, "d_mixed")
    dw_out = _matmul(mixed, dx1b, 'tn', BF16, "dw_out")
    dga, dgb, dya, dyb = _rw_backward(_merge_fn, gates + [full_w(ya), full_w(yb)], [], [full_w(dmixed)],
                                      [BF16, BF16, BF16, BF16], "merge_bwd")
    d_ya_pre = _matmul(dya, p['rproj'], 'nt', F32, "d_rwkv_out")
    dw_rproj = _matmul(ya_pre, dya, 'tn', F32, "dw_rwkv_proj")
    d_yb_pre = _matmul(dyb, p['gproj'], 'nt', F32, "d_gdn_out")
    dw_gproj = _matmul(yb_pre, dyb, 'tn', F32, "dw_gdn_proj")
    dpr, dpk, dpv, dplo, dpp, dml, dwl = _rwkv_backward(p_cat, p['ppack'], p['mulo'], p['wl'], st_r, d_ya_pre)
    dq, dk, dv, dz, dab, dcw_g, dgp = _gdn_backward(p_cat, p['cwpack'], p['gpar'], st_g, d_yb_pre)
    t = x.shape[0]
    dp_cat = jnp.concatenate([dga, dgb, dpr, dpk, dpv, dq, dk, dv, dz, dplo.astype(BF16), dab.astype(BF16),
                              jnp.zeros((t, LANES), BF16)], axis=1)
    dw_cat = _matmul(dp_cat, u, 'tn', BF16, "dw_in")
    dw_in = jnp.concatenate([dw_cat[OFF_RKV:OFF_QKV].reshape(3 * RWKV_HEADS, LANES, d)[:, :RWKV_HEAD_DIM].reshape(-1, d),
                             dw_cat[OFF_LO:OFF_AB], dw_cat[OFF_QKV:OFF_Z], dw_cat[OFF_Z:OFF_LO], dw_cat[OFF_AB:OFF_AB + 8],
                             dw_cat[OFF_GA:OFF_GB], dw_cat[OFF_GB:OFF_RKV]], axis=0)
    token = push_grads({'w_out': dw_out, 'w_in': dw_in})
    du = _matmul(dp_cat, p['w_cat'], 'nn', F32, "d_norm1")
    grad_x, dg1 = _rw_backward(_norm_skip_fn, [full_w(x)], [p['g1'] + token], [full_w(du), full_w(dx1)], [F32], "norm1_bwd")

    heads = lambda row: dpp[:, row, :RWKV_HEAD_DIM].reshape(-1)
    lora = lambda j, lo_, hi_: jnp.transpose(dwl[:, j, lo_:hi_, :RWKV_HEAD_DIM], (1, 0, 2)).reshape(hi_ - lo_, RWKV_WIDTH)
    grads = {
        'norm1_g': dg1[0],
        'w_in': dw_in,
        'rwkv_mu': jnp.concatenate([heads(0), heads(1), heads(2), jnp.sum(dml[:, 0, :], axis=0)]),
        'rwkv_w0': heads(3), 'rwkv_a0': heads(4), 'rwkv_k_k': heads(5), 'rwkv_k_a': heads(6),
        'rwkv_ln_w': heads(7), 'rwkv_ln_b': heads(8), 'rwkv_r_k': heads(9).reshape(RWKV_HEADS, RWKV_HEAD_DIM),
        'rwkv_w2': lora(0, 0, 64), 'rwkv_a2': lora(1, 64, 128), 'rwkv_g2': lora(2, 128, 256),
        'rwkv_proj': dw_rproj.reshape(RWKV_HEADS, LANES, -1)[:, :RWKV_HEAD_DIM].reshape(RWKV_WIDTH, -1),
        'gdn_conv_w': jnp.transpose(dcw_g[:, :, :GDN_CONV, :], (2, 1, 0, 3)).reshape(GDN_CONV, 3 * GDN_WIDTH),
        'gdn_a_log': jnp.sum(dgp[:, 0, :GDN_HEADS], axis=0), 'gdn_dt_bias': jnp.sum(dgp[:, 1, :GDN_HEADS], axis=0),
        'gdn_norm_w': jnp.sum(dgp[:, 2, :], axis=0),
        'gdn_proj': dw_gproj, 'w_out': dw_out, 'norm2_g': dg2[0], 'ffn_up': dw_up, 'ffn_conv_w': dcw_f,
        'ffn_down': dw_down, 'final_g': dgf[0],
    }
    return loss_vec, grad_x, grads


def kernel(x, norm1_g, w_in, rwkv_mu, rwkv_w0, rwkv_w2, rwkv_a0, rwkv_a2, rwkv_g2, rwkv_k_k, rwkv_k_a, rwkv_r_k, rwkv_ln_w, rwkv_ln_b, rwkv_proj, gdn_conv_w, gdn_a_log, gdn_dt_bias, gdn_norm_w, gdn_proj, w_out, norm2_g, ffn_up, ffn_conv_w, ffn_down, final_g, loss_target, m_norm1_g, m_w_in, m_rwkv_mu, m_rwkv_w0, m_rwkv_w2, m_rwkv_a0, m_rwkv_a2, m_rwkv_g2, m_rwkv_k_k, m_rwkv_k_a, m_rwkv_r_k, m_rwkv_ln_w, m_rwkv_ln_b, m_rwkv_proj, m_gdn_conv_w, m_gdn_a_log, m_gdn_dt_bias, m_gdn_norm_w, m_gdn_proj, m_w_out, m_norm2_g, m_ffn_up, m_ffn_conv_w, m_ffn_down, m_final_g, v_norm1_g, v_w_in, v_rwkv_mu, v_rwkv_w0, v_rwkv_w2, v_rwkv_a0, v_rwkv_a2, v_rwkv_g2, v_rwkv_k_k, v_rwkv_k_a, v_rwkv_r_k, v_rwkv_ln_w, v_rwkv_ln_b, v_rwkv_proj, v_gdn_conv_w, v_gdn_a_log, v_gdn_dt_bias, v_gdn_norm_w, v_gdn_proj, v_w_out, v_norm2_g, v_ffn_up, v_ffn_conv_w, v_ffn_down, v_final_g):
    given = dict(zip(WEIGHT_NAMES, (norm1_g, w_in, rwkv_mu, rwkv_w0, rwkv_w2, rwkv_a0, rwkv_a2, rwkv_g2, rwkv_k_k, rwkv_k_a, rwkv_r_k,
                                    rwkv_ln_w, rwkv_ln_b, rwkv_proj, gdn_conv_w, gdn_a_log, gdn_dt_bias, gdn_norm_w, gdn_proj, w_out,
                                    norm2_g, ffn_up, ffn_conv_w, ffn_down, final_g)))
    mom1 = dict(zip(WEIGHT_NAMES, (m_norm1_g, m_w_in, m_rwkv_mu, m_rwkv_w0, m_rwkv_w2, m_rwkv_a0, m_rwkv_a2, m_rwkv_g2, m_rwkv_k_k,
                                   m_rwkv_k_a, m_rwkv_r_k, m_rwkv_ln_w, m_rwkv_ln_b, m_rwkv_proj, m_gdn_conv_w, m_gdn_a_log,
                                   m_gdn_dt_bias, m_gdn_norm_w, m_gdn_proj, m_w_out, m_norm2_g, m_ffn_up, m_ffn_conv_w, m_ffn_down,
                                   m_final_g)))
    mom2 = dict(zip(WEIGHT_NAMES, (v_norm1_g, v_w_in, v_rwkv_mu, v_rwkv_w0, v_rwkv_w2, v_rwkv_a0, v_rwkv_a2, v_rwkv_g2, v_rwkv_k_k,
                                   v_rwkv_k_a, v_rwkv_r_k, v_rwkv_ln_w, v_rwkv_ln_b, v_rwkv_proj, v_gdn_conv_w, v_gdn_a_log,
                                   v_gdn_dt_bias, v_gdn_norm_w, v_gdn_proj, v_w_out, v_norm2_g, v_ffn_up, v_ffn_conv_w, v_ffn_down,
                                   v_final_g)))
    def strip(n, a):
        a = a if n == 'final_g' else a.reshape(a.shape[1:])
        return a.T if n in TRANSPOSED else a

    local = {n: strip(n, a) for n, a in given.items()}
    shard_shapes = {n: local[n].shape for n in SHARD_AXIS}
    sharded = BIG_SHARDED + SMALL_SHARDED

    late_names = [n for n in BIG_SHARDED if n != 'w_in']
    g_in = _all_gather_two_level(_row_pack([local['w_in']], BF16, 16), "gather_w_in")
    g_small = _all_gather_two_level(_flat_pack([local[n] for n in SMALL_SHARDED], F32, SUBLANES), "gather_small")
    late_pack, g_in, g_small = lax.optimization_barrier((_row_pack([local[n] for n in late_names], BF16, 16), g_in, g_small))
    send_sems, recv_sems, late_pack, landing, token = _gather_start(late_pack, "gather_late_start")
    full = _unpack_row_gathered(g_in, ['w_in'], shard_shapes)
    full.update(_unpack_gathered(g_small, SMALL_SHARDED, shard_shapes))
    rep = {n: local[n] for n in REPLICATED}
    rep['norm1_g'] = rep['norm1_g'] + token[0, 0]

    def late_weights(after):
        got = _gather_wait(send_sems, recv_sems, late_pack, landing, after, "gather_late_wait")
        me = 4 * lax.axis_index("x") + 2 * lax.axis_index("y") + lax.axis_index("c")
        slot = lax.broadcasted_iota(jnp.int32, (N_DEV, 1, 1), 0)
        got = jnp.where(slot == me, late_pack[None], got)
        return _prepare_late_weights(_unpack_row_gathered(got, late_names, shard_shapes))

    pushes = []
    me = 4 * lax.axis_index("x") + 2 * lax.axis_index("y") + lax.axis_index("c")
    slot = lax.broadcasted_iota(jnp.int32, (N_DEV, 1, 1), 0)

    def push_grads(group):
        names = list(group)
        slabs = [group[n].reshape(N_DEV, -1, group[n].shape[1]) for n in names]
        send_sems, recv_sems, slabs, landings, token = _slab_push_start(slabs, "grad_push_start_" + "_".join(names))
        pushes.append((names, send_sems, recv_sems, slabs, landings))
        return token[0, 0]

    loss_vec, grad_x, grads = _local_step(x[0], loss_target[0], _prepare_weights(full, rep), late_weights, push_grads)

    small_sharded = ['rwkv_proj', 'gdn_proj'] + SMALL_SHARDED
    small_names = small_sharded + REPLICATED
    rep_vec = jnp.concatenate([grads[n].reshape(-1) for n in REPLICATED] + [loss_vec[0, 0:1]])
    slab_small = jnp.concatenate([_shard_major(grads[n], SHARD_AXIS[n]) for n in small_sharded] +
                                 [jnp.broadcast_to(rep_vec[None], (N_DEV, rep_vec.shape[0]))], axis=1)
    small_rows = -(-slab_small.shape[1] // (PACK_W * SUBLANES)) * SUBLANES
    slab_small = jnp.pad(slab_small, ((0, 0), (0, small_rows * PACK_W - slab_small.shape[1]))).reshape(N_DEV, small_rows, PACK_W)
    small_send, small_recv, (slab_small,), (small_landing,), small_token = _slab_push_start([slab_small], "grad_push_start_small")

    landed = {}
    for names, send_sems, recv_sems, slabs, landings in pushes:
        got = _slab_push_wait(send_sems, recv_sems, slabs, landings, (grad_x, small_token),
                              "grad_push_wait_" + "_".join(names))
        for n, slab, land in zip(names, slabs, got):
            landed[n] = (land, slab)

    def pack_local(src):
        flat = jnp.concatenate([strip(n, src[n]).reshape(-1) for n in small_names])
        return jnp.pad(flat, (0, small_rows * PACK_W - flat.shape[0])).reshape(small_rows, PACK_W)

    results = [({}, None) for _ in range(4)]
    me_arr = me.astype(jnp.int32).reshape(1)
    for n in ROW_SHARDED:
        packs = _reduce_landed_adamw(*landed[n], me_arr, local[n], strip(n, mom1[n]), strip(n, mom2[n]), "adamw_" + n)
        for (out, _), pk in zip(results, packs):
            out[n] = (pk.T if n in TRANSPOSED else pk).reshape(given[n].shape)
    (small_landed,) = _slab_push_wait(small_send, small_recv, [slab_small], [small_landing], tuple(packs), "grad_push_wait_small")
    packs = _reduce_landed_adamw(small_landed, slab_small, me_arr, pack_local(given), pack_local(mom1), pack_local(mom2),
                                 "adamw_small")
    for i, pk in enumerate(packs):
        flat, off = pk.reshape(-1), 0
        for n in small_names:
            size = int(np.prod(given[n].shape))
            results[i][0][n] = flat[off:off + size].reshape(given[n].shape)
            off += size
        results[i] = (results[i][0], flat[off])
    (g_out, loss), (d_out, _), (m_out, _), (v_out, _) = results
    return (loss, grad_x[None], *[g_out[n] for n in WEIGHT_NAMES], *[d_out[n] for n in WEIGHT_NAMES],
            *[m_out[n] for n in WEIGHT_NAMES], *[v_out[n] for n in WEIGHT_NAMES])
```

```python
import functools

import jax
import jax.numpy as jnp
import numpy as np
from jax import lax
from jax.experimental import pallas as pl
from jax.experimental.pallas import tpu as pltpu

F32 = jnp.float32
BF16 = jnp.bfloat16

N_DEV = 8
D_MODEL = 1024
CHUNK = 64
RWKV_HEADS = 8
RWKV_HEAD_DIM = 64
RWKV_WIDTH = 512
GDN_HEADS = 4
GDN_HEAD_DIM = 128
GDN_WIDTH = 512
GDN_CONV = 4
FFN_HIDDEN = 2816
FFN_CONV = 3
NORM_EPS = 1e-6
L2_EPS = 1e-6
RWKV_GN_EPS = 64e-5
LANES = 128
SUBLANES = 8
VMEM_LIMIT = 56 * 1024 * 1024

ADAM_LR = 0.001
ADAM_B1 = 0.9
ADAM_B2 = 0.999
ADAM_EPS = 1e-08
ADAM_WD = 0.01
ADAM_STEP = 10

OFF_GA, OFF_GB, OFF_RKV, OFF_QKV, OFF_Z, OFF_LO, OFF_AB, CAT_W = 0, 1024, 2048, 5120, 6656, 7168, 7424, 7680
RWKV_HB = 8
RWKV_STEP_CHUNKS = 2
RWKV_TILE = RWKV_STEP_CHUNKS * CHUNK
GDN_HB = 4
GDN_STEP_CHUNKS = 4
GDN_TILE = GDN_STEP_CHUNKS * CHUNK

WEIGHT_NAMES = ['norm1_g', 'w_in', 'rwkv_mu', 'rwkv_w0', 'rwkv_w2', 'rwkv_a0', 'rwkv_a2', 'rwkv_g2', 'rwkv_k_k', 'rwkv_k_a',
                'rwkv_r_k', 'rwkv_ln_w', 'rwkv_ln_b', 'rwkv_proj', 'gdn_conv_w', 'gdn_a_log', 'gdn_dt_bias', 'gdn_norm_w',
                'gdn_proj', 'w_out', 'norm2_g', 'ffn_up', 'ffn_conv_w', 'ffn_down', 'final_g']
BIG_SHARDED = ['w_in', 'ffn_up', 'ffn_down', 'w_out', 'rwkv_proj', 'gdn_proj']
SMALL_SHARDED = ['rwkv_w2', 'rwkv_a2', 'rwkv_g2', 'gdn_conv_w', 'ffn_conv_w']
TRANSPOSED = ('w_in', 'ffn_up')
SHARD_AXIS = {'w_in': 0, 'ffn_up': 0, 'ffn_down': 0, 'w_out': 0, 'rwkv_proj': 1, 'gdn_proj': 1,
              'rwkv_w2': 1, 'rwkv_a2': 1, 'rwkv_g2': 1, 'gdn_conv_w': 1, 'ffn_conv_w': 1}
REPLICATED = [n for n in WEIGHT_NAMES if n not in SHARD_AXIS]
ROW_SHARDED = ['w_in', 'ffn_up', 'ffn_down', 'w_out']


def _cparams(sem=None):
    kw = dict(vmem_limit_bytes=VMEM_LIMIT)
    if sem is not None:
        kw['dimension_semantics'] = sem
    return pltpu.CompilerParams(**kw)


_NN, _NT, _TN = 'nn', 'nt', 'tn'
_DIMS_2D = {'nn': (((1,), (0,)), ((), ())), 'nt': (((1,), (1,)), ((), ())), 'tn': (((0,), (0,)), ((), ()))}
_DIMS_3D = {'nn': (((2,), (1,)), ((0,), (0,))), 'nt': (((2,), (2,)), ((0,), (0,))), 'tn': (((1,), (1,)), ((0,), (0,)))}


def _dg(a, b, kind):
    return lax.dot_general(a, b, (_DIMS_2D if a.ndim == 2 else _DIMS_3D)[kind], preferred_element_type=F32)


def _dot1(a, b, kind):
    return _dg(a.astype(BF16), b.astype(BF16), kind)


@jax.custom_vjp
def _dhi(a, b):
    return _dot1(a, b, _NN)


_dhi.defvjp(lambda a, b: (_dot1(a, b, _NN), (a, b)),
            lambda res, ct: (_dot1(ct, res[1], _NT), _dot1(res[0], ct, _TN)))


@jax.custom_vjp
def _dnt(a, b):
    return _dot1(a, b, _NT)


_dnt.defvjp(lambda a, b: (_dot1(a, b, _NT), (a, b)),
            lambda res, ct: (_dot1(ct, res[1], _NN), _dot1(ct, res[0], _TN)))


@jax.custom_vjp
def _dtn(a, b):
    return _dot1(a, b, _TN)


_dtn.defvjp(lambda a, b: (_dot1(a, b, _TN), (a, b)),
            lambda res, ct: (_dot1(res[1], ct, _NT), _dot1(res[0], ct, _NN)))


def _split3(x):
    x1 = x.astype(BF16)
    r1 = x - x1.astype(F32)
    x2 = r1.astype(BF16)
    return x1, x2, (r1 - x2.astype(F32)).astype(BF16)


def _dot_exact_lhs(sel, x, kind):
    parts = [_dg(sel, xi, kind) for xi in _split3(x)]
    return parts[0] + parts[1] + parts[2]


def _tril_ones(like):
    c = like.shape[-2]
    ri, ci = _iotas(c)
    return jnp.broadcast_to((ri >= ci).astype(BF16), like.shape[:-2] + (c, c))


@jax.custom_vjp
def _cumsum_rows(x):
    return _dot_exact_lhs(_tril_ones(x), x, _NN)


_cumsum_rows.defvjp(lambda x: (_dot_exact_lhs(_tril_ones(x), x, _NN), None),
                    lambda _, ct: (_dot_exact_lhs(_tril_ones(ct), ct, _TN),))


@jax.custom_vjp
def _lane_sum_as_row(x):
    return _dot_exact_lhs(jnp.ones(x.shape, BF16), x, _NT)


def _lane_sum_as_row_bwd(_, ct):
    ones = jnp.ones(ct.shape[:-1] + (LANES,), BF16)
    parts = [_dg(ci, ones, _TN) for ci in _split3(ct)]
    return (parts[0] + parts[1] + parts[2],)


_lane_sum_as_row.defvjp(lambda x: (_dot_exact_lhs(jnp.ones(x.shape, BF16), x, _NT), None), _lane_sum_as_row_bwd)


def _shift_rows(x, halo, s):
    rows = lax.broadcasted_iota(jnp.int32, x.shape, 0)
    out = pltpu.roll(x, s, 0)
    for i in range(s):
        out = jnp.where(rows == i, halo[SUBLANES - s + i:SUBLANES - s + i + 1, :], out)
    return out


def _unshift_rows(g, carry, s):
    c = g.shape[0]
    rows = lax.broadcasted_iota(jnp.int32, g.shape, 0)
    out = pltpu.roll(g, c - s, 0)
    for i in range(s):
        out = jnp.where(rows == c - s + i, carry[i:i + 1, :], out)
    return out


def _sigmoid_plain(z):
    return 1.0 / (1.0 + jnp.exp(-z))


def _sigmoid_value(z):
    t = jnp.exp(-jnp.abs(z))
    r = 1.0 / (1.0 + t)
    return jnp.where(z >= 0, r, t * r)


@jax.custom_vjp
def _sigmoid(z):
    return _sigmoid_value(z)


def _sigmoid_fwd(z):
    s = _sigmoid_value(z)
    return s, s


_sigmoid.defvjp(_sigmoid_fwd, lambda s, ct: (ct * s * (1.0 - s),))


def _silu(z):
    return z * _sigmoid(z)


def _softplus(z):
    return jnp.maximum(z, 0.0) + jnp.log(1.0 + jnp.exp(-jnp.abs(z)))


def _rms(t, gain):
    return t * lax.rsqrt(jnp.mean(t * t, axis=-1, keepdims=True) + NORM_EPS) * gain


def _iotas(c):
    return lax.broadcasted_iota(jnp.int32, (c, c), 0), lax.broadcasted_iota(jnp.int32, (c, c), 1)


def _unit_lower_inverse(xm, eye):
    t = eye + xm
    p = xm
    for _ in range(5):
        p = _dhi(p, p)
        t = t + _dhi(t, p)
    return t


def _rwkv_head(pr, pk, pv, plo, qr, qk, qv, qlo, s0, pp, mulo, wl):
    c = pr.shape[1]
    n_heads = s0.shape[0]
    n_chunks = pr.shape[0] // n_heads
    ri, ci = _iotas(c)
    if n_chunks > 1:
        pp = jnp.concatenate([pp] * n_chunks, axis=0)
        wl = jnp.concatenate([wl] * n_chunks, axis=0)

    def mix(p, q, mu):
        return p + (q - p) * mu

    r = mix(pr, qr, pp[:, 0:1])
    k = mix(pk, qk, pp[:, 1:2])
    v = mix(pv, qv, pp[:, 2:3])
    lo = mix(plo, qlo, mulo)
    w0, a0, k_k, k_a, ln_w, ln_b, r_k = (pp[:, i:i + 1] for i in range(3, 10))

    def per_head(t):
        return jnp.concatenate([jnp.broadcast_to(t[i], (n_heads,) + t.shape[1:]) for i in range(n_chunks)], axis=0)

    zw = _dhi(per_head(jnp.tanh(lo)), wl[:, 0])
    za = _dhi(per_head(lo), wl[:, 1])
    g = _dhi(per_head(_sigmoid(lo)), wl[:, 2])
    w_log = -_softplus(-(w0 + zw)) - 0.5
    lw = -jnp.exp(w_log)
    a = _sigmoid(a0 + za)
    kk = k * k_k
    kk = kk * lax.rsqrt(jnp.sum(kk * kk, axis=-1, keepdims=True) + L2_EPS)
    k2 = k * (1.0 + (a - 1.0) * k_a)
    an = -kk
    b = kk * a
    causal = ri >= ci
    strict = ri > ci
    eye = (ri == ci).astype(F32)
    cl = _cumsum_rows(lw)
    ecl = jnp.exp(-cl)
    at = an * jnp.exp(cl - lw)
    bt = b * ecl
    kt = k2 * ecl
    rt = r * jnp.exp(cl)
    a_ab = jnp.where(strict, _dnt(at, bt), 0.0)
    a_ak = jnp.where(strict, _dnt(at, kt), 0.0)
    tinv = _unit_lower_inverse(a_ab, eye)
    akv = _dhi(a_ak, v)
    r_b = jnp.where(causal, _dnt(rt, bt), 0.0)
    rkv = _dhi(jnp.where(causal, _dnt(rt, kt), 0.0), v)
    cl_end = jnp.sum(lw, axis=1, keepdims=True)
    dec_end = jnp.exp(cl_end - cl)
    b_end = b * dec_end
    sv = _dtn(v, k2 * dec_end)
    e_end = jnp.exp(cl_end)
    state, ys = s0, []
    for i in range(n_chunks):
        sl = slice(i * n_heads, (i + 1) * n_heads)
        u = _dhi(tinv[sl], _dnt(at[sl], state) + akv[sl])
        ys.append(_dnt(rt[sl], state) + _dhi(r_b[sl], u) + rkv[sl])
        state = state * e_end[sl] + _dtn(u, b_end[sl]) + sv[sl]
    y = jnp.concatenate(ys, axis=0) if n_chunks > 1 else ys[0]
    s1 = state
    m = (lax.broadcasted_iota(jnp.int32, (1, LANES), 1) < RWKV_HEAD_DIM).astype(F32)
    mean = jnp.sum(y, axis=-1, keepdims=True) * (1.0 / RWKV_HEAD_DIM)
    yc = (y - mean) * m
    var = jnp.sum(yc * yc, axis=-1, keepdims=True) * (1.0 / RWKV_HEAD_DIM)
    yn = yc * lax.rsqrt(var + RWKV_GN_EPS) * ln_w + ln_b
    y2 = yn + jnp.sum(r * k2 * r_k, axis=-1, keepdims=True) * v
    return y2 * g, s1


def _gdn_head(xq, xk, xv, z, ab, s0, cw, gp, oha, ohb):
    c = z.shape[1]
    n_heads = s0.shape[0]
    n_chunks = z.shape[0] // n_heads
    ri, ci = _iotas(c)
    cw = jnp.concatenate([cw] * n_chunks, axis=0) if n_chunks > 1 else cw

    def conv(xs, w):
        out = xs[0] * w[:, GDN_CONV - 1:GDN_CONV]
        for s in range(1, GDN_CONV):
            out = out + xs[s] * w[:, GDN_CONV - 1 - s:GDN_CONV - s]
        return out

    q = _silu(conv(xq, cw[:, 0]))
    k = _silu(conv(xk, cw[:, 1]))
    v = _silu(conv(xv, cw[:, 2]))
    q = q * lax.rsqrt(jnp.sum(q * q, axis=-1, keepdims=True) + L2_EPS) * (GDN_HEAD_DIM ** -0.5)
    k = k * lax.rsqrt(jnp.sum(k * k, axis=-1, keepdims=True) + L2_EPS)
    gg = -jnp.exp(gp[0:1]) * _softplus(ab + gp[1:2])
    beta = jnp.sum(_sigmoid(ab) * ohb, axis=-1, keepdims=True)
    causal = ri >= ci
    strict = ri > ci
    eye = (ri == ci).astype(F32)
    gcm = _cumsum_rows(gg * oha)
    gc = jnp.sum(gcm, axis=-1, keepdims=True)
    gc_row = _lane_sum_as_row(gcm)
    dec = jnp.where(causal, jnp.exp(jnp.where(causal, gc - gc_row, 0.0)), 0.0)
    kb = k * beta
    vb = v * beta
    lm = jnp.where(strict, _dnt(kb, k) * dec, 0.0)
    tinv = _unit_lower_inverse(-lm, eye)
    egc = jnp.exp(gc)
    u = _dhi(tinv, vb)
    wk = _dhi(tinv, kb * egc)
    attn = jnp.where(causal, _dnt(q, k) * dec, 0.0)
    g_last = gc[:, c - 1:c, :]
    q_dec = q * egc
    k_dec = k * jnp.exp(g_last - gc)
    e_last = jnp.exp(g_last)
    state, outs = s0, []
    for i in range(n_chunks):
        sl = slice(i * n_heads, (i + 1) * n_heads)
        v_new = u[sl] - _dhi(wk[sl], state)
        outs.append(_dhi(q_dec[sl], state) + _dhi(attn[sl], v_new))
        state = state * e_last[sl] + _dtn(k_dec[sl], v_new)
    o = jnp.concatenate(outs, axis=0) if n_chunks > 1 else outs[0]
    return _rms(o, gp[2:3]) * _silu(z), state


def _head_id(grp, i, per_step, heads):
    return i if per_step == heads else grp * per_step + i


def _head_range(grp, per_step, heads):
    return slice(None) if per_step == heads else pl.ds(grp * per_step, per_step)


def _rwkv_specs(nmap):
    hb, groups = RWKV_HB, RWKV_HEADS // RWKV_HB
    cb = OFF_RKV // (hb * LANES)
    specs = []
    for j in range(3):
        specs.append(pl.BlockSpec((RWKV_TILE, hb * LANES), lambda n, g, j=j: (nmap(n), cb + j * groups + g)))
    specs.append(pl.BlockSpec((RWKV_TILE, 2 * LANES), lambda n, g: (nmap(n), OFF_LO // (2 * LANES))))
    per = RWKV_TILE // SUBLANES
    for j in range(3):
        specs.append(pl.BlockSpec((SUBLANES, hb * LANES),
                                  lambda n, g, j=j: (jnp.maximum(nmap(n) * per - 1, 0), cb + j * groups + g)))
    specs.append(pl.BlockSpec((SUBLANES, 2 * LANES), lambda n, g: (jnp.maximum(nmap(n) * per - 1, 0), OFF_LO // (2 * LANES))))
    specs.append(pl.BlockSpec((hb, 16, LANES), lambda n, g: (g, 0, 0)))
    specs.append(pl.BlockSpec((1, 2 * LANES), lambda n, g: (0, 0)))
    specs.append(pl.BlockSpec((hb, 3, 2 * LANES, LANES), lambda n, g: (g, 0, 0, 0)))
    return specs


def _rwkv_operands(refs, halos, live):
    pr, pk, pv, plo = refs
    hr, hk, hv, hlo = halos
    cur, prev = [], []
    for x, hx in ((pr, hr), (pk, hk), (pv, hv)):
        tiles = [_lane_block(x, h) for h in range(RWKV_HB)]
        cur.append(_chunk_batch(tiles))
        prev.append(_chunk_batch([_shift_rows(t_, _lane_block(hx, h) * live, 1) for h, t_ in enumerate(tiles)]))
    lo = plo[...]
    cur.append(_chunk_batch([lo]))
    prev.append(_chunk_batch([_shift_rows(lo, hlo[...] * live, 1)]))
    return cur, prev


def _rwkv_forward(p_cat, ppack, mulo, wl):
    t = p_cat.shape[0]
    n_chunks = t // RWKV_TILE

    def body(pr, pk, pv, plo, hr, hk, hv, hlo, pp, ml, w, out, st_out, s_scr):
        n, grp = pl.program_id(0), pl.program_id(1)

        hsl = _head_range(grp, RWKV_HB, RWKV_HEADS)

        @pl.when(n == 0)
        def _():
            s_scr[hsl] = jnp.zeros((RWKV_HB, LANES, LANES), F32)

        live = (n > 0).astype(F32)
        cur, prev = _rwkv_operands((pr, pk, pv, plo), (hr, hk, hv, hlo), live)
        s0 = s_scr[hsl]
        st_out[...] = s0
        o, s1 = _rwkv_head(*cur, *prev, s0, pp[...], ml[...], w[...])
        for h, tile in enumerate(_head_tiles(o, RWKV_HB)):
            out[:, h * LANES:(h + 1) * LANES] = tile.astype(out.dtype)
        s_scr[hsl] = s1

    return pl.pallas_call(
        body, name="rwkv_fwd", grid=(n_chunks, RWKV_HEADS // RWKV_HB),
        in_specs=_rwkv_specs(lambda n: n),
        out_specs=(pl.BlockSpec((RWKV_TILE, RWKV_HB * LANES), lambda n, g: (n, g)),
                   pl.BlockSpec((None, RWKV_HB, LANES, LANES), lambda n, g: (n, g, 0, 0))),
        out_shape=(jax.ShapeDtypeStruct((t, RWKV_HEADS * LANES), BF16),
                   jax.ShapeDtypeStruct((n_chunks, RWKV_HEADS, LANES, LANES), F32)),
        scratch_shapes=[pltpu.VMEM((RWKV_HEADS, LANES, LANES), F32)],
        compiler_params=_cparams(("arbitrary", "arbitrary")),
    )(p_cat, p_cat, p_cat, p_cat, p_cat, p_cat, p_cat, p_cat, ppack, mulo, wl)


def _rwkv_backward(p_cat, ppack, mulo, wl, states, d_out):
    t = p_cat.shape[0]
    n_chunks = t // RWKV_TILE
    last = n_chunks - 1

    def body(pr, pk, pv, plo, hr, hk, hv, hlo, pp, ml, w, st, dy, dpr, dpk, dpv, dplo, dpp, dml, dw, ds_scr, car_scr, carlo_scr):
        n, grp = pl.program_id(0), pl.program_id(1)

        hsl = _head_range(grp, RWKV_HB, RWKV_HEADS)
        gi = _head_id(grp, 0, 1, RWKV_HEADS // RWKV_HB)

        @pl.when(n == 0)
        def _():
            ds_scr[hsl] = jnp.zeros((RWKV_HB, LANES, LANES), F32)
            car_scr[hsl] = jnp.zeros((RWKV_HB, 3 * SUBLANES, LANES), F32)
            carlo_scr[gi] = jnp.zeros((SUBLANES, 2 * LANES), F32)

        @pl.when((n == 0) & (grp == 0))
        def _():
            dpp[...] = jnp.zeros(dpp.shape, F32)
            dml[...] = jnp.zeros(dml.shape, F32)
            dw[...] = jnp.zeros(dw.shape, F32)

        live = (n < last).astype(F32)
        cur, prev = _rwkv_operands((pr, pk, pv, plo), (hr, hk, hv, hlo), live)
        _, vjp = jax.vjp(_rwkv_head, *cur, *prev, st[...], pp[...], ml[...], w[...])
        g = vjp((_chunk_batch([_lane_block(dy, h) for h in range(RWKV_HB)]), ds_scr[hsl]))
        outs = (dpr, dpk, dpv)
        d_cur = [_head_tiles(g[j], RWKV_HB) for j in range(3)]
        d_prev = [_head_tiles(g[4 + j], RWKV_HB) for j in range(3)]
        for i in range(RWKV_HB):
            sl = slice(i * LANES, (i + 1) * LANES)
            h = _head_id(grp, i, RWKV_HB, RWKV_HEADS)
            car = car_scr[h]
            for j in range(3):
                tot = d_cur[j][i] + _unshift_rows(d_prev[j][i], car[SUBLANES * j:SUBLANES * (j + 1), :], 1)
                outs[j][:, sl] = tot.astype(outs[j].dtype)
                car_scr[h, SUBLANES * j:SUBLANES * (j + 1), :] = d_prev[j][i][0:SUBLANES, :]
        (dlo_cur,), (dlo_prev,) = _head_tiles(g[3], 1), _head_tiles(g[7], 1)
        dlo = dlo_cur + _unshift_rows(dlo_prev, carlo_scr[gi], 1)
        carlo_scr[gi] = dlo_prev[0:SUBLANES, :]
        ds_scr[hsl] = g[8]
        dpp[hsl] += g[9]
        dml[0, 0:1, :] += g[10]
        dw[hsl] += g[11]

        @pl.when(grp == 0)
        def _():
            dplo[...] = dlo

        @pl.when(grp > 0)
        def _():
            dplo[...] += dlo

    rev = lambda n: last - n
    in_specs = _rwkv_specs(rev) + [
        pl.BlockSpec((None, RWKV_HB, LANES, LANES), lambda n, g: (rev(n), g, 0, 0)),
        pl.BlockSpec((RWKV_TILE, RWKV_HB * LANES), lambda n, g: (rev(n), g)),
    ]
    hw = RWKV_HEADS * LANES
    return pl.pallas_call(
        body, name="rwkv_bwd", grid=(n_chunks, RWKV_HEADS // RWKV_HB),
        in_specs=in_specs,
        out_specs=(pl.BlockSpec((RWKV_TILE, RWKV_HB * LANES), lambda n, g: (rev(n), g)),
                   pl.BlockSpec((RWKV_TILE, RWKV_HB * LANES), lambda n, g: (rev(n), g)),
                   pl.BlockSpec((RWKV_TILE, RWKV_HB * LANES), lambda n, g: (rev(n), g)),
                   pl.BlockSpec((RWKV_TILE, 2 * LANES), lambda n, h: (rev(n), 0)),
                   pl.BlockSpec((RWKV_HEADS, 16, LANES), lambda n, h: (0, 0, 0)),
                   pl.BlockSpec((RWKV_HEADS, SUBLANES, 2 * LANES), lambda n, h: (0, 0, 0)),
                   pl.BlockSpec((RWKV_HEADS, 3, 2 * LANES, LANES), lambda n, h: (0, 0, 0, 0))),
        out_shape=(jax.ShapeDtypeStruct((t, hw), BF16), jax.ShapeDtypeStruct((t, hw), BF16), jax.ShapeDtypeStruct((t, hw), BF16),
                   jax.ShapeDtypeStruct((t, 2 * LANES), F32),
                   jax.ShapeDtypeStruct((RWKV_HEADS, 16, LANES), F32),
                   jax.ShapeDtypeStruct((RWKV_HEADS, SUBLANES, 2 * LANES), F32),
                   jax.ShapeDtypeStruct((RWKV_HEADS, 3, 2 * LANES, LANES), F32)),
        scratch_shapes=[pltpu.VMEM((RWKV_HEADS, LANES, LANES), F32),
                        pltpu.VMEM((RWKV_HEADS, 3 * SUBLANES, LANES), F32),
                        pltpu.VMEM((RWKV_HEADS, SUBLANES, 2 * LANES), F32)],
        compiler_params=_cparams(("arbitrary", "arbitrary")),
    )(p_cat, p_cat, p_cat, p_cat, p_cat, p_cat, p_cat, p_cat, ppack, mulo, wl, states, d_out)


def _gdn_specs(nmap):
    per = GDN_TILE // SUBLANES
    hb, groups = GDN_HB, GDN_HEADS // GDN_HB
    cb = OFF_QKV // (hb * LANES)
    specs = []
    for j in range(3):
        specs.append(pl.BlockSpec((GDN_TILE, hb * LANES), lambda n, g, j=j: (nmap(n), cb + j * groups + g)))
    for j in range(3):
        specs.append(pl.BlockSpec((SUBLANES, hb * LANES),
                                  lambda n, g, j=j: (jnp.maximum(nmap(n) * per - 1, 0), cb + j * groups + g)))
    specs.append(pl.BlockSpec((GDN_TILE, hb * LANES), lambda n, g: (nmap(n), OFF_Z // (hb * LANES) + g)))
    specs.append(pl.BlockSpec((GDN_TILE, LANES), lambda n, g: (nmap(n), OFF_AB // LANES)))
    specs.append(pl.BlockSpec((hb, 3, SUBLANES, LANES), lambda n, g: (g, 0, 0, 0)))
    specs.append(pl.BlockSpec((SUBLANES, LANES), lambda n, g: (0, 0)))
    return specs


def _conv_taps(x, halo):
    return (x,) + tuple(_shift_rows(x, halo, s) for s in range(1, GDN_CONV))


def _onehots(grp):
    nb = GDN_STEP_CHUNKS * GDN_HB
    lane = lax.broadcasted_iota(jnp.int32, (nb, 1, LANES), 2)
    head = lax.broadcasted_iota(jnp.int32, (nb, 1, LANES), 0) % GDN_HB + _head_id(grp, 0, GDN_HB, GDN_HEADS)
    return (lane == head).astype(F32), (lane == GDN_HEADS + head).astype(F32)


def _chunk_batch(tiles):
    n_chunks = tiles[0].shape[0] // CHUNK
    return jnp.stack([t_[i * CHUNK:(i + 1) * CHUNK, :] for i in range(n_chunks) for t_ in tiles])


def _head_tiles(batch, n_heads=GDN_HB):
    n_chunks = batch.shape[0] // n_heads
    return [jnp.concatenate([batch[i * n_heads + h] for i in range(n_chunks)], axis=0) for h in range(n_heads)]


def _lane_block(ref, h):
    return ref[:, h * LANES:(h + 1) * LANES]


def _gdn_taps(refs, halos, live):
    out = []
    for x, hx in zip(refs, halos):
        per_head = [_conv_taps(_lane_block(x, h), _lane_block(hx, h) * live) for h in range(GDN_HB)]
        out.append(tuple(_chunk_batch([per_head[h][s] for h in range(GDN_HB)]) for s in range(GDN_CONV)))
    return out


def _gdn_forward(p_cat, cwpack, gpar):
    t = p_cat.shape[0]
    n_chunks = t // GDN_TILE

    def body(xq, xk, xv, hq, hk, hv, z, ab, cw, gp, out, st_out, s_scr):
        n, grp = pl.program_id(0), pl.program_id(1)

        hsl = _head_range(grp, GDN_HB, GDN_HEADS)

        @pl.when(n == 0)
        def _():
            s_scr[hsl] = jnp.zeros((GDN_HB, LANES, LANES), F32)

        live = (n > 0).astype(F32)
        oha, ohb = _onehots(grp)
        s0 = s_scr[hsl]
        st_out[...] = s0
        taps = _gdn_taps((xq, xk, xv), (hq, hk, hv), live)
        zb = _chunk_batch([_lane_block(z, h) for h in range(GDN_HB)])
        abb = _chunk_batch([ab[...]] * GDN_HB)
        o, s1 = _gdn_head(*taps, zb, abb, s0, cw[...], gp[...], oha, ohb)
        for h, tile in enumerate(_head_tiles(o)):
            out[:, h * LANES:(h + 1) * LANES] = tile.astype(out.dtype)
        s_scr[hsl] = s1

    return pl.pallas_call(
        body, name="gdn_fwd", grid=(n_chunks, GDN_HEADS // GDN_HB),
        in_specs=_gdn_specs(lambda n: n),
        out_specs=(pl.BlockSpec((GDN_TILE, GDN_HB * LANES), lambda n, g: (n, g)),
                   pl.BlockSpec((None, GDN_HB, LANES, LANES), lambda n, g: (n, g, 0, 0))),
        out_shape=(jax.ShapeDtypeStruct((t, GDN_WIDTH), BF16),
                   jax.ShapeDtypeStruct((n_chunks, GDN_HEADS, LANES, LANES), F32)),
        scratch_shapes=[pltpu.VMEM((GDN_HEADS, LANES, LANES), F32)],
        compiler_params=_cparams(("arbitrary", "arbitrary")),
    )(p_cat, p_cat, p_cat, p_cat, p_cat, p_cat, p_cat, p_cat, cwpack, gpar)


def _gdn_backward(p_cat, cwpack, gpar, states, d_out):
    t = p_cat.shape[0]
    n_chunks = t // GDN_TILE
    last = n_chunks - 1

    def body(xq, xk, xv, hq, hk, hv, z, ab, cw, gp, st, dy, dq, dk, dv, dz, dab, dcw, dgp, ds_scr, car_scr):
        n, grp = pl.program_id(0), pl.program_id(1)

        hsl = _head_range(grp, GDN_HB, GDN_HEADS)

        @pl.when(n == 0)
        def _():
            ds_scr[hsl] = jnp.zeros((GDN_HB, LANES, LANES), F32)
            car_scr[hsl] = jnp.zeros((GDN_HB, 3 * GDN_CONV, SUBLANES, LANES), F32)

        @pl.when((n == 0) & (grp == 0))
        def _():
            dcw[...] = jnp.zeros(dcw.shape, F32)
            dgp[...] = jnp.zeros(dgp.shape, F32)

        live = (n < last).astype(F32)
        oha, ohb = _onehots(grp)
        fn = functools.partial(_gdn_head, oha=oha, ohb=ohb)
        taps = _gdn_taps((xq, xk, xv), (hq, hk, hv), live)
        zb = _chunk_batch([_lane_block(z, h) for h in range(GDN_HB)])
        abb = _chunk_batch([ab[...]] * GDN_HB)
        _, vjp = jax.vjp(fn, *taps, zb, abb, st[...], cw[...], gp[...])
        g = vjp((_chunk_batch([_lane_block(dy, h) for h in range(GDN_HB)]), ds_scr[hsl]))
        outs = (dq, dk, dv)
        tap_tiles = [[_head_tiles(g[j][s]) for s in range(GDN_CONV)] for j in range(3)]
        dz_tiles = _head_tiles(g[3])
        for i in range(GDN_HB):
            sl = slice(i * LANES, (i + 1) * LANES)
            h = _head_id(grp, i, GDN_HB, GDN_HEADS)
            for j in range(3):
                tot = tap_tiles[j][0][i]
                for s in range(1, GDN_CONV):
                    slot = j * GDN_CONV + s
                    tot = tot + _unshift_rows(tap_tiles[j][s][i], car_scr[h, slot], s)
                    car_scr[h, slot] = tap_tiles[j][s][i][0:SUBLANES, :]
                outs[j][:, sl] = tot.astype(outs[j].dtype)
            dz[:, sl] = dz_tiles[i].astype(dz.dtype)
        dab_tiles = _head_tiles(g[4])
        dab_sum = dab_tiles[0]
        for h in range(1, GDN_HB):
            dab_sum = dab_sum + dab_tiles[h]
        ds_scr[hsl] = g[5]
        dcw[hsl] += g[6]
        dgp[0] += g[7]

        @pl.when(grp == 0)
        def _():
            dab[...] = dab_sum

        @pl.when(grp > 0)
        def _():
            dab[...] += dab_sum

    rev = lambda n: last - n
    in_specs = _gdn_specs(rev) + [
        pl.BlockSpec((None, GDN_HB, LANES, LANES), lambda n, g: (rev(n), g, 0, 0)),
        pl.BlockSpec((GDN_TILE, GDN_HB * LANES), lambda n, g: (rev(n), g)),
    ]
    blk = pl.BlockSpec((GDN_TILE, GDN_HB * LANES), lambda n, g: (rev(n), g))
    return pl.pallas_call(
        body, name="gdn_bwd", grid=(n_chunks, GDN_HEADS // GDN_HB),
        in_specs=in_specs,
        out_specs=(blk, blk, blk, blk,
                   pl.BlockSpec((GDN_TILE, LANES), lambda n, h: (rev(n), 0)),
                   pl.BlockSpec((GDN_HEADS, 3, SUBLANES, LANES), lambda n, h: (0, 0, 0, 0)),
                   pl.BlockSpec((GDN_HEADS, SUBLANES, LANES), lambda n, h: (0, 0, 0))),
        out_shape=(jax.ShapeDtypeStruct((t, GDN_WIDTH), BF16), jax.ShapeDtypeStruct((t, GDN_WIDTH), BF16),
                   jax.ShapeDtypeStruct((t, GDN_WIDTH), BF16), jax.ShapeDtypeStruct((t, GDN_WIDTH), BF16),
                   jax.ShapeDtypeStruct((t, LANES), F32),
                   jax.ShapeDtypeStruct((GDN_HEADS, 3, SUBLANES, LANES), F32),
                   jax.ShapeDtypeStruct((GDN_HEADS, SUBLANES, LANES), F32)),
        scratch_shapes=[pltpu.VMEM((GDN_HEADS, LANES, LANES), F32),
                        pltpu.VMEM((GDN_HEADS, 3 * GDN_CONV, SUBLANES, LANES), F32)],
        compiler_params=_cparams(("arbitrary", "arbitrary")),
    )(p_cat, p_cat, p_cat, p_cat, p_cat, p_cat, p_cat, p_cat, cwpack, gpar, states, d_out)


MM_VMEM_BUDGET = 44 * 1024 * 1024
MM_MIN_STEPS = 4


def _mm_tiles(mode, m, n, k, out_bytes):
    tms = [t for t in (2048, 1536, 1024, 768, 512, 256, 128, 64) if m % t == 0 and (mode != 'tn' or t % LANES == 0)]
    tns = [t for t in (1536, 1408, 1024, 768, 512, 256, 128) if n % t == 0]
    tks = [t for t in (2048, 1920, 1408, 1024, 512, 256, 128, 64) if k % t == 0]
    best, best_key = None, None
    for tm in tms:
        for tn in tns:
            for tk in tks:
                nk = k // tk
                vmem = 2 * (tm * tk * 2 + tk * tn * 2 + tm * tn * out_bytes) + (tm * tn * 4 if nk > 1 else 0)
                steps = (m // tm) * (n // tn) * nk
                if vmem > MM_VMEM_BUDGET:
                    continue
                reread = m * k * (n // tn) + k * n * (m // tm)
                key = (steps >= MM_MIN_STEPS, tn if mode == 'tn' else 0, tm * tn * tk, -nk, -reread)
                if best_key is None or key > best_key:
                    best, best_key = (tm, tn, tk), key
    if best is None:
        raise ValueError(f"no matmul tile for {mode} {m}x{n}x{k}")
    return best


_MM_DIMS = {'nn': (((1,), (0,)), ((), ())), 'nt': (((1,), (1,)), ((), ())), 'tn': (((0,), (0,)), ((), ()))}


def _matmul(a, b, mode, out_dtype, name):
    if mode == 'nn':
        (m, k), (k2, n) = a.shape, b.shape
    elif mode == 'nt':
        (m, k), (n, k2) = a.shape, b.shape
    else:
        (k, m), (k2, n) = a.shape, b.shape
    assert k == k2, (a.shape, b.shape, mode)
    tm, tn, tk = _mm_tiles(mode, m, n, k, jnp.dtype(out_dtype).itemsize)
    nk = k // tk
    dims = _MM_DIMS[mode]

    def body(a_ref, b_ref, o_ref, acc_ref):
        kk = pl.program_id(2)
        part = lax.dot_general(a_ref[...], b_ref[...], dims, preferred_element_type=F32)
        if nk == 1:
            o_ref[...] = part.astype(o_ref.dtype)
            return

        @pl.when(kk == 0)
        def _():
            acc_ref[...] = part

        @pl.when((kk > 0) & (kk < nk - 1))
        def _():
            acc_ref[...] += part

        @pl.when(kk == nk - 1)
        def _():
            o_ref[...] = (acc_ref[...] + part).astype(o_ref.dtype)

    a_spec = pl.BlockSpec((tk, tm), lambda i, j, kk: (kk, i)) if mode == 'tn' else pl.BlockSpec((tm, tk), lambda i, j, kk: (i, kk))
    b_spec = pl.BlockSpec((tn, tk), lambda i, j, kk: (j, kk)) if mode == 'nt' else pl.BlockSpec((tk, tn), lambda i, j, kk: (kk, j))
    return pl.pallas_call(
        body, name=name, grid=(m // tm, n // tn, nk),
        in_specs=[a_spec, b_spec],
        out_specs=pl.BlockSpec((tm, tn), lambda i, j, kk: (i, j)),
        out_shape=jax.ShapeDtypeStruct((m, n), out_dtype),
        scratch_shapes=[pltpu.VMEM((tm, tn), F32)],
        compiler_params=_cparams(("parallel", "parallel", "arbitrary")),
    )(a, b)


ROW_TILE = 512


def _row_specs(rows, tm):
    return [pl.BlockSpec((tm, w), lambda i, ci=ci: (i, ci)) for (_, w, ci) in rows]


def _rw_forward(fn, rows, pars, outs, name):
    t = rows[0][0].shape[0]
    tm = min(ROW_TILE, t)
    nr, npar = len(rows), len(pars)

    def body(*refs):
        vals = [r[...].astype(F32) for r in refs[:nr]] + [p[...] for p in refs[nr:nr + npar]]
        res = fn(*vals)
        for o, v in zip(refs[nr + npar:], res):
            o[...] = v.astype(o.dtype)

    return pl.pallas_call(
        body, name=name, grid=(t // tm,),
        in_specs=_row_specs(rows, tm) + [pl.BlockSpec(p.shape, lambda i: (0, 0)) for p in pars],
        out_specs=tuple(pl.BlockSpec((tm, w), lambda i: (i, 0)) for (w, _) in outs),
        out_shape=tuple(jax.ShapeDtypeStruct((t, w), dt) for (w, dt) in outs),
        compiler_params=_cparams(("parallel",)),
    )(*[r[0] for r in rows], *pars)


def _rw_backward(fn, rows, pars, cots, drow_dtypes, name):
    t = rows[0][0].shape[0]
    tm = min(ROW_TILE, t)
    nr, npar, nc = len(rows), len(pars), len(cots)
    keep = [i for i, dt in enumerate(drow_dtypes) if dt is not None]

    def body(*refs):
        vals = [r[...].astype(F32) for r in refs[:nr]] + [p[...] for p in refs[nr:nr + npar]]
        cvals = tuple(c[...].astype(F32) for c in refs[nr + npar:nr + npar + nc])
        orefs = refs[nr + npar + nc:]
        _, vjp = jax.vjp(fn, *vals)
        g = vjp(cvals)
        for o, i in zip(orefs[:len(keep)], keep):
            o[...] = g[i].astype(o.dtype)
        first = pl.program_id(0) == 0
        for o, gi in zip(orefs[len(keep):], g[nr:]):
            @pl.when(first)
            def _(o=o, gi=gi):
                o[...] = gi

            @pl.when(jnp.logical_not(first))
            def _(o=o, gi=gi):
                o[...] += gi

    out_specs = [pl.BlockSpec((tm, rows[i][1]), lambda i_: (i_, 0)) for i in keep] + \
                [pl.BlockSpec(p.shape, lambda i_: (0, 0)) for p in pars]
    out_shape = [jax.ShapeDtypeStruct((t, rows[i][1]), drow_dtypes[i]) for i in keep] + \
                [jax.ShapeDtypeStruct(p.shape, F32) for p in pars]
    return pl.pallas_call(
        body, name=name, grid=(t // tm,),
        in_specs=_row_specs(rows, tm) + [pl.BlockSpec(p.shape, lambda i: (0, 0)) for p in pars] + _row_specs(cots, tm),
        out_specs=tuple(out_specs), out_shape=tuple(out_shape),
        compiler_params=_cparams(("arbitrary",)),
    )(*[r[0] for r in rows], *pars, *[c[0] for c in cots])


def _norm_fn(x, g):
    return (_rms(x, g),)


def _norm_skip_fn(x, g):
    return _rms(x, g), x


def _merge_fn(ga, gb, ya, yb):
    return (_sigmoid(ga) * ya + _sigmoid(gb) * yb,)


def _res_norm_fn(x, mo, g):
    x1 = x + mo
    return x1, _rms(x1, g)


def _loss_head(x1, fo, gf, target, name):
    t, d = x1.shape
    tm = min(ROW_TILE, t)

    def tile_loss(x2, g, tgt):
        err = _rms(x2, g) - tgt
        per_row = jnp.sum(err * err, axis=-1, keepdims=True) * (0.5 / d)
        return jnp.sum(per_row, axis=0, keepdims=True)

    def body(x1_ref, fo_ref, g_ref, t_ref, loss_ref, dx_ref, dxb_ref, dg_ref):
        x2 = x1_ref[...] + fo_ref[...]
        val, vjp = jax.vjp(functools.partial(tile_loss, tgt=t_ref[...]), x2, g_ref[...])
        dx2, dg = vjp(jnp.ones((1, 1), F32))
        dx_ref[...] = dx2
        dxb_ref[...] = dx2.astype(BF16)
        first = pl.program_id(0) == 0

        @pl.when(first)
        def _():
            loss_ref[...] = jnp.broadcast_to(val, loss_ref.shape)
            dg_ref[...] = dg

        @pl.when(jnp.logical_not(first))
        def _():
            loss_ref[...] += jnp.broadcast_to(val, loss_ref.shape)
            dg_ref[...] += dg

    row = pl.BlockSpec((tm, d), lambda i: (i, 0))
    vec = pl.BlockSpec((1, d), lambda i: (0, 0))
    return pl.pallas_call(
        body, name=name, grid=(t // tm,),
        in_specs=[row, row, vec, row],
        out_specs=(pl.BlockSpec((1, LANES), lambda i: (0, 0)), row, row, vec),
        out_shape=(jax.ShapeDtypeStruct((1, LANES), F32), jax.ShapeDtypeStruct((t, d), F32),
                   jax.ShapeDtypeStruct((t, d), BF16), jax.ShapeDtypeStruct((1, d), F32)),
        compiler_params=_cparams(("arbitrary",)),
    )(x1, fo, gf, target)


FFN_TILE_ROWS = 2048
FFN_TILE_COLS = 256
FFN_COL_BLOCKS = FFN_HIDDEN // FFN_TILE_COLS


def _conv3_past(x, halo, w):
    rows = lax.broadcasted_iota(jnp.int32, x.shape, 0)
    x1 = jnp.where(rows == 0, halo[7:8, :], pltpu.roll(x, 1, 0))
    x2 = jnp.where(rows == 0, halo[6:7, :], jnp.where(rows == 1, halo[7:8, :], pltpu.roll(x, 2, 0)))
    return x * w[2:3] + x1 * w[1:2] + x2 * w[0:1], x1, x2


def _ffn_in_specs(tm, imap, jmap):
    per = tm // SUBLANES
    tile = lambda off: pl.BlockSpec((tm, FFN_TILE_COLS), lambda *g: (imap(*g), off + jmap(*g) % FFN_COL_BLOCKS))
    halo = lambda off: pl.BlockSpec((SUBLANES, FFN_TILE_COLS),
                                    lambda *g: (jnp.maximum(imap(*g) * per - 1, 0), off + jmap(*g) % FFN_COL_BLOCKS))
    wsp = lambda off: pl.BlockSpec((FFN_CONV, FFN_TILE_COLS), lambda *g: (0, off + jmap(*g) % FFN_COL_BLOCKS))
    return [tile(0), halo(0), wsp(0), tile(FFN_COL_BLOCKS), halo(FFN_COL_BLOCKS), wsp(FFN_COL_BLOCKS)]


def _ffn_act_forward(hpre, cw):
    t = hpre.shape[0]
    tm = min(FFN_TILE_ROWS, t)

    def body(hg, pg, wg, hu, pu, wu, out):
        live = (pl.program_id(0) > 0).astype(F32)
        cg, _, _ = _conv3_past(hg[...], pg[...] * live, wg[...])
        cu, _, _ = _conv3_past(hu[...], pu[...] * live, wu[...])
        out[...] = (cg * _sigmoid_plain(cg) * cu).astype(out.dtype)

    return pl.pallas_call(
        body, name="ffn_act_fwd", grid=(t // tm, FFN_COL_BLOCKS),
        in_specs=_ffn_in_specs(tm, lambda i, j: i, lambda i, j: j),
        out_specs=pl.BlockSpec((tm, FFN_TILE_COLS), lambda i, j: (i, j)),
        out_shape=jax.ShapeDtypeStruct((t, FFN_HIDDEN), BF16),
        compiler_params=_cparams(("parallel", "parallel")),
    )(hpre, hpre, cw, hpre, hpre, cw)


def _conv3_future(d, nxt, w):
    tm = d.shape[0]
    rows = lax.broadcasted_iota(jnp.int32, d.shape, 0)
    d1 = jnp.where(rows == tm - 1, nxt[0:1, :], pltpu.roll(d, tm - 1, 0))
    d2 = jnp.where(rows == tm - 1, nxt[1:2, :], jnp.where(rows == tm - 2, nxt[0:1, :], pltpu.roll(d, tm - 2, 0)))
    return d * w[2:3] + d1 * w[1:2] + d2 * w[0:1]


def _ffn_backward(hpre, cw, dact):
    t = hpre.shape[0]
    tm = min(FFN_TILE_ROWS, t)
    n_tiles = t // tm
    per = tm // SUBLANES

    def d_conv_out(cg, cu, d):
        s = _sigmoid_plain(cg)
        return d * cu * s * (1.0 + cg * (1.0 - s)), d * cg * s

    def body(hg, pg, ng, wg, hu, pu, nu, wu, da, dan, dhg, dhu, dwg, dwu):
        i = pl.program_id(1)
        live_prev = (i > 0).astype(F32)
        live_next = (i < n_tiles - 1).astype(F32)
        xg, xu = hg[...], hu[...]
        cg, g1, g2 = _conv3_past(xg, pg[...] * live_prev, wg[...])
        cu, u1, u2 = _conv3_past(xu, pu[...] * live_prev, wu[...])
        dg, du = d_conv_out(cg, cu, da[...])
        cgn, _, _ = _conv3_past(ng[...], hg[tm - SUBLANES:tm, :], wg[...])
        cun, _, _ = _conv3_past(nu[...], hu[tm - SUBLANES:tm, :], wu[...])
        dgn, dun = d_conv_out(cgn, cun, dan[...] * live_next)
        dhg[...] = _conv3_future(dg, dgn, wg[...]).astype(dhg.dtype)
        dhu[...] = _conv3_future(du, dun, wu[...]).astype(dhu.dtype)
        sums_g = [jnp.sum(xs * dg, axis=0, keepdims=True) for xs in (g2, g1, xg)]
        sums_u = [jnp.sum(xs * du, axis=0, keepdims=True) for xs in (u2, u1, xu)]

        @pl.when(i == 0)
        def _():
            for r_ in range(FFN_CONV):
                dwg[r_:r_ + 1, :] = sums_g[r_]
                dwu[r_:r_ + 1, :] = sums_u[r_]

        @pl.when(i > 0)
        def _():
            for r_ in range(FFN_CONV):
                dwg[r_:r_ + 1, :] += sums_g[r_]
                dwu[r_:r_ + 1, :] += sums_u[r_]

    nb = FFN_COL_BLOCKS
    nxt = lambda i: jnp.minimum((i + 1) * per, t // SUBLANES - 1)
    prv = lambda i: jnp.maximum(i * per - 1, 0)
    half = lambda off: [pl.BlockSpec((tm, FFN_TILE_COLS), lambda j, i: (i, off + j)),
                        pl.BlockSpec((SUBLANES, FFN_TILE_COLS), lambda j, i: (prv(i), off + j)),
                        pl.BlockSpec((SUBLANES, FFN_TILE_COLS), lambda j, i: (nxt(i), off + j)),
                        pl.BlockSpec((FFN_CONV, FFN_TILE_COLS), lambda j, i: (0, off + j))]
    tile = pl.BlockSpec((tm, FFN_TILE_COLS), lambda j, i: (i, j))
    taps = pl.BlockSpec((FFN_CONV, FFN_TILE_COLS), lambda j, i: (0, j))
    return pl.pallas_call(
        body, name="ffn_bwd", grid=(nb, n_tiles),
        in_specs=half(0) + half(nb) + [tile, pl.BlockSpec((SUBLANES, FFN_TILE_COLS), lambda j, i: (nxt(i), j))],
        out_specs=(tile, tile, taps, taps),
        out_shape=(jax.ShapeDtypeStruct((t, FFN_HIDDEN), BF16), jax.ShapeDtypeStruct((t, FFN_HIDDEN), BF16),
                   jax.ShapeDtypeStruct((FFN_CONV, FFN_HIDDEN), F32), jax.ShapeDtypeStruct((FFN_CONV, FFN_HIDDEN), F32)),
        compiler_params=_cparams(("parallel", "arbitrary")),
    )(hpre, hpre, hpre, cw, hpre, hpre, hpre, cw, dact, dact)


def _my_place():
    x, y, c = lax.axis_index("x"), lax.axis_index("y"), lax.axis_index("c")
    return x, y, c, 4 * x + 2 * y + c


N_CHIPS = 4


def _remote(src, dst, send_sem, recv_sem, dev):
    return pltpu.make_async_remote_copy(src_ref=src, dst_ref=dst, send_sem=send_sem, recv_sem=recv_sem, device_id=dev,
                                        device_id_type=pl.DeviceIdType.MESH)


def _chip_peer(x, y, k):
    return x ^ ((k >> 1) & 1), y ^ (k & 1)


def _all_gather_two_level(shard, name):
    r, w = shard.shape

    def body(src, out, send_sems, recv_sems, local_sem):
        x, y, c, me = _my_place()
        sibling = (x, y, 1 - c)
        mine = pltpu.make_async_copy(src, out.at[me], local_sem)
        mine.start()
        first = [_remote(src, out.at[me], send_sems.at[0], recv_sems.at[0], sibling)]
        for k in range(1, N_CHIPS):
            px, py = _chip_peer(x, y, k)
            first.append(_remote(src, out.at[me], send_sems.at[k], recv_sems.at[k], (px, py, c)))
        for cp in first:
            cp.start()
        passed = []
        for k in range(1, N_CHIPS):
            px, py = _chip_peer(x, y, k)
            landed = out.at[me ^ (2 * k)]
            _remote(src, landed, send_sems.at[k], recv_sems.at[k], (px, py, c)).wait_recv()
            fwd = _remote(landed, landed, send_sems.at[N_CHIPS - 1 + k], recv_sems.at[N_CHIPS - 1 + k], sibling)
            fwd.start()
            passed.append(fwd)
        _remote(src, out.at[me ^ 1], send_sems.at[0], recv_sems.at[0], sibling).wait_recv()
        for k in range(1, N_CHIPS):
            got = out.at[(me ^ 1) ^ (2 * k)]
            _remote(got, got, send_sems.at[N_CHIPS - 1 + k], recv_sems.at[N_CHIPS - 1 + k], sibling).wait_recv()
        for cp in first + passed:
            cp.wait_send()
        mine.wait()

    return pl.pallas_call(
        body, name=name,
        in_specs=[pl.BlockSpec(memory_space=pl.ANY)],
        out_specs=pl.BlockSpec(memory_space=pl.ANY),
        out_shape=jax.ShapeDtypeStruct((N_DEV, r, w), shard.dtype),
        scratch_shapes=[pltpu.SemaphoreType.DMA((N_DEV - 1,)), pltpu.SemaphoreType.DMA((N_DEV - 1,)), pltpu.SemaphoreType.DMA],
    )(shard)


def _device_peer(x, y, c, k):
    px, py, pc = x ^ ((k >> 2) & 1), y ^ ((k >> 1) & 1), c ^ (k & 1)
    return (px, py, pc), 4 * px + 2 * py + pc


_HBM = pl.BlockSpec(memory_space=pltpu.HBM)
_SEM = pl.BlockSpec(memory_space=pltpu.SEMAPHORE)


def _gather_start(shard, name):
    def body(src, land, send_sems, recv_sems, src_thru, land_thru, token):
        x, y, c, me = _my_place()
        for k in range(1, N_DEV):
            dev, _ = _device_peer(x, y, c, k)
            _remote(src, land.at[me], send_sems.at[k], recv_sems.at[k], dev).start()
        token[...] = jnp.zeros_like(token)

    landing = lax.empty((N_DEV,) + shard.shape, shard.dtype)
    return pl.pallas_call(
        body, name=name,
        out_shape=(pltpu.SemaphoreType.DMA((N_DEV,)), pltpu.SemaphoreType.DMA((N_DEV,)), pltpu.HBM(shard.shape, shard.dtype),
                   pltpu.HBM(landing.shape, landing.dtype), jax.ShapeDtypeStruct((SUBLANES, LANES), F32)),
        in_specs=(_HBM, _HBM), out_specs=(_SEM, _SEM, _HBM, _HBM, pl.BlockSpec(memory_space=pltpu.VMEM)),
        input_output_aliases={0: 2, 1: 3},
        compiler_params=pltpu.CompilerParams(has_side_effects=pltpu.SideEffectType.DATAFLOW_SIDE_EFFECTING),
    )(pltpu.with_memory_space_constraint(shard, pltpu.HBM), pltpu.with_memory_space_constraint(landing, pltpu.HBM))


def _gather_wait(send_sems, recv_sems, shard, landing, after, name):
    n_after = len(after)

    def body(*refs):
        src, land, send_sems, recv_sems = refs[:4]
        x, y, c, _ = _my_place()
        for k in range(1, N_DEV):
            dev, idx = _device_peer(x, y, c, k)
            cp = _remote(src, land.at[idx], send_sems.at[k], recv_sems.at[k], dev)
            cp.wait_send()
            cp.wait_recv()

    return pl.pallas_call(
        body, name=name,
        out_shape=(pltpu.HBM(shard.shape, shard.dtype), pltpu.HBM(landing.shape, landing.dtype)),
        in_specs=(_HBM, _HBM, _SEM, _SEM) + (pl.BlockSpec(memory_space=pl.ANY),) * n_after, out_specs=(_HBM, _HBM),
        input_output_aliases={0: 0, 1: 1},
        compiler_params=pltpu.CompilerParams(has_side_effects=pltpu.SideEffectType.DATAFLOW_SIDE_EFFECTING),
    )(shard, landing, send_sems, recv_sems, *after)[1]


def _slab_push_start(slabs, name):
    na = len(slabs)

    def body(*refs):
        srcs, lands = refs[:na], refs[na:2 * na]
        send_sems, recv_sems = refs[2 * na], refs[2 * na + 1]
        token = refs[-1]
        x, y, c, me = _my_place()
        for i in range(na):
            for k in range(1, N_DEV):
                dev, idx = _device_peer(x, y, c, k)
                s = i * N_DEV + k
                _remote(srcs[i].at[idx], lands[i].at[me], send_sems.at[s], recv_sems.at[s], dev).start()
        token[...] = jnp.zeros_like(token)

    hbm_shapes = [pltpu.HBM(a.shape, a.dtype) for a in slabs]
    ins = [pltpu.with_memory_space_constraint(a, pltpu.HBM) for a in slabs]
    ins += [pltpu.with_memory_space_constraint(lax.empty(a.shape, a.dtype), pltpu.HBM) for a in slabs]
    out = pl.pallas_call(
        body, name=name,
        out_shape=(pltpu.SemaphoreType.DMA((na * N_DEV,)), pltpu.SemaphoreType.DMA((na * N_DEV,)), *hbm_shapes, *hbm_shapes,
                   jax.ShapeDtypeStruct((SUBLANES, LANES), F32)),
        in_specs=(_HBM,) * (2 * na), out_specs=(_SEM, _SEM) + (_HBM,) * (2 * na) + (pl.BlockSpec(memory_space=pltpu.VMEM),),
        input_output_aliases={i: 2 + i for i in range(2 * na)},
        compiler_params=pltpu.CompilerParams(has_side_effects=pltpu.SideEffectType.DATAFLOW_SIDE_EFFECTING),
    )(*ins)
    return out[0], out[1], list(out[2:2 + na]), list(out[2 + na:2 + 2 * na]), out[-1]


def _slab_push_wait(send_sems, recv_sems, slabs, landings, after, name):
    na = len(slabs)

    def body(*refs):
        srcs, lands = refs[:na], refs[na:2 * na]
        send_sems, recv_sems = refs[2 * na], refs[2 * na + 1]
        x, y, c, me = _my_place()
        for i in range(na):
            for k in range(1, N_DEV):
                dev, idx = _device_peer(x, y, c, k)
                s = i * N_DEV + k
                cp = _remote(srcs[i].at[idx], lands[i].at[idx], send_sems.at[s], recv_sems.at[s], dev)
                cp.wait_send()
                cp.wait_recv()

    hbm_shapes = tuple(pltpu.HBM(a.shape, a.dtype) for a in slabs)
    out = pl.pallas_call(
        body, name=name, out_shape=hbm_shapes + hbm_shapes,
        in_specs=(_HBM,) * (2 * na) + (_SEM, _SEM) + (pl.BlockSpec(memory_space=pl.ANY),) * len(after),
        out_specs=(_HBM,) * (2 * na), input_output_aliases={i: i for i in range(2 * na)},
        compiler_params=pltpu.CompilerParams(has_side_effects=pltpu.SideEffectType.DATAFLOW_SIDE_EFFECTING),
    )(*slabs, *landings, send_sems, recv_sems, *after)
    return list(out[na:])


def _pair_exchange(arrays, name):
    na = len(arrays)

    def body(*refs):
        srcs, dsts, (send_sems, recv_sems) = refs[:na], refs[na:2 * na], refs[2 * na:]
        x, y, c, _ = _my_place()
        sibling = (x, y, 1 - c)
        copies = []
        for i in range(na):
            for q in range(N_CHIPS):
                s = i * N_CHIPS + q
                copies.append(_remote(srcs[i].at[2 * q + 1 - c], dsts[i].at[q], send_sems.at[s], recv_sems.at[s], sibling))
        for cp in copies:
            cp.start()
        for cp in copies:
            cp.wait_recv()
        for cp in copies:
            cp.wait_send()

    hbm = pl.BlockSpec(memory_space=pl.ANY)
    return pl.pallas_call(
        body, name=name, in_specs=[hbm] * na, out_specs=tuple([hbm] * na),
        out_shape=tuple(jax.ShapeDtypeStruct((N_CHIPS,) + a.shape[1:], a.dtype) for a in arrays),
        scratch_shapes=[pltpu.SemaphoreType.DMA((na * N_CHIPS,)), pltpu.SemaphoreType.DMA((na * N_CHIPS,))],
    )(*arrays)


ELEMENTWISE_COLS = 512


def _pair_sum(slabs, recv, core, out_dtype, name):
    _, r, w = slabs.shape
    tc = ELEMENTWISE_COLS

    def body(core_ref, mine, theirs, out):
        out[...] = (mine[...] + theirs[...]).astype(out.dtype)

    grid_spec = pltpu.PrefetchScalarGridSpec(
        num_scalar_prefetch=1, grid=(N_CHIPS, w // tc),
        in_specs=[pl.BlockSpec((None, r, tc), lambda q, j, core_ref: (2 * q + core_ref[0], 0, j)),
                  pl.BlockSpec((None, r, tc), lambda q, j, core_ref: (q, 0, j))],
        out_specs=pl.BlockSpec((None, r, tc), lambda q, j, core_ref: (q, 0, j)))
    return pl.pallas_call(body, name=name, grid_spec=grid_spec,
                          out_shape=jax.ShapeDtypeStruct((N_CHIPS, r, w), out_dtype),
                          compiler_params=_cparams(("parallel", "parallel")))(core, slabs, recv)


def _chip_exchange(arrays, name):
    na = len(arrays)

    def body(*refs):
        srcs, dsts, (send_sems, recv_sems, local_sems) = refs[:na], refs[na:2 * na], refs[2 * na:]
        x, y, c, _ = _my_place()
        chip = 2 * x + y
        own = [pltpu.make_async_copy(srcs[i].at[chip], dsts[i].at[chip], local_sems.at[i]) for i in range(na)]
        for cp in own:
            cp.start()
        sends, arrivals = [], []
        for i in range(na):
            for k in range(1, N_CHIPS):
                px, py = _chip_peer(x, y, k)
                s = i * N_CHIPS + k
                sends.append(_remote(srcs[i].at[chip ^ k], dsts[i].at[chip], send_sems.at[s], recv_sems.at[s], (px, py, c)))
                arrivals.append(_remote(srcs[i].at[chip], dsts[i].at[chip ^ k], send_sems.at[s], recv_sems.at[s], (px, py, c)))
        for cp in sends:
            cp.start()
        for cp in arrivals:
            cp.wait_recv()
        for cp in sends:
            cp.wait_send()
        for cp in own:
            cp.wait()

    hbm = pl.BlockSpec(memory_space=pl.ANY)
    return pl.pallas_call(
        body, name=name, in_specs=[hbm] * na, out_specs=tuple([hbm] * na),
        out_shape=tuple(jax.ShapeDtypeStruct(a.shape, a.dtype) for a in arrays),
        scratch_shapes=[pltpu.SemaphoreType.DMA((na * N_CHIPS,)), pltpu.SemaphoreType.DMA((na * N_CHIPS,)),
                        pltpu.SemaphoreType.DMA((na,))],
    )(*arrays)


def _adamw_update(g, w, m, v):
    c1 = 1.0 / (1.0 - ADAM_B1 ** ADAM_STEP)
    c2 = 1.0 / (1.0 - ADAM_B2 ** ADAM_STEP)
    mn = ADAM_B1 * m + (1.0 - ADAM_B1) * g
    vn = ADAM_B2 * v + (1.0 - ADAM_B2) * (g * g)
    return -ADAM_LR * ((mn * c1) / (jnp.sqrt(vn * c2) + ADAM_EPS) + ADAM_WD * w), mn, vn


def _reduce_adamw(parts, w, m, v, name):
    n_parts, r, wd = parts.shape
    tc = ELEMENTWISE_COLS

    def body(p_ref, w_ref, m_ref, v_ref, g_out, d_out, m_out, v_out):
        g = p_ref[0].astype(F32)
        for s in range(1, n_parts):
            g = g + p_ref[s].astype(F32)
        g_out[...] = g
        d_out[...], m_out[...], v_out[...] = _adamw_update(g, w_ref[...], m_ref[...], v_ref[...])

    blk = pl.BlockSpec((r, tc), lambda j: (0, j))
    shp = jax.ShapeDtypeStruct((r, wd), F32)
    return pl.pallas_call(
        body, name=name, grid=(wd // tc,),
        in_specs=[pl.BlockSpec((n_parts, r, tc), lambda j: (0, 0, j)), blk, blk, blk],
        out_specs=(blk, blk, blk, blk), out_shape=(shp, shp, shp, shp),
        compiler_params=_cparams(("parallel",)),
    )(parts, w, m, v)


def _reduce_landed_adamw(landing, own, me, w, m, v, name):
    n_parts, r, wd = landing.shape
    tc = ELEMENTWISE_COLS

    def body(me_ref, land_ref, own_ref, w_ref, m_ref, v_ref, g_out, d_out, m_out, v_out):
        mine = own_ref[...].astype(F32)
        g = None
        for s in range(n_parts):
            part = jnp.where(me_ref[0] == s, mine, land_ref[s].astype(F32))
            g = part if g is None else g + part
        g_out[...] = g
        d_out[...], m_out[...], v_out[...] = _adamw_update(g, w_ref[...], m_ref[...], v_ref[...])

    blk = pl.BlockSpec((r, tc), lambda j, me_ref: (0, j))
    shp = jax.ShapeDtypeStruct((r, wd), F32)
    grid_spec = pltpu.PrefetchScalarGridSpec(
        num_scalar_prefetch=1, grid=(wd // tc,),
        in_specs=[pl.BlockSpec((n_parts, r, tc), lambda j, me_ref: (0, 0, j)),
                  pl.BlockSpec((None, r, tc), lambda j, me_ref: (me_ref[0], 0, j)), blk, blk, blk],
        out_specs=(blk, blk, blk, blk))
    return pl.pallas_call(body, name=name, grid_spec=grid_spec, out_shape=(shp, shp, shp, shp),
                          compiler_params=_cparams(("parallel",)))(me, landing, own, w, m, v)


PACK_W = 1024


def _pad_heads(a, slots):
    lead = a.shape[:-1]
    a = a.reshape(lead + (slots, RWKV_HEAD_DIM))
    a = jnp.pad(a, [(0, 0)] * (len(lead) + 1) + [(0, LANES - RWKV_HEAD_DIM)])
    return a.reshape(lead + (slots * LANES,))


def _flat_pack(arrs, dtype, row_mult):
    flat = jnp.concatenate([a.reshape(-1).astype(dtype) for a in arrs])
    n = flat.shape[0]
    rows = -(-n // PACK_W)
    rows = -(-rows // row_mult) * row_mult
    return jnp.pad(flat, (0, rows * PACK_W - n)).reshape(rows, PACK_W)


def _row_pack(arrs, dtype, row_mult):
    parts = [a.astype(dtype) if a.shape[1] == PACK_W else a.astype(dtype).reshape(-1, PACK_W) for a in arrs]
    rows = sum(p.shape[0] for p in parts)
    pad = -(-rows // row_mult) * row_mult - rows
    return jnp.concatenate(parts + ([jnp.zeros((pad, PACK_W), dtype)] if pad else []), axis=0)


def _unpack_row_gathered(g, names, shard_shapes):
    out, r0 = {}, 0
    for n in names:
        s = shard_shapes[n]
        rows = s[0] * s[1] // PACK_W
        seg = g[:, r0:r0 + rows, :]
        r0 += rows
        if s[1] == PACK_W:
            assert SHARD_AXIS[n] == 0
            out[n] = seg.reshape(N_DEV * s[0], s[1])
        else:
            assert SHARD_AXIS[n] == 1
            out[n] = jnp.transpose(seg.reshape((N_DEV,) + tuple(s)), (1, 0, 2)).reshape(s[0], N_DEV * s[1])
    return out


def _unpack_gathered(g, names, shard_shapes):
    flat = g.reshape(N_DEV, -1)
    out, off = {}, 0
    for n in names:
        s = shard_shapes[n]
        size = s[0] * s[1]
        seg = flat[:, off:off + size].reshape((N_DEV,) + tuple(s))
        off += size
        if SHARD_AXIS[n] == 1:
            out[n] = jnp.transpose(seg, (1, 0, 2)).reshape(s[0], N_DEV * s[1])
        else:
            out[n] = seg.reshape(N_DEV * s[0], s[1])
    return out


def _shard_major(full, axis):
    a, b = full.shape
    if axis == 1:
        return jnp.transpose(full.reshape(a, N_DEV, b // N_DEV), (1, 0, 2)).reshape(N_DEV, -1)
    return full.reshape(N_DEV, -1)


def _prepare_weights(full, rep):
    w = full['w_in']
    d = w.shape[1]
    rkv = jnp.pad(w[0:1536].reshape(3 * RWKV_HEADS, RWKV_HEAD_DIM, d), ((0, 0), (0, LANES - RWKV_HEAD_DIM), (0, 0)))
    w_cat = jnp.concatenate([
        w[3848:4872], w[4872:5896], rkv.reshape(3 * RWKV_HEADS * LANES, d), w[1792:3328], w[3328:3840],
        w[1536:1792], jnp.pad(w[3840:3848], ((0, LANES - 8), (0, 0))), jnp.zeros((LANES, d), w.dtype)], axis=0)
    assert w_cat.shape[0] == CAT_W
    mu = rep['rwkv_mu']
    vecs = [mu[0:512], mu[512:1024], mu[1024:1536], rep['rwkv_w0'], rep['rwkv_a0'], rep['rwkv_k_k'], rep['rwkv_k_a'],
            rep['rwkv_ln_w'], rep['rwkv_ln_b'], rep['rwkv_r_k'].reshape(-1)]
    ppack = jnp.stack([jnp.pad(v.reshape(RWKV_HEADS, RWKV_HEAD_DIM), ((0, 0), (0, LANES - RWKV_HEAD_DIM))) for v in vecs], axis=1)
    ppack = jnp.pad(ppack, ((0, 0), (0, 16 - len(vecs)), (0, 0)))
    mulo = mu[1536:1792].reshape(1, 2 * LANES)
    wl = jnp.zeros((3, 2 * LANES, RWKV_HEADS * LANES), F32)
    wl = wl.at[0, 0:64].set(_pad_heads(full['rwkv_w2'], RWKV_HEADS))
    wl = wl.at[1, 64:128].set(_pad_heads(full['rwkv_a2'], RWKV_HEADS))
    wl = wl.at[2, 128:256].set(_pad_heads(full['rwkv_g2'], RWKV_HEADS))
    wl = jnp.transpose(wl.reshape(3, 2 * LANES, RWKV_HEADS, LANES), (2, 0, 1, 3))
    cw = full['gdn_conv_w'].reshape(GDN_CONV, 3, GDN_HEADS, LANES)
    cwpack = jnp.pad(jnp.transpose(cw, (2, 1, 0, 3)), ((0, 0), (0, 0), (0, SUBLANES - GDN_CONV), (0, 0)))
    gpar = jnp.zeros((SUBLANES, LANES), F32)
    gpar = gpar.at[0, 0:GDN_HEADS].set(rep['gdn_a_log']).at[1, 0:GDN_HEADS].set(rep['gdn_dt_bias']).at[2].set(rep['gdn_norm_w'])
    return dict(w_cat=w_cat, ffn_cw=full['ffn_conv_w'], ppack=ppack, mulo=mulo, wl=wl, cwpack=cwpack, gpar=gpar,
                g1=rep['norm1_g'].reshape(1, -1), g2=rep['norm2_g'].reshape(1, -1), gf=rep['final_g'].reshape(1, -1))


def _prepare_late_weights(full):
    rp = full['rwkv_proj']
    rproj = jnp.pad(rp.reshape(RWKV_HEADS, RWKV_HEAD_DIM, -1), ((0, 0), (0, LANES - RWKV_HEAD_DIM), (0, 0))).reshape(RWKV_HEADS * LANES, -1)
    return dict(rproj=rproj, gproj=full['gdn_proj'], w_out=full['w_out'], ffn_up=full['ffn_up'], ffn_down=full['ffn_down'])


def _local_step(x, target, p, late_weights, push_grads):
    d = x.shape[1]
    full_w = lambda a: (a, a.shape[1], 0)
    (u,) = _rw_forward(_norm_fn, [full_w(x)], [p['g1']], [(d, BF16)], "norm1")
    p_cat = _matmul(u, p['w_cat'], 'nt', F32, "proj_in")
    ya_pre, st_r = _rwkv_forward(p_cat, p['ppack'], p['mulo'], p['wl'])
    yb_pre, st_g = _gdn_forward(p_cat, p['cwpack'], p['gpar'])
    p = {**p, **late_weights((ya_pre, yb_pre))}
    ya = _matmul(ya_pre, p['rproj'], 'nn', F32, "rwkv_proj")
    yb = _matmul(yb_pre, p['gproj'], 'nn', F32, "gdn_proj")
    gates = [(p_cat, d, OFF_GA // d), (p_cat, d, OFF_GB // d)]
    (mixed,) = _rw_forward(_merge_fn, gates + [full_w(ya), full_w(yb)], [], [(d, BF16)], "merge")
    mo = _matmul(mixed, p['w_out'], 'nn', F32, "out_proj")
    x1, n2 = _rw_forward(_res_norm_fn, [full_w(x), full_w(mo)], [p['g2']], [(d, F32), (d, BF16)], "res_norm2")
    hpre = _matmul(n2, p['ffn_up'], 'nt', F32, "ffn_up")
    act = _ffn_act_forward(hpre, p['ffn_cw'])
    fo = _matmul(act, p['ffn_down'], 'nn', F32, "ffn_down")
    loss_vec, dx2, dx2b, dgf = _loss_head(x1, fo, p['gf'], target, "loss_head")

    dact = _matmul(dx2b, p['ffn_down'], 'nt', F32, "d_act")
    dw_down = _matmul(act, dx2b, 'tn', BF16, "dw_ffn_down")
    dh_gate, dh_up, dcw_gate, dcw_up = _ffn_backward(hpre, p['ffn_cw'], dact)
    dh = jnp.concatenate([dh_gate, dh_up], axis=1)
    dcw_f = jnp.concatenate([dcw_gate, dcw_up], axis=1)
    dn2 = _matmul(dh, p['ffn_up'], 'nn', F32, "d_norm2")
    dw_up = _matmul(dh, n2, 'tn', BF16, "dw_ffn_up")
    token = push_grads({'ffn_down': dw_down, 'ffn_up': dw_up})
    dx1, dx1b, dg2 = _rw_backward(_res_norm_fn, [full_w(x), full_w(mo)], [p['g2'] + token], [full_w(dx2), full_w(dn2)],
                                  [F32, BF16], "res_norm2_bwd")
    dmixed = _matmul(dx1b, p['w_out'], 'nt', F32, "d_mixed")
    dw_out = _matmul(mixed, dx1b, 'tn', BF16, "dw_out")
    dga, dgb, dya, dyb = _rw_backward(_merge_fn, gates + [full_w(ya), full_w(yb)], [], [full_w(dmixed)],
                                      [BF16, BF16, BF16, BF16], "merge_bwd")
    d_ya_pre = _matmul(dya, p['rproj'], 'nt', F32, "d_rwkv_out")
    dw_rproj = _matmul(ya_pre, dya, 'tn', F32, "dw_rwkv_proj")
    d_yb_pre = _matmul(dyb, p['gproj'], 'nt', F32, "d_gdn_out")
    dw_gproj = _matmul(yb_pre, dyb, 'tn', F32, "dw_gdn_proj")
    dpr, dpk, dpv, dplo, dpp, dml, dwl = _rwkv_backward(p_cat, p['ppack'], p['mulo'], p['wl'], st_r, d_ya_pre)
    dq, dk, dv, dz, dab, dcw_g, dgp = _gdn_backward(p_cat, p['cwpack'], p['gpar'], st_g, d_yb_pre)
    t = x.shape[0]
    dp_cat = jnp.concatenate([dga, dgb, dpr, dpk, dpv, dq, dk, dv, dz, dplo.astype(BF16), dab.astype(BF16),
                              jnp.zeros((t, LANES), BF16)], axis=1)
    dw_cat = _matmul(dp_cat, u, 'tn', BF16, "dw_in")
    dw_in = jnp.concatenate([dw_cat[OFF_RKV:OFF_QKV].reshape(3 * RWKV_HEADS, LANES, d)[:, :RWKV_HEAD_DIM].reshape(-1, d),
                             dw_cat[OFF_LO:OFF_AB], dw_cat[OFF_QKV:OFF_Z], dw_cat[OFF_Z:OFF_LO], dw_cat[OFF_AB:OFF_AB + 8],
                             dw_cat[OFF_GA:OFF_GB], dw_cat[OFF_GB:OFF_RKV]], axis=0)
    token = push_grads({'w_out': dw_out, 'w_in': dw_in})
    du = _matmul(dp_cat, p['w_cat'], 'nn', F32, "d_norm1")
    grad_x, dg1 = _rw_backward(_norm_skip_fn, [full_w(x)], [p['g1'] + token], [full_w(du), full_w(dx1)], [F32], "norm1_bwd")

    heads = lambda row: dpp[:, row, :RWKV_HEAD_DIM].reshape(-1)
    lora = lambda j, lo_, hi_: jnp.transpose(dwl[:, j, lo_:hi_, :RWKV_HEAD_DIM], (1, 0, 2)).reshape(hi_ - lo_, RWKV_WIDTH)
    grads = {
        'norm1_g': dg1[0],
        'w_in': dw_in,
        'rwkv_mu': jnp.concatenate([heads(0), heads(1), heads(2), jnp.sum(dml[:, 0, :], axis=0)]),
        'rwkv_w0': heads(3), 'rwkv_a0': heads(4), 'rwkv_k_k': heads(5), 'rwkv_k_a': heads(6),
        'rwkv_ln_w': heads(7), 'rwkv_ln_b': heads(8), 'rwkv_r_k': heads(9).reshape(RWKV_HEADS, RWKV_HEAD_DIM),
        'rwkv_w2': lora(0, 0, 64), 'rwkv_a2': lora(1, 64, 128), 'rwkv_g2': lora(2, 128, 256),
        'rwkv_proj': dw_rproj.reshape(RWKV_HEADS, LANES, -1)[:, :RWKV_HEAD_DIM].reshape(RWKV_WIDTH, -1),
        'gdn_conv_w': jnp.transpose(dcw_g[:, :, :GDN_CONV, :], (2, 1, 0, 3)).reshape(GDN_CONV, 3 * GDN_WIDTH),
        'gdn_a_log': jnp.sum(dgp[:, 0, :GDN_HEADS], axis=0), 'gdn_dt_bias': jnp.sum(dgp[:, 1, :GDN_HEADS], axis=0),
        'gdn_norm_w': jnp.sum(dgp[:, 2, :], axis=0),
        'gdn_proj': dw_gproj, 'w_out': dw_out, 'norm2_g': dg2[0], 'ffn_up': dw_up, 'ffn_conv_w': dcw_f,
        'ffn_down': dw_down, 'final_g': dgf[0],
    }
    return loss_vec, grad_x, grads


def kernel(x, norm1_g, w_in, rwkv_mu, rwkv_w0, rwkv_w2, rwkv_a0, rwkv_a2, rwkv_g2, rwkv_k_k, rwkv_k_a, rwkv_r_k, rwkv_ln_w, rwkv_ln_b, rwkv_proj, gdn_conv_w, gdn_a_log, gdn_dt_bias, gdn_norm_w, gdn_proj, w_out, norm2_g, ffn_up, ffn_conv_w, ffn_down, final_g, loss_target, m_norm1_g, m_w_in, m_rwkv_mu, m_rwkv_w0, m_rwkv_w2, m_rwkv_a0, m_rwkv_a2, m_rwkv_g2, m_rwkv_k_k, m_rwkv_k_a, m_rwkv_r_k, m_rwkv_ln_w, m_rwkv_ln_b, m_rwkv_proj, m_gdn_conv_w, m_gdn_a_log, m_gdn_dt_bias, m_gdn_norm_w, m_gdn_proj, m_w_out, m_norm2_g, m_ffn_up, m_ffn_conv_w, m_ffn_down, m_final_g, v_norm1_g, v_w_in, v_rwkv_mu, v_rwkv_w0, v_rwkv_w2, v_rwkv_a0, v_rwkv_a2, v_rwkv_g2, v_rwkv_k_k, v_rwkv_k_a, v_rwkv_r_k, v_rwkv_ln_w, v_rwkv_ln_b, v_rwkv_proj, v_gdn_conv_w, v_gdn_a_log, v_gdn_dt_bias, v_gdn_norm_w, v_gdn_proj, v_w_out, v_norm2_g, v_ffn_up, v_ffn_conv_w, v_ffn_down, v_final_g):
    given = dict(zip(WEIGHT_NAMES, (norm1_g, w_in, rwkv_mu, rwkv_w0, rwkv_w2, rwkv_a0, rwkv_a2, rwkv_g2, rwkv_k_k, rwkv_k_a, rwkv_r_k,
                                    rwkv_ln_w, rwkv_ln_b, rwkv_proj, gdn_conv_w, gdn_a_log, gdn_dt_bias, gdn_norm_w, gdn_proj, w_out,
                                    norm2_g, ffn_up, ffn_conv_w, ffn_down, final_g)))
    mom1 = dict(zip(WEIGHT_NAMES, (m_norm1_g, m_w_in, m_rwkv_mu, m_rwkv_w0, m_rwkv_w2, m_rwkv_a0, m_rwkv_a2, m_rwkv_g2, m_rwkv_k_k,
                                   m_rwkv_k_a, m_rwkv_r_k, m_rwkv_ln_w, m_rwkv_ln_b, m_rwkv_proj, m_gdn_conv_w, m_gdn_a_log,
                                   m_gdn_dt_bias, m_gdn_norm_w, m_gdn_proj, m_w_out, m_norm2_g, m_ffn_up, m_ffn_conv_w, m_ffn_down,
                                   m_final_g)))
    mom2 = dict(zip(WEIGHT_NAMES, (v_norm1_g, v_w_in, v_rwkv_mu, v_rwkv_w0, v_rwkv_w2, v_rwkv_a0, v_rwkv_a2, v_rwkv_g2, v_rwkv_k_k,
                                   v_rwkv_k_a, v_rwkv_r_k, v_rwkv_ln_w, v_rwkv_ln_b, v_rwkv_proj, v_gdn_conv_w, v_gdn_a_log,
                                   v_gdn_dt_bias, v_gdn_norm_w, v_gdn_proj, v_w_out, v_norm2_g, v_ffn_up, v_ffn_conv_w, v_ffn_down,
                                   v_final_g)))
    def strip(n, a):
        a = a if n == 'final_g' else a.reshape(a.shape[1:])
        return a.T if n in TRANSPOSED else a

    local = {n: strip(n, a) for n, a in given.items()}
    shard_shapes = {n: local[n].shape for n in SHARD_AXIS}
    sharded = BIG_SHARDED + SMALL_SHARDED

    late_names = [n for n in BIG_SHARDED if n != 'w_in']
    g_in = _all_gather_two_level(_row_pack([local['w_in']], BF16, 16), "gather_w_in")
    g_small = _all_gather_two_level(_flat_pack([local[n] for n in SMALL_SHARDED], F32, SUBLANES), "gather_small")
    late_pack, g_in, g_small = lax.optimization_barrier((_row_pack([local[n] for n in late_names], BF16, 16), g_in, g_small))
    send_sems, recv_sems, late_pack, landing, token = _gather_start(late_pack, "gather_late_start")
    full = _unpack_row_gathered(g_in, ['w_in'], shard_shapes)
    full.update(_unpack_gathered(g_small, SMALL_SHARDED, shard_shapes))
    rep = {n: local[n] for n in REPLICATED}
    rep['norm1_g'] = rep['norm1_g'] + token[0, 0]

    def late_weights(after):
        got = _gather_wait(send_sems, recv_sems, late_pack, landing, after, "gather_late_wait")
        me = 4 * lax.axis_index("x") + 2 * lax.axis_index("y") + lax.axis_index("c")
        slot = lax.broadcasted_iota(jnp.int32, (N_DEV, 1, 1), 0)
        got = jnp.where(slot == me, late_pack[None], got)
        return _prepare_late_weights(_unpack_row_gathered(got, late_names, shard_shapes))

    pushes = []
    me = 4 * lax.axis_index("x") + 2 * lax.axis_index("y") + lax.axis_index("c")
    slot = lax.broadcasted_iota(jnp.int32, (N_DEV, 1, 1), 0)

    def push_grads(group):
        names = list(group)
        slabs = [group[n].reshape(N_DEV, -1, group[n].shape[1]) for n in names]
        send_sems, recv_sems, slabs, landings, token = _slab_push_start(slabs, "grad_push_start_" + "_".join(names))
        pushes.append((names, send_sems, recv_sems, slabs, landings))
        return token[0, 0]

    loss_vec, grad_x, grads = _local_step(x[0], loss_target[0], _prepare_weights(full, rep), late_weights, push_grads)

    landed = {}
    for names, send_sems, recv_sems, slabs, landings in pushes:
        got = _slab_push_wait(send_sems, recv_sems, slabs, landings, (grad_x,), "grad_push_wait_" + "_".join(names))
        for n, slab, land in zip(names, slabs, got):
            landed[n] = (land, slab)

    small_sharded = ['rwkv_proj', 'gdn_proj'] + SMALL_SHARDED
    small_names = small_sharded + REPLICATED
    rep_vec = jnp.concatenate([grads[n].reshape(-1) for n in REPLICATED] + [loss_vec[0, 0:1]])
    slab_small = jnp.concatenate([_shard_major(grads[n], SHARD_AXIS[n]) for n in small_sharded] +
                                 [jnp.broadcast_to(rep_vec[None], (N_DEV, rep_vec.shape[0]))], axis=1)
    small_rows = -(-slab_small.shape[1] // (PACK_W * SUBLANES)) * SUBLANES
    slab_small = jnp.pad(slab_small, ((0, 0), (0, small_rows * PACK_W - slab_small.shape[1]))).reshape(N_DEV, small_rows, PACK_W)
    core = lax.axis_index("c").astype(jnp.int32).reshape(1)
    (from_sibling,) = _pair_exchange([slab_small], "grad_pair_exchange")
    chip_small = _pair_sum(slab_small, from_sibling, core, F32, "grad_pair_sum_small")
    (parts_small,) = _chip_exchange([chip_small], "grad_chip_exchange")

    def pack_local(src):
        flat = jnp.concatenate([strip(n, src[n]).reshape(-1) for n in small_names])
        return jnp.pad(flat, (0, small_rows * PACK_W - flat.shape[0])).reshape(small_rows, PACK_W)

    results = [({}, None) for _ in range(4)]
    me_arr = me.astype(jnp.int32).reshape(1)
    for n in ROW_SHARDED:
        packs = _reduce_landed_adamw(*landed[n], me_arr, local[n], strip(n, mom1[n]), strip(n, mom2[n]), "adamw_" + n)
        for (out, _), pk in zip(results, packs):
            out[n] = (pk.T if n in TRANSPOSED else pk).reshape(given[n].shape)
    packs = _reduce_adamw(parts_small, pack_local(given), pack_local(mom1), pack_local(mom2), "adamw_small")
    for i, pk in enumerate(packs):
        flat, off = pk.reshape(-1), 0
        for n in small_names:
            size = int(np.prod(given[n].shape))
            results[i][0][n] = flat[off:off + size].reshape(given[n].shape)
            off += size
        results[i] = (results[i][0], flat[off])
    (g_out, loss), (d_out, _), (m_out, _), (v_out, _) = results
    return (loss, grad_x[None], *[g_out[n] for n in WEIGHT_NAMES], *[d_out[n] for n in WEIGHT_NAMES],
            *[m_out[n] for n in WEIGHT_NAMES], *[v_out[n] for n in WEIGHT_NAMES])
```

```python
import functools

import jax
import jax.numpy as jnp
import numpy as np
from jax import lax
from jax.experimental import pallas as pl
from jax.experimental.pallas import tpu as pltpu

F32 = jnp.float32
BF16 = jnp.bfloat16

N_DEV = 8
D_MODEL = 1024
CHUNK = 64
RWKV_HEADS = 8
RWKV_HEAD_DIM = 64
RWKV_WIDTH = 512
GDN_HEADS = 4
GDN_HEAD_DIM = 128
GDN_WIDTH = 512
GDN_CONV = 4
FFN_HIDDEN = 2816
FFN_CONV = 3
NORM_EPS = 1e-6
L2_EPS = 1e-6
RWKV_GN_EPS = 64e-5
LANES = 128
SUBLANES = 8
VMEM_LIMIT = 56 * 1024 * 1024

ADAM_LR = 0.001
ADAM_B1 = 0.9
ADAM_B2 = 0.999
ADAM_EPS = 1e-08
ADAM_WD = 0.01
ADAM_STEP = 10

OFF_GA, OFF_GB, OFF_RKV, OFF_QKV, OFF_Z, OFF_LO, OFF_AB, CAT_W = 0, 1024, 2048, 5120, 6656, 7168, 7424, 7680
RWKV_HB = 8
RWKV_STEP_CHUNKS = 2
RWKV_TILE = RWKV_STEP_CHUNKS * CHUNK
RWKV_FWD_TILE = 2 * RWKV_TILE
GDN_HB = 4
GDN_STEP_CHUNKS = 4
GDN_TILE = GDN_STEP_CHUNKS * CHUNK

WEIGHT_NAMES = ['norm1_g', 'w_in', 'rwkv_mu', 'rwkv_w0', 'rwkv_w2', 'rwkv_a0', 'rwkv_a2', 'rwkv_g2', 'rwkv_k_k', 'rwkv_k_a',
                'rwkv_r_k', 'rwkv_ln_w', 'rwkv_ln_b', 'rwkv_proj', 'gdn_conv_w', 'gdn_a_log', 'gdn_dt_bias', 'gdn_norm_w',
                'gdn_proj', 'w_out', 'norm2_g', 'ffn_up', 'ffn_conv_w', 'ffn_down', 'final_g']
BIG_SHARDED = ['w_in', 'ffn_up', 'ffn_down', 'w_out', 'rwkv_proj', 'gdn_proj']
SMALL_SHARDED = ['rwkv_w2', 'rwkv_a2', 'rwkv_g2', 'gdn_conv_w', 'ffn_conv_w']
TRANSPOSED = ('w_in', 'ffn_up')
SHARD_AXIS = {'w_in': 0, 'ffn_up': 0, 'ffn_down': 0, 'w_out': 0, 'rwkv_proj': 1, 'gdn_proj': 1,
              'rwkv_w2': 1, 'rwkv_a2': 1, 'rwkv_g2': 1, 'gdn_conv_w': 1, 'ffn_conv_w': 1}
REPLICATED = [n for n in WEIGHT_NAMES if n not in SHARD_AXIS]
ROW_SHARDED = ['w_in', 'ffn_up', 'ffn_down', 'w_out']


def _cparams(sem=None):
    kw = dict(vmem_limit_bytes=VMEM_LIMIT)
    if sem is not None:
        kw['dimension_semantics'] = sem
    return pltpu.CompilerParams(**kw)


_NN, _NT, _TN = 'nn', 'nt', 'tn'
_DIMS_2D = {'nn': (((1,), (0,)), ((), ())), 'nt': (((1,), (1,)), ((), ())), 'tn': (((0,), (0,)), ((), ()))}
_DIMS_3D = {'nn': (((2,), (1,)), ((0,), (0,))), 'nt': (((2,), (2,)), ((0,), (0,))), 'tn': (((1,), (1,)), ((0,), (0,)))}


def _dg(a, b, kind):
    return lax.dot_general(a, b, (_DIMS_2D if a.ndim == 2 else _DIMS_3D)[kind], preferred_element_type=F32)


def _dot1(a, b, kind):
    return _dg(a.astype(BF16), b.astype(BF16), kind)


@jax.custom_vjp
def _dhi(a, b):
    return _dot1(a, b, _NN)


_dhi.defvjp(lambda a, b: (_dot1(a, b, _NN), (a, b)),
            lambda res, ct: (_dot1(ct, res[1], _NT), _dot1(res[0], ct, _TN)))


@jax.custom_vjp
def _dnt(a, b):
    return _dot1(a, b, _NT)


_dnt.defvjp(lambda a, b: (_dot1(a, b, _NT), (a, b)),
            lambda res, ct: (_dot1(ct, res[1], _NN), _dot1(ct, res[0], _TN)))


@jax.custom_vjp
def _dtn(a, b):
    return _dot1(a, b, _TN)


_dtn.defvjp(lambda a, b: (_dot1(a, b, _TN), (a, b)),
            lambda res, ct: (_dot1(res[1], ct, _NT), _dot1(res[0], ct, _NN)))


def _split3(x):
    x1 = x.astype(BF16)
    r1 = x - x1.astype(F32)
    x2 = r1.astype(BF16)
    return x1, x2, (r1 - x2.astype(F32)).astype(BF16)


def _dot_exact_lhs(sel, x, kind):
    parts = [_dg(sel, xi, kind) for xi in _split3(x)]
    return parts[0] + parts[1] + parts[2]


def _tril_ones(like):
    c = like.shape[-2]
    ri, ci = _iotas(c)
    return jnp.broadcast_to((ri >= ci).astype(BF16), like.shape[:-2] + (c, c))


@jax.custom_vjp
def _cumsum_rows(x):
    return _dot_exact_lhs(_tril_ones(x), x, _NN)


_cumsum_rows.defvjp(lambda x: (_dot_exact_lhs(_tril_ones(x), x, _NN), None),
                    lambda _, ct: (_dot_exact_lhs(_tril_ones(ct), ct, _TN),))


@jax.custom_vjp
def _lane_sum_as_row(x):
    return _dot_exact_lhs(jnp.ones(x.shape, BF16), x, _NT)


def _lane_sum_as_row_bwd(_, ct):
    ones = jnp.ones(ct.shape[:-1] + (LANES,), BF16)
    parts = [_dg(ci, ones, _TN) for ci in _split3(ct)]
    return (parts[0] + parts[1] + parts[2],)


_lane_sum_as_row.defvjp(lambda x: (_dot_exact_lhs(jnp.ones(x.shape, BF16), x, _NT), None), _lane_sum_as_row_bwd)


def _shift_rows(x, halo, s):
    rows = lax.broadcasted_iota(jnp.int32, x.shape, 0)
    out = pltpu.roll(x, s, 0)
    for i in range(s):
        out = jnp.where(rows == i, halo[SUBLANES - s + i:SUBLANES - s + i + 1, :], out)
    return out


def _unshift_rows(g, carry, s):
    c = g.shape[0]
    rows = lax.broadcasted_iota(jnp.int32, g.shape, 0)
    out = pltpu.roll(g, c - s, 0)
    for i in range(s):
        out = jnp.where(rows == c - s + i, carry[i:i + 1, :], out)
    return out


def _sigmoid_plain(z):
    return 1.0 / (1.0 + jnp.exp(-z))


def _sigmoid_value(z):
    t = jnp.exp(-jnp.abs(z))
    r = 1.0 / (1.0 + t)
    return jnp.where(z >= 0, r, t * r)


@jax.custom_vjp
def _sigmoid(z):
    return _sigmoid_value(z)


def _sigmoid_fwd(z):
    s = _sigmoid_value(z)
    return s, s


_sigmoid.defvjp(_sigmoid_fwd, lambda s, ct: (ct * s * (1.0 - s),))


def _silu(z):
    return z * _sigmoid(z)


def _softplus(z):
    return jnp.maximum(z, 0.0) + jnp.log(1.0 + jnp.exp(-jnp.abs(z)))


def _rms(t, gain):
    return t * lax.rsqrt(jnp.mean(t * t, axis=-1, keepdims=True) + NORM_EPS) * gain


def _iotas(c):
    return lax.broadcasted_iota(jnp.int32, (c, c), 0), lax.broadcasted_iota(jnp.int32, (c, c), 1)


def _unit_lower_inverse(xm, eye):
    t = eye + xm
    p = xm
    for _ in range(5):
        p = _dhi(p, p)
        t = t + _dhi(t, p)
    return t


def _rwkv_head(pr, pk, pv, plo, qr, qk, qv, qlo, s0, pp, mulo, wl, starts=None):
    c = pr.shape[1]
    n_heads = s0.shape[0]
    n_chunks = pr.shape[0] // n_heads
    ri, ci = _iotas(c)
    if n_chunks > 1:
        pp = jnp.concatenate([pp] * n_chunks, axis=0)
        wl = jnp.concatenate([wl] * n_chunks, axis=0)

    def mix(p, q, mu):
        return p + (q - p) * mu

    r = mix(pr, qr, pp[:, 0:1])
    k = mix(pk, qk, pp[:, 1:2])
    v = mix(pv, qv, pp[:, 2:3])
    lo = mix(plo, qlo, mulo)
    w0, a0, k_k, k_a, ln_w, ln_b, r_k = (pp[:, i:i + 1] for i in range(3, 10))

    def per_head(t):
        return jnp.concatenate([jnp.broadcast_to(t[i], (n_heads,) + t.shape[1:]) for i in range(n_chunks)], axis=0)

    zw = _dhi(per_head(jnp.tanh(lo)), wl[:, 0])
    za = _dhi(per_head(lo), wl[:, 1])
    g = _dhi(per_head(_sigmoid(lo)), wl[:, 2])
    w_log = -_softplus(-(w0 + zw)) - 0.5
    lw = -jnp.exp(w_log)
    a = _sigmoid(a0 + za)
    kk = k * k_k
    kk = kk * lax.rsqrt(jnp.sum(kk * kk, axis=-1, keepdims=True) + L2_EPS)
    k2 = k * (1.0 + (a - 1.0) * k_a)
    an = -kk
    b = kk * a
    causal = ri >= ci
    strict = ri > ci
    eye = (ri == ci).astype(F32)
    cl = _cumsum_rows(lw)
    ecl = jnp.exp(-cl)
    at = an * jnp.exp(cl - lw)
    bt = b * ecl
    kt = k2 * ecl
    rt = r * jnp.exp(cl)
    a_ab = jnp.where(strict, _dnt(at, bt), 0.0)
    a_ak = jnp.where(strict, _dnt(at, kt), 0.0)
    tinv = _unit_lower_inverse(a_ab, eye)
    akv = _dhi(a_ak, v)
    r_b = jnp.where(causal, _dnt(rt, bt), 0.0)
    rkv = _dhi(jnp.where(causal, _dnt(rt, kt), 0.0), v)
    cl_end = jnp.sum(lw, axis=1, keepdims=True)
    dec_end = jnp.exp(cl_end - cl)
    b_end = b * dec_end
    sv = _dtn(v, k2 * dec_end)
    e_end = jnp.exp(cl_end)
    state, ys = s0, []
    for i in range(n_chunks):
        sl = slice(i * n_heads, (i + 1) * n_heads)
        if starts is not None:
            starts.append(state)
        u = _dhi(tinv[sl], _dnt(at[sl], state) + akv[sl])
        ys.append(_dnt(rt[sl], state) + _dhi(r_b[sl], u) + rkv[sl])
        state = state * e_end[sl] + _dtn(u, b_end[sl]) + sv[sl]
    y = jnp.concatenate(ys, axis=0) if n_chunks > 1 else ys[0]
    s1 = state
    m = (lax.broadcasted_iota(jnp.int32, (1, LANES), 1) < RWKV_HEAD_DIM).astype(F32)
    mean = jnp.sum(y, axis=-1, keepdims=True) * (1.0 / RWKV_HEAD_DIM)
    yc = (y - mean) * m
    var = jnp.sum(yc * yc, axis=-1, keepdims=True) * (1.0 / RWKV_HEAD_DIM)
    yn = yc * lax.rsqrt(var + RWKV_GN_EPS) * ln_w + ln_b
    y2 = yn + jnp.sum(r * k2 * r_k, axis=-1, keepdims=True) * v
    return y2 * g, s1


def _gdn_head(xq, xk, xv, z, ab, s0, cw, gp, oha, ohb):
    c = z.shape[1]
    n_heads = s0.shape[0]
    n_chunks = z.shape[0] // n_heads
    ri, ci = _iotas(c)
    cw = jnp.concatenate([cw] * n_chunks, axis=0) if n_chunks > 1 else cw

    def conv(xs, w):
        out = xs[0] * w[:, GDN_CONV - 1:GDN_CONV]
        for s in range(1, GDN_CONV):
            out = out + xs[s] * w[:, GDN_CONV - 1 - s:GDN_CONV - s]
        return out

    q = _silu(conv(xq, cw[:, 0]))
    k = _silu(conv(xk, cw[:, 1]))
    v = _silu(conv(xv, cw[:, 2]))
    q = q * lax.rsqrt(jnp.sum(q * q, axis=-1, keepdims=True) + L2_EPS) * (GDN_HEAD_DIM ** -0.5)
    k = k * lax.rsqrt(jnp.sum(k * k, axis=-1, keepdims=True) + L2_EPS)
    gg = -jnp.exp(gp[0:1]) * _softplus(ab + gp[1:2])
    beta = jnp.sum(_sigmoid(ab) * ohb, axis=-1, keepdims=True)
    causal = ri >= ci
    strict = ri > ci
    eye = (ri == ci).astype(F32)
    gcm = _cumsum_rows(gg * oha)
    gc = jnp.sum(gcm, axis=-1, keepdims=True)
    gc_row = _lane_sum_as_row(gcm)
    dec = jnp.where(causal, jnp.exp(jnp.where(causal, gc - gc_row, 0.0)), 0.0)
    kb = k * beta
    vb = v * beta
    lm = jnp.where(strict, _dnt(kb, k) * dec, 0.0)
    tinv = _unit_lower_inverse(-lm, eye)
    egc = jnp.exp(gc)
    u = _dhi(tinv, vb)
    wk = _dhi(tinv, kb * egc)
    attn = jnp.where(causal, _dnt(q, k) * dec, 0.0)
    g_last = gc[:, c - 1:c, :]
    q_dec = q * egc
    k_dec = k * jnp.exp(g_last - gc)
    e_last = jnp.exp(g_last)
    state, outs = s0, []
    for i in range(n_chunks):
        sl = slice(i * n_heads, (i + 1) * n_heads)
        v_new = u[sl] - _dhi(wk[sl], state)
        outs.append(_dhi(q_dec[sl], state) + _dhi(attn[sl], v_new))
        state = state * e_last[sl] + _dtn(k_dec[sl], v_new)
    o = jnp.concatenate(outs, axis=0) if n_chunks > 1 else outs[0]
    return _rms(o, gp[2:3]) * _silu(z), state


def _head_id(grp, i, per_step, heads):
    return i if per_step == heads else grp * per_step + i


def _head_range(grp, per_step, heads):
    return slice(None) if per_step == heads else pl.ds(grp * per_step, per_step)


def _rwkv_specs(nmap, tile):
    hb, groups = RWKV_HB, RWKV_HEADS // RWKV_HB
    cb = OFF_RKV // (hb * LANES)
    specs = []
    for j in range(3):
        specs.append(pl.BlockSpec((tile, hb * LANES), lambda n, g, j=j: (nmap(n), cb + j * groups + g)))
    specs.append(pl.BlockSpec((tile, 2 * LANES), lambda n, g: (nmap(n), OFF_LO // (2 * LANES))))
    per = tile // SUBLANES
    for j in range(3):
        specs.append(pl.BlockSpec((SUBLANES, hb * LANES),
                                  lambda n, g, j=j: (jnp.maximum(nmap(n) * per - 1, 0), cb + j * groups + g)))
    specs.append(pl.BlockSpec((SUBLANES, 2 * LANES), lambda n, g: (jnp.maximum(nmap(n) * per - 1, 0), OFF_LO // (2 * LANES))))
    specs.append(pl.BlockSpec((hb, 16, LANES), lambda n, g: (g, 0, 0)))
    specs.append(pl.BlockSpec((1, 2 * LANES), lambda n, g: (0, 0)))
    specs.append(pl.BlockSpec((hb, 3, 2 * LANES, LANES), lambda n, g: (g, 0, 0, 0)))
    return specs


def _rwkv_operands(refs, halos, live):
    pr, pk, pv, plo = refs
    hr, hk, hv, hlo = halos
    cur, prev = [], []
    for x, hx in ((pr, hr), (pk, hk), (pv, hv)):
        tiles = [_lane_block(x, h) for h in range(RWKV_HB)]
        cur.append(_chunk_batch(tiles))
        prev.append(_chunk_batch([_shift_rows(t_, _lane_block(hx, h) * live, 1) for h, t_ in enumerate(tiles)]))
    lo = plo[...]
    cur.append(_chunk_batch([lo]))
    prev.append(_chunk_batch([_shift_rows(lo, hlo[...] * live, 1)]))
    return cur, prev


def _rwkv_forward(p_cat, ppack, mulo, wl):
    t = p_cat.shape[0]
    n_chunks = t // RWKV_FWD_TILE
    per_step = RWKV_FWD_TILE // RWKV_TILE

    def body(pr, pk, pv, plo, hr, hk, hv, hlo, pp, ml, w, out, st_out, s_scr):
        n, grp = pl.program_id(0), pl.program_id(1)

        hsl = _head_range(grp, RWKV_HB, RWKV_HEADS)

        @pl.when(n == 0)
        def _():
            s_scr[hsl] = jnp.zeros((RWKV_HB, LANES, LANES), F32)

        live = (n > 0).astype(F32)
        cur, prev = _rwkv_operands((pr, pk, pv, plo), (hr, hk, hv, hlo), live)
        s0 = s_scr[hsl]
        starts = []
        o, s1 = _rwkv_head(*cur, *prev, s0, pp[...], ml[...], w[...], starts=starts)
        for i in range(per_step):
            st_out[i] = starts[i * RWKV_STEP_CHUNKS]
        for h, tile in enumerate(_head_tiles(o, RWKV_HB)):
            out[:, h * LANES:(h + 1) * LANES] = tile.astype(out.dtype)
        s_scr[hsl] = s1

    return pl.pallas_call(
        body, name="rwkv_fwd", grid=(n_chunks, RWKV_HEADS // RWKV_HB),
        in_specs=_rwkv_specs(lambda n: n, RWKV_FWD_TILE),
        out_specs=(pl.BlockSpec((RWKV_FWD_TILE, RWKV_HB * LANES), lambda n, g: (n, g)),
                   pl.BlockSpec((per_step, RWKV_HB, LANES, LANES), lambda n, g: (n, g, 0, 0))),
        out_shape=(jax.ShapeDtypeStruct((t, RWKV_HEADS * LANES), BF16),
                   jax.ShapeDtypeStruct((n_chunks * per_step, RWKV_HEADS, LANES, LANES), F32)),
        scratch_shapes=[pltpu.VMEM((RWKV_HEADS, LANES, LANES), F32)],
        compiler_params=_cparams(("arbitrary", "arbitrary")),
    )(p_cat, p_cat, p_cat, p_cat, p_cat, p_cat, p_cat, p_cat, ppack, mulo, wl)


def _rwkv_backward(p_cat, ppack, mulo, wl, states, d_out):
    t = p_cat.shape[0]
    n_chunks = t // RWKV_TILE
    last = n_chunks - 1

    def body(pr, pk, pv, plo, hr, hk, hv, hlo, pp, ml, w, st, dy, dpr, dpk, dpv, dplo, dpp, dml, dw, ds_scr, car_scr, carlo_scr):
        n, grp = pl.program_id(0), pl.program_id(1)

        hsl = _head_range(grp, RWKV_HB, RWKV_HEADS)
        gi = _head_id(grp, 0, 1, RWKV_HEADS // RWKV_HB)

        @pl.when(n == 0)
        def _():
            ds_scr[hsl] = jnp.zeros((RWKV_HB, LANES, LANES), F32)
            car_scr[hsl] = jnp.zeros((RWKV_HB, 3 * SUBLANES, LANES), F32)
            carlo_scr[gi] = jnp.zeros((SUBLANES, 2 * LANES), F32)

        @pl.when((n == 0) & (grp == 0))
        def _():
            dpp[...] = jnp.zeros(dpp.shape, F32)
            dml[...] = jnp.zeros(dml.shape, F32)
            dw[...] = jnp.zeros(dw.shape, F32)

        live = (n < last).astype(F32)
        cur, prev = _rwkv_operands((pr, pk, pv, plo), (hr, hk, hv, hlo), live)
        _, vjp = jax.vjp(_rwkv_head, *cur, *prev, st[...], pp[...], ml[...], w[...])
        g = vjp((_chunk_batch([_lane_block(dy, h) for h in range(RWKV_HB)]), ds_scr[hsl]))
        outs = (dpr, dpk, dpv)
        d_cur = [_head_tiles(g[j], RWKV_HB) for j in range(3)]
        d_prev = [_head_tiles(g[4 + j], RWKV_HB) for j in range(3)]
        for i in range(RWKV_HB):
            sl = slice(i * LANES, (i + 1) * LANES)
            h = _head_id(grp, i, RWKV_HB, RWKV_HEADS)
            car = car_scr[h]
            for j in range(3):
                tot = d_cur[j][i] + _unshift_rows(d_prev[j][i], car[SUBLANES * j:SUBLANES * (j + 1), :], 1)
                outs[j][:, sl] = tot.astype(outs[j].dtype)
                car_scr[h, SUBLANES * j:SUBLANES * (j + 1), :] = d_prev[j][i][0:SUBLANES, :]
        (dlo_cur,), (dlo_prev,) = _head_tiles(g[3], 1), _head_tiles(g[7], 1)
        dlo = dlo_cur + _unshift_rows(dlo_prev, carlo_scr[gi], 1)
        carlo_scr[gi] = dlo_prev[0:SUBLANES, :]
        ds_scr[hsl] = g[8]
        dpp[hsl] += g[9]
        dml[0, 0:1, :] += g[10]
        dw[hsl] += g[11]

        @pl.when(grp == 0)
        def _():
            dplo[...] = dlo

        @pl.when(grp > 0)
        def _():
            dplo[...] += dlo

    rev = lambda n: last - n
    in_specs = _rwkv_specs(rev, RWKV_TILE) + [
        pl.BlockSpec((None, RWKV_HB, LANES, LANES), lambda n, g: (rev(n), g, 0, 0)),
        pl.BlockSpec((RWKV_TILE, RWKV_HB * LANES), lambda n, g: (rev(n), g)),
    ]
    hw = RWKV_HEADS * LANES
    return pl.pallas_call(
        body, name="rwkv_bwd", grid=(n_chunks, RWKV_HEADS // RWKV_HB),
        in_specs=in_specs,
        out_specs=(pl.BlockSpec((RWKV_TILE, RWKV_HB * LANES), lambda n, g: (rev(n), g)),
                   pl.BlockSpec((RWKV_TILE, RWKV_HB * LANES), lambda n, g: (rev(n), g)),
                   pl.BlockSpec((RWKV_TILE, RWKV_HB * LANES), lambda n, g: (rev(n), g)),
                   pl.BlockSpec((RWKV_TILE, 2 * LANES), lambda n, h: (rev(n), 0)),
                   pl.BlockSpec((RWKV_HEADS, 16, LANES), lambda n, h: (0, 0, 0)),
                   pl.BlockSpec((RWKV_HEADS, SUBLANES, 2 * LANES), lambda n, h: (0, 0, 0)),
                   pl.BlockSpec((RWKV_HEADS, 3, 2 * LANES, LANES), lambda n, h: (0, 0, 0, 0))),
        out_shape=(jax.ShapeDtypeStruct((t, hw), BF16), jax.ShapeDtypeStruct((t, hw), BF16), jax.ShapeDtypeStruct((t, hw), BF16),
                   jax.ShapeDtypeStruct((t, 2 * LANES), F32),
                   jax.ShapeDtypeStruct((RWKV_HEADS, 16, LANES), F32),
                   jax.ShapeDtypeStruct((RWKV_HEADS, SUBLANES, 2 * LANES), F32),
                   jax.ShapeDtypeStruct((RWKV_HEADS, 3, 2 * LANES, LANES), F32)),
        scratch_shapes=[pltpu.VMEM((RWKV_HEADS, LANES, LANES), F32),
                        pltpu.VMEM((RWKV_HEADS, 3 * SUBLANES, LANES), F32),
                        pltpu.VMEM((RWKV_HEADS, SUBLANES, 2 * LANES), F32)],
        compiler_params=_cparams(("arbitrary", "arbitrary")),
    )(p_cat, p_cat, p_cat, p_cat, p_cat, p_cat, p_cat, p_cat, ppack, mulo, wl, states, d_out)


def _gdn_specs(nmap):
    per = GDN_TILE // SUBLANES
    hb, groups = GDN_HB, GDN_HEADS // GDN_HB
    cb = OFF_QKV // (hb * LANES)
    specs = []
    for j in range(3):
        specs.append(pl.BlockSpec((GDN_TILE, hb * LANES), lambda n, g, j=j: (nmap(n), cb + j * groups + g)))
    for j in range(3):
        specs.append(pl.BlockSpec((SUBLANES, hb * LANES),
                                  lambda n, g, j=j: (jnp.maximum(nmap(n) * per - 1, 0), cb + j * groups + g)))
    specs.append(pl.BlockSpec((GDN_TILE, hb * LANES), lambda n, g: (nmap(n), OFF_Z // (hb * LANES) + g)))
    specs.append(pl.BlockSpec((GDN_TILE, LANES), lambda n, g: (nmap(n), OFF_AB // LANES)))
    specs.append(pl.BlockSpec((hb, 3, SUBLANES, LANES), lambda n, g: (g, 0, 0, 0)))
    specs.append(pl.BlockSpec((SUBLANES, LANES), lambda n, g: (0, 0)))
    return specs


def _conv_taps(x, halo):
    return (x,) + tuple(_shift_rows(x, halo, s) for s in range(1, GDN_CONV))


def _onehots(grp):
    nb = GDN_STEP_CHUNKS * GDN_HB
    lane = lax.broadcasted_iota(jnp.int32, (nb, 1, LANES), 2)
    head = lax.broadcasted_iota(jnp.int32, (nb, 1, LANES), 0) % GDN_HB + _head_id(grp, 0, GDN_HB, GDN_HEADS)
    return (lane == head).astype(F32), (lane == GDN_HEADS + head).astype(F32)


def _chunk_batch(tiles):
    n_chunks = tiles[0].shape[0] // CHUNK
    return jnp.stack([t_[i * CHUNK:(i + 1) * CHUNK, :] for i in range(n_chunks) for t_ in tiles])


def _head_tiles(batch, n_heads=GDN_HB):
    n_chunks = batch.shape[0] // n_heads
    return [jnp.concatenate([batch[i * n_heads + h] for i in range(n_chunks)], axis=0) for h in range(n_heads)]


def _lane_block(ref, h):
    return ref[:, h * LANES:(h + 1) * LANES]


def _gdn_taps(refs, halos, live):
    out = []
    for x, hx in zip(refs, halos):
        per_head = [_conv_taps(_lane_block(x, h), _lane_block(hx, h) * live) for h in range(GDN_HB)]
        out.append(tuple(_chunk_batch([per_head[h][s] for h in range(GDN_HB)]) for s in range(GDN_CONV)))
    return out


def _gdn_forward(p_cat, cwpack, gpar):
    t = p_cat.shape[0]
    n_chunks = t // GDN_TILE

    def body(xq, xk, xv, hq, hk, hv, z, ab, cw, gp, out, st_out, s_scr):
        n, grp = pl.program_id(0), pl.program_id(1)

        hsl = _head_range(grp, GDN_HB, GDN_HEADS)

        @pl.when(n == 0)
        def _():
            s_scr[hsl] = jnp.zeros((GDN_HB, LANES, LANES), F32)

        live = (n > 0).astype(F32)
        oha, ohb = _onehots(grp)
        s0 = s_scr[hsl]
        st_out[...] = s0
        taps = _gdn_taps((xq, xk, xv), (hq, hk, hv), live)
        zb = _chunk_batch([_lane_block(z, h) for h in range(GDN_HB)])
        abb = _chunk_batch([ab[...]] * GDN_HB)
        o, s1 = _gdn_head(*taps, zb, abb, s0, cw[...], gp[...], oha, ohb)
        for h, tile in enumerate(_head_tiles(o)):
            out[:, h * LANES:(h + 1) * LANES] = tile.astype(out.dtype)
        s_scr[hsl] = s1

    return pl.pallas_call(
        body, name="gdn_fwd", grid=(n_chunks, GDN_HEADS // GDN_HB),
        in_specs=_gdn_specs(lambda n: n),
        out_specs=(pl.BlockSpec((GDN_TILE, GDN_HB * LANES), lambda n, g: (n, g)),
                   pl.BlockSpec((None, GDN_HB, LANES, LANES), lambda n, g: (n, g, 0, 0))),
        out_shape=(jax.ShapeDtypeStruct((t, GDN_WIDTH), BF16),
                   jax.ShapeDtypeStruct((n_chunks, GDN_HEADS, LANES, LANES), F32)),
        scratch_shapes=[pltpu.VMEM((GDN_HEADS, LANES, LANES), F32)],
        compiler_params=_cparams(("arbitrary", "arbitrary")),
    )(p_cat, p_cat, p_cat, p_cat, p_cat, p_cat, p_cat, p_cat, cwpack, gpar)


def _gdn_backward(p_cat, cwpack, gpar, states, d_out):
    t = p_cat.shape[0]
    n_chunks = t // GDN_TILE
    last = n_chunks - 1

    def body(xq, xk, xv, hq, hk, hv, z, ab, cw, gp, st, dy, dq, dk, dv, dz, dab, dcw, dgp, ds_scr, car_scr):
        n, grp = pl.program_id(0), pl.program_id(1)

        hsl = _head_range(grp, GDN_HB, GDN_HEADS)

        @pl.when(n == 0)
        def _():
            ds_scr[hsl] = jnp.zeros((GDN_HB, LANES, LANES), F32)
            car_scr[hsl] = jnp.zeros((GDN_HB, 3 * GDN_CONV, SUBLANES, LANES), F32)

        @pl.when((n == 0) & (grp == 0))
        def _():
            dcw[...] = jnp.zeros(dcw.shape, F32)
            dgp[...] = jnp.zeros(dgp.shape, F32)

        live = (n < last).astype(F32)
        oha, ohb = _onehots(grp)
        fn = functools.partial(_gdn_head, oha=oha, ohb=ohb)
        taps = _gdn_taps((xq, xk, xv), (hq, hk, hv), live)
        zb = _chunk_batch([_lane_block(z, h) for h in range(GDN_HB)])
        abb = _chunk_batch([ab[...]] * GDN_HB)
        _, vjp = jax.vjp(fn, *taps, zb, abb, st[...], cw[...], gp[...])
        g = vjp((_chunk_batch([_lane_block(dy, h) for h in range(GDN_HB)]), ds_scr[hsl]))
        outs = (dq, dk, dv)
        tap_tiles = [[_head_tiles(g[j][s]) for s in range(GDN_CONV)] for j in range(3)]
        dz_tiles = _head_tiles(g[3])
        for i in range(GDN_HB):
            sl = slice(i * LANES, (i + 1) * LANES)
            h = _head_id(grp, i, GDN_HB, GDN_HEADS)
            for j in range(3):
                tot = tap_tiles[j][0][i]
                for s in range(1, GDN_CONV):
                    slot = j * GDN_CONV + s
                    tot = tot + _unshift_rows(tap_tiles[j][s][i], car_scr[h, slot], s)
                    car_scr[h, slot] = tap_tiles[j][s][i][0:SUBLANES, :]
                outs[j][:, sl] = tot.astype(outs[j].dtype)
            dz[:, sl] = dz_tiles[i].astype(dz.dtype)
        dab_tiles = _head_tiles(g[4])
        dab_sum = dab_tiles[0]
        for h in range(1, GDN_HB):
            dab_sum = dab_sum + dab_tiles[h]
        ds_scr[hsl] = g[5]
        dcw[hsl] += g[6]
        dgp[0] += g[7]

        @pl.when(grp == 0)
        def _():
            dab[...] = dab_sum

        @pl.when(grp > 0)
        def _():
            dab[...] += dab_sum

    rev = lambda n: last - n
    in_specs = _gdn_specs(rev) + [
        pl.BlockSpec((None, GDN_HB, LANES, LANES), lambda n, g: (rev(n), g, 0, 0)),
        pl.BlockSpec((GDN_TILE, GDN_HB * LANES), lambda n, g: (rev(n), g)),
    ]
    blk = pl.BlockSpec((GDN_TILE, GDN_HB * LANES), lambda n, g: (rev(n), g))
    return pl.pallas_call(
        body, name="gdn_bwd", grid=(n_chunks, GDN_HEADS // GDN_HB),
        in_specs=in_specs,
        out_specs=(blk, blk, blk, blk,
                   pl.BlockSpec((GDN_TILE, LANES), lambda n, h: (rev(n), 0)),
                   pl.BlockSpec((GDN_HEADS, 3, SUBLANES, LANES), lambda n, h: (0, 0, 0, 0)),
                   pl.BlockSpec((GDN_HEADS, SUBLANES, LANES), lambda n, h: (0, 0, 0))),
        out_shape=(jax.ShapeDtypeStruct((t, GDN_WIDTH), BF16), jax.ShapeDtypeStruct((t, GDN_WIDTH), BF16),
                   jax.ShapeDtypeStruct((t, GDN_WIDTH), BF16), jax.ShapeDtypeStruct((t, GDN_WIDTH), BF16),
                   jax.ShapeDtypeStruct((t, LANES), F32),
                   jax.ShapeDtypeStruct((GDN_HEADS, 3, SUBLANES, LANES), F32),
                   jax.ShapeDtypeStruct((GDN_HEADS, SUBLANES, LANES), F32)),
        scratch_shapes=[pltpu.VMEM((GDN_HEADS, LANES, LANES), F32),
                        pltpu.VMEM((GDN_HEADS, 3 * GDN_CONV, SUBLANES, LANES), F32)],
        compiler_params=_cparams(("arbitrary", "arbitrary")),
    )(p_cat, p_cat, p_cat, p_cat, p_cat, p_cat, p_cat, p_cat, cwpack, gpar, states, d_out)


MM_VMEM_BUDGET = 44 * 1024 * 1024
MM_MIN_STEPS = 4


def _mm_tiles(mode, m, n, k, out_bytes):
    tms = [t for t in (2048, 1024, 768, 512, 256, 128, 64) if m % t == 0 and (mode != 'tn' or t % LANES == 0)]
    tns = [t for t in (1408, 1024, 768, 512, 256, 128) if n % t == 0]
    tks = [t for t in (2048, 1920, 1408, 1024, 512, 256, 128, 64) if k % t == 0]
    best, best_key = None, None
    for tm in tms:
        for tn in tns:
            for tk in tks:
                nk = k // tk
                vmem = 2 * (tm * tk * 2 + tk * tn * 2 + tm * tn * out_bytes) + (tm * tn * 4 if nk > 1 else 0)
                steps = (m // tm) * (n // tn) * nk
                if vmem > MM_VMEM_BUDGET:
                    continue
                reread = m * k * (n // tn) + k * n * (m // tm)
                key = (steps >= MM_MIN_STEPS, tn if mode == 'tn' else 0, tm * tn * tk, -nk, -reread)
                if best_key is None or key > best_key:
                    best, best_key = (tm, tn, tk), key
    if best is None:
        raise ValueError(f"no matmul tile for {mode} {m}x{n}x{k}")
    return best


_MM_DIMS = {'nn': (((1,), (0,)), ((), ())), 'nt': (((1,), (1,)), ((), ())), 'tn': (((0,), (0,)), ((), ()))}


def _matmul(a, b, mode, out_dtype, name):
    if mode == 'nn':
        (m, k), (k2, n) = a.shape, b.shape
    elif mode == 'nt':
        (m, k), (n, k2) = a.shape, b.shape
    else:
        (k, m), (k2, n) = a.shape, b.shape
    assert k == k2, (a.shape, b.shape, mode)
    tm, tn, tk = _mm_tiles(mode, m, n, k, jnp.dtype(out_dtype).itemsize)
    nk = k // tk
    dims = _MM_DIMS[mode]

    def body(a_ref, b_ref, o_ref, acc_ref):
        kk = pl.program_id(2)
        part = lax.dot_general(a_ref[...], b_ref[...], dims, preferred_element_type=F32)
        if nk == 1:
            o_ref[...] = part.astype(o_ref.dtype)
            return

        @pl.when(kk == 0)
        def _():
            acc_ref[...] = part

        @pl.when((kk > 0) & (kk < nk - 1))
        def _():
            acc_ref[...] += part

        @pl.when(kk == nk - 1)
        def _():
            o_ref[...] = (acc_ref[...] + part).astype(o_ref.dtype)

    a_spec = pl.BlockSpec((tk, tm), lambda i, j, kk: (kk, i)) if mode == 'tn' else pl.BlockSpec((tm, tk), lambda i, j, kk: (i, kk))
    b_spec = pl.BlockSpec((tn, tk), lambda i, j, kk: (j, kk)) if mode == 'nt' else pl.BlockSpec((tk, tn), lambda i, j, kk: (kk, j))
    return pl.pallas_call(
        body, name=name, grid=(m // tm, n // tn, nk),
        in_specs=[a_spec, b_spec],
        out_specs=pl.BlockSpec((tm, tn), lambda i, j, kk: (i, j)),
        out_shape=jax.ShapeDtypeStruct((m, n), out_dtype),
        scratch_shapes=[pltpu.VMEM((tm, tn), F32)],
        compiler_params=_cparams(("parallel", "parallel", "arbitrary")),
    )(a, b)


ROW_TILE = 512


def _row_specs(rows, tm):
    return [pl.BlockSpec((tm, w), lambda i, ci=ci: (i, ci)) for (_, w, ci) in rows]


def _rw_forward(fn, rows, pars, outs, name):
    t = rows[0][0].shape[0]
    tm = min(ROW_TILE, t)
    nr, npar = len(rows), len(pars)

    def body(*refs):
        vals = [r[...].astype(F32) for r in refs[:nr]] + [p[...] for p in refs[nr:nr + npar]]
        res = fn(*vals)
        for o, v in zip(refs[nr + npar:], res):
            o[...] = v.astype(o.dtype)

    return pl.pallas_call(
        body, name=name, grid=(t // tm,),
        in_specs=_row_specs(rows, tm) + [pl.BlockSpec(p.shape, lambda i: (0, 0)) for p in pars],
        out_specs=tuple(pl.BlockSpec((tm, w), lambda i: (i, 0)) for (w, _) in outs),
        out_shape=tuple(jax.ShapeDtypeStruct((t, w), dt) for (w, dt) in outs),
        compiler_params=_cparams(("parallel",)),
    )(*[r[0] for r in rows], *pars)


def _rw_backward(fn, rows, pars, cots, drow_dtypes, name):
    t = rows[0][0].shape[0]
    tm = min(ROW_TILE, t)
    nr, npar, nc = len(rows), len(pars), len(cots)
    keep = [i for i, dt in enumerate(drow_dtypes) if dt is not None]

    def body(*refs):
        vals = [r[...].astype(F32) for r in refs[:nr]] + [p[...] for p in refs[nr:nr + npar]]
        cvals = tuple(c[...].astype(F32) for c in refs[nr + npar:nr + npar + nc])
        orefs = refs[nr + npar + nc:]
        _, vjp = jax.vjp(fn, *vals)
        g = vjp(cvals)
        for o, i in zip(orefs[:len(keep)], keep):
            o[...] = g[i].astype(o.dtype)
        first = pl.program_id(0) == 0
        for o, gi in zip(orefs[len(keep):], g[nr:]):
            @pl.when(first)
            def _(o=o, gi=gi):
                o[...] = gi

            @pl.when(jnp.logical_not(first))
            def _(o=o, gi=gi):
                o[...] += gi

    out_specs = [pl.BlockSpec((tm, rows[i][1]), lambda i_: (i_, 0)) for i in keep] + \
                [pl.BlockSpec(p.shape, lambda i_: (0, 0)) for p in pars]
    out_shape = [jax.ShapeDtypeStruct((t, rows[i][1]), drow_dtypes[i]) for i in keep] + \
                [jax.ShapeDtypeStruct(p.shape, F32) for p in pars]
    return pl.pallas_call(
        body, name=name, grid=(t // tm,),
        in_specs=_row_specs(rows, tm) + [pl.BlockSpec(p.shape, lambda i: (0, 0)) for p in pars] + _row_specs(cots, tm),
        out_specs=tuple(out_specs), out_shape=tuple(out_shape),
        compiler_params=_cparams(("arbitrary",)),
    )(*[r[0] for r in rows], *pars, *[c[0] for c in cots])


def _norm_fn(x, g):
    return (_rms(x, g),)


def _norm_skip_fn(x, g):
    return _rms(x, g), x


def _merge_fn(ga, gb, ya, yb):
    return (_sigmoid(ga) * ya + _sigmoid(gb) * yb,)


def _res_norm_fn(x, mo, g):
    x1 = x + mo
    return x1, _rms(x1, g)


def _loss_head(x1, fo, gf, target, name):
    t, d = x1.shape
    tm = min(ROW_TILE, t)

    def tile_loss(x2, g, tgt):
        err = _rms(x2, g) - tgt
        per_row = jnp.sum(err * err, axis=-1, keepdims=True) * (0.5 / d)
        return jnp.sum(per_row, axis=0, keepdims=True)

    def body(x1_ref, fo_ref, g_ref, t_ref, loss_ref, dx_ref, dxb_ref, dg_ref):
        x2 = x1_ref[...] + fo_ref[...]
        val, vjp = jax.vjp(functools.partial(tile_loss, tgt=t_ref[...]), x2, g_ref[...])
        dx2, dg = vjp(jnp.ones((1, 1), F32))
        dx_ref[...] = dx2
        dxb_ref[...] = dx2.astype(BF16)
        first = pl.program_id(0) == 0

        @pl.when(first)
        def _():
            loss_ref[...] = jnp.broadcast_to(val, loss_ref.shape)
            dg_ref[...] = dg

        @pl.when(jnp.logical_not(first))
        def _():
            loss_ref[...] += jnp.broadcast_to(val, loss_ref.shape)
            dg_ref[...] += dg

    row = pl.BlockSpec((tm, d), lambda i: (i, 0))
    vec = pl.BlockSpec((1, d), lambda i: (0, 0))
    return pl.pallas_call(
        body, name=name, grid=(t // tm,),
        in_specs=[row, row, vec, row],
        out_specs=(pl.BlockSpec((1, LANES), lambda i: (0, 0)), row, row, vec),
        out_shape=(jax.ShapeDtypeStruct((1, LANES), F32), jax.ShapeDtypeStruct((t, d), F32),
                   jax.ShapeDtypeStruct((t, d), BF16), jax.ShapeDtypeStruct((1, d), F32)),
        compiler_params=_cparams(("arbitrary",)),
    )(x1, fo, gf, target)


FFN_TILE_ROWS = 2048
FFN_TILE_COLS = 256
FFN_COL_BLOCKS = FFN_HIDDEN // FFN_TILE_COLS


def _conv3_past(x, halo, w):
    rows = lax.broadcasted_iota(jnp.int32, x.shape, 0)
    x1 = jnp.where(rows == 0, halo[7:8, :], pltpu.roll(x, 1, 0))
    x2 = jnp.where(rows == 0, halo[6:7, :], jnp.where(rows == 1, halo[7:8, :], pltpu.roll(x, 2, 0)))
    return x * w[2:3] + x1 * w[1:2] + x2 * w[0:1], x1, x2


def _ffn_in_specs(tm, imap, jmap):
    per = tm // SUBLANES
    tile = lambda off: pl.BlockSpec((tm, FFN_TILE_COLS), lambda *g: (imap(*g), off + jmap(*g) % FFN_COL_BLOCKS))
    halo = lambda off: pl.BlockSpec((SUBLANES, FFN_TILE_COLS),
                                    lambda *g: (jnp.maximum(imap(*g) * per - 1, 0), off + jmap(*g) % FFN_COL_BLOCKS))
    wsp = lambda off: pl.BlockSpec((FFN_CONV, FFN_TILE_COLS), lambda *g: (0, off + jmap(*g) % FFN_COL_BLOCKS))
    return [tile(0), halo(0), wsp(0), tile(FFN_COL_BLOCKS), halo(FFN_COL_BLOCKS), wsp(FFN_COL_BLOCKS)]


def _ffn_act_forward(hpre, cw):
    t = hpre.shape[0]
    tm = min(FFN_TILE_ROWS, t)

    def body(hg, pg, wg, hu, pu, wu, out):
        live = (pl.program_id(0) > 0).astype(F32)
        cg, _, _ = _conv3_past(hg[...], pg[...] * live, wg[...])
        cu, _, _ = _conv3_past(hu[...], pu[...] * live, wu[...])
        out[...] = (cg * _sigmoid_plain(cg) * cu).astype(out.dtype)

    return pl.pallas_call(
        body, name="ffn_act_fwd", grid=(t // tm, FFN_COL_BLOCKS),
        in_specs=_ffn_in_specs(tm, lambda i, j: i, lambda i, j: j),
        out_specs=pl.BlockSpec((tm, FFN_TILE_COLS), lambda i, j: (i, j)),
        out_shape=jax.ShapeDtypeStruct((t, FFN_HIDDEN), BF16),
        compiler_params=_cparams(("parallel", "parallel")),
    )(hpre, hpre, cw, hpre, hpre, cw)


def _conv3_future(d, nxt, w):
    tm = d.shape[0]
    rows = lax.broadcasted_iota(jnp.int32, d.shape, 0)
    d1 = jnp.where(rows == tm - 1, nxt[0:1, :], pltpu.roll(d, tm - 1, 0))
    d2 = jnp.where(rows == tm - 1, nxt[1:2, :], jnp.where(rows == tm - 2, nxt[0:1, :], pltpu.roll(d, tm - 2, 0)))
    return d * w[2:3] + d1 * w[1:2] + d2 * w[0:1]


def _ffn_backward(hpre, cw, dact):
    t = hpre.shape[0]
    tm = min(FFN_TILE_ROWS, t)
    n_tiles = t // tm
    per = tm // SUBLANES

    def d_conv_out(cg, cu, d):
        s = _sigmoid_plain(cg)
        return d * cu * s * (1.0 + cg * (1.0 - s)), d * cg * s

    def body(hg, pg, ng, wg, hu, pu, nu, wu, da, dan, dhg, dhu, dwg, dwu):
        i = pl.program_id(1)
        live_prev = (i > 0).astype(F32)
        live_next = (i < n_tiles - 1).astype(F32)
        xg, xu = hg[...], hu[...]
        cg, g1, g2 = _conv3_past(xg, pg[...] * live_prev, wg[...])
        cu, u1, u2 = _conv3_past(xu, pu[...] * live_prev, wu[...])
        dg, du = d_conv_out(cg, cu, da[...])
        cgn, _, _ = _conv3_past(ng[...], hg[tm - SUBLANES:tm, :], wg[...])
        cun, _, _ = _conv3_past(nu[...], hu[tm - SUBLANES:tm, :], wu[...])
        dgn, dun = d_conv_out(cgn, cun, dan[...] * live_next)
        dhg[...] = _conv3_future(dg, dgn, wg[...]).astype(dhg.dtype)
        dhu[...] = _conv3_future(du, dun, wu[...]).astype(dhu.dtype)
        sums_g = [jnp.sum(xs * dg, axis=0, keepdims=True) for xs in (g2, g1, xg)]
        sums_u = [jnp.sum(xs * du, axis=0, keepdims=True) for xs in (u2, u1, xu)]

        @pl.when(i == 0)
        def _():
            for r_ in range(FFN_CONV):
                dwg[r_:r_ + 1, :] = sums_g[r_]
                dwu[r_:r_ + 1, :] = sums_u[r_]

        @pl.when(i > 0)
        def _():
            for r_ in range(FFN_CONV):
                dwg[r_:r_ + 1, :] += sums_g[r_]
                dwu[r_:r_ + 1, :] += sums_u[r_]

    nb = FFN_COL_BLOCKS
    nxt = lambda i: jnp.minimum((i + 1) * per, t // SUBLANES - 1)
    prv = lambda i: jnp.maximum(i * per - 1, 0)
    half = lambda off: [pl.BlockSpec((tm, FFN_TILE_COLS), lambda j, i: (i, off + j)),
                        pl.BlockSpec((SUBLANES, FFN_TILE_COLS), lambda j, i: (prv(i), off + j)),
                        pl.BlockSpec((SUBLANES, FFN_TILE_COLS), lambda j, i: (nxt(i), off + j)),
                        pl.BlockSpec((FFN_CONV, FFN_TILE_COLS), lambda j, i: (0, off + j))]
    tile = pl.BlockSpec((tm, FFN_TILE_COLS), lambda j, i: (i, j))
    taps = pl.BlockSpec((FFN_CONV, FFN_TILE_COLS), lambda j, i: (0, j))
    return pl.pallas_call(
        body, name="ffn_bwd", grid=(nb, n_tiles),
        in_specs=half(0) + half(nb) + [tile, pl.BlockSpec((SUBLANES, FFN_TILE_COLS), lambda j, i: (nxt(i), j))],
        out_specs=(tile, tile, taps, taps),
        out_shape=(jax.ShapeDtypeStruct((t, FFN_HIDDEN), BF16), jax.ShapeDtypeStruct((t, FFN_HIDDEN), BF16),
                   jax.ShapeDtypeStruct((FFN_CONV, FFN_HIDDEN), F32), jax.ShapeDtypeStruct((FFN_CONV, FFN_HIDDEN), F32)),
        compiler_params=_cparams(("parallel", "arbitrary")),
    )(hpre, hpre, hpre, cw, hpre, hpre, hpre, cw, dact, dact)


def _my_place():
    x, y, c = lax.axis_index("x"), lax.axis_index("y"), lax.axis_index("c")
    return x, y, c, 4 * x + 2 * y + c


N_CHIPS = 4


def _remote(src, dst, send_sem, recv_sem, dev):
    return pltpu.make_async_remote_copy(src_ref=src, dst_ref=dst, send_sem=send_sem, recv_sem=recv_sem, device_id=dev,
                                        device_id_type=pl.DeviceIdType.MESH)


def _chip_peer(x, y, k):
    return x ^ ((k >> 1) & 1), y ^ (k & 1)


def _all_gather_two_level(shard, name):
    r, w = shard.shape

    def body(src, out, send_sems, recv_sems, local_sem):
        x, y, c, me = _my_place()
        sibling = (x, y, 1 - c)
        mine = pltpu.make_async_copy(src, out.at[me], local_sem)
        mine.start()
        first = [_remote(src, out.at[me], send_sems.at[0], recv_sems.at[0], sibling)]
        for k in range(1, N_CHIPS):
            px, py = _chip_peer(x, y, k)
            first.append(_remote(src, out.at[me], send_sems.at[k], recv_sems.at[k], (px, py, c)))
        for cp in first:
            cp.start()
        passed = []
        for k in range(1, N_CHIPS):
            px, py = _chip_peer(x, y, k)
            landed = out.at[me ^ (2 * k)]
            _remote(src, landed, send_sems.at[k], recv_sems.at[k], (px, py, c)).wait_recv()
            fwd = _remote(landed, landed, send_sems.at[N_CHIPS - 1 + k], recv_sems.at[N_CHIPS - 1 + k], sibling)
            fwd.start()
            passed.append(fwd)
        _remote(src, out.at[me ^ 1], send_sems.at[0], recv_sems.at[0], sibling).wait_recv()
        for k in range(1, N_CHIPS):
            got = out.at[(me ^ 1) ^ (2 * k)]
            _remote(got, got, send_sems.at[N_CHIPS - 1 + k], recv_sems.at[N_CHIPS - 1 + k], sibling).wait_recv()
        for cp in first + passed:
            cp.wait_send()
        mine.wait()

    return pl.pallas_call(
        body, name=name,
        in_specs=[pl.BlockSpec(memory_space=pl.ANY)],
        out_specs=pl.BlockSpec(memory_space=pl.ANY),
        out_shape=jax.ShapeDtypeStruct((N_DEV, r, w), shard.dtype),
        scratch_shapes=[pltpu.SemaphoreType.DMA((N_DEV - 1,)), pltpu.SemaphoreType.DMA((N_DEV - 1,)), pltpu.SemaphoreType.DMA],
    )(shard)


def _device_peer(x, y, c, k):
    px, py, pc = x ^ ((k >> 2) & 1), y ^ ((k >> 1) & 1), c ^ (k & 1)
    return (px, py, pc), 4 * px + 2 * py + pc


_HBM = pl.BlockSpec(memory_space=pltpu.HBM)
_SEM = pl.BlockSpec(memory_space=pltpu.SEMAPHORE)


def _gather_start(shard, name):
    def body(src, land, send_sems, recv_sems, src_thru, land_thru, token):
        x, y, c, me = _my_place()
        for k in range(1, N_DEV):
            dev, _ = _device_peer(x, y, c, k)
            _remote(src, land.at[me], send_sems.at[k], recv_sems.at[k], dev).start()
        token[...] = jnp.zeros_like(token)

    landing = lax.empty((N_DEV,) + shard.shape, shard.dtype)
    return pl.pallas_call(
        body, name=name,
        out_shape=(pltpu.SemaphoreType.DMA((N_DEV,)), pltpu.SemaphoreType.DMA((N_DEV,)), pltpu.HBM(shard.shape, shard.dtype),
                   pltpu.HBM(landing.shape, landing.dtype), jax.ShapeDtypeStruct((SUBLANES, LANES), F32)),
        in_specs=(_HBM, _HBM), out_specs=(_SEM, _SEM, _HBM, _HBM, pl.BlockSpec(memory_space=pltpu.VMEM)),
        input_output_aliases={0: 2, 1: 3},
        compiler_params=pltpu.CompilerParams(has_side_effects=pltpu.SideEffectType.DATAFLOW_SIDE_EFFECTING),
    )(pltpu.with_memory_space_constraint(shard, pltpu.HBM), pltpu.with_memory_space_constraint(landing, pltpu.HBM))


def _gather_wait(send_sems, recv_sems, shard, landing, after, name):
    n_after = len(after)

    def body(*refs):
        src, land, send_sems, recv_sems = refs[:4]
        x, y, c, _ = _my_place()
        for k in range(1, N_DEV):
            dev, idx = _device_peer(x, y, c, k)
            cp = _remote(src, land.at[idx], send_sems.at[k], recv_sems.at[k], dev)
            cp.wait_send()
            cp.wait_recv()

    return pl.pallas_call(
        body, name=name,
        out_shape=(pltpu.HBM(shard.shape, shard.dtype), pltpu.HBM(landing.shape, landing.dtype)),
        in_specs=(_HBM, _HBM, _SEM, _SEM) + (pl.BlockSpec(memory_space=pl.ANY),) * n_after, out_specs=(_HBM, _HBM),
        input_output_aliases={0: 0, 1: 1},
        compiler_params=pltpu.CompilerParams(has_side_effects=pltpu.SideEffectType.DATAFLOW_SIDE_EFFECTING),
    )(shard, landing, send_sems, recv_sems, *after)[1]


def _slab_push_start(slabs, name):
    na = len(slabs)

    def body(*refs):
        srcs, lands = refs[:na], refs[na:2 * na]
        send_sems, recv_sems = refs[2 * na], refs[2 * na + 1]
        token = refs[-1]
        x, y, c, me = _my_place()
        for i in range(na):
            for k in range(1, N_DEV):
                dev, idx = _device_peer(x, y, c, k)
                s = i * N_DEV + k
                _remote(srcs[i].at[idx], lands[i].at[me], send_sems.at[s], recv_sems.at[s], dev).start()
        token[...] = jnp.zeros_like(token)

    hbm_shapes = [pltpu.HBM(a.shape, a.dtype) for a in slabs]
    ins = [pltpu.with_memory_space_constraint(a, pltpu.HBM) for a in slabs]
    ins += [pltpu.with_memory_space_constraint(lax.empty(a.shape, a.dtype), pltpu.HBM) for a in slabs]
    out = pl.pallas_call(
        body, name=name,
        out_shape=(pltpu.SemaphoreType.DMA((na * N_DEV,)), pltpu.SemaphoreType.DMA((na * N_DEV,)), *hbm_shapes, *hbm_shapes,
                   jax.ShapeDtypeStruct((SUBLANES, LANES), F32)),
        in_specs=(_HBM,) * (2 * na), out_specs=(_SEM, _SEM) + (_HBM,) * (2 * na) + (pl.BlockSpec(memory_space=pltpu.VMEM),),
        input_output_aliases={i: 2 + i for i in range(2 * na)},
        compiler_params=pltpu.CompilerParams(has_side_effects=pltpu.SideEffectType.DATAFLOW_SIDE_EFFECTING),
    )(*ins)
    return out[0], out[1], list(out[2:2 + na]), list(out[2 + na:2 + 2 * na]), out[-1]


def _slab_push_wait(send_sems, recv_sems, slabs, landings, after, name):
    na = len(slabs)

    def body(*refs):
        srcs, lands = refs[:na], refs[na:2 * na]
        send_sems, recv_sems = refs[2 * na], refs[2 * na + 1]
        x, y, c, me = _my_place()
        for i in range(na):
            for k in range(1, N_DEV):
                dev, idx = _device_peer(x, y, c, k)
                s = i * N_DEV + k
                cp = _remote(srcs[i].at[idx], lands[i].at[idx], send_sems.at[s], recv_sems.at[s], dev)
                cp.wait_send()
                cp.wait_recv()

    hbm_shapes = tuple(pltpu.HBM(a.shape, a.dtype) for a in slabs)
    out = pl.pallas_call(
        body, name=name, out_shape=hbm_shapes + hbm_shapes,
        in_specs=(_HBM,) * (2 * na) + (_SEM, _SEM) + (pl.BlockSpec(memory_space=pl.ANY),) * len(after),
        out_specs=(_HBM,) * (2 * na), input_output_aliases={i: i for i in range(2 * na)},
        compiler_params=pltpu.CompilerParams(has_side_effects=pltpu.SideEffectType.DATAFLOW_SIDE_EFFECTING),
    )(*slabs, *landings, send_sems, recv_sems, *after)
    return list(out[na:])


def _pair_exchange(arrays, name):
    na = len(arrays)

    def body(*refs):
        srcs, dsts, (send_sems, recv_sems) = refs[:na], refs[na:2 * na], refs[2 * na:]
        x, y, c, _ = _my_place()
        sibling = (x, y, 1 - c)
        copies = []
        for i in range(na):
            for q in range(N_CHIPS):
                s = i * N_CHIPS + q
                copies.append(_remote(srcs[i].at[2 * q + 1 - c], dsts[i].at[q], send_sems.at[s], recv_sems.at[s], sibling))
        for cp in copies:
            cp.start()
        for cp in copies:
            cp.wait_recv()
        for cp in copies:
            cp.wait_send()

    hbm = pl.BlockSpec(memory_space=pl.ANY)
    return pl.pallas_call(
        body, name=name, in_specs=[hbm] * na, out_specs=tuple([hbm] * na),
        out_shape=tuple(jax.ShapeDtypeStruct((N_CHIPS,) + a.shape[1:], a.dtype) for a in arrays),
        scratch_shapes=[pltpu.SemaphoreType.DMA((na * N_CHIPS,)), pltpu.SemaphoreType.DMA((na * N_CHIPS,))],
    )(*arrays)


ELEMENTWISE_COLS = 512


def _pair_sum(slabs, recv, core, out_dtype, name):
    _, r, w = slabs.shape
    tc = ELEMENTWISE_COLS

    def body(core_ref, mine, theirs, out):
        out[...] = (mine[...] + theirs[...]).astype(out.dtype)

    grid_spec = pltpu.PrefetchScalarGridSpec(
        num_scalar_prefetch=1, grid=(N_CHIPS, w // tc),
        in_specs=[pl.BlockSpec((None, r, tc), lambda q, j, core_ref: (2 * q + core_ref[0], 0, j)),
                  pl.BlockSpec((None, r, tc), lambda q, j, core_ref: (q, 0, j))],
        out_specs=pl.BlockSpec((None, r, tc), lambda q, j, core_ref: (q, 0, j)))
    return pl.pallas_call(body, name=name, grid_spec=grid_spec,
                          out_shape=jax.ShapeDtypeStruct((N_CHIPS, r, w), out_dtype),
                          compiler_params=_cparams(("parallel", "parallel")))(core, slabs, recv)


def _chip_exchange(arrays, name):
    na = len(arrays)

    def body(*refs):
        srcs, dsts, (send_sems, recv_sems, local_sems) = refs[:na], refs[na:2 * na], refs[2 * na:]
        x, y, c, _ = _my_place()
        chip = 2 * x + y
        own = [pltpu.make_async_copy(srcs[i].at[chip], dsts[i].at[chip], local_sems.at[i]) for i in range(na)]
        for cp in own:
            cp.start()
        sends, arrivals = [], []
        for i in range(na):
            for k in range(1, N_CHIPS):
                px, py = _chip_peer(x, y, k)
                s = i * N_CHIPS + k
                sends.append(_remote(srcs[i].at[chip ^ k], dsts[i].at[chip], send_sems.at[s], recv_sems.at[s], (px, py, c)))
                arrivals.append(_remote(srcs[i].at[chip], dsts[i].at[chip ^ k], send_sems.at[s], recv_sems.at[s], (px, py, c)))
        for cp in sends:
            cp.start()
        for cp in arrivals:
            cp.wait_recv()
        for cp in sends:
            cp.wait_send()
        for cp in own:
            cp.wait()

    hbm = pl.BlockSpec(memory_space=pl.ANY)
    return pl.pallas_call(
        body, name=name, in_specs=[hbm] * na, out_specs=tuple([hbm] * na),
        out_shape=tuple(jax.ShapeDtypeStruct(a.shape, a.dtype) for a in arrays),
        scratch_shapes=[pltpu.SemaphoreType.DMA((na * N_CHIPS,)), pltpu.SemaphoreType.DMA((na * N_CHIPS,)),
                        pltpu.SemaphoreType.DMA((na,))],
    )(*arrays)


def _adamw_update(g, w, m, v):
    c1 = 1.0 / (1.0 - ADAM_B1 ** ADAM_STEP)
    c2 = 1.0 / (1.0 - ADAM_B2 ** ADAM_STEP)
    mn = ADAM_B1 * m + (1.0 - ADAM_B1) * g
    vn = ADAM_B2 * v + (1.0 - ADAM_B2) * (g * g)
    return -ADAM_LR * ((mn * c1) / (jnp.sqrt(vn * c2) + ADAM_EPS) + ADAM_WD * w), mn, vn


def _reduce_adamw(parts, w, m, v, name):
    n_parts, r, wd = parts.shape
    tc = ELEMENTWISE_COLS

    def body(p_ref, w_ref, m_ref, v_ref, g_out, d_out, m_out, v_out):
        g = p_ref[0].astype(F32)
        for s in range(1, n_parts):
            g = g + p_ref[s].astype(F32)
        g_out[...] = g
        d_out[...], m_out[...], v_out[...] = _adamw_update(g, w_ref[...], m_ref[...], v_ref[...])

    blk = pl.BlockSpec((r, tc), lambda j: (0, j))
    shp = jax.ShapeDtypeStruct((r, wd), F32)
    return pl.pallas_call(
        body, name=name, grid=(wd // tc,),
        in_specs=[pl.BlockSpec((n_parts, r, tc), lambda j: (0, 0, j)), blk, blk, blk],
        out_specs=(blk, blk, blk, blk), out_shape=(shp, shp, shp, shp),
        compiler_params=_cparams(("parallel",)),
    )(parts, w, m, v)


def _reduce_landed_adamw(landing, own, me, w, m, v, name):
    n_parts, r, wd = landing.shape
    tc = ELEMENTWISE_COLS

    def body(me_ref, land_ref, own_ref, w_ref, m_ref, v_ref, g_out, d_out, m_out, v_out):
        mine = own_ref[...].astype(F32)
        g = None
        for s in range(n_parts):
            part = jnp.where(me_ref[0] == s, mine, land_ref[s].astype(F32))
            g = part if g is None else g + part
        g_out[...] = g
        d_out[...], m_out[...], v_out[...] = _adamw_update(g, w_ref[...], m_ref[...], v_ref[...])

    blk = pl.BlockSpec((r, tc), lambda j, me_ref: (0, j))
    shp = jax.ShapeDtypeStruct((r, wd), F32)
    grid_spec = pltpu.PrefetchScalarGridSpec(
        num_scalar_prefetch=1, grid=(wd // tc,),
        in_specs=[pl.BlockSpec((n_parts, r, tc), lambda j, me_ref: (0, 0, j)),
                  pl.BlockSpec((None, r, tc), lambda j, me_ref: (me_ref[0], 0, j)), blk, blk, blk],
        out_specs=(blk, blk, blk, blk))
    return pl.pallas_call(body, name=name, grid_spec=grid_spec, out_shape=(shp, shp, shp, shp),
                          compiler_params=_cparams(("parallel",)))(me, landing, own, w, m, v)


PACK_W = 1024


def _pad_heads(a, slots):
    lead = a.shape[:-1]
    a = a.reshape(lead + (slots, RWKV_HEAD_DIM))
    a = jnp.pad(a, [(0, 0)] * (len(lead) + 1) + [(0, LANES - RWKV_HEAD_DIM)])
    return a.reshape(lead + (slots * LANES,))


def _flat_pack(arrs, dtype, row_mult):
    flat = jnp.concatenate([a.reshape(-1).astype(dtype) for a in arrs])
    n = flat.shape[0]
    rows = -(-n // PACK_W)
    rows = -(-rows // row_mult) * row_mult
    return jnp.pad(flat, (0, rows * PACK_W - n)).reshape(rows, PACK_W)


def _row_pack(arrs, dtype, row_mult):
    parts = [a.astype(dtype) if a.shape[1] == PACK_W else a.astype(dtype).reshape(-1, PACK_W) for a in arrs]
    rows = sum(p.shape[0] for p in parts)
    pad = -(-rows // row_mult) * row_mult - rows
    return jnp.concatenate(parts + ([jnp.zeros((pad, PACK_W), dtype)] if pad else []), axis=0)


def _unpack_row_gathered(g, names, shard_shapes):
    out, r0 = {}, 0
    for n in names:
        s = shard_shapes[n]
        rows = s[0] * s[1] // PACK_W
        seg = g[:, r0:r0 + rows, :]
        r0 += rows
        if s[1] == PACK_W:
            assert SHARD_AXIS[n] == 0
            out[n] = seg.reshape(N_DEV * s[0], s[1])
        else:
            assert SHARD_AXIS[n] == 1
            out[n] = jnp.transpose(seg.reshape((N_DEV,) + tuple(s)), (1, 0, 2)).reshape(s[0], N_DEV * s[1])
    return out


def _unpack_gathered(g, names, shard_shapes):
    flat = g.reshape(N_DEV, -1)
    out, off = {}, 0
    for n in names:
        s = shard_shapes[n]
        size = s[0] * s[1]
        seg = flat[:, off:off + size].reshape((N_DEV,) + tuple(s))
        off += size
        if SHARD_AXIS[n] == 1:
            out[n] = jnp.transpose(seg, (1, 0, 2)).reshape(s[0], N_DEV * s[1])
        else:
            out[n] = seg.reshape(N_DEV * s[0], s[1])
    return out


def _shard_major(full, axis):
    a, b = full.shape
    if axis == 1:
        return jnp.transpose(full.reshape(a, N_DEV, b // N_DEV), (1, 0, 2)).reshape(N_DEV, -1)
    return full.reshape(N_DEV, -1)


def _prepare_weights(full, rep):
    w = full['w_in']
    d = w.shape[1]
    rkv = jnp.pad(w[0:1536].reshape(3 * RWKV_HEADS, RWKV_HEAD_DIM, d), ((0, 0), (0, LANES - RWKV_HEAD_DIM), (0, 0)))
    w_cat = jnp.concatenate([
        w[3848:4872], w[4872:5896], rkv.reshape(3 * RWKV_HEADS * LANES, d), w[1792:3328], w[3328:3840],
        w[1536:1792], jnp.pad(w[3840:3848], ((0, LANES - 8), (0, 0))), jnp.zeros((LANES, d), w.dtype)], axis=0)
    assert w_cat.shape[0] == CAT_W
    mu = rep['rwkv_mu']
    vecs = [mu[0:512], mu[512:1024], mu[1024:1536], rep['rwkv_w0'], rep['rwkv_a0'], rep['rwkv_k_k'], rep['rwkv_k_a'],
            rep['rwkv_ln_w'], rep['rwkv_ln_b'], rep['rwkv_r_k'].reshape(-1)]
    ppack = jnp.stack([jnp.pad(v.reshape(RWKV_HEADS, RWKV_HEAD_DIM), ((0, 0), (0, LANES - RWKV_HEAD_DIM))) for v in vecs], axis=1)
    ppack = jnp.pad(ppack, ((0, 0), (0, 16 - len(vecs)), (0, 0)))
    mulo = mu[1536:1792].reshape(1, 2 * LANES)
    wl = jnp.zeros((3, 2 * LANES, RWKV_HEADS * LANES), F32)
    wl = wl.at[0, 0:64].set(_pad_heads(full['rwkv_w2'], RWKV_HEADS))
    wl = wl.at[1, 64:128].set(_pad_heads(full['rwkv_a2'], RWKV_HEADS))
    wl = wl.at[2, 128:256].set(_pad_heads(full['rwkv_g2'], RWKV_HEADS))
    wl = jnp.transpose(wl.reshape(3, 2 * LANES, RWKV_HEADS, LANES), (2, 0, 1, 3))
    cw = full['gdn_conv_w'].reshape(GDN_CONV, 3, GDN_HEADS, LANES)
    cwpack = jnp.pad(jnp.transpose(cw, (2, 1, 0, 3)), ((0, 0), (0, 0), (0, SUBLANES - GDN_CONV), (0, 0)))
    gpar = jnp.zeros((SUBLANES, LANES), F32)
    gpar = gpar.at[0, 0:GDN_HEADS].set(rep['gdn_a_log']).at[1, 0:GDN_HEADS].set(rep['gdn_dt_bias']).at[2].set(rep['gdn_norm_w'])
    return dict(w_cat=w_cat, ffn_cw=full['ffn_conv_w'], ppack=ppack, mulo=mulo, wl=wl, cwpack=cwpack, gpar=gpar,
                g1=rep['norm1_g'].reshape(1, -1), g2=rep['norm2_g'].reshape(1, -1), gf=rep['final_g'].reshape(1, -1))


def _prepare_late_weights(full):
    rp = full['rwkv_proj']
    rproj = jnp.pad(rp.reshape(RWKV_HEADS, RWKV_HEAD_DIM, -1), ((0, 0), (0, LANES - RWKV_HEAD_DIM), (0, 0))).reshape(RWKV_HEADS * LANES, -1)
    return dict(rproj=rproj, gproj=full['gdn_proj'], w_out=full['w_out'], ffn_up=full['ffn_up'], ffn_down=full['ffn_down'])


def _local_step(x, target, p, late_weights, push_grads):
    d = x.shape[1]
    full_w = lambda a: (a, a.shape[1], 0)
    (u,) = _rw_forward(_norm_fn, [full_w(x)], [p['g1']], [(d, BF16)], "norm1")
    p_cat = _matmul(u, p['w_cat'], 'nt', F32, "proj_in")
    ya_pre, st_r = _rwkv_forward(p_cat, p['ppack'], p['mulo'], p['wl'])
    yb_pre, st_g = _gdn_forward(p_cat, p['cwpack'], p['gpar'])
    p = {**p, **late_weights((ya_pre, yb_pre))}
    ya = _matmul(ya_pre, p['rproj'], 'nn', F32, "rwkv_proj")
    yb = _matmul(yb_pre, p['gproj'], 'nn', F32, "gdn_proj")
    gates = [(p_cat, d, OFF_GA // d), (p_cat, d, OFF_GB // d)]
    (mixed,) = _rw_forward(_merge_fn, gates + [full_w(ya), full_w(yb)], [], [(d, BF16)], "merge")
    mo = _matmul(mixed, p['w_out'], 'nn', F32, "out_proj")
    x1, n2 = _rw_forward(_res_norm_fn, [full_w(x), full_w(mo)], [p['g2']], [(d, F32), (d, BF16)], "res_norm2")
    hpre = _matmul(n2, p['ffn_up'], 'nt', F32, "ffn_up")
    act = _ffn_act_forward(hpre, p['ffn_cw'])
    fo = _matmul(act, p['ffn_down'], 'nn', F32, "ffn_down")
    loss_vec, dx2, dx2b, dgf = _loss_head(x1, fo, p['gf'], target, "loss_head")

    dact = _matmul(dx2b, p['ffn_down'], 'nt', F32, "d_act")
    dw_down = _matmul(act, dx2b, 'tn', BF16, "dw_ffn_down")
    dh_gate, dh_up, dcw_gate, dcw_up = _ffn_backward(hpre, p['ffn_cw'], dact)
    dh = jnp.concatenate([dh_gate, dh_up], axis=1)
    dcw_f = jnp.concatenate([dcw_gate, dcw_up], axis=1)
    dn2 = _matmul(dh, p['ffn_up'], 'nn', F32, "d_norm2")
    dw_up = _matmul(dh, n2, 'tn', BF16, "dw_ffn_up")
    token = push_grads({'ffn_down': dw_down, 'ffn_up': dw_up})
    dx1, dx1b, dg2 = _rw_backward(_res_norm_fn, [full_w(x), full_w(mo)], [p['g2'] + token], [full_w(dx2), full_w(dn2)],
                                  [F32, BF16], "res_norm2_bwd")
    dmixed = _matmul(dx1b, p['w_out'], 'nt', F32, "d_mixed")
    dw_out = _matmul(mixed, dx1b, 'tn', BF16, "dw_out")
    dga, dgb, dya, dyb = _rw_backward(_merge_fn, gates + [full_w(ya), full_w(yb)], [], [full_w(dmixed)],
                                      [BF16, BF16, BF16, BF16], "merge_bwd")
    d_ya_pre = _matmul(dya, p['rproj'], 'nt', F32, "d_rwkv_out")
    dw_rproj = _matmul(ya_pre, dya, 'tn', F32, "dw_rwkv_proj")
    d_yb_pre = _matmul(dyb, p['gproj'], 'nt', F32, "d_gdn_out")
    dw_gproj = _matmul(yb_pre, dyb, 'tn', F32, "dw_gdn_proj")
    dpr, dpk, dpv, dplo, dpp, dml, dwl = _rwkv_backward(p_cat, p['ppack'], p['mulo'], p['wl'], st_r, d_ya_pre)
    dq, dk, dv, dz, dab, dcw_g, dgp = _gdn_backward(p_cat, p['cwpack'], p['gpar'], st_g, d_yb_pre)
    t = x.shape[0]
    dp_cat = jnp.concatenate([dga, dgb, dpr, dpk, dpv, dq, dk, dv, dz, dplo.astype(BF16), dab.astype(BF16),
                              jnp.zeros((t, LANES), BF16)], axis=1)
    dw_cat = _matmul(dp_cat, u, 'tn', BF16, "dw_in")
    dw_in = jnp.concatenate([dw_cat[OFF_RKV:OFF_QKV].reshape(3 * RWKV_HEADS, LANES, d)[:, :RWKV_HEAD_DIM].reshape(-1, d),
                             dw_cat[OFF_LO:OFF_AB], dw_cat[OFF_QKV:OFF_Z], dw_cat[OFF_Z:OFF_LO], dw_cat[OFF_AB:OFF_AB + 8],
                             dw_cat[OFF_GA:OFF_GB], dw_cat[OFF_GB:OFF_RKV]], axis=0)
    token = push_grads({'w_out': dw_out, 'w_in': dw_in})
    du = _matmul(dp_cat, p['w_cat'], 'nn', F32, "d_norm1")
    grad_x, dg1 = _rw_backward(_norm_skip_fn, [full_w(x)], [p['g1'] + token], [full_w(du), full_w(dx1)], [F32], "norm1_bwd")

    heads = lambda row: dpp[:, row, :RWKV_HEAD_DIM].reshape(-1)
    lora = lambda j, lo_, hi_: jnp.transpose(dwl[:, j, lo_:hi_, :RWKV_HEAD_DIM], (1, 0, 2)).reshape(hi_ - lo_, RWKV_WIDTH)
    grads = {
        'norm1_g': dg1[0],
        'w_in': dw_in,
        'rwkv_mu': jnp.concatenate([heads(0), heads(1), heads(2), jnp.sum(dml[:, 0, :], axis=0)]),
        'rwkv_w0': heads(3), 'rwkv_a0': heads(4), 'rwkv_k_k': heads(5), 'rwkv_k_a': heads(6),
        'rwkv_ln_w': heads(7), 'rwkv_ln_b': heads(8), 'rwkv_r_k': heads(9).reshape(RWKV_HEADS, RWKV_HEAD_DIM),
        'rwkv_w2': lora(0, 0, 64), 'rwkv_a2': lora(1, 64, 128), 'rwkv_g2': lora(2, 128, 256),
        'rwkv_proj': dw_rproj.reshape(RWKV_HEADS, LANES, -1)[:, :RWKV_HEAD_DIM].reshape(RWKV_WIDTH, -1),
        'gdn_conv_w': jnp.transpose(dcw_g[:, :, :GDN_CONV, :], (2, 1, 0, 3)).reshape(GDN_CONV, 3 * GDN_WIDTH),
        'gdn_a_log': jnp.sum(dgp[:, 0, :GDN_HEADS], axis=0), 'gdn_dt_bias': jnp.sum(dgp[:, 1, :GDN_HEADS], axis=0),
        'gdn_norm_w': jnp.sum(dgp[:, 2, :], axis=0),
        'gdn_proj': dw_gproj, 'w_out': dw_out, 'norm2_g': dg2[0], 'ffn_up': dw_up, 'ffn_conv_w': dcw_f,
        'ffn_down': dw_down, 'final_g': dgf[0],
    }
    return loss_vec, grad_x, grads


def kernel(x, norm1_g, w_in, rwkv_mu, rwkv_w0, rwkv_w2, rwkv_a0, rwkv_a2, rwkv_g2, rwkv_k_k, rwkv_k_a, rwkv_r_k, rwkv_ln_w, rwkv_ln_b, rwkv_proj, gdn_conv_w, gdn_a_log, gdn_dt_bias, gdn_norm_w, gdn_proj, w_out, norm2_g, ffn_up, ffn_conv_w, ffn_down, final_g, loss_target, m_norm1_g, m_w_in, m_rwkv_mu, m_rwkv_w0, m_rwkv_w2, m_rwkv_a0, m_rwkv_a2, m_rwkv_g2, m_rwkv_k_k, m_rwkv_k_a, m_rwkv_r_k, m_rwkv_ln_w, m_rwkv_ln_b, m_rwkv_proj, m_gdn_conv_w, m_gdn_a_log, m_gdn_dt_bias, m_gdn_norm_w, m_gdn_proj, m_w_out, m_norm2_g, m_ffn_up, m_ffn_conv_w, m_ffn_down, m_final_g, v_norm1_g, v_w_in, v_rwkv_mu, v_rwkv_w0, v_rwkv_w2, v_rwkv_a0, v_rwkv_a2, v_rwkv_g2, v_rwkv_k_k, v_rwkv_k_a, v_rwkv_r_k, v_rwkv_ln_w, v_rwkv_ln_b, v_rwkv_proj, v_gdn_conv_w, v_gdn_a_log, v_gdn_dt_bias, v_gdn_norm_w, v_gdn_proj, v_w_out, v_norm2_g, v_ffn_up, v_ffn_conv_w, v_ffn_down, v_final_g):
    given = dict(zip(WEIGHT_NAMES, (norm1_g, w_in, rwkv_mu, rwkv_w0, rwkv_w2, rwkv_a0, rwkv_a2, rwkv_g2, rwkv_k_k, rwkv_k_a, rwkv_r_k,
                                    rwkv_ln_w, rwkv_ln_b, rwkv_proj, gdn_conv_w, gdn_a_log, gdn_dt_bias, gdn_norm_w, gdn_proj, w_out,
                                    norm2_g, ffn_up, ffn_conv_w, ffn_down, final_g)))
    mom1 = dict(zip(WEIGHT_NAMES, (m_norm1_g, m_w_in, m_rwkv_mu, m_rwkv_w0, m_rwkv_w2, m_rwkv_a0, m_rwkv_a2, m_rwkv_g2, m_rwkv_k_k,
                                   m_rwkv_k_a, m_rwkv_r_k, m_rwkv_ln_w, m_rwkv_ln_b, m_rwkv_proj, m_gdn_conv_w, m_gdn_a_log,
                                   m_gdn_dt_bias, m_gdn_norm_w, m_gdn_proj, m_w_out, m_norm2_g, m_ffn_up, m_ffn_conv_w, m_ffn_down,
                                   m_final_g)))
    mom2 = dict(zip(WEIGHT_NAMES, (v_norm1_g, v_w_in, v_rwkv_mu, v_rwkv_w0, v_rwkv_w2, v_rwkv_a0, v_rwkv_a2, v_rwkv_g2, v_rwkv_k_k,
                                   v_rwkv_k_a, v_rwkv_r_k, v_rwkv_ln_w, v_rwkv_ln_b, v_rwkv_proj, v_gdn_conv_w, v_gdn_a_log,
                                   v_gdn_dt_bias, v_gdn_norm_w, v_gdn_proj, v_w_out, v_norm2_g, v_ffn_up, v_ffn_conv_w, v_ffn_down,
                                   v_final_g)))
    def strip(n, a):
        a = a if n == 'final_g' else a.reshape(a.shape[1:])
        return a.T if n in TRANSPOSED else a

    local = {n: strip(n, a) for n, a in given.items()}
    shard_shapes = {n: local[n].shape for n in SHARD_AXIS}
    sharded = BIG_SHARDED + SMALL_SHARDED

    late_names = [n for n in BIG_SHARDED if n != 'w_in']
    g_in = _all_gather_two_level(_row_pack([local['w_in']], BF16, 16), "gather_w_in")
    g_small = _all_gather_two_level(_flat_pack([local[n] for n in SMALL_SHARDED], F32, SUBLANES), "gather_small")
    late_pack, g_in, g_small = lax.optimization_barrier((_row_pack([local[n] for n in late_names], BF16, 16), g_in, g_small))
    send_sems, recv_sems, late_pack, landing, token = _gather_start(late_pack, "gather_late_start")
    full = _unpack_row_gathered(g_in, ['w_in'], shard_shapes)
    full.update(_unpack_gathered(g_small, SMALL_SHARDED, shard_shapes))
    rep = {n: local[n] for n in REPLICATED}
    rep['norm1_g'] = rep['norm1_g'] + token[0, 0]

    def late_weights(after):
        got = _gather_wait(send_sems, recv_sems, late_pack, landing, after, "gather_late_wait")
        me = 4 * lax.axis_index("x") + 2 * lax.axis_index("y") + lax.axis_index("c")
        slot = lax.broadcasted_iota(jnp.int32, (N_DEV, 1, 1), 0)
        got = jnp.where(slot == me, late_pack[None], got)
        return _prepare_late_weights(_unpack_row_gathered(got, late_names, shard_shapes))

    pushes = []
    me = 4 * lax.axis_index("x") + 2 * lax.axis_index("y") + lax.axis_index("c")
    slot = lax.broadcasted_iota(jnp.int32, (N_DEV, 1, 1), 0)

    def push_grads(group):
        names = list(group)
        slabs = [group[n].reshape(N_DEV, -1, group[n].shape[1]) for n in names]
        send_sems, recv_sems, slabs, landings, token = _slab_push_start(slabs, "grad_push_start_" + "_".join(names))
        pushes.append((names, send_sems, recv_sems, slabs, landings))
        return token[0, 0]

    loss_vec, grad_x, grads = _local_step(x[0], loss_target[0], _prepare_weights(full, rep), late_weights, push_grads)

    landed = {}
    for names, send_sems, recv_sems, slabs, landings in pushes:
        got = _slab_push_wait(send_sems, recv_sems, slabs, landings, (grad_x,), "grad_push_wait_" + "_".join(names))
        for n, slab, land in zip(names, slabs, got):
            landed[n] = (land, slab)

    small_sharded = ['rwkv_proj', 'gdn_proj'] + SMALL_SHARDED
    small_names = small_sharded + REPLICATED
    rep_vec = jnp.concatenate([grads[n].reshape(-1) for n in REPLICATED] + [loss_vec[0, 0:1]])
    slab_small = jnp.concatenate([_shard_major(grads[n], SHARD_AXIS[n]) for n in small_sharded] +
                                 [jnp.broadcast_to(rep_vec[None], (N_DEV, rep_vec.shape[0]))], axis=1)
    small_rows = -(-slab_small.shape[1] // (PACK_W * SUBLANES)) * SUBLANES
    slab_small = jnp.pad(slab_small, ((0, 0), (0, small_rows * PACK_W - slab_small.shape[1]))).reshape(N_DEV, small_rows, PACK_W)
    core = lax.axis_index("c").astype(jnp.int32).reshape(1)
    (from_sibling,) = _pair_exchange([slab_small], "grad_pair_exchange")
    chip_small = _pair_sum(slab_small, from_sibling, core, F32, "grad_pair_sum_small")
    (parts_small,) = _chip_exchange([chip_small], "grad_chip_exchange")

    def pack_local(src):
        flat = jnp.concatenate([strip(n, src[n]).reshape(-1) for n in small_names])
        return jnp.pad(flat, (0, small_rows * PACK_W - flat.shape[0])).reshape(small_rows, PACK_W)

    results = [({}, None) for _ in range(4)]
    me_arr = me.astype(jnp.int32).reshape(1)
    for n in ROW_SHARDED:
        packs = _reduce_landed_adamw(*landed[n], me_arr, local[n], strip(n, mom1[n]), strip(n, mom2[n]), "adamw_" + n)
        for (out, _), pk in zip(results, packs):
            out[n] = (pk.T if n in TRANSPOSED else pk).reshape(given[n].shape)
    packs = _reduce_adamw(parts_small, pack_local(given), pack_local(mom1), pack_local(mom2), "adamw_small")
    for i, pk in enumerate(packs):
        flat, off = pk.reshape(-1), 0
        for n in small_names:
            size = int(np.prod(given[n].shape))
            results[i][0][n] = flat[off:off + size].reshape(given[n].shape)
            off += size
        results[i] = (results[i][0], flat[off])
    (g_out, loss), (d_out, _), (m_out, _), (v_out, _) = results
    return (loss, grad_x[None], *[g_out[n] for n in WEIGHT_NAMES], *[d_out[n] for n in WEIGHT_NAMES],
            *[m_out[n] for n in WEIGHT_NAMES], *[v_out[n] for n in WEIGHT_NAMES])
```
